```python
import jax, jax.numpy as jnp
from jax import lax
import numpy as np

D_MODEL = 1024
BATCH = 8
SEQ = 4096
DEPTH = 1

HEAD_DIM = 64
GRID_W = 64
NA_HEADS = 8
NA_WIN_ROWS = 8
NA_WIN_COLS = 16
NA_QBLOCK_COLS = 16
NA_KBLOCK_COLS = 32
DIL_GROUPS = ((128, 1), (512, 4), (2048, 16))
DIL_HEADS_PER_GROUP = 4
DIL_HEADS = DIL_HEADS_PER_GROUP * len(DIL_GROUPS)
NA_WIDTH = NA_HEADS * HEAD_DIM
DIL_WIDTH = DIL_HEADS * HEAD_DIM
DIL_OUT_WIDTH = DIL_HEADS_PER_GROUP * HEAD_DIM
IN_WIDTHS = (NA_WIDTH, NA_WIDTH, NA_WIDTH, DIL_WIDTH, DIL_WIDTH, DIL_WIDTH, D_MODEL, D_MODEL)
IN_WIDTH = sum(IN_WIDTHS)
D_FF = 4 * D_MODEL
PLE_DIM = 256
ROPE_THETA = 10000.0
RMS_EPS = 1e-6
NEG_INF = -1e30

kernel_name = "hybrid_na_dilated_gated_encoder"


def rms_norm(x, g):
    xf = x.astype(jnp.float32)
    y = xf * lax.rsqrt(jnp.mean(xf * xf, axis=-1, keepdims=True) + RMS_EPS)
    return (y * g.astype(jnp.float32)).astype(x.dtype)


def rotary(x, positions):
    half = HEAD_DIM // 2
    inv_freq = ROPE_THETA ** (-jnp.arange(half, dtype=jnp.float32) / half)
    ang = positions.astype(jnp.float32)[:, None, :, None] * inv_freq
    cos, sin = jnp.cos(ang), jnp.sin(ang)
    xf = x.astype(jnp.float32)
    x1, x2 = xf[..., :half], xf[..., half:]
    return jnp.concatenate([x1 * cos - x2 * sin, x2 * cos + x1 * sin], axis=-1).astype(x.dtype)


def neighborhood_attention(q, k, v, rpb):
    b, h, s, dh = q.shape
    rows = s // GRID_W
    wr = min(NA_WIN_ROWS, rows)
    n_cb = GRID_W // NA_QBLOCK_COLS
    r = np.arange(rows)
    rs = np.clip(r - wr // 2, 0, rows - wr)
    row_idx = rs[:, None] + np.arange(wr)[None, :]
    row_off = row_idx - r[:, None] + (NA_WIN_ROWS - 1)
    c = np.arange(GRID_W)
    cs = np.clip(c - NA_WIN_COLS // 2, 0, GRID_W - NA_WIN_COLS)
    cb = np.arange(n_cb)
    kc0 = np.clip(cb * NA_QBLOCK_COLS - NA_WIN_COLS // 2, 0, GRID_W - NA_KBLOCK_COLS)
    col_idx = kc0[:, None] + np.arange(NA_KBLOCK_COLS)[None, :]
    qcol = cb[:, None] * NA_QBLOCK_COLS + np.arange(NA_QBLOCK_COLS)[None, :]
    kcol = col_idx[:, None, :]
    qs = cs[qcol][:, :, None]
    col_valid = (kcol >= qs) & (kcol < qs + NA_WIN_COLS)
    col_off = np.clip(kcol - qcol[:, :, None], -(NA_WIN_COLS - 1), NA_WIN_COLS - 1) + (NA_WIN_COLS - 1)
    bias = jnp.take(rpb[:, row_off].astype(jnp.float32), col_off, axis=-1)
    bias = jnp.where(col_valid, bias, NEG_INF).transpose(0, 1, 3, 4, 2, 5)

    qg = (q * (dh ** -0.5)).reshape(b, h, rows, n_cb, NA_QBLOCK_COLS, dh)
    kc = jnp.take(k.reshape(b, h, rows, GRID_W, dh), col_idx, axis=3)
    vc = jnp.take(v.reshape(b, h, rows, GRID_W, dh), col_idx, axis=3)
    scores = jnp.stack(
        [jnp.einsum('bhrcqd,bhrckd->bhrcqk', qg, jnp.take(kc, row_idx[:, i], axis=2)).astype(jnp.float32)
         for i in range(wr)], axis=-2)
    scores = scores + bias[None]
    probs = jax.nn.softmax(scores.reshape(*scores.shape[:-2], wr * NA_KBLOCK_COLS), axis=-1)
    probs = probs.reshape(scores.shape).astype(v.dtype)
    out = jnp.einsum('bhrcqk,bhrckd->bhrcqd', probs[..., 0, :], jnp.take(vc, row_idx[:, 0], axis=2))
    for i in range(1, wr):
        out = out + jnp.einsum('bhrcqk,bhrckd->bhrcqd', probs[..., i, :], jnp.take(vc, row_idx[:, i], axis=2))
    return out.reshape(b, h, s, dh)


def banded_attention(q, k, v, radius):
    *lead, L, dh = q.shape
    blk = radius
    nb = -(-L // blk)
    lp = nb * blk
    nlead = len(lead)
    qb = jnp.pad(q, [(0, 0)] * nlead + [(0, lp - L), (0, 0)]).reshape(*lead, nb, blk, dh)

    def band(t):
        tb = jnp.pad(t, [(0, 0)] * nlead + [(blk, lp - L + blk), (0, 0)]).reshape(*lead, nb + 2, blk, dh)
        return jnp.concatenate([tb[..., :-2, :, :], tb[..., 1:-1, :, :], tb[..., 2:, :, :]], axis=-2)

    kb, vb = band(k), band(v)
    s = jnp.einsum('...nqd,...nkd->...nqk', qb, kb).astype(jnp.float32) * (dh ** -0.5)
    qi = np.arange(lp).reshape(nb, blk)[:, :, None]
    kj = np.arange(nb)[:, None, None] * blk - blk + np.arange(3 * blk)[None, None, :]
    valid = (kj >= 0) & (kj < L) & (np.abs(qi - kj) <= radius)
    s = jnp.where(valid, s, NEG_INF)
    lse = jax.nn.logsumexp(s, axis=-1)
    pr = jnp.exp(s - lse[..., None]).astype(v.dtype)
    o = jnp.einsum('...nqk,...nkd->...nqd', pr, vb)
    return o.reshape(*lead, lp, dh)[..., :L, :], lse.reshape(*lead, lp)[..., :L]


def dilated_attention(q, k, v):
    b, _, s, dh = q.shape
    hg = DIL_HEADS_PER_GROUP
    outs, lses = [], []
    for g, (window, dil) in enumerate(DIL_GROUPS):
        radius = window // (2 * dil)

        def split(t):
            return t[:, g * hg:(g + 1) * hg].reshape(b, hg, s // dil, dil, dh).swapaxes(2, 3)

        o, lse = banded_attention(split(q), split(k), split(v), radius)
        outs.append(o.swapaxes(2, 3).reshape(b, hg, s, dh))
        lses.append(lse.swapaxes(2, 3).reshape(b, hg, s))
    w = jax.nn.softmax(jnp.stack(lses), axis=0).astype(q.dtype)
    return jnp.einsum('gbhs,gbhsd->bhsd', w, jnp.stack(outs))


def _fwd_setup_inputs(seed: int = 0) -> dict:
    key = jax.random.key(seed)
    ks = jax.random.split(key, 20)
    f32 = jnp.float32

    def nrm(k, shape, fan_in):
        return jax.random.normal(k, shape, f32) * (fan_in ** -0.5)

    def gain(k, shape):
        return 1.0 + 0.01 * jax.random.normal(k, shape, f32)

    return {
        "x": jax.random.normal(ks[0], (BATCH, SEQ, D_MODEL), f32),
        "p": jax.random.normal(ks[1], (DEPTH, BATCH, SEQ, PLE_DIM), f32),
        "positions": (jnp.arange(SEQ, dtype=jnp.int32)[None, :]
                      + jax.random.randint(ks[2], (BATCH, 1), 0, 1024, dtype=jnp.int32)),
        "g_mix": gain(ks[3], (DEPTH, D_MODEL)),
        "w_in": nrm(ks[4], (DEPTH, D_MODEL, IN_WIDTH), D_MODEL),
        "rpb": 0.02 * jax.random.normal(ks[5], (DEPTH, NA_HEADS, 2 * NA_WIN_ROWS - 1, 2 * NA_WIN_COLS - 1), f32),
        "w_branch_na": nrm(ks[6], (DEPTH, NA_WIDTH, D_MODEL), NA_WIDTH),
        "w_branch_dil": nrm(ks[7], (DEPTH, DIL_OUT_WIDTH, D_MODEL), DIL_OUT_WIDTH),
        "w_out": nrm(ks[8], (DEPTH, D_MODEL, D_MODEL), D_MODEL),
        "g_mlp": gain(ks[9], (DEPTH, D_MODEL)),
        "w_up": nrm(ks[10], (DEPTH, D_MODEL, D_FF), D_MODEL),
        "w_down": nrm(ks[11], (DEPTH, D_FF, D_MODEL), D_FF),
        "g_ple": gain(ks[12], (DEPTH, D_MODEL)),
        "w_ple_gate": nrm(ks[13], (DEPTH, D_MODEL, D_MODEL), D_MODEL),
        "w_ple_proj": nrm(ks[14], (DEPTH, PLE_DIM, D_MODEL), PLE_DIM),
        "g_final": gain(ks[15], (D_MODEL,)),
    }


def _fwd_reference(x, p, positions, g_mix, w_in, rpb, w_branch_na, w_branch_dil, w_out, g_mlp, w_up, w_down,
              g_ple, w_ple_gate, w_ple_proj, g_final):
    b, s, _ = x.shape
    split_points = [int(v) for v in np.cumsum(IN_WIDTHS)[:-1]]

    def heads(t, n):
        return t.reshape(b, s, n, HEAD_DIM).transpose(0, 2, 1, 3)

    def merge(t):
        return t.transpose(0, 2, 1, 3).reshape(b, s, -1)

    h = x
    for i in range(DEPTH):
        a = rms_norm(h, g_mix[i])
        qa, ka, va, qd, kd, vd, gate_na, gate_dil = jnp.split(a @ w_in[i], split_points, axis=-1)
        y_na = merge(neighborhood_attention(heads(qa, NA_HEADS), heads(ka, NA_HEADS), heads(va, NA_HEADS), rpb[i]))
        y_dil = merge(dilated_attention(rotary(heads(qd, DIL_HEADS), positions),
                                        rotary(heads(kd, DIL_HEADS), positions),
                                        heads(vd, DIL_HEADS)))
        mixed = (jax.nn.sigmoid(gate_na) * (y_na @ w_branch_na[i])
                 + jax.nn.sigmoid(gate_dil) * (y_dil @ w_branch_dil[i]))
        h = h + mixed @ w_out[i]
        c = rms_norm(h, g_mlp[i])
        h = h + jnp.square(jax.nn.relu(c @ w_up[i])) @ w_down[i]
        e = rms_norm(h, g_ple[i])
        h = h + jax.nn.sigmoid(e @ w_ple_gate[i]) * (p[i] @ w_ple_proj[i])
    return rms_norm(h, g_final)


import jax as _jax
import jax.numpy as _jnp

TWIN_FORMAT = 'train_step'
FWD_PARAMS = ['x', 'p', 'positions', 'g_mix', 'w_in', 'rpb', 'w_branch_na', 'w_branch_dil', 'w_out', 'g_mlp', 'w_up', 'w_down', 'g_ple', 'w_ple_gate', 'w_ple_proj', 'g_final']
TWIN_WEIGHTS = ['g_mix', 'w_in', 'rpb', 'w_branch_na', 'w_branch_dil', 'w_out', 'g_mlp', 'w_up', 'w_down', 'g_ple', 'w_ple_gate', 'w_ple_proj', 'g_final']
TWIN_DIFF_INPUT = 'x'
TWIN_INPUTS = ['x', 'p', 'positions', 'g_mix', 'w_in', 'rpb', 'w_branch_na', 'w_branch_dil', 'w_out', 'g_mlp', 'w_up', 'w_down', 'g_ple', 'w_ple_gate', 'w_ple_proj', 'g_final', 'loss_target', 'm_g_mix', 'm_w_in', 'm_rpb', 'm_w_branch_na', 'm_w_branch_dil', 'm_w_out', 'm_g_mlp', 'm_w_up', 'm_w_down', 'm_g_ple', 'm_w_ple_gate', 'm_w_ple_proj', 'm_g_final', 'v_g_mix', 'v_w_in', 'v_rpb', 'v_w_branch_na', 'v_w_branch_dil', 'v_w_out', 'v_g_mlp', 'v_w_up', 'v_w_down', 'v_g_ple', 'v_w_ple_gate', 'v_w_ple_proj', 'v_g_final']
TWIN_OUTPUTS = ['loss', 'grad_x', 'grad_g_mix', 'grad_w_in', 'grad_rpb', 'grad_w_branch_na', 'grad_w_branch_dil', 'grad_w_out', 'grad_g_mlp', 'grad_w_up', 'grad_w_down', 'grad_g_ple', 'grad_w_ple_gate', 'grad_w_ple_proj', 'grad_g_final', 'delta_g_mix', 'delta_w_in', 'delta_rpb', 'delta_w_branch_na', 'delta_w_branch_dil', 'delta_w_out', 'delta_g_mlp', 'delta_w_up', 'delta_w_down', 'delta_g_ple', 'delta_w_ple_gate', 'delta_w_ple_proj', 'delta_g_final', 'new_m_g_mix', 'new_m_w_in', 'new_m_rpb', 'new_m_w_branch_na', 'new_m_w_branch_dil', 'new_m_w_out', 'new_m_g_mlp', 'new_m_w_up', 'new_m_w_down', 'new_m_g_ple', 'new_m_w_ple_gate', 'new_m_w_ple_proj', 'new_m_g_final', 'new_v_g_mix', 'new_v_w_in', 'new_v_rpb', 'new_v_w_branch_na', 'new_v_w_branch_dil', 'new_v_w_out', 'new_v_g_mlp', 'new_v_w_up', 'new_v_w_down', 'new_v_g_ple', 'new_v_w_ple_gate', 'new_v_w_ple_proj', 'new_v_g_final']
TWIN_LEAF_KINDS = {'loss': 'loss', 'grad_x': 'grad_x', 'grad_g_mix': 'grad_w', 'grad_w_in': 'grad_w', 'grad_rpb': 'grad_w', 'grad_w_branch_na': 'grad_w', 'grad_w_branch_dil': 'grad_w', 'grad_w_out': 'grad_w', 'grad_g_mlp': 'grad_w', 'grad_w_up': 'grad_w', 'grad_w_down': 'grad_w', 'grad_g_ple': 'grad_w', 'grad_w_ple_gate': 'grad_w', 'grad_w_ple_proj': 'grad_w', 'grad_g_final': 'grad_w', 'delta_g_mix': 'delta_w', 'delta_w_in': 'delta_w', 'delta_rpb': 'delta_w', 'delta_w_branch_na': 'delta_w', 'delta_w_branch_dil': 'delta_w', 'delta_w_out': 'delta_w', 'delta_g_mlp': 'delta_w', 'delta_w_up': 'delta_w', 'delta_w_down': 'delta_w', 'delta_g_ple': 'delta_w', 'delta_w_ple_gate': 'delta_w', 'delta_w_ple_proj': 'delta_w', 'delta_g_final': 'delta_w', 'new_m_g_mix': 'new_m', 'new_m_w_in': 'new_m', 'new_m_rpb': 'new_m', 'new_m_w_branch_na': 'new_m', 'new_m_w_branch_dil': 'new_m', 'new_m_w_out': 'new_m', 'new_m_g_mlp': 'new_m', 'new_m_w_up': 'new_m', 'new_m_w_down': 'new_m', 'new_m_g_ple': 'new_m', 'new_m_w_ple_gate': 'new_m', 'new_m_w_ple_proj': 'new_m', 'new_m_g_final': 'new_m', 'new_v_g_mix': 'new_v', 'new_v_w_in': 'new_v', 'new_v_rpb': 'new_v', 'new_v_w_branch_na': 'new_v', 'new_v_w_branch_dil': 'new_v', 'new_v_w_out': 'new_v', 'new_v_g_mlp': 'new_v', 'new_v_w_up': 'new_v', 'new_v_w_down': 'new_v', 'new_v_g_ple': 'new_v', 'new_v_w_ple_gate': 'new_v', 'new_v_w_ple_proj': 'new_v', 'new_v_g_final': 'new_v'}


def _forward(args):
    return _fwd_reference(*[args[k] for k in FWD_PARAMS])


def _output_shape():
    out = _jax.eval_shape(lambda: _forward(_fwd_setup_inputs(0)))
    return out.shape, out.dtype

N_MICROBATCH = 1
ADAM_LR = 0.001
ADAM_B1 = 0.9
ADAM_B2 = 0.999
ADAM_EPS = 1e-08
ADAM_WD = 0.01
ADAM_STEP = 10
PER_EXAMPLE_BATCH_AXIS = {'x': 0, 'p': 1, 'positions': 0, 'loss_target': 0}
SHARED_INPUTS = []
_WEIGHT_DTYPES = {'g_mix': _jnp.float32, 'w_in': _jnp.float32, 'rpb': _jnp.float32, 'w_branch_na': _jnp.float32, 'w_branch_dil': _jnp.float32, 'w_out': _jnp.float32, 'g_mlp': _jnp.float32, 'w_up': _jnp.float32, 'w_down': _jnp.float32, 'g_ple': _jnp.float32, 'w_ple_gate': _jnp.float32, 'w_ple_proj': _jnp.float32, 'g_final': _jnp.float32}
MOMENT_SCALE = {'g_mix': 4.035674e-02, 'w_in': 1.653246e-02, 'rpb': 9.546617e-03, 'w_branch_na': 1.971101e-02, 'w_branch_dil': 1.190653e-02, 'w_out': 2.303789e-02, 'g_mlp': 1.570681e-01, 'w_up': 7.543820e-02, 'w_down': 1.324548e-01, 'g_ple': 2.348947e-02, 'w_ple_gate': 2.272743e-02, 'w_ple_proj': 5.738017e-02, 'g_final': 3.223297e+01}


def _to_microbatches(a, axis):
    t = _jnp.moveaxis(a, axis, 0)
    t = t.reshape((N_MICROBATCH, t.shape[0] // N_MICROBATCH) + t.shape[1:])
    return _jnp.moveaxis(t, 1, axis + 1)


def setup_inputs(seed: int = 0) -> dict:
    inp = _fwd_setup_inputs(seed)
    key = _jax.random.fold_in(_jax.random.key(seed), 7919)
    shape, _ = _output_shape()
    out = dict(inp)
    out["loss_target"] = _jax.random.normal(_jax.random.fold_in(key, 0), shape, _jnp.float32)
    for i, name in enumerate(TWIN_WEIGHTS):
        w = inp[name].astype(_jnp.float32)
        if MOMENT_SCALE is None:
            s = _jnp.sqrt(_jnp.mean(_jnp.square(w)) + 1e-30)
        else:
            s = MOMENT_SCALE[name]
        km, kv = _jax.random.split(_jax.random.fold_in(key, i + 1))
        out[name] = w
        out["m_" + name] = s * _jax.random.normal(km, w.shape, _jnp.float32)
        out["v_" + name] = (s * s) * _jax.random.uniform(kv, w.shape, _jnp.float32, 0.5, 1.5)
    if N_MICROBATCH > 1:
        for name, axis in PER_EXAMPLE_BATCH_AXIS.items():
            out[name] = _to_microbatches(out[name], axis)
    return {'x': out['x'], 'p': out['p'], 'positions': out['positions'], 'g_mix': out['g_mix'], 'w_in': out['w_in'], 'rpb': out['rpb'], 'w_branch_na': out['w_branch_na'], 'w_branch_dil': out['w_branch_dil'], 'w_out': out['w_out'], 'g_mlp': out['g_mlp'], 'w_up': out['w_up'], 'w_down': out['w_down'], 'g_ple': out['g_ple'], 'w_ple_gate': out['w_ple_gate'], 'w_ple_proj': out['w_ple_proj'], 'g_final': out['g_final'], 'loss_target': out['loss_target'], 'm_g_mix': out['m_g_mix'], 'm_w_in': out['m_w_in'], 'm_rpb': out['m_rpb'], 'm_w_branch_na': out['m_w_branch_na'], 'm_w_branch_dil': out['m_w_branch_dil'], 'm_w_out': out['m_w_out'], 'm_g_mlp': out['m_g_mlp'], 'm_w_up': out['m_w_up'], 'm_w_down': out['m_w_down'], 'm_g_ple': out['m_g_ple'], 'm_w_ple_gate': out['m_w_ple_gate'], 'm_w_ple_proj': out['m_w_ple_proj'], 'm_g_final': out['m_g_final'], 'v_g_mix': out['v_g_mix'], 'v_w_in': out['v_w_in'], 'v_rpb': out['v_rpb'], 'v_w_branch_na': out['v_w_branch_na'], 'v_w_branch_dil': out['v_w_branch_dil'], 'v_w_out': out['v_w_out'], 'v_g_mlp': out['v_g_mlp'], 'v_w_up': out['v_w_up'], 'v_w_down': out['v_w_down'], 'v_g_ple': out['v_g_ple'], 'v_w_ple_gate': out['v_w_ple_gate'], 'v_w_ple_proj': out['v_w_ple_proj'], 'v_g_final': out['v_g_final']}


def _loss(weights, diff, rest, loss_target):
    with _jax.named_scope("forward"):
        args = {**rest, TWIN_DIFF_INPUT: diff, **{k: w.astype(_WEIGHT_DTYPES[k]) for k, w in weights.items()}}
        y = _forward(args)
    with _jax.named_scope("loss_head"):
        err = _jnp.square(y.astype(_jnp.float32) - loss_target)
        return 0.5 * _jnp.sum(_jnp.mean(err, axis=-1)) if err.ndim else 0.5 * err


def _adamw(w, g, m, v):
    m = ADAM_B1 * m + (1.0 - ADAM_B1) * g
    v = ADAM_B2 * v + (1.0 - ADAM_B2) * _jnp.square(g)
    m_hat = m / (1.0 - ADAM_B1 ** ADAM_STEP)
    v_hat = v / (1.0 - ADAM_B2 ** ADAM_STEP)
    delta = -ADAM_LR * (m_hat / (_jnp.sqrt(v_hat) + ADAM_EPS) + ADAM_WD * w)
    return delta, m, v


def reference(x, p, positions, g_mix, w_in, rpb, w_branch_na, w_branch_dil, w_out, g_mlp, w_up, w_down, g_ple, w_ple_gate, w_ple_proj, g_final, loss_target, m_g_mix, m_w_in, m_rpb, m_w_branch_na, m_w_branch_dil, m_w_out, m_g_mlp, m_w_up, m_w_down, m_g_ple, m_w_ple_gate, m_w_ple_proj, m_g_final, v_g_mix, v_w_in, v_rpb, v_w_branch_na, v_w_branch_dil, v_w_out, v_g_mlp, v_w_up, v_w_down, v_g_ple, v_w_ple_gate, v_w_ple_proj, v_g_final):
    given = dict(x=x, p=p, positions=positions, g_mix=g_mix, w_in=w_in, rpb=rpb, w_branch_na=w_branch_na, w_branch_dil=w_branch_dil, w_out=w_out, g_mlp=g_mlp, w_up=w_up, w_down=w_down, g_ple=g_ple, w_ple_gate=w_ple_gate, w_ple_proj=w_ple_proj, g_final=g_final, loss_target=loss_target, m_g_mix=m_g_mix, m_w_in=m_w_in, m_rpb=m_rpb, m_w_branch_na=m_w_branch_na, m_w_branch_dil=m_w_branch_dil, m_w_out=m_w_out, m_g_mlp=m_g_mlp, m_w_up=m_w_up, m_w_down=m_w_down, m_g_ple=m_g_ple, m_w_ple_gate=m_w_ple_gate, m_w_ple_proj=m_w_ple_proj, m_g_final=m_g_final, v_g_mix=v_g_mix, v_w_in=v_w_in, v_rpb=v_rpb, v_w_branch_na=v_w_branch_na, v_w_branch_dil=v_w_branch_dil, v_w_out=v_w_out, v_g_mlp=v_g_mlp, v_w_up=v_w_up, v_w_down=v_w_down, v_g_ple=v_g_ple, v_w_ple_gate=v_w_ple_gate, v_w_ple_proj=v_w_ple_proj, v_g_final=v_g_final)
    weights = {n: given[n] for n in TWIN_WEIGHTS}
    shared = {n: given[n] for n in SHARED_INPUTS}
    per_example = {n: given[n] for n in ['x', 'p', 'positions']}
    grad_fn = _jax.value_and_grad(_loss, argnums=(0, 1))

    def one_microbatch(ex, loss_target):
        ex = dict(ex)
        diff = ex.pop(TWIN_DIFF_INPUT)
        return grad_fn(weights, diff, {**shared, **ex}, loss_target)

    if N_MICROBATCH == 1:
        loss, (grad_w, grad_x) = one_microbatch(per_example, given["loss_target"])
    else:
        def body(carry, xs):
            loss_sum, grad_sum = carry
            l_k, (gw_k, gx_k) = one_microbatch(xs[0], xs[1])
            with _jax.named_scope("update"):
                return (loss_sum + l_k, _jax.tree.map(_jnp.add, grad_sum, gw_k)), gx_k

        init = (_jnp.zeros((), _jnp.float32), _jax.tree.map(_jnp.zeros_like, weights))
        (loss, grad_w), grad_x = _jax.lax.scan(body, init, (per_example, given["loss_target"]))
    with _jax.named_scope("update"):
        delta_w, new_m, new_v = {}, {}, {}
        for n in TWIN_WEIGHTS:
            delta_w[n], new_m[n], new_v[n] = _adamw(weights[n], grad_w[n], given["m_" + n], given["v_" + n])
    return (loss, grad_x, *[grad_w[n] for n in TWIN_WEIGHTS], *[delta_w[n] for n in TWIN_WEIGHTS],
            *[new_m[n] for n in TWIN_WEIGHTS], *[new_v[n] for n in TWIN_WEIGHTS])
```

```python
import functools

import numpy as np
import jax
import jax.numpy as jnp
from jax import lax
from jax.experimental import pallas as pl
from jax.experimental.pallas import tpu as pltpu

BF = jnp.bfloat16
F32 = jnp.float32
MESH = pl.DeviceIdType.MESH
ANY = pl.BlockSpec(memory_space=pl.ANY)

V7X_VMEM_BYTES = 64 * 1024 * 1024
VMEM_LIMIT = V7X_VMEM_BYTES - 16 * 1024 * 1024

D_MODEL = 1024
HEAD_DIM = 64
GRID_W = 64
NA_HEADS = 8
NA_WIN_ROWS = 8
NA_WIN_COLS = 16
NA_WIDTH = NA_HEADS * HEAD_DIM
DIL_GROUPS = ((128, 1), (512, 4), (2048, 16))
DIL_HPG = 4
DIL_HEADS = DIL_HPG * len(DIL_GROUPS)
DIL_WIDTH = DIL_HEADS * HEAD_DIM
DIL_OUT_WIDTH = DIL_HPG * HEAD_DIM
DIL_RADIUS = 64
QKV_WIDTH = 3 * NA_WIDTH + 3 * DIL_WIDTH
D_FF = 4 * D_MODEL
PLE_DIM = 256
ROPE_THETA = 10000.0
RMS_EPS = 1e-6
NEG_INF = -1e30
Q_SCALE = HEAD_DIM ** -0.5

ADAM_LR = 0.001
ADAM_B1 = 0.9
ADAM_B2 = 0.999
ADAM_EPS = 1e-08
ADAM_WD = 0.01
ADAM_STEP = 10

N_CHIPS = 4
N_DEV = 8
PACK_W = 1024
BIG = ("w_in", "w_branch_na", "w_branch_dil", "w_out", "w_up", "w_down", "w_ple_gate", "w_ple_proj")
SMALL_ROWS = 16


def _cparams(sem=None):
    return pltpu.CompilerParams(dimension_semantics=sem, vmem_limit_bytes=VMEM_LIMIT)


def _mm(name, a, b, mode, tm, tn, tk, out_dtypes, epilogue=None, extras=()):
    if mode == "nn":
        (M, K), N = a.shape, b.shape[1]
        a_spec = pl.BlockSpec((tm, tk), lambda i, j, k: (i, k))
        b_spec = pl.BlockSpec((tk, tn), lambda i, j, k: (k, j))
        dims = (((1,), (0,)), ((), ()))
    elif mode == "nt":
        (M, K), N = a.shape, b.shape[0]
        a_spec = pl.BlockSpec((tm, tk), lambda i, j, k: (i, k))
        b_spec = pl.BlockSpec((tn, tk), lambda i, j, k: (j, k))
        dims = (((1,), (1,)), ((), ()))
    else:
        (K, M), N = a.shape, b.shape[1]
        a_spec = pl.BlockSpec((tk, tm), lambda i, j, k: (k, i))
        b_spec = pl.BlockSpec((tk, tn), lambda i, j, k: (k, j))
        dims = (((0,), (0,)), ((), ()))
    tm, tn, tk = min(tm, M), min(tn, N), min(tk, K)
    assert M % tm == 0 and N % tn == 0 and K % tk == 0, (name, M, N, K, tm, tn, tk)
    nk = K // tk
    n_extra, n_out = len(extras), len(out_dtypes)
    tile = pl.BlockSpec((tm, tn), lambda i, j, k: (i, j))

    def body(a_ref, b_ref, *rest):
        extra_refs, out_refs, acc = rest[:n_extra], rest[n_extra:n_extra + n_out], rest[-1]
        k = pl.program_id(2)

        @pl.when(k == 0)
        def _():
            acc[...] = jnp.zeros_like(acc)

        acc[...] += lax.dot_general(a_ref[...].astype(BF), b_ref[...].astype(BF), dims, preferred_element_type=F32)

        @pl.when(k == nk - 1)
        def _():
            outs = (acc[...],) if epilogue is None else epilogue(acc[...], *[e[...] for e in extra_refs])
            for o_ref, val in zip(out_refs, outs, strict=True):
                o_ref[...] = val.astype(o_ref.dtype)

    outs = pl.pallas_call(
        body, name=name, grid=(M // tm, N // tn, nk),
        in_specs=[a_spec, b_spec] + [tile] * n_extra,
        out_specs=[tile] * n_out,
        out_shape=[jax.ShapeDtypeStruct((M, N), dt) for dt in out_dtypes],
        scratch_shapes=[pltpu.VMEM((tm, tn), F32)],
        compiler_params=_cparams(("parallel", "parallel", "arbitrary")),
    )(a, b, *extras)
    return outs[0] if n_out == 1 else outs


def _row(arr, tm, col_block=None, width=None):
    width = arr.shape[1] if width is None else width
    cb = 0 if col_block is None else col_block
    return arr, pl.BlockSpec((tm, width), lambda i: (i, cb))


def _full(arr):
    nd = arr.ndim
    return arr, pl.BlockSpec(arr.shape, lambda i: (0,) * nd)


def _rowwise(name, body, T, tm, ins, outs, sums=()):
    n_in, n_out, n_sum = len(ins), len(outs), len(sums)

    def kern(*refs):
        in_refs, out_refs, sum_refs = refs[:n_in], refs[n_in:n_in + n_out], refs[n_in + n_out:]
        res = body(*[r[...] for r in in_refs])
        res = res if isinstance(res, tuple) else (res,)
        for o_ref, val in zip(out_refs, res[:n_out], strict=True):
            o_ref[...] = val.astype(o_ref.dtype)
        if n_sum:
            @pl.when(pl.program_id(0) == 0)
            def _():
                for s_ref in sum_refs:
                    s_ref[...] = jnp.zeros_like(s_ref)

            for s_ref, val in zip(sum_refs, res[n_out:], strict=True):
                s_ref[...] += val

    res = pl.pallas_call(
        kern, name=name, grid=(T // tm,),
        in_specs=[spec for _, spec in ins],
        out_specs=[pl.BlockSpec((tm, c), lambda i: (i, 0)) for c, _ in outs]
        + [pl.BlockSpec((1, c), lambda i: (0, 0)) for c in sums],
        out_shape=[jax.ShapeDtypeStruct((T, c), dt) for c, dt in outs]
        + [jax.ShapeDtypeStruct((1, c), F32) for c in sums],
        compiler_params=_cparams(("arbitrary",)),
    )(*[a for a, _ in ins])
    return res[0] if len(res) == 1 else res


def _sigmoid(x):
    return 1.0 / (1.0 + jnp.exp(-x))


def _rms(h):
    return lax.rsqrt(jnp.mean(h * h, axis=-1, keepdims=True) + RMS_EPS)


def _rms_bwd(dy, h, g):
    r = _rms(h)
    n = h * r
    dn = dy * g
    dh = r * (dn - n * jnp.mean(dn * n, axis=-1, keepdims=True))
    return dh, jnp.sum(dy * n, axis=0, keepdims=True)


def _rope(x, cos2, sin_signed):
    lane = lax.broadcasted_iota(jnp.int32, x.shape, 1)
    swapped = jnp.where((lane % HEAD_DIM) < HEAD_DIM // 2, pltpu.roll(x, 128 - HEAD_DIM // 2, 1), pltpu.roll(x, HEAD_DIM // 2, 1))
    return x * cos2 + swapped * sin_signed


def _rope_cols(x, cos2, sin_signed):
    return jnp.concatenate([_rope(x[:, c:c + 128], cos2, sin_signed) for c in range(0, x.shape[1], 128)], axis=1)


NA_KEYS = NA_WIN_ROWS * GRID_W
NA_BASES = 8


def _na_row_geometry(r, rows):
    first = jnp.clip(r - NA_WIN_ROWS // 2, 0, rows - NA_WIN_ROWS)
    base = first - r + (NA_WIN_ROWS - 1)
    return pl.multiple_of(first * GRID_W, GRID_W), base


def _na_probs(q, kw, bias):
    s = lax.dot_general(q, kw, (((1,), (1,)), ((), ())), preferred_element_type=F32) + bias
    p = jnp.exp(s - jnp.max(s, axis=-1, keepdims=True))
    return p / jnp.sum(p, axis=-1, keepdims=True)


def _na_fwd(q, k, v, tab):
    H, T, _ = q.shape
    rows = T // GRID_W

    def body(q_ref, k_ref, v_ref, tab_ref, y_ref):
        def step(r, carry):
            k0, base = _na_row_geometry(r, rows)
            q0 = pl.multiple_of(r * GRID_W, GRID_W)
            p = _na_probs(q_ref[pl.ds(q0, GRID_W), :], k_ref[pl.ds(k0, NA_KEYS), :], tab_ref[base])
            y = jnp.dot(p.astype(BF), v_ref[pl.ds(k0, NA_KEYS), :], preferred_element_type=F32)
            y_ref[pl.ds(q0, GRID_W), :] = y.astype(y_ref.dtype)
            return carry

        lax.fori_loop(0, rows, step, 0)

    head = pl.BlockSpec((None, T, HEAD_DIM), lambda h: (h, 0, 0))
    return pl.pallas_call(
        body, name="na_fwd", grid=(H,),
        in_specs=[head, head, head, pl.BlockSpec((None, NA_BASES, GRID_W, NA_KEYS), lambda h: (h, 0, 0, 0))],
        out_specs=head, out_shape=jax.ShapeDtypeStruct((H, T, HEAD_DIM), BF),
        compiler_params=_cparams(("parallel",)),
    )(q, k, v, tab)


def _na_bwd(q, k, v, tab, do):
    H, T, _ = q.shape
    rows = T // GRID_W

    def body(q_ref, k_ref, v_ref, tab_ref, do_ref, dq_ref, dk_ref, dv_ref, dtab_ref):
        dk_ref[...] = jnp.zeros_like(dk_ref)
        dv_ref[...] = jnp.zeros_like(dv_ref)
        dtab_ref[...] = jnp.zeros_like(dtab_ref)

        def step(r, carry):
            k0, base = _na_row_geometry(r, rows)
            q0 = pl.multiple_of(r * GRID_W, GRID_W)
            qr, kw, vw = q_ref[pl.ds(q0, GRID_W), :], k_ref[pl.ds(k0, NA_KEYS), :], v_ref[pl.ds(k0, NA_KEYS), :]
            dor = do_ref[pl.ds(q0, GRID_W), :]
            p = _na_probs(qr, kw, tab_ref[base])
            dv_ref[pl.ds(k0, NA_KEYS), :] += lax.dot_general(p.astype(BF), dor, (((0,), (0,)), ((), ())), preferred_element_type=F32)
            dp = lax.dot_general(dor, vw, (((1,), (1,)), ((), ())), preferred_element_type=F32)
            ds = p * (dp - jnp.sum(dp * p, axis=-1, keepdims=True))
            dtab_ref[base] += ds
            dsb = ds.astype(BF)
            dq_ref[pl.ds(q0, GRID_W), :] = jnp.dot(dsb, kw, preferred_element_type=F32)
            dk_ref[pl.ds(k0, NA_KEYS), :] += lax.dot_general(dsb, qr, (((0,), (0,)), ((), ())), preferred_element_type=F32)
            return carry

        lax.fori_loop(0, rows, step, 0)

    head = pl.BlockSpec((None, T, HEAD_DIM), lambda h: (h, 0, 0))
    tabs = pl.BlockSpec((None, NA_BASES, GRID_W, NA_KEYS), lambda h: (h, 0, 0, 0))
    hd = jax.ShapeDtypeStruct((H, T, HEAD_DIM), F32)
    return pl.pallas_call(
        body, name="na_bwd", grid=(H,),
        in_specs=[head, head, head, tabs, head],
        out_specs=[head, head, head, tabs],
        out_shape=[hd, hd, hd, jax.ShapeDtypeStruct((H, NA_BASES, GRID_W, NA_KEYS), F32)],
        compiler_params=_cparams(("parallel",)),
    )(q, k, v, tab, do)


def _na_geometry_np():
    c = np.arange(GRID_W)
    cs = np.clip(c - NA_WIN_COLS // 2, 0, GRID_W - NA_WIN_COLS)
    valid = (c[None, :] >= cs[:, None]) & (c[None, :] < cs[:, None] + NA_WIN_COLS)
    off = c[None, :] - c[:, None] + (NA_WIN_COLS - 1)
    oh_col = np.zeros((GRID_W, GRID_W, 2 * NA_WIN_COLS - 1), np.float32)
    qq, kk = np.nonzero(valid)
    oh_col[qq, kk, off[qq, kk]] = 1.0
    oh_row = np.zeros((NA_BASES, NA_WIN_ROWS, 2 * NA_WIN_ROWS - 1), np.float32)
    for base in range(NA_BASES):
        for i in range(NA_WIN_ROWS):
            oh_row[base, i, base + i] = 1.0
    return valid, oh_col, oh_row


def _na_bias_table(rpb):
    valid, oh_col, oh_row = _na_geometry_np()
    hi = lax.Precision.HIGHEST
    t1 = jnp.einsum("hrc,pir->hpic", rpb, oh_row, precision=hi)
    tab = jnp.einsum("hpic,qkc->hpqik", t1, oh_col, precision=hi)
    tab = jnp.where(valid[None, None, :, None, :], tab, NEG_INF)
    return tab.reshape(rpb.shape[0], NA_BASES, GRID_W, NA_KEYS)


def _na_rpb_grad(dtab):
    H = dtab.shape[0]
    n_rows = 2 * NA_WIN_ROWS - 1
    n_cols = 2 * NA_WIN_COLS - 1

    def body(d_ref, o_ref):
        lane = lax.broadcasted_iota(jnp.int32, (GRID_W, 128), 1)
        low = lane < GRID_W
        out_rows = []
        for ro in range(n_rows):
            acc = jnp.zeros((GRID_W, 128), F32)
            for base in range(NA_BASES):
                i = ro - base
                if not 0 <= i < NA_WIN_ROWS:
                    continue
                pair = d_ref[base, :, pl.ds((i // 2) * 128, 128)]
                if i % 2:
                    pair = pltpu.roll(pair, GRID_W, 1)
                acc = acc + jnp.where(low, pair, 0.0)
            skew = pltpu.roll(acc, 0, 1, stride=1, stride_axis=0)
            diag = jnp.sum(skew, axis=0, keepdims=True)
            out_rows.append(pltpu.roll(jnp.broadcast_to(diag, (8, 128)), 128 - (GRID_W - NA_WIN_COLS), 1)[:1])
        out_rows.append(jnp.zeros((1, 128), F32))
        res = jnp.concatenate(out_rows, axis=0)
        o_ref[...] = jnp.where(lax.broadcasted_iota(jnp.int32, res.shape, 1) < n_cols, res, 0.0)

    return pl.pallas_call(
        body, name="na_rpb_grad", grid=(H,),
        in_specs=[pl.BlockSpec((None, NA_BASES, GRID_W, NA_KEYS), lambda h: (h, 0, 0, 0))],
        out_specs=pl.BlockSpec((None, n_rows + 1, 128), lambda h: (h, 0, 0)),
        out_shape=jax.ShapeDtypeStruct((H, n_rows + 1, 128), F32),
        compiler_params=_cparams(("parallel",)),
    )(jnp.flip(dtab, axis=2))


BAND_Q = 128
BAND_KEYS = BAND_Q + 2 * DIL_RADIUS


def _band_geometry(n, T, seg_shift):
    q0 = pl.multiple_of(n * BAND_Q, BAND_Q)
    k0 = pl.multiple_of(jnp.clip(q0 - DIL_RADIUS, 0, T - BAND_KEYS), DIL_RADIUS)
    qi = q0 + lax.broadcasted_iota(jnp.int32, (BAND_Q, BAND_KEYS), 0)
    kj = k0 + lax.broadcasted_iota(jnp.int32, (BAND_Q, BAND_KEYS), 1)
    valid = ((qi >> seg_shift) == (kj >> seg_shift)) & (jnp.abs(qi - kj) <= DIL_RADIUS)
    return q0, k0, valid


def _band_probs(q, kw, valid):
    s = lax.dot_general(q, kw, (((1,), (1,)), ((), ())), preferred_element_type=F32)
    s = jnp.where(valid, s, NEG_INF)
    m = jnp.max(s, axis=-1, keepdims=True)
    p = jnp.exp(s - m)
    l = jnp.sum(p, axis=-1, keepdims=True)
    return p / l, m + jnp.log(l)


def _band_fwd(q, k, v, group):
    _, T, _ = q.shape
    seg_shift = (T // DIL_GROUPS[group][1]).bit_length() - 1

    def body(q_ref, k_ref, v_ref, o_ref, lse_ref):
        def step(n, carry):
            q0, k0, valid = _band_geometry(n, T, seg_shift)
            p, lse = _band_probs(q_ref[pl.ds(q0, BAND_Q), :], k_ref[pl.ds(k0, BAND_KEYS), :], valid)
            o_ref[pl.ds(q0, BAND_Q), :] = jnp.dot(p.astype(BF), v_ref[pl.ds(k0, BAND_KEYS), :], preferred_element_type=F32)
            lse_ref[pl.ds(q0, BAND_Q), :] = lse
            return carry

        lax.fori_loop(0, T // BAND_Q, step, 0)

    head = pl.BlockSpec((None, T, HEAD_DIM), lambda h: (group * DIL_HPG + h, 0, 0))
    out = pl.BlockSpec((None, T, HEAD_DIM), lambda h: (h, 0, 0))
    col = pl.BlockSpec((None, T, 1), lambda h: (h, 0, 0))
    return pl.pallas_call(
        body, name=f"band_fwd_g{group}", grid=(DIL_HPG,),
        in_specs=[head, head, head], out_specs=[out, col],
        out_shape=[jax.ShapeDtypeStruct((DIL_HPG, T, HEAD_DIM), F32), jax.ShapeDtypeStruct((DIL_HPG, T, 1), F32)],
        compiler_params=_cparams(("parallel",)),
    )(q, k, v)


def _band_bwd(q, k, v, do, dlse, group):
    _, T, _ = q.shape
    seg_shift = (T // DIL_GROUPS[group][1]).bit_length() - 1

    def body(q_ref, k_ref, v_ref, do_ref, dlse_ref, dq_ref, dk_ref, dv_ref):
        dk_ref[...] = jnp.zeros_like(dk_ref)
        dv_ref[...] = jnp.zeros_like(dv_ref)

        def step(n, carry):
            q0, k0, valid = _band_geometry(n, T, seg_shift)
            qr, kw, vw = q_ref[pl.ds(q0, BAND_Q), :], k_ref[pl.ds(k0, BAND_KEYS), :], v_ref[pl.ds(k0, BAND_KEYS), :]
            dor = do_ref[pl.ds(q0, BAND_Q), :]
            p, _ = _band_probs(qr, kw, valid)
            dv_ref[pl.ds(k0, BAND_KEYS), :] += lax.dot_general(p.astype(BF), dor, (((0,), (0,)), ((), ())), preferred_element_type=F32)
            dp = lax.dot_general(dor, vw, (((1,), (1,)), ((), ())), preferred_element_type=F32)
            ds = p * (dp - jnp.sum(dp * p, axis=-1, keepdims=True) + dlse_ref[pl.ds(q0, BAND_Q), :])
            dsb = ds.astype(BF)
            dq_ref[pl.ds(q0, BAND_Q), :] = jnp.dot(dsb, kw, preferred_element_type=F32)
            dk_ref[pl.ds(k0, BAND_KEYS), :] += lax.dot_general(dsb, qr, (((0,), (0,)), ((), ())), preferred_element_type=F32)
            return carry

        lax.fori_loop(0, T // BAND_Q, step, 0)

    head = pl.BlockSpec((None, T, HEAD_DIM), lambda h: (group * DIL_HPG + h, 0, 0))
    col = pl.BlockSpec((None, T, 1), lambda h: (group * DIL_HPG + h, 0, 0))
    out = pl.BlockSpec((None, T, HEAD_DIM), lambda h: (h, 0, 0))
    hd = jax.ShapeDtypeStruct((DIL_HPG, T, HEAD_DIM), F32)
    return pl.pallas_call(
        body, name=f"band_bwd_g{group}", grid=(DIL_HPG,),
        in_specs=[head, head, head, head, col], out_specs=[out, out, out], out_shape=[hd, hd, hd],
        compiler_params=_cparams(("parallel",)),
    )(q, k, v, do, dlse)


def _dil_merge_fwd(o, lse, tm):
    G = len(DIL_GROUPS)
    _, T, _ = o.shape

    def body(o_ref, lse_ref, y_ref, w_ref):
        for j in range(DIL_HPG):
            ls = [lse_ref[g * DIL_HPG + j] for g in range(G)]
            m = functools.reduce(jnp.maximum, ls)
            es = [jnp.exp(l - m) for l in ls]
            tot = functools.reduce(jnp.add, es)
            ws = [e / tot for e in es]
            y_ref[j] = functools.reduce(jnp.add, [ws[g] * o_ref[g * DIL_HPG + j] for g in range(G)]).astype(y_ref.dtype)
            for g in range(G):
                w_ref[g * DIL_HPG + j] = ws[g]

    return pl.pallas_call(
        body, name="dil_merge_fwd", grid=(T // tm,),
        in_specs=[pl.BlockSpec((DIL_HEADS, tm, HEAD_DIM), lambda i: (0, i, 0)), pl.BlockSpec((DIL_HEADS, tm, 1), lambda i: (0, i, 0))],
        out_specs=[pl.BlockSpec((DIL_HPG, tm, HEAD_DIM), lambda i: (0, i, 0)), pl.BlockSpec((DIL_HEADS, tm, 1), lambda i: (0, i, 0))],
        out_shape=[jax.ShapeDtypeStruct((DIL_HPG, T, HEAD_DIM), BF), jax.ShapeDtypeStruct((DIL_HEADS, T, 1), F32)],
        compiler_params=_cparams(("parallel",)),
    )(o, lse)


def _dil_merge_bwd(dy, o, w, tm):
    G = len(DIL_GROUPS)
    _, T, _ = o.shape

    def body(dy_ref, o_ref, w_ref, do_ref, dlse_ref):
        for j in range(DIL_HPG):
            dyj = dy_ref[j]
            ws = [w_ref[g * DIL_HPG + j] for g in range(G)]
            dws = [jnp.sum(dyj * o_ref[g * DIL_HPG + j], axis=-1, keepdims=True) for g in range(G)]
            mean = functools.reduce(jnp.add, [ws[g] * dws[g] for g in range(G)])
            for g in range(G):
                do_ref[g * DIL_HPG + j] = (ws[g] * dyj).astype(do_ref.dtype)
                dlse_ref[g * DIL_HPG + j] = ws[g] * (dws[g] - mean)

    heads = pl.BlockSpec((DIL_HEADS, tm, HEAD_DIM), lambda i: (0, i, 0))
    cols = pl.BlockSpec((DIL_HEADS, tm, 1), lambda i: (0, i, 0))
    return pl.pallas_call(
        body, name="dil_merge_bwd", grid=(T // tm,),
        in_specs=[pl.BlockSpec((DIL_HPG, tm, HEAD_DIM), lambda i: (0, i, 0)), heads, cols],
        out_specs=[heads, cols],
        out_shape=[jax.ShapeDtypeStruct((DIL_HEADS, T, HEAD_DIM), BF), jax.ShapeDtypeStruct((DIL_HEADS, T, 1), F32)],
        compiler_params=_cparams(("parallel",)),
    )(dy, o, w)


def _to_heads(t):
    T, W = t.shape
    return t.reshape(T, W // HEAD_DIM, HEAD_DIM).transpose(1, 0, 2)


def _from_heads(t):
    H, T, dh = t.shape
    return t.transpose(1, 0, 2).reshape(T, H * dh)


def _to_segments(t):
    _, T, w = t.shape
    parts = []
    for g, (_, dil) in enumerate(DIL_GROUPS):
        part = t[g * DIL_HPG:(g + 1) * DIL_HPG]
        parts.append(part.reshape(DIL_HPG, T // dil, dil, w).transpose(0, 2, 1, 3).reshape(DIL_HPG, T, w))
    return jnp.concatenate(parts, axis=0)


def _from_segments(parts):
    out = []
    for (_, dil), part in zip(DIL_GROUPS, parts, strict=True):
        _, T, w = part.shape
        out.append(part.reshape(DIL_HPG, dil, T // dil, w).transpose(0, 2, 1, 3).reshape(DIL_HPG, T, w))
    return jnp.concatenate(out, axis=0)


def _rope_tables(positions):
    half = HEAD_DIM // 2
    inv_freq = ROPE_THETA ** (-jnp.arange(half, dtype=F32) / half)
    ang = positions.astype(F32)[:, None] * inv_freq
    cos, sin = jnp.cos(ang), jnp.sin(ang)
    return jnp.tile(jnp.concatenate([cos, cos], axis=1), (1, 2)), jnp.tile(jnp.concatenate([-sin, sin], axis=1), (1, 2))


def _pack_rows(t):
    return t.reshape(-1, PACK_W)


def _me():
    return lax.axis_index("x"), lax.axis_index("y"), lax.axis_index("c")


def _other_chips(x, y):
    return [(1 - x, y), (x, 1 - y), (1 - x, 1 - y)]


def _gather_weights(packed):
    R, W = packed.shape
    half = R // 2

    def body(in_ref, out_ref, send_sems, recv_sems, local_sem):
        x, y, c = _me()
        sibling = (x, y, 1 - c)
        chips = _other_chips(x, y)

        def block(chip, core):
            return out_ref.at[2 * chip[0] + chip[1], pl.ds(core * half, half), :]

        def copy(k, chip, core, to, src=None):
            return pltpu.make_async_remote_copy(
                src_ref=block(chip, core) if src is None else src, dst_ref=block(chip, core),
                send_sem=send_sems.at[k], recv_sem=recv_sems.at[k], device_id=to, device_id_type=MESH)

        mine = pltpu.make_async_copy(in_ref, out_ref.at[2 * x + y], local_sem)
        mine.start()
        first = [copy(j, (x, y), c, (*chip, c), src=in_ref.at[pl.ds(c * half, half), :]) for j, chip in enumerate(chips)]
        for cp in first:
            cp.start()
        passed = [copy(3 + j, chip, c, sibling) for j, chip in enumerate(chips)]
        for j, chip in enumerate(chips):
            copy(j, chip, c, (x, y, c)).wait_recv()
            passed[j].start()
        for j, chip in enumerate(chips):
            copy(3 + j, chip, 1 - c, (x, y, c)).wait_recv()
        for cp in first + passed:
            cp.wait_send()
        mine.wait()

    return pl.pallas_call(
        body, name="gather_weights",
        in_specs=[ANY], out_specs=ANY,
        out_shape=jax.ShapeDtypeStruct((N_CHIPS, R, W), packed.dtype),
        scratch_shapes=[pltpu.SemaphoreType.DMA((6,)), pltpu.SemaphoreType.DMA((6,)), pltpu.SemaphoreType.DMA],
    )(packed)


def _swap_halves(g):
    S, R, W = g.shape
    half = R // 2

    def body(g_ref, out_ref, send_sem, recv_sem):
        x, y, c = _me()
        cp = pltpu.make_async_remote_copy(
            src_ref=g_ref.at[:, pl.ds((1 - c) * half, half), :], dst_ref=out_ref,
            send_sem=send_sem, recv_sem=recv_sem, device_id=(x, y, 1 - c), device_id_type=MESH)
        cp.start()
        cp.wait()

    return pl.pallas_call(
        body, name="swap_halves", in_specs=[ANY], out_specs=ANY,
        out_shape=jax.ShapeDtypeStruct((S, half, W), g.dtype),
        scratch_shapes=[pltpu.SemaphoreType.DMA, pltpu.SemaphoreType.DMA],
    )(g)


def _pair_sum(g, got, tm):
    S, R, W = g.shape
    half = R // 2
    nb = half // tm

    def body(g_ref, got_ref, o_ref):
        o_ref[...] = g_ref[...] + got_ref[...]

    return pl.pallas_call(
        body, name="pair_sum", grid=(S, nb),
        in_specs=[pl.BlockSpec((None, tm, W), lambda s, i: (s, lax.axis_index("c") * nb + i, 0)),
                  pl.BlockSpec((None, tm, W), lambda s, i: (s, i, 0))],
        out_specs=pl.BlockSpec((None, tm, W), lambda s, i: (s, i, 0)),
        out_shape=jax.ShapeDtypeStruct((S, half, W), F32),
        compiler_params=_cparams(("parallel", "parallel")),
    )(g, got)


def _scatter_chips(part):
    S, h, W = part.shape

    def body(p_ref, out_ref, send_sems, recv_sems, local_sem):
        x, y, c = _me()
        me = 2 * x + y
        chips = _other_chips(x, y)
        mine = pltpu.make_async_copy(p_ref.at[me], out_ref.at[me], local_sem)
        mine.start()
        sends = [pltpu.make_async_remote_copy(
            src_ref=p_ref.at[2 * chip[0] + chip[1]], dst_ref=out_ref.at[me],
            send_sem=send_sems.at[j], recv_sem=recv_sems.at[j], device_id=(*chip, c), device_id_type=MESH)
            for j, chip in enumerate(chips)]
        for cp in sends:
            cp.start()
        for j, chip in enumerate(chips):
            pltpu.make_async_remote_copy(
                src_ref=p_ref.at[me], dst_ref=out_ref.at[2 * chip[0] + chip[1]],
                send_sem=send_sems.at[j], recv_sem=recv_sems.at[j], device_id=(*chip, c), device_id_type=MESH).wait_recv()
        for cp in sends:
            cp.wait_send()
        mine.wait()

    return pl.pallas_call(
        body, name="scatter_chips", in_specs=[ANY], out_specs=ANY,
        out_shape=jax.ShapeDtypeStruct((S, h, W), part.dtype),
        scratch_shapes=[pltpu.SemaphoreType.DMA((3,)), pltpu.SemaphoreType.DMA((3,)), pltpu.SemaphoreType.DMA],
    )(part)


def _chip_sum(parts, tm):
    S, h, W = parts.shape

    def body(p_ref, o_ref):
        o_ref[...] = ((p_ref[0] + p_ref[1]) + p_ref[2]) + p_ref[3]

    return pl.pallas_call(
        body, name="chip_sum", grid=(h // tm,),
        in_specs=[pl.BlockSpec((S, tm, W), lambda i: (0, i, 0))],
        out_specs=pl.BlockSpec((tm, W), lambda i: (i, 0)),
        out_shape=jax.ShapeDtypeStruct((h, W), F32),
        compiler_params=_cparams(("parallel",)),
    )(parts)


def _join_halves(mine):
    h, W = mine.shape

    def body(m_ref, out_ref, send_sem, recv_sem, local_sem):
        x, y, c = _me()
        own = pltpu.make_async_copy(m_ref, out_ref.at[pl.ds(c * h, h), :], local_sem)
        own.start()
        cp = pltpu.make_async_remote_copy(
            src_ref=m_ref, dst_ref=out_ref.at[pl.ds(c * h, h), :],
            send_sem=send_sem, recv_sem=recv_sem, device_id=(x, y, 1 - c), device_id_type=MESH)
        cp.start()
        pltpu.make_async_remote_copy(
            src_ref=m_ref, dst_ref=out_ref.at[pl.ds((1 - c) * h, h), :],
            send_sem=send_sem, recv_sem=recv_sem, device_id=(x, y, 1 - c), device_id_type=MESH).wait_recv()
        cp.wait_send()
        own.wait()

    return pl.pallas_call(
        body, name="join_halves", in_specs=[ANY], out_specs=ANY,
        out_shape=jax.ShapeDtypeStruct((2 * h, W), mine.dtype),
        scratch_shapes=[pltpu.SemaphoreType.DMA, pltpu.SemaphoreType.DMA, pltpu.SemaphoreType.DMA],
    )(mine)


def _allreduce_small(s):
    R, W = s.shape

    def body(s_ref, o_ref, buf, send_sems, recv_sems):
        x, y, c = _me()
        me = 4 * x + 2 * y + c
        buf[me] = s_ref[...]
        peers = [((x + fx) % 2, (y + fy) % 2, (c + fc) % 2) for fx in range(2) for fy in range(2) for fc in range(2)][1:]
        sends = [pltpu.make_async_remote_copy(
            src_ref=s_ref, dst_ref=buf.at[me], send_sem=send_sems.at[k], recv_sem=recv_sems.at[k],
            device_id=peer, device_id_type=MESH) for k, peer in enumerate(peers)]
        for cp in sends:
            cp.start()
        for k, peer in enumerate(peers):
            pltpu.make_async_remote_copy(
                src_ref=s_ref, dst_ref=buf.at[4 * peer[0] + 2 * peer[1] + peer[2]], send_sem=send_sems.at[k],
                recv_sem=recv_sems.at[k], device_id=peer, device_id_type=MESH).wait_recv()
        for cp in sends:
            cp.wait_send()
        total = buf[0]
        for d in range(1, N_DEV):
            total = total + buf[d]
        o_ref[...] = total

    return pl.pallas_call(
        body, name="allreduce_small",
        in_specs=[pl.BlockSpec(memory_space=pltpu.VMEM)], out_specs=pl.BlockSpec(memory_space=pltpu.VMEM),
        out_shape=jax.ShapeDtypeStruct((R, W), F32),
        scratch_shapes=[pltpu.VMEM((N_DEV, R, W), F32), pltpu.SemaphoreType.DMA((N_DEV - 1,)), pltpu.SemaphoreType.DMA((N_DEV - 1,))],
    )(s)


def _adamw(name, g, w, m, v):
    R, C = w.shape
    tm = R
    for cand in (256, 128, 64, 32, 16, 8):
        if R % cand == 0:
            tm = cand
            break

    def body(g, w, m, v):
        m = ADAM_B1 * m + (1.0 - ADAM_B1) * g
        v = ADAM_B2 * v + (1.0 - ADAM_B2) * jnp.square(g)
        m_hat = m / (1.0 - ADAM_B1 ** ADAM_STEP)
        v_hat = v / (1.0 - ADAM_B2 ** ADAM_STEP)
        delta = -ADAM_LR * (m_hat / (jnp.sqrt(v_hat) + ADAM_EPS) + ADAM_WD * w)
        return delta, m, v

    return _rowwise(name, body, R, tm, [_row(t, tm) for t in (g, w, m, v)], [(C, F32)] * 3)


def _unpack_weights(gathered):
    S = gathered.shape[0]
    shard_shapes = {"w_in": (D_MODEL, (QKV_WIDTH + 2 * D_MODEL) // S), "w_branch_na": (NA_WIDTH, D_MODEL // S),
                    "w_branch_dil": (DIL_OUT_WIDTH, D_MODEL // S), "w_out": (D_MODEL // S, D_MODEL),
                    "w_up": (D_MODEL, D_FF // S), "w_down": (D_FF // S, D_MODEL),
                    "w_ple_gate": (D_MODEL // S, D_MODEL), "w_ple_proj": (PLE_DIM, D_MODEL // S)}
    col_sharded = {"w_in", "w_branch_na", "w_branch_dil", "w_up", "w_ple_proj"}
    out, r0 = {}, 0
    for name in BIG:
        rows, cols = shard_shapes[name]
        n = rows * cols // PACK_W
        t = gathered[:, r0:r0 + n, :].reshape(S, rows, cols)
        r0 += n
        out[name] = t.transpose(1, 0, 2).reshape(rows, S * cols) if name in col_sharded else t.reshape(S * rows, cols)
    return out


def _pack_grads(grads):
    col_sharded = {"w_in", "w_branch_na", "w_branch_dil", "w_up", "w_ple_proj"}
    per_chip = []
    for s in range(N_CHIPS):
        rows = []
        for name in BIG:
            g = grads[name]
            if name in col_sharded:
                w = g.shape[1] // N_CHIPS
                rows.append(_pack_rows(g[:, s * w:(s + 1) * w]))
            else:
                h = g.shape[0] // N_CHIPS
                rows.append(_pack_rows(g[s * h:(s + 1) * h]))
        per_chip.append(jnp.concatenate(rows, axis=0))
    return jnp.stack(per_chip)


def _unpack_shard(packed, shapes):
    out, r0 = {}, 0
    for name in BIG:
        rows, cols = shapes[name]
        n = rows * cols // PACK_W
        out[name] = packed[r0:r0 + n].reshape(rows, cols)
        r0 += n
    return out


def kernel(x, p, positions, g_mix, w_in, rpb, w_branch_na, w_branch_dil, w_out, g_mlp, w_up, w_down, g_ple, w_ple_gate, w_ple_proj, g_final, loss_target, m_g_mix, m_w_in, m_rpb, m_w_branch_na, m_w_branch_dil, m_w_out, m_g_mlp, m_w_up, m_w_down, m_g_ple, m_w_ple_gate, m_w_ple_proj, m_g_final, v_g_mix, v_w_in, v_rpb, v_w_branch_na, v_w_branch_dil, v_w_out, v_g_mlp, v_w_up, v_w_down, v_g_ple, v_w_ple_gate, v_w_ple_proj, v_g_final):
    shards = {"w_in": w_in[0], "w_branch_na": w_branch_na[0], "w_branch_dil": w_branch_dil[0], "w_out": w_out[0],
              "w_up": w_up[0], "w_down": w_down[0], "w_ple_gate": w_ple_gate[0], "w_ple_proj": w_ple_proj[0]}
    m_shards = {"w_in": m_w_in[0], "w_branch_na": m_w_branch_na[0], "w_branch_dil": m_w_branch_dil[0], "w_out": m_w_out[0],
                "w_up": m_w_up[0], "w_down": m_w_down[0], "w_ple_gate": m_w_ple_gate[0], "w_ple_proj": m_w_ple_proj[0]}
    v_shards = {"w_in": v_w_in[0], "w_branch_na": v_w_branch_na[0], "w_branch_dil": v_w_branch_dil[0], "w_out": v_w_out[0],
                "w_up": v_w_up[0], "w_down": v_w_down[0], "w_ple_gate": v_w_ple_gate[0], "w_ple_proj": v_w_ple_proj[0]}

    packed_w = jnp.concatenate([_pack_rows(shards[n].astype(BF)) for n in BIG], axis=0)
    W = _unpack_weights(_gather_weights(packed_w))
    w_qkv, w_gates = W["w_in"][:, :QKV_WIDTH], W["w_in"][:, QKV_WIDTH:]

    xs, ps, tgt = x[0], p[0, 0], loss_target[0]
    T = xs.shape[0]
    TM = 256
    gm, gl, gp, gf = g_mix, g_mlp, g_ple, g_final.reshape(1, D_MODEL)
    cos2, sin_signed = _rope_tables(positions[0])
    tab = _na_bias_table(rpb[0])

    a = _rowwise("norm_mix", lambda h, g: h * _rms(h) * g, T, TM, [_row(xs, TM), _full(gm)], [(D_MODEL, BF)])
    z_qkv = _mm("in_qkv", a, w_qkv, "nn", 512, 1280, 1024, [F32])
    z_gates = _mm("in_gates", a, w_gates, "nn", 512, 1024, 1024, [F32])

    def prep(z, cs, sn):
        n3 = 3 * NA_WIDTH
        return jnp.concatenate([
            z[:, :NA_WIDTH] * Q_SCALE, z[:, NA_WIDTH:n3],
            _rope_cols(z[:, n3:n3 + DIL_WIDTH], cs, sn) * Q_SCALE,
            _rope_cols(z[:, n3 + DIL_WIDTH:n3 + 2 * DIL_WIDTH], cs, sn),
            z[:, n3 + 2 * DIL_WIDTH:]], axis=1)

    qkv = _rowwise("qkv_prep", prep, T, TM, [_row(z_qkv, TM), _row(cos2, TM), _row(sin_signed, TM)], [(QKV_WIDTH, BF)])
    n3 = 3 * NA_WIDTH
    qa, ka, va = (_to_heads(qkv[:, i * NA_WIDTH:(i + 1) * NA_WIDTH]) for i in range(3))
    qd, kd, vd = (_to_segments(_to_heads(qkv[:, n3 + i * DIL_WIDTH:n3 + (i + 1) * DIL_WIDTH])) for i in range(3))

    y_na_h = _na_fwd(qa, ka, va, tab)
    band = [_band_fwd(qd, kd, vd, g) for g in range(len(DIL_GROUPS))]
    o_nat = _from_segments([b[0] for b in band])
    lse_nat = _from_segments([b[1] for b in band])
    y_dil_h, w_grp = _dil_merge_fwd(o_nat, lse_nat, 512)
    y_na, y_dil = _from_heads(y_na_h), _from_heads(y_dil_h)

    u_na = _mm("branch_na", y_na, W["w_branch_na"], "nn", 512, 1024, 512, [F32])
    u_dil = _mm("branch_dil", y_dil, W["w_branch_dil"], "nn", 512, 1024, 256, [F32])
    mixed = _rowwise(
        "gate_mix", lambda gn, gd, un, ud: _sigmoid(gn) * un + _sigmoid(gd) * ud, T, TM,
        [_row(z_gates, TM, 0, D_MODEL), _row(z_gates, TM, 1, D_MODEL), _row(u_na, TM), _row(u_dil, TM)], [(D_MODEL, BF)])
    mix_out = _mm("out_proj", mixed, W["w_out"], "nn", 512, 1024, 1024, [F32])

    def add_norm(h, d, g):
        h = h + d
        return h, h * _rms(h) * g

    h1, cn = _rowwise("add_norm_mlp", add_norm, T, TM, [_row(xs, TM), _row(mix_out, TM), _full(gl)], [(D_MODEL, F32), (D_MODEL, BF)])
    up, act = _mm("mlp_up", cn, W["w_up"], "nn", 512, 1024, 1024, [F32, BF],
                  epilogue=lambda acc: (acc, jnp.square(jnp.maximum(acc, 0.0))))
    mlp_out = _mm("mlp_down", act, W["w_down"], "nn", 512, 1024, 1024, [F32])
    h2, en = _rowwise("add_norm_ple", add_norm, T, TM, [_row(h1, TM), _row(mlp_out, TM), _full(gp)], [(D_MODEL, F32), (D_MODEL, BF)])
    gt = _mm("ple_gate", en, W["w_ple_gate"], "nn", 512, 1024, 1024, [F32])
    pp = _mm("ple_proj", ps, W["w_ple_proj"], "nn", 512, 1024, 256, [F32])

    def head(h2t, gtt, ppt, tg, g):
        sg = _sigmoid(gtt)
        h3 = h2t + sg * ppt
        yo = h3 * _rms(h3) * g
        diff = yo - tg
        loss = 0.5 * jnp.sum(jnp.mean(jnp.square(diff), axis=-1, keepdims=True), axis=0, keepdims=True)
        dh3, dg = _rms_bwd(diff * (1.0 / D_MODEL), h3, g)
        return dh3, dh3 * ppt * sg * (1.0 - sg), dh3 * sg, jnp.broadcast_to(loss, (1, 128)), dg

    dh3, d_gt, d_pp, loss_part, dg_final = _rowwise(
        "loss_head", head, T, TM, [_row(h2, TM), _row(gt, TM), _row(pp, TM), _row(tgt, TM), _full(gf)],
        [(D_MODEL, F32), (D_MODEL, BF), (D_MODEL, BF)], sums=[128, D_MODEL])

    grads = {}
    grads["w_ple_proj"] = _mm("g_ple_proj", ps, d_pp, "tn", 256, 1024, 512, [F32])
    grads["w_ple_gate"] = _mm("g_ple_gate", en, d_gt, "tn", 512, 1024, 512, [F32])
    d_en = _mm("d_ple_gate", d_gt, W["w_ple_gate"], "nt", 512, 1024, 1024, [F32])

    def add_norm_bwd(dh_out, dn, h, g):
        dh, dg = _rms_bwd(dn, h, g)
        dh = dh_out + dh
        return dh, dh, dg

    dh2, dh2_b, dg_ple = _rowwise("add_norm_ple_bwd", add_norm_bwd, T, TM, [_row(dh3, TM), _row(d_en, TM), _row(h2, TM), _full(gp)],
                                  [(D_MODEL, F32), (D_MODEL, BF)], sums=[D_MODEL])
    d_up = _mm("d_mlp_down", dh2_b, W["w_down"], "nt", 512, 1024, 1024, [BF],
               epilogue=lambda acc, u: (acc * (2.0 * jnp.maximum(u, 0.0)),), extras=(up,))
    grads["w_down"] = _mm("g_mlp_down", act, dh2_b, "tn", 512, 1024, 512, [F32])
    grads["w_up"] = _mm("g_mlp_up", cn, d_up, "tn", 512, 1024, 512, [F32])
    d_cn = _mm("d_mlp_up", d_up, W["w_up"], "nt", 512, 1024, 1024, [F32])
    dh1, dh1_b, dg_mlp = _rowwise("add_norm_mlp_bwd", add_norm_bwd, T, TM, [_row(dh2, TM), _row(d_cn, TM), _row(h1, TM), _full(gl)],
                                  [(D_MODEL, F32), (D_MODEL, BF)], sums=[D_MODEL])
    d_mixed = _mm("d_out_proj", dh1_b, W["w_out"], "nt", 512, 1024, 1024, [F32])
    grads["w_out"] = _mm("g_out_proj", mixed, dh1_b, "tn", 512, 1024, 512, [F32])

    def gate_bwd(dm, gn, gd, un, ud):
        sn, sd = _sigmoid(gn), _sigmoid(gd)
        return jnp.concatenate([dm * un * sn * (1.0 - sn), dm * ud * sd * (1.0 - sd)], axis=1), dm * sn, dm * sd

    dz_gates, d_u_na, d_u_dil = _rowwise(
        "gate_mix_bwd", gate_bwd, T, TM,
        [_row(d_mixed, TM), _row(z_gates, TM, 0, D_MODEL), _row(z_gates, TM, 1, D_MODEL), _row(u_na, TM), _row(u_dil, TM)],
        [(2 * D_MODEL, BF), (D_MODEL, BF), (D_MODEL, BF)])
    grads["w_branch_na"] = _mm("g_branch_na", y_na, d_u_na, "tn", 512, 1024, 512, [F32])
    grads["w_branch_dil"] = _mm("g_branch_dil", y_dil, d_u_dil, "tn", 256, 1024, 512, [F32])
    d_y_na = _mm("d_branch_na", d_u_na, W["w_branch_na"], "nt", 512, 512, 1024, [BF])
    d_y_dil = _mm("d_branch_dil", d_u_dil, W["w_branch_dil"], "nt", 512, 256, 1024, [F32])

    dqa, dka, dva, dtab = _na_bwd(qa, ka, va, tab, _to_heads(d_y_na))
    d_rpb = _na_rpb_grad(dtab)[:, :2 * NA_WIN_ROWS - 1, :2 * NA_WIN_COLS - 1]

    do_nat, dlse_nat = _dil_merge_bwd(_to_heads(d_y_dil), o_nat, w_grp, 512)
    do_seg, dlse_seg = _to_segments(do_nat), _to_segments(dlse_nat)
    band_b = [_band_bwd(qd, kd, vd, do_seg, dlse_seg, g) for g in range(len(DIL_GROUPS))]
    dqd, dkd, dvd = (_from_segments([b[i] for b in band_b]) for i in range(3))
    d_na = jnp.concatenate([_from_heads(t) for t in (dqa, dka, dva)], axis=1)
    d_dil = jnp.concatenate([_from_heads(t) for t in (dqd, dkd, dvd)], axis=1)

    def unprep(dn, dd, cs, sn):
        return jnp.concatenate([
            dn[:, :NA_WIDTH] * Q_SCALE, dn[:, NA_WIDTH:],
            _rope_cols(dd[:, :DIL_WIDTH], cs, -sn) * Q_SCALE,
            _rope_cols(dd[:, DIL_WIDTH:2 * DIL_WIDTH], cs, -sn),
            dd[:, 2 * DIL_WIDTH:]], axis=1)

    dz_qkv = _rowwise("qkv_unprep", unprep, T, TM, [_row(d_na, TM), _row(d_dil, TM), _row(cos2, TM), _row(sin_signed, TM)],
                      [(QKV_WIDTH, BF)])
    grads["w_in"] = jnp.concatenate([
        _mm("g_in_qkv", a, dz_qkv, "tn", 512, 1280, 512, [F32]),
        _mm("g_in_gates", a, dz_gates, "tn", 512, 1024, 512, [F32])], axis=1)
    d_a = _mm("d_in_qkv", dz_qkv, w_qkv, "nt", 512, 1024, 1280, [F32])
    d_a = _mm("d_in_gates", dz_gates, w_gates, "nt", 512, 1024, 1024, [F32], epilogue=lambda acc, e: (acc + e,), extras=(d_a,))

    def first_bwd(dh_out, dn, h, g):
        dh, dg = _rms_bwd(dn, h, g)
        return dh_out + dh, dg

    grad_x, dg_mix = _rowwise("norm_mix_bwd", first_bwd, T, TM, [_row(dh1, TM), _row(d_a, TM), _row(xs, TM), _full(gm)],
                              [(D_MODEL, F32)], sums=[D_MODEL])

    packed_g = _pack_grads(grads)
    got = _swap_halves(packed_g)
    pair = _pair_sum(packed_g, got, 536)
    mine = _chip_sum(_scatter_chips(pair), 536)
    g_shard = _unpack_shard(_join_halves(mine), {n: shards[n].shape for n in BIG})

    n_rpb = rpb.size
    rpb_rows = 4
    small = jnp.concatenate([
        dg_mix, dg_mlp, dg_ple, dg_final,
        jnp.pad(d_rpb.reshape(-1), (0, rpb_rows * D_MODEL - n_rpb)).reshape(rpb_rows, D_MODEL),
        jnp.pad(loss_part, ((0, 0), (0, D_MODEL - loss_part.shape[1]))),
        jnp.zeros((SMALL_ROWS - 5 - rpb_rows, D_MODEL), F32)], axis=0)
    small = _allreduce_small(small)
    loss = small[4 + rpb_rows, 0]

    def small_pack(a0, a1, a2, a3, r):
        return jnp.concatenate([a0.reshape(1, -1), a1.reshape(1, -1), a2.reshape(1, -1), a3.reshape(1, -1),
                                jnp.pad(r.reshape(-1), (0, rpb_rows * D_MODEL - n_rpb)).reshape(rpb_rows, D_MODEL)], axis=0)

    g_small = small[:4 + rpb_rows]
    small_res = _adamw("adamw_small", g_small, small_pack(g_mix, g_mlp, g_ple, g_final, rpb),
                       small_pack(m_g_mix, m_g_mlp, m_g_ple, m_g_final, m_rpb), small_pack(v_g_mix, v_g_mlp, v_g_ple, v_g_final, v_rpb))

    def small_unpack(t):
        return {"g_mix": t[0].reshape(g_mix.shape), "g_mlp": t[1].reshape(g_mlp.shape), "g_ple": t[2].reshape(g_ple.shape),
                "g_final": t[3].reshape(g_final.shape), "rpb": t[4:].reshape(-1)[:n_rpb].reshape(rpb.shape)}

    out = {"grad": small_unpack(g_small)}
    for kind, t in zip(("delta", "new_m", "new_v"), small_res, strict=True):
        out[kind] = small_unpack(t)
    for n in BIG:
        out["grad"][n] = g_shard[n][None]
        res = _adamw("adamw_" + n, g_shard[n], shards[n], m_shards[n], v_shards[n])
        for kind, t in zip(("delta", "new_m", "new_v"), res, strict=True):
            out[kind][n] = t[None]

    order = ["g_mix", "w_in", "rpb", "w_branch_na", "w_branch_dil", "w_out", "g_mlp", "w_up", "w_down", "g_ple",
             "w_ple_gate", "w_ple_proj", "g_final"]
    return (loss, grad_x[None], *[out["grad"][n] for n in order], *[out["delta"][n] for n in order],
            *[out["new_m"][n] for n in order], *[out["new_v"][n] for n in order])
```

```python
import functools

import numpy as np
import jax
import jax.numpy as jnp
from jax import lax
from jax.experimental import pallas as pl
from jax.experimental.pallas import tpu as pltpu

BF = jnp.bfloat16
F32 = jnp.float32
MESH = pl.DeviceIdType.MESH
ANY = pl.BlockSpec(memory_space=pl.ANY)

V7X_VMEM_BYTES = 64 * 1024 * 1024
VMEM_LIMIT = V7X_VMEM_BYTES - 16 * 1024 * 1024

D_MODEL = 1024
HEAD_DIM = 64
GRID_W = 64
NA_HEADS = 8
NA_WIN_ROWS = 8
NA_WIN_COLS = 16
NA_WIDTH = NA_HEADS * HEAD_DIM
DIL_GROUPS = ((128, 1), (512, 4), (2048, 16))
DIL_HPG = 4
DIL_HEADS = DIL_HPG * len(DIL_GROUPS)
DIL_WIDTH = DIL_HEADS * HEAD_DIM
DIL_OUT_WIDTH = DIL_HPG * HEAD_DIM
DIL_RADIUS = 64
QKV_WIDTH = 3 * NA_WIDTH + 3 * DIL_WIDTH
D_FF = 4 * D_MODEL
PLE_DIM = 256
ROPE_THETA = 10000.0
RMS_EPS = 1e-6
NEG_INF = -1e30
Q_SCALE = HEAD_DIM ** -0.5

ADAM_LR = 0.001
ADAM_B1 = 0.9
ADAM_B2 = 0.999
ADAM_EPS = 1e-08
ADAM_WD = 0.01
ADAM_STEP = 10

N_CHIPS = 4
N_DEV = 8
PACK_W = 1024
BIG = ("w_in", "w_branch_na", "w_branch_dil", "w_out", "w_up", "w_down", "w_ple_gate", "w_ple_proj")
SMALL_ROWS = 16


def _cparams(sem=None):
    return pltpu.CompilerParams(dimension_semantics=sem, vmem_limit_bytes=VMEM_LIMIT)


def _mm(name, a, b, mode, tm, tn, tk, out_dtypes, epilogue=None, extras=()):
    if mode == "nn":
        (M, K), N = a.shape, b.shape[1]
        a_spec = pl.BlockSpec((tm, tk), lambda i, j, k: (i, k))
        b_spec = pl.BlockSpec((tk, tn), lambda i, j, k: (k, j))
        dims = (((1,), (0,)), ((), ()))
    elif mode == "nt":
        (M, K), N = a.shape, b.shape[0]
        a_spec = pl.BlockSpec((tm, tk), lambda i, j, k: (i, k))
        b_spec = pl.BlockSpec((tn, tk), lambda i, j, k: (j, k))
        dims = (((1,), (1,)), ((), ()))
    else:
        (K, M), N = a.shape, b.shape[1]
        a_spec = pl.BlockSpec((tk, tm), lambda i, j, k: (k, i))
        b_spec = pl.BlockSpec((tk, tn), lambda i, j, k: (k, j))
        dims = (((0,), (0,)), ((), ()))
    tm, tn, tk = min(tm, M), min(tn, N), min(tk, K)
    assert M % tm == 0 and N % tn == 0 and K % tk == 0, (name, M, N, K, tm, tn, tk)
    nk = K // tk
    n_extra, n_out = len(extras), len(out_dtypes)
    tile = pl.BlockSpec((tm, tn), lambda i, j, k: (i, j))

    def body(a_ref, b_ref, *rest):
        extra_refs, out_refs, acc = rest[:n_extra], rest[n_extra:n_extra + n_out], rest[-1]
        k = pl.program_id(2)

        @pl.when(k == 0)
        def _():
            acc[...] = jnp.zeros_like(acc)

        acc[...] += lax.dot_general(a_ref[...].astype(BF), b_ref[...].astype(BF), dims, preferred_element_type=F32)

        @pl.when(k == nk - 1)
        def _():
            outs = (acc[...],) if epilogue is None else epilogue(acc[...], *[e[...] for e in extra_refs])
            for o_ref, val in zip(out_refs, outs, strict=True):
                o_ref[...] = val.astype(o_ref.dtype)

    outs = pl.pallas_call(
        body, name=name, grid=(M // tm, N // tn, nk),
        in_specs=[a_spec, b_spec] + [tile] * n_extra,
        out_specs=[tile] * n_out,
        out_shape=[jax.ShapeDtypeStruct((M, N), dt) for dt in out_dtypes],
        scratch_shapes=[pltpu.VMEM((tm, tn), F32)],
        compiler_params=_cparams(("parallel", "parallel", "arbitrary")),
    )(a, b, *extras)
    return outs[0] if n_out == 1 else outs


def _row(arr, tm, col_block=None, width=None):
    width = arr.shape[1] if width is None else width
    cb = 0 if col_block is None else col_block
    return arr, pl.BlockSpec((tm, width), lambda i: (i, cb))


def _full(arr):
    nd = arr.ndim
    return arr, pl.BlockSpec(arr.shape, lambda i: (0,) * nd)


def _rowwise(name, body, T, tm, ins, outs, sums=()):
    n_in, n_out, n_sum = len(ins), len(outs), len(sums)

    def kern(*refs):
        in_refs, out_refs, sum_refs = refs[:n_in], refs[n_in:n_in + n_out], refs[n_in + n_out:]
        res = body(*[r[...] for r in in_refs])
        res = res if isinstance(res, tuple) else (res,)
        for o_ref, val in zip(out_refs, res[:n_out], strict=True):
            o_ref[...] = val.astype(o_ref.dtype)
        if n_sum:
            @pl.when(pl.program_id(0) == 0)
            def _():
                for s_ref in sum_refs:
                    s_ref[...] = jnp.zeros_like(s_ref)

            for s_ref, val in zip(sum_refs, res[n_out:], strict=True):
                s_ref[...] += val

    res = pl.pallas_call(
        kern, name=name, grid=(T // tm,),
        in_specs=[spec for _, spec in ins],
        out_specs=[pl.BlockSpec((tm, c), lambda i: (i, 0)) for c, _ in outs]
        + [pl.BlockSpec((1, c), lambda i: (0, 0)) for c in sums],
        out_shape=[jax.ShapeDtypeStruct((T, c), dt) for c, dt in outs]
        + [jax.ShapeDtypeStruct((1, c), F32) for c in sums],
        compiler_params=_cparams(("arbitrary",)),
    )(*[a for a, _ in ins])
    return res[0] if len(res) == 1 else res


def _sigmoid(x):
    return 1.0 / (1.0 + jnp.exp(-x))


def _rms(h):
    return lax.rsqrt(jnp.mean(h * h, axis=-1, keepdims=True) + RMS_EPS)


def _rms_bwd(dy, h, g):
    r = _rms(h)
    n = h * r
    dn = dy * g
    dh = r * (dn - n * jnp.mean(dn * n, axis=-1, keepdims=True))
    return dh, jnp.sum(dy * n, axis=0, keepdims=True)


def _rope(x, cos2, sin_signed):
    lane = lax.broadcasted_iota(jnp.int32, x.shape, 1)
    swapped = jnp.where((lane % HEAD_DIM) < HEAD_DIM // 2, pltpu.roll(x, 128 - HEAD_DIM // 2, 1), pltpu.roll(x, HEAD_DIM // 2, 1))
    return x * cos2 + swapped * sin_signed


def _rope_cols(x, cos2, sin_signed):
    return jnp.concatenate([_rope(x[:, c:c + 128], cos2, sin_signed) for c in range(0, x.shape[1], 128)], axis=1)


NA_KEYS = NA_WIN_ROWS * GRID_W
NA_BASES = 8


def _na_row_geometry(r, rows):
    first = jnp.clip(r - NA_WIN_ROWS // 2, 0, rows - NA_WIN_ROWS)
    base = first - r + (NA_WIN_ROWS - 1)
    return pl.multiple_of(first * GRID_W, GRID_W), base


def _na_probs(q, kw, bias):
    s = lax.dot_general(q, kw, (((1,), (1,)), ((), ())), preferred_element_type=F32) + bias
    p = jnp.exp(s - jnp.max(s, axis=-1, keepdims=True))
    return p / jnp.sum(p, axis=-1, keepdims=True)


def _split_pair(t):
    first = lax.broadcasted_iota(jnp.int32, t.shape, 1) < HEAD_DIM
    zero = jnp.zeros_like(t)
    return jnp.where(first, t, zero), jnp.where(first, zero, t)


def _join_pair(a, b):
    return jnp.where(lax.broadcasted_iota(jnp.int32, a.shape, 1) < HEAD_DIM, a, b)


_NT = (((1,), (1,)), ((), ()))
_TN = (((0,), (0,)), ((), ()))


def _na_fwd(qkv, tab):
    T = qkv.shape[0]
    rows = T // GRID_W
    n_pairs = NA_WIDTH // 128

    def body(q_ref, k_ref, v_ref, tab_ref, y_ref):
        def step(r, carry):
            k0, base = _na_row_geometry(r, rows)
            q0 = pl.multiple_of(r * GRID_W, GRID_W)
            kw, vw = k_ref[pl.ds(k0, NA_KEYS), :], v_ref[pl.ds(k0, NA_KEYS), :]
            ys = []
            for h, qh in enumerate(_split_pair(q_ref[pl.ds(q0, GRID_W), :])):
                p = _na_probs(qh, kw, tab_ref[h, base])
                ys.append(jnp.dot(p.astype(BF), vw, preferred_element_type=F32))
            y_ref[pl.ds(q0, GRID_W), :] = _join_pair(*ys).astype(y_ref.dtype)
            return carry

        lax.fori_loop(0, rows, step, 0)

    def cols(first):
        return pl.BlockSpec((T, 128), lambda j: (0, first + j))

    return pl.pallas_call(
        body, name="na_fwd", grid=(n_pairs,),
        in_specs=[cols(0), cols(n_pairs), cols(2 * n_pairs), pl.BlockSpec((2, NA_BASES, GRID_W, NA_KEYS), lambda j: (j, 0, 0, 0))],
        out_specs=cols(0), out_shape=jax.ShapeDtypeStruct((T, NA_WIDTH), BF),
        compiler_params=_cparams(("parallel",)),
    )(qkv, qkv, qkv, tab)


def _na_bwd(qkv, tab, do):
    T = qkv.shape[0]
    rows = T // GRID_W
    n_pairs = NA_WIDTH // 128

    def body(q_ref, k_ref, v_ref, tab_ref, do_ref, dq_ref, dk_ref, dv_ref, dtab_ref):
        dk_ref[...] = jnp.zeros_like(dk_ref)
        dv_ref[...] = jnp.zeros_like(dv_ref)
        dtab_ref[...] = jnp.zeros_like(dtab_ref)

        def step(r, carry):
            k0, base = _na_row_geometry(r, rows)
            q0 = pl.multiple_of(r * GRID_W, GRID_W)
            kw, vw = k_ref[pl.ds(k0, NA_KEYS), :], v_ref[pl.ds(k0, NA_KEYS), :]
            qs = _split_pair(q_ref[pl.ds(q0, GRID_W), :])
            dos = _split_pair(do_ref[pl.ds(q0, GRID_W), :])
            dqs, dk, dv = [], None, None
            for h in range(2):
                p = _na_probs(qs[h], kw, tab_ref[h, base])
                dvh = lax.dot_general(p.astype(BF), dos[h], _TN, preferred_element_type=F32)
                dp = lax.dot_general(dos[h], vw, _NT, preferred_element_type=F32)
                ds = p * (dp - jnp.sum(dp * p, axis=-1, keepdims=True))
                dtab_ref[h, base] += ds
                dsb = ds.astype(BF)
                dqs.append(jnp.dot(dsb, kw, preferred_element_type=F32))
                dkh = lax.dot_general(dsb, qs[h], _TN, preferred_element_type=F32)
                dk, dv = (dkh, dvh) if h == 0 else (dk + dkh, dv + dvh)
            dq_ref[pl.ds(q0, GRID_W), :] = _join_pair(*dqs)
            dk_ref[pl.ds(k0, NA_KEYS), :] += dk
            dv_ref[pl.ds(k0, NA_KEYS), :] += dv
            return carry

        lax.fori_loop(0, rows, step, 0)

    def cols(first):
        return pl.BlockSpec((T, 128), lambda j: (0, first + j))

    tabs = pl.BlockSpec((2, NA_BASES, GRID_W, NA_KEYS), lambda j: (j, 0, 0, 0))
    wide = jax.ShapeDtypeStruct((T, NA_WIDTH), F32)
    return pl.pallas_call(
        body, name="na_bwd", grid=(n_pairs,),
        in_specs=[cols(0), cols(n_pairs), cols(2 * n_pairs), tabs, cols(0)],
        out_specs=[cols(0), cols(0), cols(0), tabs],
        out_shape=[wide, wide, wide, jax.ShapeDtypeStruct((NA_HEADS, NA_BASES, GRID_W, NA_KEYS), F32)],
        compiler_params=_cparams(("parallel",)),
    )(qkv, qkv, qkv, tab, do)


def _na_geometry_np():
    c = np.arange(GRID_W)
    cs = np.clip(c - NA_WIN_COLS // 2, 0, GRID_W - NA_WIN_COLS)
    valid = (c[None, :] >= cs[:, None]) & (c[None, :] < cs[:, None] + NA_WIN_COLS)
    off = c[None, :] - c[:, None] + (NA_WIN_COLS - 1)
    oh_col = np.zeros((GRID_W, GRID_W, 2 * NA_WIN_COLS - 1), np.float32)
    qq, kk = np.nonzero(valid)
    oh_col[qq, kk, off[qq, kk]] = 1.0
    oh_row = np.zeros((NA_BASES, NA_WIN_ROWS, 2 * NA_WIN_ROWS - 1), np.float32)
    for base in range(NA_BASES):
        for i in range(NA_WIN_ROWS):
            oh_row[base, i, base + i] = 1.0
    return valid, oh_col, oh_row


def _na_bias_table(rpb):
    valid, oh_col, oh_row = _na_geometry_np()
    hi = lax.Precision.HIGHEST
    t1 = jnp.einsum("hrc,pir->hpic", rpb, oh_row, precision=hi)
    tab = jnp.einsum("hpic,qkc->hpqik", t1, oh_col, precision=hi)
    tab = jnp.where(valid[None, None, :, None, :], tab, NEG_INF)
    return tab.reshape(rpb.shape[0], NA_BASES, GRID_W, NA_KEYS)


def _na_rpb_grad(dtab):
    H = dtab.shape[0]
    n_rows = 2 * NA_WIN_ROWS - 1
    n_cols = 2 * NA_WIN_COLS - 1

    def body(d_ref, o_ref):
        lane = lax.broadcasted_iota(jnp.int32, (GRID_W, 128), 1)
        low = lane < GRID_W
        out_rows = []
        for ro in range(n_rows):
            acc = jnp.zeros((GRID_W, 128), F32)
            for base in range(NA_BASES):
                i = ro - base
                if not 0 <= i < NA_WIN_ROWS:
                    continue
                pair = d_ref[base, :, pl.ds((i // 2) * 128, 128)]
                if i % 2:
                    pair = pltpu.roll(pair, GRID_W, 1)
                acc = acc + jnp.where(low, pair, 0.0)
            skew = pltpu.roll(acc, 0, 1, stride=1, stride_axis=0)
            diag = jnp.sum(skew, axis=0, keepdims=True)
            out_rows.append(pltpu.roll(jnp.broadcast_to(diag, (8, 128)), 128 - (GRID_W - NA_WIN_COLS), 1)[:1])
        out_rows.append(jnp.zeros((1, 128), F32))
        res = jnp.concatenate(out_rows, axis=0)
        o_ref[...] = jnp.where(lax.broadcasted_iota(jnp.int32, res.shape, 1) < n_cols, res, 0.0)

    return pl.pallas_call(
        body, name="na_rpb_grad", grid=(H,),
        in_specs=[pl.BlockSpec((None, NA_BASES, GRID_W, NA_KEYS), lambda h: (h, 0, 0, 0))],
        out_specs=pl.BlockSpec((None, n_rows + 1, 128), lambda h: (h, 0, 0)),
        out_shape=jax.ShapeDtypeStruct((H, n_rows + 1, 128), F32),
        compiler_params=_cparams(("parallel",)),
    )(jnp.flip(dtab, axis=2))


BAND_Q = 128
BAND_KEYS = BAND_Q + 2 * DIL_RADIUS


def _band_geometry(n, T, seg_shift):
    q0 = pl.multiple_of(n * BAND_Q, BAND_Q)
    k0 = pl.multiple_of(jnp.clip(q0 - DIL_RADIUS, 0, T - BAND_KEYS), DIL_RADIUS)
    qi = q0 + lax.broadcasted_iota(jnp.int32, (BAND_Q, BAND_KEYS), 0)
    kj = k0 + lax.broadcasted_iota(jnp.int32, (BAND_Q, BAND_KEYS), 1)
    valid = ((qi >> seg_shift) == (kj >> seg_shift)) & (jnp.abs(qi - kj) <= DIL_RADIUS)
    return q0, k0, valid


def _band_probs(q, kw, valid):
    s = lax.dot_general(q, kw, (((1,), (1,)), ((), ())), preferred_element_type=F32)
    s = jnp.where(valid, s, NEG_INF)
    m = jnp.max(s, axis=-1, keepdims=True)
    p = jnp.exp(s - m)
    l = jnp.sum(p, axis=-1, keepdims=True)
    return p / l, m + jnp.log(l)


def _band_fwd(q, k, v, group):
    T, W = q.shape
    seg_shift = (T // DIL_GROUPS[group][1]).bit_length() - 1

    def body(q_ref, k_ref, v_ref, o_ref, lse_ref):
        def step(n, carry):
            q0, k0, valid = _band_geometry(n, T, seg_shift)
            kw, vw = k_ref[pl.ds(k0, BAND_KEYS), :], v_ref[pl.ds(k0, BAND_KEYS), :]
            os, lses = [], []
            for qh in _split_pair(q_ref[pl.ds(q0, BAND_Q), :]):
                p, lse = _band_probs(qh, kw, valid)
                os.append(jnp.dot(p.astype(BF), vw, preferred_element_type=F32))
                lses.append(jnp.broadcast_to(lse, (BAND_Q, 128)))
            o_ref[pl.ds(q0, BAND_Q), :] = _join_pair(*os)
            lse_ref[pl.ds(q0, BAND_Q), :] = _join_pair(*lses)
            return carry

        lax.fori_loop(0, T // BAND_Q, step, 0)

    pair = pl.BlockSpec((T, 128), lambda j: (0, j))
    wide = jax.ShapeDtypeStruct((T, W), F32)
    return pl.pallas_call(
        body, name=f"band_fwd_g{group}", grid=(W // 128,),
        in_specs=[pair, pair, pair], out_specs=[pair, pair], out_shape=[wide, wide],
        compiler_params=_cparams(("parallel",)),
    )(q, k, v)


def _band_bwd(q, k, v, do, dlse, group):
    T, W = q.shape
    seg_shift = (T // DIL_GROUPS[group][1]).bit_length() - 1

    def body(q_ref, k_ref, v_ref, do_ref, dlse_ref, dq_ref, dk_ref, dv_ref):
        dk_ref[...] = jnp.zeros_like(dk_ref)
        dv_ref[...] = jnp.zeros_like(dv_ref)

        def step(n, carry):
            q0, k0, valid = _band_geometry(n, T, seg_shift)
            kw, vw = k_ref[pl.ds(k0, BAND_KEYS), :], v_ref[pl.ds(k0, BAND_KEYS), :]
            qs = _split_pair(q_ref[pl.ds(q0, BAND_Q), :])
            dos = _split_pair(do_ref[pl.ds(q0, BAND_Q), :])
            dl = dlse_ref[pl.ds(q0, BAND_Q), :]
            dqs, dk, dv = [], None, None
            for h in range(2):
                p, _ = _band_probs(qs[h], kw, valid)
                dvh = lax.dot_general(p.astype(BF), dos[h], _TN, preferred_element_type=F32)
                dp = lax.dot_general(dos[h], vw, _NT, preferred_element_type=F32)
                ds = p * (dp - jnp.sum(dp * p, axis=-1, keepdims=True) + dl[:, h * HEAD_DIM:h * HEAD_DIM + 1])
                dsb = ds.astype(BF)
                dqs.append(jnp.dot(dsb, kw, preferred_element_type=F32))
                dkh = lax.dot_general(dsb, qs[h], _TN, preferred_element_type=F32)
                dk, dv = (dkh, dvh) if h == 0 else (dk + dkh, dv + dvh)
            dq_ref[pl.ds(q0, BAND_Q), :] = _join_pair(*dqs)
            dk_ref[pl.ds(k0, BAND_KEYS), :] += dk
            dv_ref[pl.ds(k0, BAND_KEYS), :] += dv
            return carry

        lax.fori_loop(0, T // BAND_Q, step, 0)

    pair = pl.BlockSpec((T, 128), lambda j: (0, j))
    wide = jax.ShapeDtypeStruct((T, W), F32)
    return pl.pallas_call(
        body, name=f"band_bwd_g{group}", grid=(W // 128,),
        in_specs=[pair] * 5, out_specs=[pair] * 3, out_shape=[wide] * 3,
        compiler_params=_cparams(("parallel",)),
    )(q, k, v, do, dlse)


def _head_sums(t):
    head = lax.broadcasted_iota(jnp.int32, t.shape, 1) // HEAD_DIM
    out = jnp.zeros_like(t)
    for h in range(t.shape[1] // HEAD_DIM):
        mine = head == h
        out = jnp.where(mine, jnp.sum(jnp.where(mine, t, 0.0), axis=-1, keepdims=True), out)
    return out


def _dil_merge_fwd(os, lses, tm):
    G = len(DIL_GROUPS)
    T, W = os[0].shape

    def body(*tiles):
        o, ls = tiles[:G], tiles[G:]
        m = functools.reduce(jnp.maximum, ls)
        es = [jnp.exp(l - m) for l in ls]
        tot = functools.reduce(jnp.add, es)
        ws = [e / tot for e in es]
        return (functools.reduce(jnp.add, [w * t for w, t in zip(ws, o)]), *ws)

    res = _rowwise("dil_merge_fwd", body, T, tm, [_row(t, tm) for t in (*os, *lses)], [(W, BF)] + [(W, F32)] * G)
    return res[0], res[1:]


def _dil_merge_bwd(dy, os, ws, tm):
    G = len(DIL_GROUPS)
    T, W = dy.shape

    def body(dyt, *tiles):
        o, w = tiles[:G], tiles[G:]
        dws = [_head_sums(dyt * t) for t in o]
        mean = functools.reduce(jnp.add, [a * b for a, b in zip(w, dws)])
        return (*[a * dyt for a in w], *[a * (b - mean) for a, b in zip(w, dws)])

    res = _rowwise("dil_merge_bwd", body, T, tm, [_row(t, tm) for t in (dy, *os, *ws)], [(W, BF)] * G + [(W, F32)] * G)
    return res[:G], res[G:]


def _to_segments(t, dil):
    T, w = t.shape
    return t if dil == 1 else t.reshape(T // dil, dil, w).transpose(1, 0, 2).reshape(T, w)


def _from_segments(t, dil):
    T, w = t.shape
    return t if dil == 1 else t.reshape(dil, T // dil, w).transpose(1, 0, 2).reshape(T, w)


def _rope_tables(positions):
    half = HEAD_DIM // 2
    inv_freq = ROPE_THETA ** (-jnp.arange(half, dtype=F32) / half)
    ang = positions.astype(F32)[:, None] * inv_freq
    cos, sin = jnp.cos(ang), jnp.sin(ang)
    return jnp.tile(jnp.concatenate([cos, cos], axis=1), (1, 2)), jnp.tile(jnp.concatenate([-sin, sin], axis=1), (1, 2))


def _pack_rows(t):
    return t.reshape(-1, PACK_W)


def _me():
    return lax.axis_index("x"), lax.axis_index("y"), lax.axis_index("c")


def _other_chips(x, y):
    return [(1 - x, y), (x, 1 - y), (1 - x, 1 - y)]


def _gather_weights(packed):
    R, W = packed.shape
    half = R // 2

    def body(in_ref, out_ref, send_sems, recv_sems):
        x, y, c = _me()
        sibling = (x, y, 1 - c)
        chips = _other_chips(x, y)

        def block(chip, core):
            return out_ref.at[2 * chip[0] + chip[1], pl.ds(core * half, half), :]

        def copy(k, chip, core, to, src=None):
            return pltpu.make_async_remote_copy(
                src_ref=block(chip, core) if src is None else src, dst_ref=block(chip, core),
                send_sem=send_sems.at[k], recv_sem=recv_sems.at[k], device_id=to, device_id_type=MESH)

        first = [copy(j, (x, y), c, (*chip, c), src=in_ref.at[pl.ds(c * half, half), :]) for j, chip in enumerate(chips)]
        for cp in first:
            cp.start()
        passed = [copy(3 + j, chip, c, sibling) for j, chip in enumerate(chips)]
        for j, chip in enumerate(chips):
            copy(j, chip, c, (x, y, c)).wait_recv()
            passed[j].start()
        for j, chip in enumerate(chips):
            copy(3 + j, chip, 1 - c, (x, y, c)).wait_recv()
        for cp in first + passed:
            cp.wait_send()

    others = pl.pallas_call(
        body, name="gather_weights",
        in_specs=[ANY], out_specs=ANY,
        out_shape=jax.ShapeDtypeStruct((N_CHIPS, R, W), packed.dtype),
        scratch_shapes=[pltpu.SemaphoreType.DMA((6,)), pltpu.SemaphoreType.DMA((6,))],
    )(packed)
    return lax.dynamic_update_slice(others, packed[None], (2 * lax.axis_index("x") + lax.axis_index("y"), 0, 0))


def _swap_halves(g):
    S, R, W = g.shape
    half = R // 2

    def body(g_ref, out_ref, send_sem, recv_sem):
        x, y, c = _me()
        cp = pltpu.make_async_remote_copy(
            src_ref=g_ref.at[:, pl.ds((1 - c) * half, half), :], dst_ref=out_ref,
            send_sem=send_sem, recv_sem=recv_sem, device_id=(x, y, 1 - c), device_id_type=MESH)
        cp.start()
        cp.wait()

    return pl.pallas_call(
        body, name="swap_halves", in_specs=[ANY], out_specs=ANY,
        out_shape=jax.ShapeDtypeStruct((S, half, W), g.dtype),
        scratch_shapes=[pltpu.SemaphoreType.DMA, pltpu.SemaphoreType.DMA],
    )(g)


def _pair_sum(g, got, tm):
    S, R, W = g.shape
    half = R // 2
    nb = half // tm

    def body(c_ref, g_ref, got_ref, o_ref, ob_ref):
        tot = g_ref[...] + got_ref[...]
        o_ref[...] = tot
        ob_ref[...] = tot.astype(ob_ref.dtype)

    tile = pl.BlockSpec((None, tm, W), lambda s, i, c_ref: (s, i, 0))
    return pl.pallas_call(
        body, name="pair_sum",
        grid_spec=pltpu.PrefetchScalarGridSpec(
            num_scalar_prefetch=1, grid=(S, nb),
            in_specs=[pl.BlockSpec((None, tm, W), lambda s, i, c_ref: (s, c_ref[0] * nb + i, 0)), tile],
            out_specs=[tile, tile]),
        out_shape=[jax.ShapeDtypeStruct((S, half, W), F32), jax.ShapeDtypeStruct((S, half, W), BF)],
        compiler_params=_cparams(("parallel", "parallel")),
    )(lax.axis_index("c").reshape(1).astype(jnp.int32), g, got)


def _scatter_chips(part):
    S, h, W = part.shape

    def body(p_ref, out_ref, send_sems, recv_sems):
        x, y, c = _me()
        chips = _other_chips(x, y)
        sends = [pltpu.make_async_remote_copy(
            src_ref=p_ref.at[2 * chip[0] + chip[1]], dst_ref=out_ref.at[j],
            send_sem=send_sems.at[j], recv_sem=recv_sems.at[j], device_id=(*chip, c), device_id_type=MESH)
            for j, chip in enumerate(chips)]
        for cp in sends:
            cp.start()
        for cp in sends:
            cp.wait()

    return pl.pallas_call(
        body, name="scatter_chips", in_specs=[ANY], out_specs=ANY,
        out_shape=jax.ShapeDtypeStruct((S - 1, h, W), part.dtype),
        scratch_shapes=[pltpu.SemaphoreType.DMA((3,)), pltpu.SemaphoreType.DMA((3,))],
    )(part)


def _chip_sum(own, others, tm):
    n, h, W = others.shape

    def body(own_ref, p_ref, o_ref):
        o_ref[...] = ((own_ref[...] + p_ref[0].astype(F32)) + p_ref[1].astype(F32)) + p_ref[2].astype(F32)

    return pl.pallas_call(
        body, name="chip_sum", grid=(h // tm,),
        in_specs=[pl.BlockSpec((tm, W), lambda i: (i, 0)), pl.BlockSpec((n, tm, W), lambda i: (0, i, 0))],
        out_specs=pl.BlockSpec((tm, W), lambda i: (i, 0)),
        out_shape=jax.ShapeDtypeStruct((h, W), F32),
        compiler_params=_cparams(("parallel",)),
    )(own, others)


def _join_halves(mine):
    h, W = mine.shape

    def body(m_ref, out_ref, send_sem, recv_sem):
        x, y, c = _me()
        cp = pltpu.make_async_remote_copy(
            src_ref=m_ref, dst_ref=out_ref.at[pl.ds(c * h, h), :],
            send_sem=send_sem, recv_sem=recv_sem, device_id=(x, y, 1 - c), device_id_type=MESH)
        cp.start()
        pltpu.make_async_remote_copy(
            src_ref=m_ref, dst_ref=out_ref.at[pl.ds((1 - c) * h, h), :],
            send_sem=send_sem, recv_sem=recv_sem, device_id=(x, y, 1 - c), device_id_type=MESH).wait_recv()
        cp.wait_send()

    other = pl.pallas_call(
        body, name="join_halves", in_specs=[ANY], out_specs=ANY,
        out_shape=jax.ShapeDtypeStruct((2 * h, W), mine.dtype),
        scratch_shapes=[pltpu.SemaphoreType.DMA, pltpu.SemaphoreType.DMA],
    )(mine)
    return lax.dynamic_update_slice(other, mine, (lax.axis_index("c") * h, 0))


def _allreduce_small(s):
    R, W = s.shape

    def body(s_ref, o_ref, buf, send_sems, recv_sems):
        x, y, c = _me()
        me = 4 * x + 2 * y + c
        buf[me] = s_ref[...]
        peers = [((x + fx) % 2, (y + fy) % 2, (c + fc) % 2) for fx in range(2) for fy in range(2) for fc in range(2)][1:]
        sends = [pltpu.make_async_remote_copy(
            src_ref=s_ref, dst_ref=buf.at[me], send_sem=send_sems.at[k], recv_sem=recv_sems.at[k],
            device_id=peer, device_id_type=MESH) for k, peer in enumerate(peers)]
        for cp in sends:
            cp.start()
        for k, peer in enumerate(peers):
            pltpu.make_async_remote_copy(
                src_ref=s_ref, dst_ref=buf.at[4 * peer[0] + 2 * peer[1] + peer[2]], send_sem=send_sems.at[k],
                recv_sem=recv_sems.at[k], device_id=peer, device_id_type=MESH).wait_recv()
        for cp in sends:
            cp.wait_send()
        total = buf[0]
        for d in range(1, N_DEV):
            total = total + buf[d]
        o_ref[...] = total

    return pl.pallas_call(
        body, name="allreduce_small",
        in_specs=[pl.BlockSpec(memory_space=pltpu.VMEM)], out_specs=pl.BlockSpec(memory_space=pltpu.VMEM),
        out_shape=jax.ShapeDtypeStruct((R, W), F32),
        scratch_shapes=[pltpu.VMEM((N_DEV, R, W), F32), pltpu.SemaphoreType.DMA((N_DEV - 1,)), pltpu.SemaphoreType.DMA((N_DEV - 1,))],
    )(s)


def _adamw(name, g, w, m, v):
    R, C = w.shape
    tm = R
    for cand in (256, 128, 64, 32, 16, 8):
        if R % cand == 0:
            tm = cand
            break

    def body(g, w, m, v):
        m = ADAM_B1 * m + (1.0 - ADAM_B1) * g
        v = ADAM_B2 * v + (1.0 - ADAM_B2) * jnp.square(g)
        m_hat = m / (1.0 - ADAM_B1 ** ADAM_STEP)
        v_hat = v / (1.0 - ADAM_B2 ** ADAM_STEP)
        delta = -ADAM_LR * (m_hat / (jnp.sqrt(v_hat) + ADAM_EPS) + ADAM_WD * w)
        return delta, m, v

    return _rowwise(name, body, R, tm, [_row(t, tm) for t in (g, w, m, v)], [(C, F32)] * 3)


def _unpack_weights(gathered):
    S = gathered.shape[0]
    shard_shapes = {"w_in": (D_MODEL, (QKV_WIDTH + 2 * D_MODEL) // S), "w_branch_na": (NA_WIDTH, D_MODEL // S),
                    "w_branch_dil": (DIL_OUT_WIDTH, D_MODEL // S), "w_out": (D_MODEL // S, D_MODEL),
                    "w_up": (D_MODEL, D_FF // S), "w_down": (D_FF // S, D_MODEL),
                    "w_ple_gate": (D_MODEL // S, D_MODEL), "w_ple_proj": (PLE_DIM, D_MODEL // S)}
    col_sharded = {"w_in", "w_branch_na", "w_branch_dil", "w_up", "w_ple_proj"}
    out, r0 = {}, 0
    for name in BIG:
        rows, cols = shard_shapes[name]
        n = rows * cols // PACK_W
        t = gathered[:, r0:r0 + n, :].reshape(S, rows, cols)
        r0 += n
        out[name] = t.transpose(1, 0, 2).reshape(rows, S * cols) if name in col_sharded else t.reshape(S * rows, cols)
    return out


def _pack_grads(grads):
    col_sharded = {"w_in", "w_branch_na", "w_branch_dil", "w_up", "w_ple_proj"}
    per_chip = []
    for s in range(N_CHIPS):
        rows = []
        for name in BIG:
            g = grads[name]
            if name in col_sharded:
                w = g.shape[1] // N_CHIPS
                rows.append(_pack_rows(g[:, s * w:(s + 1) * w]))
            else:
                h = g.shape[0] // N_CHIPS
                rows.append(_pack_rows(g[s * h:(s + 1) * h]))
        per_chip.append(jnp.concatenate(rows, axis=0))
    return jnp.stack(per_chip)


def _unpack_shard(packed, shapes):
    out, r0 = {}, 0
    for name in BIG:
        rows, cols = shapes[name]
        n = rows * cols // PACK_W
        out[name] = packed[r0:r0 + n].reshape(rows, cols)
        r0 += n
    return out


def kernel(x, p, positions, g_mix, w_in, rpb, w_branch_na, w_branch_dil, w_out, g_mlp, w_up, w_down, g_ple, w_ple_gate, w_ple_proj, g_final, loss_target, m_g_mix, m_w_in, m_rpb, m_w_branch_na, m_w_branch_dil, m_w_out, m_g_mlp, m_w_up, m_w_down, m_g_ple, m_w_ple_gate, m_w_ple_proj, m_g_final, v_g_mix, v_w_in, v_rpb, v_w_branch_na, v_w_branch_dil, v_w_out, v_g_mlp, v_w_up, v_w_down, v_g_ple, v_w_ple_gate, v_w_ple_proj, v_g_final):
    shards = {"w_in": w_in[0], "w_branch_na": w_branch_na[0], "w_branch_dil": w_branch_dil[0], "w_out": w_out[0],
              "w_up": w_up[0], "w_down": w_down[0], "w_ple_gate": w_ple_gate[0], "w_ple_proj": w_ple_proj[0]}
    m_shards = {"w_in": m_w_in[0], "w_branch_na": m_w_branch_na[0], "w_branch_dil": m_w_branch_dil[0], "w_out": m_w_out[0],
                "w_up": m_w_up[0], "w_down": m_w_down[0], "w_ple_gate": m_w_ple_gate[0], "w_ple_proj": m_w_ple_proj[0]}
    v_shards = {"w_in": v_w_in[0], "w_branch_na": v_w_branch_na[0], "w_branch_dil": v_w_branch_dil[0], "w_out": v_w_out[0],
                "w_up": v_w_up[0], "w_down": v_w_down[0], "w_ple_gate": v_w_ple_gate[0], "w_ple_proj": v_w_ple_proj[0]}

    packed_w = jnp.concatenate([_pack_rows(shards[n].astype(BF)) for n in BIG], axis=0)
    W = _unpack_weights(_gather_weights(packed_w))
    w_qkv, w_gates = W["w_in"][:, :QKV_WIDTH], W["w_in"][:, QKV_WIDTH:]

    xs, ps, tgt = x[0], p[0, 0], loss_target[0]
    T = xs.shape[0]
    TM = 256
    gm, gl, gp, gf = g_mix, g_mlp, g_ple, g_final.reshape(1, D_MODEL)
    cos2, sin_signed = _rope_tables(positions[0])
    tab = _na_bias_table(rpb[0])

    a = _rowwise("norm_mix", lambda h, g: h * _rms(h) * g, T, TM, [_row(xs, TM), _full(gm)], [(D_MODEL, BF)])
    z_qkv = _mm("in_qkv", a, w_qkv, "nn", 512, 1280, 1024, [F32])
    z_gates = _mm("in_gates", a, w_gates, "nn", 512, 1024, 1024, [F32])

    def prep(z, cs, sn):
        n3 = 3 * NA_WIDTH
        return jnp.concatenate([
            z[:, :NA_WIDTH] * Q_SCALE, z[:, NA_WIDTH:n3],
            _rope_cols(z[:, n3:n3 + DIL_WIDTH], cs, sn) * Q_SCALE,
            _rope_cols(z[:, n3 + DIL_WIDTH:n3 + 2 * DIL_WIDTH], cs, sn),
            z[:, n3 + 2 * DIL_WIDTH:]], axis=1)

    qkv = _rowwise("qkv_prep", prep, T, TM, [_row(z_qkv, TM), _row(cos2, TM), _row(sin_signed, TM)], [(QKV_WIDTH, BF)])
    n3 = 3 * NA_WIDTH
    gw = DIL_OUT_WIDTH

    def group_cols(t, part, g, dil):
        return _to_segments(t[:, n3 + part * DIL_WIDTH + g * gw:n3 + part * DIL_WIDTH + (g + 1) * gw], dil)

    qkv_seg = [[group_cols(qkv, part, g, dil) for part in range(3)] for g, (_, dil) in enumerate(DIL_GROUPS)]
    y_na = _na_fwd(qkv, tab)
    band = [_band_fwd(*qkv_seg[g], g) for g in range(len(DIL_GROUPS))]
    o_nat = [_from_segments(b[0], dil) for b, (_, dil) in zip(band, DIL_GROUPS)]
    lse_nat = [_from_segments(b[1], dil) for b, (_, dil) in zip(band, DIL_GROUPS)]
    y_dil, w_grp = _dil_merge_fwd(o_nat, lse_nat, TM)

    u_na = _mm("branch_na", y_na, W["w_branch_na"], "nn", 512, 1024, 512, [F32])
    u_dil = _mm("branch_dil", y_dil, W["w_branch_dil"], "nn", 512, 1024, 256, [F32])
    mixed = _rowwise(
        "gate_mix", lambda gn, gd, un, ud: _sigmoid(gn) * un + _sigmoid(gd) * ud, T, TM,
        [_row(z_gates, TM, 0, D_MODEL), _row(z_gates, TM, 1, D_MODEL), _row(u_na, TM), _row(u_dil, TM)], [(D_MODEL, BF)])
    mix_out = _mm("out_proj", mixed, W["w_out"], "nn", 512, 1024, 1024, [F32])

    def add_norm(h, d, g):
        h = h + d
        return h, h * _rms(h) * g

    h1, cn = _rowwise("add_norm_mlp", add_norm, T, TM, [_row(xs, TM), _row(mix_out, TM), _full(gl)], [(D_MODEL, F32), (D_MODEL, BF)])
    up, act = _mm("mlp_up", cn, W["w_up"], "nn", 512, 1024, 1024, [F32, BF],
                  epilogue=lambda acc: (acc, jnp.square(jnp.maximum(acc, 0.0))))
    mlp_out = _mm("mlp_down", act, W["w_down"], "nn", 512, 1024, 1024, [F32])
    h2, en = _rowwise("add_norm_ple", add_norm, T, TM, [_row(h1, TM), _row(mlp_out, TM), _full(gp)], [(D_MODEL, F32), (D_MODEL, BF)])
    gt = _mm("ple_gate", en, W["w_ple_gate"], "nn", 512, 1024, 1024, [F32])
    pp = _mm("ple_proj", ps, W["w_ple_proj"], "nn", 512, 1024, 256, [F32])

    def head(h2t, gtt, ppt, tg, g):
        sg = _sigmoid(gtt)
        h3 = h2t + sg * ppt
        yo = h3 * _rms(h3) * g
        diff = yo - tg
        loss = 0.5 * jnp.sum(jnp.mean(jnp.square(diff), axis=-1, keepdims=True), axis=0, keepdims=True)
        dh3, dg = _rms_bwd(diff * (1.0 / D_MODEL), h3, g)
        return dh3, dh3 * ppt * sg * (1.0 - sg), dh3 * sg, jnp.broadcast_to(loss, (1, 128)), dg

    dh3, d_gt, d_pp, loss_part, dg_final = _rowwise(
        "loss_head", head, T, TM, [_row(h2, TM), _row(gt, TM), _row(pp, TM), _row(tgt, TM), _full(gf)],
        [(D_MODEL, F32), (D_MODEL, BF), (D_MODEL, BF)], sums=[128, D_MODEL])

    grads = {}
    grads["w_ple_proj"] = _mm("g_ple_proj", ps, d_pp, "tn", 256, 1024, 512, [F32])
    grads["w_ple_gate"] = _mm("g_ple_gate", en, d_gt, "tn", 512, 1024, 512, [F32])
    d_en = _mm("d_ple_gate", d_gt, W["w_ple_gate"], "nt", 512, 1024, 1024, [F32])

    def add_norm_bwd(dh_out, dn, h, g):
        dh, dg = _rms_bwd(dn, h, g)
        dh = dh_out + dh
        return dh, dh, dg

    dh2, dh2_b, dg_ple = _rowwise("add_norm_ple_bwd", add_norm_bwd, T, TM, [_row(dh3, TM), _row(d_en, TM), _row(h2, TM), _full(gp)],
                                  [(D_MODEL, F32), (D_MODEL, BF)], sums=[D_MODEL])
    d_up = _mm("d_mlp_down", dh2_b, W["w_down"], "nt", 512, 1024, 1024, [BF],
               epilogue=lambda acc, u: (acc * (2.0 * jnp.maximum(u, 0.0)),), extras=(up,))
    grads["w_down"] = _mm("g_mlp_down", act, dh2_b, "tn", 512, 1024, 512, [F32])
    grads["w_up"] = _mm("g_mlp_up", cn, d_up, "tn", 512, 1024, 512, [F32])
    d_cn = _mm("d_mlp_up", d_up, W["w_up"], "nt", 512, 1024, 1024, [F32])
    dh1, dh1_b, dg_mlp = _rowwise("add_norm_mlp_bwd", add_norm_bwd, T, TM, [_row(dh2, TM), _row(d_cn, TM), _row(h1, TM), _full(gl)],
                                  [(D_MODEL, F32), (D_MODEL, BF)], sums=[D_MODEL])
    d_mixed = _mm("d_out_proj", dh1_b, W["w_out"], "nt", 512, 1024, 1024, [F32])
    grads["w_out"] = _mm("g_out_proj", mixed, dh1_b, "tn", 512, 1024, 512, [F32])

    def gate_bwd(dm, gn, gd, un, ud):
        sn, sd = _sigmoid(gn), _sigmoid(gd)
        return jnp.concatenate([dm * un * sn * (1.0 - sn), dm * ud * sd * (1.0 - sd)], axis=1), dm * sn, dm * sd

    dz_gates, d_u_na, d_u_dil = _rowwise(
        "gate_mix_bwd", gate_bwd, T, TM,
        [_row(d_mixed, TM), _row(z_gates, TM, 0, D_MODEL), _row(z_gates, TM, 1, D_MODEL), _row(u_na, TM), _row(u_dil, TM)],
        [(2 * D_MODEL, BF), (D_MODEL, BF), (D_MODEL, BF)])
    grads["w_branch_na"] = _mm("g_branch_na", y_na, d_u_na, "tn", 512, 1024, 512, [F32])
    grads["w_branch_dil"] = _mm("g_branch_dil", y_dil, d_u_dil, "tn", 256, 1024, 512, [F32])
    d_y_na = _mm("d_branch_na", d_u_na, W["w_branch_na"], "nt", 512, 512, 1024, [BF])
    d_y_dil = _mm("d_branch_dil", d_u_dil, W["w_branch_dil"], "nt", 512, 256, 1024, [F32])

    dqa, dka, dva, dtab = _na_bwd(qkv, tab, d_y_na)
    d_rpb = _na_rpb_grad(dtab)[:, :2 * NA_WIN_ROWS - 1, :2 * NA_WIN_COLS - 1]

    do_nat, dlse_nat = _dil_merge_bwd(d_y_dil, o_nat, w_grp, TM)
    d_dil = []
    for g, (_, dil) in enumerate(DIL_GROUPS):
        res = _band_bwd(*qkv_seg[g], _to_segments(do_nat[g], dil), _to_segments(dlse_nat[g], dil), g)
        d_dil.append([_from_segments(t, dil) for t in res])
    d_dil = [d_dil[g][part] for part in range(3) for g in range(len(DIL_GROUPS))]

    def unprep(dq, dk, dv, *rest):
        dd, (cs, sn) = rest[:-2], rest[-2:]
        G = len(DIL_GROUPS)
        return jnp.concatenate([
            dq * Q_SCALE, dk, dv,
            _rope_cols(jnp.concatenate(dd[:G], axis=1), cs, -sn) * Q_SCALE,
            _rope_cols(jnp.concatenate(dd[G:2 * G], axis=1), cs, -sn),
            *dd[2 * G:]], axis=1)

    dz_qkv = _rowwise("qkv_unprep", unprep, T, TM, [_row(t, TM) for t in (dqa, dka, dva, *d_dil, cos2, sin_signed)],
                      [(QKV_WIDTH, BF)])
    grads["w_in"] = jnp.concatenate([
        _mm("g_in_qkv", a, dz_qkv, "tn", 512, 1280, 512, [F32]),
        _mm("g_in_gates", a, dz_gates, "tn", 512, 1024, 512, [F32])], axis=1)
    d_a = _mm("d_in_qkv", dz_qkv, w_qkv, "nt", 512, 1024, 1280, [F32])
    d_a = _mm("d_in_gates", dz_gates, w_gates, "nt", 512, 1024, 1024, [F32], epilogue=lambda acc, e: (acc + e,), extras=(d_a,))

    def first_bwd(dh_out, dn, h, g):
        dh, dg = _rms_bwd(dn, h, g)
        return dh_out + dh, dg

    grad_x, dg_mix = _rowwise("norm_mix_bwd", first_bwd, T, TM, [_row(dh1, TM), _row(d_a, TM), _row(xs, TM), _full(gm)],
                              [(D_MODEL, F32)], sums=[D_MODEL])

    packed_g = _pack_grads(grads)
    got = _swap_halves(packed_g)
    pair, pair_b = _pair_sum(packed_g, got, 536)
    own = lax.dynamic_index_in_dim(pair, 2 * lax.axis_index("x") + lax.axis_index("y"), 0, keepdims=False)
    mine = _chip_sum(own, _scatter_chips(pair_b), 536)
    g_shard = _unpack_shard(_join_halves(mine), {n: shards[n].shape for n in BIG})

    n_rpb = rpb.size
    rpb_rows = 4
    small = jnp.concatenate([
        dg_mix, dg_mlp, dg_ple, dg_final,
        jnp.pad(d_rpb.reshape(-1), (0, rpb_rows * D_MODEL - n_rpb)).reshape(rpb_rows, D_MODEL),
        jnp.pad(loss_part, ((0, 0), (0, D_MODEL - loss_part.shape[1]))),
        jnp.zeros((SMALL_ROWS - 5 - rpb_rows, D_MODEL), F32)], axis=0)
    small = _allreduce_small(small)
    loss = small[4 + rpb_rows, 0]

    def small_pack(a0, a1, a2, a3, r):
        return jnp.concatenate([a0.reshape(1, -1), a1.reshape(1, -1), a2.reshape(1, -1), a3.reshape(1, -1),
                                jnp.pad(r.reshape(-1), (0, rpb_rows * D_MODEL - n_rpb)).reshape(rpb_rows, D_MODEL)], axis=0)

    g_small = small[:4 + rpb_rows]
    small_res = _adamw("adamw_small", g_small, small_pack(g_mix, g_mlp, g_ple, g_final, rpb),
                       small_pack(m_g_mix, m_g_mlp, m_g_ple, m_g_final, m_rpb), small_pack(v_g_mix, v_g_mlp, v_g_ple, v_g_final, v_rpb))

    def small_unpack(t):
        return {"g_mix": t[0].reshape(g_mix.shape), "g_mlp": t[1].reshape(g_mlp.shape), "g_ple": t[2].reshape(g_ple.shape),
                "g_final": t[3].reshape(g_final.shape), "rpb": t[4:].reshape(-1)[:n_rpb].reshape(rpb.shape)}

    out = {"grad": small_unpack(g_small)}
    for kind, t in zip(("delta", "new_m", "new_v"), small_res, strict=True):
        out[kind] = small_unpack(t)
    for n in BIG:
        out["grad"][n] = g_shard[n][None]
        res = _adamw("adamw_" + n, g_shard[n], shards[n], m_shards[n], v_shards[n])
        for kind, t in zip(("delta", "new_m", "new_v"), res, strict=True):
            out[kind][n] = t[None]

    order = ["g_mix", "w_in", "rpb", "w_branch_na", "w_branch_dil", "w_out", "g_mlp", "w_up", "w_down", "g_ple",
             "w_ple_gate", "w_ple_proj", "g_final"]
    return (loss, grad_x[None], *[out["grad"][n] for n in order], *[out["delta"][n] for n in order],
            *[out["new_m"][n] for n in order], *[out["new_v"][n] for n in order])
```

```python
import functools

import numpy as np
import jax
import jax.numpy as jnp
from jax import lax
from jax.experimental import pallas as pl
from jax.experimental.pallas import tpu as pltpu

BF = jnp.bfloat16
F32 = jnp.float32
MESH = pl.DeviceIdType.MESH
ANY = pl.BlockSpec(memory_space=pl.ANY)

V7X_VMEM_BYTES = 64 * 1024 * 1024
VMEM_LIMIT = V7X_VMEM_BYTES - 16 * 1024 * 1024

D_MODEL = 1024
HEAD_DIM = 64
GRID_W = 64
NA_HEADS = 8
NA_WIN_ROWS = 8
NA_WIN_COLS = 16
NA_WIDTH = NA_HEADS * HEAD_DIM
DIL_GROUPS = ((128, 1), (512, 4), (2048, 16))
DIL_HPG = 4
DIL_HEADS = DIL_HPG * len(DIL_GROUPS)
DIL_WIDTH = DIL_HEADS * HEAD_DIM
DIL_OUT_WIDTH = DIL_HPG * HEAD_DIM
DIL_RADIUS = 64
QKV_WIDTH = 3 * NA_WIDTH + 3 * DIL_WIDTH
D_FF = 4 * D_MODEL
PLE_DIM = 256
ROPE_THETA = 10000.0
RMS_EPS = 1e-6
NEG_INF = -1e30
Q_SCALE = HEAD_DIM ** -0.5

ADAM_LR = 0.001
ADAM_B1 = 0.9
ADAM_B2 = 0.999
ADAM_EPS = 1e-08
ADAM_WD = 0.01
ADAM_STEP = 10

N_CHIPS = 4
N_DEV = 8
PACK_W = 1024
BIG = ("w_in", "w_branch_na", "w_branch_dil", "w_out", "w_up", "w_down", "w_ple_gate", "w_ple_proj")
SMALL_ROWS = 16


def _cparams(sem=None):
    return pltpu.CompilerParams(dimension_semantics=sem, vmem_limit_bytes=VMEM_LIMIT)


def _mm(name, a, b, mode, tm, tn, tk, out_dtypes, epilogue=None, extras=()):
    if mode == "nn":
        (M, K), N = a.shape, b.shape[1]
        a_spec = pl.BlockSpec((tm, tk), lambda i, j, k: (i, k))
        b_spec = pl.BlockSpec((tk, tn), lambda i, j, k: (k, j))
        dims = (((1,), (0,)), ((), ()))
    elif mode == "nt":
        (M, K), N = a.shape, b.shape[0]
        a_spec = pl.BlockSpec((tm, tk), lambda i, j, k: (i, k))
        b_spec = pl.BlockSpec((tn, tk), lambda i, j, k: (j, k))
        dims = (((1,), (1,)), ((), ()))
    else:
        (K, M), N = a.shape, b.shape[1]
        a_spec = pl.BlockSpec((tk, tm), lambda i, j, k: (k, i))
        b_spec = pl.BlockSpec((tk, tn), lambda i, j, k: (k, j))
        dims = (((0,), (0,)), ((), ()))
    tm, tn, tk = min(tm, M), min(tn, N), min(tk, K)
    assert M % tm == 0 and N % tn == 0 and K % tk == 0, (name, M, N, K, tm, tn, tk)
    nk = K // tk
    n_extra, n_out = len(extras), len(out_dtypes)
    tile = pl.BlockSpec((tm, tn), lambda i, j, k: (i, j))

    def body(a_ref, b_ref, *rest):
        extra_refs, out_refs, acc = rest[:n_extra], rest[n_extra:n_extra + n_out], rest[-1]
        k = pl.program_id(2)

        @pl.when(k == 0)
        def _():
            acc[...] = jnp.zeros_like(acc)

        acc[...] += lax.dot_general(a_ref[...].astype(BF), b_ref[...].astype(BF), dims, preferred_element_type=F32)

        @pl.when(k == nk - 1)
        def _():
            outs = (acc[...],) if epilogue is None else epilogue(acc[...], *[e[...] for e in extra_refs])
            for o_ref, val in zip(out_refs, outs, strict=True):
                o_ref[...] = val.astype(o_ref.dtype)

    outs = pl.pallas_call(
        body, name=name, grid=(M // tm, N // tn, nk),
        in_specs=[a_spec, b_spec] + [tile] * n_extra,
        out_specs=[tile] * n_out,
        out_shape=[jax.ShapeDtypeStruct((M, N), dt) for dt in out_dtypes],
        scratch_shapes=[pltpu.VMEM((tm, tn), F32)],
        compiler_params=_cparams(("parallel", "parallel", "arbitrary")),
    )(a, b, *extras)
    return outs[0] if n_out == 1 else outs


def _row(arr, tm, col_block=None, width=None):
    width = arr.shape[1] if width is None else width
    cb = 0 if col_block is None else col_block
    return arr, pl.BlockSpec((tm, width), lambda i: (i, cb))


def _full(arr):
    nd = arr.ndim
    return arr, pl.BlockSpec(arr.shape, lambda i: (0,) * nd)


def _rowwise(name, body, T, tm, ins, outs, sums=()):
    n_in, n_out, n_sum = len(ins), len(outs), len(sums)

    def kern(*refs):
        in_refs, out_refs, sum_refs = refs[:n_in], refs[n_in:n_in + n_out], refs[n_in + n_out:]
        res = body(*[r[...] for r in in_refs])
        res = res if isinstance(res, tuple) else (res,)
        for o_ref, val in zip(out_refs, res[:n_out], strict=True):
            o_ref[...] = val.astype(o_ref.dtype)
        if n_sum:
            @pl.when(pl.program_id(0) == 0)
            def _():
                for s_ref in sum_refs:
                    s_ref[...] = jnp.zeros_like(s_ref)

            for s_ref, val in zip(sum_refs, res[n_out:], strict=True):
                s_ref[...] += val

    res = pl.pallas_call(
        kern, name=name, grid=(T // tm,),
        in_specs=[spec for _, spec in ins],
        out_specs=[pl.BlockSpec((tm, c), lambda i: (i, 0)) for c, _ in outs]
        + [pl.BlockSpec((1, c), lambda i: (0, 0)) for c in sums],
        out_shape=[jax.ShapeDtypeStruct((T, c), dt) for c, dt in outs]
        + [jax.ShapeDtypeStruct((1, c), F32) for c in sums],
        compiler_params=_cparams(("arbitrary",)),
    )(*[a for a, _ in ins])
    return res[0] if len(res) == 1 else res


def _sigmoid(x):
    return 1.0 / (1.0 + jnp.exp(-x))


def _rms(h):
    return lax.rsqrt(jnp.mean(h * h, axis=-1, keepdims=True) + RMS_EPS)


def _rms_bwd(dy, h, g):
    r = _rms(h)
    n = h * r
    dn = dy * g
    dh = r * (dn - n * jnp.mean(dn * n, axis=-1, keepdims=True))
    return dh, jnp.sum(dy * n, axis=0, keepdims=True)


def _rope(x, cos2, sin_signed):
    lane = lax.broadcasted_iota(jnp.int32, x.shape, 1)
    swapped = jnp.where((lane % HEAD_DIM) < HEAD_DIM // 2, pltpu.roll(x, 128 - HEAD_DIM // 2, 1), pltpu.roll(x, HEAD_DIM // 2, 1))
    return x * cos2 + swapped * sin_signed


def _rope_cols(x, cos2, sin_signed):
    return jnp.concatenate([_rope(x[:, c:c + 128], cos2, sin_signed) for c in range(0, x.shape[1], 128)], axis=1)


NA_KEYS = NA_WIN_ROWS * GRID_W
NA_BASES = 8


def _na_row_geometry(r, rows):
    first = jnp.clip(r - NA_WIN_ROWS // 2, 0, rows - NA_WIN_ROWS)
    base = first - r + (NA_WIN_ROWS - 1)
    return pl.multiple_of(first * GRID_W, GRID_W), base


NA_ROWS_PER_STEP = 8
NA_BWD_ROWS_PER_STEP = 4


def _softmax_rows(s):
    p = jnp.exp(s - jnp.max(s, axis=-1, keepdims=True))
    return p / jnp.sum(p, axis=-1, keepdims=True)


def _na_probs(q, kw, bias):
    return _softmax_rows(lax.dot_general(q, kw, (((1,), (1,)), ((), ())), preferred_element_type=F32) + bias)


def _split_pair(t):
    first = lax.broadcasted_iota(jnp.int32, t.shape, 1) < HEAD_DIM
    zero = jnp.zeros_like(t)
    return jnp.where(first, t, zero), jnp.where(first, zero, t)


def _join_pair(a, b):
    return jnp.where(lax.broadcasted_iota(jnp.int32, a.shape, 1) < HEAD_DIM, a, b)


_NT = (((1,), (1,)), ((), ()))
_TN = (((0,), (0,)), ((), ()))


def _na_fwd(qkv, tab):
    T = qkv.shape[0]
    rows = T // GRID_W
    n_pairs = NA_WIDTH // 128

    def body(q_ref, k_ref, v_ref, tab_ref, y_ref):
        def step(it, carry):
            geo = [_na_row_geometry(it * NA_ROWS_PER_STEP + u, rows) for u in range(NA_ROWS_PER_STEP)]
            q0s = [pl.multiple_of((it * NA_ROWS_PER_STEP + u) * GRID_W, GRID_W) for u in range(NA_ROWS_PER_STEP)]
            ss = [lax.dot_general(jnp.concatenate(_split_pair(q_ref[pl.ds(q0, GRID_W), :]), axis=0),
                                  k_ref[pl.ds(k0, NA_KEYS), :], _NT, preferred_element_type=F32)
                  for q0, (k0, _) in zip(q0s, geo)]
            ps = [_softmax_rows(s + jnp.concatenate([tab_ref[0, base], tab_ref[1, base]], axis=0)) for s, (_, base) in zip(ss, geo)]
            ys = [jnp.dot(p.astype(BF), v_ref[pl.ds(k0, NA_KEYS), :], preferred_element_type=F32) for p, (k0, _) in zip(ps, geo)]
            for q0, y2 in zip(q0s, ys):
                y_ref[pl.ds(q0, GRID_W), :] = _join_pair(y2[:GRID_W], y2[GRID_W:]).astype(y_ref.dtype)
            return carry

        lax.fori_loop(0, rows // NA_ROWS_PER_STEP, step, 0)

    def cols(first):
        return pl.BlockSpec((T, 128), lambda j: (0, first + j))

    return pl.pallas_call(
        body, name="na_fwd", grid=(n_pairs,),
        in_specs=[cols(0), cols(n_pairs), cols(2 * n_pairs), pl.BlockSpec((2, NA_BASES, GRID_W, NA_KEYS), lambda j: (j, 0, 0, 0))],
        out_specs=cols(0), out_shape=jax.ShapeDtypeStruct((T, NA_WIDTH), BF),
        compiler_params=_cparams(("parallel",)),
    )(qkv, qkv, qkv, tab)


def _na_bwd(qkv, tab, do):
    T = qkv.shape[0]
    rows = T // GRID_W
    n_pairs = NA_WIDTH // 128

    def body(q_ref, k_ref, v_ref, tab_ref, do_ref, dq_ref, dk_ref, dv_ref, dtab_ref):
        dk_ref[...] = jnp.zeros_like(dk_ref)
        dv_ref[...] = jnp.zeros_like(dv_ref)
        dtab_ref[...] = jnp.zeros_like(dtab_ref)

        def step(it, carry):
            U = NA_BWD_ROWS_PER_STEP
            geo = [_na_row_geometry(it * U + u, rows) for u in range(U)]
            q0s = [pl.multiple_of((it * U + u) * GRID_W, GRID_W) for u in range(U)]
            q2s = [jnp.concatenate(_split_pair(q_ref[pl.ds(q0, GRID_W), :]), axis=0) for q0 in q0s]
            do2s = [jnp.concatenate(_split_pair(do_ref[pl.ds(q0, GRID_W), :]), axis=0) for q0 in q0s]
            ss = [lax.dot_general(q2, k_ref[pl.ds(k0, NA_KEYS), :], _NT, preferred_element_type=F32) for q2, (k0, _) in zip(q2s, geo)]
            dps = [lax.dot_general(do2, v_ref[pl.ds(k0, NA_KEYS), :], _NT, preferred_element_type=F32) for do2, (k0, _) in zip(do2s, geo)]
            ps = [_softmax_rows(s + jnp.concatenate([tab_ref[0, base], tab_ref[1, base]], axis=0)) for s, (_, base) in zip(ss, geo)]
            dss = [p * (dp - jnp.sum(dp * p, axis=-1, keepdims=True)) for p, dp in zip(ps, dps)]
            dvs = [lax.dot_general(p.astype(BF), do2, _TN, preferred_element_type=F32) for p, do2 in zip(ps, do2s)]
            dsbs = [ds.astype(BF) for ds in dss]
            dqs = [jnp.dot(dsb, k_ref[pl.ds(k0, NA_KEYS), :], preferred_element_type=F32) for dsb, (k0, _) in zip(dsbs, geo)]
            dks = [lax.dot_general(dsb, q2, _TN, preferred_element_type=F32) for dsb, q2 in zip(dsbs, q2s)]
            for u in range(U):
                k0, base = geo[u]
                dtab_ref[0, base] += dss[u][:GRID_W]
                dtab_ref[1, base] += dss[u][GRID_W:]
                dq_ref[pl.ds(q0s[u], GRID_W), :] = _join_pair(dqs[u][:GRID_W], dqs[u][GRID_W:])
                dk_ref[pl.ds(k0, NA_KEYS), :] += dks[u]
                dv_ref[pl.ds(k0, NA_KEYS), :] += dvs[u]
            return carry

        lax.fori_loop(0, rows // NA_BWD_ROWS_PER_STEP, step, 0)

    def cols(first):
        return pl.BlockSpec((T, 128), lambda j: (0, first + j))

    tabs = pl.BlockSpec((2, NA_BASES, GRID_W, NA_KEYS), lambda j: (j, 0, 0, 0))
    wide = jax.ShapeDtypeStruct((T, NA_WIDTH), F32)
    return pl.pallas_call(
        body, name="na_bwd", grid=(n_pairs,),
        in_specs=[cols(0), cols(n_pairs), cols(2 * n_pairs), tabs, cols(0)],
        out_specs=[cols(0), cols(0), cols(0), tabs],
        out_shape=[wide, wide, wide, jax.ShapeDtypeStruct((NA_HEADS, NA_BASES, GRID_W, NA_KEYS), F32)],
        compiler_params=_cparams(("parallel",)),
    )(qkv, qkv, qkv, tab, do)


def _na_geometry_np():
    c = np.arange(GRID_W)
    cs = np.clip(c - NA_WIN_COLS // 2, 0, GRID_W - NA_WIN_COLS)
    valid = (c[None, :] >= cs[:, None]) & (c[None, :] < cs[:, None] + NA_WIN_COLS)
    off = c[None, :] - c[:, None] + (NA_WIN_COLS - 1)
    oh_col = np.zeros((GRID_W, GRID_W, 2 * NA_WIN_COLS - 1), np.float32)
    qq, kk = np.nonzero(valid)
    oh_col[qq, kk, off[qq, kk]] = 1.0
    oh_row = np.zeros((NA_BASES, NA_WIN_ROWS, 2 * NA_WIN_ROWS - 1), np.float32)
    for base in range(NA_BASES):
        for i in range(NA_WIN_ROWS):
            oh_row[base, i, base + i] = 1.0
    return valid, oh_col, oh_row


def _na_bias_table(rpb):
    valid, oh_col, oh_row = _na_geometry_np()
    hi = lax.Precision.HIGHEST
    t1 = jnp.einsum("hrc,pir->hpic", rpb, oh_row, precision=hi)
    tab = jnp.einsum("hpic,qkc->hpqik", t1, oh_col, precision=hi)
    tab = jnp.where(valid[None, None, :, None, :], tab, NEG_INF)
    return tab.reshape(rpb.shape[0], NA_BASES, GRID_W, NA_KEYS)


def _na_rpb_grad(dtab):
    H = dtab.shape[0]
    n_rows = 2 * NA_WIN_ROWS - 1
    n_cols = 2 * NA_WIN_COLS - 1

    def body(d_ref, o_ref):
        lane = lax.broadcasted_iota(jnp.int32, (GRID_W, 128), 1)
        low = lane < GRID_W
        out_rows = []
        for ro in range(n_rows):
            acc = jnp.zeros((GRID_W, 128), F32)
            for base in range(NA_BASES):
                i = ro - base
                if not 0 <= i < NA_WIN_ROWS:
                    continue
                pair = d_ref[base, :, pl.ds((i // 2) * 128, 128)]
                if i % 2:
                    pair = pltpu.roll(pair, GRID_W, 1)
                acc = acc + jnp.where(low, pair, 0.0)
            skew = pltpu.roll(acc, 0, 1, stride=1, stride_axis=0)
            diag = jnp.sum(skew, axis=0, keepdims=True)
            out_rows.append(pltpu.roll(jnp.broadcast_to(diag, (8, 128)), 128 - (GRID_W - NA_WIN_COLS), 1)[:1])
        out_rows.append(jnp.zeros((1, 128), F32))
        res = jnp.concatenate(out_rows, axis=0)
        o_ref[...] = jnp.where(lax.broadcasted_iota(jnp.int32, res.shape, 1) < n_cols, res, 0.0)

    return pl.pallas_call(
        body, name="na_rpb_grad", grid=(H,),
        in_specs=[pl.BlockSpec((None, NA_BASES, GRID_W, NA_KEYS), lambda h: (h, 0, 0, 0))],
        out_specs=pl.BlockSpec((None, n_rows + 1, 128), lambda h: (h, 0, 0)),
        out_shape=jax.ShapeDtypeStruct((H, n_rows + 1, 128), F32),
        compiler_params=_cparams(("parallel",)),
    )(jnp.flip(dtab, axis=2))


BAND_Q = 128
BAND_KEYS = BAND_Q + 2 * DIL_RADIUS


def _band_geometry(n, T, seg_shift):
    q0 = pl.multiple_of(n * BAND_Q, BAND_Q)
    k0 = pl.multiple_of(jnp.clip(q0 - DIL_RADIUS, 0, T - BAND_KEYS), DIL_RADIUS)
    qi = q0 + lax.broadcasted_iota(jnp.int32, (BAND_Q, BAND_KEYS), 0)
    kj = k0 + lax.broadcasted_iota(jnp.int32, (BAND_Q, BAND_KEYS), 1)
    valid = ((qi >> seg_shift) == (kj >> seg_shift)) & (jnp.abs(qi - kj) <= DIL_RADIUS)
    return q0, k0, valid


BAND_BLOCKS_PER_STEP = 4


def _band_softmax(s, valid):
    s = jnp.where(valid, s, NEG_INF)
    m = jnp.max(s, axis=-1, keepdims=True)
    p = jnp.exp(s - m)
    l = jnp.sum(p, axis=-1, keepdims=True)
    return p / l, m + jnp.log(l)


def _band_fwd(q, k, v, group):
    T, W = q.shape
    seg_shift = (T // DIL_GROUPS[group][1]).bit_length() - 1

    def body(q_ref, k_ref, v_ref, o_ref, lse_ref):
        def step(it, carry):
            U = BAND_BLOCKS_PER_STEP
            geo = [_band_geometry(it * U + u, T, seg_shift) for u in range(U)]
            ss = [lax.dot_general(jnp.concatenate(_split_pair(q_ref[pl.ds(q0, BAND_Q), :]), axis=0),
                                  k_ref[pl.ds(k0, BAND_KEYS), :], _NT, preferred_element_type=F32) for q0, k0, _ in geo]
            pls = [_band_softmax(s, jnp.concatenate([valid, valid], axis=0)) for s, (_, _, valid) in zip(ss, geo)]
            os = [jnp.dot(p.astype(BF), v_ref[pl.ds(k0, BAND_KEYS), :], preferred_element_type=F32) for (p, _), (_, k0, _) in zip(pls, geo)]
            for (q0, _, _), o2, (_, lse) in zip(geo, os, pls):
                o_ref[pl.ds(q0, BAND_Q), :] = _join_pair(o2[:BAND_Q], o2[BAND_Q:])
                lse2 = jnp.broadcast_to(lse, (2 * BAND_Q, 128))
                lse_ref[pl.ds(q0, BAND_Q), :] = _join_pair(lse2[:BAND_Q], lse2[BAND_Q:])
            return carry

        lax.fori_loop(0, T // (BAND_Q * BAND_BLOCKS_PER_STEP), step, 0)

    pair = pl.BlockSpec((T, 128), lambda j: (0, j))
    wide = jax.ShapeDtypeStruct((T, W), F32)
    return pl.pallas_call(
        body, name=f"band_fwd_g{group}", grid=(W // 128,),
        in_specs=[pair, pair, pair], out_specs=[pair, pair], out_shape=[wide, wide],
        compiler_params=_cparams(("parallel",)),
    )(q, k, v)


def _band_bwd(q, k, v, do, dlse, group):
    T, W = q.shape
    seg_shift = (T // DIL_GROUPS[group][1]).bit_length() - 1

    def body(q_ref, k_ref, v_ref, do_ref, dlse_ref, dq_ref, dk_ref, dv_ref):
        dk_ref[...] = jnp.zeros_like(dk_ref)
        dv_ref[...] = jnp.zeros_like(dv_ref)

        def step(it, carry):
            U = BAND_BLOCKS_PER_STEP
            geo = [_band_geometry(it * U + u, T, seg_shift) for u in range(U)]
            q2s = [jnp.concatenate(_split_pair(q_ref[pl.ds(q0, BAND_Q), :]), axis=0) for q0, _, _ in geo]
            do2s = [jnp.concatenate(_split_pair(do_ref[pl.ds(q0, BAND_Q), :]), axis=0) for q0, _, _ in geo]
            ss = [lax.dot_general(q2, k_ref[pl.ds(k0, BAND_KEYS), :], _NT, preferred_element_type=F32) for q2, (_, k0, _) in zip(q2s, geo)]
            dps = [lax.dot_general(do2, v_ref[pl.ds(k0, BAND_KEYS), :], _NT, preferred_element_type=F32) for do2, (_, k0, _) in zip(do2s, geo)]
            ps = [_band_softmax(s, jnp.concatenate([valid, valid], axis=0))[0] for s, (_, _, valid) in zip(ss, geo)]
            dss = []
            for p, dp, (q0, _, _) in zip(ps, dps, geo):
                dl = dlse_ref[pl.ds(q0, BAND_Q), :]
                dl2 = jnp.concatenate([dl[:, :1], dl[:, HEAD_DIM:HEAD_DIM + 1]], axis=0)
                dss.append(p * (dp - jnp.sum(dp * p, axis=-1, keepdims=True) + dl2))
            dvs = [lax.dot_general(p.astype(BF), do2, _TN, preferred_element_type=F32) for p, do2 in zip(ps, do2s)]
            dsbs = [ds.astype(BF) for ds in dss]
            dqs = [jnp.dot(dsb, k_ref[pl.ds(k0, BAND_KEYS), :], preferred_element_type=F32) for dsb, (_, k0, _) in zip(dsbs, geo)]
            dks = [lax.dot_general(dsb, q2, _TN, preferred_element_type=F32) for dsb, q2 in zip(dsbs, q2s)]
            for u, (q0, k0, _) in enumerate(geo):
                dq_ref[pl.ds(q0, BAND_Q), :] = _join_pair(dqs[u][:BAND_Q], dqs[u][BAND_Q:])
                dk_ref[pl.ds(k0, BAND_KEYS), :] += dks[u]
                dv_ref[pl.ds(k0, BAND_KEYS), :] += dvs[u]
            return carry

        lax.fori_loop(0, T // (BAND_Q * BAND_BLOCKS_PER_STEP), step, 0)

    pair = pl.BlockSpec((T, 128), lambda j: (0, j))
    wide = jax.ShapeDtypeStruct((T, W), F32)
    return pl.pallas_call(
        body, name=f"band_bwd_g{group}", grid=(W // 128,),
        in_specs=[pair] * 5, out_specs=[pair] * 3, out_shape=[wide] * 3,
        compiler_params=_cparams(("parallel",)),
    )(q, k, v, do, dlse)


def _head_sums(t):
    head = lax.broadcasted_iota(jnp.int32, t.shape, 1) // HEAD_DIM
    out = jnp.zeros_like(t)
    for h in range(t.shape[1] // HEAD_DIM):
        mine = head == h
        out = jnp.where(mine, jnp.sum(jnp.where(mine, t, 0.0), axis=-1, keepdims=True), out)
    return out


def _dil_merge_fwd(os, lses, tm):
    G = len(DIL_GROUPS)
    T, W = os[0].shape

    def body(*tiles):
        o, ls = tiles[:G], tiles[G:]
        m = functools.reduce(jnp.maximum, ls)
        es = [jnp.exp(l - m) for l in ls]
        tot = functools.reduce(jnp.add, es)
        ws = [e / tot for e in es]
        return (functools.reduce(jnp.add, [w * t for w, t in zip(ws, o)]), *ws)

    res = _rowwise("dil_merge_fwd", body, T, tm, [_row(t, tm) for t in (*os, *lses)], [(W, BF)] + [(W, F32)] * G)
    return res[0], res[1:]


def _dil_merge_bwd(dy, os, ws, tm):
    G = len(DIL_GROUPS)
    T, W = dy.shape

    def body(dyt, *tiles):
        o, w = tiles[:G], tiles[G:]
        dws = [_head_sums(dyt * t) for t in o]
        mean = functools.reduce(jnp.add, [a * b for a, b in zip(w, dws)])
        return (*[a * dyt for a in w], *[a * (b - mean) for a, b in zip(w, dws)])

    res = _rowwise("dil_merge_bwd", body, T, tm, [_row(t, tm) for t in (dy, *os, *ws)], [(W, BF)] * G + [(W, F32)] * G)
    return res[:G], res[G:]


def _to_segments(t, dil):
    T, w = t.shape
    return t if dil == 1 else t.reshape(T // dil, dil, w).transpose(1, 0, 2).reshape(T, w)


def _from_segments(t, dil):
    T, w = t.shape
    return t if dil == 1 else t.reshape(dil, T // dil, w).transpose(1, 0, 2).reshape(T, w)


def _rope_tables(positions):
    half = HEAD_DIM // 2
    inv_freq = ROPE_THETA ** (-jnp.arange(half, dtype=F32) / half)
    ang = positions.astype(F32)[:, None] * inv_freq
    cos, sin = jnp.cos(ang), jnp.sin(ang)
    return jnp.tile(jnp.concatenate([cos, cos], axis=1), (1, 2)), jnp.tile(jnp.concatenate([-sin, sin], axis=1), (1, 2))


def _pack_rows(t):
    return t.reshape(-1, PACK_W)


def _me():
    return lax.axis_index("x"), lax.axis_index("y"), lax.axis_index("c")


def _other_chips(x, y):
    return [(1 - x, y), (x, 1 - y), (1 - x, 1 - y)]


def _gather_weights(packed):
    R, W = packed.shape
    half = R // 2

    def body(in_ref, out_ref, send_sems, recv_sems):
        x, y, c = _me()
        sibling = (x, y, 1 - c)
        chips = _other_chips(x, y)

        def block(chip, core):
            return out_ref.at[2 * chip[0] + chip[1], pl.ds(core * half, half), :]

        def copy(k, chip, core, to, src=None):
            return pltpu.make_async_remote_copy(
                src_ref=block(chip, core) if src is None else src, dst_ref=block(chip, core),
                send_sem=send_sems.at[k], recv_sem=recv_sems.at[k], device_id=to, device_id_type=MESH)

        first = [copy(j, (x, y), c, (*chip, c), src=in_ref.at[pl.ds(c * half, half), :]) for j, chip in enumerate(chips)]
        for cp in first:
            cp.start()
        passed = [copy(3 + j, chip, c, sibling) for j, chip in enumerate(chips)]
        for j, chip in enumerate(chips):
            copy(j, chip, c, (x, y, c)).wait_recv()
            passed[j].start()
        for j, chip in enumerate(chips):
            copy(3 + j, chip, 1 - c, (x, y, c)).wait_recv()
        for cp in first + passed:
            cp.wait_send()

    others = pl.pallas_call(
        body, name="gather_weights",
        in_specs=[ANY], out_specs=ANY,
        out_shape=jax.ShapeDtypeStruct((N_CHIPS, R, W), packed.dtype),
        scratch_shapes=[pltpu.SemaphoreType.DMA((6,)), pltpu.SemaphoreType.DMA((6,))],
    )(packed)
    return lax.dynamic_update_slice(others, packed[None], (2 * lax.axis_index("x") + lax.axis_index("y"), 0, 0))


def _swap_halves(g):
    S, R, W = g.shape
    half = R // 2

    def body(g_ref, out_ref, send_sem, recv_sem):
        x, y, c = _me()
        cp = pltpu.make_async_remote_copy(
            src_ref=g_ref.at[:, pl.ds((1 - c) * half, half), :], dst_ref=out_ref,
            send_sem=send_sem, recv_sem=recv_sem, device_id=(x, y, 1 - c), device_id_type=MESH)
        cp.start()
        cp.wait()

    return pl.pallas_call(
        body, name="swap_halves", in_specs=[ANY], out_specs=ANY,
        out_shape=jax.ShapeDtypeStruct((S, half, W), g.dtype),
        scratch_shapes=[pltpu.SemaphoreType.DMA, pltpu.SemaphoreType.DMA],
    )(g)


def _pair_sum(g, got, tm):
    S, R, W = g.shape
    half = R // 2
    nb = half // tm

    def body(c_ref, g_ref, got_ref, o_ref, ob_ref):
        tot = g_ref[...] + got_ref[...]
        o_ref[...] = tot
        ob_ref[...] = tot.astype(ob_ref.dtype)

    tile = pl.BlockSpec((None, tm, W), lambda s, i, c_ref: (s, i, 0))
    return pl.pallas_call(
        body, name="pair_sum",
        grid_spec=pltpu.PrefetchScalarGridSpec(
            num_scalar_prefetch=1, grid=(S, nb),
            in_specs=[pl.BlockSpec((None, tm, W), lambda s, i, c_ref: (s, c_ref[0] * nb + i, 0)), tile],
            out_specs=[tile, tile]),
        out_shape=[jax.ShapeDtypeStruct((S, half, W), F32), jax.ShapeDtypeStruct((S, half, W), BF)],
        compiler_params=_cparams(("parallel", "parallel")),
    )(lax.axis_index("c").reshape(1).astype(jnp.int32), g, got)


def _scatter_chips(part):
    S, h, W = part.shape

    def body(p_ref, out_ref, send_sems, recv_sems):
        x, y, c = _me()
        chips = _other_chips(x, y)
        sends = [pltpu.make_async_remote_copy(
            src_ref=p_ref.at[2 * chip[0] + chip[1]], dst_ref=out_ref.at[j],
            send_sem=send_sems.at[j], recv_sem=recv_sems.at[j], device_id=(*chip, c), device_id_type=MESH)
            for j, chip in enumerate(chips)]
        for cp in sends:
            cp.start()
        for cp in sends:
            cp.wait()

    return pl.pallas_call(
        body, name="scatter_chips", in_specs=[ANY], out_specs=ANY,
        out_shape=jax.ShapeDtypeStruct((S - 1, h, W), part.dtype),
        scratch_shapes=[pltpu.SemaphoreType.DMA((3,)), pltpu.SemaphoreType.DMA((3,))],
    )(part)


def _chip_sum(own, others, tm):
    n, h, W = others.shape

    def body(own_ref, p_ref, o_ref):
        o_ref[...] = ((own_ref[...] + p_ref[0].astype(F32)) + p_ref[1].astype(F32)) + p_ref[2].astype(F32)

    return pl.pallas_call(
        body, name="chip_sum", grid=(h // tm,),
        in_specs=[pl.BlockSpec((tm, W), lambda i: (i, 0)), pl.BlockSpec((n, tm, W), lambda i: (0, i, 0))],
        out_specs=pl.BlockSpec((tm, W), lambda i: (i, 0)),
        out_shape=jax.ShapeDtypeStruct((h, W), F32),
        compiler_params=_cparams(("parallel",)),
    )(own, others)


def _join_halves(mine):
    h, W = mine.shape

    def body(m_ref, out_ref, send_sem, recv_sem):
        x, y, c = _me()
        cp = pltpu.make_async_remote_copy(
            src_ref=m_ref, dst_ref=out_ref.at[pl.ds(c * h, h), :],
            send_sem=send_sem, recv_sem=recv_sem, device_id=(x, y, 1 - c), device_id_type=MESH)
        cp.start()
        pltpu.make_async_remote_copy(
            src_ref=m_ref, dst_ref=out_ref.at[pl.ds((1 - c) * h, h), :],
            send_sem=send_sem, recv_sem=recv_sem, device_id=(x, y, 1 - c), device_id_type=MESH).wait_recv()
        cp.wait_send()

    other = pl.pallas_call(
        body, name="join_halves", in_specs=[ANY], out_specs=ANY,
        out_shape=jax.ShapeDtypeStruct((2 * h, W), mine.dtype),
        scratch_shapes=[pltpu.SemaphoreType.DMA, pltpu.SemaphoreType.DMA],
    )(mine)
    return lax.dynamic_update_slice(other, mine, (lax.axis_index("c") * h, 0))


def _allreduce_small(s):
    R, W = s.shape

    def body(s_ref, o_ref, buf, send_sems, recv_sems):
        x, y, c = _me()
        me = 4 * x + 2 * y + c
        buf[me] = s_ref[...]
        peers = [((x + fx) % 2, (y + fy) % 2, (c + fc) % 2) for fx in range(2) for fy in range(2) for fc in range(2)][1:]
        sends = [pltpu.make_async_remote_copy(
            src_ref=s_ref, dst_ref=buf.at[me], send_sem=send_sems.at[k], recv_sem=recv_sems.at[k],
            device_id=peer, device_id_type=MESH) for k, peer in enumerate(peers)]
        for cp in sends:
            cp.start()
        for k, peer in enumerate(peers):
            pltpu.make_async_remote_copy(
                src_ref=s_ref, dst_ref=buf.at[4 * peer[0] + 2 * peer[1] + peer[2]], send_sem=send_sems.at[k],
                recv_sem=recv_sems.at[k], device_id=peer, device_id_type=MESH).wait_recv()
        for cp in sends:
            cp.wait_send()
        total = buf[0]
        for d in range(1, N_DEV):
            total = total + buf[d]
        o_ref[...] = total

    return pl.pallas_call(
        body, name="allreduce_small",
        in_specs=[pl.BlockSpec(memory_space=pltpu.VMEM)], out_specs=pl.BlockSpec(memory_space=pltpu.VMEM),
        out_shape=jax.ShapeDtypeStruct((R, W), F32),
        scratch_shapes=[pltpu.VMEM((N_DEV, R, W), F32), pltpu.SemaphoreType.DMA((N_DEV - 1,)), pltpu.SemaphoreType.DMA((N_DEV - 1,))],
    )(s)


def _adamw(name, g, w, m, v):
    R, C = w.shape
    tm = R
    for cand in (256, 128, 64, 32, 16, 8):
        if R % cand == 0:
            tm = cand
            break

    def body(g, w, m, v):
        m = ADAM_B1 * m + (1.0 - ADAM_B1) * g
        v = ADAM_B2 * v + (1.0 - ADAM_B2) * jnp.square(g)
        m_hat = m / (1.0 - ADAM_B1 ** ADAM_STEP)
        v_hat = v / (1.0 - ADAM_B2 ** ADAM_STEP)
        delta = -ADAM_LR * (m_hat / (jnp.sqrt(v_hat) + ADAM_EPS) + ADAM_WD * w)
        return delta, m, v

    return _rowwise(name, body, R, tm, [_row(t, tm) for t in (g, w, m, v)], [(C, F32)] * 3)


def _unpack_weights(gathered):
    S = gathered.shape[0]
    shard_shapes = {"w_in": (D_MODEL, (QKV_WIDTH + 2 * D_MODEL) // S), "w_branch_na": (NA_WIDTH, D_MODEL // S),
                    "w_branch_dil": (DIL_OUT_WIDTH, D_MODEL // S), "w_out": (D_MODEL // S, D_MODEL),
                    "w_up": (D_MODEL, D_FF // S), "w_down": (D_FF // S, D_MODEL),
                    "w_ple_gate": (D_MODEL // S, D_MODEL), "w_ple_proj": (PLE_DIM, D_MODEL // S)}
    col_sharded = {"w_in", "w_branch_na", "w_branch_dil", "w_up", "w_ple_proj"}
    out, r0 = {}, 0
    for name in BIG:
        rows, cols = shard_shapes[name]
        n = rows * cols // PACK_W
        t = gathered[:, r0:r0 + n, :].reshape(S, rows, cols)
        r0 += n
        out[name] = t.transpose(1, 0, 2).reshape(rows, S * cols) if name in col_sharded else t.reshape(S * rows, cols)
    return out


def _pack_grads(grads):
    col_sharded = {"w_in", "w_branch_na", "w_branch_dil", "w_up", "w_ple_proj"}
    per_chip = []
    for s in range(N_CHIPS):
        rows = []
        for name in BIG:
            g = grads[name]
            if name in col_sharded:
                w = g.shape[1] // N_CHIPS
                rows.append(_pack_rows(g[:, s * w:(s + 1) * w]))
            else:
                h = g.shape[0] // N_CHIPS
                rows.append(_pack_rows(g[s * h:(s + 1) * h]))
        per_chip.append(jnp.concatenate(rows, axis=0))
    return jnp.stack(per_chip)


def _unpack_shard(packed, shapes):
    out, r0 = {}, 0
    for name in BIG:
        rows, cols = shapes[name]
        n = rows * cols // PACK_W
        out[name] = packed[r0:r0 + n].reshape(rows, cols)
        r0 += n
    return out


def kernel(x, p, positions, g_mix, w_in, rpb, w_branch_na, w_branch_dil, w_out, g_mlp, w_up, w_down, g_ple, w_ple_gate, w_ple_proj, g_final, loss_target, m_g_mix, m_w_in, m_rpb, m_w_branch_na, m_w_branch_dil, m_w_out, m_g_mlp, m_w_up, m_w_down, m_g_ple, m_w_ple_gate, m_w_ple_proj, m_g_final, v_g_mix, v_w_in, v_rpb, v_w_branch_na, v_w_branch_dil, v_w_out, v_g_mlp, v_w_up, v_w_down, v_g_ple, v_w_ple_gate, v_w_ple_proj, v_g_final):
    shards = {"w_in": w_in[0], "w_branch_na": w_branch_na[0], "w_branch_dil": w_branch_dil[0], "w_out": w_out[0],
              "w_up": w_up[0], "w_down": w_down[0], "w_ple_gate": w_ple_gate[0], "w_ple_proj": w_ple_proj[0]}
    m_shards = {"w_in": m_w_in[0], "w_branch_na": m_w_branch_na[0], "w_branch_dil": m_w_branch_dil[0], "w_out": m_w_out[0],
                "w_up": m_w_up[0], "w_down": m_w_down[0], "w_ple_gate": m_w_ple_gate[0], "w_ple_proj": m_w_ple_proj[0]}
    v_shards = {"w_in": v_w_in[0], "w_branch_na": v_w_branch_na[0], "w_branch_dil": v_w_branch_dil[0], "w_out": v_w_out[0],
                "w_up": v_w_up[0], "w_down": v_w_down[0], "w_ple_gate": v_w_ple_gate[0], "w_ple_proj": v_w_ple_proj[0]}

    packed_w = jnp.concatenate([_pack_rows(shards[n].astype(BF)) for n in BIG], axis=0)
    W = _unpack_weights(_gather_weights(packed_w))
    w_qkv, w_gates = W["w_in"][:, :QKV_WIDTH], W["w_in"][:, QKV_WIDTH:]

    xs, ps, tgt = x[0], p[0, 0], loss_target[0]
    T = xs.shape[0]
    TM = 256
    gm, gl, gp, gf = g_mix, g_mlp, g_ple, g_final.reshape(1, D_MODEL)
    cos2, sin_signed = _rope_tables(positions[0])
    tab = _na_bias_table(rpb[0])

    a = _rowwise("norm_mix", lambda h, g: h * _rms(h) * g, T, TM, [_row(xs, TM), _full(gm)], [(D_MODEL, BF)])
    z_qkv = _mm("in_qkv", a, w_qkv, "nn", 512, 1280, 1024, [F32])
    z_gates = _mm("in_gates", a, w_gates, "nn", 512, 1024, 1024, [F32])

    def prep(z, cs, sn):
        n3 = 3 * NA_WIDTH
        return jnp.concatenate([
            z[:, :NA_WIDTH] * Q_SCALE, z[:, NA_WIDTH:n3],
            _rope_cols(z[:, n3:n3 + DIL_WIDTH], cs, sn) * Q_SCALE,
            _rope_cols(z[:, n3 + DIL_WIDTH:n3 + 2 * DIL_WIDTH], cs, sn),
            z[:, n3 + 2 * DIL_WIDTH:]], axis=1)

    qkv = _rowwise("qkv_prep", prep, T, TM, [_row(z_qkv, TM), _row(cos2, TM), _row(sin_signed, TM)], [(QKV_WIDTH, BF)])
    n3 = 3 * NA_WIDTH
    gw = DIL_OUT_WIDTH

    def group_cols(t, part, g, dil):
        return _to_segments(t[:, n3 + part * DIL_WIDTH + g * gw:n3 + part * DIL_WIDTH + (g + 1) * gw], dil)

    qkv_seg = [[group_cols(qkv, part, g, dil) for part in range(3)] for g, (_, dil) in enumerate(DIL_GROUPS)]
    y_na = _na_fwd(qkv, tab)
    band = [_band_fwd(*qkv_seg[g], g) for g in range(len(DIL_GROUPS))]
    o_nat = [_from_segments(b[0], dil) for b, (_, dil) in zip(band, DIL_GROUPS)]
    lse_nat = [_from_segments(b[1], dil) for b, (_, dil) in zip(band, DIL_GROUPS)]
    y_dil, w_grp = _dil_merge_fwd(o_nat, lse_nat, TM)

    u_na = _mm("branch_na", y_na, W["w_branch_na"], "nn", 512, 1024, 512, [F32])
    u_dil = _mm("branch_dil", y_dil, W["w_branch_dil"], "nn", 512, 1024, 256, [F32])
    mixed = _rowwise(
        "gate_mix", lambda gn, gd, un, ud: _sigmoid(gn) * un + _sigmoid(gd) * ud, T, TM,
        [_row(z_gates, TM, 0, D_MODEL), _row(z_gates, TM, 1, D_MODEL), _row(u_na, TM), _row(u_dil, TM)], [(D_MODEL, BF)])
    mix_out = _mm("out_proj", mixed, W["w_out"], "nn", 512, 1024, 1024, [F32])

    def add_norm(h, d, g):
        h = h + d
        return h, h * _rms(h) * g

    h1, cn = _rowwise("add_norm_mlp", add_norm, T, TM, [_row(xs, TM), _row(mix_out, TM), _full(gl)], [(D_MODEL, F32), (D_MODEL, BF)])
    up, act = _mm("mlp_up", cn, W["w_up"], "nn", 512, 1024, 1024, [F32, BF],
                  epilogue=lambda acc: (acc, jnp.square(jnp.maximum(acc, 0.0))))
    mlp_out = _mm("mlp_down", act, W["w_down"], "nn", 512, 1024, 1024, [F32])
    h2, en = _rowwise("add_norm_ple", add_norm, T, TM, [_row(h1, TM), _row(mlp_out, TM), _full(gp)], [(D_MODEL, F32), (D_MODEL, BF)])
    gt = _mm("ple_gate", en, W["w_ple_gate"], "nn", 512, 1024, 1024, [F32])
    pp = _mm("ple_proj", ps, W["w_ple_proj"], "nn", 512, 1024, 256, [F32])

    def head(h2t, gtt, ppt, tg, g):
        sg = _sigmoid(gtt)
        h3 = h2t + sg * ppt
        yo = h3 * _rms(h3) * g
        diff = yo - tg
        loss = 0.5 * jnp.sum(jnp.mean(jnp.square(diff), axis=-1, keepdims=True), axis=0, keepdims=True)
        dh3, dg = _rms_bwd(diff * (1.0 / D_MODEL), h3, g)
        return dh3, dh3 * ppt * sg * (1.0 - sg), dh3 * sg, jnp.broadcast_to(loss, (1, 128)), dg

    dh3, d_gt, d_pp, loss_part, dg_final = _rowwise(
        "loss_head", head, T, TM, [_row(h2, TM), _row(gt, TM), _row(pp, TM), _row(tgt, TM), _full(gf)],
        [(D_MODEL, F32), (D_MODEL, BF), (D_MODEL, BF)], sums=[128, D_MODEL])

    grads = {}
    grads["w_ple_proj"] = _mm("g_ple_proj", ps, d_pp, "tn", 256, 1024, 512, [F32])
    grads["w_ple_gate"] = _mm("g_ple_gate", en, d_gt, "tn", 512, 1024, 512, [F32])
    d_en = _mm("d_ple_gate", d_gt, W["w_ple_gate"], "nt", 512, 1024, 1024, [F32])

    def add_norm_bwd(dh_out, dn, h, g):
        dh, dg = _rms_bwd(dn, h, g)
        dh = dh_out + dh
        return dh, dh, dg

    dh2, dh2_b, dg_ple = _rowwise("add_norm_ple_bwd", add_norm_bwd, T, TM, [_row(dh3, TM), _row(d_en, TM), _row(h2, TM), _full(gp)],
                                  [(D_MODEL, F32), (D_MODEL, BF)], sums=[D_MODEL])
    d_up = _mm("d_mlp_down", dh2_b, W["w_down"], "nt", 512, 1024, 1024, [BF],
               epilogue=lambda acc, u: (acc * (2.0 * jnp.maximum(u, 0.0)),), extras=(up,))
    grads["w_down"] = _mm("g_mlp_down", act, dh2_b, "tn", 512, 1024, 512, [F32])
    grads["w_up"] = _mm("g_mlp_up", cn, d_up, "tn", 512, 1024, 512, [F32])
    d_cn = _mm("d_mlp_up", d_up, W["w_up"], "nt", 512, 1024, 1024, [F32])
    dh1, dh1_b, dg_mlp = _rowwise("add_norm_mlp_bwd", add_norm_bwd, T, TM, [_row(dh2, TM), _row(d_cn, TM), _row(h1, TM), _full(gl)],
                                  [(D_MODEL, F32), (D_MODEL, BF)], sums=[D_MODEL])
    d_mixed = _mm("d_out_proj", dh1_b, W["w_out"], "nt", 512, 1024, 1024, [F32])
    grads["w_out"] = _mm("g_out_proj", mixed, dh1_b, "tn", 512, 1024, 512, [F32])

    def gate_bwd(dm, gn, gd, un, ud):
        sn, sd = _sigmoid(gn), _sigmoid(gd)
        return jnp.concatenate([dm * un * sn * (1.0 - sn), dm * ud * sd * (1.0 - sd)], axis=1), dm * sn, dm * sd

    dz_gates, d_u_na, d_u_dil = _rowwise(
        "gate_mix_bwd", gate_bwd, T, TM,
        [_row(d_mixed, TM), _row(z_gates, TM, 0, D_MODEL), _row(z_gates, TM, 1, D_MODEL), _row(u_na, TM), _row(u_dil, TM)],
        [(2 * D_MODEL, BF), (D_MODEL, BF), (D_MODEL, BF)])
    grads["w_branch_na"] = _mm("g_branch_na", y_na, d_u_na, "tn", 512, 1024, 512, [F32])
    grads["w_branch_dil"] = _mm("g_branch_dil", y_dil, d_u_dil, "tn", 256, 1024, 512, [F32])
    d_y_na = _mm("d_branch_na", d_u_na, W["w_branch_na"], "nt", 512, 512, 1024, [BF])
    d_y_dil = _mm("d_branch_dil", d_u_dil, W["w_branch_dil"], "nt", 512, 256, 1024, [F32])

    dqa, dka, dva, dtab = _na_bwd(qkv, tab, d_y_na)
    d_rpb = _na_rpb_grad(dtab)[:, :2 * NA_WIN_ROWS - 1, :2 * NA_WIN_COLS - 1]

    do_nat, dlse_nat = _dil_merge_bwd(d_y_dil, o_nat, w_grp, TM)
    d_dil = []
    for g, (_, dil) in enumerate(DIL_GROUPS):
        res = _band_bwd(*qkv_seg[g], _to_segments(do_nat[g], dil), _to_segments(dlse_nat[g], dil), g)
        d_dil.append([_from_segments(t, dil) for t in res])
    d_dil = [d_dil[g][part] for part in range(3) for g in range(len(DIL_GROUPS))]

    def unprep(dq, dk, dv, *rest):
        dd, (cs, sn) = rest[:-2], rest[-2:]
        G = len(DIL_GROUPS)
        return jnp.concatenate([
            dq * Q_SCALE, dk, dv,
            _rope_cols(jnp.concatenate(dd[:G], axis=1), cs, -sn) * Q_SCALE,
            _rope_cols(jnp.concatenate(dd[G:2 * G], axis=1), cs, -sn),
            *dd[2 * G:]], axis=1)

    dz_qkv = _rowwise("qkv_unprep", unprep, T, TM, [_row(t, TM) for t in (dqa, dka, dva, *d_dil, cos2, sin_signed)],
                      [(QKV_WIDTH, BF)])
    grads["w_in"] = jnp.concatenate([
        _mm("g_in_qkv", a, dz_qkv, "tn", 512, 1280, 512, [F32]),
        _mm("g_in_gates", a, dz_gates, "tn", 512, 1024, 512, [F32])], axis=1)
    d_a = _mm("d_in_qkv", dz_qkv, w_qkv, "nt", 512, 1024, 1280, [F32])
    d_a = _mm("d_in_gates", dz_gates, w_gates, "nt", 512, 1024, 1024, [F32], epilogue=lambda acc, e: (acc + e,), extras=(d_a,))

    def first_bwd(dh_out, dn, h, g):
        dh, dg = _rms_bwd(dn, h, g)
        return dh_out + dh, dg

    grad_x, dg_mix = _rowwise("norm_mix_bwd", first_bwd, T, TM, [_row(dh1, TM), _row(d_a, TM), _row(xs, TM), _full(gm)],
                              [(D_MODEL, F32)], sums=[D_MODEL])

    packed_g = _pack_grads(grads)
    got = _swap_halves(packed_g)
    pair, pair_b = _pair_sum(packed_g, got, 536)
    own = lax.dynamic_index_in_dim(pair, 2 * lax.axis_index("x") + lax.axis_index("y"), 0, keepdims=False)
    mine = _chip_sum(own, _scatter_chips(pair_b), 536)
    g_shard = _unpack_shard(_join_halves(mine), {n: shards[n].shape for n in BIG})

    n_rpb = rpb.size
    rpb_rows = 4
    small = jnp.concatenate([
        dg_mix, dg_mlp, dg_ple, dg_final,
        jnp.pad(d_rpb.reshape(-1), (0, rpb_rows * D_MODEL - n_rpb)).reshape(rpb_rows, D_MODEL),
        jnp.pad(loss_part, ((0, 0), (0, D_MODEL - loss_part.shape[1]))),
        jnp.zeros((SMALL_ROWS - 5 - rpb_rows, D_MODEL), F32)], axis=0)
    small = _allreduce_small(small)
    loss = small[4 + rpb_rows, 0]

    def small_pack(a0, a1, a2, a3, r):
        return jnp.concatenate([a0.reshape(1, -1), a1.reshape(1, -1), a2.reshape(1, -1), a3.reshape(1, -1),
                                jnp.pad(r.reshape(-1), (0, rpb_rows * D_MODEL - n_rpb)).reshape(rpb_rows, D_MODEL)], axis=0)

    g_small = small[:4 + rpb_rows]
    small_res = _adamw("adamw_small", g_small, small_pack(g_mix, g_mlp, g_ple, g_final, rpb),
                       small_pack(m_g_mix, m_g_mlp, m_g_ple, m_g_final, m_rpb), small_pack(v_g_mix, v_g_mlp, v_g_ple, v_g_final, v_rpb))

    def small_unpack(t):
        return {"g_mix": t[0].reshape(g_mix.shape), "g_mlp": t[1].reshape(g_mlp.shape), "g_ple": t[2].reshape(g_ple.shape),
                "g_final": t[3].reshape(g_final.shape), "rpb": t[4:].reshape(-1)[:n_rpb].reshape(rpb.shape)}

    out = {"grad": small_unpack(g_small)}
    for kind, t in zip(("delta", "new_m", "new_v"), small_res, strict=True):
        out[kind] = small_unpack(t)
    for n in BIG:
        out["grad"][n] = g_shard[n][None]
        res = _adamw("adamw_" + n, g_shard[n], shards[n], m_shards[n], v_shards[n])
        for kind, t in zip(("delta", "new_m", "new_v"), res, strict=True):
            out[kind][n] = t[None]

    order = ["g_mix", "w_in", "rpb", "w_branch_na", "w_branch_dil", "w_out", "g_mlp", "w_up", "w_down", "g_ple",
             "w_ple_gate", "w_ple_proj", "g_final"]
    return (loss, grad_x[None], *[out["grad"][n] for n in order], *[out["delta"][n] for n in order],
            *[out["new_m"][n] for n in order], *[out["new_v"][n] for n in order])
```

```python
import functools

import numpy as np
import jax
import jax.numpy as jnp
from jax import lax
from jax.experimental import pallas as pl
from jax.experimental.pallas import tpu as pltpu

BF = jnp.bfloat16
F32 = jnp.float32
MESH = pl.DeviceIdType.MESH
ANY = pl.BlockSpec(memory_space=pl.ANY)

V7X_VMEM_BYTES = 64 * 1024 * 1024
VMEM_LIMIT = V7X_VMEM_BYTES - 16 * 1024 * 1024

D_MODEL = 1024
HEAD_DIM = 64
GRID_W = 64
NA_HEADS = 8
NA_WIN_ROWS = 8
NA_WIN_COLS = 16
NA_WIDTH = NA_HEADS * HEAD_DIM
DIL_GROUPS = ((128, 1), (512, 4), (2048, 16))
DIL_HPG = 4
DIL_HEADS = DIL_HPG * len(DIL_GROUPS)
DIL_WIDTH = DIL_HEADS * HEAD_DIM
DIL_OUT_WIDTH = DIL_HPG * HEAD_DIM
DIL_RADIUS = 64
QKV_WIDTH = 3 * NA_WIDTH + 3 * DIL_WIDTH
D_FF = 4 * D_MODEL
PLE_DIM = 256
ROPE_THETA = 10000.0
RMS_EPS = 1e-6
NEG_INF = -1e30
Q_SCALE = HEAD_DIM ** -0.5

ADAM_LR = 0.001
ADAM_B1 = 0.9
ADAM_B2 = 0.999
ADAM_EPS = 1e-08
ADAM_WD = 0.01
ADAM_STEP = 10

N_CHIPS = 4
N_DEV = 8
PACK_W = 1024
BIG = ("w_in", "w_branch_na", "w_branch_dil", "w_out", "w_up", "w_down", "w_ple_gate", "w_ple_proj")
SMALL_ROWS = 16


def _cparams(sem=None):
    return pltpu.CompilerParams(dimension_semantics=sem, vmem_limit_bytes=VMEM_LIMIT)


def _mm(name, a, b, mode, tm, tn, tk, out_dtypes, epilogue=None, extras=()):
    if mode == "nn":
        (M, K), N = a.shape, b.shape[1]
    elif mode == "nt":
        (M, K), N = a.shape, b.shape[0]
    else:
        (K, M), N = a.shape, b.shape[1]
    tm, tn, tk = min(tm, M), min(tn, N), min(tk, K)
    assert M % tm == 0 and N % tn == 0 and K % tk == 0, (name, M, N, K, tm, tn, tk)
    if mode == "nn":
        a_spec = pl.BlockSpec((tm, tk), lambda i, j, k: (i, k))
        b_spec = pl.BlockSpec((tk, tn), lambda i, j, k: (k, j))
        dims = (((1,), (0,)), ((), ()))
    elif mode == "nt":
        a_spec = pl.BlockSpec((tm, tk), lambda i, j, k: (i, k))
        b_spec = pl.BlockSpec((tn, tk), lambda i, j, k: (j, k))
        dims = (((1,), (1,)), ((), ()))
    else:
        a_spec = pl.BlockSpec((tk, tm), lambda i, j, k: (k, i))
        b_spec = pl.BlockSpec((tk, tn), lambda i, j, k: (k, j))
        dims = (((0,), (0,)), ((), ()))
    nk = K // tk
    n_extra, n_out = len(extras), len(out_dtypes)
    tile = pl.BlockSpec((tm, tn), lambda i, j, k: (i, j))

    def body(a_ref, b_ref, *rest):
        extra_refs, out_refs, acc = rest[:n_extra], rest[n_extra:n_extra + n_out], rest[-1]
        k = pl.program_id(2)

        @pl.when(k == 0)
        def _():
            acc[...] = jnp.zeros_like(acc)

        acc[...] += lax.dot_general(a_ref[...].astype(BF), b_ref[...].astype(BF), dims, preferred_element_type=F32)

        @pl.when(k == nk - 1)
        def _():
            outs = (acc[...],) if epilogue is None else epilogue(acc[...], *[e[...] for e in extra_refs])
            for o_ref, val in zip(out_refs, outs, strict=True):
                o_ref[...] = val.astype(o_ref.dtype)

    outs = pl.pallas_call(
        body, name=name, grid=(M // tm, N // tn, nk),
        in_specs=[a_spec, b_spec] + [tile] * n_extra,
        out_specs=[tile] * n_out,
        out_shape=[jax.ShapeDtypeStruct((M, N), dt) for dt in out_dtypes],
        scratch_shapes=[pltpu.VMEM((tm, tn), F32)],
        compiler_params=_cparams(("parallel", "parallel", "arbitrary")),
    )(a, b, *extras)
    return outs[0] if n_out == 1 else outs


def _row(arr, tm, col_block=None, width=None):
    width = arr.shape[1] if width is None else width
    cb = 0 if col_block is None else col_block
    return arr, pl.BlockSpec((tm, width), lambda i: (i, cb))


def _full(arr):
    nd = arr.ndim
    return arr, pl.BlockSpec(arr.shape, lambda i: (0,) * nd)


def _rowwise(name, body, T, tm, ins, outs, sums=()):
    n_in, n_out, n_sum = len(ins), len(outs), len(sums)

    def kern(*refs):
        in_refs, out_refs, sum_refs = refs[:n_in], refs[n_in:n_in + n_out], refs[n_in + n_out:]
        res = body(*[r[...] for r in in_refs])
        res = res if isinstance(res, tuple) else (res,)
        for o_ref, val in zip(out_refs, res[:n_out], strict=True):
            o_ref[...] = val.astype(o_ref.dtype)
        if n_sum:
            @pl.when(pl.program_id(0) == 0)
            def _():
                for s_ref in sum_refs:
                    s_ref[...] = jnp.zeros_like(s_ref)

            for s_ref, val in zip(sum_refs, res[n_out:], strict=True):
                s_ref[...] += val

    res = pl.pallas_call(
        kern, name=name, grid=(T // tm,),
        in_specs=[spec for _, spec in ins],
        out_specs=[pl.BlockSpec((tm, c), lambda i: (i, 0)) for c, _ in outs]
        + [pl.BlockSpec((1, c), lambda i: (0, 0)) for c in sums],
        out_shape=[jax.ShapeDtypeStruct((T, c), dt) for c, dt in outs]
        + [jax.ShapeDtypeStruct((1, c), F32) for c in sums],
        compiler_params=_cparams(("arbitrary",)),
    )(*[a for a, _ in ins])
    return res[0] if len(res) == 1 else res


def _sigmoid(x):
    return 1.0 / (1.0 + jnp.exp(-x))


def _rms(h):
    return lax.rsqrt(jnp.mean(h * h, axis=-1, keepdims=True) + RMS_EPS)


def _rms_bwd(dy, h, g):
    r = _rms(h)
    n = h * r
    dn = dy * g
    dh = r * (dn - n * jnp.mean(dn * n, axis=-1, keepdims=True))
    return dh, jnp.sum(dy * n, axis=0, keepdims=True)


def _rope(x, cos2, sin_signed):
    lane = lax.broadcasted_iota(jnp.int32, x.shape, 1)
    swapped = jnp.where((lane % HEAD_DIM) < HEAD_DIM // 2, pltpu.roll(x, 128 - HEAD_DIM // 2, 1), pltpu.roll(x, HEAD_DIM // 2, 1))
    return x * cos2 + swapped * sin_signed


def _rope_cols(x, cos2, sin_signed):
    return jnp.concatenate([_rope(x[:, c:c + 128], cos2, sin_signed) for c in range(0, x.shape[1], 128)], axis=1)


NA_KEYS = NA_WIN_ROWS * GRID_W
NA_BASES = 8


def _na_row_geometry(r, rows):
    first = jnp.clip(r - NA_WIN_ROWS // 2, 0, rows - NA_WIN_ROWS)
    base = first - r + (NA_WIN_ROWS - 1)
    return pl.multiple_of(first * GRID_W, GRID_W), base


NA_ROWS_PER_STEP = 8
NA_BWD_ROWS_PER_STEP = 4


def _softmax_rows(s):
    p = jnp.exp(s - jnp.max(s, axis=-1, keepdims=True))
    return p / jnp.sum(p, axis=-1, keepdims=True)


def _na_probs(q, kw, bias):
    return _softmax_rows(lax.dot_general(q, kw, (((1,), (1,)), ((), ())), preferred_element_type=F32) + bias)


def _split_pair(t):
    first = lax.broadcasted_iota(jnp.int32, t.shape, 1) < HEAD_DIM
    zero = jnp.zeros_like(t)
    return jnp.where(first, t, zero), jnp.where(first, zero, t)


def _join_pair(a, b):
    return jnp.where(lax.broadcasted_iota(jnp.int32, a.shape, 1) < HEAD_DIM, a, b)


_NT = (((1,), (1,)), ((), ()))
_TN = (((0,), (0,)), ((), ()))


def _na_fwd(qkv, tab):
    T = qkv.shape[0]
    rows = T // GRID_W
    n_pairs = NA_WIDTH // 128

    def body(q_ref, k_ref, v_ref, tab_ref, y_ref):
        def step(it, carry):
            geo = [_na_row_geometry(it * NA_ROWS_PER_STEP + u, rows) for u in range(NA_ROWS_PER_STEP)]
            q0s = [pl.multiple_of((it * NA_ROWS_PER_STEP + u) * GRID_W, GRID_W) for u in range(NA_ROWS_PER_STEP)]
            ss = [lax.dot_general(jnp.concatenate(_split_pair(q_ref[pl.ds(q0, GRID_W), :]), axis=0),
                                  k_ref[pl.ds(k0, NA_KEYS), :], _NT, preferred_element_type=F32)
                  for q0, (k0, _) in zip(q0s, geo)]
            ps = [_softmax_rows(s + jnp.concatenate([tab_ref[0, base], tab_ref[1, base]], axis=0)) for s, (_, base) in zip(ss, geo)]
            ys = [jnp.dot(p.astype(BF), v_ref[pl.ds(k0, NA_KEYS), :], preferred_element_type=F32) for p, (k0, _) in zip(ps, geo)]
            for q0, y2 in zip(q0s, ys):
                y_ref[pl.ds(q0, GRID_W), :] = _join_pair(y2[:GRID_W], y2[GRID_W:]).astype(y_ref.dtype)
            return carry

        lax.fori_loop(0, rows // NA_ROWS_PER_STEP, step, 0)

    def cols(first):
        return pl.BlockSpec((T, 128), lambda j: (0, first + j))

    return pl.pallas_call(
        body, name="na_fwd", grid=(n_pairs,),
        in_specs=[cols(0), cols(n_pairs), cols(2 * n_pairs), pl.BlockSpec((2, NA_BASES, GRID_W, NA_KEYS), lambda j: (j, 0, 0, 0))],
        out_specs=cols(0), out_shape=jax.ShapeDtypeStruct((T, NA_WIDTH), BF),
        compiler_params=_cparams(("parallel",)),
    )(qkv, qkv, qkv, tab)


def _na_bwd(qkv, tab, do):
    T = qkv.shape[0]
    rows = T // GRID_W
    n_pairs = NA_WIDTH // 128

    def body(q_ref, k_ref, v_ref, tab_ref, do_ref, dq_ref, dk_ref, dv_ref, dtab_ref):
        dk_ref[...] = jnp.zeros_like(dk_ref)
        dv_ref[...] = jnp.zeros_like(dv_ref)
        dtab_ref[...] = jnp.zeros_like(dtab_ref)

        def step(it, carry):
            U = NA_BWD_ROWS_PER_STEP
            geo = [_na_row_geometry(it * U + u, rows) for u in range(U)]
            q0s = [pl.multiple_of((it * U + u) * GRID_W, GRID_W) for u in range(U)]
            q2s = [jnp.concatenate(_split_pair(q_ref[pl.ds(q0, GRID_W), :]), axis=0) for q0 in q0s]
            do2s = [jnp.concatenate(_split_pair(do_ref[pl.ds(q0, GRID_W), :]), axis=0) for q0 in q0s]
            ss = [lax.dot_general(q2, k_ref[pl.ds(k0, NA_KEYS), :], _NT, preferred_element_type=F32) for q2, (k0, _) in zip(q2s, geo)]
            dps = [lax.dot_general(do2, v_ref[pl.ds(k0, NA_KEYS), :], _NT, preferred_element_type=F32) for do2, (k0, _) in zip(do2s, geo)]
            ps = [_softmax_rows(s + jnp.concatenate([tab_ref[0, base], tab_ref[1, base]], axis=0)) for s, (_, base) in zip(ss, geo)]
            dss = [p * (dp - jnp.sum(dp * p, axis=-1, keepdims=True)) for p, dp in zip(ps, dps)]
            dvs = [lax.dot_general(p.astype(BF), do2, _TN, preferred_element_type=F32) for p, do2 in zip(ps, do2s)]
            dsbs = [ds.astype(BF) for ds in dss]
            dqs = [jnp.dot(dsb, k_ref[pl.ds(k0, NA_KEYS), :], preferred_element_type=F32) for dsb, (k0, _) in zip(dsbs, geo)]
            dks = [lax.dot_general(dsb, q2, _TN, preferred_element_type=F32) for dsb, q2 in zip(dsbs, q2s)]
            for u in range(U):
                k0, base = geo[u]
                dtab_ref[0, base] += dss[u][:GRID_W]
                dtab_ref[1, base] += dss[u][GRID_W:]
                dq_ref[pl.ds(q0s[u], GRID_W), :] = _join_pair(dqs[u][:GRID_W], dqs[u][GRID_W:])
                dk_ref[pl.ds(k0, NA_KEYS), :] += dks[u]
                dv_ref[pl.ds(k0, NA_KEYS), :] += dvs[u]
            return carry

        lax.fori_loop(0, rows // NA_BWD_ROWS_PER_STEP, step, 0)

    def cols(first):
        return pl.BlockSpec((T, 128), lambda j: (0, first + j))

    tabs = pl.BlockSpec((2, NA_BASES, GRID_W, NA_KEYS), lambda j: (j, 0, 0, 0))
    wide = jax.ShapeDtypeStruct((T, NA_WIDTH), F32)
    return pl.pallas_call(
        body, name="na_bwd", grid=(n_pairs,),
        in_specs=[cols(0), cols(n_pairs), cols(2 * n_pairs), tabs, cols(0)],
        out_specs=[cols(0), cols(0), cols(0), tabs],
        out_shape=[wide, wide, wide, jax.ShapeDtypeStruct((NA_HEADS, NA_BASES, GRID_W, NA_KEYS), F32)],
        compiler_params=_cparams(("parallel",)),
    )(qkv, qkv, qkv, tab, do)


def _na_geometry_np():
    c = np.arange(GRID_W)
    cs = np.clip(c - NA_WIN_COLS // 2, 0, GRID_W - NA_WIN_COLS)
    valid = (c[None, :] >= cs[:, None]) & (c[None, :] < cs[:, None] + NA_WIN_COLS)
    off = c[None, :] - c[:, None] + (NA_WIN_COLS - 1)
    oh_col = np.zeros((GRID_W, GRID_W, 2 * NA_WIN_COLS - 1), np.float32)
    qq, kk = np.nonzero(valid)
    oh_col[qq, kk, off[qq, kk]] = 1.0
    oh_row = np.zeros((NA_BASES, NA_WIN_ROWS, 2 * NA_WIN_ROWS - 1), np.float32)
    for base in range(NA_BASES):
        for i in range(NA_WIN_ROWS):
            oh_row[base, i, base + i] = 1.0
    return valid, oh_col, oh_row


def _na_bias_table(rpb):
    valid, oh_col, oh_row = _na_geometry_np()
    hi = lax.Precision.HIGHEST
    t1 = jnp.einsum("hrc,pir->hpic", rpb, oh_row, precision=hi)
    tab = jnp.einsum("hpic,qkc->hpqik", t1, oh_col, precision=hi)
    tab = jnp.where(valid[None, None, :, None, :], tab, NEG_INF)
    return tab.reshape(rpb.shape[0], NA_BASES, GRID_W, NA_KEYS)


def _na_rpb_grad(dtab):
    H = dtab.shape[0]
    n_rows = 2 * NA_WIN_ROWS - 1
    n_cols = 2 * NA_WIN_COLS - 1

    def body(d_ref, o_ref):
        lane = lax.broadcasted_iota(jnp.int32, (GRID_W, 128), 1)
        low = lane < GRID_W
        out_rows = []
        for ro in range(n_rows):
            acc = jnp.zeros((GRID_W, 128), F32)
            for base in range(NA_BASES):
                i = ro - base
                if not 0 <= i < NA_WIN_ROWS:
                    continue
                pair = d_ref[base, :, pl.ds((i // 2) * 128, 128)]
                if i % 2:
                    pair = pltpu.roll(pair, GRID_W, 1)
                acc = acc + jnp.where(low, pair, 0.0)
            skew = pltpu.roll(acc, 0, 1, stride=1, stride_axis=0)
            diag = jnp.sum(skew, axis=0, keepdims=True)
            out_rows.append(pltpu.roll(jnp.broadcast_to(diag, (8, 128)), 128 - (GRID_W - NA_WIN_COLS), 1)[:1])
        out_rows.append(jnp.zeros((1, 128), F32))
        res = jnp.concatenate(out_rows, axis=0)
        o_ref[...] = jnp.where(lax.broadcasted_iota(jnp.int32, res.shape, 1) < n_cols, res, 0.0)

    return pl.pallas_call(
        body, name="na_rpb_grad", grid=(H,),
        in_specs=[pl.BlockSpec((None, NA_BASES, GRID_W, NA_KEYS), lambda h: (h, 0, 0, 0))],
        out_specs=pl.BlockSpec((None, n_rows + 1, 128), lambda h: (h, 0, 0)),
        out_shape=jax.ShapeDtypeStruct((H, n_rows + 1, 128), F32),
        compiler_params=_cparams(("parallel",)),
    )(jnp.flip(dtab, axis=2))


BAND_Q = 128
BAND_KEYS = BAND_Q + 2 * DIL_RADIUS


def _band_geometry(n, T, seg_shift):
    q0 = pl.multiple_of(n * BAND_Q, BAND_Q)
    k0 = pl.multiple_of(jnp.clip(q0 - DIL_RADIUS, 0, T - BAND_KEYS), DIL_RADIUS)
    qi = q0 + lax.broadcasted_iota(jnp.int32, (BAND_Q, BAND_KEYS), 0)
    kj = k0 + lax.broadcasted_iota(jnp.int32, (BAND_Q, BAND_KEYS), 1)
    valid = ((qi >> seg_shift) == (kj >> seg_shift)) & (jnp.abs(qi - kj) <= DIL_RADIUS)
    return q0, k0, valid


BAND_BLOCKS_PER_STEP = 4


def _band_softmax(s, valid):
    s = jnp.where(valid, s, NEG_INF)
    m = jnp.max(s, axis=-1, keepdims=True)
    p = jnp.exp(s - m)
    l = jnp.sum(p, axis=-1, keepdims=True)
    return p / l, m + jnp.log(l)


def _band_fwd(q, k, v, group):
    T, W = q.shape
    seg_shift = (T // DIL_GROUPS[group][1]).bit_length() - 1

    def body(q_ref, k_ref, v_ref, o_ref, lse_ref):
        def step(it, carry):
            U = BAND_BLOCKS_PER_STEP
            geo = [_band_geometry(it * U + u, T, seg_shift) for u in range(U)]
            ss = [lax.dot_general(jnp.concatenate(_split_pair(q_ref[pl.ds(q0, BAND_Q), :]), axis=0),
                                  k_ref[pl.ds(k0, BAND_KEYS), :], _NT, preferred_element_type=F32) for q0, k0, _ in geo]
            pls = [_band_softmax(s, jnp.concatenate([valid, valid], axis=0)) for s, (_, _, valid) in zip(ss, geo)]
            os = [jnp.dot(p.astype(BF), v_ref[pl.ds(k0, BAND_KEYS), :], preferred_element_type=F32) for (p, _), (_, k0, _) in zip(pls, geo)]
            for (q0, _, _), o2, (_, lse) in zip(geo, os, pls):
                o_ref[pl.ds(q0, BAND_Q), :] = _join_pair(o2[:BAND_Q], o2[BAND_Q:])
                lse2 = jnp.broadcast_to(lse, (2 * BAND_Q, 128))
                lse_ref[pl.ds(q0, BAND_Q), :] = _join_pair(lse2[:BAND_Q], lse2[BAND_Q:])
            return carry

        lax.fori_loop(0, T // (BAND_Q * BAND_BLOCKS_PER_STEP), step, 0)

    pair = pl.BlockSpec((T, 128), lambda j: (0, j))
    wide = jax.ShapeDtypeStruct((T, W), F32)
    return pl.pallas_call(
        body, name=f"band_fwd_g{group}", grid=(W // 128,),
        in_specs=[pair, pair, pair], out_specs=[pair, pair], out_shape=[wide, wide],
        compiler_params=_cparams(("parallel",)),
    )(q, k, v)


def _band_bwd(q, k, v, do, dlse, group):
    T, W = q.shape
    seg_shift = (T // DIL_GROUPS[group][1]).bit_length() - 1

    def body(q_ref, k_ref, v_ref, do_ref, dlse_ref, dq_ref, dk_ref, dv_ref):
        dk_ref[...] = jnp.zeros_like(dk_ref)
        dv_ref[...] = jnp.zeros_like(dv_ref)

        def step(it, carry):
            U = BAND_BLOCKS_PER_STEP
            geo = [_band_geometry(it * U + u, T, seg_shift) for u in range(U)]
            q2s = [jnp.concatenate(_split_pair(q_ref[pl.ds(q0, BAND_Q), :]), axis=0) for q0, _, _ in geo]
            do2s = [jnp.concatenate(_split_pair(do_ref[pl.ds(q0, BAND_Q), :]), axis=0) for q0, _, _ in geo]
            ss = [lax.dot_general(q2, k_ref[pl.ds(k0, BAND_KEYS), :], _NT, preferred_element_type=F32) for q2, (_, k0, _) in zip(q2s, geo)]
            dps = [lax.dot_general(do2, v_ref[pl.ds(k0, BAND_KEYS), :], _NT, preferred_element_type=F32) for do2, (_, k0, _) in zip(do2s, geo)]
            ps = [_band_softmax(s, jnp.concatenate([valid, valid], axis=0))[0] for s, (_, _, valid) in zip(ss, geo)]
            dss = []
            for p, dp, (q0, _, _) in zip(ps, dps, geo):
                dl = dlse_ref[pl.ds(q0, BAND_Q), :]
                dl2 = jnp.concatenate([dl[:, :1], dl[:, HEAD_DIM:HEAD_DIM + 1]], axis=0)
                dss.append(p * (dp - jnp.sum(dp * p, axis=-1, keepdims=True) + dl2))
            dvs = [lax.dot_general(p.astype(BF), do2, _TN, preferred_element_type=F32) for p, do2 in zip(ps, do2s)]
            dsbs = [ds.astype(BF) for ds in dss]
            dqs = [jnp.dot(dsb, k_ref[pl.ds(k0, BAND_KEYS), :], preferred_element_type=F32) for dsb, (_, k0, _) in zip(dsbs, geo)]
            dks = [lax.dot_general(dsb, q2, _TN, preferred_element_type=F32) for dsb, q2 in zip(dsbs, q2s)]
            for u, (q0, k0, _) in enumerate(geo):
                dq_ref[pl.ds(q0, BAND_Q), :] = _join_pair(dqs[u][:BAND_Q], dqs[u][BAND_Q:])
                dk_ref[pl.ds(k0, BAND_KEYS), :] += dks[u]
                dv_ref[pl.ds(k0, BAND_KEYS), :] += dvs[u]
            return carry

        lax.fori_loop(0, T // (BAND_Q * BAND_BLOCKS_PER_STEP), step, 0)

    pair = pl.BlockSpec((T, 128), lambda j: (0, j))
    wide = jax.ShapeDtypeStruct((T, W), F32)
    return pl.pallas_call(
        body, name=f"band_bwd_g{group}", grid=(W // 128,),
        in_specs=[pair] * 5, out_specs=[pair] * 3, out_shape=[wide] * 3,
        compiler_params=_cparams(("parallel",)),
    )(q, k, v, do, dlse)


def _head_sums(t):
    head = lax.broadcasted_iota(jnp.int32, t.shape, 1) // HEAD_DIM
    out = jnp.zeros_like(t)
    for h in range(t.shape[1] // HEAD_DIM):
        mine = head == h
        out = jnp.where(mine, jnp.sum(jnp.where(mine, t, 0.0), axis=-1, keepdims=True), out)
    return out


def _dil_merge_fwd(os, lses, tm):
    G = len(DIL_GROUPS)
    T, W = os[0].shape

    def body(*tiles):
        o, ls = tiles[:G], tiles[G:]
        m = functools.reduce(jnp.maximum, ls)
        es = [jnp.exp(l - m) for l in ls]
        tot = functools.reduce(jnp.add, es)
        ws = [e / tot for e in es]
        return (functools.reduce(jnp.add, [w * t for w, t in zip(ws, o)]), *ws)

    res = _rowwise("dil_merge_fwd", body, T, tm, [_row(t, tm) for t in (*os, *lses)], [(W, BF)] + [(W, F32)] * G)
    return res[0], res[1:]


def _dil_merge_bwd(dy, os, ws, tm):
    G = len(DIL_GROUPS)
    T, W = dy.shape

    def body(dyt, *tiles):
        o, w = tiles[:G], tiles[G:]
        dws = [_head_sums(dyt * t) for t in o]
        mean = functools.reduce(jnp.add, [a * b for a, b in zip(w, dws)])
        return (*[a * dyt for a in w], *[a * (b - mean) for a, b in zip(w, dws)])

    res = _rowwise("dil_merge_bwd", body, T, tm, [_row(t, tm) for t in (dy, *os, *ws)], [(W, BF)] * G + [(W, F32)] * G)
    return res[:G], res[G:]


def _to_segments(t, dil):
    T, w = t.shape
    return t if dil == 1 else t.reshape(T // dil, dil, w).transpose(1, 0, 2).reshape(T, w)


def _from_segments(t, dil):
    T, w = t.shape
    return t if dil == 1 else t.reshape(dil, T // dil, w).transpose(1, 0, 2).reshape(T, w)


def _rope_tables(positions):
    half = HEAD_DIM // 2
    inv_freq = ROPE_THETA ** (-jnp.arange(half, dtype=F32) / half)
    ang = positions.astype(F32)[:, None] * inv_freq
    cos, sin = jnp.cos(ang), jnp.sin(ang)
    return jnp.tile(jnp.concatenate([cos, cos], axis=1), (1, 2)), jnp.tile(jnp.concatenate([-sin, sin], axis=1), (1, 2))


def _pack_rows(t):
    return t.reshape(-1, PACK_W)


def _me():
    return lax.axis_index("x"), lax.axis_index("y"), lax.axis_index("c")


def _other_chips(x, y):
    return [(1 - x, y), (x, 1 - y), (1 - x, 1 - y)]


def _gather_weights(packed):
    R, W = packed.shape
    half = R // 2

    def body(in_ref, out_ref, send_sems, recv_sems):
        x, y, c = _me()
        sibling = (x, y, 1 - c)
        chips = _other_chips(x, y)

        def block(chip, core):
            return out_ref.at[2 * chip[0] + chip[1], pl.ds(core * half, half), :]

        def copy(k, chip, core, to, src=None):
            return pltpu.make_async_remote_copy(
                src_ref=block(chip, core) if src is None else src, dst_ref=block(chip, core),
                send_sem=send_sems.at[k], recv_sem=recv_sems.at[k], device_id=to, device_id_type=MESH)

        first = [copy(j, (x, y), c, (*chip, c), src=in_ref.at[pl.ds(c * half, half), :]) for j, chip in enumerate(chips)]
        for cp in first:
            cp.start()
        passed = [copy(3 + j, chip, c, sibling) for j, chip in enumerate(chips)]
        for j, chip in enumerate(chips):
            copy(j, chip, c, (x, y, c)).wait_recv()
            passed[j].start()
        for j, chip in enumerate(chips):
            copy(3 + j, chip, 1 - c, (x, y, c)).wait_recv()
        for cp in first + passed:
            cp.wait_send()

    others = pl.pallas_call(
        body, name="gather_weights",
        in_specs=[ANY], out_specs=ANY,
        out_shape=jax.ShapeDtypeStruct((N_CHIPS, R, W), packed.dtype),
        scratch_shapes=[pltpu.SemaphoreType.DMA((6,)), pltpu.SemaphoreType.DMA((6,))],
    )(packed)
    return lax.dynamic_update_slice(others, packed[None], (2 * lax.axis_index("x") + lax.axis_index("y"), 0, 0))


def _swap_halves(g):
    S, R, W = g.shape
    half = R // 2

    def body(g_ref, out_ref, send_sem, recv_sem):
        x, y, c = _me()
        cp = pltpu.make_async_remote_copy(
            src_ref=g_ref.at[:, pl.ds((1 - c) * half, half), :], dst_ref=out_ref,
            send_sem=send_sem, recv_sem=recv_sem, device_id=(x, y, 1 - c), device_id_type=MESH)
        cp.start()
        cp.wait()

    return pl.pallas_call(
        body, name="swap_halves", in_specs=[ANY], out_specs=ANY,
        out_shape=jax.ShapeDtypeStruct((S, half, W), g.dtype),
        scratch_shapes=[pltpu.SemaphoreType.DMA, pltpu.SemaphoreType.DMA],
    )(g)


def _pair_sum(g, got, tm):
    S, R, W = g.shape
    half = R // 2
    nb = half // tm

    def body(c_ref, g_ref, got_ref, o_ref, ob_ref):
        tot = g_ref[...] + got_ref[...]
        o_ref[...] = tot
        ob_ref[...] = tot.astype(ob_ref.dtype)

    tile = pl.BlockSpec((None, tm, W), lambda s, i, c_ref: (s, i, 0))
    return pl.pallas_call(
        body, name="pair_sum",
        grid_spec=pltpu.PrefetchScalarGridSpec(
            num_scalar_prefetch=1, grid=(S, nb),
            in_specs=[pl.BlockSpec((None, tm, W), lambda s, i, c_ref: (s, c_ref[0] * nb + i, 0)), tile],
            out_specs=[tile, tile]),
        out_shape=[jax.ShapeDtypeStruct((S, half, W), F32), jax.ShapeDtypeStruct((S, half, W), BF)],
        compiler_params=_cparams(("parallel", "parallel")),
    )(lax.axis_index("c").reshape(1).astype(jnp.int32), g, got)


def _scatter_chips(part):
    S, h, W = part.shape

    def body(p_ref, out_ref, send_sems, recv_sems):
        x, y, c = _me()
        chips = _other_chips(x, y)
        sends = [pltpu.make_async_remote_copy(
            src_ref=p_ref.at[2 * chip[0] + chip[1]], dst_ref=out_ref.at[j],
            send_sem=send_sems.at[j], recv_sem=recv_sems.at[j], device_id=(*chip, c), device_id_type=MESH)
            for j, chip in enumerate(chips)]
        for cp in sends:
            cp.start()
        for cp in sends:
            cp.wait()

    return pl.pallas_call(
        body, name="scatter_chips", in_specs=[ANY], out_specs=ANY,
        out_shape=jax.ShapeDtypeStruct((S - 1, h, W), part.dtype),
        scratch_shapes=[pltpu.SemaphoreType.DMA((3,)), pltpu.SemaphoreType.DMA((3,))],
    )(part)


def _chip_sum(own, others, tm):
    n, h, W = others.shape

    def body(own_ref, p_ref, o_ref):
        o_ref[...] = ((own_ref[...] + p_ref[0].astype(F32)) + p_ref[1].astype(F32)) + p_ref[2].astype(F32)

    return pl.pallas_call(
        body, name="chip_sum", grid=(h // tm,),
        in_specs=[pl.BlockSpec((tm, W), lambda i: (i, 0)), pl.BlockSpec((n, tm, W), lambda i: (0, i, 0))],
        out_specs=pl.BlockSpec((tm, W), lambda i: (i, 0)),
        out_shape=jax.ShapeDtypeStruct((h, W), F32),
        compiler_params=_cparams(("parallel",)),
    )(own, others)


def _join_halves(mine):
    h, W = mine.shape

    def body(m_ref, out_ref, send_sem, recv_sem):
        x, y, c = _me()
        cp = pltpu.make_async_remote_copy(
            src_ref=m_ref, dst_ref=out_ref.at[pl.ds(c * h, h), :],
            send_sem=send_sem, recv_sem=recv_sem, device_id=(x, y, 1 - c), device_id_type=MESH)
        cp.start()
        pltpu.make_async_remote_copy(
            src_ref=m_ref, dst_ref=out_ref.at[pl.ds((1 - c) * h, h), :],
            send_sem=send_sem, recv_sem=recv_sem, device_id=(x, y, 1 - c), device_id_type=MESH).wait_recv()
        cp.wait_send()

    other = pl.pallas_call(
        body, name="join_halves", in_specs=[ANY], out_specs=ANY,
        out_shape=jax.ShapeDtypeStruct((2 * h, W), mine.dtype),
        scratch_shapes=[pltpu.SemaphoreType.DMA, pltpu.SemaphoreType.DMA],
    )(mine)
    return lax.dynamic_update_slice(other, mine, (lax.axis_index("c") * h, 0))


def _allreduce_small(s):
    R, W = s.shape

    def body(s_ref, o_ref, buf, send_sems, recv_sems):
        x, y, c = _me()
        me = 4 * x + 2 * y + c
        buf[me] = s_ref[...]
        peers = [((x + fx) % 2, (y + fy) % 2, (c + fc) % 2) for fx in range(2) for fy in range(2) for fc in range(2)][1:]
        sends = [pltpu.make_async_remote_copy(
            src_ref=s_ref, dst_ref=buf.at[me], send_sem=send_sems.at[k], recv_sem=recv_sems.at[k],
            device_id=peer, device_id_type=MESH) for k, peer in enumerate(peers)]
        for cp in sends:
            cp.start()
        for k, peer in enumerate(peers):
            pltpu.make_async_remote_copy(
                src_ref=s_ref, dst_ref=buf.at[4 * peer[0] + 2 * peer[1] + peer[2]], send_sem=send_sems.at[k],
                recv_sem=recv_sems.at[k], device_id=peer, device_id_type=MESH).wait_recv()
        for cp in sends:
            cp.wait_send()
        total = buf[0]
        for d in range(1, N_DEV):
            total = total + buf[d]
        o_ref[...] = total

    return pl.pallas_call(
        body, name="allreduce_small",
        in_specs=[pl.BlockSpec(memory_space=pltpu.VMEM)], out_specs=pl.BlockSpec(memory_space=pltpu.VMEM),
        out_shape=jax.ShapeDtypeStruct((R, W), F32),
        scratch_shapes=[pltpu.VMEM((N_DEV, R, W), F32), pltpu.SemaphoreType.DMA((N_DEV - 1,)), pltpu.SemaphoreType.DMA((N_DEV - 1,))],
    )(s)


def _adamw(name, g, w, m, v):
    R, C = w.shape
    tm = R
    for cand in (256, 128, 64, 32, 16, 8):
        if R % cand == 0:
            tm = cand
            break

    def body(g, w, m, v):
        m = ADAM_B1 * m + (1.0 - ADAM_B1) * g
        v = ADAM_B2 * v + (1.0 - ADAM_B2) * jnp.square(g)
        m_hat = m / (1.0 - ADAM_B1 ** ADAM_STEP)
        v_hat = v / (1.0 - ADAM_B2 ** ADAM_STEP)
        delta = -ADAM_LR * (m_hat / (jnp.sqrt(v_hat) + ADAM_EPS) + ADAM_WD * w)
        return delta, m, v

    return _rowwise(name, body, R, tm, [_row(t, tm) for t in (g, w, m, v)], [(C, F32)] * 3)


def _unpack_weights(gathered):
    S = gathered.shape[0]
    shard_shapes = {"w_in": (D_MODEL, (QKV_WIDTH + 2 * D_MODEL) // S), "w_branch_na": (NA_WIDTH, D_MODEL // S),
                    "w_branch_dil": (DIL_OUT_WIDTH, D_MODEL // S), "w_out": (D_MODEL // S, D_MODEL),
                    "w_up": (D_MODEL, D_FF // S), "w_down": (D_FF // S, D_MODEL),
                    "w_ple_gate": (D_MODEL // S, D_MODEL), "w_ple_proj": (PLE_DIM, D_MODEL // S)}
    col_sharded = {"w_in", "w_branch_na", "w_branch_dil", "w_up", "w_ple_proj"}
    out, r0 = {}, 0
    for name in BIG:
        rows, cols = shard_shapes[name]
        n = rows * cols // PACK_W
        t = gathered[:, r0:r0 + n, :].reshape(S, rows, cols)
        r0 += n
        out[name] = t.transpose(1, 0, 2).reshape(rows, S * cols) if name in col_sharded else t.reshape(S * rows, cols)
    return out


def _pack_grads(grads):
    col_sharded = {"w_in", "w_branch_na", "w_branch_dil", "w_up", "w_ple_proj"}
    per_chip = []
    for s in range(N_CHIPS):
        rows = []
        for name in BIG:
            g = grads[name]
            if name in col_sharded:
                w = g.shape[1] // N_CHIPS
                rows.append(_pack_rows(g[:, s * w:(s + 1) * w]))
            else:
                h = g.shape[0] // N_CHIPS
                rows.append(_pack_rows(g[s * h:(s + 1) * h]))
        per_chip.append(jnp.concatenate(rows, axis=0))
    return jnp.stack(per_chip)


def _unpack_shard(packed, shapes):
    out, r0 = {}, 0
    for name in BIG:
        rows, cols = shapes[name]
        n = rows * cols // PACK_W
        out[name] = packed[r0:r0 + n].reshape(rows, cols)
        r0 += n
    return out


def kernel(x, p, positions, g_mix, w_in, rpb, w_branch_na, w_branch_dil, w_out, g_mlp, w_up, w_down, g_ple, w_ple_gate, w_ple_proj, g_final, loss_target, m_g_mix, m_w_in, m_rpb, m_w_branch_na, m_w_branch_dil, m_w_out, m_g_mlp, m_w_up, m_w_down, m_g_ple, m_w_ple_gate, m_w_ple_proj, m_g_final, v_g_mix, v_w_in, v_rpb, v_w_branch_na, v_w_branch_dil, v_w_out, v_g_mlp, v_w_up, v_w_down, v_g_ple, v_w_ple_gate, v_w_ple_proj, v_g_final):
    shards = {"w_in": w_in[0], "w_branch_na": w_branch_na[0], "w_branch_dil": w_branch_dil[0], "w_out": w_out[0],
              "w_up": w_up[0], "w_down": w_down[0], "w_ple_gate": w_ple_gate[0], "w_ple_proj": w_ple_proj[0]}
    m_shards = {"w_in": m_w_in[0], "w_branch_na": m_w_branch_na[0], "w_branch_dil": m_w_branch_dil[0], "w_out": m_w_out[0],
                "w_up": m_w_up[0], "w_down": m_w_down[0], "w_ple_gate": m_w_ple_gate[0], "w_ple_proj": m_w_ple_proj[0]}
    v_shards = {"w_in": v_w_in[0], "w_branch_na": v_w_branch_na[0], "w_branch_dil": v_w_branch_dil[0], "w_out": v_w_out[0],
                "w_up": v_w_up[0], "w_down": v_w_down[0], "w_ple_gate": v_w_ple_gate[0], "w_ple_proj": v_w_ple_proj[0]}

    packed_w = jnp.concatenate([_pack_rows(shards[n].astype(BF)) for n in BIG], axis=0)
    W = _unpack_weights(_gather_weights(packed_w))
    w_qkv, w_gates = W["w_in"][:, :QKV_WIDTH], W["w_in"][:, QKV_WIDTH:]

    xs, ps, tgt = x[0], p[0, 0], loss_target[0]
    T = xs.shape[0]
    TM = 256
    gm, gl, gp, gf = g_mix, g_mlp, g_ple, g_final.reshape(1, D_MODEL)
    cos2, sin_signed = _rope_tables(positions[0])
    tab = _na_bias_table(rpb[0])

    a = _rowwise("norm_mix", lambda h, g: h * _rms(h) * g, T, TM, [_row(xs, TM), _full(gm)], [(D_MODEL, BF)])
    z_qkv = _mm("in_qkv", a, w_qkv, "nn", 1024,1280, 1024, [F32])
    z_gates = _mm("in_gates", a, w_gates, "nn", 1024,1024, 1024, [F32])

    def prep(z, cs, sn):
        n3 = 3 * NA_WIDTH
        return jnp.concatenate([
            z[:, :NA_WIDTH] * Q_SCALE, z[:, NA_WIDTH:n3],
            _rope_cols(z[:, n3:n3 + DIL_WIDTH], cs, sn) * Q_SCALE,
            _rope_cols(z[:, n3 + DIL_WIDTH:n3 + 2 * DIL_WIDTH], cs, sn),
            z[:, n3 + 2 * DIL_WIDTH:]], axis=1)

    qkv = _rowwise("qkv_prep", prep, T, TM, [_row(z_qkv, TM), _row(cos2, TM), _row(sin_signed, TM)], [(QKV_WIDTH, BF)])
    n3 = 3 * NA_WIDTH
    gw = DIL_OUT_WIDTH

    def group_cols(t, part, g, dil):
        return _to_segments(t[:, n3 + part * DIL_WIDTH + g * gw:n3 + part * DIL_WIDTH + (g + 1) * gw], dil)

    qkv_seg = [[group_cols(qkv, part, g, dil) for part in range(3)] for g, (_, dil) in enumerate(DIL_GROUPS)]
    y_na = _na_fwd(qkv, tab)
    band = [_band_fwd(*qkv_seg[g], g) for g in range(len(DIL_GROUPS))]
    o_nat = [_from_segments(b[0], dil) for b, (_, dil) in zip(band, DIL_GROUPS)]
    lse_nat = [_from_segments(b[1], dil) for b, (_, dil) in zip(band, DIL_GROUPS)]
    y_dil, w_grp = _dil_merge_fwd(o_nat, lse_nat, TM)

    u_na = _mm("branch_na", y_na, W["w_branch_na"], "nn", 1024,1024, 512, [F32])
    u_dil = _mm("branch_dil", y_dil, W["w_branch_dil"], "nn", 1024,1024, 256, [F32])
    mixed = _rowwise(
        "gate_mix", lambda gn, gd, un, ud: _sigmoid(gn) * un + _sigmoid(gd) * ud, T, TM,
        [_row(z_gates, TM, 0, D_MODEL), _row(z_gates, TM, 1, D_MODEL), _row(u_na, TM), _row(u_dil, TM)], [(D_MODEL, BF)])
    mix_out = _mm("out_proj", mixed, W["w_out"], "nn", 1024,1024, 1024, [F32])

    def add_norm(h, d, g):
        h = h + d
        return h, h * _rms(h) * g

    h1, cn = _rowwise("add_norm_mlp", add_norm, T, TM, [_row(xs, TM), _row(mix_out, TM), _full(gl)], [(D_MODEL, F32), (D_MODEL, BF)])
    up, act = _mm("mlp_up", cn, W["w_up"], "nn", 1024,1024, 1024, [F32, BF],
                  epilogue=lambda acc: (acc, jnp.square(jnp.maximum(acc, 0.0))))
    mlp_out = _mm("mlp_down", act, W["w_down"], "nn", 1024,1024, 1024, [F32])
    h2, en = _rowwise("add_norm_ple", add_norm, T, TM, [_row(h1, TM), _row(mlp_out, TM), _full(gp)], [(D_MODEL, F32), (D_MODEL, BF)])
    gt = _mm("ple_gate", en, W["w_ple_gate"], "nn", 1024,1024, 1024, [F32])
    pp = _mm("ple_proj", ps, W["w_ple_proj"], "nn", 1024,1024, 256, [F32])

    def head(h2t, gtt, ppt, tg, g):
        sg = _sigmoid(gtt)
        h3 = h2t + sg * ppt
        yo = h3 * _rms(h3) * g
        diff = yo - tg
        loss = 0.5 * jnp.sum(jnp.mean(jnp.square(diff), axis=-1, keepdims=True), axis=0, keepdims=True)
        dh3, dg = _rms_bwd(diff * (1.0 / D_MODEL), h3, g)
        return dh3, dh3 * ppt * sg * (1.0 - sg), dh3 * sg, jnp.broadcast_to(loss, (1, 128)), dg

    dh3, d_gt, d_pp, loss_part, dg_final = _rowwise(
        "loss_head", head, T, TM, [_row(h2, TM), _row(gt, TM), _row(pp, TM), _row(tgt, TM), _full(gf)],
        [(D_MODEL, F32), (D_MODEL, BF), (D_MODEL, BF)], sums=[128, D_MODEL])

    grads = {}
    grads["w_ple_proj"] = _mm("g_ple_proj", ps, d_pp, "tn", 256, 1024, 1024,[F32])
    grads["w_ple_gate"] = _mm("g_ple_gate", en, d_gt, "tn", 1024, 1024, 1024,[F32])
    d_en = _mm("d_ple_gate", d_gt, W["w_ple_gate"], "nt", 1024,1024, 1024, [F32])

    def add_norm_bwd(dh_out, dn, h, g):
        dh, dg = _rms_bwd(dn, h, g)
        dh = dh_out + dh
        return dh, dh, dg

    dh2, dh2_b, dg_ple = _rowwise("add_norm_ple_bwd", add_norm_bwd, T, TM, [_row(dh3, TM), _row(d_en, TM), _row(h2, TM), _full(gp)],
                                  [(D_MODEL, F32), (D_MODEL, BF)], sums=[D_MODEL])
    d_up = _mm("d_mlp_down", dh2_b, W["w_down"], "nt", 1024,1024, 1024, [BF],
               epilogue=lambda acc, u: (acc * (2.0 * jnp.maximum(u, 0.0)),), extras=(up,))
    grads["w_down"] = _mm("g_mlp_down", act, dh2_b, "tn", 1024, 1024, 1024,[F32])
    grads["w_up"] = _mm("g_mlp_up", cn, d_up, "tn", 1024, 1024, 1024,[F32])
    d_cn = _mm("d_mlp_up", d_up, W["w_up"], "nt", 1024,1024, 1024, [F32])
    dh1, dh1_b, dg_mlp = _rowwise("add_norm_mlp_bwd", add_norm_bwd, T, TM, [_row(dh2, TM), _row(d_cn, TM), _row(h1, TM), _full(gl)],
                                  [(D_MODEL, F32), (D_MODEL, BF)], sums=[D_MODEL])
    d_mixed = _mm("d_out_proj", dh1_b, W["w_out"], "nt", 1024,1024, 1024, [F32])
    grads["w_out"] = _mm("g_out_proj", mixed, dh1_b, "tn", 1024, 1024, 1024,[F32])

    def gate_bwd(dm, gn, gd, un, ud):
        sn, sd = _sigmoid(gn), _sigmoid(gd)
        return jnp.concatenate([dm * un * sn * (1.0 - sn), dm * ud * sd * (1.0 - sd)], axis=1), dm * sn, dm * sd

    dz_gates, d_u_na, d_u_dil = _rowwise(
        "gate_mix_bwd", gate_bwd, T, TM,
        [_row(d_mixed, TM), _row(z_gates, TM, 0, D_MODEL), _row(z_gates, TM, 1, D_MODEL), _row(u_na, TM), _row(u_dil, TM)],
        [(2 * D_MODEL, BF), (D_MODEL, BF), (D_MODEL, BF)])
    grads["w_branch_na"] = _mm("g_branch_na", y_na, d_u_na, "tn", 1024, 1024, 1024,[F32])
    grads["w_branch_dil"] = _mm("g_branch_dil", y_dil, d_u_dil, "tn", 256, 1024, 1024,[F32])
    d_y_na = _mm("d_branch_na", d_u_na, W["w_branch_na"], "nt", 1024,512, 1024, [BF])
    d_y_dil = _mm("d_branch_dil", d_u_dil, W["w_branch_dil"], "nt", 1024,256, 1024, [F32])

    dqa, dka, dva, dtab = _na_bwd(qkv, tab, d_y_na)
    d_rpb = _na_rpb_grad(dtab)[:, :2 * NA_WIN_ROWS - 1, :2 * NA_WIN_COLS - 1]

    do_nat, dlse_nat = _dil_merge_bwd(d_y_dil, o_nat, w_grp, TM)
    d_dil = []
    for g, (_, dil) in enumerate(DIL_GROUPS):
        res = _band_bwd(*qkv_seg[g], _to_segments(do_nat[g], dil), _to_segments(dlse_nat[g], dil), g)
        d_dil.append([_from_segments(t, dil) for t in res])
    d_dil = [d_dil[g][part] for part in range(3) for g in range(len(DIL_GROUPS))]

    def unprep(dq, dk, dv, *rest):
        dd, (cs, sn) = rest[:-2], rest[-2:]
        G = len(DIL_GROUPS)
        return jnp.concatenate([
            dq * Q_SCALE, dk, dv,
            _rope_cols(jnp.concatenate(dd[:G], axis=1), cs, -sn) * Q_SCALE,
            _rope_cols(jnp.concatenate(dd[G:2 * G], axis=1), cs, -sn),
            *dd[2 * G:]], axis=1)

    dz_qkv = _rowwise("qkv_unprep", unprep, T, TM, [_row(t, TM) for t in (dqa, dka, dva, *d_dil, cos2, sin_signed)],
                      [(QKV_WIDTH, BF)])
    grads["w_in"] = jnp.concatenate([
        _mm("g_in_qkv", a, dz_qkv, "tn", 1024, 1280, 1024,[F32]),
        _mm("g_in_gates", a, dz_gates, "tn", 1024, 1024, 1024,[F32])], axis=1)
    d_a = _mm("d_in_qkv", dz_qkv, w_qkv, "nt", 1024,1024, 1280, [F32])
    d_a = _mm("d_in_gates", dz_gates, w_gates, "nt", 1024,1024, 1024, [F32], epilogue=lambda acc, e: (acc + e,), extras=(d_a,))

    def first_bwd(dh_out, dn, h, g):
        dh, dg = _rms_bwd(dn, h, g)
        return dh_out + dh, dg

    grad_x, dg_mix = _rowwise("norm_mix_bwd", first_bwd, T, TM, [_row(dh1, TM), _row(d_a, TM), _row(xs, TM), _full(gm)],
                              [(D_MODEL, F32)], sums=[D_MODEL])

    packed_g = _pack_grads(grads)
    got = _swap_halves(packed_g)
    pair, pair_b = _pair_sum(packed_g, got, 536)
    own = lax.dynamic_index_in_dim(pair, 2 * lax.axis_index("x") + lax.axis_index("y"), 0, keepdims=False)
    mine = _chip_sum(own, _scatter_chips(pair_b), 536)
    g_shard = _unpack_shard(_join_halves(mine), {n: shards[n].shape for n in BIG})

    n_rpb = rpb.size
    rpb_rows = 4
    small = jnp.concatenate([
        dg_mix, dg_mlp, dg_ple, dg_final,
        jnp.pad(d_rpb.reshape(-1), (0, rpb_rows * D_MODEL - n_rpb)).reshape(rpb_rows, D_MODEL),
        jnp.pad(loss_part, ((0, 0), (0, D_MODEL - loss_part.shape[1]))),
        jnp.zeros((SMALL_ROWS - 5 - rpb_rows, D_MODEL), F32)], axis=0)
    small = _allreduce_small(small)
    loss = small[4 + rpb_rows, 0]

    def small_pack(a0, a1, a2, a3, r):
        return jnp.concatenate([a0.reshape(1, -1), a1.reshape(1, -1), a2.reshape(1, -1), a3.reshape(1, -1),
                                jnp.pad(r.reshape(-1), (0, rpb_rows * D_MODEL - n_rpb)).reshape(rpb_rows, D_MODEL)], axis=0)

    g_small = small[:4 + rpb_rows]
    small_res = _adamw("adamw_small", g_small, small_pack(g_mix, g_mlp, g_ple, g_final, rpb),
                       small_pack(m_g_mix, m_g_mlp, m_g_ple, m_g_final, m_rpb), small_pack(v_g_mix, v_g_mlp, v_g_ple, v_g_final, v_rpb))

    def small_unpack(t):
        return {"g_mix": t[0].reshape(g_mix.shape), "g_mlp": t[1].reshape(g_mlp.shape), "g_ple": t[2].reshape(g_ple.shape),
                "g_final": t[3].reshape(g_final.shape), "rpb": t[4:].reshape(-1)[:n_rpb].reshape(rpb.shape)}

    out = {"grad": small_unpack(g_small)}
    for kind, t in zip(("delta", "new_m", "new_v"), small_res, strict=True):
        out[kind] = small_unpack(t)
    for n in BIG:
        out["grad"][n] = g_shard[n][None]
        res = _adamw("adamw_" + n, g_shard[n], shards[n], m_shards[n], v_shards[n])
        for kind, t in zip(("delta", "new_m", "new_v"), res, strict=True):
            out[kind][n] = t[None]

    order = ["g_mix", "w_in", "rpb", "w_branch_na", "w_branch_dil", "w_out", "g_mlp", "w_up", "w_down", "g_ple",
             "w_ple_gate", "w_ple_proj", "g_final"]
    return (loss, grad_x[None], *[out["grad"][n] for n in order], *[out["delta"][n] for n in order],
            *[out["new_m"][n] for n in order], *[out["new_v"][n] for n in order])
```

```python
import functools
from typing import NamedTuple

import numpy as np
import jax
import jax.numpy as jnp
from jax import lax
from jax.experimental import pallas as pl
from jax.experimental.pallas import tpu as pltpu

BF = jnp.bfloat16
F32 = jnp.float32
MESH = pl.DeviceIdType.MESH
ANY = pl.BlockSpec(memory_space=pl.ANY)

V7X_VMEM_BYTES = 64 * 1024 * 1024
VMEM_LIMIT = V7X_VMEM_BYTES - 16 * 1024 * 1024

D_MODEL = 1024
HEAD_DIM = 64
GRID_W = 64
NA_HEADS = 8
NA_WIN_ROWS = 8
NA_WIN_COLS = 16
NA_WIDTH = NA_HEADS * HEAD_DIM
DIL_GROUPS = ((128, 1), (512, 4), (2048, 16))
DIL_HPG = 4
DIL_HEADS = DIL_HPG * len(DIL_GROUPS)
DIL_WIDTH = DIL_HEADS * HEAD_DIM
DIL_OUT_WIDTH = DIL_HPG * HEAD_DIM
DIL_RADIUS = 64
QKV_WIDTH = 3 * NA_WIDTH + 3 * DIL_WIDTH
D_FF = 4 * D_MODEL
PLE_DIM = 256
ROPE_THETA = 10000.0
RMS_EPS = 1e-6
NEG_INF = -1e30
Q_SCALE = HEAD_DIM ** -0.5

ADAM_LR = 0.001
ADAM_B1 = 0.9
ADAM_B2 = 0.999
ADAM_EPS = 1e-08
ADAM_WD = 0.01
ADAM_STEP = 10

N_CHIPS = 4
N_DEV = 8
PACK_W = 1024
BIG = ("w_in", "w_branch_na", "w_branch_dil", "w_out", "w_up", "w_down", "w_ple_gate", "w_ple_proj")
GATHER_FIRST = ("w_in",)
GATHER_REST = BIG[1:]
REDUCE_EARLY = ("w_ple_proj", "w_ple_gate", "w_down", "w_up")
REDUCE_LATE = ("w_in", "w_branch_na", "w_branch_dil", "w_out")
SMALL_ROWS = 16


def _cparams(sem=None):
    return pltpu.CompilerParams(dimension_semantics=sem, vmem_limit_bytes=VMEM_LIMIT)


def _mm(name, a, b, mode, tm, tn, tk, out_dtypes, epilogue=None, extras=(), after=()):
    if mode == "nn":
        (M, K), N = a.shape, b.shape[1]
    elif mode == "nt":
        (M, K), N = a.shape, b.shape[0]
    else:
        (K, M), N = a.shape, b.shape[1]
    tm, tn, tk = min(tm, M), min(tn, N), min(tk, K)
    assert M % tm == 0 and N % tn == 0 and K % tk == 0, (name, M, N, K, tm, tn, tk)
    if mode == "nn":
        a_spec = pl.BlockSpec((tm, tk), lambda i, j, k: (i, k))
        b_spec = pl.BlockSpec((tk, tn), lambda i, j, k: (k, j))
        dims = (((1,), (0,)), ((), ()))
    elif mode == "nt":
        a_spec = pl.BlockSpec((tm, tk), lambda i, j, k: (i, k))
        b_spec = pl.BlockSpec((tn, tk), lambda i, j, k: (j, k))
        dims = (((1,), (1,)), ((), ()))
    else:
        a_spec = pl.BlockSpec((tk, tm), lambda i, j, k: (k, i))
        b_spec = pl.BlockSpec((tk, tn), lambda i, j, k: (k, j))
        dims = (((0,), (0,)), ((), ()))
    nk = K // tk
    n_extra, n_out = len(extras), len(out_dtypes)
    tile = pl.BlockSpec((tm, tn), lambda i, j, k: (i, j))

    n_after = len(after)

    def body(a_ref, b_ref, *rest):
        extra_refs, rest = rest[:n_extra], rest[n_extra + n_after:]
        out_refs, acc = rest[:n_out], rest[-1]
        k = pl.program_id(2)

        @pl.when(k == 0)
        def _():
            acc[...] = jnp.zeros_like(acc)

        acc[...] += lax.dot_general(a_ref[...].astype(BF), b_ref[...].astype(BF), dims, preferred_element_type=F32)

        @pl.when(k == nk - 1)
        def _():
            outs = (acc[...],) if epilogue is None else epilogue(acc[...], *[e[...] for e in extra_refs])
            for o_ref, val in zip(out_refs, outs, strict=True):
                o_ref[...] = val.astype(o_ref.dtype)

    outs = pl.pallas_call(
        body, name=name, grid=(M // tm, N // tn, nk),
        in_specs=[a_spec, b_spec] + [tile] * n_extra + [ANY] * n_after,
        out_specs=[tile] * n_out,
        out_shape=[jax.ShapeDtypeStruct((M, N), dt) for dt in out_dtypes],
        scratch_shapes=[pltpu.VMEM((tm, tn), F32)],
        compiler_params=_cparams(("parallel", "parallel", "arbitrary")),
    )(a, b, *extras, *after)
    return outs[0] if n_out == 1 else outs


def _row(arr, tm, col_block=None, width=None):
    width = arr.shape[1] if width is None else width
    cb = 0 if col_block is None else col_block
    return arr, pl.BlockSpec((tm, width), lambda i: (i, cb))


def _full(arr):
    nd = arr.ndim
    return arr, pl.BlockSpec(arr.shape, lambda i: (0,) * nd)


def _rowwise(name, body, T, tm, ins, outs, sums=(), after=()):
    n_in, n_out, n_sum, n_after = len(ins), len(outs), len(sums), len(after)

    def kern(*refs):
        in_refs, refs = refs[:n_in], refs[n_in + n_after:]
        out_refs, sum_refs = refs[:n_out], refs[n_out:]
        res = body(*[r[...] for r in in_refs])
        res = res if isinstance(res, tuple) else (res,)
        for o_ref, val in zip(out_refs, res[:n_out], strict=True):
            o_ref[...] = val.astype(o_ref.dtype)
        if n_sum:
            @pl.when(pl.program_id(0) == 0)
            def _():
                for s_ref in sum_refs:
                    s_ref[...] = jnp.zeros_like(s_ref)

            for s_ref, val in zip(sum_refs, res[n_out:], strict=True):
                s_ref[...] += val

    res = pl.pallas_call(
        kern, name=name, grid=(T // tm,),
        in_specs=[spec for _, spec in ins] + [ANY] * n_after,
        out_specs=[pl.BlockSpec((tm, c), lambda i: (i, 0)) for c, _ in outs]
        + [pl.BlockSpec((1, c), lambda i: (0, 0)) for c in sums],
        out_shape=[jax.ShapeDtypeStruct((T, c), dt) for c, dt in outs]
        + [jax.ShapeDtypeStruct((1, c), F32) for c in sums],
        compiler_params=_cparams(("arbitrary",)),
    )(*[a for a, _ in ins], *after)
    return res[0] if len(res) == 1 else res


def _sigmoid(x):
    return 1.0 / (1.0 + jnp.exp(-x))


def _rms(h):
    return lax.rsqrt(jnp.mean(h * h, axis=-1, keepdims=True) + RMS_EPS)


def _rms_bwd(dy, h, g):
    r = _rms(h)
    n = h * r
    dn = dy * g
    dh = r * (dn - n * jnp.mean(dn * n, axis=-1, keepdims=True))
    return dh, jnp.sum(dy * n, axis=0, keepdims=True)


def _rope(x, cos2, sin_signed):
    lane = lax.broadcasted_iota(jnp.int32, x.shape, 1)
    swapped = jnp.where((lane % HEAD_DIM) < HEAD_DIM // 2, pltpu.roll(x, 128 - HEAD_DIM // 2, 1), pltpu.roll(x, HEAD_DIM // 2, 1))
    return x * cos2 + swapped * sin_signed


def _rope_cols(x, cos2, sin_signed):
    return jnp.concatenate([_rope(x[:, c:c + 128], cos2, sin_signed) for c in range(0, x.shape[1], 128)], axis=1)


NA_KEYS = NA_WIN_ROWS * GRID_W
NA_BASES = 8


def _na_row_geometry(r, rows):
    first = jnp.clip(r - NA_WIN_ROWS // 2, 0, rows - NA_WIN_ROWS)
    base = first - r + (NA_WIN_ROWS - 1)
    return pl.multiple_of(first * GRID_W, GRID_W), base


NA_ROWS_PER_STEP = 8
NA_BWD_ROWS_PER_STEP = 4


def _softmax_rows(s):
    p = jnp.exp(s - jnp.max(s, axis=-1, keepdims=True))
    return p / jnp.sum(p, axis=-1, keepdims=True)


def _na_probs(q, kw, bias):
    return _softmax_rows(lax.dot_general(q, kw, (((1,), (1,)), ((), ())), preferred_element_type=F32) + bias)


def _split_pair(t):
    first = lax.broadcasted_iota(jnp.int32, t.shape, 1) < HEAD_DIM
    zero = jnp.zeros_like(t)
    return jnp.where(first, t, zero), jnp.where(first, zero, t)


def _join_pair(a, b):
    return jnp.where(lax.broadcasted_iota(jnp.int32, a.shape, 1) < HEAD_DIM, a, b)


_NT = (((1,), (1,)), ((), ()))
_TN = (((0,), (0,)), ((), ()))


def _na_fwd(qkv, tab):
    T = qkv.shape[0]
    rows = T // GRID_W
    n_pairs = NA_WIDTH // 128

    def body(q_ref, k_ref, v_ref, tab_ref, y_ref):
        def step(it, carry):
            geo = [_na_row_geometry(it * NA_ROWS_PER_STEP + u, rows) for u in range(NA_ROWS_PER_STEP)]
            q0s = [pl.multiple_of((it * NA_ROWS_PER_STEP + u) * GRID_W, GRID_W) for u in range(NA_ROWS_PER_STEP)]
            ss = [lax.dot_general(jnp.concatenate(_split_pair(q_ref[pl.ds(q0, GRID_W), :]), axis=0),
                                  k_ref[pl.ds(k0, NA_KEYS), :], _NT, preferred_element_type=F32)
                  for q0, (k0, _) in zip(q0s, geo)]
            ps = [_softmax_rows(s + jnp.concatenate([tab_ref[0, base], tab_ref[1, base]], axis=0)) for s, (_, base) in zip(ss, geo)]
            ys = [jnp.dot(p.astype(BF), v_ref[pl.ds(k0, NA_KEYS), :], preferred_element_type=F32) for p, (k0, _) in zip(ps, geo)]
            for q0, y2 in zip(q0s, ys):
                y_ref[pl.ds(q0, GRID_W), :] = _join_pair(y2[:GRID_W], y2[GRID_W:]).astype(y_ref.dtype)
            return carry

        lax.fori_loop(0, rows // NA_ROWS_PER_STEP, step, 0)

    def cols(first):
        return pl.BlockSpec((T, 128), lambda j: (0, first + j))

    return pl.pallas_call(
        body, name="na_fwd", grid=(n_pairs,),
        in_specs=[cols(0), cols(n_pairs), cols(2 * n_pairs), pl.BlockSpec((2, NA_BASES, GRID_W, NA_KEYS), lambda j: (j, 0, 0, 0))],
        out_specs=cols(0), out_shape=jax.ShapeDtypeStruct((T, NA_WIDTH), BF),
        compiler_params=_cparams(("parallel",)),
    )(qkv, qkv, qkv, tab)


def _na_bwd(qkv, tab, do):
    T = qkv.shape[0]
    rows = T // GRID_W
    n_pairs = NA_WIDTH // 128

    def body(q_ref, k_ref, v_ref, tab_ref, do_ref, dq_ref, dk_ref, dv_ref, dtab_ref):
        dk_ref[...] = jnp.zeros_like(dk_ref)
        dv_ref[...] = jnp.zeros_like(dv_ref)
        dtab_ref[...] = jnp.zeros_like(dtab_ref)

        def step(it, carry):
            U = NA_BWD_ROWS_PER_STEP
            geo = [_na_row_geometry(it * U + u, rows) for u in range(U)]
            q0s = [pl.multiple_of((it * U + u) * GRID_W, GRID_W) for u in range(U)]
            q2s = [jnp.concatenate(_split_pair(q_ref[pl.ds(q0, GRID_W), :]), axis=0) for q0 in q0s]
            do2s = [jnp.concatenate(_split_pair(do_ref[pl.ds(q0, GRID_W), :]), axis=0) for q0 in q0s]
            ss = [lax.dot_general(q2, k_ref[pl.ds(k0, NA_KEYS), :], _NT, preferred_element_type=F32) for q2, (k0, _) in zip(q2s, geo)]
            dps = [lax.dot_general(do2, v_ref[pl.ds(k0, NA_KEYS), :], _NT, preferred_element_type=F32) for do2, (k0, _) in zip(do2s, geo)]
            ps = [_softmax_rows(s + jnp.concatenate([tab_ref[0, base], tab_ref[1, base]], axis=0)) for s, (_, base) in zip(ss, geo)]
            dss = [p * (dp - jnp.sum(dp * p, axis=-1, keepdims=True)) for p, dp in zip(ps, dps)]
            dvs = [lax.dot_general(p.astype(BF), do2, _TN, preferred_element_type=F32) for p, do2 in zip(ps, do2s)]
            dsbs = [ds.astype(BF) for ds in dss]
            dqs = [jnp.dot(dsb, k_ref[pl.ds(k0, NA_KEYS), :], preferred_element_type=F32) for dsb, (k0, _) in zip(dsbs, geo)]
            dks = [lax.dot_general(dsb, q2, _TN, preferred_element_type=F32) for dsb, q2 in zip(dsbs, q2s)]
            for u in range(U):
                k0, base = geo[u]
                dtab_ref[0, base] += dss[u][:GRID_W]
                dtab_ref[1, base] += dss[u][GRID_W:]
                dq_ref[pl.ds(q0s[u], GRID_W), :] = _join_pair(dqs[u][:GRID_W], dqs[u][GRID_W:])
                dk_ref[pl.ds(k0, NA_KEYS), :] += dks[u]
                dv_ref[pl.ds(k0, NA_KEYS), :] += dvs[u]
            return carry

        lax.fori_loop(0, rows // NA_BWD_ROWS_PER_STEP, step, 0)

    def cols(first):
        return pl.BlockSpec((T, 128), lambda j: (0, first + j))

    tabs = pl.BlockSpec((2, NA_BASES, GRID_W, NA_KEYS), lambda j: (j, 0, 0, 0))
    wide = jax.ShapeDtypeStruct((T, NA_WIDTH), F32)
    return pl.pallas_call(
        body, name="na_bwd", grid=(n_pairs,),
        in_specs=[cols(0), cols(n_pairs), cols(2 * n_pairs), tabs, cols(0)],
        out_specs=[cols(0), cols(0), cols(0), tabs],
        out_shape=[wide, wide, wide, jax.ShapeDtypeStruct((NA_HEADS, NA_BASES, GRID_W, NA_KEYS), F32)],
        compiler_params=_cparams(("parallel",)),
    )(qkv, qkv, qkv, tab, do)


def _na_geometry_np():
    c = np.arange(GRID_W)
    cs = np.clip(c - NA_WIN_COLS // 2, 0, GRID_W - NA_WIN_COLS)
    valid = (c[None, :] >= cs[:, None]) & (c[None, :] < cs[:, None] + NA_WIN_COLS)
    off = c[None, :] - c[:, None] + (NA_WIN_COLS - 1)
    oh_col = np.zeros((GRID_W, GRID_W, 2 * NA_WIN_COLS - 1), np.float32)
    qq, kk = np.nonzero(valid)
    oh_col[qq, kk, off[qq, kk]] = 1.0
    oh_row = np.zeros((NA_BASES, NA_WIN_ROWS, 2 * NA_WIN_ROWS - 1), np.float32)
    for base in range(NA_BASES):
        for i in range(NA_WIN_ROWS):
            oh_row[base, i, base + i] = 1.0
    return valid, oh_col, oh_row


def _na_bias_table(rpb):
    valid, oh_col, oh_row = _na_geometry_np()
    hi = lax.Precision.HIGHEST
    t1 = jnp.einsum("hrc,pir->hpic", rpb, oh_row, precision=hi)
    tab = jnp.einsum("hpic,qkc->hpqik", t1, oh_col, precision=hi)
    tab = jnp.where(valid[None, None, :, None, :], tab, NEG_INF)
    return tab.reshape(rpb.shape[0], NA_BASES, GRID_W, NA_KEYS)


def _na_rpb_grad(dtab):
    H = dtab.shape[0]
    n_rows = 2 * NA_WIN_ROWS - 1
    n_cols = 2 * NA_WIN_COLS - 1

    def body(d_ref, o_ref):
        lane = lax.broadcasted_iota(jnp.int32, (GRID_W, 128), 1)
        low = lane < GRID_W
        out_rows = []
        for ro in range(n_rows):
            acc = jnp.zeros((GRID_W, 128), F32)
            for base in range(NA_BASES):
                i = ro - base
                if not 0 <= i < NA_WIN_ROWS:
                    continue
                pair = d_ref[base, :, pl.ds((i // 2) * 128, 128)]
                if i % 2:
                    pair = pltpu.roll(pair, GRID_W, 1)
                acc = acc + jnp.where(low, pair, 0.0)
            skew = pltpu.roll(acc, 0, 1, stride=1, stride_axis=0)
            diag = jnp.sum(skew, axis=0, keepdims=True)
            out_rows.append(pltpu.roll(jnp.broadcast_to(diag, (8, 128)), 128 - (GRID_W - NA_WIN_COLS), 1)[:1])
        out_rows.append(jnp.zeros((1, 128), F32))
        res = jnp.concatenate(out_rows, axis=0)
        o_ref[...] = jnp.where(lax.broadcasted_iota(jnp.int32, res.shape, 1) < n_cols, res, 0.0)

    return pl.pallas_call(
        body, name="na_rpb_grad", grid=(H,),
        in_specs=[pl.BlockSpec((None, NA_BASES, GRID_W, NA_KEYS), lambda h: (h, 0, 0, 0))],
        out_specs=pl.BlockSpec((None, n_rows + 1, 128), lambda h: (h, 0, 0)),
        out_shape=jax.ShapeDtypeStruct((H, n_rows + 1, 128), F32),
        compiler_params=_cparams(("parallel",)),
    )(jnp.flip(dtab, axis=2))


BAND_Q = 128
BAND_KEYS = BAND_Q + 2 * DIL_RADIUS


def _band_geometry(n, T, seg_shift):
    q0 = pl.multiple_of(n * BAND_Q, BAND_Q)
    k0 = pl.multiple_of(jnp.clip(q0 - DIL_RADIUS, 0, T - BAND_KEYS), DIL_RADIUS)
    qi = q0 + lax.broadcasted_iota(jnp.int32, (BAND_Q, BAND_KEYS), 0)
    kj = k0 + lax.broadcasted_iota(jnp.int32, (BAND_Q, BAND_KEYS), 1)
    valid = ((qi >> seg_shift) == (kj >> seg_shift)) & (jnp.abs(qi - kj) <= DIL_RADIUS)
    return q0, k0, valid


BAND_BLOCKS_PER_STEP = 4


def _band_softmax(s, valid):
    s = jnp.where(valid, s, NEG_INF)
    m = jnp.max(s, axis=-1, keepdims=True)
    p = jnp.exp(s - m)
    l = jnp.sum(p, axis=-1, keepdims=True)
    return p / l, m + jnp.log(l)


def _band_fwd(q, k, v, group):
    T, W = q.shape
    seg_shift = (T // DIL_GROUPS[group][1]).bit_length() - 1

    def body(q_ref, k_ref, v_ref, o_ref, lse_ref):
        def step(it, carry):
            U = BAND_BLOCKS_PER_STEP
            geo = [_band_geometry(it * U + u, T, seg_shift) for u in range(U)]
            ss = [lax.dot_general(jnp.concatenate(_split_pair(q_ref[pl.ds(q0, BAND_Q), :]), axis=0),
                                  k_ref[pl.ds(k0, BAND_KEYS), :], _NT, preferred_element_type=F32) for q0, k0, _ in geo]
            pls = [_band_softmax(s, jnp.concatenate([valid, valid], axis=0)) for s, (_, _, valid) in zip(ss, geo)]
            os = [jnp.dot(p.astype(BF), v_ref[pl.ds(k0, BAND_KEYS), :], preferred_element_type=F32) for (p, _), (_, k0, _) in zip(pls, geo)]
            for (q0, _, _), o2, (_, lse) in zip(geo, os, pls):
                o_ref[pl.ds(q0, BAND_Q), :] = _join_pair(o2[:BAND_Q], o2[BAND_Q:])
                lse2 = jnp.broadcast_to(lse, (2 * BAND_Q, 128))
                lse_ref[pl.ds(q0, BAND_Q), :] = _join_pair(lse2[:BAND_Q], lse2[BAND_Q:])
            return carry

        lax.fori_loop(0, T // (BAND_Q * BAND_BLOCKS_PER_STEP), step, 0)

    pair = pl.BlockSpec((T, 128), lambda j: (0, j))
    wide = jax.ShapeDtypeStruct((T, W), F32)
    return pl.pallas_call(
        body, name=f"band_fwd_g{group}", grid=(W // 128,),
        in_specs=[pair, pair, pair], out_specs=[pair, pair], out_shape=[wide, wide],
        compiler_params=_cparams(("parallel",)),
    )(q, k, v)


def _band_bwd(q, k, v, do, dlse, group):
    T, W = q.shape
    seg_shift = (T // DIL_GROUPS[group][1]).bit_length() - 1

    def body(q_ref, k_ref, v_ref, do_ref, dlse_ref, dq_ref, dk_ref, dv_ref):
        dk_ref[...] = jnp.zeros_like(dk_ref)
        dv_ref[...] = jnp.zeros_like(dv_ref)

        def step(it, carry):
            U = BAND_BLOCKS_PER_STEP
            geo = [_band_geometry(it * U + u, T, seg_shift) for u in range(U)]
            q2s = [jnp.concatenate(_split_pair(q_ref[pl.ds(q0, BAND_Q), :]), axis=0) for q0, _, _ in geo]
            do2s = [jnp.concatenate(_split_pair(do_ref[pl.ds(q0, BAND_Q), :]), axis=0) for q0, _, _ in geo]
            ss = [lax.dot_general(q2, k_ref[pl.ds(k0, BAND_KEYS), :], _NT, preferred_element_type=F32) for q2, (_, k0, _) in zip(q2s, geo)]
            dps = [lax.dot_general(do2, v_ref[pl.ds(k0, BAND_KEYS), :], _NT, preferred_element_type=F32) for do2, (_, k0, _) in zip(do2s, geo)]
            ps = [_band_softmax(s, jnp.concatenate([valid, valid], axis=0))[0] for s, (_, _, valid) in zip(ss, geo)]
            dss = []
            for p, dp, (q0, _, _) in zip(ps, dps, geo):
                dl = dlse_ref[pl.ds(q0, BAND_Q), :]
                dl2 = jnp.concatenate([dl[:, :1], dl[:, HEAD_DIM:HEAD_DIM + 1]], axis=0)
                dss.append(p * (dp - jnp.sum(dp * p, axis=-1, keepdims=True) + dl2))
            dvs = [lax.dot_general(p.astype(BF), do2, _TN, preferred_element_type=F32) for p, do2 in zip(ps, do2s)]
            dsbs = [ds.astype(BF) for ds in dss]
            dqs = [jnp.dot(dsb, k_ref[pl.ds(k0, BAND_KEYS), :], preferred_element_type=F32) for dsb, (_, k0, _) in zip(dsbs, geo)]
            dks = [lax.dot_general(dsb, q2, _TN, preferred_element_type=F32) for dsb, q2 in zip(dsbs, q2s)]
            for u, (q0, k0, _) in enumerate(geo):
                dq_ref[pl.ds(q0, BAND_Q), :] = _join_pair(dqs[u][:BAND_Q], dqs[u][BAND_Q:])
                dk_ref[pl.ds(k0, BAND_KEYS), :] += dks[u]
                dv_ref[pl.ds(k0, BAND_KEYS), :] += dvs[u]
            return carry

        lax.fori_loop(0, T // (BAND_Q * BAND_BLOCKS_PER_STEP), step, 0)

    pair = pl.BlockSpec((T, 128), lambda j: (0, j))
    wide = jax.ShapeDtypeStruct((T, W), F32)
    return pl.pallas_call(
        body, name=f"band_bwd_g{group}", grid=(W // 128,),
        in_specs=[pair] * 5, out_specs=[pair] * 3, out_shape=[wide] * 3,
        compiler_params=_cparams(("parallel",)),
    )(q, k, v, do, dlse)


def _head_sums(t):
    head = lax.broadcasted_iota(jnp.int32, t.shape, 1) // HEAD_DIM
    out = jnp.zeros_like(t)
    for h in range(t.shape[1] // HEAD_DIM):
        mine = head == h
        out = jnp.where(mine, jnp.sum(jnp.where(mine, t, 0.0), axis=-1, keepdims=True), out)
    return out


def _dil_merge_fwd(os, lses, tm):
    G = len(DIL_GROUPS)
    T, W = os[0].shape

    def body(*tiles):
        o, ls = tiles[:G], tiles[G:]
        m = functools.reduce(jnp.maximum, ls)
        es = [jnp.exp(l - m) for l in ls]
        tot = functools.reduce(jnp.add, es)
        ws = [e / tot for e in es]
        return (functools.reduce(jnp.add, [w * t for w, t in zip(ws, o)]), *ws)

    res = _rowwise("dil_merge_fwd", body, T, tm, [_row(t, tm) for t in (*os, *lses)], [(W, BF)] + [(W, F32)] * G)
    return res[0], res[1:]


def _dil_merge_bwd(dy, os, ws, tm):
    G = len(DIL_GROUPS)
    T, W = dy.shape

    def body(dyt, *tiles):
        o, w = tiles[:G], tiles[G:]
        dws = [_head_sums(dyt * t) for t in o]
        mean = functools.reduce(jnp.add, [a * b for a, b in zip(w, dws)])
        return (*[a * dyt for a in w], *[a * (b - mean) for a, b in zip(w, dws)])

    res = _rowwise("dil_merge_bwd", body, T, tm, [_row(t, tm) for t in (dy, *os, *ws)], [(W, BF)] * G + [(W, F32)] * G)
    return res[:G], res[G:]


def _to_segments(t, dil):
    T, w = t.shape
    return t if dil == 1 else t.reshape(T // dil, dil, w).transpose(1, 0, 2).reshape(T, w)


def _from_segments(t, dil):
    T, w = t.shape
    return t if dil == 1 else t.reshape(dil, T // dil, w).transpose(1, 0, 2).reshape(T, w)


def _rope_tables(positions):
    half = HEAD_DIM // 2
    inv_freq = ROPE_THETA ** (-jnp.arange(half, dtype=F32) / half)
    ang = positions.astype(F32)[:, None] * inv_freq
    cos, sin = jnp.cos(ang), jnp.sin(ang)
    return jnp.tile(jnp.concatenate([cos, cos], axis=1), (1, 2)), jnp.tile(jnp.concatenate([-sin, sin], axis=1), (1, 2))


def _pack_rows(t):
    return t.reshape(-1, PACK_W)


def _me():
    return lax.axis_index("x"), lax.axis_index("y"), lax.axis_index("c")


def _other_chips(x, y):
    return [(1 - x, y), (x, 1 - y), (1 - x, 1 - y)]


def _gather_weights(packed):
    R, W = packed.shape
    half = R // 2

    def body(in_ref, out_ref, send_sems, recv_sems):
        x, y, c = _me()
        sibling = (x, y, 1 - c)
        chips = _other_chips(x, y)

        def block(chip, core):
            return out_ref.at[2 * chip[0] + chip[1], pl.ds(core * half, half), :]

        def copy(k, chip, core, to, src=None):
            return pltpu.make_async_remote_copy(
                src_ref=block(chip, core) if src is None else src, dst_ref=block(chip, core),
                send_sem=send_sems.at[k], recv_sem=recv_sems.at[k], device_id=to, device_id_type=MESH)

        first = [copy(j, (x, y), c, (*chip, c), src=in_ref.at[pl.ds(c * half, half), :]) for j, chip in enumerate(chips)]
        for cp in first:
            cp.start()
        passed = [copy(3 + j, chip, c, sibling) for j, chip in enumerate(chips)]
        for j, chip in enumerate(chips):
            copy(j, chip, c, (x, y, c)).wait_recv()
            passed[j].start()
        for j, chip in enumerate(chips):
            copy(3 + j, chip, 1 - c, (x, y, c)).wait_recv()
        for cp in first + passed:
            cp.wait_send()

    others = pl.pallas_call(
        body, name="gather_weights",
        in_specs=[ANY], out_specs=ANY,
        out_shape=jax.ShapeDtypeStruct((N_CHIPS, R, W), packed.dtype),
        scratch_shapes=[pltpu.SemaphoreType.DMA((6,)), pltpu.SemaphoreType.DMA((6,))],
    )(packed)
    return lax.dynamic_update_slice(others, packed[None], (2 * lax.axis_index("x") + lax.axis_index("y"), 0, 0))


def _swap_halves(g):
    S, R, W = g.shape
    half = R // 2

    def body(g_ref, out_ref, send_sem, recv_sem):
        x, y, c = _me()
        cp = pltpu.make_async_remote_copy(
            src_ref=g_ref.at[:, pl.ds((1 - c) * half, half), :], dst_ref=out_ref,
            send_sem=send_sem, recv_sem=recv_sem, device_id=(x, y, 1 - c), device_id_type=MESH)
        cp.start()
        cp.wait()

    return pl.pallas_call(
        body, name="swap_halves", in_specs=[ANY], out_specs=ANY,
        out_shape=jax.ShapeDtypeStruct((S, half, W), g.dtype),
        scratch_shapes=[pltpu.SemaphoreType.DMA, pltpu.SemaphoreType.DMA],
    )(g)


def _pair_sum(g, got, tm):
    S, R, W = g.shape
    half = R // 2
    nb = half // tm

    def body(c_ref, g_ref, got_ref, o_ref, ob_ref):
        tot = g_ref[...] + got_ref[...]
        o_ref[...] = tot
        ob_ref[...] = tot.astype(ob_ref.dtype)

    tile = pl.BlockSpec((None, tm, W), lambda s, i, c_ref: (s, i, 0))
    return pl.pallas_call(
        body, name="pair_sum",
        grid_spec=pltpu.PrefetchScalarGridSpec(
            num_scalar_prefetch=1, grid=(S, nb),
            in_specs=[pl.BlockSpec((None, tm, W), lambda s, i, c_ref: (s, c_ref[0] * nb + i, 0)), tile],
            out_specs=[tile, tile]),
        out_shape=[jax.ShapeDtypeStruct((S, half, W), F32), jax.ShapeDtypeStruct((S, half, W), BF)],
        compiler_params=_cparams(("parallel", "parallel")),
    )(lax.axis_index("c").reshape(1).astype(jnp.int32), g, got)


def _scatter_chips(part):
    S, h, W = part.shape

    def body(p_ref, out_ref, send_sems, recv_sems):
        x, y, c = _me()
        chips = _other_chips(x, y)
        sends = [pltpu.make_async_remote_copy(
            src_ref=p_ref.at[2 * chip[0] + chip[1]], dst_ref=out_ref.at[j],
            send_sem=send_sems.at[j], recv_sem=recv_sems.at[j], device_id=(*chip, c), device_id_type=MESH)
            for j, chip in enumerate(chips)]
        for cp in sends:
            cp.start()
        for cp in sends:
            cp.wait()

    return pl.pallas_call(
        body, name="scatter_chips", in_specs=[ANY], out_specs=ANY,
        out_shape=jax.ShapeDtypeStruct((S - 1, h, W), part.dtype),
        scratch_shapes=[pltpu.SemaphoreType.DMA((3,)), pltpu.SemaphoreType.DMA((3,))],
    )(part)


def _chip_sum(own, others, tm):
    n, h, W = others.shape

    def body(own_ref, p_ref, o_ref):
        o_ref[...] = ((own_ref[...] + p_ref[0].astype(F32)) + p_ref[1].astype(F32)) + p_ref[2].astype(F32)

    return pl.pallas_call(
        body, name="chip_sum", grid=(h // tm,),
        in_specs=[pl.BlockSpec((tm, W), lambda i: (i, 0)), pl.BlockSpec((n, tm, W), lambda i: (0, i, 0))],
        out_specs=pl.BlockSpec((tm, W), lambda i: (i, 0)),
        out_shape=jax.ShapeDtypeStruct((h, W), F32),
        compiler_params=_cparams(("parallel",)),
    )(own, others)


def _join_halves(mine):
    h, W = mine.shape

    def body(m_ref, out_ref, send_sem, recv_sem):
        x, y, c = _me()
        cp = pltpu.make_async_remote_copy(
            src_ref=m_ref, dst_ref=out_ref.at[pl.ds(c * h, h), :],
            send_sem=send_sem, recv_sem=recv_sem, device_id=(x, y, 1 - c), device_id_type=MESH)
        cp.start()
        pltpu.make_async_remote_copy(
            src_ref=m_ref, dst_ref=out_ref.at[pl.ds((1 - c) * h, h), :],
            send_sem=send_sem, recv_sem=recv_sem, device_id=(x, y, 1 - c), device_id_type=MESH).wait_recv()
        cp.wait_send()

    other = pl.pallas_call(
        body, name="join_halves", in_specs=[ANY], out_specs=ANY,
        out_shape=jax.ShapeDtypeStruct((2 * h, W), mine.dtype),
        scratch_shapes=[pltpu.SemaphoreType.DMA, pltpu.SemaphoreType.DMA],
    )(mine)
    return lax.dynamic_update_slice(other, mine, (lax.axis_index("c") * h, 0))


def _allreduce_small(s):
    R, W = s.shape

    def body(s_ref, o_ref, buf, send_sems, recv_sems):
        x, y, c = _me()
        me = 4 * x + 2 * y + c
        buf[me] = s_ref[...]
        peers = [((x + fx) % 2, (y + fy) % 2, (c + fc) % 2) for fx in range(2) for fy in range(2) for fc in range(2)][1:]
        sends = [pltpu.make_async_remote_copy(
            src_ref=s_ref, dst_ref=buf.at[me], send_sem=send_sems.at[k], recv_sem=recv_sems.at[k],
            device_id=peer, device_id_type=MESH) for k, peer in enumerate(peers)]
        for cp in sends:
            cp.start()
        for k, peer in enumerate(peers):
            pltpu.make_async_remote_copy(
                src_ref=s_ref, dst_ref=buf.at[4 * peer[0] + 2 * peer[1] + peer[2]], send_sem=send_sems.at[k],
                recv_sem=recv_sems.at[k], device_id=peer, device_id_type=MESH).wait_recv()
        for cp in sends:
            cp.wait_send()
        total = buf[0]
        for d in range(1, N_DEV):
            total = total + buf[d]
        o_ref[...] = total

    return pl.pallas_call(
        body, name="allreduce_small",
        in_specs=[pl.BlockSpec(memory_space=pltpu.VMEM)], out_specs=pl.BlockSpec(memory_space=pltpu.VMEM),
        out_shape=jax.ShapeDtypeStruct((R, W), F32),
        scratch_shapes=[pltpu.VMEM((N_DEV, R, W), F32), pltpu.SemaphoreType.DMA((N_DEV - 1,)), pltpu.SemaphoreType.DMA((N_DEV - 1,))],
    )(s)


HBM_SPEC = pl.BlockSpec(memory_space=pltpu.HBM)
SEM_SPEC = pl.BlockSpec(memory_space=pltpu.SEMAPHORE)
DATAFLOW = pltpu.SideEffectType.DATAFLOW_SIDE_EFFECTING


class _InFlight(NamedTuple):
    sems: tuple
    src: jax.Array
    land: jax.Array
    token: jax.Array


def _split_start(name, src, land_shape, land_dtype, n, copies):
    def body(src_ref, land_ref, *rest):
        sems, token = rest[:2 * n], rest[-1]
        for k, (s, d, peer) in enumerate(copies(src_ref, land_ref)):
            pltpu.make_async_remote_copy(src_ref=s, dst_ref=d, send_sem=sems[k], recv_sem=sems[n + k],
                                         device_id=peer, device_id_type=MESH).start()
        token[...] = jnp.zeros_like(token)

    outs = pl.pallas_call(
        body, name=name,
        out_shape=(*[pltpu.SemaphoreType.DMA(())] * (2 * n), pltpu.HBM(src.shape, src.dtype), pltpu.HBM(land_shape, land_dtype),
                   jax.ShapeDtypeStruct((8, 128), F32)),
        in_specs=(HBM_SPEC, HBM_SPEC),
        out_specs=(*[SEM_SPEC] * (2 * n), HBM_SPEC, HBM_SPEC, pl.BlockSpec(memory_space=pltpu.VMEM)),
        input_output_aliases={0: 2 * n, 1: 2 * n + 1},
        compiler_params=pltpu.CompilerParams(has_side_effects=DATAFLOW),
    )(pltpu.with_memory_space_constraint(src, pltpu.HBM), pltpu.with_memory_space_constraint(lax.empty(land_shape, land_dtype), pltpu.HBM))
    return _InFlight(tuple(outs[:2 * n]), outs[2 * n], outs[2 * n + 1], outs[2 * n + 2])


def _split_wait(name, flight, after, n, copies):
    def body(src_ref, land_ref, *rest):
        sems = rest[:2 * n]
        for k, (s, d, peer) in enumerate(copies(src_ref, land_ref)):
            cp = pltpu.make_async_remote_copy(src_ref=s, dst_ref=d, send_sem=sems[k], recv_sem=sems[n + k],
                                              device_id=peer, device_id_type=MESH)
            cp.wait_send()
            cp.wait_recv()

    return pl.pallas_call(
        body, name=name,
        out_shape=(pltpu.HBM(flight.src.shape, flight.src.dtype), pltpu.HBM(flight.land.shape, flight.land.dtype)),
        in_specs=(HBM_SPEC, HBM_SPEC, *[SEM_SPEC] * (2 * n), ANY),
        out_specs=(HBM_SPEC, HBM_SPEC), input_output_aliases={0: 0, 1: 1},
        compiler_params=pltpu.CompilerParams(has_side_effects=DATAFLOW),
    )(flight.src, flight.land, *flight.sems, after)


def _gather_copies(src_ref, land_ref):
    x, y, c = _me()
    return [(src_ref, land_ref.at[2 * x + y], (*chip, c)) for chip in _other_chips(x, y)]


def _gather_start(packed):
    return _split_start("gather_rest_start", packed, (N_CHIPS, *packed.shape), packed.dtype, 3, _gather_copies)


def _gather_wait(flight, after):
    src, others = _split_wait("gather_rest_wait", flight, after, 3, _gather_copies)
    return lax.dynamic_update_slice(others, src[None], (2 * lax.axis_index("x") + lax.axis_index("y"), 0, 0))


def _swap_copies(src_ref, land_ref):
    x, y, c = _me()
    half = land_ref.shape[1]
    return [(src_ref.at[:, pl.ds((1 - c) * half, half), :], land_ref, (x, y, 1 - c))]


def _swap_start(g):
    S, R, W = g.shape
    return _split_start("swap_halves_start", g, (S, R // 2, W), g.dtype, 1, _swap_copies)


def _swap_wait(flight, after):
    return _split_wait("swap_halves_wait", flight, after, 1, _swap_copies)


def _scatter_copies(src_ref, land_ref):
    x, y, c = _me()
    return [(src_ref.at[2 * chip[0] + chip[1]], land_ref.at[j], (*chip, c)) for j, chip in enumerate(_other_chips(x, y))]


def _scatter_start(part):
    S, h, W = part.shape
    return _split_start("scatter_chips_start", part, (S - 1, h, W), part.dtype, 3, _scatter_copies)


def _scatter_wait(flight, after):
    return _split_wait("scatter_chips_wait", flight, after, 3, _scatter_copies)[1]


def _join_copies(src_ref, land_ref):
    x, y, c = _me()
    h = src_ref.shape[0]
    return [(src_ref, land_ref.at[pl.ds(c * h, h), :], (x, y, 1 - c))]


def _join_start(mine):
    h, W = mine.shape
    return _split_start("join_halves_start", mine, (2 * h, W), mine.dtype, 1, _join_copies)


def _join_wait(flight, after):
    src, other = _split_wait("join_halves_wait", flight, after, 1, _join_copies)
    return lax.dynamic_update_slice(other, src, (lax.axis_index("c") * src.shape[0], 0))


def _adamw(name, g, w, m, v):
    R, C = w.shape
    tm = R
    for cand in (256, 128, 64, 32, 16, 8):
        if R % cand == 0:
            tm = cand
            break

    def body(g, w, m, v):
        m = ADAM_B1 * m + (1.0 - ADAM_B1) * g
        v = ADAM_B2 * v + (1.0 - ADAM_B2) * jnp.square(g)
        m_hat = m / (1.0 - ADAM_B1 ** ADAM_STEP)
        v_hat = v / (1.0 - ADAM_B2 ** ADAM_STEP)
        delta = -ADAM_LR * (m_hat / (jnp.sqrt(v_hat) + ADAM_EPS) + ADAM_WD * w)
        return delta, m, v

    return _rowwise(name, body, R, tm, [_row(t, tm) for t in (g, w, m, v)], [(C, F32)] * 3)


def _unpack_weights(gathered, names):
    S = gathered.shape[0]
    shard_shapes = {"w_in": (D_MODEL, (QKV_WIDTH + 2 * D_MODEL) // S), "w_branch_na": (NA_WIDTH, D_MODEL // S),
                    "w_branch_dil": (DIL_OUT_WIDTH, D_MODEL // S), "w_out": (D_MODEL // S, D_MODEL),
                    "w_up": (D_MODEL, D_FF // S), "w_down": (D_FF // S, D_MODEL),
                    "w_ple_gate": (D_MODEL // S, D_MODEL), "w_ple_proj": (PLE_DIM, D_MODEL // S)}
    col_sharded = {"w_in", "w_branch_na", "w_branch_dil", "w_up", "w_ple_proj"}
    out, r0 = {}, 0
    for name in names:
        rows, cols = shard_shapes[name]
        n = rows * cols // PACK_W
        t = gathered[:, r0:r0 + n, :].reshape(S, rows, cols)
        r0 += n
        out[name] = t.transpose(1, 0, 2).reshape(rows, S * cols) if name in col_sharded else t.reshape(S * rows, cols)
    return out


def _pack_grads(grads, names):
    col_sharded = {"w_in", "w_branch_na", "w_branch_dil", "w_up", "w_ple_proj"}
    per_chip = []
    for s in range(N_CHIPS):
        rows = []
        for name in names:
            g = grads[name]
            if name in col_sharded:
                w = g.shape[1] // N_CHIPS
                rows.append(_pack_rows(g[:, s * w:(s + 1) * w]))
            else:
                h = g.shape[0] // N_CHIPS
                rows.append(_pack_rows(g[s * h:(s + 1) * h]))
        per_chip.append(jnp.concatenate(rows, axis=0))
    return jnp.stack(per_chip)


def _unpack_shard(packed, shapes, names):
    out, r0 = {}, 0
    for name in names:
        rows, cols = shapes[name]
        n = rows * cols // PACK_W
        out[name] = packed[r0:r0 + n].reshape(rows, cols)
        r0 += n
    return out


def kernel(x, p, positions, g_mix, w_in, rpb, w_branch_na, w_branch_dil, w_out, g_mlp, w_up, w_down, g_ple, w_ple_gate, w_ple_proj, g_final, loss_target, m_g_mix, m_w_in, m_rpb, m_w_branch_na, m_w_branch_dil, m_w_out, m_g_mlp, m_w_up, m_w_down, m_g_ple, m_w_ple_gate, m_w_ple_proj, m_g_final, v_g_mix, v_w_in, v_rpb, v_w_branch_na, v_w_branch_dil, v_w_out, v_g_mlp, v_w_up, v_w_down, v_g_ple, v_w_ple_gate, v_w_ple_proj, v_g_final):
    shards = {"w_in": w_in[0], "w_branch_na": w_branch_na[0], "w_branch_dil": w_branch_dil[0], "w_out": w_out[0],
              "w_up": w_up[0], "w_down": w_down[0], "w_ple_gate": w_ple_gate[0], "w_ple_proj": w_ple_proj[0]}
    m_shards = {"w_in": m_w_in[0], "w_branch_na": m_w_branch_na[0], "w_branch_dil": m_w_branch_dil[0], "w_out": m_w_out[0],
                "w_up": m_w_up[0], "w_down": m_w_down[0], "w_ple_gate": m_w_ple_gate[0], "w_ple_proj": m_w_ple_proj[0]}
    v_shards = {"w_in": v_w_in[0], "w_branch_na": v_w_branch_na[0], "w_branch_dil": v_w_branch_dil[0], "w_out": v_w_out[0],
                "w_up": v_w_up[0], "w_down": v_w_down[0], "w_ple_gate": v_w_ple_gate[0], "w_ple_proj": v_w_ple_proj[0]}

    W = _unpack_weights(_gather_weights(_pack_rows(shards["w_in"].astype(BF))), GATHER_FIRST)
    rest_flight = _gather_start(jnp.concatenate([_pack_rows(shards[n].astype(BF)) for n in GATHER_REST], axis=0))
    w_qkv, w_gates = W["w_in"][:, :QKV_WIDTH], W["w_in"][:, QKV_WIDTH:]

    xs, ps, tgt = x[0], p[0, 0], loss_target[0]
    T = xs.shape[0]
    TM = 256
    gm, gl, gp, gf = g_mix, g_mlp, g_ple, g_final.reshape(1, D_MODEL)
    cos2, sin_signed = _rope_tables(positions[0])
    tab = _na_bias_table(rpb[0])

    a = _rowwise("norm_mix", lambda h, g: h * _rms(h) * g, T, TM, [_row(xs, TM), _full(gm)], [(D_MODEL, BF)],
                 after=(rest_flight.token,))
    z_qkv = _mm("in_qkv", a, w_qkv, "nn", 1024,1280, 1024, [F32])
    z_gates = _mm("in_gates", a, w_gates, "nn", 1024,1024, 1024, [F32])

    def prep(z, cs, sn):
        n3 = 3 * NA_WIDTH
        return jnp.concatenate([
            z[:, :NA_WIDTH] * Q_SCALE, z[:, NA_WIDTH:n3],
            _rope_cols(z[:, n3:n3 + DIL_WIDTH], cs, sn) * Q_SCALE,
            _rope_cols(z[:, n3 + DIL_WIDTH:n3 + 2 * DIL_WIDTH], cs, sn),
            z[:, n3 + 2 * DIL_WIDTH:]], axis=1)

    qkv = _rowwise("qkv_prep", prep, T, TM, [_row(z_qkv, TM), _row(cos2, TM), _row(sin_signed, TM)], [(QKV_WIDTH, BF)])
    n3 = 3 * NA_WIDTH
    gw = DIL_OUT_WIDTH

    def group_cols(t, part, g, dil):
        return _to_segments(t[:, n3 + part * DIL_WIDTH + g * gw:n3 + part * DIL_WIDTH + (g + 1) * gw], dil)

    qkv_seg = [[group_cols(qkv, part, g, dil) for part in range(3)] for g, (_, dil) in enumerate(DIL_GROUPS)]
    y_na = _na_fwd(qkv, tab)
    band = [_band_fwd(*qkv_seg[g], g) for g in range(len(DIL_GROUPS))]
    o_nat = [_from_segments(b[0], dil) for b, (_, dil) in zip(band, DIL_GROUPS)]
    lse_nat = [_from_segments(b[1], dil) for b, (_, dil) in zip(band, DIL_GROUPS)]
    y_dil, w_grp = _dil_merge_fwd(o_nat, lse_nat, TM)

    W.update(_unpack_weights(_gather_wait(rest_flight, y_dil), GATHER_REST))
    u_na = _mm("branch_na", y_na, W["w_branch_na"], "nn", 1024,1024, 512, [F32])
    u_dil = _mm("branch_dil", y_dil, W["w_branch_dil"], "nn", 1024,1024, 256, [F32])
    mixed = _rowwise(
        "gate_mix", lambda gn, gd, un, ud: _sigmoid(gn) * un + _sigmoid(gd) * ud, T, TM,
        [_row(z_gates, TM, 0, D_MODEL), _row(z_gates, TM, 1, D_MODEL), _row(u_na, TM), _row(u_dil, TM)], [(D_MODEL, BF)])
    mix_out = _mm("out_proj", mixed, W["w_out"], "nn", 1024,1024, 1024, [F32])

    def add_norm(h, d, g):
        h = h + d
        return h, h * _rms(h) * g

    h1, cn = _rowwise("add_norm_mlp", add_norm, T, TM, [_row(xs, TM), _row(mix_out, TM), _full(gl)], [(D_MODEL, F32), (D_MODEL, BF)])
    up, act = _mm("mlp_up", cn, W["w_up"], "nn", 1024,1024, 1024, [F32, BF],
                  epilogue=lambda acc: (acc, jnp.square(jnp.maximum(acc, 0.0))))
    mlp_out = _mm("mlp_down", act, W["w_down"], "nn", 1024,1024, 1024, [F32])
    h2, en = _rowwise("add_norm_ple", add_norm, T, TM, [_row(h1, TM), _row(mlp_out, TM), _full(gp)], [(D_MODEL, F32), (D_MODEL, BF)])
    gt = _mm("ple_gate", en, W["w_ple_gate"], "nn", 1024,1024, 1024, [F32])
    pp = _mm("ple_proj", ps, W["w_ple_proj"], "nn", 1024,1024, 256, [F32])

    def head(h2t, gtt, ppt, tg, g):
        sg = _sigmoid(gtt)
        h3 = h2t + sg * ppt
        yo = h3 * _rms(h3) * g
        diff = yo - tg
        loss = 0.5 * jnp.sum(jnp.mean(jnp.square(diff), axis=-1, keepdims=True), axis=0, keepdims=True)
        dh3, dg = _rms_bwd(diff * (1.0 / D_MODEL), h3, g)
        return dh3, dh3 * ppt * sg * (1.0 - sg), dh3 * sg, jnp.broadcast_to(loss, (1, 128)), dg

    dh3, d_gt, d_pp, loss_part, dg_final = _rowwise(
        "loss_head", head, T, TM, [_row(h2, TM), _row(gt, TM), _row(pp, TM), _row(tgt, TM), _full(gf)],
        [(D_MODEL, F32), (D_MODEL, BF), (D_MODEL, BF)], sums=[128, D_MODEL])

    grads = {}
    grads["w_ple_proj"] = _mm("g_ple_proj", ps, d_pp, "tn", 256, 1024, 1024,[F32])
    grads["w_ple_gate"] = _mm("g_ple_gate", en, d_gt, "tn", 1024, 1024, 1024,[F32])
    d_en = _mm("d_ple_gate", d_gt, W["w_ple_gate"], "nt", 1024,1024, 1024, [F32])

    def add_norm_bwd(dh_out, dn, h, g):
        dh, dg = _rms_bwd(dn, h, g)
        dh = dh_out + dh
        return dh, dh, dg

    dh2, dh2_b, dg_ple = _rowwise("add_norm_ple_bwd", add_norm_bwd, T, TM, [_row(dh3, TM), _row(d_en, TM), _row(h2, TM), _full(gp)],
                                  [(D_MODEL, F32), (D_MODEL, BF)], sums=[D_MODEL])
    d_up = _mm("d_mlp_down", dh2_b, W["w_down"], "nt", 1024,1024, 1024, [BF],
               epilogue=lambda acc, u: (acc * (2.0 * jnp.maximum(u, 0.0)),), extras=(up,))
    grads["w_down"] = _mm("g_mlp_down", act, dh2_b, "tn", 1024, 1024, 1024,[F32])
    grads["w_up"] = _mm("g_mlp_up", cn, d_up, "tn", 1024, 1024, 1024,[F32])
    early_shapes = {n: shards[n].shape for n in REDUCE_EARLY}
    early_tm = sum(r * c for r, c in early_shapes.values()) // PACK_W // 4
    swap_flight = _swap_start(_pack_grads(grads, REDUCE_EARLY))
    d_cn = _mm("d_mlp_up", d_up, W["w_up"], "nt", 1024,1024, 1024, [F32], after=(swap_flight.token,))
    dh1, dh1_b, dg_mlp = _rowwise("add_norm_mlp_bwd", add_norm_bwd, T, TM, [_row(dh2, TM), _row(d_cn, TM), _row(h1, TM), _full(gl)],
                                  [(D_MODEL, F32), (D_MODEL, BF)], sums=[D_MODEL])
    early_g, early_got = _swap_wait(swap_flight, dh1_b)
    early_pair, early_pair_b = _pair_sum(early_g, early_got, early_tm)
    scatter_flight = _scatter_start(early_pair_b)
    d_mixed = _mm("d_out_proj", dh1_b, W["w_out"], "nt", 1024,1024, 1024, [F32], after=(scatter_flight.token,))
    grads["w_out"] = _mm("g_out_proj", mixed, dh1_b, "tn", 1024, 1024, 1024,[F32])

    def gate_bwd(dm, gn, gd, un, ud):
        sn, sd = _sigmoid(gn), _sigmoid(gd)
        return jnp.concatenate([dm * un * sn * (1.0 - sn), dm * ud * sd * (1.0 - sd)], axis=1), dm * sn, dm * sd

    dz_gates, d_u_na, d_u_dil = _rowwise(
        "gate_mix_bwd", gate_bwd, T, TM,
        [_row(d_mixed, TM), _row(z_gates, TM, 0, D_MODEL), _row(z_gates, TM, 1, D_MODEL), _row(u_na, TM), _row(u_dil, TM)],
        [(2 * D_MODEL, BF), (D_MODEL, BF), (D_MODEL, BF)])
    grads["w_branch_na"] = _mm("g_branch_na", y_na, d_u_na, "tn", 1024, 1024, 1024,[F32])
    grads["w_branch_dil"] = _mm("g_branch_dil", y_dil, d_u_dil, "tn", 256, 1024, 1024,[F32])
    d_y_na = _mm("d_branch_na", d_u_na, W["w_branch_na"], "nt", 1024,512, 1024, [BF])
    d_y_dil = _mm("d_branch_dil", d_u_dil, W["w_branch_dil"], "nt", 1024,256, 1024, [F32])

    dqa, dka, dva, dtab = _na_bwd(qkv, tab, d_y_na)
    d_rpb = _na_rpb_grad(dtab)[:, :2 * NA_WIN_ROWS - 1, :2 * NA_WIN_COLS - 1]

    do_nat, dlse_nat = _dil_merge_bwd(d_y_dil, o_nat, w_grp, TM)
    d_dil = []
    for g, (_, dil) in enumerate(DIL_GROUPS):
        res = _band_bwd(*qkv_seg[g], _to_segments(do_nat[g], dil), _to_segments(dlse_nat[g], dil), g)
        d_dil.append([_from_segments(t, dil) for t in res])
    d_dil = [d_dil[g][part] for part in range(3) for g in range(len(DIL_GROUPS))]

    def unprep(dq, dk, dv, *rest):
        dd, (cs, sn) = rest[:-2], rest[-2:]
        G = len(DIL_GROUPS)
        return jnp.concatenate([
            dq * Q_SCALE, dk, dv,
            _rope_cols(jnp.concatenate(dd[:G], axis=1), cs, -sn) * Q_SCALE,
            _rope_cols(jnp.concatenate(dd[G:2 * G], axis=1), cs, -sn),
            *dd[2 * G:]], axis=1)

    me_chip = 2 * lax.axis_index("x") + lax.axis_index("y")
    early_mine = _chip_sum(lax.dynamic_index_in_dim(early_pair, me_chip, 0, keepdims=False),
                           _scatter_wait(scatter_flight, d_dil[-1]), early_tm)
    join_flight = _join_start(early_mine)
    dz_qkv = _rowwise("qkv_unprep", unprep, T, TM, [_row(t, TM) for t in (dqa, dka, dva, *d_dil, cos2, sin_signed)],
                      [(QKV_WIDTH, BF)], after=(join_flight.token,))
    grads["w_in"] = jnp.concatenate([
        _mm("g_in_qkv", a, dz_qkv, "tn", 1024, 1280, 1024,[F32]),
        _mm("g_in_gates", a, dz_gates, "tn", 1024, 1024, 1024,[F32])], axis=1)
    d_a = _mm("d_in_qkv", dz_qkv, w_qkv, "nt", 1024,1024, 1280, [F32])
    d_a = _mm("d_in_gates", dz_gates, w_gates, "nt", 1024,1024, 1024, [F32], epilogue=lambda acc, e: (acc + e,), extras=(d_a,))

    def first_bwd(dh_out, dn, h, g):
        dh, dg = _rms_bwd(dn, h, g)
        return dh_out + dh, dg

    grad_x, dg_mix = _rowwise("norm_mix_bwd", first_bwd, T, TM, [_row(dh1, TM), _row(d_a, TM), _row(xs, TM), _full(gm)],
                              [(D_MODEL, F32)], sums=[D_MODEL])

    late_shapes = {n: shards[n].shape for n in REDUCE_LATE}
    late_tm = sum(r * c for r, c in late_shapes.values()) // PACK_W // 4
    packed_g = _pack_grads(grads, REDUCE_LATE)
    pair, pair_b = _pair_sum(packed_g, _swap_halves(packed_g), late_tm)
    mine = _chip_sum(lax.dynamic_index_in_dim(pair, me_chip, 0, keepdims=False), _scatter_chips(pair_b), late_tm)
    g_shard = _unpack_shard(_join_halves(mine), late_shapes, REDUCE_LATE)
    g_shard.update(_unpack_shard(_join_wait(join_flight, grad_x), early_shapes, REDUCE_EARLY))

    n_rpb = rpb.size
    rpb_rows = 4
    small = jnp.concatenate([
        dg_mix, dg_mlp, dg_ple, dg_final,
        jnp.pad(d_rpb.reshape(-1), (0, rpb_rows * D_MODEL - n_rpb)).reshape(rpb_rows, D_MODEL),
        jnp.pad(loss_part, ((0, 0), (0, D_MODEL - loss_part.shape[1]))),
        jnp.zeros((SMALL_ROWS - 5 - rpb_rows, D_MODEL), F32)], axis=0)
    small = _allreduce_small(small)
    loss = small[4 + rpb_rows, 0]

    def small_pack(a0, a1, a2, a3, r):
        return jnp.concatenate([a0.reshape(1, -1), a1.reshape(1, -1), a2.reshape(1, -1), a3.reshape(1, -1),
                                jnp.pad(r.reshape(-1), (0, rpb_rows * D_MODEL - n_rpb)).reshape(rpb_rows, D_MODEL)], axis=0)

    g_small = small[:4 + rpb_rows]
    small_res = _adamw("adamw_small", g_small, small_pack(g_mix, g_mlp, g_ple, g_final, rpb),
                       small_pack(m_g_mix, m_g_mlp, m_g_ple, m_g_final, m_rpb), small_pack(v_g_mix, v_g_mlp, v_g_ple, v_g_final, v_rpb))

    def small_unpack(t):
        return {"g_mix": t[0].reshape(g_mix.shape), "g_mlp": t[1].reshape(g_mlp.shape), "g_ple": t[2].reshape(g_ple.shape),
                "g_final": t[3].reshape(g_final.shape), "rpb": t[4:].reshape(-1)[:n_rpb].reshape(rpb.shape)}

    out = {"grad": small_unpack(g_small)}
    for kind, t in zip(("delta", "new_m", "new_v"), small_res, strict=True):
        out[kind] = small_unpack(t)
    for n in BIG:
        out["grad"][n] = g_shard[n][None]
        res = _adamw("adamw_" + n, g_shard[n], shards[n], m_shards[n], v_shards[n])
        for kind, t in zip(("delta", "new_m", "new_v"), res, strict=True):
            out[kind][n] = t[None]

    order = ["g_mix", "w_in", "rpb", "w_branch_na", "w_branch_dil", "w_out", "g_mlp", "w_up", "w_down", "g_ple",
             "w_ple_gate", "w_ple_proj", "g_final"]
    return (loss, grad_x[None], *[out["grad"][n] for n in order], *[out["delta"][n] for n in order],
            *[out["new_m"][n] for n in order], *[out["new_v"][n] for n in order])
```

```python
import functools
from typing import NamedTuple

import numpy as np
import jax
import jax.numpy as jnp
from jax import lax
from jax.experimental import pallas as pl
from jax.experimental.pallas import tpu as pltpu

BF = jnp.bfloat16
F32 = jnp.float32
MESH = pl.DeviceIdType.MESH
ANY = pl.BlockSpec(memory_space=pl.ANY)

V7X_VMEM_BYTES = 64 * 1024 * 1024
VMEM_LIMIT = V7X_VMEM_BYTES - 16 * 1024 * 1024

D_MODEL = 1024
HEAD_DIM = 64
GRID_W = 64
NA_HEADS = 8
NA_WIN_ROWS = 8
NA_WIN_COLS = 16
NA_WIDTH = NA_HEADS * HEAD_DIM
DIL_GROUPS = ((128, 1), (512, 4), (2048, 16))
DIL_HPG = 4
DIL_HEADS = DIL_HPG * len(DIL_GROUPS)
DIL_WIDTH = DIL_HEADS * HEAD_DIM
DIL_OUT_WIDTH = DIL_HPG * HEAD_DIM
DIL_RADIUS = 64
QKV_WIDTH = 3 * NA_WIDTH + 3 * DIL_WIDTH
D_FF = 4 * D_MODEL
PLE_DIM = 256
ROPE_THETA = 10000.0
RMS_EPS = 1e-6
NEG_INF = -1e30
Q_SCALE = HEAD_DIM ** -0.5

ADAM_LR = 0.001
ADAM_B1 = 0.9
ADAM_B2 = 0.999
ADAM_EPS = 1e-08
ADAM_WD = 0.01
ADAM_STEP = 10

N_CHIPS = 4
N_DEV = 8
PACK_W = 1024
BIG = ("w_in", "w_branch_na", "w_branch_dil", "w_out", "w_up", "w_down", "w_ple_gate", "w_ple_proj")
GATHER_FIRST = ("w_in",)
GATHER_REST = BIG[1:]
REDUCE_EARLY = ("w_ple_proj", "w_ple_gate", "w_down", "w_up")
REDUCE_LATE = ("w_in", "w_branch_na", "w_branch_dil", "w_out")
SMALL_ROWS = 16


def _cparams(sem=None):
    return pltpu.CompilerParams(dimension_semantics=sem, vmem_limit_bytes=VMEM_LIMIT)


def _mm(name, a, b, mode, tm, tn, tk, out_dtypes, epilogue=None, extras=(), after=()):
    if mode == "nn":
        (M, K), N = a.shape, b.shape[1]
    elif mode == "nt":
        (M, K), N = a.shape, b.shape[0]
    else:
        (K, M), N = a.shape, b.shape[1]
    tm, tn, tk = min(tm, M), min(tn, N), min(tk, K)
    assert M % tm == 0 and N % tn == 0 and K % tk == 0, (name, M, N, K, tm, tn, tk)
    if mode == "nn":
        a_spec = pl.BlockSpec((tm, tk), lambda i, j, k: (i, k))
        b_spec = pl.BlockSpec((tk, tn), lambda i, j, k: (k, j))
        dims = (((1,), (0,)), ((), ()))
    elif mode == "nt":
        a_spec = pl.BlockSpec((tm, tk), lambda i, j, k: (i, k))
        b_spec = pl.BlockSpec((tn, tk), lambda i, j, k: (j, k))
        dims = (((1,), (1,)), ((), ()))
    else:
        a_spec = pl.BlockSpec((tk, tm), lambda i, j, k: (k, i))
        b_spec = pl.BlockSpec((tk, tn), lambda i, j, k: (k, j))
        dims = (((0,), (0,)), ((), ()))
    nk = K // tk
    n_extra, n_out = len(extras), len(out_dtypes)
    tile = pl.BlockSpec((tm, tn), lambda i, j, k: (i, j))

    n_after = len(after)

    def body(a_ref, b_ref, *rest):
        extra_refs, rest = rest[:n_extra], rest[n_extra + n_after:]
        out_refs, acc = rest[:n_out], rest[-1]
        k = pl.program_id(2)

        @pl.when(k == 0)
        def _():
            acc[...] = jnp.zeros_like(acc)

        acc[...] += lax.dot_general(a_ref[...].astype(BF), b_ref[...].astype(BF), dims, preferred_element_type=F32)

        @pl.when(k == nk - 1)
        def _():
            outs = (acc[...],) if epilogue is None else epilogue(acc[...], *[e[...] for e in extra_refs])
            for o_ref, val in zip(out_refs, outs, strict=True):
                o_ref[...] = val.astype(o_ref.dtype)

    outs = pl.pallas_call(
        body, name=name, grid=(M // tm, N // tn, nk),
        in_specs=[a_spec, b_spec] + [tile] * n_extra + [ANY] * n_after,
        out_specs=[tile] * n_out,
        out_shape=[jax.ShapeDtypeStruct((M, N), dt) for dt in out_dtypes],
        scratch_shapes=[pltpu.VMEM((tm, tn), F32)],
        compiler_params=_cparams(("parallel", "parallel", "arbitrary")),
    )(a, b, *extras, *after)
    return outs[0] if n_out == 1 else outs


def _row(arr, tm, col_block=None, width=None):
    width = arr.shape[1] if width is None else width
    cb = 0 if col_block is None else col_block
    return arr, pl.BlockSpec((tm, width), lambda i: (i, cb))


def _full(arr):
    nd = arr.ndim
    return arr, pl.BlockSpec(arr.shape, lambda i: (0,) * nd)


def _rowwise(name, body, T, tm, ins, outs, sums=(), after=()):
    n_in, n_out, n_sum, n_after = len(ins), len(outs), len(sums), len(after)

    def kern(*refs):
        in_refs, refs = refs[:n_in], refs[n_in + n_after:]
        out_refs, sum_refs = refs[:n_out], refs[n_out:]
        res = body(*[r[...] for r in in_refs])
        res = res if isinstance(res, tuple) else (res,)
        for o_ref, val in zip(out_refs, res[:n_out], strict=True):
            o_ref[...] = val.astype(o_ref.dtype)
        if n_sum:
            @pl.when(pl.program_id(0) == 0)
            def _():
                for s_ref in sum_refs:
                    s_ref[...] = jnp.zeros_like(s_ref)

            for s_ref, val in zip(sum_refs, res[n_out:], strict=True):
                s_ref[...] += val

    res = pl.pallas_call(
        kern, name=name, grid=(T // tm,),
        in_specs=[spec for _, spec in ins] + [ANY] * n_after,
        out_specs=[pl.BlockSpec((tm, c), lambda i: (i, 0)) for c, _ in outs]
        + [pl.BlockSpec((1, c), lambda i: (0, 0)) for c in sums],
        out_shape=[jax.ShapeDtypeStruct((T, c), dt) for c, dt in outs]
        + [jax.ShapeDtypeStruct((1, c), F32) for c in sums],
        compiler_params=_cparams(("arbitrary",)),
    )(*[a for a, _ in ins], *after)
    return res[0] if len(res) == 1 else res


def _sigmoid(x):
    return 1.0 / (1.0 + jnp.exp(-x))


def _rms(h):
    return lax.rsqrt(jnp.mean(h * h, axis=-1, keepdims=True) + RMS_EPS)


def _rms_bwd(dy, h, g):
    r = _rms(h)
    n = h * r
    dn = dy * g
    dh = r * (dn - n * jnp.mean(dn * n, axis=-1, keepdims=True))
    return dh, jnp.sum(dy * n, axis=0, keepdims=True)


def _rope(x, cos2, sin_signed):
    lane = lax.broadcasted_iota(jnp.int32, x.shape, 1)
    swapped = jnp.where((lane % HEAD_DIM) < HEAD_DIM // 2, pltpu.roll(x, 128 - HEAD_DIM // 2, 1), pltpu.roll(x, HEAD_DIM // 2, 1))
    return x * cos2 + swapped * sin_signed


def _rope_cols(x, cos2, sin_signed):
    return jnp.concatenate([_rope(x[:, c:c + 128], cos2, sin_signed) for c in range(0, x.shape[1], 128)], axis=1)


NA_KEYS = NA_WIN_ROWS * GRID_W
NA_BASES = 8


def _na_row_geometry(r, rows):
    first = jnp.clip(r - NA_WIN_ROWS // 2, 0, rows - NA_WIN_ROWS)
    base = first - r + (NA_WIN_ROWS - 1)
    return pl.multiple_of(first * GRID_W, GRID_W), base


NA_ROWS_PER_STEP = 8
NA_BWD_ROWS_PER_STEP = 4


def _softmax_rows(s):
    p = jnp.exp(s - jnp.max(s, axis=-1, keepdims=True))
    return p / jnp.sum(p, axis=-1, keepdims=True)


def _na_probs(q, kw, bias):
    return _softmax_rows(lax.dot_general(q, kw, (((1,), (1,)), ((), ())), preferred_element_type=F32) + bias)


def _split_pair(t):
    first = lax.broadcasted_iota(jnp.int32, t.shape, 1) < HEAD_DIM
    zero = jnp.zeros_like(t)
    return jnp.where(first, t, zero), jnp.where(first, zero, t)


def _join_pair(a, b):
    return jnp.where(lax.broadcasted_iota(jnp.int32, a.shape, 1) < HEAD_DIM, a, b)


_NT = (((1,), (1,)), ((), ()))
_TN = (((0,), (0,)), ((), ()))


def _na_fwd(qkv, tab):
    T = qkv.shape[0]
    rows = T // GRID_W
    n_pairs = NA_WIDTH // 128

    def body(q_ref, k_ref, v_ref, tab_ref, y_ref):
        def step(it, carry):
            geo = [_na_row_geometry(it * NA_ROWS_PER_STEP + u, rows) for u in range(NA_ROWS_PER_STEP)]
            q0s = [pl.multiple_of((it * NA_ROWS_PER_STEP + u) * GRID_W, GRID_W) for u in range(NA_ROWS_PER_STEP)]
            ss = [lax.dot_general(jnp.concatenate(_split_pair(q_ref[pl.ds(q0, GRID_W), :]), axis=0),
                                  k_ref[pl.ds(k0, NA_KEYS), :], _NT, preferred_element_type=F32)
                  for q0, (k0, _) in zip(q0s, geo)]
            ps = [_softmax_rows(s + jnp.concatenate([tab_ref[0, base], tab_ref[1, base]], axis=0)) for s, (_, base) in zip(ss, geo)]
            ys = [jnp.dot(p.astype(BF), v_ref[pl.ds(k0, NA_KEYS), :], preferred_element_type=F32) for p, (k0, _) in zip(ps, geo)]
            for q0, y2 in zip(q0s, ys):
                y_ref[pl.ds(q0, GRID_W), :] = _join_pair(y2[:GRID_W], y2[GRID_W:]).astype(y_ref.dtype)
            return carry

        lax.fori_loop(0, rows // NA_ROWS_PER_STEP, step, 0)

    def cols(first):
        return pl.BlockSpec((T, 128), lambda j: (0, first + j))

    return pl.pallas_call(
        body, name="na_fwd", grid=(n_pairs,),
        in_specs=[cols(0), cols(n_pairs), cols(2 * n_pairs), pl.BlockSpec((2, NA_BASES, GRID_W, NA_KEYS), lambda j: (j, 0, 0, 0))],
        out_specs=cols(0), out_shape=jax.ShapeDtypeStruct((T, NA_WIDTH), BF),
        compiler_params=_cparams(("parallel",)),
    )(qkv, qkv, qkv, tab)


def _na_bwd(qkv, tab, do):
    T = qkv.shape[0]
    rows = T // GRID_W
    n_pairs = NA_WIDTH // 128

    def body(q_ref, k_ref, v_ref, tab_ref, do_ref, dq_ref, dk_ref, dv_ref, dtab_ref):
        dk_ref[...] = jnp.zeros_like(dk_ref)
        dv_ref[...] = jnp.zeros_like(dv_ref)
        dtab_ref[...] = jnp.zeros_like(dtab_ref)

        def step(it, carry):
            U = NA_BWD_ROWS_PER_STEP
            geo = [_na_row_geometry(it * U + u, rows) for u in range(U)]
            q0s = [pl.multiple_of((it * U + u) * GRID_W, GRID_W) for u in range(U)]
            q2s = [jnp.concatenate(_split_pair(q_ref[pl.ds(q0, GRID_W), :]), axis=0) for q0 in q0s]
            do2s = [jnp.concatenate(_split_pair(do_ref[pl.ds(q0, GRID_W), :]), axis=0) for q0 in q0s]
            ss = [lax.dot_general(q2, k_ref[pl.ds(k0, NA_KEYS), :], _NT, preferred_element_type=F32) for q2, (k0, _) in zip(q2s, geo)]
            dps = [lax.dot_general(do2, v_ref[pl.ds(k0, NA_KEYS), :], _NT, preferred_element_type=F32) for do2, (k0, _) in zip(do2s, geo)]
            ps = [_softmax_rows(s + jnp.concatenate([tab_ref[0, base], tab_ref[1, base]], axis=0)) for s, (_, base) in zip(ss, geo)]
            dss = [p * (dp - jnp.sum(dp * p, axis=-1, keepdims=True)) for p, dp in zip(ps, dps)]
            dvs = [lax.dot_general(p.astype(BF), do2, _TN, preferred_element_type=F32) for p, do2 in zip(ps, do2s)]
            dsbs = [ds.astype(BF) for ds in dss]
            dqs = [jnp.dot(dsb, k_ref[pl.ds(k0, NA_KEYS), :], preferred_element_type=F32) for dsb, (k0, _) in zip(dsbs, geo)]
            dks = [lax.dot_general(dsb, q2, _TN, preferred_element_type=F32) for dsb, q2 in zip(dsbs, q2s)]
            for u in range(U):
                k0, base = geo[u]
                dtab_ref[0, base] += dss[u][:GRID_W]
                dtab_ref[1, base] += dss[u][GRID_W:]
                dq_ref[pl.ds(q0s[u], GRID_W), :] = _join_pair(dqs[u][:GRID_W], dqs[u][GRID_W:])
                dk_ref[pl.ds(k0, NA_KEYS), :] += dks[u]
                dv_ref[pl.ds(k0, NA_KEYS), :] += dvs[u]
            return carry

        lax.fori_loop(0, rows // NA_BWD_ROWS_PER_STEP, step, 0)

    def cols(first):
        return pl.BlockSpec((T, 128), lambda j: (0, first + j))

    tabs = pl.BlockSpec((2, NA_BASES, GRID_W, NA_KEYS), lambda j: (j, 0, 0, 0))
    wide = jax.ShapeDtypeStruct((T, NA_WIDTH), F32)
    return pl.pallas_call(
        body, name="na_bwd", grid=(n_pairs,),
        in_specs=[cols(0), cols(n_pairs), cols(2 * n_pairs), tabs, cols(0)],
        out_specs=[cols(0), cols(0), cols(0), tabs],
        out_shape=[wide, wide, wide, jax.ShapeDtypeStruct((NA_HEADS, NA_BASES, GRID_W, NA_KEYS), F32)],
        compiler_params=_cparams(("parallel",)),
    )(qkv, qkv, qkv, tab, do)


def _na_geometry_np():
    c = np.arange(GRID_W)
    cs = np.clip(c - NA_WIN_COLS // 2, 0, GRID_W - NA_WIN_COLS)
    valid = (c[None, :] >= cs[:, None]) & (c[None, :] < cs[:, None] + NA_WIN_COLS)
    off = c[None, :] - c[:, None] + (NA_WIN_COLS - 1)
    oh_col = np.zeros((GRID_W, GRID_W, 2 * NA_WIN_COLS - 1), np.float32)
    qq, kk = np.nonzero(valid)
    oh_col[qq, kk, off[qq, kk]] = 1.0
    oh_row = np.zeros((NA_BASES, NA_WIN_ROWS, 2 * NA_WIN_ROWS - 1), np.float32)
    for base in range(NA_BASES):
        for i in range(NA_WIN_ROWS):
            oh_row[base, i, base + i] = 1.0
    return valid, oh_col, oh_row


def _na_bias_table(rpb):
    valid, oh_col, oh_row = _na_geometry_np()
    hi = lax.Precision.HIGHEST
    t1 = jnp.einsum("hrc,pir->hpic", rpb, oh_row, precision=hi)
    tab = jnp.einsum("hpic,qkc->hpqik", t1, oh_col, precision=hi)
    tab = jnp.where(valid[None, None, :, None, :], tab, NEG_INF)
    return tab.reshape(rpb.shape[0], NA_BASES, GRID_W, NA_KEYS)


def _na_rpb_grad(dtab):
    H = dtab.shape[0]
    n_rows = 2 * NA_WIN_ROWS - 1
    n_cols = 2 * NA_WIN_COLS - 1

    def body(d_ref, o_ref):
        lane = lax.broadcasted_iota(jnp.int32, (GRID_W, 128), 1)
        low = lane < GRID_W
        out_rows = []
        for ro in range(n_rows):
            acc = jnp.zeros((GRID_W, 128), F32)
            for base in range(NA_BASES):
                i = ro - base
                if not 0 <= i < NA_WIN_ROWS:
                    continue
                pair = d_ref[base, :, pl.ds((i // 2) * 128, 128)]
                if i % 2:
                    pair = pltpu.roll(pair, GRID_W, 1)
                acc = acc + jnp.where(low, pair, 0.0)
            skew = pltpu.roll(acc, 0, 1, stride=1, stride_axis=0)
            diag = jnp.sum(skew, axis=0, keepdims=True)
            out_rows.append(pltpu.roll(jnp.broadcast_to(diag, (8, 128)), 128 - (GRID_W - NA_WIN_COLS), 1)[:1])
        out_rows.append(jnp.zeros((1, 128), F32))
        res = jnp.concatenate(out_rows, axis=0)
        o_ref[...] = jnp.where(lax.broadcasted_iota(jnp.int32, res.shape, 1) < n_cols, res, 0.0)

    return pl.pallas_call(
        body, name="na_rpb_grad", grid=(H,),
        in_specs=[pl.BlockSpec((None, NA_BASES, GRID_W, NA_KEYS), lambda h: (h, 0, 0, 0))],
        out_specs=pl.BlockSpec((None, n_rows + 1, 128), lambda h: (h, 0, 0)),
        out_shape=jax.ShapeDtypeStruct((H, n_rows + 1, 128), F32),
        compiler_params=_cparams(("parallel",)),
    )(jnp.flip(dtab, axis=2))


BAND_Q = 128
BAND_KEYS = BAND_Q + 2 * DIL_RADIUS


def _band_geometry(n, L):
    q0 = pl.multiple_of(n * BAND_Q, BAND_Q)
    k0 = pl.multiple_of(jnp.clip(q0 - DIL_RADIUS, 0, L - BAND_KEYS), DIL_RADIUS)
    qi = q0 + lax.broadcasted_iota(jnp.int32, (BAND_Q, BAND_KEYS), 0)
    kj = k0 + lax.broadcasted_iota(jnp.int32, (BAND_Q, BAND_KEYS), 1)
    return q0, k0, jnp.abs(qi - kj) <= DIL_RADIUS


def _band_specs(group, T, n_lane_blocks):
    dil = DIL_GROUPS[group][1]
    L = T // dil
    assert L % BAND_Q == 0 and L >= BAND_KEYS, (T, dil)
    pairs = DIL_OUT_WIDTH // 128

    def spec(first_block):
        return pl.BlockSpec((L, 128), lambda s: (0, (s // pairs) * n_lane_blocks + first_block + s % pairs))

    return dil, L, (dil * pairs,), spec


BAND_BLOCKS_PER_STEP = 4


def _band_softmax(s, valid):
    s = jnp.where(valid, s, NEG_INF)
    m = jnp.max(s, axis=-1, keepdims=True)
    p = jnp.exp(s - m)
    l = jnp.sum(p, axis=-1, keepdims=True)
    return p / l, m + jnp.log(l)


def _band_fwd(qkv, group):
    T, W = qkv.shape
    n_blocks, first = W // 128, (3 * NA_WIDTH + group * DIL_OUT_WIDTH) // 128
    dil, L, grid, spec = _band_specs(group, T, n_blocks)
    U = min(BAND_BLOCKS_PER_STEP, L // BAND_Q)

    def body(q_ref, k_ref, v_ref, o_ref, lse_ref):
        def step(it, carry):
            geo = [_band_geometry(it * U + u, L) for u in range(U)]
            ss = [lax.dot_general(jnp.concatenate(_split_pair(q_ref[pl.ds(q0, BAND_Q), :]), axis=0),
                                  k_ref[pl.ds(k0, BAND_KEYS), :], _NT, preferred_element_type=F32) for q0, k0, _ in geo]
            pls = [_band_softmax(s, jnp.concatenate([valid, valid], axis=0)) for s, (_, _, valid) in zip(ss, geo)]
            os = [jnp.dot(p.astype(BF), v_ref[pl.ds(k0, BAND_KEYS), :], preferred_element_type=F32) for (p, _), (_, k0, _) in zip(pls, geo)]
            for (q0, _, _), o2, (_, lse) in zip(geo, os, pls):
                o_ref[pl.ds(q0, BAND_Q), :] = _join_pair(o2[:BAND_Q], o2[BAND_Q:])
                lse2 = jnp.broadcast_to(lse, (2 * BAND_Q, 128))
                lse_ref[pl.ds(q0, BAND_Q), :] = _join_pair(lse2[:BAND_Q], lse2[BAND_Q:])
            return carry

        lax.fori_loop(0, L // (BAND_Q * U), step, 0)

    qkv_d = qkv.reshape(L, dil * W)
    n_out = DIL_OUT_WIDTH // 128
    _, _, _, out_spec = _band_specs(group, T, n_out)
    wide = jax.ShapeDtypeStruct((L, dil * DIL_OUT_WIDTH), F32)
    o, lse = pl.pallas_call(
        body, name=f"band_fwd_g{group}", grid=grid,
        in_specs=[spec(first), spec(first + DIL_WIDTH // 128), spec(first + 2 * DIL_WIDTH // 128)],
        out_specs=[out_spec(0), out_spec(0)], out_shape=[wide, wide],
        compiler_params=_cparams(("parallel",)),
    )(qkv_d, qkv_d, qkv_d)
    return o.reshape(T, DIL_OUT_WIDTH), lse.reshape(T, DIL_OUT_WIDTH)


def _band_bwd(qkv, do, dlse, group):
    T, W = qkv.shape
    n_blocks, first = W // 128, (3 * NA_WIDTH + group * DIL_OUT_WIDTH) // 128
    dil, L, grid, spec = _band_specs(group, T, n_blocks)
    U = min(BAND_BLOCKS_PER_STEP, L // BAND_Q)

    def body(q_ref, k_ref, v_ref, do_ref, dlse_ref, dq_ref, dk_ref, dv_ref):
        dk_ref[...] = jnp.zeros_like(dk_ref)
        dv_ref[...] = jnp.zeros_like(dv_ref)

        def step(it, carry):
            geo = [_band_geometry(it * U + u, L) for u in range(U)]
            q2s = [jnp.concatenate(_split_pair(q_ref[pl.ds(q0, BAND_Q), :]), axis=0) for q0, _, _ in geo]
            do2s = [jnp.concatenate(_split_pair(do_ref[pl.ds(q0, BAND_Q), :]), axis=0) for q0, _, _ in geo]
            ss = [lax.dot_general(q2, k_ref[pl.ds(k0, BAND_KEYS), :], _NT, preferred_element_type=F32) for q2, (_, k0, _) in zip(q2s, geo)]
            dps = [lax.dot_general(do2, v_ref[pl.ds(k0, BAND_KEYS), :], _NT, preferred_element_type=F32) for do2, (_, k0, _) in zip(do2s, geo)]
            ps = [_band_softmax(s, jnp.concatenate([valid, valid], axis=0))[0] for s, (_, _, valid) in zip(ss, geo)]
            dss = []
            for p, dp, (q0, _, _) in zip(ps, dps, geo):
                dl = dlse_ref[pl.ds(q0, BAND_Q), :]
                dl2 = jnp.concatenate([dl[:, :1], dl[:, HEAD_DIM:HEAD_DIM + 1]], axis=0)
                dss.append(p * (dp - jnp.sum(dp * p, axis=-1, keepdims=True) + dl2))
            dvs = [lax.dot_general(p.astype(BF), do2, _TN, preferred_element_type=F32) for p, do2 in zip(ps, do2s)]
            dsbs = [ds.astype(BF) for ds in dss]
            dqs = [jnp.dot(dsb, k_ref[pl.ds(k0, BAND_KEYS), :], preferred_element_type=F32) for dsb, (_, k0, _) in zip(dsbs, geo)]
            dks = [lax.dot_general(dsb, q2, _TN, preferred_element_type=F32) for dsb, q2 in zip(dsbs, q2s)]
            for u, (q0, k0, _) in enumerate(geo):
                dq_ref[pl.ds(q0, BAND_Q), :] = _join_pair(dqs[u][:BAND_Q], dqs[u][BAND_Q:])
                dk_ref[pl.ds(k0, BAND_KEYS), :] += dks[u]
                dv_ref[pl.ds(k0, BAND_KEYS), :] += dvs[u]
            return carry

        lax.fori_loop(0, L // (BAND_Q * U), step, 0)

    qkv_d = qkv.reshape(L, dil * W)
    n_out = DIL_OUT_WIDTH // 128
    _, _, _, out_spec = _band_specs(group, T, n_out)
    wide = jax.ShapeDtypeStruct((L, dil * DIL_OUT_WIDTH), F32)
    res = pl.pallas_call(
        body, name=f"band_bwd_g{group}", grid=grid,
        in_specs=[spec(first), spec(first + DIL_WIDTH // 128), spec(first + 2 * DIL_WIDTH // 128), out_spec(0), out_spec(0)],
        out_specs=[out_spec(0)] * 3, out_shape=[wide] * 3,
        compiler_params=_cparams(("parallel",)),
    )(qkv_d, qkv_d, qkv_d, do.reshape(L, dil * DIL_OUT_WIDTH), dlse.reshape(L, dil * DIL_OUT_WIDTH))
    return [t.reshape(T, DIL_OUT_WIDTH) for t in res]


def _head_sums(t):
    head = lax.broadcasted_iota(jnp.int32, t.shape, 1) // HEAD_DIM
    out = jnp.zeros_like(t)
    for h in range(t.shape[1] // HEAD_DIM):
        mine = head == h
        out = jnp.where(mine, jnp.sum(jnp.where(mine, t, 0.0), axis=-1, keepdims=True), out)
    return out


def _dil_merge_fwd(os, lses, tm):
    G = len(DIL_GROUPS)
    T, W = os[0].shape

    def body(*tiles):
        o, ls = tiles[:G], tiles[G:]
        m = functools.reduce(jnp.maximum, ls)
        es = [jnp.exp(l - m) for l in ls]
        tot = functools.reduce(jnp.add, es)
        ws = [e / tot for e in es]
        return (functools.reduce(jnp.add, [w * t for w, t in zip(ws, o)]), *ws)

    res = _rowwise("dil_merge_fwd", body, T, tm, [_row(t, tm) for t in (*os, *lses)], [(W, BF)] + [(W, F32)] * G)
    return res[0], res[1:]


def _dil_merge_bwd(dy, os, ws, tm):
    G = len(DIL_GROUPS)
    T, W = dy.shape

    def body(dyt, *tiles):
        o, w = tiles[:G], tiles[G:]
        dws = [_head_sums(dyt * t) for t in o]
        mean = functools.reduce(jnp.add, [a * b for a, b in zip(w, dws)])
        return (*[a * dyt for a in w], *[a * (b - mean) for a, b in zip(w, dws)])

    res = _rowwise("dil_merge_bwd", body, T, tm, [_row(t, tm) for t in (dy, *os, *ws)], [(W, BF)] * G + [(W, F32)] * G)
    return res[:G], res[G:]


def _rope_tables(positions):
    half = HEAD_DIM // 2
    inv_freq = ROPE_THETA ** (-jnp.arange(half, dtype=F32) / half)
    ang = positions.astype(F32)[:, None] * inv_freq
    cos, sin = jnp.cos(ang), jnp.sin(ang)
    return jnp.tile(jnp.concatenate([cos, cos], axis=1), (1, 2)), jnp.tile(jnp.concatenate([-sin, sin], axis=1), (1, 2))


def _pack_rows(t):
    return t.reshape(-1, PACK_W)


def _me():
    return lax.axis_index("x"), lax.axis_index("y"), lax.axis_index("c")


def _other_chips(x, y):
    return [(1 - x, y), (x, 1 - y), (1 - x, 1 - y)]


def _gather_weights(packed):
    R, W = packed.shape
    half = R // 2

    def body(in_ref, out_ref, send_sems, recv_sems):
        x, y, c = _me()
        sibling = (x, y, 1 - c)
        chips = _other_chips(x, y)

        def block(chip, core):
            return out_ref.at[2 * chip[0] + chip[1], pl.ds(core * half, half), :]

        def copy(k, chip, core, to, src=None):
            return pltpu.make_async_remote_copy(
                src_ref=block(chip, core) if src is None else src, dst_ref=block(chip, core),
                send_sem=send_sems.at[k], recv_sem=recv_sems.at[k], device_id=to, device_id_type=MESH)

        first = [copy(j, (x, y), c, (*chip, c), src=in_ref.at[pl.ds(c * half, half), :]) for j, chip in enumerate(chips)]
        for cp in first:
            cp.start()
        passed = [copy(3 + j, chip, c, sibling) for j, chip in enumerate(chips)]
        for j, chip in enumerate(chips):
            copy(j, chip, c, (x, y, c)).wait_recv()
            passed[j].start()
        for j, chip in enumerate(chips):
            copy(3 + j, chip, 1 - c, (x, y, c)).wait_recv()
        for cp in first + passed:
            cp.wait_send()

    others = pl.pallas_call(
        body, name="gather_weights",
        in_specs=[ANY], out_specs=ANY,
        out_shape=jax.ShapeDtypeStruct((N_CHIPS, R, W), packed.dtype),
        scratch_shapes=[pltpu.SemaphoreType.DMA((6,)), pltpu.SemaphoreType.DMA((6,))],
    )(packed)
    return lax.dynamic_update_slice(others, packed[None], (2 * lax.axis_index("x") + lax.axis_index("y"), 0, 0))


def _swap_halves(g):
    S, R, W = g.shape
    half = R // 2

    def body(g_ref, out_ref, send_sem, recv_sem):
        x, y, c = _me()
        cp = pltpu.make_async_remote_copy(
            src_ref=g_ref.at[:, pl.ds((1 - c) * half, half), :], dst_ref=out_ref,
            send_sem=send_sem, recv_sem=recv_sem, device_id=(x, y, 1 - c), device_id_type=MESH)
        cp.start()
        cp.wait()

    return pl.pallas_call(
        body, name="swap_halves", in_specs=[ANY], out_specs=ANY,
        out_shape=jax.ShapeDtypeStruct((S, half, W), g.dtype),
        scratch_shapes=[pltpu.SemaphoreType.DMA, pltpu.SemaphoreType.DMA],
    )(g)


def _pair_sum(g, got, tm):
    S, R, W = g.shape
    half = R // 2
    nb = half // tm

    def body(c_ref, g_ref, got_ref, o_ref, ob_ref):
        tot = g_ref[...] + got_ref[...]
        o_ref[...] = tot
        ob_ref[...] = tot.astype(ob_ref.dtype)

    tile = pl.BlockSpec((None, tm, W), lambda s, i, c_ref: (s, i, 0))
    return pl.pallas_call(
        body, name="pair_sum",
        grid_spec=pltpu.PrefetchScalarGridSpec(
            num_scalar_prefetch=1, grid=(S, nb),
            in_specs=[pl.BlockSpec((None, tm, W), lambda s, i, c_ref: (s, c_ref[0] * nb + i, 0)), tile],
            out_specs=[tile, tile]),
        out_shape=[jax.ShapeDtypeStruct((S, half, W), F32), jax.ShapeDtypeStruct((S, half, W), BF)],
        compiler_params=_cparams(("parallel", "parallel")),
    )(lax.axis_index("c").reshape(1).astype(jnp.int32), g, got)


def _scatter_chips(part):
    S, h, W = part.shape

    def body(p_ref, out_ref, send_sems, recv_sems):
        x, y, c = _me()
        chips = _other_chips(x, y)
        sends = [pltpu.make_async_remote_copy(
            src_ref=p_ref.at[2 * chip[0] + chip[1]], dst_ref=out_ref.at[j],
            send_sem=send_sems.at[j], recv_sem=recv_sems.at[j], device_id=(*chip, c), device_id_type=MESH)
            for j, chip in enumerate(chips)]
        for cp in sends:
            cp.start()
        for cp in sends:
            cp.wait()

    return pl.pallas_call(
        body, name="scatter_chips", in_specs=[ANY], out_specs=ANY,
        out_shape=jax.ShapeDtypeStruct((S - 1, h, W), part.dtype),
        scratch_shapes=[pltpu.SemaphoreType.DMA((3,)), pltpu.SemaphoreType.DMA((3,))],
    )(part)


def _chip_sum(own, others, tm):
    n, h, W = others.shape

    def body(own_ref, p_ref, o_ref):
        o_ref[...] = ((own_ref[...] + p_ref[0].astype(F32)) + p_ref[1].astype(F32)) + p_ref[2].astype(F32)

    return pl.pallas_call(
        body, name="chip_sum", grid=(h // tm,),
        in_specs=[pl.BlockSpec((tm, W), lambda i: (i, 0)), pl.BlockSpec((n, tm, W), lambda i: (0, i, 0))],
        out_specs=pl.BlockSpec((tm, W), lambda i: (i, 0)),
        out_shape=jax.ShapeDtypeStruct((h, W), F32),
        compiler_params=_cparams(("parallel",)),
    )(own, others)


def _join_halves(mine):
    h, W = mine.shape

    def body(m_ref, out_ref, send_sem, recv_sem):
        x, y, c = _me()
        cp = pltpu.make_async_remote_copy(
            src_ref=m_ref, dst_ref=out_ref.at[pl.ds(c * h, h), :],
            send_sem=send_sem, recv_sem=recv_sem, device_id=(x, y, 1 - c), device_id_type=MESH)
        cp.start()
        pltpu.make_async_remote_copy(
            src_ref=m_ref, dst_ref=out_ref.at[pl.ds((1 - c) * h, h), :],
            send_sem=send_sem, recv_sem=recv_sem, device_id=(x, y, 1 - c), device_id_type=MESH).wait_recv()
        cp.wait_send()

    other = pl.pallas_call(
        body, name="join_halves", in_specs=[ANY], out_specs=ANY,
        out_shape=jax.ShapeDtypeStruct((2 * h, W), mine.dtype),
        scratch_shapes=[pltpu.SemaphoreType.DMA, pltpu.SemaphoreType.DMA],
    )(mine)
    return lax.dynamic_update_slice(other, mine, (lax.axis_index("c") * h, 0))


def _allreduce_small(s):
    R, W = s.shape

    def body(s_ref, o_ref, buf, send_sems, recv_sems):
        x, y, c = _me()
        me = 4 * x + 2 * y + c
        buf[me] = s_ref[...]
        peers = [((x + fx) % 2, (y + fy) % 2, (c + fc) % 2) for fx in range(2) for fy in range(2) for fc in range(2)][1:]
        sends = [pltpu.make_async_remote_copy(
            src_ref=s_ref, dst_ref=buf.at[me], send_sem=send_sems.at[k], recv_sem=recv_sems.at[k],
            device_id=peer, device_id_type=MESH) for k, peer in enumerate(peers)]
        for cp in sends:
            cp.start()
        for k, peer in enumerate(peers):
            pltpu.make_async_remote_copy(
                src_ref=s_ref, dst_ref=buf.at[4 * peer[0] + 2 * peer[1] + peer[2]], send_sem=send_sems.at[k],
                recv_sem=recv_sems.at[k], device_id=peer, device_id_type=MESH).wait_recv()
        for cp in sends:
            cp.wait_send()
        total = buf[0]
        for d in range(1, N_DEV):
            total = total + buf[d]
        o_ref[...] = total

    return pl.pallas_call(
        body, name="allreduce_small",
        in_specs=[pl.BlockSpec(memory_space=pltpu.VMEM)], out_specs=pl.BlockSpec(memory_space=pltpu.VMEM),
        out_shape=jax.ShapeDtypeStruct((R, W), F32),
        scratch_shapes=[pltpu.VMEM((N_DEV, R, W), F32), pltpu.SemaphoreType.DMA((N_DEV - 1,)), pltpu.SemaphoreType.DMA((N_DEV - 1,))],
    )(s)


HBM_SPEC = pl.BlockSpec(memory_space=pltpu.HBM)
SEM_SPEC = pl.BlockSpec(memory_space=pltpu.SEMAPHORE)
DATAFLOW = pltpu.SideEffectType.DATAFLOW_SIDE_EFFECTING


class _InFlight(NamedTuple):
    sems: tuple
    src: jax.Array
    land: jax.Array
    token: jax.Array


def _split_start(name, src, land_shape, land_dtype, n, copies):
    def body(src_ref, land_ref, *rest):
        sems, token = rest[:2 * n], rest[-1]
        for k, (s, d, peer) in enumerate(copies(src_ref, land_ref)):
            pltpu.make_async_remote_copy(src_ref=s, dst_ref=d, send_sem=sems[k], recv_sem=sems[n + k],
                                         device_id=peer, device_id_type=MESH).start()
        token[...] = jnp.zeros_like(token)

    outs = pl.pallas_call(
        body, name=name,
        out_shape=(*[pltpu.SemaphoreType.DMA(())] * (2 * n), pltpu.HBM(src.shape, src.dtype), pltpu.HBM(land_shape, land_dtype),
                   jax.ShapeDtypeStruct((8, 128), F32)),
        in_specs=(HBM_SPEC, HBM_SPEC),
        out_specs=(*[SEM_SPEC] * (2 * n), HBM_SPEC, HBM_SPEC, pl.BlockSpec(memory_space=pltpu.VMEM)),
        input_output_aliases={0: 2 * n, 1: 2 * n + 1},
        compiler_params=pltpu.CompilerParams(has_side_effects=DATAFLOW),
    )(pltpu.with_memory_space_constraint(src, pltpu.HBM), pltpu.with_memory_space_constraint(lax.empty(land_shape, land_dtype), pltpu.HBM))
    return _InFlight(tuple(outs[:2 * n]), outs[2 * n], outs[2 * n + 1], outs[2 * n + 2])


def _split_wait(name, flight, after, n, copies):
    def body(src_ref, land_ref, *rest):
        sems = rest[:2 * n]
        for k, (s, d, peer) in enumerate(copies(src_ref, land_ref)):
            cp = pltpu.make_async_remote_copy(src_ref=s, dst_ref=d, send_sem=sems[k], recv_sem=sems[n + k],
                                              device_id=peer, device_id_type=MESH)
            cp.wait_send()
            cp.wait_recv()

    return pl.pallas_call(
        body, name=name,
        out_shape=(pltpu.HBM(flight.src.shape, flight.src.dtype), pltpu.HBM(flight.land.shape, flight.land.dtype)),
        in_specs=(HBM_SPEC, HBM_SPEC, *[SEM_SPEC] * (2 * n), ANY),
        out_specs=(HBM_SPEC, HBM_SPEC), input_output_aliases={0: 0, 1: 1},
        compiler_params=pltpu.CompilerParams(has_side_effects=DATAFLOW),
    )(flight.src, flight.land, *flight.sems, after)


def _gather_copies(src_ref, land_ref):
    x, y, c = _me()
    return [(src_ref, land_ref.at[2 * x + y], (*chip, c)) for chip in _other_chips(x, y)]


def _gather_start(packed):
    return _split_start("gather_rest_start", packed, (N_CHIPS, *packed.shape), packed.dtype, 3, _gather_copies)


def _gather_wait(flight, after):
    src, others = _split_wait("gather_rest_wait", flight, after, 3, _gather_copies)
    return lax.dynamic_update_slice(others, src[None], (2 * lax.axis_index("x") + lax.axis_index("y"), 0, 0))


def _swap_copies(src_ref, land_ref):
    x, y, c = _me()
    half = land_ref.shape[1]
    return [(src_ref.at[:, pl.ds((1 - c) * half, half), :], land_ref, (x, y, 1 - c))]


def _swap_start(g):
    S, R, W = g.shape
    return _split_start("swap_halves_start", g, (S, R // 2, W), g.dtype, 1, _swap_copies)


def _swap_wait(flight, after):
    return _split_wait("swap_halves_wait", flight, after, 1, _swap_copies)


def _scatter_copies(src_ref, land_ref):
    x, y, c = _me()
    return [(src_ref.at[2 * chip[0] + chip[1]], land_ref.at[j], (*chip, c)) for j, chip in enumerate(_other_chips(x, y))]


def _scatter_start(part):
    S, h, W = part.shape
    return _split_start("scatter_chips_start", part, (S - 1, h, W), part.dtype, 3, _scatter_copies)


def _scatter_wait(flight, after):
    return _split_wait("scatter_chips_wait", flight, after, 3, _scatter_copies)[1]


def _join_copies(src_ref, land_ref):
    x, y, c = _me()
    h = src_ref.shape[0]
    return [(src_ref, land_ref.at[pl.ds(c * h, h), :], (x, y, 1 - c))]


def _join_start(mine):
    h, W = mine.shape
    return _split_start("join_halves_start", mine, (2 * h, W), mine.dtype, 1, _join_copies)


def _join_wait(flight, after):
    src, other = _split_wait("join_halves_wait", flight, after, 1, _join_copies)
    return lax.dynamic_update_slice(other, src, (lax.axis_index("c") * src.shape[0], 0))


def _adamw(name, g, w, m, v):
    R, C = w.shape
    tm = R
    for cand in (256, 128, 64, 32, 16, 8):
        if R % cand == 0:
            tm = cand
            break

    def body(g, w, m, v):
        m = ADAM_B1 * m + (1.0 - ADAM_B1) * g
        v = ADAM_B2 * v + (1.0 - ADAM_B2) * jnp.square(g)
        m_hat = m / (1.0 - ADAM_B1 ** ADAM_STEP)
        v_hat = v / (1.0 - ADAM_B2 ** ADAM_STEP)
        delta = -ADAM_LR * (m_hat / (jnp.sqrt(v_hat) + ADAM_EPS) + ADAM_WD * w)
        return delta, m, v

    return _rowwise(name, body, R, tm, [_row(t, tm) for t in (g, w, m, v)], [(C, F32)] * 3)


def _unpack_weights(gathered, names):
    S = gathered.shape[0]
    shard_shapes = {"w_in": (D_MODEL, (QKV_WIDTH + 2 * D_MODEL) // S), "w_branch_na": (NA_WIDTH, D_MODEL // S),
                    "w_branch_dil": (DIL_OUT_WIDTH, D_MODEL // S), "w_out": (D_MODEL // S, D_MODEL),
                    "w_up": (D_MODEL, D_FF // S), "w_down": (D_FF // S, D_MODEL),
                    "w_ple_gate": (D_MODEL // S, D_MODEL), "w_ple_proj": (PLE_DIM, D_MODEL // S)}
    col_sharded = {"w_in", "w_branch_na", "w_branch_dil", "w_up", "w_ple_proj"}
    out, r0 = {}, 0
    for name in names:
        rows, cols = shard_shapes[name]
        n = rows * cols // PACK_W
        t = gathered[:, r0:r0 + n, :].reshape(S, rows, cols)
        r0 += n
        out[name] = t.transpose(1, 0, 2).reshape(rows, S * cols) if name in col_sharded else t.reshape(S * rows, cols)
    return out


def _pack_grads(grads, names):
    col_sharded = {"w_in", "w_branch_na", "w_branch_dil", "w_up", "w_ple_proj"}
    per_chip = []
    for s in range(N_CHIPS):
        rows = []
        for name in names:
            g = grads[name]
            if name in col_sharded:
                w = g.shape[1] // N_CHIPS
                rows.append(_pack_rows(g[:, s * w:(s + 1) * w]))
            else:
                h = g.shape[0] // N_CHIPS
                rows.append(_pack_rows(g[s * h:(s + 1) * h]))
        per_chip.append(jnp.concatenate(rows, axis=0))
    return jnp.stack(per_chip)


def _unpack_shard(packed, shapes, names):
    out, r0 = {}, 0
    for name in names:
        rows, cols = shapes[name]
        n = rows * cols // PACK_W
        out[name] = packed[r0:r0 + n].reshape(rows, cols)
        r0 += n
    return out


def kernel(x, p, positions, g_mix, w_in, rpb, w_branch_na, w_branch_dil, w_out, g_mlp, w_up, w_down, g_ple, w_ple_gate, w_ple_proj, g_final, loss_target, m_g_mix, m_w_in, m_rpb, m_w_branch_na, m_w_branch_dil, m_w_out, m_g_mlp, m_w_up, m_w_down, m_g_ple, m_w_ple_gate, m_w_ple_proj, m_g_final, v_g_mix, v_w_in, v_rpb, v_w_branch_na, v_w_branch_dil, v_w_out, v_g_mlp, v_w_up, v_w_down, v_g_ple, v_w_ple_gate, v_w_ple_proj, v_g_final):
    shards = {"w_in": w_in[0], "w_branch_na": w_branch_na[0], "w_branch_dil": w_branch_dil[0], "w_out": w_out[0],
              "w_up": w_up[0], "w_down": w_down[0], "w_ple_gate": w_ple_gate[0], "w_ple_proj": w_ple_proj[0]}
    m_shards = {"w_in": m_w_in[0], "w_branch_na": m_w_branch_na[0], "w_branch_dil": m_w_branch_dil[0], "w_out": m_w_out[0],
                "w_up": m_w_up[0], "w_down": m_w_down[0], "w_ple_gate": m_w_ple_gate[0], "w_ple_proj": m_w_ple_proj[0]}
    v_shards = {"w_in": v_w_in[0], "w_branch_na": v_w_branch_na[0], "w_branch_dil": v_w_branch_dil[0], "w_out": v_w_out[0],
                "w_up": v_w_up[0], "w_down": v_w_down[0], "w_ple_gate": v_w_ple_gate[0], "w_ple_proj": v_w_ple_proj[0]}

    W = _unpack_weights(_gather_weights(_pack_rows(shards["w_in"].astype(BF))), GATHER_FIRST)
    rest_flight = _gather_start(jnp.concatenate([_pack_rows(shards[n].astype(BF)) for n in GATHER_REST], axis=0))
    w_qkv, w_gates = W["w_in"][:, :QKV_WIDTH], W["w_in"][:, QKV_WIDTH:]

    xs, ps, tgt = x[0], p[0, 0], loss_target[0]
    T = xs.shape[0]
    TM = 256
    gm, gl, gp, gf = g_mix, g_mlp, g_ple, g_final.reshape(1, D_MODEL)
    cos2, sin_signed = _rope_tables(positions[0])
    tab = _na_bias_table(rpb[0])

    a = _rowwise("norm_mix", lambda h, g: h * _rms(h) * g, T, TM, [_row(xs, TM), _full(gm)], [(D_MODEL, BF)],
                 after=(rest_flight.token,))
    z_qkv = _mm("in_qkv", a, w_qkv, "nn", 1024,1280, 1024, [F32])
    z_gates = _mm("in_gates", a, w_gates, "nn", 1024,1024, 1024, [F32])

    def prep(z, cs, sn):
        n3 = 3 * NA_WIDTH
        return jnp.concatenate([
            z[:, :NA_WIDTH] * Q_SCALE, z[:, NA_WIDTH:n3],
            _rope_cols(z[:, n3:n3 + DIL_WIDTH], cs, sn) * Q_SCALE,
            _rope_cols(z[:, n3 + DIL_WIDTH:n3 + 2 * DIL_WIDTH], cs, sn),
            z[:, n3 + 2 * DIL_WIDTH:]], axis=1)

    qkv = _rowwise("qkv_prep", prep, T, TM, [_row(z_qkv, TM), _row(cos2, TM), _row(sin_signed, TM)], [(QKV_WIDTH, BF)])
    y_na = _na_fwd(qkv, tab)
    band = [_band_fwd(qkv, g) for g in range(len(DIL_GROUPS))]
    o_nat, lse_nat = [b[0] for b in band], [b[1] for b in band]
    y_dil, w_grp = _dil_merge_fwd(o_nat, lse_nat, TM)

    W.update(_unpack_weights(_gather_wait(rest_flight, y_dil), GATHER_REST))
    u_na = _mm("branch_na", y_na, W["w_branch_na"], "nn", 1024,1024, 512, [F32])
    u_dil = _mm("branch_dil", y_dil, W["w_branch_dil"], "nn", 1024,1024, 256, [F32])
    mixed = _rowwise(
        "gate_mix", lambda gn, gd, un, ud: _sigmoid(gn) * un + _sigmoid(gd) * ud, T, TM,
        [_row(z_gates, TM, 0, D_MODEL), _row(z_gates, TM, 1, D_MODEL), _row(u_na, TM), _row(u_dil, TM)], [(D_MODEL, BF)])
    mix_out = _mm("out_proj", mixed, W["w_out"], "nn", 1024,1024, 1024, [F32])

    def add_norm(h, d, g):
        h = h + d
        return h, h * _rms(h) * g

    h1, cn = _rowwise("add_norm_mlp", add_norm, T, TM, [_row(xs, TM), _row(mix_out, TM), _full(gl)], [(D_MODEL, F32), (D_MODEL, BF)])
    up, act = _mm("mlp_up", cn, W["w_up"], "nn", 1024,1024, 1024, [F32, BF],
                  epilogue=lambda acc: (acc, jnp.square(jnp.maximum(acc, 0.0))))
    mlp_out = _mm("mlp_down", act, W["w_down"], "nn", 1024,1024, 1024, [F32])
    h2, en = _rowwise("add_norm_ple", add_norm, T, TM, [_row(h1, TM), _row(mlp_out, TM), _full(gp)], [(D_MODEL, F32), (D_MODEL, BF)])
    gt = _mm("ple_gate", en, W["w_ple_gate"], "nn", 1024,1024, 1024, [F32])
    pp = _mm("ple_proj", ps, W["w_ple_proj"], "nn", 1024,1024, 256, [F32])

    def head(h2t, gtt, ppt, tg, g):
        sg = _sigmoid(gtt)
        h3 = h2t + sg * ppt
        yo = h3 * _rms(h3) * g
        diff = yo - tg
        loss = 0.5 * jnp.sum(jnp.mean(jnp.square(diff), axis=-1, keepdims=True), axis=0, keepdims=True)
        dh3, dg = _rms_bwd(diff * (1.0 / D_MODEL), h3, g)
        return dh3, dh3 * ppt * sg * (1.0 - sg), dh3 * sg, jnp.broadcast_to(loss, (1, 128)), dg

    dh3, d_gt, d_pp, loss_part, dg_final = _rowwise(
        "loss_head", head, T, TM, [_row(h2, TM), _row(gt, TM), _row(pp, TM), _row(tgt, TM), _full(gf)],
        [(D_MODEL, F32), (D_MODEL, BF), (D_MODEL, BF)], sums=[128, D_MODEL])

    grads = {}
    grads["w_ple_proj"] = _mm("g_ple_proj", ps, d_pp, "tn", 256, 1024, 1024,[F32])
    grads["w_ple_gate"] = _mm("g_ple_gate", en, d_gt, "tn", 1024, 1024, 1024,[F32])
    d_en = _mm("d_ple_gate", d_gt, W["w_ple_gate"], "nt", 1024,1024, 1024, [F32])

    def add_norm_bwd(dh_out, dn, h, g):
        dh, dg = _rms_bwd(dn, h, g)
        dh = dh_out + dh
        return dh, dh, dg

    dh2, dh2_b, dg_ple = _rowwise("add_norm_ple_bwd", add_norm_bwd, T, TM, [_row(dh3, TM), _row(d_en, TM), _row(h2, TM), _full(gp)],
                                  [(D_MODEL, F32), (D_MODEL, BF)], sums=[D_MODEL])
    d_up = _mm("d_mlp_down", dh2_b, W["w_down"], "nt", 1024,1024, 1024, [BF],
               epilogue=lambda acc, u: (acc * (2.0 * jnp.maximum(u, 0.0)),), extras=(up,))
    grads["w_down"] = _mm("g_mlp_down", act, dh2_b, "tn", 1024, 1024, 1024,[F32])
    grads["w_up"] = _mm("g_mlp_up", cn, d_up, "tn", 1024, 1024, 1024,[F32])
    early_shapes = {n: shards[n].shape for n in REDUCE_EARLY}
    early_tm = sum(r * c for r, c in early_shapes.values()) // PACK_W // 4
    swap_flight = _swap_start(_pack_grads(grads, REDUCE_EARLY))
    d_cn = _mm("d_mlp_up", d_up, W["w_up"], "nt", 1024,1024, 1024, [F32], after=(swap_flight.token,))
    dh1, dh1_b, dg_mlp = _rowwise("add_norm_mlp_bwd", add_norm_bwd, T, TM, [_row(dh2, TM), _row(d_cn, TM), _row(h1, TM), _full(gl)],
                                  [(D_MODEL, F32), (D_MODEL, BF)], sums=[D_MODEL])
    early_g, early_got = _swap_wait(swap_flight, dh1_b)
    early_pair, early_pair_b = _pair_sum(early_g, early_got, early_tm)
    scatter_flight = _scatter_start(early_pair_b)
    d_mixed = _mm("d_out_proj", dh1_b, W["w_out"], "nt", 1024,1024, 1024, [F32], after=(scatter_flight.token,))
    grads["w_out"] = _mm("g_out_proj", mixed, dh1_b, "tn", 1024, 1024, 1024,[F32])

    def gate_bwd(dm, gn, gd, un, ud):
        sn, sd = _sigmoid(gn), _sigmoid(gd)
        return jnp.concatenate([dm * un * sn * (1.0 - sn), dm * ud * sd * (1.0 - sd)], axis=1), dm * sn, dm * sd

    dz_gates, d_u_na, d_u_dil = _rowwise(
        "gate_mix_bwd", gate_bwd, T, TM,
        [_row(d_mixed, TM), _row(z_gates, TM, 0, D_MODEL), _row(z_gates, TM, 1, D_MODEL), _row(u_na, TM), _row(u_dil, TM)],
        [(2 * D_MODEL, BF), (D_MODEL, BF), (D_MODEL, BF)])
    grads["w_branch_na"] = _mm("g_branch_na", y_na, d_u_na, "tn", 1024, 1024, 1024,[F32])
    grads["w_branch_dil"] = _mm("g_branch_dil", y_dil, d_u_dil, "tn", 256, 1024, 1024,[F32])
    d_y_na = _mm("d_branch_na", d_u_na, W["w_branch_na"], "nt", 1024,512, 1024, [BF])
    d_y_dil = _mm("d_branch_dil", d_u_dil, W["w_branch_dil"], "nt", 1024,256, 1024, [F32])

    dqa, dka, dva, dtab = _na_bwd(qkv, tab, d_y_na)
    d_rpb = _na_rpb_grad(dtab)[:, :2 * NA_WIN_ROWS - 1, :2 * NA_WIN_COLS - 1]

    do_nat, dlse_nat = _dil_merge_bwd(d_y_dil, o_nat, w_grp, TM)
    d_dil = [_band_bwd(qkv, do_nat[g], dlse_nat[g], g) for g in range(len(DIL_GROUPS))]
    d_dil = [d_dil[g][part] for part in range(3) for g in range(len(DIL_GROUPS))]

    def unprep(dq, dk, dv, *rest):
        dd, (cs, sn) = rest[:-2], rest[-2:]
        G = len(DIL_GROUPS)
        return jnp.concatenate([
            dq * Q_SCALE, dk, dv,
            _rope_cols(jnp.concatenate(dd[:G], axis=1), cs, -sn) * Q_SCALE,
            _rope_cols(jnp.concatenate(dd[G:2 * G], axis=1), cs, -sn),
            *dd[2 * G:]], axis=1)

    me_chip = 2 * lax.axis_index("x") + lax.axis_index("y")
    early_mine = _chip_sum(lax.dynamic_index_in_dim(early_pair, me_chip, 0, keepdims=False),
                           _scatter_wait(scatter_flight, d_dil[-1]), early_tm)
    join_flight = _join_start(early_mine)
    dz_qkv = _rowwise("qkv_unprep", unprep, T, TM, [_row(t, TM) for t in (dqa, dka, dva, *d_dil, cos2, sin_signed)],
                      [(QKV_WIDTH, BF)], after=(join_flight.token,))
    grads["w_in"] = jnp.concatenate([
        _mm("g_in_qkv", a, dz_qkv, "tn", 1024, 1280, 1024,[F32]),
        _mm("g_in_gates", a, dz_gates, "tn", 1024, 1024, 1024,[F32])], axis=1)
    d_a = _mm("d_in_qkv", dz_qkv, w_qkv, "nt", 1024,1024, 1280, [F32])
    d_a = _mm("d_in_gates", dz_gates, w_gates, "nt", 1024,1024, 1024, [F32], epilogue=lambda acc, e: (acc + e,), extras=(d_a,))

    def first_bwd(dh_out, dn, h, g):
        dh, dg = _rms_bwd(dn, h, g)
        return dh_out + dh, dg

    grad_x, dg_mix = _rowwise("norm_mix_bwd", first_bwd, T, TM, [_row(dh1, TM), _row(d_a, TM), _row(xs, TM), _full(gm)],
                              [(D_MODEL, F32)], sums=[D_MODEL])

    late_shapes = {n: shards[n].shape for n in REDUCE_LATE}
    late_tm = sum(r * c for r, c in late_shapes.values()) // PACK_W // 4
    packed_g = _pack_grads(grads, REDUCE_LATE)
    pair, pair_b = _pair_sum(packed_g, _swap_halves(packed_g), late_tm)
    mine = _chip_sum(lax.dynamic_index_in_dim(pair, me_chip, 0, keepdims=False), _scatter_chips(pair_b), late_tm)
    g_shard = _unpack_shard(_join_halves(mine), late_shapes, REDUCE_LATE)
    g_shard.update(_unpack_shard(_join_wait(join_flight, grad_x), early_shapes, REDUCE_EARLY))

    n_rpb = rpb.size
    rpb_rows = 4
    small = jnp.concatenate([
        dg_mix, dg_mlp, dg_ple, dg_final,
        jnp.pad(d_rpb.reshape(-1), (0, rpb_rows * D_MODEL - n_rpb)).reshape(rpb_rows, D_MODEL),
        jnp.pad(loss_part, ((0, 0), (0, D_MODEL - loss_part.shape[1]))),
        jnp.zeros((SMALL_ROWS - 5 - rpb_rows, D_MODEL), F32)], axis=0)
    small = _allreduce_small(small)
    loss = small[4 + rpb_rows, 0]

    def small_pack(a0, a1, a2, a3, r):
        return jnp.concatenate([a0.reshape(1, -1), a1.reshape(1, -1), a2.reshape(1, -1), a3.reshape(1, -1),
                                jnp.pad(r.reshape(-1), (0, rpb_rows * D_MODEL - n_rpb)).reshape(rpb_rows, D_MODEL)], axis=0)

    g_small = small[:4 + rpb_rows]
    small_res = _adamw("adamw_small", g_small, small_pack(g_mix, g_mlp, g_ple, g_final, rpb),
                       small_pack(m_g_mix, m_g_mlp, m_g_ple, m_g_final, m_rpb), small_pack(v_g_mix, v_g_mlp, v_g_ple, v_g_final, v_rpb))

    def small_unpack(t):
        return {"g_mix": t[0].reshape(g_mix.shape), "g_mlp": t[1].reshape(g_mlp.shape), "g_ple": t[2].reshape(g_ple.shape),
                "g_final": t[3].reshape(g_final.shape), "rpb": t[4:].reshape(-1)[:n_rpb].reshape(rpb.shape)}

    out = {"grad": small_unpack(g_small)}
    for kind, t in zip(("delta", "new_m", "new_v"), small_res, strict=True):
        out[kind] = small_unpack(t)
    for n in BIG:
        out["grad"][n] = g_shard[n][None]
        res = _adamw("adamw_" + n, g_shard[n], shards[n], m_shards[n], v_shards[n])
        for kind, t in zip(("delta", "new_m", "new_v"), res, strict=True):
            out[kind][n] = t[None]

    order = ["g_mix", "w_in", "rpb", "w_branch_na", "w_branch_dil", "w_out", "g_mlp", "w_up", "w_down", "g_ple",
             "w_ple_gate", "w_ple_proj", "g_final"]
    return (loss, grad_x[None], *[out["grad"][n] for n in order], *[out["delta"][n] for n in order],
            *[out["new_m"][n] for n in order], *[out["new_v"][n] for n in order])
```

```python
import functools
from typing import NamedTuple

import numpy as np
import jax
import jax.numpy as jnp
from jax import lax
from jax.experimental import pallas as pl
from jax.experimental.pallas import tpu as pltpu

BF = jnp.bfloat16
F32 = jnp.float32
MESH = pl.DeviceIdType.MESH
ANY = pl.BlockSpec(memory_space=pl.ANY)

V7X_VMEM_BYTES = 64 * 1024 * 1024
VMEM_LIMIT = V7X_VMEM_BYTES - 16 * 1024 * 1024

D_MODEL = 1024
HEAD_DIM = 64
GRID_W = 64
NA_HEADS = 8
NA_WIN_ROWS = 8
NA_WIN_COLS = 16
NA_WIDTH = NA_HEADS * HEAD_DIM
DIL_GROUPS = ((128, 1), (512, 4), (2048, 16))
DIL_HPG = 4
DIL_HEADS = DIL_HPG * len(DIL_GROUPS)
DIL_WIDTH = DIL_HEADS * HEAD_DIM
DIL_OUT_WIDTH = DIL_HPG * HEAD_DIM
DIL_RADIUS = 64
QKV_WIDTH = 3 * NA_WIDTH + 3 * DIL_WIDTH
D_FF = 4 * D_MODEL
PLE_DIM = 256
ROPE_THETA = 10000.0
RMS_EPS = 1e-6
NEG_INF = -1e30
Q_SCALE = HEAD_DIM ** -0.5

ADAM_LR = 0.001
ADAM_B1 = 0.9
ADAM_B2 = 0.999
ADAM_EPS = 1e-08
ADAM_WD = 0.01
ADAM_STEP = 10

N_CHIPS = 4
N_DEV = 8
PACK_W = 1024
BIG = ("w_in", "w_branch_na", "w_branch_dil", "w_out", "w_up", "w_down", "w_ple_gate", "w_ple_proj")
GATHER_FIRST = ("w_in",)
GATHER_REST = BIG[1:]
REDUCE_EARLY = ("w_ple_proj", "w_ple_gate", "w_down", "w_up")
REDUCE_LATE = ("w_in", "w_branch_na", "w_branch_dil", "w_out")
SMALL_ROWS = 16


def _cparams(sem=None):
    return pltpu.CompilerParams(dimension_semantics=sem, vmem_limit_bytes=VMEM_LIMIT)


def _mm(name, a, b, mode, tm, tn, tk, out_dtypes, epilogue=None, extras=(), after=()):
    if mode == "nn":
        (M, K), N = a.shape, b.shape[1]
    elif mode == "nt":
        (M, K), N = a.shape, b.shape[0]
    else:
        (K, M), N = a.shape, b.shape[1]
    tm, tn, tk = min(tm, M), min(tn, N), min(tk, K)
    assert M % tm == 0 and N % tn == 0 and K % tk == 0, (name, M, N, K, tm, tn, tk)
    if mode == "nn":
        a_spec = pl.BlockSpec((tm, tk), lambda i, j, k: (i, k))
        b_spec = pl.BlockSpec((tk, tn), lambda i, j, k: (k, j))
        dims = (((1,), (0,)), ((), ()))
    elif mode == "nt":
        a_spec = pl.BlockSpec((tm, tk), lambda i, j, k: (i, k))
        b_spec = pl.BlockSpec((tn, tk), lambda i, j, k: (j, k))
        dims = (((1,), (1,)), ((), ()))
    else:
        a_spec = pl.BlockSpec((tk, tm), lambda i, j, k: (k, i))
        b_spec = pl.BlockSpec((tk, tn), lambda i, j, k: (k, j))
        dims = (((0,), (0,)), ((), ()))
    nk = K // tk
    n_extra, n_out = len(extras), len(out_dtypes)
    tile = pl.BlockSpec((tm, tn), lambda i, j, k: (i, j))

    n_after = len(after)

    def body(a_ref, b_ref, *rest):
        extra_refs, rest = rest[:n_extra], rest[n_extra + n_after:]
        out_refs, acc = rest[:n_out], rest[-1]
        k = pl.program_id(2)

        @pl.when(k == 0)
        def _():
            acc[...] = jnp.zeros_like(acc)

        acc[...] += lax.dot_general(a_ref[...].astype(BF), b_ref[...].astype(BF), dims, preferred_element_type=F32)

        @pl.when(k == nk - 1)
        def _():
            outs = (acc[...],) if epilogue is None else epilogue(acc[...], *[e[...] for e in extra_refs])
            for o_ref, val in zip(out_refs, outs, strict=True):
                o_ref[...] = val.astype(o_ref.dtype)

    outs = pl.pallas_call(
        body, name=name, grid=(M // tm, N // tn, nk),
        in_specs=[a_spec, b_spec] + [tile] * n_extra + [ANY] * n_after,
        out_specs=[tile] * n_out,
        out_shape=[jax.ShapeDtypeStruct((M, N), dt) for dt in out_dtypes],
        scratch_shapes=[pltpu.VMEM((tm, tn), F32)],
        compiler_params=_cparams(("parallel", "parallel", "arbitrary")),
    )(a, b, *extras, *after)
    return outs[0] if n_out == 1 else outs


def _row(arr, tm, col_block=None, width=None):
    width = arr.shape[1] if width is None else width
    cb = 0 if col_block is None else col_block
    return arr, pl.BlockSpec((tm, width), lambda i: (i, cb))


def _full(arr):
    nd = arr.ndim
    return arr, pl.BlockSpec(arr.shape, lambda i: (0,) * nd)


def _rowwise(name, body, T, tm, ins, outs, sums=(), after=()):
    n_in, n_out, n_sum, n_after = len(ins), len(outs), len(sums), len(after)

    def kern(*refs):
        in_refs, refs = refs[:n_in], refs[n_in + n_after:]
        out_refs, sum_refs = refs[:n_out], refs[n_out:]
        res = body(*[r[...] for r in in_refs])
        res = res if isinstance(res, tuple) else (res,)
        for o_ref, val in zip(out_refs, res[:n_out], strict=True):
            o_ref[...] = val.astype(o_ref.dtype)
        if n_sum:
            @pl.when(pl.program_id(0) == 0)
            def _():
                for s_ref in sum_refs:
                    s_ref[...] = jnp.zeros_like(s_ref)

            for s_ref, val in zip(sum_refs, res[n_out:], strict=True):
                s_ref[...] += val

    res = pl.pallas_call(
        kern, name=name, grid=(T // tm,),
        in_specs=[spec for _, spec in ins] + [ANY] * n_after,
        out_specs=[pl.BlockSpec((tm, c), lambda i: (i, 0)) for c, _ in outs]
        + [pl.BlockSpec((1, c), lambda i: (0, 0)) for c in sums],
        out_shape=[jax.ShapeDtypeStruct((T, c), dt) for c, dt in outs]
        + [jax.ShapeDtypeStruct((1, c), F32) for c in sums],
        compiler_params=_cparams(("arbitrary",)),
    )(*[a for a, _ in ins], *after)
    return res[0] if len(res) == 1 else res


def _sigmoid(x):
    return 1.0 / (1.0 + jnp.exp(-x))


def _rms(h):
    return lax.rsqrt(jnp.mean(h * h, axis=-1, keepdims=True) + RMS_EPS)


def _rms_bwd(dy, h, g):
    r = _rms(h)
    n = h * r
    dn = dy * g
    dh = r * (dn - n * jnp.mean(dn * n, axis=-1, keepdims=True))
    return dh, jnp.sum(dy * n, axis=0, keepdims=True)


def _rope(x, cos2, sin_signed):
    lane = lax.broadcasted_iota(jnp.int32, x.shape, 1)
    swapped = jnp.where((lane % HEAD_DIM) < HEAD_DIM // 2, pltpu.roll(x, 128 - HEAD_DIM // 2, 1), pltpu.roll(x, HEAD_DIM // 2, 1))
    return x * cos2 + swapped * sin_signed


def _rope_cols(x, cos2, sin_signed):
    return jnp.concatenate([_rope(x[:, c:c + 128], cos2, sin_signed) for c in range(0, x.shape[1], 128)], axis=1)


NA_KEYS = NA_WIN_ROWS * GRID_W
NA_BASES = 8


def _na_row_geometry(r, rows):
    first = jnp.clip(r - NA_WIN_ROWS // 2, 0, rows - NA_WIN_ROWS)
    base = first - r + (NA_WIN_ROWS - 1)
    return pl.multiple_of(first * GRID_W, GRID_W), base


NA_ROWS_PER_STEP = 8
NA_BWD_ROWS_PER_STEP = 4


def _softmax_rows(s):
    p = jnp.exp(s - jnp.max(s, axis=-1, keepdims=True))
    return p / jnp.sum(p, axis=-1, keepdims=True)


def _na_probs(q, kw, bias):
    return _softmax_rows(lax.dot_general(q, kw, (((1,), (1,)), ((), ())), preferred_element_type=F32) + bias)


def _split_pair(t):
    first = lax.broadcasted_iota(jnp.int32, t.shape, 1) < HEAD_DIM
    zero = jnp.zeros_like(t)
    return jnp.where(first, t, zero), jnp.where(first, zero, t)


def _join_pair(a, b):
    return jnp.where(lax.broadcasted_iota(jnp.int32, a.shape, 1) < HEAD_DIM, a, b)


_NT = (((1,), (1,)), ((), ()))
_TN = (((0,), (0,)), ((), ()))


def _na_fwd(qkv, tab):
    T = qkv.shape[0]
    rows = T // GRID_W
    n_pairs = NA_WIDTH // 128

    def body(q_ref, k_ref, v_ref, tab_ref, y_ref):
        def step(it, carry):
            geo = [_na_row_geometry(it * NA_ROWS_PER_STEP + u, rows) for u in range(NA_ROWS_PER_STEP)]
            q0s = [pl.multiple_of((it * NA_ROWS_PER_STEP + u) * GRID_W, GRID_W) for u in range(NA_ROWS_PER_STEP)]
            ss = [lax.dot_general(jnp.concatenate(_split_pair(q_ref[pl.ds(q0, GRID_W), :]), axis=0),
                                  k_ref[pl.ds(k0, NA_KEYS), :], _NT, preferred_element_type=F32)
                  for q0, (k0, _) in zip(q0s, geo)]
            ps = [_softmax_rows(s + jnp.concatenate([tab_ref[0, base], tab_ref[1, base]], axis=0)) for s, (_, base) in zip(ss, geo)]
            ys = [jnp.dot(p.astype(BF), v_ref[pl.ds(k0, NA_KEYS), :], preferred_element_type=F32) for p, (k0, _) in zip(ps, geo)]
            for q0, y2 in zip(q0s, ys):
                y_ref[pl.ds(q0, GRID_W), :] = _join_pair(y2[:GRID_W], y2[GRID_W:]).astype(y_ref.dtype)
            return carry

        lax.fori_loop(0, rows // NA_ROWS_PER_STEP, step, 0)

    def cols(first):
        return pl.BlockSpec((T, 128), lambda j: (0, first + j))

    return pl.pallas_call(
        body, name="na_fwd", grid=(n_pairs,),
        in_specs=[cols(0), cols(n_pairs), cols(2 * n_pairs), pl.BlockSpec((2, NA_BASES, GRID_W, NA_KEYS), lambda j: (j, 0, 0, 0))],
        out_specs=cols(0), out_shape=jax.ShapeDtypeStruct((T, NA_WIDTH), BF),
        compiler_params=_cparams(("parallel",)),
    )(qkv, qkv, qkv, tab)


def _na_bwd(qkv, tab, do):
    T = qkv.shape[0]
    rows = T // GRID_W
    n_pairs = NA_WIDTH // 128

    def body(q_ref, k_ref, v_ref, tab_ref, do_ref, dq_ref, dk_ref, dv_ref, dtab_ref):
        dk_ref[...] = jnp.zeros_like(dk_ref)
        dv_ref[...] = jnp.zeros_like(dv_ref)
        dtab_ref[...] = jnp.zeros_like(dtab_ref)

        def step(it, carry):
            U = NA_BWD_ROWS_PER_STEP
            geo = [_na_row_geometry(it * U + u, rows) for u in range(U)]
            q0s = [pl.multiple_of((it * U + u) * GRID_W, GRID_W) for u in range(U)]
            q2s = [jnp.concatenate(_split_pair(q_ref[pl.ds(q0, GRID_W), :]), axis=0) for q0 in q0s]
            do2s = [jnp.concatenate(_split_pair(do_ref[pl.ds(q0, GRID_W), :]), axis=0) for q0 in q0s]
            ss = [lax.dot_general(q2, k_ref[pl.ds(k0, NA_KEYS), :], _NT, preferred_element_type=F32) for q2, (k0, _) in zip(q2s, geo)]
            dps = [lax.dot_general(do2, v_ref[pl.ds(k0, NA_KEYS), :], _NT, preferred_element_type=F32) for do2, (k0, _) in zip(do2s, geo)]
            ps = [_softmax_rows(s + jnp.concatenate([tab_ref[0, base], tab_ref[1, base]], axis=0)) for s, (_, base) in zip(ss, geo)]
            dss = [p * (dp - jnp.sum(dp * p, axis=-1, keepdims=True)) for p, dp in zip(ps, dps)]
            dvs = [lax.dot_general(p.astype(BF), do2, _TN, preferred_element_type=F32) for p, do2 in zip(ps, do2s)]
            dsbs = [ds.astype(BF) for ds in dss]
            dqs = [jnp.dot(dsb, k_ref[pl.ds(k0, NA_KEYS), :], preferred_element_type=F32) for dsb, (k0, _) in zip(dsbs, geo)]
            dks = [lax.dot_general(dsb, q2, _TN, preferred_element_type=F32) for dsb, q2 in zip(dsbs, q2s)]
            for u in range(U):
                k0, base = geo[u]
                dtab_ref[0, base] += dss[u][:GRID_W]
                dtab_ref[1, base] += dss[u][GRID_W:]
                dq_ref[pl.ds(q0s[u], GRID_W), :] = _join_pair(dqs[u][:GRID_W], dqs[u][GRID_W:])
                dk_ref[pl.ds(k0, NA_KEYS), :] += dks[u]
                dv_ref[pl.ds(k0, NA_KEYS), :] += dvs[u]
            return carry

        lax.fori_loop(0, rows // NA_BWD_ROWS_PER_STEP, step, 0)

    def cols(first):
        return pl.BlockSpec((T, 128), lambda j: (0, first + j))

    tabs = pl.BlockSpec((2, NA_BASES, GRID_W, NA_KEYS), lambda j: (j, 0, 0, 0))
    wide = jax.ShapeDtypeStruct((T, NA_WIDTH), F32)
    return pl.pallas_call(
        body, name="na_bwd", grid=(n_pairs,),
        in_specs=[cols(0), cols(n_pairs), cols(2 * n_pairs), tabs, cols(0)],
        out_specs=[cols(0), cols(0), cols(0), tabs],
        out_shape=[wide, wide, wide, jax.ShapeDtypeStruct((NA_HEADS, NA_BASES, GRID_W, NA_KEYS), F32)],
        compiler_params=_cparams(("parallel",)),
    )(qkv, qkv, qkv, tab, do)


def _na_geometry_np():
    c = np.arange(GRID_W)
    cs = np.clip(c - NA_WIN_COLS // 2, 0, GRID_W - NA_WIN_COLS)
    valid = (c[None, :] >= cs[:, None]) & (c[None, :] < cs[:, None] + NA_WIN_COLS)
    off = c[None, :] - c[:, None] + (NA_WIN_COLS - 1)
    oh_col = np.zeros((GRID_W, GRID_W, 2 * NA_WIN_COLS - 1), np.float32)
    qq, kk = np.nonzero(valid)
    oh_col[qq, kk, off[qq, kk]] = 1.0
    oh_row = np.zeros((NA_BASES, NA_WIN_ROWS, 2 * NA_WIN_ROWS - 1), np.float32)
    for base in range(NA_BASES):
        for i in range(NA_WIN_ROWS):
            oh_row[base, i, base + i] = 1.0
    return valid, oh_col, oh_row


def _na_bias_table(rpb):
    valid, oh_col, oh_row = _na_geometry_np()
    hi = lax.Precision.HIGHEST
    t1 = jnp.einsum("hrc,pir->hpic", rpb, oh_row, precision=hi)
    tab = jnp.einsum("hpic,qkc->hpqik", t1, oh_col, precision=hi)
    tab = jnp.where(valid[None, None, :, None, :], tab, NEG_INF)
    return tab.reshape(rpb.shape[0], NA_BASES, GRID_W, NA_KEYS)


def _na_rpb_grad(dtab):
    H = dtab.shape[0]
    n_rows = 2 * NA_WIN_ROWS - 1
    n_cols = 2 * NA_WIN_COLS - 1

    def body(d_ref, o_ref):
        lane = lax.broadcasted_iota(jnp.int32, (GRID_W, 128), 1)
        low = lane < GRID_W
        out_rows = []
        for ro in range(n_rows):
            acc = jnp.zeros((GRID_W, 128), F32)
            for base in range(NA_BASES):
                i = ro - base
                if not 0 <= i < NA_WIN_ROWS:
                    continue
                pair = d_ref[base, :, pl.ds((i // 2) * 128, 128)]
                if i % 2:
                    pair = pltpu.roll(pair, GRID_W, 1)
                acc = acc + jnp.where(low, pair, 0.0)
            skew = pltpu.roll(acc, 0, 1, stride=1, stride_axis=0)
            diag = jnp.sum(skew, axis=0, keepdims=True)
            out_rows.append(pltpu.roll(jnp.broadcast_to(diag, (8, 128)), 128 - (GRID_W - NA_WIN_COLS), 1)[:1])
        out_rows.append(jnp.zeros((1, 128), F32))
        res = jnp.concatenate(out_rows, axis=0)
        o_ref[...] = jnp.where(lax.broadcasted_iota(jnp.int32, res.shape, 1) < n_cols, res, 0.0)

    return pl.pallas_call(
        body, name="na_rpb_grad", grid=(H,),
        in_specs=[pl.BlockSpec((None, NA_BASES, GRID_W, NA_KEYS), lambda h: (h, 0, 0, 0))],
        out_specs=pl.BlockSpec((None, n_rows + 1, 128), lambda h: (h, 0, 0)),
        out_shape=jax.ShapeDtypeStruct((H, n_rows + 1, 128), F32),
        compiler_params=_cparams(("parallel",)),
    )(jnp.flip(dtab, axis=2))


BAND_Q = 128
BAND_KEYS = BAND_Q + 2 * DIL_RADIUS


def _band_geometry(n, L):
    q0 = pl.multiple_of(n * BAND_Q, BAND_Q)
    k0 = pl.multiple_of(jnp.clip(q0 - DIL_RADIUS, 0, L - BAND_KEYS), DIL_RADIUS)
    qi = q0 + lax.broadcasted_iota(jnp.int32, (BAND_Q, BAND_KEYS), 0)
    kj = k0 + lax.broadcasted_iota(jnp.int32, (BAND_Q, BAND_KEYS), 1)
    return q0, k0, jnp.abs(qi - kj) <= DIL_RADIUS


DIL_PAIRS = DIL_OUT_WIDTH // 128


def _residue_shape(dil, T, dtype):
    return jax.ShapeDtypeStruct((DIL_PAIRS, dil, T // dil, 128), dtype)


def _residue_tile(dil, tm):
    return pl.BlockSpec((DIL_PAIRS, dil, tm // dil, 128), lambda i: (0, 0, i, 0))


def _to_natural(ref, scratch, dil, tm):
    tiles = []
    for pair in range(DIL_PAIRS):
        if dil == 1:
            tiles.append(ref[pair, 0].astype(F32))
            continue
        for r in range(dil):
            scratch[pl.ds(r, tm // dil, stride=dil), :] = ref[pair, r].astype(F32)
        tiles.append(scratch[...])
    return tiles


def _from_natural(tile, scratch, ref, pair, dil, tm):
    if dil == 1:
        ref[pair, 0] = tile.astype(ref.dtype)
        return
    scratch[...] = tile
    for r in range(dil):
        ref[pair, r] = scratch[pl.ds(r, tm // dil, stride=dil), :].astype(ref.dtype)


def _band_specs(group, T):
    dil = DIL_GROUPS[group][1]
    L = T // dil
    assert L % BAND_Q == 0 and L >= BAND_KEYS, (T, dil)
    return L, (dil * DIL_PAIRS,), pl.BlockSpec((None, None, L, 128), lambda s: (s % DIL_PAIRS, s // DIL_PAIRS, 0, 0))


BAND_BLOCKS_PER_STEP = 4


def _band_softmax(s, valid):
    s = jnp.where(valid, s, NEG_INF)
    m = jnp.max(s, axis=-1, keepdims=True)
    p = jnp.exp(s - m)
    l = jnp.sum(p, axis=-1, keepdims=True)
    return p / l, m + jnp.log(l)


def _band_fwd(q, k, v, group):
    T = q.shape[1] * q.shape[2]
    L, grid, spec = _band_specs(group, T)
    U = min(BAND_BLOCKS_PER_STEP, L // BAND_Q)

    def body(q_ref, k_ref, v_ref, o_ref, lse_ref):
        def step(it, carry):
            geo = [_band_geometry(it * U + u, L) for u in range(U)]
            ss = [lax.dot_general(jnp.concatenate(_split_pair(q_ref[pl.ds(q0, BAND_Q), :]), axis=0),
                                  k_ref[pl.ds(k0, BAND_KEYS), :], _NT, preferred_element_type=F32) for q0, k0, _ in geo]
            pls = [_band_softmax(s, jnp.concatenate([valid, valid], axis=0)) for s, (_, _, valid) in zip(ss, geo)]
            os = [jnp.dot(p.astype(BF), v_ref[pl.ds(k0, BAND_KEYS), :], preferred_element_type=F32) for (p, _), (_, k0, _) in zip(pls, geo)]
            for (q0, _, _), o2, (_, lse) in zip(geo, os, pls):
                o_ref[pl.ds(q0, BAND_Q), :] = _join_pair(o2[:BAND_Q], o2[BAND_Q:])
                lse2 = jnp.broadcast_to(lse, (2 * BAND_Q, 128))
                lse_ref[pl.ds(q0, BAND_Q), :] = _join_pair(lse2[:BAND_Q], lse2[BAND_Q:])
            return carry

        lax.fori_loop(0, L // (BAND_Q * U), step, 0)

    res = _residue_shape(DIL_GROUPS[group][1], T, F32)
    return pl.pallas_call(
        body, name=f"band_fwd_g{group}", grid=grid,
        in_specs=[spec] * 3, out_specs=[spec] * 2, out_shape=[res, res],
        compiler_params=_cparams(("parallel",)),
    )(q, k, v)


def _band_bwd(q, k, v, do, dlse, group):
    T = q.shape[1] * q.shape[2]
    L, grid, spec = _band_specs(group, T)
    U = min(BAND_BLOCKS_PER_STEP, L // BAND_Q)

    def body(q_ref, k_ref, v_ref, do_ref, dlse_ref, dq_ref, dk_ref, dv_ref):
        dk_ref[...] = jnp.zeros_like(dk_ref)
        dv_ref[...] = jnp.zeros_like(dv_ref)

        def step(it, carry):
            geo = [_band_geometry(it * U + u, L) for u in range(U)]
            q2s = [jnp.concatenate(_split_pair(q_ref[pl.ds(q0, BAND_Q), :]), axis=0) for q0, _, _ in geo]
            do2s = [jnp.concatenate(_split_pair(do_ref[pl.ds(q0, BAND_Q), :]), axis=0) for q0, _, _ in geo]
            ss = [lax.dot_general(q2, k_ref[pl.ds(k0, BAND_KEYS), :], _NT, preferred_element_type=F32) for q2, (_, k0, _) in zip(q2s, geo)]
            dps = [lax.dot_general(do2, v_ref[pl.ds(k0, BAND_KEYS), :], _NT, preferred_element_type=F32) for do2, (_, k0, _) in zip(do2s, geo)]
            ps = [_band_softmax(s, jnp.concatenate([valid, valid], axis=0))[0] for s, (_, _, valid) in zip(ss, geo)]
            dss = []
            for p, dp, (q0, _, _) in zip(ps, dps, geo):
                dl = dlse_ref[pl.ds(q0, BAND_Q), :]
                dl2 = jnp.concatenate([dl[:, :1], dl[:, HEAD_DIM:HEAD_DIM + 1]], axis=0)
                dss.append(p * (dp - jnp.sum(dp * p, axis=-1, keepdims=True) + dl2))
            dvs = [lax.dot_general(p.astype(BF), do2, _TN, preferred_element_type=F32) for p, do2 in zip(ps, do2s)]
            dsbs = [ds.astype(BF) for ds in dss]
            dqs = [jnp.dot(dsb, k_ref[pl.ds(k0, BAND_KEYS), :], preferred_element_type=F32) for dsb, (_, k0, _) in zip(dsbs, geo)]
            dks = [lax.dot_general(dsb, q2, _TN, preferred_element_type=F32) for dsb, q2 in zip(dsbs, q2s)]
            for u, (q0, k0, _) in enumerate(geo):
                dq_ref[pl.ds(q0, BAND_Q), :] = _join_pair(dqs[u][:BAND_Q], dqs[u][BAND_Q:])
                dk_ref[pl.ds(k0, BAND_KEYS), :] += dks[u]
                dv_ref[pl.ds(k0, BAND_KEYS), :] += dvs[u]
            return carry

        lax.fori_loop(0, L // (BAND_Q * U), step, 0)

    res = _residue_shape(DIL_GROUPS[group][1], T, F32)
    return pl.pallas_call(
        body, name=f"band_bwd_g{group}", grid=grid,
        in_specs=[spec] * 5, out_specs=[spec] * 3, out_shape=[res] * 3,
        compiler_params=_cparams(("parallel",)),
    )(q, k, v, do, dlse)


def _head_sums(t):
    head = lax.broadcasted_iota(jnp.int32, t.shape, 1) // HEAD_DIM
    out = jnp.zeros_like(t)
    for h in range(t.shape[1] // HEAD_DIM):
        mine = head == h
        out = jnp.where(mine, jnp.sum(jnp.where(mine, t, 0.0), axis=-1, keepdims=True), out)
    return out


def _dil_merge_fwd(os, lses, T, tm):
    G = len(DIL_GROUPS)
    W = DIL_OUT_WIDTH
    dils = [d for _, d in DIL_GROUPS]

    def body(*refs):
        o_refs, lse_refs = refs[:G], refs[G:2 * G]
        y_ref, w_refs, on_refs, scratch = refs[2 * G], refs[2 * G + 1:3 * G + 1], refs[3 * G + 1:4 * G + 1], refs[-1]
        o = [jnp.concatenate(_to_natural(r, scratch, d, tm), axis=1) for r, d in zip(o_refs, dils)]
        ls = [jnp.concatenate(_to_natural(r, scratch, d, tm), axis=1) for r, d in zip(lse_refs, dils)]
        m = functools.reduce(jnp.maximum, ls)
        es = [jnp.exp(l - m) for l in ls]
        tot = functools.reduce(jnp.add, es)
        ws = [e / tot for e in es]
        y_ref[...] = functools.reduce(jnp.add, [w * t for w, t in zip(ws, o)]).astype(y_ref.dtype)
        for g in range(G):
            w_refs[g][...] = ws[g]
            on_refs[g][...] = o[g]

    nat = pl.BlockSpec((tm, W), lambda i: (i, 0))
    res = pl.pallas_call(
        body, name="dil_merge_fwd", grid=(T // tm,),
        in_specs=[_residue_tile(d, tm) for d in dils] * 2,
        out_specs=[nat] * (2 * G + 1),
        out_shape=[jax.ShapeDtypeStruct((T, W), BF)] + [jax.ShapeDtypeStruct((T, W), F32)] * (2 * G),
        scratch_shapes=[pltpu.VMEM((tm, 128), F32)],
        compiler_params=_cparams(("parallel",)),
    )(*os, *lses)
    return res[0], res[1:G + 1], res[G + 1:]


def _dil_merge_bwd(dy, os, ws, tm):
    G = len(DIL_GROUPS)
    T, W = dy.shape
    dils = [d for _, d in DIL_GROUPS]

    def body(*refs):
        dyt = refs[0][...]
        o, w = [r[...] for r in refs[1:G + 1]], [r[...] for r in refs[G + 1:2 * G + 1]]
        do_refs, dlse_refs, scratch = refs[2 * G + 1:3 * G + 1], refs[3 * G + 1:4 * G + 1], refs[-1]
        dws = [_head_sums(dyt * t) for t in o]
        mean = functools.reduce(jnp.add, [a * b for a, b in zip(w, dws)])
        for g, d in enumerate(dils):
            do, dlse = w[g] * dyt, w[g] * (dws[g] - mean)
            for pair in range(DIL_PAIRS):
                cols = slice(pair * 128, (pair + 1) * 128)
                _from_natural(do[:, cols], scratch, do_refs[g], pair, d, tm)
                _from_natural(dlse[:, cols], scratch, dlse_refs[g], pair, d, tm)

    nat = pl.BlockSpec((tm, W), lambda i: (i, 0))
    res = pl.pallas_call(
        body, name="dil_merge_bwd", grid=(T // tm,),
        in_specs=[nat] * (2 * G + 1),
        out_specs=[_residue_tile(d, tm) for d in dils] * 2,
        out_shape=[_residue_shape(d, T, BF) for d in dils] + [_residue_shape(d, T, F32) for d in dils],
        scratch_shapes=[pltpu.VMEM((tm, 128), F32)],
        compiler_params=_cparams(("parallel",)),
    )(dy, *os, *ws)
    return res[:G], res[G:]


def _qkv_prep(z, cos2, sin_signed, tm):
    T = z.shape[0]
    G = len(DIL_GROUPS)
    dils = [d for _, d in DIL_GROUPS]
    n3 = 3 * NA_WIDTH
    n_dil_blocks = 3 * DIL_WIDTH // 128

    def body(*refs):
        na_ref, blocks = refs[0], refs[1:1 + n_dil_blocks]
        cos_ref, sin_ref = refs[1 + n_dil_blocks], refs[2 + n_dil_blocks]
        na_out, outs = refs[3 + n_dil_blocks], refs[4 + n_dil_blocks:]
        na = na_ref[...]
        na_out[...] = jnp.concatenate([na[:, :NA_WIDTH] * Q_SCALE, na[:, NA_WIDTH:]], axis=1).astype(na_out.dtype)
        for part in range(3):
            for g, d in enumerate(dils):
                out = outs[g * 3 + part]
                for pair in range(DIL_PAIRS):
                    blk = blocks[part * (DIL_WIDTH // 128) + g * DIL_PAIRS + pair]
                    for r in range(d):
                        rows = pl.ds(r, tm // d, stride=d) if d > 1 else slice(None)
                        x = blk[rows, :]
                        if part < 2:
                            x = _rope(x, cos_ref[rows, :], sin_ref[rows, :])
                        if part == 0:
                            x = x * Q_SCALE
                        out[pair, r] = x.astype(out.dtype)

    lane_block = [pl.BlockSpec((tm, 128), functools.partial(lambda b, i: (i, b), n3 // 128 + b)) for b in range(n_dil_blocks)]
    tab = pl.BlockSpec((tm, 128), lambda i: (i, 0))
    res = pl.pallas_call(
        body, name="qkv_prep", grid=(T // tm,),
        in_specs=[pl.BlockSpec((tm, n3), lambda i: (i, 0))] + lane_block + [tab, tab],
        out_specs=[pl.BlockSpec((tm, n3), lambda i: (i, 0))] + [_residue_tile(d, tm) for d in dils for _ in range(3)],
        out_shape=[jax.ShapeDtypeStruct((T, n3), BF)] + [_residue_shape(d, T, BF) for d in dils for _ in range(3)],
        compiler_params=_cparams(("parallel",)),
    )(z, *[z] * n_dil_blocks, cos2, sin_signed)
    return res[0], [res[1 + 3 * g:4 + 3 * g] for g in range(G)]


def _qkv_unprep(d_na, d_dil, cos2, sin_signed, tm, after=()):
    T = d_na[0].shape[0]
    G = len(DIL_GROUPS)
    dils = [d for _, d in DIL_GROUPS]
    n_after = len(after)

    def body(*refs):
        dq, dk, dv = (r[...] for r in refs[:3])
        res_refs = refs[3:3 + 3 * G]
        cs, sn = refs[3 + 3 * G][...], refs[4 + 3 * G][...]
        out, scratch = refs[5 + 3 * G + n_after], refs[-1]
        cols = [dq * Q_SCALE, dk, dv]
        for part in range(3):
            for g, d in enumerate(dils):
                for x in _to_natural(res_refs[g * 3 + part], scratch, d, tm):
                    if part < 2:
                        x = _rope(x, cs, -sn)
                    cols.append(x * Q_SCALE if part == 0 else x)
        out[...] = jnp.concatenate(cols, axis=1).astype(out.dtype)

    wide = pl.BlockSpec((tm, NA_WIDTH), lambda i: (i, 0))
    tab = pl.BlockSpec((tm, 128), lambda i: (i, 0))
    return pl.pallas_call(
        body, name="qkv_unprep", grid=(T // tm,),
        in_specs=[wide] * 3 + [_residue_tile(d, tm) for d in dils for _ in range(3)] + [tab, tab] + [ANY] * n_after,
        out_specs=pl.BlockSpec((tm, QKV_WIDTH), lambda i: (i, 0)),
        out_shape=jax.ShapeDtypeStruct((T, QKV_WIDTH), BF),
        scratch_shapes=[pltpu.VMEM((tm, 128), F32)],
        compiler_params=_cparams(("parallel",)),
    )(*d_na, *[t for g in range(G) for t in d_dil[g]], cos2, sin_signed, *after)


def _rope_tables(positions):
    half = HEAD_DIM // 2
    inv_freq = ROPE_THETA ** (-jnp.arange(half, dtype=F32) / half)
    ang = positions.astype(F32)[:, None] * inv_freq
    cos, sin = jnp.cos(ang), jnp.sin(ang)
    return jnp.tile(jnp.concatenate([cos, cos], axis=1), (1, 2)), jnp.tile(jnp.concatenate([-sin, sin], axis=1), (1, 2))


def _pack_rows(t):
    return t.reshape(-1, PACK_W)


def _me():
    return lax.axis_index("x"), lax.axis_index("y"), lax.axis_index("c")


def _other_chips(x, y):
    return [(1 - x, y), (x, 1 - y), (1 - x, 1 - y)]


def _gather_weights(packed):
    R, W = packed.shape
    half = R // 2

    def body(in_ref, out_ref, send_sems, recv_sems):
        x, y, c = _me()
        sibling = (x, y, 1 - c)
        chips = _other_chips(x, y)

        def block(chip, core):
            return out_ref.at[2 * chip[0] + chip[1], pl.ds(core * half, half), :]

        def copy(k, chip, core, to, src=None):
            return pltpu.make_async_remote_copy(
                src_ref=block(chip, core) if src is None else src, dst_ref=block(chip, core),
                send_sem=send_sems.at[k], recv_sem=recv_sems.at[k], device_id=to, device_id_type=MESH)

        first = [copy(j, (x, y), c, (*chip, c), src=in_ref.at[pl.ds(c * half, half), :]) for j, chip in enumerate(chips)]
        for cp in first:
            cp.start()
        passed = [copy(3 + j, chip, c, sibling) for j, chip in enumerate(chips)]
        for j, chip in enumerate(chips):
            copy(j, chip, c, (x, y, c)).wait_recv()
            passed[j].start()
        for j, chip in enumerate(chips):
            copy(3 + j, chip, 1 - c, (x, y, c)).wait_recv()
        for cp in first + passed:
            cp.wait_send()

    others = pl.pallas_call(
        body, name="gather_weights",
        in_specs=[ANY], out_specs=ANY,
        out_shape=jax.ShapeDtypeStruct((N_CHIPS, R, W), packed.dtype),
        scratch_shapes=[pltpu.SemaphoreType.DMA((6,)), pltpu.SemaphoreType.DMA((6,))],
    )(packed)
    return lax.dynamic_update_slice(others, packed[None], (2 * lax.axis_index("x") + lax.axis_index("y"), 0, 0))


def _swap_halves(g):
    S, R, W = g.shape
    half = R // 2

    def body(g_ref, out_ref, send_sem, recv_sem):
        x, y, c = _me()
        cp = pltpu.make_async_remote_copy(
            src_ref=g_ref.at[:, pl.ds((1 - c) * half, half), :], dst_ref=out_ref,
            send_sem=send_sem, recv_sem=recv_sem, device_id=(x, y, 1 - c), device_id_type=MESH)
        cp.start()
        cp.wait()

    return pl.pallas_call(
        body, name="swap_halves", in_specs=[ANY], out_specs=ANY,
        out_shape=jax.ShapeDtypeStruct((S, half, W), g.dtype),
        scratch_shapes=[pltpu.SemaphoreType.DMA, pltpu.SemaphoreType.DMA],
    )(g)


def _pair_sum(g, got, tm):
    S, R, W = g.shape
    half = R // 2
    nb = half // tm

    def body(c_ref, g_ref, got_ref, o_ref, ob_ref):
        tot = g_ref[...] + got_ref[...]
        o_ref[...] = tot
        ob_ref[...] = tot.astype(ob_ref.dtype)

    tile = pl.BlockSpec((None, tm, W), lambda s, i, c_ref: (s, i, 0))
    return pl.pallas_call(
        body, name="pair_sum",
        grid_spec=pltpu.PrefetchScalarGridSpec(
            num_scalar_prefetch=1, grid=(S, nb),
            in_specs=[pl.BlockSpec((None, tm, W), lambda s, i, c_ref: (s, c_ref[0] * nb + i, 0)), tile],
            out_specs=[tile, tile]),
        out_shape=[jax.ShapeDtypeStruct((S, half, W), F32), jax.ShapeDtypeStruct((S, half, W), BF)],
        compiler_params=_cparams(("parallel", "parallel")),
    )(lax.axis_index("c").reshape(1).astype(jnp.int32), g, got)


def _scatter_chips(part):
    S, h, W = part.shape

    def body(p_ref, out_ref, send_sems, recv_sems):
        x, y, c = _me()
        chips = _other_chips(x, y)
        sends = [pltpu.make_async_remote_copy(
            src_ref=p_ref.at[2 * chip[0] + chip[1]], dst_ref=out_ref.at[j],
            send_sem=send_sems.at[j], recv_sem=recv_sems.at[j], device_id=(*chip, c), device_id_type=MESH)
            for j, chip in enumerate(chips)]
        for cp in sends:
            cp.start()
        for cp in sends:
            cp.wait()

    return pl.pallas_call(
        body, name="scatter_chips", in_specs=[ANY], out_specs=ANY,
        out_shape=jax.ShapeDtypeStruct((S - 1, h, W), part.dtype),
        scratch_shapes=[pltpu.SemaphoreType.DMA((3,)), pltpu.SemaphoreType.DMA((3,))],
    )(part)


def _chip_sum(own, others, tm):
    n, h, W = others.shape

    def body(own_ref, p_ref, o_ref):
        o_ref[...] = ((own_ref[...] + p_ref[0].astype(F32)) + p_ref[1].astype(F32)) + p_ref[2].astype(F32)

    return pl.pallas_call(
        body, name="chip_sum", grid=(h // tm,),
        in_specs=[pl.BlockSpec((tm, W), lambda i: (i, 0)), pl.BlockSpec((n, tm, W), lambda i: (0, i, 0))],
        out_specs=pl.BlockSpec((tm, W), lambda i: (i, 0)),
        out_shape=jax.ShapeDtypeStruct((h, W), F32),
        compiler_params=_cparams(("parallel",)),
    )(own, others)


def _join_halves(mine):
    h, W = mine.shape

    def body(m_ref, out_ref, send_sem, recv_sem):
        x, y, c = _me()
        cp = pltpu.make_async_remote_copy(
            src_ref=m_ref, dst_ref=out_ref.at[pl.ds(c * h, h), :],
            send_sem=send_sem, recv_sem=recv_sem, device_id=(x, y, 1 - c), device_id_type=MESH)
        cp.start()
        pltpu.make_async_remote_copy(
            src_ref=m_ref, dst_ref=out_ref.at[pl.ds((1 - c) * h, h), :],
            send_sem=send_sem, recv_sem=recv_sem, device_id=(x, y, 1 - c), device_id_type=MESH).wait_recv()
        cp.wait_send()

    other = pl.pallas_call(
        body, name="join_halves", in_specs=[ANY], out_specs=ANY,
        out_shape=jax.ShapeDtypeStruct((2 * h, W), mine.dtype),
        scratch_shapes=[pltpu.SemaphoreType.DMA, pltpu.SemaphoreType.DMA],
    )(mine)
    return lax.dynamic_update_slice(other, mine, (lax.axis_index("c") * h, 0))


def _allreduce_small(s):
    R, W = s.shape

    def body(s_ref, o_ref, buf, send_sems, recv_sems):
        x, y, c = _me()
        me = 4 * x + 2 * y + c
        buf[me] = s_ref[...]
        peers = [((x + fx) % 2, (y + fy) % 2, (c + fc) % 2) for fx in range(2) for fy in range(2) for fc in range(2)][1:]
        sends = [pltpu.make_async_remote_copy(
            src_ref=s_ref, dst_ref=buf.at[me], send_sem=send_sems.at[k], recv_sem=recv_sems.at[k],
            device_id=peer, device_id_type=MESH) for k, peer in enumerate(peers)]
        for cp in sends:
            cp.start()
        for k, peer in enumerate(peers):
            pltpu.make_async_remote_copy(
                src_ref=s_ref, dst_ref=buf.at[4 * peer[0] + 2 * peer[1] + peer[2]], send_sem=send_sems.at[k],
                recv_sem=recv_sems.at[k], device_id=peer, device_id_type=MESH).wait_recv()
        for cp in sends:
            cp.wait_send()
        total = buf[0]
        for d in range(1, N_DEV):
            total = total + buf[d]
        o_ref[...] = total

    return pl.pallas_call(
        body, name="allreduce_small",
        in_specs=[pl.BlockSpec(memory_space=pltpu.VMEM)], out_specs=pl.BlockSpec(memory_space=pltpu.VMEM),
        out_shape=jax.ShapeDtypeStruct((R, W), F32),
        scratch_shapes=[pltpu.VMEM((N_DEV, R, W), F32), pltpu.SemaphoreType.DMA((N_DEV - 1,)), pltpu.SemaphoreType.DMA((N_DEV - 1,))],
    )(s)


HBM_SPEC = pl.BlockSpec(memory_space=pltpu.HBM)
SEM_SPEC = pl.BlockSpec(memory_space=pltpu.SEMAPHORE)
DATAFLOW = pltpu.SideEffectType.DATAFLOW_SIDE_EFFECTING


class _InFlight(NamedTuple):
    sems: tuple
    src: jax.Array
    land: jax.Array
    token: jax.Array


def _split_start(name, src, land_shape, land_dtype, n, copies):
    def body(src_ref, land_ref, *rest):
        sems, token = rest[:2 * n], rest[-1]
        for k, (s, d, peer) in enumerate(copies(src_ref, land_ref)):
            pltpu.make_async_remote_copy(src_ref=s, dst_ref=d, send_sem=sems[k], recv_sem=sems[n + k],
                                         device_id=peer, device_id_type=MESH).start()
        token[...] = jnp.zeros_like(token)

    outs = pl.pallas_call(
        body, name=name,
        out_shape=(*[pltpu.SemaphoreType.DMA(())] * (2 * n), pltpu.HBM(src.shape, src.dtype), pltpu.HBM(land_shape, land_dtype),
                   jax.ShapeDtypeStruct((8, 128), F32)),
        in_specs=(HBM_SPEC, HBM_SPEC),
        out_specs=(*[SEM_SPEC] * (2 * n), HBM_SPEC, HBM_SPEC, pl.BlockSpec(memory_space=pltpu.VMEM)),
        input_output_aliases={0: 2 * n, 1: 2 * n + 1},
        compiler_params=pltpu.CompilerParams(has_side_effects=DATAFLOW),
    )(pltpu.with_memory_space_constraint(src, pltpu.HBM), pltpu.with_memory_space_constraint(lax.empty(land_shape, land_dtype), pltpu.HBM))
    return _InFlight(tuple(outs[:2 * n]), outs[2 * n], outs[2 * n + 1], outs[2 * n + 2])


def _split_wait(name, flight, after, n, copies):
    def body(src_ref, land_ref, *rest):
        sems = rest[:2 * n]
        for k, (s, d, peer) in enumerate(copies(src_ref, land_ref)):
            cp = pltpu.make_async_remote_copy(src_ref=s, dst_ref=d, send_sem=sems[k], recv_sem=sems[n + k],
                                              device_id=peer, device_id_type=MESH)
            cp.wait_send()
            cp.wait_recv()

    return pl.pallas_call(
        body, name=name,
        out_shape=(pltpu.HBM(flight.src.shape, flight.src.dtype), pltpu.HBM(flight.land.shape, flight.land.dtype)),
        in_specs=(HBM_SPEC, HBM_SPEC, *[SEM_SPEC] * (2 * n), ANY),
        out_specs=(HBM_SPEC, HBM_SPEC), input_output_aliases={0: 0, 1: 1},
        compiler_params=pltpu.CompilerParams(has_side_effects=DATAFLOW),
    )(flight.src, flight.land, *flight.sems, after)


def _gather_copies(src_ref, land_ref):
    x, y, c = _me()
    return [(src_ref, land_ref.at[2 * x + y], (*chip, c)) for chip in _other_chips(x, y)]


def _gather_start(packed):
    return _split_start("gather_rest_start", packed, (N_CHIPS, *packed.shape), packed.dtype, 3, _gather_copies)


def _gather_wait(flight, after):
    src, others = _split_wait("gather_rest_wait", flight, after, 3, _gather_copies)
    return lax.dynamic_update_slice(others, src[None], (2 * lax.axis_index("x") + lax.axis_index("y"), 0, 0))


def _swap_copies(src_ref, land_ref):
    x, y, c = _me()
    half = land_ref.shape[1]
    return [(src_ref.at[:, pl.ds((1 - c) * half, half), :], land_ref, (x, y, 1 - c))]


def _swap_start(g, tag):
    S, R, W = g.shape
    return _split_start(f"swap_halves_start_{tag}", g, (S, R // 2, W), g.dtype, 1, _swap_copies)


def _swap_wait(flight, after, tag):
    return _split_wait(f"swap_halves_wait_{tag}", flight, after, 1, _swap_copies)


def _scatter_copies(src_ref, land_ref):
    x, y, c = _me()
    return [(src_ref.at[2 * chip[0] + chip[1]], land_ref.at[j], (*chip, c)) for j, chip in enumerate(_other_chips(x, y))]


def _scatter_start(part, tag):
    S, h, W = part.shape
    return _split_start(f"scatter_chips_start_{tag}", part, (S - 1, h, W), part.dtype, 3, _scatter_copies)


def _scatter_wait(flight, after, tag):
    return _split_wait(f"scatter_chips_wait_{tag}", flight, after, 3, _scatter_copies)[1]


def _join_copies(src_ref, land_ref):
    x, y, c = _me()
    h = src_ref.shape[0]
    return [(src_ref, land_ref.at[pl.ds(c * h, h), :], (x, y, 1 - c))]


def _join_start(mine):
    h, W = mine.shape
    return _split_start("join_halves_start", mine, (2 * h, W), mine.dtype, 1, _join_copies)


def _join_wait(flight, after):
    src, other = _split_wait("join_halves_wait", flight, after, 1, _join_copies)
    return lax.dynamic_update_slice(other, src, (lax.axis_index("c") * src.shape[0], 0))


def _adamw(name, g, w, m, v):
    R, C = w.shape
    tm = R
    for cand in (256, 128, 64, 32, 16, 8):
        if R % cand == 0:
            tm = cand
            break

    def body(g, w, m, v):
        m = ADAM_B1 * m + (1.0 - ADAM_B1) * g
        v = ADAM_B2 * v + (1.0 - ADAM_B2) * jnp.square(g)
        m_hat = m / (1.0 - ADAM_B1 ** ADAM_STEP)
        v_hat = v / (1.0 - ADAM_B2 ** ADAM_STEP)
        delta = -ADAM_LR * (m_hat / (jnp.sqrt(v_hat) + ADAM_EPS) + ADAM_WD * w)
        return delta, m, v

    return _rowwise(name, body, R, tm, [_row(t, tm) for t in (g, w, m, v)], [(C, F32)] * 3)


def _unpack_weights(gathered, names):
    S = gathered.shape[0]
    shard_shapes = {"w_in": (D_MODEL, (QKV_WIDTH + 2 * D_MODEL) // S), "w_branch_na": (NA_WIDTH, D_MODEL // S),
                    "w_branch_dil": (DIL_OUT_WIDTH, D_MODEL // S), "w_out": (D_MODEL // S, D_MODEL),
                    "w_up": (D_MODEL, D_FF // S), "w_down": (D_FF // S, D_MODEL),
                    "w_ple_gate": (D_MODEL // S, D_MODEL), "w_ple_proj": (PLE_DIM, D_MODEL // S)}
    col_sharded = {"w_in", "w_branch_na", "w_branch_dil", "w_up", "w_ple_proj"}
    out, r0 = {}, 0
    for name in names:
        rows, cols = shard_shapes[name]
        n = rows * cols // PACK_W
        t = gathered[:, r0:r0 + n, :].reshape(S, rows, cols)
        r0 += n
        out[name] = t.transpose(1, 0, 2).reshape(rows, S * cols) if name in col_sharded else t.reshape(S * rows, cols)
    return out


def _pack_grads(grads, names):
    col_sharded = {"w_in", "w_branch_na", "w_branch_dil", "w_up", "w_ple_proj"}
    per_chip = []
    for s in range(N_CHIPS):
        rows = []
        for name in names:
            g = grads[name]
            if name in col_sharded:
                w = g.shape[1] // N_CHIPS
                rows.append(_pack_rows(g[:, s * w:(s + 1) * w]))
            else:
                h = g.shape[0] // N_CHIPS
                rows.append(_pack_rows(g[s * h:(s + 1) * h]))
        per_chip.append(jnp.concatenate(rows, axis=0))
    return jnp.stack(per_chip)


def _unpack_shard(packed, shapes, names):
    out, r0 = {}, 0
    for name in names:
        rows, cols = shapes[name]
        n = rows * cols // PACK_W
        out[name] = packed[r0:r0 + n].reshape(rows, cols)
        r0 += n
    return out


def kernel(x, p, positions, g_mix, w_in, rpb, w_branch_na, w_branch_dil, w_out, g_mlp, w_up, w_down, g_ple, w_ple_gate, w_ple_proj, g_final, loss_target, m_g_mix, m_w_in, m_rpb, m_w_branch_na, m_w_branch_dil, m_w_out, m_g_mlp, m_w_up, m_w_down, m_g_ple, m_w_ple_gate, m_w_ple_proj, m_g_final, v_g_mix, v_w_in, v_rpb, v_w_branch_na, v_w_branch_dil, v_w_out, v_g_mlp, v_w_up, v_w_down, v_g_ple, v_w_ple_gate, v_w_ple_proj, v_g_final):
    shards = {"w_in": w_in[0], "w_branch_na": w_branch_na[0], "w_branch_dil": w_branch_dil[0], "w_out": w_out[0],
              "w_up": w_up[0], "w_down": w_down[0], "w_ple_gate": w_ple_gate[0], "w_ple_proj": w_ple_proj[0]}
    m_shards = {"w_in": m_w_in[0], "w_branch_na": m_w_branch_na[0], "w_branch_dil": m_w_branch_dil[0], "w_out": m_w_out[0],
                "w_up": m_w_up[0], "w_down": m_w_down[0], "w_ple_gate": m_w_ple_gate[0], "w_ple_proj": m_w_ple_proj[0]}
    v_shards = {"w_in": v_w_in[0], "w_branch_na": v_w_branch_na[0], "w_branch_dil": v_w_branch_dil[0], "w_out": v_w_out[0],
                "w_up": v_w_up[0], "w_down": v_w_down[0], "w_ple_gate": v_w_ple_gate[0], "w_ple_proj": v_w_ple_proj[0]}

    W = _unpack_weights(_gather_weights(_pack_rows(shards["w_in"].astype(BF))), GATHER_FIRST)
    rest_flight = _gather_start(jnp.concatenate([_pack_rows(shards[n].astype(BF)) for n in GATHER_REST], axis=0))
    w_qkv, w_gates = W["w_in"][:, :QKV_WIDTH], W["w_in"][:, QKV_WIDTH:]

    xs, ps, tgt = x[0], p[0, 0], loss_target[0]
    T = xs.shape[0]
    TM = 256
    gm, gl, gp, gf = g_mix, g_mlp, g_ple, g_final.reshape(1, D_MODEL)
    cos2, sin_signed = _rope_tables(positions[0])
    tab = _na_bias_table(rpb[0])

    a = _rowwise("norm_mix", lambda h, g: h * _rms(h) * g, T, TM, [_row(xs, TM), _full(gm)], [(D_MODEL, BF)],
                 after=(rest_flight.token,))
    z_qkv = _mm("in_qkv", a, w_qkv, "nn", 1024,1280, 1024, [F32])
    z_gates = _mm("in_gates", a, w_gates, "nn", 1024,1024, 1024, [F32])

    qkv, dil_ops = _qkv_prep(z_qkv, cos2, sin_signed, TM)
    y_na = _na_fwd(qkv, tab)
    band = [_band_fwd(*dil_ops[g], g) for g in range(len(DIL_GROUPS))]
    y_dil, w_grp, o_nat = _dil_merge_fwd([b[0] for b in band], [b[1] for b in band], T, TM)

    W.update(_unpack_weights(_gather_wait(rest_flight, y_dil), GATHER_REST))
    u_na = _mm("branch_na", y_na, W["w_branch_na"], "nn", 1024,1024, 512, [F32])
    u_dil = _mm("branch_dil", y_dil, W["w_branch_dil"], "nn", 1024,1024, 256, [F32])
    mixed = _rowwise(
        "gate_mix", lambda gn, gd, un, ud: _sigmoid(gn) * un + _sigmoid(gd) * ud, T, TM,
        [_row(z_gates, TM, 0, D_MODEL), _row(z_gates, TM, 1, D_MODEL), _row(u_na, TM), _row(u_dil, TM)], [(D_MODEL, BF)])
    mix_out = _mm("out_proj", mixed, W["w_out"], "nn", 1024,1024, 1024, [F32])

    def add_norm(h, d, g):
        h = h + d
        return h, h * _rms(h) * g

    h1, cn = _rowwise("add_norm_mlp", add_norm, T, TM, [_row(xs, TM), _row(mix_out, TM), _full(gl)], [(D_MODEL, F32), (D_MODEL, BF)])
    up, act = _mm("mlp_up", cn, W["w_up"], "nn", 1024,1024, 1024, [F32, BF],
                  epilogue=lambda acc: (acc, jnp.square(jnp.maximum(acc, 0.0))))
    mlp_out = _mm("mlp_down", act, W["w_down"], "nn", 1024,1024, 1024, [F32])
    h2, en = _rowwise("add_norm_ple", add_norm, T, TM, [_row(h1, TM), _row(mlp_out, TM), _full(gp)], [(D_MODEL, F32), (D_MODEL, BF)])
    gt = _mm("ple_gate", en, W["w_ple_gate"], "nn", 1024,1024, 1024, [F32])
    pp = _mm("ple_proj", ps, W["w_ple_proj"], "nn", 1024,1024, 256, [F32])

    def head(h2t, gtt, ppt, tg, g):
        sg = _sigmoid(gtt)
        h3 = h2t + sg * ppt
        yo = h3 * _rms(h3) * g
        diff = yo - tg
        loss = 0.5 * jnp.sum(jnp.mean(jnp.square(diff), axis=-1, keepdims=True), axis=0, keepdims=True)
        dh3, dg = _rms_bwd(diff * (1.0 / D_MODEL), h3, g)
        return dh3, dh3 * ppt * sg * (1.0 - sg), dh3 * sg, jnp.broadcast_to(loss, (1, 128)), dg

    dh3, d_gt, d_pp, loss_part, dg_final = _rowwise(
        "loss_head", head, T, TM, [_row(h2, TM), _row(gt, TM), _row(pp, TM), _row(tgt, TM), _full(gf)],
        [(D_MODEL, F32), (D_MODEL, BF), (D_MODEL, BF)], sums=[128, D_MODEL])

    grads = {}
    grads["w_ple_proj"] = _mm("g_ple_proj", ps, d_pp, "tn", 256, 1024, 1024,[F32])
    grads["w_ple_gate"] = _mm("g_ple_gate", en, d_gt, "tn", 1024, 1024, 1024,[F32])
    d_en = _mm("d_ple_gate", d_gt, W["w_ple_gate"], "nt", 1024,1024, 1024, [F32])

    def add_norm_bwd(dh_out, dn, h, g):
        dh, dg = _rms_bwd(dn, h, g)
        dh = dh_out + dh
        return dh, dh, dg

    dh2, dh2_b, dg_ple = _rowwise("add_norm_ple_bwd", add_norm_bwd, T, TM, [_row(dh3, TM), _row(d_en, TM), _row(h2, TM), _full(gp)],
                                  [(D_MODEL, F32), (D_MODEL, BF)], sums=[D_MODEL])
    d_up = _mm("d_mlp_down", dh2_b, W["w_down"], "nt", 1024,1024, 1024, [BF],
               epilogue=lambda acc, u: (acc * (2.0 * jnp.maximum(u, 0.0)),), extras=(up,))
    grads["w_down"] = _mm("g_mlp_down", act, dh2_b, "tn", 1024, 1024, 1024,[F32])
    grads["w_up"] = _mm("g_mlp_up", cn, d_up, "tn", 1024, 1024, 1024,[F32])
    early_shapes = {n: shards[n].shape for n in REDUCE_EARLY}
    early_tm = sum(r * c for r, c in early_shapes.values()) // PACK_W // 4
    swap_flight = _swap_start(_pack_grads(grads, REDUCE_EARLY), "early")
    d_cn = _mm("d_mlp_up", d_up, W["w_up"], "nt", 1024,1024, 1024, [F32], after=(swap_flight.token,))
    dh1, dh1_b, dg_mlp = _rowwise("add_norm_mlp_bwd", add_norm_bwd, T, TM, [_row(dh2, TM), _row(d_cn, TM), _row(h1, TM), _full(gl)],
                                  [(D_MODEL, F32), (D_MODEL, BF)], sums=[D_MODEL])
    early_g, early_got = _swap_wait(swap_flight, dh1_b, "early")
    early_pair, early_pair_b = _pair_sum(early_g, early_got, early_tm)
    scatter_flight = _scatter_start(early_pair_b, "early")
    d_mixed = _mm("d_out_proj", dh1_b, W["w_out"], "nt", 1024,1024, 1024, [F32], after=(scatter_flight.token,))
    grads["w_out"] = _mm("g_out_proj", mixed, dh1_b, "tn", 1024, 1024, 1024,[F32])

    def gate_bwd(dm, gn, gd, un, ud):
        sn, sd = _sigmoid(gn), _sigmoid(gd)
        return jnp.concatenate([dm * un * sn * (1.0 - sn), dm * ud * sd * (1.0 - sd)], axis=1), dm * sn, dm * sd

    dz_gates, d_u_na, d_u_dil = _rowwise(
        "gate_mix_bwd", gate_bwd, T, TM,
        [_row(d_mixed, TM), _row(z_gates, TM, 0, D_MODEL), _row(z_gates, TM, 1, D_MODEL), _row(u_na, TM), _row(u_dil, TM)],
        [(2 * D_MODEL, BF), (D_MODEL, BF), (D_MODEL, BF)])
    grads["w_branch_na"] = _mm("g_branch_na", y_na, d_u_na, "tn", 1024, 1024, 1024,[F32])
    grads["w_branch_dil"] = _mm("g_branch_dil", y_dil, d_u_dil, "tn", 256, 1024, 1024,[F32])
    d_y_na = _mm("d_branch_na", d_u_na, W["w_branch_na"], "nt", 1024,512, 1024, [BF])
    d_y_dil = _mm("d_branch_dil", d_u_dil, W["w_branch_dil"], "nt", 1024,256, 1024, [F32])

    dqa, dka, dva, dtab = _na_bwd(qkv, tab, d_y_na)
    d_rpb = _na_rpb_grad(dtab)[:, :2 * NA_WIN_ROWS - 1, :2 * NA_WIN_COLS - 1]

    do_res, dlse_res = _dil_merge_bwd(d_y_dil, o_nat, w_grp, TM)
    d_dil = [_band_bwd(*dil_ops[g], do_res[g], dlse_res[g], g) for g in range(len(DIL_GROUPS))]

    me_chip = 2 * lax.axis_index("x") + lax.axis_index("y")
    early_mine = _chip_sum(lax.dynamic_index_in_dim(early_pair, me_chip, 0, keepdims=False),
                           _scatter_wait(scatter_flight, d_dil[-1][-1], "early"), early_tm)
    join_flight = _join_start(early_mine)
    dz_qkv = _qkv_unprep((dqa, dka, dva), d_dil, cos2, sin_signed, TM, after=(join_flight.token,))
    grads["w_in"] = jnp.concatenate([
        _mm("g_in_qkv", a, dz_qkv, "tn", 1024, 1280, 1024,[F32]),
        _mm("g_in_gates", a, dz_gates, "tn", 1024, 1024, 1024,[F32])], axis=1)
    late_shapes = {n: shards[n].shape for n in REDUCE_LATE}
    late_tm = sum(r * c for r, c in late_shapes.values()) // PACK_W // 4
    late_swap = _swap_start(_pack_grads(grads, REDUCE_LATE), "late")
    d_a = _mm("d_in_qkv", dz_qkv, w_qkv, "nt", 1024,1024, 1280, [F32], after=(late_swap.token,))
    late_g, late_got = _swap_wait(late_swap, d_a, "late")
    late_pair, late_pair_b = _pair_sum(late_g, late_got, late_tm)
    late_scatter = _scatter_start(late_pair_b, "late")
    d_a = _mm("d_in_gates", dz_gates, w_gates, "nt", 1024,1024, 1024, [F32], epilogue=lambda acc, e: (acc + e,), extras=(d_a,),
              after=(late_scatter.token,))

    def first_bwd(dh_out, dn, h, g):
        dh, dg = _rms_bwd(dn, h, g)
        return dh_out + dh, dg

    grad_x, dg_mix = _rowwise("norm_mix_bwd", first_bwd, T, TM, [_row(dh1, TM), _row(d_a, TM), _row(xs, TM), _full(gm)],
                              [(D_MODEL, F32)], sums=[D_MODEL])
    g_shard = _unpack_shard(_join_wait(join_flight, grad_x), early_shapes, REDUCE_EARLY)

    n_rpb = rpb.size
    rpb_rows = 4
    small = jnp.concatenate([
        dg_mix, dg_mlp, dg_ple, dg_final,
        jnp.pad(d_rpb.reshape(-1), (0, rpb_rows * D_MODEL - n_rpb)).reshape(rpb_rows, D_MODEL),
        jnp.pad(loss_part, ((0, 0), (0, D_MODEL - loss_part.shape[1]))),
        jnp.zeros((SMALL_ROWS - 5 - rpb_rows, D_MODEL), F32)], axis=0)
    small = _allreduce_small(small)
    loss = small[4 + rpb_rows, 0]

    def small_pack(a0, a1, a2, a3, r):
        return jnp.concatenate([a0.reshape(1, -1), a1.reshape(1, -1), a2.reshape(1, -1), a3.reshape(1, -1),
                                jnp.pad(r.reshape(-1), (0, rpb_rows * D_MODEL - n_rpb)).reshape(rpb_rows, D_MODEL)], axis=0)

    g_small = small[:4 + rpb_rows]
    small_res = _adamw("adamw_small", g_small, small_pack(g_mix, g_mlp, g_ple, g_final, rpb),
                       small_pack(m_g_mix, m_g_mlp, m_g_ple, m_g_final, m_rpb), small_pack(v_g_mix, v_g_mlp, v_g_ple, v_g_final, v_rpb))

    def small_unpack(t):
        return {"g_mix": t[0].reshape(g_mix.shape), "g_mlp": t[1].reshape(g_mlp.shape), "g_ple": t[2].reshape(g_ple.shape),
                "g_final": t[3].reshape(g_final.shape), "rpb": t[4:].reshape(-1)[:n_rpb].reshape(rpb.shape)}

    out = {"grad": small_unpack(g_small)}
    for kind, t in zip(("delta", "new_m", "new_v"), small_res, strict=True):
        out[kind] = small_unpack(t)
    def update(names):
        for n in names:
            out["grad"][n] = g_shard[n][None]
            res = _adamw("adamw_" + n, g_shard[n], shards[n], m_shards[n], v_shards[n])
            for kind, t in zip(("delta", "new_m", "new_v"), res, strict=True):
                out[kind][n] = t[None]

    update(REDUCE_EARLY)
    late_others = _scatter_wait(late_scatter, out["new_v"][REDUCE_EARLY[-1]], "late")
    late_mine = _chip_sum(lax.dynamic_index_in_dim(late_pair, me_chip, 0, keepdims=False), late_others, late_tm)
    g_shard.update(_unpack_shard(_join_halves(late_mine), late_shapes, REDUCE_LATE))
    update(REDUCE_LATE)

    order = ["g_mix", "w_in", "rpb", "w_branch_na", "w_branch_dil", "w_out", "g_mlp", "w_up", "w_down", "g_ple",
             "w_ple_gate", "w_ple_proj", "g_final"]
    return (loss, grad_x[None], *[out["grad"][n] for n in order], *[out["delta"][n] for n in order],
            *[out["new_m"][n] for n in order], *[out["new_v"][n] for n in order])
```

```python
import functools
from typing import NamedTuple

import numpy as np
import jax
import jax.numpy as jnp
from jax import lax
from jax.experimental import pallas as pl
from jax.experimental.pallas import tpu as pltpu

BF = jnp.bfloat16
F32 = jnp.float32
MESH = pl.DeviceIdType.MESH
ANY = pl.BlockSpec(memory_space=pl.ANY)

V7X_VMEM_BYTES = 64 * 1024 * 1024
VMEM_LIMIT = V7X_VMEM_BYTES - 16 * 1024 * 1024

D_MODEL = 1024
HEAD_DIM = 64
GRID_W = 64
NA_HEADS = 8
NA_WIN_ROWS = 8
NA_WIN_COLS = 16
NA_WIDTH = NA_HEADS * HEAD_DIM
DIL_GROUPS = ((128, 1), (512, 4), (2048, 16))
DIL_HPG = 4
DIL_HEADS = DIL_HPG * len(DIL_GROUPS)
DIL_WIDTH = DIL_HEADS * HEAD_DIM
DIL_OUT_WIDTH = DIL_HPG * HEAD_DIM
DIL_RADIUS = 64
QKV_WIDTH = 3 * NA_WIDTH + 3 * DIL_WIDTH
D_FF = 4 * D_MODEL
PLE_DIM = 256
ROPE_THETA = 10000.0
RMS_EPS = 1e-6
NEG_INF = -1e30
Q_SCALE = HEAD_DIM ** -0.5

ADAM_LR = 0.001
ADAM_B1 = 0.9
ADAM_B2 = 0.999
ADAM_EPS = 1e-08
ADAM_WD = 0.01
ADAM_STEP = 10

N_CHIPS = 4
N_DEV = 8
PACK_W = 1024
BIG = ("w_in", "w_branch_na", "w_branch_dil", "w_out", "w_up", "w_down", "w_ple_gate", "w_ple_proj")
GATHER_FIRST = ("w_in",)
GATHER_REST = BIG[1:]
REDUCE_EARLY = ("w_ple_proj", "w_ple_gate", "w_down", "w_up")
REDUCE_LATE = ("w_in", "w_branch_na", "w_branch_dil", "w_out")
SMALL_ROWS = 16


def _cparams(sem=None):
    return pltpu.CompilerParams(dimension_semantics=sem, vmem_limit_bytes=VMEM_LIMIT)


def _mm(name, a, b, mode, tm, tn, tk, out_dtypes, epilogue=None, extras=(), after=()):
    if mode == "nn":
        (M, K), N = a.shape, b.shape[1]
    elif mode == "nt":
        (M, K), N = a.shape, b.shape[0]
    else:
        (K, M), N = a.shape, b.shape[1]
    tm, tn, tk = min(tm, M), min(tn, N), min(tk, K)
    assert M % tm == 0 and N % tn == 0 and K % tk == 0, (name, M, N, K, tm, tn, tk)
    if mode == "nn":
        a_spec = pl.BlockSpec((tm, tk), lambda i, j, k: (i, k))
        b_spec = pl.BlockSpec((tk, tn), lambda i, j, k: (k, j))
        dims = (((1,), (0,)), ((), ()))
    elif mode == "nt":
        a_spec = pl.BlockSpec((tm, tk), lambda i, j, k: (i, k))
        b_spec = pl.BlockSpec((tn, tk), lambda i, j, k: (j, k))
        dims = (((1,), (1,)), ((), ()))
    else:
        a_spec = pl.BlockSpec((tk, tm), lambda i, j, k: (k, i))
        b_spec = pl.BlockSpec((tk, tn), lambda i, j, k: (k, j))
        dims = (((0,), (0,)), ((), ()))
    nk = K // tk
    n_extra, n_out = len(extras), len(out_dtypes)
    tile = pl.BlockSpec((tm, tn), lambda i, j, k: (i, j))

    n_after = len(after)

    def body(a_ref, b_ref, *rest):
        extra_refs, rest = rest[:n_extra], rest[n_extra + n_after:]
        out_refs, acc = rest[:n_out], rest[-1]
        k = pl.program_id(2)

        @pl.when(k == 0)
        def _():
            acc[...] = jnp.zeros_like(acc)

        acc[...] += lax.dot_general(a_ref[...].astype(BF), b_ref[...].astype(BF), dims, preferred_element_type=F32)

        @pl.when(k == nk - 1)
        def _():
            outs = (acc[...],) if epilogue is None else epilogue(acc[...], *[e[...] for e in extra_refs])
            for o_ref, val in zip(out_refs, outs, strict=True):
                o_ref[...] = val.astype(o_ref.dtype)

    outs = pl.pallas_call(
        body, name=name, grid=(M // tm, N // tn, nk),
        in_specs=[a_spec, b_spec] + [tile] * n_extra + [ANY] * n_after,
        out_specs=[tile] * n_out,
        out_shape=[jax.ShapeDtypeStruct((M, N), dt) for dt in out_dtypes],
        scratch_shapes=[pltpu.VMEM((tm, tn), F32)],
        compiler_params=_cparams(("parallel", "parallel", "arbitrary")),
    )(a, b, *extras, *after)
    return outs[0] if n_out == 1 else outs


def _row(arr, tm, col_block=None, width=None):
    width = arr.shape[1] if width is None else width
    cb = 0 if col_block is None else col_block
    return arr, pl.BlockSpec((tm, width), lambda i: (i, cb))


def _full(arr):
    nd = arr.ndim
    return arr, pl.BlockSpec(arr.shape, lambda i: (0,) * nd)


def _rowwise(name, body, T, tm, ins, outs, sums=(), after=()):
    n_in, n_out, n_sum, n_after = len(ins), len(outs), len(sums), len(after)

    def kern(*refs):
        in_refs, refs = refs[:n_in], refs[n_in + n_after:]
        out_refs, sum_refs = refs[:n_out], refs[n_out:]
        res = body(*[r[...] for r in in_refs])
        res = res if isinstance(res, tuple) else (res,)
        for o_ref, val in zip(out_refs, res[:n_out], strict=True):
            o_ref[...] = val.astype(o_ref.dtype)
        if n_sum:
            @pl.when(pl.program_id(0) == 0)
            def _():
                for s_ref in sum_refs:
                    s_ref[...] = jnp.zeros_like(s_ref)

            for s_ref, val in zip(sum_refs, res[n_out:], strict=True):
                s_ref[...] += val

    res = pl.pallas_call(
        kern, name=name, grid=(T // tm,),
        in_specs=[spec for _, spec in ins] + [ANY] * n_after,
        out_specs=[pl.BlockSpec((tm, c), lambda i: (i, 0)) for c, _ in outs]
        + [pl.BlockSpec((1, c), lambda i: (0, 0)) for c in sums],
        out_shape=[jax.ShapeDtypeStruct((T, c), dt) for c, dt in outs]
        + [jax.ShapeDtypeStruct((1, c), F32) for c in sums],
        compiler_params=_cparams(("arbitrary",)),
    )(*[a for a, _ in ins], *after)
    return res[0] if len(res) == 1 else res


def _sigmoid(x):
    return 1.0 / (1.0 + jnp.exp(-x))


def _rms(h):
    return lax.rsqrt(jnp.mean(h * h, axis=-1, keepdims=True) + RMS_EPS)


def _rms_bwd(dy, h, g):
    r = _rms(h)
    n = h * r
    dn = dy * g
    dh = r * (dn - n * jnp.mean(dn * n, axis=-1, keepdims=True))
    return dh, jnp.sum(dy * n, axis=0, keepdims=True)


def _rope(x, cos2, sin_signed):
    lane = lax.broadcasted_iota(jnp.int32, x.shape, 1)
    swapped = jnp.where((lane % HEAD_DIM) < HEAD_DIM // 2, pltpu.roll(x, 128 - HEAD_DIM // 2, 1), pltpu.roll(x, HEAD_DIM // 2, 1))
    return x * cos2 + swapped * sin_signed


def _rope_cols(x, cos2, sin_signed):
    return jnp.concatenate([_rope(x[:, c:c + 128], cos2, sin_signed) for c in range(0, x.shape[1], 128)], axis=1)


NA_KEYS = NA_WIN_ROWS * GRID_W
NA_BASES = 8


def _na_row_geometry(r, rows):
    first = jnp.clip(r - NA_WIN_ROWS // 2, 0, rows - NA_WIN_ROWS)
    base = first - r + (NA_WIN_ROWS - 1)
    return pl.multiple_of(first * GRID_W, GRID_W), base


NA_ROWS_PER_STEP = 8
NA_BWD_ROWS_PER_STEP = 8


def _softmax_rows(s):
    p = jnp.exp(s - jnp.max(s, axis=-1, keepdims=True))
    return p / jnp.sum(p, axis=-1, keepdims=True)


def _na_probs(q, kw, bias):
    return _softmax_rows(lax.dot_general(q, kw, (((1,), (1,)), ((), ())), preferred_element_type=F32) + bias)


def _split_pair(t):
    first = lax.broadcasted_iota(jnp.int32, t.shape, 1) < HEAD_DIM
    zero = jnp.zeros_like(t)
    return jnp.where(first, t, zero), jnp.where(first, zero, t)


def _join_pair(a, b):
    return jnp.where(lax.broadcasted_iota(jnp.int32, a.shape, 1) < HEAD_DIM, a, b)


_NT = (((1,), (1,)), ((), ()))
_TN = (((0,), (0,)), ((), ()))


def _na_fwd(qkv, tab):
    T = qkv.shape[0]
    rows = T // GRID_W
    n_pairs = NA_WIDTH // 128

    def body(q_ref, k_ref, v_ref, tab_ref, y_ref):
        def step(it, carry):
            geo = [_na_row_geometry(it * NA_ROWS_PER_STEP + u, rows) for u in range(NA_ROWS_PER_STEP)]
            q0s = [pl.multiple_of((it * NA_ROWS_PER_STEP + u) * GRID_W, GRID_W) for u in range(NA_ROWS_PER_STEP)]
            ss = [lax.dot_general(jnp.concatenate(_split_pair(q_ref[pl.ds(q0, GRID_W), :] * Q_SCALE), axis=0),
                                  k_ref[pl.ds(k0, NA_KEYS), :], _NT, preferred_element_type=F32)
                  for q0, (k0, _) in zip(q0s, geo)]
            ps = [_softmax_rows(s + jnp.concatenate([tab_ref[0, base], tab_ref[1, base]], axis=0)) for s, (_, base) in zip(ss, geo)]
            ys = [jnp.dot(p.astype(BF), v_ref[pl.ds(k0, NA_KEYS), :], preferred_element_type=F32) for p, (k0, _) in zip(ps, geo)]
            for q0, y2 in zip(q0s, ys):
                y_ref[pl.ds(q0, GRID_W), :] = _join_pair(y2[:GRID_W], y2[GRID_W:]).astype(y_ref.dtype)
            return carry

        lax.fori_loop(0, rows // NA_ROWS_PER_STEP, step, 0)

    def cols(first):
        return pl.BlockSpec((T, 128), lambda j: (0, first + j))

    return pl.pallas_call(
        body, name="na_fwd", grid=(n_pairs,),
        in_specs=[cols(0), cols(n_pairs), cols(2 * n_pairs), pl.BlockSpec((2, NA_BASES, GRID_W, NA_KEYS), lambda j: (j, 0, 0, 0))],
        out_specs=cols(0), out_shape=jax.ShapeDtypeStruct((T, NA_WIDTH), BF),
        compiler_params=_cparams(("parallel",)),
    )(qkv, qkv, qkv, tab)


def _na_bwd(qkv, tab, do):
    T = qkv.shape[0]
    rows = T // GRID_W
    n_pairs = NA_WIDTH // 128

    def body(q_ref, k_ref, v_ref, tab_ref, do_ref, dq_ref, dk_ref, dv_ref, dtab_ref):
        dk_ref[...] = jnp.zeros_like(dk_ref)
        dv_ref[...] = jnp.zeros_like(dv_ref)
        dtab_ref[...] = jnp.zeros_like(dtab_ref)

        def step(it, carry):
            U = NA_BWD_ROWS_PER_STEP
            geo = [_na_row_geometry(it * U + u, rows) for u in range(U)]
            q0s = [pl.multiple_of((it * U + u) * GRID_W, GRID_W) for u in range(U)]
            q2s = [jnp.concatenate(_split_pair(q_ref[pl.ds(q0, GRID_W), :] * Q_SCALE), axis=0) for q0 in q0s]
            do2s = [jnp.concatenate(_split_pair(do_ref[pl.ds(q0, GRID_W), :]), axis=0) for q0 in q0s]
            ss = [lax.dot_general(q2, k_ref[pl.ds(k0, NA_KEYS), :], _NT, preferred_element_type=F32) for q2, (k0, _) in zip(q2s, geo)]
            dps = [lax.dot_general(do2, v_ref[pl.ds(k0, NA_KEYS), :], _NT, preferred_element_type=F32) for do2, (k0, _) in zip(do2s, geo)]
            ps = [_softmax_rows(s + jnp.concatenate([tab_ref[0, base], tab_ref[1, base]], axis=0)) for s, (_, base) in zip(ss, geo)]
            dss = [p * (dp - jnp.sum(dp * p, axis=-1, keepdims=True)) for p, dp in zip(ps, dps)]
            dvs = [lax.dot_general(p.astype(BF), do2, _TN, preferred_element_type=F32) for p, do2 in zip(ps, do2s)]
            dsbs = [ds.astype(BF) for ds in dss]
            dqs = [jnp.dot(dsb, k_ref[pl.ds(k0, NA_KEYS), :], preferred_element_type=F32) for dsb, (k0, _) in zip(dsbs, geo)]
            dks = [lax.dot_general(dsb, q2, _TN, preferred_element_type=F32) for dsb, q2 in zip(dsbs, q2s)]
            for u in range(U):
                k0, base = geo[u]
                dtab_ref[0, base] += dss[u][:GRID_W]
                dtab_ref[1, base] += dss[u][GRID_W:]
                dq_ref[pl.ds(q0s[u], GRID_W), :] = _join_pair(dqs[u][:GRID_W], dqs[u][GRID_W:])
                dk_ref[pl.ds(k0, NA_KEYS), :] += dks[u]
                dv_ref[pl.ds(k0, NA_KEYS), :] += dvs[u]
            return carry

        lax.fori_loop(0, rows // NA_BWD_ROWS_PER_STEP, step, 0)

    def cols(first):
        return pl.BlockSpec((T, 128), lambda j: (0, first + j))

    tabs = pl.BlockSpec((2, NA_BASES, GRID_W, NA_KEYS), lambda j: (j, 0, 0, 0))
    wide = jax.ShapeDtypeStruct((T, NA_WIDTH), F32)
    return pl.pallas_call(
        body, name="na_bwd", grid=(n_pairs,),
        in_specs=[cols(0), cols(n_pairs), cols(2 * n_pairs), tabs, cols(0)],
        out_specs=[cols(0), cols(0), cols(0), tabs],
        out_shape=[wide, wide, wide, jax.ShapeDtypeStruct((NA_HEADS, NA_BASES, GRID_W, NA_KEYS), F32)],
        compiler_params=_cparams(("parallel",)),
    )(qkv, qkv, qkv, tab, do)


def _na_geometry_np():
    c = np.arange(GRID_W)
    cs = np.clip(c - NA_WIN_COLS // 2, 0, GRID_W - NA_WIN_COLS)
    valid = (c[None, :] >= cs[:, None]) & (c[None, :] < cs[:, None] + NA_WIN_COLS)
    off = c[None, :] - c[:, None] + (NA_WIN_COLS - 1)
    oh_col = np.zeros((GRID_W, GRID_W, 2 * NA_WIN_COLS - 1), np.float32)
    qq, kk = np.nonzero(valid)
    oh_col[qq, kk, off[qq, kk]] = 1.0
    oh_row = np.zeros((NA_BASES, NA_WIN_ROWS, 2 * NA_WIN_ROWS - 1), np.float32)
    for base in range(NA_BASES):
        for i in range(NA_WIN_ROWS):
            oh_row[base, i, base + i] = 1.0
    return valid, oh_col, oh_row


def _na_bias_table(rpb):
    valid, oh_col, oh_row = _na_geometry_np()
    hi = lax.Precision.HIGHEST
    t1 = jnp.einsum("hrc,pir->hpic", rpb, oh_row, precision=hi)
    tab = jnp.einsum("hpic,qkc->hpqik", t1, oh_col, precision=hi)
    tab = jnp.where(valid[None, None, :, None, :], tab, NEG_INF)
    return tab.reshape(rpb.shape[0], NA_BASES, GRID_W, NA_KEYS)


def _na_rpb_grad(dtab):
    H = dtab.shape[0]
    n_rows = 2 * NA_WIN_ROWS - 1
    n_cols = 2 * NA_WIN_COLS - 1

    def body(d_ref, o_ref):
        lane = lax.broadcasted_iota(jnp.int32, (GRID_W, 128), 1)
        low = lane < GRID_W
        out_rows = []
        for ro in range(n_rows):
            acc = jnp.zeros((GRID_W, 128), F32)
            for base in range(NA_BASES):
                i = ro - base
                if not 0 <= i < NA_WIN_ROWS:
                    continue
                pair = d_ref[base, :, pl.ds((i // 2) * 128, 128)]
                if i % 2:
                    pair = pltpu.roll(pair, GRID_W, 1)
                acc = acc + jnp.where(low, pair, 0.0)
            skew = pltpu.roll(acc, 0, 1, stride=1, stride_axis=0)
            diag = jnp.sum(skew, axis=0, keepdims=True)
            out_rows.append(pltpu.roll(jnp.broadcast_to(diag, (8, 128)), 128 - (GRID_W - NA_WIN_COLS), 1)[:1])
        out_rows.append(jnp.zeros((1, 128), F32))
        res = jnp.concatenate(out_rows, axis=0)
        o_ref[...] = jnp.where(lax.broadcasted_iota(jnp.int32, res.shape, 1) < n_cols, res, 0.0)

    return pl.pallas_call(
        body, name="na_rpb_grad", grid=(H,),
        in_specs=[pl.BlockSpec((None, NA_BASES, GRID_W, NA_KEYS), lambda h: (h, 0, 0, 0))],
        out_specs=pl.BlockSpec((None, n_rows + 1, 128), lambda h: (h, 0, 0)),
        out_shape=jax.ShapeDtypeStruct((H, n_rows + 1, 128), F32),
        compiler_params=_cparams(("parallel",)),
    )(jnp.flip(dtab, axis=2))


BAND_Q = 128
BAND_KEYS = BAND_Q + 2 * DIL_RADIUS


def _band_geometry(n, L):
    q0 = pl.multiple_of(n * BAND_Q, BAND_Q)
    k0 = pl.multiple_of(jnp.clip(q0 - DIL_RADIUS, 0, L - BAND_KEYS), DIL_RADIUS)
    qi = q0 + lax.broadcasted_iota(jnp.int32, (BAND_Q, BAND_KEYS), 0)
    kj = k0 + lax.broadcasted_iota(jnp.int32, (BAND_Q, BAND_KEYS), 1)
    return q0, k0, jnp.abs(qi - kj) <= DIL_RADIUS


DIL_PAIRS = DIL_OUT_WIDTH // 128


def _residue_shape(dil, T, dtype):
    return jax.ShapeDtypeStruct((DIL_PAIRS, dil, T // dil, 128), dtype)


def _residue_tile(dil, tm):
    return pl.BlockSpec((DIL_PAIRS, dil, tm // dil, 128), lambda i: (0, 0, i, 0))


def _to_natural(ref, scratch, dil, tm):
    tiles = []
    for pair in range(DIL_PAIRS):
        if dil == 1:
            tiles.append(ref[pair, 0].astype(F32))
            continue
        for r in range(dil):
            scratch[pl.ds(r, tm // dil, stride=dil), :] = ref[pair, r].astype(F32)
        tiles.append(scratch[...])
    return tiles


def _from_natural(tile, scratch, ref, pair, dil, tm):
    if dil == 1:
        ref[pair, 0] = tile.astype(ref.dtype)
        return
    scratch[...] = tile
    for r in range(dil):
        ref[pair, r] = scratch[pl.ds(r, tm // dil, stride=dil), :].astype(ref.dtype)


def _band_specs(group, T):
    dil = DIL_GROUPS[group][1]
    L = T // dil
    assert L % BAND_Q == 0 and L >= BAND_KEYS, (T, dil)
    return L, (dil * DIL_PAIRS,), pl.BlockSpec((None, None, L, 128), lambda s: (s % DIL_PAIRS, s // DIL_PAIRS, 0, 0))


BAND_BLOCKS_PER_STEP = 4


def _band_softmax(s, valid):
    s = jnp.where(valid, s, NEG_INF)
    m = jnp.max(s, axis=-1, keepdims=True)
    p = jnp.exp(s - m)
    l = jnp.sum(p, axis=-1, keepdims=True)
    return p / l, m + jnp.log(l)


def _band_fwd(q, k, v, group):
    T = q.shape[1] * q.shape[2]
    L, grid, spec = _band_specs(group, T)
    U = min(BAND_BLOCKS_PER_STEP, L // BAND_Q)

    def body(q_ref, k_ref, v_ref, o_ref, lse_ref):
        def step(it, carry):
            geo = [_band_geometry(it * U + u, L) for u in range(U)]
            ss = [lax.dot_general(jnp.concatenate(_split_pair(q_ref[pl.ds(q0, BAND_Q), :]), axis=0),
                                  k_ref[pl.ds(k0, BAND_KEYS), :], _NT, preferred_element_type=F32) for q0, k0, _ in geo]
            pls = [_band_softmax(s, jnp.concatenate([valid, valid], axis=0)) for s, (_, _, valid) in zip(ss, geo)]
            os = [jnp.dot(p.astype(BF), v_ref[pl.ds(k0, BAND_KEYS), :], preferred_element_type=F32) for (p, _), (_, k0, _) in zip(pls, geo)]
            for (q0, _, _), o2, (_, lse) in zip(geo, os, pls):
                o_ref[pl.ds(q0, BAND_Q), :] = _join_pair(o2[:BAND_Q], o2[BAND_Q:])
                lse2 = jnp.broadcast_to(lse, (2 * BAND_Q, 128))
                lse_ref[pl.ds(q0, BAND_Q), :] = _join_pair(lse2[:BAND_Q], lse2[BAND_Q:])
            return carry

        lax.fori_loop(0, L // (BAND_Q * U), step, 0)

    res = _residue_shape(DIL_GROUPS[group][1], T, F32)
    return pl.pallas_call(
        body, name=f"band_fwd_g{group}", grid=grid,
        in_specs=[spec] * 3, out_specs=[spec] * 2, out_shape=[res, res],
        compiler_params=_cparams(("parallel",)),
    )(q, k, v)


def _band_bwd(q, k, v, do, dlse, group):
    T = q.shape[1] * q.shape[2]
    L, grid, spec = _band_specs(group, T)
    U = min(BAND_BLOCKS_PER_STEP, L // BAND_Q)

    def body(q_ref, k_ref, v_ref, do_ref, dlse_ref, dq_ref, dk_ref, dv_ref):
        dk_ref[...] = jnp.zeros_like(dk_ref)
        dv_ref[...] = jnp.zeros_like(dv_ref)

        def step(it, carry):
            geo = [_band_geometry(it * U + u, L) for u in range(U)]
            q2s = [jnp.concatenate(_split_pair(q_ref[pl.ds(q0, BAND_Q), :]), axis=0) for q0, _, _ in geo]
            do2s = [jnp.concatenate(_split_pair(do_ref[pl.ds(q0, BAND_Q), :]), axis=0) for q0, _, _ in geo]
            ss = [lax.dot_general(q2, k_ref[pl.ds(k0, BAND_KEYS), :], _NT, preferred_element_type=F32) for q2, (_, k0, _) in zip(q2s, geo)]
            dps = [lax.dot_general(do2, v_ref[pl.ds(k0, BAND_KEYS), :], _NT, preferred_element_type=F32) for do2, (_, k0, _) in zip(do2s, geo)]
            ps = [_band_softmax(s, jnp.concatenate([valid, valid], axis=0))[0] for s, (_, _, valid) in zip(ss, geo)]
            dss = []
            for p, dp, (q0, _, _) in zip(ps, dps, geo):
                dl = dlse_ref[pl.ds(q0, BAND_Q), :]
                dl2 = jnp.concatenate([dl[:, :1], dl[:, HEAD_DIM:HEAD_DIM + 1]], axis=0)
                dss.append(p * (dp - jnp.sum(dp * p, axis=-1, keepdims=True) + dl2))
            dvs = [lax.dot_general(p.astype(BF), do2, _TN, preferred_element_type=F32) for p, do2 in zip(ps, do2s)]
            dsbs = [ds.astype(BF) for ds in dss]
            dqs = [jnp.dot(dsb, k_ref[pl.ds(k0, BAND_KEYS), :], preferred_element_type=F32) for dsb, (_, k0, _) in zip(dsbs, geo)]
            dks = [lax.dot_general(dsb, q2, _TN, preferred_element_type=F32) for dsb, q2 in zip(dsbs, q2s)]
            for u, (q0, k0, _) in enumerate(geo):
                dq_ref[pl.ds(q0, BAND_Q), :] = _join_pair(dqs[u][:BAND_Q], dqs[u][BAND_Q:])
                dk_ref[pl.ds(k0, BAND_KEYS), :] += dks[u]
                dv_ref[pl.ds(k0, BAND_KEYS), :] += dvs[u]
            return carry

        lax.fori_loop(0, L // (BAND_Q * U), step, 0)

    res = _residue_shape(DIL_GROUPS[group][1], T, F32)
    return pl.pallas_call(
        body, name=f"band_bwd_g{group}", grid=grid,
        in_specs=[spec] * 5, out_specs=[spec] * 3, out_shape=[res] * 3,
        compiler_params=_cparams(("parallel",)),
    )(q, k, v, do, dlse)


def _head_sums(t):
    head = lax.broadcasted_iota(jnp.int32, t.shape, 1) // HEAD_DIM
    out = jnp.zeros_like(t)
    for h in range(t.shape[1] // HEAD_DIM):
        mine = head == h
        out = jnp.where(mine, jnp.sum(jnp.where(mine, t, 0.0), axis=-1, keepdims=True), out)
    return out


def _dil_merge_fwd(os, lses, T, tm):
    G = len(DIL_GROUPS)
    W = DIL_OUT_WIDTH
    dils = [d for _, d in DIL_GROUPS]

    def body(*refs):
        o_refs, lse_refs = refs[:G], refs[G:2 * G]
        y_ref, w_refs, on_refs, scratch = refs[2 * G], refs[2 * G + 1:3 * G + 1], refs[3 * G + 1:4 * G + 1], refs[-1]
        o = [jnp.concatenate(_to_natural(r, scratch, d, tm), axis=1) for r, d in zip(o_refs, dils)]
        ls = [jnp.concatenate(_to_natural(r, scratch, d, tm), axis=1) for r, d in zip(lse_refs, dils)]
        m = functools.reduce(jnp.maximum, ls)
        es = [jnp.exp(l - m) for l in ls]
        tot = functools.reduce(jnp.add, es)
        ws = [e / tot for e in es]
        y_ref[...] = functools.reduce(jnp.add, [w * t for w, t in zip(ws, o)]).astype(y_ref.dtype)
        for g in range(G):
            w_refs[g][...] = ws[g]
            on_refs[g][...] = o[g]

    nat = pl.BlockSpec((tm, W), lambda i: (i, 0))
    res = pl.pallas_call(
        body, name="dil_merge_fwd", grid=(T // tm,),
        in_specs=[_residue_tile(d, tm) for d in dils] * 2,
        out_specs=[nat] * (2 * G + 1),
        out_shape=[jax.ShapeDtypeStruct((T, W), BF)] + [jax.ShapeDtypeStruct((T, W), F32)] * (2 * G),
        scratch_shapes=[pltpu.VMEM((tm, 128), F32)],
        compiler_params=_cparams(("parallel",)),
    )(*os, *lses)
    return res[0], res[1:G + 1], res[G + 1:]


def _dil_merge_bwd(dy, os, ws, tm):
    G = len(DIL_GROUPS)
    T, W = dy.shape
    dils = [d for _, d in DIL_GROUPS]

    def body(*refs):
        dyt = refs[0][...]
        o, w = [r[...] for r in refs[1:G + 1]], [r[...] for r in refs[G + 1:2 * G + 1]]
        do_refs, dlse_refs, scratch = refs[2 * G + 1:3 * G + 1], refs[3 * G + 1:4 * G + 1], refs[-1]
        dws = [_head_sums(dyt * t) for t in o]
        mean = functools.reduce(jnp.add, [a * b for a, b in zip(w, dws)])
        for g, d in enumerate(dils):
            do, dlse = w[g] * dyt, w[g] * (dws[g] - mean)
            for pair in range(DIL_PAIRS):
                cols = slice(pair * 128, (pair + 1) * 128)
                _from_natural(do[:, cols], scratch, do_refs[g], pair, d, tm)
                _from_natural(dlse[:, cols], scratch, dlse_refs[g], pair, d, tm)

    nat = pl.BlockSpec((tm, W), lambda i: (i, 0))
    res = pl.pallas_call(
        body, name="dil_merge_bwd", grid=(T // tm,),
        in_specs=[nat] * (2 * G + 1),
        out_specs=[_residue_tile(d, tm) for d in dils] * 2,
        out_shape=[_residue_shape(d, T, BF) for d in dils] + [_residue_shape(d, T, F32) for d in dils],
        scratch_shapes=[pltpu.VMEM((tm, 128), F32)],
        compiler_params=_cparams(("parallel",)),
    )(dy, *os, *ws)
    return res[:G], res[G:]


def _qkv_prep(z, cos2, sin_signed, tm):
    T = z.shape[0]
    G = len(DIL_GROUPS)
    dils = [d for _, d in DIL_GROUPS]
    n_dil_blocks = 3 * DIL_WIDTH // 128

    def body(*refs):
        blocks = refs[:n_dil_blocks]
        cos_ref, sin_ref = refs[n_dil_blocks], refs[1 + n_dil_blocks]
        outs = refs[2 + n_dil_blocks:]
        for part in range(3):
            for g, d in enumerate(dils):
                out = outs[g * 3 + part]
                for pair in range(DIL_PAIRS):
                    blk = blocks[part * (DIL_WIDTH // 128) + g * DIL_PAIRS + pair]
                    for r in range(d):
                        rows = pl.ds(r, tm // d, stride=d) if d > 1 else slice(None)
                        x = blk[rows, :]
                        if part < 2:
                            x = _rope(x, cos_ref[rows, :], sin_ref[rows, :])
                        if part == 0:
                            x = x * Q_SCALE
                        out[pair, r] = x.astype(out.dtype)

    lane_block = [pl.BlockSpec((tm, 128), functools.partial(lambda b, i: (i, b), b)) for b in range(n_dil_blocks)]
    tab = pl.BlockSpec((tm, 128), lambda i: (i, 0))
    res = pl.pallas_call(
        body, name="qkv_prep", grid=(T // tm,),
        in_specs=lane_block + [tab, tab],
        out_specs=[_residue_tile(d, tm) for d in dils for _ in range(3)],
        out_shape=[_residue_shape(d, T, BF) for d in dils for _ in range(3)],
        compiler_params=_cparams(("parallel",)),
    )(*[z] * n_dil_blocks, cos2, sin_signed)
    return [res[3 * g:3 + 3 * g] for g in range(G)]


def _qkv_unprep(d_na, d_dil, cos2, sin_signed, tm, after=()):
    T = d_na[0].shape[0]
    G = len(DIL_GROUPS)
    dils = [d for _, d in DIL_GROUPS]
    n_after = len(after)

    def body(*refs):
        dq, dk, dv = (r[...] for r in refs[:3])
        res_refs = refs[3:3 + 3 * G]
        cs, sn = refs[3 + 3 * G][...], refs[4 + 3 * G][...]
        out, scratch = refs[5 + 3 * G + n_after], refs[-1]
        cols = [dq * Q_SCALE, dk, dv]
        for part in range(3):
            for g, d in enumerate(dils):
                for x in _to_natural(res_refs[g * 3 + part], scratch, d, tm):
                    if part < 2:
                        x = _rope(x, cs, -sn)
                    cols.append(x * Q_SCALE if part == 0 else x)
        out[...] = jnp.concatenate(cols, axis=1).astype(out.dtype)

    wide = pl.BlockSpec((tm, NA_WIDTH), lambda i: (i, 0))
    tab = pl.BlockSpec((tm, 128), lambda i: (i, 0))
    return pl.pallas_call(
        body, name="qkv_unprep", grid=(T // tm,),
        in_specs=[wide] * 3 + [_residue_tile(d, tm) for d in dils for _ in range(3)] + [tab, tab] + [ANY] * n_after,
        out_specs=pl.BlockSpec((tm, QKV_WIDTH), lambda i: (i, 0)),
        out_shape=jax.ShapeDtypeStruct((T, QKV_WIDTH), BF),
        scratch_shapes=[pltpu.VMEM((tm, 128), F32)],
        compiler_params=_cparams(("parallel",)),
    )(*d_na, *[t for g in range(G) for t in d_dil[g]], cos2, sin_signed, *after)


def _rope_tables(positions):
    half = HEAD_DIM // 2
    inv_freq = ROPE_THETA ** (-jnp.arange(half, dtype=F32) / half)
    ang = positions.astype(F32)[:, None] * inv_freq
    cos, sin = jnp.cos(ang), jnp.sin(ang)
    return jnp.tile(jnp.concatenate([cos, cos], axis=1), (1, 2)), jnp.tile(jnp.concatenate([-sin, sin], axis=1), (1, 2))


def _pack_rows(t):
    return t.reshape(-1, PACK_W)


def _me():
    return lax.axis_index("x"), lax.axis_index("y"), lax.axis_index("c")


def _other_chips(x, y):
    return [(1 - x, y), (x, 1 - y), (1 - x, 1 - y)]


def _gather_weights(packed):
    R, W = packed.shape
    half = R // 2

    def body(in_ref, out_ref, send_sems, recv_sems):
        x, y, c = _me()
        sibling = (x, y, 1 - c)
        chips = _other_chips(x, y)

        def block(chip, core):
            return out_ref.at[2 * chip[0] + chip[1], pl.ds(core * half, half), :]

        def copy(k, chip, core, to, src=None):
            return pltpu.make_async_remote_copy(
                src_ref=block(chip, core) if src is None else src, dst_ref=block(chip, core),
                send_sem=send_sems.at[k], recv_sem=recv_sems.at[k], device_id=to, device_id_type=MESH)

        first = [copy(j, (x, y), c, (*chip, c), src=in_ref.at[pl.ds(c * half, half), :]) for j, chip in enumerate(chips)]
        for cp in first:
            cp.start()
        passed = [copy(3 + j, chip, c, sibling) for j, chip in enumerate(chips)]
        for j, chip in enumerate(chips):
            copy(j, chip, c, (x, y, c)).wait_recv()
            passed[j].start()
        for j, chip in enumerate(chips):
            copy(3 + j, chip, 1 - c, (x, y, c)).wait_recv()
        for cp in first + passed:
            cp.wait_send()

    others = pl.pallas_call(
        body, name="gather_weights",
        in_specs=[ANY], out_specs=ANY,
        out_shape=jax.ShapeDtypeStruct((N_CHIPS, R, W), packed.dtype),
        scratch_shapes=[pltpu.SemaphoreType.DMA((6,)), pltpu.SemaphoreType.DMA((6,))],
    )(packed)
    return lax.dynamic_update_slice(others, packed[None], (2 * lax.axis_index("x") + lax.axis_index("y"), 0, 0))


def _swap_halves(g):
    S, R, W = g.shape
    half = R // 2

    def body(g_ref, out_ref, send_sem, recv_sem):
        x, y, c = _me()
        cp = pltpu.make_async_remote_copy(
            src_ref=g_ref.at[:, pl.ds((1 - c) * half, half), :], dst_ref=out_ref,
            send_sem=send_sem, recv_sem=recv_sem, device_id=(x, y, 1 - c), device_id_type=MESH)
        cp.start()
        cp.wait()

    return pl.pallas_call(
        body, name="swap_halves", in_specs=[ANY], out_specs=ANY,
        out_shape=jax.ShapeDtypeStruct((S, half, W), g.dtype),
        scratch_shapes=[pltpu.SemaphoreType.DMA, pltpu.SemaphoreType.DMA],
    )(g)


def _pair_sum(g, got, tm):
    S, R, W = g.shape
    half = R // 2
    nb = half // tm

    def body(c_ref, g_ref, got_ref, o_ref, ob_ref):
        tot = g_ref[...] + got_ref[...]
        o_ref[...] = tot
        ob_ref[...] = tot.astype(ob_ref.dtype)

    tile = pl.BlockSpec((None, tm, W), lambda s, i, c_ref: (s, i, 0))
    return pl.pallas_call(
        body, name="pair_sum",
        grid_spec=pltpu.PrefetchScalarGridSpec(
            num_scalar_prefetch=1, grid=(S, nb),
            in_specs=[pl.BlockSpec((None, tm, W), lambda s, i, c_ref: (s, c_ref[0] * nb + i, 0)), tile],
            out_specs=[tile, tile]),
        out_shape=[jax.ShapeDtypeStruct((S, half, W), F32), jax.ShapeDtypeStruct((S, half, W), BF)],
        compiler_params=_cparams(("parallel", "parallel")),
    )(lax.axis_index("c").reshape(1).astype(jnp.int32), g, got)


def _scatter_chips(part):
    S, h, W = part.shape

    def body(p_ref, out_ref, send_sems, recv_sems):
        x, y, c = _me()
        chips = _other_chips(x, y)
        sends = [pltpu.make_async_remote_copy(
            src_ref=p_ref.at[2 * chip[0] + chip[1]], dst_ref=out_ref.at[j],
            send_sem=send_sems.at[j], recv_sem=recv_sems.at[j], device_id=(*chip, c), device_id_type=MESH)
            for j, chip in enumerate(chips)]
        for cp in sends:
            cp.start()
        for cp in sends:
            cp.wait()

    return pl.pallas_call(
        body, name="scatter_chips", in_specs=[ANY], out_specs=ANY,
        out_shape=jax.ShapeDtypeStruct((S - 1, h, W), part.dtype),
        scratch_shapes=[pltpu.SemaphoreType.DMA((3,)), pltpu.SemaphoreType.DMA((3,))],
    )(part)


def _chip_sum(own, others, tm):
    n, h, W = others.shape

    def body(own_ref, p_ref, o_ref):
        o_ref[...] = ((own_ref[...] + p_ref[0].astype(F32)) + p_ref[1].astype(F32)) + p_ref[2].astype(F32)

    return pl.pallas_call(
        body, name="chip_sum", grid=(h // tm,),
        in_specs=[pl.BlockSpec((tm, W), lambda i: (i, 0)), pl.BlockSpec((n, tm, W), lambda i: (0, i, 0))],
        out_specs=pl.BlockSpec((tm, W), lambda i: (i, 0)),
        out_shape=jax.ShapeDtypeStruct((h, W), F32),
        compiler_params=_cparams(("parallel",)),
    )(own, others)


def _join_halves(mine):
    h, W = mine.shape

    def body(m_ref, out_ref, send_sem, recv_sem):
        x, y, c = _me()
        cp = pltpu.make_async_remote_copy(
            src_ref=m_ref, dst_ref=out_ref.at[pl.ds(c * h, h), :],
            send_sem=send_sem, recv_sem=recv_sem, device_id=(x, y, 1 - c), device_id_type=MESH)
        cp.start()
        pltpu.make_async_remote_copy(
            src_ref=m_ref, dst_ref=out_ref.at[pl.ds((1 - c) * h, h), :],
            send_sem=send_sem, recv_sem=recv_sem, device_id=(x, y, 1 - c), device_id_type=MESH).wait_recv()
        cp.wait_send()

    other = pl.pallas_call(
        body, name="join_halves", in_specs=[ANY], out_specs=ANY,
        out_shape=jax.ShapeDtypeStruct((2 * h, W), mine.dtype),
        scratch_shapes=[pltpu.SemaphoreType.DMA, pltpu.SemaphoreType.DMA],
    )(mine)
    return lax.dynamic_update_slice(other, mine, (lax.axis_index("c") * h, 0))


def _allreduce_small(s):
    R, W = s.shape

    def body(s_ref, o_ref, buf, send_sems, recv_sems):
        x, y, c = _me()
        me = 4 * x + 2 * y + c
        buf[me] = s_ref[...]
        peers = [((x + fx) % 2, (y + fy) % 2, (c + fc) % 2) for fx in range(2) for fy in range(2) for fc in range(2)][1:]
        sends = [pltpu.make_async_remote_copy(
            src_ref=s_ref, dst_ref=buf.at[me], send_sem=send_sems.at[k], recv_sem=recv_sems.at[k],
            device_id=peer, device_id_type=MESH) for k, peer in enumerate(peers)]
        for cp in sends:
            cp.start()
        for k, peer in enumerate(peers):
            pltpu.make_async_remote_copy(
                src_ref=s_ref, dst_ref=buf.at[4 * peer[0] + 2 * peer[1] + peer[2]], send_sem=send_sems.at[k],
                recv_sem=recv_sems.at[k], device_id=peer, device_id_type=MESH).wait_recv()
        for cp in sends:
            cp.wait_send()
        total = buf[0]
        for d in range(1, N_DEV):
            total = total + buf[d]
        o_ref[...] = total

    return pl.pallas_call(
        body, name="allreduce_small",
        in_specs=[pl.BlockSpec(memory_space=pltpu.VMEM)], out_specs=pl.BlockSpec(memory_space=pltpu.VMEM),
        out_shape=jax.ShapeDtypeStruct((R, W), F32),
        scratch_shapes=[pltpu.VMEM((N_DEV, R, W), F32), pltpu.SemaphoreType.DMA((N_DEV - 1,)), pltpu.SemaphoreType.DMA((N_DEV - 1,))],
    )(s)


HBM_SPEC = pl.BlockSpec(memory_space=pltpu.HBM)
SEM_SPEC = pl.BlockSpec(memory_space=pltpu.SEMAPHORE)
DATAFLOW = pltpu.SideEffectType.DATAFLOW_SIDE_EFFECTING


class _InFlight(NamedTuple):
    sems: tuple
    src: jax.Array
    land: jax.Array
    token: jax.Array


def _split_start(name, src, land_shape, land_dtype, n, copies):
    def body(src_ref, land_ref, *rest):
        sems, token = rest[:2 * n], rest[-1]
        for k, (s, d, peer) in enumerate(copies(src_ref, land_ref)):
            pltpu.make_async_remote_copy(src_ref=s, dst_ref=d, send_sem=sems[k], recv_sem=sems[n + k],
                                         device_id=peer, device_id_type=MESH).start()
        token[...] = jnp.zeros_like(token)

    outs = pl.pallas_call(
        body, name=name,
        out_shape=(*[pltpu.SemaphoreType.DMA(())] * (2 * n), pltpu.HBM(src.shape, src.dtype), pltpu.HBM(land_shape, land_dtype),
                   jax.ShapeDtypeStruct((8, 128), F32)),
        in_specs=(HBM_SPEC, HBM_SPEC),
        out_specs=(*[SEM_SPEC] * (2 * n), HBM_SPEC, HBM_SPEC, pl.BlockSpec(memory_space=pltpu.VMEM)),
        input_output_aliases={0: 2 * n, 1: 2 * n + 1},
        compiler_params=pltpu.CompilerParams(has_side_effects=DATAFLOW),
    )(pltpu.with_memory_space_constraint(src, pltpu.HBM), pltpu.with_memory_space_constraint(lax.empty(land_shape, land_dtype), pltpu.HBM))
    return _InFlight(tuple(outs[:2 * n]), outs[2 * n], outs[2 * n + 1], outs[2 * n + 2])


def _split_wait(name, flight, after, n, copies):
    def body(src_ref, land_ref, *rest):
        sems = rest[:2 * n]
        for k, (s, d, peer) in enumerate(copies(src_ref, land_ref)):
            cp = pltpu.make_async_remote_copy(src_ref=s, dst_ref=d, send_sem=sems[k], recv_sem=sems[n + k],
                                              device_id=peer, device_id_type=MESH)
            cp.wait_send()
            cp.wait_recv()

    return pl.pallas_call(
        body, name=name,
        out_shape=(pltpu.HBM(flight.src.shape, flight.src.dtype), pltpu.HBM(flight.land.shape, flight.land.dtype)),
        in_specs=(HBM_SPEC, HBM_SPEC, *[SEM_SPEC] * (2 * n), ANY),
        out_specs=(HBM_SPEC, HBM_SPEC), input_output_aliases={0: 0, 1: 1},
        compiler_params=pltpu.CompilerParams(has_side_effects=DATAFLOW),
    )(flight.src, flight.land, *flight.sems, after)


def _gather_copies(src_ref, land_ref):
    x, y, c = _me()
    return [(src_ref, land_ref.at[2 * x + y], (*chip, c)) for chip in _other_chips(x, y)]


def _gather_start(packed):
    return _split_start("gather_rest_start", packed, (N_CHIPS, *packed.shape), packed.dtype, 3, _gather_copies)


def _gather_wait(flight, after):
    src, others = _split_wait("gather_rest_wait", flight, after, 3, _gather_copies)
    return lax.dynamic_update_slice(others, src[None], (2 * lax.axis_index("x") + lax.axis_index("y"), 0, 0))


def _swap_copies(src_ref, land_ref):
    x, y, c = _me()
    half = land_ref.shape[1]
    return [(src_ref.at[:, pl.ds((1 - c) * half, half), :], land_ref, (x, y, 1 - c))]


def _swap_start(g, tag):
    S, R, W = g.shape
    return _split_start(f"swap_halves_start_{tag}", g, (S, R // 2, W), g.dtype, 1, _swap_copies)


def _swap_wait(flight, after, tag):
    return _split_wait(f"swap_halves_wait_{tag}", flight, after, 1, _swap_copies)


def _scatter_copies(src_ref, land_ref):
    x, y, c = _me()
    return [(src_ref.at[2 * chip[0] + chip[1]], land_ref.at[j], (*chip, c)) for j, chip in enumerate(_other_chips(x, y))]


def _scatter_start(part, tag):
    S, h, W = part.shape
    return _split_start(f"scatter_chips_start_{tag}", part, (S - 1, h, W), part.dtype, 3, _scatter_copies)


def _scatter_wait(flight, after, tag):
    return _split_wait(f"scatter_chips_wait_{tag}", flight, after, 3, _scatter_copies)[1]


def _join_copies(src_ref, land_ref):
    x, y, c = _me()
    h = src_ref.shape[0]
    return [(src_ref, land_ref.at[pl.ds(c * h, h), :], (x, y, 1 - c))]


def _join_start(mine):
    h, W = mine.shape
    return _split_start("join_halves_start", mine, (2 * h, W), mine.dtype, 1, _join_copies)


def _join_wait(flight, after):
    src, other = _split_wait("join_halves_wait", flight, after, 1, _join_copies)
    return lax.dynamic_update_slice(other, src, (lax.axis_index("c") * src.shape[0], 0))


def _adamw(name, g, w, m, v):
    R, C = w.shape
    tm = R
    for cand in (256, 128, 64, 32, 16, 8):
        if R % cand == 0:
            tm = cand
            break

    def body(g, w, m, v):
        m = ADAM_B1 * m + (1.0 - ADAM_B1) * g
        v = ADAM_B2 * v + (1.0 - ADAM_B2) * jnp.square(g)
        m_hat = m / (1.0 - ADAM_B1 ** ADAM_STEP)
        v_hat = v / (1.0 - ADAM_B2 ** ADAM_STEP)
        delta = -ADAM_LR * (m_hat / (jnp.sqrt(v_hat) + ADAM_EPS) + ADAM_WD * w)
        return delta, m, v

    return _rowwise(name, body, R, tm, [_row(t, tm) for t in (g, w, m, v)], [(C, F32)] * 3)


def _unpack_weights(gathered, names):
    S = gathered.shape[0]
    shard_shapes = {"w_in": (D_MODEL, (QKV_WIDTH + 2 * D_MODEL) // S), "w_branch_na": (NA_WIDTH, D_MODEL // S),
                    "w_branch_dil": (DIL_OUT_WIDTH, D_MODEL // S), "w_out": (D_MODEL // S, D_MODEL),
                    "w_up": (D_MODEL, D_FF // S), "w_down": (D_FF // S, D_MODEL),
                    "w_ple_gate": (D_MODEL // S, D_MODEL), "w_ple_proj": (PLE_DIM, D_MODEL // S)}
    col_sharded = {"w_in", "w_branch_na", "w_branch_dil", "w_up", "w_ple_proj"}
    out, r0 = {}, 0
    for name in names:
        rows, cols = shard_shapes[name]
        n = rows * cols // PACK_W
        t = gathered[:, r0:r0 + n, :].reshape(S, rows, cols)
        r0 += n
        out[name] = t.transpose(1, 0, 2).reshape(rows, S * cols) if name in col_sharded else t.reshape(S * rows, cols)
    return out


def _pack_grads(grads, names):
    col_sharded = {"w_in", "w_branch_na", "w_branch_dil", "w_up", "w_ple_proj"}
    per_chip = []
    for s in range(N_CHIPS):
        rows = []
        for name in names:
            g = grads[name]
            if name in col_sharded:
                w = g.shape[1] // N_CHIPS
                rows.append(_pack_rows(g[:, s * w:(s + 1) * w]))
            else:
                h = g.shape[0] // N_CHIPS
                rows.append(_pack_rows(g[s * h:(s + 1) * h]))
        per_chip.append(jnp.concatenate(rows, axis=0))
    return jnp.stack(per_chip)


def _unpack_shard(packed, shapes, names):
    out, r0 = {}, 0
    for name in names:
        rows, cols = shapes[name]
        n = rows * cols // PACK_W
        out[name] = packed[r0:r0 + n].reshape(rows, cols)
        r0 += n
    return out


def kernel(x, p, positions, g_mix, w_in, rpb, w_branch_na, w_branch_dil, w_out, g_mlp, w_up, w_down, g_ple, w_ple_gate, w_ple_proj, g_final, loss_target, m_g_mix, m_w_in, m_rpb, m_w_branch_na, m_w_branch_dil, m_w_out, m_g_mlp, m_w_up, m_w_down, m_g_ple, m_w_ple_gate, m_w_ple_proj, m_g_final, v_g_mix, v_w_in, v_rpb, v_w_branch_na, v_w_branch_dil, v_w_out, v_g_mlp, v_w_up, v_w_down, v_g_ple, v_w_ple_gate, v_w_ple_proj, v_g_final):
    shards = {"w_in": w_in[0], "w_branch_na": w_branch_na[0], "w_branch_dil": w_branch_dil[0], "w_out": w_out[0],
              "w_up": w_up[0], "w_down": w_down[0], "w_ple_gate": w_ple_gate[0], "w_ple_proj": w_ple_proj[0]}
    m_shards = {"w_in": m_w_in[0], "w_branch_na": m_w_branch_na[0], "w_branch_dil": m_w_branch_dil[0], "w_out": m_w_out[0],
                "w_up": m_w_up[0], "w_down": m_w_down[0], "w_ple_gate": m_w_ple_gate[0], "w_ple_proj": m_w_ple_proj[0]}
    v_shards = {"w_in": v_w_in[0], "w_branch_na": v_w_branch_na[0], "w_branch_dil": v_w_branch_dil[0], "w_out": v_w_out[0],
                "w_up": v_w_up[0], "w_down": v_w_down[0], "w_ple_gate": v_w_ple_gate[0], "w_ple_proj": v_w_ple_proj[0]}

    W = _unpack_weights(_gather_weights(_pack_rows(shards["w_in"].astype(BF))), GATHER_FIRST)
    rest_flight = _gather_start(jnp.concatenate([_pack_rows(shards[n].astype(BF)) for n in GATHER_REST], axis=0))
    w_qkv, w_gates = W["w_in"][:, :QKV_WIDTH], W["w_in"][:, QKV_WIDTH:]

    xs, ps, tgt = x[0], p[0, 0], loss_target[0]
    T = xs.shape[0]
    TM = 512
    gm, gl, gp, gf = g_mix, g_mlp, g_ple, g_final.reshape(1, D_MODEL)
    cos2, sin_signed = _rope_tables(positions[0])
    tab = _na_bias_table(rpb[0])

    a = _rowwise("norm_mix", lambda h, g: h * _rms(h) * g, T, TM, [_row(xs, TM), _full(gm)], [(D_MODEL, BF)],
                 after=(rest_flight.token,))
    n3 = 3 * NA_WIDTH
    qkv = _mm("in_na", a, w_qkv[:, :n3], "nn", 1024, 768, 1024, [BF])
    z_dil = _mm("in_dil", a, w_qkv[:, n3:], "nn", 1024, 1152, 1024, [F32])
    z_gates = _mm("in_gates", a, w_gates, "nn", 1024,1024, 1024, [BF])

    dil_ops = _qkv_prep(z_dil, cos2, sin_signed, TM)
    y_na = _na_fwd(qkv, tab)
    band = [_band_fwd(*dil_ops[g], g) for g in range(len(DIL_GROUPS))]
    y_dil, w_grp, o_nat = _dil_merge_fwd([b[0] for b in band], [b[1] for b in band], T, TM)

    W.update(_unpack_weights(_gather_wait(rest_flight, y_dil), GATHER_REST))
    u_na = _mm("branch_na", y_na, W["w_branch_na"], "nn", 1024,1024, 512, [BF])
    u_dil = _mm("branch_dil", y_dil, W["w_branch_dil"], "nn", 1024,1024, 256, [BF])
    mixed = _rowwise(
        "gate_mix", lambda gn, gd, un, ud: _sigmoid(gn.astype(F32)) * un.astype(F32) + _sigmoid(gd.astype(F32)) * ud.astype(F32), T, TM,
        [_row(z_gates, TM, 0, D_MODEL), _row(z_gates, TM, 1, D_MODEL), _row(u_na, TM), _row(u_dil, TM)], [(D_MODEL, BF)])
    mix_out = _mm("out_proj", mixed, W["w_out"], "nn", 1024,1024, 1024, [F32])

    def add_norm(h, d, g):
        h = h + d
        return h, h * _rms(h) * g

    h1, cn = _rowwise("add_norm_mlp", add_norm, T, TM, [_row(xs, TM), _row(mix_out, TM), _full(gl)], [(D_MODEL, F32), (D_MODEL, BF)])
    up, act = _mm("mlp_up", cn, W["w_up"], "nn", 1024,1024, 1024, [BF, BF],
                  epilogue=lambda acc: (acc, jnp.square(jnp.maximum(acc, 0.0))))
    mlp_out = _mm("mlp_down", act, W["w_down"], "nn", 1024,1024, 1024, [F32])
    h2, en = _rowwise("add_norm_ple", add_norm, T, TM, [_row(h1, TM), _row(mlp_out, TM), _full(gp)], [(D_MODEL, F32), (D_MODEL, BF)])
    gt = _mm("ple_gate", en, W["w_ple_gate"], "nn", 1024,1024, 1024, [F32])
    pp = _mm("ple_proj", ps, W["w_ple_proj"], "nn", 1024,1024, 256, [F32])

    def head(h2t, gtt, ppt, tg, g):
        sg = _sigmoid(gtt)
        h3 = h2t + sg * ppt
        yo = h3 * _rms(h3) * g
        diff = yo - tg
        loss = 0.5 * jnp.sum(jnp.mean(jnp.square(diff), axis=-1, keepdims=True), axis=0, keepdims=True)
        dh3, dg = _rms_bwd(diff * (1.0 / D_MODEL), h3, g)
        return dh3, dh3 * ppt * sg * (1.0 - sg), dh3 * sg, jnp.broadcast_to(loss, (1, 128)), dg

    dh3, d_gt, d_pp, loss_part, dg_final = _rowwise(
        "loss_head", head, T, TM, [_row(h2, TM), _row(gt, TM), _row(pp, TM), _row(tgt, TM), _full(gf)],
        [(D_MODEL, F32), (D_MODEL, BF), (D_MODEL, BF)], sums=[128, D_MODEL])

    grads = {}
    grads["w_ple_proj"] = _mm("g_ple_proj", ps, d_pp, "tn", 256, 1024, 1024,[F32])
    grads["w_ple_gate"] = _mm("g_ple_gate", en, d_gt, "tn", 1024, 1024, 1024,[F32])
    d_en = _mm("d_ple_gate", d_gt, W["w_ple_gate"], "nt", 1024,1024, 1024, [F32])

    def add_norm_bwd(dh_out, dn, h, g):
        dh, dg = _rms_bwd(dn, h, g)
        dh = dh_out + dh
        return dh, dh, dg

    dh2, dh2_b, dg_ple = _rowwise("add_norm_ple_bwd", add_norm_bwd, T, TM, [_row(dh3, TM), _row(d_en, TM), _row(h2, TM), _full(gp)],
                                  [(D_MODEL, F32), (D_MODEL, BF)], sums=[D_MODEL])
    d_up = _mm("d_mlp_down", dh2_b, W["w_down"], "nt", 1024,1024, 1024, [BF],
               epilogue=lambda acc, u: (acc * (2.0 * jnp.maximum(u.astype(F32), 0.0)),), extras=(up,))
    grads["w_down"] = _mm("g_mlp_down", act, dh2_b, "tn", 1024, 1024, 1024,[F32])
    grads["w_up"] = _mm("g_mlp_up", cn, d_up, "tn", 1024, 1024, 1024,[F32])
    early_shapes = {n: shards[n].shape for n in REDUCE_EARLY}
    early_tm = sum(r * c for r, c in early_shapes.values()) // PACK_W // 4
    swap_flight = _swap_start(_pack_grads(grads, REDUCE_EARLY), "early")
    d_cn = _mm("d_mlp_up", d_up, W["w_up"], "nt", 1024,1024, 1024, [F32], after=(swap_flight.token,))
    dh1, dh1_b, dg_mlp = _rowwise("add_norm_mlp_bwd", add_norm_bwd, T, TM, [_row(dh2, TM), _row(d_cn, TM), _row(h1, TM), _full(gl)],
                                  [(D_MODEL, F32), (D_MODEL, BF)], sums=[D_MODEL])
    early_g, early_got = _swap_wait(swap_flight, dh1_b, "early")
    early_pair, early_pair_b = _pair_sum(early_g, early_got, early_tm)
    scatter_flight = _scatter_start(early_pair_b, "early")
    d_mixed = _mm("d_out_proj", dh1_b, W["w_out"], "nt", 1024,1024, 1024, [F32], after=(scatter_flight.token,))
    grads["w_out"] = _mm("g_out_proj", mixed, dh1_b, "tn", 1024, 1024, 1024,[F32])

    def gate_bwd(dm, gn, gd, un, ud):
        gn, gd, un, ud = (t.astype(F32) for t in (gn, gd, un, ud))
        sn, sd = _sigmoid(gn), _sigmoid(gd)
        return jnp.concatenate([dm * un * sn * (1.0 - sn), dm * ud * sd * (1.0 - sd)], axis=1), dm * sn, dm * sd

    dz_gates, d_u_na, d_u_dil = _rowwise(
        "gate_mix_bwd", gate_bwd, T, TM,
        [_row(d_mixed, TM), _row(z_gates, TM, 0, D_MODEL), _row(z_gates, TM, 1, D_MODEL), _row(u_na, TM), _row(u_dil, TM)],
        [(2 * D_MODEL, BF), (D_MODEL, BF), (D_MODEL, BF)])
    grads["w_branch_na"] = _mm("g_branch_na", y_na, d_u_na, "tn", 1024, 1024, 1024,[F32])
    grads["w_branch_dil"] = _mm("g_branch_dil", y_dil, d_u_dil, "tn", 256, 1024, 1024,[F32])
    d_y_na = _mm("d_branch_na", d_u_na, W["w_branch_na"], "nt", 1024,512, 1024, [BF])
    d_y_dil = _mm("d_branch_dil", d_u_dil, W["w_branch_dil"], "nt", 1024,256, 1024, [F32])

    dqa, dka, dva, dtab = _na_bwd(qkv, tab, d_y_na)
    d_rpb = _na_rpb_grad(dtab)[:, :2 * NA_WIN_ROWS - 1, :2 * NA_WIN_COLS - 1]

    do_res, dlse_res = _dil_merge_bwd(d_y_dil, o_nat, w_grp, TM)
    d_dil = [_band_bwd(*dil_ops[g], do_res[g], dlse_res[g], g) for g in range(len(DIL_GROUPS))]

    me_chip = 2 * lax.axis_index("x") + lax.axis_index("y")
    early_mine = _chip_sum(lax.dynamic_index_in_dim(early_pair, me_chip, 0, keepdims=False),
                           _scatter_wait(scatter_flight, d_dil[-1][-1], "early"), early_tm)
    join_flight = _join_start(early_mine)
    dz_qkv = _qkv_unprep((dqa, dka, dva), d_dil, cos2, sin_signed, TM, after=(join_flight.token,))
    grads["w_in"] = jnp.concatenate([
        _mm("g_in_qkv", a, dz_qkv, "tn", 1024, 1280, 1024,[F32]),
        _mm("g_in_gates", a, dz_gates, "tn", 1024, 1024, 1024,[F32])], axis=1)
    late_shapes = {n: shards[n].shape for n in REDUCE_LATE}
    late_tm = sum(r * c for r, c in late_shapes.values()) // PACK_W // 4
    late_swap = _swap_start(_pack_grads(grads, REDUCE_LATE), "late")
    d_a = _mm("d_in_qkv", dz_qkv, w_qkv, "nt", 1024,1024, 1280, [F32], after=(late_swap.token,))
    late_g, late_got = _swap_wait(late_swap, d_a, "late")
    late_pair, late_pair_b = _pair_sum(late_g, late_got, late_tm)
    late_scatter = _scatter_start(late_pair_b, "late")
    d_a = _mm("d_in_gates", dz_gates, w_gates, "nt", 1024,1024, 1024, [F32], epilogue=lambda acc, e: (acc + e,), extras=(d_a,),
              after=(late_scatter.token,))

    def first_bwd(dh_out, dn, h, g):
        dh, dg = _rms_bwd(dn, h, g)
        return dh_out + dh, dg

    grad_x, dg_mix = _rowwise("norm_mix_bwd", first_bwd, T, TM, [_row(dh1, TM), _row(d_a, TM), _row(xs, TM), _full(gm)],
                              [(D_MODEL, F32)], sums=[D_MODEL])
    g_shard = _unpack_shard(_join_wait(join_flight, grad_x), early_shapes, REDUCE_EARLY)

    n_rpb = rpb.size
    rpb_rows = 4
    small = jnp.concatenate([
        dg_mix, dg_mlp, dg_ple, dg_final,
        jnp.pad(d_rpb.reshape(-1), (0, rpb_rows * D_MODEL - n_rpb)).reshape(rpb_rows, D_MODEL),
        jnp.pad(loss_part, ((0, 0), (0, D_MODEL - loss_part.shape[1]))),
        jnp.zeros((SMALL_ROWS - 5 - rpb_rows, D_MODEL), F32)], axis=0)
    small = _allreduce_small(small)
    loss = small[4 + rpb_rows, 0]

    def small_pack(a0, a1, a2, a3, r):
        return jnp.concatenate([a0.reshape(1, -1), a1.reshape(1, -1), a2.reshape(1, -1), a3.reshape(1, -1),
                                jnp.pad(r.reshape(-1), (0, rpb_rows * D_MODEL - n_rpb)).reshape(rpb_rows, D_MODEL)], axis=0)

    g_small = small[:4 + rpb_rows]
    small_res = _adamw("adamw_small", g_small, small_pack(g_mix, g_mlp, g_ple, g_final, rpb),
                       small_pack(m_g_mix, m_g_mlp, m_g_ple, m_g_final, m_rpb), small_pack(v_g_mix, v_g_mlp, v_g_ple, v_g_final, v_rpb))

    def small_unpack(t):
        return {"g_mix": t[0].reshape(g_mix.shape), "g_mlp": t[1].reshape(g_mlp.shape), "g_ple": t[2].reshape(g_ple.shape),
                "g_final": t[3].reshape(g_final.shape), "rpb": t[4:].reshape(-1)[:n_rpb].reshape(rpb.shape)}

    out = {"grad": small_unpack(g_small)}
    for kind, t in zip(("delta", "new_m", "new_v"), small_res, strict=True):
        out[kind] = small_unpack(t)
    def update(names):
        for n in names:
            out["grad"][n] = g_shard[n][None]
            res = _adamw("adamw_" + n, g_shard[n], shards[n], m_shards[n], v_shards[n])
            for kind, t in zip(("delta", "new_m", "new_v"), res, strict=True):
                out[kind][n] = t[None]

    update(REDUCE_EARLY)
    late_others = _scatter_wait(late_scatter, out["new_v"][REDUCE_EARLY[-1]], "late")
    late_mine = _chip_sum(lax.dynamic_index_in_dim(late_pair, me_chip, 0, keepdims=False), late_others, late_tm)
    g_shard.update(_unpack_shard(_join_halves(late_mine), late_shapes, REDUCE_LATE))
    update(REDUCE_LATE)

    order = ["g_mix", "w_in", "rpb", "w_branch_na", "w_branch_dil", "w_out", "g_mlp", "w_up", "w_down", "g_ple",
             "w_ple_gate", "w_ple_proj", "g_final"]
    return (loss, grad_x[None], *[out["grad"][n] for n in order], *[out["delta"][n] for n in order],
            *[out["new_m"][n] for n in order], *[out["new_v"][n] for n in order])
```

```python
import functools
from typing import NamedTuple

import numpy as np
import jax
import jax.numpy as jnp
from jax import lax
from jax.experimental import pallas as pl
from jax.experimental.pallas import tpu as pltpu

BF = jnp.bfloat16
F32 = jnp.float32
MESH = pl.DeviceIdType.MESH
ANY = pl.BlockSpec(memory_space=pl.ANY)

V7X_VMEM_BYTES = 64 * 1024 * 1024
VMEM_LIMIT = V7X_VMEM_BYTES - 16 * 1024 * 1024

D_MODEL = 1024
HEAD_DIM = 64
GRID_W = 64
NA_HEADS = 8
NA_WIN_ROWS = 8
NA_WIN_COLS = 16
NA_WIDTH = NA_HEADS * HEAD_DIM
DIL_GROUPS = ((128, 1), (512, 4), (2048, 16))
DIL_HPG = 4
DIL_HEADS = DIL_HPG * len(DIL_GROUPS)
DIL_WIDTH = DIL_HEADS * HEAD_DIM
DIL_OUT_WIDTH = DIL_HPG * HEAD_DIM
DIL_RADIUS = 64
QKV_WIDTH = 3 * NA_WIDTH + 3 * DIL_WIDTH
D_FF = 4 * D_MODEL
PLE_DIM = 256
ROPE_THETA = 10000.0
RMS_EPS = 1e-6
NEG_INF = -1e30
Q_SCALE = HEAD_DIM ** -0.5

ADAM_LR = 0.001
ADAM_B1 = 0.9
ADAM_B2 = 0.999
ADAM_EPS = 1e-08
ADAM_WD = 0.01
ADAM_STEP = 10

N_CHIPS = 4
N_DEV = 8
PACK_W = 1024
BIG = ("w_in", "w_branch_na", "w_branch_dil", "w_out", "w_up", "w_down", "w_ple_gate", "w_ple_proj")
GATHER_FIRST = ("w_in",)
GATHER_MIX = ("w_branch_na", "w_branch_dil", "w_out")
GATHER_MLP = ("w_up", "w_down", "w_ple_gate", "w_ple_proj")
REDUCE_EARLY = ("w_ple_proj", "w_ple_gate", "w_down", "w_up")
REDUCE_LATE = ("w_in", "w_branch_na", "w_branch_dil", "w_out")
SMALL_ROWS = 16


def _cparams(sem=None):
    return pltpu.CompilerParams(dimension_semantics=sem, vmem_limit_bytes=VMEM_LIMIT)


def _mm(name, a, b, mode, tm, tn, tk, out_dtypes, epilogue=None, extras=(), after=()):
    if mode == "nn":
        (M, K), N = a.shape, b.shape[1]
    elif mode == "nt":
        (M, K), N = a.shape, b.shape[0]
    else:
        (K, M), N = a.shape, b.shape[1]
    tm, tn, tk = min(tm, M), min(tn, N), min(tk, K)
    assert M % tm == 0 and N % tn == 0 and K % tk == 0, (name, M, N, K, tm, tn, tk)
    if mode == "nn":
        a_spec = pl.BlockSpec((tm, tk), lambda i, j, k: (i, k))
        b_spec = pl.BlockSpec((tk, tn), lambda i, j, k: (k, j))
        dims = (((1,), (0,)), ((), ()))
    elif mode == "nt":
        a_spec = pl.BlockSpec((tm, tk), lambda i, j, k: (i, k))
        b_spec = pl.BlockSpec((tn, tk), lambda i, j, k: (j, k))
        dims = (((1,), (1,)), ((), ()))
    else:
        a_spec = pl.BlockSpec((tk, tm), lambda i, j, k: (k, i))
        b_spec = pl.BlockSpec((tk, tn), lambda i, j, k: (k, j))
        dims = (((0,), (0,)), ((), ()))
    nk = K // tk
    n_extra, n_out = len(extras), len(out_dtypes)
    tile = pl.BlockSpec((tm, tn), lambda i, j, k: (i, j))

    n_after = len(after)

    def body(a_ref, b_ref, *rest):
        extra_refs, rest = rest[:n_extra], rest[n_extra + n_after:]
        out_refs, acc = rest[:n_out], rest[-1]
        k = pl.program_id(2)

        @pl.when(k == 0)
        def _():
            acc[...] = jnp.zeros_like(acc)

        acc[...] += lax.dot_general(a_ref[...].astype(BF), b_ref[...].astype(BF), dims, preferred_element_type=F32)

        @pl.when(k == nk - 1)
        def _():
            outs = (acc[...],) if epilogue is None else epilogue(acc[...], *[e[...] for e in extra_refs])
            for o_ref, val in zip(out_refs, outs, strict=True):
                o_ref[...] = val.astype(o_ref.dtype)

    outs = pl.pallas_call(
        body, name=name, grid=(M // tm, N // tn, nk),
        in_specs=[a_spec, b_spec] + [tile] * n_extra + [ANY] * n_after,
        out_specs=[tile] * n_out,
        out_shape=[jax.ShapeDtypeStruct((M, N), dt) for dt in out_dtypes],
        scratch_shapes=[pltpu.VMEM((tm, tn), F32)],
        compiler_params=_cparams(("parallel", "parallel", "arbitrary")),
    )(a, b, *extras, *after)
    return outs[0] if n_out == 1 else outs


def _row(arr, tm, col_block=None, width=None):
    width = arr.shape[1] if width is None else width
    cb = 0 if col_block is None else col_block
    return arr, pl.BlockSpec((tm, width), lambda i: (i, cb))


def _full(arr):
    nd = arr.ndim
    return arr, pl.BlockSpec(arr.shape, lambda i: (0,) * nd)


def _rowwise(name, body, T, tm, ins, outs, sums=(), after=()):
    n_in, n_out, n_sum, n_after = len(ins), len(outs), len(sums), len(after)

    def kern(*refs):
        in_refs, refs = refs[:n_in], refs[n_in + n_after:]
        out_refs, sum_refs = refs[:n_out], refs[n_out:]
        res = body(*[r[...] for r in in_refs])
        res = res if isinstance(res, tuple) else (res,)
        for o_ref, val in zip(out_refs, res[:n_out], strict=True):
            o_ref[...] = val.astype(o_ref.dtype)
        if n_sum:
            @pl.when(pl.program_id(0) == 0)
            def _():
                for s_ref in sum_refs:
                    s_ref[...] = jnp.zeros_like(s_ref)

            for s_ref, val in zip(sum_refs, res[n_out:], strict=True):
                s_ref[...] += val

    res = pl.pallas_call(
        kern, name=name, grid=(T // tm,),
        in_specs=[spec for _, spec in ins] + [ANY] * n_after,
        out_specs=[pl.BlockSpec((tm, c), lambda i: (i, 0)) for c, _ in outs]
        + [pl.BlockSpec((1, c), lambda i: (0, 0)) for c in sums],
        out_shape=[jax.ShapeDtypeStruct((T, c), dt) for c, dt in outs]
        + [jax.ShapeDtypeStruct((1, c), F32) for c in sums],
        compiler_params=_cparams(("arbitrary",)),
    )(*[a for a, _ in ins], *after)
    return res[0] if len(res) == 1 else res


def _sigmoid(x):
    return 1.0 / (1.0 + jnp.exp(-x))


def _rms(h):
    return lax.rsqrt(jnp.mean(h * h, axis=-1, keepdims=True) + RMS_EPS)


def _rms_bwd(dy, h, g):
    r = _rms(h)
    n = h * r
    dn = dy * g
    dh = r * (dn - n * jnp.mean(dn * n, axis=-1, keepdims=True))
    return dh, jnp.sum(dy * n, axis=0, keepdims=True)


def _rope(x, cos2, sin_signed):
    lane = lax.broadcasted_iota(jnp.int32, x.shape, 1)
    swapped = jnp.where((lane % HEAD_DIM) < HEAD_DIM // 2, pltpu.roll(x, 128 - HEAD_DIM // 2, 1), pltpu.roll(x, HEAD_DIM // 2, 1))
    return x * cos2 + swapped * sin_signed


def _rope_cols(x, cos2, sin_signed):
    return jnp.concatenate([_rope(x[:, c:c + 128], cos2, sin_signed) for c in range(0, x.shape[1], 128)], axis=1)


NA_KEYS = NA_WIN_ROWS * GRID_W
NA_BASES = 8


def _na_row_geometry(r, rows):
    first = jnp.clip(r - NA_WIN_ROWS // 2, 0, rows - NA_WIN_ROWS)
    base = first - r + (NA_WIN_ROWS - 1)
    return pl.multiple_of(first * GRID_W, GRID_W), base


NA_ROWS_PER_STEP = 8
NA_BWD_ROWS_PER_STEP = 8


def _softmax_rows(s):
    p = jnp.exp(s - jnp.max(s, axis=-1, keepdims=True))
    return p / jnp.sum(p, axis=-1, keepdims=True)


def _na_probs(q, kw, bias):
    return _softmax_rows(lax.dot_general(q, kw, (((1,), (1,)), ((), ())), preferred_element_type=F32) + bias)


def _split_pair(t):
    first = lax.broadcasted_iota(jnp.int32, t.shape, 1) < HEAD_DIM
    zero = jnp.zeros_like(t)
    return jnp.where(first, t, zero), jnp.where(first, zero, t)


def _join_pair(a, b):
    return jnp.where(lax.broadcasted_iota(jnp.int32, a.shape, 1) < HEAD_DIM, a, b)


_NT = (((1,), (1,)), ((), ()))
_TN = (((0,), (0,)), ((), ()))


def _na_fwd(qkv, tab):
    T = qkv.shape[0]
    rows = T // GRID_W
    n_pairs = NA_WIDTH // 128

    def body(q_ref, k_ref, v_ref, tab_ref, y_ref):
        def step(it, carry):
            geo = [_na_row_geometry(it * NA_ROWS_PER_STEP + u, rows) for u in range(NA_ROWS_PER_STEP)]
            q0s = [pl.multiple_of((it * NA_ROWS_PER_STEP + u) * GRID_W, GRID_W) for u in range(NA_ROWS_PER_STEP)]
            ss = [lax.dot_general(jnp.concatenate(_split_pair(q_ref[pl.ds(q0, GRID_W), :] * Q_SCALE), axis=0),
                                  k_ref[pl.ds(k0, NA_KEYS), :], _NT, preferred_element_type=F32)
                  for q0, (k0, _) in zip(q0s, geo)]
            ps = [_softmax_rows(s + jnp.concatenate([tab_ref[0, base], tab_ref[1, base]], axis=0)) for s, (_, base) in zip(ss, geo)]
            ys = [jnp.dot(p.astype(BF), v_ref[pl.ds(k0, NA_KEYS), :], preferred_element_type=F32) for p, (k0, _) in zip(ps, geo)]
            for q0, y2 in zip(q0s, ys):
                y_ref[pl.ds(q0, GRID_W), :] = _join_pair(y2[:GRID_W], y2[GRID_W:]).astype(y_ref.dtype)
            return carry

        lax.fori_loop(0, rows // NA_ROWS_PER_STEP, step, 0)

    def cols(first):
        return pl.BlockSpec((T, 128), lambda j: (0, first + j))

    return pl.pallas_call(
        body, name="na_fwd", grid=(n_pairs,),
        in_specs=[cols(0), cols(n_pairs), cols(2 * n_pairs), pl.BlockSpec((2, NA_BASES, GRID_W, NA_KEYS), lambda j: (j, 0, 0, 0))],
        out_specs=cols(0), out_shape=jax.ShapeDtypeStruct((T, NA_WIDTH), BF),
        compiler_params=_cparams(("parallel",)),
    )(qkv, qkv, qkv, tab)


def _na_bwd(qkv, tab, do):
    T = qkv.shape[0]
    rows = T // GRID_W
    n_pairs = NA_WIDTH // 128

    def body(q_ref, k_ref, v_ref, tab_ref, do_ref, dq_ref, dk_ref, dv_ref, dtab_ref):
        dk_ref[...] = jnp.zeros_like(dk_ref)
        dv_ref[...] = jnp.zeros_like(dv_ref)
        dtab_ref[...] = jnp.zeros_like(dtab_ref)

        def step(it, carry):
            U = NA_BWD_ROWS_PER_STEP
            geo = [_na_row_geometry(it * U + u, rows) for u in range(U)]
            q0s = [pl.multiple_of((it * U + u) * GRID_W, GRID_W) for u in range(U)]
            q2s = [jnp.concatenate(_split_pair(q_ref[pl.ds(q0, GRID_W), :] * Q_SCALE), axis=0) for q0 in q0s]
            do2s = [jnp.concatenate(_split_pair(do_ref[pl.ds(q0, GRID_W), :]), axis=0) for q0 in q0s]
            ss = [lax.dot_general(q2, k_ref[pl.ds(k0, NA_KEYS), :], _NT, preferred_element_type=F32) for q2, (k0, _) in zip(q2s, geo)]
            dps = [lax.dot_general(do2, v_ref[pl.ds(k0, NA_KEYS), :], _NT, preferred_element_type=F32) for do2, (k0, _) in zip(do2s, geo)]
            ps = [_softmax_rows(s + jnp.concatenate([tab_ref[0, base], tab_ref[1, base]], axis=0)) for s, (_, base) in zip(ss, geo)]
            dss = [p * (dp - jnp.sum(dp * p, axis=-1, keepdims=True)) for p, dp in zip(ps, dps)]
            dvs = [lax.dot_general(p.astype(BF), do2, _TN, preferred_element_type=F32) for p, do2 in zip(ps, do2s)]
            dsbs = [ds.astype(BF) for ds in dss]
            dqs = [jnp.dot(dsb, k_ref[pl.ds(k0, NA_KEYS), :], preferred_element_type=F32) for dsb, (k0, _) in zip(dsbs, geo)]
            dks = [lax.dot_general(dsb, q2, _TN, preferred_element_type=F32) for dsb, q2 in zip(dsbs, q2s)]
            for u in range(U):
                k0, base = geo[u]
                dtab_ref[0, base] += dss[u][:GRID_W]
                dtab_ref[1, base] += dss[u][GRID_W:]
                dq_ref[pl.ds(q0s[u], GRID_W), :] = _join_pair(dqs[u][:GRID_W], dqs[u][GRID_W:])
                dk_ref[pl.ds(k0, NA_KEYS), :] += dks[u]
                dv_ref[pl.ds(k0, NA_KEYS), :] += dvs[u]
            return carry

        lax.fori_loop(0, rows // NA_BWD_ROWS_PER_STEP, step, 0)

    def cols(first):
        return pl.BlockSpec((T, 128), lambda j: (0, first + j))

    tabs = pl.BlockSpec((2, NA_BASES, GRID_W, NA_KEYS), lambda j: (j, 0, 0, 0))
    wide = jax.ShapeDtypeStruct((T, NA_WIDTH), F32)
    return pl.pallas_call(
        body, name="na_bwd", grid=(n_pairs,),
        in_specs=[cols(0), cols(n_pairs), cols(2 * n_pairs), tabs, cols(0)],
        out_specs=[cols(0), cols(0), cols(0), tabs],
        out_shape=[wide, wide, wide, jax.ShapeDtypeStruct((NA_HEADS, NA_BASES, GRID_W, NA_KEYS), F32)],
        compiler_params=_cparams(("parallel",)),
    )(qkv, qkv, qkv, tab, do)


def _na_geometry_np():
    c = np.arange(GRID_W)
    cs = np.clip(c - NA_WIN_COLS // 2, 0, GRID_W - NA_WIN_COLS)
    valid = (c[None, :] >= cs[:, None]) & (c[None, :] < cs[:, None] + NA_WIN_COLS)
    off = c[None, :] - c[:, None] + (NA_WIN_COLS - 1)
    oh_col = np.zeros((GRID_W, GRID_W, 2 * NA_WIN_COLS - 1), np.float32)
    qq, kk = np.nonzero(valid)
    oh_col[qq, kk, off[qq, kk]] = 1.0
    oh_row = np.zeros((NA_BASES, NA_WIN_ROWS, 2 * NA_WIN_ROWS - 1), np.float32)
    for base in range(NA_BASES):
        for i in range(NA_WIN_ROWS):
            oh_row[base, i, base + i] = 1.0
    return valid, oh_col, oh_row


def _na_bias_table(rpb):
    valid, oh_col, oh_row = _na_geometry_np()
    hi = lax.Precision.HIGHEST
    t1 = jnp.einsum("hrc,pir->hpic", rpb, oh_row, precision=hi)
    tab = jnp.einsum("hpic,qkc->hpqik", t1, oh_col, precision=hi)
    tab = jnp.where(valid[None, None, :, None, :], tab, NEG_INF)
    return tab.reshape(rpb.shape[0], NA_BASES, GRID_W, NA_KEYS)


def _na_rpb_grad(dtab):
    H = dtab.shape[0]
    n_rows = 2 * NA_WIN_ROWS - 1
    n_cols = 2 * NA_WIN_COLS - 1

    def body(d_ref, o_ref):
        lane = lax.broadcasted_iota(jnp.int32, (GRID_W, 128), 1)
        low = lane < GRID_W
        out_rows = []
        for ro in range(n_rows):
            acc = jnp.zeros((GRID_W, 128), F32)
            for base in range(NA_BASES):
                i = ro - base
                if not 0 <= i < NA_WIN_ROWS:
                    continue
                pair = d_ref[base, :, pl.ds((i // 2) * 128, 128)]
                if i % 2:
                    pair = pltpu.roll(pair, GRID_W, 1)
                acc = acc + jnp.where(low, pair, 0.0)
            skew = pltpu.roll(acc, 0, 1, stride=1, stride_axis=0)
            diag = jnp.sum(skew, axis=0, keepdims=True)
            out_rows.append(pltpu.roll(jnp.broadcast_to(diag, (8, 128)), 128 - (GRID_W - NA_WIN_COLS), 1)[:1])
        out_rows.append(jnp.zeros((1, 128), F32))
        res = jnp.concatenate(out_rows, axis=0)
        o_ref[...] = jnp.where(lax.broadcasted_iota(jnp.int32, res.shape, 1) < n_cols, res, 0.0)

    return pl.pallas_call(
        body, name="na_rpb_grad", grid=(H,),
        in_specs=[pl.BlockSpec((None, NA_BASES, GRID_W, NA_KEYS), lambda h: (h, 0, 0, 0))],
        out_specs=pl.BlockSpec((None, n_rows + 1, 128), lambda h: (h, 0, 0)),
        out_shape=jax.ShapeDtypeStruct((H, n_rows + 1, 128), F32),
        compiler_params=_cparams(("parallel",)),
    )(jnp.flip(dtab, axis=2))


BAND_Q = 128
BAND_KEYS = BAND_Q + 2 * DIL_RADIUS


def _band_geometry(n, L):
    q0 = pl.multiple_of(n * BAND_Q, BAND_Q)
    k0 = pl.multiple_of(jnp.clip(q0 - DIL_RADIUS, 0, L - BAND_KEYS), DIL_RADIUS)
    qi = q0 + lax.broadcasted_iota(jnp.int32, (BAND_Q, BAND_KEYS), 0)
    kj = k0 + lax.broadcasted_iota(jnp.int32, (BAND_Q, BAND_KEYS), 1)
    return q0, k0, jnp.abs(qi - kj) <= DIL_RADIUS


DIL_PAIRS = DIL_OUT_WIDTH // 128


def _residue_shape(dil, T, dtype):
    return jax.ShapeDtypeStruct((DIL_PAIRS, dil, T // dil, 128), dtype)


def _residue_tile(dil, tm):
    return pl.BlockSpec((DIL_PAIRS, dil, tm // dil, 128), lambda i: (0, 0, i, 0))


def _to_natural(ref, scratch, dil, tm):
    tiles = []
    for pair in range(DIL_PAIRS):
        if dil == 1:
            tiles.append(ref[pair, 0].astype(F32))
            continue
        for r in range(dil):
            scratch[pl.ds(r, tm // dil, stride=dil), :] = ref[pair, r].astype(F32)
        tiles.append(scratch[...])
    return tiles


def _from_natural(tile, scratch, ref, pair, dil, tm):
    if dil == 1:
        ref[pair, 0] = tile.astype(ref.dtype)
        return
    scratch[...] = tile
    for r in range(dil):
        ref[pair, r] = scratch[pl.ds(r, tm // dil, stride=dil), :].astype(ref.dtype)


def _band_specs(group, T):
    dil = DIL_GROUPS[group][1]
    L = T // dil
    assert L % BAND_Q == 0 and L >= BAND_KEYS, (T, dil)
    return L, (dil * DIL_PAIRS,), pl.BlockSpec((None, None, L, 128), lambda s: (s % DIL_PAIRS, s // DIL_PAIRS, 0, 0))


BAND_BLOCKS_PER_STEP = 4


def _band_softmax(s, valid):
    s = jnp.where(valid, s, NEG_INF)
    m = jnp.max(s, axis=-1, keepdims=True)
    p = jnp.exp(s - m)
    l = jnp.sum(p, axis=-1, keepdims=True)
    return p / l, m + jnp.log(l)


def _band_fwd(q, k, v, group):
    T = q.shape[1] * q.shape[2]
    L, grid, spec = _band_specs(group, T)
    U = min(BAND_BLOCKS_PER_STEP, L // BAND_Q)

    def body(q_ref, k_ref, v_ref, o_ref, lse_ref):
        def step(it, carry):
            geo = [_band_geometry(it * U + u, L) for u in range(U)]
            ss = [lax.dot_general(jnp.concatenate(_split_pair(q_ref[pl.ds(q0, BAND_Q), :]), axis=0),
                                  k_ref[pl.ds(k0, BAND_KEYS), :], _NT, preferred_element_type=F32) for q0, k0, _ in geo]
            pls = [_band_softmax(s, jnp.concatenate([valid, valid], axis=0)) for s, (_, _, valid) in zip(ss, geo)]
            os = [jnp.dot(p.astype(BF), v_ref[pl.ds(k0, BAND_KEYS), :], preferred_element_type=F32) for (p, _), (_, k0, _) in zip(pls, geo)]
            for (q0, _, _), o2, (_, lse) in zip(geo, os, pls):
                o_ref[pl.ds(q0, BAND_Q), :] = _join_pair(o2[:BAND_Q], o2[BAND_Q:])
                lse2 = jnp.broadcast_to(lse, (2 * BAND_Q, 128))
                lse_ref[pl.ds(q0, BAND_Q), :] = _join_pair(lse2[:BAND_Q], lse2[BAND_Q:])
            return carry

        lax.fori_loop(0, L // (BAND_Q * U), step, 0)

    res = _residue_shape(DIL_GROUPS[group][1], T, F32)
    return pl.pallas_call(
        body, name=f"band_fwd_g{group}", grid=grid,
        in_specs=[spec] * 3, out_specs=[spec] * 2, out_shape=[res, res],
        compiler_params=_cparams(("parallel",)),
    )(q, k, v)


def _band_bwd(q, k, v, do, dlse, group):
    T = q.shape[1] * q.shape[2]
    L, grid, spec = _band_specs(group, T)
    U = min(BAND_BLOCKS_PER_STEP, L // BAND_Q)

    def body(q_ref, k_ref, v_ref, do_ref, dlse_ref, dq_ref, dk_ref, dv_ref):
        dk_ref[...] = jnp.zeros_like(dk_ref)
        dv_ref[...] = jnp.zeros_like(dv_ref)

        def step(it, carry):
            geo = [_band_geometry(it * U + u, L) for u in range(U)]
            q2s = [jnp.concatenate(_split_pair(q_ref[pl.ds(q0, BAND_Q), :]), axis=0) for q0, _, _ in geo]
            do2s = [jnp.concatenate(_split_pair(do_ref[pl.ds(q0, BAND_Q), :]), axis=0) for q0, _, _ in geo]
            ss = [lax.dot_general(q2, k_ref[pl.ds(k0, BAND_KEYS), :], _NT, preferred_element_type=F32) for q2, (_, k0, _) in zip(q2s, geo)]
            dps = [lax.dot_general(do2, v_ref[pl.ds(k0, BAND_KEYS), :], _NT, preferred_element_type=F32) for do2, (_, k0, _) in zip(do2s, geo)]
            ps = [_band_softmax(s, jnp.concatenate([valid, valid], axis=0))[0] for s, (_, _, valid) in zip(ss, geo)]
            dss = []
            for p, dp, (q0, _, _) in zip(ps, dps, geo):
                dl = dlse_ref[pl.ds(q0, BAND_Q), :]
                dl2 = jnp.concatenate([dl[:, :1], dl[:, HEAD_DIM:HEAD_DIM + 1]], axis=0)
                dss.append(p * (dp - jnp.sum(dp * p, axis=-1, keepdims=True) + dl2))
            dvs = [lax.dot_general(p.astype(BF), do2, _TN, preferred_element_type=F32) for p, do2 in zip(ps, do2s)]
            dsbs = [ds.astype(BF) for ds in dss]
            dqs = [jnp.dot(dsb, k_ref[pl.ds(k0, BAND_KEYS), :], preferred_element_type=F32) for dsb, (_, k0, _) in zip(dsbs, geo)]
            dks = [lax.dot_general(dsb, q2, _TN, preferred_element_type=F32) for dsb, q2 in zip(dsbs, q2s)]
            for u, (q0, k0, _) in enumerate(geo):
                dq_ref[pl.ds(q0, BAND_Q), :] = _join_pair(dqs[u][:BAND_Q], dqs[u][BAND_Q:])
                dk_ref[pl.ds(k0, BAND_KEYS), :] += dks[u]
                dv_ref[pl.ds(k0, BAND_KEYS), :] += dvs[u]
            return carry

        lax.fori_loop(0, L // (BAND_Q * U), step, 0)

    res = _residue_shape(DIL_GROUPS[group][1], T, F32)
    return pl.pallas_call(
        body, name=f"band_bwd_g{group}", grid=grid,
        in_specs=[spec] * 5, out_specs=[spec] * 3, out_shape=[res] * 3,
        compiler_params=_cparams(("parallel",)),
    )(q, k, v, do, dlse)


def _head_sums(t):
    head = lax.broadcasted_iota(jnp.int32, t.shape, 1) // HEAD_DIM
    out = jnp.zeros_like(t)
    for h in range(t.shape[1] // HEAD_DIM):
        mine = head == h
        out = jnp.where(mine, jnp.sum(jnp.where(mine, t, 0.0), axis=-1, keepdims=True), out)
    return out


def _dil_merge_fwd(os, lses, T, tm):
    G = len(DIL_GROUPS)
    W = DIL_OUT_WIDTH
    dils = [d for _, d in DIL_GROUPS]

    def body(*refs):
        o_refs, lse_refs = refs[:G], refs[G:2 * G]
        y_ref, w_refs, on_refs, scratch = refs[2 * G], refs[2 * G + 1:3 * G + 1], refs[3 * G + 1:4 * G + 1], refs[-1]
        o = [jnp.concatenate(_to_natural(r, scratch, d, tm), axis=1) for r, d in zip(o_refs, dils)]
        ls = [jnp.concatenate(_to_natural(r, scratch, d, tm), axis=1) for r, d in zip(lse_refs, dils)]
        m = functools.reduce(jnp.maximum, ls)
        es = [jnp.exp(l - m) for l in ls]
        tot = functools.reduce(jnp.add, es)
        ws = [e / tot for e in es]
        y_ref[...] = functools.reduce(jnp.add, [w * t for w, t in zip(ws, o)]).astype(y_ref.dtype)
        for g in range(G):
            w_refs[g][...] = ws[g]
            on_refs[g][...] = o[g]

    nat = pl.BlockSpec((tm, W), lambda i: (i, 0))
    res = pl.pallas_call(
        body, name="dil_merge_fwd", grid=(T // tm,),
        in_specs=[_residue_tile(d, tm) for d in dils] * 2,
        out_specs=[nat] * (2 * G + 1),
        out_shape=[jax.ShapeDtypeStruct((T, W), BF)] + [jax.ShapeDtypeStruct((T, W), F32)] * (2 * G),
        scratch_shapes=[pltpu.VMEM((tm, 128), F32)],
        compiler_params=_cparams(("parallel",)),
    )(*os, *lses)
    return res[0], res[1:G + 1], res[G + 1:]


def _dil_merge_bwd(dy, os, ws, tm):
    G = len(DIL_GROUPS)
    T, W = dy.shape
    dils = [d for _, d in DIL_GROUPS]

    def body(*refs):
        dyt = refs[0][...]
        o, w = [r[...] for r in refs[1:G + 1]], [r[...] for r in refs[G + 1:2 * G + 1]]
        do_refs, dlse_refs, scratch = refs[2 * G + 1:3 * G + 1], refs[3 * G + 1:4 * G + 1], refs[-1]
        dws = [_head_sums(dyt * t) for t in o]
        mean = functools.reduce(jnp.add, [a * b for a, b in zip(w, dws)])
        for g, d in enumerate(dils):
            do, dlse = w[g] * dyt, w[g] * (dws[g] - mean)
            for pair in range(DIL_PAIRS):
                cols = slice(pair * 128, (pair + 1) * 128)
                _from_natural(do[:, cols], scratch, do_refs[g], pair, d, tm)
                _from_natural(dlse[:, cols], scratch, dlse_refs[g], pair, d, tm)

    nat = pl.BlockSpec((tm, W), lambda i: (i, 0))
    res = pl.pallas_call(
        body, name="dil_merge_bwd", grid=(T // tm,),
        in_specs=[nat] * (2 * G + 1),
        out_specs=[_residue_tile(d, tm) for d in dils] * 2,
        out_shape=[_residue_shape(d, T, BF) for d in dils] + [_residue_shape(d, T, F32) for d in dils],
        scratch_shapes=[pltpu.VMEM((tm, 128), F32)],
        compiler_params=_cparams(("parallel",)),
    )(dy, *os, *ws)
    return res[:G], res[G:]


def _qkv_prep(z, cos2, sin_signed, tm):
    T = z.shape[0]
    G = len(DIL_GROUPS)
    dils = [d for _, d in DIL_GROUPS]
    n_dil_blocks = 3 * DIL_WIDTH // 128

    def body(*refs):
        blocks = refs[:n_dil_blocks]
        cos_ref, sin_ref = refs[n_dil_blocks], refs[1 + n_dil_blocks]
        outs = refs[2 + n_dil_blocks:]
        for part in range(3):
            for g, d in enumerate(dils):
                out = outs[g * 3 + part]
                for pair in range(DIL_PAIRS):
                    blk = blocks[part * (DIL_WIDTH // 128) + g * DIL_PAIRS + pair]
                    for r in range(d):
                        rows = pl.ds(r, tm // d, stride=d) if d > 1 else slice(None)
                        x = blk[rows, :]
                        if part < 2:
                            x = _rope(x, cos_ref[rows, :], sin_ref[rows, :])
                        if part == 0:
                            x = x * Q_SCALE
                        out[pair, r] = x.astype(out.dtype)

    lane_block = [pl.BlockSpec((tm, 128), functools.partial(lambda b, i: (i, b), b)) for b in range(n_dil_blocks)]
    tab = pl.BlockSpec((tm, 128), lambda i: (i, 0))
    res = pl.pallas_call(
        body, name="qkv_prep", grid=(T // tm,),
        in_specs=lane_block + [tab, tab],
        out_specs=[_residue_tile(d, tm) for d in dils for _ in range(3)],
        out_shape=[_residue_shape(d, T, BF) for d in dils for _ in range(3)],
        compiler_params=_cparams(("parallel",)),
    )(*[z] * n_dil_blocks, cos2, sin_signed)
    return [res[3 * g:3 + 3 * g] for g in range(G)]


def _qkv_unprep(d_na, d_dil, cos2, sin_signed, tm, after=()):
    T = d_na[0].shape[0]
    G = len(DIL_GROUPS)
    dils = [d for _, d in DIL_GROUPS]
    n_after = len(after)

    def body(*refs):
        dq, dk, dv = (r[...] for r in refs[:3])
        res_refs = refs[3:3 + 3 * G]
        cs, sn = refs[3 + 3 * G][...], refs[4 + 3 * G][...]
        out, scratch = refs[5 + 3 * G + n_after], refs[-1]
        cols = [dq * Q_SCALE, dk, dv]
        for part in range(3):
            for g, d in enumerate(dils):
                for x in _to_natural(res_refs[g * 3 + part], scratch, d, tm):
                    if part < 2:
                        x = _rope(x, cs, -sn)
                    cols.append(x * Q_SCALE if part == 0 else x)
        out[...] = jnp.concatenate(cols, axis=1).astype(out.dtype)

    wide = pl.BlockSpec((tm, NA_WIDTH), lambda i: (i, 0))
    tab = pl.BlockSpec((tm, 128), lambda i: (i, 0))
    return pl.pallas_call(
        body, name="qkv_unprep", grid=(T // tm,),
        in_specs=[wide] * 3 + [_residue_tile(d, tm) for d in dils for _ in range(3)] + [tab, tab] + [ANY] * n_after,
        out_specs=pl.BlockSpec((tm, QKV_WIDTH), lambda i: (i, 0)),
        out_shape=jax.ShapeDtypeStruct((T, QKV_WIDTH), BF),
        scratch_shapes=[pltpu.VMEM((tm, 128), F32)],
        compiler_params=_cparams(("parallel",)),
    )(*d_na, *[t for g in range(G) for t in d_dil[g]], cos2, sin_signed, *after)


def _rope_tables(positions):
    half = HEAD_DIM // 2
    inv_freq = ROPE_THETA ** (-jnp.arange(half, dtype=F32) / half)
    ang = positions.astype(F32)[:, None] * inv_freq
    cos, sin = jnp.cos(ang), jnp.sin(ang)
    return jnp.tile(jnp.concatenate([cos, cos], axis=1), (1, 2)), jnp.tile(jnp.concatenate([-sin, sin], axis=1), (1, 2))


def _pack_rows(t):
    return t.reshape(-1, PACK_W)


def _me():
    return lax.axis_index("x"), lax.axis_index("y"), lax.axis_index("c")


def _other_chips(x, y):
    return [(1 - x, y), (x, 1 - y), (1 - x, 1 - y)]


def _gather_weights(packed):
    R, W = packed.shape
    half = R // 2

    def body(in_ref, out_ref, send_sems, recv_sems):
        x, y, c = _me()
        sibling = (x, y, 1 - c)
        chips = _other_chips(x, y)

        def block(chip, core):
            return out_ref.at[2 * chip[0] + chip[1], pl.ds(core * half, half), :]

        def copy(k, chip, core, to, src=None):
            return pltpu.make_async_remote_copy(
                src_ref=block(chip, core) if src is None else src, dst_ref=block(chip, core),
                send_sem=send_sems.at[k], recv_sem=recv_sems.at[k], device_id=to, device_id_type=MESH)

        first = [copy(j, (x, y), c, (*chip, c), src=in_ref.at[pl.ds(c * half, half), :]) for j, chip in enumerate(chips)]
        for cp in first:
            cp.start()
        passed = [copy(3 + j, chip, c, sibling) for j, chip in enumerate(chips)]
        for j, chip in enumerate(chips):
            copy(j, chip, c, (x, y, c)).wait_recv()
            passed[j].start()
        for j, chip in enumerate(chips):
            copy(3 + j, chip, 1 - c, (x, y, c)).wait_recv()
        for cp in first + passed:
            cp.wait_send()

    others = pl.pallas_call(
        body, name="gather_weights",
        in_specs=[ANY], out_specs=ANY,
        out_shape=jax.ShapeDtypeStruct((N_CHIPS, R, W), packed.dtype),
        scratch_shapes=[pltpu.SemaphoreType.DMA((6,)), pltpu.SemaphoreType.DMA((6,))],
    )(packed)
    return lax.dynamic_update_slice(others, packed[None], (2 * lax.axis_index("x") + lax.axis_index("y"), 0, 0))


def _swap_halves(g):
    S, R, W = g.shape
    half = R // 2

    def body(g_ref, out_ref, send_sem, recv_sem):
        x, y, c = _me()
        cp = pltpu.make_async_remote_copy(
            src_ref=g_ref.at[:, pl.ds((1 - c) * half, half), :], dst_ref=out_ref,
            send_sem=send_sem, recv_sem=recv_sem, device_id=(x, y, 1 - c), device_id_type=MESH)
        cp.start()
        cp.wait()

    return pl.pallas_call(
        body, name="swap_halves", in_specs=[ANY], out_specs=ANY,
        out_shape=jax.ShapeDtypeStruct((S, half, W), g.dtype),
        scratch_shapes=[pltpu.SemaphoreType.DMA, pltpu.SemaphoreType.DMA],
    )(g)


def _pair_sum(g, got, tm):
    S, R, W = g.shape
    half = R // 2
    nb = half // tm

    def body(c_ref, g_ref, got_ref, o_ref, ob_ref):
        tot = g_ref[...] + got_ref[...]
        o_ref[...] = tot
        ob_ref[...] = tot.astype(ob_ref.dtype)

    tile = pl.BlockSpec((None, tm, W), lambda s, i, c_ref: (s, i, 0))
    return pl.pallas_call(
        body, name="pair_sum",
        grid_spec=pltpu.PrefetchScalarGridSpec(
            num_scalar_prefetch=1, grid=(S, nb),
            in_specs=[pl.BlockSpec((None, tm, W), lambda s, i, c_ref: (s, c_ref[0] * nb + i, 0)), tile],
            out_specs=[tile, tile]),
        out_shape=[jax.ShapeDtypeStruct((S, half, W), F32), jax.ShapeDtypeStruct((S, half, W), BF)],
        compiler_params=_cparams(("parallel", "parallel")),
    )(lax.axis_index("c").reshape(1).astype(jnp.int32), g, got)


def _scatter_chips(part):
    S, h, W = part.shape

    def body(p_ref, out_ref, send_sems, recv_sems):
        x, y, c = _me()
        chips = _other_chips(x, y)
        sends = [pltpu.make_async_remote_copy(
            src_ref=p_ref.at[2 * chip[0] + chip[1]], dst_ref=out_ref.at[j],
            send_sem=send_sems.at[j], recv_sem=recv_sems.at[j], device_id=(*chip, c), device_id_type=MESH)
            for j, chip in enumerate(chips)]
        for cp in sends:
            cp.start()
        for cp in sends:
            cp.wait()

    return pl.pallas_call(
        body, name="scatter_chips", in_specs=[ANY], out_specs=ANY,
        out_shape=jax.ShapeDtypeStruct((S - 1, h, W), part.dtype),
        scratch_shapes=[pltpu.SemaphoreType.DMA((3,)), pltpu.SemaphoreType.DMA((3,))],
    )(part)


def _chip_sum(own, others, tm):
    n, h, W = others.shape

    def body(own_ref, p_ref, o_ref):
        o_ref[...] = ((own_ref[...] + p_ref[0].astype(F32)) + p_ref[1].astype(F32)) + p_ref[2].astype(F32)

    return pl.pallas_call(
        body, name="chip_sum", grid=(h // tm,),
        in_specs=[pl.BlockSpec((tm, W), lambda i: (i, 0)), pl.BlockSpec((n, tm, W), lambda i: (0, i, 0))],
        out_specs=pl.BlockSpec((tm, W), lambda i: (i, 0)),
        out_shape=jax.ShapeDtypeStruct((h, W), F32),
        compiler_params=_cparams(("parallel",)),
    )(own, others)


def _join_halves(mine):
    h, W = mine.shape

    def body(m_ref, out_ref, send_sem, recv_sem):
        x, y, c = _me()
        cp = pltpu.make_async_remote_copy(
            src_ref=m_ref, dst_ref=out_ref.at[pl.ds(c * h, h), :],
            send_sem=send_sem, recv_sem=recv_sem, device_id=(x, y, 1 - c), device_id_type=MESH)
        cp.start()
        pltpu.make_async_remote_copy(
            src_ref=m_ref, dst_ref=out_ref.at[pl.ds((1 - c) * h, h), :],
            send_sem=send_sem, recv_sem=recv_sem, device_id=(x, y, 1 - c), device_id_type=MESH).wait_recv()
        cp.wait_send()

    other = pl.pallas_call(
        body, name="join_halves", in_specs=[ANY], out_specs=ANY,
        out_shape=jax.ShapeDtypeStruct((2 * h, W), mine.dtype),
        scratch_shapes=[pltpu.SemaphoreType.DMA, pltpu.SemaphoreType.DMA],
    )(mine)
    return lax.dynamic_update_slice(other, mine, (lax.axis_index("c") * h, 0))


def _allreduce_small(s):
    R, W = s.shape

    def body(s_ref, o_ref, buf, send_sems, recv_sems):
        x, y, c = _me()
        me = 4 * x + 2 * y + c
        buf[me] = s_ref[...]
        peers = [((x + fx) % 2, (y + fy) % 2, (c + fc) % 2) for fx in range(2) for fy in range(2) for fc in range(2)][1:]
        sends = [pltpu.make_async_remote_copy(
            src_ref=s_ref, dst_ref=buf.at[me], send_sem=send_sems.at[k], recv_sem=recv_sems.at[k],
            device_id=peer, device_id_type=MESH) for k, peer in enumerate(peers)]
        for cp in sends:
            cp.start()
        for k, peer in enumerate(peers):
            pltpu.make_async_remote_copy(
                src_ref=s_ref, dst_ref=buf.at[4 * peer[0] + 2 * peer[1] + peer[2]], send_sem=send_sems.at[k],
                recv_sem=recv_sems.at[k], device_id=peer, device_id_type=MESH).wait_recv()
        for cp in sends:
            cp.wait_send()
        total = buf[0]
        for d in range(1, N_DEV):
            total = total + buf[d]
        o_ref[...] = total

    return pl.pallas_call(
        body, name="allreduce_small",
        in_specs=[pl.BlockSpec(memory_space=pltpu.VMEM)], out_specs=pl.BlockSpec(memory_space=pltpu.VMEM),
        out_shape=jax.ShapeDtypeStruct((R, W), F32),
        scratch_shapes=[pltpu.VMEM((N_DEV, R, W), F32), pltpu.SemaphoreType.DMA((N_DEV - 1,)), pltpu.SemaphoreType.DMA((N_DEV - 1,))],
    )(s)


HBM_SPEC = pl.BlockSpec(memory_space=pltpu.HBM)
SEM_SPEC = pl.BlockSpec(memory_space=pltpu.SEMAPHORE)
DATAFLOW = pltpu.SideEffectType.DATAFLOW_SIDE_EFFECTING


class _InFlight(NamedTuple):
    sems: tuple
    src: jax.Array
    land: jax.Array
    token: jax.Array


def _split_start(name, src, land_shape, land_dtype, n, copies, after=()):
    n_after = len(after)

    def body(src_ref, land_ref, *rest):
        rest = rest[n_after:]
        sems, token = rest[:2 * n], rest[-1]
        for k, (s, d, peer) in enumerate(copies(src_ref, land_ref)):
            pltpu.make_async_remote_copy(src_ref=s, dst_ref=d, send_sem=sems[k], recv_sem=sems[n + k],
                                         device_id=peer, device_id_type=MESH).start()
        token[...] = jnp.zeros_like(token)

    outs = pl.pallas_call(
        body, name=name,
        out_shape=(*[pltpu.SemaphoreType.DMA(())] * (2 * n), pltpu.HBM(src.shape, src.dtype), pltpu.HBM(land_shape, land_dtype),
                   jax.ShapeDtypeStruct((8, 128), F32)),
        in_specs=(HBM_SPEC, HBM_SPEC, *[ANY] * n_after),
        out_specs=(*[SEM_SPEC] * (2 * n), HBM_SPEC, HBM_SPEC, pl.BlockSpec(memory_space=pltpu.VMEM)),
        input_output_aliases={0: 2 * n, 1: 2 * n + 1},
        compiler_params=pltpu.CompilerParams(has_side_effects=DATAFLOW),
    )(pltpu.with_memory_space_constraint(src, pltpu.HBM), pltpu.with_memory_space_constraint(lax.empty(land_shape, land_dtype), pltpu.HBM),
      *after)
    return _InFlight(tuple(outs[:2 * n]), outs[2 * n], outs[2 * n + 1], outs[2 * n + 2])


def _split_wait(name, flight, after, n, copies):
    def body(src_ref, land_ref, *rest):
        sems = rest[:2 * n]
        for k, (s, d, peer) in enumerate(copies(src_ref, land_ref)):
            cp = pltpu.make_async_remote_copy(src_ref=s, dst_ref=d, send_sem=sems[k], recv_sem=sems[n + k],
                                              device_id=peer, device_id_type=MESH)
            cp.wait_send()
            cp.wait_recv()

    return pl.pallas_call(
        body, name=name,
        out_shape=(pltpu.HBM(flight.src.shape, flight.src.dtype), pltpu.HBM(flight.land.shape, flight.land.dtype)),
        in_specs=(HBM_SPEC, HBM_SPEC, *[SEM_SPEC] * (2 * n), ANY),
        out_specs=(HBM_SPEC, HBM_SPEC), input_output_aliases={0: 0, 1: 1},
        compiler_params=pltpu.CompilerParams(has_side_effects=DATAFLOW),
    )(flight.src, flight.land, *flight.sems, after)


def _gather_copies(src_ref, land_ref):
    x, y, c = _me()
    return [(src_ref, land_ref.at[2 * x + y], (*chip, c)) for chip in _other_chips(x, y)]


def _gather_start(packed, tag, after=()):
    return _split_start(f"gather_start_{tag}", packed, (N_CHIPS, *packed.shape), packed.dtype, 3, _gather_copies, after)


def _gather_wait(flight, after, tag):
    src, others = _split_wait(f"gather_wait_{tag}", flight, after, 3, _gather_copies)
    return lax.dynamic_update_slice(others, src[None], (2 * lax.axis_index("x") + lax.axis_index("y"), 0, 0))


def _swap_copies(src_ref, land_ref):
    x, y, c = _me()
    half = land_ref.shape[1]
    return [(src_ref.at[:, pl.ds((1 - c) * half, half), :], land_ref, (x, y, 1 - c))]


def _swap_start(g, tag):
    S, R, W = g.shape
    return _split_start(f"swap_halves_start_{tag}", g, (S, R // 2, W), g.dtype, 1, _swap_copies)


def _swap_wait(flight, after, tag):
    return _split_wait(f"swap_halves_wait_{tag}", flight, after, 1, _swap_copies)


def _scatter_copies(src_ref, land_ref):
    x, y, c = _me()
    return [(src_ref.at[2 * chip[0] + chip[1]], land_ref.at[j], (*chip, c)) for j, chip in enumerate(_other_chips(x, y))]


def _scatter_start(part, tag):
    S, h, W = part.shape
    return _split_start(f"scatter_chips_start_{tag}", part, (S - 1, h, W), part.dtype, 3, _scatter_copies)


def _scatter_wait(flight, after, tag):
    return _split_wait(f"scatter_chips_wait_{tag}", flight, after, 3, _scatter_copies)[1]


def _join_copies(src_ref, land_ref):
    x, y, c = _me()
    h = src_ref.shape[0]
    return [(src_ref, land_ref.at[pl.ds(c * h, h), :], (x, y, 1 - c))]


def _join_start(mine):
    h, W = mine.shape
    return _split_start("join_halves_start", mine, (2 * h, W), mine.dtype, 1, _join_copies)


def _join_wait(flight, after):
    src, other = _split_wait("join_halves_wait", flight, after, 1, _join_copies)
    return lax.dynamic_update_slice(other, src, (lax.axis_index("c") * src.shape[0], 0))


def _adamw(name, g, w, m, v):
    R, C = w.shape
    tm = R
    for cand in (256, 128, 64, 32, 16, 8):
        if R % cand == 0:
            tm = cand
            break

    def body(g, w, m, v):
        m = ADAM_B1 * m + (1.0 - ADAM_B1) * g
        v = ADAM_B2 * v + (1.0 - ADAM_B2) * jnp.square(g)
        m_hat = m / (1.0 - ADAM_B1 ** ADAM_STEP)
        v_hat = v / (1.0 - ADAM_B2 ** ADAM_STEP)
        delta = -ADAM_LR * (m_hat / (jnp.sqrt(v_hat) + ADAM_EPS) + ADAM_WD * w)
        return delta, m, v

    return _rowwise(name, body, R, tm, [_row(t, tm) for t in (g, w, m, v)], [(C, F32)] * 3)


def _unpack_weights(gathered, names):
    S = gathered.shape[0]
    shard_shapes = {"w_in": (D_MODEL, (QKV_WIDTH + 2 * D_MODEL) // S), "w_branch_na": (NA_WIDTH, D_MODEL // S),
                    "w_branch_dil": (DIL_OUT_WIDTH, D_MODEL // S), "w_out": (D_MODEL // S, D_MODEL),
                    "w_up": (D_MODEL, D_FF // S), "w_down": (D_FF // S, D_MODEL),
                    "w_ple_gate": (D_MODEL // S, D_MODEL), "w_ple_proj": (PLE_DIM, D_MODEL // S)}
    col_sharded = {"w_in", "w_branch_na", "w_branch_dil", "w_up", "w_ple_proj"}
    out, r0 = {}, 0
    for name in names:
        rows, cols = shard_shapes[name]
        n = rows * cols // PACK_W
        t = gathered[:, r0:r0 + n, :].reshape(S, rows, cols)
        r0 += n
        out[name] = t.transpose(1, 0, 2).reshape(rows, S * cols) if name in col_sharded else t.reshape(S * rows, cols)
    return out


def _pack_grads(grads, names):
    col_sharded = {"w_in", "w_branch_na", "w_branch_dil", "w_up", "w_ple_proj"}
    per_chip = []
    for s in range(N_CHIPS):
        rows = []
        for name in names:
            g = grads[name]
            if name in col_sharded:
                w = g.shape[1] // N_CHIPS
                rows.append(_pack_rows(g[:, s * w:(s + 1) * w]))
            else:
                h = g.shape[0] // N_CHIPS
                rows.append(_pack_rows(g[s * h:(s + 1) * h]))
        per_chip.append(jnp.concatenate(rows, axis=0))
    return jnp.stack(per_chip)


def _unpack_shard(packed, shapes, names):
    out, r0 = {}, 0
    for name in names:
        rows, cols = shapes[name]
        n = rows * cols // PACK_W
        out[name] = packed[r0:r0 + n].reshape(rows, cols)
        r0 += n
    return out


def kernel(x, p, positions, g_mix, w_in, rpb, w_branch_na, w_branch_dil, w_out, g_mlp, w_up, w_down, g_ple, w_ple_gate, w_ple_proj, g_final, loss_target, m_g_mix, m_w_in, m_rpb, m_w_branch_na, m_w_branch_dil, m_w_out, m_g_mlp, m_w_up, m_w_down, m_g_ple, m_w_ple_gate, m_w_ple_proj, m_g_final, v_g_mix, v_w_in, v_rpb, v_w_branch_na, v_w_branch_dil, v_w_out, v_g_mlp, v_w_up, v_w_down, v_g_ple, v_w_ple_gate, v_w_ple_proj, v_g_final):
    shards = {"w_in": w_in[0], "w_branch_na": w_branch_na[0], "w_branch_dil": w_branch_dil[0], "w_out": w_out[0],
              "w_up": w_up[0], "w_down": w_down[0], "w_ple_gate": w_ple_gate[0], "w_ple_proj": w_ple_proj[0]}
    m_shards = {"w_in": m_w_in[0], "w_branch_na": m_w_branch_na[0], "w_branch_dil": m_w_branch_dil[0], "w_out": m_w_out[0],
                "w_up": m_w_up[0], "w_down": m_w_down[0], "w_ple_gate": m_w_ple_gate[0], "w_ple_proj": m_w_ple_proj[0]}
    v_shards = {"w_in": v_w_in[0], "w_branch_na": v_w_branch_na[0], "w_branch_dil": v_w_branch_dil[0], "w_out": v_w_out[0],
                "w_up": v_w_up[0], "w_down": v_w_down[0], "w_ple_gate": v_w_ple_gate[0], "w_ple_proj": v_w_ple_proj[0]}

    W = _unpack_weights(_gather_weights(_pack_rows(shards["w_in"].astype(BF))), GATHER_FIRST)
    mix_flight = _gather_start(jnp.concatenate([_pack_rows(shards[n].astype(BF)) for n in GATHER_MIX], axis=0), "mix")
    rest_flight = _gather_start(jnp.concatenate([_pack_rows(shards[n].astype(BF)) for n in GATHER_MLP], axis=0), "mlp",
                                after=(mix_flight.token,))
    w_qkv, w_gates = W["w_in"][:, :QKV_WIDTH], W["w_in"][:, QKV_WIDTH:]

    xs, ps, tgt = x[0], p[0, 0], loss_target[0]
    T = xs.shape[0]
    TM = 512
    gm, gl, gp, gf = g_mix, g_mlp, g_ple, g_final.reshape(1, D_MODEL)
    cos2, sin_signed = _rope_tables(positions[0])
    tab = _na_bias_table(rpb[0])

    a = _rowwise("norm_mix", lambda h, g: h * _rms(h) * g, T, TM, [_row(xs, TM), _full(gm)], [(D_MODEL, BF)],
                 after=(rest_flight.token,))
    n3 = 3 * NA_WIDTH
    qkv = _mm("in_na", a, w_qkv[:, :n3], "nn", 1024, 768, 1024, [BF])
    z_dil = _mm("in_dil", a, w_qkv[:, n3:], "nn", 1024, 1152, 1024, [F32])
    z_gates = _mm("in_gates", a, w_gates, "nn", 1024,1024, 1024, [BF])

    dil_ops = _qkv_prep(z_dil, cos2, sin_signed, TM)
    y_na = _na_fwd(qkv, tab)
    band = [_band_fwd(*dil_ops[g], g) for g in range(len(DIL_GROUPS))]
    y_dil, w_grp, o_nat = _dil_merge_fwd([b[0] for b in band], [b[1] for b in band], T, TM)

    W.update(_unpack_weights(_gather_wait(mix_flight, y_dil, "mix"), GATHER_MIX))
    u_na = _mm("branch_na", y_na, W["w_branch_na"], "nn", 1024,1024, 512, [BF])
    u_dil = _mm("branch_dil", y_dil, W["w_branch_dil"], "nn", 1024,1024, 256, [BF])
    mixed = _rowwise(
        "gate_mix", lambda gn, gd, un, ud: _sigmoid(gn.astype(F32)) * un.astype(F32) + _sigmoid(gd.astype(F32)) * ud.astype(F32), T, TM,
        [_row(z_gates, TM, 0, D_MODEL), _row(z_gates, TM, 1, D_MODEL), _row(u_na, TM), _row(u_dil, TM)], [(D_MODEL, BF)])
    mix_out = _mm("out_proj", mixed, W["w_out"], "nn", 1024,1024, 1024, [F32])

    def add_norm(h, d, g):
        h = h + d
        return h, h * _rms(h) * g

    h1, cn = _rowwise("add_norm_mlp", add_norm, T, TM, [_row(xs, TM), _row(mix_out, TM), _full(gl)], [(D_MODEL, F32), (D_MODEL, BF)])
    W.update(_unpack_weights(_gather_wait(rest_flight, cn, "mlp"), GATHER_MLP))
    up, act = _mm("mlp_up", cn, W["w_up"], "nn", 1024,1024, 1024, [BF, BF],
                  epilogue=lambda acc: (acc, jnp.square(jnp.maximum(acc, 0.0))))
    mlp_out = _mm("mlp_down", act, W["w_down"], "nn", 1024,1024, 1024, [F32])
    h2, en = _rowwise("add_norm_ple", add_norm, T, TM, [_row(h1, TM), _row(mlp_out, TM), _full(gp)], [(D_MODEL, F32), (D_MODEL, BF)])
    gt = _mm("ple_gate", en, W["w_ple_gate"], "nn", 1024,1024, 1024, [F32])
    pp = _mm("ple_proj", ps, W["w_ple_proj"], "nn", 1024,1024, 256, [F32])

    def head(h2t, gtt, ppt, tg, g):
        sg = _sigmoid(gtt)
        h3 = h2t + sg * ppt
        yo = h3 * _rms(h3) * g
        diff = yo - tg
        loss = 0.5 * jnp.sum(jnp.mean(jnp.square(diff), axis=-1, keepdims=True), axis=0, keepdims=True)
        dh3, dg = _rms_bwd(diff * (1.0 / D_MODEL), h3, g)
        return dh3, dh3 * ppt * sg * (1.0 - sg), dh3 * sg, jnp.broadcast_to(loss, (1, 128)), dg

    dh3, d_gt, d_pp, loss_part, dg_final = _rowwise(
        "loss_head", head, T, TM, [_row(h2, TM), _row(gt, TM), _row(pp, TM), _row(tgt, TM), _full(gf)],
        [(D_MODEL, F32), (D_MODEL, BF), (D_MODEL, BF)], sums=[128, D_MODEL])

    grads = {}
    grads["w_ple_proj"] = _mm("g_ple_proj", ps, d_pp, "tn", 256, 1024, 1024,[F32])
    grads["w_ple_gate"] = _mm("g_ple_gate", en, d_gt, "tn", 1024, 1024, 1024,[F32])
    d_en = _mm("d_ple_gate", d_gt, W["w_ple_gate"], "nt", 1024,1024, 1024, [F32])

    def add_norm_bwd(dh_out, dn, h, g):
        dh, dg = _rms_bwd(dn, h, g)
        dh = dh_out + dh
        return dh, dh, dg

    dh2, dh2_b, dg_ple = _rowwise("add_norm_ple_bwd", add_norm_bwd, T, TM, [_row(dh3, TM), _row(d_en, TM), _row(h2, TM), _full(gp)],
                                  [(D_MODEL, F32), (D_MODEL, BF)], sums=[D_MODEL])
    d_up = _mm("d_mlp_down", dh2_b, W["w_down"], "nt", 1024,1024, 1024, [BF],
               epilogue=lambda acc, u: (acc * (2.0 * jnp.maximum(u.astype(F32), 0.0)),), extras=(up,))
    grads["w_down"] = _mm("g_mlp_down", act, dh2_b, "tn", 1024, 1024, 1024,[F32])
    grads["w_up"] = _mm("g_mlp_up", cn, d_up, "tn", 1024, 1024, 1024,[F32])
    early_shapes = {n: shards[n].shape for n in REDUCE_EARLY}
    early_tm = sum(r * c for r, c in early_shapes.values()) // PACK_W // 4
    swap_flight = _swap_start(_pack_grads(grads, REDUCE_EARLY), "early")
    d_cn = _mm("d_mlp_up", d_up, W["w_up"], "nt", 1024,1024, 1024, [F32], after=(swap_flight.token,))
    dh1, dh1_b, dg_mlp = _rowwise("add_norm_mlp_bwd", add_norm_bwd, T, TM, [_row(dh2, TM), _row(d_cn, TM), _row(h1, TM), _full(gl)],
                                  [(D_MODEL, F32), (D_MODEL, BF)], sums=[D_MODEL])
    early_g, early_got = _swap_wait(swap_flight, dh1_b, "early")
    early_pair, early_pair_b = _pair_sum(early_g, early_got, early_tm)
    scatter_flight = _scatter_start(early_pair_b, "early")
    d_mixed = _mm("d_out_proj", dh1_b, W["w_out"], "nt", 1024,1024, 1024, [F32], after=(scatter_flight.token,))
    grads["w_out"] = _mm("g_out_proj", mixed, dh1_b, "tn", 1024, 1024, 1024,[F32])

    def gate_bwd(dm, gn, gd, un, ud):
        gn, gd, un, ud = (t.astype(F32) for t in (gn, gd, un, ud))
        sn, sd = _sigmoid(gn), _sigmoid(gd)
        return jnp.concatenate([dm * un * sn * (1.0 - sn), dm * ud * sd * (1.0 - sd)], axis=1), dm * sn, dm * sd

    dz_gates, d_u_na, d_u_dil = _rowwise(
        "gate_mix_bwd", gate_bwd, T, TM,
        [_row(d_mixed, TM), _row(z_gates, TM, 0, D_MODEL), _row(z_gates, TM, 1, D_MODEL), _row(u_na, TM), _row(u_dil, TM)],
        [(2 * D_MODEL, BF), (D_MODEL, BF), (D_MODEL, BF)])
    grads["w_branch_na"] = _mm("g_branch_na", y_na, d_u_na, "tn", 1024, 1024, 1024,[F32])
    grads["w_branch_dil"] = _mm("g_branch_dil", y_dil, d_u_dil, "tn", 256, 1024, 1024,[F32])
    d_y_na = _mm("d_branch_na", d_u_na, W["w_branch_na"], "nt", 1024,512, 1024, [BF])
    d_y_dil = _mm("d_branch_dil", d_u_dil, W["w_branch_dil"], "nt", 1024,256, 1024, [F32])

    dqa, dka, dva, dtab = _na_bwd(qkv, tab, d_y_na)
    d_rpb = _na_rpb_grad(dtab)[:, :2 * NA_WIN_ROWS - 1, :2 * NA_WIN_COLS - 1]

    do_res, dlse_res = _dil_merge_bwd(d_y_dil, o_nat, w_grp, TM)
    d_dil = [_band_bwd(*dil_ops[g], do_res[g], dlse_res[g], g) for g in range(len(DIL_GROUPS))]

    me_chip = 2 * lax.axis_index("x") + lax.axis_index("y")
    early_mine = _chip_sum(lax.dynamic_index_in_dim(early_pair, me_chip, 0, keepdims=False),
                           _scatter_wait(scatter_flight, d_dil[-1][-1], "early"), early_tm)
    join_flight = _join_start(early_mine)
    dz_qkv = _qkv_unprep((dqa, dka, dva), d_dil, cos2, sin_signed, TM, after=(join_flight.token,))
    grads["w_in"] = jnp.concatenate([
        _mm("g_in_qkv", a, dz_qkv, "tn", 1024, 1280, 1024,[F32]),
        _mm("g_in_gates", a, dz_gates, "tn", 1024, 1024, 1024,[F32])], axis=1)
    late_shapes = {n: shards[n].shape for n in REDUCE_LATE}
    late_tm = sum(r * c for r, c in late_shapes.values()) // PACK_W // 4
    late_swap = _swap_start(_pack_grads(grads, REDUCE_LATE), "late")
    d_a = _mm("d_in_qkv", dz_qkv, w_qkv, "nt", 1024,1024, 1280, [F32], after=(late_swap.token,))
    late_g, late_got = _swap_wait(late_swap, d_a, "late")
    late_pair, late_pair_b = _pair_sum(late_g, late_got, late_tm)
    late_scatter = _scatter_start(late_pair_b, "late")
    d_a = _mm("d_in_gates", dz_gates, w_gates, "nt", 1024,1024, 1024, [F32], epilogue=lambda acc, e: (acc + e,), extras=(d_a,),
              after=(late_scatter.token,))

    def first_bwd(dh_out, dn, h, g):
        dh, dg = _rms_bwd(dn, h, g)
        return dh_out + dh, dg

    grad_x, dg_mix = _rowwise("norm_mix_bwd", first_bwd, T, TM, [_row(dh1, TM), _row(d_a, TM), _row(xs, TM), _full(gm)],
                              [(D_MODEL, F32)], sums=[D_MODEL])
    g_shard = _unpack_shard(_join_wait(join_flight, grad_x), early_shapes, REDUCE_EARLY)

    n_rpb = rpb.size
    rpb_rows = 4
    small = jnp.concatenate([
        dg_mix, dg_mlp, dg_ple, dg_final,
        jnp.pad(d_rpb.reshape(-1), (0, rpb_rows * D_MODEL - n_rpb)).reshape(rpb_rows, D_MODEL),
        jnp.pad(loss_part, ((0, 0), (0, D_MODEL - loss_part.shape[1]))),
        jnp.zeros((SMALL_ROWS - 5 - rpb_rows, D_MODEL), F32)], axis=0)
    small = _allreduce_small(small)
    loss = small[4 + rpb_rows, 0]

    def small_pack(a0, a1, a2, a3, r):
        return jnp.concatenate([a0.reshape(1, -1), a1.reshape(1, -1), a2.reshape(1, -1), a3.reshape(1, -1),
                                jnp.pad(r.reshape(-1), (0, rpb_rows * D_MODEL - n_rpb)).reshape(rpb_rows, D_MODEL)], axis=0)

    g_small = small[:4 + rpb_rows]
    small_res = _adamw("adamw_small", g_small, small_pack(g_mix, g_mlp, g_ple, g_final, rpb),
                       small_pack(m_g_mix, m_g_mlp, m_g_ple, m_g_final, m_rpb), small_pack(v_g_mix, v_g_mlp, v_g_ple, v_g_final, v_rpb))

    def small_unpack(t):
        return {"g_mix": t[0].reshape(g_mix.shape), "g_mlp": t[1].reshape(g_mlp.shape), "g_ple": t[2].reshape(g_ple.shape),
                "g_final": t[3].reshape(g_final.shape), "rpb": t[4:].reshape(-1)[:n_rpb].reshape(rpb.shape)}

    out = {"grad": small_unpack(g_small)}
    for kind, t in zip(("delta", "new_m", "new_v"), small_res, strict=True):
        out[kind] = small_unpack(t)
    def update(names):
        for n in names:
            out["grad"][n] = g_shard[n][None]
            res = _adamw("adamw_" + n, g_shard[n], shards[n], m_shards[n], v_shards[n])
            for kind, t in zip(("delta", "new_m", "new_v"), res, strict=True):
                out[kind][n] = t[None]

    update(REDUCE_EARLY)
    late_others = _scatter_wait(late_scatter, out["new_v"][REDUCE_EARLY[-1]], "late")
    late_mine = _chip_sum(lax.dynamic_index_in_dim(late_pair, me_chip, 0, keepdims=False), late_others, late_tm)
    g_shard.update(_unpack_shard(_join_halves(late_mine), late_shapes, REDUCE_LATE))
    update(REDUCE_LATE)

    order = ["g_mix", "w_in", "rpb", "w_branch_na", "w_branch_dil", "w_out", "g_mlp", "w_up", "w_down", "g_ple",
             "w_ple_gate", "w_ple_proj", "g_final"]
    return (loss, grad_x[None], *[out["grad"][n] for n in order], *[out["delta"][n] for n in order],
            *[out["new_m"][n] for n in order], *[out["new_v"][n] for n in order])
```

```python
import functools
from typing import NamedTuple

import numpy as np
import jax
import jax.numpy as jnp
from jax import lax
from jax.experimental import pallas as pl
from jax.experimental.pallas import tpu as pltpu

BF = jnp.bfloat16
F32 = jnp.float32
MESH = pl.DeviceIdType.MESH
ANY = pl.BlockSpec(memory_space=pl.ANY)

V7X_VMEM_BYTES = 64 * 1024 * 1024
VMEM_LIMIT = V7X_VMEM_BYTES - 16 * 1024 * 1024

D_MODEL = 1024
HEAD_DIM = 64
GRID_W = 64
NA_HEADS = 8
NA_WIN_ROWS = 8
NA_WIN_COLS = 16
NA_WIDTH = NA_HEADS * HEAD_DIM
DIL_GROUPS = ((128, 1), (512, 4), (2048, 16))
DIL_HPG = 4
DIL_HEADS = DIL_HPG * len(DIL_GROUPS)
DIL_WIDTH = DIL_HEADS * HEAD_DIM
DIL_OUT_WIDTH = DIL_HPG * HEAD_DIM
DIL_RADIUS = 64
QKV_WIDTH = 3 * NA_WIDTH + 3 * DIL_WIDTH
D_FF = 4 * D_MODEL
PLE_DIM = 256
ROPE_THETA = 10000.0
RMS_EPS = 1e-6
NEG_INF = -1e30
Q_SCALE = HEAD_DIM ** -0.5

ADAM_LR = 0.001
ADAM_B1 = 0.9
ADAM_B2 = 0.999
ADAM_EPS = 1e-08
ADAM_WD = 0.01
ADAM_STEP = 10

N_CHIPS = 4
N_DEV = 8
PACK_W = 1024
BIG = ("w_in", "w_branch_na", "w_branch_dil", "w_out", "w_up", "w_down", "w_ple_gate", "w_ple_proj")
GATHER_FIRST = ("w_in",)
GATHER_MIX = ("w_branch_na", "w_branch_dil", "w_out")
GATHER_MLP = ("w_up", "w_down", "w_ple_gate", "w_ple_proj")
REDUCE_EARLY = ("w_up", "w_down", "w_ple_gate", "w_ple_proj")
REDUCE_LATE = ("w_in", "w_branch_na", "w_branch_dil", "w_out")
SMALL_ROWS = 16


def _cparams(sem=None):
    return pltpu.CompilerParams(dimension_semantics=sem, vmem_limit_bytes=VMEM_LIMIT)


def _mm(name, a, b, mode, tm, tn, tk, out_dtypes, epilogue=None, extras=(), after=(), into=None):
    if mode == "nn":
        (M, K), N = a.shape, b.shape[1]
    elif mode == "nt":
        (M, K), N = a.shape, b.shape[0]
    else:
        (K, M), N = a.shape, b.shape[1]
    tm, tn, tk = min(tm, M), min(tn, N), min(tk, K)
    assert M % tm == 0 and N % tn == 0 and K % tk == 0, (name, M, N, K, tm, tn, tk)
    if mode == "nn":
        a_spec = pl.BlockSpec((tm, tk), lambda i, j, k: (i, k))
        b_spec = pl.BlockSpec((tk, tn), lambda i, j, k: (k, j))
        dims = (((1,), (0,)), ((), ()))
    elif mode == "nt":
        a_spec = pl.BlockSpec((tm, tk), lambda i, j, k: (i, k))
        b_spec = pl.BlockSpec((tn, tk), lambda i, j, k: (j, k))
        dims = (((1,), (1,)), ((), ()))
    else:
        a_spec = pl.BlockSpec((tk, tm), lambda i, j, k: (k, i))
        b_spec = pl.BlockSpec((tk, tn), lambda i, j, k: (k, j))
        dims = (((0,), (0,)), ((), ()))
    nk = K // tk
    n_extra, n_out = len(extras), len(out_dtypes)
    tile = pl.BlockSpec((tm, tn), lambda i, j, k: (i, j))

    n_after = len(after)

    def body(a_ref, b_ref, *rest):
        extra_refs, rest = rest[:n_extra], rest[n_extra + n_after:]
        out_refs, acc = rest[:n_out], rest[-1]
        k = pl.program_id(2)

        @pl.when(k == 0)
        def _():
            acc[...] = jnp.zeros_like(acc)

        acc[...] += lax.dot_general(a_ref[...].astype(BF), b_ref[...].astype(BF), dims, preferred_element_type=F32)

        @pl.when(k == nk - 1)
        def _():
            outs = (acc[...],) if epilogue is None else epilogue(acc[...], *[e[...] for e in extra_refs])
            for o_ref, val in zip(out_refs, outs, strict=True):
                o_ref[...] = val.astype(o_ref.dtype)

    out_specs = [tile] * n_out
    out_shape = [jax.ShapeDtypeStruct((M, N), dt) for dt in out_dtypes]
    operands, aliases = [a, b, *extras, *after], {}
    in_specs = [a_spec, b_spec] + [tile] * n_extra + [ANY] * n_after
    if into is not None:
        assert n_out == 1
        target, block, index = into
        out_specs = [pl.BlockSpec(block, lambda i, j, k: index(i, j))]
        out_shape = [jax.ShapeDtypeStruct(target.shape, target.dtype)]
        if not isinstance(target, jax.ShapeDtypeStruct):
            aliases = {len(operands): 0}
            operands.append(target)
            in_specs.append(ANY)
            n_after += 1

    outs = pl.pallas_call(
        body, name=name, grid=(M // tm, N // tn, nk),
        in_specs=in_specs, out_specs=out_specs, out_shape=out_shape,
        scratch_shapes=[pltpu.VMEM((tm, tn), F32)], input_output_aliases=aliases,
        compiler_params=_cparams(("parallel", "parallel", "arbitrary")),
    )(*operands)
    return outs[0] if n_out == 1 else outs


def _row(arr, tm, col_block=None, width=None):
    width = arr.shape[1] if width is None else width
    cb = 0 if col_block is None else col_block
    return arr, pl.BlockSpec((tm, width), lambda i: (i, cb))


def _full(arr):
    nd = arr.ndim
    return arr, pl.BlockSpec(arr.shape, lambda i: (0,) * nd)


def _rowwise(name, body, T, tm, ins, outs, sums=(), after=()):
    n_in, n_out, n_sum, n_after = len(ins), len(outs), len(sums), len(after)

    def kern(*refs):
        in_refs, refs = refs[:n_in], refs[n_in + n_after:]
        out_refs, sum_refs = refs[:n_out], refs[n_out:]
        res = body(*[r[...] for r in in_refs])
        res = res if isinstance(res, tuple) else (res,)
        for o_ref, val in zip(out_refs, res[:n_out], strict=True):
            o_ref[...] = val.astype(o_ref.dtype)
        if n_sum:
            @pl.when(pl.program_id(0) == 0)
            def _():
                for s_ref in sum_refs:
                    s_ref[...] = jnp.zeros_like(s_ref)

            for s_ref, val in zip(sum_refs, res[n_out:], strict=True):
                s_ref[...] += val

    res = pl.pallas_call(
        kern, name=name, grid=(T // tm,),
        in_specs=[spec for _, spec in ins] + [ANY] * n_after,
        out_specs=[pl.BlockSpec((tm, c), lambda i: (i, 0)) for c, _ in outs]
        + [pl.BlockSpec((1, c), lambda i: (0, 0)) for c in sums],
        out_shape=[jax.ShapeDtypeStruct((T, c), dt) for c, dt in outs]
        + [jax.ShapeDtypeStruct((1, c), F32) for c in sums],
        compiler_params=_cparams(("arbitrary",)),
    )(*[a for a, _ in ins], *after)
    return res[0] if len(res) == 1 else res


def _sigmoid(x):
    return 1.0 / (1.0 + jnp.exp(-x))


def _rms(h):
    return lax.rsqrt(jnp.mean(h * h, axis=-1, keepdims=True) + RMS_EPS)


def _rms_bwd(dy, h, g):
    r = _rms(h)
    n = h * r
    dn = dy * g
    dh = r * (dn - n * jnp.mean(dn * n, axis=-1, keepdims=True))
    return dh, jnp.sum(dy * n, axis=0, keepdims=True)


def _rope(x, cos2, sin_signed):
    lane = lax.broadcasted_iota(jnp.int32, x.shape, 1)
    swapped = jnp.where((lane % HEAD_DIM) < HEAD_DIM // 2, pltpu.roll(x, 128 - HEAD_DIM // 2, 1), pltpu.roll(x, HEAD_DIM // 2, 1))
    return x * cos2 + swapped * sin_signed


def _rope_cols(x, cos2, sin_signed):
    return jnp.concatenate([_rope(x[:, c:c + 128], cos2, sin_signed) for c in range(0, x.shape[1], 128)], axis=1)


NA_KEYS = NA_WIN_ROWS * GRID_W
NA_BASES = 8


def _na_row_geometry(r, rows):
    first = jnp.clip(r - NA_WIN_ROWS // 2, 0, rows - NA_WIN_ROWS)
    base = first - r + (NA_WIN_ROWS - 1)
    return pl.multiple_of(first * GRID_W, GRID_W), base


NA_ROWS_PER_STEP = 8
NA_BWD_ROWS_PER_STEP = 8


def _softmax_rows(s):
    p = jnp.exp(s - jnp.max(s, axis=-1, keepdims=True))
    return p / jnp.sum(p, axis=-1, keepdims=True)


def _na_probs(q, kw, bias):
    return _softmax_rows(lax.dot_general(q, kw, (((1,), (1,)), ((), ())), preferred_element_type=F32) + bias)


def _split_pair(t):
    first = lax.broadcasted_iota(jnp.int32, t.shape, 1) < HEAD_DIM
    zero = jnp.zeros_like(t)
    return jnp.where(first, t, zero), jnp.where(first, zero, t)


def _join_pair(a, b):
    return jnp.where(lax.broadcasted_iota(jnp.int32, a.shape, 1) < HEAD_DIM, a, b)


_NT = (((1,), (1,)), ((), ()))
_TN = (((0,), (0,)), ((), ()))


def _na_fwd(qkv, tab):
    T = qkv.shape[0]
    rows = T // GRID_W
    n_pairs = NA_WIDTH // 128

    def body(q_ref, k_ref, v_ref, tab_ref, y_ref):
        def step(it, carry):
            geo = [_na_row_geometry(it * NA_ROWS_PER_STEP + u, rows) for u in range(NA_ROWS_PER_STEP)]
            q0s = [pl.multiple_of((it * NA_ROWS_PER_STEP + u) * GRID_W, GRID_W) for u in range(NA_ROWS_PER_STEP)]
            ss = [lax.dot_general(jnp.concatenate(_split_pair(q_ref[pl.ds(q0, GRID_W), :] * Q_SCALE), axis=0),
                                  k_ref[pl.ds(k0, NA_KEYS), :], _NT, preferred_element_type=F32)
                  for q0, (k0, _) in zip(q0s, geo)]
            ps = [_softmax_rows(s + jnp.concatenate([tab_ref[0, base], tab_ref[1, base]], axis=0)) for s, (_, base) in zip(ss, geo)]
            ys = [jnp.dot(p.astype(BF), v_ref[pl.ds(k0, NA_KEYS), :], preferred_element_type=F32) for p, (k0, _) in zip(ps, geo)]
            for q0, y2 in zip(q0s, ys):
                y_ref[pl.ds(q0, GRID_W), :] = _join_pair(y2[:GRID_W], y2[GRID_W:]).astype(y_ref.dtype)
            return carry

        lax.fori_loop(0, rows // NA_ROWS_PER_STEP, step, 0)

    def cols(first):
        return pl.BlockSpec((T, 128), lambda j: (0, first + j))

    return pl.pallas_call(
        body, name="na_fwd", grid=(n_pairs,),
        in_specs=[cols(0), cols(n_pairs), cols(2 * n_pairs), pl.BlockSpec((2, NA_BASES, GRID_W, NA_KEYS), lambda j: (j, 0, 0, 0))],
        out_specs=cols(0), out_shape=jax.ShapeDtypeStruct((T, NA_WIDTH), BF),
        compiler_params=_cparams(("parallel",)),
    )(qkv, qkv, qkv, tab)


def _na_bwd(qkv, tab, do):
    T = qkv.shape[0]
    rows = T // GRID_W
    n_pairs = NA_WIDTH // 128

    def body(q_ref, k_ref, v_ref, tab_ref, do_ref, dq_ref, dk_ref, dv_ref, dtab_ref):
        dk_ref[...] = jnp.zeros_like(dk_ref)
        dv_ref[...] = jnp.zeros_like(dv_ref)
        dtab_ref[...] = jnp.zeros_like(dtab_ref)

        def step(it, carry):
            U = NA_BWD_ROWS_PER_STEP
            geo = [_na_row_geometry(it * U + u, rows) for u in range(U)]
            q0s = [pl.multiple_of((it * U + u) * GRID_W, GRID_W) for u in range(U)]
            q2s = [jnp.concatenate(_split_pair(q_ref[pl.ds(q0, GRID_W), :] * Q_SCALE), axis=0) for q0 in q0s]
            do2s = [jnp.concatenate(_split_pair(do_ref[pl.ds(q0, GRID_W), :]), axis=0) for q0 in q0s]
            ss = [lax.dot_general(q2, k_ref[pl.ds(k0, NA_KEYS), :], _NT, preferred_element_type=F32) for q2, (k0, _) in zip(q2s, geo)]
            dps = [lax.dot_general(do2, v_ref[pl.ds(k0, NA_KEYS), :], _NT, preferred_element_type=F32) for do2, (k0, _) in zip(do2s, geo)]
            ps = [_softmax_rows(s + jnp.concatenate([tab_ref[0, base], tab_ref[1, base]], axis=0)) for s, (_, base) in zip(ss, geo)]
            dss = [p * (dp - jnp.sum(dp * p, axis=-1, keepdims=True)) for p, dp in zip(ps, dps)]
            dvs = [lax.dot_general(p.astype(BF), do2, _TN, preferred_element_type=F32) for p, do2 in zip(ps, do2s)]
            dsbs = [ds.astype(BF) for ds in dss]
            dqs = [jnp.dot(dsb, k_ref[pl.ds(k0, NA_KEYS), :], preferred_element_type=F32) for dsb, (k0, _) in zip(dsbs, geo)]
            dks = [lax.dot_general(dsb, q2, _TN, preferred_element_type=F32) for dsb, q2 in zip(dsbs, q2s)]
            for u in range(U):
                k0, base = geo[u]
                dtab_ref[0, base] += dss[u][:GRID_W]
                dtab_ref[1, base] += dss[u][GRID_W:]
                dq_ref[pl.ds(q0s[u], GRID_W), :] = _join_pair(dqs[u][:GRID_W], dqs[u][GRID_W:])
                dk_ref[pl.ds(k0, NA_KEYS), :] += dks[u]
                dv_ref[pl.ds(k0, NA_KEYS), :] += dvs[u]
            return carry

        lax.fori_loop(0, rows // NA_BWD_ROWS_PER_STEP, step, 0)

    def cols(first):
        return pl.BlockSpec((T, 128), lambda j: (0, first + j))

    tabs = pl.BlockSpec((2, NA_BASES, GRID_W, NA_KEYS), lambda j: (j, 0, 0, 0))
    wide = jax.ShapeDtypeStruct((T, NA_WIDTH), F32)
    return pl.pallas_call(
        body, name="na_bwd", grid=(n_pairs,),
        in_specs=[cols(0), cols(n_pairs), cols(2 * n_pairs), tabs, cols(0)],
        out_specs=[cols(0), cols(0), cols(0), tabs],
        out_shape=[wide, wide, wide, jax.ShapeDtypeStruct((NA_HEADS, NA_BASES, GRID_W, NA_KEYS), F32)],
        compiler_params=_cparams(("parallel",)),
    )(qkv, qkv, qkv, tab, do)


def _na_geometry_np():
    c = np.arange(GRID_W)
    cs = np.clip(c - NA_WIN_COLS // 2, 0, GRID_W - NA_WIN_COLS)
    valid = (c[None, :] >= cs[:, None]) & (c[None, :] < cs[:, None] + NA_WIN_COLS)
    off = c[None, :] - c[:, None] + (NA_WIN_COLS - 1)
    oh_col = np.zeros((GRID_W, GRID_W, 2 * NA_WIN_COLS - 1), np.float32)
    qq, kk = np.nonzero(valid)
    oh_col[qq, kk, off[qq, kk]] = 1.0
    oh_row = np.zeros((NA_BASES, NA_WIN_ROWS, 2 * NA_WIN_ROWS - 1), np.float32)
    for base in range(NA_BASES):
        for i in range(NA_WIN_ROWS):
            oh_row[base, i, base + i] = 1.0
    return valid, oh_col, oh_row


def _na_bias_table(rpb):
    valid, oh_col, oh_row = _na_geometry_np()
    hi = lax.Precision.HIGHEST
    t1 = jnp.einsum("hrc,pir->hpic", rpb, oh_row, precision=hi)
    tab = jnp.einsum("hpic,qkc->hpqik", t1, oh_col, precision=hi)
    tab = jnp.where(valid[None, None, :, None, :], tab, NEG_INF)
    return tab.reshape(rpb.shape[0], NA_BASES, GRID_W, NA_KEYS)


def _na_rpb_grad(dtab):
    H = dtab.shape[0]
    n_rows = 2 * NA_WIN_ROWS - 1
    n_cols = 2 * NA_WIN_COLS - 1

    def body(d_ref, o_ref):
        lane = lax.broadcasted_iota(jnp.int32, (GRID_W, 128), 1)
        low = lane < GRID_W
        out_rows = []
        for ro in range(n_rows):
            acc = jnp.zeros((GRID_W, 128), F32)
            for base in range(NA_BASES):
                i = ro - base
                if not 0 <= i < NA_WIN_ROWS:
                    continue
                pair = d_ref[base, :, pl.ds((i // 2) * 128, 128)]
                if i % 2:
                    pair = pltpu.roll(pair, GRID_W, 1)
                acc = acc + jnp.where(low, pair, 0.0)
            skew = pltpu.roll(acc, 0, 1, stride=1, stride_axis=0)
            diag = jnp.sum(skew, axis=0, keepdims=True)
            out_rows.append(pltpu.roll(jnp.broadcast_to(diag, (8, 128)), 128 - (GRID_W - NA_WIN_COLS), 1)[:1])
        out_rows.append(jnp.zeros((1, 128), F32))
        res = jnp.concatenate(out_rows, axis=0)
        o_ref[...] = jnp.where(lax.broadcasted_iota(jnp.int32, res.shape, 1) < n_cols, res, 0.0)

    return pl.pallas_call(
        body, name="na_rpb_grad", grid=(H,),
        in_specs=[pl.BlockSpec((None, NA_BASES, GRID_W, NA_KEYS), lambda h: (h, 0, 0, 0))],
        out_specs=pl.BlockSpec((None, n_rows + 1, 128), lambda h: (h, 0, 0)),
        out_shape=jax.ShapeDtypeStruct((H, n_rows + 1, 128), F32),
        compiler_params=_cparams(("parallel",)),
    )(jnp.flip(dtab, axis=2))


BAND_Q = 128
BAND_KEYS = BAND_Q + 2 * DIL_RADIUS


def _band_geometry(n, L):
    q0 = pl.multiple_of(n * BAND_Q, BAND_Q)
    k0 = pl.multiple_of(jnp.clip(q0 - DIL_RADIUS, 0, L - BAND_KEYS), DIL_RADIUS)
    qi = q0 + lax.broadcasted_iota(jnp.int32, (BAND_Q, BAND_KEYS), 0)
    kj = k0 + lax.broadcasted_iota(jnp.int32, (BAND_Q, BAND_KEYS), 1)
    return q0, k0, jnp.abs(qi - kj) <= DIL_RADIUS


DIL_PAIRS = DIL_OUT_WIDTH // 128


def _residue_shape(dil, T, dtype):
    return jax.ShapeDtypeStruct((DIL_PAIRS, dil, T // dil, 128), dtype)


def _residue_tile(dil, tm):
    return pl.BlockSpec((DIL_PAIRS, dil, tm // dil, 128), lambda i: (0, 0, i, 0))


def _to_natural(ref, scratch, dil, tm):
    tiles = []
    for pair in range(DIL_PAIRS):
        if dil == 1:
            tiles.append(ref[pair, 0].astype(F32))
            continue
        for r in range(dil):
            scratch[pl.ds(r, tm // dil, stride=dil), :] = ref[pair, r].astype(F32)
        tiles.append(scratch[...])
    return tiles


def _from_natural(tile, scratch, ref, pair, dil, tm):
    if dil == 1:
        ref[pair, 0] = tile.astype(ref.dtype)
        return
    scratch[...] = tile
    for r in range(dil):
        ref[pair, r] = scratch[pl.ds(r, tm // dil, stride=dil), :].astype(ref.dtype)


def _band_specs(group, T):
    dil = DIL_GROUPS[group][1]
    L = T // dil
    assert L % BAND_Q == 0 and L >= BAND_KEYS, (T, dil)
    return L, (dil * DIL_PAIRS,), pl.BlockSpec((None, None, L, 128), lambda s: (s % DIL_PAIRS, s // DIL_PAIRS, 0, 0))


BAND_BLOCKS_PER_STEP = 4


def _band_softmax(s, valid):
    s = jnp.where(valid, s, NEG_INF)
    m = jnp.max(s, axis=-1, keepdims=True)
    p = jnp.exp(s - m)
    l = jnp.sum(p, axis=-1, keepdims=True)
    return p / l, m + jnp.log(l)


def _band_fwd(q, k, v, group):
    T = q.shape[1] * q.shape[2]
    L, grid, spec = _band_specs(group, T)
    U = min(BAND_BLOCKS_PER_STEP, L // BAND_Q)

    def body(q_ref, k_ref, v_ref, o_ref, lse_ref):
        def step(it, carry):
            geo = [_band_geometry(it * U + u, L) for u in range(U)]
            ss = [lax.dot_general(jnp.concatenate(_split_pair(q_ref[pl.ds(q0, BAND_Q), :]), axis=0),
                                  k_ref[pl.ds(k0, BAND_KEYS), :], _NT, preferred_element_type=F32) for q0, k0, _ in geo]
            pls = [_band_softmax(s, jnp.concatenate([valid, valid], axis=0)) for s, (_, _, valid) in zip(ss, geo)]
            os = [jnp.dot(p.astype(BF), v_ref[pl.ds(k0, BAND_KEYS), :], preferred_element_type=F32) for (p, _), (_, k0, _) in zip(pls, geo)]
            for (q0, _, _), o2, (_, lse) in zip(geo, os, pls):
                o_ref[pl.ds(q0, BAND_Q), :] = _join_pair(o2[:BAND_Q], o2[BAND_Q:])
                lse2 = jnp.broadcast_to(lse, (2 * BAND_Q, 128))
                lse_ref[pl.ds(q0, BAND_Q), :] = _join_pair(lse2[:BAND_Q], lse2[BAND_Q:])
            return carry

        lax.fori_loop(0, L // (BAND_Q * U), step, 0)

    res = _residue_shape(DIL_GROUPS[group][1], T, F32)
    return pl.pallas_call(
        body, name=f"band_fwd_g{group}", grid=grid,
        in_specs=[spec] * 3, out_specs=[spec] * 2, out_shape=[res, res],
        compiler_params=_cparams(("parallel",)),
    )(q, k, v)


def _band_bwd(q, k, v, do, dlse, group):
    T = q.shape[1] * q.shape[2]
    L, grid, spec = _band_specs(group, T)
    U = min(BAND_BLOCKS_PER_STEP, L // BAND_Q)

    def body(q_ref, k_ref, v_ref, do_ref, dlse_ref, dq_ref, dk_ref, dv_ref):
        dk_ref[...] = jnp.zeros_like(dk_ref)
        dv_ref[...] = jnp.zeros_like(dv_ref)

        def step(it, carry):
            geo = [_band_geometry(it * U + u, L) for u in range(U)]
            q2s = [jnp.concatenate(_split_pair(q_ref[pl.ds(q0, BAND_Q), :]), axis=0) for q0, _, _ in geo]
            do2s = [jnp.concatenate(_split_pair(do_ref[pl.ds(q0, BAND_Q), :]), axis=0) for q0, _, _ in geo]
            ss = [lax.dot_general(q2, k_ref[pl.ds(k0, BAND_KEYS), :], _NT, preferred_element_type=F32) for q2, (_, k0, _) in zip(q2s, geo)]
            dps = [lax.dot_general(do2, v_ref[pl.ds(k0, BAND_KEYS), :], _NT, preferred_element_type=F32) for do2, (_, k0, _) in zip(do2s, geo)]
            ps = [_band_softmax(s, jnp.concatenate([valid, valid], axis=0))[0] for s, (_, _, valid) in zip(ss, geo)]
            dss = []
            for p, dp, (q0, _, _) in zip(ps, dps, geo):
                dl = dlse_ref[pl.ds(q0, BAND_Q), :]
                dl2 = jnp.concatenate([dl[:, :1], dl[:, HEAD_DIM:HEAD_DIM + 1]], axis=0)
                dss.append(p * (dp - jnp.sum(dp * p, axis=-1, keepdims=True) + dl2))
            dvs = [lax.dot_general(p.astype(BF), do2, _TN, preferred_element_type=F32) for p, do2 in zip(ps, do2s)]
            dsbs = [ds.astype(BF) for ds in dss]
            dqs = [jnp.dot(dsb, k_ref[pl.ds(k0, BAND_KEYS), :], preferred_element_type=F32) for dsb, (_, k0, _) in zip(dsbs, geo)]
            dks = [lax.dot_general(dsb, q2, _TN, preferred_element_type=F32) for dsb, q2 in zip(dsbs, q2s)]
            for u, (q0, k0, _) in enumerate(geo):
                dq_ref[pl.ds(q0, BAND_Q), :] = _join_pair(dqs[u][:BAND_Q], dqs[u][BAND_Q:])
                dk_ref[pl.ds(k0, BAND_KEYS), :] += dks[u]
                dv_ref[pl.ds(k0, BAND_KEYS), :] += dvs[u]
            return carry

        lax.fori_loop(0, L // (BAND_Q * U), step, 0)

    res = _residue_shape(DIL_GROUPS[group][1], T, F32)
    return pl.pallas_call(
        body, name=f"band_bwd_g{group}", grid=grid,
        in_specs=[spec] * 5, out_specs=[spec] * 3, out_shape=[res] * 3,
        compiler_params=_cparams(("parallel",)),
    )(q, k, v, do, dlse)


def _head_sums(t):
    head = lax.broadcasted_iota(jnp.int32, t.shape, 1) // HEAD_DIM
    out = jnp.zeros_like(t)
    for h in range(t.shape[1] // HEAD_DIM):
        mine = head == h
        out = jnp.where(mine, jnp.sum(jnp.where(mine, t, 0.0), axis=-1, keepdims=True), out)
    return out


def _dil_merge_fwd(os, lses, T, tm):
    G = len(DIL_GROUPS)
    W = DIL_OUT_WIDTH
    dils = [d for _, d in DIL_GROUPS]

    def body(*refs):
        o_refs, lse_refs = refs[:G], refs[G:2 * G]
        y_ref, w_refs, on_refs, scratch = refs[2 * G], refs[2 * G + 1:3 * G + 1], refs[3 * G + 1:4 * G + 1], refs[-1]
        o = [jnp.concatenate(_to_natural(r, scratch, d, tm), axis=1) for r, d in zip(o_refs, dils)]
        ls = [jnp.concatenate(_to_natural(r, scratch, d, tm), axis=1) for r, d in zip(lse_refs, dils)]
        m = functools.reduce(jnp.maximum, ls)
        es = [jnp.exp(l - m) for l in ls]
        tot = functools.reduce(jnp.add, es)
        ws = [e / tot for e in es]
        y_ref[...] = functools.reduce(jnp.add, [w * t for w, t in zip(ws, o)]).astype(y_ref.dtype)
        for g in range(G):
            w_refs[g][...] = ws[g]
            on_refs[g][...] = o[g]

    nat = pl.BlockSpec((tm, W), lambda i: (i, 0))
    res = pl.pallas_call(
        body, name="dil_merge_fwd", grid=(T // tm,),
        in_specs=[_residue_tile(d, tm) for d in dils] * 2,
        out_specs=[nat] * (2 * G + 1),
        out_shape=[jax.ShapeDtypeStruct((T, W), BF)] + [jax.ShapeDtypeStruct((T, W), F32)] * (2 * G),
        scratch_shapes=[pltpu.VMEM((tm, 128), F32)],
        compiler_params=_cparams(("parallel",)),
    )(*os, *lses)
    return res[0], res[1:G + 1], res[G + 1:]


def _dil_merge_bwd(dy, os, ws, tm):
    G = len(DIL_GROUPS)
    T, W = dy.shape
    dils = [d for _, d in DIL_GROUPS]

    def body(*refs):
        dyt = refs[0][...]
        o, w = [r[...] for r in refs[1:G + 1]], [r[...] for r in refs[G + 1:2 * G + 1]]
        do_refs, dlse_refs, scratch = refs[2 * G + 1:3 * G + 1], refs[3 * G + 1:4 * G + 1], refs[-1]
        dws = [_head_sums(dyt * t) for t in o]
        mean = functools.reduce(jnp.add, [a * b for a, b in zip(w, dws)])
        for g, d in enumerate(dils):
            do, dlse = w[g] * dyt, w[g] * (dws[g] - mean)
            for pair in range(DIL_PAIRS):
                cols = slice(pair * 128, (pair + 1) * 128)
                _from_natural(do[:, cols], scratch, do_refs[g], pair, d, tm)
                _from_natural(dlse[:, cols], scratch, dlse_refs[g], pair, d, tm)

    nat = pl.BlockSpec((tm, W), lambda i: (i, 0))
    res = pl.pallas_call(
        body, name="dil_merge_bwd", grid=(T // tm,),
        in_specs=[nat] * (2 * G + 1),
        out_specs=[_residue_tile(d, tm) for d in dils] * 2,
        out_shape=[_residue_shape(d, T, BF) for d in dils] + [_residue_shape(d, T, F32) for d in dils],
        scratch_shapes=[pltpu.VMEM((tm, 128), F32)],
        compiler_params=_cparams(("parallel",)),
    )(dy, *os, *ws)
    return res[:G], res[G:]


def _qkv_prep(z, cos2, sin_signed, tm):
    T = z.shape[0]
    G = len(DIL_GROUPS)
    dils = [d for _, d in DIL_GROUPS]
    n_dil_blocks = 3 * DIL_WIDTH // 128

    def body(*refs):
        blocks = refs[:n_dil_blocks]
        cos_ref, sin_ref = refs[n_dil_blocks], refs[1 + n_dil_blocks]
        outs = refs[2 + n_dil_blocks:]
        for part in range(3):
            for g, d in enumerate(dils):
                out = outs[g * 3 + part]
                for pair in range(DIL_PAIRS):
                    blk = blocks[part * (DIL_WIDTH // 128) + g * DIL_PAIRS + pair]
                    for r in range(d):
                        rows = pl.ds(r, tm // d, stride=d) if d > 1 else slice(None)
                        x = blk[rows, :]
                        if part < 2:
                            x = _rope(x, cos_ref[rows, :], sin_ref[rows, :])
                        if part == 0:
                            x = x * Q_SCALE
                        out[pair, r] = x.astype(out.dtype)

    lane_block = [pl.BlockSpec((tm, 128), functools.partial(lambda b, i: (i, b), b)) for b in range(n_dil_blocks)]
    tab = pl.BlockSpec((tm, 128), lambda i: (i, 0))
    res = pl.pallas_call(
        body, name="qkv_prep", grid=(T // tm,),
        in_specs=lane_block + [tab, tab],
        out_specs=[_residue_tile(d, tm) for d in dils for _ in range(3)],
        out_shape=[_residue_shape(d, T, BF) for d in dils for _ in range(3)],
        compiler_params=_cparams(("parallel",)),
    )(*[z] * n_dil_blocks, cos2, sin_signed)
    return [res[3 * g:3 + 3 * g] for g in range(G)]


def _qkv_unprep(d_na, d_dil, cos2, sin_signed, tm, after=()):
    T = d_na[0].shape[0]
    G = len(DIL_GROUPS)
    dils = [d for _, d in DIL_GROUPS]
    n_after = len(after)

    def body(*refs):
        dq, dk, dv = (r[...] for r in refs[:3])
        res_refs = refs[3:3 + 3 * G]
        cs, sn = refs[3 + 3 * G][...], refs[4 + 3 * G][...]
        out, scratch = refs[5 + 3 * G + n_after], refs[-1]
        cols = [dq * Q_SCALE, dk, dv]
        for part in range(3):
            for g, d in enumerate(dils):
                for x in _to_natural(res_refs[g * 3 + part], scratch, d, tm):
                    if part < 2:
                        x = _rope(x, cs, -sn)
                    cols.append(x * Q_SCALE if part == 0 else x)
        out[...] = jnp.concatenate(cols, axis=1).astype(out.dtype)

    wide = pl.BlockSpec((tm, NA_WIDTH), lambda i: (i, 0))
    tab = pl.BlockSpec((tm, 128), lambda i: (i, 0))
    return pl.pallas_call(
        body, name="qkv_unprep", grid=(T // tm,),
        in_specs=[wide] * 3 + [_residue_tile(d, tm) for d in dils for _ in range(3)] + [tab, tab] + [ANY] * n_after,
        out_specs=pl.BlockSpec((tm, QKV_WIDTH), lambda i: (i, 0)),
        out_shape=jax.ShapeDtypeStruct((T, QKV_WIDTH), BF),
        scratch_shapes=[pltpu.VMEM((tm, 128), F32)],
        compiler_params=_cparams(("parallel",)),
    )(*d_na, *[t for g in range(G) for t in d_dil[g]], cos2, sin_signed, *after)


def _rope_tables(positions):
    half = HEAD_DIM // 2
    inv_freq = ROPE_THETA ** (-jnp.arange(half, dtype=F32) / half)
    ang = positions.astype(F32)[:, None] * inv_freq
    cos, sin = jnp.cos(ang), jnp.sin(ang)
    return jnp.tile(jnp.concatenate([cos, cos], axis=1), (1, 2)), jnp.tile(jnp.concatenate([-sin, sin], axis=1), (1, 2))


def _pack_rows(t):
    return t.reshape(-1, PACK_W)


def _me():
    return lax.axis_index("x"), lax.axis_index("y"), lax.axis_index("c")


def _other_chips(x, y):
    return [(1 - x, y), (x, 1 - y), (1 - x, 1 - y)]


def _gather_weights(packed):
    R, W = packed.shape
    half = R // 2

    def body(in_ref, out_ref, send_sems, recv_sems):
        x, y, c = _me()
        sibling = (x, y, 1 - c)
        chips = _other_chips(x, y)

        def block(chip, core):
            return out_ref.at[2 * chip[0] + chip[1], pl.ds(core * half, half), :]

        def copy(k, chip, core, to, src=None):
            return pltpu.make_async_remote_copy(
                src_ref=block(chip, core) if src is None else src, dst_ref=block(chip, core),
                send_sem=send_sems.at[k], recv_sem=recv_sems.at[k], device_id=to, device_id_type=MESH)

        first = [copy(j, (x, y), c, (*chip, c), src=in_ref.at[pl.ds(c * half, half), :]) for j, chip in enumerate(chips)]
        for cp in first:
            cp.start()
        passed = [copy(3 + j, chip, c, sibling) for j, chip in enumerate(chips)]
        for j, chip in enumerate(chips):
            copy(j, chip, c, (x, y, c)).wait_recv()
            passed[j].start()
        for j, chip in enumerate(chips):
            copy(3 + j, chip, 1 - c, (x, y, c)).wait_recv()
        for cp in first + passed:
            cp.wait_send()

    others = pl.pallas_call(
        body, name="gather_weights",
        in_specs=[ANY], out_specs=ANY,
        out_shape=jax.ShapeDtypeStruct((N_CHIPS, R, W), packed.dtype),
        scratch_shapes=[pltpu.SemaphoreType.DMA((6,)), pltpu.SemaphoreType.DMA((6,))],
    )(packed)
    return lax.dynamic_update_slice(others, packed[None], (2 * lax.axis_index("x") + lax.axis_index("y"), 0, 0))


def _swap_halves(g):
    S, R, W = g.shape
    half = R // 2

    def body(g_ref, out_ref, send_sem, recv_sem):
        x, y, c = _me()
        cp = pltpu.make_async_remote_copy(
            src_ref=g_ref.at[:, pl.ds((1 - c) * half, half), :], dst_ref=out_ref,
            send_sem=send_sem, recv_sem=recv_sem, device_id=(x, y, 1 - c), device_id_type=MESH)
        cp.start()
        cp.wait()

    return pl.pallas_call(
        body, name="swap_halves", in_specs=[ANY], out_specs=ANY,
        out_shape=jax.ShapeDtypeStruct((S, half, W), g.dtype),
        scratch_shapes=[pltpu.SemaphoreType.DMA, pltpu.SemaphoreType.DMA],
    )(g)


def _pair_sum(g, got, tm):
    S, R, W = g.shape
    half = R // 2
    nb = half // tm

    def body(c_ref, g_ref, got_ref, o_ref, ob_ref):
        tot = g_ref[...] + got_ref[...]
        o_ref[...] = tot
        ob_ref[...] = tot.astype(ob_ref.dtype)

    tile = pl.BlockSpec((None, tm, W), lambda s, i, c_ref: (s, i, 0))
    return pl.pallas_call(
        body, name="pair_sum",
        grid_spec=pltpu.PrefetchScalarGridSpec(
            num_scalar_prefetch=1, grid=(S, nb),
            in_specs=[pl.BlockSpec((None, tm, W), lambda s, i, c_ref: (s, c_ref[0] * nb + i, 0)), tile],
            out_specs=[tile, tile]),
        out_shape=[jax.ShapeDtypeStruct((S, half, W), F32), jax.ShapeDtypeStruct((S, half, W), BF)],
        compiler_params=_cparams(("parallel", "parallel")),
    )(lax.axis_index("c").reshape(1).astype(jnp.int32), g, got)


def _scatter_chips(part):
    S, h, W = part.shape

    def body(p_ref, out_ref, send_sems, recv_sems):
        x, y, c = _me()
        chips = _other_chips(x, y)
        sends = [pltpu.make_async_remote_copy(
            src_ref=p_ref.at[2 * chip[0] + chip[1]], dst_ref=out_ref.at[j],
            send_sem=send_sems.at[j], recv_sem=recv_sems.at[j], device_id=(*chip, c), device_id_type=MESH)
            for j, chip in enumerate(chips)]
        for cp in sends:
            cp.start()
        for cp in sends:
            cp.wait()

    return pl.pallas_call(
        body, name="scatter_chips", in_specs=[ANY], out_specs=ANY,
        out_shape=jax.ShapeDtypeStruct((S - 1, h, W), part.dtype),
        scratch_shapes=[pltpu.SemaphoreType.DMA((3,)), pltpu.SemaphoreType.DMA((3,))],
    )(part)


def _chip_sum(own, others, tm):
    n, h, W = others.shape

    def body(own_ref, p_ref, o_ref):
        o_ref[...] = ((own_ref[...] + p_ref[0].astype(F32)) + p_ref[1].astype(F32)) + p_ref[2].astype(F32)

    return pl.pallas_call(
        body, name="chip_sum", grid=(h // tm,),
        in_specs=[pl.BlockSpec((tm, W), lambda i: (i, 0)), pl.BlockSpec((n, tm, W), lambda i: (0, i, 0))],
        out_specs=pl.BlockSpec((tm, W), lambda i: (i, 0)),
        out_shape=jax.ShapeDtypeStruct((h, W), F32),
        compiler_params=_cparams(("parallel",)),
    )(own, others)


def _join_halves(mine):
    h, W = mine.shape

    def body(m_ref, out_ref, send_sem, recv_sem):
        x, y, c = _me()
        cp = pltpu.make_async_remote_copy(
            src_ref=m_ref, dst_ref=out_ref.at[pl.ds(c * h, h), :],
            send_sem=send_sem, recv_sem=recv_sem, device_id=(x, y, 1 - c), device_id_type=MESH)
        cp.start()
        pltpu.make_async_remote_copy(
            src_ref=m_ref, dst_ref=out_ref.at[pl.ds((1 - c) * h, h), :],
            send_sem=send_sem, recv_sem=recv_sem, device_id=(x, y, 1 - c), device_id_type=MESH).wait_recv()
        cp.wait_send()

    other = pl.pallas_call(
        body, name="join_halves", in_specs=[ANY], out_specs=ANY,
        out_shape=jax.ShapeDtypeStruct((2 * h, W), mine.dtype),
        scratch_shapes=[pltpu.SemaphoreType.DMA, pltpu.SemaphoreType.DMA],
    )(mine)
    return lax.dynamic_update_slice(other, mine, (lax.axis_index("c") * h, 0))


def _allreduce_small(s):
    R, W = s.shape

    def body(s_ref, o_ref, buf, send_sems, recv_sems):
        x, y, c = _me()
        me = 4 * x + 2 * y + c
        buf[me] = s_ref[...]
        peers = [((x + fx) % 2, (y + fy) % 2, (c + fc) % 2) for fx in range(2) for fy in range(2) for fc in range(2)][1:]
        sends = [pltpu.make_async_remote_copy(
            src_ref=s_ref, dst_ref=buf.at[me], send_sem=send_sems.at[k], recv_sem=recv_sems.at[k],
            device_id=peer, device_id_type=MESH) for k, peer in enumerate(peers)]
        for cp in sends:
            cp.start()
        for k, peer in enumerate(peers):
            pltpu.make_async_remote_copy(
                src_ref=s_ref, dst_ref=buf.at[4 * peer[0] + 2 * peer[1] + peer[2]], send_sem=send_sems.at[k],
                recv_sem=recv_sems.at[k], device_id=peer, device_id_type=MESH).wait_recv()
        for cp in sends:
            cp.wait_send()
        total = buf[0]
        for d in range(1, N_DEV):
            total = total + buf[d]
        o_ref[...] = total

    return pl.pallas_call(
        body, name="allreduce_small",
        in_specs=[pl.BlockSpec(memory_space=pltpu.VMEM)], out_specs=pl.BlockSpec(memory_space=pltpu.VMEM),
        out_shape=jax.ShapeDtypeStruct((R, W), F32),
        scratch_shapes=[pltpu.VMEM((N_DEV, R, W), F32), pltpu.SemaphoreType.DMA((N_DEV - 1,)), pltpu.SemaphoreType.DMA((N_DEV - 1,))],
    )(s)


HBM_SPEC = pl.BlockSpec(memory_space=pltpu.HBM)
SEM_SPEC = pl.BlockSpec(memory_space=pltpu.SEMAPHORE)
DATAFLOW = pltpu.SideEffectType.DATAFLOW_SIDE_EFFECTING


class _InFlight(NamedTuple):
    sems: tuple
    src: jax.Array
    land: jax.Array
    token: jax.Array


def _split_start(name, src, land_shape, land_dtype, n, copies, after=()):
    n_after = len(after)

    def body(src_ref, land_ref, *rest):
        rest = rest[n_after:]
        sems, token = rest[:2 * n], rest[-1]
        for k, (s, d, peer) in enumerate(copies(src_ref, land_ref)):
            pltpu.make_async_remote_copy(src_ref=s, dst_ref=d, send_sem=sems[k], recv_sem=sems[n + k],
                                         device_id=peer, device_id_type=MESH).start()
        token[...] = jnp.zeros_like(token)

    outs = pl.pallas_call(
        body, name=name,
        out_shape=(*[pltpu.SemaphoreType.DMA(())] * (2 * n), pltpu.HBM(src.shape, src.dtype), pltpu.HBM(land_shape, land_dtype),
                   jax.ShapeDtypeStruct((8, 128), F32)),
        in_specs=(HBM_SPEC, HBM_SPEC, *[ANY] * n_after),
        out_specs=(*[SEM_SPEC] * (2 * n), HBM_SPEC, HBM_SPEC, pl.BlockSpec(memory_space=pltpu.VMEM)),
        input_output_aliases={0: 2 * n, 1: 2 * n + 1},
        compiler_params=pltpu.CompilerParams(has_side_effects=DATAFLOW),
    )(pltpu.with_memory_space_constraint(src, pltpu.HBM), pltpu.with_memory_space_constraint(lax.empty(land_shape, land_dtype), pltpu.HBM),
      *after)
    return _InFlight(tuple(outs[:2 * n]), outs[2 * n], outs[2 * n + 1], outs[2 * n + 2])


def _split_wait(name, flight, after, n, copies):
    def body(src_ref, land_ref, *rest):
        sems = rest[:2 * n]
        for k, (s, d, peer) in enumerate(copies(src_ref, land_ref)):
            cp = pltpu.make_async_remote_copy(src_ref=s, dst_ref=d, send_sem=sems[k], recv_sem=sems[n + k],
                                              device_id=peer, device_id_type=MESH)
            cp.wait_send()
            cp.wait_recv()

    return pl.pallas_call(
        body, name=name,
        out_shape=(pltpu.HBM(flight.src.shape, flight.src.dtype), pltpu.HBM(flight.land.shape, flight.land.dtype)),
        in_specs=(HBM_SPEC, HBM_SPEC, *[SEM_SPEC] * (2 * n), ANY),
        out_specs=(HBM_SPEC, HBM_SPEC), input_output_aliases={0: 0, 1: 1},
        compiler_params=pltpu.CompilerParams(has_side_effects=DATAFLOW),
    )(flight.src, flight.land, *flight.sems, after)


def _gather_copies(src_ref, land_ref):
    x, y, c = _me()
    return [(src_ref, land_ref.at[2 * x + y], (*chip, c)) for chip in _other_chips(x, y)]


def _gather_start(packed, tag, after=()):
    return _split_start(f"gather_start_{tag}", packed, (N_CHIPS, *packed.shape), packed.dtype, 3, _gather_copies, after)


def _gather_wait(flight, after, tag):
    src, others = _split_wait(f"gather_wait_{tag}", flight, after, 3, _gather_copies)
    return lax.dynamic_update_slice(others, src[None], (2 * lax.axis_index("x") + lax.axis_index("y"), 0, 0))


def _swap_copies(src_ref, land_ref):
    x, y, c = _me()
    half = land_ref.shape[1]
    return [(src_ref.at[:, pl.ds((1 - c) * half, half), :], land_ref, (x, y, 1 - c))]


def _swap_start(g, tag):
    S, R, W = g.shape
    return _split_start(f"swap_halves_start_{tag}", g, (S, R // 2, W), g.dtype, 1, _swap_copies)


def _swap_wait(flight, after, tag):
    return _split_wait(f"swap_halves_wait_{tag}", flight, after, 1, _swap_copies)


def _scatter_copies(src_ref, land_ref):
    x, y, c = _me()
    return [(src_ref.at[2 * chip[0] + chip[1]], land_ref.at[j], (*chip, c)) for j, chip in enumerate(_other_chips(x, y))]


def _scatter_start(part, tag):
    S, h, W = part.shape
    return _split_start(f"scatter_chips_start_{tag}", part, (S - 1, h, W), part.dtype, 3, _scatter_copies)


def _scatter_wait(flight, after, tag):
    return _split_wait(f"scatter_chips_wait_{tag}", flight, after, 3, _scatter_copies)[1]


def _join_copies(src_ref, land_ref):
    x, y, c = _me()
    h = src_ref.shape[0]
    return [(src_ref, land_ref.at[pl.ds(c * h, h), :], (x, y, 1 - c))]


def _join_start(mine):
    h, W = mine.shape
    return _split_start("join_halves_start", mine, (2 * h, W), mine.dtype, 1, _join_copies)


def _join_wait(flight, after):
    src, other = _split_wait("join_halves_wait", flight, after, 1, _join_copies)
    return lax.dynamic_update_slice(other, src, (lax.axis_index("c") * src.shape[0], 0))


def _adamw(name, g, w, m, v):
    R, C = w.shape
    tm = R
    for cand in (256, 128, 64, 32, 16, 8):
        if R % cand == 0:
            tm = cand
            break

    def body(g, w, m, v):
        m = ADAM_B1 * m + (1.0 - ADAM_B1) * g
        v = ADAM_B2 * v + (1.0 - ADAM_B2) * jnp.square(g)
        m_hat = m / (1.0 - ADAM_B1 ** ADAM_STEP)
        v_hat = v / (1.0 - ADAM_B2 ** ADAM_STEP)
        delta = -ADAM_LR * (m_hat / (jnp.sqrt(v_hat) + ADAM_EPS) + ADAM_WD * w)
        return delta, m, v

    return _rowwise(name, body, R, tm, [_row(t, tm) for t in (g, w, m, v)], [(C, F32)] * 3)


def _unpack_weights(gathered, names):
    S = gathered.shape[0]
    shard_shapes = {"w_in": (D_MODEL, (QKV_WIDTH + 2 * D_MODEL) // S), "w_branch_na": (NA_WIDTH, D_MODEL // S),
                    "w_branch_dil": (DIL_OUT_WIDTH, D_MODEL // S), "w_out": (D_MODEL // S, D_MODEL),
                    "w_up": (D_MODEL, D_FF // S), "w_down": (D_FF // S, D_MODEL),
                    "w_ple_gate": (D_MODEL // S, D_MODEL), "w_ple_proj": (PLE_DIM, D_MODEL // S)}
    col_sharded = {"w_in", "w_branch_na", "w_branch_dil", "w_up", "w_ple_proj"}
    out, r0 = {}, 0
    for name in names:
        rows, cols = shard_shapes[name]
        n = rows * cols // PACK_W
        t = gathered[:, r0:r0 + n, :].reshape(S, rows, cols)
        r0 += n
        out[name] = t.transpose(1, 0, 2).reshape(rows, S * cols) if name in col_sharded else t.reshape(S * rows, cols)
    return out


def _pack_grads(grads, names):
    col_sharded = {"w_in", "w_branch_na", "w_branch_dil", "w_up", "w_ple_proj"}
    per_chip = []
    for s in range(N_CHIPS):
        rows = []
        for name in names:
            g = grads[name]
            if name in col_sharded:
                w = g.shape[1] // N_CHIPS
                rows.append(_pack_rows(g[:, s * w:(s + 1) * w]))
            else:
                h = g.shape[0] // N_CHIPS
                rows.append(_pack_rows(g[s * h:(s + 1) * h]))
        per_chip.append(jnp.concatenate(rows, axis=0))
    return jnp.stack(per_chip)


def _unpack_shard(packed, shapes, names):
    out, r0 = {}, 0
    for name in names:
        rows, cols = shapes[name]
        n = rows * cols // PACK_W
        out[name] = packed[r0:r0 + n].reshape(rows, cols)
        r0 += n
    return out


def kernel(x, p, positions, g_mix, w_in, rpb, w_branch_na, w_branch_dil, w_out, g_mlp, w_up, w_down, g_ple, w_ple_gate, w_ple_proj, g_final, loss_target, m_g_mix, m_w_in, m_rpb, m_w_branch_na, m_w_branch_dil, m_w_out, m_g_mlp, m_w_up, m_w_down, m_g_ple, m_w_ple_gate, m_w_ple_proj, m_g_final, v_g_mix, v_w_in, v_rpb, v_w_branch_na, v_w_branch_dil, v_w_out, v_g_mlp, v_w_up, v_w_down, v_g_ple, v_w_ple_gate, v_w_ple_proj, v_g_final):
    shards = {"w_in": w_in[0], "w_branch_na": w_branch_na[0], "w_branch_dil": w_branch_dil[0], "w_out": w_out[0],
              "w_up": w_up[0], "w_down": w_down[0], "w_ple_gate": w_ple_gate[0], "w_ple_proj": w_ple_proj[0]}
    m_shards = {"w_in": m_w_in[0], "w_branch_na": m_w_branch_na[0], "w_branch_dil": m_w_branch_dil[0], "w_out": m_w_out[0],
                "w_up": m_w_up[0], "w_down": m_w_down[0], "w_ple_gate": m_w_ple_gate[0], "w_ple_proj": m_w_ple_proj[0]}
    v_shards = {"w_in": v_w_in[0], "w_branch_na": v_w_branch_na[0], "w_branch_dil": v_w_branch_dil[0], "w_out": v_w_out[0],
                "w_up": v_w_up[0], "w_down": v_w_down[0], "w_ple_gate": v_w_ple_gate[0], "w_ple_proj": v_w_ple_proj[0]}

    W = _unpack_weights(_gather_weights(_pack_rows(shards["w_in"].astype(BF))), GATHER_FIRST)
    mix_flight = _gather_start(jnp.concatenate([_pack_rows(shards[n].astype(BF)) for n in GATHER_MIX], axis=0), "mix")
    rest_flight = _gather_start(jnp.concatenate([_pack_rows(shards[n].astype(BF)) for n in GATHER_MLP], axis=0), "mlp",
                                after=(mix_flight.token,))
    w_qkv, w_gates = W["w_in"][:, :QKV_WIDTH], W["w_in"][:, QKV_WIDTH:]

    xs, ps, tgt = x[0], p[0, 0], loss_target[0]
    T = xs.shape[0]
    TM = 512
    gm, gl, gp, gf = g_mix, g_mlp, g_ple, g_final.reshape(1, D_MODEL)
    cos2, sin_signed = _rope_tables(positions[0])
    tab = _na_bias_table(rpb[0])

    a = _rowwise("norm_mix", lambda h, g: h * _rms(h) * g, T, TM, [_row(xs, TM), _full(gm)], [(D_MODEL, BF)],
                 after=(rest_flight.token,))
    n3 = 3 * NA_WIDTH
    qkv = _mm("in_na", a, w_qkv[:, :n3], "nn", 1024, 768, 1024, [BF])
    z_dil = _mm("in_dil", a, w_qkv[:, n3:], "nn", 1024, 1152, 1024, [F32])
    z_gates = _mm("in_gates", a, w_gates, "nn", 1024,1024, 1024, [BF])

    dil_ops = _qkv_prep(z_dil, cos2, sin_signed, TM)
    y_na = _na_fwd(qkv, tab)
    band = [_band_fwd(*dil_ops[g], g) for g in range(len(DIL_GROUPS))]
    y_dil, w_grp, o_nat = _dil_merge_fwd([b[0] for b in band], [b[1] for b in band], T, TM)

    W.update(_unpack_weights(_gather_wait(mix_flight, y_dil, "mix"), GATHER_MIX))
    u_na = _mm("branch_na", y_na, W["w_branch_na"], "nn", 1024,1024, 512, [BF])
    u_dil = _mm("branch_dil", y_dil, W["w_branch_dil"], "nn", 1024,1024, 256, [BF])
    mixed = _rowwise(
        "gate_mix", lambda gn, gd, un, ud: _sigmoid(gn.astype(F32)) * un.astype(F32) + _sigmoid(gd.astype(F32)) * ud.astype(F32), T, TM,
        [_row(z_gates, TM, 0, D_MODEL), _row(z_gates, TM, 1, D_MODEL), _row(u_na, TM), _row(u_dil, TM)], [(D_MODEL, BF)])
    mix_out = _mm("out_proj", mixed, W["w_out"], "nn", 1024,1024, 1024, [F32])

    def add_norm(h, d, g):
        h = h + d
        return h, h * _rms(h) * g

    h1, cn = _rowwise("add_norm_mlp", add_norm, T, TM, [_row(xs, TM), _row(mix_out, TM), _full(gl)], [(D_MODEL, F32), (D_MODEL, BF)])
    W.update(_unpack_weights(_gather_wait(rest_flight, cn, "mlp"), GATHER_MLP))
    up, act = _mm("mlp_up", cn, W["w_up"], "nn", 1024,1024, 1024, [BF, BF],
                  epilogue=lambda acc: (acc, jnp.square(jnp.maximum(acc, 0.0))))
    mlp_out = _mm("mlp_down", act, W["w_down"], "nn", 1024,1024, 1024, [F32])
    h2, en = _rowwise("add_norm_ple", add_norm, T, TM, [_row(h1, TM), _row(mlp_out, TM), _full(gp)], [(D_MODEL, F32), (D_MODEL, BF)])
    gt = _mm("ple_gate", en, W["w_ple_gate"], "nn", 1024,1024, 1024, [F32])
    pp = _mm("ple_proj", ps, W["w_ple_proj"], "nn", 1024,1024, 256, [F32])

    def head(h2t, gtt, ppt, tg, g):
        sg = _sigmoid(gtt)
        h3 = h2t + sg * ppt
        yo = h3 * _rms(h3) * g
        diff = yo - tg
        loss = 0.5 * jnp.sum(jnp.mean(jnp.square(diff), axis=-1, keepdims=True), axis=0, keepdims=True)
        dh3, dg = _rms_bwd(diff * (1.0 / D_MODEL), h3, g)
        return dh3, dh3 * ppt * sg * (1.0 - sg), dh3 * sg, jnp.broadcast_to(loss, (1, 128)), dg

    dh3, d_gt, d_pp, loss_part, dg_final = _rowwise(
        "loss_head", head, T, TM, [_row(h2, TM), _row(gt, TM), _row(pp, TM), _row(tgt, TM), _full(gf)],
        [(D_MODEL, F32), (D_MODEL, BF), (D_MODEL, BF)], sums=[128, D_MODEL])

    grads = {}
    early_shapes = {n: shards[n].shape for n in REDUCE_EARLY}
    early_rows = sum(r * c for r, c in early_shapes.values()) // PACK_W
    shard_rows = D_MODEL // N_CHIPS
    early_buf = _mm("g_ple_gate", en, d_gt, "tn", shard_rows, 1024, 1024, [F32],
                    into=(jax.ShapeDtypeStruct((N_CHIPS, early_rows, PACK_W), F32), (None, shard_rows, PACK_W),
                          lambda i, j: (i, 2 * D_MODEL // shard_rows, 0)))
    g_ple_proj = _mm("g_ple_proj", ps, d_pp, "tn", 256, 1024, 1024,[F32])
    early_buf = lax.dynamic_update_slice(
        early_buf, jnp.stack([_pack_rows(g_ple_proj[:, s * shard_rows:(s + 1) * shard_rows]) for s in range(N_CHIPS)]),
        (0, 2 * D_MODEL + shard_rows, 0))
    d_en = _mm("d_ple_gate", d_gt, W["w_ple_gate"], "nt", 1024,1024, 1024, [F32])

    def add_norm_bwd(dh_out, dn, h, g):
        dh, dg = _rms_bwd(dn, h, g)
        dh = dh_out + dh
        return dh, dh, dg

    dh2, dh2_b, dg_ple = _rowwise("add_norm_ple_bwd", add_norm_bwd, T, TM, [_row(dh3, TM), _row(d_en, TM), _row(h2, TM), _full(gp)],
                                  [(D_MODEL, F32), (D_MODEL, BF)], sums=[D_MODEL])
    d_up = _mm("d_mlp_down", dh2_b, W["w_down"], "nt", 1024,1024, 1024, [BF],
               epilogue=lambda acc, u: (acc * (2.0 * jnp.maximum(u.astype(F32), 0.0)),), extras=(up,))
    early_buf = _mm("g_mlp_down", act, dh2_b, "tn", 1024, 1024, 1024,[F32],
                    into=(early_buf, (None, D_MODEL, PACK_W), lambda i, j: (i, 1, 0)))
    early_buf = _mm("g_mlp_up", cn, d_up, "tn", 1024, 1024, 1024,[F32],
                    into=(early_buf, (None, D_MODEL, PACK_W), lambda i, j: (j, 0, 0)))
    early_tm = early_rows // 4
    swap_flight = _swap_start(early_buf, "early")
    d_cn = _mm("d_mlp_up", d_up, W["w_up"], "nt", 1024,1024, 1024, [F32], after=(swap_flight.token,))
    dh1, dh1_b, dg_mlp = _rowwise("add_norm_mlp_bwd", add_norm_bwd, T, TM, [_row(dh2, TM), _row(d_cn, TM), _row(h1, TM), _full(gl)],
                                  [(D_MODEL, F32), (D_MODEL, BF)], sums=[D_MODEL])
    early_g, early_got = _swap_wait(swap_flight, dh1_b, "early")
    early_pair, early_pair_b = _pair_sum(early_g, early_got, early_tm)
    scatter_flight = _scatter_start(early_pair_b, "early")
    d_mixed = _mm("d_out_proj", dh1_b, W["w_out"], "nt", 1024,1024, 1024, [F32], after=(scatter_flight.token,))
    grads["w_out"] = _mm("g_out_proj", mixed, dh1_b, "tn", 1024, 1024, 1024,[F32])

    def gate_bwd(dm, gn, gd, un, ud):
        gn, gd, un, ud = (t.astype(F32) for t in (gn, gd, un, ud))
        sn, sd = _sigmoid(gn), _sigmoid(gd)
        return jnp.concatenate([dm * un * sn * (1.0 - sn), dm * ud * sd * (1.0 - sd)], axis=1), dm * sn, dm * sd

    dz_gates, d_u_na, d_u_dil = _rowwise(
        "gate_mix_bwd", gate_bwd, T, TM,
        [_row(d_mixed, TM), _row(z_gates, TM, 0, D_MODEL), _row(z_gates, TM, 1, D_MODEL), _row(u_na, TM), _row(u_dil, TM)],
        [(2 * D_MODEL, BF), (D_MODEL, BF), (D_MODEL, BF)])
    grads["w_branch_na"] = _mm("g_branch_na", y_na, d_u_na, "tn", 1024, 1024, 1024,[F32])
    grads["w_branch_dil"] = _mm("g_branch_dil", y_dil, d_u_dil, "tn", 256, 1024, 1024,[F32])
    d_y_na = _mm("d_branch_na", d_u_na, W["w_branch_na"], "nt", 1024,512, 1024, [BF])
    d_y_dil = _mm("d_branch_dil", d_u_dil, W["w_branch_dil"], "nt", 1024,256, 1024, [F32])

    dqa, dka, dva, dtab = _na_bwd(qkv, tab, d_y_na)
    d_rpb = _na_rpb_grad(dtab)[:, :2 * NA_WIN_ROWS - 1, :2 * NA_WIN_COLS - 1]

    do_res, dlse_res = _dil_merge_bwd(d_y_dil, o_nat, w_grp, TM)
    d_dil = [_band_bwd(*dil_ops[g], do_res[g], dlse_res[g], g) for g in range(len(DIL_GROUPS))]

    me_chip = 2 * lax.axis_index("x") + lax.axis_index("y")
    early_mine = _chip_sum(lax.dynamic_index_in_dim(early_pair, me_chip, 0, keepdims=False),
                           _scatter_wait(scatter_flight, d_dil[-1][-1], "early"), early_tm)
    join_flight = _join_start(early_mine)
    dz_qkv = _qkv_unprep((dqa, dka, dva), d_dil, cos2, sin_signed, TM, after=(join_flight.token,))
    grads["w_in"] = jnp.concatenate([
        _mm("g_in_qkv", a, dz_qkv, "tn", 1024, 1280, 1024,[F32]),
        _mm("g_in_gates", a, dz_gates, "tn", 1024, 1024, 1024,[F32])], axis=1)
    late_shapes = {n: shards[n].shape for n in REDUCE_LATE}
    late_tm = sum(r * c for r, c in late_shapes.values()) // PACK_W // 4
    late_swap = _swap_start(_pack_grads(grads, REDUCE_LATE), "late")
    d_a = _mm("d_in_qkv", dz_qkv, w_qkv, "nt", 1024,1024, 1280, [F32], after=(late_swap.token,))
    late_g, late_got = _swap_wait(late_swap, d_a, "late")
    late_pair, late_pair_b = _pair_sum(late_g, late_got, late_tm)
    late_scatter = _scatter_start(late_pair_b, "late")
    d_a = _mm("d_in_gates", dz_gates, w_gates, "nt", 1024,1024, 1024, [F32], epilogue=lambda acc, e: (acc + e,), extras=(d_a,),
              after=(late_scatter.token,))

    def first_bwd(dh_out, dn, h, g):
        dh, dg = _rms_bwd(dn, h, g)
        return dh_out + dh, dg

    grad_x, dg_mix = _rowwise("norm_mix_bwd", first_bwd, T, TM, [_row(dh1, TM), _row(d_a, TM), _row(xs, TM), _full(gm)],
                              [(D_MODEL, F32)], sums=[D_MODEL])
    g_shard = _unpack_shard(_join_wait(join_flight, grad_x), early_shapes, REDUCE_EARLY)

    n_rpb = rpb.size
    rpb_rows = 4
    small = jnp.concatenate([
        dg_mix, dg_mlp, dg_ple, dg_final,
        jnp.pad(d_rpb.reshape(-1), (0, rpb_rows * D_MODEL - n_rpb)).reshape(rpb_rows, D_MODEL),
        jnp.pad(loss_part, ((0, 0), (0, D_MODEL - loss_part.shape[1]))),
        jnp.zeros((SMALL_ROWS - 5 - rpb_rows, D_MODEL), F32)], axis=0)
    small = _allreduce_small(small)
    loss = small[4 + rpb_rows, 0]

    def small_pack(a0, a1, a2, a3, r):
        return jnp.concatenate([a0.reshape(1, -1), a1.reshape(1, -1), a2.reshape(1, -1), a3.reshape(1, -1),
                                jnp.pad(r.reshape(-1), (0, rpb_rows * D_MODEL - n_rpb)).reshape(rpb_rows, D_MODEL)], axis=0)

    g_small = small[:4 + rpb_rows]
    small_res = _adamw("adamw_small", g_small, small_pack(g_mix, g_mlp, g_ple, g_final, rpb),
                       small_pack(m_g_mix, m_g_mlp, m_g_ple, m_g_final, m_rpb), small_pack(v_g_mix, v_g_mlp, v_g_ple, v_g_final, v_rpb))

    def small_unpack(t):
        return {"g_mix": t[0].reshape(g_mix.shape), "g_mlp": t[1].reshape(g_mlp.shape), "g_ple": t[2].reshape(g_ple.shape),
                "g_final": t[3].reshape(g_final.shape), "rpb": t[4:].reshape(-1)[:n_rpb].reshape(rpb.shape)}

    out = {"grad": small_unpack(g_small)}
    for kind, t in zip(("delta", "new_m", "new_v"), small_res, strict=True):
        out[kind] = small_unpack(t)
    def update(names):
        for n in names:
            out["grad"][n] = g_shard[n][None]
            res = _adamw("adamw_" + n, g_shard[n], shards[n], m_shards[n], v_shards[n])
            for kind, t in zip(("delta", "new_m", "new_v"), res, strict=True):
                out[kind][n] = t[None]

    update(REDUCE_EARLY)
    late_others = _scatter_wait(late_scatter, out["new_v"][REDUCE_EARLY[-1]], "late")
    late_mine = _chip_sum(lax.dynamic_index_in_dim(late_pair, me_chip, 0, keepdims=False), late_others, late_tm)
    g_shard.update(_unpack_shard(_join_halves(late_mine), late_shapes, REDUCE_LATE))
    update(REDUCE_LATE)

    order = ["g_mix", "w_in", "rpb", "w_branch_na", "w_branch_dil", "w_out", "g_mlp", "w_up", "w_down", "g_ple",
             "w_ple_gate", "w_ple_proj", "g_final"]
    return (loss, grad_x[None], *[out["grad"][n] for n in order], *[out["delta"][n] for n in order],
            *[out["new_m"][n] for n in order], *[out["new_v"][n] for n in order])
```

```python
import functools
from typing import NamedTuple

import numpy as np
import jax
import jax.numpy as jnp
from jax import lax
from jax.experimental import pallas as pl
from jax.experimental.pallas import tpu as pltpu

BF = jnp.bfloat16
F32 = jnp.float32
MESH = pl.DeviceIdType.MESH
ANY = pl.BlockSpec(memory_space=pl.ANY)

V7X_VMEM_BYTES = 64 * 1024 * 1024
VMEM_LIMIT = V7X_VMEM_BYTES - 16 * 1024 * 1024

D_MODEL = 1024
HEAD_DIM = 64
GRID_W = 64
NA_HEADS = 8
NA_WIN_ROWS = 8
NA_WIN_COLS = 16
NA_WIDTH = NA_HEADS * HEAD_DIM
DIL_GROUPS = ((128, 1), (512, 4), (2048, 16))
DIL_HPG = 4
DIL_HEADS = DIL_HPG * len(DIL_GROUPS)
DIL_WIDTH = DIL_HEADS * HEAD_DIM
DIL_OUT_WIDTH = DIL_HPG * HEAD_DIM
DIL_RADIUS = 64
QKV_WIDTH = 3 * NA_WIDTH + 3 * DIL_WIDTH
D_FF = 4 * D_MODEL
PLE_DIM = 256
ROPE_THETA = 10000.0
RMS_EPS = 1e-6
NEG_INF = -1e30
Q_SCALE = HEAD_DIM ** -0.5

ADAM_LR = 0.001
ADAM_B1 = 0.9
ADAM_B2 = 0.999
ADAM_EPS = 1e-08
ADAM_WD = 0.01
ADAM_STEP = 10

N_CHIPS = 4
N_DEV = 8
PACK_W = 1024
BIG = ("w_in", "w_branch_na", "w_branch_dil", "w_out", "w_up", "w_down", "w_ple_gate", "w_ple_proj")
GATHER_FIRST = ("w_in",)
GATHER_MIX = ("w_branch_na", "w_branch_dil", "w_out")
GATHER_MLP = ("w_up", "w_down", "w_ple_gate", "w_ple_proj")
REDUCE_EARLY = ("w_up", "w_down", "w_ple_gate", "w_ple_proj")
REDUCE_LATE = ("w_in", "w_branch_na", "w_branch_dil", "w_out")
SMALL_ROWS = 16


def _cparams(sem=None):
    return pltpu.CompilerParams(dimension_semantics=sem, vmem_limit_bytes=VMEM_LIMIT)


def _mm(name, a, b, mode, tm, tn, tk, out_dtypes, epilogue=None, extras=(), consts=(), sums=(), after=(), into=None):
    if mode == "nn":
        (M, K), N = a.shape, b.shape[1]
    elif mode == "nt":
        (M, K), N = a.shape, b.shape[0]
    else:
        (K, M), N = a.shape, b.shape[1]
    tm, tn, tk = min(tm, M), min(tn, N), min(tk, K)
    assert M % tm == 0 and N % tn == 0 and K % tk == 0, (name, M, N, K, tm, tn, tk)
    if mode == "nn":
        a_spec = pl.BlockSpec((tm, tk), lambda i, j, k: (i, k))
        b_spec = pl.BlockSpec((tk, tn), lambda i, j, k: (k, j))
        dims = (((1,), (0,)), ((), ()))
    elif mode == "nt":
        a_spec = pl.BlockSpec((tm, tk), lambda i, j, k: (i, k))
        b_spec = pl.BlockSpec((tn, tk), lambda i, j, k: (j, k))
        dims = (((1,), (1,)), ((), ()))
    else:
        a_spec = pl.BlockSpec((tk, tm), lambda i, j, k: (k, i))
        b_spec = pl.BlockSpec((tk, tn), lambda i, j, k: (k, j))
        dims = (((0,), (0,)), ((), ()))
    nk = K // tk
    n_extra, n_const, n_out, n_sum = len(extras), len(consts), len(out_dtypes), len(sums)
    tile = pl.BlockSpec((tm, tn), lambda i, j, k: (i, j))
    assert not sums or tn == N, "row sums need whole rows in a tile"

    n_after = len(after)

    def body(a_ref, b_ref, *rest):
        extra_refs, rest = rest[:n_extra + n_const], rest[n_extra + n_const + n_after:]
        out_refs, sum_refs, acc = rest[:n_out], rest[n_out:n_out + n_sum], rest[-1]
        i, k = pl.program_id(0), pl.program_id(2)

        @pl.when(k == 0)
        def _():
            acc[...] = jnp.zeros_like(acc)

        acc[...] += lax.dot_general(a_ref[...].astype(BF), b_ref[...].astype(BF), dims, preferred_element_type=F32)

        @pl.when(k == nk - 1)
        def _():
            outs = (acc[...],) if epilogue is None else epilogue(acc[...], *[e[...] for e in extra_refs])
            for o_ref, val in zip(out_refs, outs[:n_out], strict=True):
                o_ref[...] = val.astype(o_ref.dtype)
            for s_ref, val in zip(sum_refs, outs[n_out:], strict=True):
                @pl.when(i == 0)
                def _():
                    s_ref[...] = val

                @pl.when(i != 0)
                def _():
                    s_ref[...] += val

    out_specs = [tile] * n_out + [pl.BlockSpec((1, c), lambda i, j, k: (0, 0)) for c in sums]
    out_shape = [jax.ShapeDtypeStruct((M, N), dt) for dt in out_dtypes] + [jax.ShapeDtypeStruct((1, c), F32) for c in sums]
    operands, aliases = [a, b, *extras, *consts, *after], {}
    in_specs = ([a_spec, b_spec] + [tile] * n_extra
                + [pl.BlockSpec(c.shape, functools.partial(lambda nd, i, j, k: (0,) * nd, c.ndim)) for c in consts] + [ANY] * n_after)
    if into is not None:
        assert n_out == 1
        target, block, index = into
        out_specs = [pl.BlockSpec(block, lambda i, j, k: index(i, j))]
        out_shape = [jax.ShapeDtypeStruct(target.shape, target.dtype)]
        if not isinstance(target, jax.ShapeDtypeStruct):
            aliases = {len(operands): 0}
            operands.append(target)
            in_specs.append(ANY)
            n_after += 1

    outs = pl.pallas_call(
        body, name=name, grid=(M // tm, N // tn, nk),
        in_specs=in_specs, out_specs=out_specs, out_shape=out_shape,
        scratch_shapes=[pltpu.VMEM((tm, tn), F32)], input_output_aliases=aliases,
        compiler_params=_cparams(("arbitrary",) * 3 if sums else ("parallel", "parallel", "arbitrary")),
    )(*operands)
    return outs[0] if len(outs) == 1 else outs


def _row(arr, tm, col_block=None, width=None):
    width = arr.shape[1] if width is None else width
    cb = 0 if col_block is None else col_block
    return arr, pl.BlockSpec((tm, width), lambda i: (i, cb))


def _full(arr):
    nd = arr.ndim
    return arr, pl.BlockSpec(arr.shape, lambda i: (0,) * nd)


def _rowwise(name, body, T, tm, ins, outs, sums=(), after=()):
    n_in, n_out, n_sum, n_after = len(ins), len(outs), len(sums), len(after)

    def kern(*refs):
        in_refs, refs = refs[:n_in], refs[n_in + n_after:]
        out_refs, sum_refs = refs[:n_out], refs[n_out:]
        res = body(*[r[...] for r in in_refs])
        res = res if isinstance(res, tuple) else (res,)
        for o_ref, val in zip(out_refs, res[:n_out], strict=True):
            o_ref[...] = val.astype(o_ref.dtype)
        if n_sum:
            @pl.when(pl.program_id(0) == 0)
            def _():
                for s_ref in sum_refs:
                    s_ref[...] = jnp.zeros_like(s_ref)

            for s_ref, val in zip(sum_refs, res[n_out:], strict=True):
                s_ref[...] += val

    res = pl.pallas_call(
        kern, name=name, grid=(T // tm,),
        in_specs=[spec for _, spec in ins] + [ANY] * n_after,
        out_specs=[pl.BlockSpec((tm, c), lambda i: (i, 0)) for c, _ in outs]
        + [pl.BlockSpec((1, c), lambda i: (0, 0)) for c in sums],
        out_shape=[jax.ShapeDtypeStruct((T, c), dt) for c, dt in outs]
        + [jax.ShapeDtypeStruct((1, c), F32) for c in sums],
        compiler_params=_cparams(("arbitrary",)),
    )(*[a for a, _ in ins], *after)
    return res[0] if len(res) == 1 else res


def _sigmoid(x):
    return 1.0 / (1.0 + jnp.exp(-x))


def _rms(h):
    return lax.rsqrt(jnp.mean(h * h, axis=-1, keepdims=True) + RMS_EPS)


def _rms_bwd(dy, h, g):
    r = _rms(h)
    n = h * r
    dn = dy * g
    dh = r * (dn - n * jnp.mean(dn * n, axis=-1, keepdims=True))
    return dh, jnp.sum(dy * n, axis=0, keepdims=True)


def _rope(x, cos2, sin_signed):
    lane = lax.broadcasted_iota(jnp.int32, x.shape, 1)
    swapped = jnp.where((lane % HEAD_DIM) < HEAD_DIM // 2, pltpu.roll(x, 128 - HEAD_DIM // 2, 1), pltpu.roll(x, HEAD_DIM // 2, 1))
    return x * cos2 + swapped * sin_signed


def _rope_cols(x, cos2, sin_signed):
    return jnp.concatenate([_rope(x[:, c:c + 128], cos2, sin_signed) for c in range(0, x.shape[1], 128)], axis=1)


NA_KEYS = NA_WIN_ROWS * GRID_W
NA_BASES = 8


def _na_row_geometry(r, rows):
    first = jnp.clip(r - NA_WIN_ROWS // 2, 0, rows - NA_WIN_ROWS)
    base = first - r + (NA_WIN_ROWS - 1)
    return pl.multiple_of(first * GRID_W, GRID_W), base


NA_ROWS_PER_STEP = 8
NA_BWD_ROWS_PER_STEP = 8


def _softmax_rows(s):
    p = jnp.exp(s - jnp.max(s, axis=-1, keepdims=True))
    return p / jnp.sum(p, axis=-1, keepdims=True)


def _na_probs(q, kw, bias):
    return _softmax_rows(lax.dot_general(q, kw, (((1,), (1,)), ((), ())), preferred_element_type=F32) + bias)


def _split_pair(t):
    first = lax.broadcasted_iota(jnp.int32, t.shape, 1) < HEAD_DIM
    zero = jnp.zeros_like(t)
    return jnp.where(first, t, zero), jnp.where(first, zero, t)


def _join_pair(a, b):
    return jnp.where(lax.broadcasted_iota(jnp.int32, a.shape, 1) < HEAD_DIM, a, b)


_NT = (((1,), (1,)), ((), ()))
_TN = (((0,), (0,)), ((), ()))


def _na_fwd(qkv, tab):
    T = qkv.shape[0]
    rows = T // GRID_W
    n_pairs = NA_WIDTH // 128

    def body(q_ref, k_ref, v_ref, tab_ref, y_ref):
        def step(it, carry):
            geo = [_na_row_geometry(it * NA_ROWS_PER_STEP + u, rows) for u in range(NA_ROWS_PER_STEP)]
            q0s = [pl.multiple_of((it * NA_ROWS_PER_STEP + u) * GRID_W, GRID_W) for u in range(NA_ROWS_PER_STEP)]
            ss = [lax.dot_general(jnp.concatenate(_split_pair(q_ref[pl.ds(q0, GRID_W), :] * Q_SCALE), axis=0),
                                  k_ref[pl.ds(k0, NA_KEYS), :], _NT, preferred_element_type=F32)
                  for q0, (k0, _) in zip(q0s, geo)]
            ps = [_softmax_rows(s + jnp.concatenate([tab_ref[0, base], tab_ref[1, base]], axis=0)) for s, (_, base) in zip(ss, geo)]
            ys = [jnp.dot(p.astype(BF), v_ref[pl.ds(k0, NA_KEYS), :], preferred_element_type=F32) for p, (k0, _) in zip(ps, geo)]
            for q0, y2 in zip(q0s, ys):
                y_ref[pl.ds(q0, GRID_W), :] = _join_pair(y2[:GRID_W], y2[GRID_W:]).astype(y_ref.dtype)
            return carry

        lax.fori_loop(0, rows // NA_ROWS_PER_STEP, step, 0)

    def cols(first):
        return pl.BlockSpec((T, 128), lambda j: (0, first + j))

    return pl.pallas_call(
        body, name="na_fwd", grid=(n_pairs,),
        in_specs=[cols(0), cols(n_pairs), cols(2 * n_pairs), pl.BlockSpec((2, NA_BASES, GRID_W, NA_KEYS), lambda j: (j, 0, 0, 0))],
        out_specs=cols(0), out_shape=jax.ShapeDtypeStruct((T, NA_WIDTH), BF),
        compiler_params=_cparams(("parallel",)),
    )(qkv, qkv, qkv, tab)


def _na_bwd(qkv, tab, do):
    T = qkv.shape[0]
    rows = T // GRID_W
    n_pairs = NA_WIDTH // 128

    def body(q_ref, k_ref, v_ref, tab_ref, do_ref, dq_ref, dk_ref, dv_ref, dtab_ref):
        dk_ref[...] = jnp.zeros_like(dk_ref)
        dv_ref[...] = jnp.zeros_like(dv_ref)
        dtab_ref[...] = jnp.zeros_like(dtab_ref)

        def step(it, carry):
            U = NA_BWD_ROWS_PER_STEP
            geo = [_na_row_geometry(it * U + u, rows) for u in range(U)]
            q0s = [pl.multiple_of((it * U + u) * GRID_W, GRID_W) for u in range(U)]
            q2s = [jnp.concatenate(_split_pair(q_ref[pl.ds(q0, GRID_W), :] * Q_SCALE), axis=0) for q0 in q0s]
            do2s = [jnp.concatenate(_split_pair(do_ref[pl.ds(q0, GRID_W), :]), axis=0) for q0 in q0s]
            ss = [lax.dot_general(q2, k_ref[pl.ds(k0, NA_KEYS), :], _NT, preferred_element_type=F32) for q2, (k0, _) in zip(q2s, geo)]
            dps = [lax.dot_general(do2, v_ref[pl.ds(k0, NA_KEYS), :], _NT, preferred_element_type=F32) for do2, (k0, _) in zip(do2s, geo)]
            ps = [_softmax_rows(s + jnp.concatenate([tab_ref[0, base], tab_ref[1, base]], axis=0)) for s, (_, base) in zip(ss, geo)]
            dss = [p * (dp - jnp.sum(dp * p, axis=-1, keepdims=True)) for p, dp in zip(ps, dps)]
            dvs = [lax.dot_general(p.astype(BF), do2, _TN, preferred_element_type=F32) for p, do2 in zip(ps, do2s)]
            dsbs = [ds.astype(BF) for ds in dss]
            dqs = [jnp.dot(dsb, k_ref[pl.ds(k0, NA_KEYS), :], preferred_element_type=F32) for dsb, (k0, _) in zip(dsbs, geo)]
            dks = [lax.dot_general(dsb, q2, _TN, preferred_element_type=F32) for dsb, q2 in zip(dsbs, q2s)]
            for u in range(U):
                k0, base = geo[u]
                dtab_ref[0, base] += dss[u][:GRID_W]
                dtab_ref[1, base] += dss[u][GRID_W:]
                dq_ref[pl.ds(q0s[u], GRID_W), :] = _join_pair(dqs[u][:GRID_W], dqs[u][GRID_W:])
                dk_ref[pl.ds(k0, NA_KEYS), :] += dks[u]
                dv_ref[pl.ds(k0, NA_KEYS), :] += dvs[u]
            return carry

        lax.fori_loop(0, rows // NA_BWD_ROWS_PER_STEP, step, 0)

    def cols(first):
        return pl.BlockSpec((T, 128), lambda j: (0, first + j))

    tabs = pl.BlockSpec((2, NA_BASES, GRID_W, NA_KEYS), lambda j: (j, 0, 0, 0))
    wide = jax.ShapeDtypeStruct((T, NA_WIDTH), F32)
    return pl.pallas_call(
        body, name="na_bwd", grid=(n_pairs,),
        in_specs=[cols(0), cols(n_pairs), cols(2 * n_pairs), tabs, cols(0)],
        out_specs=[cols(0), cols(0), cols(0), tabs],
        out_shape=[wide, wide, wide, jax.ShapeDtypeStruct((NA_HEADS, NA_BASES, GRID_W, NA_KEYS), F32)],
        compiler_params=_cparams(("parallel",)),
    )(qkv, qkv, qkv, tab, do)


def _na_geometry_np():
    c = np.arange(GRID_W)
    cs = np.clip(c - NA_WIN_COLS // 2, 0, GRID_W - NA_WIN_COLS)
    valid = (c[None, :] >= cs[:, None]) & (c[None, :] < cs[:, None] + NA_WIN_COLS)
    off = c[None, :] - c[:, None] + (NA_WIN_COLS - 1)
    oh_col = np.zeros((GRID_W, GRID_W, 2 * NA_WIN_COLS - 1), np.float32)
    qq, kk = np.nonzero(valid)
    oh_col[qq, kk, off[qq, kk]] = 1.0
    oh_row = np.zeros((NA_BASES, NA_WIN_ROWS, 2 * NA_WIN_ROWS - 1), np.float32)
    for base in range(NA_BASES):
        for i in range(NA_WIN_ROWS):
            oh_row[base, i, base + i] = 1.0
    return valid, oh_col, oh_row


def _na_bias_table(rpb):
    valid, oh_col, oh_row = _na_geometry_np()
    hi = lax.Precision.HIGHEST
    t1 = jnp.einsum("hrc,pir->hpic", rpb, oh_row, precision=hi)
    tab = jnp.einsum("hpic,qkc->hpqik", t1, oh_col, precision=hi)
    tab = jnp.where(valid[None, None, :, None, :], tab, NEG_INF)
    return tab.reshape(rpb.shape[0], NA_BASES, GRID_W, NA_KEYS)


def _na_rpb_grad(dtab):
    H = dtab.shape[0]
    n_rows = 2 * NA_WIN_ROWS - 1
    n_cols = 2 * NA_WIN_COLS - 1

    def body(d_ref, o_ref):
        lane = lax.broadcasted_iota(jnp.int32, (GRID_W, 128), 1)
        low = lane < GRID_W
        out_rows = []
        for ro in range(n_rows):
            acc = jnp.zeros((GRID_W, 128), F32)
            for base in range(NA_BASES):
                i = ro - base
                if not 0 <= i < NA_WIN_ROWS:
                    continue
                pair = d_ref[base, :, pl.ds((i // 2) * 128, 128)]
                if i % 2:
                    pair = pltpu.roll(pair, GRID_W, 1)
                acc = acc + jnp.where(low, pair, 0.0)
            skew = pltpu.roll(acc, 0, 1, stride=1, stride_axis=0)
            diag = jnp.sum(skew, axis=0, keepdims=True)
            out_rows.append(pltpu.roll(jnp.broadcast_to(diag, (8, 128)), 128 - (GRID_W - NA_WIN_COLS), 1)[:1])
        out_rows.append(jnp.zeros((1, 128), F32))
        res = jnp.concatenate(out_rows, axis=0)
        o_ref[...] = jnp.where(lax.broadcasted_iota(jnp.int32, res.shape, 1) < n_cols, res, 0.0)

    return pl.pallas_call(
        body, name="na_rpb_grad", grid=(H,),
        in_specs=[pl.BlockSpec((None, NA_BASES, GRID_W, NA_KEYS), lambda h: (h, 0, 0, 0))],
        out_specs=pl.BlockSpec((None, n_rows + 1, 128), lambda h: (h, 0, 0)),
        out_shape=jax.ShapeDtypeStruct((H, n_rows + 1, 128), F32),
        compiler_params=_cparams(("parallel",)),
    )(jnp.flip(dtab, axis=2))


BAND_Q = 128
BAND_KEYS = BAND_Q + 2 * DIL_RADIUS


def _band_geometry(n, L):
    q0 = pl.multiple_of(n * BAND_Q, BAND_Q)
    k0 = pl.multiple_of(jnp.clip(q0 - DIL_RADIUS, 0, L - BAND_KEYS), DIL_RADIUS)
    qi = q0 + lax.broadcasted_iota(jnp.int32, (BAND_Q, BAND_KEYS), 0)
    kj = k0 + lax.broadcasted_iota(jnp.int32, (BAND_Q, BAND_KEYS), 1)
    return q0, k0, jnp.abs(qi - kj) <= DIL_RADIUS


DIL_PAIRS = DIL_OUT_WIDTH // 128


def _residue_shape(dil, T, dtype):
    return jax.ShapeDtypeStruct((DIL_PAIRS, dil, T // dil, 128), dtype)


def _residue_tile(dil, tm):
    return pl.BlockSpec((DIL_PAIRS, dil, tm // dil, 128), lambda i: (0, 0, i, 0))


def _to_natural(ref, scratch, dil, tm):
    tiles = []
    for pair in range(DIL_PAIRS):
        if dil == 1:
            tiles.append(ref[pair, 0].astype(F32))
            continue
        for r in range(dil):
            scratch[pl.ds(r, tm // dil, stride=dil), :] = ref[pair, r].astype(F32)
        tiles.append(scratch[...])
    return tiles


def _from_natural(tile, scratch, ref, pair, dil, tm):
    if dil == 1:
        ref[pair, 0] = tile.astype(ref.dtype)
        return
    scratch[...] = tile
    for r in range(dil):
        ref[pair, r] = scratch[pl.ds(r, tm // dil, stride=dil), :].astype(ref.dtype)


def _band_specs(group, T):
    dil = DIL_GROUPS[group][1]
    L = T // dil
    assert L % BAND_Q == 0 and L >= BAND_KEYS, (T, dil)
    return L, (dil * DIL_PAIRS,), pl.BlockSpec((None, None, L, 128), lambda s: (s % DIL_PAIRS, s // DIL_PAIRS, 0, 0))


BAND_BLOCKS_PER_STEP = 4


def _band_softmax(s, valid):
    s = jnp.where(valid, s, NEG_INF)
    m = jnp.max(s, axis=-1, keepdims=True)
    p = jnp.exp(s - m)
    l = jnp.sum(p, axis=-1, keepdims=True)
    return p / l, m + jnp.log(l)


def _band_fwd(q, k, v, group):
    T = q.shape[1] * q.shape[2]
    L, grid, spec = _band_specs(group, T)
    U = min(BAND_BLOCKS_PER_STEP, L // BAND_Q)

    def body(q_ref, k_ref, v_ref, o_ref, lse_ref):
        def step(it, carry):
            geo = [_band_geometry(it * U + u, L) for u in range(U)]
            ss = [lax.dot_general(jnp.concatenate(_split_pair(q_ref[pl.ds(q0, BAND_Q), :]), axis=0),
                                  k_ref[pl.ds(k0, BAND_KEYS), :], _NT, preferred_element_type=F32) for q0, k0, _ in geo]
            pls = [_band_softmax(s, jnp.concatenate([valid, valid], axis=0)) for s, (_, _, valid) in zip(ss, geo)]
            os = [jnp.dot(p.astype(BF), v_ref[pl.ds(k0, BAND_KEYS), :], preferred_element_type=F32) for (p, _), (_, k0, _) in zip(pls, geo)]
            for (q0, _, _), o2, (_, lse) in zip(geo, os, pls):
                o_ref[pl.ds(q0, BAND_Q), :] = _join_pair(o2[:BAND_Q], o2[BAND_Q:])
                lse2 = jnp.broadcast_to(lse, (2 * BAND_Q, 128))
                lse_ref[pl.ds(q0, BAND_Q), :] = _join_pair(lse2[:BAND_Q], lse2[BAND_Q:])
            return carry

        lax.fori_loop(0, L // (BAND_Q * U), step, 0)

    res = _residue_shape(DIL_GROUPS[group][1], T, F32)
    return pl.pallas_call(
        body, name=f"band_fwd_g{group}", grid=grid,
        in_specs=[spec] * 3, out_specs=[spec] * 2, out_shape=[res, res],
        compiler_params=_cparams(("parallel",)),
    )(q, k, v)


def _band_bwd(q, k, v, do, dlse, group):
    T = q.shape[1] * q.shape[2]
    L, grid, spec = _band_specs(group, T)
    U = min(BAND_BLOCKS_PER_STEP, L // BAND_Q)

    def body(q_ref, k_ref, v_ref, do_ref, dlse_ref, dq_ref, dk_ref, dv_ref):
        dk_ref[...] = jnp.zeros_like(dk_ref)
        dv_ref[...] = jnp.zeros_like(dv_ref)

        def step(it, carry):
            geo = [_band_geometry(it * U + u, L) for u in range(U)]
            q2s = [jnp.concatenate(_split_pair(q_ref[pl.ds(q0, BAND_Q), :]), axis=0) for q0, _, _ in geo]
            do2s = [jnp.concatenate(_split_pair(do_ref[pl.ds(q0, BAND_Q), :]), axis=0) for q0, _, _ in geo]
            ss = [lax.dot_general(q2, k_ref[pl.ds(k0, BAND_KEYS), :], _NT, preferred_element_type=F32) for q2, (_, k0, _) in zip(q2s, geo)]
            dps = [lax.dot_general(do2, v_ref[pl.ds(k0, BAND_KEYS), :], _NT, preferred_element_type=F32) for do2, (_, k0, _) in zip(do2s, geo)]
            ps = [_band_softmax(s, jnp.concatenate([valid, valid], axis=0))[0] for s, (_, _, valid) in zip(ss, geo)]
            dss = []
            for p, dp, (q0, _, _) in zip(ps, dps, geo):
                dl = dlse_ref[pl.ds(q0, BAND_Q), :]
                dl2 = jnp.concatenate([dl[:, :1], dl[:, HEAD_DIM:HEAD_DIM + 1]], axis=0)
                dss.append(p * (dp - jnp.sum(dp * p, axis=-1, keepdims=True) + dl2))
            dvs = [lax.dot_general(p.astype(BF), do2, _TN, preferred_element_type=F32) for p, do2 in zip(ps, do2s)]
            dsbs = [ds.astype(BF) for ds in dss]
            dqs = [jnp.dot(dsb, k_ref[pl.ds(k0, BAND_KEYS), :], preferred_element_type=F32) for dsb, (_, k0, _) in zip(dsbs, geo)]
            dks = [lax.dot_general(dsb, q2, _TN, preferred_element_type=F32) for dsb, q2 in zip(dsbs, q2s)]
            for u, (q0, k0, _) in enumerate(geo):
                dq_ref[pl.ds(q0, BAND_Q), :] = _join_pair(dqs[u][:BAND_Q], dqs[u][BAND_Q:])
                dk_ref[pl.ds(k0, BAND_KEYS), :] += dks[u]
                dv_ref[pl.ds(k0, BAND_KEYS), :] += dvs[u]
            return carry

        lax.fori_loop(0, L // (BAND_Q * U), step, 0)

    res = _residue_shape(DIL_GROUPS[group][1], T, F32)
    return pl.pallas_call(
        body, name=f"band_bwd_g{group}", grid=grid,
        in_specs=[spec] * 5, out_specs=[spec] * 3, out_shape=[res] * 3,
        compiler_params=_cparams(("parallel",)),
    )(q, k, v, do, dlse)


def _head_sums(t):
    head = lax.broadcasted_iota(jnp.int32, t.shape, 1) // HEAD_DIM
    out = jnp.zeros_like(t)
    for h in range(t.shape[1] // HEAD_DIM):
        mine = head == h
        out = jnp.where(mine, jnp.sum(jnp.where(mine, t, 0.0), axis=-1, keepdims=True), out)
    return out


def _dil_merge_fwd(os, lses, T, tm):
    G = len(DIL_GROUPS)
    W = DIL_OUT_WIDTH
    dils = [d for _, d in DIL_GROUPS]

    def body(*refs):
        o_refs, lse_refs = refs[:G], refs[G:2 * G]
        y_ref, w_refs, on_refs, scratch = refs[2 * G], refs[2 * G + 1:3 * G + 1], refs[3 * G + 1:4 * G + 1], refs[-1]
        o = [jnp.concatenate(_to_natural(r, scratch, d, tm), axis=1) for r, d in zip(o_refs, dils)]
        ls = [jnp.concatenate(_to_natural(r, scratch, d, tm), axis=1) for r, d in zip(lse_refs, dils)]
        m = functools.reduce(jnp.maximum, ls)
        es = [jnp.exp(l - m) for l in ls]
        tot = functools.reduce(jnp.add, es)
        ws = [e / tot for e in es]
        y_ref[...] = functools.reduce(jnp.add, [w * t for w, t in zip(ws, o)]).astype(y_ref.dtype)
        for g in range(G):
            w_refs[g][...] = ws[g]
            on_refs[g][...] = o[g]

    nat = pl.BlockSpec((tm, W), lambda i: (i, 0))
    res = pl.pallas_call(
        body, name="dil_merge_fwd", grid=(T // tm,),
        in_specs=[_residue_tile(d, tm) for d in dils] * 2,
        out_specs=[nat] * (2 * G + 1),
        out_shape=[jax.ShapeDtypeStruct((T, W), BF)] + [jax.ShapeDtypeStruct((T, W), F32)] * (2 * G),
        scratch_shapes=[pltpu.VMEM((tm, 128), F32)],
        compiler_params=_cparams(("parallel",)),
    )(*os, *lses)
    return res[0], res[1:G + 1], res[G + 1:]


def _dil_merge_bwd(dy, os, ws, tm):
    G = len(DIL_GROUPS)
    T, W = dy.shape
    dils = [d for _, d in DIL_GROUPS]

    def body(*refs):
        dyt = refs[0][...]
        o, w = [r[...] for r in refs[1:G + 1]], [r[...] for r in refs[G + 1:2 * G + 1]]
        do_refs, dlse_refs, scratch = refs[2 * G + 1:3 * G + 1], refs[3 * G + 1:4 * G + 1], refs[-1]
        dws = [_head_sums(dyt * t) for t in o]
        mean = functools.reduce(jnp.add, [a * b for a, b in zip(w, dws)])
        for g, d in enumerate(dils):
            do, dlse = w[g] * dyt, w[g] * (dws[g] - mean)
            for pair in range(DIL_PAIRS):
                cols = slice(pair * 128, (pair + 1) * 128)
                _from_natural(do[:, cols], scratch, do_refs[g], pair, d, tm)
                _from_natural(dlse[:, cols], scratch, dlse_refs[g], pair, d, tm)

    nat = pl.BlockSpec((tm, W), lambda i: (i, 0))
    res = pl.pallas_call(
        body, name="dil_merge_bwd", grid=(T // tm,),
        in_specs=[nat] * (2 * G + 1),
        out_specs=[_residue_tile(d, tm) for d in dils] * 2,
        out_shape=[_residue_shape(d, T, BF) for d in dils] + [_residue_shape(d, T, F32) for d in dils],
        scratch_shapes=[pltpu.VMEM((tm, 128), F32)],
        compiler_params=_cparams(("parallel",)),
    )(dy, *os, *ws)
    return res[:G], res[G:]


def _qkv_prep(z, cos2, sin_signed, tm):
    T = z.shape[0]
    G = len(DIL_GROUPS)
    dils = [d for _, d in DIL_GROUPS]
    n_dil_blocks = 3 * DIL_WIDTH // 128

    def body(*refs):
        blocks = refs[:n_dil_blocks]
        cos_ref, sin_ref = refs[n_dil_blocks], refs[1 + n_dil_blocks]
        outs = refs[2 + n_dil_blocks:]
        for part in range(3):
            for g, d in enumerate(dils):
                out = outs[g * 3 + part]
                for pair in range(DIL_PAIRS):
                    blk = blocks[part * (DIL_WIDTH // 128) + g * DIL_PAIRS + pair]
                    for r in range(d):
                        rows = pl.ds(r, tm // d, stride=d) if d > 1 else slice(None)
                        x = blk[rows, :]
                        if part < 2:
                            x = _rope(x, cos_ref[rows, :], sin_ref[rows, :])
                        if part == 0:
                            x = x * Q_SCALE
                        out[pair, r] = x.astype(out.dtype)

    lane_block = [pl.BlockSpec((tm, 128), functools.partial(lambda b, i: (i, b), b)) for b in range(n_dil_blocks)]
    tab = pl.BlockSpec((tm, 128), lambda i: (i, 0))
    res = pl.pallas_call(
        body, name="qkv_prep", grid=(T // tm,),
        in_specs=lane_block + [tab, tab],
        out_specs=[_residue_tile(d, tm) for d in dils for _ in range(3)],
        out_shape=[_residue_shape(d, T, BF) for d in dils for _ in range(3)],
        compiler_params=_cparams(("parallel",)),
    )(*[z] * n_dil_blocks, cos2, sin_signed)
    return [res[3 * g:3 + 3 * g] for g in range(G)]


def _qkv_unprep(d_na, d_dil, cos2, sin_signed, tm, after=()):
    T = d_na[0].shape[0]
    G = len(DIL_GROUPS)
    dils = [d for _, d in DIL_GROUPS]
    n_after = len(after)

    def body(*refs):
        dq, dk, dv = (r[...] for r in refs[:3])
        res_refs = refs[3:3 + 3 * G]
        cs, sn = refs[3 + 3 * G][...], refs[4 + 3 * G][...]
        out, scratch = refs[5 + 3 * G + n_after], refs[-1]
        cols = [dq * Q_SCALE, dk, dv]
        for part in range(3):
            for g, d in enumerate(dils):
                for x in _to_natural(res_refs[g * 3 + part], scratch, d, tm):
                    if part < 2:
                        x = _rope(x, cs, -sn)
                    cols.append(x * Q_SCALE if part == 0 else x)
        out[...] = jnp.concatenate(cols, axis=1).astype(out.dtype)

    wide = pl.BlockSpec((tm, NA_WIDTH), lambda i: (i, 0))
    tab = pl.BlockSpec((tm, 128), lambda i: (i, 0))
    return pl.pallas_call(
        body, name="qkv_unprep", grid=(T // tm,),
        in_specs=[wide] * 3 + [_residue_tile(d, tm) for d in dils for _ in range(3)] + [tab, tab] + [ANY] * n_after,
        out_specs=pl.BlockSpec((tm, QKV_WIDTH), lambda i: (i, 0)),
        out_shape=jax.ShapeDtypeStruct((T, QKV_WIDTH), BF),
        scratch_shapes=[pltpu.VMEM((tm, 128), F32)],
        compiler_params=_cparams(("parallel",)),
    )(*d_na, *[t for g in range(G) for t in d_dil[g]], cos2, sin_signed, *after)


def _rope_tables(positions):
    half = HEAD_DIM // 2
    inv_freq = ROPE_THETA ** (-jnp.arange(half, dtype=F32) / half)
    ang = positions.astype(F32)[:, None] * inv_freq
    cos, sin = jnp.cos(ang), jnp.sin(ang)
    return jnp.tile(jnp.concatenate([cos, cos], axis=1), (1, 2)), jnp.tile(jnp.concatenate([-sin, sin], axis=1), (1, 2))


def _pack_rows(t):
    return t.reshape(-1, PACK_W)


def _me():
    return lax.axis_index("x"), lax.axis_index("y"), lax.axis_index("c")


def _other_chips(x, y):
    return [(1 - x, y), (x, 1 - y), (1 - x, 1 - y)]


def _gather_weights(packed):
    R, W = packed.shape
    half = R // 2

    def body(in_ref, out_ref, send_sems, recv_sems):
        x, y, c = _me()
        sibling = (x, y, 1 - c)
        chips = _other_chips(x, y)

        def block(chip, core):
            return out_ref.at[2 * chip[0] + chip[1], pl.ds(core * half, half), :]

        def copy(k, chip, core, to, src=None):
            return pltpu.make_async_remote_copy(
                src_ref=block(chip, core) if src is None else src, dst_ref=block(chip, core),
                send_sem=send_sems.at[k], recv_sem=recv_sems.at[k], device_id=to, device_id_type=MESH)

        first = [copy(j, (x, y), c, (*chip, c), src=in_ref.at[pl.ds(c * half, half), :]) for j, chip in enumerate(chips)]
        for cp in first:
            cp.start()
        passed = [copy(3 + j, chip, c, sibling) for j, chip in enumerate(chips)]
        for j, chip in enumerate(chips):
            copy(j, chip, c, (x, y, c)).wait_recv()
            passed[j].start()
        for j, chip in enumerate(chips):
            copy(3 + j, chip, 1 - c, (x, y, c)).wait_recv()
        for cp in first + passed:
            cp.wait_send()

    others = pl.pallas_call(
        body, name="gather_weights",
        in_specs=[ANY], out_specs=ANY,
        out_shape=jax.ShapeDtypeStruct((N_CHIPS, R, W), packed.dtype),
        scratch_shapes=[pltpu.SemaphoreType.DMA((6,)), pltpu.SemaphoreType.DMA((6,))],
    )(packed)
    return lax.dynamic_update_slice(others, packed[None], (2 * lax.axis_index("x") + lax.axis_index("y"), 0, 0))


def _swap_halves(g):
    S, R, W = g.shape
    half = R // 2

    def body(g_ref, out_ref, send_sem, recv_sem):
        x, y, c = _me()
        cp = pltpu.make_async_remote_copy(
            src_ref=g_ref.at[:, pl.ds((1 - c) * half, half), :], dst_ref=out_ref,
            send_sem=send_sem, recv_sem=recv_sem, device_id=(x, y, 1 - c), device_id_type=MESH)
        cp.start()
        cp.wait()

    return pl.pallas_call(
        body, name="swap_halves", in_specs=[ANY], out_specs=ANY,
        out_shape=jax.ShapeDtypeStruct((S, half, W), g.dtype),
        scratch_shapes=[pltpu.SemaphoreType.DMA, pltpu.SemaphoreType.DMA],
    )(g)


def _pair_sum(g, got, tm):
    S, R, W = g.shape
    half = R // 2
    nb = half // tm

    def body(c_ref, g_ref, got_ref, o_ref, ob_ref):
        tot = g_ref[...] + got_ref[...]
        o_ref[...] = tot
        ob_ref[...] = tot.astype(ob_ref.dtype)

    tile = pl.BlockSpec((None, tm, W), lambda s, i, c_ref: (s, i, 0))
    return pl.pallas_call(
        body, name="pair_sum",
        grid_spec=pltpu.PrefetchScalarGridSpec(
            num_scalar_prefetch=1, grid=(S, nb),
            in_specs=[pl.BlockSpec((None, tm, W), lambda s, i, c_ref: (s, c_ref[0] * nb + i, 0)), tile],
            out_specs=[tile, tile]),
        out_shape=[jax.ShapeDtypeStruct((S, half, W), F32), jax.ShapeDtypeStruct((S, half, W), BF)],
        compiler_params=_cparams(("parallel", "parallel")),
    )(lax.axis_index("c").reshape(1).astype(jnp.int32), g, got)


def _scatter_chips(part):
    S, h, W = part.shape

    def body(p_ref, out_ref, send_sems, recv_sems):
        x, y, c = _me()
        chips = _other_chips(x, y)
        sends = [pltpu.make_async_remote_copy(
            src_ref=p_ref.at[2 * chip[0] + chip[1]], dst_ref=out_ref.at[j],
            send_sem=send_sems.at[j], recv_sem=recv_sems.at[j], device_id=(*chip, c), device_id_type=MESH)
            for j, chip in enumerate(chips)]
        for cp in sends:
            cp.start()
        for cp in sends:
            cp.wait()

    return pl.pallas_call(
        body, name="scatter_chips", in_specs=[ANY], out_specs=ANY,
        out_shape=jax.ShapeDtypeStruct((S - 1, h, W), part.dtype),
        scratch_shapes=[pltpu.SemaphoreType.DMA((3,)), pltpu.SemaphoreType.DMA((3,))],
    )(part)


def _chip_sum(own, others, tm):
    n, h, W = others.shape

    def body(own_ref, p_ref, o_ref):
        o_ref[...] = ((own_ref[...] + p_ref[0].astype(F32)) + p_ref[1].astype(F32)) + p_ref[2].astype(F32)

    return pl.pallas_call(
        body, name="chip_sum", grid=(h // tm,),
        in_specs=[pl.BlockSpec((tm, W), lambda i: (i, 0)), pl.BlockSpec((n, tm, W), lambda i: (0, i, 0))],
        out_specs=pl.BlockSpec((tm, W), lambda i: (i, 0)),
        out_shape=jax.ShapeDtypeStruct((h, W), F32),
        compiler_params=_cparams(("parallel",)),
    )(own, others)


def _join_halves(mine):
    h, W = mine.shape

    def body(m_ref, out_ref, send_sem, recv_sem):
        x, y, c = _me()
        cp = pltpu.make_async_remote_copy(
            src_ref=m_ref, dst_ref=out_ref.at[pl.ds(c * h, h), :],
            send_sem=send_sem, recv_sem=recv_sem, device_id=(x, y, 1 - c), device_id_type=MESH)
        cp.start()
        pltpu.make_async_remote_copy(
            src_ref=m_ref, dst_ref=out_ref.at[pl.ds((1 - c) * h, h), :],
            send_sem=send_sem, recv_sem=recv_sem, device_id=(x, y, 1 - c), device_id_type=MESH).wait_recv()
        cp.wait_send()

    other = pl.pallas_call(
        body, name="join_halves", in_specs=[ANY], out_specs=ANY,
        out_shape=jax.ShapeDtypeStruct((2 * h, W), mine.dtype),
        scratch_shapes=[pltpu.SemaphoreType.DMA, pltpu.SemaphoreType.DMA],
    )(mine)
    return lax.dynamic_update_slice(other, mine, (lax.axis_index("c") * h, 0))


def _allreduce_small(s):
    R, W = s.shape

    def body(s_ref, o_ref, buf, send_sems, recv_sems):
        x, y, c = _me()
        me = 4 * x + 2 * y + c
        buf[me] = s_ref[...]
        peers = [((x + fx) % 2, (y + fy) % 2, (c + fc) % 2) for fx in range(2) for fy in range(2) for fc in range(2)][1:]
        sends = [pltpu.make_async_remote_copy(
            src_ref=s_ref, dst_ref=buf.at[me], send_sem=send_sems.at[k], recv_sem=recv_sems.at[k],
            device_id=peer, device_id_type=MESH) for k, peer in enumerate(peers)]
        for cp in sends:
            cp.start()
        for k, peer in enumerate(peers):
            pltpu.make_async_remote_copy(
                src_ref=s_ref, dst_ref=buf.at[4 * peer[0] + 2 * peer[1] + peer[2]], send_sem=send_sems.at[k],
                recv_sem=recv_sems.at[k], device_id=peer, device_id_type=MESH).wait_recv()
        for cp in sends:
            cp.wait_send()
        total = buf[0]
        for d in range(1, N_DEV):
            total = total + buf[d]
        o_ref[...] = total

    return pl.pallas_call(
        body, name="allreduce_small",
        in_specs=[pl.BlockSpec(memory_space=pltpu.VMEM)], out_specs=pl.BlockSpec(memory_space=pltpu.VMEM),
        out_shape=jax.ShapeDtypeStruct((R, W), F32),
        scratch_shapes=[pltpu.VMEM((N_DEV, R, W), F32), pltpu.SemaphoreType.DMA((N_DEV - 1,)), pltpu.SemaphoreType.DMA((N_DEV - 1,))],
    )(s)


HBM_SPEC = pl.BlockSpec(memory_space=pltpu.HBM)
SEM_SPEC = pl.BlockSpec(memory_space=pltpu.SEMAPHORE)
DATAFLOW = pltpu.SideEffectType.DATAFLOW_SIDE_EFFECTING


class _InFlight(NamedTuple):
    sems: tuple
    src: jax.Array
    land: jax.Array
    token: jax.Array


def _split_start(name, src, land_shape, land_dtype, n, copies, after=()):
    n_after = len(after)

    def body(src_ref, land_ref, *rest):
        rest = rest[n_after:]
        sems, token = rest[:2 * n], rest[-1]
        for k, (s, d, peer) in enumerate(copies(src_ref, land_ref)):
            pltpu.make_async_remote_copy(src_ref=s, dst_ref=d, send_sem=sems[k], recv_sem=sems[n + k],
                                         device_id=peer, device_id_type=MESH).start()
        token[...] = jnp.zeros_like(token)

    outs = pl.pallas_call(
        body, name=name,
        out_shape=(*[pltpu.SemaphoreType.DMA(())] * (2 * n), pltpu.HBM(src.shape, src.dtype), pltpu.HBM(land_shape, land_dtype),
                   jax.ShapeDtypeStruct((8, 128), F32)),
        in_specs=(HBM_SPEC, HBM_SPEC, *[ANY] * n_after),
        out_specs=(*[SEM_SPEC] * (2 * n), HBM_SPEC, HBM_SPEC, pl.BlockSpec(memory_space=pltpu.VMEM)),
        input_output_aliases={0: 2 * n, 1: 2 * n + 1},
        compiler_params=pltpu.CompilerParams(has_side_effects=DATAFLOW),
    )(pltpu.with_memory_space_constraint(src, pltpu.HBM), pltpu.with_memory_space_constraint(lax.empty(land_shape, land_dtype), pltpu.HBM),
      *after)
    return _InFlight(tuple(outs[:2 * n]), outs[2 * n], outs[2 * n + 1], outs[2 * n + 2])


def _split_wait(name, flight, after, n, copies):
    def body(src_ref, land_ref, *rest):
        sems = rest[:2 * n]
        for k, (s, d, peer) in enumerate(copies(src_ref, land_ref)):
            cp = pltpu.make_async_remote_copy(src_ref=s, dst_ref=d, send_sem=sems[k], recv_sem=sems[n + k],
                                              device_id=peer, device_id_type=MESH)
            cp.wait_send()
            cp.wait_recv()

    return pl.pallas_call(
        body, name=name,
        out_shape=(pltpu.HBM(flight.src.shape, flight.src.dtype), pltpu.HBM(flight.land.shape, flight.land.dtype)),
        in_specs=(HBM_SPEC, HBM_SPEC, *[SEM_SPEC] * (2 * n), ANY),
        out_specs=(HBM_SPEC, HBM_SPEC), input_output_aliases={0: 0, 1: 1},
        compiler_params=pltpu.CompilerParams(has_side_effects=DATAFLOW),
    )(flight.src, flight.land, *flight.sems, after)


def _gather_copies(src_ref, land_ref):
    x, y, c = _me()
    return [(src_ref, land_ref.at[2 * x + y], (*chip, c)) for chip in _other_chips(x, y)]


def _gather_start(packed, tag, after=()):
    return _split_start(f"gather_start_{tag}", packed, (N_CHIPS, *packed.shape), packed.dtype, 3, _gather_copies, after)


def _gather_wait(flight, after, tag):
    src, others = _split_wait(f"gather_wait_{tag}", flight, after, 3, _gather_copies)
    return lax.dynamic_update_slice(others, src[None], (2 * lax.axis_index("x") + lax.axis_index("y"), 0, 0))


def _swap_copies(src_ref, land_ref):
    x, y, c = _me()
    half = land_ref.shape[1]
    return [(src_ref.at[:, pl.ds((1 - c) * half, half), :], land_ref, (x, y, 1 - c))]


def _swap_start(g, tag):
    S, R, W = g.shape
    return _split_start(f"swap_halves_start_{tag}", g, (S, R // 2, W), g.dtype, 1, _swap_copies)


def _swap_wait(flight, after, tag):
    return _split_wait(f"swap_halves_wait_{tag}", flight, after, 1, _swap_copies)


def _scatter_copies(src_ref, land_ref):
    x, y, c = _me()
    return [(src_ref.at[2 * chip[0] + chip[1]], land_ref.at[j], (*chip, c)) for j, chip in enumerate(_other_chips(x, y))]


def _scatter_start(part, tag):
    S, h, W = part.shape
    return _split_start(f"scatter_chips_start_{tag}", part, (S - 1, h, W), part.dtype, 3, _scatter_copies)


def _scatter_wait(flight, after, tag):
    return _split_wait(f"scatter_chips_wait_{tag}", flight, after, 3, _scatter_copies)[1]


def _join_copies(src_ref, land_ref):
    x, y, c = _me()
    h = src_ref.shape[0]
    return [(src_ref, land_ref.at[pl.ds(c * h, h), :], (x, y, 1 - c))]


def _join_start(mine):
    h, W = mine.shape
    return _split_start("join_halves_start", mine, (2 * h, W), mine.dtype, 1, _join_copies)


def _join_wait(flight, after):
    src, other = _split_wait("join_halves_wait", flight, after, 1, _join_copies)
    return lax.dynamic_update_slice(other, src, (lax.axis_index("c") * src.shape[0], 0))


def _adamw(name, g, w, m, v):
    R, C = w.shape
    tm = R
    for cand in (256, 128, 64, 32, 16, 8):
        if R % cand == 0:
            tm = cand
            break

    def body(g, w, m, v):
        m = ADAM_B1 * m + (1.0 - ADAM_B1) * g
        v = ADAM_B2 * v + (1.0 - ADAM_B2) * jnp.square(g)
        m_hat = m / (1.0 - ADAM_B1 ** ADAM_STEP)
        v_hat = v / (1.0 - ADAM_B2 ** ADAM_STEP)
        delta = -ADAM_LR * (m_hat / (jnp.sqrt(v_hat) + ADAM_EPS) + ADAM_WD * w)
        return delta, m, v

    return _rowwise(name, body, R, tm, [_row(t, tm) for t in (g, w, m, v)], [(C, F32)] * 3)


def _unpack_weights(gathered, names):
    S = gathered.shape[0]
    shard_shapes = {"w_in": (D_MODEL, (QKV_WIDTH + 2 * D_MODEL) // S), "w_branch_na": (NA_WIDTH, D_MODEL // S),
                    "w_branch_dil": (DIL_OUT_WIDTH, D_MODEL // S), "w_out": (D_MODEL // S, D_MODEL),
                    "w_up": (D_MODEL, D_FF // S), "w_down": (D_FF // S, D_MODEL),
                    "w_ple_gate": (D_MODEL // S, D_MODEL), "w_ple_proj": (PLE_DIM, D_MODEL // S)}
    col_sharded = {"w_in", "w_branch_na", "w_branch_dil", "w_up", "w_ple_proj"}
    out, r0 = {}, 0
    for name in names:
        rows, cols = shard_shapes[name]
        n = rows * cols // PACK_W
        t = gathered[:, r0:r0 + n, :].reshape(S, rows, cols)
        r0 += n
        out[name] = t.transpose(1, 0, 2).reshape(rows, S * cols) if name in col_sharded else t.reshape(S * rows, cols)
    return out


def _pack_grads(grads, names):
    col_sharded = {"w_in", "w_branch_na", "w_branch_dil", "w_up", "w_ple_proj"}
    per_chip = []
    for s in range(N_CHIPS):
        rows = []
        for name in names:
            g = grads[name]
            if name in col_sharded:
                w = g.shape[1] // N_CHIPS
                rows.append(_pack_rows(g[:, s * w:(s + 1) * w]))
            else:
                h = g.shape[0] // N_CHIPS
                rows.append(_pack_rows(g[s * h:(s + 1) * h]))
        per_chip.append(jnp.concatenate(rows, axis=0))
    return jnp.stack(per_chip)


def _unpack_shard(packed, shapes, names):
    out, r0 = {}, 0
    for name in names:
        rows, cols = shapes[name]
        n = rows * cols // PACK_W
        out[name] = packed[r0:r0 + n].reshape(rows, cols)
        r0 += n
    return out


def kernel(x, p, positions, g_mix, w_in, rpb, w_branch_na, w_branch_dil, w_out, g_mlp, w_up, w_down, g_ple, w_ple_gate, w_ple_proj, g_final, loss_target, m_g_mix, m_w_in, m_rpb, m_w_branch_na, m_w_branch_dil, m_w_out, m_g_mlp, m_w_up, m_w_down, m_g_ple, m_w_ple_gate, m_w_ple_proj, m_g_final, v_g_mix, v_w_in, v_rpb, v_w_branch_na, v_w_branch_dil, v_w_out, v_g_mlp, v_w_up, v_w_down, v_g_ple, v_w_ple_gate, v_w_ple_proj, v_g_final):
    shards = {"w_in": w_in[0], "w_branch_na": w_branch_na[0], "w_branch_dil": w_branch_dil[0], "w_out": w_out[0],
              "w_up": w_up[0], "w_down": w_down[0], "w_ple_gate": w_ple_gate[0], "w_ple_proj": w_ple_proj[0]}
    m_shards = {"w_in": m_w_in[0], "w_branch_na": m_w_branch_na[0], "w_branch_dil": m_w_branch_dil[0], "w_out": m_w_out[0],
                "w_up": m_w_up[0], "w_down": m_w_down[0], "w_ple_gate": m_w_ple_gate[0], "w_ple_proj": m_w_ple_proj[0]}
    v_shards = {"w_in": v_w_in[0], "w_branch_na": v_w_branch_na[0], "w_branch_dil": v_w_branch_dil[0], "w_out": v_w_out[0],
                "w_up": v_w_up[0], "w_down": v_w_down[0], "w_ple_gate": v_w_ple_gate[0], "w_ple_proj": v_w_ple_proj[0]}

    W = _unpack_weights(_gather_weights(_pack_rows(shards["w_in"].astype(BF))), GATHER_FIRST)
    mix_flight = _gather_start(jnp.concatenate([_pack_rows(shards[n].astype(BF)) for n in GATHER_MIX], axis=0), "mix")
    rest_flight = _gather_start(jnp.concatenate([_pack_rows(shards[n].astype(BF)) for n in GATHER_MLP], axis=0), "mlp",
                                after=(mix_flight.token,))
    w_qkv, w_gates = W["w_in"][:, :QKV_WIDTH], W["w_in"][:, QKV_WIDTH:]

    xs, ps, tgt = x[0], p[0, 0], loss_target[0]
    T = xs.shape[0]
    TM = 512
    gm, gl, gp, gf = g_mix, g_mlp, g_ple, g_final.reshape(1, D_MODEL)
    cos2, sin_signed = _rope_tables(positions[0])
    tab = _na_bias_table(rpb[0])

    a = _rowwise("norm_mix", lambda h, g: h * _rms(h) * g, T, TM, [_row(xs, TM), _full(gm)], [(D_MODEL, BF)],
                 after=(rest_flight.token,))
    n3 = 3 * NA_WIDTH
    qkv = _mm("in_na", a, w_qkv[:, :n3], "nn", 1024, 768, 1024, [BF])
    z_dil = _mm("in_dil", a, w_qkv[:, n3:], "nn", 1024, 1152, 1024, [F32])
    z_gates = _mm("in_gates", a, w_gates, "nn", 1024,1024, 1024, [BF])

    dil_ops = _qkv_prep(z_dil, cos2, sin_signed, TM)
    y_na = _na_fwd(qkv, tab)
    band = [_band_fwd(*dil_ops[g], g) for g in range(len(DIL_GROUPS))]
    y_dil, w_grp, o_nat = _dil_merge_fwd([b[0] for b in band], [b[1] for b in band], T, TM)

    W.update(_unpack_weights(_gather_wait(mix_flight, y_dil, "mix"), GATHER_MIX))
    u_na = _mm("branch_na", y_na, W["w_branch_na"], "nn", 1024,1024, 512, [BF])
    u_dil = _mm("branch_dil", y_dil, W["w_branch_dil"], "nn", 1024,1024, 256, [BF])
    mixed = _rowwise(
        "gate_mix", lambda gn, gd, un, ud: _sigmoid(gn.astype(F32)) * un.astype(F32) + _sigmoid(gd.astype(F32)) * ud.astype(F32), T, TM,
        [_row(z_gates, TM, 0, D_MODEL), _row(z_gates, TM, 1, D_MODEL), _row(u_na, TM), _row(u_dil, TM)], [(D_MODEL, BF)])
    def add_norm(d, h, g):
        h = h + d
        return h, h * _rms(h) * g

    h1, cn = _mm("out_proj", mixed, W["w_out"], "nn", 512, 1024, 1024, [F32, BF], epilogue=add_norm, extras=(xs,), consts=(gl,))
    W.update(_unpack_weights(_gather_wait(rest_flight, cn, "mlp"), GATHER_MLP))
    up, act = _mm("mlp_up", cn, W["w_up"], "nn", 1024,1024, 1024, [BF, BF],
                  epilogue=lambda acc: (acc, jnp.square(jnp.maximum(acc, 0.0))))
    h2, en = _mm("mlp_down", act, W["w_down"], "nn", 512, 1024, 1024, [F32, BF], epilogue=add_norm, extras=(h1,), consts=(gp,))
    gt = _mm("ple_gate", en, W["w_ple_gate"], "nn", 1024,1024, 1024, [F32])
    pp = _mm("ple_proj", ps, W["w_ple_proj"], "nn", 1024,1024, 256, [F32])

    def head(h2t, gtt, ppt, tg, g):
        sg = _sigmoid(gtt)
        h3 = h2t + sg * ppt
        yo = h3 * _rms(h3) * g
        diff = yo - tg
        loss = 0.5 * jnp.sum(jnp.mean(jnp.square(diff), axis=-1, keepdims=True), axis=0, keepdims=True)
        dh3, dg = _rms_bwd(diff * (1.0 / D_MODEL), h3, g)
        return dh3, dh3 * ppt * sg * (1.0 - sg), dh3 * sg, jnp.broadcast_to(loss, (1, 128)), dg

    dh3, d_gt, d_pp, loss_part, dg_final = _rowwise(
        "loss_head", head, T, TM, [_row(h2, TM), _row(gt, TM), _row(pp, TM), _row(tgt, TM), _full(gf)],
        [(D_MODEL, F32), (D_MODEL, BF), (D_MODEL, BF)], sums=[128, D_MODEL])

    grads = {}
    early_shapes = {n: shards[n].shape for n in REDUCE_EARLY}
    early_rows = sum(r * c for r, c in early_shapes.values()) // PACK_W
    shard_rows = D_MODEL // N_CHIPS
    early_buf = _mm("g_ple_gate", en, d_gt, "tn", shard_rows, 1024, 1024, [F32],
                    into=(jax.ShapeDtypeStruct((N_CHIPS, early_rows, PACK_W), F32), (None, shard_rows, PACK_W),
                          lambda i, j: (i, 2 * D_MODEL // shard_rows, 0)))
    g_ple_proj = _mm("g_ple_proj", ps, d_pp, "tn", 256, 1024, 1024,[F32])
    early_buf = lax.dynamic_update_slice(
        early_buf, jnp.stack([_pack_rows(g_ple_proj[:, s * shard_rows:(s + 1) * shard_rows]) for s in range(N_CHIPS)]),
        (0, 2 * D_MODEL + shard_rows, 0))
    def add_norm_bwd(dn, dh_out, h, g):
        dh, dg = _rms_bwd(dn, h, g)
        dh = dh_out + dh
        return dh, dh, dg

    dh2, dh2_b, dg_ple = _mm("d_ple_gate", d_gt, W["w_ple_gate"], "nt", 512, 1024, 1024, [F32, BF],
                             epilogue=add_norm_bwd, extras=(dh3, h2), consts=(gp,), sums=[D_MODEL])
    d_up = _mm("d_mlp_down", dh2_b, W["w_down"], "nt", 1024,1024, 1024, [BF],
               epilogue=lambda acc, u: (acc * (2.0 * jnp.maximum(u.astype(F32), 0.0)),), extras=(up,))
    early_buf = _mm("g_mlp_down", act, dh2_b, "tn", 1024, 1024, 1024,[F32],
                    into=(early_buf, (None, D_MODEL, PACK_W), lambda i, j: (i, 1, 0)))
    early_buf = _mm("g_mlp_up", cn, d_up, "tn", 1024, 1024, 1024,[F32],
                    into=(early_buf, (None, D_MODEL, PACK_W), lambda i, j: (j, 0, 0)))
    early_tm = early_rows // 4
    swap_flight = _swap_start(early_buf, "early")
    dh1, dh1_b, dg_mlp = _mm("d_mlp_up", d_up, W["w_up"], "nt", 512, 1024, 1024, [F32, BF], epilogue=add_norm_bwd,
                             extras=(dh2, h1), consts=(gl,), sums=[D_MODEL], after=(swap_flight.token,))
    early_g, early_got = _swap_wait(swap_flight, dh1_b, "early")
    early_pair, early_pair_b = _pair_sum(early_g, early_got, early_tm)
    scatter_flight = _scatter_start(early_pair_b, "early")
    d_mixed = _mm("d_out_proj", dh1_b, W["w_out"], "nt", 1024,1024, 1024, [F32], after=(scatter_flight.token,))
    grads["w_out"] = _mm("g_out_proj", mixed, dh1_b, "tn", 1024, 1024, 1024,[F32])

    def gate_bwd(dm, gn, gd, un, ud):
        gn, gd, un, ud = (t.astype(F32) for t in (gn, gd, un, ud))
        sn, sd = _sigmoid(gn), _sigmoid(gd)
        return jnp.concatenate([dm * un * sn * (1.0 - sn), dm * ud * sd * (1.0 - sd)], axis=1), dm * sn, dm * sd

    dz_gates, d_u_na, d_u_dil = _rowwise(
        "gate_mix_bwd", gate_bwd, T, TM,
        [_row(d_mixed, TM), _row(z_gates, TM, 0, D_MODEL), _row(z_gates, TM, 1, D_MODEL), _row(u_na, TM), _row(u_dil, TM)],
        [(2 * D_MODEL, BF), (D_MODEL, BF), (D_MODEL, BF)])
    grads["w_branch_na"] = _mm("g_branch_na", y_na, d_u_na, "tn", 1024, 1024, 1024,[F32])
    grads["w_branch_dil"] = _mm("g_branch_dil", y_dil, d_u_dil, "tn", 256, 1024, 1024,[F32])
    d_y_na = _mm("d_branch_na", d_u_na, W["w_branch_na"], "nt", 1024,512, 1024, [BF])
    d_y_dil = _mm("d_branch_dil", d_u_dil, W["w_branch_dil"], "nt", 1024,256, 1024, [F32])

    dqa, dka, dva, dtab = _na_bwd(qkv, tab, d_y_na)
    d_rpb = _na_rpb_grad(dtab)[:, :2 * NA_WIN_ROWS - 1, :2 * NA_WIN_COLS - 1]

    do_res, dlse_res = _dil_merge_bwd(d_y_dil, o_nat, w_grp, TM)
    d_dil = [_band_bwd(*dil_ops[g], do_res[g], dlse_res[g], g) for g in range(len(DIL_GROUPS))]

    me_chip = 2 * lax.axis_index("x") + lax.axis_index("y")
    early_mine = _chip_sum(lax.dynamic_index_in_dim(early_pair, me_chip, 0, keepdims=False),
                           _scatter_wait(scatter_flight, d_dil[-1][-1], "early"), early_tm)
    join_flight = _join_start(early_mine)
    dz_qkv = _qkv_unprep((dqa, dka, dva), d_dil, cos2, sin_signed, TM, after=(join_flight.token,))
    grads["w_in"] = jnp.concatenate([
        _mm("g_in_qkv", a, dz_qkv, "tn", 1024, 1280, 1024,[F32]),
        _mm("g_in_gates", a, dz_gates, "tn", 1024, 1024, 1024,[F32])], axis=1)
    late_shapes = {n: shards[n].shape for n in REDUCE_LATE}
    late_tm = sum(r * c for r, c in late_shapes.values()) // PACK_W // 4
    late_swap = _swap_start(_pack_grads(grads, REDUCE_LATE), "late")
    d_a = _mm("d_in_qkv", dz_qkv, w_qkv, "nt", 1024,1024, 1280, [F32], after=(late_swap.token,))
    late_g, late_got = _swap_wait(late_swap, d_a, "late")
    late_pair, late_pair_b = _pair_sum(late_g, late_got, late_tm)
    late_scatter = _scatter_start(late_pair_b, "late")
    def first_bwd(dn_gates, dn_qkv, dh_out, h, g):
        dh, dg = _rms_bwd(dn_gates + dn_qkv, h, g)
        return dh_out + dh, dg

    grad_x, dg_mix = _mm("d_in_gates", dz_gates, w_gates, "nt", 512, 1024, 1024, [F32], epilogue=first_bwd,
                         extras=(d_a, dh1, xs), consts=(gm,), sums=[D_MODEL], after=(late_scatter.token,))
    g_shard = _unpack_shard(_join_wait(join_flight, grad_x), early_shapes, REDUCE_EARLY)

    n_rpb = rpb.size
    rpb_rows = 4
    small = jnp.concatenate([
        dg_mix, dg_mlp, dg_ple, dg_final,
        jnp.pad(d_rpb.reshape(-1), (0, rpb_rows * D_MODEL - n_rpb)).reshape(rpb_rows, D_MODEL),
        jnp.pad(loss_part, ((0, 0), (0, D_MODEL - loss_part.shape[1]))),
        jnp.zeros((SMALL_ROWS - 5 - rpb_rows, D_MODEL), F32)], axis=0)
    small = _allreduce_small(small)
    loss = small[4 + rpb_rows, 0]

    def small_pack(a0, a1, a2, a3, r):
        return jnp.concatenate([a0.reshape(1, -1), a1.reshape(1, -1), a2.reshape(1, -1), a3.reshape(1, -1),
                                jnp.pad(r.reshape(-1), (0, rpb_rows * D_MODEL - n_rpb)).reshape(rpb_rows, D_MODEL)], axis=0)

    g_small = small[:4 + rpb_rows]
    small_res = _adamw("adamw_small", g_small, small_pack(g_mix, g_mlp, g_ple, g_final, rpb),
                       small_pack(m_g_mix, m_g_mlp, m_g_ple, m_g_final, m_rpb), small_pack(v_g_mix, v_g_mlp, v_g_ple, v_g_final, v_rpb))

    def small_unpack(t):
        return {"g_mix": t[0].reshape(g_mix.shape), "g_mlp": t[1].reshape(g_mlp.shape), "g_ple": t[2].reshape(g_ple.shape),
                "g_final": t[3].reshape(g_final.shape), "rpb": t[4:].reshape(-1)[:n_rpb].reshape(rpb.shape)}

    out = {"grad": small_unpack(g_small)}
    for kind, t in zip(("delta", "new_m", "new_v"), small_res, strict=True):
        out[kind] = small_unpack(t)
    def update(names):
        for n in names:
            out["grad"][n] = g_shard[n][None]
            res = _adamw("adamw_" + n, g_shard[n], shards[n], m_shards[n], v_shards[n])
            for kind, t in zip(("delta", "new_m", "new_v"), res, strict=True):
                out[kind][n] = t[None]

    update(REDUCE_EARLY)
    late_others = _scatter_wait(late_scatter, out["new_v"][REDUCE_EARLY[-1]], "late")
    late_mine = _chip_sum(lax.dynamic_index_in_dim(late_pair, me_chip, 0, keepdims=False), late_others, late_tm)
    g_shard.update(_unpack_shard(_join_halves(late_mine), late_shapes, REDUCE_LATE))
    update(REDUCE_LATE)

    order = ["g_mix", "w_in", "rpb", "w_branch_na", "w_branch_dil", "w_out", "g_mlp", "w_up", "w_down", "g_ple",
             "w_ple_gate", "w_ple_proj", "g_final"]
    return (loss, grad_x[None], *[out["grad"][n] for n in order], *[out["delta"][n] for n in order],
            *[out["new_m"][n] for n in order], *[out["new_v"][n] for n in order])
```

```python
import functools
from typing import NamedTuple

import numpy as np
import jax
import jax.numpy as jnp
from jax import lax
from jax.experimental import pallas as pl
from jax.experimental.pallas import tpu as pltpu

BF = jnp.bfloat16
F32 = jnp.float32
MESH = pl.DeviceIdType.MESH
ANY = pl.BlockSpec(memory_space=pl.ANY)

V7X_VMEM_BYTES = 64 * 1024 * 1024
VMEM_LIMIT = V7X_VMEM_BYTES - 16 * 1024 * 1024

D_MODEL = 1024
HEAD_DIM = 64
GRID_W = 64
NA_HEADS = 8
NA_WIN_ROWS = 8
NA_WIN_COLS = 16
NA_WIDTH = NA_HEADS * HEAD_DIM
DIL_GROUPS = ((128, 1), (512, 4), (2048, 16))
DIL_HPG = 4
DIL_HEADS = DIL_HPG * len(DIL_GROUPS)
DIL_WIDTH = DIL_HEADS * HEAD_DIM
DIL_OUT_WIDTH = DIL_HPG * HEAD_DIM
DIL_RADIUS = 64
QKV_WIDTH = 3 * NA_WIDTH + 3 * DIL_WIDTH
D_FF = 4 * D_MODEL
PLE_DIM = 256
ROPE_THETA = 10000.0
RMS_EPS = 1e-6
NEG_INF = -1e30
Q_SCALE = HEAD_DIM ** -0.5

ADAM_LR = 0.001
ADAM_B1 = 0.9
ADAM_B2 = 0.999
ADAM_EPS = 1e-08
ADAM_WD = 0.01
ADAM_STEP = 10

N_CHIPS = 4
N_DEV = 8
PACK_W = 1024
BIG = ("w_in", "w_branch_na", "w_branch_dil", "w_out", "w_up", "w_down", "w_ple_gate", "w_ple_proj")
GATHER_FIRST = ("w_in",)
GATHER_MIX = ("w_branch_na", "w_branch_dil", "w_out")
GATHER_MLP = ("w_up", "w_down", "w_ple_gate", "w_ple_proj")
REDUCE_EARLY = ("w_up", "w_down", "w_ple_gate", "w_ple_proj")
REDUCE_LATE = ("w_in", "w_branch_na", "w_branch_dil", "w_out")
SMALL_ROWS = 16


def _cparams(sem=None):
    return pltpu.CompilerParams(dimension_semantics=sem, vmem_limit_bytes=VMEM_LIMIT)


def _mm(name, a, b, mode, tm, tn, tk, out_dtypes, epilogue=None, extras=(), consts=(), sums=(), after=(), into=None):
    if mode == "nn":
        (M, K), N = a.shape, b.shape[1]
    elif mode == "nt":
        (M, K), N = a.shape, b.shape[0]
    else:
        (K, M), N = a.shape, b.shape[1]
    tm, tn, tk = min(tm, M), min(tn, N), min(tk, K)
    assert M % tm == 0 and N % tn == 0 and K % tk == 0, (name, M, N, K, tm, tn, tk)
    if mode == "nn":
        a_spec = pl.BlockSpec((tm, tk), lambda i, j, k: (i, k))
        b_spec = pl.BlockSpec((tk, tn), lambda i, j, k: (k, j))
        dims = (((1,), (0,)), ((), ()))
    elif mode == "nt":
        a_spec = pl.BlockSpec((tm, tk), lambda i, j, k: (i, k))
        b_spec = pl.BlockSpec((tn, tk), lambda i, j, k: (j, k))
        dims = (((1,), (1,)), ((), ()))
    else:
        a_spec = pl.BlockSpec((tk, tm), lambda i, j, k: (k, i))
        b_spec = pl.BlockSpec((tk, tn), lambda i, j, k: (k, j))
        dims = (((0,), (0,)), ((), ()))
    nk = K // tk
    n_extra, n_const, n_out, n_sum = len(extras), len(consts), len(out_dtypes), len(sums)
    tile = pl.BlockSpec((tm, tn), lambda i, j, k: (i, j))
    assert not sums or tn == N, "row sums need whole rows in a tile"

    n_after = len(after)

    def body(a_ref, b_ref, *rest):
        extra_refs, rest = rest[:n_extra + n_const], rest[n_extra + n_const + n_after:]
        out_refs, sum_refs, acc = rest[:n_out], rest[n_out:n_out + n_sum], rest[-1]
        i, k = pl.program_id(0), pl.program_id(2)

        @pl.when(k == 0)
        def _():
            acc[...] = jnp.zeros_like(acc)

        acc[...] += lax.dot_general(a_ref[...].astype(BF), b_ref[...].astype(BF), dims, preferred_element_type=F32)

        @pl.when(k == nk - 1)
        def _():
            outs = (acc[...],) if epilogue is None else epilogue(acc[...], *[e[...] for e in extra_refs])
            for o_ref, val in zip(out_refs, outs[:n_out], strict=True):
                o_ref[...] = val.astype(o_ref.dtype)
            for s_ref, val in zip(sum_refs, outs[n_out:], strict=True):
                @pl.when(i == 0)
                def _():
                    s_ref[...] = val

                @pl.when(i != 0)
                def _():
                    s_ref[...] += val

    out_specs = [tile] * n_out + [pl.BlockSpec((1, c), lambda i, j, k: (0, 0)) for c in sums]
    out_shape = [jax.ShapeDtypeStruct((M, N), dt) for dt in out_dtypes] + [jax.ShapeDtypeStruct((1, c), F32) for c in sums]
    operands, aliases = [a, b, *extras, *consts, *after], {}
    in_specs = ([a_spec, b_spec] + [tile] * n_extra
                + [pl.BlockSpec(c.shape, functools.partial(lambda nd, i, j, k: (0,) * nd, c.ndim)) for c in consts] + [ANY] * n_after)
    if into is not None:
        assert n_out == 1
        target, block, index = into
        out_specs = [pl.BlockSpec(block, lambda i, j, k: index(i, j))]
        out_shape = [jax.ShapeDtypeStruct(target.shape, target.dtype)]
        if not isinstance(target, jax.ShapeDtypeStruct):
            aliases = {len(operands): 0}
            operands.append(target)
            in_specs.append(ANY)
            n_after += 1

    outs = pl.pallas_call(
        body, name=name, grid=(M // tm, N // tn, nk),
        in_specs=in_specs, out_specs=out_specs, out_shape=out_shape,
        scratch_shapes=[pltpu.VMEM((tm, tn), F32)], input_output_aliases=aliases,
        compiler_params=_cparams(("arbitrary",) * 3 if sums else ("parallel", "parallel", "arbitrary")),
    )(*operands)
    return outs[0] if len(outs) == 1 else outs


def _row(arr, tm, col_block=None, width=None):
    width = arr.shape[1] if width is None else width
    cb = 0 if col_block is None else col_block
    return arr, pl.BlockSpec((tm, width), lambda i: (i, cb))


def _full(arr):
    nd = arr.ndim
    return arr, pl.BlockSpec(arr.shape, lambda i: (0,) * nd)


def _rowwise(name, body, T, tm, ins, outs, sums=(), after=()):
    n_in, n_out, n_sum, n_after = len(ins), len(outs), len(sums), len(after)

    def kern(*refs):
        in_refs, refs = refs[:n_in], refs[n_in + n_after:]
        out_refs, sum_refs = refs[:n_out], refs[n_out:]
        res = body(*[r[...] for r in in_refs])
        res = res if isinstance(res, tuple) else (res,)
        for o_ref, val in zip(out_refs, res[:n_out], strict=True):
            o_ref[...] = val.astype(o_ref.dtype)
        if n_sum:
            @pl.when(pl.program_id(0) == 0)
            def _():
                for s_ref in sum_refs:
                    s_ref[...] = jnp.zeros_like(s_ref)

            for s_ref, val in zip(sum_refs, res[n_out:], strict=True):
                s_ref[...] += val

    res = pl.pallas_call(
        kern, name=name, grid=(T // tm,),
        in_specs=[spec for _, spec in ins] + [ANY] * n_after,
        out_specs=[pl.BlockSpec((tm, c), lambda i: (i, 0)) for c, _ in outs]
        + [pl.BlockSpec((1, c), lambda i: (0, 0)) for c in sums],
        out_shape=[jax.ShapeDtypeStruct((T, c), dt) for c, dt in outs]
        + [jax.ShapeDtypeStruct((1, c), F32) for c in sums],
        compiler_params=_cparams(("arbitrary",)),
    )(*[a for a, _ in ins], *after)
    return res[0] if len(res) == 1 else res


def _sigmoid(x):
    return 1.0 / (1.0 + jnp.exp(-x))


def _rms(h):
    return lax.rsqrt(jnp.mean(h * h, axis=-1, keepdims=True) + RMS_EPS)


def _rms_bwd(dy, h, g):
    r = _rms(h)
    n = h * r
    dn = dy * g
    dh = r * (dn - n * jnp.mean(dn * n, axis=-1, keepdims=True))
    return dh, jnp.sum(dy * n, axis=0, keepdims=True)


def _rope(x, cos2, sin_signed):
    lane = lax.broadcasted_iota(jnp.int32, x.shape, 1)
    swapped = jnp.where((lane % HEAD_DIM) < HEAD_DIM // 2, pltpu.roll(x, 128 - HEAD_DIM // 2, 1), pltpu.roll(x, HEAD_DIM // 2, 1))
    return x * cos2 + swapped * sin_signed


def _rope_cols(x, cos2, sin_signed):
    return jnp.concatenate([_rope(x[:, c:c + 128], cos2, sin_signed) for c in range(0, x.shape[1], 128)], axis=1)


NA_KEYS = NA_WIN_ROWS * GRID_W
NA_BASES = 8


def _na_row_geometry(r, rows):
    first = jnp.clip(r - NA_WIN_ROWS // 2, 0, rows - NA_WIN_ROWS)
    base = first - r + (NA_WIN_ROWS - 1)
    return pl.multiple_of(first * GRID_W, GRID_W), base


NA_ROWS_PER_STEP = 8
NA_BWD_ROWS_PER_STEP = 8


def _softmax_rows(s):
    p = jnp.exp(s - jnp.max(s, axis=-1, keepdims=True))
    return p / jnp.sum(p, axis=-1, keepdims=True)


def _na_probs(q, kw, bias):
    return _softmax_rows(lax.dot_general(q, kw, (((1,), (1,)), ((), ())), preferred_element_type=F32) + bias)


def _split_pair(t):
    first = lax.broadcasted_iota(jnp.int32, t.shape, 1) < HEAD_DIM
    zero = jnp.zeros_like(t)
    return jnp.where(first, t, zero), jnp.where(first, zero, t)


def _join_pair(a, b):
    return jnp.where(lax.broadcasted_iota(jnp.int32, a.shape, 1) < HEAD_DIM, a, b)


_NT = (((1,), (1,)), ((), ()))
_TN = (((0,), (0,)), ((), ()))


def _na_fwd(qkv, tab):
    T = qkv.shape[0]
    rows = T // GRID_W
    n_pairs = NA_WIDTH // 128

    def body(q_ref, k_ref, v_ref, tab_ref, y_ref):
        def step(it, carry):
            geo = [_na_row_geometry(it * NA_ROWS_PER_STEP + u, rows) for u in range(NA_ROWS_PER_STEP)]
            q0s = [pl.multiple_of((it * NA_ROWS_PER_STEP + u) * GRID_W, GRID_W) for u in range(NA_ROWS_PER_STEP)]
            ss = [lax.dot_general(jnp.concatenate(_split_pair(q_ref[pl.ds(q0, GRID_W), :] * Q_SCALE), axis=0),
                                  k_ref[pl.ds(k0, NA_KEYS), :], _NT, preferred_element_type=F32)
                  for q0, (k0, _) in zip(q0s, geo)]
            ps = [_softmax_rows(s + jnp.concatenate([tab_ref[0, base], tab_ref[1, base]], axis=0)) for s, (_, base) in zip(ss, geo)]
            ys = [jnp.dot(p.astype(BF), v_ref[pl.ds(k0, NA_KEYS), :], preferred_element_type=F32) for p, (k0, _) in zip(ps, geo)]
            for q0, y2 in zip(q0s, ys):
                y_ref[pl.ds(q0, GRID_W), :] = _join_pair(y2[:GRID_W], y2[GRID_W:]).astype(y_ref.dtype)
            return carry

        lax.fori_loop(0, rows // NA_ROWS_PER_STEP, step, 0)

    def cols(first):
        return pl.BlockSpec((T, 128), lambda j: (0, first + j))

    return pl.pallas_call(
        body, name="na_fwd", grid=(n_pairs,),
        in_specs=[cols(0), cols(n_pairs), cols(2 * n_pairs), pl.BlockSpec((2, NA_BASES, GRID_W, NA_KEYS), lambda j: (j, 0, 0, 0))],
        out_specs=cols(0), out_shape=jax.ShapeDtypeStruct((T, NA_WIDTH), BF),
        compiler_params=_cparams(("parallel",)),
    )(qkv, qkv, qkv, tab)


def _na_bwd(qkv, tab, do):
    T = qkv.shape[0]
    rows = T // GRID_W
    n_pairs = NA_WIDTH // 128

    def body(q_ref, k_ref, v_ref, tab_ref, do_ref, dq_ref, dk_ref, dv_ref, dtab_ref):
        dk_ref[...] = jnp.zeros_like(dk_ref)
        dv_ref[...] = jnp.zeros_like(dv_ref)
        dtab_ref[...] = jnp.zeros_like(dtab_ref)

        def step(it, carry):
            U = NA_BWD_ROWS_PER_STEP
            geo = [_na_row_geometry(it * U + u, rows) for u in range(U)]
            q0s = [pl.multiple_of((it * U + u) * GRID_W, GRID_W) for u in range(U)]
            q2s = [jnp.concatenate(_split_pair(q_ref[pl.ds(q0, GRID_W), :] * Q_SCALE), axis=0) for q0 in q0s]
            do2s = [jnp.concatenate(_split_pair(do_ref[pl.ds(q0, GRID_W), :]), axis=0) for q0 in q0s]
            ss = [lax.dot_general(q2, k_ref[pl.ds(k0, NA_KEYS), :], _NT, preferred_element_type=F32) for q2, (k0, _) in zip(q2s, geo)]
            dps = [lax.dot_general(do2, v_ref[pl.ds(k0, NA_KEYS), :], _NT, preferred_element_type=F32) for do2, (k0, _) in zip(do2s, geo)]
            ps = [_softmax_rows(s + jnp.concatenate([tab_ref[0, base], tab_ref[1, base]], axis=0)) for s, (_, base) in zip(ss, geo)]
            dss = [p * (dp - jnp.sum(dp * p, axis=-1, keepdims=True)) for p, dp in zip(ps, dps)]
            dvs = [lax.dot_general(p.astype(BF), do2, _TN, preferred_element_type=F32) for p, do2 in zip(ps, do2s)]
            dsbs = [ds.astype(BF) for ds in dss]
            dqs = [jnp.dot(dsb, k_ref[pl.ds(k0, NA_KEYS), :], preferred_element_type=F32) for dsb, (k0, _) in zip(dsbs, geo)]
            dks = [lax.dot_general(dsb, q2, _TN, preferred_element_type=F32) for dsb, q2 in zip(dsbs, q2s)]
            for u in range(U):
                k0, base = geo[u]
                dtab_ref[0, base] += dss[u][:GRID_W]
                dtab_ref[1, base] += dss[u][GRID_W:]
                dq_ref[pl.ds(q0s[u], GRID_W), :] = _join_pair(dqs[u][:GRID_W], dqs[u][GRID_W:])
                dk_ref[pl.ds(k0, NA_KEYS), :] += dks[u]
                dv_ref[pl.ds(k0, NA_KEYS), :] += dvs[u]
            return carry

        lax.fori_loop(0, rows // NA_BWD_ROWS_PER_STEP, step, 0)

    def cols(first):
        return pl.BlockSpec((T, 128), lambda j: (0, first + j))

    tabs = pl.BlockSpec((2, NA_BASES, GRID_W, NA_KEYS), lambda j: (j, 0, 0, 0))
    wide = jax.ShapeDtypeStruct((T, NA_WIDTH), F32)
    return pl.pallas_call(
        body, name="na_bwd", grid=(n_pairs,),
        in_specs=[cols(0), cols(n_pairs), cols(2 * n_pairs), tabs, cols(0)],
        out_specs=[cols(0), cols(0), cols(0), tabs],
        out_shape=[wide, wide, wide, jax.ShapeDtypeStruct((NA_HEADS, NA_BASES, GRID_W, NA_KEYS), F32)],
        compiler_params=_cparams(("parallel",)),
    )(qkv, qkv, qkv, tab, do)


def _na_bias_table(rpb):
    H, n_rows, n_cols = rpb.shape

    def body(r_ref, tab_ref):
        q = lax.broadcasted_iota(jnp.int32, (GRID_W, 128), 0)
        kc = lax.broadcasted_iota(jnp.int32, (GRID_W, 128), 1)
        first = jnp.clip(q - NA_WIN_COLS // 2, 0, GRID_W - NA_WIN_COLS)
        valid = (kc >= first) & (kc < first + NA_WIN_COLS)
        toeplitz = []
        for ro in range(n_rows):
            row = jnp.broadcast_to(r_ref[pl.ds(ro, 1), :], (GRID_W, 128))
            shifted = pltpu.roll(pltpu.roll(row, 128 - (NA_WIN_COLS - 1), 1), 0, 1, stride=1, stride_axis=0)
            toeplitz.append(jnp.where(valid, shifted, NEG_INF))
        for base in range(NA_BASES):
            for j in range(NA_WIN_ROWS // 2):
                even, odd = toeplitz[base + 2 * j], toeplitz[base + 2 * j + 1]
                tab_ref[base, :, pl.ds(j * 128, 128)] = jnp.where(kc < GRID_W, even, pltpu.roll(odd, GRID_W, 1))

    padded = jnp.pad(rpb, ((0, 0), (0, 16 - n_rows), (0, 128 - n_cols)))
    return pl.pallas_call(
        body, name="na_bias_table", grid=(H,),
        in_specs=[pl.BlockSpec((None, 16, 128), lambda h: (h, 0, 0))],
        out_specs=pl.BlockSpec((None, NA_BASES, GRID_W, NA_KEYS), lambda h: (h, 0, 0, 0)),
        out_shape=jax.ShapeDtypeStruct((H, NA_BASES, GRID_W, NA_KEYS), F32),
        compiler_params=_cparams(("parallel",)),
    )(padded)


def _na_rpb_grad(dtab):
    H = dtab.shape[0]
    n_rows = 2 * NA_WIN_ROWS - 1
    n_cols = 2 * NA_WIN_COLS - 1

    def body(d_ref, o_ref):
        lane = lax.broadcasted_iota(jnp.int32, (GRID_W, 128), 1)
        low = lane < GRID_W
        out_rows = []
        for ro in range(n_rows):
            acc = jnp.zeros((GRID_W, 128), F32)
            for base in range(NA_BASES):
                i = ro - base
                if not 0 <= i < NA_WIN_ROWS:
                    continue
                pair = d_ref[base, :, pl.ds((i // 2) * 128, 128)]
                if i % 2:
                    pair = pltpu.roll(pair, GRID_W, 1)
                acc = acc + jnp.where(low, pair, 0.0)
            skew = pltpu.roll(acc, 0, 1, stride=1, stride_axis=0)
            diag = jnp.sum(skew, axis=0, keepdims=True)
            out_rows.append(pltpu.roll(jnp.broadcast_to(diag, (8, 128)), 128 - (GRID_W - NA_WIN_COLS), 1)[:1])
        out_rows.append(jnp.zeros((1, 128), F32))
        res = jnp.concatenate(out_rows, axis=0)
        o_ref[...] = jnp.where(lax.broadcasted_iota(jnp.int32, res.shape, 1) < n_cols, res, 0.0)

    return pl.pallas_call(
        body, name="na_rpb_grad", grid=(H,),
        in_specs=[pl.BlockSpec((None, NA_BASES, GRID_W, NA_KEYS), lambda h: (h, 0, 0, 0))],
        out_specs=pl.BlockSpec((None, n_rows + 1, 128), lambda h: (h, 0, 0)),
        out_shape=jax.ShapeDtypeStruct((H, n_rows + 1, 128), F32),
        compiler_params=_cparams(("parallel",)),
    )(jnp.flip(dtab, axis=2))


BAND_Q = 128
BAND_KEYS = BAND_Q + 2 * DIL_RADIUS


def _band_geometry(n, L):
    q0 = pl.multiple_of(n * BAND_Q, BAND_Q)
    k0 = pl.multiple_of(jnp.clip(q0 - DIL_RADIUS, 0, L - BAND_KEYS), DIL_RADIUS)
    qi = q0 + lax.broadcasted_iota(jnp.int32, (BAND_Q, BAND_KEYS), 0)
    kj = k0 + lax.broadcasted_iota(jnp.int32, (BAND_Q, BAND_KEYS), 1)
    return q0, k0, jnp.abs(qi - kj) <= DIL_RADIUS


DIL_PAIRS = DIL_OUT_WIDTH // 128


def _residue_shape(dil, T, dtype):
    return jax.ShapeDtypeStruct((DIL_PAIRS, dil, T // dil, 128), dtype)


def _residue_tile(dil, tm):
    return pl.BlockSpec((DIL_PAIRS, dil, tm // dil, 128), lambda i: (0, 0, i, 0))


def _to_natural(ref, scratch, dil, tm):
    tiles = []
    for pair in range(DIL_PAIRS):
        if dil == 1:
            tiles.append(ref[pair, 0].astype(F32))
            continue
        for r in range(dil):
            scratch[pl.ds(r, tm // dil, stride=dil), :] = ref[pair, r].astype(F32)
        tiles.append(scratch[...])
    return tiles


def _from_natural(tile, scratch, ref, pair, dil, tm):
    if dil == 1:
        ref[pair, 0] = tile.astype(ref.dtype)
        return
    scratch[...] = tile
    for r in range(dil):
        ref[pair, r] = scratch[pl.ds(r, tm // dil, stride=dil), :].astype(ref.dtype)


def _band_specs(group, T):
    dil = DIL_GROUPS[group][1]
    L = T // dil
    assert L % BAND_Q == 0 and L >= BAND_KEYS, (T, dil)
    return L, (dil * DIL_PAIRS,), pl.BlockSpec((None, None, L, 128), lambda s: (s % DIL_PAIRS, s // DIL_PAIRS, 0, 0))


BAND_BLOCKS_PER_STEP = 4


def _band_softmax(s, valid):
    s = jnp.where(valid, s, NEG_INF)
    m = jnp.max(s, axis=-1, keepdims=True)
    p = jnp.exp(s - m)
    l = jnp.sum(p, axis=-1, keepdims=True)
    return p / l, m + jnp.log(l)


def _band_fwd(q, k, v, group):
    T = q.shape[1] * q.shape[2]
    L, grid, spec = _band_specs(group, T)
    U = min(BAND_BLOCKS_PER_STEP, L // BAND_Q)

    def body(q_ref, k_ref, v_ref, o_ref, lse_ref):
        def step(it, carry):
            geo = [_band_geometry(it * U + u, L) for u in range(U)]
            ss = [lax.dot_general(jnp.concatenate(_split_pair(q_ref[pl.ds(q0, BAND_Q), :]), axis=0),
                                  k_ref[pl.ds(k0, BAND_KEYS), :], _NT, preferred_element_type=F32) for q0, k0, _ in geo]
            pls = [_band_softmax(s, jnp.concatenate([valid, valid], axis=0)) for s, (_, _, valid) in zip(ss, geo)]
            os = [jnp.dot(p.astype(BF), v_ref[pl.ds(k0, BAND_KEYS), :], preferred_element_type=F32) for (p, _), (_, k0, _) in zip(pls, geo)]
            for (q0, _, _), o2, (_, lse) in zip(geo, os, pls):
                o_ref[pl.ds(q0, BAND_Q), :] = _join_pair(o2[:BAND_Q], o2[BAND_Q:])
                lse2 = jnp.broadcast_to(lse, (2 * BAND_Q, 128))
                lse_ref[pl.ds(q0, BAND_Q), :] = _join_pair(lse2[:BAND_Q], lse2[BAND_Q:])
            return carry

        lax.fori_loop(0, L // (BAND_Q * U), step, 0)

    res = _residue_shape(DIL_GROUPS[group][1], T, F32)
    return pl.pallas_call(
        body, name=f"band_fwd_g{group}", grid=grid,
        in_specs=[spec] * 3, out_specs=[spec] * 2, out_shape=[res, res],
        compiler_params=_cparams(("parallel",)),
    )(q, k, v)


def _band_bwd(q, k, v, do, dlse, group):
    T = q.shape[1] * q.shape[2]
    L, grid, spec = _band_specs(group, T)
    U = min(BAND_BLOCKS_PER_STEP, L // BAND_Q)

    def body(q_ref, k_ref, v_ref, do_ref, dlse_ref, dq_ref, dk_ref, dv_ref):
        dk_ref[...] = jnp.zeros_like(dk_ref)
        dv_ref[...] = jnp.zeros_like(dv_ref)

        def step(it, carry):
            geo = [_band_geometry(it * U + u, L) for u in range(U)]
            q2s = [jnp.concatenate(_split_pair(q_ref[pl.ds(q0, BAND_Q), :]), axis=0) for q0, _, _ in geo]
            do2s = [jnp.concatenate(_split_pair(do_ref[pl.ds(q0, BAND_Q), :]), axis=0) for q0, _, _ in geo]
            ss = [lax.dot_general(q2, k_ref[pl.ds(k0, BAND_KEYS), :], _NT, preferred_element_type=F32) for q2, (_, k0, _) in zip(q2s, geo)]
            dps = [lax.dot_general(do2, v_ref[pl.ds(k0, BAND_KEYS), :], _NT, preferred_element_type=F32) for do2, (_, k0, _) in zip(do2s, geo)]
            ps = [_band_softmax(s, jnp.concatenate([valid, valid], axis=0))[0] for s, (_, _, valid) in zip(ss, geo)]
            dss = []
            for p, dp, (q0, _, _) in zip(ps, dps, geo):
                dl = dlse_ref[pl.ds(q0, BAND_Q), :]
                dl2 = jnp.concatenate([dl[:, :1], dl[:, HEAD_DIM:HEAD_DIM + 1]], axis=0)
                dss.append(p * (dp - jnp.sum(dp * p, axis=-1, keepdims=True) + dl2))
            dvs = [lax.dot_general(p.astype(BF), do2, _TN, preferred_element_type=F32) for p, do2 in zip(ps, do2s)]
            dsbs = [ds.astype(BF) for ds in dss]
            dqs = [jnp.dot(dsb, k_ref[pl.ds(k0, BAND_KEYS), :], preferred_element_type=F32) for dsb, (_, k0, _) in zip(dsbs, geo)]
            dks = [lax.dot_general(dsb, q2, _TN, preferred_element_type=F32) for dsb, q2 in zip(dsbs, q2s)]
            for u, (q0, k0, _) in enumerate(geo):
                dq_ref[pl.ds(q0, BAND_Q), :] = _join_pair(dqs[u][:BAND_Q], dqs[u][BAND_Q:])
                dk_ref[pl.ds(k0, BAND_KEYS), :] += dks[u]
                dv_ref[pl.ds(k0, BAND_KEYS), :] += dvs[u]
            return carry

        lax.fori_loop(0, L // (BAND_Q * U), step, 0)

    res = _residue_shape(DIL_GROUPS[group][1], T, F32)
    return pl.pallas_call(
        body, name=f"band_bwd_g{group}", grid=grid,
        in_specs=[spec] * 5, out_specs=[spec] * 3, out_shape=[res] * 3,
        compiler_params=_cparams(("parallel",)),
    )(q, k, v, do, dlse)


def _head_sums(t):
    head = lax.broadcasted_iota(jnp.int32, t.shape, 1) // HEAD_DIM
    out = jnp.zeros_like(t)
    for h in range(t.shape[1] // HEAD_DIM):
        mine = head == h
        out = jnp.where(mine, jnp.sum(jnp.where(mine, t, 0.0), axis=-1, keepdims=True), out)
    return out


def _dil_merge_fwd(os, lses, T, tm):
    G = len(DIL_GROUPS)
    W = DIL_OUT_WIDTH
    dils = [d for _, d in DIL_GROUPS]

    def body(*refs):
        o_refs, lse_refs = refs[:G], refs[G:2 * G]
        y_ref, w_refs, on_refs, scratch = refs[2 * G], refs[2 * G + 1:3 * G + 1], refs[3 * G + 1:4 * G + 1], refs[-1]
        o = [jnp.concatenate(_to_natural(r, scratch, d, tm), axis=1) for r, d in zip(o_refs, dils)]
        ls = [jnp.concatenate(_to_natural(r, scratch, d, tm), axis=1) for r, d in zip(lse_refs, dils)]
        m = functools.reduce(jnp.maximum, ls)
        es = [jnp.exp(l - m) for l in ls]
        tot = functools.reduce(jnp.add, es)
        ws = [e / tot for e in es]
        y_ref[...] = functools.reduce(jnp.add, [w * t for w, t in zip(ws, o)]).astype(y_ref.dtype)
        for g in range(G):
            w_refs[g][...] = ws[g]
            on_refs[g][...] = o[g]

    nat = pl.BlockSpec((tm, W), lambda i: (i, 0))
    res = pl.pallas_call(
        body, name="dil_merge_fwd", grid=(T // tm,),
        in_specs=[_residue_tile(d, tm) for d in dils] * 2,
        out_specs=[nat] * (2 * G + 1),
        out_shape=[jax.ShapeDtypeStruct((T, W), BF)] + [jax.ShapeDtypeStruct((T, W), F32)] * (2 * G),
        scratch_shapes=[pltpu.VMEM((tm, 128), F32)],
        compiler_params=_cparams(("parallel",)),
    )(*os, *lses)
    return res[0], res[1:G + 1], res[G + 1:]


def _dil_merge_bwd(dy, os, ws, tm):
    G = len(DIL_GROUPS)
    T, W = dy.shape
    dils = [d for _, d in DIL_GROUPS]

    def body(*refs):
        dyt = refs[0][...]
        o, w = [r[...] for r in refs[1:G + 1]], [r[...] for r in refs[G + 1:2 * G + 1]]
        do_refs, dlse_refs, scratch = refs[2 * G + 1:3 * G + 1], refs[3 * G + 1:4 * G + 1], refs[-1]
        dws = [_head_sums(dyt * t) for t in o]
        mean = functools.reduce(jnp.add, [a * b for a, b in zip(w, dws)])
        for g, d in enumerate(dils):
            do, dlse = w[g] * dyt, w[g] * (dws[g] - mean)
            for pair in range(DIL_PAIRS):
                cols = slice(pair * 128, (pair + 1) * 128)
                _from_natural(do[:, cols], scratch, do_refs[g], pair, d, tm)
                _from_natural(dlse[:, cols], scratch, dlse_refs[g], pair, d, tm)

    nat = pl.BlockSpec((tm, W), lambda i: (i, 0))
    res = pl.pallas_call(
        body, name="dil_merge_bwd", grid=(T // tm,),
        in_specs=[nat] * (2 * G + 1),
        out_specs=[_residue_tile(d, tm) for d in dils] * 2,
        out_shape=[_residue_shape(d, T, BF) for d in dils] + [_residue_shape(d, T, F32) for d in dils],
        scratch_shapes=[pltpu.VMEM((tm, 128), F32)],
        compiler_params=_cparams(("parallel",)),
    )(dy, *os, *ws)
    return res[:G], res[G:]


def _qkv_prep(z, cos2, sin_signed, tm):
    T = z.shape[0]
    G = len(DIL_GROUPS)
    dils = [d for _, d in DIL_GROUPS]
    n_dil_blocks = 3 * DIL_WIDTH // 128

    def body(*refs):
        blocks = refs[:n_dil_blocks]
        cos_ref, sin_ref = refs[n_dil_blocks], refs[1 + n_dil_blocks]
        outs = refs[2 + n_dil_blocks:]
        for part in range(3):
            for g, d in enumerate(dils):
                out = outs[g * 3 + part]
                for pair in range(DIL_PAIRS):
                    blk = blocks[part * (DIL_WIDTH // 128) + g * DIL_PAIRS + pair]
                    for r in range(d):
                        rows = pl.ds(r, tm // d, stride=d) if d > 1 else slice(None)
                        x = blk[rows, :]
                        if part < 2:
                            x = _rope(x, cos_ref[rows, :], sin_ref[rows, :])
                        if part == 0:
                            x = x * Q_SCALE
                        out[pair, r] = x.astype(out.dtype)

    lane_block = [pl.BlockSpec((tm, 128), functools.partial(lambda b, i: (i, b), b)) for b in range(n_dil_blocks)]
    tab = pl.BlockSpec((tm, 128), lambda i: (i, 0))
    res = pl.pallas_call(
        body, name="qkv_prep", grid=(T // tm,),
        in_specs=lane_block + [tab, tab],
        out_specs=[_residue_tile(d, tm) for d in dils for _ in range(3)],
        out_shape=[_residue_shape(d, T, BF) for d in dils for _ in range(3)],
        compiler_params=_cparams(("parallel",)),
    )(*[z] * n_dil_blocks, cos2, sin_signed)
    return [res[3 * g:3 + 3 * g] for g in range(G)]


def _qkv_unprep(d_na, d_dil, cos2, sin_signed, tm, after=()):
    T = d_na[0].shape[0]
    G = len(DIL_GROUPS)
    dils = [d for _, d in DIL_GROUPS]
    n_after = len(after)

    def body(*refs):
        dq, dk, dv = (r[...] for r in refs[:3])
        res_refs = refs[3:3 + 3 * G]
        cs, sn = refs[3 + 3 * G][...], refs[4 + 3 * G][...]
        out, scratch = refs[5 + 3 * G + n_after], refs[-1]
        cols = [dq * Q_SCALE, dk, dv]
        for part in range(3):
            for g, d in enumerate(dils):
                for x in _to_natural(res_refs[g * 3 + part], scratch, d, tm):
                    if part < 2:
                        x = _rope(x, cs, -sn)
                    cols.append(x * Q_SCALE if part == 0 else x)
        out[...] = jnp.concatenate(cols, axis=1).astype(out.dtype)

    wide = pl.BlockSpec((tm, NA_WIDTH), lambda i: (i, 0))
    tab = pl.BlockSpec((tm, 128), lambda i: (i, 0))
    return pl.pallas_call(
        body, name="qkv_unprep", grid=(T // tm,),
        in_specs=[wide] * 3 + [_residue_tile(d, tm) for d in dils for _ in range(3)] + [tab, tab] + [ANY] * n_after,
        out_specs=pl.BlockSpec((tm, QKV_WIDTH), lambda i: (i, 0)),
        out_shape=jax.ShapeDtypeStruct((T, QKV_WIDTH), BF),
        scratch_shapes=[pltpu.VMEM((tm, 128), F32)],
        compiler_params=_cparams(("parallel",)),
    )(*d_na, *[t for g in range(G) for t in d_dil[g]], cos2, sin_signed, *after)


def _rope_tables(positions):
    half = HEAD_DIM // 2
    inv_freq = ROPE_THETA ** (-jnp.arange(half, dtype=F32) / half)
    ang = positions.astype(F32)[:, None] * inv_freq
    cos, sin = jnp.cos(ang), jnp.sin(ang)
    return jnp.tile(jnp.concatenate([cos, cos], axis=1), (1, 2)), jnp.tile(jnp.concatenate([-sin, sin], axis=1), (1, 2))


def _pack_rows(t):
    return t.reshape(-1, PACK_W)


def _me():
    return lax.axis_index("x"), lax.axis_index("y"), lax.axis_index("c")


def _other_chips(x, y):
    return [(1 - x, y), (x, 1 - y), (1 - x, 1 - y)]


def _gather_weights(packed):
    R, W = packed.shape
    half = R // 2

    def body(in_ref, out_ref, send_sems, recv_sems):
        x, y, c = _me()
        sibling = (x, y, 1 - c)
        chips = _other_chips(x, y)

        def block(chip, core):
            return out_ref.at[2 * chip[0] + chip[1], pl.ds(core * half, half), :]

        def copy(k, chip, core, to, src=None):
            return pltpu.make_async_remote_copy(
                src_ref=block(chip, core) if src is None else src, dst_ref=block(chip, core),
                send_sem=send_sems.at[k], recv_sem=recv_sems.at[k], device_id=to, device_id_type=MESH)

        first = [copy(j, (x, y), c, (*chip, c), src=in_ref.at[pl.ds(c * half, half), :]) for j, chip in enumerate(chips)]
        for cp in first:
            cp.start()
        passed = [copy(3 + j, chip, c, sibling) for j, chip in enumerate(chips)]
        for j, chip in enumerate(chips):
            copy(j, chip, c, (x, y, c)).wait_recv()
            passed[j].start()
        for j, chip in enumerate(chips):
            copy(3 + j, chip, 1 - c, (x, y, c)).wait_recv()
        for cp in first + passed:
            cp.wait_send()

    others = pl.pallas_call(
        body, name="gather_weights",
        in_specs=[ANY], out_specs=ANY,
        out_shape=jax.ShapeDtypeStruct((N_CHIPS, R, W), packed.dtype),
        scratch_shapes=[pltpu.SemaphoreType.DMA((6,)), pltpu.SemaphoreType.DMA((6,))],
    )(packed)
    return lax.dynamic_update_slice(others, packed[None], (2 * lax.axis_index("x") + lax.axis_index("y"), 0, 0))


def _swap_halves(g):
    S, R, W = g.shape
    half = R // 2

    def body(g_ref, out_ref, send_sem, recv_sem):
        x, y, c = _me()
        cp = pltpu.make_async_remote_copy(
            src_ref=g_ref.at[:, pl.ds((1 - c) * half, half), :], dst_ref=out_ref,
            send_sem=send_sem, recv_sem=recv_sem, device_id=(x, y, 1 - c), device_id_type=MESH)
        cp.start()
        cp.wait()

    return pl.pallas_call(
        body, name="swap_halves", in_specs=[ANY], out_specs=ANY,
        out_shape=jax.ShapeDtypeStruct((S, half, W), g.dtype),
        scratch_shapes=[pltpu.SemaphoreType.DMA, pltpu.SemaphoreType.DMA],
    )(g)


def _pair_sum(g, got, tm):
    S, R, W = g.shape
    half = R // 2
    nb = half // tm

    def body(c_ref, g_ref, got_ref, o_ref, ob_ref):
        tot = g_ref[...] + got_ref[...]
        o_ref[...] = tot
        ob_ref[...] = tot.astype(ob_ref.dtype)

    tile = pl.BlockSpec((None, tm, W), lambda s, i, c_ref: (s, i, 0))
    return pl.pallas_call(
        body, name="pair_sum",
        grid_spec=pltpu.PrefetchScalarGridSpec(
            num_scalar_prefetch=1, grid=(S, nb),
            in_specs=[pl.BlockSpec((None, tm, W), lambda s, i, c_ref: (s, c_ref[0] * nb + i, 0)), tile],
            out_specs=[tile, tile]),
        out_shape=[jax.ShapeDtypeStruct((S, half, W), F32), jax.ShapeDtypeStruct((S, half, W), BF)],
        compiler_params=_cparams(("parallel", "parallel")),
    )(lax.axis_index("c").reshape(1).astype(jnp.int32), g, got)


def _scatter_chips(part):
    S, h, W = part.shape

    def body(p_ref, out_ref, send_sems, recv_sems):
        x, y, c = _me()
        chips = _other_chips(x, y)
        sends = [pltpu.make_async_remote_copy(
            src_ref=p_ref.at[2 * chip[0] + chip[1]], dst_ref=out_ref.at[j],
            send_sem=send_sems.at[j], recv_sem=recv_sems.at[j], device_id=(*chip, c), device_id_type=MESH)
            for j, chip in enumerate(chips)]
        for cp in sends:
            cp.start()
        for cp in sends:
            cp.wait()

    return pl.pallas_call(
        body, name="scatter_chips", in_specs=[ANY], out_specs=ANY,
        out_shape=jax.ShapeDtypeStruct((S - 1, h, W), part.dtype),
        scratch_shapes=[pltpu.SemaphoreType.DMA((3,)), pltpu.SemaphoreType.DMA((3,))],
    )(part)


def _chip_sum(own, others, tm):
    n, h, W = others.shape

    def body(own_ref, p_ref, o_ref):
        o_ref[...] = ((own_ref[...] + p_ref[0].astype(F32)) + p_ref[1].astype(F32)) + p_ref[2].astype(F32)

    return pl.pallas_call(
        body, name="chip_sum", grid=(h // tm,),
        in_specs=[pl.BlockSpec((tm, W), lambda i: (i, 0)), pl.BlockSpec((n, tm, W), lambda i: (0, i, 0))],
        out_specs=pl.BlockSpec((tm, W), lambda i: (i, 0)),
        out_shape=jax.ShapeDtypeStruct((h, W), F32),
        compiler_params=_cparams(("parallel",)),
    )(own, others)


def _join_halves(mine):
    h, W = mine.shape

    def body(m_ref, out_ref, send_sem, recv_sem):
        x, y, c = _me()
        cp = pltpu.make_async_remote_copy(
            src_ref=m_ref, dst_ref=out_ref.at[pl.ds(c * h, h), :],
            send_sem=send_sem, recv_sem=recv_sem, device_id=(x, y, 1 - c), device_id_type=MESH)
        cp.start()
        pltpu.make_async_remote_copy(
            src_ref=m_ref, dst_ref=out_ref.at[pl.ds((1 - c) * h, h), :],
            send_sem=send_sem, recv_sem=recv_sem, device_id=(x, y, 1 - c), device_id_type=MESH).wait_recv()
        cp.wait_send()

    other = pl.pallas_call(
        body, name="join_halves", in_specs=[ANY], out_specs=ANY,
        out_shape=jax.ShapeDtypeStruct((2 * h, W), mine.dtype),
        scratch_shapes=[pltpu.SemaphoreType.DMA, pltpu.SemaphoreType.DMA],
    )(mine)
    return lax.dynamic_update_slice(other, mine, (lax.axis_index("c") * h, 0))


def _allreduce_small(s):
    R, W = s.shape

    def body(s_ref, o_ref, buf, send_sems, recv_sems):
        x, y, c = _me()
        me = 4 * x + 2 * y + c
        buf[me] = s_ref[...]
        peers = [((x + fx) % 2, (y + fy) % 2, (c + fc) % 2) for fx in range(2) for fy in range(2) for fc in range(2)][1:]
        sends = [pltpu.make_async_remote_copy(
            src_ref=s_ref, dst_ref=buf.at[me], send_sem=send_sems.at[k], recv_sem=recv_sems.at[k],
            device_id=peer, device_id_type=MESH) for k, peer in enumerate(peers)]
        for cp in sends:
            cp.start()
        for k, peer in enumerate(peers):
            pltpu.make_async_remote_copy(
                src_ref=s_ref, dst_ref=buf.at[4 * peer[0] + 2 * peer[1] + peer[2]], send_sem=send_sems.at[k],
                recv_sem=recv_sems.at[k], device_id=peer, device_id_type=MESH).wait_recv()
        for cp in sends:
            cp.wait_send()
        total = buf[0]
        for d in range(1, N_DEV):
            total = total + buf[d]
        o_ref[...] = total

    return pl.pallas_call(
        body, name="allreduce_small",
        in_specs=[pl.BlockSpec(memory_space=pltpu.VMEM)], out_specs=pl.BlockSpec(memory_space=pltpu.VMEM),
        out_shape=jax.ShapeDtypeStruct((R, W), F32),
        scratch_shapes=[pltpu.VMEM((N_DEV, R, W), F32), pltpu.SemaphoreType.DMA((N_DEV - 1,)), pltpu.SemaphoreType.DMA((N_DEV - 1,))],
    )(s)


HBM_SPEC = pl.BlockSpec(memory_space=pltpu.HBM)
SEM_SPEC = pl.BlockSpec(memory_space=pltpu.SEMAPHORE)
DATAFLOW = pltpu.SideEffectType.DATAFLOW_SIDE_EFFECTING


class _InFlight(NamedTuple):
    sems: tuple
    src: jax.Array
    land: jax.Array
    token: jax.Array


def _split_start(name, src, land_shape, land_dtype, n, copies, after=()):
    n_after = len(after)

    def body(src_ref, land_ref, *rest):
        rest = rest[n_after:]
        sems, token = rest[:2 * n], rest[-1]
        for k, (s, d, peer) in enumerate(copies(src_ref, land_ref)):
            pltpu.make_async_remote_copy(src_ref=s, dst_ref=d, send_sem=sems[k], recv_sem=sems[n + k],
                                         device_id=peer, device_id_type=MESH).start()
        token[...] = jnp.zeros_like(token)

    outs = pl.pallas_call(
        body, name=name,
        out_shape=(*[pltpu.SemaphoreType.DMA(())] * (2 * n), pltpu.HBM(src.shape, src.dtype), pltpu.HBM(land_shape, land_dtype),
                   jax.ShapeDtypeStruct((8, 128), F32)),
        in_specs=(HBM_SPEC, HBM_SPEC, *[ANY] * n_after),
        out_specs=(*[SEM_SPEC] * (2 * n), HBM_SPEC, HBM_SPEC, pl.BlockSpec(memory_space=pltpu.VMEM)),
        input_output_aliases={0: 2 * n, 1: 2 * n + 1},
        compiler_params=pltpu.CompilerParams(has_side_effects=DATAFLOW),
    )(pltpu.with_memory_space_constraint(src, pltpu.HBM), pltpu.with_memory_space_constraint(lax.empty(land_shape, land_dtype), pltpu.HBM),
      *after)
    return _InFlight(tuple(outs[:2 * n]), outs[2 * n], outs[2 * n + 1], outs[2 * n + 2])


def _split_wait(name, flight, after, n, copies):
    def body(src_ref, land_ref, *rest):
        sems = rest[:2 * n]
        for k, (s, d, peer) in enumerate(copies(src_ref, land_ref)):
            cp = pltpu.make_async_remote_copy(src_ref=s, dst_ref=d, send_sem=sems[k], recv_sem=sems[n + k],
                                              device_id=peer, device_id_type=MESH)
            cp.wait_send()
            cp.wait_recv()

    return pl.pallas_call(
        body, name=name,
        out_shape=(pltpu.HBM(flight.src.shape, flight.src.dtype), pltpu.HBM(flight.land.shape, flight.land.dtype)),
        in_specs=(HBM_SPEC, HBM_SPEC, *[SEM_SPEC] * (2 * n), ANY),
        out_specs=(HBM_SPEC, HBM_SPEC), input_output_aliases={0: 0, 1: 1},
        compiler_params=pltpu.CompilerParams(has_side_effects=DATAFLOW),
    )(flight.src, flight.land, *flight.sems, after)


def _gather_copies(src_ref, land_ref):
    x, y, c = _me()
    return [(src_ref, land_ref.at[2 * x + y], (*chip, c)) for chip in _other_chips(x, y)]


def _gather_start(packed, tag, after=()):
    return _split_start(f"gather_start_{tag}", packed, (N_CHIPS, *packed.shape), packed.dtype, 3, _gather_copies, after)


def _gather_wait(flight, after, tag):
    src, others = _split_wait(f"gather_wait_{tag}", flight, after, 3, _gather_copies)
    return lax.dynamic_update_slice(others, src[None], (2 * lax.axis_index("x") + lax.axis_index("y"), 0, 0))


def _swap_copies(src_ref, land_ref):
    x, y, c = _me()
    half = land_ref.shape[1]
    return [(src_ref.at[:, pl.ds((1 - c) * half, half), :], land_ref, (x, y, 1 - c))]


def _swap_start(g, tag):
    S, R, W = g.shape
    return _split_start(f"swap_halves_start_{tag}", g, (S, R // 2, W), g.dtype, 1, _swap_copies)


def _swap_wait(flight, after, tag):
    return _split_wait(f"swap_halves_wait_{tag}", flight, after, 1, _swap_copies)


def _scatter_copies(src_ref, land_ref):
    x, y, c = _me()
    return [(src_ref.at[2 * chip[0] + chip[1]], land_ref.at[j], (*chip, c)) for j, chip in enumerate(_other_chips(x, y))]


def _scatter_start(part, tag):
    S, h, W = part.shape
    return _split_start(f"scatter_chips_start_{tag}", part, (S - 1, h, W), part.dtype, 3, _scatter_copies)


def _scatter_wait(flight, after, tag):
    return _split_wait(f"scatter_chips_wait_{tag}", flight, after, 3, _scatter_copies)[1]


def _join_copies(src_ref, land_ref):
    x, y, c = _me()
    h = src_ref.shape[0]
    return [(src_ref, land_ref.at[pl.ds(c * h, h), :], (x, y, 1 - c))]


def _join_start(mine):
    h, W = mine.shape
    return _split_start("join_halves_start", mine, (2 * h, W), mine.dtype, 1, _join_copies)


def _join_wait(flight, after):
    src, other = _split_wait("join_halves_wait", flight, after, 1, _join_copies)
    return lax.dynamic_update_slice(other, src, (lax.axis_index("c") * src.shape[0], 0))


def _adamw(name, g, w, m, v):
    R, C = w.shape
    tm = R
    for cand in (256, 128, 64, 32, 16, 8):
        if R % cand == 0:
            tm = cand
            break

    def body(g, w, m, v):
        m = ADAM_B1 * m + (1.0 - ADAM_B1) * g
        v = ADAM_B2 * v + (1.0 - ADAM_B2) * jnp.square(g)
        m_hat = m / (1.0 - ADAM_B1 ** ADAM_STEP)
        v_hat = v / (1.0 - ADAM_B2 ** ADAM_STEP)
        delta = -ADAM_LR * (m_hat / (jnp.sqrt(v_hat) + ADAM_EPS) + ADAM_WD * w)
        return delta, m, v

    return _rowwise(name, body, R, tm, [_row(t, tm) for t in (g, w, m, v)], [(C, F32)] * 3)


def _unpack_weights(gathered, names):
    S = gathered.shape[0]
    shard_shapes = {"w_in": (D_MODEL, (QKV_WIDTH + 2 * D_MODEL) // S), "w_branch_na": (NA_WIDTH, D_MODEL // S),
                    "w_branch_dil": (DIL_OUT_WIDTH, D_MODEL // S), "w_out": (D_MODEL // S, D_MODEL),
                    "w_up": (D_MODEL, D_FF // S), "w_down": (D_FF // S, D_MODEL),
                    "w_ple_gate": (D_MODEL // S, D_MODEL), "w_ple_proj": (PLE_DIM, D_MODEL // S)}
    col_sharded = {"w_in", "w_branch_na", "w_branch_dil", "w_up", "w_ple_proj"}
    out, r0 = {}, 0
    for name in names:
        rows, cols = shard_shapes[name]
        n = rows * cols // PACK_W
        t = gathered[:, r0:r0 + n, :].reshape(S, rows, cols)
        r0 += n
        out[name] = t.transpose(1, 0, 2).reshape(rows, S * cols) if name in col_sharded else t.reshape(S * rows, cols)
    return out


def _pack_grads(grads, names):
    col_sharded = {"w_in", "w_branch_na", "w_branch_dil", "w_up", "w_ple_proj"}
    per_chip = []
    for s in range(N_CHIPS):
        rows = []
        for name in names:
            g = grads[name]
            if name in col_sharded:
                w = g.shape[1] // N_CHIPS
                rows.append(_pack_rows(g[:, s * w:(s + 1) * w]))
            else:
                h = g.shape[0] // N_CHIPS
                rows.append(_pack_rows(g[s * h:(s + 1) * h]))
        per_chip.append(jnp.concatenate(rows, axis=0))
    return jnp.stack(per_chip)


def _unpack_shard(packed, shapes, names):
    out, r0 = {}, 0
    for name in names:
        rows, cols = shapes[name]
        n = rows * cols // PACK_W
        out[name] = packed[r0:r0 + n].reshape(rows, cols)
        r0 += n
    return out


def kernel(x, p, positions, g_mix, w_in, rpb, w_branch_na, w_branch_dil, w_out, g_mlp, w_up, w_down, g_ple, w_ple_gate, w_ple_proj, g_final, loss_target, m_g_mix, m_w_in, m_rpb, m_w_branch_na, m_w_branch_dil, m_w_out, m_g_mlp, m_w_up, m_w_down, m_g_ple, m_w_ple_gate, m_w_ple_proj, m_g_final, v_g_mix, v_w_in, v_rpb, v_w_branch_na, v_w_branch_dil, v_w_out, v_g_mlp, v_w_up, v_w_down, v_g_ple, v_w_ple_gate, v_w_ple_proj, v_g_final):
    shards = {"w_in": w_in[0], "w_branch_na": w_branch_na[0], "w_branch_dil": w_branch_dil[0], "w_out": w_out[0],
              "w_up": w_up[0], "w_down": w_down[0], "w_ple_gate": w_ple_gate[0], "w_ple_proj": w_ple_proj[0]}
    m_shards = {"w_in": m_w_in[0], "w_branch_na": m_w_branch_na[0], "w_branch_dil": m_w_branch_dil[0], "w_out": m_w_out[0],
                "w_up": m_w_up[0], "w_down": m_w_down[0], "w_ple_gate": m_w_ple_gate[0], "w_ple_proj": m_w_ple_proj[0]}
    v_shards = {"w_in": v_w_in[0], "w_branch_na": v_w_branch_na[0], "w_branch_dil": v_w_branch_dil[0], "w_out": v_w_out[0],
                "w_up": v_w_up[0], "w_down": v_w_down[0], "w_ple_gate": v_w_ple_gate[0], "w_ple_proj": v_w_ple_proj[0]}

    W = _unpack_weights(_gather_weights(_pack_rows(shards["w_in"].astype(BF))), GATHER_FIRST)
    mix_flight = _gather_start(jnp.concatenate([_pack_rows(shards[n].astype(BF)) for n in GATHER_MIX], axis=0), "mix")
    rest_flight = _gather_start(jnp.concatenate([_pack_rows(shards[n].astype(BF)) for n in GATHER_MLP], axis=0), "mlp",
                                after=(mix_flight.token,))
    w_qkv, w_gates = W["w_in"][:, :QKV_WIDTH], W["w_in"][:, QKV_WIDTH:]

    xs, ps, tgt = x[0], p[0, 0], loss_target[0]
    T = xs.shape[0]
    TM = 512
    gm, gl, gp, gf = g_mix, g_mlp, g_ple, g_final.reshape(1, D_MODEL)
    cos2, sin_signed = _rope_tables(positions[0])
    tab = _na_bias_table(rpb[0])

    a = _rowwise("norm_mix", lambda h, g: h * _rms(h) * g, T, TM, [_row(xs, TM), _full(gm)], [(D_MODEL, BF)],
                 after=(rest_flight.token,))
    n3 = 3 * NA_WIDTH
    qkv = _mm("in_na", a, w_qkv[:, :n3], "nn", 1024, 768, 1024, [BF])
    z_dil = _mm("in_dil", a, w_qkv[:, n3:], "nn", 1024, 1152, 1024, [F32])
    z_gates = _mm("in_gates", a, w_gates, "nn", 1024,1024, 1024, [BF])

    dil_ops = _qkv_prep(z_dil, cos2, sin_signed, TM)
    y_na = _na_fwd(qkv, tab)
    band = [_band_fwd(*dil_ops[g], g) for g in range(len(DIL_GROUPS))]
    y_dil, w_grp, o_nat = _dil_merge_fwd([b[0] for b in band], [b[1] for b in band], T, TM)

    W.update(_unpack_weights(_gather_wait(mix_flight, y_dil, "mix"), GATHER_MIX))
    u_na = _mm("branch_na", y_na, W["w_branch_na"], "nn", 1024,1024, 512, [BF])
    u_dil = _mm("branch_dil", y_dil, W["w_branch_dil"], "nn", 1024,1024, 256, [BF])
    mixed = _rowwise(
        "gate_mix", lambda gn, gd, un, ud: _sigmoid(gn.astype(F32)) * un.astype(F32) + _sigmoid(gd.astype(F32)) * ud.astype(F32), T, TM,
        [_row(z_gates, TM, 0, D_MODEL), _row(z_gates, TM, 1, D_MODEL), _row(u_na, TM), _row(u_dil, TM)], [(D_MODEL, BF)])
    def add_norm(d, h, g):
        h = h + d
        return h, h * _rms(h) * g

    h1, cn = _mm("out_proj", mixed, W["w_out"], "nn", 512, 1024, 1024, [F32, BF], epilogue=add_norm, extras=(xs,), consts=(gl,))
    W.update(_unpack_weights(_gather_wait(rest_flight, cn, "mlp"), GATHER_MLP))
    up, act = _mm("mlp_up", cn, W["w_up"], "nn", 1024,1024, 1024, [BF, BF],
                  epilogue=lambda acc: (acc, jnp.square(jnp.maximum(acc, 0.0))))
    h2, en = _mm("mlp_down", act, W["w_down"], "nn", 1024, 1024, 1024, [F32, BF], epilogue=add_norm, extras=(h1,), consts=(gp,))
    gt = _mm("ple_gate", en, W["w_ple_gate"], "nn", 1024,1024, 1024, [F32])
    pp = _mm("ple_proj", ps, W["w_ple_proj"], "nn", 1024,1024, 256, [F32])

    def head(h2t, gtt, ppt, tg, g):
        sg = _sigmoid(gtt)
        h3 = h2t + sg * ppt
        yo = h3 * _rms(h3) * g
        diff = yo - tg
        loss = 0.5 * jnp.sum(jnp.mean(jnp.square(diff), axis=-1, keepdims=True), axis=0, keepdims=True)
        dh3, dg = _rms_bwd(diff * (1.0 / D_MODEL), h3, g)
        return dh3, dh3 * ppt * sg * (1.0 - sg), dh3 * sg, jnp.broadcast_to(loss, (1, 128)), dg

    dh3, d_gt, d_pp, loss_part, dg_final = _rowwise(
        "loss_head", head, T, TM, [_row(h2, TM), _row(gt, TM), _row(pp, TM), _row(tgt, TM), _full(gf)],
        [(D_MODEL, F32), (D_MODEL, BF), (D_MODEL, BF)], sums=[128, D_MODEL])

    grads = {}
    early_shapes = {n: shards[n].shape for n in REDUCE_EARLY}
    early_rows = sum(r * c for r, c in early_shapes.values()) // PACK_W
    shard_rows = D_MODEL // N_CHIPS
    early_buf = _mm("g_ple_gate", en, d_gt, "tn", shard_rows, 1024, 1024, [F32],
                    into=(jax.ShapeDtypeStruct((N_CHIPS, early_rows, PACK_W), F32), (None, shard_rows, PACK_W),
                          lambda i, j: (i, 2 * D_MODEL // shard_rows, 0)))
    g_ple_proj = _mm("g_ple_proj", ps, d_pp, "tn", 256, 1024, 1024,[F32])
    early_buf = lax.dynamic_update_slice(
        early_buf, jnp.stack([_pack_rows(g_ple_proj[:, s * shard_rows:(s + 1) * shard_rows]) for s in range(N_CHIPS)]),
        (0, 2 * D_MODEL + shard_rows, 0))
    def add_norm_bwd(dn, dh_out, h, g):
        dh, dg = _rms_bwd(dn, h, g)
        dh = dh_out + dh
        return dh, dh, dg

    dh2, dh2_b, dg_ple = _mm("d_ple_gate", d_gt, W["w_ple_gate"], "nt", 512, 1024, 1024, [F32, BF],
                             epilogue=add_norm_bwd, extras=(dh3, h2), consts=(gp,), sums=[D_MODEL])
    d_up = _mm("d_mlp_down", dh2_b, W["w_down"], "nt", 1024,1024, 1024, [BF],
               epilogue=lambda acc, u: (acc * (2.0 * jnp.maximum(u.astype(F32), 0.0)),), extras=(up,))
    early_buf = _mm("g_mlp_down", act, dh2_b, "tn", 1024, 1024, 1024,[F32],
                    into=(early_buf, (None, D_MODEL, PACK_W), lambda i, j: (i, 1, 0)))
    early_buf = _mm("g_mlp_up", cn, d_up, "tn", 1024, 1024, 1024,[F32],
                    into=(early_buf, (None, D_MODEL, PACK_W), lambda i, j: (j, 0, 0)))
    early_tm = early_rows // 4
    swap_flight = _swap_start(early_buf, "early")
    dh1, dh1_b, dg_mlp = _mm("d_mlp_up", d_up, W["w_up"], "nt", 1024, 1024, 1024, [F32, BF], epilogue=add_norm_bwd,
                             extras=(dh2, h1), consts=(gl,), sums=[D_MODEL], after=(swap_flight.token,))
    early_g, early_got = _swap_wait(swap_flight, dh1_b, "early")
    early_pair, early_pair_b = _pair_sum(early_g, early_got, early_tm)
    scatter_flight = _scatter_start(early_pair_b, "early")
    d_mixed = _mm("d_out_proj", dh1_b, W["w_out"], "nt", 1024,1024, 1024, [F32], after=(scatter_flight.token,))
    grads["w_out"] = _mm("g_out_proj", mixed, dh1_b, "tn", 1024, 1024, 1024,[F32])

    def gate_bwd(dm, gn, gd, un, ud):
        gn, gd, un, ud = (t.astype(F32) for t in (gn, gd, un, ud))
        sn, sd = _sigmoid(gn), _sigmoid(gd)
        return jnp.concatenate([dm * un * sn * (1.0 - sn), dm * ud * sd * (1.0 - sd)], axis=1), dm * sn, dm * sd

    dz_gates, d_u_na, d_u_dil = _rowwise(
        "gate_mix_bwd", gate_bwd, T, TM,
        [_row(d_mixed, TM), _row(z_gates, TM, 0, D_MODEL), _row(z_gates, TM, 1, D_MODEL), _row(u_na, TM), _row(u_dil, TM)],
        [(2 * D_MODEL, BF), (D_MODEL, BF), (D_MODEL, BF)])
    grads["w_branch_na"] = _mm("g_branch_na", y_na, d_u_na, "tn", 1024, 1024, 1024,[F32])
    grads["w_branch_dil"] = _mm("g_branch_dil", y_dil, d_u_dil, "tn", 256, 1024, 1024,[F32])
    d_y_na = _mm("d_branch_na", d_u_na, W["w_branch_na"], "nt", 1024,512, 1024, [BF])
    d_y_dil = _mm("d_branch_dil", d_u_dil, W["w_branch_dil"], "nt", 1024,256, 1024, [F32])

    dqa, dka, dva, dtab = _na_bwd(qkv, tab, d_y_na)
    d_rpb = _na_rpb_grad(dtab)[:, :2 * NA_WIN_ROWS - 1, :2 * NA_WIN_COLS - 1]

    do_res, dlse_res = _dil_merge_bwd(d_y_dil, o_nat, w_grp, TM)
    d_dil = [_band_bwd(*dil_ops[g], do_res[g], dlse_res[g], g) for g in range(len(DIL_GROUPS))]

    me_chip = 2 * lax.axis_index("x") + lax.axis_index("y")
    early_mine = _chip_sum(lax.dynamic_index_in_dim(early_pair, me_chip, 0, keepdims=False),
                           _scatter_wait(scatter_flight, d_dil[-1][-1], "early"), early_tm)
    join_flight = _join_start(early_mine)
    dz_qkv = _qkv_unprep((dqa, dka, dva), d_dil, cos2, sin_signed, TM, after=(join_flight.token,))
    grads["w_in"] = jnp.concatenate([
        _mm("g_in_qkv", a, dz_qkv, "tn", 1024, 1280, 1024,[F32]),
        _mm("g_in_gates", a, dz_gates, "tn", 1024, 1024, 1024,[F32])], axis=1)
    late_shapes = {n: shards[n].shape for n in REDUCE_LATE}
    late_tm = sum(r * c for r, c in late_shapes.values()) // PACK_W // 4
    late_swap = _swap_start(_pack_grads(grads, REDUCE_LATE), "late")
    d_a = _mm("d_in_qkv", dz_qkv, w_qkv, "nt", 1024,1024, 1280, [F32], after=(late_swap.token,))
    late_g, late_got = _swap_wait(late_swap, d_a, "late")
    late_pair, late_pair_b = _pair_sum(late_g, late_got, late_tm)
    late_scatter = _scatter_start(late_pair_b, "late")
    def first_bwd(dn_gates, dn_qkv, dh_out, h, g):
        dh, dg = _rms_bwd(dn_gates + dn_qkv, h, g)
        return dh_out + dh, dg

    grad_x, dg_mix = _mm("d_in_gates", dz_gates, w_gates, "nt", 512, 1024, 1024, [F32], epilogue=first_bwd,
                         extras=(d_a, dh1, xs), consts=(gm,), sums=[D_MODEL], after=(late_scatter.token,))
    g_shard = _unpack_shard(_join_wait(join_flight, grad_x), early_shapes, REDUCE_EARLY)

    n_rpb = rpb.size
    rpb_rows = 4
    small = jnp.concatenate([
        dg_mix, dg_mlp, dg_ple, dg_final,
        jnp.pad(d_rpb.reshape(-1), (0, rpb_rows * D_MODEL - n_rpb)).reshape(rpb_rows, D_MODEL),
        jnp.pad(loss_part, ((0, 0), (0, D_MODEL - loss_part.shape[1]))),
        jnp.zeros((SMALL_ROWS - 5 - rpb_rows, D_MODEL), F32)], axis=0)
    small = _allreduce_small(small)
    loss = small[4 + rpb_rows, 0]

    def small_pack(a0, a1, a2, a3, r):
        return jnp.concatenate([a0.reshape(1, -1), a1.reshape(1, -1), a2.reshape(1, -1), a3.reshape(1, -1),
                                jnp.pad(r.reshape(-1), (0, rpb_rows * D_MODEL - n_rpb)).reshape(rpb_rows, D_MODEL)], axis=0)

    g_small = small[:4 + rpb_rows]
    small_res = _adamw("adamw_small", g_small, small_pack(g_mix, g_mlp, g_ple, g_final, rpb),
                       small_pack(m_g_mix, m_g_mlp, m_g_ple, m_g_final, m_rpb), small_pack(v_g_mix, v_g_mlp, v_g_ple, v_g_final, v_rpb))

    def small_unpack(t):
        return {"g_mix": t[0].reshape(g_mix.shape), "g_mlp": t[1].reshape(g_mlp.shape), "g_ple": t[2].reshape(g_ple.shape),
                "g_final": t[3].reshape(g_final.shape), "rpb": t[4:].reshape(-1)[:n_rpb].reshape(rpb.shape)}

    out = {"grad": small_unpack(g_small)}
    for kind, t in zip(("delta", "new_m", "new_v"), small_res, strict=True):
        out[kind] = small_unpack(t)
    def update(names):
        for n in names:
            out["grad"][n] = g_shard[n][None]
            res = _adamw("adamw_" + n, g_shard[n], shards[n], m_shards[n], v_shards[n])
            for kind, t in zip(("delta", "new_m", "new_v"), res, strict=True):
                out[kind][n] = t[None]

    update(REDUCE_EARLY)
    late_others = _scatter_wait(late_scatter, out["new_v"][REDUCE_EARLY[-1]], "late")
    late_mine = _chip_sum(lax.dynamic_index_in_dim(late_pair, me_chip, 0, keepdims=False), late_others, late_tm)
    g_shard.update(_unpack_shard(_join_halves(late_mine), late_shapes, REDUCE_LATE))
    update(REDUCE_LATE)

    order = ["g_mix", "w_in", "rpb", "w_branch_na", "w_branch_dil", "w_out", "g_mlp", "w_up", "w_down", "g_ple",
             "w_ple_gate", "w_ple_proj", "g_final"]
    return (loss, grad_x[None], *[out["grad"][n] for n in order], *[out["delta"][n] for n in order],
            *[out["new_m"][n] for n in order], *[out["new_v"][n] for n in order])
```

```python
import functools
from typing import NamedTuple

import numpy as np
import jax
import jax.numpy as jnp
from jax import lax
from jax.experimental import pallas as pl
from jax.experimental.pallas import tpu as pltpu

BF = jnp.bfloat16
F32 = jnp.float32
MESH = pl.DeviceIdType.MESH
ANY = pl.BlockSpec(memory_space=pl.ANY)

V7X_VMEM_BYTES = 64 * 1024 * 1024
VMEM_LIMIT = V7X_VMEM_BYTES - 16 * 1024 * 1024

D_MODEL = 1024
HEAD_DIM = 64
GRID_W = 64
NA_HEADS = 8
NA_WIN_ROWS = 8
NA_WIN_COLS = 16
NA_WIDTH = NA_HEADS * HEAD_DIM
DIL_GROUPS = ((128, 1), (512, 4), (2048, 16))
DIL_HPG = 4
DIL_HEADS = DIL_HPG * len(DIL_GROUPS)
DIL_WIDTH = DIL_HEADS * HEAD_DIM
DIL_OUT_WIDTH = DIL_HPG * HEAD_DIM
DIL_RADIUS = 64
QKV_WIDTH = 3 * NA_WIDTH + 3 * DIL_WIDTH
D_FF = 4 * D_MODEL
PLE_DIM = 256
ROPE_THETA = 10000.0
RMS_EPS = 1e-6
NEG_INF = -1e30
Q_SCALE = HEAD_DIM ** -0.5

ADAM_LR = 0.001
ADAM_B1 = 0.9
ADAM_B2 = 0.999
ADAM_EPS = 1e-08
ADAM_WD = 0.01
ADAM_STEP = 10

N_CHIPS = 4
N_DEV = 8
PACK_W = 1024
BIG = ("w_in", "w_branch_na", "w_branch_dil", "w_out", "w_up", "w_down", "w_ple_gate", "w_ple_proj")
GATHER_FIRST = ("w_in",)
GATHER_MIX = ("w_branch_na", "w_branch_dil", "w_out")
GATHER_MLP = ("w_up", "w_down", "w_ple_gate", "w_ple_proj")
REDUCE_EARLY = ("w_up", "w_down", "w_ple_gate", "w_ple_proj")
REDUCE_LATE = ("w_in", "w_branch_na", "w_branch_dil", "w_out")
SMALL_ROWS = 16


def _cparams(sem=None):
    return pltpu.CompilerParams(dimension_semantics=sem, vmem_limit_bytes=VMEM_LIMIT)


def _mm(name, a, b, mode, tm, tn, tk, out_dtypes, epilogue=None, extras=(), consts=(), sums=(), after=(), into=None):
    if mode == "nn":
        (M, K), N = a.shape, b.shape[1]
    elif mode == "nt":
        (M, K), N = a.shape, b.shape[0]
    else:
        (K, M), N = a.shape, b.shape[1]
    tm, tn, tk = min(tm, M), min(tn, N), min(tk, K)
    assert M % tm == 0 and N % tn == 0 and K % tk == 0, (name, M, N, K, tm, tn, tk)
    if mode == "nn":
        a_spec = pl.BlockSpec((tm, tk), lambda i, j, k: (i, k))
        b_spec = pl.BlockSpec((tk, tn), lambda i, j, k: (k, j))
        dims = (((1,), (0,)), ((), ()))
    elif mode == "nt":
        a_spec = pl.BlockSpec((tm, tk), lambda i, j, k: (i, k))
        b_spec = pl.BlockSpec((tn, tk), lambda i, j, k: (j, k))
        dims = (((1,), (1,)), ((), ()))
    else:
        a_spec = pl.BlockSpec((tk, tm), lambda i, j, k: (k, i))
        b_spec = pl.BlockSpec((tk, tn), lambda i, j, k: (k, j))
        dims = (((0,), (0,)), ((), ()))
    nk = K // tk
    n_extra, n_const, n_out, n_sum = len(extras), len(consts), len(out_dtypes), len(sums)
    tile = pl.BlockSpec((tm, tn), lambda i, j, k: (i, j))
    assert not sums or tn == N, "row sums need whole rows in a tile"

    n_after = len(after)

    def body(a_ref, b_ref, *rest):
        extra_refs, rest = rest[:n_extra + n_const], rest[n_extra + n_const + n_after:]
        out_refs, sum_refs, acc = rest[:n_out], rest[n_out:n_out + n_sum], rest[-1]
        i, k = pl.program_id(0), pl.program_id(2)

        @pl.when(k == 0)
        def _():
            acc[...] = jnp.zeros_like(acc)

        acc[...] += lax.dot_general(a_ref[...].astype(BF), b_ref[...].astype(BF), dims, preferred_element_type=F32)

        @pl.when(k == nk - 1)
        def _():
            outs = (acc[...],) if epilogue is None else epilogue(acc[...], *[e[...] for e in extra_refs])
            for o_ref, val in zip(out_refs, outs[:n_out], strict=True):
                o_ref[...] = val.astype(o_ref.dtype)
            for s_ref, val in zip(sum_refs, outs[n_out:], strict=True):
                @pl.when(i == 0)
                def _():
                    s_ref[...] = val

                @pl.when(i != 0)
                def _():
                    s_ref[...] += val

    out_specs = [tile] * n_out + [pl.BlockSpec((1, c), lambda i, j, k: (0, 0)) for c in sums]
    out_shape = [jax.ShapeDtypeStruct((M, N), dt) for dt in out_dtypes] + [jax.ShapeDtypeStruct((1, c), F32) for c in sums]
    operands, aliases = [a, b, *extras, *consts, *after], {}
    in_specs = ([a_spec, b_spec] + [tile] * n_extra
                + [pl.BlockSpec(c.shape, functools.partial(lambda nd, i, j, k: (0,) * nd, c.ndim)) for c in consts] + [ANY] * n_after)
    if into is not None:
        assert n_out == 1
        target, block, index = into
        out_specs = [pl.BlockSpec(block, lambda i, j, k: index(i, j))]
        out_shape = [jax.ShapeDtypeStruct(target.shape, target.dtype)]
        if not isinstance(target, jax.ShapeDtypeStruct):
            aliases = {len(operands): 0}
            operands.append(target)
            in_specs.append(ANY)
            n_after += 1

    outs = pl.pallas_call(
        body, name=name, grid=(M // tm, N // tn, nk),
        in_specs=in_specs, out_specs=out_specs, out_shape=out_shape,
        scratch_shapes=[pltpu.VMEM((tm, tn), F32)], input_output_aliases=aliases,
        compiler_params=_cparams(("arbitrary",) * 3 if sums else ("parallel", "parallel", "arbitrary")),
    )(*operands)
    return outs[0] if len(outs) == 1 else outs


def _row(arr, tm, col_block=None, width=None):
    width = arr.shape[1] if width is None else width
    cb = 0 if col_block is None else col_block
    return arr, pl.BlockSpec((tm, width), lambda i: (i, cb))


def _full(arr):
    nd = arr.ndim
    return arr, pl.BlockSpec(arr.shape, lambda i: (0,) * nd)


def _rowwise(name, body, T, tm, ins, outs, sums=(), after=()):
    n_in, n_out, n_sum, n_after = len(ins), len(outs), len(sums), len(after)

    def kern(*refs):
        in_refs, refs = refs[:n_in], refs[n_in + n_after:]
        out_refs, sum_refs = refs[:n_out], refs[n_out:]
        res = body(*[r[...] for r in in_refs])
        res = res if isinstance(res, tuple) else (res,)
        for o_ref, val in zip(out_refs, res[:n_out], strict=True):
            o_ref[...] = val.astype(o_ref.dtype)
        if n_sum:
            @pl.when(pl.program_id(0) == 0)
            def _():
                for s_ref in sum_refs:
                    s_ref[...] = jnp.zeros_like(s_ref)

            for s_ref, val in zip(sum_refs, res[n_out:], strict=True):
                s_ref[...] += val

    res = pl.pallas_call(
        kern, name=name, grid=(T // tm,),
        in_specs=[spec for _, spec in ins] + [ANY] * n_after,
        out_specs=[pl.BlockSpec((tm, c), lambda i: (i, 0)) for c, _ in outs]
        + [pl.BlockSpec((1, c), lambda i: (0, 0)) for c in sums],
        out_shape=[jax.ShapeDtypeStruct((T, c), dt) for c, dt in outs]
        + [jax.ShapeDtypeStruct((1, c), F32) for c in sums],
        compiler_params=_cparams(("arbitrary",)),
    )(*[a for a, _ in ins], *after)
    return res[0] if len(res) == 1 else res


def _sigmoid(x):
    return 1.0 / (1.0 + jnp.exp(-x))


def _rms(h):
    return lax.rsqrt(jnp.mean(h * h, axis=-1, keepdims=True) + RMS_EPS)


def _rms_bwd(dy, h, g):
    r = _rms(h)
    n = h * r
    dn = dy * g
    dh = r * (dn - n * jnp.mean(dn * n, axis=-1, keepdims=True))
    return dh, jnp.sum(dy * n, axis=0, keepdims=True)


def _rope(x, cos2, sin_signed):
    lane = lax.broadcasted_iota(jnp.int32, x.shape, 1)
    swapped = jnp.where((lane % HEAD_DIM) < HEAD_DIM // 2, pltpu.roll(x, 128 - HEAD_DIM // 2, 1), pltpu.roll(x, HEAD_DIM // 2, 1))
    return x * cos2 + swapped * sin_signed


def _rope_cols(x, cos2, sin_signed):
    return jnp.concatenate([_rope(x[:, c:c + 128], cos2, sin_signed) for c in range(0, x.shape[1], 128)], axis=1)


NA_KEYS = NA_WIN_ROWS * GRID_W
NA_BASES = 8


def _na_row_geometry(r, rows):
    first = jnp.clip(r - NA_WIN_ROWS // 2, 0, rows - NA_WIN_ROWS)
    base = first - r + (NA_WIN_ROWS - 1)
    return pl.multiple_of(first * GRID_W, GRID_W), base


NA_ROWS_PER_STEP = 8
NA_BWD_ROWS_PER_STEP = 8


def _softmax_rows(s):
    p = jnp.exp(s - jnp.max(s, axis=-1, keepdims=True))
    return p / jnp.sum(p, axis=-1, keepdims=True)


def _na_probs(q, kw, bias):
    return _softmax_rows(lax.dot_general(q, kw, (((1,), (1,)), ((), ())), preferred_element_type=F32) + bias)


def _split_pair(t):
    first = lax.broadcasted_iota(jnp.int32, t.shape, 1) < HEAD_DIM
    zero = jnp.zeros_like(t)
    return jnp.where(first, t, zero), jnp.where(first, zero, t)


def _join_pair(a, b):
    return jnp.where(lax.broadcasted_iota(jnp.int32, a.shape, 1) < HEAD_DIM, a, b)


_NT = (((1,), (1,)), ((), ()))
_TN = (((0,), (0,)), ((), ()))


def _na_fwd(qkv, tab):
    T = qkv.shape[0]
    rows = T // GRID_W
    n_pairs = NA_WIDTH // 128

    def body(q_ref, k_ref, v_ref, tab_ref, y_ref):
        def step(it, carry):
            geo = [_na_row_geometry(it * NA_ROWS_PER_STEP + u, rows) for u in range(NA_ROWS_PER_STEP)]
            q0s = [pl.multiple_of((it * NA_ROWS_PER_STEP + u) * GRID_W, GRID_W) for u in range(NA_ROWS_PER_STEP)]
            ss = [lax.dot_general(jnp.concatenate(_split_pair(q_ref[pl.ds(q0, GRID_W), :] * Q_SCALE), axis=0),
                                  k_ref[pl.ds(k0, NA_KEYS), :], _NT, preferred_element_type=F32)
                  for q0, (k0, _) in zip(q0s, geo)]
            ps = [_softmax_rows(s + jnp.concatenate([tab_ref[0, base], tab_ref[1, base]], axis=0)) for s, (_, base) in zip(ss, geo)]
            ys = [jnp.dot(p.astype(BF), v_ref[pl.ds(k0, NA_KEYS), :], preferred_element_type=F32) for p, (k0, _) in zip(ps, geo)]
            for q0, y2 in zip(q0s, ys):
                y_ref[pl.ds(q0, GRID_W), :] = _join_pair(y2[:GRID_W], y2[GRID_W:]).astype(y_ref.dtype)
            return carry

        lax.fori_loop(0, rows // NA_ROWS_PER_STEP, step, 0)

    def cols(first):
        return pl.BlockSpec((T, 128), lambda j: (0, first + j))

    return pl.pallas_call(
        body, name="na_fwd", grid=(n_pairs,),
        in_specs=[cols(0), cols(n_pairs), cols(2 * n_pairs), pl.BlockSpec((2, NA_BASES, GRID_W, NA_KEYS), lambda j: (j, 0, 0, 0))],
        out_specs=cols(0), out_shape=jax.ShapeDtypeStruct((T, NA_WIDTH), BF),
        compiler_params=_cparams(("parallel",)),
    )(qkv, qkv, qkv, tab)


def _na_bwd(qkv, tab, do):
    T = qkv.shape[0]
    rows = T // GRID_W
    n_pairs = NA_WIDTH // 128

    def body(q_ref, k_ref, v_ref, tab_ref, do_ref, dq_ref, dk_ref, dv_ref, dtab_ref):
        dk_ref[...] = jnp.zeros_like(dk_ref)
        dv_ref[...] = jnp.zeros_like(dv_ref)
        dtab_ref[...] = jnp.zeros_like(dtab_ref)

        def step(it, carry):
            U = NA_BWD_ROWS_PER_STEP
            geo = [_na_row_geometry(it * U + u, rows) for u in range(U)]
            q0s = [pl.multiple_of((it * U + u) * GRID_W, GRID_W) for u in range(U)]
            q2s = [jnp.concatenate(_split_pair(q_ref[pl.ds(q0, GRID_W), :] * Q_SCALE), axis=0) for q0 in q0s]
            do2s = [jnp.concatenate(_split_pair(do_ref[pl.ds(q0, GRID_W), :]), axis=0) for q0 in q0s]
            ss = [lax.dot_general(q2, k_ref[pl.ds(k0, NA_KEYS), :], _NT, preferred_element_type=F32) for q2, (k0, _) in zip(q2s, geo)]
            dps = [lax.dot_general(do2, v_ref[pl.ds(k0, NA_KEYS), :], _NT, preferred_element_type=F32) for do2, (k0, _) in zip(do2s, geo)]
            ps = [_softmax_rows(s + jnp.concatenate([tab_ref[0, base], tab_ref[1, base]], axis=0)) for s, (_, base) in zip(ss, geo)]
            dss = [p * (dp - jnp.sum(dp * p, axis=-1, keepdims=True)) for p, dp in zip(ps, dps)]
            dvs = [lax.dot_general(p.astype(BF), do2, _TN, preferred_element_type=F32) for p, do2 in zip(ps, do2s)]
            dsbs = [ds.astype(BF) for ds in dss]
            dqs = [jnp.dot(dsb, k_ref[pl.ds(k0, NA_KEYS), :], preferred_element_type=F32) for dsb, (k0, _) in zip(dsbs, geo)]
            dks = [lax.dot_general(dsb, q2, _TN, preferred_element_type=F32) for dsb, q2 in zip(dsbs, q2s)]
            for u in range(U):
                k0, base = geo[u]
                dtab_ref[0, base] += dss[u][:GRID_W]
                dtab_ref[1, base] += dss[u][GRID_W:]
                dq_ref[pl.ds(q0s[u], GRID_W), :] = _join_pair(dqs[u][:GRID_W], dqs[u][GRID_W:])
                dk_ref[pl.ds(k0, NA_KEYS), :] += dks[u]
                dv_ref[pl.ds(k0, NA_KEYS), :] += dvs[u]
            return carry

        lax.fori_loop(0, rows // NA_BWD_ROWS_PER_STEP, step, 0)

    def cols(first):
        return pl.BlockSpec((T, 128), lambda j: (0, first + j))

    tabs = pl.BlockSpec((2, NA_BASES, GRID_W, NA_KEYS), lambda j: (j, 0, 0, 0))
    wide = jax.ShapeDtypeStruct((T, NA_WIDTH), F32)
    return pl.pallas_call(
        body, name="na_bwd", grid=(n_pairs,),
        in_specs=[cols(0), cols(n_pairs), cols(2 * n_pairs), tabs, cols(0)],
        out_specs=[cols(0), cols(0), cols(0), tabs],
        out_shape=[wide, wide, wide, jax.ShapeDtypeStruct((NA_HEADS, NA_BASES, GRID_W, NA_KEYS), F32)],
        compiler_params=_cparams(("parallel",)),
    )(qkv, qkv, qkv, tab, do)


def _na_bias_table(rpb):
    H, n_rows, n_cols = rpb.shape

    def body(r_ref, tab_ref):
        q = lax.broadcasted_iota(jnp.int32, (GRID_W, 128), 0)
        kc = lax.broadcasted_iota(jnp.int32, (GRID_W, 128), 1)
        first = jnp.clip(q - NA_WIN_COLS // 2, 0, GRID_W - NA_WIN_COLS)
        valid = (kc >= first) & (kc < first + NA_WIN_COLS)
        toeplitz = []
        for ro in range(n_rows):
            row = jnp.broadcast_to(r_ref[pl.ds(ro, 1), :], (GRID_W, 128))
            shifted = pltpu.roll(pltpu.roll(row, 128 - (NA_WIN_COLS - 1), 1), 0, 1, stride=1, stride_axis=0)
            toeplitz.append(jnp.where(valid, shifted, NEG_INF))
        for base in range(NA_BASES):
            for j in range(NA_WIN_ROWS // 2):
                even, odd = toeplitz[base + 2 * j], toeplitz[base + 2 * j + 1]
                tab_ref[base, :, pl.ds(j * 128, 128)] = jnp.where(kc < GRID_W, even, pltpu.roll(odd, GRID_W, 1))

    padded = jnp.pad(rpb, ((0, 0), (0, 16 - n_rows), (0, 128 - n_cols)))
    return pl.pallas_call(
        body, name="na_bias_table", grid=(H,),
        in_specs=[pl.BlockSpec((None, 16, 128), lambda h: (h, 0, 0))],
        out_specs=pl.BlockSpec((None, NA_BASES, GRID_W, NA_KEYS), lambda h: (h, 0, 0, 0)),
        out_shape=jax.ShapeDtypeStruct((H, NA_BASES, GRID_W, NA_KEYS), F32),
        compiler_params=_cparams(("parallel",)),
    )(padded)


def _na_rpb_grad(dtab):
    H = dtab.shape[0]
    n_rows = 2 * NA_WIN_ROWS - 1
    n_cols = 2 * NA_WIN_COLS - 1

    def body(d_ref, o_ref):
        lane = lax.broadcasted_iota(jnp.int32, (GRID_W, 128), 1)
        low = lane < GRID_W
        out_rows = []
        for ro in range(n_rows):
            acc = jnp.zeros((GRID_W, 128), F32)
            for base in range(NA_BASES):
                i = ro - base
                if not 0 <= i < NA_WIN_ROWS:
                    continue
                pair = d_ref[base, :, pl.ds((i // 2) * 128, 128)]
                if i % 2:
                    pair = pltpu.roll(pair, GRID_W, 1)
                acc = acc + jnp.where(low, pair, 0.0)
            skew = pltpu.roll(acc, 0, 1, stride=1, stride_axis=0)
            diag = jnp.sum(skew, axis=0, keepdims=True)
            out_rows.append(pltpu.roll(jnp.broadcast_to(diag, (8, 128)), 128 - (GRID_W - NA_WIN_COLS), 1)[:1])
        out_rows.append(jnp.zeros((1, 128), F32))
        res = jnp.concatenate(out_rows, axis=0)
        o_ref[...] = jnp.where(lax.broadcasted_iota(jnp.int32, res.shape, 1) < n_cols, res, 0.0)

    return pl.pallas_call(
        body, name="na_rpb_grad", grid=(H,),
        in_specs=[pl.BlockSpec((None, NA_BASES, GRID_W, NA_KEYS), lambda h: (h, 0, 0, 0))],
        out_specs=pl.BlockSpec((None, n_rows + 1, 128), lambda h: (h, 0, 0)),
        out_shape=jax.ShapeDtypeStruct((H, n_rows + 1, 128), F32),
        compiler_params=_cparams(("parallel",)),
    )(jnp.flip(dtab, axis=2))


BAND_Q = 128
BAND_KEYS = BAND_Q + 2 * DIL_RADIUS


def _band_geometry(n, L):
    q0 = pl.multiple_of(n * BAND_Q, BAND_Q)
    k0 = pl.multiple_of(jnp.clip(q0 - DIL_RADIUS, 0, L - BAND_KEYS), DIL_RADIUS)
    qi = q0 + lax.broadcasted_iota(jnp.int32, (BAND_Q, BAND_KEYS), 0)
    kj = k0 + lax.broadcasted_iota(jnp.int32, (BAND_Q, BAND_KEYS), 1)
    return q0, k0, jnp.abs(qi - kj) <= DIL_RADIUS


DIL_PAIRS = DIL_OUT_WIDTH // 128


def _residue_shape(dil, T, dtype):
    return jax.ShapeDtypeStruct((DIL_PAIRS, dil, T // dil, 128), dtype)


def _residue_tile(dil, tm):
    return pl.BlockSpec((DIL_PAIRS, dil, tm // dil, 128), lambda i: (0, 0, i, 0))


def _to_natural(ref, scratch, dil, tm):
    tiles = []
    for pair in range(DIL_PAIRS):
        if dil == 1:
            tiles.append(ref[pair, 0].astype(F32))
            continue
        for r in range(dil):
            scratch[pl.ds(r, tm // dil, stride=dil), :] = ref[pair, r].astype(F32)
        tiles.append(scratch[...])
    return tiles


def _from_natural(tile, scratch, ref, pair, dil, tm):
    if dil == 1:
        ref[pair, 0] = tile.astype(ref.dtype)
        return
    scratch[...] = tile
    for r in range(dil):
        ref[pair, r] = scratch[pl.ds(r, tm // dil, stride=dil), :].astype(ref.dtype)


def _band_specs(group, T):
    dil = DIL_GROUPS[group][1]
    L = T // dil
    assert L % BAND_Q == 0 and L >= BAND_KEYS, (T, dil)
    return L, (dil * DIL_PAIRS,), pl.BlockSpec((None, None, L, 128), lambda s: (s % DIL_PAIRS, s // DIL_PAIRS, 0, 0))


BAND_BLOCKS_PER_STEP = 4


def _band_softmax(s, valid):
    s = jnp.where(valid, s, NEG_INF)
    m = jnp.max(s, axis=-1, keepdims=True)
    p = jnp.exp(s - m)
    l = jnp.sum(p, axis=-1, keepdims=True)
    return p / l, m + jnp.log(l)


def _band_fwd(q, k, v, group):
    T = q.shape[1] * q.shape[2]
    L, grid, spec = _band_specs(group, T)
    U = min(BAND_BLOCKS_PER_STEP, L // BAND_Q)

    def body(q_ref, k_ref, v_ref, o_ref, lse_ref):
        def step(it, carry):
            geo = [_band_geometry(it * U + u, L) for u in range(U)]
            ss = [lax.dot_general(jnp.concatenate(_split_pair(q_ref[pl.ds(q0, BAND_Q), :]), axis=0),
                                  k_ref[pl.ds(k0, BAND_KEYS), :], _NT, preferred_element_type=F32) for q0, k0, _ in geo]
            pls = [_band_softmax(s, jnp.concatenate([valid, valid], axis=0)) for s, (_, _, valid) in zip(ss, geo)]
            os = [jnp.dot(p.astype(BF), v_ref[pl.ds(k0, BAND_KEYS), :], preferred_element_type=F32) for (p, _), (_, k0, _) in zip(pls, geo)]
            for (q0, _, _), o2, (_, lse) in zip(geo, os, pls):
                o_ref[pl.ds(q0, BAND_Q), :] = _join_pair(o2[:BAND_Q], o2[BAND_Q:])
                lse2 = jnp.broadcast_to(lse, (2 * BAND_Q, 128))
                lse_ref[pl.ds(q0, BAND_Q), :] = _join_pair(lse2[:BAND_Q], lse2[BAND_Q:])
            return carry

        lax.fori_loop(0, L // (BAND_Q * U), step, 0)

    res = _residue_shape(DIL_GROUPS[group][1], T, F32)
    return pl.pallas_call(
        body, name=f"band_fwd_g{group}", grid=grid,
        in_specs=[spec] * 3, out_specs=[spec] * 2, out_shape=[res, res],
        compiler_params=_cparams(("parallel",)),
    )(q, k, v)


def _band_bwd(q, k, v, do, dlse, group):
    T = q.shape[1] * q.shape[2]
    L, grid, spec = _band_specs(group, T)
    U = min(BAND_BLOCKS_PER_STEP, L // BAND_Q)

    def body(q_ref, k_ref, v_ref, do_ref, dlse_ref, dq_ref, dk_ref, dv_ref):
        dk_ref[...] = jnp.zeros_like(dk_ref)
        dv_ref[...] = jnp.zeros_like(dv_ref)

        def step(it, carry):
            geo = [_band_geometry(it * U + u, L) for u in range(U)]
            q2s = [jnp.concatenate(_split_pair(q_ref[pl.ds(q0, BAND_Q), :]), axis=0) for q0, _, _ in geo]
            do2s = [jnp.concatenate(_split_pair(do_ref[pl.ds(q0, BAND_Q), :]), axis=0) for q0, _, _ in geo]
            ss = [lax.dot_general(q2, k_ref[pl.ds(k0, BAND_KEYS), :], _NT, preferred_element_type=F32) for q2, (_, k0, _) in zip(q2s, geo)]
            dps = [lax.dot_general(do2, v_ref[pl.ds(k0, BAND_KEYS), :], _NT, preferred_element_type=F32) for do2, (_, k0, _) in zip(do2s, geo)]
            ps = [_band_softmax(s, jnp.concatenate([valid, valid], axis=0))[0] for s, (_, _, valid) in zip(ss, geo)]
            dss = []
            for p, dp, (q0, _, _) in zip(ps, dps, geo):
                dl = dlse_ref[pl.ds(q0, BAND_Q), :]
                dl2 = jnp.concatenate([dl[:, :1], dl[:, HEAD_DIM:HEAD_DIM + 1]], axis=0)
                dss.append(p * (dp - jnp.sum(dp * p, axis=-1, keepdims=True) + dl2))
            dvs = [lax.dot_general(p.astype(BF), do2, _TN, preferred_element_type=F32) for p, do2 in zip(ps, do2s)]
            dsbs = [ds.astype(BF) for ds in dss]
            dqs = [jnp.dot(dsb, k_ref[pl.ds(k0, BAND_KEYS), :], preferred_element_type=F32) for dsb, (_, k0, _) in zip(dsbs, geo)]
            dks = [lax.dot_general(dsb, q2, _TN, preferred_element_type=F32) for dsb, q2 in zip(dsbs, q2s)]
            for u, (q0, k0, _) in enumerate(geo):
                dq_ref[pl.ds(q0, BAND_Q), :] = _join_pair(dqs[u][:BAND_Q], dqs[u][BAND_Q:])
                dk_ref[pl.ds(k0, BAND_KEYS), :] += dks[u]
                dv_ref[pl.ds(k0, BAND_KEYS), :] += dvs[u]
            return carry

        lax.fori_loop(0, L // (BAND_Q * U), step, 0)

    res = _residue_shape(DIL_GROUPS[group][1], T, F32)
    return pl.pallas_call(
        body, name=f"band_bwd_g{group}", grid=grid,
        in_specs=[spec] * 5, out_specs=[spec] * 3, out_shape=[res] * 3,
        compiler_params=_cparams(("parallel",)),
    )(q, k, v, do, dlse)


def _head_sums(t):
    head = lax.broadcasted_iota(jnp.int32, t.shape, 1) // HEAD_DIM
    out = jnp.zeros_like(t)
    for h in range(t.shape[1] // HEAD_DIM):
        mine = head == h
        out = jnp.where(mine, jnp.sum(jnp.where(mine, t, 0.0), axis=-1, keepdims=True), out)
    return out


def _dil_merge_fwd(os, lses, T, tm):
    G = len(DIL_GROUPS)
    W = DIL_OUT_WIDTH
    dils = [d for _, d in DIL_GROUPS]

    def body(*refs):
        o_refs, lse_refs = refs[:G], refs[G:2 * G]
        y_ref, w_refs, on_refs, scratch = refs[2 * G], refs[2 * G + 1:3 * G + 1], refs[3 * G + 1:4 * G + 1], refs[-1]
        o = [jnp.concatenate(_to_natural(r, scratch, d, tm), axis=1) for r, d in zip(o_refs, dils)]
        ls = [jnp.concatenate(_to_natural(r, scratch, d, tm), axis=1) for r, d in zip(lse_refs, dils)]
        m = functools.reduce(jnp.maximum, ls)
        es = [jnp.exp(l - m) for l in ls]
        tot = functools.reduce(jnp.add, es)
        ws = [e / tot for e in es]
        y_ref[...] = functools.reduce(jnp.add, [w * t for w, t in zip(ws, o)]).astype(y_ref.dtype)
        for g in range(G):
            w_refs[g][...] = ws[g]
            on_refs[g][...] = o[g]

    nat = pl.BlockSpec((tm, W), lambda i: (i, 0))
    res = pl.pallas_call(
        body, name="dil_merge_fwd", grid=(T // tm,),
        in_specs=[_residue_tile(d, tm) for d in dils] * 2,
        out_specs=[nat] * (2 * G + 1),
        out_shape=[jax.ShapeDtypeStruct((T, W), BF)] + [jax.ShapeDtypeStruct((T, W), F32)] * (2 * G),
        scratch_shapes=[pltpu.VMEM((tm, 128), F32)],
        compiler_params=_cparams(("parallel",)),
    )(*os, *lses)
    return res[0], res[1:G + 1], res[G + 1:]


def _dil_merge_bwd(dy, os, ws, tm):
    G = len(DIL_GROUPS)
    T, W = dy.shape
    dils = [d for _, d in DIL_GROUPS]

    def body(*refs):
        dyt = refs[0][...]
        o, w = [r[...] for r in refs[1:G + 1]], [r[...] for r in refs[G + 1:2 * G + 1]]
        do_refs, dlse_refs, scratch = refs[2 * G + 1:3 * G + 1], refs[3 * G + 1:4 * G + 1], refs[-1]
        dws = [_head_sums(dyt * t) for t in o]
        mean = functools.reduce(jnp.add, [a * b for a, b in zip(w, dws)])
        for g, d in enumerate(dils):
            do, dlse = w[g] * dyt, w[g] * (dws[g] - mean)
            for pair in range(DIL_PAIRS):
                cols = slice(pair * 128, (pair + 1) * 128)
                _from_natural(do[:, cols], scratch, do_refs[g], pair, d, tm)
                _from_natural(dlse[:, cols], scratch, dlse_refs[g], pair, d, tm)

    nat = pl.BlockSpec((tm, W), lambda i: (i, 0))
    res = pl.pallas_call(
        body, name="dil_merge_bwd", grid=(T // tm,),
        in_specs=[nat] * (2 * G + 1),
        out_specs=[_residue_tile(d, tm) for d in dils] * 2,
        out_shape=[_residue_shape(d, T, BF) for d in dils] + [_residue_shape(d, T, F32) for d in dils],
        scratch_shapes=[pltpu.VMEM((tm, 128), F32)],
        compiler_params=_cparams(("parallel",)),
    )(dy, *os, *ws)
    return res[:G], res[G:]


def _qkv_prep(z, cos2, sin_signed, tm):
    T = z.shape[0]
    G = len(DIL_GROUPS)
    dils = [d for _, d in DIL_GROUPS]
    n_dil_blocks = 3 * DIL_WIDTH // 128

    def body(*refs):
        blocks = refs[:n_dil_blocks]
        cos_ref, sin_ref = refs[n_dil_blocks], refs[1 + n_dil_blocks]
        outs = refs[2 + n_dil_blocks:]
        for part in range(3):
            for g, d in enumerate(dils):
                out = outs[g * 3 + part]
                for pair in range(DIL_PAIRS):
                    blk = blocks[part * (DIL_WIDTH // 128) + g * DIL_PAIRS + pair]
                    for r in range(d):
                        rows = pl.ds(r, tm // d, stride=d) if d > 1 else slice(None)
                        x = blk[rows, :]
                        if part < 2:
                            x = _rope(x, cos_ref[rows, :], sin_ref[rows, :])
                        if part == 0:
                            x = x * Q_SCALE
                        out[pair, r] = x.astype(out.dtype)

    lane_block = [pl.BlockSpec((tm, 128), functools.partial(lambda b, i: (i, b), b)) for b in range(n_dil_blocks)]
    tab = pl.BlockSpec((tm, 128), lambda i: (i, 0))
    res = pl.pallas_call(
        body, name="qkv_prep", grid=(T // tm,),
        in_specs=lane_block + [tab, tab],
        out_specs=[_residue_tile(d, tm) for d in dils for _ in range(3)],
        out_shape=[_residue_shape(d, T, BF) for d in dils for _ in range(3)],
        compiler_params=_cparams(("parallel",)),
    )(*[z] * n_dil_blocks, cos2, sin_signed)
    return [res[3 * g:3 + 3 * g] for g in range(G)]


def _qkv_unprep(d_na, d_dil, cos2, sin_signed, tm, after=()):
    T = d_na[0].shape[0]
    G = len(DIL_GROUPS)
    dils = [d for _, d in DIL_GROUPS]
    n_after = len(after)

    def body(*refs):
        dq, dk, dv = (r[...] for r in refs[:3])
        res_refs = refs[3:3 + 3 * G]
        cs, sn = refs[3 + 3 * G][...], refs[4 + 3 * G][...]
        out, scratch = refs[5 + 3 * G + n_after], refs[-1]
        cols = [dq * Q_SCALE, dk, dv]
        for part in range(3):
            for g, d in enumerate(dils):
                for x in _to_natural(res_refs[g * 3 + part], scratch, d, tm):
                    if part < 2:
                        x = _rope(x, cs, -sn)
                    cols.append(x * Q_SCALE if part == 0 else x)
        out[...] = jnp.concatenate(cols, axis=1).astype(out.dtype)

    wide = pl.BlockSpec((tm, NA_WIDTH), lambda i: (i, 0))
    tab = pl.BlockSpec((tm, 128), lambda i: (i, 0))
    return pl.pallas_call(
        body, name="qkv_unprep", grid=(T // tm,),
        in_specs=[wide] * 3 + [_residue_tile(d, tm) for d in dils for _ in range(3)] + [tab, tab] + [ANY] * n_after,
        out_specs=pl.BlockSpec((tm, QKV_WIDTH), lambda i: (i, 0)),
        out_shape=jax.ShapeDtypeStruct((T, QKV_WIDTH), BF),
        scratch_shapes=[pltpu.VMEM((tm, 128), F32)],
        compiler_params=_cparams(("parallel",)),
    )(*d_na, *[t for g in range(G) for t in d_dil[g]], cos2, sin_signed, *after)


def _rope_tables(positions):
    half = HEAD_DIM // 2
    inv_freq = ROPE_THETA ** (-jnp.arange(half, dtype=F32) / half)
    ang = positions.astype(F32)[:, None] * inv_freq
    cos, sin = jnp.cos(ang), jnp.sin(ang)
    return jnp.tile(jnp.concatenate([cos, cos], axis=1), (1, 2)), jnp.tile(jnp.concatenate([-sin, sin], axis=1), (1, 2))


def _pack_rows(t):
    return t.reshape(-1, PACK_W)


def _me():
    return lax.axis_index("x"), lax.axis_index("y"), lax.axis_index("c")


def _other_chips(x, y):
    return [(1 - x, y), (x, 1 - y), (1 - x, 1 - y)]


def _gather_weights(packed):
    R, W = packed.shape
    half = R // 2

    def body(in_ref, out_ref, send_sems, recv_sems):
        x, y, c = _me()
        sibling = (x, y, 1 - c)
        chips = _other_chips(x, y)

        def block(chip, core):
            return out_ref.at[2 * chip[0] + chip[1], pl.ds(core * half, half), :]

        def copy(k, chip, core, to, src=None):
            return pltpu.make_async_remote_copy(
                src_ref=block(chip, core) if src is None else src, dst_ref=block(chip, core),
                send_sem=send_sems.at[k], recv_sem=recv_sems.at[k], device_id=to, device_id_type=MESH)

        first = [copy(j, (x, y), c, (*chip, c), src=in_ref.at[pl.ds(c * half, half), :]) for j, chip in enumerate(chips)]
        for cp in first:
            cp.start()
        passed = [copy(3 + j, chip, c, sibling) for j, chip in enumerate(chips)]
        for j, chip in enumerate(chips):
            copy(j, chip, c, (x, y, c)).wait_recv()
            passed[j].start()
        for j, chip in enumerate(chips):
            copy(3 + j, chip, 1 - c, (x, y, c)).wait_recv()
        for cp in first + passed:
            cp.wait_send()

    others = pl.pallas_call(
        body, name="gather_weights",
        in_specs=[ANY], out_specs=ANY,
        out_shape=jax.ShapeDtypeStruct((N_CHIPS, R, W), packed.dtype),
        scratch_shapes=[pltpu.SemaphoreType.DMA((6,)), pltpu.SemaphoreType.DMA((6,))],
    )(packed)
    return lax.dynamic_update_slice(others, packed[None], (2 * lax.axis_index("x") + lax.axis_index("y"), 0, 0))


def _swap_halves(g):
    S, R, W = g.shape
    half = R // 2

    def body(g_ref, out_ref, send_sem, recv_sem):
        x, y, c = _me()
        cp = pltpu.make_async_remote_copy(
            src_ref=g_ref.at[:, pl.ds((1 - c) * half, half), :], dst_ref=out_ref,
            send_sem=send_sem, recv_sem=recv_sem, device_id=(x, y, 1 - c), device_id_type=MESH)
        cp.start()
        cp.wait()

    return pl.pallas_call(
        body, name="swap_halves", in_specs=[ANY], out_specs=ANY,
        out_shape=jax.ShapeDtypeStruct((S, half, W), g.dtype),
        scratch_shapes=[pltpu.SemaphoreType.DMA, pltpu.SemaphoreType.DMA],
    )(g)


def _pair_sum(g, got, tm):
    S, R, W = g.shape
    half = R // 2
    nb = half // tm

    def body(c_ref, g_ref, got_ref, o_ref, ob_ref):
        tot = g_ref[...] + got_ref[...]
        o_ref[...] = tot
        ob_ref[...] = tot.astype(ob_ref.dtype)

    tile = pl.BlockSpec((None, tm, W), lambda s, i, c_ref: (s, i, 0))
    return pl.pallas_call(
        body, name="pair_sum",
        grid_spec=pltpu.PrefetchScalarGridSpec(
            num_scalar_prefetch=1, grid=(S, nb),
            in_specs=[pl.BlockSpec((None, tm, W), lambda s, i, c_ref: (s, c_ref[0] * nb + i, 0)), tile],
            out_specs=[tile, tile]),
        out_shape=[jax.ShapeDtypeStruct((S, half, W), F32), jax.ShapeDtypeStruct((S, half, W), BF)],
        compiler_params=_cparams(("parallel", "parallel")),
    )(lax.axis_index("c").reshape(1).astype(jnp.int32), g, got)


def _scatter_chips(part):
    S, h, W = part.shape

    def body(p_ref, out_ref, send_sems, recv_sems):
        x, y, c = _me()
        chips = _other_chips(x, y)
        sends = [pltpu.make_async_remote_copy(
            src_ref=p_ref.at[2 * chip[0] + chip[1]], dst_ref=out_ref.at[j],
            send_sem=send_sems.at[j], recv_sem=recv_sems.at[j], device_id=(*chip, c), device_id_type=MESH)
            for j, chip in enumerate(chips)]
        for cp in sends:
            cp.start()
        for cp in sends:
            cp.wait()

    return pl.pallas_call(
        body, name="scatter_chips", in_specs=[ANY], out_specs=ANY,
        out_shape=jax.ShapeDtypeStruct((S - 1, h, W), part.dtype),
        scratch_shapes=[pltpu.SemaphoreType.DMA((3,)), pltpu.SemaphoreType.DMA((3,))],
    )(part)


def _chip_sum(own, others, tm):
    n, h, W = others.shape

    def body(own_ref, p_ref, o_ref):
        o_ref[...] = ((own_ref[...] + p_ref[0].astype(F32)) + p_ref[1].astype(F32)) + p_ref[2].astype(F32)

    return pl.pallas_call(
        body, name="chip_sum", grid=(h // tm,),
        in_specs=[pl.BlockSpec((tm, W), lambda i: (i, 0)), pl.BlockSpec((n, tm, W), lambda i: (0, i, 0))],
        out_specs=pl.BlockSpec((tm, W), lambda i: (i, 0)),
        out_shape=jax.ShapeDtypeStruct((h, W), F32),
        compiler_params=_cparams(("parallel",)),
    )(own, others)


def _join_halves(mine):
    h, W = mine.shape

    def body(m_ref, out_ref, send_sem, recv_sem):
        x, y, c = _me()
        cp = pltpu.make_async_remote_copy(
            src_ref=m_ref, dst_ref=out_ref.at[pl.ds(c * h, h), :],
            send_sem=send_sem, recv_sem=recv_sem, device_id=(x, y, 1 - c), device_id_type=MESH)
        cp.start()
        pltpu.make_async_remote_copy(
            src_ref=m_ref, dst_ref=out_ref.at[pl.ds((1 - c) * h, h), :],
            send_sem=send_sem, recv_sem=recv_sem, device_id=(x, y, 1 - c), device_id_type=MESH).wait_recv()
        cp.wait_send()

    other = pl.pallas_call(
        body, name="join_halves", in_specs=[ANY], out_specs=ANY,
        out_shape=jax.ShapeDtypeStruct((2 * h, W), mine.dtype),
        scratch_shapes=[pltpu.SemaphoreType.DMA, pltpu.SemaphoreType.DMA],
    )(mine)
    return lax.dynamic_update_slice(other, mine, (lax.axis_index("c") * h, 0))


def _allreduce_small(s, after=()):
    R, W = s.shape
    n_after = len(after)

    def body(s_ref, *rest):
        o_ref, buf, send_sems, recv_sems = rest[n_after:]
        x, y, c = _me()
        me = 4 * x + 2 * y + c
        buf[me] = s_ref[...]
        peers = [((x + fx) % 2, (y + fy) % 2, (c + fc) % 2) for fx in range(2) for fy in range(2) for fc in range(2)][1:]
        sends = [pltpu.make_async_remote_copy(
            src_ref=s_ref, dst_ref=buf.at[me], send_sem=send_sems.at[k], recv_sem=recv_sems.at[k],
            device_id=peer, device_id_type=MESH) for k, peer in enumerate(peers)]
        for cp in sends:
            cp.start()
        for k, peer in enumerate(peers):
            pltpu.make_async_remote_copy(
                src_ref=s_ref, dst_ref=buf.at[4 * peer[0] + 2 * peer[1] + peer[2]], send_sem=send_sems.at[k],
                recv_sem=recv_sems.at[k], device_id=peer, device_id_type=MESH).wait_recv()
        for cp in sends:
            cp.wait_send()
        total = buf[0]
        for d in range(1, N_DEV):
            total = total + buf[d]
        o_ref[...] = total

    return pl.pallas_call(
        body, name="allreduce_small",
        in_specs=[pl.BlockSpec(memory_space=pltpu.VMEM)] + [ANY] * n_after, out_specs=pl.BlockSpec(memory_space=pltpu.VMEM),
        out_shape=jax.ShapeDtypeStruct((R, W), F32),
        scratch_shapes=[pltpu.VMEM((N_DEV, R, W), F32), pltpu.SemaphoreType.DMA((N_DEV - 1,)), pltpu.SemaphoreType.DMA((N_DEV - 1,))],
    )(s, *after)


HBM_SPEC = pl.BlockSpec(memory_space=pltpu.HBM)
SEM_SPEC = pl.BlockSpec(memory_space=pltpu.SEMAPHORE)
DATAFLOW = pltpu.SideEffectType.DATAFLOW_SIDE_EFFECTING


class _InFlight(NamedTuple):
    sems: tuple
    src: jax.Array
    land: jax.Array
    token: jax.Array


def _split_start(name, src, land_shape, land_dtype, n, copies, after=()):
    n_after = len(after)

    def body(src_ref, land_ref, *rest):
        rest = rest[n_after:]
        sems, token = rest[:2 * n], rest[-1]
        for k, (s, d, peer) in enumerate(copies(src_ref, land_ref)):
            pltpu.make_async_remote_copy(src_ref=s, dst_ref=d, send_sem=sems[k], recv_sem=sems[n + k],
                                         device_id=peer, device_id_type=MESH).start()
        token[...] = jnp.zeros_like(token)

    outs = pl.pallas_call(
        body, name=name,
        out_shape=(*[pltpu.SemaphoreType.DMA(())] * (2 * n), pltpu.HBM(src.shape, src.dtype), pltpu.HBM(land_shape, land_dtype),
                   jax.ShapeDtypeStruct((8, 128), F32)),
        in_specs=(HBM_SPEC, HBM_SPEC, *[ANY] * n_after),
        out_specs=(*[SEM_SPEC] * (2 * n), HBM_SPEC, HBM_SPEC, pl.BlockSpec(memory_space=pltpu.VMEM)),
        input_output_aliases={0: 2 * n, 1: 2 * n + 1},
        compiler_params=pltpu.CompilerParams(has_side_effects=DATAFLOW),
    )(pltpu.with_memory_space_constraint(src, pltpu.HBM), pltpu.with_memory_space_constraint(lax.empty(land_shape, land_dtype), pltpu.HBM),
      *after)
    return _InFlight(tuple(outs[:2 * n]), outs[2 * n], outs[2 * n + 1], outs[2 * n + 2])


def _split_wait(name, flight, after, n, copies):
    def body(src_ref, land_ref, *rest):
        sems = rest[:2 * n]
        for k, (s, d, peer) in enumerate(copies(src_ref, land_ref)):
            cp = pltpu.make_async_remote_copy(src_ref=s, dst_ref=d, send_sem=sems[k], recv_sem=sems[n + k],
                                              device_id=peer, device_id_type=MESH)
            cp.wait_send()
            cp.wait_recv()

    return pl.pallas_call(
        body, name=name,
        out_shape=(pltpu.HBM(flight.src.shape, flight.src.dtype), pltpu.HBM(flight.land.shape, flight.land.dtype)),
        in_specs=(HBM_SPEC, HBM_SPEC, *[SEM_SPEC] * (2 * n), ANY),
        out_specs=(HBM_SPEC, HBM_SPEC), input_output_aliases={0: 0, 1: 1},
        compiler_params=pltpu.CompilerParams(has_side_effects=DATAFLOW),
    )(flight.src, flight.land, *flight.sems, after)


def _gather_copies(src_ref, land_ref):
    x, y, c = _me()
    return [(src_ref, land_ref.at[2 * x + y], (*chip, c)) for chip in _other_chips(x, y)]


def _gather_start(packed, tag, after=()):
    return _split_start(f"gather_start_{tag}", packed, (N_CHIPS, *packed.shape), packed.dtype, 3, _gather_copies, after)


def _gather_wait(flight, after, tag):
    src, others = _split_wait(f"gather_wait_{tag}", flight, after, 3, _gather_copies)
    return lax.dynamic_update_slice(others, src[None], (2 * lax.axis_index("x") + lax.axis_index("y"), 0, 0))


def _swap_copies(src_ref, land_ref):
    x, y, c = _me()
    half = land_ref.shape[1]
    return [(src_ref.at[:, pl.ds((1 - c) * half, half), :], land_ref, (x, y, 1 - c))]


def _swap_start(g, tag):
    S, R, W = g.shape
    return _split_start(f"swap_halves_start_{tag}", g, (S, R // 2, W), g.dtype, 1, _swap_copies)


def _swap_wait(flight, after, tag):
    return _split_wait(f"swap_halves_wait_{tag}", flight, after, 1, _swap_copies)


def _scatter_copies(src_ref, land_ref):
    x, y, c = _me()
    return [(src_ref.at[2 * chip[0] + chip[1]], land_ref.at[j], (*chip, c)) for j, chip in enumerate(_other_chips(x, y))]


def _scatter_start(part, tag):
    S, h, W = part.shape
    return _split_start(f"scatter_chips_start_{tag}", part, (S - 1, h, W), part.dtype, 3, _scatter_copies)


def _scatter_wait(flight, after, tag):
    return _split_wait(f"scatter_chips_wait_{tag}", flight, after, 3, _scatter_copies)[1]


def _join_copies(src_ref, land_ref):
    x, y, c = _me()
    h = src_ref.shape[0]
    return [(src_ref, land_ref.at[pl.ds(c * h, h), :], (x, y, 1 - c))]


def _join_start(mine):
    h, W = mine.shape
    return _split_start("join_halves_start", mine, (2 * h, W), mine.dtype, 1, _join_copies)


def _join_wait(flight, after):
    src, other = _split_wait("join_halves_wait", flight, after, 1, _join_copies)
    return lax.dynamic_update_slice(other, src, (lax.axis_index("c") * src.shape[0], 0))


def _adamw(name, g, w, m, v):
    R, C = w.shape
    tm = R
    for cand in (256, 128, 64, 32, 16, 8):
        if R % cand == 0:
            tm = cand
            break

    def body(g, w, m, v):
        m = ADAM_B1 * m + (1.0 - ADAM_B1) * g
        v = ADAM_B2 * v + (1.0 - ADAM_B2) * jnp.square(g)
        m_hat = m / (1.0 - ADAM_B1 ** ADAM_STEP)
        v_hat = v / (1.0 - ADAM_B2 ** ADAM_STEP)
        delta = -ADAM_LR * (m_hat / (jnp.sqrt(v_hat) + ADAM_EPS) + ADAM_WD * w)
        return delta, m, v

    return _rowwise(name, body, R, tm, [_row(t, tm) for t in (g, w, m, v)], [(C, F32)] * 3)


def _unpack_weights(gathered, names):
    S = gathered.shape[0]
    shard_shapes = {"w_in": (D_MODEL, (QKV_WIDTH + 2 * D_MODEL) // S), "w_branch_na": (NA_WIDTH, D_MODEL // S),
                    "w_branch_dil": (DIL_OUT_WIDTH, D_MODEL // S), "w_out": (D_MODEL // S, D_MODEL),
                    "w_up": (D_MODEL, D_FF // S), "w_down": (D_FF // S, D_MODEL),
                    "w_ple_gate": (D_MODEL // S, D_MODEL), "w_ple_proj": (PLE_DIM, D_MODEL // S)}
    col_sharded = {"w_in", "w_branch_na", "w_branch_dil", "w_up", "w_ple_proj"}
    out, r0 = {}, 0
    for name in names:
        rows, cols = shard_shapes[name]
        n = rows * cols // PACK_W
        t = gathered[:, r0:r0 + n, :].reshape(S, rows, cols)
        r0 += n
        out[name] = t.transpose(1, 0, 2).reshape(rows, S * cols) if name in col_sharded else t.reshape(S * rows, cols)
    return out


def _pack_grads(grads, names):
    col_sharded = {"w_in", "w_branch_na", "w_branch_dil", "w_up", "w_ple_proj"}
    per_chip = []
    for s in range(N_CHIPS):
        rows = []
        for name in names:
            g = grads[name]
            if name in col_sharded:
                w = g.shape[1] // N_CHIPS
                rows.append(_pack_rows(g[:, s * w:(s + 1) * w]))
            else:
                h = g.shape[0] // N_CHIPS
                rows.append(_pack_rows(g[s * h:(s + 1) * h]))
        per_chip.append(jnp.concatenate(rows, axis=0))
    return jnp.stack(per_chip)


def _unpack_shard(packed, shapes, names):
    out, r0 = {}, 0
    for name in names:
        rows, cols = shapes[name]
        n = rows * cols // PACK_W
        out[name] = packed[r0:r0 + n].reshape(rows, cols)
        r0 += n
    return out


def kernel(x, p, positions, g_mix, w_in, rpb, w_branch_na, w_branch_dil, w_out, g_mlp, w_up, w_down, g_ple, w_ple_gate, w_ple_proj, g_final, loss_target, m_g_mix, m_w_in, m_rpb, m_w_branch_na, m_w_branch_dil, m_w_out, m_g_mlp, m_w_up, m_w_down, m_g_ple, m_w_ple_gate, m_w_ple_proj, m_g_final, v_g_mix, v_w_in, v_rpb, v_w_branch_na, v_w_branch_dil, v_w_out, v_g_mlp, v_w_up, v_w_down, v_g_ple, v_w_ple_gate, v_w_ple_proj, v_g_final):
    shards = {"w_in": w_in[0], "w_branch_na": w_branch_na[0], "w_branch_dil": w_branch_dil[0], "w_out": w_out[0],
              "w_up": w_up[0], "w_down": w_down[0], "w_ple_gate": w_ple_gate[0], "w_ple_proj": w_ple_proj[0]}
    m_shards = {"w_in": m_w_in[0], "w_branch_na": m_w_branch_na[0], "w_branch_dil": m_w_branch_dil[0], "w_out": m_w_out[0],
                "w_up": m_w_up[0], "w_down": m_w_down[0], "w_ple_gate": m_w_ple_gate[0], "w_ple_proj": m_w_ple_proj[0]}
    v_shards = {"w_in": v_w_in[0], "w_branch_na": v_w_branch_na[0], "w_branch_dil": v_w_branch_dil[0], "w_out": v_w_out[0],
                "w_up": v_w_up[0], "w_down": v_w_down[0], "w_ple_gate": v_w_ple_gate[0], "w_ple_proj": v_w_ple_proj[0]}

    W = _unpack_weights(_gather_weights(_pack_rows(shards["w_in"].astype(BF))), GATHER_FIRST)
    mix_flight = _gather_start(jnp.concatenate([_pack_rows(shards[n].astype(BF)) for n in GATHER_MIX], axis=0), "mix")
    rest_flight = _gather_start(jnp.concatenate([_pack_rows(shards[n].astype(BF)) for n in GATHER_MLP], axis=0), "mlp",
                                after=(mix_flight.token,))
    w_qkv, w_gates = W["w_in"][:, :QKV_WIDTH], W["w_in"][:, QKV_WIDTH:]

    xs, ps, tgt = x[0], p[0, 0], loss_target[0]
    T = xs.shape[0]
    TM = 512
    gm, gl, gp, gf = g_mix, g_mlp, g_ple, g_final.reshape(1, D_MODEL)
    cos2, sin_signed = _rope_tables(positions[0])
    tab = _na_bias_table(rpb[0])

    a = _rowwise("norm_mix", lambda h, g: h * _rms(h) * g, T, TM, [_row(xs, TM), _full(gm)], [(D_MODEL, BF)],
                 after=(rest_flight.token,))
    n3 = 3 * NA_WIDTH
    qkv = _mm("in_na", a, w_qkv[:, :n3], "nn", 1024, 768, 1024, [BF])
    z_dil = _mm("in_dil", a, w_qkv[:, n3:], "nn", 1024, 1152, 1024, [F32])
    z_gates = _mm("in_gates", a, w_gates, "nn", 1024,1024, 1024, [BF])

    dil_ops = _qkv_prep(z_dil, cos2, sin_signed, TM)
    y_na = _na_fwd(qkv, tab)
    band = [_band_fwd(*dil_ops[g], g) for g in range(len(DIL_GROUPS))]
    y_dil, w_grp, o_nat = _dil_merge_fwd([b[0] for b in band], [b[1] for b in band], T, TM)

    W.update(_unpack_weights(_gather_wait(mix_flight, y_dil, "mix"), GATHER_MIX))
    u_na = _mm("branch_na", y_na, W["w_branch_na"], "nn", 1024,1024, 512, [BF])
    u_dil = _mm("branch_dil", y_dil, W["w_branch_dil"], "nn", 1024,1024, 256, [BF])
    mixed = _rowwise(
        "gate_mix", lambda gn, gd, un, ud: _sigmoid(gn.astype(F32)) * un.astype(F32) + _sigmoid(gd.astype(F32)) * ud.astype(F32), T, TM,
        [_row(z_gates, TM, 0, D_MODEL), _row(z_gates, TM, 1, D_MODEL), _row(u_na, TM), _row(u_dil, TM)], [(D_MODEL, BF)])
    def add_norm(d, h, g):
        h = h + d
        return h, h * _rms(h) * g

    h1, cn = _mm("out_proj", mixed, W["w_out"], "nn", 512, 1024, 1024, [F32, BF], epilogue=add_norm, extras=(xs,), consts=(gl,))
    W.update(_unpack_weights(_gather_wait(rest_flight, cn, "mlp"), GATHER_MLP))
    up, act = _mm("mlp_up", cn, W["w_up"], "nn", 1024,1024, 1024, [BF, BF],
                  epilogue=lambda acc: (acc, jnp.square(jnp.maximum(acc, 0.0))))
    h2, en = _mm("mlp_down", act, W["w_down"], "nn", 1024, 1024, 1024, [F32, BF], epilogue=add_norm, extras=(h1,), consts=(gp,))
    gt = _mm("ple_gate", en, W["w_ple_gate"], "nn", 1024,1024, 1024, [F32])
    pp = _mm("ple_proj", ps, W["w_ple_proj"], "nn", 1024,1024, 256, [F32])

    def head(h2t, gtt, ppt, tg, g):
        sg = _sigmoid(gtt)
        h3 = h2t + sg * ppt
        yo = h3 * _rms(h3) * g
        diff = yo - tg
        loss = 0.5 * jnp.sum(jnp.mean(jnp.square(diff), axis=-1, keepdims=True), axis=0, keepdims=True)
        dh3, dg = _rms_bwd(diff * (1.0 / D_MODEL), h3, g)
        return dh3, dh3 * ppt * sg * (1.0 - sg), dh3 * sg, jnp.broadcast_to(loss, (1, 128)), dg

    dh3, d_gt, d_pp, loss_part, dg_final = _rowwise(
        "loss_head", head, T, TM, [_row(h2, TM), _row(gt, TM), _row(pp, TM), _row(tgt, TM), _full(gf)],
        [(D_MODEL, F32), (D_MODEL, BF), (D_MODEL, BF)], sums=[128, D_MODEL])

    grads = {}
    early_shapes = {n: shards[n].shape for n in REDUCE_EARLY}
    early_rows = sum(r * c for r, c in early_shapes.values()) // PACK_W
    shard_rows = D_MODEL // N_CHIPS
    early_buf = _mm("g_ple_gate", en, d_gt, "tn", shard_rows, 1024, 1024, [F32],
                    into=(jax.ShapeDtypeStruct((N_CHIPS, early_rows, PACK_W), F32), (None, shard_rows, PACK_W),
                          lambda i, j: (i, 2 * D_MODEL // shard_rows, 0)))
    g_ple_proj = _mm("g_ple_proj", ps, d_pp, "tn", 256, 1024, 1024,[F32])
    early_buf = lax.dynamic_update_slice(
        early_buf, jnp.stack([_pack_rows(g_ple_proj[:, s * shard_rows:(s + 1) * shard_rows]) for s in range(N_CHIPS)]),
        (0, 2 * D_MODEL + shard_rows, 0))
    def add_norm_bwd(dn, dh_out, h, g):
        dh, dg = _rms_bwd(dn, h, g)
        dh = dh_out + dh
        return dh, dh, dg

    dh2, dh2_b, dg_ple = _mm("d_ple_gate", d_gt, W["w_ple_gate"], "nt", 512, 1024, 1024, [F32, BF],
                             epilogue=add_norm_bwd, extras=(dh3, h2), consts=(gp,), sums=[D_MODEL])
    d_up = _mm("d_mlp_down", dh2_b, W["w_down"], "nt", 1024,1024, 1024, [BF],
               epilogue=lambda acc, u: (acc * (2.0 * jnp.maximum(u.astype(F32), 0.0)),), extras=(up,))
    early_buf = _mm("g_mlp_down", act, dh2_b, "tn", 1024, 1024, 1024,[F32],
                    into=(early_buf, (None, D_MODEL, PACK_W), lambda i, j: (i, 1, 0)))
    early_buf = _mm("g_mlp_up", cn, d_up, "tn", 1024, 1024, 1024,[F32],
                    into=(early_buf, (None, D_MODEL, PACK_W), lambda i, j: (j, 0, 0)))
    early_tm = early_rows // 4
    swap_flight = _swap_start(early_buf, "early")
    dh1, dh1_b, dg_mlp = _mm("d_mlp_up", d_up, W["w_up"], "nt", 1024, 1024, 1024, [F32, BF], epilogue=add_norm_bwd,
                             extras=(dh2, h1), consts=(gl,), sums=[D_MODEL], after=(swap_flight.token,))
    early_g, early_got = _swap_wait(swap_flight, dh1_b, "early")
    early_pair, early_pair_b = _pair_sum(early_g, early_got, early_tm)
    scatter_flight = _scatter_start(early_pair_b, "early")
    d_mixed = _mm("d_out_proj", dh1_b, W["w_out"], "nt", 1024,1024, 1024, [F32], after=(scatter_flight.token,))
    grads["w_out"] = _mm("g_out_proj", mixed, dh1_b, "tn", 1024, 1024, 1024,[F32])

    def gate_bwd(dm, gn, gd, un, ud):
        gn, gd, un, ud = (t.astype(F32) for t in (gn, gd, un, ud))
        sn, sd = _sigmoid(gn), _sigmoid(gd)
        return jnp.concatenate([dm * un * sn * (1.0 - sn), dm * ud * sd * (1.0 - sd)], axis=1), dm * sn, dm * sd

    dz_gates, d_u_na, d_u_dil = _rowwise(
        "gate_mix_bwd", gate_bwd, T, TM,
        [_row(d_mixed, TM), _row(z_gates, TM, 0, D_MODEL), _row(z_gates, TM, 1, D_MODEL), _row(u_na, TM), _row(u_dil, TM)],
        [(2 * D_MODEL, BF), (D_MODEL, BF), (D_MODEL, BF)])
    grads["w_branch_na"] = _mm("g_branch_na", y_na, d_u_na, "tn", 1024, 1024, 1024,[F32])
    grads["w_branch_dil"] = _mm("g_branch_dil", y_dil, d_u_dil, "tn", 256, 1024, 1024,[F32])
    d_y_na = _mm("d_branch_na", d_u_na, W["w_branch_na"], "nt", 1024,512, 1024, [BF])
    d_y_dil = _mm("d_branch_dil", d_u_dil, W["w_branch_dil"], "nt", 1024,256, 1024, [F32])

    dqa, dka, dva, dtab = _na_bwd(qkv, tab, d_y_na)
    d_rpb = _na_rpb_grad(dtab)[:, :2 * NA_WIN_ROWS - 1, :2 * NA_WIN_COLS - 1]

    do_res, dlse_res = _dil_merge_bwd(d_y_dil, o_nat, w_grp, TM)
    d_dil = [_band_bwd(*dil_ops[g], do_res[g], dlse_res[g], g) for g in range(len(DIL_GROUPS))]

    me_chip = 2 * lax.axis_index("x") + lax.axis_index("y")
    early_mine = _chip_sum(lax.dynamic_index_in_dim(early_pair, me_chip, 0, keepdims=False),
                           _scatter_wait(scatter_flight, d_dil[-1][-1], "early"), early_tm)
    join_flight = _join_start(early_mine)
    dz_qkv = _qkv_unprep((dqa, dka, dva), d_dil, cos2, sin_signed, TM, after=(join_flight.token,))
    grads["w_in"] = jnp.concatenate([
        _mm("g_in_qkv", a, dz_qkv, "tn", 1024, 1280, 1024,[F32]),
        _mm("g_in_gates", a, dz_gates, "tn", 1024, 1024, 1024,[F32])], axis=1)
    late_shapes = {n: shards[n].shape for n in REDUCE_LATE}
    late_tm = sum(r * c for r, c in late_shapes.values()) // PACK_W // 4
    late_swap = _swap_start(_pack_grads(grads, REDUCE_LATE), "late")
    d_a = _mm("d_in_qkv", dz_qkv, w_qkv, "nt", 1024,1024, 1280, [F32], after=(late_swap.token,))
    late_g, late_got = _swap_wait(late_swap, d_a, "late")
    late_pair, late_pair_b = _pair_sum(late_g, late_got, late_tm)
    late_scatter = _scatter_start(late_pair_b, "late")
    def first_bwd(dn_gates, dn_qkv, dh_out, h, g):
        dh, dg = _rms_bwd(dn_gates + dn_qkv, h, g)
        return dh_out + dh, dg

    grad_x, dg_mix = _mm("d_in_gates", dz_gates, w_gates, "nt", 512, 1024, 1024, [F32], epilogue=first_bwd,
                         extras=(d_a, dh1, xs), consts=(gm,), sums=[D_MODEL], after=(late_scatter.token,))
    g_shard = _unpack_shard(_join_wait(join_flight, grad_x), early_shapes, REDUCE_EARLY)

    n_rpb = rpb.size
    rpb_rows = 4
    small = jnp.concatenate([
        dg_mix, dg_mlp, dg_ple, dg_final,
        jnp.pad(d_rpb.reshape(-1), (0, rpb_rows * D_MODEL - n_rpb)).reshape(rpb_rows, D_MODEL),
        jnp.pad(loss_part, ((0, 0), (0, D_MODEL - loss_part.shape[1]))),
        jnp.zeros((SMALL_ROWS - 5 - rpb_rows, D_MODEL), F32)], axis=0)
    out = {"grad": {}, "delta": {}, "new_m": {}, "new_v": {}}

    def update(names):
        for n in names:
            out["grad"][n] = g_shard[n][None]
            res = _adamw("adamw_" + n, g_shard[n], shards[n], m_shards[n], v_shards[n])
            for kind, t in zip(("delta", "new_m", "new_v"), res, strict=True):
                out[kind][n] = t[None]

    update(REDUCE_EARLY)
    late_others = _scatter_wait(late_scatter, out["new_v"][REDUCE_EARLY[-1]], "late")
    late_mine = _chip_sum(lax.dynamic_index_in_dim(late_pair, me_chip, 0, keepdims=False), late_others, late_tm)
    small = _allreduce_small(small, after=(late_mine,))
    g_shard.update(_unpack_shard(_join_halves(late_mine), late_shapes, REDUCE_LATE))
    update(REDUCE_LATE)
    loss = small[4 + rpb_rows, 0]

    def small_pack(a0, a1, a2, a3, r):
        return jnp.concatenate([a0.reshape(1, -1), a1.reshape(1, -1), a2.reshape(1, -1), a3.reshape(1, -1),
                                jnp.pad(r.reshape(-1), (0, rpb_rows * D_MODEL - n_rpb)).reshape(rpb_rows, D_MODEL)], axis=0)

    g_small = small[:4 + rpb_rows]
    small_res = _adamw("adamw_small", g_small, small_pack(g_mix, g_mlp, g_ple, g_final, rpb),
                       small_pack(m_g_mix, m_g_mlp, m_g_ple, m_g_final, m_rpb), small_pack(v_g_mix, v_g_mlp, v_g_ple, v_g_final, v_rpb))

    def small_unpack(t):
        return {"g_mix": t[0].reshape(g_mix.shape), "g_mlp": t[1].reshape(g_mlp.shape), "g_ple": t[2].reshape(g_ple.shape),
                "g_final": t[3].reshape(g_final.shape), "rpb": t[4:].reshape(-1)[:n_rpb].reshape(rpb.shape)}

    for kind, t in zip(("grad", "delta", "new_m", "new_v"), (g_small, *small_res), strict=True):
        out[kind].update(small_unpack(t))

    order = ["g_mix", "w_in", "rpb", "w_branch_na", "w_branch_dil", "w_out", "g_mlp", "w_up", "w_down", "g_ple",
             "w_ple_gate", "w_ple_proj", "g_final"]
    return (loss, grad_x[None], *[out["grad"][n] for n in order], *[out["delta"][n] for n in order],
            *[out["new_m"][n] for n in order], *[out["new_v"][n] for n in order])
```

```python
import functools
from typing import NamedTuple

import numpy as np
import jax
import jax.numpy as jnp
from jax import lax
from jax.experimental import pallas as pl
from jax.experimental.pallas import tpu as pltpu

BF = jnp.bfloat16
F32 = jnp.float32
MESH = pl.DeviceIdType.MESH
ANY = pl.BlockSpec(memory_space=pl.ANY)

V7X_VMEM_BYTES = 64 * 1024 * 1024
VMEM_LIMIT = V7X_VMEM_BYTES - 16 * 1024 * 1024

D_MODEL = 1024
HEAD_DIM = 64
GRID_W = 64
NA_HEADS = 8
NA_WIN_ROWS = 8
NA_WIN_COLS = 16
NA_WIDTH = NA_HEADS * HEAD_DIM
DIL_GROUPS = ((128, 1), (512, 4), (2048, 16))
DIL_HPG = 4
DIL_HEADS = DIL_HPG * len(DIL_GROUPS)
DIL_WIDTH = DIL_HEADS * HEAD_DIM
DIL_OUT_WIDTH = DIL_HPG * HEAD_DIM
DIL_RADIUS = 64
QKV_WIDTH = 3 * NA_WIDTH + 3 * DIL_WIDTH
D_FF = 4 * D_MODEL
PLE_DIM = 256
ROPE_THETA = 10000.0
RMS_EPS = 1e-6
NEG_INF = -1e30
Q_SCALE = HEAD_DIM ** -0.5

ADAM_LR = 0.001
ADAM_B1 = 0.9
ADAM_B2 = 0.999
ADAM_EPS = 1e-08
ADAM_WD = 0.01
ADAM_STEP = 10

N_CHIPS = 4
N_DEV = 8
PACK_W = 1024
BIG = ("w_in", "w_branch_na", "w_branch_dil", "w_out", "w_up", "w_down", "w_ple_gate", "w_ple_proj")
GATHER_FIRST = ("w_in",)
GATHER_MIX = ("w_branch_na", "w_branch_dil", "w_out")
GATHER_MLP = ("w_up", "w_down", "w_ple_gate", "w_ple_proj")
REDUCE_EARLY = ("w_up", "w_down", "w_ple_gate", "w_out", "w_ple_proj", "w_branch_na", "w_branch_dil")
SMALL_ROWS = 16


def _cparams(sem=None):
    return pltpu.CompilerParams(dimension_semantics=sem, vmem_limit_bytes=VMEM_LIMIT)


def _mm(name, a, b, mode, tm, tn, tk, out_dtypes, epilogue=None, extras=(), consts=(), sums=(), after=(), into=None):
    if mode == "nn":
        (M, K), N = a.shape, b.shape[1]
    elif mode == "nt":
        (M, K), N = a.shape, b.shape[0]
    else:
        (K, M), N = a.shape, b.shape[1]
    tm, tn, tk = min(tm, M), min(tn, N), min(tk, K)
    assert M % tm == 0 and N % tn == 0 and K % tk == 0, (name, M, N, K, tm, tn, tk)
    if mode == "nn":
        a_spec = pl.BlockSpec((tm, tk), lambda i, j, k: (i, k))
        b_spec = pl.BlockSpec((tk, tn), lambda i, j, k: (k, j))
        dims = (((1,), (0,)), ((), ()))
    elif mode == "nt":
        a_spec = pl.BlockSpec((tm, tk), lambda i, j, k: (i, k))
        b_spec = pl.BlockSpec((tn, tk), lambda i, j, k: (j, k))
        dims = (((1,), (1,)), ((), ()))
    else:
        a_spec = pl.BlockSpec((tk, tm), lambda i, j, k: (k, i))
        b_spec = pl.BlockSpec((tk, tn), lambda i, j, k: (k, j))
        dims = (((0,), (0,)), ((), ()))
    nk = K // tk
    n_extra, n_const, n_out, n_sum = len(extras), len(consts), len(out_dtypes), len(sums)
    tile = pl.BlockSpec((tm, tn), lambda i, j, k: (i, j))
    assert not sums or tn == N, "row sums need whole rows in a tile"

    n_after = len(after)

    def body(a_ref, b_ref, *rest):
        extra_refs, rest = rest[:n_extra + n_const], rest[n_extra + n_const + n_after:]
        out_refs, sum_refs, acc = rest[:n_out], rest[n_out:n_out + n_sum], rest[-1]
        i, k = pl.program_id(0), pl.program_id(2)

        @pl.when(k == 0)
        def _():
            acc[...] = jnp.zeros_like(acc)

        acc[...] += lax.dot_general(a_ref[...].astype(BF), b_ref[...].astype(BF), dims, preferred_element_type=F32)

        @pl.when(k == nk - 1)
        def _():
            outs = (acc[...],) if epilogue is None else epilogue(acc[...], *[e[...] for e in extra_refs])
            for o_ref, val in zip(out_refs, outs[:n_out], strict=True):
                o_ref[...] = val.astype(o_ref.dtype)
            for s_ref, val in zip(sum_refs, outs[n_out:], strict=True):
                @pl.when(i == 0)
                def _():
                    s_ref[...] = val

                @pl.when(i != 0)
                def _():
                    s_ref[...] += val

    out_specs = [tile] * n_out + [pl.BlockSpec((1, c), lambda i, j, k: (0, 0)) for c in sums]
    out_shape = [jax.ShapeDtypeStruct((M, N), dt) for dt in out_dtypes] + [jax.ShapeDtypeStruct((1, c), F32) for c in sums]
    operands, aliases = [a, b, *extras, *consts, *after], {}
    in_specs = ([a_spec, b_spec] + [tile] * n_extra
                + [pl.BlockSpec(c.shape, functools.partial(lambda nd, i, j, k: (0,) * nd, c.ndim)) for c in consts] + [ANY] * n_after)
    if into is not None:
        assert n_out == 1
        target, block, index = into
        out_specs = [pl.BlockSpec(block, lambda i, j, k: index(i, j))]
        out_shape = [jax.ShapeDtypeStruct(target.shape, target.dtype)]
        if not isinstance(target, jax.ShapeDtypeStruct):
            aliases = {len(operands): 0}
            operands.append(target)
            in_specs.append(ANY)
            n_after += 1

    outs = pl.pallas_call(
        body, name=name, grid=(M // tm, N // tn, nk),
        in_specs=in_specs, out_specs=out_specs, out_shape=out_shape,
        scratch_shapes=[pltpu.VMEM((tm, tn), F32)], input_output_aliases=aliases,
        compiler_params=_cparams(("arbitrary",) * 3 if sums else ("parallel", "parallel", "arbitrary")),
    )(*operands)
    return outs[0] if len(outs) == 1 else outs


def _row(arr, tm, col_block=None, width=None):
    width = arr.shape[1] if width is None else width
    cb = 0 if col_block is None else col_block
    return arr, pl.BlockSpec((tm, width), lambda i: (i, cb))


def _full(arr):
    nd = arr.ndim
    return arr, pl.BlockSpec(arr.shape, lambda i: (0,) * nd)


def _rowwise(name, body, T, tm, ins, outs, sums=(), after=()):
    n_in, n_out, n_sum, n_after = len(ins), len(outs), len(sums), len(after)

    def kern(*refs):
        in_refs, refs = refs[:n_in], refs[n_in + n_after:]
        out_refs, sum_refs = refs[:n_out], refs[n_out:]
        res = body(*[r[...] for r in in_refs])
        res = res if isinstance(res, tuple) else (res,)
        for o_ref, val in zip(out_refs, res[:n_out], strict=True):
            o_ref[...] = val.astype(o_ref.dtype)
        if n_sum:
            @pl.when(pl.program_id(0) == 0)
            def _():
                for s_ref in sum_refs:
                    s_ref[...] = jnp.zeros_like(s_ref)

            for s_ref, val in zip(sum_refs, res[n_out:], strict=True):
                s_ref[...] += val

    res = pl.pallas_call(
        kern, name=name, grid=(T // tm,),
        in_specs=[spec for _, spec in ins] + [ANY] * n_after,
        out_specs=[pl.BlockSpec((tm, c), lambda i: (i, 0)) for c, _ in outs]
        + [pl.BlockSpec((1, c), lambda i: (0, 0)) for c in sums],
        out_shape=[jax.ShapeDtypeStruct((T, c), dt) for c, dt in outs]
        + [jax.ShapeDtypeStruct((1, c), F32) for c in sums],
        compiler_params=_cparams(("arbitrary",)),
    )(*[a for a, _ in ins], *after)
    return res[0] if len(res) == 1 else res


def _sigmoid(x):
    return 1.0 / (1.0 + jnp.exp(-x))


def _rms(h):
    return lax.rsqrt(jnp.mean(h * h, axis=-1, keepdims=True) + RMS_EPS)


def _rms_bwd(dy, h, g):
    r = _rms(h)
    n = h * r
    dn = dy * g
    dh = r * (dn - n * jnp.mean(dn * n, axis=-1, keepdims=True))
    return dh, jnp.sum(dy * n, axis=0, keepdims=True)


def _rope(x, cos2, sin_signed):
    lane = lax.broadcasted_iota(jnp.int32, x.shape, 1)
    swapped = jnp.where((lane % HEAD_DIM) < HEAD_DIM // 2, pltpu.roll(x, 128 - HEAD_DIM // 2, 1), pltpu.roll(x, HEAD_DIM // 2, 1))
    return x * cos2 + swapped * sin_signed


def _rope_cols(x, cos2, sin_signed):
    return jnp.concatenate([_rope(x[:, c:c + 128], cos2, sin_signed) for c in range(0, x.shape[1], 128)], axis=1)


NA_KEYS = NA_WIN_ROWS * GRID_W
NA_BASES = 8


def _na_row_geometry(r, rows):
    first = jnp.clip(r - NA_WIN_ROWS // 2, 0, rows - NA_WIN_ROWS)
    base = first - r + (NA_WIN_ROWS - 1)
    return pl.multiple_of(first * GRID_W, GRID_W), base


NA_ROWS_PER_STEP = 8
NA_BWD_ROWS_PER_STEP = 8


def _softmax_rows(s):
    p = jnp.exp(s - jnp.max(s, axis=-1, keepdims=True))
    return p / jnp.sum(p, axis=-1, keepdims=True)


def _na_probs(q, kw, bias):
    return _softmax_rows(lax.dot_general(q, kw, (((1,), (1,)), ((), ())), preferred_element_type=F32) + bias)


def _split_pair(t):
    first = lax.broadcasted_iota(jnp.int32, t.shape, 1) < HEAD_DIM
    zero = jnp.zeros_like(t)
    return jnp.where(first, t, zero), jnp.where(first, zero, t)


def _join_pair(a, b):
    return jnp.where(lax.broadcasted_iota(jnp.int32, a.shape, 1) < HEAD_DIM, a, b)


_NT = (((1,), (1,)), ((), ()))
_TN = (((0,), (0,)), ((), ()))


def _na_fwd(qkv, tab):
    T = qkv.shape[0]
    rows = T // GRID_W
    n_pairs = NA_WIDTH // 128

    def body(q_ref, k_ref, v_ref, tab_ref, y_ref):
        def step(it, carry):
            geo = [_na_row_geometry(it * NA_ROWS_PER_STEP + u, rows) for u in range(NA_ROWS_PER_STEP)]
            q0s = [pl.multiple_of((it * NA_ROWS_PER_STEP + u) * GRID_W, GRID_W) for u in range(NA_ROWS_PER_STEP)]
            ss = [lax.dot_general(jnp.concatenate(_split_pair(q_ref[pl.ds(q0, GRID_W), :] * Q_SCALE), axis=0),
                                  k_ref[pl.ds(k0, NA_KEYS), :], _NT, preferred_element_type=F32)
                  for q0, (k0, _) in zip(q0s, geo)]
            ps = [_softmax_rows(s + jnp.concatenate([tab_ref[0, base], tab_ref[1, base]], axis=0)) for s, (_, base) in zip(ss, geo)]
            ys = [jnp.dot(p.astype(BF), v_ref[pl.ds(k0, NA_KEYS), :], preferred_element_type=F32) for p, (k0, _) in zip(ps, geo)]
            for q0, y2 in zip(q0s, ys):
                y_ref[pl.ds(q0, GRID_W), :] = _join_pair(y2[:GRID_W], y2[GRID_W:]).astype(y_ref.dtype)
            return carry

        lax.fori_loop(0, rows // NA_ROWS_PER_STEP, step, 0)

    def cols(first):
        return pl.BlockSpec((T, 128), lambda j: (0, first + j))

    return pl.pallas_call(
        body, name="na_fwd", grid=(n_pairs,),
        in_specs=[cols(0), cols(n_pairs), cols(2 * n_pairs), pl.BlockSpec((2, NA_BASES, GRID_W, NA_KEYS), lambda j: (j, 0, 0, 0))],
        out_specs=cols(0), out_shape=jax.ShapeDtypeStruct((T, NA_WIDTH), BF),
        compiler_params=_cparams(("parallel",)),
    )(qkv, qkv, qkv, tab)


def _na_bwd(qkv, tab, do):
    T = qkv.shape[0]
    rows = T // GRID_W
    n_pairs = NA_WIDTH // 128

    def body(q_ref, k_ref, v_ref, tab_ref, do_ref, dq_ref, dk_ref, dv_ref, dtab_ref):
        dk_ref[...] = jnp.zeros_like(dk_ref)
        dv_ref[...] = jnp.zeros_like(dv_ref)
        dtab_ref[...] = jnp.zeros_like(dtab_ref)

        def step(it, carry):
            U = NA_BWD_ROWS_PER_STEP
            geo = [_na_row_geometry(it * U + u, rows) for u in range(U)]
            q0s = [pl.multiple_of((it * U + u) * GRID_W, GRID_W) for u in range(U)]
            q2s = [jnp.concatenate(_split_pair(q_ref[pl.ds(q0, GRID_W), :] * Q_SCALE), axis=0) for q0 in q0s]
            do2s = [jnp.concatenate(_split_pair(do_ref[pl.ds(q0, GRID_W), :]), axis=0) for q0 in q0s]
            ss = [lax.dot_general(q2, k_ref[pl.ds(k0, NA_KEYS), :], _NT, preferred_element_type=F32) for q2, (k0, _) in zip(q2s, geo)]
            dps = [lax.dot_general(do2, v_ref[pl.ds(k0, NA_KEYS), :], _NT, preferred_element_type=F32) for do2, (k0, _) in zip(do2s, geo)]
            ps = [_softmax_rows(s + jnp.concatenate([tab_ref[0, base], tab_ref[1, base]], axis=0)) for s, (_, base) in zip(ss, geo)]
            dss = [p * (dp - jnp.sum(dp * p, axis=-1, keepdims=True)) for p, dp in zip(ps, dps)]
            dvs = [lax.dot_general(p.astype(BF), do2, _TN, preferred_element_type=F32) for p, do2 in zip(ps, do2s)]
            dsbs = [ds.astype(BF) for ds in dss]
            dqs = [jnp.dot(dsb, k_ref[pl.ds(k0, NA_KEYS), :], preferred_element_type=F32) for dsb, (k0, _) in zip(dsbs, geo)]
            dks = [lax.dot_general(dsb, q2, _TN, preferred_element_type=F32) for dsb, q2 in zip(dsbs, q2s)]
            for u in range(U):
                k0, base = geo[u]
                dtab_ref[0, base] += dss[u][:GRID_W]
                dtab_ref[1, base] += dss[u][GRID_W:]
                dq_ref[pl.ds(q0s[u], GRID_W), :] = _join_pair(dqs[u][:GRID_W], dqs[u][GRID_W:])
                dk_ref[pl.ds(k0, NA_KEYS), :] += dks[u]
                dv_ref[pl.ds(k0, NA_KEYS), :] += dvs[u]
            return carry

        lax.fori_loop(0, rows // NA_BWD_ROWS_PER_STEP, step, 0)

    def cols(first):
        return pl.BlockSpec((T, 128), lambda j: (0, first + j))

    tabs = pl.BlockSpec((2, NA_BASES, GRID_W, NA_KEYS), lambda j: (j, 0, 0, 0))
    wide = jax.ShapeDtypeStruct((T, NA_WIDTH), F32)
    return pl.pallas_call(
        body, name="na_bwd", grid=(n_pairs,),
        in_specs=[cols(0), cols(n_pairs), cols(2 * n_pairs), tabs, cols(0)],
        out_specs=[cols(0), cols(0), cols(0), tabs],
        out_shape=[wide, wide, wide, jax.ShapeDtypeStruct((NA_HEADS, NA_BASES, GRID_W, NA_KEYS), F32)],
        compiler_params=_cparams(("parallel",)),
    )(qkv, qkv, qkv, tab, do)


def _na_bias_table(rpb):
    H, n_rows, n_cols = rpb.shape

    def body(r_ref, tab_ref):
        q = lax.broadcasted_iota(jnp.int32, (GRID_W, 128), 0)
        kc = lax.broadcasted_iota(jnp.int32, (GRID_W, 128), 1)
        first = jnp.clip(q - NA_WIN_COLS // 2, 0, GRID_W - NA_WIN_COLS)
        valid = (kc >= first) & (kc < first + NA_WIN_COLS)
        toeplitz = []
        for ro in range(n_rows):
            row = jnp.broadcast_to(r_ref[pl.ds(ro, 1), :], (GRID_W, 128))
            shifted = pltpu.roll(pltpu.roll(row, 128 - (NA_WIN_COLS - 1), 1), 0, 1, stride=1, stride_axis=0)
            toeplitz.append(jnp.where(valid, shifted, NEG_INF))
        for base in range(NA_BASES):
            for j in range(NA_WIN_ROWS // 2):
                even, odd = toeplitz[base + 2 * j], toeplitz[base + 2 * j + 1]
                tab_ref[base, :, pl.ds(j * 128, 128)] = jnp.where(kc < GRID_W, even, pltpu.roll(odd, GRID_W, 1))

    padded = jnp.pad(rpb, ((0, 0), (0, 16 - n_rows), (0, 128 - n_cols)))
    return pl.pallas_call(
        body, name="na_bias_table", grid=(H,),
        in_specs=[pl.BlockSpec((None, 16, 128), lambda h: (h, 0, 0))],
        out_specs=pl.BlockSpec((None, NA_BASES, GRID_W, NA_KEYS), lambda h: (h, 0, 0, 0)),
        out_shape=jax.ShapeDtypeStruct((H, NA_BASES, GRID_W, NA_KEYS), F32),
        compiler_params=_cparams(("parallel",)),
    )(padded)


def _na_rpb_grad(dtab):
    H = dtab.shape[0]
    n_rows = 2 * NA_WIN_ROWS - 1
    n_cols = 2 * NA_WIN_COLS - 1

    def body(d_ref, o_ref):
        lane = lax.broadcasted_iota(jnp.int32, (GRID_W, 128), 1)
        low = lane < GRID_W
        out_rows = []
        for ro in range(n_rows):
            acc = jnp.zeros((GRID_W, 128), F32)
            for base in range(NA_BASES):
                i = ro - base
                if not 0 <= i < NA_WIN_ROWS:
                    continue
                pair = d_ref[base, :, pl.ds((i // 2) * 128, 128)]
                if i % 2:
                    pair = pltpu.roll(pair, GRID_W, 1)
                acc = acc + jnp.where(low, pair, 0.0)
            skew = pltpu.roll(acc, 0, 1, stride=1, stride_axis=0)
            diag = jnp.sum(skew, axis=0, keepdims=True)
            out_rows.append(pltpu.roll(jnp.broadcast_to(diag, (8, 128)), 128 - (GRID_W - NA_WIN_COLS), 1)[:1])
        out_rows.append(jnp.zeros((1, 128), F32))
        res = jnp.concatenate(out_rows, axis=0)
        o_ref[...] = jnp.where(lax.broadcasted_iota(jnp.int32, res.shape, 1) < n_cols, res, 0.0)

    return pl.pallas_call(
        body, name="na_rpb_grad", grid=(H,),
        in_specs=[pl.BlockSpec((None, NA_BASES, GRID_W, NA_KEYS), lambda h: (h, 0, 0, 0))],
        out_specs=pl.BlockSpec((None, n_rows + 1, 128), lambda h: (h, 0, 0)),
        out_shape=jax.ShapeDtypeStruct((H, n_rows + 1, 128), F32),
        compiler_params=_cparams(("parallel",)),
    )(jnp.flip(dtab, axis=2))


BAND_Q = 128
BAND_KEYS = BAND_Q + 2 * DIL_RADIUS


def _band_geometry(n, L):
    q0 = pl.multiple_of(n * BAND_Q, BAND_Q)
    k0 = pl.multiple_of(jnp.clip(q0 - DIL_RADIUS, 0, L - BAND_KEYS), DIL_RADIUS)
    qi = q0 + lax.broadcasted_iota(jnp.int32, (BAND_Q, BAND_KEYS), 0)
    kj = k0 + lax.broadcasted_iota(jnp.int32, (BAND_Q, BAND_KEYS), 1)
    return q0, k0, jnp.abs(qi - kj) <= DIL_RADIUS


DIL_PAIRS = DIL_OUT_WIDTH // 128


def _residue_shape(dil, T, dtype):
    return jax.ShapeDtypeStruct((DIL_PAIRS, dil, T // dil, 128), dtype)


def _residue_tile(dil, tm):
    return pl.BlockSpec((DIL_PAIRS, dil, tm // dil, 128), lambda i: (0, 0, i, 0))


def _to_natural(ref, scratch, dil, tm):
    tiles = []
    for pair in range(DIL_PAIRS):
        if dil == 1:
            tiles.append(ref[pair, 0].astype(F32))
            continue
        for r in range(dil):
            scratch[pl.ds(r, tm // dil, stride=dil), :] = ref[pair, r].astype(F32)
        tiles.append(scratch[...])
    return tiles


def _from_natural(tile, scratch, ref, pair, dil, tm):
    if dil == 1:
        ref[pair, 0] = tile.astype(ref.dtype)
        return
    scratch[...] = tile
    for r in range(dil):
        ref[pair, r] = scratch[pl.ds(r, tm // dil, stride=dil), :].astype(ref.dtype)


def _band_specs(group, T):
    dil = DIL_GROUPS[group][1]
    L = T // dil
    assert L % BAND_Q == 0 and L >= BAND_KEYS, (T, dil)
    return L, (dil * DIL_PAIRS,), pl.BlockSpec((None, None, L, 128), lambda s: (s % DIL_PAIRS, s // DIL_PAIRS, 0, 0))


BAND_BLOCKS_PER_STEP = 4


def _band_softmax(s, valid):
    s = jnp.where(valid, s, NEG_INF)
    m = jnp.max(s, axis=-1, keepdims=True)
    p = jnp.exp(s - m)
    l = jnp.sum(p, axis=-1, keepdims=True)
    return p / l, m + jnp.log(l)


def _band_fwd(q, k, v, group):
    T = q.shape[1] * q.shape[2]
    L, grid, spec = _band_specs(group, T)
    U = min(BAND_BLOCKS_PER_STEP, L // BAND_Q)

    def body(q_ref, k_ref, v_ref, o_ref, lse_ref):
        def step(it, carry):
            geo = [_band_geometry(it * U + u, L) for u in range(U)]
            ss = [lax.dot_general(jnp.concatenate(_split_pair(q_ref[pl.ds(q0, BAND_Q), :]), axis=0),
                                  k_ref[pl.ds(k0, BAND_KEYS), :], _NT, preferred_element_type=F32) for q0, k0, _ in geo]
            pls = [_band_softmax(s, jnp.concatenate([valid, valid], axis=0)) for s, (_, _, valid) in zip(ss, geo)]
            os = [jnp.dot(p.astype(BF), v_ref[pl.ds(k0, BAND_KEYS), :], preferred_element_type=F32) for (p, _), (_, k0, _) in zip(pls, geo)]
            for (q0, _, _), o2, (_, lse) in zip(geo, os, pls):
                o_ref[pl.ds(q0, BAND_Q), :] = _join_pair(o2[:BAND_Q], o2[BAND_Q:])
                lse2 = jnp.broadcast_to(lse, (2 * BAND_Q, 128))
                lse_ref[pl.ds(q0, BAND_Q), :] = _join_pair(lse2[:BAND_Q], lse2[BAND_Q:])
            return carry

        lax.fori_loop(0, L // (BAND_Q * U), step, 0)

    res = _residue_shape(DIL_GROUPS[group][1], T, F32)
    return pl.pallas_call(
        body, name=f"band_fwd_g{group}", grid=grid,
        in_specs=[spec] * 3, out_specs=[spec] * 2, out_shape=[res, res],
        compiler_params=_cparams(("parallel",)),
    )(q, k, v)


def _band_bwd(q, k, v, do, dlse, group):
    T = q.shape[1] * q.shape[2]
    L, grid, spec = _band_specs(group, T)
    U = min(BAND_BLOCKS_PER_STEP, L // BAND_Q)

    def body(q_ref, k_ref, v_ref, do_ref, dlse_ref, dq_ref, dk_ref, dv_ref):
        dk_ref[...] = jnp.zeros_like(dk_ref)
        dv_ref[...] = jnp.zeros_like(dv_ref)

        def step(it, carry):
            geo = [_band_geometry(it * U + u, L) for u in range(U)]
            q2s = [jnp.concatenate(_split_pair(q_ref[pl.ds(q0, BAND_Q), :]), axis=0) for q0, _, _ in geo]
            do2s = [jnp.concatenate(_split_pair(do_ref[pl.ds(q0, BAND_Q), :]), axis=0) for q0, _, _ in geo]
            ss = [lax.dot_general(q2, k_ref[pl.ds(k0, BAND_KEYS), :], _NT, preferred_element_type=F32) for q2, (_, k0, _) in zip(q2s, geo)]
            dps = [lax.dot_general(do2, v_ref[pl.ds(k0, BAND_KEYS), :], _NT, preferred_element_type=F32) for do2, (_, k0, _) in zip(do2s, geo)]
            ps = [_band_softmax(s, jnp.concatenate([valid, valid], axis=0))[0] for s, (_, _, valid) in zip(ss, geo)]
            dss = []
            for p, dp, (q0, _, _) in zip(ps, dps, geo):
                dl = dlse_ref[pl.ds(q0, BAND_Q), :]
                dl2 = jnp.concatenate([dl[:, :1], dl[:, HEAD_DIM:HEAD_DIM + 1]], axis=0)
                dss.append(p * (dp - jnp.sum(dp * p, axis=-1, keepdims=True) + dl2))
            dvs = [lax.dot_general(p.astype(BF), do2, _TN, preferred_element_type=F32) for p, do2 in zip(ps, do2s)]
            dsbs = [ds.astype(BF) for ds in dss]
            dqs = [jnp.dot(dsb, k_ref[pl.ds(k0, BAND_KEYS), :], preferred_element_type=F32) for dsb, (_, k0, _) in zip(dsbs, geo)]
            dks = [lax.dot_general(dsb, q2, _TN, preferred_element_type=F32) for dsb, q2 in zip(dsbs, q2s)]
            for u, (q0, k0, _) in enumerate(geo):
                dq_ref[pl.ds(q0, BAND_Q), :] = _join_pair(dqs[u][:BAND_Q], dqs[u][BAND_Q:])
                dk_ref[pl.ds(k0, BAND_KEYS), :] += dks[u]
                dv_ref[pl.ds(k0, BAND_KEYS), :] += dvs[u]
            return carry

        lax.fori_loop(0, L // (BAND_Q * U), step, 0)

    res = _residue_shape(DIL_GROUPS[group][1], T, F32)
    return pl.pallas_call(
        body, name=f"band_bwd_g{group}", grid=grid,
        in_specs=[spec] * 5, out_specs=[spec] * 3, out_shape=[res] * 3,
        compiler_params=_cparams(("parallel",)),
    )(q, k, v, do, dlse)


def _head_sums(t):
    head = lax.broadcasted_iota(jnp.int32, t.shape, 1) // HEAD_DIM
    out = jnp.zeros_like(t)
    for h in range(t.shape[1] // HEAD_DIM):
        mine = head == h
        out = jnp.where(mine, jnp.sum(jnp.where(mine, t, 0.0), axis=-1, keepdims=True), out)
    return out


def _dil_merge_fwd(os, lses, T, tm):
    G = len(DIL_GROUPS)
    W = DIL_OUT_WIDTH
    dils = [d for _, d in DIL_GROUPS]

    def body(*refs):
        o_refs, lse_refs = refs[:G], refs[G:2 * G]
        y_ref, w_refs, on_refs, scratch = refs[2 * G], refs[2 * G + 1:3 * G + 1], refs[3 * G + 1:4 * G + 1], refs[-1]
        o = [jnp.concatenate(_to_natural(r, scratch, d, tm), axis=1) for r, d in zip(o_refs, dils)]
        ls = [jnp.concatenate(_to_natural(r, scratch, d, tm), axis=1) for r, d in zip(lse_refs, dils)]
        m = functools.reduce(jnp.maximum, ls)
        es = [jnp.exp(l - m) for l in ls]
        tot = functools.reduce(jnp.add, es)
        ws = [e / tot for e in es]
        y_ref[...] = functools.reduce(jnp.add, [w * t for w, t in zip(ws, o)]).astype(y_ref.dtype)
        for g in range(G):
            w_refs[g][...] = ws[g]
            on_refs[g][...] = o[g]

    nat = pl.BlockSpec((tm, W), lambda i: (i, 0))
    res = pl.pallas_call(
        body, name="dil_merge_fwd", grid=(T // tm,),
        in_specs=[_residue_tile(d, tm) for d in dils] * 2,
        out_specs=[nat] * (2 * G + 1),
        out_shape=[jax.ShapeDtypeStruct((T, W), BF)] + [jax.ShapeDtypeStruct((T, W), F32)] * (2 * G),
        scratch_shapes=[pltpu.VMEM((tm, 128), F32)],
        compiler_params=_cparams(("parallel",)),
    )(*os, *lses)
    return res[0], res[1:G + 1], res[G + 1:]


def _dil_merge_bwd(dy, os, ws, tm, after=()):
    G = len(DIL_GROUPS)
    T, W = dy.shape
    dils = [d for _, d in DIL_GROUPS]
    n_after = len(after)

    def body(*refs):
        dyt = refs[0][...]
        o, w = [r[...] for r in refs[1:G + 1]], [r[...] for r in refs[G + 1:2 * G + 1]]
        refs = refs[2 * G + 1 + n_after:]
        do_refs, dlse_refs, scratch = refs[:G], refs[G:2 * G], refs[-1]
        dws = [_head_sums(dyt * t) for t in o]
        mean = functools.reduce(jnp.add, [a * b for a, b in zip(w, dws)])
        for g, d in enumerate(dils):
            do, dlse = w[g] * dyt, w[g] * (dws[g] - mean)
            for pair in range(DIL_PAIRS):
                cols = slice(pair * 128, (pair + 1) * 128)
                _from_natural(do[:, cols], scratch, do_refs[g], pair, d, tm)
                _from_natural(dlse[:, cols], scratch, dlse_refs[g], pair, d, tm)

    nat = pl.BlockSpec((tm, W), lambda i: (i, 0))
    res = pl.pallas_call(
        body, name="dil_merge_bwd", grid=(T // tm,),
        in_specs=[nat] * (2 * G + 1) + [ANY] * n_after,
        out_specs=[_residue_tile(d, tm) for d in dils] * 2,
        out_shape=[_residue_shape(d, T, BF) for d in dils] + [_residue_shape(d, T, F32) for d in dils],
        scratch_shapes=[pltpu.VMEM((tm, 128), F32)],
        compiler_params=_cparams(("parallel",)),
    )(dy, *os, *ws, *after)
    return res[:G], res[G:]


def _qkv_prep(z, cos2, sin_signed, tm):
    T = z.shape[0]
    G = len(DIL_GROUPS)
    dils = [d for _, d in DIL_GROUPS]
    n_dil_blocks = 3 * DIL_WIDTH // 128

    def body(*refs):
        blocks = refs[:n_dil_blocks]
        cos_ref, sin_ref = refs[n_dil_blocks], refs[1 + n_dil_blocks]
        outs = refs[2 + n_dil_blocks:]
        for part in range(3):
            for g, d in enumerate(dils):
                out = outs[g * 3 + part]
                for pair in range(DIL_PAIRS):
                    blk = blocks[part * (DIL_WIDTH // 128) + g * DIL_PAIRS + pair]
                    for r in range(d):
                        rows = pl.ds(r, tm // d, stride=d) if d > 1 else slice(None)
                        x = blk[rows, :]
                        if part < 2:
                            x = _rope(x, cos_ref[rows, :], sin_ref[rows, :])
                        if part == 0:
                            x = x * Q_SCALE
                        out[pair, r] = x.astype(out.dtype)

    lane_block = [pl.BlockSpec((tm, 128), functools.partial(lambda b, i: (i, b), b)) for b in range(n_dil_blocks)]
    tab = pl.BlockSpec((tm, 128), lambda i: (i, 0))
    res = pl.pallas_call(
        body, name="qkv_prep", grid=(T // tm,),
        in_specs=lane_block + [tab, tab],
        out_specs=[_residue_tile(d, tm) for d in dils for _ in range(3)],
        out_shape=[_residue_shape(d, T, BF) for d in dils for _ in range(3)],
        compiler_params=_cparams(("parallel",)),
    )(*[z] * n_dil_blocks, cos2, sin_signed)
    return [res[3 * g:3 + 3 * g] for g in range(G)]


def _qkv_unprep(d_na, d_dil, cos2, sin_signed, tm, after=()):
    T = d_na[0].shape[0]
    G = len(DIL_GROUPS)
    dils = [d for _, d in DIL_GROUPS]
    n_after = len(after)

    def body(*refs):
        dq, dk, dv = (r[...] for r in refs[:3])
        res_refs = refs[3:3 + 3 * G]
        cs, sn = refs[3 + 3 * G][...], refs[4 + 3 * G][...]
        out, scratch = refs[5 + 3 * G + n_after], refs[-1]
        cols = [dq * Q_SCALE, dk, dv]
        for part in range(3):
            for g, d in enumerate(dils):
                for x in _to_natural(res_refs[g * 3 + part], scratch, d, tm):
                    if part < 2:
                        x = _rope(x, cs, -sn)
                    cols.append(x * Q_SCALE if part == 0 else x)
        out[...] = jnp.concatenate(cols, axis=1).astype(out.dtype)

    wide = pl.BlockSpec((tm, NA_WIDTH), lambda i: (i, 0))
    tab = pl.BlockSpec((tm, 128), lambda i: (i, 0))
    return pl.pallas_call(
        body, name="qkv_unprep", grid=(T // tm,),
        in_specs=[wide] * 3 + [_residue_tile(d, tm) for d in dils for _ in range(3)] + [tab, tab] + [ANY] * n_after,
        out_specs=pl.BlockSpec((tm, QKV_WIDTH), lambda i: (i, 0)),
        out_shape=jax.ShapeDtypeStruct((T, QKV_WIDTH), BF),
        scratch_shapes=[pltpu.VMEM((tm, 128), F32)],
        compiler_params=_cparams(("parallel",)),
    )(*d_na, *[t for g in range(G) for t in d_dil[g]], cos2, sin_signed, *after)


def _rope_tables(positions):
    half = HEAD_DIM // 2
    inv_freq = ROPE_THETA ** (-jnp.arange(half, dtype=F32) / half)
    ang = positions.astype(F32)[:, None] * inv_freq
    cos, sin = jnp.cos(ang), jnp.sin(ang)
    return jnp.tile(jnp.concatenate([cos, cos], axis=1), (1, 2)), jnp.tile(jnp.concatenate([-sin, sin], axis=1), (1, 2))


def _pack_rows(t):
    return t.reshape(-1, PACK_W)


def _me():
    return lax.axis_index("x"), lax.axis_index("y"), lax.axis_index("c")


def _other_chips(x, y):
    return [(1 - x, y), (x, 1 - y), (1 - x, 1 - y)]


def _gather_weights(packed):
    R, W = packed.shape
    half = R // 2

    def body(in_ref, out_ref, send_sems, recv_sems):
        x, y, c = _me()
        sibling = (x, y, 1 - c)
        chips = _other_chips(x, y)

        def block(chip, core):
            return out_ref.at[2 * chip[0] + chip[1], pl.ds(core * half, half), :]

        def copy(k, chip, core, to, src=None):
            return pltpu.make_async_remote_copy(
                src_ref=block(chip, core) if src is None else src, dst_ref=block(chip, core),
                send_sem=send_sems.at[k], recv_sem=recv_sems.at[k], device_id=to, device_id_type=MESH)

        first = [copy(j, (x, y), c, (*chip, c), src=in_ref.at[pl.ds(c * half, half), :]) for j, chip in enumerate(chips)]
        for cp in first:
            cp.start()
        passed = [copy(3 + j, chip, c, sibling) for j, chip in enumerate(chips)]
        for j, chip in enumerate(chips):
            copy(j, chip, c, (x, y, c)).wait_recv()
            passed[j].start()
        for j, chip in enumerate(chips):
            copy(3 + j, chip, 1 - c, (x, y, c)).wait_recv()
        for cp in first + passed:
            cp.wait_send()

    others = pl.pallas_call(
        body, name="gather_weights",
        in_specs=[ANY], out_specs=ANY,
        out_shape=jax.ShapeDtypeStruct((N_CHIPS, R, W), packed.dtype),
        scratch_shapes=[pltpu.SemaphoreType.DMA((6,)), pltpu.SemaphoreType.DMA((6,))],
    )(packed)
    return lax.dynamic_update_slice(others, packed[None], (2 * lax.axis_index("x") + lax.axis_index("y"), 0, 0))


def _swap_halves(g):
    S, R, W = g.shape
    half = R // 2

    def body(g_ref, out_ref, send_sem, recv_sem):
        x, y, c = _me()
        cp = pltpu.make_async_remote_copy(
            src_ref=g_ref.at[:, pl.ds((1 - c) * half, half), :], dst_ref=out_ref,
            send_sem=send_sem, recv_sem=recv_sem, device_id=(x, y, 1 - c), device_id_type=MESH)
        cp.start()
        cp.wait()

    return pl.pallas_call(
        body, name="swap_halves", in_specs=[ANY], out_specs=ANY,
        out_shape=jax.ShapeDtypeStruct((S, half, W), g.dtype),
        scratch_shapes=[pltpu.SemaphoreType.DMA, pltpu.SemaphoreType.DMA],
    )(g)


def _pair_sum(g, got, tm):
    S, R, W = g.shape
    half = R // 2
    nb = half // tm

    def body(c_ref, g_ref, got_ref, o_ref, ob_ref):
        tot = g_ref[...] + got_ref[...]
        o_ref[...] = tot
        ob_ref[...] = tot.astype(ob_ref.dtype)

    tile = pl.BlockSpec((None, tm, W), lambda s, i, c_ref: (s, i, 0))
    return pl.pallas_call(
        body, name="pair_sum",
        grid_spec=pltpu.PrefetchScalarGridSpec(
            num_scalar_prefetch=1, grid=(S, nb),
            in_specs=[pl.BlockSpec((None, tm, W), lambda s, i, c_ref: (s, c_ref[0] * nb + i, 0)), tile],
            out_specs=[tile, tile]),
        out_shape=[jax.ShapeDtypeStruct((S, half, W), F32), jax.ShapeDtypeStruct((S, half, W), BF)],
        compiler_params=_cparams(("parallel", "parallel")),
    )(lax.axis_index("c").reshape(1).astype(jnp.int32), g, got)


def _scatter_chips(part):
    S, h, W = part.shape

    def body(p_ref, out_ref, send_sems, recv_sems):
        x, y, c = _me()
        chips = _other_chips(x, y)
        sends = [pltpu.make_async_remote_copy(
            src_ref=p_ref.at[2 * chip[0] + chip[1]], dst_ref=out_ref.at[j],
            send_sem=send_sems.at[j], recv_sem=recv_sems.at[j], device_id=(*chip, c), device_id_type=MESH)
            for j, chip in enumerate(chips)]
        for cp in sends:
            cp.start()
        for cp in sends:
            cp.wait()

    return pl.pallas_call(
        body, name="scatter_chips", in_specs=[ANY], out_specs=ANY,
        out_shape=jax.ShapeDtypeStruct((S - 1, h, W), part.dtype),
        scratch_shapes=[pltpu.SemaphoreType.DMA((3,)), pltpu.SemaphoreType.DMA((3,))],
    )(part)


def _chip_sum(own, others, tm):
    n, h, W = others.shape

    def body(own_ref, p_ref, o_ref):
        o_ref[...] = ((own_ref[...] + p_ref[0].astype(F32)) + p_ref[1].astype(F32)) + p_ref[2].astype(F32)

    return pl.pallas_call(
        body, name="chip_sum", grid=(h // tm,),
        in_specs=[pl.BlockSpec((tm, W), lambda i: (i, 0)), pl.BlockSpec((n, tm, W), lambda i: (0, i, 0))],
        out_specs=pl.BlockSpec((tm, W), lambda i: (i, 0)),
        out_shape=jax.ShapeDtypeStruct((h, W), F32),
        compiler_params=_cparams(("parallel",)),
    )(own, others)


def _join_halves(mine):
    h, W = mine.shape

    def body(m_ref, out_ref, send_sem, recv_sem):
        x, y, c = _me()
        cp = pltpu.make_async_remote_copy(
            src_ref=m_ref, dst_ref=out_ref.at[pl.ds(c * h, h), :],
            send_sem=send_sem, recv_sem=recv_sem, device_id=(x, y, 1 - c), device_id_type=MESH)
        cp.start()
        pltpu.make_async_remote_copy(
            src_ref=m_ref, dst_ref=out_ref.at[pl.ds((1 - c) * h, h), :],
            send_sem=send_sem, recv_sem=recv_sem, device_id=(x, y, 1 - c), device_id_type=MESH).wait_recv()
        cp.wait_send()

    other = pl.pallas_call(
        body, name="join_halves", in_specs=[ANY], out_specs=ANY,
        out_shape=jax.ShapeDtypeStruct((2 * h, W), mine.dtype),
        scratch_shapes=[pltpu.SemaphoreType.DMA, pltpu.SemaphoreType.DMA],
    )(mine)
    return lax.dynamic_update_slice(other, mine, (lax.axis_index("c") * h, 0))


def _allreduce_small(s, after=()):
    R, W = s.shape
    n_after = len(after)

    def body(s_ref, *rest):
        o_ref, buf, send_sems, recv_sems = rest[n_after:]
        x, y, c = _me()
        me = 4 * x + 2 * y + c
        buf[me] = s_ref[...]
        peers = [((x + fx) % 2, (y + fy) % 2, (c + fc) % 2) for fx in range(2) for fy in range(2) for fc in range(2)][1:]
        sends = [pltpu.make_async_remote_copy(
            src_ref=s_ref, dst_ref=buf.at[me], send_sem=send_sems.at[k], recv_sem=recv_sems.at[k],
            device_id=peer, device_id_type=MESH) for k, peer in enumerate(peers)]
        for cp in sends:
            cp.start()
        for k, peer in enumerate(peers):
            pltpu.make_async_remote_copy(
                src_ref=s_ref, dst_ref=buf.at[4 * peer[0] + 2 * peer[1] + peer[2]], send_sem=send_sems.at[k],
                recv_sem=recv_sems.at[k], device_id=peer, device_id_type=MESH).wait_recv()
        for cp in sends:
            cp.wait_send()
        total = buf[0]
        for d in range(1, N_DEV):
            total = total + buf[d]
        o_ref[...] = total

    return pl.pallas_call(
        body, name="allreduce_small",
        in_specs=[pl.BlockSpec(memory_space=pltpu.VMEM)] + [ANY] * n_after, out_specs=pl.BlockSpec(memory_space=pltpu.VMEM),
        out_shape=jax.ShapeDtypeStruct((R, W), F32),
        scratch_shapes=[pltpu.VMEM((N_DEV, R, W), F32), pltpu.SemaphoreType.DMA((N_DEV - 1,)), pltpu.SemaphoreType.DMA((N_DEV - 1,))],
    )(s, *after)


HBM_SPEC = pl.BlockSpec(memory_space=pltpu.HBM)
SEM_SPEC = pl.BlockSpec(memory_space=pltpu.SEMAPHORE)
DATAFLOW = pltpu.SideEffectType.DATAFLOW_SIDE_EFFECTING


class _InFlight(NamedTuple):
    sems: tuple
    src: jax.Array
    land: jax.Array
    token: jax.Array


def _split_start(name, src, land_shape, land_dtype, n, copies, after=()):
    n_after = len(after)

    def body(src_ref, land_ref, *rest):
        rest = rest[n_after:]
        sems, token = rest[:2 * n], rest[-1]
        for k, (s, d, peer) in enumerate(copies(src_ref, land_ref)):
            pltpu.make_async_remote_copy(src_ref=s, dst_ref=d, send_sem=sems[k], recv_sem=sems[n + k],
                                         device_id=peer, device_id_type=MESH).start()
        token[...] = jnp.zeros_like(token)

    outs = pl.pallas_call(
        body, name=name,
        out_shape=(*[pltpu.SemaphoreType.DMA(())] * (2 * n), pltpu.HBM(src.shape, src.dtype), pltpu.HBM(land_shape, land_dtype),
                   jax.ShapeDtypeStruct((8, 128), F32)),
        in_specs=(HBM_SPEC, HBM_SPEC, *[ANY] * n_after),
        out_specs=(*[SEM_SPEC] * (2 * n), HBM_SPEC, HBM_SPEC, pl.BlockSpec(memory_space=pltpu.VMEM)),
        input_output_aliases={0: 2 * n, 1: 2 * n + 1},
        compiler_params=pltpu.CompilerParams(has_side_effects=DATAFLOW),
    )(pltpu.with_memory_space_constraint(src, pltpu.HBM), pltpu.with_memory_space_constraint(lax.empty(land_shape, land_dtype), pltpu.HBM),
      *after)
    return _InFlight(tuple(outs[:2 * n]), outs[2 * n], outs[2 * n + 1], outs[2 * n + 2])


def _split_wait(name, flight, after, n, copies):
    def body(src_ref, land_ref, *rest):
        sems = rest[:2 * n]
        for k, (s, d, peer) in enumerate(copies(src_ref, land_ref)):
            cp = pltpu.make_async_remote_copy(src_ref=s, dst_ref=d, send_sem=sems[k], recv_sem=sems[n + k],
                                              device_id=peer, device_id_type=MESH)
            cp.wait_send()
            cp.wait_recv()

    return pl.pallas_call(
        body, name=name,
        out_shape=(pltpu.HBM(flight.src.shape, flight.src.dtype), pltpu.HBM(flight.land.shape, flight.land.dtype)),
        in_specs=(HBM_SPEC, HBM_SPEC, *[SEM_SPEC] * (2 * n), ANY),
        out_specs=(HBM_SPEC, HBM_SPEC), input_output_aliases={0: 0, 1: 1},
        compiler_params=pltpu.CompilerParams(has_side_effects=DATAFLOW),
    )(flight.src, flight.land, *flight.sems, after)


def _gather_copies(src_ref, land_ref):
    x, y, c = _me()
    return [(src_ref, land_ref.at[2 * x + y], (*chip, c)) for chip in _other_chips(x, y)]


def _gather_start(packed, tag, after=()):
    return _split_start(f"gather_start_{tag}", packed, (N_CHIPS, *packed.shape), packed.dtype, 3, _gather_copies, after)


def _gather_wait(flight, after, tag):
    src, others = _split_wait(f"gather_wait_{tag}", flight, after, 3, _gather_copies)
    return lax.dynamic_update_slice(others, src[None], (2 * lax.axis_index("x") + lax.axis_index("y"), 0, 0))


def _swap_copies(src_ref, land_ref):
    x, y, c = _me()
    half = land_ref.shape[1]
    return [(src_ref.at[:, pl.ds((1 - c) * half, half), :], land_ref, (x, y, 1 - c))]


def _swap_start(g, tag):
    S, R, W = g.shape
    return _split_start(f"swap_halves_start_{tag}", g, (S, R // 2, W), g.dtype, 1, _swap_copies)


def _swap_wait(flight, after, tag):
    return _split_wait(f"swap_halves_wait_{tag}", flight, after, 1, _swap_copies)


def _scatter_copies(src_ref, land_ref):
    x, y, c = _me()
    return [(src_ref.at[2 * chip[0] + chip[1]], land_ref.at[j], (*chip, c)) for j, chip in enumerate(_other_chips(x, y))]


def _scatter_start(part, tag):
    S, h, W = part.shape
    return _split_start(f"scatter_chips_start_{tag}", part, (S - 1, h, W), part.dtype, 3, _scatter_copies)


def _scatter_wait(flight, after, tag):
    return _split_wait(f"scatter_chips_wait_{tag}", flight, after, 3, _scatter_copies)[1]


def _join_copies(src_ref, land_ref):
    x, y, c = _me()
    h = src_ref.shape[0]
    return [(src_ref, land_ref.at[pl.ds(c * h, h), :], (x, y, 1 - c))]


def _join_start(mine):
    h, W = mine.shape
    return _split_start("join_halves_start", mine, (2 * h, W), mine.dtype, 1, _join_copies)


def _join_wait(flight, after):
    src, other = _split_wait("join_halves_wait", flight, after, 1, _join_copies)
    return lax.dynamic_update_slice(other, src, (lax.axis_index("c") * src.shape[0], 0))


def _adamw(name, g, w, m, v):
    R, C = w.shape
    tm = R
    for cand in (256, 128, 64, 32, 16, 8):
        if R % cand == 0:
            tm = cand
            break

    def body(g, w, m, v):
        m = ADAM_B1 * m + (1.0 - ADAM_B1) * g
        v = ADAM_B2 * v + (1.0 - ADAM_B2) * jnp.square(g)
        m_hat = m / (1.0 - ADAM_B1 ** ADAM_STEP)
        v_hat = v / (1.0 - ADAM_B2 ** ADAM_STEP)
        delta = -ADAM_LR * (m_hat / (jnp.sqrt(v_hat) + ADAM_EPS) + ADAM_WD * w)
        return delta, m, v

    return _rowwise(name, body, R, tm, [_row(t, tm) for t in (g, w, m, v)], [(C, F32)] * 3)


def _unpack_weights(gathered, names):
    S = gathered.shape[0]
    shard_shapes = {"w_in": (D_MODEL, (QKV_WIDTH + 2 * D_MODEL) // S), "w_branch_na": (NA_WIDTH, D_MODEL // S),
                    "w_branch_dil": (DIL_OUT_WIDTH, D_MODEL // S), "w_out": (D_MODEL // S, D_MODEL),
                    "w_up": (D_MODEL, D_FF // S), "w_down": (D_FF // S, D_MODEL),
                    "w_ple_gate": (D_MODEL // S, D_MODEL), "w_ple_proj": (PLE_DIM, D_MODEL // S)}
    col_sharded = {"w_in", "w_branch_na", "w_branch_dil", "w_up", "w_ple_proj"}
    out, r0 = {}, 0
    for name in names:
        rows, cols = shard_shapes[name]
        n = rows * cols // PACK_W
        t = gathered[:, r0:r0 + n, :].reshape(S, rows, cols)
        r0 += n
        out[name] = t.transpose(1, 0, 2).reshape(rows, S * cols) if name in col_sharded else t.reshape(S * rows, cols)
    return out


def _pack_grads(grads, names):
    col_sharded = {"w_in", "w_branch_na", "w_branch_dil", "w_up", "w_ple_proj"}
    per_chip = []
    for s in range(N_CHIPS):
        rows = []
        for name in names:
            g = grads[name]
            if name in col_sharded:
                w = g.shape[1] // N_CHIPS
                rows.append(_pack_rows(g[:, s * w:(s + 1) * w]))
            else:
                h = g.shape[0] // N_CHIPS
                rows.append(_pack_rows(g[s * h:(s + 1) * h]))
        per_chip.append(jnp.concatenate(rows, axis=0))
    return jnp.stack(per_chip)


def _unpack_shard(packed, shapes, names):
    out, r0 = {}, 0
    for name in names:
        rows, cols = shapes[name]
        n = rows * cols // PACK_W
        out[name] = packed[r0:r0 + n].reshape(rows, cols)
        r0 += n
    return out


def kernel(x, p, positions, g_mix, w_in, rpb, w_branch_na, w_branch_dil, w_out, g_mlp, w_up, w_down, g_ple, w_ple_gate, w_ple_proj, g_final, loss_target, m_g_mix, m_w_in, m_rpb, m_w_branch_na, m_w_branch_dil, m_w_out, m_g_mlp, m_w_up, m_w_down, m_g_ple, m_w_ple_gate, m_w_ple_proj, m_g_final, v_g_mix, v_w_in, v_rpb, v_w_branch_na, v_w_branch_dil, v_w_out, v_g_mlp, v_w_up, v_w_down, v_g_ple, v_w_ple_gate, v_w_ple_proj, v_g_final):
    shards = {"w_in": w_in[0], "w_branch_na": w_branch_na[0], "w_branch_dil": w_branch_dil[0], "w_out": w_out[0],
              "w_up": w_up[0], "w_down": w_down[0], "w_ple_gate": w_ple_gate[0], "w_ple_proj": w_ple_proj[0]}
    m_shards = {"w_in": m_w_in[0], "w_branch_na": m_w_branch_na[0], "w_branch_dil": m_w_branch_dil[0], "w_out": m_w_out[0],
                "w_up": m_w_up[0], "w_down": m_w_down[0], "w_ple_gate": m_w_ple_gate[0], "w_ple_proj": m_w_ple_proj[0]}
    v_shards = {"w_in": v_w_in[0], "w_branch_na": v_w_branch_na[0], "w_branch_dil": v_w_branch_dil[0], "w_out": v_w_out[0],
                "w_up": v_w_up[0], "w_down": v_w_down[0], "w_ple_gate": v_w_ple_gate[0], "w_ple_proj": v_w_ple_proj[0]}

    W = {"w_in": _gather_weights(shards["w_in"].astype(BF)).transpose(1, 0, 2).reshape(D_MODEL, -1)}
    mix_flight = _gather_start(jnp.concatenate([_pack_rows(shards[n].astype(BF)) for n in GATHER_MIX], axis=0), "mix")
    rest_flight = _gather_start(jnp.concatenate([_pack_rows(shards[n].astype(BF)) for n in GATHER_MLP], axis=0), "mlp",
                                after=(mix_flight.token,))
    w_qkv, w_gates = W["w_in"][:, :QKV_WIDTH], W["w_in"][:, QKV_WIDTH:]

    xs, ps, tgt = x[0], p[0, 0], loss_target[0]
    T = xs.shape[0]
    TM = 512
    gm, gl, gp, gf = g_mix, g_mlp, g_ple, g_final.reshape(1, D_MODEL)
    cos2, sin_signed = _rope_tables(positions[0])
    tab = _na_bias_table(rpb[0])

    a = _rowwise("norm_mix", lambda h, g: h * _rms(h) * g, T, TM, [_row(xs, TM), _full(gm)], [(D_MODEL, BF)],
                 after=(rest_flight.token,))
    n3 = 3 * NA_WIDTH
    qkv = _mm("in_na", a, w_qkv[:, :n3], "nn", 1024, 768, 1024, [BF])
    z_dil = _mm("in_dil", a, w_qkv[:, n3:], "nn", 1024, 1152, 1024, [F32])
    z_gates = _mm("in_gates", a, w_gates, "nn", 1024,1024, 1024, [BF])

    dil_ops = _qkv_prep(z_dil, cos2, sin_signed, TM)
    y_na = _na_fwd(qkv, tab)
    band = [_band_fwd(*dil_ops[g], g) for g in range(len(DIL_GROUPS))]
    y_dil, w_grp, o_nat = _dil_merge_fwd([b[0] for b in band], [b[1] for b in band], T, TM)

    W.update(_unpack_weights(_gather_wait(mix_flight, y_dil, "mix"), GATHER_MIX))
    u_na = _mm("branch_na", y_na, W["w_branch_na"], "nn", 1024,1024, 512, [BF])
    u_dil = _mm("branch_dil", y_dil, W["w_branch_dil"], "nn", 1024,1024, 256, [BF])
    mixed = _rowwise(
        "gate_mix", lambda gn, gd, un, ud: _sigmoid(gn.astype(F32)) * un.astype(F32) + _sigmoid(gd.astype(F32)) * ud.astype(F32), T, TM,
        [_row(z_gates, TM, 0, D_MODEL), _row(z_gates, TM, 1, D_MODEL), _row(u_na, TM), _row(u_dil, TM)], [(D_MODEL, BF)])
    def add_norm(d, h, g):
        h = h + d
        return h, h * _rms(h) * g

    h1, cn = _mm("out_proj", mixed, W["w_out"], "nn", 512, 1024, 1024, [F32, BF], epilogue=add_norm, extras=(xs,), consts=(gl,))
    W.update(_unpack_weights(_gather_wait(rest_flight, cn, "mlp"), GATHER_MLP))
    up, act = _mm("mlp_up", cn, W["w_up"], "nn", 1024,1024, 1024, [BF, BF],
                  epilogue=lambda acc: (acc, jnp.square(jnp.maximum(acc, 0.0))))
    h2, en = _mm("mlp_down", act, W["w_down"], "nn", 1024, 1024, 1024, [F32, BF], epilogue=add_norm, extras=(h1,), consts=(gp,))
    gt = _mm("ple_gate", en, W["w_ple_gate"], "nn", 1024,1024, 1024, [F32])
    pp = _mm("ple_proj", ps, W["w_ple_proj"], "nn", 1024,1024, 256, [F32])

    def head(h2t, gtt, ppt, tg, g):
        sg = _sigmoid(gtt)
        h3 = h2t + sg * ppt
        yo = h3 * _rms(h3) * g
        diff = yo - tg
        loss = 0.5 * jnp.sum(jnp.mean(jnp.square(diff), axis=-1, keepdims=True), axis=0, keepdims=True)
        dh3, dg = _rms_bwd(diff * (1.0 / D_MODEL), h3, g)
        return dh3, dh3 * ppt * sg * (1.0 - sg), dh3 * sg, jnp.broadcast_to(loss, (1, 128)), dg

    dh3, d_gt, d_pp, loss_part, dg_final = _rowwise(
        "loss_head", head, T, TM, [_row(h2, TM), _row(gt, TM), _row(pp, TM), _row(tgt, TM), _full(gf)],
        [(D_MODEL, F32), (D_MODEL, BF), (D_MODEL, BF)], sums=[128, D_MODEL])

    early_shapes = {n: shards[n].shape for n in REDUCE_EARLY}
    early_rows = sum(r * c for r, c in early_shapes.values()) // PACK_W
    shard_rows = D_MODEL // N_CHIPS
    early_buf = _mm("g_ple_gate", en, d_gt, "tn", shard_rows, 1024, 1024, [F32],
                    into=(jax.ShapeDtypeStruct((N_CHIPS, early_rows, PACK_W), F32), (None, shard_rows, PACK_W),
                          lambda i, j: (i, 2 * D_MODEL // shard_rows, 0)))
    g_ple_proj = _mm("g_ple_proj", ps, d_pp, "tn", 256, 1024, 1024,[F32])

    def add_norm_bwd(dn, dh_out, h, g):
        dh, dg = _rms_bwd(dn, h, g)
        dh = dh_out + dh
        return dh, dh, dg

    dh2, dh2_b, dg_ple = _mm("d_ple_gate", d_gt, W["w_ple_gate"], "nt", 512, 1024, 1024, [F32, BF],
                             epilogue=add_norm_bwd, extras=(dh3, h2), consts=(gp,), sums=[D_MODEL])
    d_up = _mm("d_mlp_down", dh2_b, W["w_down"], "nt", 1024,1024, 1024, [BF],
               epilogue=lambda acc, u: (acc * (2.0 * jnp.maximum(u.astype(F32), 0.0)),), extras=(up,))
    early_buf = _mm("g_mlp_down", act, dh2_b, "tn", 1024, 1024, 1024,[F32],
                    into=(early_buf, (None, D_MODEL, PACK_W), lambda i, j: (i, 1, 0)))
    early_buf = _mm("g_mlp_up", cn, d_up, "tn", 1024, 1024, 1024,[F32],
                    into=(early_buf, (None, D_MODEL, PACK_W), lambda i, j: (j, 0, 0)))
    dh1, dh1_b, dg_mlp = _mm("d_mlp_up", d_up, W["w_up"], "nt", 1024, 1024, 1024, [F32, BF], epilogue=add_norm_bwd,
                             extras=(dh2, h1), consts=(gl,), sums=[D_MODEL])
    d_mixed = _mm("d_out_proj", dh1_b, W["w_out"], "nt", 1024,1024, 1024, [F32])
    early_buf = _mm("g_out_proj", mixed, dh1_b, "tn", shard_rows, 1024, 1024, [F32],
                    into=(early_buf, (None, shard_rows, PACK_W), lambda i, j: (i, 2 * D_MODEL // shard_rows + 1, 0)))

    def gate_bwd(dm, gn, gd, un, ud):
        gn, gd, un, ud = (t.astype(F32) for t in (gn, gd, un, ud))
        sn, sd = _sigmoid(gn), _sigmoid(gd)
        return jnp.concatenate([dm * un * sn * (1.0 - sn), dm * ud * sd * (1.0 - sd)], axis=1), dm * sn, dm * sd

    dz_gates, d_u_na, d_u_dil = _rowwise(
        "gate_mix_bwd", gate_bwd, T, TM,
        [_row(d_mixed, TM), _row(z_gates, TM, 0, D_MODEL), _row(z_gates, TM, 1, D_MODEL), _row(u_na, TM), _row(u_dil, TM)],
        [(2 * D_MODEL, BF), (D_MODEL, BF), (D_MODEL, BF)])
    g_branch_na = _mm("g_branch_na", y_na, d_u_na, "tn", 1024, 1024, 1024,[F32])
    g_branch_dil = _mm("g_branch_dil", y_dil, d_u_dil, "tn", 256, 1024, 1024,[F32])
    small_rows = [jnp.concatenate([_pack_rows(g[:, s * shard_rows:(s + 1) * shard_rows]) for g in (g_ple_proj, g_branch_na, g_branch_dil)],
                                  axis=0) for s in range(N_CHIPS)]
    early_buf = lax.dynamic_update_slice(early_buf, jnp.stack(small_rows), (0, 2 * D_MODEL + 2 * shard_rows, 0))
    early_tm = early_rows // 4
    swap_flight = _swap_start(early_buf, "early")
    d_y_na = _mm("d_branch_na", d_u_na, W["w_branch_na"], "nt", 1024,512, 1024, [BF], after=(swap_flight.token,))
    d_y_dil = _mm("d_branch_dil", d_u_dil, W["w_branch_dil"], "nt", 1024,256, 1024, [F32])

    dqa, dka, dva, dtab = _na_bwd(qkv, tab, d_y_na)
    early_g, early_got = _swap_wait(swap_flight, dqa, "early")
    early_pair, early_pair_b = _pair_sum(early_g, early_got, early_tm)
    scatter_flight = _scatter_start(early_pair_b, "early")
    d_rpb = _na_rpb_grad(dtab)[:, :2 * NA_WIN_ROWS - 1, :2 * NA_WIN_COLS - 1]

    do_res, dlse_res = _dil_merge_bwd(d_y_dil, o_nat, w_grp, TM, after=(scatter_flight.token,))
    d_dil = [_band_bwd(*dil_ops[g], do_res[g], dlse_res[g], g) for g in range(len(DIL_GROUPS))]

    me_chip = 2 * lax.axis_index("x") + lax.axis_index("y")
    early_mine = _chip_sum(lax.dynamic_index_in_dim(early_pair, me_chip, 0, keepdims=False),
                           _scatter_wait(scatter_flight, d_dil[-1][-1], "early"), early_tm)
    join_flight = _join_start(early_mine)
    dz_qkv = _qkv_unprep((dqa, dka, dva), d_dil, cos2, sin_signed, TM, after=(join_flight.token,))
    g_in = jnp.concatenate([
        _mm("g_in_qkv", a, dz_qkv, "tn", 1024, 1280, 1024,[F32]),
        _mm("g_in_gates", a, dz_gates, "tn", 1024, 1024, 1024,[F32])], axis=1)
    in_cols = g_in.shape[1] // N_CHIPS
    late_tm = 256
    late_swap = _swap_start(jnp.stack([g_in[:, s * in_cols:(s + 1) * in_cols] for s in range(N_CHIPS)]), "late")
    d_a = _mm("d_in_qkv", dz_qkv, w_qkv, "nt", 1024,1024, 1280, [F32], after=(late_swap.token,))
    late_g, late_got = _swap_wait(late_swap, d_a, "late")
    late_pair, late_pair_b = _pair_sum(late_g, late_got, late_tm)
    late_scatter = _scatter_start(late_pair_b, "late")
    def first_bwd(dn_gates, dn_qkv, dh_out, h, g):
        dh, dg = _rms_bwd(dn_gates + dn_qkv, h, g)
        return dh_out + dh, dg

    grad_x, dg_mix = _mm("d_in_gates", dz_gates, w_gates, "nt", 512, 1024, 1024, [F32], epilogue=first_bwd,
                         extras=(d_a, dh1, xs), consts=(gm,), sums=[D_MODEL], after=(late_scatter.token,))
    g_shard = _unpack_shard(_join_wait(join_flight, grad_x), early_shapes, REDUCE_EARLY)

    n_rpb = rpb.size
    rpb_rows = 4
    small = jnp.concatenate([
        dg_mix, dg_mlp, dg_ple, dg_final,
        jnp.pad(d_rpb.reshape(-1), (0, rpb_rows * D_MODEL - n_rpb)).reshape(rpb_rows, D_MODEL),
        jnp.pad(loss_part, ((0, 0), (0, D_MODEL - loss_part.shape[1]))),
        jnp.zeros((SMALL_ROWS - 5 - rpb_rows, D_MODEL), F32)], axis=0)
    out = {"grad": {}, "delta": {}, "new_m": {}, "new_v": {}}

    def update(names):
        for n in names:
            out["grad"][n] = g_shard[n][None]
            res = _adamw("adamw_" + n, g_shard[n], shards[n], m_shards[n], v_shards[n])
            for kind, t in zip(("delta", "new_m", "new_v"), res, strict=True):
                out[kind][n] = t[None]

    update(REDUCE_EARLY)
    late_others = _scatter_wait(late_scatter, out["new_v"][REDUCE_EARLY[-1]], "late")
    late_mine = _chip_sum(lax.dynamic_index_in_dim(late_pair, me_chip, 0, keepdims=False), late_others, late_tm)
    small = _allreduce_small(small, after=(late_mine,))
    g_shard["w_in"] = _join_halves(late_mine)
    update(("w_in",))
    loss = small[4 + rpb_rows, 0]

    def small_pack(a0, a1, a2, a3, r):
        return jnp.concatenate([a0.reshape(1, -1), a1.reshape(1, -1), a2.reshape(1, -1), a3.reshape(1, -1),
                                jnp.pad(r.reshape(-1), (0, rpb_rows * D_MODEL - n_rpb)).reshape(rpb_rows, D_MODEL)], axis=0)

    g_small = small[:4 + rpb_rows]
    small_res = _adamw("adamw_small", g_small, small_pack(g_mix, g_mlp, g_ple, g_final, rpb),
                       small_pack(m_g_mix, m_g_mlp, m_g_ple, m_g_final, m_rpb), small_pack(v_g_mix, v_g_mlp, v_g_ple, v_g_final, v_rpb))

    def small_unpack(t):
        return {"g_mix": t[0].reshape(g_mix.shape), "g_mlp": t[1].reshape(g_mlp.shape), "g_ple": t[2].reshape(g_ple.shape),
                "g_final": t[3].reshape(g_final.shape), "rpb": t[4:].reshape(-1)[:n_rpb].reshape(rpb.shape)}

    for kind, t in zip(("grad", "delta", "new_m", "new_v"), (g_small, *small_res), strict=True):
        out[kind].update(small_unpack(t))

    order = ["g_mix", "w_in", "rpb", "w_branch_na", "w_branch_dil", "w_out", "g_mlp", "w_up", "w_down", "g_ple",
             "w_ple_gate", "w_ple_proj", "g_final"]
    return (loss, grad_x[None], *[out["grad"][n] for n in order], *[out["delta"][n] for n in order],
            *[out["new_m"][n] for n in order], *[out["new_v"][n] for n in order])
```

```python
import functools
from typing import NamedTuple

import numpy as np
import jax
import jax.numpy as jnp
from jax import lax
from jax.experimental import pallas as pl
from jax.experimental.pallas import tpu as pltpu

BF = jnp.bfloat16
F32 = jnp.float32
MESH = pl.DeviceIdType.MESH
ANY = pl.BlockSpec(memory_space=pl.ANY)

V7X_VMEM_BYTES = 64 * 1024 * 1024
VMEM_LIMIT = V7X_VMEM_BYTES - 16 * 1024 * 1024

D_MODEL = 1024
HEAD_DIM = 64
GRID_W = 64
NA_HEADS = 8
NA_WIN_ROWS = 8
NA_WIN_COLS = 16
NA_WIDTH = NA_HEADS * HEAD_DIM
DIL_GROUPS = ((128, 1), (512, 4), (2048, 16))
DIL_HPG = 4
DIL_HEADS = DIL_HPG * len(DIL_GROUPS)
DIL_WIDTH = DIL_HEADS * HEAD_DIM
DIL_OUT_WIDTH = DIL_HPG * HEAD_DIM
DIL_RADIUS = 64
QKV_WIDTH = 3 * NA_WIDTH + 3 * DIL_WIDTH
D_FF = 4 * D_MODEL
PLE_DIM = 256
ROPE_THETA = 10000.0
RMS_EPS = 1e-6
NEG_INF = -1e30
Q_SCALE = HEAD_DIM ** -0.5

ADAM_LR = 0.001
ADAM_B1 = 0.9
ADAM_B2 = 0.999
ADAM_EPS = 1e-08
ADAM_WD = 0.01
ADAM_STEP = 10

N_CHIPS = 4
N_DEV = 8
PACK_W = 1024
BIG = ("w_in", "w_branch_na", "w_branch_dil", "w_out", "w_up", "w_down", "w_ple_gate", "w_ple_proj")
GATHER_FIRST = ("w_in",)
GATHER_MIX = ("w_branch_na", "w_branch_dil", "w_out")
GATHER_MLP = ("w_up", "w_down", "w_ple_gate", "w_ple_proj")
REDUCE_EARLY = ("w_up", "w_down", "w_ple_gate", "w_out", "w_ple_proj", "w_branch_na", "w_branch_dil")
SMALL_ROWS = 16


def _cparams(sem=None):
    return pltpu.CompilerParams(dimension_semantics=sem, vmem_limit_bytes=VMEM_LIMIT)


def _mm(name, a, b, mode, tm, tn, tk, out_dtypes, epilogue=None, extras=(), consts=(), sums=(), after=(), into=None):
    if mode == "nn":
        (M, K), N = a.shape, b.shape[1]
    elif mode == "nt":
        (M, K), N = a.shape, b.shape[0]
    else:
        (K, M), N = a.shape, b.shape[1]
    tm, tn, tk = min(tm, M), min(tn, N), min(tk, K)
    assert M % tm == 0 and N % tn == 0 and K % tk == 0, (name, M, N, K, tm, tn, tk)
    if mode == "nn":
        a_spec = pl.BlockSpec((tm, tk), lambda i, j, k: (i, k))
        b_spec = pl.BlockSpec((tk, tn), lambda i, j, k: (k, j))
        dims = (((1,), (0,)), ((), ()))
    elif mode == "nt":
        a_spec = pl.BlockSpec((tm, tk), lambda i, j, k: (i, k))
        b_spec = pl.BlockSpec((tn, tk), lambda i, j, k: (j, k))
        dims = (((1,), (1,)), ((), ()))
    else:
        a_spec = pl.BlockSpec((tk, tm), lambda i, j, k: (k, i))
        b_spec = pl.BlockSpec((tk, tn), lambda i, j, k: (k, j))
        dims = (((0,), (0,)), ((), ()))
    nk = K // tk
    n_extra, n_const, n_out, n_sum = len(extras), len(consts), len(out_dtypes), len(sums)
    tile = pl.BlockSpec((tm, tn), lambda i, j, k: (i, j))
    assert not sums or tn == N, "row sums need whole rows in a tile"

    n_after = len(after)

    def body(a_ref, b_ref, *rest):
        extra_refs, rest = rest[:n_extra + n_const], rest[n_extra + n_const + n_after:]
        out_refs, sum_refs, acc = rest[:n_out], rest[n_out:n_out + n_sum], rest[-1]
        i, k = pl.program_id(0), pl.program_id(2)

        @pl.when(k == 0)
        def _():
            acc[...] = jnp.zeros_like(acc)

        acc[...] += lax.dot_general(a_ref[...].astype(BF), b_ref[...].astype(BF), dims, preferred_element_type=F32)

        @pl.when(k == nk - 1)
        def _():
            outs = (acc[...],) if epilogue is None else epilogue(acc[...], *[e[...] for e in extra_refs])
            for o_ref, val in zip(out_refs, outs[:n_out], strict=True):
                o_ref[...] = val.astype(o_ref.dtype)
            for s_ref, val in zip(sum_refs, outs[n_out:], strict=True):
                @pl.when(i == 0)
                def _():
                    s_ref[...] = val

                @pl.when(i != 0)
                def _():
                    s_ref[...] += val

    out_specs = [tile] * n_out + [pl.BlockSpec((1, c), lambda i, j, k: (0, 0)) for c in sums]
    out_shape = [jax.ShapeDtypeStruct((M, N), dt) for dt in out_dtypes] + [jax.ShapeDtypeStruct((1, c), F32) for c in sums]
    operands, aliases = [a, b, *extras, *consts, *after], {}
    in_specs = ([a_spec, b_spec] + [tile] * n_extra
                + [pl.BlockSpec(c.shape, functools.partial(lambda nd, i, j, k: (0,) * nd, c.ndim)) for c in consts] + [ANY] * n_after)
    if into is not None:
        assert n_out == 1
        target, block, index = into
        out_specs = [pl.BlockSpec(block, lambda i, j, k: index(i, j))]
        out_shape = [jax.ShapeDtypeStruct(target.shape, target.dtype)]
        if not isinstance(target, jax.ShapeDtypeStruct):
            aliases = {len(operands): 0}
            operands.append(target)
            in_specs.append(ANY)
            n_after += 1

    outs = pl.pallas_call(
        body, name=name, grid=(M // tm, N // tn, nk),
        in_specs=in_specs, out_specs=out_specs, out_shape=out_shape,
        scratch_shapes=[pltpu.VMEM((tm, tn), F32)], input_output_aliases=aliases,
        compiler_params=_cparams(("arbitrary",) * 3 if sums else ("parallel", "parallel", "arbitrary")),
    )(*operands)
    return outs[0] if len(outs) == 1 else outs


def _row(arr, tm, col_block=None, width=None):
    width = arr.shape[1] if width is None else width
    cb = 0 if col_block is None else col_block
    return arr, pl.BlockSpec((tm, width), lambda i: (i, cb))


def _full(arr):
    nd = arr.ndim
    return arr, pl.BlockSpec(arr.shape, lambda i: (0,) * nd)


def _rowwise(name, body, T, tm, ins, outs, sums=(), after=()):
    n_in, n_out, n_sum, n_after = len(ins), len(outs), len(sums), len(after)

    def kern(*refs):
        in_refs, refs = refs[:n_in], refs[n_in + n_after:]
        out_refs, sum_refs = refs[:n_out], refs[n_out:]
        res = body(*[r[...] for r in in_refs])
        res = res if isinstance(res, tuple) else (res,)
        for o_ref, val in zip(out_refs, res[:n_out], strict=True):
            o_ref[...] = val.astype(o_ref.dtype)
        if n_sum:
            @pl.when(pl.program_id(0) == 0)
            def _():
                for s_ref in sum_refs:
                    s_ref[...] = jnp.zeros_like(s_ref)

            for s_ref, val in zip(sum_refs, res[n_out:], strict=True):
                s_ref[...] += val

    res = pl.pallas_call(
        kern, name=name, grid=(T // tm,),
        in_specs=[spec for _, spec in ins] + [ANY] * n_after,
        out_specs=[pl.BlockSpec((tm, c), lambda i: (i, 0)) for c, _ in outs]
        + [pl.BlockSpec((1, c), lambda i: (0, 0)) for c in sums],
        out_shape=[jax.ShapeDtypeStruct((T, c), dt) for c, dt in outs]
        + [jax.ShapeDtypeStruct((1, c), F32) for c in sums],
        compiler_params=_cparams(("arbitrary",)),
    )(*[a for a, _ in ins], *after)
    return res[0] if len(res) == 1 else res


def _sigmoid(x):
    return 1.0 / (1.0 + jnp.exp(-x))


def _rms(h):
    return lax.rsqrt(jnp.mean(h * h, axis=-1, keepdims=True) + RMS_EPS)


def _rms_bwd(dy, h, g):
    r = _rms(h)
    n = h * r
    dn = dy * g
    dh = r * (dn - n * jnp.mean(dn * n, axis=-1, keepdims=True))
    return dh, jnp.sum(dy * n, axis=0, keepdims=True)


def _rope(x, cos2, sin_signed):
    lane = lax.broadcasted_iota(jnp.int32, x.shape, 1)
    swapped = jnp.where((lane % HEAD_DIM) < HEAD_DIM // 2, pltpu.roll(x, 128 - HEAD_DIM // 2, 1), pltpu.roll(x, HEAD_DIM // 2, 1))
    return x * cos2 + swapped * sin_signed


def _rope_cols(x, cos2, sin_signed):
    return jnp.concatenate([_rope(x[:, c:c + 128], cos2, sin_signed) for c in range(0, x.shape[1], 128)], axis=1)


NA_KEYS = NA_WIN_ROWS * GRID_W
NA_BASES = 8


def _na_row_geometry(r, rows):
    first = jnp.clip(r - NA_WIN_ROWS // 2, 0, rows - NA_WIN_ROWS)
    base = first - r + (NA_WIN_ROWS - 1)
    return pl.multiple_of(first * GRID_W, GRID_W), base


NA_ROWS_PER_STEP = 8
NA_BWD_ROWS_PER_STEP = 8


def _softmax_rows(s):
    p = jnp.exp(s - jnp.max(s, axis=-1, keepdims=True))
    return p / jnp.sum(p, axis=-1, keepdims=True)


def _na_probs(q, kw, bias):
    return _softmax_rows(lax.dot_general(q, kw, (((1,), (1,)), ((), ())), preferred_element_type=F32) + bias)


def _split_pair(t):
    first = lax.broadcasted_iota(jnp.int32, t.shape, 1) < HEAD_DIM
    zero = jnp.zeros_like(t)
    return jnp.where(first, t, zero), jnp.where(first, zero, t)


def _join_pair(a, b):
    return jnp.where(lax.broadcasted_iota(jnp.int32, a.shape, 1) < HEAD_DIM, a, b)


_NT = (((1,), (1,)), ((), ()))
_TN = (((0,), (0,)), ((), ()))


def _na_fwd(qkv, tab):
    T = qkv.shape[0]
    rows = T // GRID_W
    n_pairs = NA_WIDTH // 128

    def body(q_ref, k_ref, v_ref, tab_ref, y_ref):
        def step(it, carry):
            geo = [_na_row_geometry(it * NA_ROWS_PER_STEP + u, rows) for u in range(NA_ROWS_PER_STEP)]
            q0s = [pl.multiple_of((it * NA_ROWS_PER_STEP + u) * GRID_W, GRID_W) for u in range(NA_ROWS_PER_STEP)]
            ss = [lax.dot_general(jnp.concatenate(_split_pair(q_ref[pl.ds(q0, GRID_W), :] * Q_SCALE), axis=0),
                                  k_ref[pl.ds(k0, NA_KEYS), :], _NT, preferred_element_type=F32)
                  for q0, (k0, _) in zip(q0s, geo)]
            ps = [_softmax_rows(s + jnp.concatenate([tab_ref[0, base], tab_ref[1, base]], axis=0)) for s, (_, base) in zip(ss, geo)]
            ys = [jnp.dot(p.astype(BF), v_ref[pl.ds(k0, NA_KEYS), :], preferred_element_type=F32) for p, (k0, _) in zip(ps, geo)]
            for q0, y2 in zip(q0s, ys):
                y_ref[pl.ds(q0, GRID_W), :] = _join_pair(y2[:GRID_W], y2[GRID_W:]).astype(y_ref.dtype)
            return carry

        lax.fori_loop(0, rows // NA_ROWS_PER_STEP, step, 0)

    def cols(first):
        return pl.BlockSpec((T, 128), lambda j: (0, first + j))

    return pl.pallas_call(
        body, name="na_fwd", grid=(n_pairs,),
        in_specs=[cols(0), cols(n_pairs), cols(2 * n_pairs), pl.BlockSpec((2, NA_BASES, GRID_W, NA_KEYS), lambda j: (j, 0, 0, 0))],
        out_specs=cols(0), out_shape=jax.ShapeDtypeStruct((T, NA_WIDTH), BF),
        compiler_params=_cparams(("parallel",)),
    )(qkv, qkv, qkv, tab)


def _na_bwd(qkv, tab, do):
    T = qkv.shape[0]
    rows = T // GRID_W
    n_pairs = NA_WIDTH // 128

    def body(q_ref, k_ref, v_ref, tab_ref, do_ref, dq_ref, dk_ref, dv_ref, dtab_ref):
        dk_ref[...] = jnp.zeros_like(dk_ref)
        dv_ref[...] = jnp.zeros_like(dv_ref)
        dtab_ref[...] = jnp.zeros_like(dtab_ref)

        def step(it, carry):
            U = NA_BWD_ROWS_PER_STEP
            geo = [_na_row_geometry(it * U + u, rows) for u in range(U)]
            q0s = [pl.multiple_of((it * U + u) * GRID_W, GRID_W) for u in range(U)]
            q2s = [jnp.concatenate(_split_pair(q_ref[pl.ds(q0, GRID_W), :] * Q_SCALE), axis=0) for q0 in q0s]
            do2s = [jnp.concatenate(_split_pair(do_ref[pl.ds(q0, GRID_W), :]), axis=0) for q0 in q0s]
            ss = [lax.dot_general(q2, k_ref[pl.ds(k0, NA_KEYS), :], _NT, preferred_element_type=F32) for q2, (k0, _) in zip(q2s, geo)]
            dps = [lax.dot_general(do2, v_ref[pl.ds(k0, NA_KEYS), :], _NT, preferred_element_type=F32) for do2, (k0, _) in zip(do2s, geo)]
            ps = [_softmax_rows(s + jnp.concatenate([tab_ref[0, base], tab_ref[1, base]], axis=0)) for s, (_, base) in zip(ss, geo)]
            dss = [p * (dp - jnp.sum(dp * p, axis=-1, keepdims=True)) for p, dp in zip(ps, dps)]
            dvs = [lax.dot_general(p.astype(BF), do2, _TN, preferred_element_type=F32) for p, do2 in zip(ps, do2s)]
            dsbs = [ds.astype(BF) for ds in dss]
            dqs = [jnp.dot(dsb, k_ref[pl.ds(k0, NA_KEYS), :], preferred_element_type=F32) for dsb, (k0, _) in zip(dsbs, geo)]
            dks = [lax.dot_general(dsb, q2, _TN, preferred_element_type=F32) for dsb, q2 in zip(dsbs, q2s)]
            for u in range(U):
                k0, base = geo[u]
                dtab_ref[0, base] += dss[u][:GRID_W]
                dtab_ref[1, base] += dss[u][GRID_W:]
                dq_ref[pl.ds(q0s[u], GRID_W), :] = _join_pair(dqs[u][:GRID_W], dqs[u][GRID_W:])
                dk_ref[pl.ds(k0, NA_KEYS), :] += dks[u]
                dv_ref[pl.ds(k0, NA_KEYS), :] += dvs[u]
            return carry

        lax.fori_loop(0, rows // NA_BWD_ROWS_PER_STEP, step, 0)

    def cols(first):
        return pl.BlockSpec((T, 128), lambda j: (0, first + j))

    tabs = pl.BlockSpec((2, NA_BASES, GRID_W, NA_KEYS), lambda j: (j, 0, 0, 0))
    wide = jax.ShapeDtypeStruct((T, NA_WIDTH), F32)
    return pl.pallas_call(
        body, name="na_bwd", grid=(n_pairs,),
        in_specs=[cols(0), cols(n_pairs), cols(2 * n_pairs), tabs, cols(0)],
        out_specs=[cols(0), cols(0), cols(0), tabs],
        out_shape=[wide, wide, wide, jax.ShapeDtypeStruct((NA_HEADS, NA_BASES, GRID_W, NA_KEYS), F32)],
        compiler_params=_cparams(("parallel",)),
    )(qkv, qkv, qkv, tab, do)


def _na_bias_table(rpb):
    H, n_rows, n_cols = rpb.shape

    def body(r_ref, tab_ref):
        q = lax.broadcasted_iota(jnp.int32, (GRID_W, 128), 0)
        kc = lax.broadcasted_iota(jnp.int32, (GRID_W, 128), 1)
        first = jnp.clip(q - NA_WIN_COLS // 2, 0, GRID_W - NA_WIN_COLS)
        valid = (kc >= first) & (kc < first + NA_WIN_COLS)
        toeplitz = []
        for ro in range(n_rows):
            row = jnp.broadcast_to(r_ref[pl.ds(ro, 1), :], (GRID_W, 128))
            shifted = pltpu.roll(pltpu.roll(row, 128 - (NA_WIN_COLS - 1), 1), 0, 1, stride=1, stride_axis=0)
            toeplitz.append(jnp.where(valid, shifted, NEG_INF))
        for base in range(NA_BASES):
            for j in range(NA_WIN_ROWS // 2):
                even, odd = toeplitz[base + 2 * j], toeplitz[base + 2 * j + 1]
                tab_ref[base, :, pl.ds(j * 128, 128)] = jnp.where(kc < GRID_W, even, pltpu.roll(odd, GRID_W, 1))

    padded = jnp.pad(rpb, ((0, 0), (0, 16 - n_rows), (0, 128 - n_cols)))
    return pl.pallas_call(
        body, name="na_bias_table", grid=(H,),
        in_specs=[pl.BlockSpec((None, 16, 128), lambda h: (h, 0, 0))],
        out_specs=pl.BlockSpec((None, NA_BASES, GRID_W, NA_KEYS), lambda h: (h, 0, 0, 0)),
        out_shape=jax.ShapeDtypeStruct((H, NA_BASES, GRID_W, NA_KEYS), F32),
        compiler_params=_cparams(("parallel",)),
    )(padded)


def _na_rpb_grad(dtab):
    H = dtab.shape[0]
    n_rows = 2 * NA_WIN_ROWS - 1
    n_cols = 2 * NA_WIN_COLS - 1

    def body(d_ref, o_ref):
        lane = lax.broadcasted_iota(jnp.int32, (GRID_W, 128), 1)
        low = lane < GRID_W
        out_rows = []
        for ro in range(n_rows):
            acc = jnp.zeros((GRID_W, 128), F32)
            for base in range(NA_BASES):
                i = ro - base
                if not 0 <= i < NA_WIN_ROWS:
                    continue
                pair = d_ref[base, :, pl.ds((i // 2) * 128, 128)]
                if i % 2:
                    pair = pltpu.roll(pair, GRID_W, 1)
                acc = acc + jnp.where(low, pair, 0.0)
            skew = pltpu.roll(acc, 0, 1, stride=1, stride_axis=0)
            diag = jnp.sum(skew, axis=0, keepdims=True)
            out_rows.append(pltpu.roll(jnp.broadcast_to(diag, (8, 128)), 128 - (GRID_W - NA_WIN_COLS), 1)[:1])
        out_rows.append(jnp.zeros((1, 128), F32))
        res = jnp.concatenate(out_rows, axis=0)
        o_ref[...] = jnp.where(lax.broadcasted_iota(jnp.int32, res.shape, 1) < n_cols, res, 0.0)

    return pl.pallas_call(
        body, name="na_rpb_grad", grid=(H,),
        in_specs=[pl.BlockSpec((None, NA_BASES, GRID_W, NA_KEYS), lambda h: (h, 0, 0, 0))],
        out_specs=pl.BlockSpec((None, n_rows + 1, 128), lambda h: (h, 0, 0)),
        out_shape=jax.ShapeDtypeStruct((H, n_rows + 1, 128), F32),
        compiler_params=_cparams(("parallel",)),
    )(jnp.flip(dtab, axis=2))


BAND_Q = 128
BAND_KEYS = BAND_Q + 2 * DIL_RADIUS


def _band_geometry(n, L):
    q0 = pl.multiple_of(n * BAND_Q, BAND_Q)
    k0 = pl.multiple_of(jnp.clip(q0 - DIL_RADIUS, 0, L - BAND_KEYS), DIL_RADIUS)
    qi = q0 + lax.broadcasted_iota(jnp.int32, (BAND_Q, BAND_KEYS), 0)
    kj = k0 + lax.broadcasted_iota(jnp.int32, (BAND_Q, BAND_KEYS), 1)
    return q0, k0, jnp.abs(qi - kj) <= DIL_RADIUS


DIL_PAIRS = DIL_OUT_WIDTH // 128


def _residue_shape(dil, T, dtype):
    return jax.ShapeDtypeStruct((DIL_PAIRS, dil, T // dil, 128), dtype)


def _residue_tile(dil, tm):
    return pl.BlockSpec((DIL_PAIRS, dil, tm // dil, 128), lambda i: (0, 0, i, 0))


def _to_natural(ref, scratch, dil, tm):
    tiles = []
    for pair in range(DIL_PAIRS):
        if dil == 1:
            tiles.append(ref[pair, 0].astype(F32))
            continue
        for r in range(dil):
            scratch[pl.ds(r, tm // dil, stride=dil), :] = ref[pair, r].astype(F32)
        tiles.append(scratch[...])
    return tiles


def _from_natural(tile, scratch, ref, pair, dil, tm):
    if dil == 1:
        ref[pair, 0] = tile.astype(ref.dtype)
        return
    scratch[...] = tile
    for r in range(dil):
        ref[pair, r] = scratch[pl.ds(r, tm // dil, stride=dil), :].astype(ref.dtype)


def _band_specs(group, T):
    dil = DIL_GROUPS[group][1]
    L = T // dil
    assert L % BAND_Q == 0 and L >= BAND_KEYS, (T, dil)
    return L, (dil * DIL_PAIRS,), pl.BlockSpec((None, None, L, 128), lambda s: (s % DIL_PAIRS, s // DIL_PAIRS, 0, 0))


BAND_BLOCKS_PER_STEP = 4


def _band_softmax(s, valid):
    s = jnp.where(valid, s, NEG_INF)
    m = jnp.max(s, axis=-1, keepdims=True)
    p = jnp.exp(s - m)
    l = jnp.sum(p, axis=-1, keepdims=True)
    return p / l, m + jnp.log(l)


def _band_fwd(q, k, v, group):
    T = q.shape[1] * q.shape[2]
    L, grid, spec = _band_specs(group, T)
    U = min(BAND_BLOCKS_PER_STEP, L // BAND_Q)

    def body(q_ref, k_ref, v_ref, o_ref, lse_ref):
        def step(it, carry):
            geo = [_band_geometry(it * U + u, L) for u in range(U)]
            ss = [lax.dot_general(jnp.concatenate(_split_pair(q_ref[pl.ds(q0, BAND_Q), :]), axis=0),
                                  k_ref[pl.ds(k0, BAND_KEYS), :], _NT, preferred_element_type=F32) for q0, k0, _ in geo]
            pls = [_band_softmax(s, jnp.concatenate([valid, valid], axis=0)) for s, (_, _, valid) in zip(ss, geo)]
            os = [jnp.dot(p.astype(BF), v_ref[pl.ds(k0, BAND_KEYS), :], preferred_element_type=F32) for (p, _), (_, k0, _) in zip(pls, geo)]
            for (q0, _, _), o2, (_, lse) in zip(geo, os, pls):
                o_ref[pl.ds(q0, BAND_Q), :] = _join_pair(o2[:BAND_Q], o2[BAND_Q:])
                lse2 = jnp.broadcast_to(lse, (2 * BAND_Q, 128))
                lse_ref[pl.ds(q0, BAND_Q), :] = _join_pair(lse2[:BAND_Q], lse2[BAND_Q:])
            return carry

        lax.fori_loop(0, L // (BAND_Q * U), step, 0)

    res = _residue_shape(DIL_GROUPS[group][1], T, F32)
    return pl.pallas_call(
        body, name=f"band_fwd_g{group}", grid=grid,
        in_specs=[spec] * 3, out_specs=[spec] * 2, out_shape=[res, res],
        compiler_params=_cparams(("parallel",)),
    )(q, k, v)


def _band_bwd(q, k, v, do, dlse, group):
    T = q.shape[1] * q.shape[2]
    L, grid, spec = _band_specs(group, T)
    U = min(BAND_BLOCKS_PER_STEP, L // BAND_Q)

    def body(q_ref, k_ref, v_ref, do_ref, dlse_ref, dq_ref, dk_ref, dv_ref):
        dk_ref[...] = jnp.zeros_like(dk_ref)
        dv_ref[...] = jnp.zeros_like(dv_ref)

        def step(it, carry):
            geo = [_band_geometry(it * U + u, L) for u in range(U)]
            q2s = [jnp.concatenate(_split_pair(q_ref[pl.ds(q0, BAND_Q), :]), axis=0) for q0, _, _ in geo]
            do2s = [jnp.concatenate(_split_pair(do_ref[pl.ds(q0, BAND_Q), :]), axis=0) for q0, _, _ in geo]
            ss = [lax.dot_general(q2, k_ref[pl.ds(k0, BAND_KEYS), :], _NT, preferred_element_type=F32) for q2, (_, k0, _) in zip(q2s, geo)]
            dps = [lax.dot_general(do2, v_ref[pl.ds(k0, BAND_KEYS), :], _NT, preferred_element_type=F32) for do2, (_, k0, _) in zip(do2s, geo)]
            ps = [_band_softmax(s, jnp.concatenate([valid, valid], axis=0))[0] for s, (_, _, valid) in zip(ss, geo)]
            dss = []
            for p, dp, (q0, _, _) in zip(ps, dps, geo):
                dl = dlse_ref[pl.ds(q0, BAND_Q), :]
                dl2 = jnp.concatenate([dl[:, :1], dl[:, HEAD_DIM:HEAD_DIM + 1]], axis=0)
                dss.append(p * (dp - jnp.sum(dp * p, axis=-1, keepdims=True) + dl2))
            dvs = [lax.dot_general(p.astype(BF), do2, _TN, preferred_element_type=F32) for p, do2 in zip(ps, do2s)]
            dsbs = [ds.astype(BF) for ds in dss]
            dqs = [jnp.dot(dsb, k_ref[pl.ds(k0, BAND_KEYS), :], preferred_element_type=F32) for dsb, (_, k0, _) in zip(dsbs, geo)]
            dks = [lax.dot_general(dsb, q2, _TN, preferred_element_type=F32) for dsb, q2 in zip(dsbs, q2s)]
            for u, (q0, k0, _) in enumerate(geo):
                dq_ref[pl.ds(q0, BAND_Q), :] = _join_pair(dqs[u][:BAND_Q], dqs[u][BAND_Q:])
                dk_ref[pl.ds(k0, BAND_KEYS), :] += dks[u]
                dv_ref[pl.ds(k0, BAND_KEYS), :] += dvs[u]
            return carry

        lax.fori_loop(0, L // (BAND_Q * U), step, 0)

    res = _residue_shape(DIL_GROUPS[group][1], T, F32)
    return pl.pallas_call(
        body, name=f"band_bwd_g{group}", grid=grid,
        in_specs=[spec] * 5, out_specs=[spec] * 3, out_shape=[res] * 3,
        compiler_params=_cparams(("parallel",)),
    )(q, k, v, do, dlse)


def _head_sums(t):
    head = lax.broadcasted_iota(jnp.int32, t.shape, 1) // HEAD_DIM
    out = jnp.zeros_like(t)
    for h in range(t.shape[1] // HEAD_DIM):
        mine = head == h
        out = jnp.where(mine, jnp.sum(jnp.where(mine, t, 0.0), axis=-1, keepdims=True), out)
    return out


def _dil_merge_fwd(os, lses, T, tm):
    G = len(DIL_GROUPS)
    W = DIL_OUT_WIDTH
    dils = [d for _, d in DIL_GROUPS]

    def body(*refs):
        o_refs, lse_refs = refs[:G], refs[G:2 * G]
        y_ref, w_refs, on_refs, scratch = refs[2 * G], refs[2 * G + 1:3 * G + 1], refs[3 * G + 1:4 * G + 1], refs[-1]
        o = [jnp.concatenate(_to_natural(r, scratch, d, tm), axis=1) for r, d in zip(o_refs, dils)]
        ls = [jnp.concatenate(_to_natural(r, scratch, d, tm), axis=1) for r, d in zip(lse_refs, dils)]
        m = functools.reduce(jnp.maximum, ls)
        es = [jnp.exp(l - m) for l in ls]
        tot = functools.reduce(jnp.add, es)
        ws = [e / tot for e in es]
        y_ref[...] = functools.reduce(jnp.add, [w * t for w, t in zip(ws, o)]).astype(y_ref.dtype)
        for g in range(G):
            w_refs[g][...] = ws[g]
            on_refs[g][...] = o[g]

    nat = pl.BlockSpec((tm, W), lambda i: (i, 0))
    res = pl.pallas_call(
        body, name="dil_merge_fwd", grid=(T // tm,),
        in_specs=[_residue_tile(d, tm) for d in dils] * 2,
        out_specs=[nat] * (2 * G + 1),
        out_shape=[jax.ShapeDtypeStruct((T, W), BF)] + [jax.ShapeDtypeStruct((T, W), F32)] * (2 * G),
        scratch_shapes=[pltpu.VMEM((tm, 128), F32)],
        compiler_params=_cparams(("parallel",)),
    )(*os, *lses)
    return res[0], res[1:G + 1], res[G + 1:]


def _dil_merge_bwd(dy, os, ws, tm, after=()):
    G = len(DIL_GROUPS)
    T, W = dy.shape
    dils = [d for _, d in DIL_GROUPS]
    n_after = len(after)

    def body(*refs):
        dyt = refs[0][...]
        o, w = [r[...] for r in refs[1:G + 1]], [r[...] for r in refs[G + 1:2 * G + 1]]
        refs = refs[2 * G + 1 + n_after:]
        do_refs, dlse_refs, scratch = refs[:G], refs[G:2 * G], refs[-1]
        dws = [_head_sums(dyt * t) for t in o]
        mean = functools.reduce(jnp.add, [a * b for a, b in zip(w, dws)])
        for g, d in enumerate(dils):
            do, dlse = w[g] * dyt, w[g] * (dws[g] - mean)
            for pair in range(DIL_PAIRS):
                cols = slice(pair * 128, (pair + 1) * 128)
                _from_natural(do[:, cols], scratch, do_refs[g], pair, d, tm)
                _from_natural(dlse[:, cols], scratch, dlse_refs[g], pair, d, tm)

    nat = pl.BlockSpec((tm, W), lambda i: (i, 0))
    res = pl.pallas_call(
        body, name="dil_merge_bwd", grid=(T // tm,),
        in_specs=[nat] * (2 * G + 1) + [ANY] * n_after,
        out_specs=[_residue_tile(d, tm) for d in dils] * 2,
        out_shape=[_residue_shape(d, T, BF) for d in dils] + [_residue_shape(d, T, F32) for d in dils],
        scratch_shapes=[pltpu.VMEM((tm, 128), F32)],
        compiler_params=_cparams(("parallel",)),
    )(dy, *os, *ws, *after)
    return res[:G], res[G:]


def _qkv_prep(z, cos2, sin_signed, tm):
    T = z.shape[0]
    G = len(DIL_GROUPS)
    dils = [d for _, d in DIL_GROUPS]
    n_dil_blocks = 3 * DIL_WIDTH // 128

    def body(*refs):
        blocks = refs[:n_dil_blocks]
        cos_ref, sin_ref = refs[n_dil_blocks], refs[1 + n_dil_blocks]
        outs = refs[2 + n_dil_blocks:]
        for part in range(3):
            for g, d in enumerate(dils):
                out = outs[g * 3 + part]
                for pair in range(DIL_PAIRS):
                    blk = blocks[part * (DIL_WIDTH // 128) + g * DIL_PAIRS + pair]
                    for r in range(d):
                        rows = pl.ds(r, tm // d, stride=d) if d > 1 else slice(None)
                        x = blk[rows, :]
                        if part < 2:
                            x = _rope(x, cos_ref[rows, :], sin_ref[rows, :])
                        if part == 0:
                            x = x * Q_SCALE
                        out[pair, r] = x.astype(out.dtype)

    lane_block = [pl.BlockSpec((tm, 128), functools.partial(lambda b, i: (i, b), b)) for b in range(n_dil_blocks)]
    tab = pl.BlockSpec((tm, 128), lambda i: (i, 0))
    res = pl.pallas_call(
        body, name="qkv_prep", grid=(T // tm,),
        in_specs=lane_block + [tab, tab],
        out_specs=[_residue_tile(d, tm) for d in dils for _ in range(3)],
        out_shape=[_residue_shape(d, T, BF) for d in dils for _ in range(3)],
        compiler_params=_cparams(("parallel",)),
    )(*[z] * n_dil_blocks, cos2, sin_signed)
    return [res[3 * g:3 + 3 * g] for g in range(G)]


def _qkv_unprep(d_na, d_dil, cos2, sin_signed, tm, after=()):
    T = d_na[0].shape[0]
    G = len(DIL_GROUPS)
    dils = [d for _, d in DIL_GROUPS]
    n_after = len(after)

    def body(*refs):
        dq, dk, dv = (r[...] for r in refs[:3])
        res_refs = refs[3:3 + 3 * G]
        cs, sn = refs[3 + 3 * G][...], refs[4 + 3 * G][...]
        out, scratch = refs[5 + 3 * G + n_after], refs[-1]
        cols = [dq * Q_SCALE, dk, dv]
        for part in range(3):
            for g, d in enumerate(dils):
                for x in _to_natural(res_refs[g * 3 + part], scratch, d, tm):
                    if part < 2:
                        x = _rope(x, cs, -sn)
                    cols.append(x * Q_SCALE if part == 0 else x)
        out[...] = jnp.concatenate(cols, axis=1).astype(out.dtype)

    wide = pl.BlockSpec((tm, NA_WIDTH), lambda i: (i, 0))
    tab = pl.BlockSpec((tm, 128), lambda i: (i, 0))
    return pl.pallas_call(
        body, name="qkv_unprep", grid=(T // tm,),
        in_specs=[wide] * 3 + [_residue_tile(d, tm) for d in dils for _ in range(3)] + [tab, tab] + [ANY] * n_after,
        out_specs=pl.BlockSpec((tm, QKV_WIDTH), lambda i: (i, 0)),
        out_shape=jax.ShapeDtypeStruct((T, QKV_WIDTH), BF),
        scratch_shapes=[pltpu.VMEM((tm, 128), F32)],
        compiler_params=_cparams(("parallel",)),
    )(*d_na, *[t for g in range(G) for t in d_dil[g]], cos2, sin_signed, *after)


def _rope_tables(positions):
    half = HEAD_DIM // 2
    inv_freq = ROPE_THETA ** (-jnp.arange(half, dtype=F32) / half)
    ang = positions.astype(F32)[:, None] * inv_freq
    cos, sin = jnp.cos(ang), jnp.sin(ang)
    return jnp.tile(jnp.concatenate([cos, cos], axis=1), (1, 2)), jnp.tile(jnp.concatenate([-sin, sin], axis=1), (1, 2))


def _pack_rows(t):
    return t.reshape(-1, PACK_W)


def _me():
    return lax.axis_index("x"), lax.axis_index("y"), lax.axis_index("c")


def _other_chips(x, y):
    return [(1 - x, y), (x, 1 - y), (1 - x, 1 - y)]


def _gather_weights(packed):
    R, W = packed.shape
    half = R // 2

    def body(in_ref, out_ref, send_sems, recv_sems):
        x, y, c = _me()
        sibling = (x, y, 1 - c)
        chips = _other_chips(x, y)

        def block(chip, core):
            return out_ref.at[2 * chip[0] + chip[1], pl.ds(core * half, half), :]

        def copy(k, chip, core, to, src=None):
            return pltpu.make_async_remote_copy(
                src_ref=block(chip, core) if src is None else src, dst_ref=block(chip, core),
                send_sem=send_sems.at[k], recv_sem=recv_sems.at[k], device_id=to, device_id_type=MESH)

        first = [copy(j, (x, y), c, (*chip, c), src=in_ref.at[pl.ds(c * half, half), :]) for j, chip in enumerate(chips)]
        for cp in first:
            cp.start()
        passed = [copy(3 + j, chip, c, sibling) for j, chip in enumerate(chips)]
        for j, chip in enumerate(chips):
            copy(j, chip, c, (x, y, c)).wait_recv()
            passed[j].start()
        for j, chip in enumerate(chips):
            copy(3 + j, chip, 1 - c, (x, y, c)).wait_recv()
        for cp in first + passed:
            cp.wait_send()

    others = pl.pallas_call(
        body, name="gather_weights",
        in_specs=[ANY], out_specs=ANY,
        out_shape=jax.ShapeDtypeStruct((N_CHIPS, R, W), packed.dtype),
        scratch_shapes=[pltpu.SemaphoreType.DMA((6,)), pltpu.SemaphoreType.DMA((6,))],
    )(packed)
    return lax.dynamic_update_slice(others, packed[None], (2 * lax.axis_index("x") + lax.axis_index("y"), 0, 0))


def _swap_halves(g):
    S, R, W = g.shape
    half = R // 2

    def body(g_ref, out_ref, send_sem, recv_sem):
        x, y, c = _me()
        cp = pltpu.make_async_remote_copy(
            src_ref=g_ref.at[:, pl.ds((1 - c) * half, half), :], dst_ref=out_ref,
            send_sem=send_sem, recv_sem=recv_sem, device_id=(x, y, 1 - c), device_id_type=MESH)
        cp.start()
        cp.wait()

    return pl.pallas_call(
        body, name="swap_halves", in_specs=[ANY], out_specs=ANY,
        out_shape=jax.ShapeDtypeStruct((S, half, W), g.dtype),
        scratch_shapes=[pltpu.SemaphoreType.DMA, pltpu.SemaphoreType.DMA],
    )(g)


def _pair_sum(g, got, tm):
    S, R, W = g.shape
    half = R // 2
    nb = half // tm

    def body(c_ref, g_ref, got_ref, o_ref, ob_ref):
        tot = g_ref[...] + got_ref[...]
        o_ref[...] = tot
        ob_ref[...] = tot.astype(ob_ref.dtype)

    tile = pl.BlockSpec((None, tm, W), lambda s, i, c_ref: (s, i, 0))
    return pl.pallas_call(
        body, name="pair_sum",
        grid_spec=pltpu.PrefetchScalarGridSpec(
            num_scalar_prefetch=1, grid=(S, nb),
            in_specs=[pl.BlockSpec((None, tm, W), lambda s, i, c_ref: (s, c_ref[0] * nb + i, 0)), tile],
            out_specs=[tile, tile]),
        out_shape=[jax.ShapeDtypeStruct((S, half, W), F32), jax.ShapeDtypeStruct((S, half, W), BF)],
        compiler_params=_cparams(("parallel", "parallel")),
    )(lax.axis_index("c").reshape(1).astype(jnp.int32), g, got)


def _scatter_chips(part):
    S, h, W = part.shape

    def body(p_ref, out_ref, send_sems, recv_sems):
        x, y, c = _me()
        chips = _other_chips(x, y)
        sends = [pltpu.make_async_remote_copy(
            src_ref=p_ref.at[2 * chip[0] + chip[1]], dst_ref=out_ref.at[j],
            send_sem=send_sems.at[j], recv_sem=recv_sems.at[j], device_id=(*chip, c), device_id_type=MESH)
            for j, chip in enumerate(chips)]
        for cp in sends:
            cp.start()
        for cp in sends:
            cp.wait()

    return pl.pallas_call(
        body, name="scatter_chips", in_specs=[ANY], out_specs=ANY,
        out_shape=jax.ShapeDtypeStruct((S - 1, h, W), part.dtype),
        scratch_shapes=[pltpu.SemaphoreType.DMA((3,)), pltpu.SemaphoreType.DMA((3,))],
    )(part)


def _chip_sum(own, others, tm):
    n, h, W = others.shape

    def body(own_ref, p_ref, o_ref):
        o_ref[...] = ((own_ref[...] + p_ref[0].astype(F32)) + p_ref[1].astype(F32)) + p_ref[2].astype(F32)

    return pl.pallas_call(
        body, name="chip_sum", grid=(h // tm,),
        in_specs=[pl.BlockSpec((tm, W), lambda i: (i, 0)), pl.BlockSpec((n, tm, W), lambda i: (0, i, 0))],
        out_specs=pl.BlockSpec((tm, W), lambda i: (i, 0)),
        out_shape=jax.ShapeDtypeStruct((h, W), F32),
        compiler_params=_cparams(("parallel",)),
    )(own, others)


def _join_halves(mine):
    h, W = mine.shape

    def body(m_ref, out_ref, send_sem, recv_sem):
        x, y, c = _me()
        cp = pltpu.make_async_remote_copy(
            src_ref=m_ref, dst_ref=out_ref.at[pl.ds(c * h, h), :],
            send_sem=send_sem, recv_sem=recv_sem, device_id=(x, y, 1 - c), device_id_type=MESH)
        cp.start()
        pltpu.make_async_remote_copy(
            src_ref=m_ref, dst_ref=out_ref.at[pl.ds((1 - c) * h, h), :],
            send_sem=send_sem, recv_sem=recv_sem, device_id=(x, y, 1 - c), device_id_type=MESH).wait_recv()
        cp.wait_send()

    other = pl.pallas_call(
        body, name="join_halves", in_specs=[ANY], out_specs=ANY,
        out_shape=jax.ShapeDtypeStruct((2 * h, W), mine.dtype),
        scratch_shapes=[pltpu.SemaphoreType.DMA, pltpu.SemaphoreType.DMA],
    )(mine)
    return lax.dynamic_update_slice(other, mine, (lax.axis_index("c") * h, 0))


def _allreduce_small(s, after=()):
    R, W = s.shape
    n_after = len(after)

    def body(s_ref, *rest):
        o_ref, buf, send_sems, recv_sems = rest[n_after:]
        x, y, c = _me()
        me = 4 * x + 2 * y + c
        buf[me] = s_ref[...]
        peers = [((x + fx) % 2, (y + fy) % 2, (c + fc) % 2) for fx in range(2) for fy in range(2) for fc in range(2)][1:]
        sends = [pltpu.make_async_remote_copy(
            src_ref=s_ref, dst_ref=buf.at[me], send_sem=send_sems.at[k], recv_sem=recv_sems.at[k],
            device_id=peer, device_id_type=MESH) for k, peer in enumerate(peers)]
        for cp in sends:
            cp.start()
        for k, peer in enumerate(peers):
            pltpu.make_async_remote_copy(
                src_ref=s_ref, dst_ref=buf.at[4 * peer[0] + 2 * peer[1] + peer[2]], send_sem=send_sems.at[k],
                recv_sem=recv_sems.at[k], device_id=peer, device_id_type=MESH).wait_recv()
        for cp in sends:
            cp.wait_send()
        total = buf[0]
        for d in range(1, N_DEV):
            total = total + buf[d]
        o_ref[...] = total

    return pl.pallas_call(
        body, name="allreduce_small",
        in_specs=[pl.BlockSpec(memory_space=pltpu.VMEM)] + [ANY] * n_after, out_specs=pl.BlockSpec(memory_space=pltpu.VMEM),
        out_shape=jax.ShapeDtypeStruct((R, W), F32),
        scratch_shapes=[pltpu.VMEM((N_DEV, R, W), F32), pltpu.SemaphoreType.DMA((N_DEV - 1,)), pltpu.SemaphoreType.DMA((N_DEV - 1,))],
    )(s, *after)


HBM_SPEC = pl.BlockSpec(memory_space=pltpu.HBM)
SEM_SPEC = pl.BlockSpec(memory_space=pltpu.SEMAPHORE)
DATAFLOW = pltpu.SideEffectType.DATAFLOW_SIDE_EFFECTING


class _InFlight(NamedTuple):
    sems: tuple
    src: jax.Array
    land: jax.Array
    token: jax.Array


def _split_start(name, src, land_shape, land_dtype, n, copies, after=()):
    n_after = len(after)

    def body(src_ref, land_ref, *rest):
        rest = rest[n_after:]
        sems, token = rest[:2 * n], rest[-1]
        for k, (s, d, peer) in enumerate(copies(src_ref, land_ref)):
            pltpu.make_async_remote_copy(src_ref=s, dst_ref=d, send_sem=sems[k], recv_sem=sems[n + k],
                                         device_id=peer, device_id_type=MESH).start()
        token[...] = jnp.zeros_like(token)

    outs = pl.pallas_call(
        body, name=name,
        out_shape=(*[pltpu.SemaphoreType.DMA(())] * (2 * n), pltpu.HBM(src.shape, src.dtype), pltpu.HBM(land_shape, land_dtype),
                   jax.ShapeDtypeStruct((8, 128), F32)),
        in_specs=(HBM_SPEC, HBM_SPEC, *[ANY] * n_after),
        out_specs=(*[SEM_SPEC] * (2 * n), HBM_SPEC, HBM_SPEC, pl.BlockSpec(memory_space=pltpu.VMEM)),
        input_output_aliases={0: 2 * n, 1: 2 * n + 1},
        compiler_params=pltpu.CompilerParams(has_side_effects=DATAFLOW),
    )(pltpu.with_memory_space_constraint(src, pltpu.HBM), pltpu.with_memory_space_constraint(lax.empty(land_shape, land_dtype), pltpu.HBM),
      *after)
    return _InFlight(tuple(outs[:2 * n]), outs[2 * n], outs[2 * n + 1], outs[2 * n + 2])


def _split_wait(name, flight, after, n, copies):
    def body(src_ref, land_ref, *rest):
        sems = rest[:2 * n]
        for k, (s, d, peer) in enumerate(copies(src_ref, land_ref)):
            cp = pltpu.make_async_remote_copy(src_ref=s, dst_ref=d, send_sem=sems[k], recv_sem=sems[n + k],
                                              device_id=peer, device_id_type=MESH)
            cp.wait_send()
            cp.wait_recv()

    return pl.pallas_call(
        body, name=name,
        out_shape=(pltpu.HBM(flight.src.shape, flight.src.dtype), pltpu.HBM(flight.land.shape, flight.land.dtype)),
        in_specs=(HBM_SPEC, HBM_SPEC, *[SEM_SPEC] * (2 * n), ANY),
        out_specs=(HBM_SPEC, HBM_SPEC), input_output_aliases={0: 0, 1: 1},
        compiler_params=pltpu.CompilerParams(has_side_effects=DATAFLOW),
    )(flight.src, flight.land, *flight.sems, after)


def _gather_copies(src_ref, land_ref):
    x, y, c = _me()
    return [(src_ref, land_ref.at[2 * x + y], (*chip, c)) for chip in _other_chips(x, y)]


def _gather_start(packed, tag, after=()):
    return _split_start(f"gather_start_{tag}", packed, (N_CHIPS, *packed.shape), packed.dtype, 3, _gather_copies, after)


def _gather_wait(flight, after, tag):
    src, others = _split_wait(f"gather_wait_{tag}", flight, after, 3, _gather_copies)
    return lax.dynamic_update_slice(others, src[None], (2 * lax.axis_index("x") + lax.axis_index("y"), 0, 0))


def _swap_copies(src_ref, land_ref):
    x, y, c = _me()
    half = land_ref.shape[1]
    return [(src_ref.at[:, pl.ds((1 - c) * half, half), :], land_ref, (x, y, 1 - c))]


def _swap_start(g, tag):
    S, R, W = g.shape
    return _split_start(f"swap_halves_start_{tag}", g, (S, R // 2, W), g.dtype, 1, _swap_copies)


def _swap_wait(flight, after, tag):
    return _split_wait(f"swap_halves_wait_{tag}", flight, after, 1, _swap_copies)


def _scatter_copies(src_ref, land_ref):
    x, y, c = _me()
    return [(src_ref.at[2 * chip[0] + chip[1]], land_ref.at[j], (*chip, c)) for j, chip in enumerate(_other_chips(x, y))]


def _scatter_start(part, tag):
    S, h, W = part.shape
    return _split_start(f"scatter_chips_start_{tag}", part, (S - 1, h, W), part.dtype, 3, _scatter_copies)


def _scatter_wait(flight, after, tag):
    return _split_wait(f"scatter_chips_wait_{tag}", flight, after, 3, _scatter_copies)[1]


def _join_copies(src_ref, land_ref):
    x, y, c = _me()
    h = src_ref.shape[0]
    return [(src_ref, land_ref.at[pl.ds(c * h, h), :], (x, y, 1 - c))]


def _join_start(mine):
    h, W = mine.shape
    return _split_start("join_halves_start", mine, (2 * h, W), mine.dtype, 1, _join_copies)


def _join_wait(flight, after):
    src, other = _split_wait("join_halves_wait", flight, after, 1, _join_copies)
    return lax.dynamic_update_slice(other, src, (lax.axis_index("c") * src.shape[0], 0))


def _adamw(name, g, w, m, v):
    R, C = w.shape
    tm = R
    for cand in (256, 128, 64, 32, 16, 8):
        if R % cand == 0:
            tm = cand
            break

    def body(g, w, m, v):
        m = ADAM_B1 * m + (1.0 - ADAM_B1) * g
        v = ADAM_B2 * v + (1.0 - ADAM_B2) * jnp.square(g)
        m_hat = m / (1.0 - ADAM_B1 ** ADAM_STEP)
        v_hat = v / (1.0 - ADAM_B2 ** ADAM_STEP)
        delta = -ADAM_LR * (m_hat / (jnp.sqrt(v_hat) + ADAM_EPS) + ADAM_WD * w)
        return delta, m, v

    return _rowwise(name, body, R, tm, [_row(t, tm) for t in (g, w, m, v)], [(C, F32)] * 3)


def _unpack_weights(gathered, names):
    S = gathered.shape[0]
    shard_shapes = {"w_in": (D_MODEL, (QKV_WIDTH + 2 * D_MODEL) // S), "w_branch_na": (NA_WIDTH, D_MODEL // S),
                    "w_branch_dil": (DIL_OUT_WIDTH, D_MODEL // S), "w_out": (D_MODEL // S, D_MODEL),
                    "w_up": (D_MODEL, D_FF // S), "w_down": (D_FF // S, D_MODEL),
                    "w_ple_gate": (D_MODEL // S, D_MODEL), "w_ple_proj": (PLE_DIM, D_MODEL // S)}
    col_sharded = {"w_in", "w_branch_na", "w_branch_dil", "w_up", "w_ple_proj"}
    out, r0 = {}, 0
    for name in names:
        rows, cols = shard_shapes[name]
        n = rows * cols // PACK_W
        t = gathered[:, r0:r0 + n, :].reshape(S, rows, cols)
        r0 += n
        out[name] = t.transpose(1, 0, 2).reshape(rows, S * cols) if name in col_sharded else t.reshape(S * rows, cols)
    return out


def _pack_grads(grads, names):
    col_sharded = {"w_in", "w_branch_na", "w_branch_dil", "w_up", "w_ple_proj"}
    per_chip = []
    for s in range(N_CHIPS):
        rows = []
        for name in names:
            g = grads[name]
            if name in col_sharded:
                w = g.shape[1] // N_CHIPS
                rows.append(_pack_rows(g[:, s * w:(s + 1) * w]))
            else:
                h = g.shape[0] // N_CHIPS
                rows.append(_pack_rows(g[s * h:(s + 1) * h]))
        per_chip.append(jnp.concatenate(rows, axis=0))
    return jnp.stack(per_chip)


def _unpack_shard(packed, shapes, names):
    out, r0 = {}, 0
    for name in names:
        rows, cols = shapes[name]
        n = rows * cols // PACK_W
        out[name] = packed[r0:r0 + n].reshape(rows, cols)
        r0 += n
    return out


def kernel(x, p, positions, g_mix, w_in, rpb, w_branch_na, w_branch_dil, w_out, g_mlp, w_up, w_down, g_ple, w_ple_gate, w_ple_proj, g_final, loss_target, m_g_mix, m_w_in, m_rpb, m_w_branch_na, m_w_branch_dil, m_w_out, m_g_mlp, m_w_up, m_w_down, m_g_ple, m_w_ple_gate, m_w_ple_proj, m_g_final, v_g_mix, v_w_in, v_rpb, v_w_branch_na, v_w_branch_dil, v_w_out, v_g_mlp, v_w_up, v_w_down, v_g_ple, v_w_ple_gate, v_w_ple_proj, v_g_final):
    shards = {"w_in": w_in[0], "w_branch_na": w_branch_na[0], "w_branch_dil": w_branch_dil[0], "w_out": w_out[0],
              "w_up": w_up[0], "w_down": w_down[0], "w_ple_gate": w_ple_gate[0], "w_ple_proj": w_ple_proj[0]}
    m_shards = {"w_in": m_w_in[0], "w_branch_na": m_w_branch_na[0], "w_branch_dil": m_w_branch_dil[0], "w_out": m_w_out[0],
                "w_up": m_w_up[0], "w_down": m_w_down[0], "w_ple_gate": m_w_ple_gate[0], "w_ple_proj": m_w_ple_proj[0]}
    v_shards = {"w_in": v_w_in[0], "w_branch_na": v_w_branch_na[0], "w_branch_dil": v_w_branch_dil[0], "w_out": v_w_out[0],
                "w_up": v_w_up[0], "w_down": v_w_down[0], "w_ple_gate": v_w_ple_gate[0], "w_ple_proj": v_w_ple_proj[0]}

    W = {"w_in": _gather_weights(shards["w_in"].astype(BF)).transpose(1, 0, 2).reshape(D_MODEL, -1)}
    mix_flight = _gather_start(jnp.concatenate([_pack_rows(shards[n].astype(BF)) for n in GATHER_MIX], axis=0), "mix")
    rest_flight = _gather_start(jnp.concatenate([_pack_rows(shards[n].astype(BF)) for n in GATHER_MLP], axis=0), "mlp",
                                after=(mix_flight.token,))
    w_qkv, w_gates = W["w_in"][:, :QKV_WIDTH], W["w_in"][:, QKV_WIDTH:]

    xs, ps, tgt = x[0], p[0, 0], loss_target[0]
    T = xs.shape[0]
    TM = 512
    gm, gl, gp, gf = g_mix, g_mlp, g_ple, g_final.reshape(1, D_MODEL)
    cos2, sin_signed = _rope_tables(positions[0])
    tab = _na_bias_table(rpb[0])

    a = _rowwise("norm_mix", lambda h, g: h * _rms(h) * g, T, TM, [_row(xs, TM), _full(gm)], [(D_MODEL, BF)],
                 after=(rest_flight.token,))
    n3 = 3 * NA_WIDTH
    qkv = _mm("in_na", a, w_qkv[:, :n3], "nn", 1024, 768, 1024, [BF])
    z_dil = _mm("in_dil", a, w_qkv[:, n3:], "nn", 1024, 1152, 1024, [F32])
    z_gates = _mm("in_gates", a, w_gates, "nn", 1024,1024, 1024, [BF])

    dil_ops = _qkv_prep(z_dil, cos2, sin_signed, TM)
    y_na = _na_fwd(qkv, tab)
    band = [_band_fwd(*dil_ops[g], g) for g in range(len(DIL_GROUPS))]
    y_dil, w_grp, o_nat = _dil_merge_fwd([b[0] for b in band], [b[1] for b in band], T, TM)

    W.update(_unpack_weights(_gather_wait(mix_flight, y_dil, "mix"), GATHER_MIX))
    u_na = _mm("branch_na", y_na, W["w_branch_na"], "nn", 1024,1024, 512, [BF])
    u_dil = _mm("branch_dil", y_dil, W["w_branch_dil"], "nn", 1024,1024, 256, [BF])
    mixed = _rowwise(
        "gate_mix", lambda gn, gd, un, ud: _sigmoid(gn.astype(F32)) * un.astype(F32) + _sigmoid(gd.astype(F32)) * ud.astype(F32), T, TM,
        [_row(z_gates, TM, 0, D_MODEL), _row(z_gates, TM, 1, D_MODEL), _row(u_na, TM), _row(u_dil, TM)], [(D_MODEL, BF)])
    def add_norm(d, h, g):
        h = h + d
        return h, h * _rms(h) * g

    h1, cn = _mm("out_proj", mixed, W["w_out"], "nn", 512, 1024, 1024, [F32, BF], epilogue=add_norm, extras=(xs,), consts=(gl,))
    W.update(_unpack_weights(_gather_wait(rest_flight, cn, "mlp"), GATHER_MLP))
    up, act = _mm("mlp_up", cn, W["w_up"], "nn", 1024,1024, 1024, [BF, BF],
                  epilogue=lambda acc: (acc, jnp.square(jnp.maximum(acc, 0.0))))
    h2, en = _mm("mlp_down", act, W["w_down"], "nn", 1024, 1024, 1024, [F32, BF], epilogue=add_norm, extras=(h1,), consts=(gp,))
    gt = _mm("ple_gate", en, W["w_ple_gate"], "nn", 1024,1024, 1024, [F32])
    pp = _mm("ple_proj", ps, W["w_ple_proj"], "nn", 1024,1024, 256, [F32])

    def head(h2t, gtt, ppt, tg, g):
        sg = _sigmoid(gtt)
        h3 = h2t + sg * ppt
        yo = h3 * _rms(h3) * g
        diff = yo - tg
        loss = 0.5 * jnp.sum(jnp.mean(jnp.square(diff), axis=-1, keepdims=True), axis=0, keepdims=True)
        dh3, dg = _rms_bwd(diff * (1.0 / D_MODEL), h3, g)
        return dh3, dh3 * ppt * sg * (1.0 - sg), dh3 * sg, jnp.broadcast_to(loss, (1, 128)), dg

    dh3, d_gt, d_pp, loss_part, dg_final = _rowwise(
        "loss_head", head, T, TM, [_row(h2, TM), _row(gt, TM), _row(pp, TM), _row(tgt, TM), _full(gf)],
        [(D_MODEL, F32), (D_MODEL, BF), (D_MODEL, BF)], sums=[128, D_MODEL])

    early_shapes = {n: shards[n].shape for n in REDUCE_EARLY}
    early_rows = sum(r * c for r, c in early_shapes.values()) // PACK_W
    shard_rows = D_MODEL // N_CHIPS
    early_buf = _mm("g_ple_gate", en, d_gt, "tn", shard_rows, 1024, 1024, [F32],
                    into=(jax.ShapeDtypeStruct((N_CHIPS, early_rows, PACK_W), F32), (None, shard_rows, PACK_W),
                          lambda i, j: (i, 2 * D_MODEL // shard_rows, 0)))
    g_ple_proj = _mm("g_ple_proj", ps, d_pp, "tn", 256, 1024, 1024,[F32])

    def add_norm_bwd(dn, dh_out, h, g):
        dh, dg = _rms_bwd(dn, h, g)
        dh = dh_out + dh
        return dh, dh, dg

    dh2, dh2_b, dg_ple = _mm("d_ple_gate", d_gt, W["w_ple_gate"], "nt", 512, 1024, 1024, [F32, BF],
                             epilogue=add_norm_bwd, extras=(dh3, h2), consts=(gp,), sums=[D_MODEL])
    d_up = _mm("d_mlp_down", dh2_b, W["w_down"], "nt", 1024,1024, 1024, [BF],
               epilogue=lambda acc, u: (acc * (2.0 * jnp.maximum(u.astype(F32), 0.0)),), extras=(up,))
    early_buf = _mm("g_mlp_down", act, dh2_b, "tn", 1024, 1024, 1024,[F32],
                    into=(early_buf, (None, D_MODEL, PACK_W), lambda i, j: (i, 1, 0)))
    early_buf = _mm("g_mlp_up", cn, d_up, "tn", 1024, 1024, 1024,[F32],
                    into=(early_buf, (None, D_MODEL, PACK_W), lambda i, j: (j, 0, 0)))
    dh1, dh1_b, dg_mlp = _mm("d_mlp_up", d_up, W["w_up"], "nt", 1024, 1024, 1024, [F32, BF], epilogue=add_norm_bwd,
                             extras=(dh2, h1), consts=(gl,), sums=[D_MODEL])
    d_mixed = _mm("d_out_proj", dh1_b, W["w_out"], "nt", 1024,1024, 1024, [F32])
    early_buf = _mm("g_out_proj", mixed, dh1_b, "tn", shard_rows, 1024, 1024, [F32],
                    into=(early_buf, (None, shard_rows, PACK_W), lambda i, j: (i, 2 * D_MODEL // shard_rows + 1, 0)))

    def gate_bwd(dm, gn, gd, un, ud):
        gn, gd, un, ud = (t.astype(F32) for t in (gn, gd, un, ud))
        sn, sd = _sigmoid(gn), _sigmoid(gd)
        return jnp.concatenate([dm * un * sn * (1.0 - sn), dm * ud * sd * (1.0 - sd)], axis=1), dm * sn, dm * sd

    dz_gates, d_u_na, d_u_dil = _rowwise(
        "gate_mix_bwd", gate_bwd, T, TM,
        [_row(d_mixed, TM), _row(z_gates, TM, 0, D_MODEL), _row(z_gates, TM, 1, D_MODEL), _row(u_na, TM), _row(u_dil, TM)],
        [(2 * D_MODEL, BF), (D_MODEL, BF), (D_MODEL, BF)])
    g_branch_na = _mm("g_branch_na", y_na, d_u_na, "tn", 1024, 1024, 1024,[F32])
    g_branch_dil = _mm("g_branch_dil", y_dil, d_u_dil, "tn", 256, 1024, 1024,[F32])
    small_rows = [jnp.concatenate([_pack_rows(g[:, s * shard_rows:(s + 1) * shard_rows]) for g in (g_ple_proj, g_branch_na, g_branch_dil)],
                                  axis=0) for s in range(N_CHIPS)]
    early_buf = lax.dynamic_update_slice(early_buf, jnp.stack(small_rows), (0, 2 * D_MODEL + 2 * shard_rows, 0))
    early_tm = early_rows // 4
    swap_flight = _swap_start(early_buf, "early")
    d_y_na = _mm("d_branch_na", d_u_na, W["w_branch_na"], "nt", 1024,512, 1024, [BF], after=(swap_flight.token,))
    d_y_dil = _mm("d_branch_dil", d_u_dil, W["w_branch_dil"], "nt", 1024,256, 1024, [F32])

    dqa, dka, dva, dtab = _na_bwd(qkv, tab, d_y_na)
    early_g, early_got = _swap_wait(swap_flight, dqa, "early")
    early_pair, early_pair_b = _pair_sum(early_g, early_got, early_tm)
    scatter_flight = _scatter_start(early_pair_b, "early")
    d_rpb = _na_rpb_grad(dtab)[:, :2 * NA_WIN_ROWS - 1, :2 * NA_WIN_COLS - 1]

    do_res, dlse_res = _dil_merge_bwd(d_y_dil, o_nat, w_grp, TM, after=(scatter_flight.token,))
    d_dil = [_band_bwd(*dil_ops[g], do_res[g], dlse_res[g], g) for g in range(len(DIL_GROUPS))]

    dz_qkv = _qkv_unprep((dqa, dka, dva), d_dil, cos2, sin_signed, TM)
    g_in = jnp.concatenate([
        _mm("g_in_qkv", a, dz_qkv, "tn", 1024, 1280, 1024,[F32]),
        _mm("g_in_gates", a, dz_gates, "tn", 1024, 1024, 1024,[F32])], axis=1)
    me_chip = 2 * lax.axis_index("x") + lax.axis_index("y")
    early_mine = _chip_sum(lax.dynamic_index_in_dim(early_pair, me_chip, 0, keepdims=False),
                           _scatter_wait(scatter_flight, g_in, "early"), early_tm)
    join_flight = _join_start(early_mine)
    in_cols = g_in.shape[1] // N_CHIPS
    late_tm = 256
    late_swap = _swap_start(jnp.stack([g_in[:, s * in_cols:(s + 1) * in_cols] for s in range(N_CHIPS)]), "late")
    d_a = _mm("d_in_qkv", dz_qkv, w_qkv, "nt", 1024,1024, 1280, [F32], after=(late_swap.token, join_flight.token))
    late_g, late_got = _swap_wait(late_swap, d_a, "late")
    late_pair, late_pair_b = _pair_sum(late_g, late_got, late_tm)
    late_scatter = _scatter_start(late_pair_b, "late")
    def first_bwd(dn_gates, dn_qkv, dh_out, h, g):
        dh, dg = _rms_bwd(dn_gates + dn_qkv, h, g)
        return dh_out + dh, dg

    grad_x, dg_mix = _mm("d_in_gates", dz_gates, w_gates, "nt", 512, 1024, 1024, [F32], epilogue=first_bwd,
                         extras=(d_a, dh1, xs), consts=(gm,), sums=[D_MODEL], after=(late_scatter.token,))
    g_shard = _unpack_shard(_join_wait(join_flight, grad_x), early_shapes, REDUCE_EARLY)

    n_rpb = rpb.size
    rpb_rows = 4
    small = jnp.concatenate([
        dg_mix, dg_mlp, dg_ple, dg_final,
        jnp.pad(d_rpb.reshape(-1), (0, rpb_rows * D_MODEL - n_rpb)).reshape(rpb_rows, D_MODEL),
        jnp.pad(loss_part, ((0, 0), (0, D_MODEL - loss_part.shape[1]))),
        jnp.zeros((SMALL_ROWS - 5 - rpb_rows, D_MODEL), F32)], axis=0)
    out = {"grad": {}, "delta": {}, "new_m": {}, "new_v": {}}

    def update(names):
        for n in names:
            out["grad"][n] = g_shard[n][None]
            res = _adamw("adamw_" + n, g_shard[n], shards[n], m_shards[n], v_shards[n])
            for kind, t in zip(("delta", "new_m", "new_v"), res, strict=True):
                out[kind][n] = t[None]

    update(REDUCE_EARLY)
    late_others = _scatter_wait(late_scatter, out["new_v"][REDUCE_EARLY[-1]], "late")
    late_mine = _chip_sum(lax.dynamic_index_in_dim(late_pair, me_chip, 0, keepdims=False), late_others, late_tm)
    small = _allreduce_small(small, after=(late_mine,))
    g_shard["w_in"] = _join_halves(late_mine)
    update(("w_in",))
    loss = small[4 + rpb_rows, 0]

    def small_pack(a0, a1, a2, a3, r):
        return jnp.concatenate([a0.reshape(1, -1), a1.reshape(1, -1), a2.reshape(1, -1), a3.reshape(1, -1),
                                jnp.pad(r.reshape(-1), (0, rpb_rows * D_MODEL - n_rpb)).reshape(rpb_rows, D_MODEL)], axis=0)

    g_small = small[:4 + rpb_rows]
    small_res = _adamw("adamw_small", g_small, small_pack(g_mix, g_mlp, g_ple, g_final, rpb),
                       small_pack(m_g_mix, m_g_mlp, m_g_ple, m_g_final, m_rpb), small_pack(v_g_mix, v_g_mlp, v_g_ple, v_g_final, v_rpb))

    def small_unpack(t):
        return {"g_mix": t[0].reshape(g_mix.shape), "g_mlp": t[1].reshape(g_mlp.shape), "g_ple": t[2].reshape(g_ple.shape),
                "g_final": t[3].reshape(g_final.shape), "rpb": t[4:].reshape(-1)[:n_rpb].reshape(rpb.shape)}

    for kind, t in zip(("grad", "delta", "new_m", "new_v"), (g_small, *small_res), strict=True):
        out[kind].update(small_unpack(t))

    order = ["g_mix", "w_in", "rpb", "w_branch_na", "w_branch_dil", "w_out", "g_mlp", "w_up", "w_down", "g_ple",
             "w_ple_gate", "w_ple_proj", "g_final"]
    return (loss, grad_x[None], *[out["grad"][n] for n in order], *[out["delta"][n] for n in order],
            *[out["new_m"][n] for n in order], *[out["new_v"][n] for n in order])
```

```python
import functools
from typing import NamedTuple

import jax
import jax.numpy as jnp
from jax import lax
from jax.experimental import pallas as pl
from jax.experimental.pallas import tpu as pltpu

BF = jnp.bfloat16
F32 = jnp.float32
MESH = pl.DeviceIdType.MESH
ANY = pl.BlockSpec(memory_space=pl.ANY)

V7X_VMEM_BYTES = 64 * 1024 * 1024
VMEM_LIMIT = V7X_VMEM_BYTES - 16 * 1024 * 1024

D_MODEL = 1024
HEAD_DIM = 64
GRID_W = 64
NA_HEADS = 8
NA_WIN_ROWS = 8
NA_WIN_COLS = 16
NA_WIDTH = NA_HEADS * HEAD_DIM
DIL_GROUPS = ((128, 1), (512, 4), (2048, 16))
DIL_HPG = 4
DIL_HEADS = DIL_HPG * len(DIL_GROUPS)
DIL_WIDTH = DIL_HEADS * HEAD_DIM
DIL_OUT_WIDTH = DIL_HPG * HEAD_DIM
DIL_RADIUS = 64
QKV_WIDTH = 3 * NA_WIDTH + 3 * DIL_WIDTH
D_FF = 4 * D_MODEL
PLE_DIM = 256
ROPE_THETA = 10000.0
RMS_EPS = 1e-6
NEG_INF = -1e30
Q_SCALE = HEAD_DIM ** -0.5

ADAM_LR = 0.001
ADAM_B1 = 0.9
ADAM_B2 = 0.999
ADAM_EPS = 1e-08
ADAM_WD = 0.01
ADAM_STEP = 10

N_CHIPS = 4
N_DEV = 8
PACK_W = 1024
BIG = ("w_in", "w_branch_na", "w_branch_dil", "w_out", "w_up", "w_down", "w_ple_gate", "w_ple_proj")
GATHER_MIX = ("w_branch_na", "w_branch_dil", "w_out")
GATHER_MLP = ("w_up", "w_down", "w_ple_gate", "w_ple_proj")
REDUCE_EARLY = ("w_up", "w_down", "w_ple_gate", "w_out", "w_ple_proj", "w_branch_na", "w_branch_dil")
SMALL_ROWS = 16


def _cparams(sem=None):
    return pltpu.CompilerParams(dimension_semantics=sem, vmem_limit_bytes=VMEM_LIMIT)


def _mm(name, a, b, mode, tm, tn, tk, out_dtypes, epilogue=None, extras=(), consts=(), sums=(), after=(), into=None):
    if mode == "nn":
        (M, K), N = a.shape, b.shape[1]
    elif mode == "nt":
        (M, K), N = a.shape, b.shape[0]
    else:
        (K, M), N = a.shape, b.shape[1]
    tm, tn, tk = min(tm, M), min(tn, N), min(tk, K)
    assert M % tm == 0 and N % tn == 0 and K % tk == 0, (name, M, N, K, tm, tn, tk)
    if mode == "nn":
        a_spec = pl.BlockSpec((tm, tk), lambda i, j, k: (i, k))
        b_spec = pl.BlockSpec((tk, tn), lambda i, j, k: (k, j))
        dims = (((1,), (0,)), ((), ()))
    elif mode == "nt":
        a_spec = pl.BlockSpec((tm, tk), lambda i, j, k: (i, k))
        b_spec = pl.BlockSpec((tn, tk), lambda i, j, k: (j, k))
        dims = (((1,), (1,)), ((), ()))
    else:
        a_spec = pl.BlockSpec((tk, tm), lambda i, j, k: (k, i))
        b_spec = pl.BlockSpec((tk, tn), lambda i, j, k: (k, j))
        dims = (((0,), (0,)), ((), ()))
    nk = K // tk
    n_extra, n_const, n_out, n_sum = len(extras), len(consts), len(out_dtypes), len(sums)
    tile = pl.BlockSpec((tm, tn), lambda i, j, k: (i, j))
    assert not sums or tn == N, "row sums need whole rows in a tile"

    n_after = len(after)

    def body(a_ref, b_ref, *rest):
        extra_refs, rest = rest[:n_extra + n_const], rest[n_extra + n_const + n_after:]
        out_refs, sum_refs, acc = rest[:n_out], rest[n_out:n_out + n_sum], rest[-1]
        i, k = pl.program_id(0), pl.program_id(2)

        @pl.when(k == 0)
        def _():
            acc[...] = jnp.zeros_like(acc)

        acc[...] += lax.dot_general(a_ref[...].astype(BF), b_ref[...].astype(BF), dims, preferred_element_type=F32)

        @pl.when(k == nk - 1)
        def _():
            outs = (acc[...],) if epilogue is None else epilogue(acc[...], *[e[...] for e in extra_refs])
            for o_ref, val in zip(out_refs, outs[:n_out], strict=True):
                o_ref[...] = val.astype(o_ref.dtype).reshape(o_ref.shape)
            for s_ref, val in zip(sum_refs, outs[n_out:], strict=True):
                @pl.when(i == 0)
                def _():
                    s_ref[...] = val

                @pl.when(i != 0)
                def _():
                    s_ref[...] += val

    out_specs = [tile] * n_out + [pl.BlockSpec((1, c), lambda i, j, k: (0, 0)) for c in sums]
    out_shape = [jax.ShapeDtypeStruct((M, N), dt) for dt in out_dtypes] + [jax.ShapeDtypeStruct((1, c), F32) for c in sums]
    operands, aliases = [a, b, *extras, *consts, *after], {}
    in_specs = ([a_spec, b_spec] + [tile] * n_extra
                + [pl.BlockSpec(c.shape, functools.partial(lambda nd, i, j, k: (0,) * nd, c.ndim)) for c in consts] + [ANY] * n_after)
    if into is not None:
        assert n_out == 1
        target, block, index = into
        out_specs = [pl.BlockSpec(block, lambda i, j, k: index(i, j))]
        out_shape = [jax.ShapeDtypeStruct(target.shape, target.dtype)]
        if not isinstance(target, jax.ShapeDtypeStruct):
            aliases = {len(operands): 0}
            operands.append(target)
            in_specs.append(ANY)
            n_after += 1

    outs = pl.pallas_call(
        body, name=name, grid=(M // tm, N // tn, nk),
        in_specs=in_specs, out_specs=out_specs, out_shape=out_shape,
        scratch_shapes=[pltpu.VMEM((tm, tn), F32)], input_output_aliases=aliases,
        compiler_params=_cparams(("arbitrary",) * 3 if sums else ("parallel", "parallel", "arbitrary")),
    )(*operands)
    return outs[0] if len(outs) == 1 else outs


def _row(arr, tm, col_block=None, width=None):
    width = arr.shape[1] if width is None else width
    cb = 0 if col_block is None else col_block
    return arr, pl.BlockSpec((tm, width), lambda i: (i, cb))


def _full(arr):
    nd = arr.ndim
    return arr, pl.BlockSpec(arr.shape, lambda i: (0,) * nd)


def _rowwise(name, body, T, tm, ins, outs, sums=(), after=()):
    n_in, n_out, n_sum, n_after = len(ins), len(outs), len(sums), len(after)

    def kern(*refs):
        in_refs, refs = refs[:n_in], refs[n_in + n_after:]
        out_refs, sum_refs = refs[:n_out], refs[n_out:]
        res = body(*[r[...] for r in in_refs])
        res = res if isinstance(res, tuple) else (res,)
        for o_ref, val in zip(out_refs, res[:n_out], strict=True):
            o_ref[...] = val.astype(o_ref.dtype)
        if n_sum:
            @pl.when(pl.program_id(0) == 0)
            def _():
                for s_ref in sum_refs:
                    s_ref[...] = jnp.zeros_like(s_ref)

            for s_ref, val in zip(sum_refs, res[n_out:], strict=True):
                s_ref[...] += val

    res = pl.pallas_call(
        kern, name=name, grid=(T // tm,),
        in_specs=[spec for _, spec in ins] + [ANY] * n_after,
        out_specs=[pl.BlockSpec((tm, c), lambda i: (i, 0)) for c, _ in outs]
        + [pl.BlockSpec((1, c), lambda i: (0, 0)) for c in sums],
        out_shape=[jax.ShapeDtypeStruct((T, c), dt) for c, dt in outs]
        + [jax.ShapeDtypeStruct((1, c), F32) for c in sums],
        compiler_params=_cparams(("arbitrary",)),
    )(*[a for a, _ in ins], *after)
    return res[0] if len(res) == 1 else res


def _sigmoid(x):
    return 1.0 / (1.0 + jnp.exp(-x))


def _rms(h):
    return lax.rsqrt(jnp.mean(h * h, axis=-1, keepdims=True) + RMS_EPS)


def _rms_bwd(dy, h, g):
    r = _rms(h)
    n = h * r
    dn = dy * g
    dh = r * (dn - n * jnp.mean(dn * n, axis=-1, keepdims=True))
    return dh, jnp.sum(dy * n, axis=0, keepdims=True)


def _rope(x, cos2, sin_signed):
    lane = lax.broadcasted_iota(jnp.int32, x.shape, 1)
    swapped = jnp.where((lane % HEAD_DIM) < HEAD_DIM // 2, pltpu.roll(x, 128 - HEAD_DIM // 2, 1), pltpu.roll(x, HEAD_DIM // 2, 1))
    return x * cos2 + swapped * sin_signed


NA_KEYS = NA_WIN_ROWS * GRID_W
NA_BASES = 8


def _na_row_geometry(r, rows):
    first = jnp.clip(r - NA_WIN_ROWS // 2, 0, rows - NA_WIN_ROWS)
    base = first - r + (NA_WIN_ROWS - 1)
    return pl.multiple_of(first * GRID_W, GRID_W), base


NA_ROWS_PER_STEP = 8
NA_BWD_ROWS_PER_STEP = 8


def _softmax_rows(s):
    p = jnp.exp(s - jnp.max(s, axis=-1, keepdims=True))
    return p / jnp.sum(p, axis=-1, keepdims=True)


def _na_probs(q, kw, bias):
    return _softmax_rows(lax.dot_general(q, kw, (((1,), (1,)), ((), ())), preferred_element_type=F32) + bias)


def _split_pair(t):
    first = lax.broadcasted_iota(jnp.int32, t.shape, 1) < HEAD_DIM
    zero = jnp.zeros_like(t)
    return jnp.where(first, t, zero), jnp.where(first, zero, t)


def _join_pair(a, b):
    return jnp.where(lax.broadcasted_iota(jnp.int32, a.shape, 1) < HEAD_DIM, a, b)


_NT = (((1,), (1,)), ((), ()))
_TN = (((0,), (0,)), ((), ()))


def _na_fwd(qkv, tab):
    T = qkv.shape[0]
    rows = T // GRID_W
    n_pairs = NA_WIDTH // 128

    def body(q_ref, k_ref, v_ref, tab_ref, y_ref):
        def step(it, carry):
            geo = [_na_row_geometry(it * NA_ROWS_PER_STEP + u, rows) for u in range(NA_ROWS_PER_STEP)]
            q0s = [pl.multiple_of((it * NA_ROWS_PER_STEP + u) * GRID_W, GRID_W) for u in range(NA_ROWS_PER_STEP)]
            ss = [lax.dot_general(jnp.concatenate(_split_pair(q_ref[pl.ds(q0, GRID_W), :] * Q_SCALE), axis=0),
                                  k_ref[pl.ds(k0, NA_KEYS), :], _NT, preferred_element_type=F32)
                  for q0, (k0, _) in zip(q0s, geo)]
            ps = [_softmax_rows(s + jnp.concatenate([tab_ref[0, base], tab_ref[1, base]], axis=0)) for s, (_, base) in zip(ss, geo)]
            ys = [jnp.dot(p.astype(BF), v_ref[pl.ds(k0, NA_KEYS), :], preferred_element_type=F32) for p, (k0, _) in zip(ps, geo)]
            for q0, y2 in zip(q0s, ys):
                y_ref[pl.ds(q0, GRID_W), :] = _join_pair(y2[:GRID_W], y2[GRID_W:]).astype(y_ref.dtype)
            return carry

        lax.fori_loop(0, rows // NA_ROWS_PER_STEP, step, 0)

    def cols(first):
        return pl.BlockSpec((T, 128), lambda j: (0, first + j))

    return pl.pallas_call(
        body, name="na_fwd", grid=(n_pairs,),
        in_specs=[cols(0), cols(n_pairs), cols(2 * n_pairs), pl.BlockSpec((2, NA_BASES, GRID_W, NA_KEYS), lambda j: (j, 0, 0, 0))],
        out_specs=cols(0), out_shape=jax.ShapeDtypeStruct((T, NA_WIDTH), BF),
        compiler_params=_cparams(("parallel",)),
    )(qkv, qkv, qkv, tab)


def _na_bwd(qkv, tab, do):
    T = qkv.shape[0]
    rows = T // GRID_W
    n_pairs = NA_WIDTH // 128

    def body(q_ref, k_ref, v_ref, tab_ref, do_ref, dq_ref, dk_ref, dv_ref, dtab_ref):
        dk_ref[...] = jnp.zeros_like(dk_ref)
        dv_ref[...] = jnp.zeros_like(dv_ref)
        dtab_ref[...] = jnp.zeros_like(dtab_ref)

        def step(it, carry):
            U = NA_BWD_ROWS_PER_STEP
            geo = [_na_row_geometry(it * U + u, rows) for u in range(U)]
            q0s = [pl.multiple_of((it * U + u) * GRID_W, GRID_W) for u in range(U)]
            q2s = [jnp.concatenate(_split_pair(q_ref[pl.ds(q0, GRID_W), :] * Q_SCALE), axis=0) for q0 in q0s]
            do2s = [jnp.concatenate(_split_pair(do_ref[pl.ds(q0, GRID_W), :]), axis=0) for q0 in q0s]
            ss = [lax.dot_general(q2, k_ref[pl.ds(k0, NA_KEYS), :], _NT, preferred_element_type=F32) for q2, (k0, _) in zip(q2s, geo)]
            dps = [lax.dot_general(do2, v_ref[pl.ds(k0, NA_KEYS), :], _NT, preferred_element_type=F32) for do2, (k0, _) in zip(do2s, geo)]
            ps = [_softmax_rows(s + jnp.concatenate([tab_ref[0, base], tab_ref[1, base]], axis=0)) for s, (_, base) in zip(ss, geo)]
            dss = [p * (dp - jnp.sum(dp * p, axis=-1, keepdims=True)) for p, dp in zip(ps, dps)]
            dvs = [lax.dot_general(p.astype(BF), do2, _TN, preferred_element_type=F32) for p, do2 in zip(ps, do2s)]
            dsbs = [ds.astype(BF) for ds in dss]
            dqs = [jnp.dot(dsb, k_ref[pl.ds(k0, NA_KEYS), :], preferred_element_type=F32) for dsb, (k0, _) in zip(dsbs, geo)]
            dks = [lax.dot_general(dsb, q2, _TN, preferred_element_type=F32) for dsb, q2 in zip(dsbs, q2s)]
            for u in range(U):
                k0, base = geo[u]
                dtab_ref[0, base] += dss[u][:GRID_W]
                dtab_ref[1, base] += dss[u][GRID_W:]
                dq_ref[pl.ds(q0s[u], GRID_W), :] = _join_pair(dqs[u][:GRID_W], dqs[u][GRID_W:])
                dk_ref[pl.ds(k0, NA_KEYS), :] += dks[u]
                dv_ref[pl.ds(k0, NA_KEYS), :] += dvs[u]
            return carry

        lax.fori_loop(0, rows // NA_BWD_ROWS_PER_STEP, step, 0)

    def cols(first):
        return pl.BlockSpec((T, 128), lambda j: (0, first + j))

    tabs = pl.BlockSpec((2, NA_BASES, GRID_W, NA_KEYS), lambda j: (j, 0, 0, 0))
    wide = jax.ShapeDtypeStruct((T, NA_WIDTH), F32)
    return pl.pallas_call(
        body, name="na_bwd", grid=(n_pairs,),
        in_specs=[cols(0), cols(n_pairs), cols(2 * n_pairs), tabs, cols(0)],
        out_specs=[cols(0), cols(0), cols(0), tabs],
        out_shape=[wide, wide, wide, jax.ShapeDtypeStruct((NA_HEADS, NA_BASES, GRID_W, NA_KEYS), F32)],
        compiler_params=_cparams(("parallel",)),
    )(qkv, qkv, qkv, tab, do)


def _na_bias_table(rpb):
    H, n_rows, n_cols = rpb.shape

    def body(r_ref, tab_ref):
        q = lax.broadcasted_iota(jnp.int32, (GRID_W, 128), 0)
        kc = lax.broadcasted_iota(jnp.int32, (GRID_W, 128), 1)
        first = jnp.clip(q - NA_WIN_COLS // 2, 0, GRID_W - NA_WIN_COLS)
        valid = (kc >= first) & (kc < first + NA_WIN_COLS)
        toeplitz = []
        for ro in range(n_rows):
            row = jnp.broadcast_to(r_ref[pl.ds(ro, 1), :], (GRID_W, 128))
            shifted = pltpu.roll(pltpu.roll(row, 128 - (NA_WIN_COLS - 1), 1), 0, 1, stride=1, stride_axis=0)
            toeplitz.append(jnp.where(valid, shifted, NEG_INF))
        for base in range(NA_BASES):
            for j in range(NA_WIN_ROWS // 2):
                even, odd = toeplitz[base + 2 * j], toeplitz[base + 2 * j + 1]
                tab_ref[base, :, pl.ds(j * 128, 128)] = jnp.where(kc < GRID_W, even, pltpu.roll(odd, GRID_W, 1))

    padded = jnp.pad(rpb, ((0, 0), (0, 16 - n_rows), (0, 128 - n_cols)))
    return pl.pallas_call(
        body, name="na_bias_table", grid=(H,),
        in_specs=[pl.BlockSpec((None, 16, 128), lambda h: (h, 0, 0))],
        out_specs=pl.BlockSpec((None, NA_BASES, GRID_W, NA_KEYS), lambda h: (h, 0, 0, 0)),
        out_shape=jax.ShapeDtypeStruct((H, NA_BASES, GRID_W, NA_KEYS), F32),
        compiler_params=_cparams(("parallel",)),
    )(padded)


def _na_rpb_grad(dtab):
    H = dtab.shape[0]
    n_rows = 2 * NA_WIN_ROWS - 1
    n_cols = 2 * NA_WIN_COLS - 1

    def body(d_ref, o_ref):
        lane = lax.broadcasted_iota(jnp.int32, (GRID_W, 128), 1)
        low = lane < GRID_W
        out_rows = []
        for ro in range(n_rows):
            acc = jnp.zeros((GRID_W, 128), F32)
            for base in range(NA_BASES):
                i = ro - base
                if not 0 <= i < NA_WIN_ROWS:
                    continue
                pair = d_ref[base, :, pl.ds((i // 2) * 128, 128)]
                if i % 2:
                    pair = pltpu.roll(pair, GRID_W, 1)
                acc = acc + jnp.where(low, pair, 0.0)
            skew = pltpu.roll(acc, 0, 1, stride=1, stride_axis=0)
            diag = jnp.sum(skew, axis=0, keepdims=True)
            out_rows.append(pltpu.roll(jnp.broadcast_to(diag, (8, 128)), 128 - (GRID_W - NA_WIN_COLS), 1)[:1])
        out_rows.append(jnp.zeros((1, 128), F32))
        res = jnp.concatenate(out_rows, axis=0)
        o_ref[...] = jnp.where(lax.broadcasted_iota(jnp.int32, res.shape, 1) < n_cols, res, 0.0)

    return pl.pallas_call(
        body, name="na_rpb_grad", grid=(H,),
        in_specs=[pl.BlockSpec((None, NA_BASES, GRID_W, NA_KEYS), lambda h: (h, 0, 0, 0))],
        out_specs=pl.BlockSpec((None, n_rows + 1, 128), lambda h: (h, 0, 0)),
        out_shape=jax.ShapeDtypeStruct((H, n_rows + 1, 128), F32),
        compiler_params=_cparams(("parallel",)),
    )(jnp.flip(dtab, axis=2))


BAND_Q = 128
BAND_KEYS = BAND_Q + 2 * DIL_RADIUS


def _band_geometry(n, L):
    q0 = pl.multiple_of(n * BAND_Q, BAND_Q)
    k0 = pl.multiple_of(jnp.clip(q0 - DIL_RADIUS, 0, L - BAND_KEYS), DIL_RADIUS)
    qi = q0 + lax.broadcasted_iota(jnp.int32, (BAND_Q, BAND_KEYS), 0)
    kj = k0 + lax.broadcasted_iota(jnp.int32, (BAND_Q, BAND_KEYS), 1)
    return q0, k0, jnp.abs(qi - kj) <= DIL_RADIUS


DIL_PAIRS = DIL_OUT_WIDTH // 128


def _residue_shape(dil, T, dtype):
    return jax.ShapeDtypeStruct((DIL_PAIRS, dil, T // dil, 128), dtype)


def _residue_tile(dil, tm):
    return pl.BlockSpec((DIL_PAIRS, dil, tm // dil, 128), lambda i: (0, 0, i, 0))


def _to_natural(ref, scratch, dil, tm):
    tiles = []
    for pair in range(DIL_PAIRS):
        if dil == 1:
            tiles.append(ref[pair, 0].astype(F32))
            continue
        for r in range(dil):
            scratch[pl.ds(r, tm // dil, stride=dil), :] = ref[pair, r].astype(F32)
        tiles.append(scratch[...])
    return tiles


def _from_natural(tile, scratch, ref, pair, dil, tm):
    if dil == 1:
        ref[pair, 0] = tile.astype(ref.dtype)
        return
    scratch[...] = tile
    for r in range(dil):
        ref[pair, r] = scratch[pl.ds(r, tm // dil, stride=dil), :].astype(ref.dtype)


def _band_specs(group, T):
    dil = DIL_GROUPS[group][1]
    L = T // dil
    assert L % BAND_Q == 0 and L >= BAND_KEYS, (T, dil)
    return L, (dil * DIL_PAIRS,), pl.BlockSpec((None, None, L, 128), lambda s: (s % DIL_PAIRS, s // DIL_PAIRS, 0, 0))


BAND_BLOCKS_PER_STEP = 4


def _band_softmax(s, valid):
    s = jnp.where(valid, s, NEG_INF)
    m = jnp.max(s, axis=-1, keepdims=True)
    p = jnp.exp(s - m)
    l = jnp.sum(p, axis=-1, keepdims=True)
    return p / l, m + jnp.log(l)


def _band_fwd(q, k, v, group):
    T = q.shape[1] * q.shape[2]
    L, grid, spec = _band_specs(group, T)
    U = min(BAND_BLOCKS_PER_STEP, L // BAND_Q)

    def body(q_ref, k_ref, v_ref, o_ref, lse_ref):
        def step(it, carry):
            geo = [_band_geometry(it * U + u, L) for u in range(U)]
            ss = [lax.dot_general(jnp.concatenate(_split_pair(q_ref[pl.ds(q0, BAND_Q), :]), axis=0),
                                  k_ref[pl.ds(k0, BAND_KEYS), :], _NT, preferred_element_type=F32) for q0, k0, _ in geo]
            pls = [_band_softmax(s, jnp.concatenate([valid, valid], axis=0)) for s, (_, _, valid) in zip(ss, geo)]
            os = [jnp.dot(p.astype(BF), v_ref[pl.ds(k0, BAND_KEYS), :], preferred_element_type=F32) for (p, _), (_, k0, _) in zip(pls, geo)]
            for (q0, _, _), o2, (_, lse) in zip(geo, os, pls):
                o_ref[pl.ds(q0, BAND_Q), :] = _join_pair(o2[:BAND_Q], o2[BAND_Q:])
                lse2 = jnp.broadcast_to(lse, (2 * BAND_Q, 128))
                lse_ref[pl.ds(q0, BAND_Q), :] = _join_pair(lse2[:BAND_Q], lse2[BAND_Q:])
            return carry

        lax.fori_loop(0, L // (BAND_Q * U), step, 0)

    res = _residue_shape(DIL_GROUPS[group][1], T, F32)
    return pl.pallas_call(
        body, name=f"band_fwd_g{group}", grid=grid,
        in_specs=[spec] * 3, out_specs=[spec] * 2, out_shape=[res, res],
        compiler_params=_cparams(("parallel",)),
    )(q, k, v)


def _band_bwd(q, k, v, do, dlse, group):
    T = q.shape[1] * q.shape[2]
    L, grid, spec = _band_specs(group, T)
    U = min(BAND_BLOCKS_PER_STEP, L // BAND_Q)

    def body(q_ref, k_ref, v_ref, do_ref, dlse_ref, dq_ref, dk_ref, dv_ref):
        dk_ref[...] = jnp.zeros_like(dk_ref)
        dv_ref[...] = jnp.zeros_like(dv_ref)

        def step(it, carry):
            geo = [_band_geometry(it * U + u, L) for u in range(U)]
            q2s = [jnp.concatenate(_split_pair(q_ref[pl.ds(q0, BAND_Q), :]), axis=0) for q0, _, _ in geo]
            do2s = [jnp.concatenate(_split_pair(do_ref[pl.ds(q0, BAND_Q), :]), axis=0) for q0, _, _ in geo]
            ss = [lax.dot_general(q2, k_ref[pl.ds(k0, BAND_KEYS), :], _NT, preferred_element_type=F32) for q2, (_, k0, _) in zip(q2s, geo)]
            dps = [lax.dot_general(do2, v_ref[pl.ds(k0, BAND_KEYS), :], _NT, preferred_element_type=F32) for do2, (_, k0, _) in zip(do2s, geo)]
            ps = [_band_softmax(s, jnp.concatenate([valid, valid], axis=0))[0] for s, (_, _, valid) in zip(ss, geo)]
            dss = []
            for p, dp, (q0, _, _) in zip(ps, dps, geo):
                dl = dlse_ref[pl.ds(q0, BAND_Q), :]
                dl2 = jnp.concatenate([dl[:, :1], dl[:, HEAD_DIM:HEAD_DIM + 1]], axis=0)
                dss.append(p * (dp - jnp.sum(dp * p, axis=-1, keepdims=True) + dl2))
            dvs = [lax.dot_general(p.astype(BF), do2, _TN, preferred_element_type=F32) for p, do2 in zip(ps, do2s)]
            dsbs = [ds.astype(BF) for ds in dss]
            dqs = [jnp.dot(dsb, k_ref[pl.ds(k0, BAND_KEYS), :], preferred_element_type=F32) for dsb, (_, k0, _) in zip(dsbs, geo)]
            dks = [lax.dot_general(dsb, q2, _TN, preferred_element_type=F32) for dsb, q2 in zip(dsbs, q2s)]
            for u, (q0, k0, _) in enumerate(geo):
                dq_ref[pl.ds(q0, BAND_Q), :] = _join_pair(dqs[u][:BAND_Q], dqs[u][BAND_Q:])
                dk_ref[pl.ds(k0, BAND_KEYS), :] += dks[u]
                dv_ref[pl.ds(k0, BAND_KEYS), :] += dvs[u]
            return carry

        lax.fori_loop(0, L // (BAND_Q * U), step, 0)

    res = _residue_shape(DIL_GROUPS[group][1], T, F32)
    return pl.pallas_call(
        body, name=f"band_bwd_g{group}", grid=grid,
        in_specs=[spec] * 5, out_specs=[spec] * 3, out_shape=[res] * 3,
        compiler_params=_cparams(("parallel",)),
    )(q, k, v, do, dlse)


def _head_sums(t):
    head = lax.broadcasted_iota(jnp.int32, t.shape, 1) // HEAD_DIM
    out = jnp.zeros_like(t)
    for h in range(t.shape[1] // HEAD_DIM):
        mine = head == h
        out = jnp.where(mine, jnp.sum(jnp.where(mine, t, 0.0), axis=-1, keepdims=True), out)
    return out


def _dil_merge_fwd(os, lses, T, tm):
    G = len(DIL_GROUPS)
    W = DIL_OUT_WIDTH
    dils = [d for _, d in DIL_GROUPS]

    def body(*refs):
        o_refs, lse_refs = refs[:G], refs[G:2 * G]
        y_ref, w_refs, on_refs, scratch = refs[2 * G], refs[2 * G + 1:3 * G + 1], refs[3 * G + 1:4 * G + 1], refs[-1]
        o = [jnp.concatenate(_to_natural(r, scratch, d, tm), axis=1) for r, d in zip(o_refs, dils)]
        ls = [jnp.concatenate(_to_natural(r, scratch, d, tm), axis=1) for r, d in zip(lse_refs, dils)]
        m = functools.reduce(jnp.maximum, ls)
        es = [jnp.exp(l - m) for l in ls]
        tot = functools.reduce(jnp.add, es)
        ws = [e / tot for e in es]
        y_ref[...] = functools.reduce(jnp.add, [w * t for w, t in zip(ws, o)]).astype(y_ref.dtype)
        for g in range(G):
            w_refs[g][...] = ws[g]
            on_refs[g][...] = o[g]

    nat = pl.BlockSpec((tm, W), lambda i: (i, 0))
    res = pl.pallas_call(
        body, name="dil_merge_fwd", grid=(T // tm,),
        in_specs=[_residue_tile(d, tm) for d in dils] * 2,
        out_specs=[nat] * (2 * G + 1),
        out_shape=[jax.ShapeDtypeStruct((T, W), BF)] + [jax.ShapeDtypeStruct((T, W), F32)] * (2 * G),
        scratch_shapes=[pltpu.VMEM((tm, 128), F32)],
        compiler_params=_cparams(("parallel",)),
    )(*os, *lses)
    return res[0], res[1:G + 1], res[G + 1:]


def _dil_merge_bwd(dy, os, ws, tm, after=()):
    G = len(DIL_GROUPS)
    T, W = dy.shape
    dils = [d for _, d in DIL_GROUPS]
    n_after = len(after)

    def body(*refs):
        dyt = refs[0][...]
        o, w = [r[...] for r in refs[1:G + 1]], [r[...] for r in refs[G + 1:2 * G + 1]]
        refs = refs[2 * G + 1 + n_after:]
        do_refs, dlse_refs, scratch = refs[:G], refs[G:2 * G], refs[-1]
        dws = [_head_sums(dyt * t) for t in o]
        mean = functools.reduce(jnp.add, [a * b for a, b in zip(w, dws)])
        for g, d in enumerate(dils):
            do, dlse = w[g] * dyt, w[g] * (dws[g] - mean)
            for pair in range(DIL_PAIRS):
                cols = slice(pair * 128, (pair + 1) * 128)
                _from_natural(do[:, cols], scratch, do_refs[g], pair, d, tm)
                _from_natural(dlse[:, cols], scratch, dlse_refs[g], pair, d, tm)

    nat = pl.BlockSpec((tm, W), lambda i: (i, 0))
    res = pl.pallas_call(
        body, name="dil_merge_bwd", grid=(T // tm,),
        in_specs=[nat] * (2 * G + 1) + [ANY] * n_after,
        out_specs=[_residue_tile(d, tm) for d in dils] * 2,
        out_shape=[_residue_shape(d, T, BF) for d in dils] + [_residue_shape(d, T, F32) for d in dils],
        scratch_shapes=[pltpu.VMEM((tm, 128), F32)],
        compiler_params=_cparams(("parallel",)),
    )(dy, *os, *ws, *after)
    return res[:G], res[G:]


def _qkv_prep(z, cos2, sin_signed, tm):
    T = z.shape[0]
    G = len(DIL_GROUPS)
    dils = [d for _, d in DIL_GROUPS]
    n_dil_blocks = 3 * DIL_WIDTH // 128

    def body(*refs):
        blocks = refs[:n_dil_blocks]
        cos_ref, sin_ref = refs[n_dil_blocks], refs[1 + n_dil_blocks]
        outs = refs[2 + n_dil_blocks:]
        for part in range(3):
            for g, d in enumerate(dils):
                out = outs[g * 3 + part]
                for pair in range(DIL_PAIRS):
                    blk = blocks[part * (DIL_WIDTH // 128) + g * DIL_PAIRS + pair]
                    for r in range(d):
                        rows = pl.ds(r, tm // d, stride=d) if d > 1 else slice(None)
                        x = blk[rows, :]
                        if part < 2:
                            x = _rope(x, cos_ref[rows, :], sin_ref[rows, :])
                        if part == 0:
                            x = x * Q_SCALE
                        out[pair, r] = x.astype(out.dtype)

    lane_block = [pl.BlockSpec((tm, 128), functools.partial(lambda b, i: (i, b), b)) for b in range(n_dil_blocks)]
    tab = pl.BlockSpec((tm, 128), lambda i: (i, 0))
    res = pl.pallas_call(
        body, name="qkv_prep", grid=(T // tm,),
        in_specs=lane_block + [tab, tab],
        out_specs=[_residue_tile(d, tm) for d in dils for _ in range(3)],
        out_shape=[_residue_shape(d, T, BF) for d in dils for _ in range(3)],
        compiler_params=_cparams(("parallel",)),
    )(*[z] * n_dil_blocks, cos2, sin_signed)
    return [res[3 * g:3 + 3 * g] for g in range(G)]


def _qkv_unprep(d_na, d_dil, cos2, sin_signed, tm, after=()):
    T = d_na[0].shape[0]
    G = len(DIL_GROUPS)
    dils = [d for _, d in DIL_GROUPS]
    n_after = len(after)

    def body(*refs):
        dq, dk, dv = (r[...] for r in refs[:3])
        res_refs = refs[3:3 + 3 * G]
        cs, sn = refs[3 + 3 * G][...], refs[4 + 3 * G][...]
        out, scratch = refs[5 + 3 * G + n_after], refs[-1]
        cols = [dq * Q_SCALE, dk, dv]
        for part in range(3):
            for g, d in enumerate(dils):
                for x in _to_natural(res_refs[g * 3 + part], scratch, d, tm):
                    if part < 2:
                        x = _rope(x, cs, -sn)
                    cols.append(x * Q_SCALE if part == 0 else x)
        out[...] = jnp.concatenate(cols, axis=1).astype(out.dtype)

    wide = pl.BlockSpec((tm, NA_WIDTH), lambda i: (i, 0))
    tab = pl.BlockSpec((tm, 128), lambda i: (i, 0))
    return pl.pallas_call(
        body, name="qkv_unprep", grid=(T // tm,),
        in_specs=[wide] * 3 + [_residue_tile(d, tm) for d in dils for _ in range(3)] + [tab, tab] + [ANY] * n_after,
        out_specs=pl.BlockSpec((tm, QKV_WIDTH), lambda i: (i, 0)),
        out_shape=jax.ShapeDtypeStruct((T, QKV_WIDTH), BF),
        scratch_shapes=[pltpu.VMEM((tm, 128), F32)],
        compiler_params=_cparams(("parallel",)),
    )(*d_na, *[t for g in range(G) for t in d_dil[g]], cos2, sin_signed, *after)


def _rope_tables(positions):
    half = HEAD_DIM // 2
    inv_freq = ROPE_THETA ** (-jnp.arange(half, dtype=F32) / half)
    ang = positions.astype(F32)[:, None] * inv_freq
    cos, sin = jnp.cos(ang), jnp.sin(ang)
    return jnp.tile(jnp.concatenate([cos, cos], axis=1), (1, 2)), jnp.tile(jnp.concatenate([-sin, sin], axis=1), (1, 2))


def _pack_rows(t):
    return t.reshape(-1, PACK_W)


def _me():
    return lax.axis_index("x"), lax.axis_index("y"), lax.axis_index("c")


def _other_chips(x, y):
    return [(1 - x, y), (x, 1 - y), (1 - x, 1 - y)]


def _gather_weights(packed):
    R, W = packed.shape
    half = R // 2

    def body(in_ref, out_ref, send_sems, recv_sems):
        x, y, c = _me()
        sibling = (x, y, 1 - c)
        chips = _other_chips(x, y)

        def block(chip, core):
            return out_ref.at[2 * chip[0] + chip[1], pl.ds(core * half, half), :]

        def copy(k, chip, core, to, src=None):
            return pltpu.make_async_remote_copy(
                src_ref=block(chip, core) if src is None else src, dst_ref=block(chip, core),
                send_sem=send_sems.at[k], recv_sem=recv_sems.at[k], device_id=to, device_id_type=MESH)

        first = [copy(j, (x, y), c, (*chip, c), src=in_ref.at[pl.ds(c * half, half), :]) for j, chip in enumerate(chips)]
        for cp in first:
            cp.start()
        passed = [copy(3 + j, chip, c, sibling) for j, chip in enumerate(chips)]
        for j, chip in enumerate(chips):
            copy(j, chip, c, (x, y, c)).wait_recv()
            passed[j].start()
        for j, chip in enumerate(chips):
            copy(3 + j, chip, 1 - c, (x, y, c)).wait_recv()
        for cp in first + passed:
            cp.wait_send()

    others = pl.pallas_call(
        body, name="gather_weights",
        in_specs=[ANY], out_specs=ANY,
        out_shape=jax.ShapeDtypeStruct((N_CHIPS, R, W), packed.dtype),
        scratch_shapes=[pltpu.SemaphoreType.DMA((6,)), pltpu.SemaphoreType.DMA((6,))],
    )(packed)
    return lax.dynamic_update_slice(others, packed[None], (2 * lax.axis_index("x") + lax.axis_index("y"), 0, 0))


def _pair_sum(g, got, tm):
    S, R, W = g.shape
    half = R // 2
    nb = half // tm

    def body(c_ref, g_ref, got_ref, o_ref, ob_ref):
        tot = g_ref[...] + got_ref[...]
        o_ref[...] = tot
        ob_ref[...] = tot.astype(ob_ref.dtype)

    tile = pl.BlockSpec((None, tm, W), lambda s, i, c_ref: (s, i, 0))
    return pl.pallas_call(
        body, name="pair_sum",
        grid_spec=pltpu.PrefetchScalarGridSpec(
            num_scalar_prefetch=1, grid=(S, nb),
            in_specs=[pl.BlockSpec((None, tm, W), lambda s, i, c_ref: (s, c_ref[0] * nb + i, 0)), tile],
            out_specs=[tile, tile]),
        out_shape=[jax.ShapeDtypeStruct((S, half, W), F32), jax.ShapeDtypeStruct((S, half, W), BF)],
        compiler_params=_cparams(("parallel", "parallel")),
    )(lax.axis_index("c").reshape(1).astype(jnp.int32), g, got)


def _chip_sum(own, others, tm):
    n, h, W = others.shape

    def body(own_ref, p_ref, o_ref):
        o_ref[...] = ((own_ref[...] + p_ref[0].astype(F32)) + p_ref[1].astype(F32)) + p_ref[2].astype(F32)

    return pl.pallas_call(
        body, name="chip_sum", grid=(h // tm,),
        in_specs=[pl.BlockSpec((tm, W), lambda i: (i, 0)), pl.BlockSpec((n, tm, W), lambda i: (0, i, 0))],
        out_specs=pl.BlockSpec((tm, W), lambda i: (i, 0)),
        out_shape=jax.ShapeDtypeStruct((h, W), F32),
        compiler_params=_cparams(("parallel",)),
    )(own, others)


def _join_halves(mine):
    h, W = mine.shape

    def body(m_ref, out_ref, send_sem, recv_sem):
        x, y, c = _me()
        cp = pltpu.make_async_remote_copy(
            src_ref=m_ref, dst_ref=out_ref.at[pl.ds(c * h, h), :],
            send_sem=send_sem, recv_sem=recv_sem, device_id=(x, y, 1 - c), device_id_type=MESH)
        cp.start()
        pltpu.make_async_remote_copy(
            src_ref=m_ref, dst_ref=out_ref.at[pl.ds((1 - c) * h, h), :],
            send_sem=send_sem, recv_sem=recv_sem, device_id=(x, y, 1 - c), device_id_type=MESH).wait_recv()
        cp.wait_send()

    other = pl.pallas_call(
        body, name="join_halves", in_specs=[ANY], out_specs=ANY,
        out_shape=jax.ShapeDtypeStruct((2 * h, W), mine.dtype),
        scratch_shapes=[pltpu.SemaphoreType.DMA, pltpu.SemaphoreType.DMA],
    )(mine)
    return lax.dynamic_update_slice(other, mine, (lax.axis_index("c") * h, 0))


def _allreduce_small(s, after=()):
    R, W = s.shape
    n_after = len(after)

    def body(s_ref, *rest):
        o_ref, buf, send_sems, recv_sems = rest[n_after:]
        x, y, c = _me()
        me = 4 * x + 2 * y + c
        buf[me] = s_ref[...]
        peers = [((x + fx) % 2, (y + fy) % 2, (c + fc) % 2) for fx in range(2) for fy in range(2) for fc in range(2)][1:]
        sends = [pltpu.make_async_remote_copy(
            src_ref=s_ref, dst_ref=buf.at[me], send_sem=send_sems.at[k], recv_sem=recv_sems.at[k],
            device_id=peer, device_id_type=MESH) for k, peer in enumerate(peers)]
        for cp in sends:
            cp.start()
        for k, peer in enumerate(peers):
            pltpu.make_async_remote_copy(
                src_ref=s_ref, dst_ref=buf.at[4 * peer[0] + 2 * peer[1] + peer[2]], send_sem=send_sems.at[k],
                recv_sem=recv_sems.at[k], device_id=peer, device_id_type=MESH).wait_recv()
        for cp in sends:
            cp.wait_send()
        total = buf[0]
        for d in range(1, N_DEV):
            total = total + buf[d]
        o_ref[...] = total

    return pl.pallas_call(
        body, name="allreduce_small",
        in_specs=[pl.BlockSpec(memory_space=pltpu.VMEM)] + [ANY] * n_after, out_specs=pl.BlockSpec(memory_space=pltpu.VMEM),
        out_shape=jax.ShapeDtypeStruct((R, W), F32),
        scratch_shapes=[pltpu.VMEM((N_DEV, R, W), F32), pltpu.SemaphoreType.DMA((N_DEV - 1,)), pltpu.SemaphoreType.DMA((N_DEV - 1,))],
    )(s, *after)


HBM_SPEC = pl.BlockSpec(memory_space=pltpu.HBM)
SEM_SPEC = pl.BlockSpec(memory_space=pltpu.SEMAPHORE)
DATAFLOW = pltpu.SideEffectType.DATAFLOW_SIDE_EFFECTING


class _InFlight(NamedTuple):
    sems: tuple
    src: jax.Array
    land: jax.Array
    token: jax.Array


def _split_start(name, src, land_shape, land_dtype, n, copies, after=()):
    n_after = len(after)

    def body(src_ref, land_ref, *rest):
        rest = rest[n_after:]
        sems, token = rest[:2 * n], rest[-1]
        for k, (s, d, peer) in enumerate(copies(src_ref, land_ref)):
            pltpu.make_async_remote_copy(src_ref=s, dst_ref=d, send_sem=sems[k], recv_sem=sems[n + k],
                                         device_id=peer, device_id_type=MESH).start()
        token[...] = jnp.zeros_like(token)

    outs = pl.pallas_call(
        body, name=name,
        out_shape=(*[pltpu.SemaphoreType.DMA(())] * (2 * n), pltpu.HBM(src.shape, src.dtype), pltpu.HBM(land_shape, land_dtype),
                   jax.ShapeDtypeStruct((8, 128), F32)),
        in_specs=(HBM_SPEC, HBM_SPEC, *[ANY] * n_after),
        out_specs=(*[SEM_SPEC] * (2 * n), HBM_SPEC, HBM_SPEC, pl.BlockSpec(memory_space=pltpu.VMEM)),
        input_output_aliases={0: 2 * n, 1: 2 * n + 1},
        compiler_params=pltpu.CompilerParams(has_side_effects=DATAFLOW),
    )(pltpu.with_memory_space_constraint(src, pltpu.HBM), pltpu.with_memory_space_constraint(lax.empty(land_shape, land_dtype), pltpu.HBM),
      *after)
    return _InFlight(tuple(outs[:2 * n]), outs[2 * n], outs[2 * n + 1], outs[2 * n + 2])


def _split_wait(name, flight, after, n, copies):
    def body(src_ref, land_ref, *rest):
        sems = rest[:2 * n]
        for k, (s, d, peer) in enumerate(copies(src_ref, land_ref)):
            cp = pltpu.make_async_remote_copy(src_ref=s, dst_ref=d, send_sem=sems[k], recv_sem=sems[n + k],
                                              device_id=peer, device_id_type=MESH)
            cp.wait_send()
            cp.wait_recv()

    return pl.pallas_call(
        body, name=name,
        out_shape=(pltpu.HBM(flight.src.shape, flight.src.dtype), pltpu.HBM(flight.land.shape, flight.land.dtype)),
        in_specs=(HBM_SPEC, HBM_SPEC, *[SEM_SPEC] * (2 * n), ANY),
        out_specs=(HBM_SPEC, HBM_SPEC), input_output_aliases={0: 0, 1: 1},
        compiler_params=pltpu.CompilerParams(has_side_effects=DATAFLOW),
    )(flight.src, flight.land, *flight.sems, after)


def _gather_copies(src_ref, land_ref):
    x, y, c = _me()
    return [(src_ref, land_ref.at[2 * x + y], (*chip, c)) for chip in _other_chips(x, y)]


def _gather_start(packed, tag, after=()):
    return _split_start(f"gather_start_{tag}", packed, (N_CHIPS, *packed.shape), packed.dtype, 3, _gather_copies, after)


def _gather_wait(flight, after, tag):
    src, others = _split_wait(f"gather_wait_{tag}", flight, after, 3, _gather_copies)
    return lax.dynamic_update_slice(others, src[None], (2 * lax.axis_index("x") + lax.axis_index("y"), 0, 0))


def _swap_copies(src_ref, land_ref):
    x, y, c = _me()
    half = land_ref.shape[1]
    return [(src_ref.at[:, pl.ds((1 - c) * half, half), :], land_ref, (x, y, 1 - c))]


def _swap_start(g, tag):
    S, R, W = g.shape
    return _split_start(f"swap_halves_start_{tag}", g, (S, R // 2, W), g.dtype, 1, _swap_copies)


def _swap_wait(flight, after, tag):
    return _split_wait(f"swap_halves_wait_{tag}", flight, after, 1, _swap_copies)


def _scatter_copies(src_ref, land_ref):
    x, y, c = _me()
    return [(src_ref.at[2 * chip[0] + chip[1]], land_ref.at[j], (*chip, c)) for j, chip in enumerate(_other_chips(x, y))]


def _scatter_start(part, tag):
    S, h, W = part.shape
    return _split_start(f"scatter_chips_start_{tag}", part, (S - 1, h, W), part.dtype, 3, _scatter_copies)


def _scatter_wait(flight, after, tag):
    return _split_wait(f"scatter_chips_wait_{tag}", flight, after, 3, _scatter_copies)[1]


def _join_copies(src_ref, land_ref):
    x, y, c = _me()
    h = src_ref.shape[0]
    return [(src_ref, land_ref.at[pl.ds(c * h, h), :], (x, y, 1 - c))]


def _join_start(mine):
    h, W = mine.shape
    return _split_start("join_halves_start", mine, (2 * h, W), mine.dtype, 1, _join_copies)


def _join_wait(flight, after):
    src, other = _split_wait("join_halves_wait", flight, after, 1, _join_copies)
    return lax.dynamic_update_slice(other, src, (lax.axis_index("c") * src.shape[0], 0))


def _adamw(name, g, g_row0, w, m, v):
    _, R, C = w.shape
    tm = next(cand for cand in (256, 128, 64, 32, 16, 8) if R % cand == 0)
    assert g_row0 % tm == 0 and g.shape[1] == C

    def body(g_ref, w_ref, m_ref, v_ref, go_ref, d_ref, mo_ref, vo_ref):
        gt = g_ref[...]
        mt = ADAM_B1 * m_ref[...] + (1.0 - ADAM_B1) * gt
        vt = ADAM_B2 * v_ref[...] + (1.0 - ADAM_B2) * jnp.square(gt)
        m_hat = mt / (1.0 - ADAM_B1 ** ADAM_STEP)
        v_hat = vt / (1.0 - ADAM_B2 ** ADAM_STEP)
        go_ref[...] = gt
        d_ref[...] = -ADAM_LR * (m_hat / (jnp.sqrt(v_hat) + ADAM_EPS) + ADAM_WD * w_ref[...])
        mo_ref[...] = mt
        vo_ref[...] = vt

    state = pl.BlockSpec((None, tm, C), lambda i: (0, i, 0))
    return pl.pallas_call(
        body, name=name, grid=(R // tm,),
        in_specs=[pl.BlockSpec((tm, C), lambda i: (g_row0 // tm + i, 0)), state, state, state],
        out_specs=[state] * 4, out_shape=[jax.ShapeDtypeStruct((1, R, C), F32)] * 4,
        compiler_params=_cparams(("parallel",)),
    )(g, w, m, v)


def _unpack_weights(gathered, names):
    S = gathered.shape[0]
    shard_shapes = {"w_in": (D_MODEL, (QKV_WIDTH + 2 * D_MODEL) // S), "w_branch_na": (NA_WIDTH, D_MODEL // S),
                    "w_branch_dil": (DIL_OUT_WIDTH, D_MODEL // S), "w_out": (D_MODEL // S, D_MODEL),
                    "w_up": (D_MODEL, D_FF // S), "w_down": (D_FF // S, D_MODEL),
                    "w_ple_gate": (D_MODEL // S, D_MODEL), "w_ple_proj": (PLE_DIM, D_MODEL // S)}
    col_sharded = {"w_in", "w_branch_na", "w_branch_dil", "w_up", "w_ple_proj"}
    out, r0 = {}, 0
    for name in names:
        rows, cols = shard_shapes[name]
        n = rows * cols // PACK_W
        t = gathered[:, r0:r0 + n, :].reshape(S, rows, cols)
        r0 += n
        out[name] = t.transpose(1, 0, 2).reshape(rows, S * cols) if name in col_sharded else t.reshape(S * rows, cols)
    return out


def kernel(x, p, positions, g_mix, w_in, rpb, w_branch_na, w_branch_dil, w_out, g_mlp, w_up, w_down, g_ple, w_ple_gate, w_ple_proj, g_final, loss_target, m_g_mix, m_w_in, m_rpb, m_w_branch_na, m_w_branch_dil, m_w_out, m_g_mlp, m_w_up, m_w_down, m_g_ple, m_w_ple_gate, m_w_ple_proj, m_g_final, v_g_mix, v_w_in, v_rpb, v_w_branch_na, v_w_branch_dil, v_w_out, v_g_mlp, v_w_up, v_w_down, v_g_ple, v_w_ple_gate, v_w_ple_proj, v_g_final):
    shards = {"w_in": w_in[0], "w_branch_na": w_branch_na[0], "w_branch_dil": w_branch_dil[0], "w_out": w_out[0],
              "w_up": w_up[0], "w_down": w_down[0], "w_ple_gate": w_ple_gate[0], "w_ple_proj": w_ple_proj[0]}
    params = {"w_in": w_in, "w_branch_na": w_branch_na, "w_branch_dil": w_branch_dil, "w_out": w_out, "w_up": w_up,
              "w_down": w_down, "w_ple_gate": w_ple_gate, "w_ple_proj": w_ple_proj,
              "m_w_in": m_w_in, "m_w_branch_na": m_w_branch_na, "m_w_branch_dil": m_w_branch_dil, "m_w_out": m_w_out,
              "m_w_up": m_w_up, "m_w_down": m_w_down, "m_w_ple_gate": m_w_ple_gate, "m_w_ple_proj": m_w_ple_proj,
              "v_w_in": v_w_in, "v_w_branch_na": v_w_branch_na, "v_w_branch_dil": v_w_branch_dil, "v_w_out": v_w_out,
              "v_w_up": v_w_up, "v_w_down": v_w_down, "v_w_ple_gate": v_w_ple_gate, "v_w_ple_proj": v_w_ple_proj}

    W = {"w_in": _gather_weights(shards["w_in"].astype(BF)).transpose(1, 0, 2).reshape(D_MODEL, -1)}
    mix_flight = _gather_start(jnp.concatenate([_pack_rows(shards[n].astype(BF)) for n in GATHER_MIX], axis=0), "mix")
    rest_flight = _gather_start(jnp.concatenate([_pack_rows(shards[n].astype(BF)) for n in GATHER_MLP], axis=0), "mlp",
                                after=(mix_flight.token,))
    w_qkv, w_gates = W["w_in"][:, :QKV_WIDTH], W["w_in"][:, QKV_WIDTH:]

    xs, ps, tgt = x[0], p[0, 0], loss_target[0]
    T = xs.shape[0]
    TM = 512
    gm, gl, gp, gf = g_mix, g_mlp, g_ple, g_final.reshape(1, D_MODEL)
    cos2, sin_signed = _rope_tables(positions[0])
    tab = _na_bias_table(rpb[0])

    a = _rowwise("norm_mix", lambda h, g: h * _rms(h) * g, T, TM, [_row(xs, TM), _full(gm)], [(D_MODEL, BF)],
                 after=(rest_flight.token,))
    n3 = 3 * NA_WIDTH
    qkv = _mm("in_na", a, w_qkv[:, :n3], "nn", 1024, 768, 1024, [BF])
    z_dil = _mm("in_dil", a, w_qkv[:, n3:], "nn", 1024, 1152, 1024, [F32])
    z_gates = _mm("in_gates", a, w_gates, "nn", 1024,1024, 1024, [BF])

    dil_ops = _qkv_prep(z_dil, cos2, sin_signed, TM)
    y_na = _na_fwd(qkv, tab)
    band = [_band_fwd(*dil_ops[g], g) for g in range(len(DIL_GROUPS))]
    y_dil, w_grp, o_nat = _dil_merge_fwd([b[0] for b in band], [b[1] for b in band], T, TM)

    W.update(_unpack_weights(_gather_wait(mix_flight, y_dil, "mix"), GATHER_MIX))
    u_na = _mm("branch_na", y_na, W["w_branch_na"], "nn", 1024,1024, 512, [BF])
    u_dil = _mm("branch_dil", y_dil, W["w_branch_dil"], "nn", 1024,1024, 256, [BF])
    mixed = _rowwise(
        "gate_mix", lambda gn, gd, un, ud: _sigmoid(gn.astype(F32)) * un.astype(F32) + _sigmoid(gd.astype(F32)) * ud.astype(F32), T, TM,
        [_row(z_gates, TM, 0, D_MODEL), _row(z_gates, TM, 1, D_MODEL), _row(u_na, TM), _row(u_dil, TM)], [(D_MODEL, BF)])
    def add_norm(d, h, g):
        h = h + d
        return h, h * _rms(h) * g

    h1, cn = _mm("out_proj", mixed, W["w_out"], "nn", 512, 1024, 1024, [F32, BF], epilogue=add_norm, extras=(xs,), consts=(gl,))
    W.update(_unpack_weights(_gather_wait(rest_flight, cn, "mlp"), GATHER_MLP))
    up, act = _mm("mlp_up", cn, W["w_up"], "nn", 1024,1024, 1024, [BF, BF],
                  epilogue=lambda acc: (acc, jnp.square(jnp.maximum(acc, 0.0))))
    h2, en = _mm("mlp_down", act, W["w_down"], "nn", 1024, 1024, 1024, [F32, BF], epilogue=add_norm, extras=(h1,), consts=(gp,))
    gt = _mm("ple_gate", en, W["w_ple_gate"], "nn", 1024,1024, 1024, [F32])
    pp = _mm("ple_proj", ps, W["w_ple_proj"], "nn", 1024,1024, 256, [F32])

    def head(h2t, gtt, ppt, tg, g):
        sg = _sigmoid(gtt)
        h3 = h2t + sg * ppt
        yo = h3 * _rms(h3) * g
        diff = yo - tg
        loss = 0.5 * jnp.sum(jnp.mean(jnp.square(diff), axis=-1, keepdims=True), axis=0, keepdims=True)
        dh3, dg = _rms_bwd(diff * (1.0 / D_MODEL), h3, g)
        return dh3, dh3 * ppt * sg * (1.0 - sg), dh3 * sg, jnp.broadcast_to(loss, (1, 128)), dg

    dh3, d_gt, d_pp, loss_part, dg_final = _rowwise(
        "loss_head", head, T, TM, [_row(h2, TM), _row(gt, TM), _row(pp, TM), _row(tgt, TM), _full(gf)],
        [(D_MODEL, F32), (D_MODEL, BF), (D_MODEL, BF)], sums=[128, D_MODEL])

    early_shapes = {n: shards[n].shape for n in REDUCE_EARLY}
    early_rows = sum(r * c for r, c in early_shapes.values()) // PACK_W
    shard_rows = D_MODEL // N_CHIPS
    early_buf = _mm("g_ple_gate", en, d_gt, "tn", 1024, 1024, 1024, [F32],
                    into=(jax.ShapeDtypeStruct((N_CHIPS, early_rows, PACK_W), F32), (N_CHIPS, shard_rows, PACK_W),
                          lambda i, j: (0, 2 * D_MODEL // shard_rows, 0)))
    g_ple_proj = _mm("g_ple_proj", ps, d_pp, "tn", 256, 1024, 1024,[F32])

    def add_norm_bwd(dn, dh_out, h, g):
        dh, dg = _rms_bwd(dn, h, g)
        dh = dh_out + dh
        return dh, dh, dg

    dh2, dh2_b, dg_ple = _mm("d_ple_gate", d_gt, W["w_ple_gate"], "nt", 512, 1024, 1024, [F32, BF],
                             epilogue=add_norm_bwd, extras=(dh3, h2), consts=(gp,), sums=[D_MODEL])
    d_up = _mm("d_mlp_down", dh2_b, W["w_down"], "nt", 1024,1024, 1024, [BF],
               epilogue=lambda acc, u: (acc * (2.0 * jnp.maximum(u.astype(F32), 0.0)),), extras=(up,))
    early_buf = _mm("g_mlp_down", act, dh2_b, "tn", 1024, 1024, 1024,[F32],
                    into=(early_buf, (None, D_MODEL, PACK_W), lambda i, j: (i, 1, 0)))
    early_buf = _mm("g_mlp_up", cn, d_up, "tn", 1024, 1024, 1024,[F32],
                    into=(early_buf, (None, D_MODEL, PACK_W), lambda i, j: (j, 0, 0)))
    dh1, dh1_b, dg_mlp = _mm("d_mlp_up", d_up, W["w_up"], "nt", 1024, 1024, 1024, [F32, BF], epilogue=add_norm_bwd,
                             extras=(dh2, h1), consts=(gl,), sums=[D_MODEL])
    d_mixed = _mm("d_out_proj", dh1_b, W["w_out"], "nt", 1024,1024, 1024, [F32])
    early_buf = _mm("g_out_proj", mixed, dh1_b, "tn", 1024, 1024, 1024, [F32],
                    into=(early_buf, (N_CHIPS, shard_rows, PACK_W), lambda i, j: (0, 2 * D_MODEL // shard_rows + 1, 0)))

    def gate_bwd(dm, gn, gd, un, ud):
        gn, gd, un, ud = (t.astype(F32) for t in (gn, gd, un, ud))
        sn, sd = _sigmoid(gn), _sigmoid(gd)
        return jnp.concatenate([dm * un * sn * (1.0 - sn), dm * ud * sd * (1.0 - sd)], axis=1), dm * sn, dm * sd

    dz_gates, d_u_na, d_u_dil = _rowwise(
        "gate_mix_bwd", gate_bwd, T, TM,
        [_row(d_mixed, TM), _row(z_gates, TM, 0, D_MODEL), _row(z_gates, TM, 1, D_MODEL), _row(u_na, TM), _row(u_dil, TM)],
        [(2 * D_MODEL, BF), (D_MODEL, BF), (D_MODEL, BF)])
    g_branch_na = _mm("g_branch_na", y_na, d_u_na, "tn", 1024, 1024, 1024,[F32])
    g_branch_dil = _mm("g_branch_dil", y_dil, d_u_dil, "tn", 256, 1024, 1024,[F32])
    small_rows = [jnp.concatenate([_pack_rows(g[:, s * shard_rows:(s + 1) * shard_rows]) for g in (g_ple_proj, g_branch_na, g_branch_dil)],
                                  axis=0) for s in range(N_CHIPS)]
    early_buf = lax.dynamic_update_slice(early_buf, jnp.stack(small_rows), (0, 2 * D_MODEL + 2 * shard_rows, 0))
    early_tm = early_rows // 4
    swap_flight = _swap_start(early_buf, "early")
    d_y_na = _mm("d_branch_na", d_u_na, W["w_branch_na"], "nt", 1024,512, 1024, [BF], after=(swap_flight.token,))
    d_y_dil = _mm("d_branch_dil", d_u_dil, W["w_branch_dil"], "nt", 1024,256, 1024, [F32])

    dqa, dka, dva, dtab = _na_bwd(qkv, tab, d_y_na)
    early_g, early_got = _swap_wait(swap_flight, dqa, "early")
    early_pair, early_pair_b = _pair_sum(early_g, early_got, early_tm)
    scatter_flight = _scatter_start(early_pair_b, "early")
    d_rpb = _na_rpb_grad(dtab)[:, :2 * NA_WIN_ROWS - 1, :2 * NA_WIN_COLS - 1]

    do_res, dlse_res = _dil_merge_bwd(d_y_dil, o_nat, w_grp, TM, after=(scatter_flight.token,))
    d_dil = [_band_bwd(*dil_ops[g], do_res[g], dlse_res[g], g) for g in range(len(DIL_GROUPS))]

    dz_qkv = _qkv_unprep((dqa, dka, dva), d_dil, cos2, sin_signed, TM)
    g_in = jnp.concatenate([
        _mm("g_in_qkv", a, dz_qkv, "tn", 1024, 1280, 1024,[F32]),
        _mm("g_in_gates", a, dz_gates, "tn", 1024, 1024, 1024,[F32])], axis=1)
    me_chip = 2 * lax.axis_index("x") + lax.axis_index("y")
    early_mine = _chip_sum(lax.dynamic_index_in_dim(early_pair, me_chip, 0, keepdims=False),
                           _scatter_wait(scatter_flight, g_in, "early"), early_tm)
    join_flight = _join_start(early_mine)
    in_cols = g_in.shape[1] // N_CHIPS
    late_tm = 256
    late_swap = _swap_start(jnp.stack([g_in[:, s * in_cols:(s + 1) * in_cols] for s in range(N_CHIPS)]), "late")
    d_a = _mm("d_in_qkv", dz_qkv, w_qkv, "nt", 1024,1024, 1280, [F32], after=(late_swap.token, join_flight.token))
    late_g, late_got = _swap_wait(late_swap, d_a, "late")
    late_pair, late_pair_b = _pair_sum(late_g, late_got, late_tm)
    late_scatter = _scatter_start(late_pair_b, "late")
    def first_bwd(dn_gates, dn_qkv, dh_out, h, g):
        dh, dg = _rms_bwd(dn_gates + dn_qkv, h, g)
        return dh_out + dh, dg

    grad_x, dg_mix = _mm("d_in_gates", dz_gates, w_gates, "nt", 512, 1024, 1024, [F32], epilogue=first_bwd,
                         extras=(d_a, dh1, xs), consts=(gm,), sums=[D_MODEL], after=(late_scatter.token,))
    early_shard = _join_wait(join_flight, grad_x)

    n_rpb = rpb.size
    rpb_rows = 4
    small = jnp.concatenate([
        dg_mix, dg_mlp, dg_ple, dg_final,
        jnp.pad(d_rpb.reshape(-1), (0, rpb_rows * D_MODEL - n_rpb)).reshape(rpb_rows, D_MODEL),
        jnp.pad(loss_part, ((0, 0), (0, D_MODEL - loss_part.shape[1]))),
        jnp.zeros((SMALL_ROWS - 5 - rpb_rows, D_MODEL), F32)], axis=0)
    out = {"grad": {}, "delta": {}, "new_m": {}, "new_v": {}}

    def update(n, g, row0):
        res = _adamw("adamw_" + n, g, row0, params[n], params["m_" + n], params["v_" + n])
        for kind, t in zip(("grad", "delta", "new_m", "new_v"), res, strict=True):
            out[kind][n] = t

    row0 = 0
    for n in REDUCE_EARLY:
        rows, cols = early_shapes[n]
        n_rows = rows * cols // PACK_W
        if cols == PACK_W:
            update(n, early_shard, row0)
        else:
            update(n, early_shard[row0:row0 + n_rows].reshape(rows, cols), 0)
        row0 += n_rows
    late_others = _scatter_wait(late_scatter, out["new_v"][REDUCE_EARLY[-1]], "late")
    late_mine = _chip_sum(lax.dynamic_index_in_dim(late_pair, me_chip, 0, keepdims=False), late_others, late_tm)
    small = _allreduce_small(small, after=(late_mine,))
    update("w_in", _join_halves(late_mine), 0)
    loss = small[4 + rpb_rows, 0]

    def small_pack(a0, a1, a2, a3, r):
        return jnp.concatenate([a0.reshape(1, -1), a1.reshape(1, -1), a2.reshape(1, -1), a3.reshape(1, -1),
                                jnp.pad(r.reshape(-1), (0, rpb_rows * D_MODEL - n_rpb)).reshape(rpb_rows, D_MODEL)], axis=0)

    small_res = _adamw("adamw_small", small, 0, small_pack(g_mix, g_mlp, g_ple, g_final, rpb)[None],
                       small_pack(m_g_mix, m_g_mlp, m_g_ple, m_g_final, m_rpb)[None],
                       small_pack(v_g_mix, v_g_mlp, v_g_ple, v_g_final, v_rpb)[None])

    def small_unpack(t):
        return {"g_mix": t[0].reshape(g_mix.shape), "g_mlp": t[1].reshape(g_mlp.shape), "g_ple": t[2].reshape(g_ple.shape),
                "g_final": t[3].reshape(g_final.shape), "rpb": t[4:].reshape(-1)[:n_rpb].reshape(rpb.shape)}

    for kind, t in zip(("grad", "delta", "new_m", "new_v"), small_res, strict=True):
        out[kind].update(small_unpack(t[0]))

    order = ["g_mix", "w_in", "rpb", "w_branch_na", "w_branch_dil", "w_out", "g_mlp", "w_up", "w_down", "g_ple",
             "w_ple_gate", "w_ple_proj", "g_final"]
    return (loss, grad_x[None], *[out["grad"][n] for n in order], *[out["delta"][n] for n in order],
            *[out["new_m"][n] for n in order], *[out["new_v"][n] for n in order])
```

```python
import functools
from typing import NamedTuple

import jax
import jax.numpy as jnp
from jax import lax
from jax.experimental import pallas as pl
from jax.experimental.pallas import tpu as pltpu

BF = jnp.bfloat16
F32 = jnp.float32
MESH = pl.DeviceIdType.MESH
ANY = pl.BlockSpec(memory_space=pl.ANY)

V7X_VMEM_BYTES = 64 * 1024 * 1024
VMEM_LIMIT = V7X_VMEM_BYTES - 16 * 1024 * 1024

D_MODEL = 1024
HEAD_DIM = 64
GRID_W = 64
NA_HEADS = 8
NA_WIN_ROWS = 8
NA_WIN_COLS = 16
NA_WIDTH = NA_HEADS * HEAD_DIM
DIL_GROUPS = ((128, 1), (512, 4), (2048, 16))
DIL_HPG = 4
DIL_HEADS = DIL_HPG * len(DIL_GROUPS)
DIL_WIDTH = DIL_HEADS * HEAD_DIM
DIL_OUT_WIDTH = DIL_HPG * HEAD_DIM
DIL_RADIUS = 64
QKV_WIDTH = 3 * NA_WIDTH + 3 * DIL_WIDTH
D_FF = 4 * D_MODEL
PLE_DIM = 256
ROPE_THETA = 10000.0
RMS_EPS = 1e-6
NEG_INF = -1e30
Q_SCALE = HEAD_DIM ** -0.5

ADAM_LR = 0.001
ADAM_B1 = 0.9
ADAM_B2 = 0.999
ADAM_EPS = 1e-08
ADAM_WD = 0.01
ADAM_STEP = 10

N_CHIPS = 4
N_DEV = 8
PACK_W = 1024
BIG = ("w_in", "w_branch_na", "w_branch_dil", "w_out", "w_up", "w_down", "w_ple_gate", "w_ple_proj")
GATHER_MIX = ("w_branch_na", "w_branch_dil", "w_out")
GATHER_MLP = ("w_up", "w_down", "w_ple_gate", "w_ple_proj")
REDUCE_EARLY = ("w_up", "w_down", "w_ple_gate", "w_out", "w_ple_proj", "w_branch_na", "w_branch_dil")
SMALL_ROWS = 16


def _cparams(sem=None):
    return pltpu.CompilerParams(dimension_semantics=sem, vmem_limit_bytes=VMEM_LIMIT)


def _mm(name, a, b, mode, tm, tn, tk, out_dtypes, epilogue=None, extras=(), consts=(), sums=(), after=(), into=None):
    if mode == "nn":
        (M, K), N = a.shape, b.shape[1]
    elif mode == "nt":
        (M, K), N = a.shape, b.shape[0]
    else:
        (K, M), N = a.shape, b.shape[1]
    tm, tn, tk = min(tm, M), min(tn, N), min(tk, K)
    assert M % tm == 0 and N % tn == 0 and K % tk == 0, (name, M, N, K, tm, tn, tk)
    if mode == "nn":
        a_spec = pl.BlockSpec((tm, tk), lambda i, j, k: (i, k))
        b_spec = pl.BlockSpec((tk, tn), lambda i, j, k: (k, j))
        dims = (((1,), (0,)), ((), ()))
    elif mode == "nt":
        a_spec = pl.BlockSpec((tm, tk), lambda i, j, k: (i, k))
        b_spec = pl.BlockSpec((tn, tk), lambda i, j, k: (j, k))
        dims = (((1,), (1,)), ((), ()))
    else:
        a_spec = pl.BlockSpec((tk, tm), lambda i, j, k: (k, i))
        b_spec = pl.BlockSpec((tk, tn), lambda i, j, k: (k, j))
        dims = (((0,), (0,)), ((), ()))
    nk = K // tk
    n_extra, n_const, n_out, n_sum = len(extras), len(consts), len(out_dtypes), len(sums)
    tile = pl.BlockSpec((tm, tn), lambda i, j, k: (i, j))
    assert not sums or tn == N, "row sums need whole rows in a tile"

    n_after = len(after)

    def body(a_ref, b_ref, *rest):
        extra_refs, rest = rest[:n_extra + n_const], rest[n_extra + n_const + n_after:]
        out_refs, sum_refs, acc = rest[:n_out], rest[n_out:n_out + n_sum], rest[-1]
        i, k = pl.program_id(0), pl.program_id(2)
        def product():
            return lax.dot_general(a_ref[...].astype(BF), b_ref[...].astype(BF), dims, preferred_element_type=F32)

        if nk > 1:
            @pl.when(k == 0)
            def _():
                acc[...] = jnp.zeros_like(acc)

            acc[...] += product()

        @pl.when(k == nk - 1)
        def _():
            total = product() if nk == 1 else acc[...]
            outs = (total,) if epilogue is None else epilogue(total, *[e[...] for e in extra_refs])
            for o_ref, val in zip(out_refs, outs[:n_out], strict=True):
                o_ref[...] = val.astype(o_ref.dtype).reshape(o_ref.shape)
            for s_ref, val in zip(sum_refs, outs[n_out:], strict=True):
                @pl.when(i == 0)
                def _():
                    s_ref[...] = val

                @pl.when(i != 0)
                def _():
                    s_ref[...] += val

    out_specs = [tile] * n_out + [pl.BlockSpec((1, c), lambda i, j, k: (0, 0)) for c in sums]
    out_shape = [jax.ShapeDtypeStruct((M, N), dt) for dt in out_dtypes] + [jax.ShapeDtypeStruct((1, c), F32) for c in sums]
    operands, aliases = [a, b, *extras, *consts, *after], {}
    in_specs = ([a_spec, b_spec] + [tile] * n_extra
                + [pl.BlockSpec(c.shape, functools.partial(lambda nd, i, j, k: (0,) * nd, c.ndim)) for c in consts] + [ANY] * n_after)
    if into is not None:
        assert n_out == 1
        target, block, index = into
        out_specs = [pl.BlockSpec(block, lambda i, j, k: index(i, j))]
        out_shape = [jax.ShapeDtypeStruct(target.shape, target.dtype)]
        if not isinstance(target, jax.ShapeDtypeStruct):
            aliases = {len(operands): 0}
            operands.append(target)
            in_specs.append(ANY)
            n_after += 1

    outs = pl.pallas_call(
        body, name=name, grid=(M // tm, N // tn, nk),
        in_specs=in_specs, out_specs=out_specs, out_shape=out_shape,
        scratch_shapes=[pltpu.VMEM((tm, tn) if nk > 1 else (8, 128), F32)], input_output_aliases=aliases,
        compiler_params=_cparams(("arbitrary",) * 3 if sums else ("parallel", "parallel", "arbitrary")),
    )(*operands)
    return outs[0] if len(outs) == 1 else outs


def _row(arr, tm, col_block=None, width=None):
    width = arr.shape[1] if width is None else width
    cb = 0 if col_block is None else col_block
    return arr, pl.BlockSpec((tm, width), lambda i: (i, cb))


def _full(arr):
    nd = arr.ndim
    return arr, pl.BlockSpec(arr.shape, lambda i: (0,) * nd)


def _rowwise(name, body, T, tm, ins, outs, sums=(), after=()):
    n_in, n_out, n_sum, n_after = len(ins), len(outs), len(sums), len(after)

    def kern(*refs):
        in_refs, refs = refs[:n_in], refs[n_in + n_after:]
        out_refs, sum_refs = refs[:n_out], refs[n_out:]
        res = body(*[r[...] for r in in_refs])
        res = res if isinstance(res, tuple) else (res,)
        for o_ref, val in zip(out_refs, res[:n_out], strict=True):
            o_ref[...] = val.astype(o_ref.dtype)
        if n_sum:
            @pl.when(pl.program_id(0) == 0)
            def _():
                for s_ref in sum_refs:
                    s_ref[...] = jnp.zeros_like(s_ref)

            for s_ref, val in zip(sum_refs, res[n_out:], strict=True):
                s_ref[...] += val

    res = pl.pallas_call(
        kern, name=name, grid=(T // tm,),
        in_specs=[spec for _, spec in ins] + [ANY] * n_after,
        out_specs=[pl.BlockSpec((tm, c), lambda i: (i, 0)) for c, _ in outs]
        + [pl.BlockSpec((1, c), lambda i: (0, 0)) for c in sums],
        out_shape=[jax.ShapeDtypeStruct((T, c), dt) for c, dt in outs]
        + [jax.ShapeDtypeStruct((1, c), F32) for c in sums],
        compiler_params=_cparams(("arbitrary",)),
    )(*[a for a, _ in ins], *after)
    return res[0] if len(res) == 1 else res


def _sigmoid(x):
    return 1.0 / (1.0 + jnp.exp(-x))


def _rms(h):
    return lax.rsqrt(jnp.mean(h * h, axis=-1, keepdims=True) + RMS_EPS)


def _rms_bwd(dy, h, g):
    r = _rms(h)
    n = h * r
    dn = dy * g
    dh = r * (dn - n * jnp.mean(dn * n, axis=-1, keepdims=True))
    return dh, jnp.sum(dy * n, axis=0, keepdims=True)


def _rope(x, cos2, sin_signed):
    lane = lax.broadcasted_iota(jnp.int32, x.shape, 1)
    swapped = jnp.where((lane % HEAD_DIM) < HEAD_DIM // 2, pltpu.roll(x, 128 - HEAD_DIM // 2, 1), pltpu.roll(x, HEAD_DIM // 2, 1))
    return x * cos2 + swapped * sin_signed


NA_KEYS = NA_WIN_ROWS * GRID_W
NA_BASES = 8


def _na_row_geometry(r, rows):
    first = jnp.clip(r - NA_WIN_ROWS // 2, 0, rows - NA_WIN_ROWS)
    base = first - r + (NA_WIN_ROWS - 1)
    return pl.multiple_of(first * GRID_W, GRID_W), base


NA_ROWS_PER_STEP = 8
NA_BWD_ROWS_PER_STEP = 8


def _softmax_rows(s):
    p = jnp.exp(s - jnp.max(s, axis=-1, keepdims=True))
    return p / jnp.sum(p, axis=-1, keepdims=True)


def _na_probs(q, kw, bias):
    return _softmax_rows(lax.dot_general(q, kw, (((1,), (1,)), ((), ())), preferred_element_type=F32) + bias)


def _split_pair(t):
    first = lax.broadcasted_iota(jnp.int32, t.shape, 1) < HEAD_DIM
    zero = jnp.zeros_like(t)
    return jnp.where(first, t, zero), jnp.where(first, zero, t)


def _join_pair(a, b):
    return jnp.where(lax.broadcasted_iota(jnp.int32, a.shape, 1) < HEAD_DIM, a, b)


_NT = (((1,), (1,)), ((), ()))
_TN = (((0,), (0,)), ((), ()))


def _na_fwd(qkv, tab):
    T = qkv.shape[0]
    rows = T // GRID_W
    n_pairs = NA_WIDTH // 128

    def body(q_ref, k_ref, v_ref, tab_ref, y_ref):
        def step(it, carry):
            geo = [_na_row_geometry(it * NA_ROWS_PER_STEP + u, rows) for u in range(NA_ROWS_PER_STEP)]
            q0s = [pl.multiple_of((it * NA_ROWS_PER_STEP + u) * GRID_W, GRID_W) for u in range(NA_ROWS_PER_STEP)]
            ss = [lax.dot_general(jnp.concatenate(_split_pair(q_ref[pl.ds(q0, GRID_W), :] * Q_SCALE), axis=0),
                                  k_ref[pl.ds(k0, NA_KEYS), :], _NT, preferred_element_type=F32)
                  for q0, (k0, _) in zip(q0s, geo)]
            ps = [_softmax_rows(s + jnp.concatenate([tab_ref[0, base], tab_ref[1, base]], axis=0)) for s, (_, base) in zip(ss, geo)]
            ys = [jnp.dot(p.astype(BF), v_ref[pl.ds(k0, NA_KEYS), :], preferred_element_type=F32) for p, (k0, _) in zip(ps, geo)]
            for q0, y2 in zip(q0s, ys):
                y_ref[pl.ds(q0, GRID_W), :] = _join_pair(y2[:GRID_W], y2[GRID_W:]).astype(y_ref.dtype)
            return carry

        lax.fori_loop(0, rows // NA_ROWS_PER_STEP, step, 0)

    def cols(first):
        return pl.BlockSpec((T, 128), lambda j: (0, first + j))

    return pl.pallas_call(
        body, name="na_fwd", grid=(n_pairs,),
        in_specs=[cols(0), cols(n_pairs), cols(2 * n_pairs), pl.BlockSpec((2, NA_BASES, GRID_W, NA_KEYS), lambda j: (j, 0, 0, 0))],
        out_specs=cols(0), out_shape=jax.ShapeDtypeStruct((T, NA_WIDTH), BF),
        compiler_params=_cparams(("parallel",)),
    )(qkv, qkv, qkv, tab)


def _na_bwd(qkv, tab, do):
    T = qkv.shape[0]
    rows = T // GRID_W
    n_pairs = NA_WIDTH // 128

    def body(q_ref, k_ref, v_ref, tab_ref, do_ref, dq_ref, dk_ref, dv_ref, dtab_ref):
        dk_ref[...] = jnp.zeros_like(dk_ref)
        dv_ref[...] = jnp.zeros_like(dv_ref)
        dtab_ref[...] = jnp.zeros_like(dtab_ref)

        def step(it, carry):
            U = NA_BWD_ROWS_PER_STEP
            geo = [_na_row_geometry(it * U + u, rows) for u in range(U)]
            q0s = [pl.multiple_of((it * U + u) * GRID_W, GRID_W) for u in range(U)]
            q2s = [jnp.concatenate(_split_pair(q_ref[pl.ds(q0, GRID_W), :] * Q_SCALE), axis=0) for q0 in q0s]
            do2s = [jnp.concatenate(_split_pair(do_ref[pl.ds(q0, GRID_W), :]), axis=0) for q0 in q0s]
            ss = [lax.dot_general(q2, k_ref[pl.ds(k0, NA_KEYS), :], _NT, preferred_element_type=F32) for q2, (k0, _) in zip(q2s, geo)]
            dps = [lax.dot_general(do2, v_ref[pl.ds(k0, NA_KEYS), :], _NT, preferred_element_type=F32) for do2, (k0, _) in zip(do2s, geo)]
            ps = [_softmax_rows(s + jnp.concatenate([tab_ref[0, base], tab_ref[1, base]], axis=0)) for s, (_, base) in zip(ss, geo)]
            dss = [p * (dp - jnp.sum(dp * p, axis=-1, keepdims=True)) for p, dp in zip(ps, dps)]
            dvs = [lax.dot_general(p.astype(BF), do2, _TN, preferred_element_type=F32) for p, do2 in zip(ps, do2s)]
            dsbs = [ds.astype(BF) for ds in dss]
            dqs = [jnp.dot(dsb, k_ref[pl.ds(k0, NA_KEYS), :], preferred_element_type=F32) for dsb, (k0, _) in zip(dsbs, geo)]
            dks = [lax.dot_general(dsb, q2, _TN, preferred_element_type=F32) for dsb, q2 in zip(dsbs, q2s)]
            for u in range(U):
                k0, base = geo[u]
                dtab_ref[0, base] += dss[u][:GRID_W]
                dtab_ref[1, base] += dss[u][GRID_W:]
                dq_ref[pl.ds(q0s[u], GRID_W), :] = _join_pair(dqs[u][:GRID_W], dqs[u][GRID_W:])
                dk_ref[pl.ds(k0, NA_KEYS), :] += dks[u]
                dv_ref[pl.ds(k0, NA_KEYS), :] += dvs[u]
            return carry

        lax.fori_loop(0, rows // NA_BWD_ROWS_PER_STEP, step, 0)

    def cols(first):
        return pl.BlockSpec((T, 128), lambda j: (0, first + j))

    tabs = pl.BlockSpec((2, NA_BASES, GRID_W, NA_KEYS), lambda j: (j, 0, 0, 0))
    wide = jax.ShapeDtypeStruct((T, NA_WIDTH), F32)
    return pl.pallas_call(
        body, name="na_bwd", grid=(n_pairs,),
        in_specs=[cols(0), cols(n_pairs), cols(2 * n_pairs), tabs, cols(0)],
        out_specs=[cols(0), cols(0), cols(0), tabs],
        out_shape=[wide, wide, wide, jax.ShapeDtypeStruct((NA_HEADS, NA_BASES, GRID_W, NA_KEYS), F32)],
        compiler_params=_cparams(("parallel",)),
    )(qkv, qkv, qkv, tab, do)


def _na_bias_table(rpb):
    H, n_rows, n_cols = rpb.shape

    def body(r_ref, tab_ref):
        q = lax.broadcasted_iota(jnp.int32, (GRID_W, 128), 0)
        kc = lax.broadcasted_iota(jnp.int32, (GRID_W, 128), 1)
        first = jnp.clip(q - NA_WIN_COLS // 2, 0, GRID_W - NA_WIN_COLS)
        valid = (kc >= first) & (kc < first + NA_WIN_COLS)
        toeplitz = []
        for ro in range(n_rows):
            row = jnp.broadcast_to(r_ref[pl.ds(ro, 1), :], (GRID_W, 128))
            shifted = pltpu.roll(pltpu.roll(row, 128 - (NA_WIN_COLS - 1), 1), 0, 1, stride=1, stride_axis=0)
            toeplitz.append(jnp.where(valid, shifted, NEG_INF))
        for base in range(NA_BASES):
            for j in range(NA_WIN_ROWS // 2):
                even, odd = toeplitz[base + 2 * j], toeplitz[base + 2 * j + 1]
                tab_ref[base, :, pl.ds(j * 128, 128)] = jnp.where(kc < GRID_W, even, pltpu.roll(odd, GRID_W, 1))

    padded = jnp.pad(rpb, ((0, 0), (0, 16 - n_rows), (0, 128 - n_cols)))
    return pl.pallas_call(
        body, name="na_bias_table", grid=(H,),
        in_specs=[pl.BlockSpec((None, 16, 128), lambda h: (h, 0, 0))],
        out_specs=pl.BlockSpec((None, NA_BASES, GRID_W, NA_KEYS), lambda h: (h, 0, 0, 0)),
        out_shape=jax.ShapeDtypeStruct((H, NA_BASES, GRID_W, NA_KEYS), F32),
        compiler_params=_cparams(("parallel",)),
    )(padded)


def _na_rpb_grad(dtab):
    H = dtab.shape[0]
    n_rows = 2 * NA_WIN_ROWS - 1
    n_cols = 2 * NA_WIN_COLS - 1

    def body(d_ref, o_ref):
        lane = lax.broadcasted_iota(jnp.int32, (GRID_W, 128), 1)
        low = lane < GRID_W
        out_rows = []
        for ro in range(n_rows):
            acc = jnp.zeros((GRID_W, 128), F32)
            for base in range(NA_BASES):
                i = ro - base
                if not 0 <= i < NA_WIN_ROWS:
                    continue
                pair = d_ref[base, :, pl.ds((i // 2) * 128, 128)]
                if i % 2:
                    pair = pltpu.roll(pair, GRID_W, 1)
                acc = acc + jnp.where(low, pair, 0.0)
            skew = pltpu.roll(acc, 0, 1, stride=1, stride_axis=0)
            diag = jnp.sum(skew, axis=0, keepdims=True)
            out_rows.append(pltpu.roll(jnp.broadcast_to(diag, (8, 128)), 128 - (GRID_W - NA_WIN_COLS), 1)[:1])
        out_rows.append(jnp.zeros((1, 128), F32))
        res = jnp.concatenate(out_rows, axis=0)
        o_ref[...] = jnp.where(lax.broadcasted_iota(jnp.int32, res.shape, 1) < n_cols, res, 0.0)

    return pl.pallas_call(
        body, name="na_rpb_grad", grid=(H,),
        in_specs=[pl.BlockSpec((None, NA_BASES, GRID_W, NA_KEYS), lambda h: (h, 0, 0, 0))],
        out_specs=pl.BlockSpec((None, n_rows + 1, 128), lambda h: (h, 0, 0)),
        out_shape=jax.ShapeDtypeStruct((H, n_rows + 1, 128), F32),
        compiler_params=_cparams(("parallel",)),
    )(jnp.flip(dtab, axis=2))


BAND_Q = 128
BAND_KEYS = BAND_Q + 2 * DIL_RADIUS


def _band_geometry(n, L):
    q0 = pl.multiple_of(n * BAND_Q, BAND_Q)
    k0 = pl.multiple_of(jnp.clip(q0 - DIL_RADIUS, 0, L - BAND_KEYS), DIL_RADIUS)
    qi = q0 + lax.broadcasted_iota(jnp.int32, (BAND_Q, BAND_KEYS), 0)
    kj = k0 + lax.broadcasted_iota(jnp.int32, (BAND_Q, BAND_KEYS), 1)
    return q0, k0, jnp.abs(qi - kj) <= DIL_RADIUS


DIL_PAIRS = DIL_OUT_WIDTH // 128


def _residue_shape(dil, T, dtype):
    return jax.ShapeDtypeStruct((DIL_PAIRS, dil, T // dil, 128), dtype)


def _residue_tile(dil, tm):
    return pl.BlockSpec((DIL_PAIRS, dil, tm // dil, 128), lambda i: (0, 0, i, 0))


def _to_natural(ref, scratch, dil, tm):
    tiles = []
    for pair in range(DIL_PAIRS):
        if dil == 1:
            tiles.append(ref[pair, 0].astype(F32))
            continue
        for r in range(dil):
            scratch[pl.ds(r, tm // dil, stride=dil), :] = ref[pair, r].astype(F32)
        tiles.append(scratch[...])
    return tiles


def _from_natural(tile, scratch, ref, pair, dil, tm):
    if dil == 1:
        ref[pair, 0] = tile.astype(ref.dtype)
        return
    scratch[...] = tile
    for r in range(dil):
        ref[pair, r] = scratch[pl.ds(r, tm // dil, stride=dil), :].astype(ref.dtype)


def _band_specs(group, T):
    dil = DIL_GROUPS[group][1]
    L = T // dil
    assert L % BAND_Q == 0 and L >= BAND_KEYS, (T, dil)
    return L, (dil * DIL_PAIRS,), pl.BlockSpec((None, None, L, 128), lambda s: (s % DIL_PAIRS, s // DIL_PAIRS, 0, 0))


BAND_BLOCKS_PER_STEP = 4


def _band_softmax(s, valid):
    s = jnp.where(valid, s, NEG_INF)
    m = jnp.max(s, axis=-1, keepdims=True)
    p = jnp.exp(s - m)
    l = jnp.sum(p, axis=-1, keepdims=True)
    return p / l, m + jnp.log(l)


def _band_fwd(q, k, v, group):
    T = q.shape[1] * q.shape[2]
    L, grid, spec = _band_specs(group, T)
    U = min(BAND_BLOCKS_PER_STEP, L // BAND_Q)

    def body(q_ref, k_ref, v_ref, o_ref, lse_ref):
        def step(it, carry):
            geo = [_band_geometry(it * U + u, L) for u in range(U)]
            ss = [lax.dot_general(jnp.concatenate(_split_pair(q_ref[pl.ds(q0, BAND_Q), :]), axis=0),
                                  k_ref[pl.ds(k0, BAND_KEYS), :], _NT, preferred_element_type=F32) for q0, k0, _ in geo]
            pls = [_band_softmax(s, jnp.concatenate([valid, valid], axis=0)) for s, (_, _, valid) in zip(ss, geo)]
            os = [jnp.dot(p.astype(BF), v_ref[pl.ds(k0, BAND_KEYS), :], preferred_element_type=F32) for (p, _), (_, k0, _) in zip(pls, geo)]
            for (q0, _, _), o2, (_, lse) in zip(geo, os, pls):
                o_ref[pl.ds(q0, BAND_Q), :] = _join_pair(o2[:BAND_Q], o2[BAND_Q:])
                lse2 = jnp.broadcast_to(lse, (2 * BAND_Q, 128))
                lse_ref[pl.ds(q0, BAND_Q), :] = _join_pair(lse2[:BAND_Q], lse2[BAND_Q:])
            return carry

        lax.fori_loop(0, L // (BAND_Q * U), step, 0)

    res = _residue_shape(DIL_GROUPS[group][1], T, F32)
    return pl.pallas_call(
        body, name=f"band_fwd_g{group}", grid=grid,
        in_specs=[spec] * 3, out_specs=[spec] * 2, out_shape=[res, res],
        compiler_params=_cparams(("parallel",)),
    )(q, k, v)


def _band_bwd(q, k, v, do, dlse, group):
    T = q.shape[1] * q.shape[2]
    L, grid, spec = _band_specs(group, T)
    U = min(BAND_BLOCKS_PER_STEP, L // BAND_Q)

    def body(q_ref, k_ref, v_ref, do_ref, dlse_ref, dq_ref, dk_ref, dv_ref):
        dk_ref[...] = jnp.zeros_like(dk_ref)
        dv_ref[...] = jnp.zeros_like(dv_ref)

        def step(it, carry):
            geo = [_band_geometry(it * U + u, L) for u in range(U)]
            q2s = [jnp.concatenate(_split_pair(q_ref[pl.ds(q0, BAND_Q), :]), axis=0) for q0, _, _ in geo]
            do2s = [jnp.concatenate(_split_pair(do_ref[pl.ds(q0, BAND_Q), :]), axis=0) for q0, _, _ in geo]
            ss = [lax.dot_general(q2, k_ref[pl.ds(k0, BAND_KEYS), :], _NT, preferred_element_type=F32) for q2, (_, k0, _) in zip(q2s, geo)]
            dps = [lax.dot_general(do2, v_ref[pl.ds(k0, BAND_KEYS), :], _NT, preferred_element_type=F32) for do2, (_, k0, _) in zip(do2s, geo)]
            ps = [_band_softmax(s, jnp.concatenate([valid, valid], axis=0))[0] for s, (_, _, valid) in zip(ss, geo)]
            dss = []
            for p, dp, (q0, _, _) in zip(ps, dps, geo):
                dl = dlse_ref[pl.ds(q0, BAND_Q), :]
                dl2 = jnp.concatenate([dl[:, :1], dl[:, HEAD_DIM:HEAD_DIM + 1]], axis=0)
                dss.append(p * (dp - jnp.sum(dp * p, axis=-1, keepdims=True) + dl2))
            dvs = [lax.dot_general(p.astype(BF), do2, _TN, preferred_element_type=F32) for p, do2 in zip(ps, do2s)]
            dsbs = [ds.astype(BF) for ds in dss]
            dqs = [jnp.dot(dsb, k_ref[pl.ds(k0, BAND_KEYS), :], preferred_element_type=F32) for dsb, (_, k0, _) in zip(dsbs, geo)]
            dks = [lax.dot_general(dsb, q2, _TN, preferred_element_type=F32) for dsb, q2 in zip(dsbs, q2s)]
            for u, (q0, k0, _) in enumerate(geo):
                dq_ref[pl.ds(q0, BAND_Q), :] = _join_pair(dqs[u][:BAND_Q], dqs[u][BAND_Q:])
                dk_ref[pl.ds(k0, BAND_KEYS), :] += dks[u]
                dv_ref[pl.ds(k0, BAND_KEYS), :] += dvs[u]
            return carry

        lax.fori_loop(0, L // (BAND_Q * U), step, 0)

    res = _residue_shape(DIL_GROUPS[group][1], T, F32)
    return pl.pallas_call(
        body, name=f"band_bwd_g{group}", grid=grid,
        in_specs=[spec] * 5, out_specs=[spec] * 3, out_shape=[res] * 3,
        compiler_params=_cparams(("parallel",)),
    )(q, k, v, do, dlse)


def _head_sums(t):
    head = lax.broadcasted_iota(jnp.int32, t.shape, 1) // HEAD_DIM
    out = jnp.zeros_like(t)
    for h in range(t.shape[1] // HEAD_DIM):
        mine = head == h
        out = jnp.where(mine, jnp.sum(jnp.where(mine, t, 0.0), axis=-1, keepdims=True), out)
    return out


def _dil_merge_fwd(os, lses, T, tm):
    G = len(DIL_GROUPS)
    W = DIL_OUT_WIDTH
    dils = [d for _, d in DIL_GROUPS]

    def body(*refs):
        o_refs, lse_refs = refs[:G], refs[G:2 * G]
        y_ref, w_refs, on_refs, scratch = refs[2 * G], refs[2 * G + 1:3 * G + 1], refs[3 * G + 1:4 * G + 1], refs[-1]
        o = [jnp.concatenate(_to_natural(r, scratch, d, tm), axis=1) for r, d in zip(o_refs, dils)]
        ls = [jnp.concatenate(_to_natural(r, scratch, d, tm), axis=1) for r, d in zip(lse_refs, dils)]
        m = functools.reduce(jnp.maximum, ls)
        es = [jnp.exp(l - m) for l in ls]
        tot = functools.reduce(jnp.add, es)
        ws = [e / tot for e in es]
        y_ref[...] = functools.reduce(jnp.add, [w * t for w, t in zip(ws, o)]).astype(y_ref.dtype)
        for g in range(G):
            w_refs[g][...] = ws[g]
            on_refs[g][...] = o[g]

    nat = pl.BlockSpec((tm, W), lambda i: (i, 0))
    res = pl.pallas_call(
        body, name="dil_merge_fwd", grid=(T // tm,),
        in_specs=[_residue_tile(d, tm) for d in dils] * 2,
        out_specs=[nat] * (2 * G + 1),
        out_shape=[jax.ShapeDtypeStruct((T, W), BF)] + [jax.ShapeDtypeStruct((T, W), F32)] * (2 * G),
        scratch_shapes=[pltpu.VMEM((tm, 128), F32)],
        compiler_params=_cparams(("parallel",)),
    )(*os, *lses)
    return res[0], res[1:G + 1], res[G + 1:]


def _dil_merge_bwd(dy, os, ws, tm, after=()):
    G = len(DIL_GROUPS)
    T, W = dy.shape
    dils = [d for _, d in DIL_GROUPS]
    n_after = len(after)

    def body(*refs):
        dyt = refs[0][...]
        o, w = [r[...] for r in refs[1:G + 1]], [r[...] for r in refs[G + 1:2 * G + 1]]
        refs = refs[2 * G + 1 + n_after:]
        do_refs, dlse_refs, scratch = refs[:G], refs[G:2 * G], refs[-1]
        dws = [_head_sums(dyt * t) for t in o]
        mean = functools.reduce(jnp.add, [a * b for a, b in zip(w, dws)])
        for g, d in enumerate(dils):
            do, dlse = w[g] * dyt, w[g] * (dws[g] - mean)
            for pair in range(DIL_PAIRS):
                cols = slice(pair * 128, (pair + 1) * 128)
                _from_natural(do[:, cols], scratch, do_refs[g], pair, d, tm)
                _from_natural(dlse[:, cols], scratch, dlse_refs[g], pair, d, tm)

    nat = pl.BlockSpec((tm, W), lambda i: (i, 0))
    res = pl.pallas_call(
        body, name="dil_merge_bwd", grid=(T // tm,),
        in_specs=[nat] * (2 * G + 1) + [ANY] * n_after,
        out_specs=[_residue_tile(d, tm) for d in dils] * 2,
        out_shape=[_residue_shape(d, T, BF) for d in dils] + [_residue_shape(d, T, F32) for d in dils],
        scratch_shapes=[pltpu.VMEM((tm, 128), F32)],
        compiler_params=_cparams(("parallel",)),
    )(dy, *os, *ws, *after)
    return res[:G], res[G:]


def _qkv_prep(z, cos2, sin_signed, tm):
    T = z.shape[0]
    G = len(DIL_GROUPS)
    dils = [d for _, d in DIL_GROUPS]
    n_dil_blocks = 3 * DIL_WIDTH // 128

    def body(*refs):
        blocks = refs[:n_dil_blocks]
        cos_ref, sin_ref = refs[n_dil_blocks], refs[1 + n_dil_blocks]
        outs = refs[2 + n_dil_blocks:]
        for part in range(3):
            for g, d in enumerate(dils):
                out = outs[g * 3 + part]
                for pair in range(DIL_PAIRS):
                    blk = blocks[part * (DIL_WIDTH // 128) + g * DIL_PAIRS + pair]
                    for r in range(d):
                        rows = pl.ds(r, tm // d, stride=d) if d > 1 else slice(None)
                        x = blk[rows, :]
                        if part < 2:
                            x = _rope(x, cos_ref[rows, :], sin_ref[rows, :])
                        if part == 0:
                            x = x * Q_SCALE
                        out[pair, r] = x.astype(out.dtype)

    lane_block = [pl.BlockSpec((tm, 128), functools.partial(lambda b, i: (i, b), b)) for b in range(n_dil_blocks)]
    tab = pl.BlockSpec((tm, 128), lambda i: (i, 0))
    res = pl.pallas_call(
        body, name="qkv_prep", grid=(T // tm,),
        in_specs=lane_block + [tab, tab],
        out_specs=[_residue_tile(d, tm) for d in dils for _ in range(3)],
        out_shape=[_residue_shape(d, T, BF) for d in dils for _ in range(3)],
        compiler_params=_cparams(("parallel",)),
    )(*[z] * n_dil_blocks, cos2, sin_signed)
    return [res[3 * g:3 + 3 * g] for g in range(G)]


def _qkv_unprep(d_na, d_dil, cos2, sin_signed, tm, after=()):
    T = d_na[0].shape[0]
    G = len(DIL_GROUPS)
    dils = [d for _, d in DIL_GROUPS]
    n_after = len(after)

    def body(*refs):
        dq, dk, dv = (r[...] for r in refs[:3])
        res_refs = refs[3:3 + 3 * G]
        cs, sn = refs[3 + 3 * G][...], refs[4 + 3 * G][...]
        out, scratch = refs[5 + 3 * G + n_after], refs[-1]
        cols = [dq * Q_SCALE, dk, dv]
        for part in range(3):
            for g, d in enumerate(dils):
                for x in _to_natural(res_refs[g * 3 + part], scratch, d, tm):
                    if part < 2:
                        x = _rope(x, cs, -sn)
                    cols.append(x * Q_SCALE if part == 0 else x)
        out[...] = jnp.concatenate(cols, axis=1).astype(out.dtype)

    wide = pl.BlockSpec((tm, NA_WIDTH), lambda i: (i, 0))
    tab = pl.BlockSpec((tm, 128), lambda i: (i, 0))
    return pl.pallas_call(
        body, name="qkv_unprep", grid=(T // tm,),
        in_specs=[wide] * 3 + [_residue_tile(d, tm) for d in dils for _ in range(3)] + [tab, tab] + [ANY] * n_after,
        out_specs=pl.BlockSpec((tm, QKV_WIDTH), lambda i: (i, 0)),
        out_shape=jax.ShapeDtypeStruct((T, QKV_WIDTH), BF),
        scratch_shapes=[pltpu.VMEM((tm, 128), F32)],
        compiler_params=_cparams(("parallel",)),
    )(*d_na, *[t for g in range(G) for t in d_dil[g]], cos2, sin_signed, *after)


def _rope_tables(positions):
    half = HEAD_DIM // 2
    inv_freq = ROPE_THETA ** (-jnp.arange(half, dtype=F32) / half)
    ang = positions.astype(F32)[:, None] * inv_freq
    cos, sin = jnp.cos(ang), jnp.sin(ang)
    return jnp.tile(jnp.concatenate([cos, cos], axis=1), (1, 2)), jnp.tile(jnp.concatenate([-sin, sin], axis=1), (1, 2))


def _pack_rows(t):
    return t.reshape(-1, PACK_W)


def _me():
    return lax.axis_index("x"), lax.axis_index("y"), lax.axis_index("c")


def _other_chips(x, y):
    return [(1 - x, y), (x, 1 - y), (1 - x, 1 - y)]


def _gather_weights(packed):
    R, W = packed.shape
    half = R // 2

    def body(in_ref, out_ref, send_sems, recv_sems):
        x, y, c = _me()
        sibling = (x, y, 1 - c)
        chips = _other_chips(x, y)

        def block(chip, core):
            return out_ref.at[2 * chip[0] + chip[1], pl.ds(core * half, half), :]

        def copy(k, chip, core, to, src=None):
            return pltpu.make_async_remote_copy(
                src_ref=block(chip, core) if src is None else src, dst_ref=block(chip, core),
                send_sem=send_sems.at[k], recv_sem=recv_sems.at[k], device_id=to, device_id_type=MESH)

        first = [copy(j, (x, y), c, (*chip, c), src=in_ref.at[pl.ds(c * half, half), :]) for j, chip in enumerate(chips)]
        for cp in first:
            cp.start()
        passed = [copy(3 + j, chip, c, sibling) for j, chip in enumerate(chips)]
        for j, chip in enumerate(chips):
            copy(j, chip, c, (x, y, c)).wait_recv()
            passed[j].start()
        for j, chip in enumerate(chips):
            copy(3 + j, chip, 1 - c, (x, y, c)).wait_recv()
        for cp in first + passed:
            cp.wait_send()

    others = pl.pallas_call(
        body, name="gather_weights",
        in_specs=[ANY], out_specs=ANY,
        out_shape=jax.ShapeDtypeStruct((N_CHIPS, R, W), packed.dtype),
        scratch_shapes=[pltpu.SemaphoreType.DMA((6,)), pltpu.SemaphoreType.DMA((6,))],
    )(packed)
    return lax.dynamic_update_slice(others, packed[None], (2 * lax.axis_index("x") + lax.axis_index("y"), 0, 0))


def _pair_sum(g, got, tm):
    S, R, W = g.shape
    half = R // 2
    nb = half // tm

    def body(c_ref, g_ref, got_ref, o_ref, ob_ref):
        tot = g_ref[...] + got_ref[...]
        o_ref[...] = tot
        ob_ref[...] = tot.astype(ob_ref.dtype)

    tile = pl.BlockSpec((None, tm, W), lambda s, i, c_ref: (s, i, 0))
    return pl.pallas_call(
        body, name="pair_sum",
        grid_spec=pltpu.PrefetchScalarGridSpec(
            num_scalar_prefetch=1, grid=(S, nb),
            in_specs=[pl.BlockSpec((None, tm, W), lambda s, i, c_ref: (s, c_ref[0] * nb + i, 0)), tile],
            out_specs=[tile, tile]),
        out_shape=[jax.ShapeDtypeStruct((S, half, W), F32), jax.ShapeDtypeStruct((S, half, W), BF)],
        compiler_params=_cparams(("parallel", "parallel")),
    )(lax.axis_index("c").reshape(1).astype(jnp.int32), g, got)


def _chip_sum(own, others, tm):
    n, h, W = others.shape
    nb = h // tm

    def body(c_ref, own_ref, p_ref, o_ref):
        o_ref[...] = ((own_ref[...] + p_ref[0].astype(F32)) + p_ref[1].astype(F32)) + p_ref[2].astype(F32)

    return pl.pallas_call(
        body, name="chip_sum",
        grid_spec=pltpu.PrefetchScalarGridSpec(
            num_scalar_prefetch=1, grid=(nb,),
            in_specs=[pl.BlockSpec((tm, W), lambda i, c_ref: (i, 0)), pl.BlockSpec((n, tm, W), lambda i, c_ref: (0, i, 0))],
            out_specs=pl.BlockSpec((tm, W), lambda i, c_ref: (c_ref[0] * nb + i, 0))),
        out_shape=jax.ShapeDtypeStruct((2 * h, W), F32),
        compiler_params=_cparams(("parallel",)),
    )(lax.axis_index("c").reshape(1).astype(jnp.int32), own, others)


def _join_halves(shard):
    h = shard.shape[0] // 2

    def body(in_ref, out_ref, send_sem, recv_sem):
        x, y, c = _me()
        cp = pltpu.make_async_remote_copy(
            src_ref=in_ref.at[pl.ds(c * h, h), :], dst_ref=out_ref.at[pl.ds(c * h, h), :],
            send_sem=send_sem, recv_sem=recv_sem, device_id=(x, y, 1 - c), device_id_type=MESH)
        cp.start()
        pltpu.make_async_remote_copy(
            src_ref=in_ref.at[pl.ds(c * h, h), :], dst_ref=out_ref.at[pl.ds((1 - c) * h, h), :],
            send_sem=send_sem, recv_sem=recv_sem, device_id=(x, y, 1 - c), device_id_type=MESH).wait_recv()
        cp.wait_send()

    return pl.pallas_call(
        body, name="join_halves", in_specs=[ANY], out_specs=ANY,
        out_shape=jax.ShapeDtypeStruct(shard.shape, shard.dtype), input_output_aliases={0: 0},
        scratch_shapes=[pltpu.SemaphoreType.DMA, pltpu.SemaphoreType.DMA],
    )(shard)


def _allreduce_small(s, after=()):
    R, W = s.shape
    n_after = len(after)

    def body(s_ref, *rest):
        o_ref, buf, send_sems, recv_sems = rest[n_after:]
        x, y, c = _me()
        me = 4 * x + 2 * y + c
        buf[me] = s_ref[...]
        peers = [((x + fx) % 2, (y + fy) % 2, (c + fc) % 2) for fx in range(2) for fy in range(2) for fc in range(2)][1:]
        sends = [pltpu.make_async_remote_copy(
            src_ref=s_ref, dst_ref=buf.at[me], send_sem=send_sems.at[k], recv_sem=recv_sems.at[k],
            device_id=peer, device_id_type=MESH) for k, peer in enumerate(peers)]
        for cp in sends:
            cp.start()
        for k, peer in enumerate(peers):
            pltpu.make_async_remote_copy(
                src_ref=s_ref, dst_ref=buf.at[4 * peer[0] + 2 * peer[1] + peer[2]], send_sem=send_sems.at[k],
                recv_sem=recv_sems.at[k], device_id=peer, device_id_type=MESH).wait_recv()
        for cp in sends:
            cp.wait_send()
        total = buf[0]
        for d in range(1, N_DEV):
            total = total + buf[d]
        o_ref[...] = total

    return pl.pallas_call(
        body, name="allreduce_small",
        in_specs=[pl.BlockSpec(memory_space=pltpu.VMEM)] + [ANY] * n_after, out_specs=pl.BlockSpec(memory_space=pltpu.VMEM),
        out_shape=jax.ShapeDtypeStruct((R, W), F32),
        scratch_shapes=[pltpu.VMEM((N_DEV, R, W), F32), pltpu.SemaphoreType.DMA((N_DEV - 1,)), pltpu.SemaphoreType.DMA((N_DEV - 1,))],
    )(s, *after)


HBM_SPEC = pl.BlockSpec(memory_space=pltpu.HBM)
SEM_SPEC = pl.BlockSpec(memory_space=pltpu.SEMAPHORE)
DATAFLOW = pltpu.SideEffectType.DATAFLOW_SIDE_EFFECTING


class _InFlight(NamedTuple):
    sems: tuple
    src: jax.Array
    land: jax.Array
    token: jax.Array


def _split_start(name, src, land_shape, land_dtype, n, copies, after=()):
    n_after = len(after)

    def body(src_ref, land_ref, *rest):
        rest = rest[n_after:]
        sems, token = rest[:2 * n], rest[-1]
        for k, (s, d, peer) in enumerate(copies(src_ref, land_ref)):
            pltpu.make_async_remote_copy(src_ref=s, dst_ref=d, send_sem=sems[k], recv_sem=sems[n + k],
                                         device_id=peer, device_id_type=MESH).start()
        token[...] = jnp.zeros_like(token)

    outs = pl.pallas_call(
        body, name=name,
        out_shape=(*[pltpu.SemaphoreType.DMA(())] * (2 * n), pltpu.HBM(src.shape, src.dtype), pltpu.HBM(land_shape, land_dtype),
                   jax.ShapeDtypeStruct((8, 128), F32)),
        in_specs=(HBM_SPEC, HBM_SPEC, *[ANY] * n_after),
        out_specs=(*[SEM_SPEC] * (2 * n), HBM_SPEC, HBM_SPEC, pl.BlockSpec(memory_space=pltpu.VMEM)),
        input_output_aliases={0: 2 * n, 1: 2 * n + 1},
        compiler_params=pltpu.CompilerParams(has_side_effects=DATAFLOW),
    )(pltpu.with_memory_space_constraint(src, pltpu.HBM), pltpu.with_memory_space_constraint(lax.empty(land_shape, land_dtype), pltpu.HBM),
      *after)
    return _InFlight(tuple(outs[:2 * n]), outs[2 * n], outs[2 * n + 1], outs[2 * n + 2])


def _split_wait(name, flight, after, n, copies):
    def body(src_ref, land_ref, *rest):
        sems = rest[:2 * n]
        for k, (s, d, peer) in enumerate(copies(src_ref, land_ref)):
            cp = pltpu.make_async_remote_copy(src_ref=s, dst_ref=d, send_sem=sems[k], recv_sem=sems[n + k],
                                              device_id=peer, device_id_type=MESH)
            cp.wait_send()
            cp.wait_recv()

    return pl.pallas_call(
        body, name=name,
        out_shape=(pltpu.HBM(flight.src.shape, flight.src.dtype), pltpu.HBM(flight.land.shape, flight.land.dtype)),
        in_specs=(HBM_SPEC, HBM_SPEC, *[SEM_SPEC] * (2 * n), ANY),
        out_specs=(HBM_SPEC, HBM_SPEC), input_output_aliases={0: 0, 1: 1},
        compiler_params=pltpu.CompilerParams(has_side_effects=DATAFLOW),
    )(flight.src, flight.land, *flight.sems, after)


def _gather_copies(src_ref, land_ref):
    x, y, c = _me()
    return [(src_ref, land_ref.at[2 * x + y], (*chip, c)) for chip in _other_chips(x, y)]


def _gather_start(packed, tag, after=()):
    return _split_start(f"gather_start_{tag}", packed, (N_CHIPS, *packed.shape), packed.dtype, 3, _gather_copies, after)


def _gather_wait(flight, after, tag):
    src, others = _split_wait(f"gather_wait_{tag}", flight, after, 3, _gather_copies)
    return lax.dynamic_update_slice(others, src[None], (2 * lax.axis_index("x") + lax.axis_index("y"), 0, 0))


def _swap_copies(src_ref, land_ref):
    x, y, c = _me()
    half = land_ref.shape[1]
    return [(src_ref.at[:, pl.ds((1 - c) * half, half), :], land_ref, (x, y, 1 - c))]


def _swap_start(g, tag):
    S, R, W = g.shape
    return _split_start(f"swap_halves_start_{tag}", g, (S, R // 2, W), g.dtype, 1, _swap_copies)


def _swap_wait(flight, after, tag):
    return _split_wait(f"swap_halves_wait_{tag}", flight, after, 1, _swap_copies)


def _scatter_copies(src_ref, land_ref):
    x, y, c = _me()
    return [(src_ref.at[2 * chip[0] + chip[1]], land_ref.at[j], (*chip, c)) for j, chip in enumerate(_other_chips(x, y))]


def _scatter_start(part, tag):
    S, h, W = part.shape
    return _split_start(f"scatter_chips_start_{tag}", part, (S - 1, h, W), part.dtype, 3, _scatter_copies)


def _scatter_wait(flight, after, tag):
    return _split_wait(f"scatter_chips_wait_{tag}", flight, after, 3, _scatter_copies)[1]


def _join_copies(shard_ref, unused_ref):
    x, y, c = _me()
    h = shard_ref.shape[0] // 2
    rows = shard_ref.at[pl.ds(c * h, h), :]
    return [(rows, rows, (x, y, 1 - c))]


def _join_start(shard):
    return _split_start("join_halves_start", shard, (8, 128), shard.dtype, 1, _join_copies)


def _join_wait(flight, after):
    return _split_wait("join_halves_wait", flight, after, 1, _join_copies)[0]


def _adamw(name, g, g_row0, w, m, v):
    _, R, C = w.shape
    tm = next(cand for cand in (256, 128, 64, 32, 16, 8) if R % cand == 0)
    assert g_row0 % tm == 0 and g.shape[1] == C

    def body(g_ref, w_ref, m_ref, v_ref, go_ref, d_ref, mo_ref, vo_ref):
        gt = g_ref[...]
        mt = ADAM_B1 * m_ref[...] + (1.0 - ADAM_B1) * gt
        vt = ADAM_B2 * v_ref[...] + (1.0 - ADAM_B2) * jnp.square(gt)
        m_hat = mt / (1.0 - ADAM_B1 ** ADAM_STEP)
        v_hat = vt / (1.0 - ADAM_B2 ** ADAM_STEP)
        go_ref[...] = gt
        d_ref[...] = -ADAM_LR * (m_hat / (jnp.sqrt(v_hat) + ADAM_EPS) + ADAM_WD * w_ref[...])
        mo_ref[...] = mt
        vo_ref[...] = vt

    state = pl.BlockSpec((None, tm, C), lambda i: (0, i, 0))
    return pl.pallas_call(
        body, name=name, grid=(R // tm,),
        in_specs=[pl.BlockSpec((tm, C), lambda i: (g_row0 // tm + i, 0)), state, state, state],
        out_specs=[state] * 4, out_shape=[jax.ShapeDtypeStruct((1, R, C), F32)] * 4,
        compiler_params=_cparams(("parallel",)),
    )(g, w, m, v)


def _unpack_weights(gathered, names):
    S = gathered.shape[0]
    shard_shapes = {"w_in": (D_MODEL, (QKV_WIDTH + 2 * D_MODEL) // S), "w_branch_na": (NA_WIDTH, D_MODEL // S),
                    "w_branch_dil": (DIL_OUT_WIDTH, D_MODEL // S), "w_out": (D_MODEL // S, D_MODEL),
                    "w_up": (D_MODEL, D_FF // S), "w_down": (D_FF // S, D_MODEL),
                    "w_ple_gate": (D_MODEL // S, D_MODEL), "w_ple_proj": (PLE_DIM, D_MODEL // S)}
    col_sharded = {"w_in", "w_branch_na", "w_branch_dil", "w_up", "w_ple_proj"}
    out, r0 = {}, 0
    for name in names:
        rows, cols = shard_shapes[name]
        n = rows * cols // PACK_W
        t = gathered[:, r0:r0 + n, :].reshape(S, rows, cols)
        r0 += n
        out[name] = t.transpose(1, 0, 2).reshape(rows, S * cols) if name in col_sharded else t.reshape(S * rows, cols)
    return out


def kernel(x, p, positions, g_mix, w_in, rpb, w_branch_na, w_branch_dil, w_out, g_mlp, w_up, w_down, g_ple, w_ple_gate, w_ple_proj, g_final, loss_target, m_g_mix, m_w_in, m_rpb, m_w_branch_na, m_w_branch_dil, m_w_out, m_g_mlp, m_w_up, m_w_down, m_g_ple, m_w_ple_gate, m_w_ple_proj, m_g_final, v_g_mix, v_w_in, v_rpb, v_w_branch_na, v_w_branch_dil, v_w_out, v_g_mlp, v_w_up, v_w_down, v_g_ple, v_w_ple_gate, v_w_ple_proj, v_g_final):
    shards = {"w_in": w_in[0], "w_branch_na": w_branch_na[0], "w_branch_dil": w_branch_dil[0], "w_out": w_out[0],
              "w_up": w_up[0], "w_down": w_down[0], "w_ple_gate": w_ple_gate[0], "w_ple_proj": w_ple_proj[0]}
    params = {"w_in": w_in, "w_branch_na": w_branch_na, "w_branch_dil": w_branch_dil, "w_out": w_out, "w_up": w_up,
              "w_down": w_down, "w_ple_gate": w_ple_gate, "w_ple_proj": w_ple_proj,
              "m_w_in": m_w_in, "m_w_branch_na": m_w_branch_na, "m_w_branch_dil": m_w_branch_dil, "m_w_out": m_w_out,
              "m_w_up": m_w_up, "m_w_down": m_w_down, "m_w_ple_gate": m_w_ple_gate, "m_w_ple_proj": m_w_ple_proj,
              "v_w_in": v_w_in, "v_w_branch_na": v_w_branch_na, "v_w_branch_dil": v_w_branch_dil, "v_w_out": v_w_out,
              "v_w_up": v_w_up, "v_w_down": v_w_down, "v_w_ple_gate": v_w_ple_gate, "v_w_ple_proj": v_w_ple_proj}

    W = {"w_in": _gather_weights(shards["w_in"].astype(BF)).transpose(1, 0, 2).reshape(D_MODEL, -1)}
    mix_flight = _gather_start(jnp.concatenate([_pack_rows(shards[n].astype(BF)) for n in GATHER_MIX], axis=0), "mix")
    rest_flight = _gather_start(jnp.concatenate([_pack_rows(shards[n].astype(BF)) for n in GATHER_MLP], axis=0), "mlp",
                                after=(mix_flight.token,))
    w_qkv, w_gates = W["w_in"][:, :QKV_WIDTH], W["w_in"][:, QKV_WIDTH:]

    xs, ps, tgt = x[0], p[0, 0], loss_target[0]
    T = xs.shape[0]
    TM = 512
    gm, gl, gp, gf = g_mix, g_mlp, g_ple, g_final.reshape(1, D_MODEL)
    cos2, sin_signed = _rope_tables(positions[0])
    tab = _na_bias_table(rpb[0])

    a = _rowwise("norm_mix", lambda h, g: h * _rms(h) * g, T, TM, [_row(xs, TM), _full(gm)], [(D_MODEL, BF)],
                 after=(rest_flight.token,))
    n3 = 3 * NA_WIDTH
    qkv = _mm("in_na", a, w_qkv[:, :n3], "nn", 1024, 768, 1024, [BF])
    z_dil = _mm("in_dil", a, w_qkv[:, n3:], "nn", 1024, 1152, 1024, [F32])
    z_gates = _mm("in_gates", a, w_gates, "nn", 1024,1024, 1024, [BF])

    dil_ops = _qkv_prep(z_dil, cos2, sin_signed, TM)
    y_na = _na_fwd(qkv, tab)
    band = [_band_fwd(*dil_ops[g], g) for g in range(len(DIL_GROUPS))]
    y_dil, w_grp, o_nat = _dil_merge_fwd([b[0] for b in band], [b[1] for b in band], T, TM)

    W.update(_unpack_weights(_gather_wait(mix_flight, y_dil, "mix"), GATHER_MIX))
    u_na = _mm("branch_na", y_na, W["w_branch_na"], "nn", 1024,1024, 512, [BF])
    u_dil = _mm("branch_dil", y_dil, W["w_branch_dil"], "nn", 1024,1024, 256, [BF])
    mixed = _rowwise(
        "gate_mix", lambda gn, gd, un, ud: _sigmoid(gn.astype(F32)) * un.astype(F32) + _sigmoid(gd.astype(F32)) * ud.astype(F32), T, TM,
        [_row(z_gates, TM, 0, D_MODEL), _row(z_gates, TM, 1, D_MODEL), _row(u_na, TM), _row(u_dil, TM)], [(D_MODEL, BF)])
    def add_norm(d, h, g):
        h = h + d
        return h, h * _rms(h) * g

    h1, cn = _mm("out_proj", mixed, W["w_out"], "nn", 512, 1024, 1024, [F32, BF], epilogue=add_norm, extras=(xs,), consts=(gl,))
    W.update(_unpack_weights(_gather_wait(rest_flight, cn, "mlp"), GATHER_MLP))
    up, act = _mm("mlp_up", cn, W["w_up"], "nn", 1024,1024, 1024, [BF, BF],
                  epilogue=lambda acc: (acc, jnp.square(jnp.maximum(acc, 0.0))))
    h2, en = _mm("mlp_down", act, W["w_down"], "nn", 1024, 1024, 1024, [F32, BF], epilogue=add_norm, extras=(h1,), consts=(gp,))
    gt = _mm("ple_gate", en, W["w_ple_gate"], "nn", 1024,1024, 1024, [F32])
    pp = _mm("ple_proj", ps, W["w_ple_proj"], "nn", 1024,1024, 256, [F32])

    def head(h2t, gtt, ppt, tg, g):
        sg = _sigmoid(gtt)
        h3 = h2t + sg * ppt
        yo = h3 * _rms(h3) * g
        diff = yo - tg
        loss = 0.5 * jnp.sum(jnp.mean(jnp.square(diff), axis=-1, keepdims=True), axis=0, keepdims=True)
        dh3, dg = _rms_bwd(diff * (1.0 / D_MODEL), h3, g)
        return dh3, dh3 * ppt * sg * (1.0 - sg), dh3 * sg, jnp.broadcast_to(loss, (1, 128)), dg

    dh3, d_gt, d_pp, loss_part, dg_final = _rowwise(
        "loss_head", head, T, TM, [_row(h2, TM), _row(gt, TM), _row(pp, TM), _row(tgt, TM), _full(gf)],
        [(D_MODEL, F32), (D_MODEL, BF), (D_MODEL, BF)], sums=[128, D_MODEL])

    early_shapes = {n: shards[n].shape for n in REDUCE_EARLY}
    early_rows = sum(r * c for r, c in early_shapes.values()) // PACK_W
    shard_rows = D_MODEL // N_CHIPS
    early_buf = _mm("g_ple_gate", en, d_gt, "tn", 1024, 1024, 1024, [F32],
                    into=(jax.ShapeDtypeStruct((N_CHIPS, early_rows, PACK_W), F32), (N_CHIPS, shard_rows, PACK_W),
                          lambda i, j: (0, 2 * D_MODEL // shard_rows, 0)))
    g_ple_proj = _mm("g_ple_proj", ps, d_pp, "tn", 256, 1024, 1024,[F32])

    def add_norm_bwd(dn, dh_out, h, g):
        dh, dg = _rms_bwd(dn, h, g)
        dh = dh_out + dh
        return dh, dh, dg

    dh2, dh2_b, dg_ple = _mm("d_ple_gate", d_gt, W["w_ple_gate"], "nt", 512, 1024, 1024, [F32, BF],
                             epilogue=add_norm_bwd, extras=(dh3, h2), consts=(gp,), sums=[D_MODEL])
    d_up = _mm("d_mlp_down", dh2_b, W["w_down"], "nt", 1024,1024, 1024, [BF],
               epilogue=lambda acc, u: (acc * (2.0 * jnp.maximum(u.astype(F32), 0.0)),), extras=(up,))
    early_buf = _mm("g_mlp_down", act, dh2_b, "tn", 1024, 1024, 1024,[F32],
                    into=(early_buf, (None, D_MODEL, PACK_W), lambda i, j: (i, 1, 0)))
    early_buf = _mm("g_mlp_up", cn, d_up, "tn", 1024, 1024, 1024,[F32],
                    into=(early_buf, (None, D_MODEL, PACK_W), lambda i, j: (j, 0, 0)))
    dh1, dh1_b, dg_mlp = _mm("d_mlp_up", d_up, W["w_up"], "nt", 1024, 1024, 1024, [F32, BF], epilogue=add_norm_bwd,
                             extras=(dh2, h1), consts=(gl,), sums=[D_MODEL])
    d_mixed = _mm("d_out_proj", dh1_b, W["w_out"], "nt", 1024,1024, 1024, [F32])
    early_buf = _mm("g_out_proj", mixed, dh1_b, "tn", 1024, 1024, 1024, [F32],
                    into=(early_buf, (N_CHIPS, shard_rows, PACK_W), lambda i, j: (0, 2 * D_MODEL // shard_rows + 1, 0)))

    def gate_bwd(dm, gn, gd, un, ud):
        gn, gd, un, ud = (t.astype(F32) for t in (gn, gd, un, ud))
        sn, sd = _sigmoid(gn), _sigmoid(gd)
        return jnp.concatenate([dm * un * sn * (1.0 - sn), dm * ud * sd * (1.0 - sd)], axis=1), dm * sn, dm * sd

    dz_gates, d_u_na, d_u_dil = _rowwise(
        "gate_mix_bwd", gate_bwd, T, TM,
        [_row(d_mixed, TM), _row(z_gates, TM, 0, D_MODEL), _row(z_gates, TM, 1, D_MODEL), _row(u_na, TM), _row(u_dil, TM)],
        [(2 * D_MODEL, BF), (D_MODEL, BF), (D_MODEL, BF)])
    g_branch_na = _mm("g_branch_na", y_na, d_u_na, "tn", 1024, 1024, 1024,[F32])
    g_branch_dil = _mm("g_branch_dil", y_dil, d_u_dil, "tn", 256, 1024, 1024,[F32])
    small_rows = [jnp.concatenate([_pack_rows(g[:, s * shard_rows:(s + 1) * shard_rows]) for g in (g_ple_proj, g_branch_na, g_branch_dil)],
                                  axis=0) for s in range(N_CHIPS)]
    early_buf = lax.dynamic_update_slice(early_buf, jnp.stack(small_rows), (0, 2 * D_MODEL + 2 * shard_rows, 0))
    early_tm = early_rows // 4
    swap_flight = _swap_start(early_buf, "early")
    d_y_na = _mm("d_branch_na", d_u_na, W["w_branch_na"], "nt", 1024,512, 1024, [BF], after=(swap_flight.token,))
    d_y_dil = _mm("d_branch_dil", d_u_dil, W["w_branch_dil"], "nt", 1024,256, 1024, [F32])

    dqa, dka, dva, dtab = _na_bwd(qkv, tab, d_y_na)
    early_g, early_got = _swap_wait(swap_flight, dqa, "early")
    early_pair, early_pair_b = _pair_sum(early_g, early_got, early_tm)
    scatter_flight = _scatter_start(early_pair_b, "early")
    d_rpb = _na_rpb_grad(dtab)[:, :2 * NA_WIN_ROWS - 1, :2 * NA_WIN_COLS - 1]

    do_res, dlse_res = _dil_merge_bwd(d_y_dil, o_nat, w_grp, TM, after=(scatter_flight.token,))
    d_dil = [_band_bwd(*dil_ops[g], do_res[g], dlse_res[g], g) for g in range(len(DIL_GROUPS))]

    dz_qkv = _qkv_unprep((dqa, dka, dva), d_dil, cos2, sin_signed, TM)
    g_in = jnp.concatenate([
        _mm("g_in_qkv", a, dz_qkv, "tn", 1024, 1280, 1024,[F32]),
        _mm("g_in_gates", a, dz_gates, "tn", 1024, 1024, 1024,[F32])], axis=1)
    me_chip = 2 * lax.axis_index("x") + lax.axis_index("y")
    early_mine = _chip_sum(lax.dynamic_index_in_dim(early_pair, me_chip, 0, keepdims=False),
                           _scatter_wait(scatter_flight, g_in, "early"), early_tm)
    join_flight = _join_start(early_mine)
    in_cols = g_in.shape[1] // N_CHIPS
    late_tm = 256
    late_swap = _swap_start(jnp.stack([g_in[:, s * in_cols:(s + 1) * in_cols] for s in range(N_CHIPS)]), "late")
    d_a = _mm("d_in_qkv", dz_qkv, w_qkv, "nt", 1024,1024, 1280, [F32], after=(late_swap.token, join_flight.token))
    late_g, late_got = _swap_wait(late_swap, d_a, "late")
    late_pair, late_pair_b = _pair_sum(late_g, late_got, late_tm)
    late_scatter = _scatter_start(late_pair_b, "late")
    def first_bwd(dn_gates, dn_qkv, dh_out, h, g):
        dh, dg = _rms_bwd(dn_gates + dn_qkv, h, g)
        return dh_out + dh, dg

    grad_x, dg_mix = _mm("d_in_gates", dz_gates, w_gates, "nt", 512, 1024, 1024, [F32], epilogue=first_bwd,
                         extras=(d_a, dh1, xs), consts=(gm,), sums=[D_MODEL], after=(late_scatter.token,))
    early_shard = _join_wait(join_flight, grad_x)

    n_rpb = rpb.size
    rpb_rows = 4
    small = jnp.concatenate([
        dg_mix, dg_mlp, dg_ple, dg_final,
        jnp.pad(d_rpb.reshape(-1), (0, rpb_rows * D_MODEL - n_rpb)).reshape(rpb_rows, D_MODEL),
        jnp.pad(loss_part, ((0, 0), (0, D_MODEL - loss_part.shape[1]))),
        jnp.zeros((SMALL_ROWS - 5 - rpb_rows, D_MODEL), F32)], axis=0)
    out = {"grad": {}, "delta": {}, "new_m": {}, "new_v": {}}

    def update(n, g, row0):
        res = _adamw("adamw_" + n, g, row0, params[n], params["m_" + n], params["v_" + n])
        for kind, t in zip(("grad", "delta", "new_m", "new_v"), res, strict=True):
            out[kind][n] = t

    row0 = 0
    for n in REDUCE_EARLY:
        rows, cols = early_shapes[n]
        n_rows = rows * cols // PACK_W
        if cols == PACK_W:
            update(n, early_shard, row0)
        else:
            update(n, early_shard[row0:row0 + n_rows].reshape(rows, cols), 0)
        row0 += n_rows
    late_others = _scatter_wait(late_scatter, out["new_v"][REDUCE_EARLY[-1]], "late")
    late_mine = _chip_sum(lax.dynamic_index_in_dim(late_pair, me_chip, 0, keepdims=False), late_others, late_tm)
    small = _allreduce_small(small, after=(late_mine,))
    update("w_in", _join_halves(late_mine), 0)
    loss = small[4 + rpb_rows, 0]

    def small_pack(a0, a1, a2, a3, r):
        return jnp.concatenate([a0.reshape(1, -1), a1.reshape(1, -1), a2.reshape(1, -1), a3.reshape(1, -1),
                                jnp.pad(r.reshape(-1), (0, rpb_rows * D_MODEL - n_rpb)).reshape(rpb_rows, D_MODEL)], axis=0)

    small_res = _adamw("adamw_small", small, 0, small_pack(g_mix, g_mlp, g_ple, g_final, rpb)[None],
                       small_pack(m_g_mix, m_g_mlp, m_g_ple, m_g_final, m_rpb)[None],
                       small_pack(v_g_mix, v_g_mlp, v_g_ple, v_g_final, v_rpb)[None])

    def small_unpack(t):
        return {"g_mix": t[0].reshape(g_mix.shape), "g_mlp": t[1].reshape(g_mlp.shape), "g_ple": t[2].reshape(g_ple.shape),
                "g_final": t[3].reshape(g_final.shape), "rpb": t[4:].reshape(-1)[:n_rpb].reshape(rpb.shape)}

    for kind, t in zip(("grad", "delta", "new_m", "new_v"), small_res, strict=True):
        out[kind].update(small_unpack(t[0]))

    order = ["g_mix", "w_in", "rpb", "w_branch_na", "w_branch_dil", "w_out", "g_mlp", "w_up", "w_down", "g_ple",
             "w_ple_gate", "w_ple_proj", "g_final"]
    return (loss, grad_x[None], *[out["grad"][n] for n in order], *[out["delta"][n] for n in order],
            *[out["new_m"][n] for n in order], *[out["new_v"][n] for n in order])
```

```python
import functools
from typing import NamedTuple

import jax
import jax.numpy as jnp
from jax import lax
from jax.experimental import pallas as pl
from jax.experimental.pallas import tpu as pltpu

BF = jnp.bfloat16
F32 = jnp.float32
MESH = pl.DeviceIdType.MESH
ANY = pl.BlockSpec(memory_space=pl.ANY)

V7X_VMEM_BYTES = 64 * 1024 * 1024
VMEM_LIMIT = V7X_VMEM_BYTES - 16 * 1024 * 1024

D_MODEL = 1024
HEAD_DIM = 64
GRID_W = 64
NA_HEADS = 8
NA_WIN_ROWS = 8
NA_WIN_COLS = 16
NA_WIDTH = NA_HEADS * HEAD_DIM
DIL_GROUPS = ((128, 1), (512, 4), (2048, 16))
DIL_HPG = 4
DIL_HEADS = DIL_HPG * len(DIL_GROUPS)
DIL_WIDTH = DIL_HEADS * HEAD_DIM
DIL_OUT_WIDTH = DIL_HPG * HEAD_DIM
DIL_RADIUS = 64
QKV_WIDTH = 3 * NA_WIDTH + 3 * DIL_WIDTH
D_FF = 4 * D_MODEL
PLE_DIM = 256
ROPE_THETA = 10000.0
RMS_EPS = 1e-6
NEG_INF = -1e30
Q_SCALE = HEAD_DIM ** -0.5

ADAM_LR = 0.001
ADAM_B1 = 0.9
ADAM_B2 = 0.999
ADAM_EPS = 1e-08
ADAM_WD = 0.01
ADAM_STEP = 10

N_CHIPS = 4
N_DEV = 8
PACK_W = 1024
BIG = ("w_in", "w_branch_na", "w_branch_dil", "w_out", "w_up", "w_down", "w_ple_gate", "w_ple_proj")
GATHER_MIX = ("w_branch_na", "w_branch_dil", "w_out")
GATHER_MLP = ("w_up", "w_down", "w_ple_gate", "w_ple_proj")
REDUCE_EARLY = ("w_up", "w_down", "w_ple_gate", "w_out", "w_ple_proj", "w_branch_na", "w_branch_dil")
SMALL_ROWS = 16


def _cparams(sem=None):
    return pltpu.CompilerParams(dimension_semantics=sem, vmem_limit_bytes=VMEM_LIMIT)


def _mm(name, a, b, mode, tm, tn, tk, out_dtypes, epilogue=None, extras=(), consts=(), sums=(), after=(), into=None):
    if mode == "nn":
        (M, K), N = a.shape, b.shape[1]
    elif mode == "nt":
        (M, K), N = a.shape, b.shape[0]
    else:
        (K, M), N = a.shape, b.shape[1]
    tm, tn, tk = min(tm, M), min(tn, N), min(tk, K)
    assert M % tm == 0 and N % tn == 0 and K % tk == 0, (name, M, N, K, tm, tn, tk)
    if mode == "nn":
        a_spec = pl.BlockSpec((tm, tk), lambda i, j, k: (i, k))
        b_spec = pl.BlockSpec((tk, tn), lambda i, j, k: (k, j))
        dims = (((1,), (0,)), ((), ()))
    elif mode == "nt":
        a_spec = pl.BlockSpec((tm, tk), lambda i, j, k: (i, k))
        b_spec = pl.BlockSpec((tn, tk), lambda i, j, k: (j, k))
        dims = (((1,), (1,)), ((), ()))
    else:
        a_spec = pl.BlockSpec((tk, tm), lambda i, j, k: (k, i))
        b_spec = pl.BlockSpec((tk, tn), lambda i, j, k: (k, j))
        dims = (((0,), (0,)), ((), ()))
    nk = K // tk
    n_extra, n_const, n_out, n_sum = len(extras), len(consts), len(out_dtypes), len(sums)
    tile = pl.BlockSpec((tm, tn), lambda i, j, k: (i, j))
    assert not sums or tn == N, "row sums need whole rows in a tile"

    n_after = len(after)

    def body(a_ref, b_ref, *rest):
        extra_refs, rest = rest[:n_extra + n_const], rest[n_extra + n_const + n_after:]
        out_refs, sum_refs, acc = rest[:n_out], rest[n_out:n_out + n_sum], rest[-1]
        i, k = pl.program_id(0), pl.program_id(2)
        def product():
            return lax.dot_general(a_ref[...].astype(BF), b_ref[...].astype(BF), dims, preferred_element_type=F32)

        if nk > 1:
            @pl.when(k == 0)
            def _():
                acc[...] = jnp.zeros_like(acc)

            acc[...] += product()

        @pl.when(k == nk - 1)
        def _():
            total = product() if nk == 1 else acc[...]
            outs = (total,) if epilogue is None else epilogue(total, *[e[...] for e in extra_refs])
            for o_ref, val in zip(out_refs, outs[:n_out], strict=True):
                o_ref[...] = val.astype(o_ref.dtype).reshape(o_ref.shape)
            for s_ref, val in zip(sum_refs, outs[n_out:], strict=True):
                @pl.when(i == 0)
                def _():
                    s_ref[...] = val

                @pl.when(i != 0)
                def _():
                    s_ref[...] += val

    out_specs = [tile] * n_out + [pl.BlockSpec((1, c), lambda i, j, k: (0, 0)) for c in sums]
    out_shape = [jax.ShapeDtypeStruct((M, N), dt) for dt in out_dtypes] + [jax.ShapeDtypeStruct((1, c), F32) for c in sums]
    operands, aliases = [a, b, *extras, *consts, *after], {}
    in_specs = ([a_spec, b_spec] + [tile] * n_extra
                + [pl.BlockSpec(c.shape, functools.partial(lambda nd, i, j, k: (0,) * nd, c.ndim)) for c in consts] + [ANY] * n_after)
    if into is not None:
        assert n_out == 1
        target, block, index = into
        out_specs = [pl.BlockSpec(block, lambda i, j, k: index(i, j))]
        out_shape = [jax.ShapeDtypeStruct(target.shape, target.dtype)]
        if not isinstance(target, jax.ShapeDtypeStruct):
            aliases = {len(operands): 0}
            operands.append(target)
            in_specs.append(ANY)
            n_after += 1

    outs = pl.pallas_call(
        body, name=name, grid=(M // tm, N // tn, nk),
        in_specs=in_specs, out_specs=out_specs, out_shape=out_shape,
        scratch_shapes=[pltpu.VMEM((tm, tn) if nk > 1 else (8, 128), F32)], input_output_aliases=aliases,
        compiler_params=_cparams(("arbitrary",) * 3 if sums else ("parallel", "parallel", "arbitrary")),
    )(*operands)
    return outs[0] if len(outs) == 1 else outs


def _row(arr, tm, col_block=None, width=None):
    width = arr.shape[1] if width is None else width
    cb = 0 if col_block is None else col_block
    return arr, pl.BlockSpec((tm, width), lambda i: (i, cb))


def _full(arr):
    nd = arr.ndim
    return arr, pl.BlockSpec(arr.shape, lambda i: (0,) * nd)


def _rowwise(name, body, T, tm, ins, outs, sums=(), after=()):
    n_in, n_out, n_sum, n_after = len(ins), len(outs), len(sums), len(after)

    def kern(*refs):
        in_refs, refs = refs[:n_in], refs[n_in + n_after:]
        out_refs, sum_refs = refs[:n_out], refs[n_out:]
        res = body(*[r[...] for r in in_refs])
        res = res if isinstance(res, tuple) else (res,)
        for o_ref, val in zip(out_refs, res[:n_out], strict=True):
            o_ref[...] = val.astype(o_ref.dtype)
        if n_sum:
            @pl.when(pl.program_id(0) == 0)
            def _():
                for s_ref in sum_refs:
                    s_ref[...] = jnp.zeros_like(s_ref)

            for s_ref, val in zip(sum_refs, res[n_out:], strict=True):
                s_ref[...] += val

    res = pl.pallas_call(
        kern, name=name, grid=(T // tm,),
        in_specs=[spec for _, spec in ins] + [ANY] * n_after,
        out_specs=[pl.BlockSpec((tm, c), lambda i: (i, 0)) for c, _ in outs]
        + [pl.BlockSpec((1, c), lambda i: (0, 0)) for c in sums],
        out_shape=[jax.ShapeDtypeStruct((T, c), dt) for c, dt in outs]
        + [jax.ShapeDtypeStruct((1, c), F32) for c in sums],
        compiler_params=_cparams(("arbitrary",)),
    )(*[a for a, _ in ins], *after)
    return res[0] if len(res) == 1 else res


def _sigmoid(x):
    return 1.0 / (1.0 + jnp.exp(-x))


def _rms(h):
    return lax.rsqrt(jnp.mean(h * h, axis=-1, keepdims=True) + RMS_EPS)


def _rms_bwd(dy, h, g):
    r = _rms(h)
    n = h * r
    dn = dy * g
    dh = r * (dn - n * jnp.mean(dn * n, axis=-1, keepdims=True))
    return dh, jnp.sum(dy * n, axis=0, keepdims=True)


def _rope(x, cos2, sin_signed):
    lane = lax.broadcasted_iota(jnp.int32, x.shape, 1)
    swapped = jnp.where((lane % HEAD_DIM) < HEAD_DIM // 2, pltpu.roll(x, 128 - HEAD_DIM // 2, 1), pltpu.roll(x, HEAD_DIM // 2, 1))
    return x * cos2 + swapped * sin_signed


NA_KEYS = NA_WIN_ROWS * GRID_W
NA_BASES = 8


def _na_row_geometry(r, rows):
    first = jnp.clip(r - NA_WIN_ROWS // 2, 0, rows - NA_WIN_ROWS)
    base = first - r + (NA_WIN_ROWS - 1)
    return pl.multiple_of(first * GRID_W, GRID_W), base


NA_ROWS_PER_STEP = 8
NA_BWD_ROWS_PER_STEP = 8


def _softmax_rows(s):
    p = jnp.exp(s - jnp.max(s, axis=-1, keepdims=True))
    return p / jnp.sum(p, axis=-1, keepdims=True)


def _na_probs(q, kw, bias):
    return _softmax_rows(lax.dot_general(q, kw, (((1,), (1,)), ((), ())), preferred_element_type=F32) + bias)


def _split_pair(t):
    first = lax.broadcasted_iota(jnp.int32, t.shape, 1) < HEAD_DIM
    zero = jnp.zeros_like(t)
    return jnp.where(first, t, zero), jnp.where(first, zero, t)


def _join_pair(a, b):
    return jnp.where(lax.broadcasted_iota(jnp.int32, a.shape, 1) < HEAD_DIM, a, b)


_NT = (((1,), (1,)), ((), ()))
_TN = (((0,), (0,)), ((), ()))


def _na_fwd(qkv, tab):
    T = qkv.shape[0]
    rows = T // GRID_W
    n_pairs = NA_WIDTH // 128

    def body(q_ref, k_ref, v_ref, tab_ref, y_ref):
        def step(it, carry):
            geo = [_na_row_geometry(it * NA_ROWS_PER_STEP + u, rows) for u in range(NA_ROWS_PER_STEP)]
            q0s = [pl.multiple_of((it * NA_ROWS_PER_STEP + u) * GRID_W, GRID_W) for u in range(NA_ROWS_PER_STEP)]
            ss = [lax.dot_general(jnp.concatenate(_split_pair(q_ref[pl.ds(q0, GRID_W), :] * Q_SCALE), axis=0),
                                  k_ref[pl.ds(k0, NA_KEYS), :], _NT, preferred_element_type=F32)
                  for q0, (k0, _) in zip(q0s, geo)]
            ps = [_softmax_rows(s + jnp.concatenate([tab_ref[0, base], tab_ref[1, base]], axis=0)) for s, (_, base) in zip(ss, geo)]
            ys = [jnp.dot(p.astype(BF), v_ref[pl.ds(k0, NA_KEYS), :], preferred_element_type=F32) for p, (k0, _) in zip(ps, geo)]
            for q0, y2 in zip(q0s, ys):
                y_ref[pl.ds(q0, GRID_W), :] = _join_pair(y2[:GRID_W], y2[GRID_W:]).astype(y_ref.dtype)
            return carry

        lax.fori_loop(0, rows // NA_ROWS_PER_STEP, step, 0)

    def cols(first):
        return pl.BlockSpec((T, 128), lambda j: (0, first + j))

    return pl.pallas_call(
        body, name="na_fwd", grid=(n_pairs,),
        in_specs=[cols(0), cols(n_pairs), cols(2 * n_pairs), pl.BlockSpec((2, NA_BASES, GRID_W, NA_KEYS), lambda j: (j, 0, 0, 0))],
        out_specs=cols(0), out_shape=jax.ShapeDtypeStruct((T, NA_WIDTH), BF),
        compiler_params=_cparams(("parallel",)),
    )(qkv, qkv, qkv, tab)


def _na_bwd(qkv, tab, do):
    T = qkv.shape[0]
    rows = T // GRID_W
    n_pairs = NA_WIDTH // 128

    def body(q_ref, k_ref, v_ref, tab_ref, do_ref, dq_ref, dk_ref, dv_ref, dtab_ref):
        dk_ref[...] = jnp.zeros_like(dk_ref)
        dv_ref[...] = jnp.zeros_like(dv_ref)
        dtab_ref[...] = jnp.zeros_like(dtab_ref)

        def step(it, carry):
            U = NA_BWD_ROWS_PER_STEP
            geo = [_na_row_geometry(it * U + u, rows) for u in range(U)]
            q0s = [pl.multiple_of((it * U + u) * GRID_W, GRID_W) for u in range(U)]
            q2s = [jnp.concatenate(_split_pair(q_ref[pl.ds(q0, GRID_W), :] * Q_SCALE), axis=0) for q0 in q0s]
            do2s = [jnp.concatenate(_split_pair(do_ref[pl.ds(q0, GRID_W), :]), axis=0) for q0 in q0s]
            ss = [lax.dot_general(q2, k_ref[pl.ds(k0, NA_KEYS), :], _NT, preferred_element_type=F32) for q2, (k0, _) in zip(q2s, geo)]
            dps = [lax.dot_general(do2, v_ref[pl.ds(k0, NA_KEYS), :], _NT, preferred_element_type=F32) for do2, (k0, _) in zip(do2s, geo)]
            ps = [_softmax_rows(s + jnp.concatenate([tab_ref[0, base], tab_ref[1, base]], axis=0)) for s, (_, base) in zip(ss, geo)]
            dss = [p * (dp - jnp.sum(dp * p, axis=-1, keepdims=True)) for p, dp in zip(ps, dps)]
            dvs = [lax.dot_general(p.astype(BF), do2, _TN, preferred_element_type=F32) for p, do2 in zip(ps, do2s)]
            dsbs = [ds.astype(BF) for ds in dss]
            dqs = [jnp.dot(dsb, k_ref[pl.ds(k0, NA_KEYS), :], preferred_element_type=F32) for dsb, (k0, _) in zip(dsbs, geo)]
            dks = [lax.dot_general(dsb, q2, _TN, preferred_element_type=F32) for dsb, q2 in zip(dsbs, q2s)]
            for u in range(U):
                k0, base = geo[u]
                dtab_ref[0, base] += dss[u][:GRID_W]
                dtab_ref[1, base] += dss[u][GRID_W:]
                dq_ref[pl.ds(q0s[u], GRID_W), :] = _join_pair(dqs[u][:GRID_W], dqs[u][GRID_W:])
                dk_ref[pl.ds(k0, NA_KEYS), :] += dks[u]
                dv_ref[pl.ds(k0, NA_KEYS), :] += dvs[u]
            return carry

        lax.fori_loop(0, rows // NA_BWD_ROWS_PER_STEP, step, 0)

    def cols(first):
        return pl.BlockSpec((T, 128), lambda j: (0, first + j))

    tabs = pl.BlockSpec((2, NA_BASES, GRID_W, NA_KEYS), lambda j: (j, 0, 0, 0))
    wide = jax.ShapeDtypeStruct((T, NA_WIDTH), F32)
    return pl.pallas_call(
        body, name="na_bwd", grid=(n_pairs,),
        in_specs=[cols(0), cols(n_pairs), cols(2 * n_pairs), tabs, cols(0)],
        out_specs=[cols(0), cols(0), cols(0), tabs],
        out_shape=[wide, wide, wide, jax.ShapeDtypeStruct((NA_HEADS, NA_BASES, GRID_W, NA_KEYS), F32)],
        compiler_params=_cparams(("parallel",)),
    )(qkv, qkv, qkv, tab, do)


def _na_bias_table(rpb):
    H, n_rows, n_cols = rpb.shape

    def body(r_ref, tab_ref):
        q = lax.broadcasted_iota(jnp.int32, (GRID_W, 128), 0)
        kc = lax.broadcasted_iota(jnp.int32, (GRID_W, 128), 1)
        first = jnp.clip(q - NA_WIN_COLS // 2, 0, GRID_W - NA_WIN_COLS)
        valid = (kc >= first) & (kc < first + NA_WIN_COLS)
        toeplitz = []
        for ro in range(n_rows):
            row = jnp.broadcast_to(r_ref[pl.ds(ro, 1), :], (GRID_W, 128))
            shifted = pltpu.roll(pltpu.roll(row, 128 - (NA_WIN_COLS - 1), 1), 0, 1, stride=1, stride_axis=0)
            toeplitz.append(jnp.where(valid, shifted, NEG_INF))
        for base in range(NA_BASES):
            for j in range(NA_WIN_ROWS // 2):
                even, odd = toeplitz[base + 2 * j], toeplitz[base + 2 * j + 1]
                tab_ref[base, :, pl.ds(j * 128, 128)] = jnp.where(kc < GRID_W, even, pltpu.roll(odd, GRID_W, 1))

    padded = jnp.pad(rpb, ((0, 0), (0, 16 - n_rows), (0, 128 - n_cols)))
    return pl.pallas_call(
        body, name="na_bias_table", grid=(H,),
        in_specs=[pl.BlockSpec((None, 16, 128), lambda h: (h, 0, 0))],
        out_specs=pl.BlockSpec((None, NA_BASES, GRID_W, NA_KEYS), lambda h: (h, 0, 0, 0)),
        out_shape=jax.ShapeDtypeStruct((H, NA_BASES, GRID_W, NA_KEYS), F32),
        compiler_params=_cparams(("parallel",)),
    )(padded)


def _na_rpb_grad(dtab):
    H = dtab.shape[0]
    n_rows = 2 * NA_WIN_ROWS - 1
    n_cols = 2 * NA_WIN_COLS - 1

    def body(d_ref, o_ref):
        lane = lax.broadcasted_iota(jnp.int32, (GRID_W, 128), 1)
        low = lane < GRID_W
        out_rows = []
        for ro in range(n_rows):
            acc = jnp.zeros((GRID_W, 128), F32)
            for base in range(NA_BASES):
                i = ro - base
                if not 0 <= i < NA_WIN_ROWS:
                    continue
                pair = d_ref[base, :, pl.ds((i // 2) * 128, 128)]
                if i % 2:
                    pair = pltpu.roll(pair, GRID_W, 1)
                acc = acc + jnp.where(low, pair, 0.0)
            skew = pltpu.roll(acc, 0, 1, stride=1, stride_axis=0)
            diag = jnp.sum(skew, axis=0, keepdims=True)
            out_rows.append(pltpu.roll(jnp.broadcast_to(diag, (8, 128)), 128 - (GRID_W - NA_WIN_COLS), 1)[:1])
        out_rows.append(jnp.zeros((1, 128), F32))
        res = jnp.concatenate(out_rows, axis=0)
        o_ref[...] = jnp.where(lax.broadcasted_iota(jnp.int32, res.shape, 1) < n_cols, res, 0.0)

    return pl.pallas_call(
        body, name="na_rpb_grad", grid=(H,),
        in_specs=[pl.BlockSpec((None, NA_BASES, GRID_W, NA_KEYS), lambda h: (h, 0, 0, 0))],
        out_specs=pl.BlockSpec((None, n_rows + 1, 128), lambda h: (h, 0, 0)),
        out_shape=jax.ShapeDtypeStruct((H, n_rows + 1, 128), F32),
        compiler_params=_cparams(("parallel",)),
    )(jnp.flip(dtab, axis=2))


BAND_Q = 128
BAND_KEYS = BAND_Q + 2 * DIL_RADIUS


def _band_geometry(n, L):
    q0 = pl.multiple_of(n * BAND_Q, BAND_Q)
    k0 = pl.multiple_of(jnp.clip(q0 - DIL_RADIUS, 0, L - BAND_KEYS), DIL_RADIUS)
    qi = q0 + lax.broadcasted_iota(jnp.int32, (BAND_Q, BAND_KEYS), 0)
    kj = k0 + lax.broadcasted_iota(jnp.int32, (BAND_Q, BAND_KEYS), 1)
    return q0, k0, jnp.abs(qi - kj) <= DIL_RADIUS


DIL_PAIRS = DIL_OUT_WIDTH // 128


def _residue_shape(dil, T, dtype):
    return jax.ShapeDtypeStruct((DIL_PAIRS, dil, T // dil, 128), dtype)


def _residue_tile(dil, tm):
    return pl.BlockSpec((DIL_PAIRS, dil, tm // dil, 128), lambda i: (0, 0, i, 0))


def _to_natural(ref, scratch, dil, tm):
    tiles = []
    for pair in range(DIL_PAIRS):
        if dil == 1:
            tiles.append(ref[pair, 0].astype(F32))
            continue
        for r in range(dil):
            scratch[pl.ds(r, tm // dil, stride=dil), :] = ref[pair, r].astype(F32)
        tiles.append(scratch[...])
    return tiles


def _from_natural(tile, scratch, ref, pair, dil, tm):
    if dil == 1:
        ref[pair, 0] = tile.astype(ref.dtype)
        return
    scratch[...] = tile
    for r in range(dil):
        ref[pair, r] = scratch[pl.ds(r, tm // dil, stride=dil), :].astype(ref.dtype)


def _band_specs(group, T):
    dil = DIL_GROUPS[group][1]
    L = T // dil
    assert L % BAND_Q == 0 and L >= BAND_KEYS, (T, dil)
    return L, (dil * DIL_PAIRS,), pl.BlockSpec((None, None, L, 128), lambda s: (s % DIL_PAIRS, s // DIL_PAIRS, 0, 0))


BAND_BLOCKS_PER_STEP = 4


def _band_softmax(s, valid):
    s = jnp.where(valid, s, NEG_INF)
    m = jnp.max(s, axis=-1, keepdims=True)
    p = jnp.exp(s - m)
    l = jnp.sum(p, axis=-1, keepdims=True)
    return p / l, m + jnp.log(l)


def _band_fwd(q, k, v, group):
    T = q.shape[1] * q.shape[2]
    L, grid, spec = _band_specs(group, T)
    U = min(BAND_BLOCKS_PER_STEP, L // BAND_Q)

    def body(q_ref, k_ref, v_ref, o_ref, lse_ref):
        def step(it, carry):
            geo = [_band_geometry(it * U + u, L) for u in range(U)]
            ss = [lax.dot_general(jnp.concatenate(_split_pair(q_ref[pl.ds(q0, BAND_Q), :]), axis=0),
                                  k_ref[pl.ds(k0, BAND_KEYS), :], _NT, preferred_element_type=F32) for q0, k0, _ in geo]
            pls = [_band_softmax(s, jnp.concatenate([valid, valid], axis=0)) for s, (_, _, valid) in zip(ss, geo)]
            os = [jnp.dot(p.astype(BF), v_ref[pl.ds(k0, BAND_KEYS), :], preferred_element_type=F32) for (p, _), (_, k0, _) in zip(pls, geo)]
            for (q0, _, _), o2, (_, lse) in zip(geo, os, pls):
                o_ref[pl.ds(q0, BAND_Q), :] = _join_pair(o2[:BAND_Q], o2[BAND_Q:])
                lse2 = jnp.broadcast_to(lse, (2 * BAND_Q, 128))
                lse_ref[pl.ds(q0, BAND_Q), :] = _join_pair(lse2[:BAND_Q], lse2[BAND_Q:])
            return carry

        lax.fori_loop(0, L // (BAND_Q * U), step, 0)

    res = _residue_shape(DIL_GROUPS[group][1], T, F32)
    return pl.pallas_call(
        body, name=f"band_fwd_g{group}", grid=grid,
        in_specs=[spec] * 3, out_specs=[spec] * 2, out_shape=[res, res],
        compiler_params=_cparams(("parallel",)),
    )(q, k, v)


def _band_bwd(q, k, v, do, dlse, group):
    T = q.shape[1] * q.shape[2]
    L, grid, spec = _band_specs(group, T)
    U = min(BAND_BLOCKS_PER_STEP, L // BAND_Q)

    def body(q_ref, k_ref, v_ref, do_ref, dlse_ref, dq_ref, dk_ref, dv_ref):
        dk_ref[...] = jnp.zeros_like(dk_ref)
        dv_ref[...] = jnp.zeros_like(dv_ref)

        def step(it, carry):
            geo = [_band_geometry(it * U + u, L) for u in range(U)]
            q2s = [jnp.concatenate(_split_pair(q_ref[pl.ds(q0, BAND_Q), :]), axis=0) for q0, _, _ in geo]
            do2s = [jnp.concatenate(_split_pair(do_ref[pl.ds(q0, BAND_Q), :]), axis=0) for q0, _, _ in geo]
            ss = [lax.dot_general(q2, k_ref[pl.ds(k0, BAND_KEYS), :], _NT, preferred_element_type=F32) for q2, (_, k0, _) in zip(q2s, geo)]
            dps = [lax.dot_general(do2, v_ref[pl.ds(k0, BAND_KEYS), :], _NT, preferred_element_type=F32) for do2, (_, k0, _) in zip(do2s, geo)]
            ps = [_band_softmax(s, jnp.concatenate([valid, valid], axis=0))[0] for s, (_, _, valid) in zip(ss, geo)]
            dss = []
            for p, dp, (q0, _, _) in zip(ps, dps, geo):
                dl = dlse_ref[pl.ds(q0, BAND_Q), :]
                dl2 = jnp.concatenate([dl[:, :1], dl[:, HEAD_DIM:HEAD_DIM + 1]], axis=0)
                dss.append(p * (dp - jnp.sum(dp * p, axis=-1, keepdims=True) + dl2))
            dvs = [lax.dot_general(p.astype(BF), do2, _TN, preferred_element_type=F32) for p, do2 in zip(ps, do2s)]
            dsbs = [ds.astype(BF) for ds in dss]
            dqs = [jnp.dot(dsb, k_ref[pl.ds(k0, BAND_KEYS), :], preferred_element_type=F32) for dsb, (_, k0, _) in zip(dsbs, geo)]
            dks = [lax.dot_general(dsb, q2, _TN, preferred_element_type=F32) for dsb, q2 in zip(dsbs, q2s)]
            for u, (q0, k0, _) in enumerate(geo):
                dq_ref[pl.ds(q0, BAND_Q), :] = _join_pair(dqs[u][:BAND_Q], dqs[u][BAND_Q:])
                dk_ref[pl.ds(k0, BAND_KEYS), :] += dks[u]
                dv_ref[pl.ds(k0, BAND_KEYS), :] += dvs[u]
            return carry

        lax.fori_loop(0, L // (BAND_Q * U), step, 0)

    res = _residue_shape(DIL_GROUPS[group][1], T, F32)
    return pl.pallas_call(
        body, name=f"band_bwd_g{group}", grid=grid,
        in_specs=[spec] * 5, out_specs=[spec] * 3, out_shape=[res] * 3,
        compiler_params=_cparams(("parallel",)),
    )(q, k, v, do, dlse)


def _head_sums(t):
    head = lax.broadcasted_iota(jnp.int32, t.shape, 1) // HEAD_DIM
    out = jnp.zeros_like(t)
    for h in range(t.shape[1] // HEAD_DIM):
        mine = head == h
        out = jnp.where(mine, jnp.sum(jnp.where(mine, t, 0.0), axis=-1, keepdims=True), out)
    return out


def _dil_merge_fwd(os, lses, T, tm):
    G = len(DIL_GROUPS)
    W = DIL_OUT_WIDTH
    dils = [d for _, d in DIL_GROUPS]

    def body(*refs):
        o_refs, lse_refs = refs[:G], refs[G:2 * G]
        y_ref, w_refs, on_refs, scratch = refs[2 * G], refs[2 * G + 1:3 * G + 1], refs[3 * G + 1:4 * G + 1], refs[-1]
        o = [jnp.concatenate(_to_natural(r, scratch, d, tm), axis=1) for r, d in zip(o_refs, dils)]
        ls = [jnp.concatenate(_to_natural(r, scratch, d, tm), axis=1) for r, d in zip(lse_refs, dils)]
        m = functools.reduce(jnp.maximum, ls)
        es = [jnp.exp(l - m) for l in ls]
        tot = functools.reduce(jnp.add, es)
        ws = [e / tot for e in es]
        y_ref[...] = functools.reduce(jnp.add, [w * t for w, t in zip(ws, o)]).astype(y_ref.dtype)
        for g in range(G):
            w_refs[g][...] = ws[g]
            on_refs[g][...] = o[g]

    nat = pl.BlockSpec((tm, W), lambda i: (i, 0))
    res = pl.pallas_call(
        body, name="dil_merge_fwd", grid=(T // tm,),
        in_specs=[_residue_tile(d, tm) for d in dils] * 2,
        out_specs=[nat] * (2 * G + 1),
        out_shape=[jax.ShapeDtypeStruct((T, W), BF)] + [jax.ShapeDtypeStruct((T, W), F32)] * (2 * G),
        scratch_shapes=[pltpu.VMEM((tm, 128), F32)],
        compiler_params=_cparams(("parallel",)),
    )(*os, *lses)
    return res[0], res[1:G + 1], res[G + 1:]


def _dil_merge_bwd(dy, os, ws, tm, after=()):
    G = len(DIL_GROUPS)
    T, W = dy.shape
    dils = [d for _, d in DIL_GROUPS]
    n_after = len(after)

    def body(*refs):
        dyt = refs[0][...]
        o, w = [r[...] for r in refs[1:G + 1]], [r[...] for r in refs[G + 1:2 * G + 1]]
        refs = refs[2 * G + 1 + n_after:]
        do_refs, dlse_refs, scratch = refs[:G], refs[G:2 * G], refs[-1]
        dws = [_head_sums(dyt * t) for t in o]
        mean = functools.reduce(jnp.add, [a * b for a, b in zip(w, dws)])
        for g, d in enumerate(dils):
            do, dlse = w[g] * dyt, w[g] * (dws[g] - mean)
            for pair in range(DIL_PAIRS):
                cols = slice(pair * 128, (pair + 1) * 128)
                _from_natural(do[:, cols], scratch, do_refs[g], pair, d, tm)
                _from_natural(dlse[:, cols], scratch, dlse_refs[g], pair, d, tm)

    nat = pl.BlockSpec((tm, W), lambda i: (i, 0))
    res = pl.pallas_call(
        body, name="dil_merge_bwd", grid=(T // tm,),
        in_specs=[nat] * (2 * G + 1) + [ANY] * n_after,
        out_specs=[_residue_tile(d, tm) for d in dils] * 2,
        out_shape=[_residue_shape(d, T, BF) for d in dils] + [_residue_shape(d, T, F32) for d in dils],
        scratch_shapes=[pltpu.VMEM((tm, 128), F32)],
        compiler_params=_cparams(("parallel",)),
    )(dy, *os, *ws, *after)
    return res[:G], res[G:]


def _qkv_prep(z, cos2, sin_signed, tm):
    T = z.shape[0]
    G = len(DIL_GROUPS)
    dils = [d for _, d in DIL_GROUPS]
    n_dil_blocks = 3 * DIL_WIDTH // 128

    def body(*refs):
        blocks = refs[:n_dil_blocks]
        cos_ref, sin_ref = refs[n_dil_blocks], refs[1 + n_dil_blocks]
        outs = refs[2 + n_dil_blocks:]
        for part in range(3):
            for g, d in enumerate(dils):
                out = outs[g * 3 + part]
                for pair in range(DIL_PAIRS):
                    blk = blocks[part * (DIL_WIDTH // 128) + g * DIL_PAIRS + pair]
                    for r in range(d):
                        rows = pl.ds(r, tm // d, stride=d) if d > 1 else slice(None)
                        x = blk[rows, :]
                        if part < 2:
                            x = _rope(x, cos_ref[rows, :], sin_ref[rows, :])
                        if part == 0:
                            x = x * Q_SCALE
                        out[pair, r] = x.astype(out.dtype)

    lane_block = [pl.BlockSpec((tm, 128), functools.partial(lambda b, i: (i, b), b)) for b in range(n_dil_blocks)]
    tab = pl.BlockSpec((tm, 128), lambda i: (i, 0))
    res = pl.pallas_call(
        body, name="qkv_prep", grid=(T // tm,),
        in_specs=lane_block + [tab, tab],
        out_specs=[_residue_tile(d, tm) for d in dils for _ in range(3)],
        out_shape=[_residue_shape(d, T, BF) for d in dils for _ in range(3)],
        compiler_params=_cparams(("parallel",)),
    )(*[z] * n_dil_blocks, cos2, sin_signed)
    return [res[3 * g:3 + 3 * g] for g in range(G)]


def _qkv_unprep(d_na, d_dil, cos2, sin_signed, tm, after=()):
    T = d_na[0].shape[0]
    G = len(DIL_GROUPS)
    dils = [d for _, d in DIL_GROUPS]
    n_after = len(after)

    def body(*refs):
        dq, dk, dv = (r[...] for r in refs[:3])
        res_refs = refs[3:3 + 3 * G]
        cs, sn = refs[3 + 3 * G][...], refs[4 + 3 * G][...]
        out, scratch = refs[5 + 3 * G + n_after], refs[-1]
        cols = [dq * Q_SCALE, dk, dv]
        for part in range(3):
            for g, d in enumerate(dils):
                for x in _to_natural(res_refs[g * 3 + part], scratch, d, tm):
                    if part < 2:
                        x = _rope(x, cs, -sn)
                    cols.append(x * Q_SCALE if part == 0 else x)
        out[...] = jnp.concatenate(cols, axis=1).astype(out.dtype)

    wide = pl.BlockSpec((tm, NA_WIDTH), lambda i: (i, 0))
    tab = pl.BlockSpec((tm, 128), lambda i: (i, 0))
    return pl.pallas_call(
        body, name="qkv_unprep", grid=(T // tm,),
        in_specs=[wide] * 3 + [_residue_tile(d, tm) for d in dils for _ in range(3)] + [tab, tab] + [ANY] * n_after,
        out_specs=pl.BlockSpec((tm, QKV_WIDTH), lambda i: (i, 0)),
        out_shape=jax.ShapeDtypeStruct((T, QKV_WIDTH), BF),
        scratch_shapes=[pltpu.VMEM((tm, 128), F32)],
        compiler_params=_cparams(("parallel",)),
    )(*d_na, *[t for g in range(G) for t in d_dil[g]], cos2, sin_signed, *after)


def _rope_tables(positions):
    half = HEAD_DIM // 2
    inv_freq = ROPE_THETA ** (-jnp.arange(half, dtype=F32) / half)
    ang = positions.astype(F32)[:, None] * inv_freq
    cos, sin = jnp.cos(ang), jnp.sin(ang)
    return jnp.tile(jnp.concatenate([cos, cos], axis=1), (1, 2)), jnp.tile(jnp.concatenate([-sin, sin], axis=1), (1, 2))


def _pack_rows(t):
    return t.reshape(-1, PACK_W)


def _me():
    return lax.axis_index("x"), lax.axis_index("y"), lax.axis_index("c")


def _other_chips(x, y):
    return [(1 - x, y), (x, 1 - y), (1 - x, 1 - y)]


def _gather_weights(packed):
    R, W = packed.shape
    half = R // 2

    def body(in_ref, out_ref, send_sems, recv_sems):
        x, y, c = _me()
        sibling = (x, y, 1 - c)
        chips = _other_chips(x, y)

        def block(chip, core):
            return out_ref.at[2 * chip[0] + chip[1], pl.ds(core * half, half), :]

        def copy(k, chip, core, to, src=None):
            return pltpu.make_async_remote_copy(
                src_ref=block(chip, core) if src is None else src, dst_ref=block(chip, core),
                send_sem=send_sems.at[k], recv_sem=recv_sems.at[k], device_id=to, device_id_type=MESH)

        first = [copy(j, (x, y), c, (*chip, c), src=in_ref.at[pl.ds(c * half, half), :]) for j, chip in enumerate(chips)]
        for cp in first:
            cp.start()
        passed = [copy(3 + j, chip, c, sibling) for j, chip in enumerate(chips)]
        for j, chip in enumerate(chips):
            copy(j, chip, c, (x, y, c)).wait_recv()
            passed[j].start()
        for j, chip in enumerate(chips):
            copy(3 + j, chip, 1 - c, (x, y, c)).wait_recv()
        for cp in first + passed:
            cp.wait_send()

    others = pl.pallas_call(
        body, name="gather_weights",
        in_specs=[ANY], out_specs=ANY,
        out_shape=jax.ShapeDtypeStruct((N_CHIPS, R, W), packed.dtype),
        scratch_shapes=[pltpu.SemaphoreType.DMA((6,)), pltpu.SemaphoreType.DMA((6,))],
    )(packed)
    return lax.dynamic_update_slice(others, packed[None], (2 * lax.axis_index("x") + lax.axis_index("y"), 0, 0))


def _pair_sum(g, got, tm):
    S, R, W = g.shape
    half = R // 2
    nb = half // tm

    def body(c_ref, g_ref, got_ref, o_ref, ob_ref):
        tot = g_ref[...] + got_ref[...]
        o_ref[...] = tot
        ob_ref[...] = tot.astype(ob_ref.dtype)

    tile = pl.BlockSpec((None, tm, W), lambda s, i, c_ref: (s, i, 0))
    return pl.pallas_call(
        body, name="pair_sum",
        grid_spec=pltpu.PrefetchScalarGridSpec(
            num_scalar_prefetch=1, grid=(S, nb),
            in_specs=[pl.BlockSpec((None, tm, W), lambda s, i, c_ref: (s, c_ref[0] * nb + i, 0)), tile],
            out_specs=[tile, tile]),
        out_shape=[jax.ShapeDtypeStruct((S, half, W), F32), jax.ShapeDtypeStruct((S, half, W), BF)],
        compiler_params=_cparams(("parallel", "parallel")),
    )(lax.axis_index("c").reshape(1).astype(jnp.int32), g, got)


def _chip_sum(own, others, tm):
    n, h, W = others.shape
    nb = h // tm

    def body(c_ref, own_ref, p_ref, o_ref):
        o_ref[...] = ((own_ref[...] + p_ref[0].astype(F32)) + p_ref[1].astype(F32)) + p_ref[2].astype(F32)

    return pl.pallas_call(
        body, name="chip_sum",
        grid_spec=pltpu.PrefetchScalarGridSpec(
            num_scalar_prefetch=1, grid=(nb,),
            in_specs=[pl.BlockSpec((tm, W), lambda i, c_ref: (i, 0)), pl.BlockSpec((n, tm, W), lambda i, c_ref: (0, i, 0))],
            out_specs=pl.BlockSpec((tm, W), lambda i, c_ref: (c_ref[0] * nb + i, 0))),
        out_shape=jax.ShapeDtypeStruct((2 * h, W), F32),
        compiler_params=_cparams(("parallel",)),
    )(lax.axis_index("c").reshape(1).astype(jnp.int32), own, others)


def _join_halves(shard):
    h = shard.shape[0] // 2

    def body(in_ref, out_ref, send_sem, recv_sem):
        x, y, c = _me()
        cp = pltpu.make_async_remote_copy(
            src_ref=in_ref.at[pl.ds(c * h, h), :], dst_ref=out_ref.at[pl.ds(c * h, h), :],
            send_sem=send_sem, recv_sem=recv_sem, device_id=(x, y, 1 - c), device_id_type=MESH)
        cp.start()
        pltpu.make_async_remote_copy(
            src_ref=in_ref.at[pl.ds(c * h, h), :], dst_ref=out_ref.at[pl.ds((1 - c) * h, h), :],
            send_sem=send_sem, recv_sem=recv_sem, device_id=(x, y, 1 - c), device_id_type=MESH).wait_recv()
        cp.wait_send()

    return pl.pallas_call(
        body, name="join_halves", in_specs=[ANY], out_specs=ANY,
        out_shape=jax.ShapeDtypeStruct(shard.shape, shard.dtype), input_output_aliases={0: 0},
        scratch_shapes=[pltpu.SemaphoreType.DMA, pltpu.SemaphoreType.DMA],
    )(shard)


def _allreduce_small(s, after=()):
    R, W = s.shape
    n_after = len(after)

    def body(s_ref, *rest):
        o_ref, buf, send_sems, recv_sems = rest[n_after:]
        x, y, c = _me()
        me = 4 * x + 2 * y + c
        buf[me] = s_ref[...]
        peers = [((x + fx) % 2, (y + fy) % 2, (c + fc) % 2) for fx in range(2) for fy in range(2) for fc in range(2)][1:]
        sends = [pltpu.make_async_remote_copy(
            src_ref=s_ref, dst_ref=buf.at[me], send_sem=send_sems.at[k], recv_sem=recv_sems.at[k],
            device_id=peer, device_id_type=MESH) for k, peer in enumerate(peers)]
        for cp in sends:
            cp.start()
        for k, peer in enumerate(peers):
            pltpu.make_async_remote_copy(
                src_ref=s_ref, dst_ref=buf.at[4 * peer[0] + 2 * peer[1] + peer[2]], send_sem=send_sems.at[k],
                recv_sem=recv_sems.at[k], device_id=peer, device_id_type=MESH).wait_recv()
        for cp in sends:
            cp.wait_send()
        total = buf[0]
        for d in range(1, N_DEV):
            total = total + buf[d]
        o_ref[...] = total

    return pl.pallas_call(
        body, name="allreduce_small",
        in_specs=[pl.BlockSpec(memory_space=pltpu.VMEM)] + [ANY] * n_after, out_specs=pl.BlockSpec(memory_space=pltpu.VMEM),
        out_shape=jax.ShapeDtypeStruct((R, W), F32),
        scratch_shapes=[pltpu.VMEM((N_DEV, R, W), F32), pltpu.SemaphoreType.DMA((N_DEV - 1,)), pltpu.SemaphoreType.DMA((N_DEV - 1,))],
    )(s, *after)


HBM_SPEC = pl.BlockSpec(memory_space=pltpu.HBM)
SEM_SPEC = pl.BlockSpec(memory_space=pltpu.SEMAPHORE)
DATAFLOW = pltpu.SideEffectType.DATAFLOW_SIDE_EFFECTING


class _InFlight(NamedTuple):
    sems: tuple
    src: jax.Array
    land: jax.Array
    token: jax.Array


def _split_start(name, src, land_shape, land_dtype, n, copies, after=()):
    n_after = len(after)

    def body(src_ref, land_ref, *rest):
        rest = rest[n_after:]
        sems, token = rest[:2 * n], rest[-1]
        for k, (s, d, peer) in enumerate(copies(src_ref, land_ref)):
            pltpu.make_async_remote_copy(src_ref=s, dst_ref=d, send_sem=sems[k], recv_sem=sems[n + k],
                                         device_id=peer, device_id_type=MESH).start()
        token[...] = jnp.zeros_like(token)

    outs = pl.pallas_call(
        body, name=name,
        out_shape=(*[pltpu.SemaphoreType.DMA(())] * (2 * n), pltpu.HBM(src.shape, src.dtype), pltpu.HBM(land_shape, land_dtype),
                   jax.ShapeDtypeStruct((8, 128), F32)),
        in_specs=(HBM_SPEC, HBM_SPEC, *[ANY] * n_after),
        out_specs=(*[SEM_SPEC] * (2 * n), HBM_SPEC, HBM_SPEC, pl.BlockSpec(memory_space=pltpu.VMEM)),
        input_output_aliases={0: 2 * n, 1: 2 * n + 1},
        compiler_params=pltpu.CompilerParams(has_side_effects=DATAFLOW),
    )(pltpu.with_memory_space_constraint(src, pltpu.HBM), pltpu.with_memory_space_constraint(lax.empty(land_shape, land_dtype), pltpu.HBM),
      *after)
    return _InFlight(tuple(outs[:2 * n]), outs[2 * n], outs[2 * n + 1], outs[2 * n + 2])


def _split_wait(name, flight, after, n, copies):
    def body(src_ref, land_ref, *rest):
        sems = rest[:2 * n]
        for k, (s, d, peer) in enumerate(copies(src_ref, land_ref)):
            cp = pltpu.make_async_remote_copy(src_ref=s, dst_ref=d, send_sem=sems[k], recv_sem=sems[n + k],
                                              device_id=peer, device_id_type=MESH)
            cp.wait_send()
            cp.wait_recv()

    return pl.pallas_call(
        body, name=name,
        out_shape=(pltpu.HBM(flight.src.shape, flight.src.dtype), pltpu.HBM(flight.land.shape, flight.land.dtype)),
        in_specs=(HBM_SPEC, HBM_SPEC, *[SEM_SPEC] * (2 * n), ANY),
        out_specs=(HBM_SPEC, HBM_SPEC), input_output_aliases={0: 0, 1: 1},
        compiler_params=pltpu.CompilerParams(has_side_effects=DATAFLOW),
    )(flight.src, flight.land, *flight.sems, after)


def _gather_copies(src_ref, land_ref):
    x, y, c = _me()
    return [(src_ref, land_ref.at[2 * x + y], (*chip, c)) for chip in _other_chips(x, y)]


def _gather_start(packed, tag, after=()):
    return _split_start(f"gather_start_{tag}", packed, (N_CHIPS, *packed.shape), packed.dtype, 3, _gather_copies, after)


def _gather_wait(flight, after, tag):
    src, others = _split_wait(f"gather_wait_{tag}", flight, after, 3, _gather_copies)
    return lax.dynamic_update_slice(others, src[None], (2 * lax.axis_index("x") + lax.axis_index("y"), 0, 0))


def _swap_copies(src_ref, land_ref):
    x, y, c = _me()
    half = land_ref.shape[1]
    return [(src_ref.at[:, pl.ds((1 - c) * half, half), :], land_ref, (x, y, 1 - c))]


def _swap_start(g, tag):
    S, R, W = g.shape
    return _split_start(f"swap_halves_start_{tag}", g, (S, R // 2, W), g.dtype, 1, _swap_copies)


def _swap_wait(flight, after, tag):
    return _split_wait(f"swap_halves_wait_{tag}", flight, after, 1, _swap_copies)


def _scatter_copies(src_ref, land_ref):
    x, y, c = _me()
    return [(src_ref.at[2 * chip[0] + chip[1]], land_ref.at[j], (*chip, c)) for j, chip in enumerate(_other_chips(x, y))]


def _scatter_start(part, tag):
    S, h, W = part.shape
    return _split_start(f"scatter_chips_start_{tag}", part, (S - 1, h, W), part.dtype, 3, _scatter_copies)


def _scatter_wait(flight, after, tag):
    return _split_wait(f"scatter_chips_wait_{tag}", flight, after, 3, _scatter_copies)[1]


def _join_copies(shard_ref, unused_ref):
    x, y, c = _me()
    h = shard_ref.shape[0] // 2
    rows = shard_ref.at[pl.ds(c * h, h), :]
    return [(rows, rows, (x, y, 1 - c))]


def _join_start(shard):
    return _split_start("join_halves_start", shard, (8, 128), shard.dtype, 1, _join_copies)


def _join_wait(flight, after):
    return _split_wait("join_halves_wait", flight, after, 1, _join_copies)[0]


def _adamw(name, g, g_row0, w, m, v):
    _, R, C = w.shape
    tm = next(cand for cand in (256, 128, 64, 32, 16, 8) if R % cand == 0)
    assert g_row0 % tm == 0 and g.shape[1] == C

    def body(g_ref, w_ref, m_ref, v_ref, go_ref, d_ref, mo_ref, vo_ref):
        gt = g_ref[...]
        mt = ADAM_B1 * m_ref[...] + (1.0 - ADAM_B1) * gt
        vt = ADAM_B2 * v_ref[...] + (1.0 - ADAM_B2) * jnp.square(gt)
        m_hat = mt / (1.0 - ADAM_B1 ** ADAM_STEP)
        v_hat = vt / (1.0 - ADAM_B2 ** ADAM_STEP)
        go_ref[...] = gt
        d_ref[...] = -ADAM_LR * (m_hat / (jnp.sqrt(v_hat) + ADAM_EPS) + ADAM_WD * w_ref[...])
        mo_ref[...] = mt
        vo_ref[...] = vt

    state = pl.BlockSpec((None, tm, C), lambda i: (0, i, 0))
    return pl.pallas_call(
        body, name=name, grid=(R // tm,),
        in_specs=[pl.BlockSpec((tm, C), lambda i: (g_row0 // tm + i, 0)), state, state, state],
        out_specs=[state] * 4, out_shape=[jax.ShapeDtypeStruct((1, R, C), F32)] * 4,
        compiler_params=_cparams(("parallel",)),
    )(g, w, m, v)


def _unpack_weights(gathered, names):
    S = gathered.shape[0]
    shard_shapes = {"w_in": (D_MODEL, (QKV_WIDTH + 2 * D_MODEL) // S), "w_branch_na": (NA_WIDTH, D_MODEL // S),
                    "w_branch_dil": (DIL_OUT_WIDTH, D_MODEL // S), "w_out": (D_MODEL // S, D_MODEL),
                    "w_up": (D_MODEL, D_FF // S), "w_down": (D_FF // S, D_MODEL),
                    "w_ple_gate": (D_MODEL // S, D_MODEL), "w_ple_proj": (PLE_DIM, D_MODEL // S)}
    col_sharded = {"w_in", "w_branch_na", "w_branch_dil", "w_up", "w_ple_proj"}
    out, r0 = {}, 0
    for name in names:
        rows, cols = shard_shapes[name]
        n = rows * cols // PACK_W
        t = gathered[:, r0:r0 + n, :].reshape(S, rows, cols)
        r0 += n
        out[name] = t.transpose(1, 0, 2).reshape(rows, S * cols) if name in col_sharded else t.reshape(S * rows, cols)
    return out


def kernel(x, p, positions, g_mix, w_in, rpb, w_branch_na, w_branch_dil, w_out, g_mlp, w_up, w_down, g_ple, w_ple_gate, w_ple_proj, g_final, loss_target, m_g_mix, m_w_in, m_rpb, m_w_branch_na, m_w_branch_dil, m_w_out, m_g_mlp, m_w_up, m_w_down, m_g_ple, m_w_ple_gate, m_w_ple_proj, m_g_final, v_g_mix, v_w_in, v_rpb, v_w_branch_na, v_w_branch_dil, v_w_out, v_g_mlp, v_w_up, v_w_down, v_g_ple, v_w_ple_gate, v_w_ple_proj, v_g_final):
    shards = {"w_in": w_in[0], "w_branch_na": w_branch_na[0], "w_branch_dil": w_branch_dil[0], "w_out": w_out[0],
              "w_up": w_up[0], "w_down": w_down[0], "w_ple_gate": w_ple_gate[0], "w_ple_proj": w_ple_proj[0]}
    params = {"w_in": w_in, "w_branch_na": w_branch_na, "w_branch_dil": w_branch_dil, "w_out": w_out, "w_up": w_up,
              "w_down": w_down, "w_ple_gate": w_ple_gate, "w_ple_proj": w_ple_proj,
              "m_w_in": m_w_in, "m_w_branch_na": m_w_branch_na, "m_w_branch_dil": m_w_branch_dil, "m_w_out": m_w_out,
              "m_w_up": m_w_up, "m_w_down": m_w_down, "m_w_ple_gate": m_w_ple_gate, "m_w_ple_proj": m_w_ple_proj,
              "v_w_in": v_w_in, "v_w_branch_na": v_w_branch_na, "v_w_branch_dil": v_w_branch_dil, "v_w_out": v_w_out,
              "v_w_up": v_w_up, "v_w_down": v_w_down, "v_w_ple_gate": v_w_ple_gate, "v_w_ple_proj": v_w_ple_proj}

    W = {"w_in": _gather_weights(shards["w_in"].astype(BF)).transpose(1, 0, 2).reshape(D_MODEL, -1)}
    mix_flight = _gather_start(jnp.concatenate([_pack_rows(shards[n].astype(BF)) for n in GATHER_MIX], axis=0), "mix")
    rest_flight = _gather_start(jnp.concatenate([_pack_rows(shards[n].astype(BF)) for n in GATHER_MLP], axis=0), "mlp",
                                after=(mix_flight.token,))
    w_qkv, w_gates = W["w_in"][:, :QKV_WIDTH], W["w_in"][:, QKV_WIDTH:]

    xs, ps, tgt = x[0], p[0, 0], loss_target[0]
    T = xs.shape[0]
    TM = 512
    gm, gl, gp, gf = g_mix, g_mlp, g_ple, g_final.reshape(1, D_MODEL)
    cos2, sin_signed = _rope_tables(positions[0])
    tab = _na_bias_table(rpb[0])

    a = _rowwise("norm_mix", lambda h, g: h * _rms(h) * g, T, TM, [_row(xs, TM), _full(gm)], [(D_MODEL, BF)],
                 after=(rest_flight.token,))
    n3 = 3 * NA_WIDTH
    qkv = _mm("in_na", a, w_qkv[:, :n3], "nn", 1024, 768, 1024, [BF])
    z_dil = _mm("in_dil", a, w_qkv[:, n3:], "nn", 1024, 1152, 1024, [F32])
    z_gates = _mm("in_gates", a, w_gates, "nn", 1024,1024, 1024, [BF])

    dil_ops = _qkv_prep(z_dil, cos2, sin_signed, TM)
    y_na = _na_fwd(qkv, tab)
    band = [_band_fwd(*dil_ops[g], g) for g in range(len(DIL_GROUPS))]
    y_dil, w_grp, o_nat = _dil_merge_fwd([b[0] for b in band], [b[1] for b in band], T, TM)

    W.update(_unpack_weights(_gather_wait(mix_flight, y_dil, "mix"), GATHER_MIX))
    u_na = _mm("branch_na", y_na, W["w_branch_na"], "nn", 1024,1024, 512, [BF])
    u_dil = _mm("branch_dil", y_dil, W["w_branch_dil"], "nn", 1024,1024, 256, [BF])
    mixed = _rowwise(
        "gate_mix", lambda gn, gd, un, ud: _sigmoid(gn.astype(F32)) * un.astype(F32) + _sigmoid(gd.astype(F32)) * ud.astype(F32), T, TM,
        [_row(z_gates, TM, 0, D_MODEL), _row(z_gates, TM, 1, D_MODEL), _row(u_na, TM), _row(u_dil, TM)], [(D_MODEL, BF)])
    def add_norm(d, h, g):
        h = h + d
        return h, h * _rms(h) * g

    h1, cn = _mm("out_proj", mixed, W["w_out"], "nn", 512, 1024, 1024, [F32, BF], epilogue=add_norm, extras=(xs,), consts=(gl,))
    W.update(_unpack_weights(_gather_wait(rest_flight, cn, "mlp"), GATHER_MLP))
    up, act = _mm("mlp_up", cn, W["w_up"], "nn", 1024,1024, 1024, [BF, BF],
                  epilogue=lambda acc: (acc, jnp.square(jnp.maximum(acc, 0.0))))
    h2, en = _mm("mlp_down", act, W["w_down"], "nn", 1024, 1024, 1024, [F32, BF], epilogue=add_norm, extras=(h1,), consts=(gp,))
    pp = _mm("ple_proj", ps, W["w_ple_proj"], "nn", 1024,1024, 256, [F32])

    def head(gtt, h2t, ppt, tg, g):
        sg = _sigmoid(gtt)
        h3 = h2t + sg * ppt
        yo = h3 * _rms(h3) * g
        diff = yo - tg
        loss = 0.5 * jnp.sum(jnp.mean(jnp.square(diff), axis=-1, keepdims=True), axis=0, keepdims=True)
        dh3, dg = _rms_bwd(diff * (1.0 / D_MODEL), h3, g)
        return dh3, dh3 * ppt * sg * (1.0 - sg), dh3 * sg, jnp.broadcast_to(loss, (1, 128)), dg

    dh3, d_gt, d_pp, loss_part, dg_final = _mm(
        "ple_gate_loss_head", en, W["w_ple_gate"], "nn", 512, 1024, 1024, [F32, BF, BF], epilogue=head,
        extras=(h2, pp, tgt), consts=(gf,), sums=[128, D_MODEL])

    early_shapes = {n: shards[n].shape for n in REDUCE_EARLY}
    early_rows = sum(r * c for r, c in early_shapes.values()) // PACK_W
    shard_rows = D_MODEL // N_CHIPS
    early_buf = _mm("g_ple_gate", en, d_gt, "tn", 1024, 1024, 1024, [F32],
                    into=(jax.ShapeDtypeStruct((N_CHIPS, early_rows, PACK_W), F32), (N_CHIPS, shard_rows, PACK_W),
                          lambda i, j: (0, 2 * D_MODEL // shard_rows, 0)))
    g_ple_proj = _mm("g_ple_proj", ps, d_pp, "tn", 256, 1024, 1024,[F32])

    def add_norm_bwd(dn, dh_out, h, g):
        dh, dg = _rms_bwd(dn, h, g)
        dh = dh_out + dh
        return dh, dh, dg

    dh2, dh2_b, dg_ple = _mm("d_ple_gate", d_gt, W["w_ple_gate"], "nt", 512, 1024, 1024, [F32, BF],
                             epilogue=add_norm_bwd, extras=(dh3, h2), consts=(gp,), sums=[D_MODEL])
    d_up = _mm("d_mlp_down", dh2_b, W["w_down"], "nt", 1024,1024, 1024, [BF],
               epilogue=lambda acc, u: (acc * (2.0 * jnp.maximum(u.astype(F32), 0.0)),), extras=(up,))
    early_buf = _mm("g_mlp_down", act, dh2_b, "tn", 1024, 1024, 1024,[F32],
                    into=(early_buf, (None, D_MODEL, PACK_W), lambda i, j: (i, 1, 0)))
    early_buf = _mm("g_mlp_up", cn, d_up, "tn", 1024, 1024, 1024,[F32],
                    into=(early_buf, (None, D_MODEL, PACK_W), lambda i, j: (j, 0, 0)))
    dh1, dh1_b, dg_mlp = _mm("d_mlp_up", d_up, W["w_up"], "nt", 1024, 1024, 1024, [F32, BF], epilogue=add_norm_bwd,
                             extras=(dh2, h1), consts=(gl,), sums=[D_MODEL])
    d_mixed = _mm("d_out_proj", dh1_b, W["w_out"], "nt", 1024,1024, 1024, [F32])
    early_buf = _mm("g_out_proj", mixed, dh1_b, "tn", 1024, 1024, 1024, [F32],
                    into=(early_buf, (N_CHIPS, shard_rows, PACK_W), lambda i, j: (0, 2 * D_MODEL // shard_rows + 1, 0)))

    def gate_bwd(dm, gn, gd, un, ud):
        gn, gd, un, ud = (t.astype(F32) for t in (gn, gd, un, ud))
        sn, sd = _sigmoid(gn), _sigmoid(gd)
        return jnp.concatenate([dm * un * sn * (1.0 - sn), dm * ud * sd * (1.0 - sd)], axis=1), dm * sn, dm * sd

    dz_gates, d_u_na, d_u_dil = _rowwise(
        "gate_mix_bwd", gate_bwd, T, TM,
        [_row(d_mixed, TM), _row(z_gates, TM, 0, D_MODEL), _row(z_gates, TM, 1, D_MODEL), _row(u_na, TM), _row(u_dil, TM)],
        [(2 * D_MODEL, BF), (D_MODEL, BF), (D_MODEL, BF)])
    g_branch_na = _mm("g_branch_na", y_na, d_u_na, "tn", 1024, 1024, 1024,[F32])
    g_branch_dil = _mm("g_branch_dil", y_dil, d_u_dil, "tn", 256, 1024, 1024,[F32])
    small_rows = [jnp.concatenate([_pack_rows(g[:, s * shard_rows:(s + 1) * shard_rows]) for g in (g_ple_proj, g_branch_na, g_branch_dil)],
                                  axis=0) for s in range(N_CHIPS)]
    early_buf = lax.dynamic_update_slice(early_buf, jnp.stack(small_rows), (0, 2 * D_MODEL + 2 * shard_rows, 0))
    early_tm = early_rows // 4
    swap_flight = _swap_start(early_buf, "early")
    d_y_na = _mm("d_branch_na", d_u_na, W["w_branch_na"], "nt", 1024,512, 1024, [BF], after=(swap_flight.token,))
    d_y_dil = _mm("d_branch_dil", d_u_dil, W["w_branch_dil"], "nt", 1024,256, 1024, [F32])

    dqa, dka, dva, dtab = _na_bwd(qkv, tab, d_y_na)
    early_g, early_got = _swap_wait(swap_flight, dqa, "early")
    early_pair, early_pair_b = _pair_sum(early_g, early_got, early_tm)
    scatter_flight = _scatter_start(early_pair_b, "early")
    d_rpb = _na_rpb_grad(dtab)[:, :2 * NA_WIN_ROWS - 1, :2 * NA_WIN_COLS - 1]

    do_res, dlse_res = _dil_merge_bwd(d_y_dil, o_nat, w_grp, TM, after=(scatter_flight.token,))
    d_dil = [_band_bwd(*dil_ops[g], do_res[g], dlse_res[g], g) for g in range(len(DIL_GROUPS))]

    dz_qkv = _qkv_unprep((dqa, dka, dva), d_dil, cos2, sin_signed, TM)
    g_in = jnp.concatenate([
        _mm("g_in_qkv", a, dz_qkv, "tn", 1024, 1280, 1024,[F32]),
        _mm("g_in_gates", a, dz_gates, "tn", 1024, 1024, 1024,[F32])], axis=1)
    me_chip = 2 * lax.axis_index("x") + lax.axis_index("y")
    early_mine = _chip_sum(lax.dynamic_index_in_dim(early_pair, me_chip, 0, keepdims=False),
                           _scatter_wait(scatter_flight, g_in, "early"), early_tm)
    join_flight = _join_start(early_mine)
    in_cols = g_in.shape[1] // N_CHIPS
    late_tm = 256
    late_swap = _swap_start(jnp.stack([g_in[:, s * in_cols:(s + 1) * in_cols] for s in range(N_CHIPS)]), "late")
    d_a = _mm("d_in_qkv", dz_qkv, w_qkv, "nt", 1024,1024, 1280, [F32], after=(late_swap.token, join_flight.token))
    late_g, late_got = _swap_wait(late_swap, d_a, "late")
    late_pair, late_pair_b = _pair_sum(late_g, late_got, late_tm)
    late_scatter = _scatter_start(late_pair_b, "late")
    def first_bwd(dn_gates, dn_qkv, dh_out, h, g):
        dh, dg = _rms_bwd(dn_gates + dn_qkv, h, g)
        return dh_out + dh, dg

    grad_x, dg_mix = _mm("d_in_gates", dz_gates, w_gates, "nt", 512, 1024, 1024, [F32], epilogue=first_bwd,
                         extras=(d_a, dh1, xs), consts=(gm,), sums=[D_MODEL], after=(late_scatter.token,))
    early_shard = _join_wait(join_flight, grad_x)

    n_rpb = rpb.size
    rpb_rows = 4
    small = jnp.concatenate([
        dg_mix, dg_mlp, dg_ple, dg_final,
        jnp.pad(d_rpb.reshape(-1), (0, rpb_rows * D_MODEL - n_rpb)).reshape(rpb_rows, D_MODEL),
        jnp.pad(loss_part, ((0, 0), (0, D_MODEL - loss_part.shape[1]))),
        jnp.zeros((SMALL_ROWS - 5 - rpb_rows, D_MODEL), F32)], axis=0)
    out = {"grad": {}, "delta": {}, "new_m": {}, "new_v": {}}

    def update(n, g, row0):
        res = _adamw("adamw_" + n, g, row0, params[n], params["m_" + n], params["v_" + n])
        for kind, t in zip(("grad", "delta", "new_m", "new_v"), res, strict=True):
            out[kind][n] = t

    row0 = 0
    for n in REDUCE_EARLY:
        rows, cols = early_shapes[n]
        n_rows = rows * cols // PACK_W
        if cols == PACK_W:
            update(n, early_shard, row0)
        else:
            update(n, early_shard[row0:row0 + n_rows].reshape(rows, cols), 0)
        row0 += n_rows
    late_others = _scatter_wait(late_scatter, out["new_v"][REDUCE_EARLY[-1]], "late")
    late_mine = _chip_sum(lax.dynamic_index_in_dim(late_pair, me_chip, 0, keepdims=False), late_others, late_tm)
    small = _allreduce_small(small, after=(late_mine,))
    update("w_in", _join_halves(late_mine), 0)
    loss = small[4 + rpb_rows, 0]

    def small_pack(a0, a1, a2, a3, r):
        return jnp.concatenate([a0.reshape(1, -1), a1.reshape(1, -1), a2.reshape(1, -1), a3.reshape(1, -1),
                                jnp.pad(r.reshape(-1), (0, rpb_rows * D_MODEL - n_rpb)).reshape(rpb_rows, D_MODEL)], axis=0)

    small_res = _adamw("adamw_small", small, 0, small_pack(g_mix, g_mlp, g_ple, g_final, rpb)[None],
                       small_pack(m_g_mix, m_g_mlp, m_g_ple, m_g_final, m_rpb)[None],
                       small_pack(v_g_mix, v_g_mlp, v_g_ple, v_g_final, v_rpb)[None])

    def small_unpack(t):
        return {"g_mix": t[0].reshape(g_mix.shape), "g_mlp": t[1].reshape(g_mlp.shape), "g_ple": t[2].reshape(g_ple.shape),
                "g_final": t[3].reshape(g_final.shape), "rpb": t[4:].reshape(-1)[:n_rpb].reshape(rpb.shape)}

    for kind, t in zip(("grad", "delta", "new_m", "new_v"), small_res, strict=True):
        out[kind].update(small_unpack(t[0]))

    order = ["g_mix", "w_in", "rpb", "w_branch_na", "w_branch_dil", "w_out", "g_mlp", "w_up", "w_down", "g_ple",
             "w_ple_gate", "w_ple_proj", "g_final"]
    return (loss, grad_x[None], *[out["grad"][n] for n in order], *[out["delta"][n] for n in order],
            *[out["new_m"][n] for n in order], *[out["new_v"][n] for n in order])
```

```python
import functools
from typing import NamedTuple

import jax
import jax.numpy as jnp
from jax import lax
from jax.experimental import pallas as pl
from jax.experimental.pallas import tpu as pltpu

BF = jnp.bfloat16
F32 = jnp.float32
MESH = pl.DeviceIdType.MESH
ANY = pl.BlockSpec(memory_space=pl.ANY)

V7X_VMEM_BYTES = 64 * 1024 * 1024
VMEM_LIMIT = V7X_VMEM_BYTES - 16 * 1024 * 1024

D_MODEL = 1024
HEAD_DIM = 64
GRID_W = 64
NA_HEADS = 8
NA_WIN_ROWS = 8
NA_WIN_COLS = 16
NA_WIDTH = NA_HEADS * HEAD_DIM
DIL_GROUPS = ((128, 1), (512, 4), (2048, 16))
DIL_HPG = 4
DIL_HEADS = DIL_HPG * len(DIL_GROUPS)
DIL_WIDTH = DIL_HEADS * HEAD_DIM
DIL_OUT_WIDTH = DIL_HPG * HEAD_DIM
DIL_RADIUS = 64
QKV_WIDTH = 3 * NA_WIDTH + 3 * DIL_WIDTH
D_FF = 4 * D_MODEL
PLE_DIM = 256
ROPE_THETA = 10000.0
RMS_EPS = 1e-6
NEG_INF = -1e30
Q_SCALE = HEAD_DIM ** -0.5

ADAM_LR = 0.001
ADAM_B1 = 0.9
ADAM_B2 = 0.999
ADAM_EPS = 1e-08
ADAM_WD = 0.01
ADAM_STEP = 10

N_CHIPS = 4
N_DEV = 8
PACK_W = 1024
BIG = ("w_in", "w_branch_na", "w_branch_dil", "w_out", "w_up", "w_down", "w_ple_gate", "w_ple_proj")
GATHER_MIX = ("w_branch_na", "w_branch_dil", "w_out")
GATHER_MLP = ("w_up", "w_down", "w_ple_gate", "w_ple_proj")
REDUCE_EARLY = ("w_up", "w_down", "w_ple_gate", "w_out", "w_ple_proj", "w_branch_na", "w_branch_dil")
SMALL_ROWS = 16


def _cparams(sem=None):
    return pltpu.CompilerParams(dimension_semantics=sem, vmem_limit_bytes=VMEM_LIMIT)


def _mm(name, a, b, mode, tm, tn, tk, out_dtypes, epilogue=None, extras=(), consts=(), sums=(), after=(), into=None):
    if mode == "nn":
        (M, K), N = a.shape, b.shape[1]
    elif mode == "nt":
        (M, K), N = a.shape, b.shape[0]
    else:
        (K, M), N = a.shape, b.shape[1]
    tm, tn, tk = min(tm, M), min(tn, N), min(tk, K)
    assert M % tm == 0 and N % tn == 0 and K % tk == 0, (name, M, N, K, tm, tn, tk)
    if mode == "nn":
        a_spec = pl.BlockSpec((tm, tk), lambda i, j, k: (i, k))
        b_spec = pl.BlockSpec((tk, tn), lambda i, j, k: (k, j))
        dims = (((1,), (0,)), ((), ()))
    elif mode == "nt":
        a_spec = pl.BlockSpec((tm, tk), lambda i, j, k: (i, k))
        b_spec = pl.BlockSpec((tn, tk), lambda i, j, k: (j, k))
        dims = (((1,), (1,)), ((), ()))
    else:
        a_spec = pl.BlockSpec((tk, tm), lambda i, j, k: (k, i))
        b_spec = pl.BlockSpec((tk, tn), lambda i, j, k: (k, j))
        dims = (((0,), (0,)), ((), ()))
    nk = K // tk
    n_extra, n_const, n_out, n_sum = len(extras), len(consts), len(out_dtypes), len(sums)
    tile = pl.BlockSpec((tm, tn), lambda i, j, k: (i, j))
    assert not sums or tn == N, "row sums need whole rows in a tile"

    n_after = len(after)

    def body(a_ref, b_ref, *rest):
        extra_refs, rest = rest[:n_extra + n_const], rest[n_extra + n_const + n_after:]
        out_refs, sum_refs, acc = rest[:n_out], rest[n_out:n_out + n_sum], rest[-1]
        i, k = pl.program_id(0), pl.program_id(2)
        def product():
            return lax.dot_general(a_ref[...].astype(BF), b_ref[...].astype(BF), dims, preferred_element_type=F32)

        if nk > 1:
            @pl.when(k == 0)
            def _():
                acc[...] = jnp.zeros_like(acc)

            acc[...] += product()

        @pl.when(k == nk - 1)
        def _():
            total = product() if nk == 1 else acc[...]
            outs = (total,) if epilogue is None else epilogue(total, *[e[...] for e in extra_refs])
            for o_ref, val in zip(out_refs, outs[:n_out], strict=True):
                o_ref[...] = val.astype(o_ref.dtype).reshape(o_ref.shape)
            for s_ref, val in zip(sum_refs, outs[n_out:], strict=True):
                @pl.when(i == 0)
                def _():
                    s_ref[...] = val

                @pl.when(i != 0)
                def _():
                    s_ref[...] += val

    out_specs = [tile] * n_out + [pl.BlockSpec((1, c), lambda i, j, k: (0, 0)) for c in sums]
    out_shape = [jax.ShapeDtypeStruct((M, N), dt) for dt in out_dtypes] + [jax.ShapeDtypeStruct((1, c), F32) for c in sums]
    operands, aliases = [a, b, *extras, *consts, *after], {}
    in_specs = ([a_spec, b_spec] + [tile] * n_extra
                + [pl.BlockSpec(c.shape, functools.partial(lambda nd, i, j, k: (0,) * nd, c.ndim)) for c in consts] + [ANY] * n_after)
    if into is not None:
        assert n_out == 1
        target, block, index = into
        out_specs = [pl.BlockSpec(block, lambda i, j, k: index(i, j))]
        out_shape = [jax.ShapeDtypeStruct(target.shape, target.dtype)]
        if not isinstance(target, jax.ShapeDtypeStruct):
            aliases = {len(operands): 0}
            operands.append(target)
            in_specs.append(ANY)
            n_after += 1

    outs = pl.pallas_call(
        body, name=name, grid=(M // tm, N // tn, nk),
        in_specs=in_specs, out_specs=out_specs, out_shape=out_shape,
        scratch_shapes=[pltpu.VMEM((tm, tn) if nk > 1 else (8, 128), F32)], input_output_aliases=aliases,
        compiler_params=_cparams(("arbitrary",) * 3 if sums else ("parallel", "parallel", "arbitrary")),
    )(*operands)
    return outs[0] if len(outs) == 1 else outs


def _row(arr, tm, col_block=None, width=None):
    width = arr.shape[1] if width is None else width
    cb = 0 if col_block is None else col_block
    return arr, pl.BlockSpec((tm, width), lambda i: (i, cb))


def _full(arr):
    nd = arr.ndim
    return arr, pl.BlockSpec(arr.shape, lambda i: (0,) * nd)


def _rowwise(name, body, T, tm, ins, outs, sums=(), after=()):
    n_in, n_out, n_sum, n_after = len(ins), len(outs), len(sums), len(after)

    def kern(*refs):
        in_refs, refs = refs[:n_in], refs[n_in + n_after:]
        out_refs, sum_refs = refs[:n_out], refs[n_out:]
        res = body(*[r[...] for r in in_refs])
        res = res if isinstance(res, tuple) else (res,)
        for o_ref, val in zip(out_refs, res[:n_out], strict=True):
            o_ref[...] = val.astype(o_ref.dtype)
        if n_sum:
            @pl.when(pl.program_id(0) == 0)
            def _():
                for s_ref in sum_refs:
                    s_ref[...] = jnp.zeros_like(s_ref)

            for s_ref, val in zip(sum_refs, res[n_out:], strict=True):
                s_ref[...] += val

    res = pl.pallas_call(
        kern, name=name, grid=(T // tm,),
        in_specs=[spec for _, spec in ins] + [ANY] * n_after,
        out_specs=[pl.BlockSpec((tm, c), lambda i: (i, 0)) for c, _ in outs]
        + [pl.BlockSpec((1, c), lambda i: (0, 0)) for c in sums],
        out_shape=[jax.ShapeDtypeStruct((T, c), dt) for c, dt in outs]
        + [jax.ShapeDtypeStruct((1, c), F32) for c in sums],
        compiler_params=_cparams(("arbitrary",)),
    )(*[a for a, _ in ins], *after)
    return res[0] if len(res) == 1 else res


def _sigmoid(x):
    return 1.0 / (1.0 + jnp.exp(-x))


def _rms(h):
    return lax.rsqrt(jnp.mean(h * h, axis=-1, keepdims=True) + RMS_EPS)


def _rms_bwd(dy, h, g):
    r = _rms(h)
    n = h * r
    dn = dy * g
    dh = r * (dn - n * jnp.mean(dn * n, axis=-1, keepdims=True))
    return dh, jnp.sum(dy * n, axis=0, keepdims=True)


def _rope(x, cos2, sin_signed):
    lane = lax.broadcasted_iota(jnp.int32, x.shape, 1)
    swapped = jnp.where((lane % HEAD_DIM) < HEAD_DIM // 2, pltpu.roll(x, 128 - HEAD_DIM // 2, 1), pltpu.roll(x, HEAD_DIM // 2, 1))
    return x * cos2 + swapped * sin_signed


NA_KEYS = NA_WIN_ROWS * GRID_W
NA_BASES = 8


def _na_row_geometry(r, rows):
    first = jnp.clip(r - NA_WIN_ROWS // 2, 0, rows - NA_WIN_ROWS)
    base = first - r + (NA_WIN_ROWS - 1)
    return pl.multiple_of(first * GRID_W, GRID_W), base


NA_ROWS_PER_STEP = 8
NA_BWD_ROWS_PER_STEP = 8


def _softmax_rows(s):
    p = jnp.exp(s - jnp.max(s, axis=-1, keepdims=True))
    return p / jnp.sum(p, axis=-1, keepdims=True)


def _na_probs(q, kw, bias):
    return _softmax_rows(lax.dot_general(q, kw, (((1,), (1,)), ((), ())), preferred_element_type=F32) + bias)


def _split_pair(t):
    first = lax.broadcasted_iota(jnp.int32, t.shape, 1) < HEAD_DIM
    zero = jnp.zeros_like(t)
    return jnp.where(first, t, zero), jnp.where(first, zero, t)


def _join_pair(a, b):
    return jnp.where(lax.broadcasted_iota(jnp.int32, a.shape, 1) < HEAD_DIM, a, b)


_NT = (((1,), (1,)), ((), ()))
_TN = (((0,), (0,)), ((), ()))


def _na_fwd(qkv, tab):
    T = qkv.shape[0]
    rows = T // GRID_W
    n_pairs = NA_WIDTH // 128

    def body(q_ref, k_ref, v_ref, tab_ref, y_ref):
        def step(it, carry):
            geo = [_na_row_geometry(it * NA_ROWS_PER_STEP + u, rows) for u in range(NA_ROWS_PER_STEP)]
            q0s = [pl.multiple_of((it * NA_ROWS_PER_STEP + u) * GRID_W, GRID_W) for u in range(NA_ROWS_PER_STEP)]
            ss = [lax.dot_general(jnp.concatenate(_split_pair(q_ref[pl.ds(q0, GRID_W), :] * Q_SCALE), axis=0),
                                  k_ref[pl.ds(k0, NA_KEYS), :], _NT, preferred_element_type=F32)
                  for q0, (k0, _) in zip(q0s, geo)]
            ps = [_softmax_rows(s + jnp.concatenate([tab_ref[0, base], tab_ref[1, base]], axis=0)) for s, (_, base) in zip(ss, geo)]
            ys = [jnp.dot(p.astype(BF), v_ref[pl.ds(k0, NA_KEYS), :], preferred_element_type=F32) for p, (k0, _) in zip(ps, geo)]
            for q0, y2 in zip(q0s, ys):
                y_ref[pl.ds(q0, GRID_W), :] = _join_pair(y2[:GRID_W], y2[GRID_W:]).astype(y_ref.dtype)
            return carry

        lax.fori_loop(0, rows // NA_ROWS_PER_STEP, step, 0)

    def cols(first):
        return pl.BlockSpec((T, 128), lambda j: (0, first + j))

    return pl.pallas_call(
        body, name="na_fwd", grid=(n_pairs,),
        in_specs=[cols(0), cols(n_pairs), cols(2 * n_pairs), pl.BlockSpec((2, NA_BASES, GRID_W, NA_KEYS), lambda j: (j, 0, 0, 0))],
        out_specs=cols(0), out_shape=jax.ShapeDtypeStruct((T, NA_WIDTH), BF),
        compiler_params=_cparams(("parallel",)),
    )(qkv, qkv, qkv, tab)


def _na_bwd(qkv, tab, do):
    T = qkv.shape[0]
    rows = T // GRID_W
    n_pairs = NA_WIDTH // 128

    def body(q_ref, k_ref, v_ref, tab_ref, do_ref, dq_ref, dk_ref, dv_ref, dtab_ref):
        dk_ref[...] = jnp.zeros_like(dk_ref)
        dv_ref[...] = jnp.zeros_like(dv_ref)
        dtab_ref[...] = jnp.zeros_like(dtab_ref)

        def step(it, carry):
            U = NA_BWD_ROWS_PER_STEP
            geo = [_na_row_geometry(it * U + u, rows) for u in range(U)]
            q0s = [pl.multiple_of((it * U + u) * GRID_W, GRID_W) for u in range(U)]
            q2s = [jnp.concatenate(_split_pair(q_ref[pl.ds(q0, GRID_W), :] * Q_SCALE), axis=0) for q0 in q0s]
            do2s = [jnp.concatenate(_split_pair(do_ref[pl.ds(q0, GRID_W), :]), axis=0) for q0 in q0s]
            ss = [lax.dot_general(q2, k_ref[pl.ds(k0, NA_KEYS), :], _NT, preferred_element_type=F32) for q2, (k0, _) in zip(q2s, geo)]
            dps = [lax.dot_general(do2, v_ref[pl.ds(k0, NA_KEYS), :], _NT, preferred_element_type=F32) for do2, (k0, _) in zip(do2s, geo)]
            ps = [_softmax_rows(s + jnp.concatenate([tab_ref[0, base], tab_ref[1, base]], axis=0)) for s, (_, base) in zip(ss, geo)]
            dss = [p * (dp - jnp.sum(dp * p, axis=-1, keepdims=True)) for p, dp in zip(ps, dps)]
            dvs = [lax.dot_general(p.astype(BF), do2, _TN, preferred_element_type=F32) for p, do2 in zip(ps, do2s)]
            dsbs = [ds.astype(BF) for ds in dss]
            dqs = [jnp.dot(dsb, k_ref[pl.ds(k0, NA_KEYS), :], preferred_element_type=F32) for dsb, (k0, _) in zip(dsbs, geo)]
            dks = [lax.dot_general(dsb, q2, _TN, preferred_element_type=F32) for dsb, q2 in zip(dsbs, q2s)]
            for u in range(U):
                k0, base = geo[u]
                dtab_ref[0, base] += dss[u][:GRID_W]
                dtab_ref[1, base] += dss[u][GRID_W:]
                dq_ref[pl.ds(q0s[u], GRID_W), :] = _join_pair(dqs[u][:GRID_W], dqs[u][GRID_W:])
                dk_ref[pl.ds(k0, NA_KEYS), :] += dks[u]
                dv_ref[pl.ds(k0, NA_KEYS), :] += dvs[u]
            return carry

        lax.fori_loop(0, rows // NA_BWD_ROWS_PER_STEP, step, 0)

    def cols(first):
        return pl.BlockSpec((T, 128), lambda j: (0, first + j))

    tabs = pl.BlockSpec((2, NA_BASES, GRID_W, NA_KEYS), lambda j: (j, 0, 0, 0))
    wide = jax.ShapeDtypeStruct((T, NA_WIDTH), F32)
    return pl.pallas_call(
        body, name="na_bwd", grid=(n_pairs,),
        in_specs=[cols(0), cols(n_pairs), cols(2 * n_pairs), tabs, cols(0)],
        out_specs=[cols(0), cols(0), cols(0), tabs],
        out_shape=[wide, wide, wide, jax.ShapeDtypeStruct((NA_HEADS, NA_BASES, GRID_W, NA_KEYS), F32)],
        compiler_params=_cparams(("parallel",)),
    )(qkv, qkv, qkv, tab, do)


def _na_bias_table(rpb):
    H, n_rows, n_cols = rpb.shape

    def body(r_ref, tab_ref):
        q = lax.broadcasted_iota(jnp.int32, (GRID_W, 128), 0)
        kc = lax.broadcasted_iota(jnp.int32, (GRID_W, 128), 1)
        first = jnp.clip(q - NA_WIN_COLS // 2, 0, GRID_W - NA_WIN_COLS)
        valid = (kc >= first) & (kc < first + NA_WIN_COLS)
        toeplitz = []
        for ro in range(n_rows):
            row = jnp.broadcast_to(r_ref[pl.ds(ro, 1), :], (GRID_W, 128))
            shifted = pltpu.roll(pltpu.roll(row, 128 - (NA_WIN_COLS - 1), 1), 0, 1, stride=1, stride_axis=0)
            toeplitz.append(jnp.where(valid, shifted, NEG_INF))
        for base in range(NA_BASES):
            for j in range(NA_WIN_ROWS // 2):
                even, odd = toeplitz[base + 2 * j], toeplitz[base + 2 * j + 1]
                tab_ref[base, :, pl.ds(j * 128, 128)] = jnp.where(kc < GRID_W, even, pltpu.roll(odd, GRID_W, 1))

    padded = jnp.pad(rpb, ((0, 0), (0, 16 - n_rows), (0, 128 - n_cols)))
    return pl.pallas_call(
        body, name="na_bias_table", grid=(H,),
        in_specs=[pl.BlockSpec((None, 16, 128), lambda h: (h, 0, 0))],
        out_specs=pl.BlockSpec((None, NA_BASES, GRID_W, NA_KEYS), lambda h: (h, 0, 0, 0)),
        out_shape=jax.ShapeDtypeStruct((H, NA_BASES, GRID_W, NA_KEYS), F32),
        compiler_params=_cparams(("parallel",)),
    )(padded)


def _na_rpb_grad(dtab):
    H = dtab.shape[0]
    n_rows = 2 * NA_WIN_ROWS - 1
    n_cols = 2 * NA_WIN_COLS - 1

    def body(d_ref, o_ref):
        lane = lax.broadcasted_iota(jnp.int32, (GRID_W, 128), 1)
        low = lane < GRID_W
        out_rows = []
        for ro in range(n_rows):
            acc = jnp.zeros((GRID_W, 128), F32)
            for base in range(NA_BASES):
                i = ro - base
                if not 0 <= i < NA_WIN_ROWS:
                    continue
                pair = d_ref[base, :, pl.ds((i // 2) * 128, 128)]
                if i % 2:
                    pair = pltpu.roll(pair, GRID_W, 1)
                acc = acc + jnp.where(low, pair, 0.0)
            skew = pltpu.roll(acc, 0, 1, stride=1, stride_axis=0)
            diag = jnp.sum(skew, axis=0, keepdims=True)
            out_rows.append(pltpu.roll(jnp.broadcast_to(diag, (8, 128)), 128 - (GRID_W - NA_WIN_COLS), 1)[:1])
        out_rows.append(jnp.zeros((1, 128), F32))
        res = jnp.concatenate(out_rows, axis=0)
        o_ref[...] = jnp.where(lax.broadcasted_iota(jnp.int32, res.shape, 1) < n_cols, res, 0.0)

    return pl.pallas_call(
        body, name="na_rpb_grad", grid=(H,),
        in_specs=[pl.BlockSpec((None, NA_BASES, GRID_W, NA_KEYS), lambda h: (h, 0, 0, 0))],
        out_specs=pl.BlockSpec((None, n_rows + 1, 128), lambda h: (h, 0, 0)),
        out_shape=jax.ShapeDtypeStruct((H, n_rows + 1, 128), F32),
        compiler_params=_cparams(("parallel",)),
    )(jnp.flip(dtab, axis=2))


BAND_Q = 128
BAND_KEYS = BAND_Q + 2 * DIL_RADIUS


def _band_geometry(n, L):
    q0 = pl.multiple_of(n * BAND_Q, BAND_Q)
    k0 = pl.multiple_of(jnp.clip(q0 - DIL_RADIUS, 0, L - BAND_KEYS), DIL_RADIUS)
    qi = q0 + lax.broadcasted_iota(jnp.int32, (BAND_Q, BAND_KEYS), 0)
    kj = k0 + lax.broadcasted_iota(jnp.int32, (BAND_Q, BAND_KEYS), 1)
    return q0, k0, jnp.abs(qi - kj) <= DIL_RADIUS


DIL_PAIRS = DIL_OUT_WIDTH // 128


def _residue_shape(dil, T, dtype):
    return jax.ShapeDtypeStruct((DIL_PAIRS, dil, T // dil, 128), dtype)


def _residue_tile(dil, tm):
    return pl.BlockSpec((DIL_PAIRS, dil, tm // dil, 128), lambda i: (0, 0, i, 0))


def _to_natural(ref, scratch, dil, tm):
    tiles = []
    for pair in range(DIL_PAIRS):
        if dil == 1:
            tiles.append(ref[pair, 0].astype(F32))
            continue
        for r in range(dil):
            scratch[pl.ds(r, tm // dil, stride=dil), :] = ref[pair, r].astype(F32)
        tiles.append(scratch[...])
    return tiles


def _from_natural(tile, scratch, ref, pair, dil, tm):
    if dil == 1:
        ref[pair, 0] = tile.astype(ref.dtype)
        return
    scratch[...] = tile
    for r in range(dil):
        ref[pair, r] = scratch[pl.ds(r, tm // dil, stride=dil), :].astype(ref.dtype)


def _band_specs(group, T):
    dil = DIL_GROUPS[group][1]
    L = T // dil
    assert L % BAND_Q == 0 and L >= BAND_KEYS, (T, dil)
    return L, (dil * DIL_PAIRS,), pl.BlockSpec((None, None, L, 128), lambda s: (s % DIL_PAIRS, s // DIL_PAIRS, 0, 0))


BAND_BLOCKS_PER_STEP = 4


def _band_softmax(s, valid):
    s = jnp.where(valid, s, NEG_INF)
    m = jnp.max(s, axis=-1, keepdims=True)
    p = jnp.exp(s - m)
    l = jnp.sum(p, axis=-1, keepdims=True)
    return p / l, m + jnp.log(l)


def _band_fwd(q, k, v, group):
    T = q.shape[1] * q.shape[2]
    L, grid, spec = _band_specs(group, T)
    U = min(BAND_BLOCKS_PER_STEP, L // BAND_Q)

    def body(q_ref, k_ref, v_ref, o_ref, lse_ref):
        def step(it, carry):
            geo = [_band_geometry(it * U + u, L) for u in range(U)]
            ss = [lax.dot_general(jnp.concatenate(_split_pair(q_ref[pl.ds(q0, BAND_Q), :]), axis=0),
                                  k_ref[pl.ds(k0, BAND_KEYS), :], _NT, preferred_element_type=F32) for q0, k0, _ in geo]
            pls = [_band_softmax(s, jnp.concatenate([valid, valid], axis=0)) for s, (_, _, valid) in zip(ss, geo)]
            os = [jnp.dot(p.astype(BF), v_ref[pl.ds(k0, BAND_KEYS), :], preferred_element_type=F32) for (p, _), (_, k0, _) in zip(pls, geo)]
            for (q0, _, _), o2, (_, lse) in zip(geo, os, pls):
                o_ref[pl.ds(q0, BAND_Q), :] = _join_pair(o2[:BAND_Q], o2[BAND_Q:])
                lse2 = jnp.broadcast_to(lse, (2 * BAND_Q, 128))
                lse_ref[pl.ds(q0, BAND_Q), :] = _join_pair(lse2[:BAND_Q], lse2[BAND_Q:])
            return carry

        lax.fori_loop(0, L // (BAND_Q * U), step, 0)

    res = _residue_shape(DIL_GROUPS[group][1], T, F32)
    return pl.pallas_call(
        body, name=f"band_fwd_g{group}", grid=grid,
        in_specs=[spec] * 3, out_specs=[spec] * 2, out_shape=[res, res],
        compiler_params=_cparams(("parallel",)),
    )(q, k, v)


def _band_bwd(q, k, v, do, dlse, group):
    T = q.shape[1] * q.shape[2]
    L, grid, spec = _band_specs(group, T)
    U = min(BAND_BLOCKS_PER_STEP, L // BAND_Q)

    def body(q_ref, k_ref, v_ref, do_ref, dlse_ref, dq_ref, dk_ref, dv_ref):
        dk_ref[...] = jnp.zeros_like(dk_ref)
        dv_ref[...] = jnp.zeros_like(dv_ref)

        def step(it, carry):
            geo = [_band_geometry(it * U + u, L) for u in range(U)]
            q2s = [jnp.concatenate(_split_pair(q_ref[pl.ds(q0, BAND_Q), :]), axis=0) for q0, _, _ in geo]
            do2s = [jnp.concatenate(_split_pair(do_ref[pl.ds(q0, BAND_Q), :]), axis=0) for q0, _, _ in geo]
            ss = [lax.dot_general(q2, k_ref[pl.ds(k0, BAND_KEYS), :], _NT, preferred_element_type=F32) for q2, (_, k0, _) in zip(q2s, geo)]
            dps = [lax.dot_general(do2, v_ref[pl.ds(k0, BAND_KEYS), :], _NT, preferred_element_type=F32) for do2, (_, k0, _) in zip(do2s, geo)]
            ps = [_band_softmax(s, jnp.concatenate([valid, valid], axis=0))[0] for s, (_, _, valid) in zip(ss, geo)]
            dss = []
            for p, dp, (q0, _, _) in zip(ps, dps, geo):
                dl = dlse_ref[pl.ds(q0, BAND_Q), :]
                dl2 = jnp.concatenate([dl[:, :1], dl[:, HEAD_DIM:HEAD_DIM + 1]], axis=0)
                dss.append(p * (dp - jnp.sum(dp * p, axis=-1, keepdims=True) + dl2))
            dvs = [lax.dot_general(p.astype(BF), do2, _TN, preferred_element_type=F32) for p, do2 in zip(ps, do2s)]
            dsbs = [ds.astype(BF) for ds in dss]
            dqs = [jnp.dot(dsb, k_ref[pl.ds(k0, BAND_KEYS), :], preferred_element_type=F32) for dsb, (_, k0, _) in zip(dsbs, geo)]
            dks = [lax.dot_general(dsb, q2, _TN, preferred_element_type=F32) for dsb, q2 in zip(dsbs, q2s)]
            for u, (q0, k0, _) in enumerate(geo):
                dq_ref[pl.ds(q0, BAND_Q), :] = _join_pair(dqs[u][:BAND_Q], dqs[u][BAND_Q:])
                dk_ref[pl.ds(k0, BAND_KEYS), :] += dks[u]
                dv_ref[pl.ds(k0, BAND_KEYS), :] += dvs[u]
            return carry

        lax.fori_loop(0, L // (BAND_Q * U), step, 0)

    res = _residue_shape(DIL_GROUPS[group][1], T, F32)
    return pl.pallas_call(
        body, name=f"band_bwd_g{group}", grid=grid,
        in_specs=[spec] * 5, out_specs=[spec] * 3, out_shape=[res] * 3,
        compiler_params=_cparams(("parallel",)),
    )(q, k, v, do, dlse)


def _head_sums(t):
    head = lax.broadcasted_iota(jnp.int32, t.shape, 1) // HEAD_DIM
    out = jnp.zeros_like(t)
    for h in range(t.shape[1] // HEAD_DIM):
        mine = head == h
        out = jnp.where(mine, jnp.sum(jnp.where(mine, t, 0.0), axis=-1, keepdims=True), out)
    return out


def _dil_merge_fwd(os, lses, T, tm):
    G = len(DIL_GROUPS)
    W = DIL_OUT_WIDTH
    dils = [d for _, d in DIL_GROUPS]

    def body(*refs):
        o_refs, lse_refs = refs[:G], refs[G:2 * G]
        y_ref, w_refs, on_refs, scratch = refs[2 * G], refs[2 * G + 1:3 * G + 1], refs[3 * G + 1:4 * G + 1], refs[-1]
        o = [jnp.concatenate(_to_natural(r, scratch, d, tm), axis=1) for r, d in zip(o_refs, dils)]
        ls = [jnp.concatenate(_to_natural(r, scratch, d, tm), axis=1) for r, d in zip(lse_refs, dils)]
        m = functools.reduce(jnp.maximum, ls)
        es = [jnp.exp(l - m) for l in ls]
        tot = functools.reduce(jnp.add, es)
        ws = [e / tot for e in es]
        y_ref[...] = functools.reduce(jnp.add, [w * t for w, t in zip(ws, o)]).astype(y_ref.dtype)
        for g in range(G):
            w_refs[g][...] = ws[g]
            on_refs[g][...] = o[g]

    nat = pl.BlockSpec((tm, W), lambda i: (i, 0))
    res = pl.pallas_call(
        body, name="dil_merge_fwd", grid=(T // tm,),
        in_specs=[_residue_tile(d, tm) for d in dils] * 2,
        out_specs=[nat] * (2 * G + 1),
        out_shape=[jax.ShapeDtypeStruct((T, W), BF)] + [jax.ShapeDtypeStruct((T, W), F32)] * (2 * G),
        scratch_shapes=[pltpu.VMEM((tm, 128), F32)],
        compiler_params=_cparams(("parallel",)),
    )(*os, *lses)
    return res[0], res[1:G + 1], res[G + 1:]


def _dil_merge_bwd(dy, os, ws, tm, after=()):
    G = len(DIL_GROUPS)
    T, W = dy.shape
    dils = [d for _, d in DIL_GROUPS]
    n_after = len(after)

    def body(*refs):
        dyt = refs[0][...]
        o, w = [r[...] for r in refs[1:G + 1]], [r[...] for r in refs[G + 1:2 * G + 1]]
        refs = refs[2 * G + 1 + n_after:]
        do_refs, dlse_refs, scratch = refs[:G], refs[G:2 * G], refs[-1]
        dws = [_head_sums(dyt * t) for t in o]
        mean = functools.reduce(jnp.add, [a * b for a, b in zip(w, dws)])
        for g, d in enumerate(dils):
            do, dlse = w[g] * dyt, w[g] * (dws[g] - mean)
            for pair in range(DIL_PAIRS):
                cols = slice(pair * 128, (pair + 1) * 128)
                _from_natural(do[:, cols], scratch, do_refs[g], pair, d, tm)
                _from_natural(dlse[:, cols], scratch, dlse_refs[g], pair, d, tm)

    nat = pl.BlockSpec((tm, W), lambda i: (i, 0))
    res = pl.pallas_call(
        body, name="dil_merge_bwd", grid=(T // tm,),
        in_specs=[nat] * (2 * G + 1) + [ANY] * n_after,
        out_specs=[_residue_tile(d, tm) for d in dils] * 2,
        out_shape=[_residue_shape(d, T, BF) for d in dils] + [_residue_shape(d, T, F32) for d in dils],
        scratch_shapes=[pltpu.VMEM((tm, 128), F32)],
        compiler_params=_cparams(("parallel",)),
    )(dy, *os, *ws, *after)
    return res[:G], res[G:]


def _qkv_prep(z, cos2, sin_signed, tm):
    T = z.shape[0]
    G = len(DIL_GROUPS)
    dils = [d for _, d in DIL_GROUPS]
    n_dil_blocks = 3 * DIL_WIDTH // 128

    def body(*refs):
        blocks = refs[:n_dil_blocks]
        cos_ref, sin_ref = refs[n_dil_blocks], refs[1 + n_dil_blocks]
        outs = refs[2 + n_dil_blocks:]
        for part in range(3):
            for g, d in enumerate(dils):
                out = outs[g * 3 + part]
                for pair in range(DIL_PAIRS):
                    blk = blocks[part * (DIL_WIDTH // 128) + g * DIL_PAIRS + pair]
                    for r in range(d):
                        rows = pl.ds(r, tm // d, stride=d) if d > 1 else slice(None)
                        x = blk[rows, :]
                        if part < 2:
                            x = _rope(x, cos_ref[rows, :], sin_ref[rows, :])
                        if part == 0:
                            x = x * Q_SCALE
                        out[pair, r] = x.astype(out.dtype)

    lane_block = [pl.BlockSpec((tm, 128), functools.partial(lambda b, i: (i, b), b)) for b in range(n_dil_blocks)]
    tab = pl.BlockSpec((tm, 128), lambda i: (i, 0))
    res = pl.pallas_call(
        body, name="qkv_prep", grid=(T // tm,),
        in_specs=lane_block + [tab, tab],
        out_specs=[_residue_tile(d, tm) for d in dils for _ in range(3)],
        out_shape=[_residue_shape(d, T, BF) for d in dils for _ in range(3)],
        compiler_params=_cparams(("parallel",)),
    )(*[z] * n_dil_blocks, cos2, sin_signed)
    return [res[3 * g:3 + 3 * g] for g in range(G)]


def _qkv_unprep(d_na, d_dil, cos2, sin_signed, tm, after=()):
    T = d_na[0].shape[0]
    G = len(DIL_GROUPS)
    dils = [d for _, d in DIL_GROUPS]
    n_after = len(after)

    def body(*refs):
        dq, dk, dv = (r[...] for r in refs[:3])
        res_refs = refs[3:3 + 3 * G]
        cs, sn = refs[3 + 3 * G][...], refs[4 + 3 * G][...]
        out, scratch = refs[5 + 3 * G + n_after], refs[-1]
        cols = [dq * Q_SCALE, dk, dv]
        for part in range(3):
            for g, d in enumerate(dils):
                for x in _to_natural(res_refs[g * 3 + part], scratch, d, tm):
                    if part < 2:
                        x = _rope(x, cs, -sn)
                    cols.append(x * Q_SCALE if part == 0 else x)
        out[...] = jnp.concatenate(cols, axis=1).astype(out.dtype)

    wide = pl.BlockSpec((tm, NA_WIDTH), lambda i: (i, 0))
    tab = pl.BlockSpec((tm, 128), lambda i: (i, 0))
    return pl.pallas_call(
        body, name="qkv_unprep", grid=(T // tm,),
        in_specs=[wide] * 3 + [_residue_tile(d, tm) for d in dils for _ in range(3)] + [tab, tab] + [ANY] * n_after,
        out_specs=pl.BlockSpec((tm, QKV_WIDTH), lambda i: (i, 0)),
        out_shape=jax.ShapeDtypeStruct((T, QKV_WIDTH), BF),
        scratch_shapes=[pltpu.VMEM((tm, 128), F32)],
        compiler_params=_cparams(("parallel",)),
    )(*d_na, *[t for g in range(G) for t in d_dil[g]], cos2, sin_signed, *after)


def _rope_tables(positions):
    half = HEAD_DIM // 2
    inv_freq = ROPE_THETA ** (-jnp.arange(half, dtype=F32) / half)
    ang = positions.astype(F32)[:, None] * inv_freq
    cos, sin = jnp.cos(ang), jnp.sin(ang)
    return jnp.tile(jnp.concatenate([cos, cos], axis=1), (1, 2)), jnp.tile(jnp.concatenate([-sin, sin], axis=1), (1, 2))


def _pack_rows(t):
    return t.reshape(-1, PACK_W)


def _me():
    return lax.axis_index("x"), lax.axis_index("y"), lax.axis_index("c")


def _other_chips(x, y):
    return [(1 - x, y), (x, 1 - y), (1 - x, 1 - y)]


def _pair_sum(g, got, tm):
    S, R, W = g.shape
    half = R // 2
    nb = half // tm

    def body(c_ref, g_ref, got_ref, o_ref, ob_ref):
        tot = g_ref[...] + got_ref[...]
        o_ref[...] = tot
        ob_ref[...] = tot.astype(ob_ref.dtype)

    tile = pl.BlockSpec((None, tm, W), lambda s, i, c_ref: (s, i, 0))
    return pl.pallas_call(
        body, name="pair_sum",
        grid_spec=pltpu.PrefetchScalarGridSpec(
            num_scalar_prefetch=1, grid=(S, nb),
            in_specs=[pl.BlockSpec((None, tm, W), lambda s, i, c_ref: (s, c_ref[0] * nb + i, 0)), tile],
            out_specs=[tile, tile]),
        out_shape=[jax.ShapeDtypeStruct((S, half, W), F32), jax.ShapeDtypeStruct((S, half, W), BF)],
        compiler_params=_cparams(("parallel", "parallel")),
    )(lax.axis_index("c").reshape(1).astype(jnp.int32), g, got)


def _chip_sum(own, others, tm):
    n, h, W = others.shape
    nb = h // tm

    def body(c_ref, own_ref, p_ref, o_ref):
        o_ref[...] = ((own_ref[...] + p_ref[0].astype(F32)) + p_ref[1].astype(F32)) + p_ref[2].astype(F32)

    return pl.pallas_call(
        body, name="chip_sum",
        grid_spec=pltpu.PrefetchScalarGridSpec(
            num_scalar_prefetch=1, grid=(nb,),
            in_specs=[pl.BlockSpec((tm, W), lambda i, c_ref: (i, 0)), pl.BlockSpec((n, tm, W), lambda i, c_ref: (0, i, 0))],
            out_specs=pl.BlockSpec((tm, W), lambda i, c_ref: (c_ref[0] * nb + i, 0))),
        out_shape=jax.ShapeDtypeStruct((2 * h, W), F32),
        compiler_params=_cparams(("parallel",)),
    )(lax.axis_index("c").reshape(1).astype(jnp.int32), own, others)


def _join_halves(shard):
    h = shard.shape[0] // 2

    def body(in_ref, out_ref, send_sem, recv_sem):
        x, y, c = _me()
        cp = pltpu.make_async_remote_copy(
            src_ref=in_ref.at[pl.ds(c * h, h), :], dst_ref=out_ref.at[pl.ds(c * h, h), :],
            send_sem=send_sem, recv_sem=recv_sem, device_id=(x, y, 1 - c), device_id_type=MESH)
        cp.start()
        pltpu.make_async_remote_copy(
            src_ref=in_ref.at[pl.ds(c * h, h), :], dst_ref=out_ref.at[pl.ds((1 - c) * h, h), :],
            send_sem=send_sem, recv_sem=recv_sem, device_id=(x, y, 1 - c), device_id_type=MESH).wait_recv()
        cp.wait_send()

    return pl.pallas_call(
        body, name="join_halves", in_specs=[ANY], out_specs=ANY,
        out_shape=jax.ShapeDtypeStruct(shard.shape, shard.dtype), input_output_aliases={0: 0},
        scratch_shapes=[pltpu.SemaphoreType.DMA, pltpu.SemaphoreType.DMA],
    )(shard)


def _allreduce_small(s, after=()):
    R, W = s.shape
    n_after = len(after)

    def body(s_ref, *rest):
        o_ref, buf, send_sems, recv_sems = rest[n_after:]
        x, y, c = _me()
        me = 4 * x + 2 * y + c
        buf[me] = s_ref[...]
        peers = [((x + fx) % 2, (y + fy) % 2, (c + fc) % 2) for fx in range(2) for fy in range(2) for fc in range(2)][1:]
        sends = [pltpu.make_async_remote_copy(
            src_ref=s_ref, dst_ref=buf.at[me], send_sem=send_sems.at[k], recv_sem=recv_sems.at[k],
            device_id=peer, device_id_type=MESH) for k, peer in enumerate(peers)]
        for cp in sends:
            cp.start()
        for k, peer in enumerate(peers):
            pltpu.make_async_remote_copy(
                src_ref=s_ref, dst_ref=buf.at[4 * peer[0] + 2 * peer[1] + peer[2]], send_sem=send_sems.at[k],
                recv_sem=recv_sems.at[k], device_id=peer, device_id_type=MESH).wait_recv()
        for cp in sends:
            cp.wait_send()
        total = buf[0]
        for d in range(1, N_DEV):
            total = total + buf[d]
        o_ref[...] = total

    return pl.pallas_call(
        body, name="allreduce_small",
        in_specs=[pl.BlockSpec(memory_space=pltpu.VMEM)] + [ANY] * n_after, out_specs=pl.BlockSpec(memory_space=pltpu.VMEM),
        out_shape=jax.ShapeDtypeStruct((R, W), F32),
        scratch_shapes=[pltpu.VMEM((N_DEV, R, W), F32), pltpu.SemaphoreType.DMA((N_DEV - 1,)), pltpu.SemaphoreType.DMA((N_DEV - 1,))],
    )(s, *after)


HBM_SPEC = pl.BlockSpec(memory_space=pltpu.HBM)
SEM_SPEC = pl.BlockSpec(memory_space=pltpu.SEMAPHORE)
DATAFLOW = pltpu.SideEffectType.DATAFLOW_SIDE_EFFECTING


class _InFlight(NamedTuple):
    sems: tuple
    src: jax.Array
    land: jax.Array
    token: jax.Array


def _split_start(name, src, land_shape, land_dtype, n, copies, after=()):
    n_after = len(after)

    def body(src_ref, land_ref, *rest):
        rest = rest[n_after:]
        sems, token = rest[:2 * n], rest[-1]
        for k, (s, d, peer) in enumerate(copies(src_ref, land_ref)):
            pltpu.make_async_remote_copy(src_ref=s, dst_ref=d, send_sem=sems[k], recv_sem=sems[n + k],
                                         device_id=peer, device_id_type=MESH).start()
        token[...] = jnp.zeros_like(token)

    outs = pl.pallas_call(
        body, name=name,
        out_shape=(*[pltpu.SemaphoreType.DMA(())] * (2 * n), pltpu.HBM(src.shape, src.dtype), pltpu.HBM(land_shape, land_dtype),
                   jax.ShapeDtypeStruct((8, 128), F32)),
        in_specs=(HBM_SPEC, HBM_SPEC, *[ANY] * n_after),
        out_specs=(*[SEM_SPEC] * (2 * n), HBM_SPEC, HBM_SPEC, pl.BlockSpec(memory_space=pltpu.VMEM)),
        input_output_aliases={0: 2 * n, 1: 2 * n + 1},
        compiler_params=pltpu.CompilerParams(has_side_effects=DATAFLOW),
    )(pltpu.with_memory_space_constraint(src, pltpu.HBM), pltpu.with_memory_space_constraint(lax.empty(land_shape, land_dtype), pltpu.HBM),
      *after)
    return _InFlight(tuple(outs[:2 * n]), outs[2 * n], outs[2 * n + 1], outs[2 * n + 2])


def _split_wait(name, flight, after, n, copies):
    def body(src_ref, land_ref, *rest):
        sems = rest[:2 * n]
        for k, (s, d, peer) in enumerate(copies(src_ref, land_ref)):
            cp = pltpu.make_async_remote_copy(src_ref=s, dst_ref=d, send_sem=sems[k], recv_sem=sems[n + k],
                                              device_id=peer, device_id_type=MESH)
            cp.wait_send()
            cp.wait_recv()

    return pl.pallas_call(
        body, name=name,
        out_shape=(pltpu.HBM(flight.src.shape, flight.src.dtype), pltpu.HBM(flight.land.shape, flight.land.dtype)),
        in_specs=(HBM_SPEC, HBM_SPEC, *[SEM_SPEC] * (2 * n), ANY),
        out_specs=(HBM_SPEC, HBM_SPEC), input_output_aliases={0: 0, 1: 1},
        compiler_params=pltpu.CompilerParams(has_side_effects=DATAFLOW),
    )(flight.src, flight.land, *flight.sems, after)


def _gather_copies(src_ref, land_ref):
    x, y, c = _me()
    return [(src_ref, land_ref.at[2 * x + y], (*chip, c)) for chip in _other_chips(x, y)]


def _gather_start(packed, tag, after=()):
    return _split_start(f"gather_start_{tag}", packed, (N_CHIPS, *packed.shape), packed.dtype, 3, _gather_copies, after)


def _gather_wait(flight, after, tag):
    src, others = _split_wait(f"gather_wait_{tag}", flight, after, 3, _gather_copies)
    return lax.dynamic_update_slice(others, src[None], (2 * lax.axis_index("x") + lax.axis_index("y"), 0, 0))


def _across_copies(src_ref, land_ref):
    x, y, c = _me()
    half = src_ref.shape[0] // 2
    rows = pl.ds(c * half, half)
    return [(src_ref.at[rows, :], land_ref.at[2 * x + y, rows, :], (*chip, c)) for chip in _other_chips(x, y)]


def _to_sibling_copies(all_ref, unused_ref):
    x, y, c = _me()
    half = all_ref.shape[1] // 2
    places = [all_ref.at[2 * chip[0] + chip[1], pl.ds(c * half, half), :] for chip in _other_chips(x, y)]
    return [(place, place, (x, y, 1 - c)) for place in places]


def _gather_halves_start(shard):
    return _split_start("gather_in_across_start", shard, (N_CHIPS, *shard.shape), shard.dtype, 3, _across_copies)


def _gather_halves_finish(flight, after):
    shard, landed = _split_wait("gather_in_across_wait", flight, after, 3, _across_copies)
    forward = _split_start("gather_in_sibling_start", landed, (8, 128), landed.dtype, 3, _to_sibling_copies)
    others = _split_wait("gather_in_sibling_wait", forward, forward.token, 3, _to_sibling_copies)[0]
    return lax.dynamic_update_slice(others, shard[None], (2 * lax.axis_index("x") + lax.axis_index("y"), 0, 0))


def _swap_copies(src_ref, land_ref):
    x, y, c = _me()
    half = land_ref.shape[1]
    return [(src_ref.at[:, pl.ds((1 - c) * half, half), :], land_ref, (x, y, 1 - c))]


def _swap_start(g, tag):
    S, R, W = g.shape
    return _split_start(f"swap_halves_start_{tag}", g, (S, R // 2, W), g.dtype, 1, _swap_copies)


def _swap_wait(flight, after, tag):
    return _split_wait(f"swap_halves_wait_{tag}", flight, after, 1, _swap_copies)


def _scatter_copies(src_ref, land_ref):
    x, y, c = _me()
    return [(src_ref.at[2 * chip[0] + chip[1]], land_ref.at[j], (*chip, c)) for j, chip in enumerate(_other_chips(x, y))]


def _scatter_start(part, tag):
    S, h, W = part.shape
    return _split_start(f"scatter_chips_start_{tag}", part, (S - 1, h, W), part.dtype, 3, _scatter_copies)


def _scatter_wait(flight, after, tag):
    return _split_wait(f"scatter_chips_wait_{tag}", flight, after, 3, _scatter_copies)[1]


def _join_copies(shard_ref, unused_ref):
    x, y, c = _me()
    h = shard_ref.shape[0] // 2
    rows = shard_ref.at[pl.ds(c * h, h), :]
    return [(rows, rows, (x, y, 1 - c))]


def _join_start(shard):
    return _split_start("join_halves_start", shard, (8, 128), shard.dtype, 1, _join_copies)


def _join_wait(flight, after):
    return _split_wait("join_halves_wait", flight, after, 1, _join_copies)[0]


def _adamw(name, g, g_row0, w, m, v):
    _, R, C = w.shape
    tm = next(cand for cand in (256, 128, 64, 32, 16, 8) if R % cand == 0)
    assert g_row0 % tm == 0 and g.shape[1] == C

    def body(g_ref, w_ref, m_ref, v_ref, go_ref, d_ref, mo_ref, vo_ref):
        gt = g_ref[...]
        mt = ADAM_B1 * m_ref[...] + (1.0 - ADAM_B1) * gt
        vt = ADAM_B2 * v_ref[...] + (1.0 - ADAM_B2) * jnp.square(gt)
        m_hat = mt / (1.0 - ADAM_B1 ** ADAM_STEP)
        v_hat = vt / (1.0 - ADAM_B2 ** ADAM_STEP)
        go_ref[...] = gt
        d_ref[...] = -ADAM_LR * (m_hat / (jnp.sqrt(v_hat) + ADAM_EPS) + ADAM_WD * w_ref[...])
        mo_ref[...] = mt
        vo_ref[...] = vt

    state = pl.BlockSpec((None, tm, C), lambda i: (0, i, 0))
    return pl.pallas_call(
        body, name=name, grid=(R // tm,),
        in_specs=[pl.BlockSpec((tm, C), lambda i: (g_row0 // tm + i, 0)), state, state, state],
        out_specs=[state] * 4, out_shape=[jax.ShapeDtypeStruct((1, R, C), F32)] * 4,
        compiler_params=_cparams(("parallel",)),
    )(g, w, m, v)


def _unpack_weights(gathered, names):
    S = gathered.shape[0]
    shard_shapes = {"w_in": (D_MODEL, (QKV_WIDTH + 2 * D_MODEL) // S), "w_branch_na": (NA_WIDTH, D_MODEL // S),
                    "w_branch_dil": (DIL_OUT_WIDTH, D_MODEL // S), "w_out": (D_MODEL // S, D_MODEL),
                    "w_up": (D_MODEL, D_FF // S), "w_down": (D_FF // S, D_MODEL),
                    "w_ple_gate": (D_MODEL // S, D_MODEL), "w_ple_proj": (PLE_DIM, D_MODEL // S)}
    col_sharded = {"w_in", "w_branch_na", "w_branch_dil", "w_up", "w_ple_proj"}
    out, r0 = {}, 0
    for name in names:
        rows, cols = shard_shapes[name]
        n = rows * cols // PACK_W
        t = gathered[:, r0:r0 + n, :].reshape(S, rows, cols)
        r0 += n
        out[name] = t.transpose(1, 0, 2).reshape(rows, S * cols) if name in col_sharded else t.reshape(S * rows, cols)
    return out


def kernel(x, p, positions, g_mix, w_in, rpb, w_branch_na, w_branch_dil, w_out, g_mlp, w_up, w_down, g_ple, w_ple_gate, w_ple_proj, g_final, loss_target, m_g_mix, m_w_in, m_rpb, m_w_branch_na, m_w_branch_dil, m_w_out, m_g_mlp, m_w_up, m_w_down, m_g_ple, m_w_ple_gate, m_w_ple_proj, m_g_final, v_g_mix, v_w_in, v_rpb, v_w_branch_na, v_w_branch_dil, v_w_out, v_g_mlp, v_w_up, v_w_down, v_g_ple, v_w_ple_gate, v_w_ple_proj, v_g_final):
    shards = {"w_in": w_in[0], "w_branch_na": w_branch_na[0], "w_branch_dil": w_branch_dil[0], "w_out": w_out[0],
              "w_up": w_up[0], "w_down": w_down[0], "w_ple_gate": w_ple_gate[0], "w_ple_proj": w_ple_proj[0]}
    params = {"w_in": w_in, "w_branch_na": w_branch_na, "w_branch_dil": w_branch_dil, "w_out": w_out, "w_up": w_up,
              "w_down": w_down, "w_ple_gate": w_ple_gate, "w_ple_proj": w_ple_proj,
              "m_w_in": m_w_in, "m_w_branch_na": m_w_branch_na, "m_w_branch_dil": m_w_branch_dil, "m_w_out": m_w_out,
              "m_w_up": m_w_up, "m_w_down": m_w_down, "m_w_ple_gate": m_w_ple_gate, "m_w_ple_proj": m_w_ple_proj,
              "v_w_in": v_w_in, "v_w_branch_na": v_w_branch_na, "v_w_branch_dil": v_w_branch_dil, "v_w_out": v_w_out,
              "v_w_up": v_w_up, "v_w_down": v_w_down, "v_w_ple_gate": v_w_ple_gate, "v_w_ple_proj": v_w_ple_proj}

    xs, ps, tgt = x[0], p[0, 0], loss_target[0]
    T = xs.shape[0]
    TM = 512
    gm, gl, gp, gf = g_mix, g_mlp, g_ple, g_final.reshape(1, D_MODEL)

    across = _gather_halves_start(shards["w_in"].astype(BF))
    a = _rowwise("norm_mix", lambda h, g: h * _rms(h) * g, T, TM, [_row(xs, TM), _full(gm)], [(D_MODEL, BF)],
                 after=(across.token,))
    cos2, sin_signed = _rope_tables(positions[0])
    tab = _na_bias_table(rpb[0])
    w_in_all = _gather_halves_finish(across, a)
    W = {"w_in": w_in_all.transpose(1, 0, 2).reshape(D_MODEL, -1)}
    mix_flight = _gather_start(jnp.concatenate([_pack_rows(shards[n].astype(BF)) for n in GATHER_MIX], axis=0), "mix",
                               after=(w_in_all,))
    rest_flight = _gather_start(jnp.concatenate([_pack_rows(shards[n].astype(BF)) for n in GATHER_MLP], axis=0), "mlp",
                                after=(mix_flight.token,))
    w_qkv, w_gates = W["w_in"][:, :QKV_WIDTH], W["w_in"][:, QKV_WIDTH:]

    n3 = 3 * NA_WIDTH
    qkv = _mm("in_na", a, w_qkv[:, :n3], "nn", 1024, 768, 1024, [BF], after=(rest_flight.token,))
    z_dil = _mm("in_dil", a, w_qkv[:, n3:], "nn", 1024, 1152, 1024, [F32])
    z_gates = _mm("in_gates", a, w_gates, "nn", 1024,1024, 1024, [BF])

    dil_ops = _qkv_prep(z_dil, cos2, sin_signed, TM)
    y_na = _na_fwd(qkv, tab)
    band = [_band_fwd(*dil_ops[g], g) for g in range(len(DIL_GROUPS))]
    y_dil, w_grp, o_nat = _dil_merge_fwd([b[0] for b in band], [b[1] for b in band], T, TM)

    W.update(_unpack_weights(_gather_wait(mix_flight, y_dil, "mix"), GATHER_MIX))
    u_na = _mm("branch_na", y_na, W["w_branch_na"], "nn", 1024,1024, 512, [BF])
    u_dil = _mm("branch_dil", y_dil, W["w_branch_dil"], "nn", 1024,1024, 256, [BF])
    mixed = _rowwise(
        "gate_mix", lambda gn, gd, un, ud: _sigmoid(gn.astype(F32)) * un.astype(F32) + _sigmoid(gd.astype(F32)) * ud.astype(F32), T, TM,
        [_row(z_gates, TM, 0, D_MODEL), _row(z_gates, TM, 1, D_MODEL), _row(u_na, TM), _row(u_dil, TM)], [(D_MODEL, BF)])
    def add_norm(d, h, g):
        h = h + d
        return h, h * _rms(h) * g

    h1, cn = _mm("out_proj", mixed, W["w_out"], "nn", 512, 1024, 1024, [F32, BF], epilogue=add_norm, extras=(xs,), consts=(gl,))
    W.update(_unpack_weights(_gather_wait(rest_flight, cn, "mlp"), GATHER_MLP))
    up, act = _mm("mlp_up", cn, W["w_up"], "nn", 1024,1024, 1024, [BF, BF],
                  epilogue=lambda acc: (acc, jnp.square(jnp.maximum(acc, 0.0))))
    h2, en = _mm("mlp_down", act, W["w_down"], "nn", 1024, 1024, 1024, [F32, BF], epilogue=add_norm, extras=(h1,), consts=(gp,))
    pp = _mm("ple_proj", ps, W["w_ple_proj"], "nn", 1024,1024, 256, [F32])

    def head(gtt, h2t, ppt, tg, g):
        sg = _sigmoid(gtt)
        h3 = h2t + sg * ppt
        yo = h3 * _rms(h3) * g
        diff = yo - tg
        loss = 0.5 * jnp.sum(jnp.mean(jnp.square(diff), axis=-1, keepdims=True), axis=0, keepdims=True)
        dh3, dg = _rms_bwd(diff * (1.0 / D_MODEL), h3, g)
        return dh3, dh3 * ppt * sg * (1.0 - sg), dh3 * sg, jnp.broadcast_to(loss, (1, 128)), dg

    dh3, d_gt, d_pp, loss_part, dg_final = _mm(
        "ple_gate_loss_head", en, W["w_ple_gate"], "nn", 512, 1024, 1024, [F32, BF, BF], epilogue=head,
        extras=(h2, pp, tgt), consts=(gf,), sums=[128, D_MODEL])

    early_shapes = {n: shards[n].shape for n in REDUCE_EARLY}
    early_rows = sum(r * c for r, c in early_shapes.values()) // PACK_W
    shard_rows = D_MODEL // N_CHIPS
    early_buf = _mm("g_ple_gate", en, d_gt, "tn", 1024, 1024, 1024, [F32],
                    into=(jax.ShapeDtypeStruct((N_CHIPS, early_rows, PACK_W), F32), (N_CHIPS, shard_rows, PACK_W),
                          lambda i, j: (0, 2 * D_MODEL // shard_rows, 0)))
    g_ple_proj = _mm("g_ple_proj", ps, d_pp, "tn", 256, 1024, 1024,[F32])

    def add_norm_bwd(dn, dh_out, h, g):
        dh, dg = _rms_bwd(dn, h, g)
        dh = dh_out + dh
        return dh, dh, dg

    dh2, dh2_b, dg_ple = _mm("d_ple_gate", d_gt, W["w_ple_gate"], "nt", 512, 1024, 1024, [F32, BF],
                             epilogue=add_norm_bwd, extras=(dh3, h2), consts=(gp,), sums=[D_MODEL])
    d_up = _mm("d_mlp_down", dh2_b, W["w_down"], "nt", 1024,1024, 1024, [BF],
               epilogue=lambda acc, u: (acc * (2.0 * jnp.maximum(u.astype(F32), 0.0)),), extras=(up,))
    early_buf = _mm("g_mlp_down", act, dh2_b, "tn", 1024, 1024, 1024,[F32],
                    into=(early_buf, (None, D_MODEL, PACK_W), lambda i, j: (i, 1, 0)))
    early_buf = _mm("g_mlp_up", cn, d_up, "tn", 1024, 1024, 1024,[F32],
                    into=(early_buf, (None, D_MODEL, PACK_W), lambda i, j: (j, 0, 0)))
    dh1, dh1_b, dg_mlp = _mm("d_mlp_up", d_up, W["w_up"], "nt", 1024, 1024, 1024, [F32, BF], epilogue=add_norm_bwd,
                             extras=(dh2, h1), consts=(gl,), sums=[D_MODEL])
    d_mixed = _mm("d_out_proj", dh1_b, W["w_out"], "nt", 1024,1024, 1024, [F32])
    early_buf = _mm("g_out_proj", mixed, dh1_b, "tn", 1024, 1024, 1024, [F32],
                    into=(early_buf, (N_CHIPS, shard_rows, PACK_W), lambda i, j: (0, 2 * D_MODEL // shard_rows + 1, 0)))

    def gate_bwd(dm, gn, gd, un, ud):
        gn, gd, un, ud = (t.astype(F32) for t in (gn, gd, un, ud))
        sn, sd = _sigmoid(gn), _sigmoid(gd)
        return jnp.concatenate([dm * un * sn * (1.0 - sn), dm * ud * sd * (1.0 - sd)], axis=1), dm * sn, dm * sd

    dz_gates, d_u_na, d_u_dil = _rowwise(
        "gate_mix_bwd", gate_bwd, T, TM,
        [_row(d_mixed, TM), _row(z_gates, TM, 0, D_MODEL), _row(z_gates, TM, 1, D_MODEL), _row(u_na, TM), _row(u_dil, TM)],
        [(2 * D_MODEL, BF), (D_MODEL, BF), (D_MODEL, BF)])
    g_branch_na = _mm("g_branch_na", y_na, d_u_na, "tn", 1024, 1024, 1024,[F32])
    g_branch_dil = _mm("g_branch_dil", y_dil, d_u_dil, "tn", 256, 1024, 1024,[F32])
    small_rows = [jnp.concatenate([_pack_rows(g[:, s * shard_rows:(s + 1) * shard_rows]) for g in (g_ple_proj, g_branch_na, g_branch_dil)],
                                  axis=0) for s in range(N_CHIPS)]
    early_buf = lax.dynamic_update_slice(early_buf, jnp.stack(small_rows), (0, 2 * D_MODEL + 2 * shard_rows, 0))
    early_tm = early_rows // 4
    swap_flight = _swap_start(early_buf, "early")
    d_y_na = _mm("d_branch_na", d_u_na, W["w_branch_na"], "nt", 1024,512, 1024, [BF], after=(swap_flight.token,))
    d_y_dil = _mm("d_branch_dil", d_u_dil, W["w_branch_dil"], "nt", 1024,256, 1024, [F32])

    dqa, dka, dva, dtab = _na_bwd(qkv, tab, d_y_na)
    early_g, early_got = _swap_wait(swap_flight, dqa, "early")
    early_pair, early_pair_b = _pair_sum(early_g, early_got, early_tm)
    scatter_flight = _scatter_start(early_pair_b, "early")
    d_rpb = _na_rpb_grad(dtab)[:, :2 * NA_WIN_ROWS - 1, :2 * NA_WIN_COLS - 1]

    do_res, dlse_res = _dil_merge_bwd(d_y_dil, o_nat, w_grp, TM, after=(scatter_flight.token,))
    d_dil = [_band_bwd(*dil_ops[g], do_res[g], dlse_res[g], g) for g in range(len(DIL_GROUPS))]

    dz_qkv = _qkv_unprep((dqa, dka, dva), d_dil, cos2, sin_signed, TM)
    g_in = jnp.concatenate([
        _mm("g_in_qkv", a, dz_qkv, "tn", 1024, 1280, 1024,[F32]),
        _mm("g_in_gates", a, dz_gates, "tn", 1024, 1024, 1024,[F32])], axis=1)
    me_chip = 2 * lax.axis_index("x") + lax.axis_index("y")
    early_mine = _chip_sum(lax.dynamic_index_in_dim(early_pair, me_chip, 0, keepdims=False),
                           _scatter_wait(scatter_flight, g_in, "early"), early_tm)
    join_flight = _join_start(early_mine)
    in_cols = g_in.shape[1] // N_CHIPS
    late_tm = 256
    late_swap = _swap_start(jnp.stack([g_in[:, s * in_cols:(s + 1) * in_cols] for s in range(N_CHIPS)]), "late")
    d_a = _mm("d_in_qkv", dz_qkv, w_qkv, "nt", 1024,1024, 1280, [F32], after=(late_swap.token, join_flight.token))
    late_g, late_got = _swap_wait(late_swap, d_a, "late")
    late_pair, late_pair_b = _pair_sum(late_g, late_got, late_tm)
    late_scatter = _scatter_start(late_pair_b, "late")
    def first_bwd(dn_gates, dn_qkv, dh_out, h, g):
        dh, dg = _rms_bwd(dn_gates + dn_qkv, h, g)
        return dh_out + dh, dg

    grad_x, dg_mix = _mm("d_in_gates", dz_gates, w_gates, "nt", 512, 1024, 1024, [F32], epilogue=first_bwd,
                         extras=(d_a, dh1, xs), consts=(gm,), sums=[D_MODEL], after=(late_scatter.token,))
    early_shard = _join_wait(join_flight, grad_x)

    n_rpb = rpb.size
    rpb_rows = 4
    small = jnp.concatenate([
        dg_mix, dg_mlp, dg_ple, dg_final,
        jnp.pad(d_rpb.reshape(-1), (0, rpb_rows * D_MODEL - n_rpb)).reshape(rpb_rows, D_MODEL),
        jnp.pad(loss_part, ((0, 0), (0, D_MODEL - loss_part.shape[1]))),
        jnp.zeros((SMALL_ROWS - 5 - rpb_rows, D_MODEL), F32)], axis=0)
    out = {"grad": {}, "delta": {}, "new_m": {}, "new_v": {}}

    def update(n, g, row0):
        res = _adamw("adamw_" + n, g, row0, params[n], params["m_" + n], params["v_" + n])
        for kind, t in zip(("grad", "delta", "new_m", "new_v"), res, strict=True):
            out[kind][n] = t

    row0 = 0
    for n in REDUCE_EARLY:
        rows, cols = early_shapes[n]
        n_rows = rows * cols // PACK_W
        if cols == PACK_W:
            update(n, early_shard, row0)
        else:
            update(n, early_shard[row0:row0 + n_rows].reshape(rows, cols), 0)
        row0 += n_rows
    late_others = _scatter_wait(late_scatter, out["new_v"][REDUCE_EARLY[-1]], "late")
    late_mine = _chip_sum(lax.dynamic_index_in_dim(late_pair, me_chip, 0, keepdims=False), late_others, late_tm)
    small = _allreduce_small(small, after=(late_mine,))
    update("w_in", _join_halves(late_mine), 0)
    loss = small[4 + rpb_rows, 0]

    def small_pack(a0, a1, a2, a3, r):
        return jnp.concatenate([a0.reshape(1, -1), a1.reshape(1, -1), a2.reshape(1, -1), a3.reshape(1, -1),
                                jnp.pad(r.reshape(-1), (0, rpb_rows * D_MODEL - n_rpb)).reshape(rpb_rows, D_MODEL)], axis=0)

    small_res = _adamw("adamw_small", small, 0, small_pack(g_mix, g_mlp, g_ple, g_final, rpb)[None],
                       small_pack(m_g_mix, m_g_mlp, m_g_ple, m_g_final, m_rpb)[None],
                       small_pack(v_g_mix, v_g_mlp, v_g_ple, v_g_final, v_rpb)[None])

    def small_unpack(t):
        return {"g_mix": t[0].reshape(g_mix.shape), "g_mlp": t[1].reshape(g_mlp.shape), "g_ple": t[2].reshape(g_ple.shape),
                "g_final": t[3].reshape(g_final.shape), "rpb": t[4:].reshape(-1)[:n_rpb].reshape(rpb.shape)}

    for kind, t in zip(("grad", "delta", "new_m", "new_v"), small_res, strict=True):
        out[kind].update(small_unpack(t[0]))

    order = ["g_mix", "w_in", "rpb", "w_branch_na", "w_branch_dil", "w_out", "g_mlp", "w_up", "w_down", "g_ple",
             "w_ple_gate", "w_ple_proj", "g_final"]
    return (loss, grad_x[None], *[out["grad"][n] for n in order], *[out["delta"][n] for n in order],
            *[out["new_m"][n] for n in order], *[out["new_v"][n] for n in order])
```

```python
import functools
from typing import NamedTuple

import jax
import jax.numpy as jnp
from jax import lax
from jax.experimental import pallas as pl
from jax.experimental.pallas import tpu as pltpu

BF = jnp.bfloat16
F32 = jnp.float32
MESH = pl.DeviceIdType.MESH
ANY = pl.BlockSpec(memory_space=pl.ANY)

V7X_VMEM_BYTES = 64 * 1024 * 1024
VMEM_LIMIT = V7X_VMEM_BYTES - 16 * 1024 * 1024

D_MODEL = 1024
HEAD_DIM = 64
GRID_W = 64
NA_HEADS = 8
NA_WIN_ROWS = 8
NA_WIN_COLS = 16
NA_WIDTH = NA_HEADS * HEAD_DIM
DIL_GROUPS = ((128, 1), (512, 4), (2048, 16))
DIL_HPG = 4
DIL_HEADS = DIL_HPG * len(DIL_GROUPS)
DIL_WIDTH = DIL_HEADS * HEAD_DIM
DIL_OUT_WIDTH = DIL_HPG * HEAD_DIM
DIL_RADIUS = 64
QKV_WIDTH = 3 * NA_WIDTH + 3 * DIL_WIDTH
D_FF = 4 * D_MODEL
PLE_DIM = 256
ROPE_THETA = 10000.0
RMS_EPS = 1e-6
NEG_INF = -1e30
Q_SCALE = HEAD_DIM ** -0.5

ADAM_LR = 0.001
ADAM_B1 = 0.9
ADAM_B2 = 0.999
ADAM_EPS = 1e-08
ADAM_WD = 0.01
ADAM_STEP = 10

N_CHIPS = 4
N_DEV = 8
PACK_W = 1024
BIG = ("w_in", "w_branch_na", "w_branch_dil", "w_out", "w_up", "w_down", "w_ple_gate", "w_ple_proj")
GATHER_MIX = ("w_branch_na", "w_branch_dil", "w_out")
GATHER_MLP = ("w_up", "w_down", "w_ple_gate", "w_ple_proj")
REDUCE_EARLY = ("w_up", "w_down", "w_ple_gate", "w_out", "w_ple_proj", "w_branch_na", "w_branch_dil")
SMALL_ROWS = 16


def _cparams(sem=None):
    return pltpu.CompilerParams(dimension_semantics=sem, vmem_limit_bytes=VMEM_LIMIT)


def _mm(name, a, b, mode, tm, tn, tk, out_dtypes, epilogue=None, extras=(), consts=(), sums=(), after=(), into=None,
        b_view=None):
    if mode == "nn":
        (M, K), N = a.shape, b.shape[1]
    elif mode == "nt":
        (M, K), N = a.shape, b.shape[0]
    else:
        (K, M), N = a.shape, b.shape[1]
    if b_view is not None:
        N = b_view[0]
    tm, tn, tk = min(tm, M), min(tn, N), min(tk, K)
    assert M % tm == 0 and N % tn == 0 and K % tk == 0, (name, M, N, K, tm, tn, tk)
    if mode == "nn":
        a_spec = pl.BlockSpec((tm, tk), lambda i, j, k: (i, k))
        b_spec = pl.BlockSpec((tk, tn), lambda i, j, k: (k, j))
        dims = (((1,), (0,)), ((), ()))
    elif mode == "nt":
        a_spec = pl.BlockSpec((tm, tk), lambda i, j, k: (i, k))
        b_spec = pl.BlockSpec((tn, tk), lambda i, j, k: (j, k))
        dims = (((1,), (1,)), ((), ()))
    else:
        a_spec = pl.BlockSpec((tk, tm), lambda i, j, k: (k, i))
        b_spec = pl.BlockSpec((tk, tn), lambda i, j, k: (k, j))
        dims = (((0,), (0,)), ((), ()))
    if b_view is not None:
        b_spec = pl.BlockSpec(b_view[1], lambda i, j, k: b_view[2](j, k))
    nk = K // tk
    n_extra, n_const, n_out, n_sum = len(extras), len(consts), len(out_dtypes), len(sums)
    tile = pl.BlockSpec((tm, tn), lambda i, j, k: (i, j))
    assert not sums or tn == N, "row sums need whole rows in a tile"

    n_after = len(after)

    def body(a_ref, b_ref, *rest):
        extra_refs, rest = rest[:n_extra + n_const], rest[n_extra + n_const + n_after:]
        out_refs, sum_refs, acc = rest[:n_out], rest[n_out:n_out + n_sum], rest[-1]
        i, k = pl.program_id(0), pl.program_id(2)
        def product():
            return lax.dot_general(a_ref[...].astype(BF), b_ref[...].astype(BF), dims, preferred_element_type=F32)

        if nk > 1:
            @pl.when(k == 0)
            def _():
                acc[...] = jnp.zeros_like(acc)

            acc[...] += product()

        @pl.when(k == nk - 1)
        def _():
            total = product() if nk == 1 else acc[...]
            outs = (total,) if epilogue is None else epilogue(total, *[e[...] for e in extra_refs])
            for o_ref, val in zip(out_refs, outs[:n_out], strict=True):
                o_ref[...] = val.astype(o_ref.dtype).reshape(o_ref.shape)
            for s_ref, val in zip(sum_refs, outs[n_out:], strict=True):
                @pl.when(i == 0)
                def _():
                    s_ref[...] = val

                @pl.when(i != 0)
                def _():
                    s_ref[...] += val

    out_specs = [tile] * n_out + [pl.BlockSpec((1, c), lambda i, j, k: (0, 0)) for c in sums]
    out_shape = [jax.ShapeDtypeStruct((M, N), dt) for dt in out_dtypes] + [jax.ShapeDtypeStruct((1, c), F32) for c in sums]
    operands, aliases = [a, b, *extras, *consts, *after], {}
    in_specs = ([a_spec, b_spec] + [tile] * n_extra
                + [pl.BlockSpec(c.shape, functools.partial(lambda nd, i, j, k: (0,) * nd, c.ndim)) for c in consts] + [ANY] * n_after)
    if into is not None:
        assert n_out == 1
        target, block, index = into
        out_specs = [pl.BlockSpec(block, lambda i, j, k: index(i, j))]
        out_shape = [jax.ShapeDtypeStruct(target.shape, target.dtype)]
        if not isinstance(target, jax.ShapeDtypeStruct):
            aliases = {len(operands): 0}
            operands.append(target)
            in_specs.append(ANY)
            n_after += 1

    outs = pl.pallas_call(
        body, name=name, grid=(M // tm, N // tn, nk),
        in_specs=in_specs, out_specs=out_specs, out_shape=out_shape,
        scratch_shapes=[pltpu.VMEM((tm, tn) if nk > 1 else (8, 128), F32)], input_output_aliases=aliases,
        compiler_params=_cparams(("arbitrary",) * 3 if sums else ("parallel", "parallel", "arbitrary")),
    )(*operands)
    return outs[0] if len(outs) == 1 else outs


def _row(arr, tm, col_block=None, width=None):
    width = arr.shape[1] if width is None else width
    cb = 0 if col_block is None else col_block
    return arr, pl.BlockSpec((tm, width), lambda i: (i, cb))


def _full(arr):
    nd = arr.ndim
    return arr, pl.BlockSpec(arr.shape, lambda i: (0,) * nd)


def _rowwise(name, body, T, tm, ins, outs, sums=(), after=()):
    n_in, n_out, n_sum, n_after = len(ins), len(outs), len(sums), len(after)

    def kern(*refs):
        in_refs, refs = refs[:n_in], refs[n_in + n_after:]
        out_refs, sum_refs = refs[:n_out], refs[n_out:]
        res = body(*[r[...] for r in in_refs])
        res = res if isinstance(res, tuple) else (res,)
        for o_ref, val in zip(out_refs, res[:n_out], strict=True):
            o_ref[...] = val.astype(o_ref.dtype)
        if n_sum:
            @pl.when(pl.program_id(0) == 0)
            def _():
                for s_ref in sum_refs:
                    s_ref[...] = jnp.zeros_like(s_ref)

            for s_ref, val in zip(sum_refs, res[n_out:], strict=True):
                s_ref[...] += val

    res = pl.pallas_call(
        kern, name=name, grid=(T // tm,),
        in_specs=[spec for _, spec in ins] + [ANY] * n_after,
        out_specs=[pl.BlockSpec((tm, c), lambda i: (i, 0)) for c, _ in outs]
        + [pl.BlockSpec((1, c), lambda i: (0, 0)) for c in sums],
        out_shape=[jax.ShapeDtypeStruct((T, c), dt) for c, dt in outs]
        + [jax.ShapeDtypeStruct((1, c), F32) for c in sums],
        compiler_params=_cparams(("arbitrary",)),
    )(*[a for a, _ in ins], *after)
    return res[0] if len(res) == 1 else res


def _sigmoid(x):
    return 1.0 / (1.0 + jnp.exp(-x))


def _rms(h):
    return lax.rsqrt(jnp.mean(h * h, axis=-1, keepdims=True) + RMS_EPS)


def _rms_bwd(dy, h, g):
    r = _rms(h)
    n = h * r
    dn = dy * g
    dh = r * (dn - n * jnp.mean(dn * n, axis=-1, keepdims=True))
    return dh, jnp.sum(dy * n, axis=0, keepdims=True)


def _rope(x, cos2, sin_signed):
    lane = lax.broadcasted_iota(jnp.int32, x.shape, 1)
    swapped = jnp.where((lane % HEAD_DIM) < HEAD_DIM // 2, pltpu.roll(x, 128 - HEAD_DIM // 2, 1), pltpu.roll(x, HEAD_DIM // 2, 1))
    return x * cos2 + swapped * sin_signed


NA_KEYS = NA_WIN_ROWS * GRID_W
NA_BASES = 8


def _na_row_geometry(r, rows):
    first = jnp.clip(r - NA_WIN_ROWS // 2, 0, rows - NA_WIN_ROWS)
    base = first - r + (NA_WIN_ROWS - 1)
    return pl.multiple_of(first * GRID_W, GRID_W), base


NA_ROWS_PER_STEP = 8
NA_BWD_ROWS_PER_STEP = 8


def _softmax_rows(s):
    p = jnp.exp(s - jnp.max(s, axis=-1, keepdims=True))
    return p / jnp.sum(p, axis=-1, keepdims=True)


def _na_probs(q, kw, bias):
    return _softmax_rows(lax.dot_general(q, kw, (((1,), (1,)), ((), ())), preferred_element_type=F32) + bias)


def _split_pair(t):
    first = lax.broadcasted_iota(jnp.int32, t.shape, 1) < HEAD_DIM
    zero = jnp.zeros_like(t)
    return jnp.where(first, t, zero), jnp.where(first, zero, t)


def _join_pair(a, b):
    return jnp.where(lax.broadcasted_iota(jnp.int32, a.shape, 1) < HEAD_DIM, a, b)


_NT = (((1,), (1,)), ((), ()))
_TN = (((0,), (0,)), ((), ()))


def _na_fwd(qkv, tab):
    T = qkv.shape[0]
    rows = T // GRID_W
    n_pairs = NA_WIDTH // 128

    def body(q_ref, k_ref, v_ref, tab_ref, y_ref):
        def step(it, carry):
            geo = [_na_row_geometry(it * NA_ROWS_PER_STEP + u, rows) for u in range(NA_ROWS_PER_STEP)]
            q0s = [pl.multiple_of((it * NA_ROWS_PER_STEP + u) * GRID_W, GRID_W) for u in range(NA_ROWS_PER_STEP)]
            ss = [lax.dot_general(jnp.concatenate(_split_pair(q_ref[pl.ds(q0, GRID_W), :] * Q_SCALE), axis=0),
                                  k_ref[pl.ds(k0, NA_KEYS), :], _NT, preferred_element_type=F32)
                  for q0, (k0, _) in zip(q0s, geo)]
            ps = [_softmax_rows(s + jnp.concatenate([tab_ref[0, base], tab_ref[1, base]], axis=0)) for s, (_, base) in zip(ss, geo)]
            ys = [jnp.dot(p.astype(BF), v_ref[pl.ds(k0, NA_KEYS), :], preferred_element_type=F32) for p, (k0, _) in zip(ps, geo)]
            for q0, y2 in zip(q0s, ys):
                y_ref[pl.ds(q0, GRID_W), :] = _join_pair(y2[:GRID_W], y2[GRID_W:]).astype(y_ref.dtype)
            return carry

        lax.fori_loop(0, rows // NA_ROWS_PER_STEP, step, 0)

    def cols(first):
        return pl.BlockSpec((T, 128), lambda j: (0, first + j))

    return pl.pallas_call(
        body, name="na_fwd", grid=(n_pairs,),
        in_specs=[cols(0), cols(n_pairs), cols(2 * n_pairs), pl.BlockSpec((2, NA_BASES, GRID_W, NA_KEYS), lambda j: (j, 0, 0, 0))],
        out_specs=cols(0), out_shape=jax.ShapeDtypeStruct((T, NA_WIDTH), BF),
        compiler_params=_cparams(("parallel",)),
    )(qkv, qkv, qkv, tab)


def _na_bwd(qkv, tab, do):
    T = qkv.shape[0]
    rows = T // GRID_W
    n_pairs = NA_WIDTH // 128

    def body(q_ref, k_ref, v_ref, tab_ref, do_ref, dq_ref, dk_ref, dv_ref, dtab_ref):
        dk_ref[...] = jnp.zeros_like(dk_ref)
        dv_ref[...] = jnp.zeros_like(dv_ref)
        dtab_ref[...] = jnp.zeros_like(dtab_ref)

        def step(it, carry):
            U = NA_BWD_ROWS_PER_STEP
            geo = [_na_row_geometry(it * U + u, rows) for u in range(U)]
            q0s = [pl.multiple_of((it * U + u) * GRID_W, GRID_W) for u in range(U)]
            q2s = [jnp.concatenate(_split_pair(q_ref[pl.ds(q0, GRID_W), :] * Q_SCALE), axis=0) for q0 in q0s]
            do2s = [jnp.concatenate(_split_pair(do_ref[pl.ds(q0, GRID_W), :]), axis=0) for q0 in q0s]
            ss = [lax.dot_general(q2, k_ref[pl.ds(k0, NA_KEYS), :], _NT, preferred_element_type=F32) for q2, (k0, _) in zip(q2s, geo)]
            dps = [lax.dot_general(do2, v_ref[pl.ds(k0, NA_KEYS), :], _NT, preferred_element_type=F32) for do2, (k0, _) in zip(do2s, geo)]
            ps = [_softmax_rows(s + jnp.concatenate([tab_ref[0, base], tab_ref[1, base]], axis=0)) for s, (_, base) in zip(ss, geo)]
            dss = [p * (dp - jnp.sum(dp * p, axis=-1, keepdims=True)) for p, dp in zip(ps, dps)]
            dvs = [lax.dot_general(p.astype(BF), do2, _TN, preferred_element_type=F32) for p, do2 in zip(ps, do2s)]
            dsbs = [ds.astype(BF) for ds in dss]
            dqs = [jnp.dot(dsb, k_ref[pl.ds(k0, NA_KEYS), :], preferred_element_type=F32) for dsb, (k0, _) in zip(dsbs, geo)]
            dks = [lax.dot_general(dsb, q2, _TN, preferred_element_type=F32) for dsb, q2 in zip(dsbs, q2s)]
            for u in range(U):
                k0, base = geo[u]
                dtab_ref[0, base] += dss[u][:GRID_W]
                dtab_ref[1, base] += dss[u][GRID_W:]
                dq_ref[pl.ds(q0s[u], GRID_W), :] = _join_pair(dqs[u][:GRID_W], dqs[u][GRID_W:])
                dk_ref[pl.ds(k0, NA_KEYS), :] += dks[u]
                dv_ref[pl.ds(k0, NA_KEYS), :] += dvs[u]
            return carry

        lax.fori_loop(0, rows // NA_BWD_ROWS_PER_STEP, step, 0)

    def cols(first):
        return pl.BlockSpec((T, 128), lambda j: (0, first + j))

    tabs = pl.BlockSpec((2, NA_BASES, GRID_W, NA_KEYS), lambda j: (j, 0, 0, 0))
    wide = jax.ShapeDtypeStruct((T, NA_WIDTH), F32)
    return pl.pallas_call(
        body, name="na_bwd", grid=(n_pairs,),
        in_specs=[cols(0), cols(n_pairs), cols(2 * n_pairs), tabs, cols(0)],
        out_specs=[cols(0), cols(0), cols(0), tabs],
        out_shape=[wide, wide, wide, jax.ShapeDtypeStruct((NA_HEADS, NA_BASES, GRID_W, NA_KEYS), F32)],
        compiler_params=_cparams(("parallel",)),
    )(qkv, qkv, qkv, tab, do)


def _na_bias_table(rpb):
    H, n_rows, n_cols = rpb.shape

    def body(r_ref, tab_ref):
        q = lax.broadcasted_iota(jnp.int32, (GRID_W, 128), 0)
        kc = lax.broadcasted_iota(jnp.int32, (GRID_W, 128), 1)
        first = jnp.clip(q - NA_WIN_COLS // 2, 0, GRID_W - NA_WIN_COLS)
        valid = (kc >= first) & (kc < first + NA_WIN_COLS)
        toeplitz = []
        for ro in range(n_rows):
            row = jnp.broadcast_to(r_ref[pl.ds(ro, 1), :], (GRID_W, 128))
            shifted = pltpu.roll(pltpu.roll(row, 128 - (NA_WIN_COLS - 1), 1), 0, 1, stride=1, stride_axis=0)
            toeplitz.append(jnp.where(valid, shifted, NEG_INF))
        for base in range(NA_BASES):
            for j in range(NA_WIN_ROWS // 2):
                even, odd = toeplitz[base + 2 * j], toeplitz[base + 2 * j + 1]
                tab_ref[base, :, pl.ds(j * 128, 128)] = jnp.where(kc < GRID_W, even, pltpu.roll(odd, GRID_W, 1))

    padded = jnp.pad(rpb, ((0, 0), (0, 16 - n_rows), (0, 128 - n_cols)))
    return pl.pallas_call(
        body, name="na_bias_table", grid=(H,),
        in_specs=[pl.BlockSpec((None, 16, 128), lambda h: (h, 0, 0))],
        out_specs=pl.BlockSpec((None, NA_BASES, GRID_W, NA_KEYS), lambda h: (h, 0, 0, 0)),
        out_shape=jax.ShapeDtypeStruct((H, NA_BASES, GRID_W, NA_KEYS), F32),
        compiler_params=_cparams(("parallel",)),
    )(padded)


def _na_rpb_grad(dtab):
    H = dtab.shape[0]
    n_rows = 2 * NA_WIN_ROWS - 1
    n_cols = 2 * NA_WIN_COLS - 1

    def body(d_ref, o_ref):
        lane = lax.broadcasted_iota(jnp.int32, (GRID_W, 128), 1)
        low = lane < GRID_W
        out_rows = []
        for ro in range(n_rows):
            acc = jnp.zeros((GRID_W, 128), F32)
            for base in range(NA_BASES):
                i = ro - base
                if not 0 <= i < NA_WIN_ROWS:
                    continue
                pair = d_ref[base, :, pl.ds((i // 2) * 128, 128)]
                if i % 2:
                    pair = pltpu.roll(pair, GRID_W, 1)
                acc = acc + jnp.where(low, pair, 0.0)
            skew = pltpu.roll(acc, 0, 1, stride=1, stride_axis=0)
            diag = jnp.sum(skew, axis=0, keepdims=True)
            out_rows.append(pltpu.roll(jnp.broadcast_to(diag, (8, 128)), 128 - (GRID_W - NA_WIN_COLS), 1)[:1])
        out_rows.append(jnp.zeros((1, 128), F32))
        res = jnp.concatenate(out_rows, axis=0)
        o_ref[...] = jnp.where(lax.broadcasted_iota(jnp.int32, res.shape, 1) < n_cols, res, 0.0)

    return pl.pallas_call(
        body, name="na_rpb_grad", grid=(H,),
        in_specs=[pl.BlockSpec((None, NA_BASES, GRID_W, NA_KEYS), lambda h: (h, 0, 0, 0))],
        out_specs=pl.BlockSpec((None, n_rows + 1, 128), lambda h: (h, 0, 0)),
        out_shape=jax.ShapeDtypeStruct((H, n_rows + 1, 128), F32),
        compiler_params=_cparams(("parallel",)),
    )(jnp.flip(dtab, axis=2))


BAND_Q = 128
BAND_KEYS = BAND_Q + 2 * DIL_RADIUS


def _band_geometry(n, L):
    q0 = pl.multiple_of(n * BAND_Q, BAND_Q)
    k0 = pl.multiple_of(jnp.clip(q0 - DIL_RADIUS, 0, L - BAND_KEYS), DIL_RADIUS)
    qi = q0 + lax.broadcasted_iota(jnp.int32, (BAND_Q, BAND_KEYS), 0)
    kj = k0 + lax.broadcasted_iota(jnp.int32, (BAND_Q, BAND_KEYS), 1)
    return q0, k0, jnp.abs(qi - kj) <= DIL_RADIUS


DIL_PAIRS = DIL_OUT_WIDTH // 128


def _residue_shape(dil, T, dtype):
    return jax.ShapeDtypeStruct((DIL_PAIRS, dil, T // dil, 128), dtype)


def _residue_tile(dil, tm):
    return pl.BlockSpec((DIL_PAIRS, dil, tm // dil, 128), lambda i: (0, 0, i, 0))


def _to_natural(ref, scratch, dil, tm):
    tiles = []
    for pair in range(DIL_PAIRS):
        if dil == 1:
            tiles.append(ref[pair, 0].astype(F32))
            continue
        for r in range(dil):
            scratch[pl.ds(r, tm // dil, stride=dil), :] = ref[pair, r].astype(F32)
        tiles.append(scratch[...])
    return tiles


def _from_natural(tile, scratch, ref, pair, dil, tm):
    if dil == 1:
        ref[pair, 0] = tile.astype(ref.dtype)
        return
    scratch[...] = tile
    for r in range(dil):
        ref[pair, r] = scratch[pl.ds(r, tm // dil, stride=dil), :].astype(ref.dtype)


def _band_specs(group, T):
    dil = DIL_GROUPS[group][1]
    L = T // dil
    assert L % BAND_Q == 0 and L >= BAND_KEYS, (T, dil)
    return L, (dil * DIL_PAIRS,), pl.BlockSpec((None, None, L, 128), lambda s: (s % DIL_PAIRS, s // DIL_PAIRS, 0, 0))


BAND_BLOCKS_PER_STEP = 4


def _band_softmax(s, valid):
    s = jnp.where(valid, s, NEG_INF)
    m = jnp.max(s, axis=-1, keepdims=True)
    p = jnp.exp(s - m)
    l = jnp.sum(p, axis=-1, keepdims=True)
    return p / l, m + jnp.log(l)


def _band_fwd(q, k, v, group):
    T = q.shape[1] * q.shape[2]
    L, grid, spec = _band_specs(group, T)
    U = min(BAND_BLOCKS_PER_STEP, L // BAND_Q)

    def body(q_ref, k_ref, v_ref, o_ref, lse_ref):
        def step(it, carry):
            geo = [_band_geometry(it * U + u, L) for u in range(U)]
            ss = [lax.dot_general(jnp.concatenate(_split_pair(q_ref[pl.ds(q0, BAND_Q), :]), axis=0),
                                  k_ref[pl.ds(k0, BAND_KEYS), :], _NT, preferred_element_type=F32) for q0, k0, _ in geo]
            pls = [_band_softmax(s, jnp.concatenate([valid, valid], axis=0)) for s, (_, _, valid) in zip(ss, geo)]
            os = [jnp.dot(p.astype(BF), v_ref[pl.ds(k0, BAND_KEYS), :], preferred_element_type=F32) for (p, _), (_, k0, _) in zip(pls, geo)]
            for (q0, _, _), o2, (_, lse) in zip(geo, os, pls):
                o_ref[pl.ds(q0, BAND_Q), :] = _join_pair(o2[:BAND_Q], o2[BAND_Q:])
                lse2 = jnp.broadcast_to(lse, (2 * BAND_Q, 128))
                lse_ref[pl.ds(q0, BAND_Q), :] = _join_pair(lse2[:BAND_Q], lse2[BAND_Q:])
            return carry

        lax.fori_loop(0, L // (BAND_Q * U), step, 0)

    res = _residue_shape(DIL_GROUPS[group][1], T, F32)
    return pl.pallas_call(
        body, name=f"band_fwd_g{group}", grid=grid,
        in_specs=[spec] * 3, out_specs=[spec] * 2, out_shape=[res, res],
        compiler_params=_cparams(("parallel",)),
    )(q, k, v)


def _band_bwd(q, k, v, do, dlse, group):
    T = q.shape[1] * q.shape[2]
    L, grid, spec = _band_specs(group, T)
    U = min(BAND_BLOCKS_PER_STEP, L // BAND_Q)

    def body(q_ref, k_ref, v_ref, do_ref, dlse_ref, dq_ref, dk_ref, dv_ref):
        dk_ref[...] = jnp.zeros_like(dk_ref)
        dv_ref[...] = jnp.zeros_like(dv_ref)

        def step(it, carry):
            geo = [_band_geometry(it * U + u, L) for u in range(U)]
            q2s = [jnp.concatenate(_split_pair(q_ref[pl.ds(q0, BAND_Q), :]), axis=0) for q0, _, _ in geo]
            do2s = [jnp.concatenate(_split_pair(do_ref[pl.ds(q0, BAND_Q), :]), axis=0) for q0, _, _ in geo]
            ss = [lax.dot_general(q2, k_ref[pl.ds(k0, BAND_KEYS), :], _NT, preferred_element_type=F32) for q2, (_, k0, _) in zip(q2s, geo)]
            dps = [lax.dot_general(do2, v_ref[pl.ds(k0, BAND_KEYS), :], _NT, preferred_element_type=F32) for do2, (_, k0, _) in zip(do2s, geo)]
            ps = [_band_softmax(s, jnp.concatenate([valid, valid], axis=0))[0] for s, (_, _, valid) in zip(ss, geo)]
            dss = []
            for p, dp, (q0, _, _) in zip(ps, dps, geo):
                dl = dlse_ref[pl.ds(q0, BAND_Q), :]
                dl2 = jnp.concatenate([dl[:, :1], dl[:, HEAD_DIM:HEAD_DIM + 1]], axis=0)
                dss.append(p * (dp - jnp.sum(dp * p, axis=-1, keepdims=True) + dl2))
            dvs = [lax.dot_general(p.astype(BF), do2, _TN, preferred_element_type=F32) for p, do2 in zip(ps, do2s)]
            dsbs = [ds.astype(BF) for ds in dss]
            dqs = [jnp.dot(dsb, k_ref[pl.ds(k0, BAND_KEYS), :], preferred_element_type=F32) for dsb, (_, k0, _) in zip(dsbs, geo)]
            dks = [lax.dot_general(dsb, q2, _TN, preferred_element_type=F32) for dsb, q2 in zip(dsbs, q2s)]
            for u, (q0, k0, _) in enumerate(geo):
                dq_ref[pl.ds(q0, BAND_Q), :] = _join_pair(dqs[u][:BAND_Q], dqs[u][BAND_Q:])
                dk_ref[pl.ds(k0, BAND_KEYS), :] += dks[u]
                dv_ref[pl.ds(k0, BAND_KEYS), :] += dvs[u]
            return carry

        lax.fori_loop(0, L // (BAND_Q * U), step, 0)

    res = _residue_shape(DIL_GROUPS[group][1], T, F32)
    return pl.pallas_call(
        body, name=f"band_bwd_g{group}", grid=grid,
        in_specs=[spec] * 5, out_specs=[spec] * 3, out_shape=[res] * 3,
        compiler_params=_cparams(("parallel",)),
    )(q, k, v, do, dlse)


def _head_sums(t):
    head = lax.broadcasted_iota(jnp.int32, t.shape, 1) // HEAD_DIM
    out = jnp.zeros_like(t)
    for h in range(t.shape[1] // HEAD_DIM):
        mine = head == h
        out = jnp.where(mine, jnp.sum(jnp.where(mine, t, 0.0), axis=-1, keepdims=True), out)
    return out


def _dil_merge_fwd(os, lses, T, tm):
    G = len(DIL_GROUPS)
    W = DIL_OUT_WIDTH
    dils = [d for _, d in DIL_GROUPS]

    def body(*refs):
        o_refs, lse_refs = refs[:G], refs[G:2 * G]
        y_ref, w_refs, on_refs, scratch = refs[2 * G], refs[2 * G + 1:3 * G + 1], refs[3 * G + 1:4 * G + 1], refs[-1]
        o = [jnp.concatenate(_to_natural(r, scratch, d, tm), axis=1) for r, d in zip(o_refs, dils)]
        ls = [jnp.concatenate(_to_natural(r, scratch, d, tm), axis=1) for r, d in zip(lse_refs, dils)]
        m = functools.reduce(jnp.maximum, ls)
        es = [jnp.exp(l - m) for l in ls]
        tot = functools.reduce(jnp.add, es)
        ws = [e / tot for e in es]
        y_ref[...] = functools.reduce(jnp.add, [w * t for w, t in zip(ws, o)]).astype(y_ref.dtype)
        for g in range(G):
            w_refs[g][...] = ws[g]
            on_refs[g][...] = o[g]

    nat = pl.BlockSpec((tm, W), lambda i: (i, 0))
    res = pl.pallas_call(
        body, name="dil_merge_fwd", grid=(T // tm,),
        in_specs=[_residue_tile(d, tm) for d in dils] * 2,
        out_specs=[nat] * (2 * G + 1),
        out_shape=[jax.ShapeDtypeStruct((T, W), BF)] + [jax.ShapeDtypeStruct((T, W), F32)] * (2 * G),
        scratch_shapes=[pltpu.VMEM((tm, 128), F32)],
        compiler_params=_cparams(("parallel",)),
    )(*os, *lses)
    return res[0], res[1:G + 1], res[G + 1:]


def _dil_merge_bwd(dy, os, ws, tm, after=()):
    G = len(DIL_GROUPS)
    T, W = dy.shape
    dils = [d for _, d in DIL_GROUPS]
    n_after = len(after)

    def body(*refs):
        dyt = refs[0][...]
        o, w = [r[...] for r in refs[1:G + 1]], [r[...] for r in refs[G + 1:2 * G + 1]]
        refs = refs[2 * G + 1 + n_after:]
        do_refs, dlse_refs, scratch = refs[:G], refs[G:2 * G], refs[-1]
        dws = [_head_sums(dyt * t) for t in o]
        mean = functools.reduce(jnp.add, [a * b for a, b in zip(w, dws)])
        for g, d in enumerate(dils):
            do, dlse = w[g] * dyt, w[g] * (dws[g] - mean)
            for pair in range(DIL_PAIRS):
                cols = slice(pair * 128, (pair + 1) * 128)
                _from_natural(do[:, cols], scratch, do_refs[g], pair, d, tm)
                _from_natural(dlse[:, cols], scratch, dlse_refs[g], pair, d, tm)

    nat = pl.BlockSpec((tm, W), lambda i: (i, 0))
    res = pl.pallas_call(
        body, name="dil_merge_bwd", grid=(T // tm,),
        in_specs=[nat] * (2 * G + 1) + [ANY] * n_after,
        out_specs=[_residue_tile(d, tm) for d in dils] * 2,
        out_shape=[_residue_shape(d, T, BF) for d in dils] + [_residue_shape(d, T, F32) for d in dils],
        scratch_shapes=[pltpu.VMEM((tm, 128), F32)],
        compiler_params=_cparams(("parallel",)),
    )(dy, *os, *ws, *after)
    return res[:G], res[G:]


def _qkv_prep(z, cos2, sin_signed, tm):
    T = z.shape[0]
    G = len(DIL_GROUPS)
    dils = [d for _, d in DIL_GROUPS]
    n_dil_blocks = 3 * DIL_WIDTH // 128

    def body(*refs):
        blocks = refs[:n_dil_blocks]
        cos_ref, sin_ref = refs[n_dil_blocks], refs[1 + n_dil_blocks]
        outs = refs[2 + n_dil_blocks:]
        for part in range(3):
            for g, d in enumerate(dils):
                out = outs[g * 3 + part]
                for pair in range(DIL_PAIRS):
                    blk = blocks[part * (DIL_WIDTH // 128) + g * DIL_PAIRS + pair]
                    for r in range(d):
                        rows = pl.ds(r, tm // d, stride=d) if d > 1 else slice(None)
                        x = blk[rows, :]
                        if part < 2:
                            x = _rope(x, cos_ref[rows, :], sin_ref[rows, :])
                        if part == 0:
                            x = x * Q_SCALE
                        out[pair, r] = x.astype(out.dtype)

    lane_block = [pl.BlockSpec((tm, 128), functools.partial(lambda b, i: (i, b), b)) for b in range(n_dil_blocks)]
    tab = pl.BlockSpec((tm, 128), lambda i: (i, 0))
    res = pl.pallas_call(
        body, name="qkv_prep", grid=(T // tm,),
        in_specs=lane_block + [tab, tab],
        out_specs=[_residue_tile(d, tm) for d in dils for _ in range(3)],
        out_shape=[_residue_shape(d, T, BF) for d in dils for _ in range(3)],
        compiler_params=_cparams(("parallel",)),
    )(*[z] * n_dil_blocks, cos2, sin_signed)
    return [res[3 * g:3 + 3 * g] for g in range(G)]


def _qkv_unprep(d_na, d_dil, cos2, sin_signed, tm, after=()):
    T = d_na[0].shape[0]
    G = len(DIL_GROUPS)
    dils = [d for _, d in DIL_GROUPS]
    n_after = len(after)

    def body(*refs):
        dq, dk, dv = (r[...] for r in refs[:3])
        res_refs = refs[3:3 + 3 * G]
        cs, sn = refs[3 + 3 * G][...], refs[4 + 3 * G][...]
        out, scratch = refs[5 + 3 * G + n_after], refs[-1]
        cols = [dq * Q_SCALE, dk, dv]
        for part in range(3):
            for g, d in enumerate(dils):
                for x in _to_natural(res_refs[g * 3 + part], scratch, d, tm):
                    if part < 2:
                        x = _rope(x, cs, -sn)
                    cols.append(x * Q_SCALE if part == 0 else x)
        out[...] = jnp.concatenate(cols, axis=1).astype(out.dtype)

    wide = pl.BlockSpec((tm, NA_WIDTH), lambda i: (i, 0))
    tab = pl.BlockSpec((tm, 128), lambda i: (i, 0))
    return pl.pallas_call(
        body, name="qkv_unprep", grid=(T // tm,),
        in_specs=[wide] * 3 + [_residue_tile(d, tm) for d in dils for _ in range(3)] + [tab, tab] + [ANY] * n_after,
        out_specs=pl.BlockSpec((tm, QKV_WIDTH), lambda i: (i, 0)),
        out_shape=jax.ShapeDtypeStruct((T, QKV_WIDTH), BF),
        scratch_shapes=[pltpu.VMEM((tm, 128), F32)],
        compiler_params=_cparams(("parallel",)),
    )(*d_na, *[t for g in range(G) for t in d_dil[g]], cos2, sin_signed, *after)


def _rope_tables(positions):
    half = HEAD_DIM // 2
    inv_freq = ROPE_THETA ** (-jnp.arange(half, dtype=F32) / half)
    ang = positions.astype(F32)[:, None] * inv_freq
    cos, sin = jnp.cos(ang), jnp.sin(ang)
    return jnp.tile(jnp.concatenate([cos, cos], axis=1), (1, 2)), jnp.tile(jnp.concatenate([-sin, sin], axis=1), (1, 2))


def _pack_rows(t):
    return t.reshape(-1, PACK_W)


def _me():
    return lax.axis_index("x"), lax.axis_index("y"), lax.axis_index("c")


def _other_chips(x, y):
    return [(1 - x, y), (x, 1 - y), (1 - x, 1 - y)]


def _pair_sum(g, got, tm):
    S, R, W = g.shape
    half = R // 2
    nb = half // tm

    def body(c_ref, g_ref, got_ref, o_ref, ob_ref):
        tot = g_ref[...] + got_ref[...]
        o_ref[...] = tot
        ob_ref[...] = tot.astype(ob_ref.dtype)

    tile = pl.BlockSpec((None, tm, W), lambda s, i, c_ref: (s, i, 0))
    return pl.pallas_call(
        body, name="pair_sum",
        grid_spec=pltpu.PrefetchScalarGridSpec(
            num_scalar_prefetch=1, grid=(S, nb),
            in_specs=[pl.BlockSpec((None, tm, W), lambda s, i, c_ref: (s, c_ref[0] * nb + i, 0)), tile],
            out_specs=[tile, tile]),
        out_shape=[jax.ShapeDtypeStruct((S, half, W), F32), jax.ShapeDtypeStruct((S, half, W), BF)],
        compiler_params=_cparams(("parallel", "parallel")),
    )(lax.axis_index("c").reshape(1).astype(jnp.int32), g, got)


def _chip_sum(own, others, tm):
    n, h, W = others.shape
    nb = h // tm

    def body(c_ref, own_ref, p_ref, o_ref):
        o_ref[...] = ((own_ref[...] + p_ref[0].astype(F32)) + p_ref[1].astype(F32)) + p_ref[2].astype(F32)

    return pl.pallas_call(
        body, name="chip_sum",
        grid_spec=pltpu.PrefetchScalarGridSpec(
            num_scalar_prefetch=1, grid=(nb,),
            in_specs=[pl.BlockSpec((tm, W), lambda i, c_ref: (i, 0)), pl.BlockSpec((n, tm, W), lambda i, c_ref: (0, i, 0))],
            out_specs=pl.BlockSpec((tm, W), lambda i, c_ref: (c_ref[0] * nb + i, 0))),
        out_shape=jax.ShapeDtypeStruct((2 * h, W), F32),
        compiler_params=_cparams(("parallel",)),
    )(lax.axis_index("c").reshape(1).astype(jnp.int32), own, others)


def _join_halves(shard):
    h = shard.shape[0] // 2

    def body(in_ref, out_ref, send_sem, recv_sem):
        x, y, c = _me()
        cp = pltpu.make_async_remote_copy(
            src_ref=in_ref.at[pl.ds(c * h, h), :], dst_ref=out_ref.at[pl.ds(c * h, h), :],
            send_sem=send_sem, recv_sem=recv_sem, device_id=(x, y, 1 - c), device_id_type=MESH)
        cp.start()
        pltpu.make_async_remote_copy(
            src_ref=in_ref.at[pl.ds(c * h, h), :], dst_ref=out_ref.at[pl.ds((1 - c) * h, h), :],
            send_sem=send_sem, recv_sem=recv_sem, device_id=(x, y, 1 - c), device_id_type=MESH).wait_recv()
        cp.wait_send()

    return pl.pallas_call(
        body, name="join_halves", in_specs=[ANY], out_specs=ANY,
        out_shape=jax.ShapeDtypeStruct(shard.shape, shard.dtype), input_output_aliases={0: 0},
        scratch_shapes=[pltpu.SemaphoreType.DMA, pltpu.SemaphoreType.DMA],
    )(shard)


def _allreduce_small(s, after=()):
    R, W = s.shape
    n_after = len(after)

    def body(s_ref, *rest):
        o_ref, buf, send_sems, recv_sems = rest[n_after:]
        x, y, c = _me()
        me = 4 * x + 2 * y + c
        buf[me] = s_ref[...]
        peers = [((x + fx) % 2, (y + fy) % 2, (c + fc) % 2) for fx in range(2) for fy in range(2) for fc in range(2)][1:]
        sends = [pltpu.make_async_remote_copy(
            src_ref=s_ref, dst_ref=buf.at[me], send_sem=send_sems.at[k], recv_sem=recv_sems.at[k],
            device_id=peer, device_id_type=MESH) for k, peer in enumerate(peers)]
        for cp in sends:
            cp.start()
        for k, peer in enumerate(peers):
            pltpu.make_async_remote_copy(
                src_ref=s_ref, dst_ref=buf.at[4 * peer[0] + 2 * peer[1] + peer[2]], send_sem=send_sems.at[k],
                recv_sem=recv_sems.at[k], device_id=peer, device_id_type=MESH).wait_recv()
        for cp in sends:
            cp.wait_send()
        total = buf[0]
        for d in range(1, N_DEV):
            total = total + buf[d]
        o_ref[...] = total

    return pl.pallas_call(
        body, name="allreduce_small",
        in_specs=[pl.BlockSpec(memory_space=pltpu.VMEM)] + [ANY] * n_after, out_specs=pl.BlockSpec(memory_space=pltpu.VMEM),
        out_shape=jax.ShapeDtypeStruct((R, W), F32),
        scratch_shapes=[pltpu.VMEM((N_DEV, R, W), F32), pltpu.SemaphoreType.DMA((N_DEV - 1,)), pltpu.SemaphoreType.DMA((N_DEV - 1,))],
    )(s, *after)


HBM_SPEC = pl.BlockSpec(memory_space=pltpu.HBM)
SEM_SPEC = pl.BlockSpec(memory_space=pltpu.SEMAPHORE)
DATAFLOW = pltpu.SideEffectType.DATAFLOW_SIDE_EFFECTING


class _InFlight(NamedTuple):
    sems: tuple
    src: jax.Array
    land: jax.Array
    token: jax.Array


def _split_start(name, src, land_shape, land_dtype, n, copies, after=()):
    n_after = len(after)

    def body(src_ref, land_ref, *rest):
        rest = rest[n_after:]
        sems, token = rest[:2 * n], rest[-1]
        for k, (s, d, peer) in enumerate(copies(src_ref, land_ref)):
            pltpu.make_async_remote_copy(src_ref=s, dst_ref=d, send_sem=sems[k], recv_sem=sems[n + k],
                                         device_id=peer, device_id_type=MESH).start()
        token[...] = jnp.zeros_like(token)

    outs = pl.pallas_call(
        body, name=name,
        out_shape=(*[pltpu.SemaphoreType.DMA(())] * (2 * n), pltpu.HBM(src.shape, src.dtype), pltpu.HBM(land_shape, land_dtype),
                   jax.ShapeDtypeStruct((8, 128), F32)),
        in_specs=(HBM_SPEC, HBM_SPEC, *[ANY] * n_after),
        out_specs=(*[SEM_SPEC] * (2 * n), HBM_SPEC, HBM_SPEC, pl.BlockSpec(memory_space=pltpu.VMEM)),
        input_output_aliases={0: 2 * n, 1: 2 * n + 1},
        compiler_params=pltpu.CompilerParams(has_side_effects=DATAFLOW),
    )(pltpu.with_memory_space_constraint(src, pltpu.HBM), pltpu.with_memory_space_constraint(lax.empty(land_shape, land_dtype), pltpu.HBM),
      *after)
    return _InFlight(tuple(outs[:2 * n]), outs[2 * n], outs[2 * n + 1], outs[2 * n + 2])


def _split_wait(name, flight, after, n, copies):
    def body(src_ref, land_ref, *rest):
        sems = rest[:2 * n]
        for k, (s, d, peer) in enumerate(copies(src_ref, land_ref)):
            cp = pltpu.make_async_remote_copy(src_ref=s, dst_ref=d, send_sem=sems[k], recv_sem=sems[n + k],
                                              device_id=peer, device_id_type=MESH)
            cp.wait_send()
            cp.wait_recv()

    return pl.pallas_call(
        body, name=name,
        out_shape=(pltpu.HBM(flight.src.shape, flight.src.dtype), pltpu.HBM(flight.land.shape, flight.land.dtype)),
        in_specs=(HBM_SPEC, HBM_SPEC, *[SEM_SPEC] * (2 * n), ANY),
        out_specs=(HBM_SPEC, HBM_SPEC), input_output_aliases={0: 0, 1: 1},
        compiler_params=pltpu.CompilerParams(has_side_effects=DATAFLOW),
    )(flight.src, flight.land, *flight.sems, after)


def _gather_copies(src_ref, land_ref):
    x, y, c = _me()
    return [(src_ref, land_ref.at[2 * x + y], (*chip, c)) for chip in _other_chips(x, y)]


def _gather_start(packed, tag, after=()):
    return _split_start(f"gather_start_{tag}", packed, (N_CHIPS, *packed.shape), packed.dtype, 3, _gather_copies, after)


def _gather_wait(flight, after, tag):
    src, others = _split_wait(f"gather_wait_{tag}", flight, after, 3, _gather_copies)
    return lax.dynamic_update_slice(others, src[None], (2 * lax.axis_index("x") + lax.axis_index("y"), 0, 0))


def _across_copies(src_ref, land_ref):
    x, y, c = _me()
    half = src_ref.shape[0] // 2
    rows = pl.ds(c * half, half)
    return [(src_ref.at[rows, :], land_ref.at[2 * x + y, rows, :], (*chip, c)) for chip in _other_chips(x, y)]


def _to_sibling_copies(all_ref, unused_ref):
    x, y, c = _me()
    half = all_ref.shape[1] // 2
    places = [all_ref.at[2 * chip[0] + chip[1], pl.ds(c * half, half), :] for chip in _other_chips(x, y)]
    return [(place, place, (x, y, 1 - c)) for place in places]


def _gather_halves_start(shard):
    return _split_start("gather_in_across_start", shard, (N_CHIPS, *shard.shape), shard.dtype, 3, _across_copies)


def _gather_halves_finish(flight, after):
    shard, landed = _split_wait("gather_in_across_wait", flight, after, 3, _across_copies)
    forward = _split_start("gather_in_sibling_start", landed, (8, 128), landed.dtype, 3, _to_sibling_copies)
    others = _split_wait("gather_in_sibling_wait", forward, forward.token, 3, _to_sibling_copies)[0]
    return lax.dynamic_update_slice(others, shard[None], (2 * lax.axis_index("x") + lax.axis_index("y"), 0, 0))


def _swap_copies(src_ref, land_ref):
    x, y, c = _me()
    half = land_ref.shape[1]
    return [(src_ref.at[:, pl.ds((1 - c) * half, half), :], land_ref, (x, y, 1 - c))]


def _swap_start(g, tag):
    S, R, W = g.shape
    return _split_start(f"swap_halves_start_{tag}", g, (S, R // 2, W), g.dtype, 1, _swap_copies)


def _swap_wait(flight, after, tag):
    return _split_wait(f"swap_halves_wait_{tag}", flight, after, 1, _swap_copies)


def _scatter_copies(src_ref, land_ref):
    x, y, c = _me()
    return [(src_ref.at[2 * chip[0] + chip[1]], land_ref.at[j], (*chip, c)) for j, chip in enumerate(_other_chips(x, y))]


def _scatter_start(part, tag):
    S, h, W = part.shape
    return _split_start(f"scatter_chips_start_{tag}", part, (S - 1, h, W), part.dtype, 3, _scatter_copies)


def _scatter_wait(flight, after, tag):
    return _split_wait(f"scatter_chips_wait_{tag}", flight, after, 3, _scatter_copies)[1]


def _join_copies(shard_ref, unused_ref):
    x, y, c = _me()
    h = shard_ref.shape[0] // 2
    rows = shard_ref.at[pl.ds(c * h, h), :]
    return [(rows, rows, (x, y, 1 - c))]


def _join_start(shard):
    return _split_start("join_halves_start", shard, (8, 128), shard.dtype, 1, _join_copies)


def _join_wait(flight, after):
    return _split_wait("join_halves_wait", flight, after, 1, _join_copies)[0]


def _adamw(name, g, g_row0, w, m, v):
    _, R, C = w.shape
    tm = next(cand for cand in (256, 128, 64, 32, 16, 8) if R % cand == 0)
    assert g_row0 % tm == 0 and g.shape[1] == C

    def body(g_ref, w_ref, m_ref, v_ref, go_ref, d_ref, mo_ref, vo_ref):
        gt = g_ref[...]
        mt = ADAM_B1 * m_ref[...] + (1.0 - ADAM_B1) * gt
        vt = ADAM_B2 * v_ref[...] + (1.0 - ADAM_B2) * jnp.square(gt)
        m_hat = mt / (1.0 - ADAM_B1 ** ADAM_STEP)
        v_hat = vt / (1.0 - ADAM_B2 ** ADAM_STEP)
        go_ref[...] = gt
        d_ref[...] = -ADAM_LR * (m_hat / (jnp.sqrt(v_hat) + ADAM_EPS) + ADAM_WD * w_ref[...])
        mo_ref[...] = mt
        vo_ref[...] = vt

    state = pl.BlockSpec((None, tm, C), lambda i: (0, i, 0))
    return pl.pallas_call(
        body, name=name, grid=(R // tm,),
        in_specs=[pl.BlockSpec((tm, C), lambda i: (g_row0 // tm + i, 0)), state, state, state],
        out_specs=[state] * 4, out_shape=[jax.ShapeDtypeStruct((1, R, C), F32)] * 4,
        compiler_params=_cparams(("parallel",)),
    )(g, w, m, v)


def _unpack_weights(gathered, names):
    S = gathered.shape[0]
    shard_shapes = {"w_in": (D_MODEL, (QKV_WIDTH + 2 * D_MODEL) // S), "w_branch_na": (NA_WIDTH, D_MODEL // S),
                    "w_branch_dil": (DIL_OUT_WIDTH, D_MODEL // S), "w_out": (D_MODEL // S, D_MODEL),
                    "w_up": (D_MODEL, D_FF // S), "w_down": (D_FF // S, D_MODEL),
                    "w_ple_gate": (D_MODEL // S, D_MODEL), "w_ple_proj": (PLE_DIM, D_MODEL // S)}
    col_sharded = {"w_in", "w_branch_na", "w_branch_dil", "w_up", "w_ple_proj"}
    out, r0 = {}, 0
    for name in names:
        rows, cols = shard_shapes[name]
        n = rows * cols // PACK_W
        t = gathered[:, r0:r0 + n, :].reshape(S, rows, cols)
        r0 += n
        out[name] = t.transpose(1, 0, 2).reshape(rows, S * cols) if name in col_sharded else t.reshape(S * rows, cols)
    return out


def kernel(x, p, positions, g_mix, w_in, rpb, w_branch_na, w_branch_dil, w_out, g_mlp, w_up, w_down, g_ple, w_ple_gate, w_ple_proj, g_final, loss_target, m_g_mix, m_w_in, m_rpb, m_w_branch_na, m_w_branch_dil, m_w_out, m_g_mlp, m_w_up, m_w_down, m_g_ple, m_w_ple_gate, m_w_ple_proj, m_g_final, v_g_mix, v_w_in, v_rpb, v_w_branch_na, v_w_branch_dil, v_w_out, v_g_mlp, v_w_up, v_w_down, v_g_ple, v_w_ple_gate, v_w_ple_proj, v_g_final):
    shards = {"w_in": w_in[0], "w_branch_na": w_branch_na[0], "w_branch_dil": w_branch_dil[0], "w_out": w_out[0],
              "w_up": w_up[0], "w_down": w_down[0], "w_ple_gate": w_ple_gate[0], "w_ple_proj": w_ple_proj[0]}
    params = {"w_in": w_in, "w_branch_na": w_branch_na, "w_branch_dil": w_branch_dil, "w_out": w_out, "w_up": w_up,
              "w_down": w_down, "w_ple_gate": w_ple_gate, "w_ple_proj": w_ple_proj,
              "m_w_in": m_w_in, "m_w_branch_na": m_w_branch_na, "m_w_branch_dil": m_w_branch_dil, "m_w_out": m_w_out,
              "m_w_up": m_w_up, "m_w_down": m_w_down, "m_w_ple_gate": m_w_ple_gate, "m_w_ple_proj": m_w_ple_proj,
              "v_w_in": v_w_in, "v_w_branch_na": v_w_branch_na, "v_w_branch_dil": v_w_branch_dil, "v_w_out": v_w_out,
              "v_w_up": v_w_up, "v_w_down": v_w_down, "v_w_ple_gate": v_w_ple_gate, "v_w_ple_proj": v_w_ple_proj}

    xs, ps, tgt = x[0], p[0, 0], loss_target[0]
    T = xs.shape[0]
    TM = 512
    gm, gl, gp, gf = g_mix, g_mlp, g_ple, g_final.reshape(1, D_MODEL)

    across = _gather_halves_start(shards["w_in"].astype(BF))
    a = _rowwise("norm_mix", lambda h, g: h * _rms(h) * g, T, TM, [_row(xs, TM), _full(gm)], [(D_MODEL, BF)],
                 after=(across.token,))
    cos2, sin_signed = _rope_tables(positions[0])
    tab = _na_bias_table(rpb[0])
    w_in_all = _gather_halves_finish(across, a)
    W = {"w_in": w_in_all.transpose(1, 0, 2).reshape(D_MODEL, -1)}
    mix_flight = _gather_start(jnp.concatenate([_pack_rows(shards[n].astype(BF)) for n in GATHER_MIX], axis=0), "mix",
                               after=(w_in_all,))
    rest_flight = _gather_start(jnp.concatenate([_pack_rows(shards[n].astype(BF)) for n in GATHER_MLP], axis=0), "mlp",
                                after=(mix_flight.token,))
    w_gates = W["w_in"][:, QKV_WIDTH:]

    n3 = 3 * NA_WIDTH
    qkv = _mm("in_na", a, W["w_in"], "nn", 1024, 768, 1024, [BF], after=(rest_flight.token,),
              b_view=(n3, (D_MODEL, 768), lambda j, k: (k, j)))
    z_dil = _mm("in_dil", a, W["w_in"], "nn", 1024, 768, 1024, [F32],
                b_view=(3 * DIL_WIDTH, (D_MODEL, 768), lambda j, k: (k, n3 // 768 + j)))
    z_gates = _mm("in_gates", a, w_gates, "nn", 1024,1024, 1024, [BF])

    dil_ops = _qkv_prep(z_dil, cos2, sin_signed, TM)
    y_na = _na_fwd(qkv, tab)
    band = [_band_fwd(*dil_ops[g], g) for g in range(len(DIL_GROUPS))]
    y_dil, w_grp, o_nat = _dil_merge_fwd([b[0] for b in band], [b[1] for b in band], T, TM)

    W.update(_unpack_weights(_gather_wait(mix_flight, y_dil, "mix"), GATHER_MIX))
    u_na = _mm("branch_na", y_na, W["w_branch_na"], "nn", 1024,1024, 512, [BF])
    u_dil = _mm("branch_dil", y_dil, W["w_branch_dil"], "nn", 1024,1024, 256, [BF])
    mixed = _rowwise(
        "gate_mix", lambda gn, gd, un, ud: _sigmoid(gn.astype(F32)) * un.astype(F32) + _sigmoid(gd.astype(F32)) * ud.astype(F32), T, TM,
        [_row(z_gates, TM, 0, D_MODEL), _row(z_gates, TM, 1, D_MODEL), _row(u_na, TM), _row(u_dil, TM)], [(D_MODEL, BF)])
    def add_norm(d, h, g):
        h = h + d
        return h, h * _rms(h) * g

    h1, cn = _mm("out_proj", mixed, W["w_out"], "nn", 512, 1024, 1024, [F32, BF], epilogue=add_norm, extras=(xs,), consts=(gl,))
    mlp_all = _gather_wait(rest_flight, cn, "mlp")
    W.update({n: t for n, t in _unpack_weights(mlp_all, GATHER_MLP).items() if n.startswith("w_ple")})
    chip_block = (None, D_MODEL, PACK_W)
    up, act = _mm("mlp_up", cn, mlp_all, "nn", 1024,1024, 1024, [BF, BF],
                  epilogue=lambda acc: (acc, jnp.square(jnp.maximum(acc, 0.0))), b_view=(D_FF, chip_block, lambda j, k: (j, 0, 0)))
    h2, en = _mm("mlp_down", act, mlp_all, "nn", 1024, 1024, 1024, [F32, BF], epilogue=add_norm, extras=(h1,), consts=(gp,),
                 b_view=(D_MODEL, chip_block, lambda j, k: (k, 1, 0)))
    pp = _mm("ple_proj", ps, W["w_ple_proj"], "nn", 1024,1024, 256, [F32])

    def head(gtt, h2t, ppt, tg, g):
        sg = _sigmoid(gtt)
        h3 = h2t + sg * ppt
        yo = h3 * _rms(h3) * g
        diff = yo - tg
        loss = 0.5 * jnp.sum(jnp.mean(jnp.square(diff), axis=-1, keepdims=True), axis=0, keepdims=True)
        dh3, dg = _rms_bwd(diff * (1.0 / D_MODEL), h3, g)
        return dh3, dh3 * ppt * sg * (1.0 - sg), dh3 * sg, jnp.broadcast_to(loss, (1, 128)), dg

    dh3, d_gt, d_pp, loss_part, dg_final = _mm(
        "ple_gate_loss_head", en, W["w_ple_gate"], "nn", 512, 1024, 1024, [F32, BF, BF], epilogue=head,
        extras=(h2, pp, tgt), consts=(gf,), sums=[128, D_MODEL])

    early_shapes = {n: shards[n].shape for n in REDUCE_EARLY}
    early_rows = sum(r * c for r, c in early_shapes.values()) // PACK_W
    shard_rows = D_MODEL // N_CHIPS
    early_buf = _mm("g_ple_gate", en, d_gt, "tn", 1024, 1024, 1024, [F32],
                    into=(jax.ShapeDtypeStruct((N_CHIPS, early_rows, PACK_W), F32), (N_CHIPS, shard_rows, PACK_W),
                          lambda i, j: (0, 2 * D_MODEL // shard_rows, 0)))
    g_ple_proj = _mm("g_ple_proj", ps, d_pp, "tn", 256, 1024, 1024,[F32])

    def add_norm_bwd(dn, dh_out, h, g):
        dh, dg = _rms_bwd(dn, h, g)
        dh = dh_out + dh
        return dh, dh, dg

    dh2, dh2_b, dg_ple = _mm("d_ple_gate", d_gt, W["w_ple_gate"], "nt", 512, 1024, 1024, [F32, BF],
                             epilogue=add_norm_bwd, extras=(dh3, h2), consts=(gp,), sums=[D_MODEL])
    d_up = _mm("d_mlp_down", dh2_b, mlp_all, "nt", 1024,1024, 1024, [BF], b_view=(D_FF, chip_block, lambda j, k: (j, 1, 0)),
               epilogue=lambda acc, u: (acc * (2.0 * jnp.maximum(u.astype(F32), 0.0)),), extras=(up,))
    early_buf = _mm("g_mlp_down", act, dh2_b, "tn", 1024, 1024, 1024,[F32],
                    into=(early_buf, (None, D_MODEL, PACK_W), lambda i, j: (i, 1, 0)))
    early_buf = _mm("g_mlp_up", cn, d_up, "tn", 1024, 1024, 1024,[F32],
                    into=(early_buf, (None, D_MODEL, PACK_W), lambda i, j: (j, 0, 0)))
    dh1, dh1_b, dg_mlp = _mm("d_mlp_up", d_up, mlp_all, "nt", 1024, 1024, 1024, [F32, BF], epilogue=add_norm_bwd,
                             b_view=(D_MODEL, chip_block, lambda j, k: (k, 0, 0)),
                             extras=(dh2, h1), consts=(gl,), sums=[D_MODEL])
    d_mixed = _mm("d_out_proj", dh1_b, W["w_out"], "nt", 1024,1024, 1024, [F32])
    early_buf = _mm("g_out_proj", mixed, dh1_b, "tn", 1024, 1024, 1024, [F32],
                    into=(early_buf, (N_CHIPS, shard_rows, PACK_W), lambda i, j: (0, 2 * D_MODEL // shard_rows + 1, 0)))

    def gate_bwd(dm, gn, gd, un, ud):
        gn, gd, un, ud = (t.astype(F32) for t in (gn, gd, un, ud))
        sn, sd = _sigmoid(gn), _sigmoid(gd)
        return jnp.concatenate([dm * un * sn * (1.0 - sn), dm * ud * sd * (1.0 - sd)], axis=1), dm * sn, dm * sd

    dz_gates, d_u_na, d_u_dil = _rowwise(
        "gate_mix_bwd", gate_bwd, T, TM,
        [_row(d_mixed, TM), _row(z_gates, TM, 0, D_MODEL), _row(z_gates, TM, 1, D_MODEL), _row(u_na, TM), _row(u_dil, TM)],
        [(2 * D_MODEL, BF), (D_MODEL, BF), (D_MODEL, BF)])
    g_branch_na = _mm("g_branch_na", y_na, d_u_na, "tn", 1024, 1024, 1024,[F32])
    g_branch_dil = _mm("g_branch_dil", y_dil, d_u_dil, "tn", 256, 1024, 1024,[F32])
    small_rows = [jnp.concatenate([_pack_rows(g[:, s * shard_rows:(s + 1) * shard_rows]) for g in (g_ple_proj, g_branch_na, g_branch_dil)],
                                  axis=0) for s in range(N_CHIPS)]
    early_buf = lax.dynamic_update_slice(early_buf, jnp.stack(small_rows), (0, 2 * D_MODEL + 2 * shard_rows, 0))
    early_tm = early_rows // 4
    swap_flight = _swap_start(early_buf, "early")
    d_y_na = _mm("d_branch_na", d_u_na, W["w_branch_na"], "nt", 1024,512, 1024, [BF], after=(swap_flight.token,))
    d_y_dil = _mm("d_branch_dil", d_u_dil, W["w_branch_dil"], "nt", 1024,256, 1024, [F32])

    dqa, dka, dva, dtab = _na_bwd(qkv, tab, d_y_na)
    early_g, early_got = _swap_wait(swap_flight, dqa, "early")
    early_pair, early_pair_b = _pair_sum(early_g, early_got, early_tm)
    scatter_flight = _scatter_start(early_pair_b, "early")
    d_rpb = _na_rpb_grad(dtab)[:, :2 * NA_WIN_ROWS - 1, :2 * NA_WIN_COLS - 1]

    do_res, dlse_res = _dil_merge_bwd(d_y_dil, o_nat, w_grp, TM, after=(scatter_flight.token,))
    d_dil = [_band_bwd(*dil_ops[g], do_res[g], dlse_res[g], g) for g in range(len(DIL_GROUPS))]

    dz_qkv = _qkv_unprep((dqa, dka, dva), d_dil, cos2, sin_signed, TM)
    g_in_parts = [_mm("g_in_qkv", a, dz_qkv, "tn", 1024, 1280, 1024,[F32]), _mm("g_in_gates", a, dz_gates, "tn", 1024, 1024, 1024,[F32])]
    me_chip = 2 * lax.axis_index("x") + lax.axis_index("y")
    early_mine = _chip_sum(lax.dynamic_index_in_dim(early_pair, me_chip, 0, keepdims=False),
                           _scatter_wait(scatter_flight, g_in_parts[1], "early"), early_tm)
    join_flight = _join_start(early_mine)
    in_cols = shards["w_in"].shape[1]

    def owner_columns(s):
        lo, hi, split = s * in_cols, (s + 1) * in_cols, g_in_parts[0].shape[1]
        pieces = [g_in_parts[0][:, lo:min(hi, split)]] if lo < split else []
        pieces += [g_in_parts[1][:, max(lo, split) - split:hi - split]] if hi > split else []
        return pieces[0] if len(pieces) == 1 else jnp.concatenate(pieces, axis=1)

    late_tm = 256
    late_swap = _swap_start(jnp.stack([owner_columns(s) for s in range(N_CHIPS)]), "late")
    d_a = _mm("d_in_qkv", dz_qkv, W["w_in"], "nt", 1024,1024, 1280, [F32], after=(late_swap.token, join_flight.token),
              b_view=(D_MODEL, (D_MODEL, 1280), lambda j, k: (j, k)))
    late_g, late_got = _swap_wait(late_swap, d_a, "late")
    late_pair, late_pair_b = _pair_sum(late_g, late_got, late_tm)
    late_scatter = _scatter_start(late_pair_b, "late")
    def first_bwd(dn_gates, dn_qkv, dh_out, h, g):
        dh, dg = _rms_bwd(dn_gates + dn_qkv, h, g)
        return dh_out + dh, dg

    grad_x, dg_mix = _mm("d_in_gates", dz_gates, w_gates, "nt", 512, 1024, 1024, [F32], epilogue=first_bwd,
                         extras=(d_a, dh1, xs), consts=(gm,), sums=[D_MODEL], after=(late_scatter.token,))
    early_shard = _join_wait(join_flight, grad_x)

    n_rpb = rpb.size
    rpb_rows = 4
    small = jnp.concatenate([
        dg_mix, dg_mlp, dg_ple, dg_final,
        jnp.pad(d_rpb.reshape(-1), (0, rpb_rows * D_MODEL - n_rpb)).reshape(rpb_rows, D_MODEL),
        jnp.pad(loss_part, ((0, 0), (0, D_MODEL - loss_part.shape[1]))),
        jnp.zeros((SMALL_ROWS - 5 - rpb_rows, D_MODEL), F32)], axis=0)
    out = {"grad": {}, "delta": {}, "new_m": {}, "new_v": {}}

    def update(n, g, row0):
        res = _adamw("adamw_" + n, g, row0, params[n], params["m_" + n], params["v_" + n])
        for kind, t in zip(("grad", "delta", "new_m", "new_v"), res, strict=True):
            out[kind][n] = t

    row0 = 0
    for n in REDUCE_EARLY:
        rows, cols = early_shapes[n]
        n_rows = rows * cols // PACK_W
        if cols == PACK_W:
            update(n, early_shard, row0)
        else:
            update(n, early_shard[row0:row0 + n_rows].reshape(rows, cols), 0)
        row0 += n_rows
    late_others = _scatter_wait(late_scatter, out["new_v"][REDUCE_EARLY[-1]], "late")
    late_mine = _chip_sum(lax.dynamic_index_in_dim(late_pair, me_chip, 0, keepdims=False), late_others, late_tm)
    small = _allreduce_small(small, after=(late_mine,))
    update("w_in", _join_halves(late_mine), 0)
    loss = small[4 + rpb_rows, 0]

    def small_pack(a0, a1, a2, a3, r):
        return jnp.concatenate([a0.reshape(1, -1), a1.reshape(1, -1), a2.reshape(1, -1), a3.reshape(1, -1),
                                jnp.pad(r.reshape(-1), (0, rpb_rows * D_MODEL - n_rpb)).reshape(rpb_rows, D_MODEL)], axis=0)

    small_res = _adamw("adamw_small", small, 0, small_pack(g_mix, g_mlp, g_ple, g_final, rpb)[None],
                       small_pack(m_g_mix, m_g_mlp, m_g_ple, m_g_final, m_rpb)[None],
                       small_pack(v_g_mix, v_g_mlp, v_g_ple, v_g_final, v_rpb)[None])

    def small_unpack(t):
        return {"g_mix": t[0].reshape(g_mix.shape), "g_mlp": t[1].reshape(g_mlp.shape), "g_ple": t[2].reshape(g_ple.shape),
                "g_final": t[3].reshape(g_final.shape), "rpb": t[4:].reshape(-1)[:n_rpb].reshape(rpb.shape)}

    for kind, t in zip(("grad", "delta", "new_m", "new_v"), small_res, strict=True):
        out[kind].update(small_unpack(t[0]))

    order = ["g_mix", "w_in", "rpb", "w_branch_na", "w_branch_dil", "w_out", "g_mlp", "w_up", "w_down", "g_ple",
             "w_ple_gate", "w_ple_proj", "g_final"]
    return (loss, grad_x[None], *[out["grad"][n] for n in order], *[out["delta"][n] for n in order],
            *[out["new_m"][n] for n in order], *[out["new_v"][n] for n in order])
```

```python
import functools
from typing import NamedTuple

import jax
import jax.numpy as jnp
from jax import lax
from jax.experimental import pallas as pl
from jax.experimental.pallas import tpu as pltpu

BF = jnp.bfloat16
F32 = jnp.float32
MESH = pl.DeviceIdType.MESH
ANY = pl.BlockSpec(memory_space=pl.ANY)

V7X_VMEM_BYTES = 64 * 1024 * 1024
VMEM_LIMIT = V7X_VMEM_BYTES - 16 * 1024 * 1024

D_MODEL = 1024
HEAD_DIM = 64
GRID_W = 64
NA_HEADS = 8
NA_WIN_ROWS = 8
NA_WIN_COLS = 16
NA_WIDTH = NA_HEADS * HEAD_DIM
DIL_GROUPS = ((128, 1), (512, 4), (2048, 16))
DIL_HPG = 4
DIL_HEADS = DIL_HPG * len(DIL_GROUPS)
DIL_WIDTH = DIL_HEADS * HEAD_DIM
DIL_OUT_WIDTH = DIL_HPG * HEAD_DIM
DIL_RADIUS = 64
QKV_WIDTH = 3 * NA_WIDTH + 3 * DIL_WIDTH
D_FF = 4 * D_MODEL
PLE_DIM = 256
ROPE_THETA = 10000.0
RMS_EPS = 1e-6
NEG_INF = -1e30
Q_SCALE = HEAD_DIM ** -0.5

ADAM_LR = 0.001
ADAM_B1 = 0.9
ADAM_B2 = 0.999
ADAM_EPS = 1e-08
ADAM_WD = 0.01
ADAM_STEP = 10

N_CHIPS = 4
N_DEV = 8
PACK_W = 1024
BIG = ("w_in", "w_branch_na", "w_branch_dil", "w_out", "w_up", "w_down", "w_ple_gate", "w_ple_proj")
GATHER_MIX = ("w_branch_na", "w_branch_dil", "w_out")
GATHER_MLP = ("w_up", "w_down", "w_ple_gate", "w_ple_proj")
REDUCE_EARLY = ("w_up", "w_down", "w_ple_gate", "w_out", "w_ple_proj", "w_branch_na", "w_branch_dil")
SMALL_ROWS = 16


def _cparams(sem=None):
    return pltpu.CompilerParams(dimension_semantics=sem, vmem_limit_bytes=VMEM_LIMIT)


def _mm(name, a, b, mode, tm, tn, tk, out_dtypes, epilogue=None, extras=(), consts=(), sums=(), after=(), into=None,
        b_view=None):
    if mode == "nn":
        (M, K), N = a.shape, b.shape[1]
    elif mode == "nt":
        (M, K), N = a.shape, b.shape[0]
    else:
        (K, M), N = a.shape, b.shape[1]
    if b_view is not None:
        N = b_view[0]
    tm, tn, tk = min(tm, M), min(tn, N), min(tk, K)
    assert M % tm == 0 and N % tn == 0 and K % tk == 0, (name, M, N, K, tm, tn, tk)
    if mode == "nn":
        a_spec = pl.BlockSpec((tm, tk), lambda i, j, k: (i, k))
        b_spec = pl.BlockSpec((tk, tn), lambda i, j, k: (k, j))
        dims = (((1,), (0,)), ((), ()))
    elif mode == "nt":
        a_spec = pl.BlockSpec((tm, tk), lambda i, j, k: (i, k))
        b_spec = pl.BlockSpec((tn, tk), lambda i, j, k: (j, k))
        dims = (((1,), (1,)), ((), ()))
    else:
        a_spec = pl.BlockSpec((tk, tm), lambda i, j, k: (k, i))
        b_spec = pl.BlockSpec((tk, tn), lambda i, j, k: (k, j))
        dims = (((0,), (0,)), ((), ()))
    if b_view is not None:
        b_spec = pl.BlockSpec(b_view[1], lambda i, j, k: b_view[2](j, k))
    nk = K // tk
    n_extra, n_const, n_out, n_sum = len(extras), len(consts), len(out_dtypes), len(sums)
    tile = pl.BlockSpec((tm, tn), lambda i, j, k: (i, j))
    assert not sums or tn == N, "row sums need whole rows in a tile"

    n_after = len(after)

    def body(a_ref, b_ref, *rest):
        extra_refs, rest = rest[:n_extra + n_const], rest[n_extra + n_const + n_after:]
        out_refs, sum_refs, acc = rest[:n_out], rest[n_out:n_out + n_sum], rest[-1]
        i, k = pl.program_id(0), pl.program_id(2)
        def product():
            return lax.dot_general(a_ref[...].astype(BF), b_ref[...].astype(BF), dims, preferred_element_type=F32)

        if nk > 1:
            @pl.when(k == 0)
            def _():
                acc[...] = jnp.zeros_like(acc)

            acc[...] += product()

        @pl.when(k == nk - 1)
        def _():
            total = product() if nk == 1 else acc[...]
            outs = (total,) if epilogue is None else epilogue(total, *[e[...] for e in extra_refs])
            for o_ref, val in zip(out_refs, outs[:n_out], strict=True):
                o_ref[...] = val.astype(o_ref.dtype).reshape(o_ref.shape)
            for s_ref, val in zip(sum_refs, outs[n_out:], strict=True):
                @pl.when(i == 0)
                def _():
                    s_ref[...] = val

                @pl.when(i != 0)
                def _():
                    s_ref[...] += val

    out_specs = [tile] * n_out + [pl.BlockSpec((1, c), lambda i, j, k: (0, 0)) for c in sums]
    out_shape = [jax.ShapeDtypeStruct((M, N), dt) for dt in out_dtypes] + [jax.ShapeDtypeStruct((1, c), F32) for c in sums]
    operands, aliases = [a, b, *extras, *consts, *after], {}
    in_specs = ([a_spec, b_spec] + [tile] * n_extra
                + [pl.BlockSpec(c.shape, functools.partial(lambda nd, i, j, k: (0,) * nd, c.ndim)) for c in consts] + [ANY] * n_after)
    if into is not None:
        assert n_out == 1
        target, block, index = into
        out_specs = [pl.BlockSpec(block, lambda i, j, k: index(i, j))]
        out_shape = [jax.ShapeDtypeStruct(target.shape, target.dtype)]
        if not isinstance(target, jax.ShapeDtypeStruct):
            aliases = {len(operands): 0}
            operands.append(target)
            in_specs.append(ANY)
            n_after += 1

    outs = pl.pallas_call(
        body, name=name, grid=(M // tm, N // tn, nk),
        in_specs=in_specs, out_specs=out_specs, out_shape=out_shape,
        scratch_shapes=[pltpu.VMEM((tm, tn) if nk > 1 else (8, 128), F32)], input_output_aliases=aliases,
        compiler_params=_cparams(("arbitrary",) * 3 if sums else ("parallel", "parallel", "arbitrary")),
    )(*operands)
    return outs[0] if len(outs) == 1 else outs


def _row(arr, tm, col_block=None, width=None):
    width = arr.shape[1] if width is None else width
    cb = 0 if col_block is None else col_block
    return arr, pl.BlockSpec((tm, width), lambda i: (i, cb))


def _full(arr):
    nd = arr.ndim
    return arr, pl.BlockSpec(arr.shape, lambda i: (0,) * nd)


def _rowwise(name, body, T, tm, ins, outs, sums=(), after=()):
    n_in, n_out, n_sum, n_after = len(ins), len(outs), len(sums), len(after)

    def kern(*refs):
        in_refs, refs = refs[:n_in], refs[n_in + n_after:]
        out_refs, sum_refs = refs[:n_out], refs[n_out:]
        res = body(*[r[...] for r in in_refs])
        res = res if isinstance(res, tuple) else (res,)
        for o_ref, val in zip(out_refs, res[:n_out], strict=True):
            o_ref[...] = val.astype(o_ref.dtype)
        if n_sum:
            @pl.when(pl.program_id(0) == 0)
            def _():
                for s_ref in sum_refs:
                    s_ref[...] = jnp.zeros_like(s_ref)

            for s_ref, val in zip(sum_refs, res[n_out:], strict=True):
                s_ref[...] += val

    res = pl.pallas_call(
        kern, name=name, grid=(T // tm,),
        in_specs=[spec for _, spec in ins] + [ANY] * n_after,
        out_specs=[pl.BlockSpec((tm, c), lambda i: (i, 0)) for c, _ in outs]
        + [pl.BlockSpec((1, c), lambda i: (0, 0)) for c in sums],
        out_shape=[jax.ShapeDtypeStruct((T, c), dt) for c, dt in outs]
        + [jax.ShapeDtypeStruct((1, c), F32) for c in sums],
        compiler_params=_cparams(("arbitrary",)),
    )(*[a for a, _ in ins], *after)
    return res[0] if len(res) == 1 else res


def _sigmoid(x):
    return 1.0 / (1.0 + jnp.exp(-x))


def _rms(h):
    return lax.rsqrt(jnp.mean(h * h, axis=-1, keepdims=True) + RMS_EPS)


def _rms_bwd(dy, h, g):
    r = _rms(h)
    n = h * r
    dn = dy * g
    dh = r * (dn - n * jnp.mean(dn * n, axis=-1, keepdims=True))
    return dh, jnp.sum(dy * n, axis=0, keepdims=True)


def _rope(x, cos2, sin_signed):
    lane = lax.broadcasted_iota(jnp.int32, x.shape, 1)
    swapped = jnp.where((lane % HEAD_DIM) < HEAD_DIM // 2, pltpu.roll(x, 128 - HEAD_DIM // 2, 1), pltpu.roll(x, HEAD_DIM // 2, 1))
    return x * cos2 + swapped * sin_signed


NA_KEYS = NA_WIN_ROWS * GRID_W
NA_BASES = 8


def _na_row_geometry(r, rows):
    first = jnp.clip(r - NA_WIN_ROWS // 2, 0, rows - NA_WIN_ROWS)
    base = first - r + (NA_WIN_ROWS - 1)
    return pl.multiple_of(first * GRID_W, GRID_W), base


NA_ROWS_PER_STEP = 8
NA_BWD_ROWS_PER_STEP = 8


def _softmax_rows(s):
    p = jnp.exp(s - jnp.max(s, axis=-1, keepdims=True))
    return p / jnp.sum(p, axis=-1, keepdims=True)


def _na_probs(q, kw, bias):
    return _softmax_rows(lax.dot_general(q, kw, (((1,), (1,)), ((), ())), preferred_element_type=F32) + bias)


def _split_pair(t):
    first = lax.broadcasted_iota(jnp.int32, t.shape, 1) < HEAD_DIM
    zero = jnp.zeros_like(t)
    return jnp.where(first, t, zero), jnp.where(first, zero, t)


def _join_pair(a, b):
    return jnp.where(lax.broadcasted_iota(jnp.int32, a.shape, 1) < HEAD_DIM, a, b)


_NT = (((1,), (1,)), ((), ()))
_TN = (((0,), (0,)), ((), ()))


def _na_fwd(qkv, tab):
    T = qkv.shape[0]
    rows = T // GRID_W
    n_pairs = NA_WIDTH // 128

    def body(q_ref, k_ref, v_ref, tab_ref, y_ref):
        def step(it, carry):
            geo = [_na_row_geometry(it * NA_ROWS_PER_STEP + u, rows) for u in range(NA_ROWS_PER_STEP)]
            q0s = [pl.multiple_of((it * NA_ROWS_PER_STEP + u) * GRID_W, GRID_W) for u in range(NA_ROWS_PER_STEP)]
            ss = [lax.dot_general(jnp.concatenate(_split_pair(q_ref[pl.ds(q0, GRID_W), :] * Q_SCALE), axis=0),
                                  k_ref[pl.ds(k0, NA_KEYS), :], _NT, preferred_element_type=F32)
                  for q0, (k0, _) in zip(q0s, geo)]
            ps = [_softmax_rows(s + jnp.concatenate([tab_ref[0, base], tab_ref[1, base]], axis=0)) for s, (_, base) in zip(ss, geo)]
            ys = [jnp.dot(p.astype(BF), v_ref[pl.ds(k0, NA_KEYS), :], preferred_element_type=F32) for p, (k0, _) in zip(ps, geo)]
            for q0, y2 in zip(q0s, ys):
                y_ref[pl.ds(q0, GRID_W), :] = _join_pair(y2[:GRID_W], y2[GRID_W:]).astype(y_ref.dtype)
            return carry

        lax.fori_loop(0, rows // NA_ROWS_PER_STEP, step, 0)

    def cols(first):
        return pl.BlockSpec((T, 128), lambda j: (0, first + j))

    return pl.pallas_call(
        body, name="na_fwd", grid=(n_pairs,),
        in_specs=[cols(0), cols(n_pairs), cols(2 * n_pairs), pl.BlockSpec((2, NA_BASES, GRID_W, NA_KEYS), lambda j: (j, 0, 0, 0))],
        out_specs=cols(0), out_shape=jax.ShapeDtypeStruct((T, NA_WIDTH), BF),
        compiler_params=_cparams(("parallel",)),
    )(qkv, qkv, qkv, tab)


def _na_bwd(qkv, tab, do):
    T = qkv.shape[0]
    rows = T // GRID_W
    n_pairs = NA_WIDTH // 128

    def body(q_ref, k_ref, v_ref, tab_ref, do_ref, dq_ref, dk_ref, dv_ref, dtab_ref):
        dk_ref[...] = jnp.zeros_like(dk_ref)
        dv_ref[...] = jnp.zeros_like(dv_ref)
        dtab_ref[...] = jnp.zeros_like(dtab_ref)

        def step(it, carry):
            U = NA_BWD_ROWS_PER_STEP
            geo = [_na_row_geometry(it * U + u, rows) for u in range(U)]
            q0s = [pl.multiple_of((it * U + u) * GRID_W, GRID_W) for u in range(U)]
            q2s = [jnp.concatenate(_split_pair(q_ref[pl.ds(q0, GRID_W), :] * Q_SCALE), axis=0) for q0 in q0s]
            do2s = [jnp.concatenate(_split_pair(do_ref[pl.ds(q0, GRID_W), :]), axis=0) for q0 in q0s]
            ss = [lax.dot_general(q2, k_ref[pl.ds(k0, NA_KEYS), :], _NT, preferred_element_type=F32) for q2, (k0, _) in zip(q2s, geo)]
            dps = [lax.dot_general(do2, v_ref[pl.ds(k0, NA_KEYS), :], _NT, preferred_element_type=F32) for do2, (k0, _) in zip(do2s, geo)]
            ps = [_softmax_rows(s + jnp.concatenate([tab_ref[0, base], tab_ref[1, base]], axis=0)) for s, (_, base) in zip(ss, geo)]
            dss = [p * (dp - jnp.sum(dp * p, axis=-1, keepdims=True)) for p, dp in zip(ps, dps)]
            dvs = [lax.dot_general(p.astype(BF), do2, _TN, preferred_element_type=F32) for p, do2 in zip(ps, do2s)]
            dsbs = [ds.astype(BF) for ds in dss]
            dqs = [jnp.dot(dsb, k_ref[pl.ds(k0, NA_KEYS), :], preferred_element_type=F32) for dsb, (k0, _) in zip(dsbs, geo)]
            dks = [lax.dot_general(dsb, q2, _TN, preferred_element_type=F32) for dsb, q2 in zip(dsbs, q2s)]
            for u in range(U):
                k0, base = geo[u]
                dtab_ref[0, base] += dss[u][:GRID_W]
                dtab_ref[1, base] += dss[u][GRID_W:]
                dq_ref[pl.ds(q0s[u], GRID_W), :] = _join_pair(dqs[u][:GRID_W], dqs[u][GRID_W:])
                dk_ref[pl.ds(k0, NA_KEYS), :] += dks[u]
                dv_ref[pl.ds(k0, NA_KEYS), :] += dvs[u]
            return carry

        lax.fori_loop(0, rows // NA_BWD_ROWS_PER_STEP, step, 0)

    def cols(first):
        return pl.BlockSpec((T, 128), lambda j: (0, first + j))

    tabs = pl.BlockSpec((2, NA_BASES, GRID_W, NA_KEYS), lambda j: (j, 0, 0, 0))
    wide = jax.ShapeDtypeStruct((T, NA_WIDTH), F32)
    return pl.pallas_call(
        body, name="na_bwd", grid=(n_pairs,),
        in_specs=[cols(0), cols(n_pairs), cols(2 * n_pairs), tabs, cols(0)],
        out_specs=[cols(0), cols(0), cols(0), tabs],
        out_shape=[wide, wide, wide, jax.ShapeDtypeStruct((NA_HEADS, NA_BASES, GRID_W, NA_KEYS), F32)],
        compiler_params=_cparams(("parallel",)),
    )(qkv, qkv, qkv, tab, do)


def _na_bias_table(rpb):
    H, n_rows, n_cols = rpb.shape

    def body(r_ref, tab_ref):
        q = lax.broadcasted_iota(jnp.int32, (GRID_W, 128), 0)
        kc = lax.broadcasted_iota(jnp.int32, (GRID_W, 128), 1)
        first = jnp.clip(q - NA_WIN_COLS // 2, 0, GRID_W - NA_WIN_COLS)
        valid = (kc >= first) & (kc < first + NA_WIN_COLS)
        toeplitz = []
        for ro in range(n_rows):
            row = jnp.broadcast_to(r_ref[pl.ds(ro, 1), :], (GRID_W, 128))
            shifted = pltpu.roll(pltpu.roll(row, 128 - (NA_WIN_COLS - 1), 1), 0, 1, stride=1, stride_axis=0)
            toeplitz.append(jnp.where(valid, shifted, NEG_INF))
        for base in range(NA_BASES):
            for j in range(NA_WIN_ROWS // 2):
                even, odd = toeplitz[base + 2 * j], toeplitz[base + 2 * j + 1]
                tab_ref[base, :, pl.ds(j * 128, 128)] = jnp.where(kc < GRID_W, even, pltpu.roll(odd, GRID_W, 1))

    padded = jnp.pad(rpb, ((0, 0), (0, 16 - n_rows), (0, 128 - n_cols)))
    return pl.pallas_call(
        body, name="na_bias_table", grid=(H,),
        in_specs=[pl.BlockSpec((None, 16, 128), lambda h: (h, 0, 0))],
        out_specs=pl.BlockSpec((None, NA_BASES, GRID_W, NA_KEYS), lambda h: (h, 0, 0, 0)),
        out_shape=jax.ShapeDtypeStruct((H, NA_BASES, GRID_W, NA_KEYS), F32),
        compiler_params=_cparams(("parallel",)),
    )(padded)


def _na_rpb_grad(dtab):
    H = dtab.shape[0]
    n_rows = 2 * NA_WIN_ROWS - 1
    n_cols = 2 * NA_WIN_COLS - 1

    def body(d_ref, o_ref):
        lane = lax.broadcasted_iota(jnp.int32, (GRID_W, 128), 1)
        low = lane < GRID_W
        out_rows = []
        for ro in range(n_rows):
            acc = jnp.zeros((GRID_W, 128), F32)
            for base in range(NA_BASES):
                i = ro - base
                if not 0 <= i < NA_WIN_ROWS:
                    continue
                pair = d_ref[base, :, pl.ds((i // 2) * 128, 128)]
                if i % 2:
                    pair = pltpu.roll(pair, GRID_W, 1)
                acc = acc + jnp.where(low, pair, 0.0)
            skew = pltpu.roll(acc, 0, 1, stride=1, stride_axis=0)
            diag = jnp.sum(skew, axis=0, keepdims=True)
            out_rows.append(pltpu.roll(jnp.broadcast_to(diag, (8, 128)), 128 - (GRID_W - NA_WIN_COLS), 1)[:1])
        out_rows.append(jnp.zeros((1, 128), F32))
        res = jnp.concatenate(out_rows, axis=0)
        o_ref[...] = jnp.where(lax.broadcasted_iota(jnp.int32, res.shape, 1) < n_cols, res, 0.0)

    return pl.pallas_call(
        body, name="na_rpb_grad", grid=(H,),
        in_specs=[pl.BlockSpec((None, NA_BASES, GRID_W, NA_KEYS), lambda h: (h, 0, 0, 0))],
        out_specs=pl.BlockSpec((None, n_rows + 1, 128), lambda h: (h, 0, 0)),
        out_shape=jax.ShapeDtypeStruct((H, n_rows + 1, 128), F32),
        compiler_params=_cparams(("parallel",)),
    )(jnp.flip(dtab, axis=2))


BAND_Q = 128
BAND_KEYS = BAND_Q + 2 * DIL_RADIUS


def _band_geometry(n, L):
    q0 = pl.multiple_of(n * BAND_Q, BAND_Q)
    k0 = pl.multiple_of(jnp.clip(q0 - DIL_RADIUS, 0, L - BAND_KEYS), DIL_RADIUS)
    qi = q0 + lax.broadcasted_iota(jnp.int32, (BAND_Q, BAND_KEYS), 0)
    kj = k0 + lax.broadcasted_iota(jnp.int32, (BAND_Q, BAND_KEYS), 1)
    return q0, k0, jnp.abs(qi - kj) <= DIL_RADIUS


DIL_PAIRS = DIL_OUT_WIDTH // 128


def _residue_shape(dil, T, dtype):
    return jax.ShapeDtypeStruct((DIL_PAIRS, dil, T // dil, 128), dtype)


def _residue_tile(dil, tm):
    return pl.BlockSpec((DIL_PAIRS, dil, tm // dil, 128), lambda i: (0, 0, i, 0))


def _to_natural(ref, scratch, dil, tm):
    tiles = []
    for pair in range(DIL_PAIRS):
        if dil == 1:
            tiles.append(ref[pair, 0].astype(F32))
            continue
        for r in range(dil):
            scratch[pl.ds(r, tm // dil, stride=dil), :] = ref[pair, r].astype(F32)
        tiles.append(scratch[...])
    return tiles


def _from_natural(tile, scratch, ref, pair, dil, tm):
    if dil == 1:
        ref[pair, 0] = tile.astype(ref.dtype)
        return
    scratch[...] = tile
    for r in range(dil):
        ref[pair, r] = scratch[pl.ds(r, tm // dil, stride=dil), :].astype(ref.dtype)


def _band_specs(group, T):
    dil = DIL_GROUPS[group][1]
    L = T // dil
    assert L % BAND_Q == 0 and L >= BAND_KEYS, (T, dil)
    return L, (dil * DIL_PAIRS,), pl.BlockSpec((None, None, L, 128), lambda s: (s % DIL_PAIRS, s // DIL_PAIRS, 0, 0))


BAND_BLOCKS_PER_STEP = 4


def _band_softmax(s, valid):
    s = jnp.where(valid, s, NEG_INF)
    m = jnp.max(s, axis=-1, keepdims=True)
    p = jnp.exp(s - m)
    l = jnp.sum(p, axis=-1, keepdims=True)
    return p / l, m + jnp.log(l)


def _band_fwd(q, k, v, group):
    T = q.shape[1] * q.shape[2]
    L, grid, spec = _band_specs(group, T)
    U = min(BAND_BLOCKS_PER_STEP, L // BAND_Q)

    def body(q_ref, k_ref, v_ref, o_ref, lse_ref):
        def step(it, carry):
            geo = [_band_geometry(it * U + u, L) for u in range(U)]
            ss = [lax.dot_general(jnp.concatenate(_split_pair(q_ref[pl.ds(q0, BAND_Q), :]), axis=0),
                                  k_ref[pl.ds(k0, BAND_KEYS), :], _NT, preferred_element_type=F32) for q0, k0, _ in geo]
            pls = [_band_softmax(s, jnp.concatenate([valid, valid], axis=0)) for s, (_, _, valid) in zip(ss, geo)]
            os = [jnp.dot(p.astype(BF), v_ref[pl.ds(k0, BAND_KEYS), :], preferred_element_type=F32) for (p, _), (_, k0, _) in zip(pls, geo)]
            for (q0, _, _), o2, (_, lse) in zip(geo, os, pls):
                o_ref[pl.ds(q0, BAND_Q), :] = _join_pair(o2[:BAND_Q], o2[BAND_Q:])
                lse2 = jnp.broadcast_to(lse, (2 * BAND_Q, 128))
                lse_ref[pl.ds(q0, BAND_Q), :] = _join_pair(lse2[:BAND_Q], lse2[BAND_Q:])
            return carry

        lax.fori_loop(0, L // (BAND_Q * U), step, 0)

    res = _residue_shape(DIL_GROUPS[group][1], T, F32)
    return pl.pallas_call(
        body, name=f"band_fwd_g{group}", grid=grid,
        in_specs=[spec] * 3, out_specs=[spec] * 2, out_shape=[res, res],
        compiler_params=_cparams(("parallel",)),
    )(q, k, v)


def _band_bwd(q, k, v, do, dlse, group):
    T = q.shape[1] * q.shape[2]
    L, grid, spec = _band_specs(group, T)
    U = min(BAND_BLOCKS_PER_STEP, L // BAND_Q)

    def body(q_ref, k_ref, v_ref, do_ref, dlse_ref, dq_ref, dk_ref, dv_ref):
        dk_ref[...] = jnp.zeros_like(dk_ref)
        dv_ref[...] = jnp.zeros_like(dv_ref)

        def step(it, carry):
            geo = [_band_geometry(it * U + u, L) for u in range(U)]
            q2s = [jnp.concatenate(_split_pair(q_ref[pl.ds(q0, BAND_Q), :]), axis=0) for q0, _, _ in geo]
            do2s = [jnp.concatenate(_split_pair(do_ref[pl.ds(q0, BAND_Q), :]), axis=0) for q0, _, _ in geo]
            ss = [lax.dot_general(q2, k_ref[pl.ds(k0, BAND_KEYS), :], _NT, preferred_element_type=F32) for q2, (_, k0, _) in zip(q2s, geo)]
            dps = [lax.dot_general(do2, v_ref[pl.ds(k0, BAND_KEYS), :], _NT, preferred_element_type=F32) for do2, (_, k0, _) in zip(do2s, geo)]
            ps = [_band_softmax(s, jnp.concatenate([valid, valid], axis=0))[0] for s, (_, _, valid) in zip(ss, geo)]
            dss = []
            for p, dp, (q0, _, _) in zip(ps, dps, geo):
                dl = dlse_ref[pl.ds(q0, BAND_Q), :]
                dl2 = jnp.concatenate([dl[:, :1], dl[:, HEAD_DIM:HEAD_DIM + 1]], axis=0)
                dss.append(p * (dp - jnp.sum(dp * p, axis=-1, keepdims=True) + dl2))
            dvs = [lax.dot_general(p.astype(BF), do2, _TN, preferred_element_type=F32) for p, do2 in zip(ps, do2s)]
            dsbs = [ds.astype(BF) for ds in dss]
            dqs = [jnp.dot(dsb, k_ref[pl.ds(k0, BAND_KEYS), :], preferred_element_type=F32) for dsb, (_, k0, _) in zip(dsbs, geo)]
            dks = [lax.dot_general(dsb, q2, _TN, preferred_element_type=F32) for dsb, q2 in zip(dsbs, q2s)]
            for u, (q0, k0, _) in enumerate(geo):
                dq_ref[pl.ds(q0, BAND_Q), :] = _join_pair(dqs[u][:BAND_Q], dqs[u][BAND_Q:])
                dk_ref[pl.ds(k0, BAND_KEYS), :] += dks[u]
                dv_ref[pl.ds(k0, BAND_KEYS), :] += dvs[u]
            return carry

        lax.fori_loop(0, L // (BAND_Q * U), step, 0)

    res = _residue_shape(DIL_GROUPS[group][1], T, F32)
    return pl.pallas_call(
        body, name=f"band_bwd_g{group}", grid=grid,
        in_specs=[spec] * 5, out_specs=[spec] * 3, out_shape=[res] * 3,
        compiler_params=_cparams(("parallel",)),
    )(q, k, v, do, dlse)


def _head_sums(t):
    head = lax.broadcasted_iota(jnp.int32, t.shape, 1) // HEAD_DIM
    out = jnp.zeros_like(t)
    for h in range(t.shape[1] // HEAD_DIM):
        mine = head == h
        out = jnp.where(mine, jnp.sum(jnp.where(mine, t, 0.0), axis=-1, keepdims=True), out)
    return out


def _dil_merge_fwd(os, lses, T, tm):
    G = len(DIL_GROUPS)
    W = DIL_OUT_WIDTH
    dils = [d for _, d in DIL_GROUPS]

    def body(*refs):
        o_refs, lse_refs = refs[:G], refs[G:2 * G]
        y_ref, w_refs, on_refs, scratch = refs[2 * G], refs[2 * G + 1:3 * G + 1], refs[3 * G + 1:4 * G + 1], refs[-1]
        o = [jnp.concatenate(_to_natural(r, scratch, d, tm), axis=1) for r, d in zip(o_refs, dils)]
        ls = [jnp.concatenate(_to_natural(r, scratch, d, tm), axis=1) for r, d in zip(lse_refs, dils)]
        m = functools.reduce(jnp.maximum, ls)
        es = [jnp.exp(l - m) for l in ls]
        tot = functools.reduce(jnp.add, es)
        ws = [e / tot for e in es]
        y_ref[...] = functools.reduce(jnp.add, [w * t for w, t in zip(ws, o)]).astype(y_ref.dtype)
        for g in range(G):
            w_refs[g][...] = ws[g]
            on_refs[g][...] = o[g]

    nat = pl.BlockSpec((tm, W), lambda i: (i, 0))
    res = pl.pallas_call(
        body, name="dil_merge_fwd", grid=(T // tm,),
        in_specs=[_residue_tile(d, tm) for d in dils] * 2,
        out_specs=[nat] * (2 * G + 1),
        out_shape=[jax.ShapeDtypeStruct((T, W), BF)] + [jax.ShapeDtypeStruct((T, W), F32)] * (2 * G),
        scratch_shapes=[pltpu.VMEM((tm, 128), F32)],
        compiler_params=_cparams(("parallel",)),
    )(*os, *lses)
    return res[0], res[1:G + 1], res[G + 1:]


def _dil_merge_bwd(dy, os, ws, tm, after=()):
    G = len(DIL_GROUPS)
    T, W = dy.shape
    dils = [d for _, d in DIL_GROUPS]
    n_after = len(after)

    def body(*refs):
        dyt = refs[0][...]
        o, w = [r[...] for r in refs[1:G + 1]], [r[...] for r in refs[G + 1:2 * G + 1]]
        refs = refs[2 * G + 1 + n_after:]
        do_refs, dlse_refs, scratch = refs[:G], refs[G:2 * G], refs[-1]
        dws = [_head_sums(dyt * t) for t in o]
        mean = functools.reduce(jnp.add, [a * b for a, b in zip(w, dws)])
        for g, d in enumerate(dils):
            do, dlse = w[g] * dyt, w[g] * (dws[g] - mean)
            for pair in range(DIL_PAIRS):
                cols = slice(pair * 128, (pair + 1) * 128)
                _from_natural(do[:, cols], scratch, do_refs[g], pair, d, tm)
                _from_natural(dlse[:, cols], scratch, dlse_refs[g], pair, d, tm)

    nat = pl.BlockSpec((tm, W), lambda i: (i, 0))
    res = pl.pallas_call(
        body, name="dil_merge_bwd", grid=(T // tm,),
        in_specs=[nat] * (2 * G + 1) + [ANY] * n_after,
        out_specs=[_residue_tile(d, tm) for d in dils] * 2,
        out_shape=[_residue_shape(d, T, BF) for d in dils] + [_residue_shape(d, T, F32) for d in dils],
        scratch_shapes=[pltpu.VMEM((tm, 128), F32)],
        compiler_params=_cparams(("parallel",)),
    )(dy, *os, *ws, *after)
    return res[:G], res[G:]


def _qkv_prep(z, cos2, sin_signed, tm):
    T = z.shape[0]
    G = len(DIL_GROUPS)
    dils = [d for _, d in DIL_GROUPS]
    n_dil_blocks = 3 * DIL_WIDTH // 128

    def body(*refs):
        blocks = refs[:n_dil_blocks]
        cos_ref, sin_ref = refs[n_dil_blocks], refs[1 + n_dil_blocks]
        outs = refs[2 + n_dil_blocks:]
        for part in range(3):
            for g, d in enumerate(dils):
                out = outs[g * 3 + part]
                for pair in range(DIL_PAIRS):
                    blk = blocks[part * (DIL_WIDTH // 128) + g * DIL_PAIRS + pair]
                    for r in range(d):
                        rows = pl.ds(r, tm // d, stride=d) if d > 1 else slice(None)
                        x = blk[rows, :]
                        if part < 2:
                            x = _rope(x, cos_ref[rows, :], sin_ref[rows, :])
                        if part == 0:
                            x = x * Q_SCALE
                        out[pair, r] = x.astype(out.dtype)

    lane_block = [pl.BlockSpec((tm, 128), functools.partial(lambda b, i: (i, b), b)) for b in range(n_dil_blocks)]
    tab = pl.BlockSpec((tm, 128), lambda i: (i, 0))
    res = pl.pallas_call(
        body, name="qkv_prep", grid=(T // tm,),
        in_specs=lane_block + [tab, tab],
        out_specs=[_residue_tile(d, tm) for d in dils for _ in range(3)],
        out_shape=[_residue_shape(d, T, BF) for d in dils for _ in range(3)],
        compiler_params=_cparams(("parallel",)),
    )(*[z] * n_dil_blocks, cos2, sin_signed)
    return [res[3 * g:3 + 3 * g] for g in range(G)]


def _qkv_unprep(d_na, d_dil, cos2, sin_signed, tm, after=()):
    T = d_na[0].shape[0]
    G = len(DIL_GROUPS)
    dils = [d for _, d in DIL_GROUPS]
    n_after = len(after)

    def body(*refs):
        dq, dk, dv = (r[...] for r in refs[:3])
        res_refs = refs[3:3 + 3 * G]
        cs, sn = refs[3 + 3 * G][...], refs[4 + 3 * G][...]
        out, scratch = refs[5 + 3 * G + n_after], refs[-1]
        cols = [dq * Q_SCALE, dk, dv]
        for part in range(3):
            for g, d in enumerate(dils):
                for x in _to_natural(res_refs[g * 3 + part], scratch, d, tm):
                    if part < 2:
                        x = _rope(x, cs, -sn)
                    cols.append(x * Q_SCALE if part == 0 else x)
        out[...] = jnp.concatenate(cols, axis=1).astype(out.dtype)

    wide = pl.BlockSpec((tm, NA_WIDTH), lambda i: (i, 0))
    tab = pl.BlockSpec((tm, 128), lambda i: (i, 0))
    return pl.pallas_call(
        body, name="qkv_unprep", grid=(T // tm,),
        in_specs=[wide] * 3 + [_residue_tile(d, tm) for d in dils for _ in range(3)] + [tab, tab] + [ANY] * n_after,
        out_specs=pl.BlockSpec((tm, QKV_WIDTH), lambda i: (i, 0)),
        out_shape=jax.ShapeDtypeStruct((T, QKV_WIDTH), BF),
        scratch_shapes=[pltpu.VMEM((tm, 128), F32)],
        compiler_params=_cparams(("parallel",)),
    )(*d_na, *[t for g in range(G) for t in d_dil[g]], cos2, sin_signed, *after)


def _rope_tables(positions):
    half = HEAD_DIM // 2
    inv_freq = ROPE_THETA ** (-jnp.arange(half, dtype=F32) / half)
    ang = positions.astype(F32)[:, None] * inv_freq
    cos, sin = jnp.cos(ang), jnp.sin(ang)
    return jnp.tile(jnp.concatenate([cos, cos], axis=1), (1, 2)), jnp.tile(jnp.concatenate([-sin, sin], axis=1), (1, 2))


def _pack_rows(t):
    return t.reshape(-1, PACK_W)


def _me():
    return lax.axis_index("x"), lax.axis_index("y"), lax.axis_index("c")


def _other_chips(x, y):
    return [(1 - x, y), (x, 1 - y), (1 - x, 1 - y)]


def _pair_sum(g, got, tm):
    S, R, W = g.shape
    half = R // 2
    nb = half // tm

    def body(c_ref, g_ref, got_ref, o_ref, ob_ref):
        tot = g_ref[...] + got_ref[...]
        o_ref[...] = tot
        ob_ref[...] = tot.astype(ob_ref.dtype)

    tile = pl.BlockSpec((None, tm, W), lambda s, i, c_ref: (s, i, 0))
    return pl.pallas_call(
        body, name="pair_sum",
        grid_spec=pltpu.PrefetchScalarGridSpec(
            num_scalar_prefetch=1, grid=(S, nb),
            in_specs=[pl.BlockSpec((None, tm, W), lambda s, i, c_ref: (s, c_ref[0] * nb + i, 0)), tile],
            out_specs=[tile, tile]),
        out_shape=[jax.ShapeDtypeStruct((S, half, W), F32), jax.ShapeDtypeStruct((S, half, W), BF)],
        compiler_params=_cparams(("parallel", "parallel")),
    )(lax.axis_index("c").reshape(1).astype(jnp.int32), g, got)


def _chip_sum(own, others, tm):
    n, h, W = others.shape
    nb = h // tm

    def body(c_ref, own_ref, p_ref, o_ref):
        o_ref[...] = ((own_ref[...] + p_ref[0].astype(F32)) + p_ref[1].astype(F32)) + p_ref[2].astype(F32)

    return pl.pallas_call(
        body, name="chip_sum",
        grid_spec=pltpu.PrefetchScalarGridSpec(
            num_scalar_prefetch=1, grid=(nb,),
            in_specs=[pl.BlockSpec((tm, W), lambda i, c_ref: (i, 0)), pl.BlockSpec((n, tm, W), lambda i, c_ref: (0, i, 0))],
            out_specs=pl.BlockSpec((tm, W), lambda i, c_ref: (c_ref[0] * nb + i, 0))),
        out_shape=jax.ShapeDtypeStruct((2 * h, W), F32),
        compiler_params=_cparams(("parallel",)),
    )(lax.axis_index("c").reshape(1).astype(jnp.int32), own, others)


def _join_halves(shard):
    h = shard.shape[0] // 2

    def body(in_ref, out_ref, send_sem, recv_sem):
        x, y, c = _me()
        cp = pltpu.make_async_remote_copy(
            src_ref=in_ref.at[pl.ds(c * h, h), :], dst_ref=out_ref.at[pl.ds(c * h, h), :],
            send_sem=send_sem, recv_sem=recv_sem, device_id=(x, y, 1 - c), device_id_type=MESH)
        cp.start()
        pltpu.make_async_remote_copy(
            src_ref=in_ref.at[pl.ds(c * h, h), :], dst_ref=out_ref.at[pl.ds((1 - c) * h, h), :],
            send_sem=send_sem, recv_sem=recv_sem, device_id=(x, y, 1 - c), device_id_type=MESH).wait_recv()
        cp.wait_send()

    return pl.pallas_call(
        body, name="join_halves", in_specs=[ANY], out_specs=ANY,
        out_shape=jax.ShapeDtypeStruct(shard.shape, shard.dtype), input_output_aliases={0: 0},
        scratch_shapes=[pltpu.SemaphoreType.DMA, pltpu.SemaphoreType.DMA],
    )(shard)


def _allreduce_small(s, after=()):
    R, W = s.shape
    n_after = len(after)

    def body(s_ref, *rest):
        o_ref, buf, send_sems, recv_sems = rest[n_after:]
        x, y, c = _me()
        me = 4 * x + 2 * y + c
        buf[me] = s_ref[...]
        peers = [((x + fx) % 2, (y + fy) % 2, (c + fc) % 2) for fx in range(2) for fy in range(2) for fc in range(2)][1:]
        sends = [pltpu.make_async_remote_copy(
            src_ref=s_ref, dst_ref=buf.at[me], send_sem=send_sems.at[k], recv_sem=recv_sems.at[k],
            device_id=peer, device_id_type=MESH) for k, peer in enumerate(peers)]
        for cp in sends:
            cp.start()
        for k, peer in enumerate(peers):
            pltpu.make_async_remote_copy(
                src_ref=s_ref, dst_ref=buf.at[4 * peer[0] + 2 * peer[1] + peer[2]], send_sem=send_sems.at[k],
                recv_sem=recv_sems.at[k], device_id=peer, device_id_type=MESH).wait_recv()
        for cp in sends:
            cp.wait_send()
        total = buf[0]
        for d in range(1, N_DEV):
            total = total + buf[d]
        o_ref[...] = total

    return pl.pallas_call(
        body, name="allreduce_small",
        in_specs=[pl.BlockSpec(memory_space=pltpu.VMEM)] + [ANY] * n_after, out_specs=pl.BlockSpec(memory_space=pltpu.VMEM),
        out_shape=jax.ShapeDtypeStruct((R, W), F32),
        scratch_shapes=[pltpu.VMEM((N_DEV, R, W), F32), pltpu.SemaphoreType.DMA((N_DEV - 1,)), pltpu.SemaphoreType.DMA((N_DEV - 1,))],
    )(s, *after)


HBM_SPEC = pl.BlockSpec(memory_space=pltpu.HBM)
SEM_SPEC = pl.BlockSpec(memory_space=pltpu.SEMAPHORE)
DATAFLOW = pltpu.SideEffectType.DATAFLOW_SIDE_EFFECTING


class _InFlight(NamedTuple):
    sems: tuple
    src: jax.Array
    land: jax.Array
    token: jax.Array


def _split_start(name, src, land_shape, land_dtype, n, copies, after=()):
    n_after = len(after)

    def body(src_ref, land_ref, *rest):
        rest = rest[n_after:]
        sems, token = rest[:2 * n], rest[-1]
        for k, (s, d, peer) in enumerate(copies(src_ref, land_ref)):
            pltpu.make_async_remote_copy(src_ref=s, dst_ref=d, send_sem=sems[k], recv_sem=sems[n + k],
                                         device_id=peer, device_id_type=MESH).start()
        token[...] = jnp.zeros_like(token)

    outs = pl.pallas_call(
        body, name=name,
        out_shape=(*[pltpu.SemaphoreType.DMA(())] * (2 * n), pltpu.HBM(src.shape, src.dtype), pltpu.HBM(land_shape, land_dtype),
                   jax.ShapeDtypeStruct((8, 128), F32)),
        in_specs=(HBM_SPEC, HBM_SPEC, *[ANY] * n_after),
        out_specs=(*[SEM_SPEC] * (2 * n), HBM_SPEC, HBM_SPEC, pl.BlockSpec(memory_space=pltpu.VMEM)),
        input_output_aliases={0: 2 * n, 1: 2 * n + 1},
        compiler_params=pltpu.CompilerParams(has_side_effects=DATAFLOW),
    )(pltpu.with_memory_space_constraint(src, pltpu.HBM), pltpu.with_memory_space_constraint(lax.empty(land_shape, land_dtype), pltpu.HBM),
      *after)
    return _InFlight(tuple(outs[:2 * n]), outs[2 * n], outs[2 * n + 1], outs[2 * n + 2])


def _split_wait(name, flight, after, n, copies):
    def body(src_ref, land_ref, *rest):
        sems = rest[:2 * n]
        for k, (s, d, peer) in enumerate(copies(src_ref, land_ref)):
            cp = pltpu.make_async_remote_copy(src_ref=s, dst_ref=d, send_sem=sems[k], recv_sem=sems[n + k],
                                              device_id=peer, device_id_type=MESH)
            cp.wait_send()
            cp.wait_recv()

    return pl.pallas_call(
        body, name=name,
        out_shape=(pltpu.HBM(flight.src.shape, flight.src.dtype), pltpu.HBM(flight.land.shape, flight.land.dtype)),
        in_specs=(HBM_SPEC, HBM_SPEC, *[SEM_SPEC] * (2 * n), ANY),
        out_specs=(HBM_SPEC, HBM_SPEC), input_output_aliases={0: 0, 1: 1},
        compiler_params=pltpu.CompilerParams(has_side_effects=DATAFLOW),
    )(flight.src, flight.land, *flight.sems, after)


def _gather_copies(src_ref, land_ref):
    x, y, c = _me()
    return [(src_ref, land_ref.at[2 * x + y], (*chip, c)) for chip in _other_chips(x, y)]


def _gather_start(packed, tag, after=()):
    return _split_start(f"gather_start_{tag}", packed, (N_CHIPS, *packed.shape), packed.dtype, 3, _gather_copies, after)


def _gather_wait(flight, after, tag):
    src, others = _split_wait(f"gather_wait_{tag}", flight, after, 3, _gather_copies)
    return lax.dynamic_update_slice(others, src[None], (2 * lax.axis_index("x") + lax.axis_index("y"), 0, 0))


def _across_copies(src_ref, land_ref):
    x, y, c = _me()
    half = src_ref.shape[0] // 2
    rows = pl.ds(c * half, half)
    return [(src_ref.at[rows, :], land_ref.at[2 * x + y, rows, :], (*chip, c)) for chip in _other_chips(x, y)]


def _to_sibling_copies(all_ref, unused_ref):
    x, y, c = _me()
    half = all_ref.shape[1] // 2
    places = [all_ref.at[2 * chip[0] + chip[1], pl.ds(c * half, half), :] for chip in _other_chips(x, y)]
    return [(place, place, (x, y, 1 - c)) for place in places]


def _gather_halves_start(shard):
    return _split_start("gather_in_across_start", shard, (N_CHIPS, *shard.shape), shard.dtype, 3, _across_copies)


def _gather_halves_finish(flight, after):
    shard, landed = _split_wait("gather_in_across_wait", flight, after, 3, _across_copies)
    forward = _split_start("gather_in_sibling_start", landed, (8, 128), landed.dtype, 3, _to_sibling_copies)
    others = _split_wait("gather_in_sibling_wait", forward, forward.token, 3, _to_sibling_copies)[0]
    return lax.dynamic_update_slice(others, shard[None], (2 * lax.axis_index("x") + lax.axis_index("y"), 0, 0))


def _swap_copies(src_ref, land_ref):
    x, y, c = _me()
    half = land_ref.shape[1]
    return [(src_ref.at[:, pl.ds((1 - c) * half, half), :], land_ref, (x, y, 1 - c))]


def _swap_start(g, tag):
    S, R, W = g.shape
    return _split_start(f"swap_halves_start_{tag}", g, (S, R // 2, W), g.dtype, 1, _swap_copies)


def _swap_wait(flight, after, tag):
    return _split_wait(f"swap_halves_wait_{tag}", flight, after, 1, _swap_copies)


def _scatter_copies(src_ref, land_ref):
    x, y, c = _me()
    return [(src_ref.at[2 * chip[0] + chip[1]], land_ref.at[j], (*chip, c)) for j, chip in enumerate(_other_chips(x, y))]


def _scatter_start(part, tag):
    S, h, W = part.shape
    return _split_start(f"scatter_chips_start_{tag}", part, (S - 1, h, W), part.dtype, 3, _scatter_copies)


def _scatter_wait(flight, after, tag):
    return _split_wait(f"scatter_chips_wait_{tag}", flight, after, 3, _scatter_copies)[1]


def _join_copies(shard_ref, unused_ref):
    x, y, c = _me()
    h = shard_ref.shape[0] // 2
    rows = shard_ref.at[pl.ds(c * h, h), :]
    return [(rows, rows, (x, y, 1 - c))]


def _join_start(shard):
    return _split_start("join_halves_start", shard, (8, 128), shard.dtype, 1, _join_copies)


def _join_wait(flight, after):
    return _split_wait("join_halves_wait", flight, after, 1, _join_copies)[0]


def _adamw(name, g, g_row0, w, m, v):
    _, R, C = w.shape
    tm = next(cand for cand in (368, 256, 128, 64, 32, 16, 8) if R % cand == 0)
    assert g_row0 % tm == 0 and g.shape[1] == C

    def body(g_ref, w_ref, m_ref, v_ref, go_ref, d_ref, mo_ref, vo_ref):
        gt = g_ref[...]
        mt = ADAM_B1 * m_ref[...] + (1.0 - ADAM_B1) * gt
        vt = ADAM_B2 * v_ref[...] + (1.0 - ADAM_B2) * jnp.square(gt)
        m_hat = mt / (1.0 - ADAM_B1 ** ADAM_STEP)
        v_hat = vt / (1.0 - ADAM_B2 ** ADAM_STEP)
        go_ref[...] = gt
        d_ref[...] = -ADAM_LR * (m_hat / (jnp.sqrt(v_hat) + ADAM_EPS) + ADAM_WD * w_ref[...])
        mo_ref[...] = mt
        vo_ref[...] = vt

    state = pl.BlockSpec((None, tm, C), lambda i: (0, i, 0))
    return pl.pallas_call(
        body, name=name, grid=(R // tm,),
        in_specs=[pl.BlockSpec((tm, C), lambda i: (g_row0 // tm + i, 0)), state, state, state],
        out_specs=[state] * 4, out_shape=[jax.ShapeDtypeStruct((1, R, C), F32)] * 4,
        compiler_params=_cparams(("parallel",)),
    )(g, w, m, v)


def _unpack_weights(gathered, names):
    S = gathered.shape[0]
    shard_shapes = {"w_in": (D_MODEL, (QKV_WIDTH + 2 * D_MODEL) // S), "w_branch_na": (NA_WIDTH, D_MODEL // S),
                    "w_branch_dil": (DIL_OUT_WIDTH, D_MODEL // S), "w_out": (D_MODEL // S, D_MODEL),
                    "w_up": (D_MODEL, D_FF // S), "w_down": (D_FF // S, D_MODEL),
                    "w_ple_gate": (D_MODEL // S, D_MODEL), "w_ple_proj": (PLE_DIM, D_MODEL // S)}
    col_sharded = {"w_in", "w_branch_na", "w_branch_dil", "w_up", "w_ple_proj"}
    out, r0 = {}, 0
    for name in names:
        rows, cols = shard_shapes[name]
        n = rows * cols // PACK_W
        t = gathered[:, r0:r0 + n, :].reshape(S, rows, cols)
        r0 += n
        out[name] = t.transpose(1, 0, 2).reshape(rows, S * cols) if name in col_sharded else t.reshape(S * rows, cols)
    return out


def kernel(x, p, positions, g_mix, w_in, rpb, w_branch_na, w_branch_dil, w_out, g_mlp, w_up, w_down, g_ple, w_ple_gate, w_ple_proj, g_final, loss_target, m_g_mix, m_w_in, m_rpb, m_w_branch_na, m_w_branch_dil, m_w_out, m_g_mlp, m_w_up, m_w_down, m_g_ple, m_w_ple_gate, m_w_ple_proj, m_g_final, v_g_mix, v_w_in, v_rpb, v_w_branch_na, v_w_branch_dil, v_w_out, v_g_mlp, v_w_up, v_w_down, v_g_ple, v_w_ple_gate, v_w_ple_proj, v_g_final):
    shards = {"w_in": w_in[0], "w_branch_na": w_branch_na[0], "w_branch_dil": w_branch_dil[0], "w_out": w_out[0],
              "w_up": w_up[0], "w_down": w_down[0], "w_ple_gate": w_ple_gate[0], "w_ple_proj": w_ple_proj[0]}
    params = {"w_in": w_in, "w_branch_na": w_branch_na, "w_branch_dil": w_branch_dil, "w_out": w_out, "w_up": w_up,
              "w_down": w_down, "w_ple_gate": w_ple_gate, "w_ple_proj": w_ple_proj,
              "m_w_in": m_w_in, "m_w_branch_na": m_w_branch_na, "m_w_branch_dil": m_w_branch_dil, "m_w_out": m_w_out,
              "m_w_up": m_w_up, "m_w_down": m_w_down, "m_w_ple_gate": m_w_ple_gate, "m_w_ple_proj": m_w_ple_proj,
              "v_w_in": v_w_in, "v_w_branch_na": v_w_branch_na, "v_w_branch_dil": v_w_branch_dil, "v_w_out": v_w_out,
              "v_w_up": v_w_up, "v_w_down": v_w_down, "v_w_ple_gate": v_w_ple_gate, "v_w_ple_proj": v_w_ple_proj}

    xs, ps, tgt = x[0], p[0, 0], loss_target[0]
    T = xs.shape[0]
    TM = 512
    gm, gl, gp, gf = g_mix, g_mlp, g_ple, g_final.reshape(1, D_MODEL)

    across = _gather_halves_start(shards["w_in"].astype(BF))
    a = _rowwise("norm_mix", lambda h, g: h * _rms(h) * g, T, TM, [_row(xs, TM), _full(gm)], [(D_MODEL, BF)],
                 after=(across.token,))
    cos2, sin_signed = _rope_tables(positions[0])
    tab = _na_bias_table(rpb[0])
    w_in_all = _gather_halves_finish(across, a)
    W = {"w_in": w_in_all.transpose(1, 0, 2).reshape(D_MODEL, -1)}
    mix_flight = _gather_start(jnp.concatenate([_pack_rows(shards[n].astype(BF)) for n in GATHER_MIX], axis=0), "mix",
                               after=(w_in_all,))
    rest_flight = _gather_start(jnp.concatenate([_pack_rows(shards[n].astype(BF)) for n in GATHER_MLP], axis=0), "mlp",
                                after=(mix_flight.token,))
    w_gates = W["w_in"][:, QKV_WIDTH:]

    n3 = 3 * NA_WIDTH
    qkv = _mm("in_na", a, W["w_in"], "nn", 1024, 768, 1024, [BF], after=(rest_flight.token,),
              b_view=(n3, (D_MODEL, 768), lambda j, k: (k, j)))
    z_dil = _mm("in_dil", a, W["w_in"], "nn", 1024, 768, 1024, [F32],
                b_view=(3 * DIL_WIDTH, (D_MODEL, 768), lambda j, k: (k, n3 // 768 + j)))
    z_gates = _mm("in_gates", a, w_gates, "nn", 1024,1024, 1024, [BF])

    dil_ops = _qkv_prep(z_dil, cos2, sin_signed, TM)
    y_na = _na_fwd(qkv, tab)
    band = [_band_fwd(*dil_ops[g], g) for g in range(len(DIL_GROUPS))]
    y_dil, w_grp, o_nat = _dil_merge_fwd([b[0] for b in band], [b[1] for b in band], T, TM)

    W.update(_unpack_weights(_gather_wait(mix_flight, y_dil, "mix"), GATHER_MIX))
    u_na = _mm("branch_na", y_na, W["w_branch_na"], "nn", 1024,1024, 512, [BF])
    u_dil = _mm("branch_dil", y_dil, W["w_branch_dil"], "nn", 1024,1024, 256, [BF])
    mixed = _rowwise(
        "gate_mix", lambda gn, gd, un, ud: _sigmoid(gn.astype(F32)) * un.astype(F32) + _sigmoid(gd.astype(F32)) * ud.astype(F32), T, TM,
        [_row(z_gates, TM, 0, D_MODEL), _row(z_gates, TM, 1, D_MODEL), _row(u_na, TM), _row(u_dil, TM)], [(D_MODEL, BF)])
    def add_norm(d, h, g):
        h = h + d
        return h, h * _rms(h) * g

    h1, cn = _mm("out_proj", mixed, W["w_out"], "nn", 512, 1024, 1024, [F32, BF], epilogue=add_norm, extras=(xs,), consts=(gl,))
    mlp_all = _gather_wait(rest_flight, cn, "mlp")
    W.update({n: t for n, t in _unpack_weights(mlp_all, GATHER_MLP).items() if n.startswith("w_ple")})
    chip_block = (None, D_MODEL, PACK_W)
    up, act = _mm("mlp_up", cn, mlp_all, "nn", 1024,1024, 1024, [BF, BF],
                  epilogue=lambda acc: (acc, jnp.square(jnp.maximum(acc, 0.0))), b_view=(D_FF, chip_block, lambda j, k: (j, 0, 0)))
    h2, en = _mm("mlp_down", act, mlp_all, "nn", 1024, 1024, 1024, [F32, BF], epilogue=add_norm, extras=(h1,), consts=(gp,),
                 b_view=(D_MODEL, chip_block, lambda j, k: (k, 1, 0)))
    pp = _mm("ple_proj", ps, W["w_ple_proj"], "nn", 1024,1024, 256, [F32])

    def head(gtt, h2t, ppt, tg, g):
        sg = _sigmoid(gtt)
        h3 = h2t + sg * ppt
        yo = h3 * _rms(h3) * g
        diff = yo - tg
        loss = 0.5 * jnp.sum(jnp.mean(jnp.square(diff), axis=-1, keepdims=True), axis=0, keepdims=True)
        dh3, dg = _rms_bwd(diff * (1.0 / D_MODEL), h3, g)
        return dh3, dh3 * ppt * sg * (1.0 - sg), dh3 * sg, jnp.broadcast_to(loss, (1, 128)), dg

    dh3, d_gt, d_pp, loss_part, dg_final = _mm(
        "ple_gate_loss_head", en, W["w_ple_gate"], "nn", 512, 1024, 1024, [F32, BF, BF], epilogue=head,
        extras=(h2, pp, tgt), consts=(gf,), sums=[128, D_MODEL])

    early_shapes = {n: shards[n].shape for n in REDUCE_EARLY}
    early_rows = sum(r * c for r, c in early_shapes.values()) // PACK_W
    shard_rows = D_MODEL // N_CHIPS
    early_buf = _mm("g_ple_gate", en, d_gt, "tn", 1024, 1024, 1024, [F32],
                    into=(jax.ShapeDtypeStruct((N_CHIPS, early_rows, PACK_W), F32), (N_CHIPS, shard_rows, PACK_W),
                          lambda i, j: (0, 2 * D_MODEL // shard_rows, 0)))
    g_ple_proj = _mm("g_ple_proj", ps, d_pp, "tn", 256, 1024, 1024,[F32])

    def add_norm_bwd(dn, dh_out, h, g):
        dh, dg = _rms_bwd(dn, h, g)
        dh = dh_out + dh
        return dh, dh, dg

    dh2, dh2_b, dg_ple = _mm("d_ple_gate", d_gt, W["w_ple_gate"], "nt", 512, 1024, 1024, [F32, BF],
                             epilogue=add_norm_bwd, extras=(dh3, h2), consts=(gp,), sums=[D_MODEL])
    d_up = _mm("d_mlp_down", dh2_b, mlp_all, "nt", 1024,1024, 1024, [BF], b_view=(D_FF, chip_block, lambda j, k: (j, 1, 0)),
               epilogue=lambda acc, u: (acc * (2.0 * jnp.maximum(u.astype(F32), 0.0)),), extras=(up,))
    early_buf = _mm("g_mlp_down", act, dh2_b, "tn", 1024, 1024, 1024,[F32],
                    into=(early_buf, (None, D_MODEL, PACK_W), lambda i, j: (i, 1, 0)))
    early_buf = _mm("g_mlp_up", cn, d_up, "tn", 1024, 1024, 1024,[F32],
                    into=(early_buf, (None, D_MODEL, PACK_W), lambda i, j: (j, 0, 0)))
    dh1, dh1_b, dg_mlp = _mm("d_mlp_up", d_up, mlp_all, "nt", 1024, 1024, 1024, [F32, BF], epilogue=add_norm_bwd,
                             b_view=(D_MODEL, chip_block, lambda j, k: (k, 0, 0)),
                             extras=(dh2, h1), consts=(gl,), sums=[D_MODEL])
    d_mixed = _mm("d_out_proj", dh1_b, W["w_out"], "nt", 1024,1024, 1024, [F32])
    early_buf = _mm("g_out_proj", mixed, dh1_b, "tn", 1024, 1024, 1024, [F32],
                    into=(early_buf, (N_CHIPS, shard_rows, PACK_W), lambda i, j: (0, 2 * D_MODEL // shard_rows + 1, 0)))

    def gate_bwd(dm, gn, gd, un, ud):
        gn, gd, un, ud = (t.astype(F32) for t in (gn, gd, un, ud))
        sn, sd = _sigmoid(gn), _sigmoid(gd)
        return jnp.concatenate([dm * un * sn * (1.0 - sn), dm * ud * sd * (1.0 - sd)], axis=1), dm * sn, dm * sd

    dz_gates, d_u_na, d_u_dil = _rowwise(
        "gate_mix_bwd", gate_bwd, T, TM,
        [_row(d_mixed, TM), _row(z_gates, TM, 0, D_MODEL), _row(z_gates, TM, 1, D_MODEL), _row(u_na, TM), _row(u_dil, TM)],
        [(2 * D_MODEL, BF), (D_MODEL, BF), (D_MODEL, BF)])
    g_branch_na = _mm("g_branch_na", y_na, d_u_na, "tn", 1024, 1024, 1024,[F32])
    g_branch_dil = _mm("g_branch_dil", y_dil, d_u_dil, "tn", 256, 1024, 1024,[F32])
    small_rows = [jnp.concatenate([_pack_rows(g[:, s * shard_rows:(s + 1) * shard_rows]) for g in (g_ple_proj, g_branch_na, g_branch_dil)],
                                  axis=0) for s in range(N_CHIPS)]
    early_buf = lax.dynamic_update_slice(early_buf, jnp.stack(small_rows), (0, 2 * D_MODEL + 2 * shard_rows, 0))
    early_tm = early_rows // 4
    swap_flight = _swap_start(early_buf, "early")
    d_y_na = _mm("d_branch_na", d_u_na, W["w_branch_na"], "nt", 1024,512, 1024, [BF], after=(swap_flight.token,))
    d_y_dil = _mm("d_branch_dil", d_u_dil, W["w_branch_dil"], "nt", 1024,256, 1024, [F32])

    dqa, dka, dva, dtab = _na_bwd(qkv, tab, d_y_na)
    early_g, early_got = _swap_wait(swap_flight, dqa, "early")
    early_pair, early_pair_b = _pair_sum(early_g, early_got, early_tm)
    scatter_flight = _scatter_start(early_pair_b, "early")
    d_rpb = _na_rpb_grad(dtab)[:, :2 * NA_WIN_ROWS - 1, :2 * NA_WIN_COLS - 1]

    do_res, dlse_res = _dil_merge_bwd(d_y_dil, o_nat, w_grp, TM, after=(scatter_flight.token,))
    d_dil = [_band_bwd(*dil_ops[g], do_res[g], dlse_res[g], g) for g in range(len(DIL_GROUPS))]

    dz_qkv = _qkv_unprep((dqa, dka, dva), d_dil, cos2, sin_signed, TM)
    g_in_parts = [_mm("g_in_qkv", a, dz_qkv, "tn", 1024, 1280, 1024,[F32]), _mm("g_in_gates", a, dz_gates, "tn", 1024, 1024, 1024,[F32])]
    me_chip = 2 * lax.axis_index("x") + lax.axis_index("y")
    early_mine = _chip_sum(lax.dynamic_index_in_dim(early_pair, me_chip, 0, keepdims=False),
                           _scatter_wait(scatter_flight, g_in_parts[1], "early"), early_tm)
    join_flight = _join_start(early_mine)
    in_cols = shards["w_in"].shape[1]

    def owner_columns(s):
        lo, hi, split = s * in_cols, (s + 1) * in_cols, g_in_parts[0].shape[1]
        pieces = [g_in_parts[0][:, lo:min(hi, split)]] if lo < split else []
        pieces += [g_in_parts[1][:, max(lo, split) - split:hi - split]] if hi > split else []
        return pieces[0] if len(pieces) == 1 else jnp.concatenate(pieces, axis=1)

    late_tm = in_cols // 4
    late_swap = _swap_start(jnp.stack([owner_columns(s).T for s in range(N_CHIPS)]), "late")
    d_a = _mm("d_in_qkv", dz_qkv, W["w_in"], "nt", 1024,1024, 1280, [F32], after=(late_swap.token, join_flight.token),
              b_view=(D_MODEL, (D_MODEL, 1280), lambda j, k: (j, k)))
    late_g, late_got = _swap_wait(late_swap, d_a, "late")
    late_pair, late_pair_b = _pair_sum(late_g, late_got, late_tm)
    late_scatter = _scatter_start(late_pair_b, "late")
    def first_bwd(dn_gates, dn_qkv, dh_out, h, g):
        dh, dg = _rms_bwd(dn_gates + dn_qkv, h, g)
        return dh_out + dh, dg

    grad_x, dg_mix = _mm("d_in_gates", dz_gates, w_gates, "nt", 512, 1024, 1024, [F32], epilogue=first_bwd,
                         extras=(d_a, dh1, xs), consts=(gm,), sums=[D_MODEL], after=(late_scatter.token,))
    early_shard = _join_wait(join_flight, grad_x)

    n_rpb = rpb.size
    rpb_rows = 4
    small = jnp.concatenate([
        dg_mix, dg_mlp, dg_ple, dg_final,
        jnp.pad(d_rpb.reshape(-1), (0, rpb_rows * D_MODEL - n_rpb)).reshape(rpb_rows, D_MODEL),
        jnp.pad(loss_part, ((0, 0), (0, D_MODEL - loss_part.shape[1]))),
        jnp.zeros((SMALL_ROWS - 5 - rpb_rows, D_MODEL), F32)], axis=0)
    out = {"grad": {}, "delta": {}, "new_m": {}, "new_v": {}}

    def update(n, g, row0):
        res = _adamw("adamw_" + n, g, row0, params[n], params["m_" + n], params["v_" + n])
        for kind, t in zip(("grad", "delta", "new_m", "new_v"), res, strict=True):
            out[kind][n] = t

    row0 = 0
    for n in REDUCE_EARLY:
        rows, cols = early_shapes[n]
        n_rows = rows * cols // PACK_W
        if cols == PACK_W:
            update(n, early_shard, row0)
        else:
            update(n, early_shard[row0:row0 + n_rows].reshape(rows, cols), 0)
        row0 += n_rows
    late_others = _scatter_wait(late_scatter, out["new_v"][REDUCE_EARLY[-1]], "late")
    late_mine = _chip_sum(lax.dynamic_index_in_dim(late_pair, me_chip, 0, keepdims=False), late_others, late_tm)
    small = _allreduce_small(small, after=(late_mine,))
    res = _adamw("adamw_w_in", _join_halves(late_mine), 0, *[jnp.swapaxes(params[n], 1, 2) for n in ("w_in", "m_w_in", "v_w_in")])
    for kind, t in zip(("grad", "delta", "new_m", "new_v"), res, strict=True):
        out[kind]["w_in"] = jnp.swapaxes(t, 1, 2)
    loss = small[4 + rpb_rows, 0]

    def small_pack(a0, a1, a2, a3, r):
        return jnp.concatenate([a0.reshape(1, -1), a1.reshape(1, -1), a2.reshape(1, -1), a3.reshape(1, -1),
                                jnp.pad(r.reshape(-1), (0, rpb_rows * D_MODEL - n_rpb)).reshape(rpb_rows, D_MODEL)], axis=0)

    small_res = _adamw("adamw_small", small, 0, small_pack(g_mix, g_mlp, g_ple, g_final, rpb)[None],
                       small_pack(m_g_mix, m_g_mlp, m_g_ple, m_g_final, m_rpb)[None],
                       small_pack(v_g_mix, v_g_mlp, v_g_ple, v_g_final, v_rpb)[None])

    def small_unpack(t):
        return {"g_mix": t[0].reshape(g_mix.shape), "g_mlp": t[1].reshape(g_mlp.shape), "g_ple": t[2].reshape(g_ple.shape),
                "g_final": t[3].reshape(g_final.shape), "rpb": t[4:].reshape(-1)[:n_rpb].reshape(rpb.shape)}

    for kind, t in zip(("grad", "delta", "new_m", "new_v"), small_res, strict=True):
        out[kind].update(small_unpack(t[0]))

    order = ["g_mix", "w_in", "rpb", "w_branch_na", "w_branch_dil", "w_out", "g_mlp", "w_up", "w_down", "g_ple",
             "w_ple_gate", "w_ple_proj", "g_final"]
    return (loss, grad_x[None], *[out["grad"][n] for n in order], *[out["delta"][n] for n in order],
            *[out["new_m"][n] for n in order], *[out["new_v"][n] for n in order])
```

```python
import functools
from typing import NamedTuple

import jax
import jax.numpy as jnp
from jax import lax
from jax.experimental import pallas as pl
from jax.experimental.pallas import tpu as pltpu

BF = jnp.bfloat16
F32 = jnp.float32
MESH = pl.DeviceIdType.MESH
ANY = pl.BlockSpec(memory_space=pl.ANY)

V7X_VMEM_BYTES = 64 * 1024 * 1024
VMEM_LIMIT = V7X_VMEM_BYTES - 16 * 1024 * 1024

D_MODEL = 1024
HEAD_DIM = 64
GRID_W = 64
NA_HEADS = 8
NA_WIN_ROWS = 8
NA_WIN_COLS = 16
NA_WIDTH = NA_HEADS * HEAD_DIM
DIL_GROUPS = ((128, 1), (512, 4), (2048, 16))
DIL_HPG = 4
DIL_HEADS = DIL_HPG * len(DIL_GROUPS)
DIL_WIDTH = DIL_HEADS * HEAD_DIM
DIL_OUT_WIDTH = DIL_HPG * HEAD_DIM
DIL_RADIUS = 64
QKV_WIDTH = 3 * NA_WIDTH + 3 * DIL_WIDTH
D_FF = 4 * D_MODEL
PLE_DIM = 256
ROPE_THETA = 10000.0
RMS_EPS = 1e-6
NEG_INF = -1e30
Q_SCALE = HEAD_DIM ** -0.5

ADAM_LR = 0.001
ADAM_B1 = 0.9
ADAM_B2 = 0.999
ADAM_EPS = 1e-08
ADAM_WD = 0.01
ADAM_STEP = 10

N_CHIPS = 4
N_DEV = 8
PACK_W = 1024
BIG = ("w_in", "w_branch_na", "w_branch_dil", "w_out", "w_up", "w_down", "w_ple_gate", "w_ple_proj")
GATHER_MIX = ("w_branch_na", "w_branch_dil", "w_out")
GATHER_MLP = ("w_up", "w_down", "w_ple_gate", "w_ple_proj")
REDUCE_EARLY = ("w_up", "w_down", "w_ple_gate", "w_out", "w_ple_proj", "w_branch_na", "w_branch_dil")
SMALL_ROWS = 16


def _cparams(sem=None):
    return pltpu.CompilerParams(dimension_semantics=sem, vmem_limit_bytes=VMEM_LIMIT)


def _mm(name, a, b, mode, tm, tn, tk, out_dtypes, epilogue=None, extras=(), consts=(), sums=(), after=(), into=None,
        b_view=None):
    if mode == "nn":
        (M, K), N = a.shape, b.shape[1]
    elif mode == "nt":
        (M, K), N = a.shape, b.shape[0]
    else:
        (K, M), N = a.shape, b.shape[1]
    if b_view is not None:
        N = b_view[0]
    tm, tn, tk = min(tm, M), min(tn, N), min(tk, K)
    assert M % tm == 0 and N % tn == 0 and K % tk == 0, (name, M, N, K, tm, tn, tk)
    if mode == "nn":
        a_spec = pl.BlockSpec((tm, tk), lambda i, j, k: (i, k))
        b_spec = pl.BlockSpec((tk, tn), lambda i, j, k: (k, j))
        dims = (((1,), (0,)), ((), ()))
    elif mode == "nt":
        a_spec = pl.BlockSpec((tm, tk), lambda i, j, k: (i, k))
        b_spec = pl.BlockSpec((tn, tk), lambda i, j, k: (j, k))
        dims = (((1,), (1,)), ((), ()))
    else:
        a_spec = pl.BlockSpec((tk, tm), lambda i, j, k: (k, i))
        b_spec = pl.BlockSpec((tk, tn), lambda i, j, k: (k, j))
        dims = (((0,), (0,)), ((), ()))
    if b_view is not None:
        b_spec = pl.BlockSpec(b_view[1], lambda i, j, k: b_view[2](j, k))
    nk = K // tk
    n_extra, n_const, n_out, n_sum = len(extras), len(consts), len(out_dtypes), len(sums)
    tile = pl.BlockSpec((tm, tn), lambda i, j, k: (i, j))
    assert not sums or tn == N, "row sums need whole rows in a tile"

    n_after = len(after)

    def body(a_ref, b_ref, *rest):
        extra_refs, rest = rest[:n_extra + n_const], rest[n_extra + n_const + n_after:]
        out_refs, sum_refs, acc = rest[:n_out], rest[n_out:n_out + n_sum], rest[-1]
        i, k = pl.program_id(0), pl.program_id(2)
        def product():
            return lax.dot_general(a_ref[...].astype(BF), b_ref[...].astype(BF), dims, preferred_element_type=F32)

        if nk > 1:
            @pl.when(k == 0)
            def _():
                acc[...] = jnp.zeros_like(acc)

            acc[...] += product()

        @pl.when(k == nk - 1)
        def _():
            total = product() if nk == 1 else acc[...]
            outs = (total,) if epilogue is None else epilogue(total, *[e[...] for e in extra_refs])
            for o_ref, val in zip(out_refs, outs[:n_out], strict=True):
                o_ref[...] = val.astype(o_ref.dtype).reshape(o_ref.shape)
            for s_ref, val in zip(sum_refs, outs[n_out:], strict=True):
                @pl.when(i == 0)
                def _():
                    s_ref[...] = val

                @pl.when(i != 0)
                def _():
                    s_ref[...] += val

    out_specs = [tile] * n_out + [pl.BlockSpec((1, c), lambda i, j, k: (0, 0)) for c in sums]
    out_shape = [jax.ShapeDtypeStruct((M, N), dt) for dt in out_dtypes] + [jax.ShapeDtypeStruct((1, c), F32) for c in sums]
    operands, aliases = [a, b, *extras, *consts, *after], {}
    in_specs = ([a_spec, b_spec] + [tile] * n_extra
                + [pl.BlockSpec(c.shape, functools.partial(lambda nd, i, j, k: (0,) * nd, c.ndim)) for c in consts] + [ANY] * n_after)
    if into is not None:
        assert n_out == 1
        target, block, index = into
        out_specs = [pl.BlockSpec(block, lambda i, j, k: index(i, j))]
        out_shape = [jax.ShapeDtypeStruct(target.shape, target.dtype)]
        if not isinstance(target, jax.ShapeDtypeStruct):
            aliases = {len(operands): 0}
            operands.append(target)
            in_specs.append(ANY)
            n_after += 1

    outs = pl.pallas_call(
        body, name=name, grid=(M // tm, N // tn, nk),
        in_specs=in_specs, out_specs=out_specs, out_shape=out_shape,
        scratch_shapes=[pltpu.VMEM((tm, tn) if nk > 1 else (8, 128), F32)], input_output_aliases=aliases,
        compiler_params=_cparams(("arbitrary",) * 3 if sums else ("parallel", "parallel", "arbitrary")),
    )(*operands)
    return outs[0] if len(outs) == 1 else outs


def _row(arr, tm, col_block=None, width=None):
    width = arr.shape[1] if width is None else width
    cb = 0 if col_block is None else col_block
    return arr, pl.BlockSpec((tm, width), lambda i: (i, cb))


def _full(arr):
    nd = arr.ndim
    return arr, pl.BlockSpec(arr.shape, lambda i: (0,) * nd)


def _rowwise(name, body, T, tm, ins, outs, sums=(), after=()):
    n_in, n_out, n_sum, n_after = len(ins), len(outs), len(sums), len(after)

    def kern(*refs):
        in_refs, refs = refs[:n_in], refs[n_in + n_after:]
        out_refs, sum_refs = refs[:n_out], refs[n_out:]
        res = body(*[r[...] for r in in_refs])
        res = res if isinstance(res, tuple) else (res,)
        for o_ref, val in zip(out_refs, res[:n_out], strict=True):
            o_ref[...] = val.astype(o_ref.dtype)
        if n_sum:
            @pl.when(pl.program_id(0) == 0)
            def _():
                for s_ref in sum_refs:
                    s_ref[...] = jnp.zeros_like(s_ref)

            for s_ref, val in zip(sum_refs, res[n_out:], strict=True):
                s_ref[...] += val

    res = pl.pallas_call(
        kern, name=name, grid=(T // tm,),
        in_specs=[spec for _, spec in ins] + [ANY] * n_after,
        out_specs=[pl.BlockSpec((tm, c), lambda i: (i, 0)) for c, _ in outs]
        + [pl.BlockSpec((1, c), lambda i: (0, 0)) for c in sums],
        out_shape=[jax.ShapeDtypeStruct((T, c), dt) for c, dt in outs]
        + [jax.ShapeDtypeStruct((1, c), F32) for c in sums],
        compiler_params=_cparams(("arbitrary",)),
    )(*[a for a, _ in ins], *after)
    return res[0] if len(res) == 1 else res


def _sigmoid(x):
    return 1.0 / (1.0 + jnp.exp(-x))


def _rms(h):
    return lax.rsqrt(jnp.mean(h * h, axis=-1, keepdims=True) + RMS_EPS)


def _rms_bwd(dy, h, g):
    r = _rms(h)
    n = h * r
    dn = dy * g
    dh = r * (dn - n * jnp.mean(dn * n, axis=-1, keepdims=True))
    return dh, jnp.sum(dy * n, axis=0, keepdims=True)


def _rope(x, cos2, sin_signed):
    lane = lax.broadcasted_iota(jnp.int32, x.shape, 1)
    swapped = jnp.where((lane % HEAD_DIM) < HEAD_DIM // 2, pltpu.roll(x, 128 - HEAD_DIM // 2, 1), pltpu.roll(x, HEAD_DIM // 2, 1))
    return x * cos2 + swapped * sin_signed


NA_KEYS = NA_WIN_ROWS * GRID_W
NA_BASES = 8


def _na_row_geometry(r, rows):
    first = jnp.clip(r - NA_WIN_ROWS // 2, 0, rows - NA_WIN_ROWS)
    base = first - r + (NA_WIN_ROWS - 1)
    return pl.multiple_of(first * GRID_W, GRID_W), base


NA_ROWS_PER_STEP = 8
NA_BWD_ROWS_PER_STEP = 8


def _softmax_rows(s):
    p = jnp.exp(s - jnp.max(s, axis=-1, keepdims=True))
    return p / jnp.sum(p, axis=-1, keepdims=True)


def _na_probs(q, kw, bias):
    return _softmax_rows(lax.dot_general(q, kw, (((1,), (1,)), ((), ())), preferred_element_type=F32) + bias)


def _split_pair(t):
    first = lax.broadcasted_iota(jnp.int32, t.shape, 1) < HEAD_DIM
    zero = jnp.zeros_like(t)
    return jnp.where(first, t, zero), jnp.where(first, zero, t)


def _join_pair(a, b):
    return jnp.where(lax.broadcasted_iota(jnp.int32, a.shape, 1) < HEAD_DIM, a, b)


_NT = (((1,), (1,)), ((), ()))
_TN = (((0,), (0,)), ((), ()))


def _na_fwd(qkv, tab):
    T = qkv.shape[0]
    rows = T // GRID_W
    n_pairs = NA_WIDTH // 128

    def body(q_ref, k_ref, v_ref, tab_ref, y_ref):
        def step(it, carry):
            geo = [_na_row_geometry(it * NA_ROWS_PER_STEP + u, rows) for u in range(NA_ROWS_PER_STEP)]
            q0s = [pl.multiple_of((it * NA_ROWS_PER_STEP + u) * GRID_W, GRID_W) for u in range(NA_ROWS_PER_STEP)]
            ss = [lax.dot_general(jnp.concatenate(_split_pair(q_ref[pl.ds(q0, GRID_W), :] * Q_SCALE), axis=0),
                                  k_ref[pl.ds(k0, NA_KEYS), :], _NT, preferred_element_type=F32)
                  for q0, (k0, _) in zip(q0s, geo)]
            ps = [_softmax_rows(s + jnp.concatenate([tab_ref[0, base], tab_ref[1, base]], axis=0)) for s, (_, base) in zip(ss, geo)]
            ys = [jnp.dot(p.astype(BF), v_ref[pl.ds(k0, NA_KEYS), :], preferred_element_type=F32) for p, (k0, _) in zip(ps, geo)]
            for q0, y2 in zip(q0s, ys):
                y_ref[pl.ds(q0, GRID_W), :] = _join_pair(y2[:GRID_W], y2[GRID_W:]).astype(y_ref.dtype)
            return carry

        lax.fori_loop(0, rows // NA_ROWS_PER_STEP, step, 0)

    def cols(first):
        return pl.BlockSpec((T, 128), lambda j: (0, first + j))

    return pl.pallas_call(
        body, name="na_fwd", grid=(n_pairs,),
        in_specs=[cols(0), cols(n_pairs), cols(2 * n_pairs), pl.BlockSpec((2, NA_BASES, GRID_W, NA_KEYS), lambda j: (j, 0, 0, 0))],
        out_specs=cols(0), out_shape=jax.ShapeDtypeStruct((T, NA_WIDTH), BF),
        compiler_params=_cparams(("parallel",)),
    )(qkv, qkv, qkv, tab)


def _na_bwd(qkv, tab, do):
    T = qkv.shape[0]
    rows = T // GRID_W
    n_pairs = NA_WIDTH // 128

    def body(q_ref, k_ref, v_ref, tab_ref, do_ref, dq_ref, dk_ref, dv_ref, dtab_ref):
        dk_ref[...] = jnp.zeros_like(dk_ref)
        dv_ref[...] = jnp.zeros_like(dv_ref)
        dtab_ref[...] = jnp.zeros_like(dtab_ref)

        def step(it, carry):
            U = NA_BWD_ROWS_PER_STEP
            geo = [_na_row_geometry(it * U + u, rows) for u in range(U)]
            q0s = [pl.multiple_of((it * U + u) * GRID_W, GRID_W) for u in range(U)]
            q2s = [jnp.concatenate(_split_pair(q_ref[pl.ds(q0, GRID_W), :] * Q_SCALE), axis=0) for q0 in q0s]
            do2s = [jnp.concatenate(_split_pair(do_ref[pl.ds(q0, GRID_W), :]), axis=0) for q0 in q0s]
            ss = [lax.dot_general(q2, k_ref[pl.ds(k0, NA_KEYS), :], _NT, preferred_element_type=F32) for q2, (k0, _) in zip(q2s, geo)]
            dps = [lax.dot_general(do2, v_ref[pl.ds(k0, NA_KEYS), :], _NT, preferred_element_type=F32) for do2, (k0, _) in zip(do2s, geo)]
            ps = [_softmax_rows(s + jnp.concatenate([tab_ref[0, base], tab_ref[1, base]], axis=0)) for s, (_, base) in zip(ss, geo)]
            dss = [p * (dp - jnp.sum(dp * p, axis=-1, keepdims=True)) for p, dp in zip(ps, dps)]
            dvs = [lax.dot_general(p.astype(BF), do2, _TN, preferred_element_type=F32) for p, do2 in zip(ps, do2s)]
            dsbs = [ds.astype(BF) for ds in dss]
            dqs = [jnp.dot(dsb, k_ref[pl.ds(k0, NA_KEYS), :], preferred_element_type=F32) for dsb, (k0, _) in zip(dsbs, geo)]
            dks = [lax.dot_general(dsb, q2, _TN, preferred_element_type=F32) for dsb, q2 in zip(dsbs, q2s)]
            for u in range(U):
                k0, base = geo[u]
                dtab_ref[0, base] += dss[u][:GRID_W]
                dtab_ref[1, base] += dss[u][GRID_W:]
                dq_ref[pl.ds(q0s[u], GRID_W), :] = _join_pair(dqs[u][:GRID_W], dqs[u][GRID_W:])
                dk_ref[pl.ds(k0, NA_KEYS), :] += dks[u]
                dv_ref[pl.ds(k0, NA_KEYS), :] += dvs[u]
            return carry

        lax.fori_loop(0, rows // NA_BWD_ROWS_PER_STEP, step, 0)

    def cols(first):
        return pl.BlockSpec((T, 128), lambda j: (0, first + j))

    tabs = pl.BlockSpec((2, NA_BASES, GRID_W, NA_KEYS), lambda j: (j, 0, 0, 0))
    wide = jax.ShapeDtypeStruct((T, NA_WIDTH), F32)
    return pl.pallas_call(
        body, name="na_bwd", grid=(n_pairs,),
        in_specs=[cols(0), cols(n_pairs), cols(2 * n_pairs), tabs, cols(0)],
        out_specs=[cols(0), cols(0), cols(0), tabs],
        out_shape=[wide, wide, wide, jax.ShapeDtypeStruct((NA_HEADS, NA_BASES, GRID_W, NA_KEYS), F32)],
        compiler_params=_cparams(("parallel",)),
    )(qkv, qkv, qkv, tab, do)


def _na_bias_table(rpb):
    H, n_rows, n_cols = rpb.shape

    def body(r_ref, tab_ref):
        q = lax.broadcasted_iota(jnp.int32, (GRID_W, 128), 0)
        kc = lax.broadcasted_iota(jnp.int32, (GRID_W, 128), 1)
        first = jnp.clip(q - NA_WIN_COLS // 2, 0, GRID_W - NA_WIN_COLS)
        valid = (kc >= first) & (kc < first + NA_WIN_COLS)
        toeplitz = []
        for ro in range(n_rows):
            row = jnp.broadcast_to(r_ref[pl.ds(ro, 1), :], (GRID_W, 128))
            shifted = pltpu.roll(pltpu.roll(row, 128 - (NA_WIN_COLS - 1), 1), 0, 1, stride=1, stride_axis=0)
            toeplitz.append(jnp.where(valid, shifted, NEG_INF))
        for base in range(NA_BASES):
            for j in range(NA_WIN_ROWS // 2):
                even, odd = toeplitz[base + 2 * j], toeplitz[base + 2 * j + 1]
                tab_ref[base, :, pl.ds(j * 128, 128)] = jnp.where(kc < GRID_W, even, pltpu.roll(odd, GRID_W, 1))

    padded = jnp.pad(rpb, ((0, 0), (0, 16 - n_rows), (0, 128 - n_cols)))
    return pl.pallas_call(
        body, name="na_bias_table", grid=(H,),
        in_specs=[pl.BlockSpec((None, 16, 128), lambda h: (h, 0, 0))],
        out_specs=pl.BlockSpec((None, NA_BASES, GRID_W, NA_KEYS), lambda h: (h, 0, 0, 0)),
        out_shape=jax.ShapeDtypeStruct((H, NA_BASES, GRID_W, NA_KEYS), F32),
        compiler_params=_cparams(("parallel",)),
    )(padded)


def _na_rpb_grad(dtab):
    H = dtab.shape[0]
    n_rows = 2 * NA_WIN_ROWS - 1
    n_cols = 2 * NA_WIN_COLS - 1

    def body(d_ref, o_ref):
        lane = lax.broadcasted_iota(jnp.int32, (GRID_W, 128), 1)
        low = lane < GRID_W
        out_rows = []
        for ro in range(n_rows):
            acc = jnp.zeros((GRID_W, 128), F32)
            for base in range(NA_BASES):
                i = ro - base
                if not 0 <= i < NA_WIN_ROWS:
                    continue
                pair = d_ref[base, :, pl.ds((i // 2) * 128, 128)]
                if i % 2:
                    pair = pltpu.roll(pair, GRID_W, 1)
                acc = acc + jnp.where(low, pair, 0.0)
            skew = pltpu.roll(acc, 0, 1, stride=1, stride_axis=0)
            diag = jnp.sum(skew, axis=0, keepdims=True)
            out_rows.append(pltpu.roll(jnp.broadcast_to(diag, (8, 128)), 128 - (GRID_W - NA_WIN_COLS), 1)[:1])
        out_rows.append(jnp.zeros((1, 128), F32))
        res = jnp.concatenate(out_rows, axis=0)
        o_ref[...] = jnp.where(lax.broadcasted_iota(jnp.int32, res.shape, 1) < n_cols, res, 0.0)

    return pl.pallas_call(
        body, name="na_rpb_grad", grid=(H,),
        in_specs=[pl.BlockSpec((None, NA_BASES, GRID_W, NA_KEYS), lambda h: (h, 0, 0, 0))],
        out_specs=pl.BlockSpec((None, n_rows + 1, 128), lambda h: (h, 0, 0)),
        out_shape=jax.ShapeDtypeStruct((H, n_rows + 1, 128), F32),
        compiler_params=_cparams(("parallel",)),
    )(jnp.flip(dtab, axis=2))


BAND_Q = 128
BAND_KEYS = BAND_Q + 2 * DIL_RADIUS


def _band_geometry(n, L):
    q0 = pl.multiple_of(n * BAND_Q, BAND_Q)
    k0 = pl.multiple_of(jnp.clip(q0 - DIL_RADIUS, 0, L - BAND_KEYS), DIL_RADIUS)
    qi = q0 + lax.broadcasted_iota(jnp.int32, (BAND_Q, BAND_KEYS), 0)
    kj = k0 + lax.broadcasted_iota(jnp.int32, (BAND_Q, BAND_KEYS), 1)
    return q0, k0, jnp.abs(qi - kj) <= DIL_RADIUS


DIL_PAIRS = DIL_OUT_WIDTH // 128


def _residue_shape(dil, T, dtype):
    return jax.ShapeDtypeStruct((DIL_PAIRS, dil, T // dil, 128), dtype)


def _residue_tile(dil, tm):
    return pl.BlockSpec((DIL_PAIRS, dil, tm // dil, 128), lambda i: (0, 0, i, 0))


def _to_natural(ref, scratch, dil, tm):
    tiles = []
    for pair in range(DIL_PAIRS):
        if dil == 1:
            tiles.append(ref[pair, 0].astype(F32))
            continue
        for r in range(dil):
            scratch[pl.ds(r, tm // dil, stride=dil), :] = ref[pair, r].astype(F32)
        tiles.append(scratch[...])
    return tiles


def _from_natural(tile, scratch, ref, pair, dil, tm):
    if dil == 1:
        ref[pair, 0] = tile.astype(ref.dtype)
        return
    scratch[...] = tile
    for r in range(dil):
        ref[pair, r] = scratch[pl.ds(r, tm // dil, stride=dil), :].astype(ref.dtype)


def _band_specs(group, T):
    dil = DIL_GROUPS[group][1]
    L = T // dil
    assert L % BAND_Q == 0 and L >= BAND_KEYS, (T, dil)
    return L, (dil * DIL_PAIRS,), pl.BlockSpec((None, None, L, 128), lambda s: (s % DIL_PAIRS, s // DIL_PAIRS, 0, 0))


BAND_BLOCKS_PER_STEP = 4


def _band_softmax(s, valid):
    s = jnp.where(valid, s, NEG_INF)
    m = jnp.max(s, axis=-1, keepdims=True)
    p = jnp.exp(s - m)
    l = jnp.sum(p, axis=-1, keepdims=True)
    return p / l, m + jnp.log(l)


def _band_fwd(q, k, v, group):
    T = q.shape[1] * q.shape[2]
    L, grid, spec = _band_specs(group, T)
    U = min(BAND_BLOCKS_PER_STEP, L // BAND_Q)

    def body(q_ref, k_ref, v_ref, o_ref, lse_ref):
        def step(it, carry):
            geo = [_band_geometry(it * U + u, L) for u in range(U)]
            ss = [lax.dot_general(jnp.concatenate(_split_pair(q_ref[pl.ds(q0, BAND_Q), :]), axis=0),
                                  k_ref[pl.ds(k0, BAND_KEYS), :], _NT, preferred_element_type=F32) for q0, k0, _ in geo]
            pls = [_band_softmax(s, jnp.concatenate([valid, valid], axis=0)) for s, (_, _, valid) in zip(ss, geo)]
            os = [jnp.dot(p.astype(BF), v_ref[pl.ds(k0, BAND_KEYS), :], preferred_element_type=F32) for (p, _), (_, k0, _) in zip(pls, geo)]
            for (q0, _, _), o2, (_, lse) in zip(geo, os, pls):
                o_ref[pl.ds(q0, BAND_Q), :] = _join_pair(o2[:BAND_Q], o2[BAND_Q:])
                lse2 = jnp.broadcast_to(lse, (2 * BAND_Q, 128))
                lse_ref[pl.ds(q0, BAND_Q), :] = _join_pair(lse2[:BAND_Q], lse2[BAND_Q:])
            return carry

        lax.fori_loop(0, L // (BAND_Q * U), step, 0)

    res = _residue_shape(DIL_GROUPS[group][1], T, F32)
    return pl.pallas_call(
        body, name=f"band_fwd_g{group}", grid=grid,
        in_specs=[spec] * 3, out_specs=[spec] * 2, out_shape=[res, res],
        compiler_params=_cparams(("parallel",)),
    )(q, k, v)


def _band_bwd(q, k, v, do, dlse, group):
    T = q.shape[1] * q.shape[2]
    L, grid, spec = _band_specs(group, T)
    U = min(BAND_BLOCKS_PER_STEP, L // BAND_Q)

    def body(q_ref, k_ref, v_ref, do_ref, dlse_ref, dq_ref, dk_ref, dv_ref):
        dk_ref[...] = jnp.zeros_like(dk_ref)
        dv_ref[...] = jnp.zeros_like(dv_ref)

        def step(it, carry):
            geo = [_band_geometry(it * U + u, L) for u in range(U)]
            q2s = [jnp.concatenate(_split_pair(q_ref[pl.ds(q0, BAND_Q), :]), axis=0) for q0, _, _ in geo]
            do2s = [jnp.concatenate(_split_pair(do_ref[pl.ds(q0, BAND_Q), :]), axis=0) for q0, _, _ in geo]
            ss = [lax.dot_general(q2, k_ref[pl.ds(k0, BAND_KEYS), :], _NT, preferred_element_type=F32) for q2, (_, k0, _) in zip(q2s, geo)]
            dps = [lax.dot_general(do2, v_ref[pl.ds(k0, BAND_KEYS), :], _NT, preferred_element_type=F32) for do2, (_, k0, _) in zip(do2s, geo)]
            ps = [_band_softmax(s, jnp.concatenate([valid, valid], axis=0))[0] for s, (_, _, valid) in zip(ss, geo)]
            dss = []
            for p, dp, (q0, _, _) in zip(ps, dps, geo):
                dl = dlse_ref[pl.ds(q0, BAND_Q), :]
                dl2 = jnp.concatenate([dl[:, :1], dl[:, HEAD_DIM:HEAD_DIM + 1]], axis=0)
                dss.append(p * (dp - jnp.sum(dp * p, axis=-1, keepdims=True) + dl2))
            dvs = [lax.dot_general(p.astype(BF), do2, _TN, preferred_element_type=F32) for p, do2 in zip(ps, do2s)]
            dsbs = [ds.astype(BF) for ds in dss]
            dqs = [jnp.dot(dsb, k_ref[pl.ds(k0, BAND_KEYS), :], preferred_element_type=F32) for dsb, (_, k0, _) in zip(dsbs, geo)]
            dks = [lax.dot_general(dsb, q2, _TN, preferred_element_type=F32) for dsb, q2 in zip(dsbs, q2s)]
            for u, (q0, k0, _) in enumerate(geo):
                dq_ref[pl.ds(q0, BAND_Q), :] = _join_pair(dqs[u][:BAND_Q], dqs[u][BAND_Q:])
                dk_ref[pl.ds(k0, BAND_KEYS), :] += dks[u]
                dv_ref[pl.ds(k0, BAND_KEYS), :] += dvs[u]
            return carry

        lax.fori_loop(0, L // (BAND_Q * U), step, 0)

    res = _residue_shape(DIL_GROUPS[group][1], T, F32)
    return pl.pallas_call(
        body, name=f"band_bwd_g{group}", grid=grid,
        in_specs=[spec] * 5, out_specs=[spec] * 3, out_shape=[res] * 3,
        compiler_params=_cparams(("parallel",)),
    )(q, k, v, do, dlse)


def _head_sums(t):
    head = lax.broadcasted_iota(jnp.int32, t.shape, 1) // HEAD_DIM
    out = jnp.zeros_like(t)
    for h in range(t.shape[1] // HEAD_DIM):
        mine = head == h
        out = jnp.where(mine, jnp.sum(jnp.where(mine, t, 0.0), axis=-1, keepdims=True), out)
    return out


def _dil_merge_fwd(os, lses, T, tm):
    G = len(DIL_GROUPS)
    W = DIL_OUT_WIDTH
    dils = [d for _, d in DIL_GROUPS]

    def body(*refs):
        o_refs, lse_refs = refs[:G], refs[G:2 * G]
        y_ref, w_refs, on_refs, scratch = refs[2 * G], refs[2 * G + 1:3 * G + 1], refs[3 * G + 1:4 * G + 1], refs[-1]
        o = [jnp.concatenate(_to_natural(r, scratch, d, tm), axis=1) for r, d in zip(o_refs, dils)]
        ls = [jnp.concatenate(_to_natural(r, scratch, d, tm), axis=1) for r, d in zip(lse_refs, dils)]
        m = functools.reduce(jnp.maximum, ls)
        es = [jnp.exp(l - m) for l in ls]
        tot = functools.reduce(jnp.add, es)
        ws = [e / tot for e in es]
        y_ref[...] = functools.reduce(jnp.add, [w * t for w, t in zip(ws, o)]).astype(y_ref.dtype)
        for g in range(G):
            w_refs[g][...] = ws[g]
            on_refs[g][...] = o[g]

    nat = pl.BlockSpec((tm, W), lambda i: (i, 0))
    res = pl.pallas_call(
        body, name="dil_merge_fwd", grid=(T // tm,),
        in_specs=[_residue_tile(d, tm) for d in dils] * 2,
        out_specs=[nat] * (2 * G + 1),
        out_shape=[jax.ShapeDtypeStruct((T, W), BF)] + [jax.ShapeDtypeStruct((T, W), F32)] * (2 * G),
        scratch_shapes=[pltpu.VMEM((tm, 128), F32)],
        compiler_params=_cparams(("parallel",)),
    )(*os, *lses)
    return res[0], res[1:G + 1], res[G + 1:]


def _dil_merge_bwd(dy, os, ws, tm, after=()):
    G = len(DIL_GROUPS)
    T, W = dy.shape
    dils = [d for _, d in DIL_GROUPS]
    n_after = len(after)

    def body(*refs):
        dyt = refs[0][...]
        o, w = [r[...] for r in refs[1:G + 1]], [r[...] for r in refs[G + 1:2 * G + 1]]
        refs = refs[2 * G + 1 + n_after:]
        do_refs, dlse_refs, scratch = refs[:G], refs[G:2 * G], refs[-1]
        dws = [_head_sums(dyt * t) for t in o]
        mean = functools.reduce(jnp.add, [a * b for a, b in zip(w, dws)])
        for g, d in enumerate(dils):
            do, dlse = w[g] * dyt, w[g] * (dws[g] - mean)
            for pair in range(DIL_PAIRS):
                cols = slice(pair * 128, (pair + 1) * 128)
                _from_natural(do[:, cols], scratch, do_refs[g], pair, d, tm)
                _from_natural(dlse[:, cols], scratch, dlse_refs[g], pair, d, tm)

    nat = pl.BlockSpec((tm, W), lambda i: (i, 0))
    res = pl.pallas_call(
        body, name="dil_merge_bwd", grid=(T // tm,),
        in_specs=[nat] * (2 * G + 1) + [ANY] * n_after,
        out_specs=[_residue_tile(d, tm) for d in dils] * 2,
        out_shape=[_residue_shape(d, T, BF) for d in dils] + [_residue_shape(d, T, F32) for d in dils],
        scratch_shapes=[pltpu.VMEM((tm, 128), F32)],
        compiler_params=_cparams(("parallel",)),
    )(dy, *os, *ws, *after)
    return res[:G], res[G:]


def _qkv_prep(z, cos2, sin_signed, tm):
    T = z.shape[0]
    G = len(DIL_GROUPS)
    dils = [d for _, d in DIL_GROUPS]
    n_dil_blocks = 3 * DIL_WIDTH // 128

    def body(*refs):
        blocks = refs[:n_dil_blocks]
        cos_ref, sin_ref = refs[n_dil_blocks], refs[1 + n_dil_blocks]
        outs = refs[2 + n_dil_blocks:]
        for part in range(3):
            for g, d in enumerate(dils):
                out = outs[g * 3 + part]
                for pair in range(DIL_PAIRS):
                    blk = blocks[part * (DIL_WIDTH // 128) + g * DIL_PAIRS + pair]
                    for r in range(d):
                        rows = pl.ds(r, tm // d, stride=d) if d > 1 else slice(None)
                        x = blk[rows, :]
                        if part < 2:
                            x = _rope(x, cos_ref[rows, :], sin_ref[rows, :])
                        if part == 0:
                            x = x * Q_SCALE
                        out[pair, r] = x.astype(out.dtype)

    lane_block = [pl.BlockSpec((tm, 128), functools.partial(lambda b, i: (i, b), b)) for b in range(n_dil_blocks)]
    tab = pl.BlockSpec((tm, 128), lambda i: (i, 0))
    res = pl.pallas_call(
        body, name="qkv_prep", grid=(T // tm,),
        in_specs=lane_block + [tab, tab],
        out_specs=[_residue_tile(d, tm) for d in dils for _ in range(3)],
        out_shape=[_residue_shape(d, T, BF) for d in dils for _ in range(3)],
        compiler_params=_cparams(("parallel",)),
    )(*[z] * n_dil_blocks, cos2, sin_signed)
    return [res[3 * g:3 + 3 * g] for g in range(G)]


def _qkv_unprep(d_na, d_dil, cos2, sin_signed, tm, after=()):
    T = d_na[0].shape[0]
    G = len(DIL_GROUPS)
    dils = [d for _, d in DIL_GROUPS]
    n_after = len(after)

    def body(*refs):
        dq, dk, dv = (r[...] for r in refs[:3])
        res_refs = refs[3:3 + 3 * G]
        cs, sn = refs[3 + 3 * G][...], refs[4 + 3 * G][...]
        out, scratch = refs[5 + 3 * G + n_after], refs[-1]
        cols = [dq * Q_SCALE, dk, dv]
        for part in range(3):
            for g, d in enumerate(dils):
                for x in _to_natural(res_refs[g * 3 + part], scratch, d, tm):
                    if part < 2:
                        x = _rope(x, cs, -sn)
                    cols.append(x * Q_SCALE if part == 0 else x)
        out[...] = jnp.concatenate(cols, axis=1).astype(out.dtype)

    wide = pl.BlockSpec((tm, NA_WIDTH), lambda i: (i, 0))
    tab = pl.BlockSpec((tm, 128), lambda i: (i, 0))
    return pl.pallas_call(
        body, name="qkv_unprep", grid=(T // tm,),
        in_specs=[wide] * 3 + [_residue_tile(d, tm) for d in dils for _ in range(3)] + [tab, tab] + [ANY] * n_after,
        out_specs=pl.BlockSpec((tm, QKV_WIDTH), lambda i: (i, 0)),
        out_shape=jax.ShapeDtypeStruct((T, QKV_WIDTH), BF),
        scratch_shapes=[pltpu.VMEM((tm, 128), F32)],
        compiler_params=_cparams(("parallel",)),
    )(*d_na, *[t for g in range(G) for t in d_dil[g]], cos2, sin_signed, *after)


def _rope_tables(positions):
    half = HEAD_DIM // 2
    inv_freq = ROPE_THETA ** (-jnp.arange(half, dtype=F32) / half)
    ang = positions.astype(F32)[:, None] * inv_freq
    cos, sin = jnp.cos(ang), jnp.sin(ang)
    return jnp.tile(jnp.concatenate([cos, cos], axis=1), (1, 2)), jnp.tile(jnp.concatenate([-sin, sin], axis=1), (1, 2))


def _pack_rows(t):
    return t.reshape(-1, PACK_W)


def _me():
    return lax.axis_index("x"), lax.axis_index("y"), lax.axis_index("c")


def _other_chips(x, y):
    return [(1 - x, y), (x, 1 - y), (1 - x, 1 - y)]


def _pair_sum(g, got, tm):
    S, R, W = g.shape
    half = R // 2
    nb = half // tm

    def body(pos_ref, g_ref, got_ref, own_ref, ob_ref):
        tot = g_ref[...] + got_ref[...]
        ob_ref[...] = tot.astype(ob_ref.dtype)

        @pl.when(pl.program_id(1) == pos_ref[1])
        def _():
            own_ref[...] = tot

    tile = pl.BlockSpec((None, tm, W), lambda i, s, pos_ref: (s, i, 0))
    c, chip = lax.axis_index("c"), 2 * lax.axis_index("x") + lax.axis_index("y")
    return pl.pallas_call(
        body, name="pair_sum",
        grid_spec=pltpu.PrefetchScalarGridSpec(
            num_scalar_prefetch=1, grid=(nb, S),
            in_specs=[pl.BlockSpec((None, tm, W), lambda i, s, pos_ref: (s, pos_ref[0] * nb + i, 0)), tile],
            out_specs=[pl.BlockSpec((tm, W), lambda i, s, pos_ref: (i, 0)), tile]),
        out_shape=[jax.ShapeDtypeStruct((half, W), F32), jax.ShapeDtypeStruct((S, half, W), BF)],
        compiler_params=_cparams(("parallel", "arbitrary")),
    )(jnp.stack([c, chip]).astype(jnp.int32), g, got)


def _chip_sum(own, others, tm):
    n, h, W = others.shape
    nb = h // tm

    def body(c_ref, own_ref, p_ref, o_ref):
        o_ref[...] = ((own_ref[...] + p_ref[0].astype(F32)) + p_ref[1].astype(F32)) + p_ref[2].astype(F32)

    return pl.pallas_call(
        body, name="chip_sum",
        grid_spec=pltpu.PrefetchScalarGridSpec(
            num_scalar_prefetch=1, grid=(nb,),
            in_specs=[pl.BlockSpec((tm, W), lambda i, c_ref: (i, 0)), pl.BlockSpec((n, tm, W), lambda i, c_ref: (0, i, 0))],
            out_specs=pl.BlockSpec((tm, W), lambda i, c_ref: (c_ref[0] * nb + i, 0))),
        out_shape=jax.ShapeDtypeStruct((2 * h, W), F32),
        compiler_params=_cparams(("parallel",)),
    )(lax.axis_index("c").reshape(1).astype(jnp.int32), own, others)


def _join_halves(shard):
    h = shard.shape[0] // 2

    def body(in_ref, out_ref, send_sem, recv_sem):
        x, y, c = _me()
        cp = pltpu.make_async_remote_copy(
            src_ref=in_ref.at[pl.ds(c * h, h), :], dst_ref=out_ref.at[pl.ds(c * h, h), :],
            send_sem=send_sem, recv_sem=recv_sem, device_id=(x, y, 1 - c), device_id_type=MESH)
        cp.start()
        pltpu.make_async_remote_copy(
            src_ref=in_ref.at[pl.ds(c * h, h), :], dst_ref=out_ref.at[pl.ds((1 - c) * h, h), :],
            send_sem=send_sem, recv_sem=recv_sem, device_id=(x, y, 1 - c), device_id_type=MESH).wait_recv()
        cp.wait_send()

    return pl.pallas_call(
        body, name="join_halves", in_specs=[ANY], out_specs=ANY,
        out_shape=jax.ShapeDtypeStruct(shard.shape, shard.dtype), input_output_aliases={0: 0},
        scratch_shapes=[pltpu.SemaphoreType.DMA, pltpu.SemaphoreType.DMA],
    )(shard)


def _allreduce_small(s, after=()):
    R, W = s.shape
    n_after = len(after)

    def body(s_ref, *rest):
        o_ref, buf, send_sems, recv_sems = rest[n_after:]
        x, y, c = _me()
        me = 4 * x + 2 * y + c
        buf[me] = s_ref[...]
        peers = [((x + fx) % 2, (y + fy) % 2, (c + fc) % 2) for fx in range(2) for fy in range(2) for fc in range(2)][1:]
        sends = [pltpu.make_async_remote_copy(
            src_ref=s_ref, dst_ref=buf.at[me], send_sem=send_sems.at[k], recv_sem=recv_sems.at[k],
            device_id=peer, device_id_type=MESH) for k, peer in enumerate(peers)]
        for cp in sends:
            cp.start()
        for k, peer in enumerate(peers):
            pltpu.make_async_remote_copy(
                src_ref=s_ref, dst_ref=buf.at[4 * peer[0] + 2 * peer[1] + peer[2]], send_sem=send_sems.at[k],
                recv_sem=recv_sems.at[k], device_id=peer, device_id_type=MESH).wait_recv()
        for cp in sends:
            cp.wait_send()
        total = buf[0]
        for d in range(1, N_DEV):
            total = total + buf[d]
        o_ref[...] = total

    return pl.pallas_call(
        body, name="allreduce_small",
        in_specs=[pl.BlockSpec(memory_space=pltpu.VMEM)] + [ANY] * n_after, out_specs=pl.BlockSpec(memory_space=pltpu.VMEM),
        out_shape=jax.ShapeDtypeStruct((R, W), F32),
        scratch_shapes=[pltpu.VMEM((N_DEV, R, W), F32), pltpu.SemaphoreType.DMA((N_DEV - 1,)), pltpu.SemaphoreType.DMA((N_DEV - 1,))],
    )(s, *after)


HBM_SPEC = pl.BlockSpec(memory_space=pltpu.HBM)
SEM_SPEC = pl.BlockSpec(memory_space=pltpu.SEMAPHORE)
DATAFLOW = pltpu.SideEffectType.DATAFLOW_SIDE_EFFECTING


class _InFlight(NamedTuple):
    sems: tuple
    src: jax.Array
    land: jax.Array
    token: jax.Array


def _split_start(name, src, land_shape, land_dtype, n, copies, after=()):
    n_after = len(after)

    def body(src_ref, land_ref, *rest):
        rest = rest[n_after:]
        sems, token = rest[:2 * n], rest[-1]
        for k, (s, d, peer) in enumerate(copies(src_ref, land_ref)):
            pltpu.make_async_remote_copy(src_ref=s, dst_ref=d, send_sem=sems[k], recv_sem=sems[n + k],
                                         device_id=peer, device_id_type=MESH).start()
        token[...] = jnp.zeros_like(token)

    outs = pl.pallas_call(
        body, name=name,
        out_shape=(*[pltpu.SemaphoreType.DMA(())] * (2 * n), pltpu.HBM(src.shape, src.dtype), pltpu.HBM(land_shape, land_dtype),
                   jax.ShapeDtypeStruct((8, 128), F32)),
        in_specs=(HBM_SPEC, HBM_SPEC, *[ANY] * n_after),
        out_specs=(*[SEM_SPEC] * (2 * n), HBM_SPEC, HBM_SPEC, pl.BlockSpec(memory_space=pltpu.VMEM)),
        input_output_aliases={0: 2 * n, 1: 2 * n + 1},
        compiler_params=pltpu.CompilerParams(has_side_effects=DATAFLOW),
    )(pltpu.with_memory_space_constraint(src, pltpu.HBM), pltpu.with_memory_space_constraint(lax.empty(land_shape, land_dtype), pltpu.HBM),
      *after)
    return _InFlight(tuple(outs[:2 * n]), outs[2 * n], outs[2 * n + 1], outs[2 * n + 2])


def _split_wait(name, flight, after, n, copies):
    def body(src_ref, land_ref, *rest):
        sems = rest[:2 * n]
        for k, (s, d, peer) in enumerate(copies(src_ref, land_ref)):
            cp = pltpu.make_async_remote_copy(src_ref=s, dst_ref=d, send_sem=sems[k], recv_sem=sems[n + k],
                                              device_id=peer, device_id_type=MESH)
            cp.wait_send()
            cp.wait_recv()

    return pl.pallas_call(
        body, name=name,
        out_shape=(pltpu.HBM(flight.src.shape, flight.src.dtype), pltpu.HBM(flight.land.shape, flight.land.dtype)),
        in_specs=(HBM_SPEC, HBM_SPEC, *[SEM_SPEC] * (2 * n), ANY),
        out_specs=(HBM_SPEC, HBM_SPEC), input_output_aliases={0: 0, 1: 1},
        compiler_params=pltpu.CompilerParams(has_side_effects=DATAFLOW),
    )(flight.src, flight.land, *flight.sems, after)


def _gather_copies(src_ref, land_ref):
    x, y, c = _me()
    return [(src_ref, land_ref.at[2 * x + y], (*chip, c)) for chip in _other_chips(x, y)]


def _gather_start(packed, tag, after=()):
    return _split_start(f"gather_start_{tag}", packed, (N_CHIPS, *packed.shape), packed.dtype, 3, _gather_copies, after)


def _gather_wait(flight, after, tag):
    src, others = _split_wait(f"gather_wait_{tag}", flight, after, 3, _gather_copies)
    return lax.dynamic_update_slice(others, src[None], (2 * lax.axis_index("x") + lax.axis_index("y"), 0, 0))


def _across_copies(src_ref, land_ref):
    x, y, c = _me()
    half = src_ref.shape[0] // 2
    rows = pl.ds(c * half, half)
    return [(src_ref.at[rows, :], land_ref.at[2 * x + y, rows, :], (*chip, c)) for chip in _other_chips(x, y)]


def _to_sibling_copies(all_ref, unused_ref):
    x, y, c = _me()
    half = all_ref.shape[1] // 2
    places = [all_ref.at[2 * chip[0] + chip[1], pl.ds(c * half, half), :] for chip in _other_chips(x, y)]
    return [(place, place, (x, y, 1 - c)) for place in places]


def _gather_halves_start(shard):
    return _split_start("gather_in_across_start", shard, (N_CHIPS, *shard.shape), shard.dtype, 3, _across_copies)


def _gather_halves_finish(flight, after):
    shard, landed = _split_wait("gather_in_across_wait", flight, after, 3, _across_copies)
    forward = _split_start("gather_in_sibling_start", landed, (8, 128), landed.dtype, 3, _to_sibling_copies)
    others = _split_wait("gather_in_sibling_wait", forward, forward.token, 3, _to_sibling_copies)[0]
    return lax.dynamic_update_slice(others, shard[None], (2 * lax.axis_index("x") + lax.axis_index("y"), 0, 0))


def _swap_copies(src_ref, land_ref):
    x, y, c = _me()
    half = land_ref.shape[1]
    return [(src_ref.at[:, pl.ds((1 - c) * half, half), :], land_ref, (x, y, 1 - c))]


def _swap_start(g, tag):
    S, R, W = g.shape
    return _split_start(f"swap_halves_start_{tag}", g, (S, R // 2, W), g.dtype, 1, _swap_copies)


def _swap_wait(flight, after, tag):
    return _split_wait(f"swap_halves_wait_{tag}", flight, after, 1, _swap_copies)


def _scatter_copies(src_ref, land_ref):
    x, y, c = _me()
    return [(src_ref.at[2 * chip[0] + chip[1]], land_ref.at[j], (*chip, c)) for j, chip in enumerate(_other_chips(x, y))]


def _scatter_start(part, tag):
    S, h, W = part.shape
    return _split_start(f"scatter_chips_start_{tag}", part, (S - 1, h, W), part.dtype, 3, _scatter_copies)


def _scatter_wait(flight, after, tag):
    return _split_wait(f"scatter_chips_wait_{tag}", flight, after, 3, _scatter_copies)[1]


def _join_copies(shard_ref, unused_ref):
    x, y, c = _me()
    h = shard_ref.shape[0] // 2
    rows = shard_ref.at[pl.ds(c * h, h), :]
    return [(rows, rows, (x, y, 1 - c))]


def _join_start(shard):
    return _split_start("join_halves_start", shard, (8, 128), shard.dtype, 1, _join_copies)


def _join_wait(flight, after):
    return _split_wait("join_halves_wait", flight, after, 1, _join_copies)[0]


def _adamw(name, g, g_row0, w, m, v):
    _, R, C = w.shape
    tm = next(cand for cand in (368, 256, 128, 64, 32, 16, 8) if R % cand == 0)
    assert g_row0 % tm == 0 and g.shape[1] == C

    def body(g_ref, w_ref, m_ref, v_ref, go_ref, d_ref, mo_ref, vo_ref):
        gt = g_ref[...]
        mt = ADAM_B1 * m_ref[...] + (1.0 - ADAM_B1) * gt
        vt = ADAM_B2 * v_ref[...] + (1.0 - ADAM_B2) * jnp.square(gt)
        m_hat = mt / (1.0 - ADAM_B1 ** ADAM_STEP)
        v_hat = vt / (1.0 - ADAM_B2 ** ADAM_STEP)
        go_ref[...] = gt
        d_ref[...] = -ADAM_LR * (m_hat / (jnp.sqrt(v_hat) + ADAM_EPS) + ADAM_WD * w_ref[...])
        mo_ref[...] = mt
        vo_ref[...] = vt

    state = pl.BlockSpec((None, tm, C), lambda i: (0, i, 0))
    return pl.pallas_call(
        body, name=name, grid=(R // tm,),
        in_specs=[pl.BlockSpec((tm, C), lambda i: (g_row0 // tm + i, 0)), state, state, state],
        out_specs=[state] * 4, out_shape=[jax.ShapeDtypeStruct((1, R, C), F32)] * 4,
        compiler_params=_cparams(("parallel",)),
    )(g, w, m, v)


def _unpack_weights(gathered, names):
    S = gathered.shape[0]
    shard_shapes = {"w_in": (D_MODEL, (QKV_WIDTH + 2 * D_MODEL) // S), "w_branch_na": (NA_WIDTH, D_MODEL // S),
                    "w_branch_dil": (DIL_OUT_WIDTH, D_MODEL // S), "w_out": (D_MODEL // S, D_MODEL),
                    "w_up": (D_MODEL, D_FF // S), "w_down": (D_FF // S, D_MODEL),
                    "w_ple_gate": (D_MODEL // S, D_MODEL), "w_ple_proj": (PLE_DIM, D_MODEL // S)}
    col_sharded = {"w_in", "w_branch_na", "w_branch_dil", "w_up", "w_ple_proj"}
    out, r0 = {}, 0
    for name in names:
        rows, cols = shard_shapes[name]
        n = rows * cols // PACK_W
        t = gathered[:, r0:r0 + n, :].reshape(S, rows, cols)
        r0 += n
        out[name] = t.transpose(1, 0, 2).reshape(rows, S * cols) if name in col_sharded else t.reshape(S * rows, cols)
    return out


def kernel(x, p, positions, g_mix, w_in, rpb, w_branch_na, w_branch_dil, w_out, g_mlp, w_up, w_down, g_ple, w_ple_gate, w_ple_proj, g_final, loss_target, m_g_mix, m_w_in, m_rpb, m_w_branch_na, m_w_branch_dil, m_w_out, m_g_mlp, m_w_up, m_w_down, m_g_ple, m_w_ple_gate, m_w_ple_proj, m_g_final, v_g_mix, v_w_in, v_rpb, v_w_branch_na, v_w_branch_dil, v_w_out, v_g_mlp, v_w_up, v_w_down, v_g_ple, v_w_ple_gate, v_w_ple_proj, v_g_final):
    shards = {"w_in": w_in[0], "w_branch_na": w_branch_na[0], "w_branch_dil": w_branch_dil[0], "w_out": w_out[0],
              "w_up": w_up[0], "w_down": w_down[0], "w_ple_gate": w_ple_gate[0], "w_ple_proj": w_ple_proj[0]}
    params = {"w_in": w_in, "w_branch_na": w_branch_na, "w_branch_dil": w_branch_dil, "w_out": w_out, "w_up": w_up,
              "w_down": w_down, "w_ple_gate": w_ple_gate, "w_ple_proj": w_ple_proj,
              "m_w_in": m_w_in, "m_w_branch_na": m_w_branch_na, "m_w_branch_dil": m_w_branch_dil, "m_w_out": m_w_out,
              "m_w_up": m_w_up, "m_w_down": m_w_down, "m_w_ple_gate": m_w_ple_gate, "m_w_ple_proj": m_w_ple_proj,
              "v_w_in": v_w_in, "v_w_branch_na": v_w_branch_na, "v_w_branch_dil": v_w_branch_dil, "v_w_out": v_w_out,
              "v_w_up": v_w_up, "v_w_down": v_w_down, "v_w_ple_gate": v_w_ple_gate, "v_w_ple_proj": v_w_ple_proj}

    xs, ps, tgt = x[0], p[0, 0], loss_target[0]
    T = xs.shape[0]
    TM = 512
    gm, gl, gp, gf = g_mix, g_mlp, g_ple, g_final.reshape(1, D_MODEL)

    across = _gather_halves_start(shards["w_in"].astype(BF))
    a = _rowwise("norm_mix", lambda h, g: h * _rms(h) * g, T, TM, [_row(xs, TM), _full(gm)], [(D_MODEL, BF)],
                 after=(across.token,))
    cos2, sin_signed = _rope_tables(positions[0])
    tab = _na_bias_table(rpb[0])
    w_in_all = _gather_halves_finish(across, a)
    W = {"w_in": w_in_all.transpose(1, 0, 2).reshape(D_MODEL, -1)}
    mix_flight = _gather_start(jnp.concatenate([_pack_rows(shards[n].astype(BF)) for n in GATHER_MIX], axis=0), "mix",
                               after=(w_in_all,))
    rest_flight = _gather_start(jnp.concatenate([_pack_rows(shards[n].astype(BF)) for n in GATHER_MLP], axis=0), "mlp",
                                after=(mix_flight.token,))
    w_gates = W["w_in"][:, QKV_WIDTH:]

    n3 = 3 * NA_WIDTH
    qkv = _mm("in_na", a, W["w_in"], "nn", 1024, 768, 1024, [BF], after=(rest_flight.token,),
              b_view=(n3, (D_MODEL, 768), lambda j, k: (k, j)))
    z_dil = _mm("in_dil", a, W["w_in"], "nn", 1024, 768, 1024, [F32],
                b_view=(3 * DIL_WIDTH, (D_MODEL, 768), lambda j, k: (k, n3 // 768 + j)))
    z_gates = _mm("in_gates", a, w_gates, "nn", 1024,1024, 1024, [BF])

    dil_ops = _qkv_prep(z_dil, cos2, sin_signed, TM)
    y_na = _na_fwd(qkv, tab)
    band = [_band_fwd(*dil_ops[g], g) for g in range(len(DIL_GROUPS))]
    y_dil, w_grp, o_nat = _dil_merge_fwd([b[0] for b in band], [b[1] for b in band], T, TM)

    W.update(_unpack_weights(_gather_wait(mix_flight, y_dil, "mix"), GATHER_MIX))
    u_na = _mm("branch_na", y_na, W["w_branch_na"], "nn", 1024,1024, 512, [BF])
    u_dil = _mm("branch_dil", y_dil, W["w_branch_dil"], "nn", 1024,1024, 256, [BF])
    mixed = _rowwise(
        "gate_mix", lambda gn, gd, un, ud: _sigmoid(gn.astype(F32)) * un.astype(F32) + _sigmoid(gd.astype(F32)) * ud.astype(F32), T, TM,
        [_row(z_gates, TM, 0, D_MODEL), _row(z_gates, TM, 1, D_MODEL), _row(u_na, TM), _row(u_dil, TM)], [(D_MODEL, BF)])
    def add_norm(d, h, g):
        h = h + d
        return h, h * _rms(h) * g

    h1, cn = _mm("out_proj", mixed, W["w_out"], "nn", 512, 1024, 1024, [F32, BF], epilogue=add_norm, extras=(xs,), consts=(gl,))
    mlp_all = _gather_wait(rest_flight, cn, "mlp")
    W.update({n: t for n, t in _unpack_weights(mlp_all, GATHER_MLP).items() if n.startswith("w_ple")})
    chip_block = (None, D_MODEL, PACK_W)
    up, act = _mm("mlp_up", cn, mlp_all, "nn", 1024,1024, 1024, [BF, BF],
                  epilogue=lambda acc: (acc, jnp.square(jnp.maximum(acc, 0.0))), b_view=(D_FF, chip_block, lambda j, k: (j, 0, 0)))
    h2, en = _mm("mlp_down", act, mlp_all, "nn", 1024, 1024, 1024, [F32, BF], epilogue=add_norm, extras=(h1,), consts=(gp,),
                 b_view=(D_MODEL, chip_block, lambda j, k: (k, 1, 0)))
    pp = _mm("ple_proj", ps, W["w_ple_proj"], "nn", 1024,1024, 256, [F32])

    def head(gtt, h2t, ppt, tg, g):
        sg = _sigmoid(gtt)
        h3 = h2t + sg * ppt
        yo = h3 * _rms(h3) * g
        diff = yo - tg
        loss = 0.5 * jnp.sum(jnp.mean(jnp.square(diff), axis=-1, keepdims=True), axis=0, keepdims=True)
        dh3, dg = _rms_bwd(diff * (1.0 / D_MODEL), h3, g)
        return dh3, dh3 * ppt * sg * (1.0 - sg), dh3 * sg, jnp.broadcast_to(loss, (1, 128)), dg

    dh3, d_gt, d_pp, loss_part, dg_final = _mm(
        "ple_gate_loss_head", en, W["w_ple_gate"], "nn", 512, 1024, 1024, [F32, BF, BF], epilogue=head,
        extras=(h2, pp, tgt), consts=(gf,), sums=[128, D_MODEL])

    early_shapes = {n: shards[n].shape for n in REDUCE_EARLY}
    early_rows = sum(r * c for r, c in early_shapes.values()) // PACK_W
    shard_rows = D_MODEL // N_CHIPS
    early_buf = _mm("g_ple_gate", en, d_gt, "tn", 1024, 1024, 1024, [F32],
                    into=(jax.ShapeDtypeStruct((N_CHIPS, early_rows, PACK_W), F32), (N_CHIPS, shard_rows, PACK_W),
                          lambda i, j: (0, 2 * D_MODEL // shard_rows, 0)))
    g_ple_proj = _mm("g_ple_proj", ps, d_pp, "tn", 256, 1024, 1024,[F32])

    def add_norm_bwd(dn, dh_out, h, g):
        dh, dg = _rms_bwd(dn, h, g)
        dh = dh_out + dh
        return dh, dh, dg

    dh2, dh2_b, dg_ple = _mm("d_ple_gate", d_gt, W["w_ple_gate"], "nt", 512, 1024, 1024, [F32, BF],
                             epilogue=add_norm_bwd, extras=(dh3, h2), consts=(gp,), sums=[D_MODEL])
    d_up = _mm("d_mlp_down", dh2_b, mlp_all, "nt", 1024,1024, 1024, [BF], b_view=(D_FF, chip_block, lambda j, k: (j, 1, 0)),
               epilogue=lambda acc, u: (acc * (2.0 * jnp.maximum(u.astype(F32), 0.0)),), extras=(up,))
    early_buf = _mm("g_mlp_down", act, dh2_b, "tn", 1024, 1024, 1024,[F32],
                    into=(early_buf, (None, D_MODEL, PACK_W), lambda i, j: (i, 1, 0)))
    early_buf = _mm("g_mlp_up", cn, d_up, "tn", 1024, 1024, 1024,[F32],
                    into=(early_buf, (None, D_MODEL, PACK_W), lambda i, j: (j, 0, 0)))
    dh1, dh1_b, dg_mlp = _mm("d_mlp_up", d_up, mlp_all, "nt", 1024, 1024, 1024, [F32, BF], epilogue=add_norm_bwd,
                             b_view=(D_MODEL, chip_block, lambda j, k: (k, 0, 0)),
                             extras=(dh2, h1), consts=(gl,), sums=[D_MODEL])
    d_mixed = _mm("d_out_proj", dh1_b, W["w_out"], "nt", 1024,1024, 1024, [F32])
    early_buf = _mm("g_out_proj", mixed, dh1_b, "tn", 1024, 1024, 1024, [F32],
                    into=(early_buf, (N_CHIPS, shard_rows, PACK_W), lambda i, j: (0, 2 * D_MODEL // shard_rows + 1, 0)))

    def gate_bwd(dm, gn, gd, un, ud):
        gn, gd, un, ud = (t.astype(F32) for t in (gn, gd, un, ud))
        sn, sd = _sigmoid(gn), _sigmoid(gd)
        return jnp.concatenate([dm * un * sn * (1.0 - sn), dm * ud * sd * (1.0 - sd)], axis=1), dm * sn, dm * sd

    dz_gates, d_u_na, d_u_dil = _rowwise(
        "gate_mix_bwd", gate_bwd, T, TM,
        [_row(d_mixed, TM), _row(z_gates, TM, 0, D_MODEL), _row(z_gates, TM, 1, D_MODEL), _row(u_na, TM), _row(u_dil, TM)],
        [(2 * D_MODEL, BF), (D_MODEL, BF), (D_MODEL, BF)])
    g_branch_na = _mm("g_branch_na", y_na, d_u_na, "tn", 1024, 1024, 1024,[F32])
    g_branch_dil = _mm("g_branch_dil", y_dil, d_u_dil, "tn", 256, 1024, 1024,[F32])
    small_rows = [jnp.concatenate([_pack_rows(g[:, s * shard_rows:(s + 1) * shard_rows]) for g in (g_ple_proj, g_branch_na, g_branch_dil)],
                                  axis=0) for s in range(N_CHIPS)]
    early_buf = lax.dynamic_update_slice(early_buf, jnp.stack(small_rows), (0, 2 * D_MODEL + 2 * shard_rows, 0))
    early_tm = early_rows // 4
    swap_flight = _swap_start(early_buf, "early")
    d_y_na = _mm("d_branch_na", d_u_na, W["w_branch_na"], "nt", 1024,512, 1024, [BF], after=(swap_flight.token,))
    d_y_dil = _mm("d_branch_dil", d_u_dil, W["w_branch_dil"], "nt", 1024,256, 1024, [F32])

    dqa, dka, dva, dtab = _na_bwd(qkv, tab, d_y_na)
    early_g, early_got = _swap_wait(swap_flight, dqa, "early")
    early_pair, early_pair_b = _pair_sum(early_g, early_got, early_tm)
    scatter_flight = _scatter_start(early_pair_b, "early")
    d_rpb = _na_rpb_grad(dtab)[:, :2 * NA_WIN_ROWS - 1, :2 * NA_WIN_COLS - 1]

    do_res, dlse_res = _dil_merge_bwd(d_y_dil, o_nat, w_grp, TM, after=(scatter_flight.token,))
    d_dil = [_band_bwd(*dil_ops[g], do_res[g], dlse_res[g], g) for g in range(len(DIL_GROUPS))]

    dz_qkv = _qkv_unprep((dqa, dka, dva), d_dil, cos2, sin_signed, TM)
    g_in_parts = [_mm("g_in_qkv", a, dz_qkv, "tn", 1024, 1280, 1024,[F32]), _mm("g_in_gates", a, dz_gates, "tn", 1024, 1024, 1024,[F32])]
    early_mine = _chip_sum(early_pair, _scatter_wait(scatter_flight, g_in_parts[1], "early"), early_tm)
    join_flight = _join_start(early_mine)
    in_cols = shards["w_in"].shape[1]

    def owner_columns(s):
        lo, hi, split = s * in_cols, (s + 1) * in_cols, g_in_parts[0].shape[1]
        pieces = [g_in_parts[0][:, lo:min(hi, split)]] if lo < split else []
        pieces += [g_in_parts[1][:, max(lo, split) - split:hi - split]] if hi > split else []
        return pieces[0] if len(pieces) == 1 else jnp.concatenate(pieces, axis=1)

    late_tm = in_cols // 4
    late_swap = _swap_start(jnp.stack([owner_columns(s).T for s in range(N_CHIPS)]), "late")
    d_a = _mm("d_in_qkv", dz_qkv, W["w_in"], "nt", 1024,1024, 1280, [F32], after=(late_swap.token, join_flight.token),
              b_view=(D_MODEL, (D_MODEL, 1280), lambda j, k: (j, k)))
    late_g, late_got = _swap_wait(late_swap, d_a, "late")
    late_pair, late_pair_b = _pair_sum(late_g, late_got, late_tm)
    late_scatter = _scatter_start(late_pair_b, "late")
    def first_bwd(dn_gates, dn_qkv, dh_out, h, g):
        dh, dg = _rms_bwd(dn_gates + dn_qkv, h, g)
        return dh_out + dh, dg

    grad_x, dg_mix = _mm("d_in_gates", dz_gates, w_gates, "nt", 512, 1024, 1024, [F32], epilogue=first_bwd,
                         extras=(d_a, dh1, xs), consts=(gm,), sums=[D_MODEL], after=(late_scatter.token,))
    early_shard = _join_wait(join_flight, grad_x)

    n_rpb = rpb.size
    rpb_rows = 4
    small = jnp.concatenate([
        dg_mix, dg_mlp, dg_ple, dg_final,
        jnp.pad(d_rpb.reshape(-1), (0, rpb_rows * D_MODEL - n_rpb)).reshape(rpb_rows, D_MODEL),
        jnp.pad(loss_part, ((0, 0), (0, D_MODEL - loss_part.shape[1]))),
        jnp.zeros((SMALL_ROWS - 5 - rpb_rows, D_MODEL), F32)], axis=0)
    out = {"grad": {}, "delta": {}, "new_m": {}, "new_v": {}}

    def update(n, g, row0):
        res = _adamw("adamw_" + n, g, row0, params[n], params["m_" + n], params["v_" + n])
        for kind, t in zip(("grad", "delta", "new_m", "new_v"), res, strict=True):
            out[kind][n] = t

    row0 = 0
    for n in REDUCE_EARLY:
        rows, cols = early_shapes[n]
        n_rows = rows * cols // PACK_W
        if cols == PACK_W:
            update(n, early_shard, row0)
        else:
            update(n, early_shard[row0:row0 + n_rows].reshape(rows, cols), 0)
        row0 += n_rows
    late_others = _scatter_wait(late_scatter, out["new_v"][REDUCE_EARLY[-1]], "late")
    late_mine = _chip_sum(late_pair, late_others, late_tm)
    small = _allreduce_small(small, after=(late_mine,))
    res = _adamw("adamw_w_in", _join_halves(late_mine), 0, *[jnp.swapaxes(params[n], 1, 2) for n in ("w_in", "m_w_in", "v_w_in")])
    for kind, t in zip(("grad", "delta", "new_m", "new_v"), res, strict=True):
        out[kind]["w_in"] = jnp.swapaxes(t, 1, 2)
    loss = small[4 + rpb_rows, 0]

    def small_pack(a0, a1, a2, a3, r):
        return jnp.concatenate([a0.reshape(1, -1), a1.reshape(1, -1), a2.reshape(1, -1), a3.reshape(1, -1),
                                jnp.pad(r.reshape(-1), (0, rpb_rows * D_MODEL - n_rpb)).reshape(rpb_rows, D_MODEL)], axis=0)

    small_res = _adamw("adamw_small", small, 0, small_pack(g_mix, g_mlp, g_ple, g_final, rpb)[None],
                       small_pack(m_g_mix, m_g_mlp, m_g_ple, m_g_final, m_rpb)[None],
                       small_pack(v_g_mix, v_g_mlp, v_g_ple, v_g_final, v_rpb)[None])

    def small_unpack(t):
        return {"g_mix": t[0].reshape(g_mix.shape), "g_mlp": t[1].reshape(g_mlp.shape), "g_ple": t[2].reshape(g_ple.shape),
                "g_final": t[3].reshape(g_final.shape), "rpb": t[4:].reshape(-1)[:n_rpb].reshape(rpb.shape)}

    for kind, t in zip(("grad", "delta", "new_m", "new_v"), small_res, strict=True):
        out[kind].update(small_unpack(t[0]))

    order = ["g_mix", "w_in", "rpb", "w_branch_na", "w_branch_dil", "w_out", "g_mlp", "w_up", "w_down", "g_ple",
             "w_ple_gate", "w_ple_proj", "g_final"]
    return (loss, grad_x[None], *[out["grad"][n] for n in order], *[out["delta"][n] for n in order],
            *[out["new_m"][n] for n in order], *[out["new_v"][n] for n in order])
```

```python
import functools
from typing import NamedTuple

import jax
import jax.numpy as jnp
from jax import lax
from jax.experimental import pallas as pl
from jax.experimental.pallas import tpu as pltpu

BF = jnp.bfloat16
F32 = jnp.float32
MESH = pl.DeviceIdType.MESH
ANY = pl.BlockSpec(memory_space=pl.ANY)

V7X_VMEM_BYTES = 64 * 1024 * 1024
VMEM_LIMIT = V7X_VMEM_BYTES - 16 * 1024 * 1024

D_MODEL = 1024
HEAD_DIM = 64
GRID_W = 64
NA_HEADS = 8
NA_WIN_ROWS = 8
NA_WIN_COLS = 16
NA_WIDTH = NA_HEADS * HEAD_DIM
DIL_GROUPS = ((128, 1), (512, 4), (2048, 16))
DIL_HPG = 4
DIL_HEADS = DIL_HPG * len(DIL_GROUPS)
DIL_WIDTH = DIL_HEADS * HEAD_DIM
DIL_OUT_WIDTH = DIL_HPG * HEAD_DIM
DIL_RADIUS = 64
QKV_WIDTH = 3 * NA_WIDTH + 3 * DIL_WIDTH
D_FF = 4 * D_MODEL
PLE_DIM = 256
ROPE_THETA = 10000.0
RMS_EPS = 1e-6
NEG_INF = -1e30
Q_SCALE = HEAD_DIM ** -0.5

ADAM_LR = 0.001
ADAM_B1 = 0.9
ADAM_B2 = 0.999
ADAM_EPS = 1e-08
ADAM_WD = 0.01
ADAM_STEP = 10

N_CHIPS = 4
N_DEV = 8
PACK_W = 1024
BIG = ("w_in", "w_branch_na", "w_branch_dil", "w_out", "w_up", "w_down", "w_ple_gate", "w_ple_proj")
GATHER_MIX = ("w_branch_na", "w_branch_dil", "w_out")
GATHER_MLP = ("w_up", "w_down", "w_ple_gate", "w_ple_proj")
REDUCE_EARLY = ("w_up", "w_down", "w_ple_gate", "w_out", "w_ple_proj", "w_branch_na", "w_branch_dil")
SMALL_ROWS = 16


def _cparams(sem=None):
    return pltpu.CompilerParams(dimension_semantics=sem, vmem_limit_bytes=VMEM_LIMIT)


def _mm(name, a, b, mode, tm, tn, tk, out_dtypes, epilogue=None, extras=(), consts=(), sums=(), after=(), into=None,
        b_view=None):
    if mode == "nn":
        (M, K), N = a.shape, b.shape[1]
    elif mode == "nt":
        (M, K), N = a.shape, b.shape[0]
    else:
        (K, M), N = a.shape, b.shape[1]
    if b_view is not None:
        N = b_view[0]
    tm, tn, tk = min(tm, M), min(tn, N), min(tk, K)
    assert M % tm == 0 and N % tn == 0 and K % tk == 0, (name, M, N, K, tm, tn, tk)
    if mode == "nn":
        a_spec = pl.BlockSpec((tm, tk), lambda i, j, k: (i, k))
        b_spec = pl.BlockSpec((tk, tn), lambda i, j, k: (k, j))
        dims = (((1,), (0,)), ((), ()))
    elif mode == "nt":
        a_spec = pl.BlockSpec((tm, tk), lambda i, j, k: (i, k))
        b_spec = pl.BlockSpec((tn, tk), lambda i, j, k: (j, k))
        dims = (((1,), (1,)), ((), ()))
    else:
        a_spec = pl.BlockSpec((tk, tm), lambda i, j, k: (k, i))
        b_spec = pl.BlockSpec((tk, tn), lambda i, j, k: (k, j))
        dims = (((0,), (0,)), ((), ()))
    if b_view is not None:
        b_spec = pl.BlockSpec(b_view[1], lambda i, j, k: b_view[2](j, k))
    nk = K // tk
    n_extra, n_const, n_out, n_sum = len(extras), len(consts), len(out_dtypes), len(sums)
    tile = pl.BlockSpec((tm, tn), lambda i, j, k: (i, j))
    assert not sums or tn == N, "row sums need whole rows in a tile"

    n_after = len(after)

    def body(a_ref, b_ref, *rest):
        extra_refs, rest = rest[:n_extra + n_const], rest[n_extra + n_const + n_after:]
        out_refs, sum_refs, acc = rest[:n_out], rest[n_out:n_out + n_sum], rest[-1]
        i, k = pl.program_id(0), pl.program_id(2)
        def product():
            return lax.dot_general(a_ref[...].astype(BF), b_ref[...].astype(BF), dims, preferred_element_type=F32)

        if nk > 1:
            @pl.when(k == 0)
            def _():
                acc[...] = jnp.zeros_like(acc)

            acc[...] += product()

        @pl.when(k == nk - 1)
        def _():
            total = product() if nk == 1 else acc[...]
            outs = (total,) if epilogue is None else epilogue(total, *[e[...] for e in extra_refs])
            for o_ref, val in zip(out_refs, outs[:n_out], strict=True):
                o_ref[...] = val.astype(o_ref.dtype).reshape(o_ref.shape)
            for s_ref, val in zip(sum_refs, outs[n_out:], strict=True):
                @pl.when(i == 0)
                def _():
                    s_ref[...] = val

                @pl.when(i != 0)
                def _():
                    s_ref[...] += val

    out_specs = [tile] * n_out + [pl.BlockSpec((1, c), lambda i, j, k: (0, 0)) for c in sums]
    out_shape = [jax.ShapeDtypeStruct((M, N), dt) for dt in out_dtypes] + [jax.ShapeDtypeStruct((1, c), F32) for c in sums]
    operands, aliases = [a, b, *extras, *consts, *after], {}
    in_specs = ([a_spec, b_spec] + [tile] * n_extra
                + [pl.BlockSpec(c.shape, functools.partial(lambda nd, i, j, k: (0,) * nd, c.ndim)) for c in consts] + [ANY] * n_after)
    if into is not None:
        assert n_out == 1
        target, block, index = into
        out_specs = [pl.BlockSpec(block, lambda i, j, k: index(i, j))]
        out_shape = [jax.ShapeDtypeStruct(target.shape, target.dtype)]
        if not isinstance(target, jax.ShapeDtypeStruct):
            aliases = {len(operands): 0}
            operands.append(target)
            in_specs.append(ANY)
            n_after += 1

    outs = pl.pallas_call(
        body, name=name, grid=(M // tm, N // tn, nk),
        in_specs=in_specs, out_specs=out_specs, out_shape=out_shape,
        scratch_shapes=[pltpu.VMEM((tm, tn) if nk > 1 else (8, 128), F32)], input_output_aliases=aliases,
        compiler_params=_cparams(("arbitrary",) * 3 if sums else ("parallel", "parallel", "arbitrary")),
    )(*operands)
    return outs[0] if len(outs) == 1 else outs


def _row(arr, tm, col_block=None, width=None):
    width = arr.shape[1] if width is None else width
    cb = 0 if col_block is None else col_block
    return arr, pl.BlockSpec((tm, width), lambda i: (i, cb))


def _full(arr):
    nd = arr.ndim
    return arr, pl.BlockSpec(arr.shape, lambda i: (0,) * nd)


def _rowwise(name, body, T, tm, ins, outs, sums=(), after=()):
    n_in, n_out, n_sum, n_after = len(ins), len(outs), len(sums), len(after)

    def kern(*refs):
        in_refs, refs = refs[:n_in], refs[n_in + n_after:]
        out_refs, sum_refs = refs[:n_out], refs[n_out:]
        res = body(*[r[...] for r in in_refs])
        res = res if isinstance(res, tuple) else (res,)
        for o_ref, val in zip(out_refs, res[:n_out], strict=True):
            o_ref[...] = val.astype(o_ref.dtype)
        if n_sum:
            @pl.when(pl.program_id(0) == 0)
            def _():
                for s_ref in sum_refs:
                    s_ref[...] = jnp.zeros_like(s_ref)

            for s_ref, val in zip(sum_refs, res[n_out:], strict=True):
                s_ref[...] += val

    res = pl.pallas_call(
        kern, name=name, grid=(T // tm,),
        in_specs=[spec for _, spec in ins] + [ANY] * n_after,
        out_specs=[pl.BlockSpec((tm, c), lambda i: (i, 0)) for c, _ in outs]
        + [pl.BlockSpec((1, c), lambda i: (0, 0)) for c in sums],
        out_shape=[jax.ShapeDtypeStruct((T, c), dt) for c, dt in outs]
        + [jax.ShapeDtypeStruct((1, c), F32) for c in sums],
        compiler_params=_cparams(("arbitrary",)),
    )(*[a for a, _ in ins], *after)
    return res[0] if len(res) == 1 else res


def _sigmoid(x):
    return 1.0 / (1.0 + jnp.exp(-x))


def _rms(h):
    return lax.rsqrt(jnp.mean(h * h, axis=-1, keepdims=True) + RMS_EPS)


def _rms_bwd(dy, h, g):
    r = _rms(h)
    n = h * r
    dn = dy * g
    dh = r * (dn - n * jnp.mean(dn * n, axis=-1, keepdims=True))
    return dh, jnp.sum(dy * n, axis=0, keepdims=True)


def _rope(x, cos2, sin_signed):
    lane = lax.broadcasted_iota(jnp.int32, x.shape, 1)
    swapped = jnp.where((lane % HEAD_DIM) < HEAD_DIM // 2, pltpu.roll(x, 128 - HEAD_DIM // 2, 1), pltpu.roll(x, HEAD_DIM // 2, 1))
    return x * cos2 + swapped * sin_signed


NA_KEYS = NA_WIN_ROWS * GRID_W
NA_BASES = 8


def _na_row_geometry(r, rows):
    first = jnp.clip(r - NA_WIN_ROWS // 2, 0, rows - NA_WIN_ROWS)
    base = first - r + (NA_WIN_ROWS - 1)
    return pl.multiple_of(first * GRID_W, GRID_W), base


NA_ROWS_PER_STEP = 16
NA_BWD_ROWS_PER_STEP = 8


def _softmax_rows(s):
    p = jnp.exp(s - jnp.max(s, axis=-1, keepdims=True))
    return p / jnp.sum(p, axis=-1, keepdims=True)


def _na_probs(q, kw, bias):
    return _softmax_rows(lax.dot_general(q, kw, (((1,), (1,)), ((), ())), preferred_element_type=F32) + bias)


def _split_pair(t):
    first = lax.broadcasted_iota(jnp.int32, t.shape, 1) < HEAD_DIM
    zero = jnp.zeros_like(t)
    return jnp.where(first, t, zero), jnp.where(first, zero, t)


def _join_pair(a, b):
    return jnp.where(lax.broadcasted_iota(jnp.int32, a.shape, 1) < HEAD_DIM, a, b)


_NT = (((1,), (1,)), ((), ()))
_TN = (((0,), (0,)), ((), ()))


def _na_fwd(qkv, tab):
    T = qkv.shape[0]
    rows = T // GRID_W
    n_pairs = NA_WIDTH // 128

    def body(q_ref, k_ref, v_ref, tab_ref, y_ref):
        def step(it, carry):
            geo = [_na_row_geometry(it * NA_ROWS_PER_STEP + u, rows) for u in range(NA_ROWS_PER_STEP)]
            q0s = [pl.multiple_of((it * NA_ROWS_PER_STEP + u) * GRID_W, GRID_W) for u in range(NA_ROWS_PER_STEP)]
            ss = [lax.dot_general(jnp.concatenate(_split_pair(q_ref[pl.ds(q0, GRID_W), :] * Q_SCALE), axis=0),
                                  k_ref[pl.ds(k0, NA_KEYS), :], _NT, preferred_element_type=F32)
                  for q0, (k0, _) in zip(q0s, geo)]
            ps = [_softmax_rows(s + jnp.concatenate([tab_ref[0, base], tab_ref[1, base]], axis=0)) for s, (_, base) in zip(ss, geo)]
            ys = [jnp.dot(p.astype(BF), v_ref[pl.ds(k0, NA_KEYS), :], preferred_element_type=F32) for p, (k0, _) in zip(ps, geo)]
            for q0, y2 in zip(q0s, ys):
                y_ref[pl.ds(q0, GRID_W), :] = _join_pair(y2[:GRID_W], y2[GRID_W:]).astype(y_ref.dtype)
            return carry

        lax.fori_loop(0, rows // NA_ROWS_PER_STEP, step, 0)

    def cols(first):
        return pl.BlockSpec((T, 128), lambda j: (0, first + j))

    return pl.pallas_call(
        body, name="na_fwd", grid=(n_pairs,),
        in_specs=[cols(0), cols(n_pairs), cols(2 * n_pairs), pl.BlockSpec((2, NA_BASES, GRID_W, NA_KEYS), lambda j: (j, 0, 0, 0))],
        out_specs=cols(0), out_shape=jax.ShapeDtypeStruct((T, NA_WIDTH), BF),
        compiler_params=_cparams(("parallel",)),
    )(qkv, qkv, qkv, tab)


def _na_bwd(qkv, tab, do):
    T = qkv.shape[0]
    rows = T // GRID_W
    n_pairs = NA_WIDTH // 128

    def body(q_ref, k_ref, v_ref, tab_ref, do_ref, dq_ref, dk_ref, dv_ref, dtab_ref):
        dk_ref[...] = jnp.zeros_like(dk_ref)
        dv_ref[...] = jnp.zeros_like(dv_ref)
        dtab_ref[...] = jnp.zeros_like(dtab_ref)

        def step(it, carry):
            U = NA_BWD_ROWS_PER_STEP
            geo = [_na_row_geometry(it * U + u, rows) for u in range(U)]
            q0s = [pl.multiple_of((it * U + u) * GRID_W, GRID_W) for u in range(U)]
            q2s = [jnp.concatenate(_split_pair(q_ref[pl.ds(q0, GRID_W), :] * Q_SCALE), axis=0) for q0 in q0s]
            do2s = [jnp.concatenate(_split_pair(do_ref[pl.ds(q0, GRID_W), :]), axis=0) for q0 in q0s]
            ss = [lax.dot_general(q2, k_ref[pl.ds(k0, NA_KEYS), :], _NT, preferred_element_type=F32) for q2, (k0, _) in zip(q2s, geo)]
            dps = [lax.dot_general(do2, v_ref[pl.ds(k0, NA_KEYS), :], _NT, preferred_element_type=F32) for do2, (k0, _) in zip(do2s, geo)]
            ps = [_softmax_rows(s + jnp.concatenate([tab_ref[0, base], tab_ref[1, base]], axis=0)) for s, (_, base) in zip(ss, geo)]
            dss = [p * (dp - jnp.sum(dp * p, axis=-1, keepdims=True)) for p, dp in zip(ps, dps)]
            dvs = [lax.dot_general(p.astype(BF), do2, _TN, preferred_element_type=F32) for p, do2 in zip(ps, do2s)]
            dsbs = [ds.astype(BF) for ds in dss]
            dqs = [jnp.dot(dsb, k_ref[pl.ds(k0, NA_KEYS), :], preferred_element_type=F32) for dsb, (k0, _) in zip(dsbs, geo)]
            dks = [lax.dot_general(dsb, q2, _TN, preferred_element_type=F32) for dsb, q2 in zip(dsbs, q2s)]
            for u in range(U):
                k0, base = geo[u]
                dtab_ref[0, base] += dss[u][:GRID_W]
                dtab_ref[1, base] += dss[u][GRID_W:]
                dq_ref[pl.ds(q0s[u], GRID_W), :] = _join_pair(dqs[u][:GRID_W], dqs[u][GRID_W:])
                dk_ref[pl.ds(k0, NA_KEYS), :] += dks[u]
                dv_ref[pl.ds(k0, NA_KEYS), :] += dvs[u]
            return carry

        lax.fori_loop(0, rows // NA_BWD_ROWS_PER_STEP, step, 0)

    def cols(first):
        return pl.BlockSpec((T, 128), lambda j: (0, first + j))

    tabs = pl.BlockSpec((2, NA_BASES, GRID_W, NA_KEYS), lambda j: (j, 0, 0, 0))
    wide = jax.ShapeDtypeStruct((T, NA_WIDTH), F32)
    return pl.pallas_call(
        body, name="na_bwd", grid=(n_pairs,),
        in_specs=[cols(0), cols(n_pairs), cols(2 * n_pairs), tabs, cols(0)],
        out_specs=[cols(0), cols(0), cols(0), tabs],
        out_shape=[wide, wide, wide, jax.ShapeDtypeStruct((NA_HEADS, NA_BASES, GRID_W, NA_KEYS), F32)],
        compiler_params=_cparams(("parallel",)),
    )(qkv, qkv, qkv, tab, do)


def _na_bias_table(rpb):
    H, n_rows, n_cols = rpb.shape

    def body(r_ref, tab_ref):
        q = lax.broadcasted_iota(jnp.int32, (GRID_W, 128), 0)
        kc = lax.broadcasted_iota(jnp.int32, (GRID_W, 128), 1)
        first = jnp.clip(q - NA_WIN_COLS // 2, 0, GRID_W - NA_WIN_COLS)
        valid = (kc >= first) & (kc < first + NA_WIN_COLS)
        toeplitz = []
        for ro in range(n_rows):
            row = jnp.broadcast_to(r_ref[pl.ds(ro, 1), :], (GRID_W, 128))
            shifted = pltpu.roll(pltpu.roll(row, 128 - (NA_WIN_COLS - 1), 1), 0, 1, stride=1, stride_axis=0)
            toeplitz.append(jnp.where(valid, shifted, NEG_INF))
        for base in range(NA_BASES):
            for j in range(NA_WIN_ROWS // 2):
                even, odd = toeplitz[base + 2 * j], toeplitz[base + 2 * j + 1]
                tab_ref[base, :, pl.ds(j * 128, 128)] = jnp.where(kc < GRID_W, even, pltpu.roll(odd, GRID_W, 1))

    padded = jnp.pad(rpb, ((0, 0), (0, 16 - n_rows), (0, 128 - n_cols)))
    return pl.pallas_call(
        body, name="na_bias_table", grid=(H,),
        in_specs=[pl.BlockSpec((None, 16, 128), lambda h: (h, 0, 0))],
        out_specs=pl.BlockSpec((None, NA_BASES, GRID_W, NA_KEYS), lambda h: (h, 0, 0, 0)),
        out_shape=jax.ShapeDtypeStruct((H, NA_BASES, GRID_W, NA_KEYS), F32),
        compiler_params=_cparams(("parallel",)),
    )(padded)


def _na_rpb_grad(dtab):
    H = dtab.shape[0]
    n_rows = 2 * NA_WIN_ROWS - 1
    n_cols = 2 * NA_WIN_COLS - 1

    def body(d_ref, o_ref):
        lane = lax.broadcasted_iota(jnp.int32, (GRID_W, 128), 1)
        low = lane < GRID_W
        out_rows = []
        for ro in range(n_rows):
            acc = jnp.zeros((GRID_W, 128), F32)
            for base in range(NA_BASES):
                i = ro - base
                if not 0 <= i < NA_WIN_ROWS:
                    continue
                pair = d_ref[base, :, pl.ds((i // 2) * 128, 128)]
                if i % 2:
                    pair = pltpu.roll(pair, GRID_W, 1)
                acc = acc + jnp.where(low, pair, 0.0)
            skew = pltpu.roll(acc, 0, 1, stride=1, stride_axis=0)
            diag = jnp.sum(skew, axis=0, keepdims=True)
            out_rows.append(pltpu.roll(jnp.broadcast_to(diag, (8, 128)), 128 - (GRID_W - NA_WIN_COLS), 1)[:1])
        out_rows.append(jnp.zeros((1, 128), F32))
        res = jnp.concatenate(out_rows, axis=0)
        o_ref[...] = jnp.where(lax.broadcasted_iota(jnp.int32, res.shape, 1) < n_cols, res, 0.0)

    return pl.pallas_call(
        body, name="na_rpb_grad", grid=(H,),
        in_specs=[pl.BlockSpec((None, NA_BASES, GRID_W, NA_KEYS), lambda h: (h, 0, 0, 0))],
        out_specs=pl.BlockSpec((None, n_rows + 1, 128), lambda h: (h, 0, 0)),
        out_shape=jax.ShapeDtypeStruct((H, n_rows + 1, 128), F32),
        compiler_params=_cparams(("parallel",)),
    )(jnp.flip(dtab, axis=2))


BAND_Q = 128
BAND_KEYS = BAND_Q + 2 * DIL_RADIUS


def _band_geometry(n, L):
    q0 = pl.multiple_of(n * BAND_Q, BAND_Q)
    k0 = pl.multiple_of(jnp.clip(q0 - DIL_RADIUS, 0, L - BAND_KEYS), DIL_RADIUS)
    qi = q0 + lax.broadcasted_iota(jnp.int32, (BAND_Q, BAND_KEYS), 0)
    kj = k0 + lax.broadcasted_iota(jnp.int32, (BAND_Q, BAND_KEYS), 1)
    return q0, k0, jnp.abs(qi - kj) <= DIL_RADIUS


DIL_PAIRS = DIL_OUT_WIDTH // 128


def _residue_shape(dil, T, dtype):
    return jax.ShapeDtypeStruct((DIL_PAIRS, dil, T // dil, 128), dtype)


def _residue_tile(dil, tm):
    return pl.BlockSpec((DIL_PAIRS, dil, tm // dil, 128), lambda i: (0, 0, i, 0))


def _to_natural(ref, scratch, dil, tm):
    tiles = []
    for pair in range(DIL_PAIRS):
        if dil == 1:
            tiles.append(ref[pair, 0].astype(F32))
            continue
        for r in range(dil):
            scratch[pl.ds(r, tm // dil, stride=dil), :] = ref[pair, r].astype(F32)
        tiles.append(scratch[...])
    return tiles


def _from_natural(tile, scratch, ref, pair, dil, tm):
    if dil == 1:
        ref[pair, 0] = tile.astype(ref.dtype)
        return
    scratch[...] = tile
    for r in range(dil):
        ref[pair, r] = scratch[pl.ds(r, tm // dil, stride=dil), :].astype(ref.dtype)


def _band_specs(group, T):
    dil = DIL_GROUPS[group][1]
    L = T // dil
    assert L % BAND_Q == 0 and L >= BAND_KEYS, (T, dil)
    return L, (dil * DIL_PAIRS,), pl.BlockSpec((None, None, L, 128), lambda s: (s % DIL_PAIRS, s // DIL_PAIRS, 0, 0))


BAND_BLOCKS_PER_STEP = 8


def _band_softmax(s, valid):
    s = jnp.where(valid, s, NEG_INF)
    m = jnp.max(s, axis=-1, keepdims=True)
    p = jnp.exp(s - m)
    l = jnp.sum(p, axis=-1, keepdims=True)
    return p / l, m + jnp.log(l)


def _band_fwd(q, k, v, group):
    T = q.shape[1] * q.shape[2]
    L, grid, spec = _band_specs(group, T)
    U = min(BAND_BLOCKS_PER_STEP, L // BAND_Q)

    def body(q_ref, k_ref, v_ref, o_ref, lse_ref):
        def step(it, carry):
            geo = [_band_geometry(it * U + u, L) for u in range(U)]
            ss = [lax.dot_general(jnp.concatenate(_split_pair(q_ref[pl.ds(q0, BAND_Q), :]), axis=0),
                                  k_ref[pl.ds(k0, BAND_KEYS), :], _NT, preferred_element_type=F32) for q0, k0, _ in geo]
            pls = [_band_softmax(s, jnp.concatenate([valid, valid], axis=0)) for s, (_, _, valid) in zip(ss, geo)]
            os = [jnp.dot(p.astype(BF), v_ref[pl.ds(k0, BAND_KEYS), :], preferred_element_type=F32) for (p, _), (_, k0, _) in zip(pls, geo)]
            for (q0, _, _), o2, (_, lse) in zip(geo, os, pls):
                o_ref[pl.ds(q0, BAND_Q), :] = _join_pair(o2[:BAND_Q], o2[BAND_Q:])
                lse2 = jnp.broadcast_to(lse, (2 * BAND_Q, 128))
                lse_ref[pl.ds(q0, BAND_Q), :] = _join_pair(lse2[:BAND_Q], lse2[BAND_Q:])
            return carry

        lax.fori_loop(0, L // (BAND_Q * U), step, 0)

    res = _residue_shape(DIL_GROUPS[group][1], T, F32)
    return pl.pallas_call(
        body, name=f"band_fwd_g{group}", grid=grid,
        in_specs=[spec] * 3, out_specs=[spec] * 2, out_shape=[res, res],
        compiler_params=_cparams(("parallel",)),
    )(q, k, v)


def _band_bwd(q, k, v, do, dlse, group):
    T = q.shape[1] * q.shape[2]
    L, grid, spec = _band_specs(group, T)
    U = min(BAND_BLOCKS_PER_STEP, L // BAND_Q)

    def body(q_ref, k_ref, v_ref, do_ref, dlse_ref, dq_ref, dk_ref, dv_ref):
        dk_ref[...] = jnp.zeros_like(dk_ref)
        dv_ref[...] = jnp.zeros_like(dv_ref)

        def step(it, carry):
            geo = [_band_geometry(it * U + u, L) for u in range(U)]
            q2s = [jnp.concatenate(_split_pair(q_ref[pl.ds(q0, BAND_Q), :]), axis=0) for q0, _, _ in geo]
            do2s = [jnp.concatenate(_split_pair(do_ref[pl.ds(q0, BAND_Q), :]), axis=0) for q0, _, _ in geo]
            ss = [lax.dot_general(q2, k_ref[pl.ds(k0, BAND_KEYS), :], _NT, preferred_element_type=F32) for q2, (_, k0, _) in zip(q2s, geo)]
            dps = [lax.dot_general(do2, v_ref[pl.ds(k0, BAND_KEYS), :], _NT, preferred_element_type=F32) for do2, (_, k0, _) in zip(do2s, geo)]
            ps = [_band_softmax(s, jnp.concatenate([valid, valid], axis=0))[0] for s, (_, _, valid) in zip(ss, geo)]
            dss = []
            for p, dp, (q0, _, _) in zip(ps, dps, geo):
                dl = dlse_ref[pl.ds(q0, BAND_Q), :]
                dl2 = jnp.concatenate([dl[:, :1], dl[:, HEAD_DIM:HEAD_DIM + 1]], axis=0)
                dss.append(p * (dp - jnp.sum(dp * p, axis=-1, keepdims=True) + dl2))
            dvs = [lax.dot_general(p.astype(BF), do2, _TN, preferred_element_type=F32) for p, do2 in zip(ps, do2s)]
            dsbs = [ds.astype(BF) for ds in dss]
            dqs = [jnp.dot(dsb, k_ref[pl.ds(k0, BAND_KEYS), :], preferred_element_type=F32) for dsb, (_, k0, _) in zip(dsbs, geo)]
            dks = [lax.dot_general(dsb, q2, _TN, preferred_element_type=F32) for dsb, q2 in zip(dsbs, q2s)]
            for u, (q0, k0, _) in enumerate(geo):
                dq_ref[pl.ds(q0, BAND_Q), :] = _join_pair(dqs[u][:BAND_Q], dqs[u][BAND_Q:])
                dk_ref[pl.ds(k0, BAND_KEYS), :] += dks[u]
                dv_ref[pl.ds(k0, BAND_KEYS), :] += dvs[u]
            return carry

        lax.fori_loop(0, L // (BAND_Q * U), step, 0)

    res = _residue_shape(DIL_GROUPS[group][1], T, F32)
    return pl.pallas_call(
        body, name=f"band_bwd_g{group}", grid=grid,
        in_specs=[spec] * 5, out_specs=[spec] * 3, out_shape=[res] * 3,
        compiler_params=_cparams(("parallel",)),
    )(q, k, v, do, dlse)


def _head_sums(t):
    head = lax.broadcasted_iota(jnp.int32, t.shape, 1) // HEAD_DIM
    out = jnp.zeros_like(t)
    for h in range(t.shape[1] // HEAD_DIM):
        mine = head == h
        out = jnp.where(mine, jnp.sum(jnp.where(mine, t, 0.0), axis=-1, keepdims=True), out)
    return out


def _dil_merge_fwd(os, lses, T, tm):
    G = len(DIL_GROUPS)
    W = DIL_OUT_WIDTH
    dils = [d for _, d in DIL_GROUPS]

    def body(*refs):
        o_refs, lse_refs = refs[:G], refs[G:2 * G]
        y_ref, w_refs, on_refs, scratch = refs[2 * G], refs[2 * G + 1:3 * G + 1], refs[3 * G + 1:4 * G + 1], refs[-1]
        o = [jnp.concatenate(_to_natural(r, scratch, d, tm), axis=1) for r, d in zip(o_refs, dils)]
        ls = [jnp.concatenate(_to_natural(r, scratch, d, tm), axis=1) for r, d in zip(lse_refs, dils)]
        m = functools.reduce(jnp.maximum, ls)
        es = [jnp.exp(l - m) for l in ls]
        tot = functools.reduce(jnp.add, es)
        ws = [e / tot for e in es]
        y_ref[...] = functools.reduce(jnp.add, [w * t for w, t in zip(ws, o)]).astype(y_ref.dtype)
        for g in range(G):
            w_refs[g][...] = ws[g]
            on_refs[g][...] = o[g]

    nat = pl.BlockSpec((tm, W), lambda i: (i, 0))
    res = pl.pallas_call(
        body, name="dil_merge_fwd", grid=(T // tm,),
        in_specs=[_residue_tile(d, tm) for d in dils] * 2,
        out_specs=[nat] * (2 * G + 1),
        out_shape=[jax.ShapeDtypeStruct((T, W), BF)] + [jax.ShapeDtypeStruct((T, W), F32)] * (2 * G),
        scratch_shapes=[pltpu.VMEM((tm, 128), F32)],
        compiler_params=_cparams(("parallel",)),
    )(*os, *lses)
    return res[0], res[1:G + 1], res[G + 1:]


def _dil_merge_bwd(dy, os, ws, tm, after=()):
    G = len(DIL_GROUPS)
    T, W = dy.shape
    dils = [d for _, d in DIL_GROUPS]
    n_after = len(after)

    def body(*refs):
        dyt = refs[0][...]
        o, w = [r[...] for r in refs[1:G + 1]], [r[...] for r in refs[G + 1:2 * G + 1]]
        refs = refs[2 * G + 1 + n_after:]
        do_refs, dlse_refs, scratch = refs[:G], refs[G:2 * G], refs[-1]
        dws = [_head_sums(dyt * t) for t in o]
        mean = functools.reduce(jnp.add, [a * b for a, b in zip(w, dws)])
        for g, d in enumerate(dils):
            do, dlse = w[g] * dyt, w[g] * (dws[g] - mean)
            for pair in range(DIL_PAIRS):
                cols = slice(pair * 128, (pair + 1) * 128)
                _from_natural(do[:, cols], scratch, do_refs[g], pair, d, tm)
                _from_natural(dlse[:, cols], scratch, dlse_refs[g], pair, d, tm)

    nat = pl.BlockSpec((tm, W), lambda i: (i, 0))
    res = pl.pallas_call(
        body, name="dil_merge_bwd", grid=(T // tm,),
        in_specs=[nat] * (2 * G + 1) + [ANY] * n_after,
        out_specs=[_residue_tile(d, tm) for d in dils] * 2,
        out_shape=[_residue_shape(d, T, BF) for d in dils] + [_residue_shape(d, T, F32) for d in dils],
        scratch_shapes=[pltpu.VMEM((tm, 128), F32)],
        compiler_params=_cparams(("parallel",)),
    )(dy, *os, *ws, *after)
    return res[:G], res[G:]


def _qkv_prep(z, cos2, sin_signed, tm):
    T = z.shape[0]
    G = len(DIL_GROUPS)
    dils = [d for _, d in DIL_GROUPS]
    n_dil_blocks = 3 * DIL_WIDTH // 128

    def body(*refs):
        blocks = refs[:n_dil_blocks]
        cos_ref, sin_ref = refs[n_dil_blocks], refs[1 + n_dil_blocks]
        outs = refs[2 + n_dil_blocks:]
        for part in range(3):
            for g, d in enumerate(dils):
                out = outs[g * 3 + part]
                for pair in range(DIL_PAIRS):
                    blk = blocks[part * (DIL_WIDTH // 128) + g * DIL_PAIRS + pair]
                    for r in range(d):
                        rows = pl.ds(r, tm // d, stride=d) if d > 1 else slice(None)
                        x = blk[rows, :]
                        if part < 2:
                            x = _rope(x, cos_ref[rows, :], sin_ref[rows, :])
                        if part == 0:
                            x = x * Q_SCALE
                        out[pair, r] = x.astype(out.dtype)

    lane_block = [pl.BlockSpec((tm, 128), functools.partial(lambda b, i: (i, b), b)) for b in range(n_dil_blocks)]
    tab = pl.BlockSpec((tm, 128), lambda i: (i, 0))
    res = pl.pallas_call(
        body, name="qkv_prep", grid=(T // tm,),
        in_specs=lane_block + [tab, tab],
        out_specs=[_residue_tile(d, tm) for d in dils for _ in range(3)],
        out_shape=[_residue_shape(d, T, BF) for d in dils for _ in range(3)],
        compiler_params=_cparams(("parallel",)),
    )(*[z] * n_dil_blocks, cos2, sin_signed)
    return [res[3 * g:3 + 3 * g] for g in range(G)]


def _qkv_unprep(d_na, d_dil, cos2, sin_signed, tm, after=()):
    T = d_na[0].shape[0]
    G = len(DIL_GROUPS)
    dils = [d for _, d in DIL_GROUPS]
    n_after = len(after)

    def body(*refs):
        dq, dk, dv = (r[...] for r in refs[:3])
        res_refs = refs[3:3 + 3 * G]
        cs, sn = refs[3 + 3 * G][...], refs[4 + 3 * G][...]
        out, scratch = refs[5 + 3 * G + n_after], refs[-1]
        cols = [dq * Q_SCALE, dk, dv]
        for part in range(3):
            for g, d in enumerate(dils):
                for x in _to_natural(res_refs[g * 3 + part], scratch, d, tm):
                    if part < 2:
                        x = _rope(x, cs, -sn)
                    cols.append(x * Q_SCALE if part == 0 else x)
        out[...] = jnp.concatenate(cols, axis=1).astype(out.dtype)

    wide = pl.BlockSpec((tm, NA_WIDTH), lambda i: (i, 0))
    tab = pl.BlockSpec((tm, 128), lambda i: (i, 0))
    return pl.pallas_call(
        body, name="qkv_unprep", grid=(T // tm,),
        in_specs=[wide] * 3 + [_residue_tile(d, tm) for d in dils for _ in range(3)] + [tab, tab] + [ANY] * n_after,
        out_specs=pl.BlockSpec((tm, QKV_WIDTH), lambda i: (i, 0)),
        out_shape=jax.ShapeDtypeStruct((T, QKV_WIDTH), BF),
        scratch_shapes=[pltpu.VMEM((tm, 128), F32)],
        compiler_params=_cparams(("parallel",)),
    )(*d_na, *[t for g in range(G) for t in d_dil[g]], cos2, sin_signed, *after)


def _rope_tables(positions):
    half = HEAD_DIM // 2
    inv_freq = ROPE_THETA ** (-jnp.arange(half, dtype=F32) / half)
    ang = positions.astype(F32)[:, None] * inv_freq
    cos, sin = jnp.cos(ang), jnp.sin(ang)
    return jnp.tile(jnp.concatenate([cos, cos], axis=1), (1, 2)), jnp.tile(jnp.concatenate([-sin, sin], axis=1), (1, 2))


def _pack_rows(t):
    return t.reshape(-1, PACK_W)


def _me():
    return lax.axis_index("x"), lax.axis_index("y"), lax.axis_index("c")


def _other_chips(x, y):
    return [(1 - x, y), (x, 1 - y), (1 - x, 1 - y)]


def _pair_sum(g, got, tm):
    S, R, W = g.shape
    half = R // 2
    nb = half // tm

    def body(pos_ref, g_ref, got_ref, own_ref, ob_ref):
        tot = g_ref[...] + got_ref[...]
        ob_ref[...] = tot.astype(ob_ref.dtype)

        @pl.when(pl.program_id(1) == pos_ref[1])
        def _():
            own_ref[...] = tot

    tile = pl.BlockSpec((None, tm, W), lambda i, s, pos_ref: (s, i, 0))
    c, chip = lax.axis_index("c"), 2 * lax.axis_index("x") + lax.axis_index("y")
    return pl.pallas_call(
        body, name="pair_sum",
        grid_spec=pltpu.PrefetchScalarGridSpec(
            num_scalar_prefetch=1, grid=(nb, S),
            in_specs=[pl.BlockSpec((None, tm, W), lambda i, s, pos_ref: (s, pos_ref[0] * nb + i, 0)), tile],
            out_specs=[pl.BlockSpec((tm, W), lambda i, s, pos_ref: (i, 0)), tile]),
        out_shape=[jax.ShapeDtypeStruct((half, W), F32), jax.ShapeDtypeStruct((S, half, W), BF)],
        compiler_params=_cparams(("parallel", "arbitrary")),
    )(jnp.stack([c, chip]).astype(jnp.int32), g, got)


def _chip_sum(own, others, tm):
    n, h, W = others.shape
    nb = h // tm

    def body(c_ref, own_ref, p_ref, o_ref):
        o_ref[...] = ((own_ref[...] + p_ref[0].astype(F32)) + p_ref[1].astype(F32)) + p_ref[2].astype(F32)

    return pl.pallas_call(
        body, name="chip_sum",
        grid_spec=pltpu.PrefetchScalarGridSpec(
            num_scalar_prefetch=1, grid=(nb,),
            in_specs=[pl.BlockSpec((tm, W), lambda i, c_ref: (i, 0)), pl.BlockSpec((n, tm, W), lambda i, c_ref: (0, i, 0))],
            out_specs=pl.BlockSpec((tm, W), lambda i, c_ref: (c_ref[0] * nb + i, 0))),
        out_shape=jax.ShapeDtypeStruct((2 * h, W), F32),
        compiler_params=_cparams(("parallel",)),
    )(lax.axis_index("c").reshape(1).astype(jnp.int32), own, others)


def _join_halves(shard):
    h = shard.shape[0] // 2

    def body(in_ref, out_ref, send_sem, recv_sem):
        x, y, c = _me()
        cp = pltpu.make_async_remote_copy(
            src_ref=in_ref.at[pl.ds(c * h, h), :], dst_ref=out_ref.at[pl.ds(c * h, h), :],
            send_sem=send_sem, recv_sem=recv_sem, device_id=(x, y, 1 - c), device_id_type=MESH)
        cp.start()
        pltpu.make_async_remote_copy(
            src_ref=in_ref.at[pl.ds(c * h, h), :], dst_ref=out_ref.at[pl.ds((1 - c) * h, h), :],
            send_sem=send_sem, recv_sem=recv_sem, device_id=(x, y, 1 - c), device_id_type=MESH).wait_recv()
        cp.wait_send()

    return pl.pallas_call(
        body, name="join_halves", in_specs=[ANY], out_specs=ANY,
        out_shape=jax.ShapeDtypeStruct(shard.shape, shard.dtype), input_output_aliases={0: 0},
        scratch_shapes=[pltpu.SemaphoreType.DMA, pltpu.SemaphoreType.DMA],
    )(shard)


def _allreduce_small(s, after=()):
    R, W = s.shape
    n_after = len(after)

    def body(s_ref, *rest):
        o_ref, buf, send_sems, recv_sems = rest[n_after:]
        x, y, c = _me()
        me = 4 * x + 2 * y + c
        buf[me] = s_ref[...]
        peers = [((x + fx) % 2, (y + fy) % 2, (c + fc) % 2) for fx in range(2) for fy in range(2) for fc in range(2)][1:]
        sends = [pltpu.make_async_remote_copy(
            src_ref=s_ref, dst_ref=buf.at[me], send_sem=send_sems.at[k], recv_sem=recv_sems.at[k],
            device_id=peer, device_id_type=MESH) for k, peer in enumerate(peers)]
        for cp in sends:
            cp.start()
        for k, peer in enumerate(peers):
            pltpu.make_async_remote_copy(
                src_ref=s_ref, dst_ref=buf.at[4 * peer[0] + 2 * peer[1] + peer[2]], send_sem=send_sems.at[k],
                recv_sem=recv_sems.at[k], device_id=peer, device_id_type=MESH).wait_recv()
        for cp in sends:
            cp.wait_send()
        total = buf[0]
        for d in range(1, N_DEV):
            total = total + buf[d]
        o_ref[...] = total

    return pl.pallas_call(
        body, name="allreduce_small",
        in_specs=[pl.BlockSpec(memory_space=pltpu.VMEM)] + [ANY] * n_after, out_specs=pl.BlockSpec(memory_space=pltpu.VMEM),
        out_shape=jax.ShapeDtypeStruct((R, W), F32),
        scratch_shapes=[pltpu.VMEM((N_DEV, R, W), F32), pltpu.SemaphoreType.DMA((N_DEV - 1,)), pltpu.SemaphoreType.DMA((N_DEV - 1,))],
    )(s, *after)


HBM_SPEC = pl.BlockSpec(memory_space=pltpu.HBM)
SEM_SPEC = pl.BlockSpec(memory_space=pltpu.SEMAPHORE)
DATAFLOW = pltpu.SideEffectType.DATAFLOW_SIDE_EFFECTING


class _InFlight(NamedTuple):
    sems: tuple
    src: jax.Array
    land: jax.Array
    token: jax.Array


def _split_start(name, src, land_shape, land_dtype, n, copies, after=()):
    n_after = len(after)

    def body(src_ref, land_ref, *rest):
        rest = rest[n_after:]
        sems, token = rest[:2 * n], rest[-1]
        for k, (s, d, peer) in enumerate(copies(src_ref, land_ref)):
            pltpu.make_async_remote_copy(src_ref=s, dst_ref=d, send_sem=sems[k], recv_sem=sems[n + k],
                                         device_id=peer, device_id_type=MESH).start()
        token[...] = jnp.zeros_like(token)

    outs = pl.pallas_call(
        body, name=name,
        out_shape=(*[pltpu.SemaphoreType.DMA(())] * (2 * n), pltpu.HBM(src.shape, src.dtype), pltpu.HBM(land_shape, land_dtype),
                   jax.ShapeDtypeStruct((8, 128), F32)),
        in_specs=(HBM_SPEC, HBM_SPEC, *[ANY] * n_after),
        out_specs=(*[SEM_SPEC] * (2 * n), HBM_SPEC, HBM_SPEC, pl.BlockSpec(memory_space=pltpu.VMEM)),
        input_output_aliases={0: 2 * n, 1: 2 * n + 1},
        compiler_params=pltpu.CompilerParams(has_side_effects=DATAFLOW),
    )(pltpu.with_memory_space_constraint(src, pltpu.HBM), pltpu.with_memory_space_constraint(lax.empty(land_shape, land_dtype), pltpu.HBM),
      *after)
    return _InFlight(tuple(outs[:2 * n]), outs[2 * n], outs[2 * n + 1], outs[2 * n + 2])


def _split_wait(name, flight, after, n, copies):
    def body(src_ref, land_ref, *rest):
        sems = rest[:2 * n]
        for k, (s, d, peer) in enumerate(copies(src_ref, land_ref)):
            cp = pltpu.make_async_remote_copy(src_ref=s, dst_ref=d, send_sem=sems[k], recv_sem=sems[n + k],
                                              device_id=peer, device_id_type=MESH)
            cp.wait_send()
            cp.wait_recv()

    return pl.pallas_call(
        body, name=name,
        out_shape=(pltpu.HBM(flight.src.shape, flight.src.dtype), pltpu.HBM(flight.land.shape, flight.land.dtype)),
        in_specs=(HBM_SPEC, HBM_SPEC, *[SEM_SPEC] * (2 * n), ANY),
        out_specs=(HBM_SPEC, HBM_SPEC), input_output_aliases={0: 0, 1: 1},
        compiler_params=pltpu.CompilerParams(has_side_effects=DATAFLOW),
    )(flight.src, flight.land, *flight.sems, after)


def _gather_copies(src_ref, land_ref):
    x, y, c = _me()
    return [(src_ref, land_ref.at[2 * x + y], (*chip, c)) for chip in _other_chips(x, y)]


def _gather_start(packed, tag, after=()):
    return _split_start(f"gather_start_{tag}", packed, (N_CHIPS, *packed.shape), packed.dtype, 3, _gather_copies, after)


def _gather_wait(flight, after, tag):
    src, others = _split_wait(f"gather_wait_{tag}", flight, after, 3, _gather_copies)
    return lax.dynamic_update_slice(others, src[None], (2 * lax.axis_index("x") + lax.axis_index("y"), 0, 0))


def _across_copies(src_ref, land_ref):
    x, y, c = _me()
    half = src_ref.shape[0] // 2
    rows = pl.ds(c * half, half)
    return [(src_ref.at[rows, :], land_ref.at[2 * x + y, rows, :], (*chip, c)) for chip in _other_chips(x, y)]


def _to_sibling_copies(all_ref, unused_ref):
    x, y, c = _me()
    half = all_ref.shape[1] // 2
    places = [all_ref.at[2 * chip[0] + chip[1], pl.ds(c * half, half), :] for chip in _other_chips(x, y)]
    return [(place, place, (x, y, 1 - c)) for place in places]


def _gather_halves_start(shard):
    return _split_start("gather_in_across_start", shard, (N_CHIPS, *shard.shape), shard.dtype, 3, _across_copies)


def _gather_halves_finish(flight, after):
    shard, landed = _split_wait("gather_in_across_wait", flight, after, 3, _across_copies)
    forward = _split_start("gather_in_sibling_start", landed, (8, 128), landed.dtype, 3, _to_sibling_copies)
    others = _split_wait("gather_in_sibling_wait", forward, forward.token, 3, _to_sibling_copies)[0]
    return lax.dynamic_update_slice(others, shard[None], (2 * lax.axis_index("x") + lax.axis_index("y"), 0, 0))


def _swap_copies(src_ref, land_ref):
    x, y, c = _me()
    half = land_ref.shape[1]
    return [(src_ref.at[:, pl.ds((1 - c) * half, half), :], land_ref, (x, y, 1 - c))]


def _swap_start(g, tag):
    S, R, W = g.shape
    return _split_start(f"swap_halves_start_{tag}", g, (S, R // 2, W), g.dtype, 1, _swap_copies)


def _swap_wait(flight, after, tag):
    return _split_wait(f"swap_halves_wait_{tag}", flight, after, 1, _swap_copies)


def _scatter_copies(src_ref, land_ref):
    x, y, c = _me()
    return [(src_ref.at[2 * chip[0] + chip[1]], land_ref.at[j], (*chip, c)) for j, chip in enumerate(_other_chips(x, y))]


def _scatter_start(part, tag):
    S, h, W = part.shape
    return _split_start(f"scatter_chips_start_{tag}", part, (S - 1, h, W), part.dtype, 3, _scatter_copies)


def _scatter_wait(flight, after, tag):
    return _split_wait(f"scatter_chips_wait_{tag}", flight, after, 3, _scatter_copies)[1]


def _join_copies(shard_ref, unused_ref):
    x, y, c = _me()
    h = shard_ref.shape[0] // 2
    rows = shard_ref.at[pl.ds(c * h, h), :]
    return [(rows, rows, (x, y, 1 - c))]


def _join_start(shard):
    return _split_start("join_halves_start", shard, (8, 128), shard.dtype, 1, _join_copies)


def _join_wait(flight, after):
    return _split_wait("join_halves_wait", flight, after, 1, _join_copies)[0]


def _adamw(name, g, g_row0, w, m, v):
    _, R, C = w.shape
    tm = next(cand for cand in (368, 256, 128, 64, 32, 16, 8) if R % cand == 0)
    assert g_row0 % tm == 0 and g.shape[1] == C

    def body(g_ref, w_ref, m_ref, v_ref, go_ref, d_ref, mo_ref, vo_ref):
        gt = g_ref[...]
        mt = ADAM_B1 * m_ref[...] + (1.0 - ADAM_B1) * gt
        vt = ADAM_B2 * v_ref[...] + (1.0 - ADAM_B2) * jnp.square(gt)
        m_hat = mt / (1.0 - ADAM_B1 ** ADAM_STEP)
        v_hat = vt / (1.0 - ADAM_B2 ** ADAM_STEP)
        go_ref[...] = gt
        d_ref[...] = -ADAM_LR * (m_hat / (jnp.sqrt(v_hat) + ADAM_EPS) + ADAM_WD * w_ref[...])
        mo_ref[...] = mt
        vo_ref[...] = vt

    state = pl.BlockSpec((None, tm, C), lambda i: (0, i, 0))
    return pl.pallas_call(
        body, name=name, grid=(R // tm,),
        in_specs=[pl.BlockSpec((tm, C), lambda i: (g_row0 // tm + i, 0)), state, state, state],
        out_specs=[state] * 4, out_shape=[jax.ShapeDtypeStruct((1, R, C), F32)] * 4,
        compiler_params=_cparams(("parallel",)),
    )(g, w, m, v)


def _unpack_weights(gathered, names):
    S = gathered.shape[0]
    shard_shapes = {"w_in": (D_MODEL, (QKV_WIDTH + 2 * D_MODEL) // S), "w_branch_na": (NA_WIDTH, D_MODEL // S),
                    "w_branch_dil": (DIL_OUT_WIDTH, D_MODEL // S), "w_out": (D_MODEL // S, D_MODEL),
                    "w_up": (D_MODEL, D_FF // S), "w_down": (D_FF // S, D_MODEL),
                    "w_ple_gate": (D_MODEL // S, D_MODEL), "w_ple_proj": (PLE_DIM, D_MODEL // S)}
    col_sharded = {"w_in", "w_branch_na", "w_branch_dil", "w_up", "w_ple_proj"}
    out, r0 = {}, 0
    for name in names:
        rows, cols = shard_shapes[name]
        n = rows * cols // PACK_W
        t = gathered[:, r0:r0 + n, :].reshape(S, rows, cols)
        r0 += n
        out[name] = t.transpose(1, 0, 2).reshape(rows, S * cols) if name in col_sharded else t.reshape(S * rows, cols)
    return out


def kernel(x, p, positions, g_mix, w_in, rpb, w_branch_na, w_branch_dil, w_out, g_mlp, w_up, w_down, g_ple, w_ple_gate, w_ple_proj, g_final, loss_target, m_g_mix, m_w_in, m_rpb, m_w_branch_na, m_w_branch_dil, m_w_out, m_g_mlp, m_w_up, m_w_down, m_g_ple, m_w_ple_gate, m_w_ple_proj, m_g_final, v_g_mix, v_w_in, v_rpb, v_w_branch_na, v_w_branch_dil, v_w_out, v_g_mlp, v_w_up, v_w_down, v_g_ple, v_w_ple_gate, v_w_ple_proj, v_g_final):
    shards = {"w_in": w_in[0], "w_branch_na": w_branch_na[0], "w_branch_dil": w_branch_dil[0], "w_out": w_out[0],
              "w_up": w_up[0], "w_down": w_down[0], "w_ple_gate": w_ple_gate[0], "w_ple_proj": w_ple_proj[0]}
    params = {"w_in": w_in, "w_branch_na": w_branch_na, "w_branch_dil": w_branch_dil, "w_out": w_out, "w_up": w_up,
              "w_down": w_down, "w_ple_gate": w_ple_gate, "w_ple_proj": w_ple_proj,
              "m_w_in": m_w_in, "m_w_branch_na": m_w_branch_na, "m_w_branch_dil": m_w_branch_dil, "m_w_out": m_w_out,
              "m_w_up": m_w_up, "m_w_down": m_w_down, "m_w_ple_gate": m_w_ple_gate, "m_w_ple_proj": m_w_ple_proj,
              "v_w_in": v_w_in, "v_w_branch_na": v_w_branch_na, "v_w_branch_dil": v_w_branch_dil, "v_w_out": v_w_out,
              "v_w_up": v_w_up, "v_w_down": v_w_down, "v_w_ple_gate": v_w_ple_gate, "v_w_ple_proj": v_w_ple_proj}

    xs, ps, tgt = x[0], p[0, 0], loss_target[0]
    T = xs.shape[0]
    TM = 512
    gm, gl, gp, gf = g_mix, g_mlp, g_ple, g_final.reshape(1, D_MODEL)

    across = _gather_halves_start(shards["w_in"].astype(BF))
    a = _rowwise("norm_mix", lambda h, g: h * _rms(h) * g, T, TM, [_row(xs, TM), _full(gm)], [(D_MODEL, BF)],
                 after=(across.token,))
    cos2, sin_signed = _rope_tables(positions[0])
    tab = _na_bias_table(rpb[0])
    w_in_all = _gather_halves_finish(across, a)
    W = {"w_in": w_in_all.transpose(1, 0, 2).reshape(D_MODEL, -1)}
    mix_flight = _gather_start(jnp.concatenate([_pack_rows(shards[n].astype(BF)) for n in GATHER_MIX], axis=0), "mix",
                               after=(w_in_all,))
    rest_flight = _gather_start(jnp.concatenate([_pack_rows(shards[n].astype(BF)) for n in GATHER_MLP], axis=0), "mlp",
                                after=(mix_flight.token,))
    w_gates = W["w_in"][:, QKV_WIDTH:]

    n3 = 3 * NA_WIDTH
    qkv = _mm("in_na", a, W["w_in"], "nn", 1024, 768, 1024, [BF], after=(rest_flight.token,),
              b_view=(n3, (D_MODEL, 768), lambda j, k: (k, j)))
    z_dil = _mm("in_dil", a, W["w_in"], "nn", 1024, 768, 1024, [F32],
                b_view=(3 * DIL_WIDTH, (D_MODEL, 768), lambda j, k: (k, n3 // 768 + j)))
    z_gates = _mm("in_gates", a, w_gates, "nn", 1024,1024, 1024, [BF])

    dil_ops = _qkv_prep(z_dil, cos2, sin_signed, TM)
    y_na = _na_fwd(qkv, tab)
    band = [_band_fwd(*dil_ops[g], g) for g in range(len(DIL_GROUPS))]
    y_dil, w_grp, o_nat = _dil_merge_fwd([b[0] for b in band], [b[1] for b in band], T, TM)

    W.update(_unpack_weights(_gather_wait(mix_flight, y_dil, "mix"), GATHER_MIX))
    u_na = _mm("branch_na", y_na, W["w_branch_na"], "nn", 1024,1024, 512, [BF])
    u_dil = _mm("branch_dil", y_dil, W["w_branch_dil"], "nn", 1024,1024, 256, [BF])
    mixed = _rowwise(
        "gate_mix", lambda gn, gd, un, ud: _sigmoid(gn.astype(F32)) * un.astype(F32) + _sigmoid(gd.astype(F32)) * ud.astype(F32), T, TM,
        [_row(z_gates, TM, 0, D_MODEL), _row(z_gates, TM, 1, D_MODEL), _row(u_na, TM), _row(u_dil, TM)], [(D_MODEL, BF)])
    def add_norm(d, h, g):
        h = h + d
        return h, h * _rms(h) * g

    h1, cn = _mm("out_proj", mixed, W["w_out"], "nn", 512, 1024, 1024, [F32, BF], epilogue=add_norm, extras=(xs,), consts=(gl,))
    mlp_all = _gather_wait(rest_flight, cn, "mlp")
    W.update({n: t for n, t in _unpack_weights(mlp_all, GATHER_MLP).items() if n.startswith("w_ple")})
    chip_block = (None, D_MODEL, PACK_W)
    up, act = _mm("mlp_up", cn, mlp_all, "nn", 1024,1024, 1024, [BF, BF],
                  epilogue=lambda acc: (acc, jnp.square(jnp.maximum(acc, 0.0))), b_view=(D_FF, chip_block, lambda j, k: (j, 0, 0)))
    h2, en = _mm("mlp_down", act, mlp_all, "nn", 1024, 1024, 1024, [F32, BF], epilogue=add_norm, extras=(h1,), consts=(gp,),
                 b_view=(D_MODEL, chip_block, lambda j, k: (k, 1, 0)))
    pp = _mm("ple_proj", ps, W["w_ple_proj"], "nn", 1024,1024, 256, [F32])

    def head(gtt, h2t, ppt, tg, g):
        sg = _sigmoid(gtt)
        h3 = h2t + sg * ppt
        yo = h3 * _rms(h3) * g
        diff = yo - tg
        loss = 0.5 * jnp.sum(jnp.mean(jnp.square(diff), axis=-1, keepdims=True), axis=0, keepdims=True)
        dh3, dg = _rms_bwd(diff * (1.0 / D_MODEL), h3, g)
        return dh3, dh3 * ppt * sg * (1.0 - sg), dh3 * sg, jnp.broadcast_to(loss, (1, 128)), dg

    dh3, d_gt, d_pp, loss_part, dg_final = _mm(
        "ple_gate_loss_head", en, W["w_ple_gate"], "nn", 512, 1024, 1024, [F32, BF, BF], epilogue=head,
        extras=(h2, pp, tgt), consts=(gf,), sums=[128, D_MODEL])

    early_shapes = {n: shards[n].shape for n in REDUCE_EARLY}
    early_rows = sum(r * c for r, c in early_shapes.values()) // PACK_W
    shard_rows = D_MODEL // N_CHIPS
    early_buf = _mm("g_ple_gate", en, d_gt, "tn", 1024, 1024, 1024, [F32],
                    into=(jax.ShapeDtypeStruct((N_CHIPS, early_rows, PACK_W), F32), (N_CHIPS, shard_rows, PACK_W),
                          lambda i, j: (0, 2 * D_MODEL // shard_rows, 0)))
    g_ple_proj = _mm("g_ple_proj", ps, d_pp, "tn", 256, 1024, 1024,[F32])

    def add_norm_bwd(dn, dh_out, h, g):
        dh, dg = _rms_bwd(dn, h, g)
        dh = dh_out + dh
        return dh, dh, dg

    dh2, dh2_b, dg_ple = _mm("d_ple_gate", d_gt, W["w_ple_gate"], "nt", 512, 1024, 1024, [F32, BF],
                             epilogue=add_norm_bwd, extras=(dh3, h2), consts=(gp,), sums=[D_MODEL])
    d_up = _mm("d_mlp_down", dh2_b, mlp_all, "nt", 1024,1024, 1024, [BF], b_view=(D_FF, chip_block, lambda j, k: (j, 1, 0)),
               epilogue=lambda acc, u: (acc * (2.0 * jnp.maximum(u.astype(F32), 0.0)),), extras=(up,))
    early_buf = _mm("g_mlp_down", act, dh2_b, "tn", 1024, 1024, 1024,[F32],
                    into=(early_buf, (None, D_MODEL, PACK_W), lambda i, j: (i, 1, 0)))
    early_buf = _mm("g_mlp_up", cn, d_up, "tn", 1024, 1024, 1024,[F32],
                    into=(early_buf, (None, D_MODEL, PACK_W), lambda i, j: (j, 0, 0)))
    dh1, dh1_b, dg_mlp = _mm("d_mlp_up", d_up, mlp_all, "nt", 1024, 1024, 1024, [F32, BF], epilogue=add_norm_bwd,
                             b_view=(D_MODEL, chip_block, lambda j, k: (k, 0, 0)),
                             extras=(dh2, h1), consts=(gl,), sums=[D_MODEL])
    d_mixed = _mm("d_out_proj", dh1_b, W["w_out"], "nt", 1024,1024, 1024, [F32])
    early_buf = _mm("g_out_proj", mixed, dh1_b, "tn", 1024, 1024, 1024, [F32],
                    into=(early_buf, (N_CHIPS, shard_rows, PACK_W), lambda i, j: (0, 2 * D_MODEL // shard_rows + 1, 0)))

    def gate_bwd(dm, gn, gd, un, ud):
        gn, gd, un, ud = (t.astype(F32) for t in (gn, gd, un, ud))
        sn, sd = _sigmoid(gn), _sigmoid(gd)
        return jnp.concatenate([dm * un * sn * (1.0 - sn), dm * ud * sd * (1.0 - sd)], axis=1), dm * sn, dm * sd

    dz_gates, d_u_na, d_u_dil = _rowwise(
        "gate_mix_bwd", gate_bwd, T, TM,
        [_row(d_mixed, TM), _row(z_gates, TM, 0, D_MODEL), _row(z_gates, TM, 1, D_MODEL), _row(u_na, TM), _row(u_dil, TM)],
        [(2 * D_MODEL, BF), (D_MODEL, BF), (D_MODEL, BF)])
    g_branch_na = _mm("g_branch_na", y_na, d_u_na, "tn", 1024, 1024, 1024,[F32])
    g_branch_dil = _mm("g_branch_dil", y_dil, d_u_dil, "tn", 256, 1024, 1024,[F32])
    small_rows = [jnp.concatenate([_pack_rows(g[:, s * shard_rows:(s + 1) * shard_rows]) for g in (g_ple_proj, g_branch_na, g_branch_dil)],
                                  axis=0) for s in range(N_CHIPS)]
    early_buf = lax.dynamic_update_slice(early_buf, jnp.stack(small_rows), (0, 2 * D_MODEL + 2 * shard_rows, 0))
    early_tm = early_rows // 4
    swap_flight = _swap_start(early_buf, "early")
    d_y_na = _mm("d_branch_na", d_u_na, W["w_branch_na"], "nt", 1024,512, 1024, [BF], after=(swap_flight.token,))
    d_y_dil = _mm("d_branch_dil", d_u_dil, W["w_branch_dil"], "nt", 1024,256, 1024, [F32])

    dqa, dka, dva, dtab = _na_bwd(qkv, tab, d_y_na)
    early_g, early_got = _swap_wait(swap_flight, dqa, "early")
    early_pair, early_pair_b = _pair_sum(early_g, early_got, early_tm)
    scatter_flight = _scatter_start(early_pair_b, "early")
    d_rpb = _na_rpb_grad(dtab)[:, :2 * NA_WIN_ROWS - 1, :2 * NA_WIN_COLS - 1]

    do_res, dlse_res = _dil_merge_bwd(d_y_dil, o_nat, w_grp, TM, after=(scatter_flight.token,))
    d_dil = [_band_bwd(*dil_ops[g], do_res[g], dlse_res[g], g) for g in range(len(DIL_GROUPS))]

    dz_qkv = _qkv_unprep((dqa, dka, dva), d_dil, cos2, sin_signed, TM)
    g_in_parts = [_mm("g_in_qkv", a, dz_qkv, "tn", 1024, 1280, 1024,[F32]), _mm("g_in_gates", a, dz_gates, "tn", 1024, 1024, 1024,[F32])]
    early_mine = _chip_sum(early_pair, _scatter_wait(scatter_flight, g_in_parts[1], "early"), early_tm)
    join_flight = _join_start(early_mine)
    in_cols = shards["w_in"].shape[1]

    def owner_columns(s):
        lo, hi, split = s * in_cols, (s + 1) * in_cols, g_in_parts[0].shape[1]
        pieces = [g_in_parts[0][:, lo:min(hi, split)]] if lo < split else []
        pieces += [g_in_parts[1][:, max(lo, split) - split:hi - split]] if hi > split else []
        return pieces[0] if len(pieces) == 1 else jnp.concatenate(pieces, axis=1)

    late_tm = in_cols // 4
    late_swap = _swap_start(jnp.stack([owner_columns(s).T for s in range(N_CHIPS)]), "late")
    d_a = _mm("d_in_qkv", dz_qkv, W["w_in"], "nt", 1024,1024, 1280, [F32], after=(late_swap.token, join_flight.token),
              b_view=(D_MODEL, (D_MODEL, 1280), lambda j, k: (j, k)))
    late_g, late_got = _swap_wait(late_swap, d_a, "late")
    late_pair, late_pair_b = _pair_sum(late_g, late_got, late_tm)
    late_scatter = _scatter_start(late_pair_b, "late")
    def first_bwd(dn_gates, dn_qkv, dh_out, h, g):
        dh, dg = _rms_bwd(dn_gates + dn_qkv, h, g)
        return dh_out + dh, dg

    grad_x, dg_mix = _mm("d_in_gates", dz_gates, w_gates, "nt", 512, 1024, 1024, [F32], epilogue=first_bwd,
                         extras=(d_a, dh1, xs), consts=(gm,), sums=[D_MODEL], after=(late_scatter.token,))
    early_shard = _join_wait(join_flight, grad_x)

    n_rpb = rpb.size
    rpb_rows = 4
    small = jnp.concatenate([
        dg_mix, dg_mlp, dg_ple, dg_final,
        jnp.pad(d_rpb.reshape(-1), (0, rpb_rows * D_MODEL - n_rpb)).reshape(rpb_rows, D_MODEL),
        jnp.pad(loss_part, ((0, 0), (0, D_MODEL - loss_part.shape[1]))),
        jnp.zeros((SMALL_ROWS - 5 - rpb_rows, D_MODEL), F32)], axis=0)
    out = {"grad": {}, "delta": {}, "new_m": {}, "new_v": {}}

    def update(n, g, row0):
        res = _adamw("adamw_" + n, g, row0, params[n], params["m_" + n], params["v_" + n])
        for kind, t in zip(("grad", "delta", "new_m", "new_v"), res, strict=True):
            out[kind][n] = t

    row0 = 0
    for n in REDUCE_EARLY:
        rows, cols = early_shapes[n]
        n_rows = rows * cols // PACK_W
        if cols == PACK_W:
            update(n, early_shard, row0)
        else:
            update(n, early_shard[row0:row0 + n_rows].reshape(rows, cols), 0)
        row0 += n_rows
    late_others = _scatter_wait(late_scatter, out["new_v"][REDUCE_EARLY[-1]], "late")
    late_mine = _chip_sum(late_pair, late_others, late_tm)
    small = _allreduce_small(small, after=(late_mine,))
    res = _adamw("adamw_w_in", _join_halves(late_mine), 0, *[jnp.swapaxes(params[n], 1, 2) for n in ("w_in", "m_w_in", "v_w_in")])
    for kind, t in zip(("grad", "delta", "new_m", "new_v"), res, strict=True):
        out[kind]["w_in"] = jnp.swapaxes(t, 1, 2)
    loss = small[4 + rpb_rows, 0]

    def small_pack(a0, a1, a2, a3, r):
        return jnp.concatenate([a0.reshape(1, -1), a1.reshape(1, -1), a2.reshape(1, -1), a3.reshape(1, -1),
                                jnp.pad(r.reshape(-1), (0, rpb_rows * D_MODEL - n_rpb)).reshape(rpb_rows, D_MODEL)], axis=0)

    small_res = _adamw("adamw_small", small, 0, small_pack(g_mix, g_mlp, g_ple, g_final, rpb)[None],
                       small_pack(m_g_mix, m_g_mlp, m_g_ple, m_g_final, m_rpb)[None],
                       small_pack(v_g_mix, v_g_mlp, v_g_ple, v_g_final, v_rpb)[None])

    def small_unpack(t):
        return {"g_mix": t[0].reshape(g_mix.shape), "g_mlp": t[1].reshape(g_mlp.shape), "g_ple": t[2].reshape(g_ple.shape),
                "g_final": t[3].reshape(g_final.shape), "rpb": t[4:].reshape(-1)[:n_rpb].reshape(rpb.shape)}

    for kind, t in zip(("grad", "delta", "new_m", "new_v"), small_res, strict=True):
        out[kind].update(small_unpack(t[0]))

    order = ["g_mix", "w_in", "rpb", "w_branch_na", "w_branch_dil", "w_out", "g_mlp", "w_up", "w_down", "g_ple",
             "w_ple_gate", "w_ple_proj", "g_final"]
    return (loss, grad_x[None], *[out["grad"][n] for n in order], *[out["delta"][n] for n in order],
            *[out["new_m"][n] for n in order], *[out["new_v"][n] for n in order])
```

```python
import functools
from typing import NamedTuple

import jax
import jax.numpy as jnp
from jax import lax
from jax.experimental import pallas as pl
from jax.experimental.pallas import tpu as pltpu

BF = jnp.bfloat16
F32 = jnp.float32
MESH = pl.DeviceIdType.MESH
ANY = pl.BlockSpec(memory_space=pl.ANY)

V7X_VMEM_BYTES = 64 * 1024 * 1024
VMEM_LIMIT = V7X_VMEM_BYTES - 16 * 1024 * 1024

D_MODEL = 1024
HEAD_DIM = 64
GRID_W = 64
NA_HEADS = 8
NA_WIN_ROWS = 8
NA_WIN_COLS = 16
NA_WIDTH = NA_HEADS * HEAD_DIM
DIL_GROUPS = ((128, 1), (512, 4), (2048, 16))
DIL_HPG = 4
DIL_HEADS = DIL_HPG * len(DIL_GROUPS)
DIL_WIDTH = DIL_HEADS * HEAD_DIM
DIL_OUT_WIDTH = DIL_HPG * HEAD_DIM
DIL_RADIUS = 64
QKV_WIDTH = 3 * NA_WIDTH + 3 * DIL_WIDTH
D_FF = 4 * D_MODEL
PLE_DIM = 256
ROPE_THETA = 10000.0
RMS_EPS = 1e-6
NEG_INF = -1e30
Q_SCALE = HEAD_DIM ** -0.5

ADAM_LR = 0.001
ADAM_B1 = 0.9
ADAM_B2 = 0.999
ADAM_EPS = 1e-08
ADAM_WD = 0.01
ADAM_STEP = 10

N_CHIPS = 4
N_DEV = 8
PACK_W = 1024
BIG = ("w_in", "w_branch_na", "w_branch_dil", "w_out", "w_up", "w_down", "w_ple_gate", "w_ple_proj")
GATHER_MIX = ("w_branch_na", "w_branch_dil", "w_out")
GATHER_MLP = ("w_up", "w_down", "w_ple_gate", "w_ple_proj")
REDUCE_EARLY = ("w_up", "w_down", "w_ple_gate", "w_out", "w_ple_proj", "w_branch_na", "w_branch_dil")
SMALL_ROWS = 16


def _cparams(sem=None):
    return pltpu.CompilerParams(dimension_semantics=sem, vmem_limit_bytes=VMEM_LIMIT)


def _mm(name, a, b, mode, tm, tn, tk, out_dtypes, epilogue=None, extras=(), consts=(), sums=(), after=(), into=None,
        b_view=None):
    if mode == "nn":
        (M, K), N = a.shape, b.shape[1]
    elif mode == "nt":
        (M, K), N = a.shape, b.shape[0]
    else:
        (K, M), N = a.shape, b.shape[1]
    if b_view is not None:
        N = b_view[0]
    tm, tn, tk = min(tm, M), min(tn, N), min(tk, K)
    assert M % tm == 0 and N % tn == 0 and K % tk == 0, (name, M, N, K, tm, tn, tk)
    if mode == "nn":
        a_spec = pl.BlockSpec((tm, tk), lambda i, j, k: (i, k))
        b_spec = pl.BlockSpec((tk, tn), lambda i, j, k: (k, j))
        dims = (((1,), (0,)), ((), ()))
    elif mode == "nt":
        a_spec = pl.BlockSpec((tm, tk), lambda i, j, k: (i, k))
        b_spec = pl.BlockSpec((tn, tk), lambda i, j, k: (j, k))
        dims = (((1,), (1,)), ((), ()))
    else:
        a_spec = pl.BlockSpec((tk, tm), lambda i, j, k: (k, i))
        b_spec = pl.BlockSpec((tk, tn), lambda i, j, k: (k, j))
        dims = (((0,), (0,)), ((), ()))
    if b_view is not None:
        b_spec = pl.BlockSpec(b_view[1], lambda i, j, k: b_view[2](j, k))
    nk = K // tk
    n_extra, n_const, n_out, n_sum = len(extras), len(consts), len(out_dtypes), len(sums)
    tile = pl.BlockSpec((tm, tn), lambda i, j, k: (i, j))
    assert not sums or tn == N, "row sums need whole rows in a tile"

    n_after = len(after)

    def body(a_ref, b_ref, *rest):
        extra_refs, rest = rest[:n_extra + n_const], rest[n_extra + n_const + n_after:]
        out_refs, sum_refs, acc = rest[:n_out], rest[n_out:n_out + n_sum], rest[-1]
        i, k = pl.program_id(0), pl.program_id(2)
        def product():
            return lax.dot_general(a_ref[...].astype(BF), b_ref[...].astype(BF), dims, preferred_element_type=F32)

        if nk > 1:
            @pl.when(k == 0)
            def _():
                acc[...] = jnp.zeros_like(acc)

            acc[...] += product()

        @pl.when(k == nk - 1)
        def _():
            total = product() if nk == 1 else acc[...]
            outs = (total,) if epilogue is None else epilogue(total, *[e[...] for e in extra_refs])
            for o_ref, val in zip(out_refs, outs[:n_out], strict=True):
                o_ref[...] = val.astype(o_ref.dtype).reshape(o_ref.shape)
            for s_ref, val in zip(sum_refs, outs[n_out:], strict=True):
                @pl.when(i == 0)
                def _():
                    s_ref[...] = val

                @pl.when(i != 0)
                def _():
                    s_ref[...] += val

    out_specs = [tile] * n_out + [pl.BlockSpec((1, c), lambda i, j, k: (0, 0)) for c in sums]
    out_shape = [jax.ShapeDtypeStruct((M, N), dt) for dt in out_dtypes] + [jax.ShapeDtypeStruct((1, c), F32) for c in sums]
    operands, aliases = [a, b, *extras, *consts, *after], {}
    in_specs = ([a_spec, b_spec] + [tile] * n_extra
                + [pl.BlockSpec(c.shape, functools.partial(lambda nd, i, j, k: (0,) * nd, c.ndim)) for c in consts] + [ANY] * n_after)
    if into is not None:
        assert n_out == 1
        target, block, index = into
        out_specs = [pl.BlockSpec(block, lambda i, j, k: index(i, j))]
        out_shape = [jax.ShapeDtypeStruct(target.shape, target.dtype)]
        if not isinstance(target, jax.ShapeDtypeStruct):
            aliases = {len(operands): 0}
            operands.append(target)
            in_specs.append(ANY)
            n_after += 1

    outs = pl.pallas_call(
        body, name=name, grid=(M // tm, N // tn, nk),
        in_specs=in_specs, out_specs=out_specs, out_shape=out_shape,
        scratch_shapes=[pltpu.VMEM((tm, tn) if nk > 1 else (8, 128), F32)], input_output_aliases=aliases,
        compiler_params=_cparams(("arbitrary",) * 3 if sums else ("parallel", "parallel", "arbitrary")),
    )(*operands)
    return outs[0] if len(outs) == 1 else outs


def _row(arr, tm, col_block=None, width=None):
    width = arr.shape[1] if width is None else width
    cb = 0 if col_block is None else col_block
    return arr, pl.BlockSpec((tm, width), lambda i: (i, cb))


def _full(arr):
    nd = arr.ndim
    return arr, pl.BlockSpec(arr.shape, lambda i: (0,) * nd)


def _rowwise(name, body, T, tm, ins, outs, sums=(), after=()):
    n_in, n_out, n_sum, n_after = len(ins), len(outs), len(sums), len(after)

    def kern(*refs):
        in_refs, refs = refs[:n_in], refs[n_in + n_after:]
        out_refs, sum_refs = refs[:n_out], refs[n_out:]
        res = body(*[r[...] for r in in_refs])
        res = res if isinstance(res, tuple) else (res,)
        for o_ref, val in zip(out_refs, res[:n_out], strict=True):
            o_ref[...] = val.astype(o_ref.dtype)
        if n_sum:
            @pl.when(pl.program_id(0) == 0)
            def _():
                for s_ref in sum_refs:
                    s_ref[...] = jnp.zeros_like(s_ref)

            for s_ref, val in zip(sum_refs, res[n_out:], strict=True):
                s_ref[...] += val

    res = pl.pallas_call(
        kern, name=name, grid=(T // tm,),
        in_specs=[spec for _, spec in ins] + [ANY] * n_after,
        out_specs=[pl.BlockSpec((tm, c), lambda i: (i, 0)) for c, _ in outs]
        + [pl.BlockSpec((1, c), lambda i: (0, 0)) for c in sums],
        out_shape=[jax.ShapeDtypeStruct((T, c), dt) for c, dt in outs]
        + [jax.ShapeDtypeStruct((1, c), F32) for c in sums],
        compiler_params=_cparams(("arbitrary",)),
    )(*[a for a, _ in ins], *after)
    return res[0] if len(res) == 1 else res


def _sigmoid(x):
    return 1.0 / (1.0 + jnp.exp(-x))


def _rms(h):
    return lax.rsqrt(jnp.mean(h * h, axis=-1, keepdims=True) + RMS_EPS)


def _rms_bwd(dy, h, g):
    r = _rms(h)
    n = h * r
    dn = dy * g
    dh = r * (dn - n * jnp.mean(dn * n, axis=-1, keepdims=True))
    return dh, jnp.sum(dy * n, axis=0, keepdims=True)


def _rope(x, cos2, sin_signed):
    lane = lax.broadcasted_iota(jnp.int32, x.shape, 1)
    swapped = jnp.where((lane % HEAD_DIM) < HEAD_DIM // 2, pltpu.roll(x, 128 - HEAD_DIM // 2, 1), pltpu.roll(x, HEAD_DIM // 2, 1))
    return x * cos2 + swapped * sin_signed


NA_KEYS = NA_WIN_ROWS * GRID_W
NA_BASES = 8


def _na_row_geometry(r, rows):
    first = jnp.clip(r - NA_WIN_ROWS // 2, 0, rows - NA_WIN_ROWS)
    base = first - r + (NA_WIN_ROWS - 1)
    return pl.multiple_of(first * GRID_W, GRID_W), base


NA_ROWS_PER_STEP = 16
NA_BWD_ROWS_PER_STEP = 8


def _softmax_rows(s):
    p = jnp.exp(s - jnp.max(s, axis=-1, keepdims=True))
    return p / jnp.sum(p, axis=-1, keepdims=True)


def _na_probs(q, kw, bias):
    return _softmax_rows(lax.dot_general(q, kw, (((1,), (1,)), ((), ())), preferred_element_type=F32) + bias)


def _split_pair(t):
    first = lax.broadcasted_iota(jnp.int32, t.shape, 1) < HEAD_DIM
    zero = jnp.zeros_like(t)
    return jnp.where(first, t, zero), jnp.where(first, zero, t)


def _join_pair(a, b):
    return jnp.where(lax.broadcasted_iota(jnp.int32, a.shape, 1) < HEAD_DIM, a, b)


_NT = (((1,), (1,)), ((), ()))
_TN = (((0,), (0,)), ((), ()))


def _na_fwd(qkv, tab):
    T = qkv.shape[0]
    rows = T // GRID_W
    n_pairs = NA_WIDTH // 128

    def body(q_ref, k_ref, v_ref, tab_ref, y_ref):
        def step(it, carry):
            geo = [_na_row_geometry(it * NA_ROWS_PER_STEP + u, rows) for u in range(NA_ROWS_PER_STEP)]
            q0s = [pl.multiple_of((it * NA_ROWS_PER_STEP + u) * GRID_W, GRID_W) for u in range(NA_ROWS_PER_STEP)]
            ss = [lax.dot_general(jnp.concatenate(_split_pair(q_ref[pl.ds(q0, GRID_W), :] * Q_SCALE), axis=0),
                                  k_ref[pl.ds(k0, NA_KEYS), :], _NT, preferred_element_type=F32)
                  for q0, (k0, _) in zip(q0s, geo)]
            ps = [_softmax_rows(s + jnp.concatenate([tab_ref[0, base], tab_ref[1, base]], axis=0)) for s, (_, base) in zip(ss, geo)]
            ys = [jnp.dot(p.astype(BF), v_ref[pl.ds(k0, NA_KEYS), :], preferred_element_type=F32) for p, (k0, _) in zip(ps, geo)]
            for q0, y2 in zip(q0s, ys):
                y_ref[pl.ds(q0, GRID_W), :] = _join_pair(y2[:GRID_W], y2[GRID_W:]).astype(y_ref.dtype)
            return carry

        lax.fori_loop(0, rows // NA_ROWS_PER_STEP, step, 0)

    def cols(first):
        return pl.BlockSpec((T, 128), lambda j: (0, first + j))

    return pl.pallas_call(
        body, name="na_fwd", grid=(n_pairs,),
        in_specs=[cols(0), cols(n_pairs), cols(2 * n_pairs), pl.BlockSpec((2, NA_BASES, GRID_W, NA_KEYS), lambda j: (j, 0, 0, 0))],
        out_specs=cols(0), out_shape=jax.ShapeDtypeStruct((T, NA_WIDTH), BF),
        compiler_params=_cparams(("parallel",)),
    )(qkv, qkv, qkv, tab)


def _na_bwd(qkv, tab, do):
    T = qkv.shape[0]
    rows = T // GRID_W
    n_pairs = NA_WIDTH // 128

    def body(q_ref, k_ref, v_ref, tab_ref, do_ref, dq_ref, dk_ref, dv_ref, dtab_ref):
        dk_ref[...] = jnp.zeros_like(dk_ref)
        dv_ref[...] = jnp.zeros_like(dv_ref)
        dtab_ref[...] = jnp.zeros_like(dtab_ref)

        def step(it, carry):
            U = NA_BWD_ROWS_PER_STEP
            geo = [_na_row_geometry(it * U + u, rows) for u in range(U)]
            q0s = [pl.multiple_of((it * U + u) * GRID_W, GRID_W) for u in range(U)]
            q2s = [jnp.concatenate(_split_pair(q_ref[pl.ds(q0, GRID_W), :] * Q_SCALE), axis=0) for q0 in q0s]
            do2s = [jnp.concatenate(_split_pair(do_ref[pl.ds(q0, GRID_W), :]), axis=0) for q0 in q0s]
            ss = [lax.dot_general(q2, k_ref[pl.ds(k0, NA_KEYS), :], _NT, preferred_element_type=F32) for q2, (k0, _) in zip(q2s, geo)]
            dps = [lax.dot_general(do2, v_ref[pl.ds(k0, NA_KEYS), :], _NT, preferred_element_type=F32) for do2, (k0, _) in zip(do2s, geo)]
            ps = [_softmax_rows(s + jnp.concatenate([tab_ref[0, base], tab_ref[1, base]], axis=0)) for s, (_, base) in zip(ss, geo)]
            dss = [p * (dp - jnp.sum(dp * p, axis=-1, keepdims=True)) for p, dp in zip(ps, dps)]
            dvs = [lax.dot_general(p.astype(BF), do2, _TN, preferred_element_type=F32) for p, do2 in zip(ps, do2s)]
            dsbs = [ds.astype(BF) for ds in dss]
            dqs = [jnp.dot(dsb, k_ref[pl.ds(k0, NA_KEYS), :], preferred_element_type=F32) for dsb, (k0, _) in zip(dsbs, geo)]
            dks = [lax.dot_general(dsb, q2, _TN, preferred_element_type=F32) for dsb, q2 in zip(dsbs, q2s)]
            for u in range(U):
                k0, base = geo[u]
                dtab_ref[0, base] += dss[u][:GRID_W]
                dtab_ref[1, base] += dss[u][GRID_W:]
                dq_ref[pl.ds(q0s[u], GRID_W), :] = _join_pair(dqs[u][:GRID_W], dqs[u][GRID_W:])
                dk_ref[pl.ds(k0, NA_KEYS), :] += dks[u]
                dv_ref[pl.ds(k0, NA_KEYS), :] += dvs[u]
            return carry

        lax.fori_loop(0, rows // NA_BWD_ROWS_PER_STEP, step, 0)

    def cols(first):
        return pl.BlockSpec((T, 128), lambda j: (0, first + j))

    tabs = pl.BlockSpec((2, NA_BASES, GRID_W, NA_KEYS), lambda j: (j, 0, 0, 0))
    wide = jax.ShapeDtypeStruct((T, NA_WIDTH), F32)
    return pl.pallas_call(
        body, name="na_bwd", grid=(n_pairs,),
        in_specs=[cols(0), cols(n_pairs), cols(2 * n_pairs), tabs, cols(0)],
        out_specs=[cols(0), cols(0), cols(0), tabs],
        out_shape=[wide, wide, wide, jax.ShapeDtypeStruct((NA_HEADS, NA_BASES, GRID_W, NA_KEYS), F32)],
        compiler_params=_cparams(("parallel",)),
    )(qkv, qkv, qkv, tab, do)


def _na_bias_table(rpb):
    H, n_rows, n_cols = rpb.shape

    def body(r_ref, tab_ref):
        q = lax.broadcasted_iota(jnp.int32, (GRID_W, 128), 0)
        kc = lax.broadcasted_iota(jnp.int32, (GRID_W, 128), 1)
        first = jnp.clip(q - NA_WIN_COLS // 2, 0, GRID_W - NA_WIN_COLS)
        valid = (kc >= first) & (kc < first + NA_WIN_COLS)
        toeplitz = []
        for ro in range(n_rows):
            row = jnp.broadcast_to(r_ref[pl.ds(ro, 1), :], (GRID_W, 128))
            shifted = pltpu.roll(pltpu.roll(row, 128 - (NA_WIN_COLS - 1), 1), 0, 1, stride=1, stride_axis=0)
            toeplitz.append(jnp.where(valid, shifted, NEG_INF))
        for base in range(NA_BASES):
            for j in range(NA_WIN_ROWS // 2):
                even, odd = toeplitz[base + 2 * j], toeplitz[base + 2 * j + 1]
                tab_ref[base, :, pl.ds(j * 128, 128)] = jnp.where(kc < GRID_W, even, pltpu.roll(odd, GRID_W, 1))

    padded = jnp.pad(rpb, ((0, 0), (0, 16 - n_rows), (0, 128 - n_cols)))
    return pl.pallas_call(
        body, name="na_bias_table", grid=(H,),
        in_specs=[pl.BlockSpec((None, 16, 128), lambda h: (h, 0, 0))],
        out_specs=pl.BlockSpec((None, NA_BASES, GRID_W, NA_KEYS), lambda h: (h, 0, 0, 0)),
        out_shape=jax.ShapeDtypeStruct((H, NA_BASES, GRID_W, NA_KEYS), F32),
        compiler_params=_cparams(("parallel",)),
    )(padded)


def _na_rpb_grad(dtab):
    H = dtab.shape[0]
    n_rows = 2 * NA_WIN_ROWS - 1
    n_cols = 2 * NA_WIN_COLS - 1

    def body(d_ref, o_ref):
        lane = lax.broadcasted_iota(jnp.int32, (GRID_W, 128), 1)
        low = lane < GRID_W
        out_rows = []
        for ro in range(n_rows):
            acc = jnp.zeros((GRID_W, 128), F32)
            for base in range(NA_BASES):
                i = ro - base
                if not 0 <= i < NA_WIN_ROWS:
                    continue
                pair = d_ref[base, :, pl.ds((i // 2) * 128, 128)]
                if i % 2:
                    pair = pltpu.roll(pair, GRID_W, 1)
                acc = acc + jnp.where(low, pair, 0.0)
            skew = pltpu.roll(acc, 0, 1, stride=1, stride_axis=0)
            diag = jnp.sum(skew, axis=0, keepdims=True)
            out_rows.append(pltpu.roll(jnp.broadcast_to(diag, (8, 128)), 128 - (GRID_W - NA_WIN_COLS), 1)[:1])
        out_rows.append(jnp.zeros((1, 128), F32))
        res = jnp.concatenate(out_rows, axis=0)
        o_ref[...] = jnp.where(lax.broadcasted_iota(jnp.int32, res.shape, 1) < n_cols, res, 0.0)

    return pl.pallas_call(
        body, name="na_rpb_grad", grid=(H,),
        in_specs=[pl.BlockSpec((None, NA_BASES, GRID_W, NA_KEYS), lambda h: (h, 0, 0, 0))],
        out_specs=pl.BlockSpec((None, n_rows + 1, 128), lambda h: (h, 0, 0)),
        out_shape=jax.ShapeDtypeStruct((H, n_rows + 1, 128), F32),
        compiler_params=_cparams(("parallel",)),
    )(jnp.flip(dtab, axis=2))


BAND_Q = 128
BAND_KEYS = BAND_Q + 2 * DIL_RADIUS


def _band_geometry(n, L):
    q0 = pl.multiple_of(n * BAND_Q, BAND_Q)
    k0 = pl.multiple_of(jnp.clip(q0 - DIL_RADIUS, 0, L - BAND_KEYS), DIL_RADIUS)
    qi = q0 + lax.broadcasted_iota(jnp.int32, (BAND_Q, BAND_KEYS), 0)
    kj = k0 + lax.broadcasted_iota(jnp.int32, (BAND_Q, BAND_KEYS), 1)
    return q0, k0, jnp.abs(qi - kj) <= DIL_RADIUS


DIL_PAIRS = DIL_OUT_WIDTH // 128


def _residue_shape(dil, T, dtype):
    return jax.ShapeDtypeStruct((DIL_PAIRS, dil, T // dil, 128), dtype)


def _residue_tile(dil, tm):
    return pl.BlockSpec((DIL_PAIRS, dil, tm // dil, 128), lambda i: (0, 0, i, 0))


def _to_natural(ref, scratch, dil, tm):
    tiles = []
    for pair in range(DIL_PAIRS):
        if dil == 1:
            tiles.append(ref[pair, 0].astype(F32))
            continue
        for r in range(dil):
            scratch[pl.ds(r, tm // dil, stride=dil), :] = ref[pair, r].astype(F32)
        tiles.append(scratch[...])
    return tiles


def _from_natural(tile, scratch, ref, pair, dil, tm):
    if dil == 1:
        ref[pair, 0] = tile.astype(ref.dtype)
        return
    scratch[...] = tile
    for r in range(dil):
        ref[pair, r] = scratch[pl.ds(r, tm // dil, stride=dil), :].astype(ref.dtype)


def _band_specs(group, T):
    dil = DIL_GROUPS[group][1]
    L = T // dil
    assert L % BAND_Q == 0 and L >= BAND_KEYS, (T, dil)
    per_residue = min(BAND_BLOCKS_PER_STEP, L // BAND_Q)
    residues = min(dil, BAND_BLOCKS_PER_STEP // per_residue)
    spec = pl.BlockSpec((None, residues, L, 128), lambda s: (s % DIL_PAIRS, s // DIL_PAIRS, 0, 0))
    return L, residues, per_residue, (dil // residues * DIL_PAIRS,), spec


BAND_BLOCKS_PER_STEP = 8


def _band_softmax(s, valid):
    s = jnp.where(valid, s, NEG_INF)
    m = jnp.max(s, axis=-1, keepdims=True)
    p = jnp.exp(s - m)
    l = jnp.sum(p, axis=-1, keepdims=True)
    return p / l, m + jnp.log(l)


def _band_fwd(q, k, v, group):
    T = q.shape[1] * q.shape[2]
    L, residues, U, grid, spec = _band_specs(group, T)

    def body(q_ref, k_ref, v_ref, o_ref, lse_ref):
        def step(it, carry):
            geo = [(r, *_band_geometry(it * U + u, L)) for r in range(residues) for u in range(U)]
            ss = [lax.dot_general(jnp.concatenate(_split_pair(q_ref[r, pl.ds(q0, BAND_Q), :]), axis=0),
                                  k_ref[r, pl.ds(k0, BAND_KEYS), :], _NT, preferred_element_type=F32) for r, q0, k0, _ in geo]
            pls = [_band_softmax(s, jnp.concatenate([valid, valid], axis=0)) for s, (_, _, _, valid) in zip(ss, geo)]
            os = [jnp.dot(p.astype(BF), v_ref[r, pl.ds(k0, BAND_KEYS), :], preferred_element_type=F32)
                  for (p, _), (r, _, k0, _) in zip(pls, geo)]
            for (r, q0, _, _), o2, (_, lse) in zip(geo, os, pls):
                o_ref[r, pl.ds(q0, BAND_Q), :] = _join_pair(o2[:BAND_Q], o2[BAND_Q:])
                lse2 = jnp.broadcast_to(lse, (2 * BAND_Q, 128))
                lse_ref[r, pl.ds(q0, BAND_Q), :] = _join_pair(lse2[:BAND_Q], lse2[BAND_Q:])
            return carry

        lax.fori_loop(0, L // (BAND_Q * U), step, 0)

    res = _residue_shape(DIL_GROUPS[group][1], T, F32)
    return pl.pallas_call(
        body, name=f"band_fwd_g{group}", grid=grid,
        in_specs=[spec] * 3, out_specs=[spec] * 2, out_shape=[res, res],
        compiler_params=_cparams(("parallel",)),
    )(q, k, v)


def _band_bwd(q, k, v, do, dlse, group):
    T = q.shape[1] * q.shape[2]
    L, residues, U, grid, spec = _band_specs(group, T)

    def body(q_ref, k_ref, v_ref, do_ref, dlse_ref, dq_ref, dk_ref, dv_ref):
        dk_ref[...] = jnp.zeros_like(dk_ref)
        dv_ref[...] = jnp.zeros_like(dv_ref)

        def step(it, carry):
            geo = [(r, *_band_geometry(it * U + u, L)) for r in range(residues) for u in range(U)]
            q2s = [jnp.concatenate(_split_pair(q_ref[r, pl.ds(q0, BAND_Q), :]), axis=0) for r, q0, _, _ in geo]
            do2s = [jnp.concatenate(_split_pair(do_ref[r, pl.ds(q0, BAND_Q), :]), axis=0) for r, q0, _, _ in geo]
            ss = [lax.dot_general(q2, k_ref[r, pl.ds(k0, BAND_KEYS), :], _NT, preferred_element_type=F32)
                  for q2, (r, _, k0, _) in zip(q2s, geo)]
            dps = [lax.dot_general(do2, v_ref[r, pl.ds(k0, BAND_KEYS), :], _NT, preferred_element_type=F32)
                   for do2, (r, _, k0, _) in zip(do2s, geo)]
            ps = [_band_softmax(s, jnp.concatenate([valid, valid], axis=0))[0] for s, (_, _, _, valid) in zip(ss, geo)]
            dss = []
            for p, dp, (r, q0, _, _) in zip(ps, dps, geo):
                dl = dlse_ref[r, pl.ds(q0, BAND_Q), :]
                dl2 = jnp.concatenate([dl[:, :1], dl[:, HEAD_DIM:HEAD_DIM + 1]], axis=0)
                dss.append(p * (dp - jnp.sum(dp * p, axis=-1, keepdims=True) + dl2))
            dvs = [lax.dot_general(p.astype(BF), do2, _TN, preferred_element_type=F32) for p, do2 in zip(ps, do2s)]
            dsbs = [ds.astype(BF) for ds in dss]
            dqs = [jnp.dot(dsb, k_ref[r, pl.ds(k0, BAND_KEYS), :], preferred_element_type=F32) for dsb, (r, _, k0, _) in zip(dsbs, geo)]
            dks = [lax.dot_general(dsb, q2, _TN, preferred_element_type=F32) for dsb, q2 in zip(dsbs, q2s)]
            for u, (r, q0, k0, _) in enumerate(geo):
                dq_ref[r, pl.ds(q0, BAND_Q), :] = _join_pair(dqs[u][:BAND_Q], dqs[u][BAND_Q:])
                dk_ref[r, pl.ds(k0, BAND_KEYS), :] += dks[u]
                dv_ref[r, pl.ds(k0, BAND_KEYS), :] += dvs[u]
            return carry

        lax.fori_loop(0, L // (BAND_Q * U), step, 0)

    res = _residue_shape(DIL_GROUPS[group][1], T, F32)
    return pl.pallas_call(
        body, name=f"band_bwd_g{group}", grid=grid,
        in_specs=[spec] * 5, out_specs=[spec] * 3, out_shape=[res] * 3,
        compiler_params=_cparams(("parallel",)),
    )(q, k, v, do, dlse)


def _head_sums(t):
    head = lax.broadcasted_iota(jnp.int32, t.shape, 1) // HEAD_DIM
    out = jnp.zeros_like(t)
    for h in range(t.shape[1] // HEAD_DIM):
        mine = head == h
        out = jnp.where(mine, jnp.sum(jnp.where(mine, t, 0.0), axis=-1, keepdims=True), out)
    return out


def _dil_merge_fwd(os, lses, T, tm):
    G = len(DIL_GROUPS)
    W = DIL_OUT_WIDTH
    dils = [d for _, d in DIL_GROUPS]

    def body(*refs):
        o_refs, lse_refs = refs[:G], refs[G:2 * G]
        y_ref, w_refs, on_refs, scratch = refs[2 * G], refs[2 * G + 1:3 * G + 1], refs[3 * G + 1:4 * G + 1], refs[-1]
        o = [jnp.concatenate(_to_natural(r, scratch, d, tm), axis=1) for r, d in zip(o_refs, dils)]
        ls = [jnp.concatenate(_to_natural(r, scratch, d, tm), axis=1) for r, d in zip(lse_refs, dils)]
        m = functools.reduce(jnp.maximum, ls)
        es = [jnp.exp(l - m) for l in ls]
        tot = functools.reduce(jnp.add, es)
        ws = [e / tot for e in es]
        y_ref[...] = functools.reduce(jnp.add, [w * t for w, t in zip(ws, o)]).astype(y_ref.dtype)
        for g in range(G):
            w_refs[g][...] = ws[g]
            on_refs[g][...] = o[g]

    nat = pl.BlockSpec((tm, W), lambda i: (i, 0))
    res = pl.pallas_call(
        body, name="dil_merge_fwd", grid=(T // tm,),
        in_specs=[_residue_tile(d, tm) for d in dils] * 2,
        out_specs=[nat] * (2 * G + 1),
        out_shape=[jax.ShapeDtypeStruct((T, W), BF)] + [jax.ShapeDtypeStruct((T, W), F32)] * (2 * G),
        scratch_shapes=[pltpu.VMEM((tm, 128), F32)],
        compiler_params=_cparams(("parallel",)),
    )(*os, *lses)
    return res[0], res[1:G + 1], res[G + 1:]


def _dil_merge_bwd(dy, os, ws, tm, after=()):
    G = len(DIL_GROUPS)
    T, W = dy.shape
    dils = [d for _, d in DIL_GROUPS]
    n_after = len(after)

    def body(*refs):
        dyt = refs[0][...]
        o, w = [r[...] for r in refs[1:G + 1]], [r[...] for r in refs[G + 1:2 * G + 1]]
        refs = refs[2 * G + 1 + n_after:]
        do_refs, dlse_refs, scratch = refs[:G], refs[G:2 * G], refs[-1]
        dws = [_head_sums(dyt * t) for t in o]
        mean = functools.reduce(jnp.add, [a * b for a, b in zip(w, dws)])
        for g, d in enumerate(dils):
            do, dlse = w[g] * dyt, w[g] * (dws[g] - mean)
            for pair in range(DIL_PAIRS):
                cols = slice(pair * 128, (pair + 1) * 128)
                _from_natural(do[:, cols], scratch, do_refs[g], pair, d, tm)
                _from_natural(dlse[:, cols], scratch, dlse_refs[g], pair, d, tm)

    nat = pl.BlockSpec((tm, W), lambda i: (i, 0))
    res = pl.pallas_call(
        body, name="dil_merge_bwd", grid=(T // tm,),
        in_specs=[nat] * (2 * G + 1) + [ANY] * n_after,
        out_specs=[_residue_tile(d, tm) for d in dils] * 2,
        out_shape=[_residue_shape(d, T, BF) for d in dils] + [_residue_shape(d, T, F32) for d in dils],
        scratch_shapes=[pltpu.VMEM((tm, 128), F32)],
        compiler_params=_cparams(("parallel",)),
    )(dy, *os, *ws, *after)
    return res[:G], res[G:]


def _qkv_prep(z, cos2, sin_signed, tm):
    T = z.shape[0]
    G = len(DIL_GROUPS)
    dils = [d for _, d in DIL_GROUPS]
    n_dil_blocks = 3 * DIL_WIDTH // 128

    def body(*refs):
        blocks = refs[:n_dil_blocks]
        cos_ref, sin_ref = refs[n_dil_blocks], refs[1 + n_dil_blocks]
        outs = refs[2 + n_dil_blocks:]
        for part in range(3):
            for g, d in enumerate(dils):
                out = outs[g * 3 + part]
                for pair in range(DIL_PAIRS):
                    blk = blocks[part * (DIL_WIDTH // 128) + g * DIL_PAIRS + pair]
                    for r in range(d):
                        rows = pl.ds(r, tm // d, stride=d) if d > 1 else slice(None)
                        x = blk[rows, :]
                        if part < 2:
                            x = _rope(x, cos_ref[rows, :], sin_ref[rows, :])
                        if part == 0:
                            x = x * Q_SCALE
                        out[pair, r] = x.astype(out.dtype)

    lane_block = [pl.BlockSpec((tm, 128), functools.partial(lambda b, i: (i, b), b)) for b in range(n_dil_blocks)]
    tab = pl.BlockSpec((tm, 128), lambda i: (i, 0))
    res = pl.pallas_call(
        body, name="qkv_prep", grid=(T // tm,),
        in_specs=lane_block + [tab, tab],
        out_specs=[_residue_tile(d, tm) for d in dils for _ in range(3)],
        out_shape=[_residue_shape(d, T, BF) for d in dils for _ in range(3)],
        compiler_params=_cparams(("parallel",)),
    )(*[z] * n_dil_blocks, cos2, sin_signed)
    return [res[3 * g:3 + 3 * g] for g in range(G)]


def _qkv_unprep(d_na, d_dil, cos2, sin_signed, tm, after=()):
    T = d_na[0].shape[0]
    G = len(DIL_GROUPS)
    dils = [d for _, d in DIL_GROUPS]
    n_after = len(after)

    def body(*refs):
        dq, dk, dv = (r[...] for r in refs[:3])
        res_refs = refs[3:3 + 3 * G]
        cs, sn = refs[3 + 3 * G][...], refs[4 + 3 * G][...]
        out, scratch = refs[5 + 3 * G + n_after], refs[-1]
        cols = [dq * Q_SCALE, dk, dv]
        for part in range(3):
            for g, d in enumerate(dils):
                for x in _to_natural(res_refs[g * 3 + part], scratch, d, tm):
                    if part < 2:
                        x = _rope(x, cs, -sn)
                    cols.append(x * Q_SCALE if part == 0 else x)
        out[...] = jnp.concatenate(cols, axis=1).astype(out.dtype)

    wide = pl.BlockSpec((tm, NA_WIDTH), lambda i: (i, 0))
    tab = pl.BlockSpec((tm, 128), lambda i: (i, 0))
    return pl.pallas_call(
        body, name="qkv_unprep", grid=(T // tm,),
        in_specs=[wide] * 3 + [_residue_tile(d, tm) for d in dils for _ in range(3)] + [tab, tab] + [ANY] * n_after,
        out_specs=pl.BlockSpec((tm, QKV_WIDTH), lambda i: (i, 0)),
        out_shape=jax.ShapeDtypeStruct((T, QKV_WIDTH), BF),
        scratch_shapes=[pltpu.VMEM((tm, 128), F32)],
        compiler_params=_cparams(("parallel",)),
    )(*d_na, *[t for g in range(G) for t in d_dil[g]], cos2, sin_signed, *after)


def _rope_tables(positions):
    half = HEAD_DIM // 2
    inv_freq = ROPE_THETA ** (-jnp.arange(half, dtype=F32) / half)
    ang = positions.astype(F32)[:, None] * inv_freq
    cos, sin = jnp.cos(ang), jnp.sin(ang)
    return jnp.tile(jnp.concatenate([cos, cos], axis=1), (1, 2)), jnp.tile(jnp.concatenate([-sin, sin], axis=1), (1, 2))


def _pack_rows(t):
    return t.reshape(-1, PACK_W)


def _me():
    return lax.axis_index("x"), lax.axis_index("y"), lax.axis_index("c")


def _other_chips(x, y):
    return [(1 - x, y), (x, 1 - y), (1 - x, 1 - y)]


def _pair_sum(g, got, tm):
    S, R, W = g.shape
    half = R // 2
    nb = half // tm

    def body(pos_ref, g_ref, got_ref, own_ref, ob_ref):
        tot = g_ref[...] + got_ref[...]
        ob_ref[...] = tot.astype(ob_ref.dtype)

        @pl.when(pl.program_id(1) == pos_ref[1])
        def _():
            own_ref[...] = tot

    tile = pl.BlockSpec((None, tm, W), lambda i, s, pos_ref: (s, i, 0))
    c, chip = lax.axis_index("c"), 2 * lax.axis_index("x") + lax.axis_index("y")
    return pl.pallas_call(
        body, name="pair_sum",
        grid_spec=pltpu.PrefetchScalarGridSpec(
            num_scalar_prefetch=1, grid=(nb, S),
            in_specs=[pl.BlockSpec((None, tm, W), lambda i, s, pos_ref: (s, pos_ref[0] * nb + i, 0)), tile],
            out_specs=[pl.BlockSpec((tm, W), lambda i, s, pos_ref: (i, 0)), tile]),
        out_shape=[jax.ShapeDtypeStruct((half, W), F32), jax.ShapeDtypeStruct((S, half, W), BF)],
        compiler_params=_cparams(("parallel", "arbitrary")),
    )(jnp.stack([c, chip]).astype(jnp.int32), g, got)


def _chip_sum(own, others, tm):
    n, h, W = others.shape
    nb = h // tm

    def body(c_ref, own_ref, p_ref, o_ref):
        o_ref[...] = ((own_ref[...] + p_ref[0].astype(F32)) + p_ref[1].astype(F32)) + p_ref[2].astype(F32)

    return pl.pallas_call(
        body, name="chip_sum",
        grid_spec=pltpu.PrefetchScalarGridSpec(
            num_scalar_prefetch=1, grid=(nb,),
            in_specs=[pl.BlockSpec((tm, W), lambda i, c_ref: (i, 0)), pl.BlockSpec((n, tm, W), lambda i, c_ref: (0, i, 0))],
            out_specs=pl.BlockSpec((tm, W), lambda i, c_ref: (c_ref[0] * nb + i, 0))),
        out_shape=jax.ShapeDtypeStruct((2 * h, W), F32),
        compiler_params=_cparams(("parallel",)),
    )(lax.axis_index("c").reshape(1).astype(jnp.int32), own, others)


def _join_halves(shard):
    h = shard.shape[0] // 2

    def body(in_ref, out_ref, send_sem, recv_sem):
        x, y, c = _me()
        cp = pltpu.make_async_remote_copy(
            src_ref=in_ref.at[pl.ds(c * h, h), :], dst_ref=out_ref.at[pl.ds(c * h, h), :],
            send_sem=send_sem, recv_sem=recv_sem, device_id=(x, y, 1 - c), device_id_type=MESH)
        cp.start()
        pltpu.make_async_remote_copy(
            src_ref=in_ref.at[pl.ds(c * h, h), :], dst_ref=out_ref.at[pl.ds((1 - c) * h, h), :],
            send_sem=send_sem, recv_sem=recv_sem, device_id=(x, y, 1 - c), device_id_type=MESH).wait_recv()
        cp.wait_send()

    return pl.pallas_call(
        body, name="join_halves", in_specs=[ANY], out_specs=ANY,
        out_shape=jax.ShapeDtypeStruct(shard.shape, shard.dtype), input_output_aliases={0: 0},
        scratch_shapes=[pltpu.SemaphoreType.DMA, pltpu.SemaphoreType.DMA],
    )(shard)


def _allreduce_small(s, after=()):
    R, W = s.shape
    n_after = len(after)

    def body(s_ref, *rest):
        o_ref, buf, send_sems, recv_sems = rest[n_after:]
        x, y, c = _me()
        me = 4 * x + 2 * y + c
        buf[me] = s_ref[...]
        peers = [((x + fx) % 2, (y + fy) % 2, (c + fc) % 2) for fx in range(2) for fy in range(2) for fc in range(2)][1:]
        sends = [pltpu.make_async_remote_copy(
            src_ref=s_ref, dst_ref=buf.at[me], send_sem=send_sems.at[k], recv_sem=recv_sems.at[k],
            device_id=peer, device_id_type=MESH) for k, peer in enumerate(peers)]
        for cp in sends:
            cp.start()
        for k, peer in enumerate(peers):
            pltpu.make_async_remote_copy(
                src_ref=s_ref, dst_ref=buf.at[4 * peer[0] + 2 * peer[1] + peer[2]], send_sem=send_sems.at[k],
                recv_sem=recv_sems.at[k], device_id=peer, device_id_type=MESH).wait_recv()
        for cp in sends:
            cp.wait_send()
        total = buf[0]
        for d in range(1, N_DEV):
            total = total + buf[d]
        o_ref[...] = total

    return pl.pallas_call(
        body, name="allreduce_small",
        in_specs=[pl.BlockSpec(memory_space=pltpu.VMEM)] + [ANY] * n_after, out_specs=pl.BlockSpec(memory_space=pltpu.VMEM),
        out_shape=jax.ShapeDtypeStruct((R, W), F32),
        scratch_shapes=[pltpu.VMEM((N_DEV, R, W), F32), pltpu.SemaphoreType.DMA((N_DEV - 1,)), pltpu.SemaphoreType.DMA((N_DEV - 1,))],
    )(s, *after)


HBM_SPEC = pl.BlockSpec(memory_space=pltpu.HBM)
SEM_SPEC = pl.BlockSpec(memory_space=pltpu.SEMAPHORE)
DATAFLOW = pltpu.SideEffectType.DATAFLOW_SIDE_EFFECTING


class _InFlight(NamedTuple):
    sems: tuple
    src: jax.Array
    land: jax.Array
    token: jax.Array


def _split_start(name, src, land_shape, land_dtype, n, copies, after=()):
    n_after = len(after)

    def body(src_ref, land_ref, *rest):
        rest = rest[n_after:]
        sems, token = rest[:2 * n], rest[-1]
        for k, (s, d, peer) in enumerate(copies(src_ref, land_ref)):
            pltpu.make_async_remote_copy(src_ref=s, dst_ref=d, send_sem=sems[k], recv_sem=sems[n + k],
                                         device_id=peer, device_id_type=MESH).start()
        token[...] = jnp.zeros_like(token)

    outs = pl.pallas_call(
        body, name=name,
        out_shape=(*[pltpu.SemaphoreType.DMA(())] * (2 * n), pltpu.HBM(src.shape, src.dtype), pltpu.HBM(land_shape, land_dtype),
                   jax.ShapeDtypeStruct((8, 128), F32)),
        in_specs=(HBM_SPEC, HBM_SPEC, *[ANY] * n_after),
        out_specs=(*[SEM_SPEC] * (2 * n), HBM_SPEC, HBM_SPEC, pl.BlockSpec(memory_space=pltpu.VMEM)),
        input_output_aliases={0: 2 * n, 1: 2 * n + 1},
        compiler_params=pltpu.CompilerParams(has_side_effects=DATAFLOW),
    )(pltpu.with_memory_space_constraint(src, pltpu.HBM), pltpu.with_memory_space_constraint(lax.empty(land_shape, land_dtype), pltpu.HBM),
      *after)
    return _InFlight(tuple(outs[:2 * n]), outs[2 * n], outs[2 * n + 1], outs[2 * n + 2])


def _split_wait(name, flight, after, n, copies):
    def body(src_ref, land_ref, *rest):
        sems = rest[:2 * n]
        for k, (s, d, peer) in enumerate(copies(src_ref, land_ref)):
            cp = pltpu.make_async_remote_copy(src_ref=s, dst_ref=d, send_sem=sems[k], recv_sem=sems[n + k],
                                              device_id=peer, device_id_type=MESH)
            cp.wait_send()
            cp.wait_recv()

    return pl.pallas_call(
        body, name=name,
        out_shape=(pltpu.HBM(flight.src.shape, flight.src.dtype), pltpu.HBM(flight.land.shape, flight.land.dtype)),
        in_specs=(HBM_SPEC, HBM_SPEC, *[SEM_SPEC] * (2 * n), ANY),
        out_specs=(HBM_SPEC, HBM_SPEC), input_output_aliases={0: 0, 1: 1},
        compiler_params=pltpu.CompilerParams(has_side_effects=DATAFLOW),
    )(flight.src, flight.land, *flight.sems, after)


def _gather_copies(src_ref, land_ref):
    x, y, c = _me()
    return [(src_ref, land_ref.at[2 * x + y], (*chip, c)) for chip in _other_chips(x, y)]


def _gather_start(packed, tag, after=()):
    return _split_start(f"gather_start_{tag}", packed, (N_CHIPS, *packed.shape), packed.dtype, 3, _gather_copies, after)


def _gather_wait(flight, after, tag):
    src, others = _split_wait(f"gather_wait_{tag}", flight, after, 3, _gather_copies)
    return lax.dynamic_update_slice(others, src[None], (2 * lax.axis_index("x") + lax.axis_index("y"), 0, 0))


def _across_copies(src_ref, land_ref):
    x, y, c = _me()
    half = src_ref.shape[0] // 2
    rows = pl.ds(c * half, half)
    return [(src_ref.at[rows, :], land_ref.at[2 * x + y, rows, :], (*chip, c)) for chip in _other_chips(x, y)]


def _to_sibling_copies(all_ref, unused_ref):
    x, y, c = _me()
    half = all_ref.shape[1] // 2
    places = [all_ref.at[2 * chip[0] + chip[1], pl.ds(c * half, half), :] for chip in _other_chips(x, y)]
    return [(place, place, (x, y, 1 - c)) for place in places]


def _gather_halves_start(shard):
    return _split_start("gather_in_across_start", shard, (N_CHIPS, *shard.shape), shard.dtype, 3, _across_copies)


def _gather_halves_finish(flight, after):
    shard, landed = _split_wait("gather_in_across_wait", flight, after, 3, _across_copies)
    forward = _split_start("gather_in_sibling_start", landed, (8, 128), landed.dtype, 3, _to_sibling_copies)
    others = _split_wait("gather_in_sibling_wait", forward, forward.token, 3, _to_sibling_copies)[0]
    return lax.dynamic_update_slice(others, shard[None], (2 * lax.axis_index("x") + lax.axis_index("y"), 0, 0))


def _swap_copies(src_ref, land_ref):
    x, y, c = _me()
    half = land_ref.shape[1]
    return [(src_ref.at[:, pl.ds((1 - c) * half, half), :], land_ref, (x, y, 1 - c))]


def _swap_start(g, tag):
    S, R, W = g.shape
    return _split_start(f"swap_halves_start_{tag}", g, (S, R // 2, W), g.dtype, 1, _swap_copies)


def _swap_wait(flight, after, tag):
    return _split_wait(f"swap_halves_wait_{tag}", flight, after, 1, _swap_copies)


def _scatter_copies(src_ref, land_ref):
    x, y, c = _me()
    return [(src_ref.at[2 * chip[0] + chip[1]], land_ref.at[j], (*chip, c)) for j, chip in enumerate(_other_chips(x, y))]


def _scatter_start(part, tag):
    S, h, W = part.shape
    return _split_start(f"scatter_chips_start_{tag}", part, (S - 1, h, W), part.dtype, 3, _scatter_copies)


def _scatter_wait(flight, after, tag):
    return _split_wait(f"scatter_chips_wait_{tag}", flight, after, 3, _scatter_copies)[1]


def _join_copies(shard_ref, unused_ref):
    x, y, c = _me()
    h = shard_ref.shape[0] // 2
    rows = shard_ref.at[pl.ds(c * h, h), :]
    return [(rows, rows, (x, y, 1 - c))]


def _join_start(shard):
    return _split_start("join_halves_start", shard, (8, 128), shard.dtype, 1, _join_copies)


def _join_wait(flight, after):
    return _split_wait("join_halves_wait", flight, after, 1, _join_copies)[0]


def _adamw(name, g, g_row0, w, m, v):
    _, R, C = w.shape
    tm = next(cand for cand in (368, 256, 128, 64, 32, 16, 8) if R % cand == 0)
    assert g_row0 % tm == 0 and g.shape[1] == C

    def body(g_ref, w_ref, m_ref, v_ref, go_ref, d_ref, mo_ref, vo_ref):
        gt = g_ref[...]
        mt = ADAM_B1 * m_ref[...] + (1.0 - ADAM_B1) * gt
        vt = ADAM_B2 * v_ref[...] + (1.0 - ADAM_B2) * jnp.square(gt)
        m_hat = mt / (1.0 - ADAM_B1 ** ADAM_STEP)
        v_hat = vt / (1.0 - ADAM_B2 ** ADAM_STEP)
        go_ref[...] = gt
        d_ref[...] = -ADAM_LR * (m_hat / (jnp.sqrt(v_hat) + ADAM_EPS) + ADAM_WD * w_ref[...])
        mo_ref[...] = mt
        vo_ref[...] = vt

    state = pl.BlockSpec((None, tm, C), lambda i: (0, i, 0))
    return pl.pallas_call(
        body, name=name, grid=(R // tm,),
        in_specs=[pl.BlockSpec((tm, C), lambda i: (g_row0 // tm + i, 0)), state, state, state],
        out_specs=[state] * 4, out_shape=[jax.ShapeDtypeStruct((1, R, C), F32)] * 4,
        compiler_params=_cparams(("parallel",)),
    )(g, w, m, v)


def _unpack_weights(gathered, names):
    S = gathered.shape[0]
    shard_shapes = {"w_in": (D_MODEL, (QKV_WIDTH + 2 * D_MODEL) // S), "w_branch_na": (NA_WIDTH, D_MODEL // S),
                    "w_branch_dil": (DIL_OUT_WIDTH, D_MODEL // S), "w_out": (D_MODEL // S, D_MODEL),
                    "w_up": (D_MODEL, D_FF // S), "w_down": (D_FF // S, D_MODEL),
                    "w_ple_gate": (D_MODEL // S, D_MODEL), "w_ple_proj": (PLE_DIM, D_MODEL // S)}
    col_sharded = {"w_in", "w_branch_na", "w_branch_dil", "w_up", "w_ple_proj"}
    out, r0 = {}, 0
    for name in names:
        rows, cols = shard_shapes[name]
        n = rows * cols // PACK_W
        t = gathered[:, r0:r0 + n, :].reshape(S, rows, cols)
        r0 += n
        out[name] = t.transpose(1, 0, 2).reshape(rows, S * cols) if name in col_sharded else t.reshape(S * rows, cols)
    return out


def kernel(x, p, positions, g_mix, w_in, rpb, w_branch_na, w_branch_dil, w_out, g_mlp, w_up, w_down, g_ple, w_ple_gate, w_ple_proj, g_final, loss_target, m_g_mix, m_w_in, m_rpb, m_w_branch_na, m_w_branch_dil, m_w_out, m_g_mlp, m_w_up, m_w_down, m_g_ple, m_w_ple_gate, m_w_ple_proj, m_g_final, v_g_mix, v_w_in, v_rpb, v_w_branch_na, v_w_branch_dil, v_w_out, v_g_mlp, v_w_up, v_w_down, v_g_ple, v_w_ple_gate, v_w_ple_proj, v_g_final):
    shards = {"w_in": w_in[0], "w_branch_na": w_branch_na[0], "w_branch_dil": w_branch_dil[0], "w_out": w_out[0],
              "w_up": w_up[0], "w_down": w_down[0], "w_ple_gate": w_ple_gate[0], "w_ple_proj": w_ple_proj[0]}
    params = {"w_in": w_in, "w_branch_na": w_branch_na, "w_branch_dil": w_branch_dil, "w_out": w_out, "w_up": w_up,
              "w_down": w_down, "w_ple_gate": w_ple_gate, "w_ple_proj": w_ple_proj,
              "m_w_in": m_w_in, "m_w_branch_na": m_w_branch_na, "m_w_branch_dil": m_w_branch_dil, "m_w_out": m_w_out,
              "m_w_up": m_w_up, "m_w_down": m_w_down, "m_w_ple_gate": m_w_ple_gate, "m_w_ple_proj": m_w_ple_proj,
              "v_w_in": v_w_in, "v_w_branch_na": v_w_branch_na, "v_w_branch_dil": v_w_branch_dil, "v_w_out": v_w_out,
              "v_w_up": v_w_up, "v_w_down": v_w_down, "v_w_ple_gate": v_w_ple_gate, "v_w_ple_proj": v_w_ple_proj}

    xs, ps, tgt = x[0], p[0, 0], loss_target[0]
    T = xs.shape[0]
    TM = 512
    gm, gl, gp, gf = g_mix, g_mlp, g_ple, g_final.reshape(1, D_MODEL)

    across = _gather_halves_start(shards["w_in"].astype(BF))
    a = _rowwise("norm_mix", lambda h, g: h * _rms(h) * g, T, TM, [_row(xs, TM), _full(gm)], [(D_MODEL, BF)],
                 after=(across.token,))
    cos2, sin_signed = _rope_tables(positions[0])
    tab = _na_bias_table(rpb[0])
    w_in_all = _gather_halves_finish(across, a)
    W = {"w_in": w_in_all.transpose(1, 0, 2).reshape(D_MODEL, -1)}
    mix_flight = _gather_start(jnp.concatenate([_pack_rows(shards[n].astype(BF)) for n in GATHER_MIX], axis=0), "mix",
                               after=(w_in_all,))
    rest_flight = _gather_start(jnp.concatenate([_pack_rows(shards[n].astype(BF)) for n in GATHER_MLP], axis=0), "mlp",
                                after=(mix_flight.token,))
    w_gates = W["w_in"][:, QKV_WIDTH:]

    n3 = 3 * NA_WIDTH
    qkv = _mm("in_na", a, W["w_in"], "nn", 1024, 768, 1024, [BF], after=(rest_flight.token,),
              b_view=(n3, (D_MODEL, 768), lambda j, k: (k, j)))
    z_dil = _mm("in_dil", a, W["w_in"], "nn", 1024, 768, 1024, [F32],
                b_view=(3 * DIL_WIDTH, (D_MODEL, 768), lambda j, k: (k, n3 // 768 + j)))
    z_gates = _mm("in_gates", a, w_gates, "nn", 1024,1024, 1024, [BF])

    dil_ops = _qkv_prep(z_dil, cos2, sin_signed, TM)
    y_na = _na_fwd(qkv, tab)
    band = [_band_fwd(*dil_ops[g], g) for g in range(len(DIL_GROUPS))]
    y_dil, w_grp, o_nat = _dil_merge_fwd([b[0] for b in band], [b[1] for b in band], T, TM)

    W.update(_unpack_weights(_gather_wait(mix_flight, y_dil, "mix"), GATHER_MIX))
    u_na = _mm("branch_na", y_na, W["w_branch_na"], "nn", 1024,1024, 512, [BF])
    u_dil = _mm("branch_dil", y_dil, W["w_branch_dil"], "nn", 1024,1024, 256, [BF])
    mixed = _rowwise(
        "gate_mix", lambda gn, gd, un, ud: _sigmoid(gn.astype(F32)) * un.astype(F32) + _sigmoid(gd.astype(F32)) * ud.astype(F32), T, TM,
        [_row(z_gates, TM, 0, D_MODEL), _row(z_gates, TM, 1, D_MODEL), _row(u_na, TM), _row(u_dil, TM)], [(D_MODEL, BF)])
    def add_norm(d, h, g):
        h = h + d
        return h, h * _rms(h) * g

    h1, cn = _mm("out_proj", mixed, W["w_out"], "nn", 512, 1024, 1024, [F32, BF], epilogue=add_norm, extras=(xs,), consts=(gl,))
    mlp_all = _gather_wait(rest_flight, cn, "mlp")
    W.update({n: t for n, t in _unpack_weights(mlp_all, GATHER_MLP).items() if n.startswith("w_ple")})
    chip_block = (None, D_MODEL, PACK_W)
    up, act = _mm("mlp_up", cn, mlp_all, "nn", 1024,1024, 1024, [BF, BF],
                  epilogue=lambda acc: (acc, jnp.square(jnp.maximum(acc, 0.0))), b_view=(D_FF, chip_block, lambda j, k: (j, 0, 0)))
    h2, en = _mm("mlp_down", act, mlp_all, "nn", 1024, 1024, 1024, [F32, BF], epilogue=add_norm, extras=(h1,), consts=(gp,),
                 b_view=(D_MODEL, chip_block, lambda j, k: (k, 1, 0)))
    pp = _mm("ple_proj", ps, W["w_ple_proj"], "nn", 1024,1024, 256, [F32])

    def head(gtt, h2t, ppt, tg, g):
        sg = _sigmoid(gtt)
        h3 = h2t + sg * ppt
        yo = h3 * _rms(h3) * g
        diff = yo - tg
        loss = 0.5 * jnp.sum(jnp.mean(jnp.square(diff), axis=-1, keepdims=True), axis=0, keepdims=True)
        dh3, dg = _rms_bwd(diff * (1.0 / D_MODEL), h3, g)
        return dh3, dh3 * ppt * sg * (1.0 - sg), dh3 * sg, jnp.broadcast_to(loss, (1, 128)), dg

    dh3, d_gt, d_pp, loss_part, dg_final = _mm(
        "ple_gate_loss_head", en, W["w_ple_gate"], "nn", 512, 1024, 1024, [F32, BF, BF], epilogue=head,
        extras=(h2, pp, tgt), consts=(gf,), sums=[128, D_MODEL])

    early_shapes = {n: shards[n].shape for n in REDUCE_EARLY}
    early_rows = sum(r * c for r, c in early_shapes.values()) // PACK_W
    shard_rows = D_MODEL // N_CHIPS
    early_buf = _mm("g_ple_gate", en, d_gt, "tn", 1024, 1024, 1024, [F32],
                    into=(jax.ShapeDtypeStruct((N_CHIPS, early_rows, PACK_W), F32), (N_CHIPS, shard_rows, PACK_W),
                          lambda i, j: (0, 2 * D_MODEL // shard_rows, 0)))
    g_ple_proj = _mm("g_ple_proj", ps, d_pp, "tn", 256, 1024, 1024,[F32])

    def add_norm_bwd(dn, dh_out, h, g):
        dh, dg = _rms_bwd(dn, h, g)
        dh = dh_out + dh
        return dh, dh, dg

    dh2, dh2_b, dg_ple = _mm("d_ple_gate", d_gt, W["w_ple_gate"], "nt", 512, 1024, 1024, [F32, BF],
                             epilogue=add_norm_bwd, extras=(dh3, h2), consts=(gp,), sums=[D_MODEL])
    d_up = _mm("d_mlp_down", dh2_b, mlp_all, "nt", 1024,1024, 1024, [BF], b_view=(D_FF, chip_block, lambda j, k: (j, 1, 0)),
               epilogue=lambda acc, u: (acc * (2.0 * jnp.maximum(u.astype(F32), 0.0)),), extras=(up,))
    early_buf = _mm("g_mlp_down", act, dh2_b, "tn", 1024, 1024, 1024,[F32],
                    into=(early_buf, (None, D_MODEL, PACK_W), lambda i, j: (i, 1, 0)))
    early_buf = _mm("g_mlp_up", cn, d_up, "tn", 1024, 1024, 1024,[F32],
                    into=(early_buf, (None, D_MODEL, PACK_W), lambda i, j: (j, 0, 0)))
    dh1, dh1_b, dg_mlp = _mm("d_mlp_up", d_up, mlp_all, "nt", 1024, 1024, 1024, [F32, BF], epilogue=add_norm_bwd,
                             b_view=(D_MODEL, chip_block, lambda j, k: (k, 0, 0)),
                             extras=(dh2, h1), consts=(gl,), sums=[D_MODEL])
    d_mixed = _mm("d_out_proj", dh1_b, W["w_out"], "nt", 1024,1024, 1024, [F32])
    early_buf = _mm("g_out_proj", mixed, dh1_b, "tn", 1024, 1024, 1024, [F32],
                    into=(early_buf, (N_CHIPS, shard_rows, PACK_W), lambda i, j: (0, 2 * D_MODEL // shard_rows + 1, 0)))

    def gate_bwd(dm, gn, gd, un, ud):
        gn, gd, un, ud = (t.astype(F32) for t in (gn, gd, un, ud))
        sn, sd = _sigmoid(gn), _sigmoid(gd)
        return jnp.concatenate([dm * un * sn * (1.0 - sn), dm * ud * sd * (1.0 - sd)], axis=1), dm * sn, dm * sd

    dz_gates, d_u_na, d_u_dil = _rowwise(
        "gate_mix_bwd", gate_bwd, T, TM,
        [_row(d_mixed, TM), _row(z_gates, TM, 0, D_MODEL), _row(z_gates, TM, 1, D_MODEL), _row(u_na, TM), _row(u_dil, TM)],
        [(2 * D_MODEL, BF), (D_MODEL, BF), (D_MODEL, BF)])
    g_branch_na = _mm("g_branch_na", y_na, d_u_na, "tn", 1024, 1024, 1024,[F32])
    g_branch_dil = _mm("g_branch_dil", y_dil, d_u_dil, "tn", 256, 1024, 1024,[F32])
    small_rows = [jnp.concatenate([_pack_rows(g[:, s * shard_rows:(s + 1) * shard_rows]) for g in (g_ple_proj, g_branch_na, g_branch_dil)],
                                  axis=0) for s in range(N_CHIPS)]
    early_buf = lax.dynamic_update_slice(early_buf, jnp.stack(small_rows), (0, 2 * D_MODEL + 2 * shard_rows, 0))
    early_tm = early_rows // 4
    swap_flight = _swap_start(early_buf, "early")
    d_y_na = _mm("d_branch_na", d_u_na, W["w_branch_na"], "nt", 1024,512, 1024, [BF], after=(swap_flight.token,))
    d_y_dil = _mm("d_branch_dil", d_u_dil, W["w_branch_dil"], "nt", 1024,256, 1024, [F32])

    dqa, dka, dva, dtab = _na_bwd(qkv, tab, d_y_na)
    early_g, early_got = _swap_wait(swap_flight, dqa, "early")
    early_pair, early_pair_b = _pair_sum(early_g, early_got, early_tm)
    scatter_flight = _scatter_start(early_pair_b, "early")
    d_rpb = _na_rpb_grad(dtab)[:, :2 * NA_WIN_ROWS - 1, :2 * NA_WIN_COLS - 1]

    do_res, dlse_res = _dil_merge_bwd(d_y_dil, o_nat, w_grp, TM, after=(scatter_flight.token,))
    d_dil = [_band_bwd(*dil_ops[g], do_res[g], dlse_res[g], g) for g in range(len(DIL_GROUPS))]

    dz_qkv = _qkv_unprep((dqa, dka, dva), d_dil, cos2, sin_signed, TM)
    g_in_parts = [_mm("g_in_qkv", a, dz_qkv, "tn", 1024, 1280, 1024,[F32]), _mm("g_in_gates", a, dz_gates, "tn", 1024, 1024, 1024,[F32])]
    early_mine = _chip_sum(early_pair, _scatter_wait(scatter_flight, g_in_parts[1], "early"), early_tm)
    join_flight = _join_start(early_mine)
    in_cols = shards["w_in"].shape[1]

    def owner_columns(s):
        lo, hi, split = s * in_cols, (s + 1) * in_cols, g_in_parts[0].shape[1]
        pieces = [g_in_parts[0][:, lo:min(hi, split)]] if lo < split else []
        pieces += [g_in_parts[1][:, max(lo, split) - split:hi - split]] if hi > split else []
        return pieces[0] if len(pieces) == 1 else jnp.concatenate(pieces, axis=1)

    late_tm = in_cols // 4
    late_swap = _swap_start(jnp.stack([owner_columns(s).T for s in range(N_CHIPS)]), "late")
    d_a = _mm("d_in_qkv", dz_qkv, W["w_in"], "nt", 1024,1024, 1280, [F32], after=(late_swap.token, join_flight.token),
              b_view=(D_MODEL, (D_MODEL, 1280), lambda j, k: (j, k)))
    late_g, late_got = _swap_wait(late_swap, d_a, "late")
    late_pair, late_pair_b = _pair_sum(late_g, late_got, late_tm)
    late_scatter = _scatter_start(late_pair_b, "late")
    def first_bwd(dn_gates, dn_qkv, dh_out, h, g):
        dh, dg = _rms_bwd(dn_gates + dn_qkv, h, g)
        return dh_out + dh, dg

    grad_x, dg_mix = _mm("d_in_gates", dz_gates, w_gates, "nt", 512, 1024, 1024, [F32], epilogue=first_bwd,
                         extras=(d_a, dh1, xs), consts=(gm,), sums=[D_MODEL], after=(late_scatter.token,))
    early_shard = _join_wait(join_flight, grad_x)

    n_rpb = rpb.size
    rpb_rows = 4
    small = jnp.concatenate([
        dg_mix, dg_mlp, dg_ple, dg_final,
        jnp.pad(d_rpb.reshape(-1), (0, rpb_rows * D_MODEL - n_rpb)).reshape(rpb_rows, D_MODEL),
        jnp.pad(loss_part, ((0, 0), (0, D_MODEL - loss_part.shape[1]))),
        jnp.zeros((SMALL_ROWS - 5 - rpb_rows, D_MODEL), F32)], axis=0)
    out = {"grad": {}, "delta": {}, "new_m": {}, "new_v": {}}

    def update(n, g, row0):
        res = _adamw("adamw_" + n, g, row0, params[n], params["m_" + n], params["v_" + n])
        for kind, t in zip(("grad", "delta", "new_m", "new_v"), res, strict=True):
            out[kind][n] = t

    row0 = 0
    for n in REDUCE_EARLY:
        rows, cols = early_shapes[n]
        n_rows = rows * cols // PACK_W
        if cols == PACK_W:
            update(n, early_shard, row0)
        else:
            update(n, early_shard[row0:row0 + n_rows].reshape(rows, cols), 0)
        row0 += n_rows
    late_others = _scatter_wait(late_scatter, out["new_v"][REDUCE_EARLY[-1]], "late")
    late_mine = _chip_sum(late_pair, late_others, late_tm)
    small = _allreduce_small(small, after=(late_mine,))
    res = _adamw("adamw_w_in", _join_halves(late_mine), 0, *[jnp.swapaxes(params[n], 1, 2) for n in ("w_in", "m_w_in", "v_w_in")])
    for kind, t in zip(("grad", "delta", "new_m", "new_v"), res, strict=True):
        out[kind]["w_in"] = jnp.swapaxes(t, 1, 2)
    loss = small[4 + rpb_rows, 0]

    def small_pack(a0, a1, a2, a3, r):
        return jnp.concatenate([a0.reshape(1, -1), a1.reshape(1, -1), a2.reshape(1, -1), a3.reshape(1, -1),
                                jnp.pad(r.reshape(-1), (0, rpb_rows * D_MODEL - n_rpb)).reshape(rpb_rows, D_MODEL)], axis=0)

    small_res = _adamw("adamw_small", small, 0, small_pack(g_mix, g_mlp, g_ple, g_final, rpb)[None],
                       small_pack(m_g_mix, m_g_mlp, m_g_ple, m_g_final, m_rpb)[None],
                       small_pack(v_g_mix, v_g_mlp, v_g_ple, v_g_final, v_rpb)[None])

    def small_unpack(t):
        return {"g_mix": t[0].reshape(g_mix.shape), "g_mlp": t[1].reshape(g_mlp.shape), "g_ple": t[2].reshape(g_ple.shape),
                "g_final": t[3].reshape(g_final.shape), "rpb": t[4:].reshape(-1)[:n_rpb].reshape(rpb.shape)}

    for kind, t in zip(("grad", "delta", "new_m", "new_v"), small_res, strict=True):
        out[kind].update(small_unpack(t[0]))

    order = ["g_mix", "w_in", "rpb", "w_branch_na", "w_branch_dil", "w_out", "g_mlp", "w_up", "w_down", "g_ple",
             "w_ple_gate", "w_ple_proj", "g_final"]
    return (loss, grad_x[None], *[out["grad"][n] for n in order], *[out["delta"][n] for n in order],
            *[out["new_m"][n] for n in order], *[out["new_v"][n] for n in order])
```

```python
import functools
from typing import NamedTuple

import jax
import jax.numpy as jnp
from jax import lax
from jax.experimental import pallas as pl
from jax.experimental.pallas import tpu as pltpu

BF = jnp.bfloat16
F32 = jnp.float32
MESH = pl.DeviceIdType.MESH
ANY = pl.BlockSpec(memory_space=pl.ANY)

V7X_VMEM_BYTES = 64 * 1024 * 1024
VMEM_LIMIT = V7X_VMEM_BYTES - 16 * 1024 * 1024

D_MODEL = 1024
HEAD_DIM = 64
GRID_W = 64
NA_HEADS = 8
NA_WIN_ROWS = 8
NA_WIN_COLS = 16
NA_WIDTH = NA_HEADS * HEAD_DIM
DIL_GROUPS = ((128, 1), (512, 4), (2048, 16))
DIL_HPG = 4
DIL_HEADS = DIL_HPG * len(DIL_GROUPS)
DIL_WIDTH = DIL_HEADS * HEAD_DIM
DIL_OUT_WIDTH = DIL_HPG * HEAD_DIM
DIL_RADIUS = 64
QKV_WIDTH = 3 * NA_WIDTH + 3 * DIL_WIDTH
D_FF = 4 * D_MODEL
PLE_DIM = 256
ROPE_THETA = 10000.0
RMS_EPS = 1e-6
NEG_INF = -1e30
Q_SCALE = HEAD_DIM ** -0.5

ADAM_LR = 0.001
ADAM_B1 = 0.9
ADAM_B2 = 0.999
ADAM_EPS = 1e-08
ADAM_WD = 0.01
ADAM_STEP = 10

N_CHIPS = 4
N_DEV = 8
PACK_W = 1024
BIG = ("w_in", "w_branch_na", "w_branch_dil", "w_out", "w_up", "w_down", "w_ple_gate", "w_ple_proj")
GATHER_MIX = ("w_branch_na", "w_branch_dil", "w_out")
GATHER_MLP = ("w_up", "w_down", "w_ple_gate", "w_ple_proj")
REDUCE_EARLY = ("w_up", "w_down", "w_ple_gate", "w_out", "w_ple_proj", "w_branch_na", "w_branch_dil")
SMALL_ROWS = 16


def _cparams(sem=None):
    return pltpu.CompilerParams(dimension_semantics=sem, vmem_limit_bytes=VMEM_LIMIT)


def _mm(name, a, b, mode, tm, tn, tk, out_dtypes, epilogue=None, extras=(), consts=(), sums=(), after=(), into=None,
        b_view=None):
    if mode == "nn":
        (M, K), N = a.shape, b.shape[1]
    elif mode == "nt":
        (M, K), N = a.shape, b.shape[0]
    else:
        (K, M), N = a.shape, b.shape[1]
    if b_view is not None:
        N = b_view[0]
    tm, tn, tk = min(tm, M), min(tn, N), min(tk, K)
    assert M % tm == 0 and N % tn == 0 and K % tk == 0, (name, M, N, K, tm, tn, tk)
    if mode == "nn":
        a_spec = pl.BlockSpec((tm, tk), lambda i, j, k: (i, k))
        b_spec = pl.BlockSpec((tk, tn), lambda i, j, k: (k, j))
        dims = (((1,), (0,)), ((), ()))
    elif mode == "nt":
        a_spec = pl.BlockSpec((tm, tk), lambda i, j, k: (i, k))
        b_spec = pl.BlockSpec((tn, tk), lambda i, j, k: (j, k))
        dims = (((1,), (1,)), ((), ()))
    else:
        a_spec = pl.BlockSpec((tk, tm), lambda i, j, k: (k, i))
        b_spec = pl.BlockSpec((tk, tn), lambda i, j, k: (k, j))
        dims = (((0,), (0,)), ((), ()))
    if b_view is not None:
        b_spec = pl.BlockSpec(b_view[1], lambda i, j, k: b_view[2](j, k))
    nk = K // tk
    n_extra, n_const, n_out, n_sum = len(extras), len(consts), len(out_dtypes), len(sums)
    tile = pl.BlockSpec((tm, tn), lambda i, j, k: (i, j))
    assert not sums or tn == N, "row sums need whole rows in a tile"

    n_after = len(after)

    def body(a_ref, b_ref, *rest):
        extra_refs, rest = rest[:n_extra + n_const], rest[n_extra + n_const + n_after:]
        out_refs, sum_refs, acc = rest[:n_out], rest[n_out:n_out + n_sum], rest[-1]
        i, k = pl.program_id(0), pl.program_id(2)
        def product():
            return lax.dot_general(a_ref[...].astype(BF), b_ref[...].astype(BF), dims, preferred_element_type=F32)

        if nk > 1:
            @pl.when(k == 0)
            def _():
                acc[...] = jnp.zeros_like(acc)

            acc[...] += product()

        @pl.when(k == nk - 1)
        def _():
            total = product() if nk == 1 else acc[...]
            outs = (total,) if epilogue is None else epilogue(total, *[e[...] for e in extra_refs])
            for o_ref, val in zip(out_refs, outs[:n_out], strict=True):
                o_ref[...] = val.astype(o_ref.dtype).reshape(o_ref.shape)
            for s_ref, val in zip(sum_refs, outs[n_out:], strict=True):
                @pl.when(i == 0)
                def _():
                    s_ref[...] = val

                @pl.when(i != 0)
                def _():
                    s_ref[...] += val

    out_specs = [tile] * n_out + [pl.BlockSpec((1, c), lambda i, j, k: (0, 0)) for c in sums]
    out_shape = [jax.ShapeDtypeStruct((M, N), dt) for dt in out_dtypes] + [jax.ShapeDtypeStruct((1, c), F32) for c in sums]
    operands, aliases = [a, b, *extras, *consts, *after], {}
    in_specs = ([a_spec, b_spec] + [tile] * n_extra
                + [pl.BlockSpec(c.shape, functools.partial(lambda nd, i, j, k: (0,) * nd, c.ndim)) for c in consts] + [ANY] * n_after)
    if into is not None:
        assert n_out == 1
        target, block, index = into
        out_specs = [pl.BlockSpec(block, lambda i, j, k: index(i, j))]
        out_shape = [jax.ShapeDtypeStruct(target.shape, target.dtype)]
        if not isinstance(target, jax.ShapeDtypeStruct):
            aliases = {len(operands): 0}
            operands.append(target)
            in_specs.append(ANY)
            n_after += 1

    outs = pl.pallas_call(
        body, name=name, grid=(M // tm, N // tn, nk),
        in_specs=in_specs, out_specs=out_specs, out_shape=out_shape,
        scratch_shapes=[pltpu.VMEM((tm, tn) if nk > 1 else (8, 128), F32)], input_output_aliases=aliases,
        compiler_params=_cparams(("arbitrary",) * 3 if sums else ("parallel", "parallel", "arbitrary")),
    )(*operands)
    return outs[0] if len(outs) == 1 else outs


def _row(arr, tm, col_block=None, width=None):
    width = arr.shape[1] if width is None else width
    cb = 0 if col_block is None else col_block
    return arr, pl.BlockSpec((tm, width), lambda i: (i, cb))


def _full(arr):
    nd = arr.ndim
    return arr, pl.BlockSpec(arr.shape, lambda i: (0,) * nd)


def _rowwise(name, body, T, tm, ins, outs, sums=(), after=()):
    n_in, n_out, n_sum, n_after = len(ins), len(outs), len(sums), len(after)

    def kern(*refs):
        in_refs, refs = refs[:n_in], refs[n_in + n_after:]
        out_refs, sum_refs = refs[:n_out], refs[n_out:]
        res = body(*[r[...] for r in in_refs])
        res = res if isinstance(res, tuple) else (res,)
        for o_ref, val in zip(out_refs, res[:n_out], strict=True):
            o_ref[...] = val.astype(o_ref.dtype)
        if n_sum:
            @pl.when(pl.program_id(0) == 0)
            def _():
                for s_ref in sum_refs:
                    s_ref[...] = jnp.zeros_like(s_ref)

            for s_ref, val in zip(sum_refs, res[n_out:], strict=True):
                s_ref[...] += val

    res = pl.pallas_call(
        kern, name=name, grid=(T // tm,),
        in_specs=[spec for _, spec in ins] + [ANY] * n_after,
        out_specs=[pl.BlockSpec((tm, c), lambda i: (i, 0)) for c, _ in outs]
        + [pl.BlockSpec((1, c), lambda i: (0, 0)) for c in sums],
        out_shape=[jax.ShapeDtypeStruct((T, c), dt) for c, dt in outs]
        + [jax.ShapeDtypeStruct((1, c), F32) for c in sums],
        compiler_params=_cparams(("arbitrary",)),
    )(*[a for a, _ in ins], *after)
    return res[0] if len(res) == 1 else res


def _sigmoid(x):
    return 1.0 / (1.0 + jnp.exp(-x))


def _rms(h):
    return lax.rsqrt(jnp.mean(h * h, axis=-1, keepdims=True) + RMS_EPS)


def _rms_bwd(dy, h, g):
    r = _rms(h)
    n = h * r
    dn = dy * g
    dh = r * (dn - n * jnp.mean(dn * n, axis=-1, keepdims=True))
    return dh, jnp.sum(dy * n, axis=0, keepdims=True)


def _rope(x, cos2, sin_signed):
    lane = lax.broadcasted_iota(jnp.int32, x.shape, 1)
    swapped = jnp.where((lane % HEAD_DIM) < HEAD_DIM // 2, pltpu.roll(x, 128 - HEAD_DIM // 2, 1), pltpu.roll(x, HEAD_DIM // 2, 1))
    return x * cos2 + swapped * sin_signed


NA_KEYS = NA_WIN_ROWS * GRID_W
NA_BASES = 8


def _na_row_geometry(r, rows):
    first = jnp.clip(r - NA_WIN_ROWS // 2, 0, rows - NA_WIN_ROWS)
    base = first - r + (NA_WIN_ROWS - 1)
    return pl.multiple_of(first * GRID_W, GRID_W), base


NA_ROWS_PER_STEP = 16
NA_BWD_ROWS_PER_STEP = 8


def _softmax_rows(s):
    p = jnp.exp(s - jnp.max(s, axis=-1, keepdims=True))
    return p / jnp.sum(p, axis=-1, keepdims=True)


def _na_probs(q, kw, bias):
    return _softmax_rows(lax.dot_general(q, kw, (((1,), (1,)), ((), ())), preferred_element_type=F32) + bias)


def _split_pair(t):
    first = lax.broadcasted_iota(jnp.int32, t.shape, 1) < HEAD_DIM
    zero = jnp.zeros_like(t)
    return jnp.where(first, t, zero), jnp.where(first, zero, t)


def _join_pair(a, b):
    return jnp.where(lax.broadcasted_iota(jnp.int32, a.shape, 1) < HEAD_DIM, a, b)


_NT = (((1,), (1,)), ((), ()))
_TN = (((0,), (0,)), ((), ()))


def _na_fwd(qkv, tab):
    T = qkv.shape[0]
    rows = T // GRID_W
    n_pairs = NA_WIDTH // 128

    def body(q_ref, k_ref, v_ref, tab_ref, y_ref):
        def step(it, carry):
            geo = [_na_row_geometry(it * NA_ROWS_PER_STEP + u, rows) for u in range(NA_ROWS_PER_STEP)]
            q0s = [pl.multiple_of((it * NA_ROWS_PER_STEP + u) * GRID_W, GRID_W) for u in range(NA_ROWS_PER_STEP)]
            ss = [lax.dot_general(jnp.concatenate(_split_pair(q_ref[pl.ds(q0, GRID_W), :] * Q_SCALE), axis=0),
                                  k_ref[pl.ds(k0, NA_KEYS), :], _NT, preferred_element_type=F32)
                  for q0, (k0, _) in zip(q0s, geo)]
            ps = [_softmax_rows(s + jnp.concatenate([tab_ref[0, base], tab_ref[1, base]], axis=0)) for s, (_, base) in zip(ss, geo)]
            ys = [jnp.dot(p.astype(BF), v_ref[pl.ds(k0, NA_KEYS), :], preferred_element_type=F32) for p, (k0, _) in zip(ps, geo)]
            for q0, y2 in zip(q0s, ys):
                y_ref[pl.ds(q0, GRID_W), :] = _join_pair(y2[:GRID_W], y2[GRID_W:]).astype(y_ref.dtype)
            return carry

        lax.fori_loop(0, rows // NA_ROWS_PER_STEP, step, 0)

    def cols(first):
        return pl.BlockSpec((T, 128), lambda j: (0, first + j))

    return pl.pallas_call(
        body, name="na_fwd", grid=(n_pairs,),
        in_specs=[cols(0), cols(n_pairs), cols(2 * n_pairs), pl.BlockSpec((2, NA_BASES, GRID_W, NA_KEYS), lambda j: (j, 0, 0, 0))],
        out_specs=cols(0), out_shape=jax.ShapeDtypeStruct((T, NA_WIDTH), BF),
        compiler_params=_cparams(("parallel",)),
    )(qkv, qkv, qkv, tab)


def _na_bwd(qkv, tab, do):
    T = qkv.shape[0]
    rows = T // GRID_W
    n_pairs = NA_WIDTH // 128

    def body(q_ref, k_ref, v_ref, tab_ref, do_ref, dq_ref, dk_ref, dv_ref, dtab_ref):
        dk_ref[...] = jnp.zeros_like(dk_ref)
        dv_ref[...] = jnp.zeros_like(dv_ref)
        dtab_ref[...] = jnp.zeros_like(dtab_ref)

        def step(it, carry):
            U = NA_BWD_ROWS_PER_STEP
            geo = [_na_row_geometry(it * U + u, rows) for u in range(U)]
            q0s = [pl.multiple_of((it * U + u) * GRID_W, GRID_W) for u in range(U)]
            q2s = [jnp.concatenate(_split_pair(q_ref[pl.ds(q0, GRID_W), :] * Q_SCALE), axis=0) for q0 in q0s]
            do2s = [jnp.concatenate(_split_pair(do_ref[pl.ds(q0, GRID_W), :]), axis=0) for q0 in q0s]
            ss = [lax.dot_general(q2, k_ref[pl.ds(k0, NA_KEYS), :], _NT, preferred_element_type=F32) for q2, (k0, _) in zip(q2s, geo)]
            dps = [lax.dot_general(do2, v_ref[pl.ds(k0, NA_KEYS), :], _NT, preferred_element_type=F32) for do2, (k0, _) in zip(do2s, geo)]
            ps = [_softmax_rows(s + jnp.concatenate([tab_ref[0, base], tab_ref[1, base]], axis=0)) for s, (_, base) in zip(ss, geo)]
            dss = [p * (dp - jnp.sum(dp * p, axis=-1, keepdims=True)) for p, dp in zip(ps, dps)]
            dvs = [lax.dot_general(p.astype(BF), do2, _TN, preferred_element_type=F32) for p, do2 in zip(ps, do2s)]
            dsbs = [ds.astype(BF) for ds in dss]
            dqs = [jnp.dot(dsb, k_ref[pl.ds(k0, NA_KEYS), :], preferred_element_type=F32) for dsb, (k0, _) in zip(dsbs, geo)]
            dks = [lax.dot_general(dsb, q2, _TN, preferred_element_type=F32) for dsb, q2 in zip(dsbs, q2s)]
            for u in range(U):
                k0, base = geo[u]
                dtab_ref[0, base] += dss[u][:GRID_W]
                dtab_ref[1, base] += dss[u][GRID_W:]
                dq_ref[pl.ds(q0s[u], GRID_W), :] = _join_pair(dqs[u][:GRID_W], dqs[u][GRID_W:])
                dk_ref[pl.ds(k0, NA_KEYS), :] += dks[u]
                dv_ref[pl.ds(k0, NA_KEYS), :] += dvs[u]
            return carry

        lax.fori_loop(0, rows // NA_BWD_ROWS_PER_STEP, step, 0)

    def cols(first):
        return pl.BlockSpec((T, 128), lambda j: (0, first + j))

    tabs = pl.BlockSpec((2, NA_BASES, GRID_W, NA_KEYS), lambda j: (j, 0, 0, 0))
    wide = jax.ShapeDtypeStruct((T, NA_WIDTH), F32)
    return pl.pallas_call(
        body, name="na_bwd", grid=(n_pairs,),
        in_specs=[cols(0), cols(n_pairs), cols(2 * n_pairs), tabs, cols(0)],
        out_specs=[cols(0), cols(0), cols(0), tabs],
        out_shape=[wide, wide, wide, jax.ShapeDtypeStruct((NA_HEADS, NA_BASES, GRID_W, NA_KEYS), F32)],
        compiler_params=_cparams(("parallel",)),
    )(qkv, qkv, qkv, tab, do)


def _na_bias_table(rpb):
    H, n_rows, n_cols = rpb.shape

    def body(r_ref, tab_ref):
        q = lax.broadcasted_iota(jnp.int32, (GRID_W, 128), 0)
        kc = lax.broadcasted_iota(jnp.int32, (GRID_W, 128), 1)
        first = jnp.clip(q - NA_WIN_COLS // 2, 0, GRID_W - NA_WIN_COLS)
        valid = (kc >= first) & (kc < first + NA_WIN_COLS)
        toeplitz = []
        for ro in range(n_rows):
            row = jnp.broadcast_to(r_ref[pl.ds(ro, 1), :], (GRID_W, 128))
            shifted = pltpu.roll(pltpu.roll(row, 128 - (NA_WIN_COLS - 1), 1), 0, 1, stride=1, stride_axis=0)
            toeplitz.append(jnp.where(valid, shifted, NEG_INF))
        for base in range(NA_BASES):
            for j in range(NA_WIN_ROWS // 2):
                even, odd = toeplitz[base + 2 * j], toeplitz[base + 2 * j + 1]
                tab_ref[base, :, pl.ds(j * 128, 128)] = jnp.where(kc < GRID_W, even, pltpu.roll(odd, GRID_W, 1))

    padded = jnp.pad(rpb, ((0, 0), (0, 16 - n_rows), (0, 128 - n_cols)))
    return pl.pallas_call(
        body, name="na_bias_table", grid=(H,),
        in_specs=[pl.BlockSpec((None, 16, 128), lambda h: (h, 0, 0))],
        out_specs=pl.BlockSpec((None, NA_BASES, GRID_W, NA_KEYS), lambda h: (h, 0, 0, 0)),
        out_shape=jax.ShapeDtypeStruct((H, NA_BASES, GRID_W, NA_KEYS), F32),
        compiler_params=_cparams(("parallel",)),
    )(padded)


def _na_rpb_grad(dtab, after=()):
    H = dtab.shape[0]
    n_rows = 2 * NA_WIN_ROWS - 1
    n_cols = 2 * NA_WIN_COLS - 1

    def body(d_ref, *rest):
        o_ref = rest[-1]
        lane = lax.broadcasted_iota(jnp.int32, (GRID_W, 128), 1)
        low = lane < GRID_W
        out_rows = []
        for ro in range(n_rows):
            acc = jnp.zeros((GRID_W, 128), F32)
            for base in range(NA_BASES):
                i = ro - base
                if not 0 <= i < NA_WIN_ROWS:
                    continue
                pair = d_ref[base, :, pl.ds((i // 2) * 128, 128)]
                if i % 2:
                    pair = pltpu.roll(pair, GRID_W, 1)
                acc = acc + jnp.where(low, pair, 0.0)
            skew = pltpu.roll(acc, 0, 1, stride=1, stride_axis=0)
            diag = jnp.sum(skew, axis=0, keepdims=True)
            out_rows.append(pltpu.roll(jnp.broadcast_to(diag, (8, 128)), 128 - (GRID_W - NA_WIN_COLS), 1)[:1])
        out_rows.append(jnp.zeros((1, 128), F32))
        res = jnp.concatenate(out_rows, axis=0)
        o_ref[...] = jnp.where(lax.broadcasted_iota(jnp.int32, res.shape, 1) < n_cols, res, 0.0)

    return pl.pallas_call(
        body, name="na_rpb_grad", grid=(H,),
        in_specs=[pl.BlockSpec((None, NA_BASES, GRID_W, NA_KEYS), lambda h: (h, 0, 0, 0))] + [ANY] * len(after),
        out_specs=pl.BlockSpec((None, n_rows + 1, 128), lambda h: (h, 0, 0)),
        out_shape=jax.ShapeDtypeStruct((H, n_rows + 1, 128), F32),
        compiler_params=_cparams(("parallel",)),
    )(jnp.flip(dtab, axis=2), *after)


BAND_Q = 128
BAND_KEYS = BAND_Q + 2 * DIL_RADIUS


def _band_geometry(n, L):
    q0 = pl.multiple_of(n * BAND_Q, BAND_Q)
    k0 = pl.multiple_of(jnp.clip(q0 - DIL_RADIUS, 0, L - BAND_KEYS), DIL_RADIUS)
    qi = q0 + lax.broadcasted_iota(jnp.int32, (BAND_Q, BAND_KEYS), 0)
    kj = k0 + lax.broadcasted_iota(jnp.int32, (BAND_Q, BAND_KEYS), 1)
    return q0, k0, jnp.abs(qi - kj) <= DIL_RADIUS


DIL_PAIRS = DIL_OUT_WIDTH // 128


def _residue_shape(dil, T, dtype):
    return jax.ShapeDtypeStruct((DIL_PAIRS, dil, T // dil, 128), dtype)


def _residue_tile(dil, tm):
    return pl.BlockSpec((DIL_PAIRS, dil, tm // dil, 128), lambda i: (0, 0, i, 0))


def _to_natural(ref, scratch, dil, tm):
    tiles = []
    for pair in range(DIL_PAIRS):
        if dil == 1:
            tiles.append(ref[pair, 0].astype(F32))
            continue
        for r in range(dil):
            scratch[pl.ds(r, tm // dil, stride=dil), :] = ref[pair, r].astype(F32)
        tiles.append(scratch[...])
    return tiles


def _from_natural(tile, scratch, ref, pair, dil, tm):
    if dil == 1:
        ref[pair, 0] = tile.astype(ref.dtype)
        return
    scratch[...] = tile
    for r in range(dil):
        ref[pair, r] = scratch[pl.ds(r, tm // dil, stride=dil), :].astype(ref.dtype)


def _band_specs(group, T):
    dil = DIL_GROUPS[group][1]
    L = T // dil
    assert L % BAND_Q == 0 and L >= BAND_KEYS, (T, dil)
    per_residue = min(BAND_BLOCKS_PER_STEP, L // BAND_Q)
    residues = min(dil, BAND_BLOCKS_PER_STEP // per_residue)
    spec = pl.BlockSpec((None, residues, L, 128), lambda s: (s % DIL_PAIRS, s // DIL_PAIRS, 0, 0))
    return L, residues, per_residue, (dil // residues * DIL_PAIRS,), spec


BAND_BLOCKS_PER_STEP = 8


def _band_softmax(s, valid):
    s = jnp.where(valid, s, NEG_INF)
    m = jnp.max(s, axis=-1, keepdims=True)
    p = jnp.exp(s - m)
    l = jnp.sum(p, axis=-1, keepdims=True)
    return p / l, m + jnp.log(l)


def _band_fwd(q, k, v, group):
    T = q.shape[1] * q.shape[2]
    L, residues, U, grid, spec = _band_specs(group, T)

    def body(q_ref, k_ref, v_ref, o_ref, lse_ref):
        def step(it, carry):
            geo = [(r, *_band_geometry(it * U + u, L)) for r in range(residues) for u in range(U)]
            ss = [lax.dot_general(jnp.concatenate(_split_pair(q_ref[r, pl.ds(q0, BAND_Q), :]), axis=0),
                                  k_ref[r, pl.ds(k0, BAND_KEYS), :], _NT, preferred_element_type=F32) for r, q0, k0, _ in geo]
            pls = [_band_softmax(s, jnp.concatenate([valid, valid], axis=0)) for s, (_, _, _, valid) in zip(ss, geo)]
            os = [jnp.dot(p.astype(BF), v_ref[r, pl.ds(k0, BAND_KEYS), :], preferred_element_type=F32)
                  for (p, _), (r, _, k0, _) in zip(pls, geo)]
            for (r, q0, _, _), o2, (_, lse) in zip(geo, os, pls):
                o_ref[r, pl.ds(q0, BAND_Q), :] = _join_pair(o2[:BAND_Q], o2[BAND_Q:])
                lse2 = jnp.broadcast_to(lse, (2 * BAND_Q, 128))
                lse_ref[r, pl.ds(q0, BAND_Q), :] = _join_pair(lse2[:BAND_Q], lse2[BAND_Q:])
            return carry

        lax.fori_loop(0, L // (BAND_Q * U), step, 0)

    res = _residue_shape(DIL_GROUPS[group][1], T, F32)
    return pl.pallas_call(
        body, name=f"band_fwd_g{group}", grid=grid,
        in_specs=[spec] * 3, out_specs=[spec] * 2, out_shape=[res, res],
        compiler_params=_cparams(("parallel",)),
    )(q, k, v)


def _band_bwd(q, k, v, do, dlse, group):
    T = q.shape[1] * q.shape[2]
    L, residues, U, grid, spec = _band_specs(group, T)

    def body(q_ref, k_ref, v_ref, do_ref, dlse_ref, dq_ref, dk_ref, dv_ref):
        dk_ref[...] = jnp.zeros_like(dk_ref)
        dv_ref[...] = jnp.zeros_like(dv_ref)

        def step(it, carry):
            geo = [(r, *_band_geometry(it * U + u, L)) for r in range(residues) for u in range(U)]
            q2s = [jnp.concatenate(_split_pair(q_ref[r, pl.ds(q0, BAND_Q), :]), axis=0) for r, q0, _, _ in geo]
            do2s = [jnp.concatenate(_split_pair(do_ref[r, pl.ds(q0, BAND_Q), :]), axis=0) for r, q0, _, _ in geo]
            ss = [lax.dot_general(q2, k_ref[r, pl.ds(k0, BAND_KEYS), :], _NT, preferred_element_type=F32)
                  for q2, (r, _, k0, _) in zip(q2s, geo)]
            dps = [lax.dot_general(do2, v_ref[r, pl.ds(k0, BAND_KEYS), :], _NT, preferred_element_type=F32)
                   for do2, (r, _, k0, _) in zip(do2s, geo)]
            ps = [_band_softmax(s, jnp.concatenate([valid, valid], axis=0))[0] for s, (_, _, _, valid) in zip(ss, geo)]
            dss = []
            for p, dp, (r, q0, _, _) in zip(ps, dps, geo):
                dl = dlse_ref[r, pl.ds(q0, BAND_Q), :]
                dl2 = jnp.concatenate([dl[:, :1], dl[:, HEAD_DIM:HEAD_DIM + 1]], axis=0)
                dss.append(p * (dp - jnp.sum(dp * p, axis=-1, keepdims=True) + dl2))
            dvs = [lax.dot_general(p.astype(BF), do2, _TN, preferred_element_type=F32) for p, do2 in zip(ps, do2s)]
            dsbs = [ds.astype(BF) for ds in dss]
            dqs = [jnp.dot(dsb, k_ref[r, pl.ds(k0, BAND_KEYS), :], preferred_element_type=F32) for dsb, (r, _, k0, _) in zip(dsbs, geo)]
            dks = [lax.dot_general(dsb, q2, _TN, preferred_element_type=F32) for dsb, q2 in zip(dsbs, q2s)]
            for u, (r, q0, k0, _) in enumerate(geo):
                dq_ref[r, pl.ds(q0, BAND_Q), :] = _join_pair(dqs[u][:BAND_Q], dqs[u][BAND_Q:])
                dk_ref[r, pl.ds(k0, BAND_KEYS), :] += dks[u]
                dv_ref[r, pl.ds(k0, BAND_KEYS), :] += dvs[u]
            return carry

        lax.fori_loop(0, L // (BAND_Q * U), step, 0)

    res = _residue_shape(DIL_GROUPS[group][1], T, F32)
    return pl.pallas_call(
        body, name=f"band_bwd_g{group}", grid=grid,
        in_specs=[spec] * 5, out_specs=[spec] * 3, out_shape=[res] * 3,
        compiler_params=_cparams(("parallel",)),
    )(q, k, v, do, dlse)


def _head_sums(t):
    head = lax.broadcasted_iota(jnp.int32, t.shape, 1) // HEAD_DIM
    out = jnp.zeros_like(t)
    for h in range(t.shape[1] // HEAD_DIM):
        mine = head == h
        out = jnp.where(mine, jnp.sum(jnp.where(mine, t, 0.0), axis=-1, keepdims=True), out)
    return out


def _dil_merge_fwd(os, lses, T, tm):
    G = len(DIL_GROUPS)
    W = DIL_OUT_WIDTH
    dils = [d for _, d in DIL_GROUPS]

    def body(*refs):
        o_refs, lse_refs = refs[:G], refs[G:2 * G]
        y_ref, w_refs, on_refs, scratch = refs[2 * G], refs[2 * G + 1:3 * G + 1], refs[3 * G + 1:4 * G + 1], refs[-1]
        o = [jnp.concatenate(_to_natural(r, scratch, d, tm), axis=1) for r, d in zip(o_refs, dils)]
        ls = [jnp.concatenate(_to_natural(r, scratch, d, tm), axis=1) for r, d in zip(lse_refs, dils)]
        m = functools.reduce(jnp.maximum, ls)
        es = [jnp.exp(l - m) for l in ls]
        tot = functools.reduce(jnp.add, es)
        ws = [e / tot for e in es]
        y_ref[...] = functools.reduce(jnp.add, [w * t for w, t in zip(ws, o)]).astype(y_ref.dtype)
        for g in range(G):
            w_refs[g][...] = ws[g]
            on_refs[g][...] = o[g]

    nat = pl.BlockSpec((tm, W), lambda i: (i, 0))
    res = pl.pallas_call(
        body, name="dil_merge_fwd", grid=(T // tm,),
        in_specs=[_residue_tile(d, tm) for d in dils] * 2,
        out_specs=[nat] * (2 * G + 1),
        out_shape=[jax.ShapeDtypeStruct((T, W), BF)] + [jax.ShapeDtypeStruct((T, W), F32)] * (2 * G),
        scratch_shapes=[pltpu.VMEM((tm, 128), F32)],
        compiler_params=_cparams(("parallel",)),
    )(*os, *lses)
    return res[0], res[1:G + 1], res[G + 1:]


def _dil_merge_bwd(dy, os, ws, tm, after=()):
    G = len(DIL_GROUPS)
    T, W = dy.shape
    dils = [d for _, d in DIL_GROUPS]
    n_after = len(after)

    def body(*refs):
        dyt = refs[0][...]
        o, w = [r[...] for r in refs[1:G + 1]], [r[...] for r in refs[G + 1:2 * G + 1]]
        refs = refs[2 * G + 1 + n_after:]
        do_refs, dlse_refs, scratch = refs[:G], refs[G:2 * G], refs[-1]
        dws = [_head_sums(dyt * t) for t in o]
        mean = functools.reduce(jnp.add, [a * b for a, b in zip(w, dws)])
        for g, d in enumerate(dils):
            do, dlse = w[g] * dyt, w[g] * (dws[g] - mean)
            for pair in range(DIL_PAIRS):
                cols = slice(pair * 128, (pair + 1) * 128)
                _from_natural(do[:, cols], scratch, do_refs[g], pair, d, tm)
                _from_natural(dlse[:, cols], scratch, dlse_refs[g], pair, d, tm)

    nat = pl.BlockSpec((tm, W), lambda i: (i, 0))
    res = pl.pallas_call(
        body, name="dil_merge_bwd", grid=(T // tm,),
        in_specs=[nat] * (2 * G + 1) + [ANY] * n_after,
        out_specs=[_residue_tile(d, tm) for d in dils] * 2,
        out_shape=[_residue_shape(d, T, BF) for d in dils] + [_residue_shape(d, T, F32) for d in dils],
        scratch_shapes=[pltpu.VMEM((tm, 128), F32)],
        compiler_params=_cparams(("parallel",)),
    )(dy, *os, *ws, *after)
    return res[:G], res[G:]


def _qkv_prep(z, cos2, sin_signed, tm):
    T = z.shape[0]
    G = len(DIL_GROUPS)
    dils = [d for _, d in DIL_GROUPS]
    n_dil_blocks = 3 * DIL_WIDTH // 128

    def body(*refs):
        blocks = refs[:n_dil_blocks]
        cos_ref, sin_ref = refs[n_dil_blocks], refs[1 + n_dil_blocks]
        outs = refs[2 + n_dil_blocks:]
        for part in range(3):
            for g, d in enumerate(dils):
                out = outs[g * 3 + part]
                for pair in range(DIL_PAIRS):
                    blk = blocks[part * (DIL_WIDTH // 128) + g * DIL_PAIRS + pair]
                    for r in range(d):
                        rows = pl.ds(r, tm // d, stride=d) if d > 1 else slice(None)
                        x = blk[rows, :]
                        if part < 2:
                            x = _rope(x, cos_ref[rows, :], sin_ref[rows, :])
                        if part == 0:
                            x = x * Q_SCALE
                        out[pair, r] = x.astype(out.dtype)

    lane_block = [pl.BlockSpec((tm, 128), functools.partial(lambda b, i: (i, b), b)) for b in range(n_dil_blocks)]
    tab = pl.BlockSpec((tm, 128), lambda i: (i, 0))
    res = pl.pallas_call(
        body, name="qkv_prep", grid=(T // tm,),
        in_specs=lane_block + [tab, tab],
        out_specs=[_residue_tile(d, tm) for d in dils for _ in range(3)],
        out_shape=[_residue_shape(d, T, BF) for d in dils for _ in range(3)],
        compiler_params=_cparams(("parallel",)),
    )(*[z] * n_dil_blocks, cos2, sin_signed)
    return [res[3 * g:3 + 3 * g] for g in range(G)]


def _qkv_unprep(d_na, d_dil, cos2, sin_signed, tm, after=()):
    T = d_na[0].shape[0]
    G = len(DIL_GROUPS)
    dils = [d for _, d in DIL_GROUPS]
    n_after = len(after)

    def body(*refs):
        dq, dk, dv = (r[...] for r in refs[:3])
        res_refs = refs[3:3 + 3 * G]
        cs, sn = refs[3 + 3 * G][...], refs[4 + 3 * G][...]
        out, scratch = refs[5 + 3 * G + n_after], refs[-1]
        cols = [dq * Q_SCALE, dk, dv]
        for part in range(3):
            for g, d in enumerate(dils):
                for x in _to_natural(res_refs[g * 3 + part], scratch, d, tm):
                    if part < 2:
                        x = _rope(x, cs, -sn)
                    cols.append(x * Q_SCALE if part == 0 else x)
        out[...] = jnp.concatenate(cols, axis=1).astype(out.dtype)

    wide = pl.BlockSpec((tm, NA_WIDTH), lambda i: (i, 0))
    tab = pl.BlockSpec((tm, 128), lambda i: (i, 0))
    return pl.pallas_call(
        body, name="qkv_unprep", grid=(T // tm,),
        in_specs=[wide] * 3 + [_residue_tile(d, tm) for d in dils for _ in range(3)] + [tab, tab] + [ANY] * n_after,
        out_specs=pl.BlockSpec((tm, QKV_WIDTH), lambda i: (i, 0)),
        out_shape=jax.ShapeDtypeStruct((T, QKV_WIDTH), BF),
        scratch_shapes=[pltpu.VMEM((tm, 128), F32)],
        compiler_params=_cparams(("parallel",)),
    )(*d_na, *[t for g in range(G) for t in d_dil[g]], cos2, sin_signed, *after)


def _rope_tables(positions):
    half = HEAD_DIM // 2
    inv_freq = ROPE_THETA ** (-jnp.arange(half, dtype=F32) / half)
    ang = positions.astype(F32)[:, None] * inv_freq
    cos, sin = jnp.cos(ang), jnp.sin(ang)
    return jnp.tile(jnp.concatenate([cos, cos], axis=1), (1, 2)), jnp.tile(jnp.concatenate([-sin, sin], axis=1), (1, 2))


def _pack_rows(t):
    return t.reshape(-1, PACK_W)


def _me():
    return lax.axis_index("x"), lax.axis_index("y"), lax.axis_index("c")


def _other_chips(x, y):
    return [(1 - x, y), (x, 1 - y), (1 - x, 1 - y)]


def _pair_sum(g, got, tm):
    S, R, W = g.shape
    half = R // 2
    nb = half // tm

    def body(pos_ref, g_ref, got_ref, own_ref, ob_ref):
        tot = g_ref[...] + got_ref[...]
        ob_ref[...] = tot.astype(ob_ref.dtype)

        @pl.when(pl.program_id(1) == pos_ref[1])
        def _():
            own_ref[...] = tot

    tile = pl.BlockSpec((None, tm, W), lambda i, s, pos_ref: (s, i, 0))
    c, chip = lax.axis_index("c"), 2 * lax.axis_index("x") + lax.axis_index("y")
    return pl.pallas_call(
        body, name="pair_sum",
        grid_spec=pltpu.PrefetchScalarGridSpec(
            num_scalar_prefetch=1, grid=(nb, S),
            in_specs=[pl.BlockSpec((None, tm, W), lambda i, s, pos_ref: (s, pos_ref[0] * nb + i, 0)), tile],
            out_specs=[pl.BlockSpec((tm, W), lambda i, s, pos_ref: (i, 0)), tile]),
        out_shape=[jax.ShapeDtypeStruct((half, W), F32), jax.ShapeDtypeStruct((S, half, W), BF)],
        compiler_params=_cparams(("parallel", "arbitrary")),
    )(jnp.stack([c, chip]).astype(jnp.int32), g, got)


def _chip_sum(own, others, tm):
    n, h, W = others.shape
    nb = h // tm

    def body(c_ref, own_ref, p_ref, o_ref):
        o_ref[...] = ((own_ref[...] + p_ref[0].astype(F32)) + p_ref[1].astype(F32)) + p_ref[2].astype(F32)

    return pl.pallas_call(
        body, name="chip_sum",
        grid_spec=pltpu.PrefetchScalarGridSpec(
            num_scalar_prefetch=1, grid=(nb,),
            in_specs=[pl.BlockSpec((tm, W), lambda i, c_ref: (i, 0)), pl.BlockSpec((n, tm, W), lambda i, c_ref: (0, i, 0))],
            out_specs=pl.BlockSpec((tm, W), lambda i, c_ref: (c_ref[0] * nb + i, 0))),
        out_shape=jax.ShapeDtypeStruct((2 * h, W), F32),
        compiler_params=_cparams(("parallel",)),
    )(lax.axis_index("c").reshape(1).astype(jnp.int32), own, others)


def _join_halves(shard):
    h = shard.shape[0] // 2

    def body(in_ref, out_ref, send_sem, recv_sem):
        x, y, c = _me()
        cp = pltpu.make_async_remote_copy(
            src_ref=in_ref.at[pl.ds(c * h, h), :], dst_ref=out_ref.at[pl.ds(c * h, h), :],
            send_sem=send_sem, recv_sem=recv_sem, device_id=(x, y, 1 - c), device_id_type=MESH)
        cp.start()
        pltpu.make_async_remote_copy(
            src_ref=in_ref.at[pl.ds(c * h, h), :], dst_ref=out_ref.at[pl.ds((1 - c) * h, h), :],
            send_sem=send_sem, recv_sem=recv_sem, device_id=(x, y, 1 - c), device_id_type=MESH).wait_recv()
        cp.wait_send()

    return pl.pallas_call(
        body, name="join_halves", in_specs=[ANY], out_specs=ANY,
        out_shape=jax.ShapeDtypeStruct(shard.shape, shard.dtype), input_output_aliases={0: 0},
        scratch_shapes=[pltpu.SemaphoreType.DMA, pltpu.SemaphoreType.DMA],
    )(shard)


def _allreduce_small(s, after=()):
    R, W = s.shape
    n_after = len(after)

    def body(s_ref, *rest):
        o_ref, buf, send_sems, recv_sems = rest[n_after:]
        x, y, c = _me()
        me = 4 * x + 2 * y + c
        buf[me] = s_ref[...]
        peers = [((x + fx) % 2, (y + fy) % 2, (c + fc) % 2) for fx in range(2) for fy in range(2) for fc in range(2)][1:]
        sends = [pltpu.make_async_remote_copy(
            src_ref=s_ref, dst_ref=buf.at[me], send_sem=send_sems.at[k], recv_sem=recv_sems.at[k],
            device_id=peer, device_id_type=MESH) for k, peer in enumerate(peers)]
        for cp in sends:
            cp.start()
        for k, peer in enumerate(peers):
            pltpu.make_async_remote_copy(
                src_ref=s_ref, dst_ref=buf.at[4 * peer[0] + 2 * peer[1] + peer[2]], send_sem=send_sems.at[k],
                recv_sem=recv_sems.at[k], device_id=peer, device_id_type=MESH).wait_recv()
        for cp in sends:
            cp.wait_send()
        total = buf[0]
        for d in range(1, N_DEV):
            total = total + buf[d]
        o_ref[...] = total

    return pl.pallas_call(
        body, name="allreduce_small",
        in_specs=[pl.BlockSpec(memory_space=pltpu.VMEM)] + [ANY] * n_after, out_specs=pl.BlockSpec(memory_space=pltpu.VMEM),
        out_shape=jax.ShapeDtypeStruct((R, W), F32),
        scratch_shapes=[pltpu.VMEM((N_DEV, R, W), F32), pltpu.SemaphoreType.DMA((N_DEV - 1,)), pltpu.SemaphoreType.DMA((N_DEV - 1,))],
    )(s, *after)


HBM_SPEC = pl.BlockSpec(memory_space=pltpu.HBM)
SEM_SPEC = pl.BlockSpec(memory_space=pltpu.SEMAPHORE)
DATAFLOW = pltpu.SideEffectType.DATAFLOW_SIDE_EFFECTING


class _InFlight(NamedTuple):
    sems: tuple
    src: jax.Array
    land: jax.Array
    token: jax.Array


def _split_start(name, src, land_shape, land_dtype, n, copies, after=()):
    n_after = len(after)

    def body(src_ref, land_ref, *rest):
        rest = rest[n_after:]
        sems, token = rest[:2 * n], rest[-1]
        for k, (s, d, peer) in enumerate(copies(src_ref, land_ref)):
            pltpu.make_async_remote_copy(src_ref=s, dst_ref=d, send_sem=sems[k], recv_sem=sems[n + k],
                                         device_id=peer, device_id_type=MESH).start()
        token[...] = jnp.zeros_like(token)

    outs = pl.pallas_call(
        body, name=name,
        out_shape=(*[pltpu.SemaphoreType.DMA(())] * (2 * n), pltpu.HBM(src.shape, src.dtype), pltpu.HBM(land_shape, land_dtype),
                   jax.ShapeDtypeStruct((8, 128), F32)),
        in_specs=(HBM_SPEC, HBM_SPEC, *[ANY] * n_after),
        out_specs=(*[SEM_SPEC] * (2 * n), HBM_SPEC, HBM_SPEC, pl.BlockSpec(memory_space=pltpu.VMEM)),
        input_output_aliases={0: 2 * n, 1: 2 * n + 1},
        compiler_params=pltpu.CompilerParams(has_side_effects=DATAFLOW),
    )(pltpu.with_memory_space_constraint(src, pltpu.HBM), pltpu.with_memory_space_constraint(lax.empty(land_shape, land_dtype), pltpu.HBM),
      *after)
    return _InFlight(tuple(outs[:2 * n]), outs[2 * n], outs[2 * n + 1], outs[2 * n + 2])


def _split_wait(name, flight, after, n, copies):
    def body(src_ref, land_ref, *rest):
        sems = rest[:2 * n]
        for k, (s, d, peer) in enumerate(copies(src_ref, land_ref)):
            cp = pltpu.make_async_remote_copy(src_ref=s, dst_ref=d, send_sem=sems[k], recv_sem=sems[n + k],
                                              device_id=peer, device_id_type=MESH)
            cp.wait_send()
            cp.wait_recv()

    return pl.pallas_call(
        body, name=name,
        out_shape=(pltpu.HBM(flight.src.shape, flight.src.dtype), pltpu.HBM(flight.land.shape, flight.land.dtype)),
        in_specs=(HBM_SPEC, HBM_SPEC, *[SEM_SPEC] * (2 * n), ANY),
        out_specs=(HBM_SPEC, HBM_SPEC), input_output_aliases={0: 0, 1: 1},
        compiler_params=pltpu.CompilerParams(has_side_effects=DATAFLOW),
    )(flight.src, flight.land, *flight.sems, after)


def _gather_copies(src_ref, land_ref):
    x, y, c = _me()
    return [(src_ref, land_ref.at[2 * x + y], (*chip, c)) for chip in _other_chips(x, y)]


def _gather_start(packed, tag, after=()):
    return _split_start(f"gather_start_{tag}", packed, (N_CHIPS, *packed.shape), packed.dtype, 3, _gather_copies, after)


def _gather_wait(flight, after, tag):
    src, others = _split_wait(f"gather_wait_{tag}", flight, after, 3, _gather_copies)
    return lax.dynamic_update_slice(others, src[None], (2 * lax.axis_index("x") + lax.axis_index("y"), 0, 0))


def _across_copies(src_ref, land_ref):
    x, y, c = _me()
    half = src_ref.shape[0] // 2
    rows = pl.ds(c * half, half)
    return [(src_ref.at[rows, :], land_ref.at[2 * x + y, rows, :], (*chip, c)) for chip in _other_chips(x, y)]


def _to_sibling_copies(all_ref, unused_ref):
    x, y, c = _me()
    half = all_ref.shape[1] // 2
    places = [all_ref.at[2 * chip[0] + chip[1], pl.ds(c * half, half), :] for chip in _other_chips(x, y)]
    return [(place, place, (x, y, 1 - c)) for place in places]


def _gather_halves_start(shard):
    return _split_start("gather_in_across_start", shard, (N_CHIPS, *shard.shape), shard.dtype, 3, _across_copies)


def _gather_halves_finish(flight, after):
    shard, landed = _split_wait("gather_in_across_wait", flight, after, 3, _across_copies)
    forward = _split_start("gather_in_sibling_start", landed, (8, 128), landed.dtype, 3, _to_sibling_copies)
    others = _split_wait("gather_in_sibling_wait", forward, forward.token, 3, _to_sibling_copies)[0]
    return lax.dynamic_update_slice(others, shard[None], (2 * lax.axis_index("x") + lax.axis_index("y"), 0, 0))


def _swap_copies(src_ref, land_ref):
    x, y, c = _me()
    half = land_ref.shape[1]
    return [(src_ref.at[:, pl.ds((1 - c) * half, half), :], land_ref, (x, y, 1 - c))]


def _swap_start(g, tag):
    S, R, W = g.shape
    return _split_start(f"swap_halves_start_{tag}", g, (S, R // 2, W), g.dtype, 1, _swap_copies)


def _swap_wait(flight, after, tag):
    return _split_wait(f"swap_halves_wait_{tag}", flight, after, 1, _swap_copies)


def _scatter_copies(src_ref, land_ref):
    x, y, c = _me()
    return [(src_ref.at[2 * chip[0] + chip[1]], land_ref.at[j], (*chip, c)) for j, chip in enumerate(_other_chips(x, y))]


def _scatter_start(part, tag):
    S, h, W = part.shape
    return _split_start(f"scatter_chips_start_{tag}", part, (S - 1, h, W), part.dtype, 3, _scatter_copies)


def _scatter_wait(flight, after, tag):
    return _split_wait(f"scatter_chips_wait_{tag}", flight, after, 3, _scatter_copies)[1]


def _join_copies(shard_ref, unused_ref):
    x, y, c = _me()
    h = shard_ref.shape[0] // 2
    rows = shard_ref.at[pl.ds(c * h, h), :]
    return [(rows, rows, (x, y, 1 - c))]


def _join_start(shard):
    return _split_start("join_halves_start", shard, (8, 128), shard.dtype, 1, _join_copies)


def _join_wait(flight, after):
    return _split_wait("join_halves_wait", flight, after, 1, _join_copies)[0]


def _adamw(name, g, g_row0, w, m, v):
    _, R, C = w.shape
    tm = next(cand for cand in (368, 256, 128, 64, 32, 16, 8) if R % cand == 0)
    assert g_row0 % tm == 0 and g.shape[1] == C

    def body(g_ref, w_ref, m_ref, v_ref, go_ref, d_ref, mo_ref, vo_ref):
        gt = g_ref[...]
        mt = ADAM_B1 * m_ref[...] + (1.0 - ADAM_B1) * gt
        vt = ADAM_B2 * v_ref[...] + (1.0 - ADAM_B2) * jnp.square(gt)
        m_hat = mt / (1.0 - ADAM_B1 ** ADAM_STEP)
        v_hat = vt / (1.0 - ADAM_B2 ** ADAM_STEP)
        go_ref[...] = gt
        d_ref[...] = -ADAM_LR * (m_hat / (jnp.sqrt(v_hat) + ADAM_EPS) + ADAM_WD * w_ref[...])
        mo_ref[...] = mt
        vo_ref[...] = vt

    state = pl.BlockSpec((None, tm, C), lambda i: (0, i, 0))
    return pl.pallas_call(
        body, name=name, grid=(R // tm,),
        in_specs=[pl.BlockSpec((tm, C), lambda i: (g_row0 // tm + i, 0)), state, state, state],
        out_specs=[state] * 4, out_shape=[jax.ShapeDtypeStruct((1, R, C), F32)] * 4,
        compiler_params=_cparams(("parallel",)),
    )(g, w, m, v)


def _unpack_weights(gathered, names):
    S = gathered.shape[0]
    shard_shapes = {"w_in": (D_MODEL, (QKV_WIDTH + 2 * D_MODEL) // S), "w_branch_na": (NA_WIDTH, D_MODEL // S),
                    "w_branch_dil": (DIL_OUT_WIDTH, D_MODEL // S), "w_out": (D_MODEL // S, D_MODEL),
                    "w_up": (D_MODEL, D_FF // S), "w_down": (D_FF // S, D_MODEL),
                    "w_ple_gate": (D_MODEL // S, D_MODEL), "w_ple_proj": (PLE_DIM, D_MODEL // S)}
    col_sharded = {"w_in", "w_branch_na", "w_branch_dil", "w_up", "w_ple_proj"}
    out, r0 = {}, 0
    for name in names:
        rows, cols = shard_shapes[name]
        n = rows * cols // PACK_W
        t = gathered[:, r0:r0 + n, :].reshape(S, rows, cols)
        r0 += n
        out[name] = t.transpose(1, 0, 2).reshape(rows, S * cols) if name in col_sharded else t.reshape(S * rows, cols)
    return out


def kernel(x, p, positions, g_mix, w_in, rpb, w_branch_na, w_branch_dil, w_out, g_mlp, w_up, w_down, g_ple, w_ple_gate, w_ple_proj, g_final, loss_target, m_g_mix, m_w_in, m_rpb, m_w_branch_na, m_w_branch_dil, m_w_out, m_g_mlp, m_w_up, m_w_down, m_g_ple, m_w_ple_gate, m_w_ple_proj, m_g_final, v_g_mix, v_w_in, v_rpb, v_w_branch_na, v_w_branch_dil, v_w_out, v_g_mlp, v_w_up, v_w_down, v_g_ple, v_w_ple_gate, v_w_ple_proj, v_g_final):
    shards = {"w_in": w_in[0], "w_branch_na": w_branch_na[0], "w_branch_dil": w_branch_dil[0], "w_out": w_out[0],
              "w_up": w_up[0], "w_down": w_down[0], "w_ple_gate": w_ple_gate[0], "w_ple_proj": w_ple_proj[0]}
    params = {"w_in": w_in, "w_branch_na": w_branch_na, "w_branch_dil": w_branch_dil, "w_out": w_out, "w_up": w_up,
              "w_down": w_down, "w_ple_gate": w_ple_gate, "w_ple_proj": w_ple_proj,
              "m_w_in": m_w_in, "m_w_branch_na": m_w_branch_na, "m_w_branch_dil": m_w_branch_dil, "m_w_out": m_w_out,
              "m_w_up": m_w_up, "m_w_down": m_w_down, "m_w_ple_gate": m_w_ple_gate, "m_w_ple_proj": m_w_ple_proj,
              "v_w_in": v_w_in, "v_w_branch_na": v_w_branch_na, "v_w_branch_dil": v_w_branch_dil, "v_w_out": v_w_out,
              "v_w_up": v_w_up, "v_w_down": v_w_down, "v_w_ple_gate": v_w_ple_gate, "v_w_ple_proj": v_w_ple_proj}

    xs, ps, tgt = x[0], p[0, 0], loss_target[0]
    T = xs.shape[0]
    TM = 512
    gm, gl, gp, gf = g_mix, g_mlp, g_ple, g_final.reshape(1, D_MODEL)

    across = _gather_halves_start(shards["w_in"].astype(BF))
    a = _rowwise("norm_mix", lambda h, g: h * _rms(h) * g, T, TM, [_row(xs, TM), _full(gm)], [(D_MODEL, BF)],
                 after=(across.token,))
    cos2, sin_signed = _rope_tables(positions[0])
    tab = _na_bias_table(rpb[0])
    w_in_all = _gather_halves_finish(across, a)
    W = {"w_in": w_in_all.transpose(1, 0, 2).reshape(D_MODEL, -1)}
    mix_flight = _gather_start(jnp.concatenate([_pack_rows(shards[n].astype(BF)) for n in GATHER_MIX], axis=0), "mix",
                               after=(w_in_all,))
    rest_flight = _gather_start(jnp.concatenate([_pack_rows(shards[n].astype(BF)) for n in GATHER_MLP], axis=0), "mlp",
                                after=(mix_flight.token,))
    w_gates = W["w_in"][:, QKV_WIDTH:]

    n3 = 3 * NA_WIDTH
    qkv = _mm("in_na", a, W["w_in"], "nn", 1024, 768, 1024, [BF], after=(rest_flight.token,),
              b_view=(n3, (D_MODEL, 768), lambda j, k: (k, j)))
    z_dil = _mm("in_dil", a, W["w_in"], "nn", 1024, 768, 1024, [F32],
                b_view=(3 * DIL_WIDTH, (D_MODEL, 768), lambda j, k: (k, n3 // 768 + j)))
    z_gates = _mm("in_gates", a, w_gates, "nn", 1024,1024, 1024, [BF])

    dil_ops = _qkv_prep(z_dil, cos2, sin_signed, TM)
    y_na = _na_fwd(qkv, tab)
    band = [_band_fwd(*dil_ops[g], g) for g in range(len(DIL_GROUPS))]
    y_dil, w_grp, o_nat = _dil_merge_fwd([b[0] for b in band], [b[1] for b in band], T, TM)

    W.update(_unpack_weights(_gather_wait(mix_flight, y_dil, "mix"), GATHER_MIX))
    u_na = _mm("branch_na", y_na, W["w_branch_na"], "nn", 1024,1024, 512, [BF])
    u_dil = _mm("branch_dil", y_dil, W["w_branch_dil"], "nn", 1024,1024, 256, [BF])
    mixed = _rowwise(
        "gate_mix", lambda gn, gd, un, ud: _sigmoid(gn.astype(F32)) * un.astype(F32) + _sigmoid(gd.astype(F32)) * ud.astype(F32), T, TM,
        [_row(z_gates, TM, 0, D_MODEL), _row(z_gates, TM, 1, D_MODEL), _row(u_na, TM), _row(u_dil, TM)], [(D_MODEL, BF)])
    def add_norm(d, h, g):
        h = h + d
        return h, h * _rms(h) * g

    h1, cn = _mm("out_proj", mixed, W["w_out"], "nn", 512, 1024, 1024, [F32, BF], epilogue=add_norm, extras=(xs,), consts=(gl,))
    mlp_all = _gather_wait(rest_flight, cn, "mlp")
    W.update({n: t for n, t in _unpack_weights(mlp_all, GATHER_MLP).items() if n.startswith("w_ple")})
    chip_block = (None, D_MODEL, PACK_W)
    up, act = _mm("mlp_up", cn, mlp_all, "nn", 1024,1024, 1024, [BF, BF],
                  epilogue=lambda acc: (acc, jnp.square(jnp.maximum(acc, 0.0))), b_view=(D_FF, chip_block, lambda j, k: (j, 0, 0)))
    h2, en = _mm("mlp_down", act, mlp_all, "nn", 1024, 1024, 1024, [F32, BF], epilogue=add_norm, extras=(h1,), consts=(gp,),
                 b_view=(D_MODEL, chip_block, lambda j, k: (k, 1, 0)))
    pp = _mm("ple_proj", ps, W["w_ple_proj"], "nn", 1024,1024, 256, [F32])

    def head(gtt, h2t, ppt, tg, g):
        sg = _sigmoid(gtt)
        h3 = h2t + sg * ppt
        yo = h3 * _rms(h3) * g
        diff = yo - tg
        loss = 0.5 * jnp.sum(jnp.mean(jnp.square(diff), axis=-1, keepdims=True), axis=0, keepdims=True)
        dh3, dg = _rms_bwd(diff * (1.0 / D_MODEL), h3, g)
        return dh3, dh3 * ppt * sg * (1.0 - sg), dh3 * sg, jnp.broadcast_to(loss, (1, 128)), dg

    dh3, d_gt, d_pp, loss_part, dg_final = _mm(
        "ple_gate_loss_head", en, W["w_ple_gate"], "nn", 512, 1024, 1024, [F32, BF, BF], epilogue=head,
        extras=(h2, pp, tgt), consts=(gf,), sums=[128, D_MODEL])

    early_shapes = {n: shards[n].shape for n in REDUCE_EARLY}
    early_rows = sum(r * c for r, c in early_shapes.values()) // PACK_W
    shard_rows = D_MODEL // N_CHIPS
    early_buf = _mm("g_ple_gate", en, d_gt, "tn", 1024, 1024, 1024, [F32],
                    into=(jax.ShapeDtypeStruct((N_CHIPS, early_rows, PACK_W), F32), (N_CHIPS, shard_rows, PACK_W),
                          lambda i, j: (0, 2 * D_MODEL // shard_rows, 0)))
    g_ple_proj = _mm("g_ple_proj", ps, d_pp, "tn", 256, 1024, 1024,[F32])

    def add_norm_bwd(dn, dh_out, h, g):
        dh, dg = _rms_bwd(dn, h, g)
        dh = dh_out + dh
        return dh, dh, dg

    dh2, dh2_b, dg_ple = _mm("d_ple_gate", d_gt, W["w_ple_gate"], "nt", 512, 1024, 1024, [F32, BF],
                             epilogue=add_norm_bwd, extras=(dh3, h2), consts=(gp,), sums=[D_MODEL])
    d_up = _mm("d_mlp_down", dh2_b, mlp_all, "nt", 1024,1024, 1024, [BF], b_view=(D_FF, chip_block, lambda j, k: (j, 1, 0)),
               epilogue=lambda acc, u: (acc * (2.0 * jnp.maximum(u.astype(F32), 0.0)),), extras=(up,))
    early_buf = _mm("g_mlp_down", act, dh2_b, "tn", 1024, 1024, 1024,[F32],
                    into=(early_buf, (None, D_MODEL, PACK_W), lambda i, j: (i, 1, 0)))
    early_buf = _mm("g_mlp_up", cn, d_up, "tn", 1024, 1024, 1024,[F32],
                    into=(early_buf, (None, D_MODEL, PACK_W), lambda i, j: (j, 0, 0)))
    dh1, dh1_b, dg_mlp = _mm("d_mlp_up", d_up, mlp_all, "nt", 1024, 1024, 1024, [F32, BF], epilogue=add_norm_bwd,
                             b_view=(D_MODEL, chip_block, lambda j, k: (k, 0, 0)),
                             extras=(dh2, h1), consts=(gl,), sums=[D_MODEL])
    d_mixed = _mm("d_out_proj", dh1_b, W["w_out"], "nt", 1024,1024, 1024, [F32])
    early_buf = _mm("g_out_proj", mixed, dh1_b, "tn", 1024, 1024, 1024, [F32],
                    into=(early_buf, (N_CHIPS, shard_rows, PACK_W), lambda i, j: (0, 2 * D_MODEL // shard_rows + 1, 0)))

    def gate_bwd(dm, gn, gd, un, ud):
        gn, gd, un, ud = (t.astype(F32) for t in (gn, gd, un, ud))
        sn, sd = _sigmoid(gn), _sigmoid(gd)
        return jnp.concatenate([dm * un * sn * (1.0 - sn), dm * ud * sd * (1.0 - sd)], axis=1), dm * sn, dm * sd

    dz_gates, d_u_na, d_u_dil = _rowwise(
        "gate_mix_bwd", gate_bwd, T, TM,
        [_row(d_mixed, TM), _row(z_gates, TM, 0, D_MODEL), _row(z_gates, TM, 1, D_MODEL), _row(u_na, TM), _row(u_dil, TM)],
        [(2 * D_MODEL, BF), (D_MODEL, BF), (D_MODEL, BF)])
    g_branch_na = _mm("g_branch_na", y_na, d_u_na, "tn", 1024, 1024, 1024,[F32])
    g_branch_dil = _mm("g_branch_dil", y_dil, d_u_dil, "tn", 256, 1024, 1024,[F32])
    small_rows = [jnp.concatenate([_pack_rows(g[:, s * shard_rows:(s + 1) * shard_rows]) for g in (g_ple_proj, g_branch_na, g_branch_dil)],
                                  axis=0) for s in range(N_CHIPS)]
    early_buf = lax.dynamic_update_slice(early_buf, jnp.stack(small_rows), (0, 2 * D_MODEL + 2 * shard_rows, 0))
    early_tm = early_rows // 4
    swap_flight = _swap_start(early_buf, "early")
    d_y_na = _mm("d_branch_na", d_u_na, W["w_branch_na"], "nt", 1024,512, 1024, [BF], after=(swap_flight.token,))
    d_y_dil = _mm("d_branch_dil", d_u_dil, W["w_branch_dil"], "nt", 1024,256, 1024, [F32])

    dqa, dka, dva, dtab = _na_bwd(qkv, tab, d_y_na)
    early_g, early_got = _swap_wait(swap_flight, dqa, "early")
    early_pair, early_pair_b = _pair_sum(early_g, early_got, early_tm)
    scatter_flight = _scatter_start(early_pair_b, "early")

    do_res, dlse_res = _dil_merge_bwd(d_y_dil, o_nat, w_grp, TM, after=(scatter_flight.token,))
    d_dil = [_band_bwd(*dil_ops[g], do_res[g], dlse_res[g], g) for g in range(len(DIL_GROUPS))]

    dz_qkv = _qkv_unprep((dqa, dka, dva), d_dil, cos2, sin_signed, TM)
    g_in_parts = [_mm("g_in_qkv", a, dz_qkv, "tn", 1024, 1280, 1024,[F32]), _mm("g_in_gates", a, dz_gates, "tn", 1024, 1024, 1024,[F32])]
    early_mine = _chip_sum(early_pair, _scatter_wait(scatter_flight, g_in_parts[1], "early"), early_tm)
    join_flight = _join_start(early_mine)
    in_cols = shards["w_in"].shape[1]

    def owner_columns(s):
        lo, hi, split = s * in_cols, (s + 1) * in_cols, g_in_parts[0].shape[1]
        pieces = [g_in_parts[0][:, lo:min(hi, split)]] if lo < split else []
        pieces += [g_in_parts[1][:, max(lo, split) - split:hi - split]] if hi > split else []
        return pieces[0] if len(pieces) == 1 else jnp.concatenate(pieces, axis=1)

    late_tm = in_cols // 4
    late_swap = _swap_start(jnp.stack([owner_columns(s).T for s in range(N_CHIPS)]), "late")
    d_a = _mm("d_in_qkv", dz_qkv, W["w_in"], "nt", 1024,1024, 1280, [F32], after=(late_swap.token, join_flight.token),
              b_view=(D_MODEL, (D_MODEL, 1280), lambda j, k: (j, k)))
    late_g, late_got = _swap_wait(late_swap, d_a, "late")
    late_pair, late_pair_b = _pair_sum(late_g, late_got, late_tm)
    late_scatter = _scatter_start(late_pair_b, "late")
    d_rpb = _na_rpb_grad(dtab, after=(late_scatter.token,))[:, :2 * NA_WIN_ROWS - 1, :2 * NA_WIN_COLS - 1]
    def first_bwd(dn_gates, dn_qkv, dh_out, h, g):
        dh, dg = _rms_bwd(dn_gates + dn_qkv, h, g)
        return dh_out + dh, dg

    grad_x, dg_mix = _mm("d_in_gates", dz_gates, w_gates, "nt", 512, 1024, 1024, [F32], epilogue=first_bwd,
                         extras=(d_a, dh1, xs), consts=(gm,), sums=[D_MODEL], after=(late_scatter.token,))
    early_shard = _join_wait(join_flight, grad_x)

    n_rpb = rpb.size
    rpb_rows = 4
    small = jnp.concatenate([
        dg_mix, dg_mlp, dg_ple, dg_final,
        jnp.pad(d_rpb.reshape(-1), (0, rpb_rows * D_MODEL - n_rpb)).reshape(rpb_rows, D_MODEL),
        jnp.pad(loss_part, ((0, 0), (0, D_MODEL - loss_part.shape[1]))),
        jnp.zeros((SMALL_ROWS - 5 - rpb_rows, D_MODEL), F32)], axis=0)
    out = {"grad": {}, "delta": {}, "new_m": {}, "new_v": {}}

    def update(n, g, row0):
        res = _adamw("adamw_" + n, g, row0, params[n], params["m_" + n], params["v_" + n])
        for kind, t in zip(("grad", "delta", "new_m", "new_v"), res, strict=True):
            out[kind][n] = t

    row0 = 0
    for n in REDUCE_EARLY:
        rows, cols = early_shapes[n]
        n_rows = rows * cols // PACK_W
        if cols == PACK_W:
            update(n, early_shard, row0)
        else:
            update(n, early_shard[row0:row0 + n_rows].reshape(rows, cols), 0)
        row0 += n_rows
    late_others = _scatter_wait(late_scatter, out["new_v"][REDUCE_EARLY[-1]], "late")
    late_mine = _chip_sum(late_pair, late_others, late_tm)
    small = _allreduce_small(small, after=(late_mine,))
    res = _adamw("adamw_w_in", _join_halves(late_mine), 0, *[jnp.swapaxes(params[n], 1, 2) for n in ("w_in", "m_w_in", "v_w_in")])
    for kind, t in zip(("grad", "delta", "new_m", "new_v"), res, strict=True):
        out[kind]["w_in"] = jnp.swapaxes(t, 1, 2)
    loss = small[4 + rpb_rows, 0]

    def small_pack(a0, a1, a2, a3, r):
        return jnp.concatenate([a0.reshape(1, -1), a1.reshape(1, -1), a2.reshape(1, -1), a3.reshape(1, -1),
                                jnp.pad(r.reshape(-1), (0, rpb_rows * D_MODEL - n_rpb)).reshape(rpb_rows, D_MODEL)], axis=0)

    small_res = _adamw("adamw_small", small, 0, small_pack(g_mix, g_mlp, g_ple, g_final, rpb)[None],
                       small_pack(m_g_mix, m_g_mlp, m_g_ple, m_g_final, m_rpb)[None],
                       small_pack(v_g_mix, v_g_mlp, v_g_ple, v_g_final, v_rpb)[None])

    def small_unpack(t):
        return {"g_mix": t[0].reshape(g_mix.shape), "g_mlp": t[1].reshape(g_mlp.shape), "g_ple": t[2].reshape(g_ple.shape),
                "g_final": t[3].reshape(g_final.shape), "rpb": t[4:].reshape(-1)[:n_rpb].reshape(rpb.shape)}

    for kind, t in zip(("grad", "delta", "new_m", "new_v"), small_res, strict=True):
        out[kind].update(small_unpack(t[0]))

    order = ["g_mix", "w_in", "rpb", "w_branch_na", "w_branch_dil", "w_out", "g_mlp", "w_up", "w_down", "g_ple",
             "w_ple_gate", "w_ple_proj", "g_final"]
    return (loss, grad_x[None], *[out["grad"][n] for n in order], *[out["delta"][n] for n in order],
            *[out["new_m"][n] for n in order], *[out["new_v"][n] for n in order])
```

```python
import functools
from typing import NamedTuple

import jax
import jax.numpy as jnp
from jax import lax
from jax.experimental import pallas as pl
from jax.experimental.pallas import tpu as pltpu

BF = jnp.bfloat16
F32 = jnp.float32
MESH = pl.DeviceIdType.MESH
ANY = pl.BlockSpec(memory_space=pl.ANY)

V7X_VMEM_BYTES = 64 * 1024 * 1024
VMEM_LIMIT = V7X_VMEM_BYTES - 16 * 1024 * 1024

D_MODEL = 1024
HEAD_DIM = 64
GRID_W = 64
NA_HEADS = 8
NA_WIN_ROWS = 8
NA_WIN_COLS = 16
NA_WIDTH = NA_HEADS * HEAD_DIM
DIL_GROUPS = ((128, 1), (512, 4), (2048, 16))
DIL_HPG = 4
DIL_HEADS = DIL_HPG * len(DIL_GROUPS)
DIL_WIDTH = DIL_HEADS * HEAD_DIM
DIL_OUT_WIDTH = DIL_HPG * HEAD_DIM
DIL_RADIUS = 64
QKV_WIDTH = 3 * NA_WIDTH + 3 * DIL_WIDTH
D_FF = 4 * D_MODEL
PLE_DIM = 256
ROPE_THETA = 10000.0
RMS_EPS = 1e-6
NEG_INF = -1e30
Q_SCALE = HEAD_DIM ** -0.5

ADAM_LR = 0.001
ADAM_B1 = 0.9
ADAM_B2 = 0.999
ADAM_EPS = 1e-08
ADAM_WD = 0.01
ADAM_STEP = 10

N_CHIPS = 4
N_DEV = 8
PACK_W = 1024
BIG = ("w_in", "w_branch_na", "w_branch_dil", "w_out", "w_up", "w_down", "w_ple_gate", "w_ple_proj")
GATHER_MIX = ("w_branch_na", "w_branch_dil", "w_out")
GATHER_MLP = ("w_up", "w_down", "w_ple_gate", "w_ple_proj")
REDUCE_EARLY = ("w_up", "w_down", "w_ple_gate", "w_out", "w_ple_proj", "w_branch_na", "w_branch_dil")
SMALL_ROWS = 16


def _cparams(sem=None):
    return pltpu.CompilerParams(dimension_semantics=sem, vmem_limit_bytes=VMEM_LIMIT)


def _mm(name, a, b, mode, tm, tn, tk, out_dtypes, epilogue=None, extras=(), consts=(), sums=(), after=(), into=None,
        b_view=None):
    if mode == "nn":
        (M, K), N = a.shape, b.shape[1]
    elif mode == "nt":
        (M, K), N = a.shape, b.shape[0]
    else:
        (K, M), N = a.shape, b.shape[1]
    if b_view is not None:
        N = b_view[0]
    tm, tn, tk = min(tm, M), min(tn, N), min(tk, K)
    assert M % tm == 0 and N % tn == 0 and K % tk == 0, (name, M, N, K, tm, tn, tk)
    if mode == "nn":
        a_spec = pl.BlockSpec((tm, tk), lambda i, j, k: (i, k))
        b_spec = pl.BlockSpec((tk, tn), lambda i, j, k: (k, j))
        dims = (((1,), (0,)), ((), ()))
    elif mode == "nt":
        a_spec = pl.BlockSpec((tm, tk), lambda i, j, k: (i, k))
        b_spec = pl.BlockSpec((tn, tk), lambda i, j, k: (j, k))
        dims = (((1,), (1,)), ((), ()))
    else:
        a_spec = pl.BlockSpec((tk, tm), lambda i, j, k: (k, i))
        b_spec = pl.BlockSpec((tk, tn), lambda i, j, k: (k, j))
        dims = (((0,), (0,)), ((), ()))
    if b_view is not None:
        b_spec = pl.BlockSpec(b_view[1], lambda i, j, k: b_view[2](j, k))
    nk = K // tk
    n_extra, n_const, n_out, n_sum = len(extras), len(consts), len(out_dtypes), len(sums)
    tile = pl.BlockSpec((tm, tn), lambda i, j, k: (i, j))
    assert not sums or tn == N, "row sums need whole rows in a tile"

    n_after = len(after)

    def body(a_ref, b_ref, *rest):
        extra_refs, rest = rest[:n_extra + n_const], rest[n_extra + n_const + n_after:]
        out_refs, sum_refs, acc = rest[:n_out], rest[n_out:n_out + n_sum], rest[-1]
        i, k = pl.program_id(0), pl.program_id(2)
        def product():
            return lax.dot_general(a_ref[...].astype(BF), b_ref[...].astype(BF), dims, preferred_element_type=F32)

        if nk > 1:
            @pl.when(k == 0)
            def _():
                acc[...] = jnp.zeros_like(acc)

            acc[...] += product()

        @pl.when(k == nk - 1)
        def _():
            total = product() if nk == 1 else acc[...]
            outs = (total,) if epilogue is None else epilogue(total, *[e[...] for e in extra_refs])
            for o_ref, val in zip(out_refs, outs[:n_out], strict=True):
                o_ref[...] = val.astype(o_ref.dtype).reshape(o_ref.shape)
            for s_ref, val in zip(sum_refs, outs[n_out:], strict=True):
                @pl.when(i == 0)
                def _():
                    s_ref[...] = val

                @pl.when(i != 0)
                def _():
                    s_ref[...] += val

    out_specs = [tile] * n_out + [pl.BlockSpec((1, c), lambda i, j, k: (0, 0)) for c in sums]
    out_shape = [jax.ShapeDtypeStruct((M, N), dt) for dt in out_dtypes] + [jax.ShapeDtypeStruct((1, c), F32) for c in sums]
    operands, aliases = [a, b, *extras, *consts, *after], {}
    in_specs = ([a_spec, b_spec] + [tile] * n_extra
                + [pl.BlockSpec(c.shape, functools.partial(lambda nd, i, j, k: (0,) * nd, c.ndim)) for c in consts] + [ANY] * n_after)
    if into is not None:
        assert n_out == 1
        target, block, index = into
        out_specs = [pl.BlockSpec(block, lambda i, j, k: index(i, j))]
        out_shape = [jax.ShapeDtypeStruct(target.shape, target.dtype)]
        if not isinstance(target, jax.ShapeDtypeStruct):
            aliases = {len(operands): 0}
            operands.append(target)
            in_specs.append(ANY)
            n_after += 1

    outs = pl.pallas_call(
        body, name=name, grid=(M // tm, N // tn, nk),
        in_specs=in_specs, out_specs=out_specs, out_shape=out_shape,
        scratch_shapes=[pltpu.VMEM((tm, tn) if nk > 1 else (8, 128), F32)], input_output_aliases=aliases,
        compiler_params=_cparams(("arbitrary",) * 3 if sums else ("parallel", "parallel", "arbitrary")),
    )(*operands)
    return outs[0] if len(outs) == 1 else outs


def _row(arr, tm, col_block=None, width=None):
    width = arr.shape[1] if width is None else width
    cb = 0 if col_block is None else col_block
    return arr, pl.BlockSpec((tm, width), lambda i: (i, cb))


def _full(arr):
    nd = arr.ndim
    return arr, pl.BlockSpec(arr.shape, lambda i: (0,) * nd)


def _rowwise(name, body, T, tm, ins, outs, sums=(), after=()):
    n_in, n_out, n_sum, n_after = len(ins), len(outs), len(sums), len(after)

    def kern(*refs):
        in_refs, refs = refs[:n_in], refs[n_in + n_after:]
        out_refs, sum_refs = refs[:n_out], refs[n_out:]
        res = body(*[r[...] for r in in_refs])
        res = res if isinstance(res, tuple) else (res,)
        for o_ref, val in zip(out_refs, res[:n_out], strict=True):
            o_ref[...] = val.astype(o_ref.dtype)
        if n_sum:
            @pl.when(pl.program_id(0) == 0)
            def _():
                for s_ref in sum_refs:
                    s_ref[...] = jnp.zeros_like(s_ref)

            for s_ref, val in zip(sum_refs, res[n_out:], strict=True):
                s_ref[...] += val

    res = pl.pallas_call(
        kern, name=name, grid=(T // tm,),
        in_specs=[spec for _, spec in ins] + [ANY] * n_after,
        out_specs=[pl.BlockSpec((tm, c), lambda i: (i, 0)) for c, _ in outs]
        + [pl.BlockSpec((1, c), lambda i: (0, 0)) for c in sums],
        out_shape=[jax.ShapeDtypeStruct((T, c), dt) for c, dt in outs]
        + [jax.ShapeDtypeStruct((1, c), F32) for c in sums],
        compiler_params=_cparams(("arbitrary",)),
    )(*[a for a, _ in ins], *after)
    return res[0] if len(res) == 1 else res


def _sigmoid(x):
    return 1.0 / (1.0 + jnp.exp(-x))


def _rms(h):
    return lax.rsqrt(jnp.mean(h * h, axis=-1, keepdims=True) + RMS_EPS)


def _rms_bwd(dy, h, g):
    r = _rms(h)
    n = h * r
    dn = dy * g
    dh = r * (dn - n * jnp.mean(dn * n, axis=-1, keepdims=True))
    return dh, jnp.sum(dy * n, axis=0, keepdims=True)


def _rope(x, cos2, sin_signed):
    lane = lax.broadcasted_iota(jnp.int32, x.shape, 1)
    swapped = jnp.where((lane % HEAD_DIM) < HEAD_DIM // 2, pltpu.roll(x, 128 - HEAD_DIM // 2, 1), pltpu.roll(x, HEAD_DIM // 2, 1))
    return x * cos2 + swapped * sin_signed


NA_KEYS = NA_WIN_ROWS * GRID_W
NA_BASES = 8


def _na_row_geometry(r, rows):
    first = jnp.clip(r - NA_WIN_ROWS // 2, 0, rows - NA_WIN_ROWS)
    base = first - r + (NA_WIN_ROWS - 1)
    return pl.multiple_of(first * GRID_W, GRID_W), base


NA_ROWS_PER_STEP = 16
NA_BWD_ROWS_PER_STEP = 8


def _softmax_rows(s):
    p = jnp.exp(s - jnp.max(s, axis=-1, keepdims=True))
    return p / jnp.sum(p, axis=-1, keepdims=True)


def _na_probs(q, kw, bias):
    return _softmax_rows(lax.dot_general(q, kw, (((1,), (1,)), ((), ())), preferred_element_type=F32) + bias)


def _split_pair(t):
    first = lax.broadcasted_iota(jnp.int32, t.shape, 1) < HEAD_DIM
    zero = jnp.zeros_like(t)
    return jnp.where(first, t, zero), jnp.where(first, zero, t)


def _join_pair(a, b):
    return jnp.where(lax.broadcasted_iota(jnp.int32, a.shape, 1) < HEAD_DIM, a, b)


_NT = (((1,), (1,)), ((), ()))
_TN = (((0,), (0,)), ((), ()))


def _na_fwd(qkv, tab):
    T = qkv.shape[0]
    rows = T // GRID_W
    n_pairs = NA_WIDTH // 128

    def body(q_ref, k_ref, v_ref, tab_ref, y_ref):
        def step(it, carry):
            geo = [_na_row_geometry(it * NA_ROWS_PER_STEP + u, rows) for u in range(NA_ROWS_PER_STEP)]
            q0s = [pl.multiple_of((it * NA_ROWS_PER_STEP + u) * GRID_W, GRID_W) for u in range(NA_ROWS_PER_STEP)]
            ss = [lax.dot_general(jnp.concatenate(_split_pair(q_ref[pl.ds(q0, GRID_W), :] * Q_SCALE), axis=0),
                                  k_ref[pl.ds(k0, NA_KEYS), :], _NT, preferred_element_type=F32)
                  for q0, (k0, _) in zip(q0s, geo)]
            ps = [_softmax_rows(s + jnp.concatenate([tab_ref[0, base], tab_ref[1, base]], axis=0)) for s, (_, base) in zip(ss, geo)]
            ys = [jnp.dot(p.astype(BF), v_ref[pl.ds(k0, NA_KEYS), :], preferred_element_type=F32) for p, (k0, _) in zip(ps, geo)]
            for q0, y2 in zip(q0s, ys):
                y_ref[pl.ds(q0, GRID_W), :] = _join_pair(y2[:GRID_W], y2[GRID_W:]).astype(y_ref.dtype)
            return carry

        lax.fori_loop(0, rows // NA_ROWS_PER_STEP, step, 0)

    def cols(first):
        return pl.BlockSpec((T, 128), lambda j: (0, first + j))

    return pl.pallas_call(
        body, name="na_fwd", grid=(n_pairs,),
        in_specs=[cols(0), cols(n_pairs), cols(2 * n_pairs), pl.BlockSpec((2, NA_BASES, GRID_W, NA_KEYS), lambda j: (j, 0, 0, 0))],
        out_specs=cols(0), out_shape=jax.ShapeDtypeStruct((T, NA_WIDTH), BF),
        compiler_params=_cparams(("parallel",)),
    )(qkv, qkv, qkv, tab)


def _na_bwd(qkv, tab, do):
    T = qkv.shape[0]
    rows = T // GRID_W
    n_pairs = NA_WIDTH // 128

    def body(q_ref, k_ref, v_ref, tab_ref, do_ref, dq_ref, dk_ref, dv_ref, dtab_ref):
        dk_ref[...] = jnp.zeros_like(dk_ref)
        dv_ref[...] = jnp.zeros_like(dv_ref)
        dtab_ref[...] = jnp.zeros_like(dtab_ref)

        def step(it, carry):
            U = NA_BWD_ROWS_PER_STEP
            geo = [_na_row_geometry(it * U + u, rows) for u in range(U)]
            q0s = [pl.multiple_of((it * U + u) * GRID_W, GRID_W) for u in range(U)]
            q2s = [jnp.concatenate(_split_pair(q_ref[pl.ds(q0, GRID_W), :] * Q_SCALE), axis=0) for q0 in q0s]
            do2s = [jnp.concatenate(_split_pair(do_ref[pl.ds(q0, GRID_W), :]), axis=0) for q0 in q0s]
            ss = [lax.dot_general(q2, k_ref[pl.ds(k0, NA_KEYS), :], _NT, preferred_element_type=F32) for q2, (k0, _) in zip(q2s, geo)]
            dps = [lax.dot_general(do2, v_ref[pl.ds(k0, NA_KEYS), :], _NT, preferred_element_type=F32) for do2, (k0, _) in zip(do2s, geo)]
            ps = [_softmax_rows(s + jnp.concatenate([tab_ref[0, base], tab_ref[1, base]], axis=0)) for s, (_, base) in zip(ss, geo)]
            dss = [p * (dp - jnp.sum(dp * p, axis=-1, keepdims=True)) for p, dp in zip(ps, dps)]
            dvs = [lax.dot_general(p.astype(BF), do2, _TN, preferred_element_type=F32) for p, do2 in zip(ps, do2s)]
            dsbs = [ds.astype(BF) for ds in dss]
            dqs = [jnp.dot(dsb, k_ref[pl.ds(k0, NA_KEYS), :], preferred_element_type=F32) for dsb, (k0, _) in zip(dsbs, geo)]
            dks = [lax.dot_general(dsb, q2, _TN, preferred_element_type=F32) for dsb, q2 in zip(dsbs, q2s)]
            for u in range(U):
                k0, base = geo[u]
                dtab_ref[0, base] += dss[u][:GRID_W]
                dtab_ref[1, base] += dss[u][GRID_W:]
                dq_ref[pl.ds(q0s[u], GRID_W), :] = _join_pair(dqs[u][:GRID_W], dqs[u][GRID_W:])
                dk_ref[pl.ds(k0, NA_KEYS), :] += dks[u]
                dv_ref[pl.ds(k0, NA_KEYS), :] += dvs[u]
            return carry

        lax.fori_loop(0, rows // NA_BWD_ROWS_PER_STEP, step, 0)

    def cols(first):
        return pl.BlockSpec((T, 128), lambda j: (0, first + j))

    tabs = pl.BlockSpec((2, NA_BASES, GRID_W, NA_KEYS), lambda j: (j, 0, 0, 0))
    wide = jax.ShapeDtypeStruct((T, NA_WIDTH), F32)
    return pl.pallas_call(
        body, name="na_bwd", grid=(n_pairs,),
        in_specs=[cols(0), cols(n_pairs), cols(2 * n_pairs), tabs, cols(0)],
        out_specs=[cols(0), cols(0), cols(0), tabs],
        out_shape=[wide, wide, wide, jax.ShapeDtypeStruct((NA_HEADS, NA_BASES, GRID_W, NA_KEYS), F32)],
        compiler_params=_cparams(("parallel",)),
    )(qkv, qkv, qkv, tab, do)


def _na_bias_table(rpb):
    H, n_rows, n_cols = rpb.shape

    def body(r_ref, tab_ref):
        q = lax.broadcasted_iota(jnp.int32, (GRID_W, 128), 0)
        kc = lax.broadcasted_iota(jnp.int32, (GRID_W, 128), 1)
        first = jnp.clip(q - NA_WIN_COLS // 2, 0, GRID_W - NA_WIN_COLS)
        valid = (kc >= first) & (kc < first + NA_WIN_COLS)
        toeplitz = []
        for ro in range(n_rows):
            row = jnp.broadcast_to(r_ref[pl.ds(ro, 1), :], (GRID_W, 128))
            shifted = pltpu.roll(pltpu.roll(row, 128 - (NA_WIN_COLS - 1), 1), 0, 1, stride=1, stride_axis=0)
            toeplitz.append(jnp.where(valid, shifted, NEG_INF))
        for base in range(NA_BASES):
            for j in range(NA_WIN_ROWS // 2):
                even, odd = toeplitz[base + 2 * j], toeplitz[base + 2 * j + 1]
                tab_ref[base, :, pl.ds(j * 128, 128)] = jnp.where(kc < GRID_W, even, pltpu.roll(odd, GRID_W, 1))

    padded = jnp.pad(rpb, ((0, 0), (0, 16 - n_rows), (0, 128 - n_cols)))
    return pl.pallas_call(
        body, name="na_bias_table", grid=(H,),
        in_specs=[pl.BlockSpec((None, 16, 128), lambda h: (h, 0, 0))],
        out_specs=pl.BlockSpec((None, NA_BASES, GRID_W, NA_KEYS), lambda h: (h, 0, 0, 0)),
        out_shape=jax.ShapeDtypeStruct((H, NA_BASES, GRID_W, NA_KEYS), F32),
        compiler_params=_cparams(("parallel",)),
    )(padded)


def _na_rpb_grad(dtab, after=()):
    H = dtab.shape[0]
    n_rows = 2 * NA_WIN_ROWS - 1
    n_cols = 2 * NA_WIN_COLS - 1

    def body(d_ref, *rest):
        o_ref = rest[-1]
        lane = lax.broadcasted_iota(jnp.int32, (GRID_W, 128), 1)
        low = lane < GRID_W
        out_rows = []
        for ro in range(n_rows):
            acc = jnp.zeros((GRID_W, 128), F32)
            for base in range(NA_BASES):
                i = ro - base
                if not 0 <= i < NA_WIN_ROWS:
                    continue
                pair = d_ref[base, :, pl.ds((i // 2) * 128, 128)]
                if i % 2:
                    pair = pltpu.roll(pair, GRID_W, 1)
                acc = acc + jnp.where(low, pair, 0.0)
            skew = pltpu.roll(acc, 0, 1, stride=1, stride_axis=0)
            diag = jnp.sum(skew, axis=0, keepdims=True)
            out_rows.append(pltpu.roll(jnp.broadcast_to(diag, (8, 128)), 128 - (GRID_W - NA_WIN_COLS), 1)[:1])
        out_rows.append(jnp.zeros((1, 128), F32))
        res = jnp.concatenate(out_rows, axis=0)
        o_ref[...] = jnp.where(lax.broadcasted_iota(jnp.int32, res.shape, 1) < n_cols, res, 0.0)

    return pl.pallas_call(
        body, name="na_rpb_grad", grid=(H,),
        in_specs=[pl.BlockSpec((None, NA_BASES, GRID_W, NA_KEYS), lambda h: (h, 0, 0, 0))] + [ANY] * len(after),
        out_specs=pl.BlockSpec((None, n_rows + 1, 128), lambda h: (h, 0, 0)),
        out_shape=jax.ShapeDtypeStruct((H, n_rows + 1, 128), F32),
        compiler_params=_cparams(("parallel",)),
    )(jnp.flip(dtab, axis=2), *after)


BAND_Q = 128
BAND_KEYS = BAND_Q + 2 * DIL_RADIUS


def _band_geometry(n, L):
    q0 = pl.multiple_of(n * BAND_Q, BAND_Q)
    k0 = pl.multiple_of(jnp.clip(q0 - DIL_RADIUS, 0, L - BAND_KEYS), DIL_RADIUS)
    qi = q0 + lax.broadcasted_iota(jnp.int32, (BAND_Q, BAND_KEYS), 0)
    kj = k0 + lax.broadcasted_iota(jnp.int32, (BAND_Q, BAND_KEYS), 1)
    return q0, k0, jnp.abs(qi - kj) <= DIL_RADIUS


DIL_PAIRS = DIL_OUT_WIDTH // 128


def _residue_shape(dil, T, dtype):
    return jax.ShapeDtypeStruct((DIL_PAIRS, dil, T // dil, 128), dtype)


def _residue_tile(dil, tm):
    return pl.BlockSpec((DIL_PAIRS, dil, tm // dil, 128), lambda i: (0, 0, i, 0))


def _to_natural(ref, scratch, dil, tm):
    tiles = []
    for pair in range(DIL_PAIRS):
        if dil == 1:
            tiles.append(ref[pair, 0].astype(F32))
            continue
        for r in range(dil):
            scratch[pl.ds(r, tm // dil, stride=dil), :] = ref[pair, r].astype(F32)
        tiles.append(scratch[...])
    return tiles


def _from_natural(tile, scratch, ref, pair, dil, tm):
    if dil == 1:
        ref[pair, 0] = tile.astype(ref.dtype)
        return
    scratch[...] = tile
    for r in range(dil):
        ref[pair, r] = scratch[pl.ds(r, tm // dil, stride=dil), :].astype(ref.dtype)


def _band_specs(group, T):
    dil = DIL_GROUPS[group][1]
    L = T // dil
    assert L % BAND_Q == 0 and L >= BAND_KEYS, (T, dil)
    per_residue = min(BAND_BLOCKS_PER_STEP, L // BAND_Q)
    residues = min(dil, BAND_BLOCKS_PER_STEP // per_residue)
    spec = pl.BlockSpec((None, residues, L, 128), lambda s: (s % DIL_PAIRS, s // DIL_PAIRS, 0, 0))
    return L, residues, per_residue, (dil // residues * DIL_PAIRS,), spec


BAND_BLOCKS_PER_STEP = 8


def _band_softmax(s, valid):
    s = jnp.where(valid, s, NEG_INF)
    m = jnp.max(s, axis=-1, keepdims=True)
    p = jnp.exp(s - m)
    l = jnp.sum(p, axis=-1, keepdims=True)
    return p / l, m + jnp.log(l)


def _band_fwd(q, k, v, group):
    T = q.shape[1] * q.shape[2]
    L, residues, U, grid, spec = _band_specs(group, T)

    def body(q_ref, k_ref, v_ref, o_ref, lse_ref):
        def step(it, carry):
            geo = [(r, *_band_geometry(it * U + u, L)) for r in range(residues) for u in range(U)]
            ss = [lax.dot_general(jnp.concatenate(_split_pair(q_ref[r, pl.ds(q0, BAND_Q), :]), axis=0),
                                  k_ref[r, pl.ds(k0, BAND_KEYS), :], _NT, preferred_element_type=F32) for r, q0, k0, _ in geo]
            pls = [_band_softmax(s, jnp.concatenate([valid, valid], axis=0)) for s, (_, _, _, valid) in zip(ss, geo)]
            os = [jnp.dot(p.astype(BF), v_ref[r, pl.ds(k0, BAND_KEYS), :], preferred_element_type=F32)
                  for (p, _), (r, _, k0, _) in zip(pls, geo)]
            for (r, q0, _, _), o2, (_, lse) in zip(geo, os, pls):
                o_ref[r, pl.ds(q0, BAND_Q), :] = _join_pair(o2[:BAND_Q], o2[BAND_Q:])
                lse2 = jnp.broadcast_to(lse, (2 * BAND_Q, 128))
                lse_ref[r, pl.ds(q0, BAND_Q), :] = _join_pair(lse2[:BAND_Q], lse2[BAND_Q:])
            return carry

        lax.fori_loop(0, L // (BAND_Q * U), step, 0)

    res = _residue_shape(DIL_GROUPS[group][1], T, F32)
    return pl.pallas_call(
        body, name=f"band_fwd_g{group}", grid=grid,
        in_specs=[spec] * 3, out_specs=[spec] * 2, out_shape=[res, res],
        compiler_params=_cparams(("parallel",)),
    )(q, k, v)


def _band_bwd(q, k, v, do, dlse, group):
    T = q.shape[1] * q.shape[2]
    L, residues, U, grid, spec = _band_specs(group, T)

    def body(q_ref, k_ref, v_ref, do_ref, dlse_ref, dq_ref, dk_ref, dv_ref):
        dk_ref[...] = jnp.zeros_like(dk_ref)
        dv_ref[...] = jnp.zeros_like(dv_ref)

        def step(it, carry):
            geo = [(r, *_band_geometry(it * U + u, L)) for r in range(residues) for u in range(U)]
            q2s = [jnp.concatenate(_split_pair(q_ref[r, pl.ds(q0, BAND_Q), :]), axis=0) for r, q0, _, _ in geo]
            do2s = [jnp.concatenate(_split_pair(do_ref[r, pl.ds(q0, BAND_Q), :]), axis=0) for r, q0, _, _ in geo]
            ss = [lax.dot_general(q2, k_ref[r, pl.ds(k0, BAND_KEYS), :], _NT, preferred_element_type=F32)
                  for q2, (r, _, k0, _) in zip(q2s, geo)]
            dps = [lax.dot_general(do2, v_ref[r, pl.ds(k0, BAND_KEYS), :], _NT, preferred_element_type=F32)
                   for do2, (r, _, k0, _) in zip(do2s, geo)]
            ps = [_band_softmax(s, jnp.concatenate([valid, valid], axis=0))[0] for s, (_, _, _, valid) in zip(ss, geo)]
            dss = []
            for p, dp, (r, q0, _, _) in zip(ps, dps, geo):
                dl = dlse_ref[r, pl.ds(q0, BAND_Q), :]
                dl2 = jnp.concatenate([dl[:, :1], dl[:, HEAD_DIM:HEAD_DIM + 1]], axis=0)
                dss.append(p * (dp - jnp.sum(dp * p, axis=-1, keepdims=True) + dl2))
            dvs = [lax.dot_general(p.astype(BF), do2, _TN, preferred_element_type=F32) for p, do2 in zip(ps, do2s)]
            dsbs = [ds.astype(BF) for ds in dss]
            dqs = [jnp.dot(dsb, k_ref[r, pl.ds(k0, BAND_KEYS), :], preferred_element_type=F32) for dsb, (r, _, k0, _) in zip(dsbs, geo)]
            dks = [lax.dot_general(dsb, q2, _TN, preferred_element_type=F32) for dsb, q2 in zip(dsbs, q2s)]
            for u, (r, q0, k0, _) in enumerate(geo):
                dq_ref[r, pl.ds(q0, BAND_Q), :] = _join_pair(dqs[u][:BAND_Q], dqs[u][BAND_Q:])
                dk_ref[r, pl.ds(k0, BAND_KEYS), :] += dks[u]
                dv_ref[r, pl.ds(k0, BAND_KEYS), :] += dvs[u]
            return carry

        lax.fori_loop(0, L // (BAND_Q * U), step, 0)

    res = _residue_shape(DIL_GROUPS[group][1], T, F32)
    return pl.pallas_call(
        body, name=f"band_bwd_g{group}", grid=grid,
        in_specs=[spec] * 5, out_specs=[spec] * 3, out_shape=[res] * 3,
        compiler_params=_cparams(("parallel",)),
    )(q, k, v, do, dlse)


def _head_sums(t):
    head = lax.broadcasted_iota(jnp.int32, t.shape, 1) // HEAD_DIM
    out = jnp.zeros_like(t)
    for h in range(t.shape[1] // HEAD_DIM):
        mine = head == h
        out = jnp.where(mine, jnp.sum(jnp.where(mine, t, 0.0), axis=-1, keepdims=True), out)
    return out


def _dil_merge_fwd(os, lses, T, tm):
    G = len(DIL_GROUPS)
    W = DIL_OUT_WIDTH
    dils = [d for _, d in DIL_GROUPS]

    def body(*refs):
        o_refs, lse_refs = refs[:G], refs[G:2 * G]
        y_ref, w_refs, on_refs, scratch = refs[2 * G], refs[2 * G + 1:3 * G + 1], refs[3 * G + 1:4 * G + 1], refs[-1]
        o = [jnp.concatenate(_to_natural(r, scratch, d, tm), axis=1) for r, d in zip(o_refs, dils)]
        ls = [jnp.concatenate(_to_natural(r, scratch, d, tm), axis=1) for r, d in zip(lse_refs, dils)]
        m = functools.reduce(jnp.maximum, ls)
        es = [jnp.exp(l - m) for l in ls]
        tot = functools.reduce(jnp.add, es)
        ws = [e / tot for e in es]
        y_ref[...] = functools.reduce(jnp.add, [w * t for w, t in zip(ws, o)]).astype(y_ref.dtype)
        for g in range(G):
            w_refs[g][...] = ws[g]
            on_refs[g][...] = o[g]

    nat = pl.BlockSpec((tm, W), lambda i: (i, 0))
    res = pl.pallas_call(
        body, name="dil_merge_fwd", grid=(T // tm,),
        in_specs=[_residue_tile(d, tm) for d in dils] * 2,
        out_specs=[nat] * (2 * G + 1),
        out_shape=[jax.ShapeDtypeStruct((T, W), BF)] + [jax.ShapeDtypeStruct((T, W), F32)] * (2 * G),
        scratch_shapes=[pltpu.VMEM((tm, 128), F32)],
        compiler_params=_cparams(("parallel",)),
    )(*os, *lses)
    return res[0], res[1:G + 1], res[G + 1:]


def _dil_merge_bwd(dy, os, ws, tm, after=()):
    G = len(DIL_GROUPS)
    T, W = dy.shape
    dils = [d for _, d in DIL_GROUPS]
    n_after = len(after)

    def body(*refs):
        dyt = refs[0][...]
        o, w = [r[...] for r in refs[1:G + 1]], [r[...] for r in refs[G + 1:2 * G + 1]]
        refs = refs[2 * G + 1 + n_after:]
        do_refs, dlse_refs, scratch = refs[:G], refs[G:2 * G], refs[-1]
        dws = [_head_sums(dyt * t) for t in o]
        mean = functools.reduce(jnp.add, [a * b for a, b in zip(w, dws)])
        for g, d in enumerate(dils):
            do, dlse = w[g] * dyt, w[g] * (dws[g] - mean)
            for pair in range(DIL_PAIRS):
                cols = slice(pair * 128, (pair + 1) * 128)
                _from_natural(do[:, cols], scratch, do_refs[g], pair, d, tm)
                _from_natural(dlse[:, cols], scratch, dlse_refs[g], pair, d, tm)

    nat = pl.BlockSpec((tm, W), lambda i: (i, 0))
    res = pl.pallas_call(
        body, name="dil_merge_bwd", grid=(T // tm,),
        in_specs=[nat] * (2 * G + 1) + [ANY] * n_after,
        out_specs=[_residue_tile(d, tm) for d in dils] * 2,
        out_shape=[_residue_shape(d, T, BF) for d in dils] + [_residue_shape(d, T, F32) for d in dils],
        scratch_shapes=[pltpu.VMEM((tm, 128), F32)],
        compiler_params=_cparams(("parallel",)),
    )(dy, *os, *ws, *after)
    return res[:G], res[G:]


def _qkv_prep(z, cos2, sin_signed, tm):
    T = z.shape[0]
    G = len(DIL_GROUPS)
    dils = [d for _, d in DIL_GROUPS]
    n_dil_blocks = 3 * DIL_WIDTH // 128

    def body(*refs):
        blocks = refs[:n_dil_blocks]
        cos_ref, sin_ref = refs[n_dil_blocks], refs[1 + n_dil_blocks]
        outs = refs[2 + n_dil_blocks:]
        for part in range(3):
            for g, d in enumerate(dils):
                out = outs[g * 3 + part]
                for pair in range(DIL_PAIRS):
                    blk = blocks[part * (DIL_WIDTH // 128) + g * DIL_PAIRS + pair]
                    for r in range(d):
                        rows = pl.ds(r, tm // d, stride=d) if d > 1 else slice(None)
                        x = blk[rows, :]
                        if part < 2:
                            x = _rope(x, cos_ref[rows, :], sin_ref[rows, :])
                        if part == 0:
                            x = x * Q_SCALE
                        out[pair, r] = x.astype(out.dtype)

    lane_block = [pl.BlockSpec((tm, 128), functools.partial(lambda b, i: (i, b), b)) for b in range(n_dil_blocks)]
    tab = pl.BlockSpec((tm, 128), lambda i: (i, 0))
    res = pl.pallas_call(
        body, name="qkv_prep", grid=(T // tm,),
        in_specs=lane_block + [tab, tab],
        out_specs=[_residue_tile(d, tm) for d in dils for _ in range(3)],
        out_shape=[_residue_shape(d, T, BF) for d in dils for _ in range(3)],
        compiler_params=_cparams(("parallel",)),
    )(*[z] * n_dil_blocks, cos2, sin_signed)
    return [res[3 * g:3 + 3 * g] for g in range(G)]


def _qkv_unprep(d_na, d_dil, cos2, sin_signed, tm, after=()):
    T = d_na[0].shape[0]
    G = len(DIL_GROUPS)
    dils = [d for _, d in DIL_GROUPS]
    n_after = len(after)

    def body(*refs):
        dq, dk, dv = (r[...] for r in refs[:3])
        res_refs = refs[3:3 + 3 * G]
        cs, sn = refs[3 + 3 * G][...], refs[4 + 3 * G][...]
        out, scratch = refs[5 + 3 * G + n_after], refs[-1]
        cols = [dq * Q_SCALE, dk, dv]
        for part in range(3):
            for g, d in enumerate(dils):
                for x in _to_natural(res_refs[g * 3 + part], scratch, d, tm):
                    if part < 2:
                        x = _rope(x, cs, -sn)
                    cols.append(x * Q_SCALE if part == 0 else x)
        out[...] = jnp.concatenate(cols, axis=1).astype(out.dtype)

    wide = pl.BlockSpec((tm, NA_WIDTH), lambda i: (i, 0))
    tab = pl.BlockSpec((tm, 128), lambda i: (i, 0))
    return pl.pallas_call(
        body, name="qkv_unprep", grid=(T // tm,),
        in_specs=[wide] * 3 + [_residue_tile(d, tm) for d in dils for _ in range(3)] + [tab, tab] + [ANY] * n_after,
        out_specs=pl.BlockSpec((tm, QKV_WIDTH), lambda i: (i, 0)),
        out_shape=jax.ShapeDtypeStruct((T, QKV_WIDTH), BF),
        scratch_shapes=[pltpu.VMEM((tm, 128), F32)],
        compiler_params=_cparams(("parallel",)),
    )(*d_na, *[t for g in range(G) for t in d_dil[g]], cos2, sin_signed, *after)


def _rope_tables(positions):
    half = HEAD_DIM // 2
    inv_freq = ROPE_THETA ** (-jnp.arange(half, dtype=F32) / half)
    ang = positions.astype(F32)[:, None] * inv_freq
    cos, sin = jnp.cos(ang), jnp.sin(ang)
    return jnp.tile(jnp.concatenate([cos, cos], axis=1), (1, 2)), jnp.tile(jnp.concatenate([-sin, sin], axis=1), (1, 2))


def _pack_rows(t):
    return t.reshape(-1, PACK_W)


def _me():
    return lax.axis_index("x"), lax.axis_index("y"), lax.axis_index("c")


def _other_chips(x, y):
    return [(1 - x, y), (x, 1 - y), (1 - x, 1 - y)]


def _pair_sum(g, got, tm):
    S, R, W = g.shape
    half = R // 2
    nb = half // tm

    def body(pos_ref, g_ref, got_ref, own_ref, ob_ref):
        tot = g_ref[...] + got_ref[...]
        ob_ref[...] = tot.astype(ob_ref.dtype)

        @pl.when(pl.program_id(1) == pos_ref[1])
        def _():
            own_ref[...] = tot

    tile = pl.BlockSpec((None, tm, W), lambda i, s, pos_ref: (s, i, 0))
    c, chip = lax.axis_index("c"), 2 * lax.axis_index("x") + lax.axis_index("y")
    return pl.pallas_call(
        body, name="pair_sum",
        grid_spec=pltpu.PrefetchScalarGridSpec(
            num_scalar_prefetch=1, grid=(nb, S),
            in_specs=[pl.BlockSpec((None, tm, W), lambda i, s, pos_ref: (s, pos_ref[0] * nb + i, 0)), tile],
            out_specs=[pl.BlockSpec((tm, W), lambda i, s, pos_ref: (i, 0)), tile]),
        out_shape=[jax.ShapeDtypeStruct((half, W), F32), jax.ShapeDtypeStruct((S, half, W), BF)],
        compiler_params=_cparams(("parallel", "arbitrary")),
    )(jnp.stack([c, chip]).astype(jnp.int32), g, got)


def _chip_sum(own, others, tm):
    n, h, W = others.shape
    nb = h // tm

    def body(c_ref, own_ref, p_ref, o_ref):
        o_ref[...] = ((own_ref[...] + p_ref[0].astype(F32)) + p_ref[1].astype(F32)) + p_ref[2].astype(F32)

    return pl.pallas_call(
        body, name="chip_sum",
        grid_spec=pltpu.PrefetchScalarGridSpec(
            num_scalar_prefetch=1, grid=(nb,),
            in_specs=[pl.BlockSpec((tm, W), lambda i, c_ref: (i, 0)), pl.BlockSpec((n, tm, W), lambda i, c_ref: (0, i, 0))],
            out_specs=pl.BlockSpec((tm, W), lambda i, c_ref: (c_ref[0] * nb + i, 0))),
        out_shape=jax.ShapeDtypeStruct((2 * h, W), F32),
        compiler_params=_cparams(("parallel",)),
    )(lax.axis_index("c").reshape(1).astype(jnp.int32), own, others)


def _join_halves(shard):
    h = shard.shape[0] // 2

    def body(in_ref, out_ref, send_sem, recv_sem):
        x, y, c = _me()
        cp = pltpu.make_async_remote_copy(
            src_ref=in_ref.at[pl.ds(c * h, h), :], dst_ref=out_ref.at[pl.ds(c * h, h), :],
            send_sem=send_sem, recv_sem=recv_sem, device_id=(x, y, 1 - c), device_id_type=MESH)
        cp.start()
        pltpu.make_async_remote_copy(
            src_ref=in_ref.at[pl.ds(c * h, h), :], dst_ref=out_ref.at[pl.ds((1 - c) * h, h), :],
            send_sem=send_sem, recv_sem=recv_sem, device_id=(x, y, 1 - c), device_id_type=MESH).wait_recv()
        cp.wait_send()

    return pl.pallas_call(
        body, name="join_halves", in_specs=[ANY], out_specs=ANY,
        out_shape=jax.ShapeDtypeStruct(shard.shape, shard.dtype), input_output_aliases={0: 0},
        scratch_shapes=[pltpu.SemaphoreType.DMA, pltpu.SemaphoreType.DMA],
    )(shard)


def _allreduce_small(s, after=()):
    R, W = s.shape
    n_after = len(after)

    def body(s_ref, *rest):
        o_ref, buf, send_sems, recv_sems = rest[n_after:]
        x, y, c = _me()
        me = 4 * x + 2 * y + c
        buf[me] = s_ref[...]
        peers = [((x + fx) % 2, (y + fy) % 2, (c + fc) % 2) for fx in range(2) for fy in range(2) for fc in range(2)][1:]
        sends = [pltpu.make_async_remote_copy(
            src_ref=s_ref, dst_ref=buf.at[me], send_sem=send_sems.at[k], recv_sem=recv_sems.at[k],
            device_id=peer, device_id_type=MESH) for k, peer in enumerate(peers)]
        for cp in sends:
            cp.start()
        for k, peer in enumerate(peers):
            pltpu.make_async_remote_copy(
                src_ref=s_ref, dst_ref=buf.at[4 * peer[0] + 2 * peer[1] + peer[2]], send_sem=send_sems.at[k],
                recv_sem=recv_sems.at[k], device_id=peer, device_id_type=MESH).wait_recv()
        for cp in sends:
            cp.wait_send()
        total = buf[0]
        for d in range(1, N_DEV):
            total = total + buf[d]
        o_ref[...] = total

    return pl.pallas_call(
        body, name="allreduce_small",
        in_specs=[pl.BlockSpec(memory_space=pltpu.VMEM)] + [ANY] * n_after, out_specs=pl.BlockSpec(memory_space=pltpu.VMEM),
        out_shape=jax.ShapeDtypeStruct((R, W), F32),
        scratch_shapes=[pltpu.VMEM((N_DEV, R, W), F32), pltpu.SemaphoreType.DMA((N_DEV - 1,)), pltpu.SemaphoreType.DMA((N_DEV - 1,))],
    )(s, *after)


HBM_SPEC = pl.BlockSpec(memory_space=pltpu.HBM)
SEM_SPEC = pl.BlockSpec(memory_space=pltpu.SEMAPHORE)
DATAFLOW = pltpu.SideEffectType.DATAFLOW_SIDE_EFFECTING


class _InFlight(NamedTuple):
    sems: tuple
    src: jax.Array
    land: jax.Array
    token: jax.Array


def _split_start(name, src, land_shape, land_dtype, n, copies, after=()):
    n_after = len(after)

    def body(src_ref, land_ref, *rest):
        rest = rest[n_after:]
        sems, token = rest[:2 * n], rest[-1]
        for k, (s, d, peer) in enumerate(copies(src_ref, land_ref)):
            pltpu.make_async_remote_copy(src_ref=s, dst_ref=d, send_sem=sems[k], recv_sem=sems[n + k],
                                         device_id=peer, device_id_type=MESH).start()
        token[...] = jnp.zeros_like(token)

    outs = pl.pallas_call(
        body, name=name,
        out_shape=(*[pltpu.SemaphoreType.DMA(())] * (2 * n), pltpu.HBM(src.shape, src.dtype), pltpu.HBM(land_shape, land_dtype),
                   jax.ShapeDtypeStruct((8, 128), F32)),
        in_specs=(HBM_SPEC, HBM_SPEC, *[ANY] * n_after),
        out_specs=(*[SEM_SPEC] * (2 * n), HBM_SPEC, HBM_SPEC, pl.BlockSpec(memory_space=pltpu.VMEM)),
        input_output_aliases={0: 2 * n, 1: 2 * n + 1},
        compiler_params=pltpu.CompilerParams(has_side_effects=DATAFLOW),
    )(pltpu.with_memory_space_constraint(src, pltpu.HBM), pltpu.with_memory_space_constraint(lax.empty(land_shape, land_dtype), pltpu.HBM),
      *after)
    return _InFlight(tuple(outs[:2 * n]), outs[2 * n], outs[2 * n + 1], outs[2 * n + 2])


def _split_wait(name, flight, after, n, copies):
    def body(src_ref, land_ref, *rest):
        sems = rest[:2 * n]
        for k, (s, d, peer) in enumerate(copies(src_ref, land_ref)):
            cp = pltpu.make_async_remote_copy(src_ref=s, dst_ref=d, send_sem=sems[k], recv_sem=sems[n + k],
                                              device_id=peer, device_id_type=MESH)
            cp.wait_send()
            cp.wait_recv()

    return pl.pallas_call(
        body, name=name,
        out_shape=(pltpu.HBM(flight.src.shape, flight.src.dtype), pltpu.HBM(flight.land.shape, flight.land.dtype)),
        in_specs=(HBM_SPEC, HBM_SPEC, *[SEM_SPEC] * (2 * n), ANY),
        out_specs=(HBM_SPEC, HBM_SPEC), input_output_aliases={0: 0, 1: 1},
        compiler_params=pltpu.CompilerParams(has_side_effects=DATAFLOW),
    )(flight.src, flight.land, *flight.sems, after)


def _gather_copies(src_ref, land_ref):
    x, y, c = _me()
    return [(src_ref, land_ref.at[2 * x + y], (*chip, c)) for chip in _other_chips(x, y)]


def _gather_start(packed, tag, after=()):
    return _split_start(f"gather_start_{tag}", packed, (N_CHIPS, *packed.shape), packed.dtype, 3, _gather_copies, after)


def _gather_wait(flight, after, tag):
    src, others = _split_wait(f"gather_wait_{tag}", flight, after, 3, _gather_copies)
    return lax.dynamic_update_slice(others, src[None], (2 * lax.axis_index("x") + lax.axis_index("y"), 0, 0))


def _across_copies(src_ref, land_ref):
    x, y, c = _me()
    half = src_ref.shape[0] // 2
    rows = pl.ds(c * half, half)
    return [(src_ref.at[rows, :], land_ref.at[2 * x + y, rows, :], (*chip, c)) for chip in _other_chips(x, y)]


def _to_sibling_copies(all_ref, unused_ref):
    x, y, c = _me()
    half = all_ref.shape[1] // 2
    places = [all_ref.at[2 * chip[0] + chip[1], pl.ds(c * half, half), :] for chip in _other_chips(x, y)]
    return [(place, place, (x, y, 1 - c)) for place in places]


def _gather_halves_start(shard):
    return _split_start("gather_in_across_start", shard, (N_CHIPS, *shard.shape), shard.dtype, 3, _across_copies)


def _gather_halves_finish(flight, after):
    shard, landed = _split_wait("gather_in_across_wait", flight, after, 3, _across_copies)
    forward = _split_start("gather_in_sibling_start", landed, (8, 128), landed.dtype, 3, _to_sibling_copies)
    others = _split_wait("gather_in_sibling_wait", forward, forward.token, 3, _to_sibling_copies)[0]
    return lax.dynamic_update_slice(others, shard[None], (2 * lax.axis_index("x") + lax.axis_index("y"), 0, 0))


def _assemble_w_in(shards, tm):
    S, R, C = shards.shape
    n_gates = 2 * D_MODEL

    def body(s_ref, w_ref, g_ref):
        full = jnp.concatenate([s_ref[s] for s in range(S)], axis=1)
        w_ref[...] = full
        g_ref[...] = full[:, S * C - n_gates:]

    return pl.pallas_call(
        body, name="assemble_w_in", grid=(R // tm,),
        in_specs=[pl.BlockSpec((S, tm, C), lambda i: (0, i, 0))],
        out_specs=[pl.BlockSpec((tm, S * C), lambda i: (i, 0)), pl.BlockSpec((tm, n_gates), lambda i: (i, 0))],
        out_shape=[jax.ShapeDtypeStruct((R, S * C), shards.dtype), jax.ShapeDtypeStruct((R, n_gates), shards.dtype)],
        compiler_params=_cparams(("parallel",)),
    )(shards)


def _swap_copies(src_ref, land_ref):
    x, y, c = _me()
    half = land_ref.shape[1]
    return [(src_ref.at[:, pl.ds((1 - c) * half, half), :], land_ref, (x, y, 1 - c))]


def _swap_start(g, tag):
    S, R, W = g.shape
    return _split_start(f"swap_halves_start_{tag}", g, (S, R // 2, W), g.dtype, 1, _swap_copies)


def _swap_wait(flight, after, tag):
    return _split_wait(f"swap_halves_wait_{tag}", flight, after, 1, _swap_copies)


def _scatter_copies(src_ref, land_ref):
    x, y, c = _me()
    return [(src_ref.at[2 * chip[0] + chip[1]], land_ref.at[j], (*chip, c)) for j, chip in enumerate(_other_chips(x, y))]


def _scatter_start(part, tag):
    S, h, W = part.shape
    return _split_start(f"scatter_chips_start_{tag}", part, (S - 1, h, W), part.dtype, 3, _scatter_copies)


def _scatter_wait(flight, after, tag):
    return _split_wait(f"scatter_chips_wait_{tag}", flight, after, 3, _scatter_copies)[1]


def _join_copies(shard_ref, unused_ref):
    x, y, c = _me()
    h = shard_ref.shape[0] // 2
    rows = shard_ref.at[pl.ds(c * h, h), :]
    return [(rows, rows, (x, y, 1 - c))]


def _join_start(shard):
    return _split_start("join_halves_start", shard, (8, 128), shard.dtype, 1, _join_copies)


def _join_wait(flight, after):
    return _split_wait("join_halves_wait", flight, after, 1, _join_copies)[0]


def _adamw(name, g, g_row0, w, m, v):
    _, R, C = w.shape
    tm = next(cand for cand in (368, 256, 128, 64, 32, 16, 8) if R % cand == 0)
    assert g_row0 % tm == 0 and g.shape[1] == C

    def body(g_ref, w_ref, m_ref, v_ref, go_ref, d_ref, mo_ref, vo_ref):
        gt = g_ref[...]
        mt = ADAM_B1 * m_ref[...] + (1.0 - ADAM_B1) * gt
        vt = ADAM_B2 * v_ref[...] + (1.0 - ADAM_B2) * jnp.square(gt)
        m_hat = mt / (1.0 - ADAM_B1 ** ADAM_STEP)
        v_hat = vt / (1.0 - ADAM_B2 ** ADAM_STEP)
        go_ref[...] = gt
        d_ref[...] = -ADAM_LR * (m_hat / (jnp.sqrt(v_hat) + ADAM_EPS) + ADAM_WD * w_ref[...])
        mo_ref[...] = mt
        vo_ref[...] = vt

    state = pl.BlockSpec((None, tm, C), lambda i: (0, i, 0))
    return pl.pallas_call(
        body, name=name, grid=(R // tm,),
        in_specs=[pl.BlockSpec((tm, C), lambda i: (g_row0 // tm + i, 0)), state, state, state],
        out_specs=[state] * 4, out_shape=[jax.ShapeDtypeStruct((1, R, C), F32)] * 4,
        compiler_params=_cparams(("parallel",)),
    )(g, w, m, v)


def _unpack_weights(gathered, names):
    S = gathered.shape[0]
    shard_shapes = {"w_in": (D_MODEL, (QKV_WIDTH + 2 * D_MODEL) // S), "w_branch_na": (NA_WIDTH, D_MODEL // S),
                    "w_branch_dil": (DIL_OUT_WIDTH, D_MODEL // S), "w_out": (D_MODEL // S, D_MODEL),
                    "w_up": (D_MODEL, D_FF // S), "w_down": (D_FF // S, D_MODEL),
                    "w_ple_gate": (D_MODEL // S, D_MODEL), "w_ple_proj": (PLE_DIM, D_MODEL // S)}
    col_sharded = {"w_in", "w_branch_na", "w_branch_dil", "w_up", "w_ple_proj"}
    out, r0 = {}, 0
    for name in names:
        rows, cols = shard_shapes[name]
        n = rows * cols // PACK_W
        t = gathered[:, r0:r0 + n, :].reshape(S, rows, cols)
        r0 += n
        out[name] = t.transpose(1, 0, 2).reshape(rows, S * cols) if name in col_sharded else t.reshape(S * rows, cols)
    return out


def kernel(x, p, positions, g_mix, w_in, rpb, w_branch_na, w_branch_dil, w_out, g_mlp, w_up, w_down, g_ple, w_ple_gate, w_ple_proj, g_final, loss_target, m_g_mix, m_w_in, m_rpb, m_w_branch_na, m_w_branch_dil, m_w_out, m_g_mlp, m_w_up, m_w_down, m_g_ple, m_w_ple_gate, m_w_ple_proj, m_g_final, v_g_mix, v_w_in, v_rpb, v_w_branch_na, v_w_branch_dil, v_w_out, v_g_mlp, v_w_up, v_w_down, v_g_ple, v_w_ple_gate, v_w_ple_proj, v_g_final):
    shards = {"w_in": w_in[0], "w_branch_na": w_branch_na[0], "w_branch_dil": w_branch_dil[0], "w_out": w_out[0],
              "w_up": w_up[0], "w_down": w_down[0], "w_ple_gate": w_ple_gate[0], "w_ple_proj": w_ple_proj[0]}
    params = {"w_in": w_in, "w_branch_na": w_branch_na, "w_branch_dil": w_branch_dil, "w_out": w_out, "w_up": w_up,
              "w_down": w_down, "w_ple_gate": w_ple_gate, "w_ple_proj": w_ple_proj,
              "m_w_in": m_w_in, "m_w_branch_na": m_w_branch_na, "m_w_branch_dil": m_w_branch_dil, "m_w_out": m_w_out,
              "m_w_up": m_w_up, "m_w_down": m_w_down, "m_w_ple_gate": m_w_ple_gate, "m_w_ple_proj": m_w_ple_proj,
              "v_w_in": v_w_in, "v_w_branch_na": v_w_branch_na, "v_w_branch_dil": v_w_branch_dil, "v_w_out": v_w_out,
              "v_w_up": v_w_up, "v_w_down": v_w_down, "v_w_ple_gate": v_w_ple_gate, "v_w_ple_proj": v_w_ple_proj}

    xs, ps, tgt = x[0], p[0, 0], loss_target[0]
    T = xs.shape[0]
    TM = 512
    gm, gl, gp, gf = g_mix, g_mlp, g_ple, g_final.reshape(1, D_MODEL)

    across = _gather_halves_start(shards["w_in"].astype(BF))
    a = _rowwise("norm_mix", lambda h, g: h * _rms(h) * g, T, TM, [_row(xs, TM), _full(gm)], [(D_MODEL, BF)],
                 after=(across.token,))
    cos2, sin_signed = _rope_tables(positions[0])
    tab = _na_bias_table(rpb[0])
    w_in_all = _gather_halves_finish(across, a)
    w_in_full, w_gates = _assemble_w_in(w_in_all, 256)
    W = {"w_in": w_in_full}
    mix_flight = _gather_start(jnp.concatenate([_pack_rows(shards[n].astype(BF)) for n in GATHER_MIX], axis=0), "mix",
                               after=(w_in_all,))
    rest_flight = _gather_start(jnp.concatenate([_pack_rows(shards[n].astype(BF)) for n in GATHER_MLP], axis=0), "mlp",
                                after=(mix_flight.token,))

    n3 = 3 * NA_WIDTH
    qkv = _mm("in_na", a, W["w_in"], "nn", 1024, 768, 1024, [BF], after=(rest_flight.token,),
              b_view=(n3, (D_MODEL, 768), lambda j, k: (k, j)))
    z_dil = _mm("in_dil", a, W["w_in"], "nn", 1024, 768, 1024, [F32],
                b_view=(3 * DIL_WIDTH, (D_MODEL, 768), lambda j, k: (k, n3 // 768 + j)))
    z_gates = _mm("in_gates", a, w_gates, "nn", 1024,1024, 1024, [BF])

    dil_ops = _qkv_prep(z_dil, cos2, sin_signed, TM)
    y_na = _na_fwd(qkv, tab)
    band = [_band_fwd(*dil_ops[g], g) for g in range(len(DIL_GROUPS))]
    y_dil, w_grp, o_nat = _dil_merge_fwd([b[0] for b in band], [b[1] for b in band], T, TM)

    W.update(_unpack_weights(_gather_wait(mix_flight, y_dil, "mix"), GATHER_MIX))
    u_na = _mm("branch_na", y_na, W["w_branch_na"], "nn", 1024,1024, 512, [BF])
    u_dil = _mm("branch_dil", y_dil, W["w_branch_dil"], "nn", 1024,1024, 256, [BF])
    mixed = _rowwise(
        "gate_mix", lambda gn, gd, un, ud: _sigmoid(gn.astype(F32)) * un.astype(F32) + _sigmoid(gd.astype(F32)) * ud.astype(F32), T, TM,
        [_row(z_gates, TM, 0, D_MODEL), _row(z_gates, TM, 1, D_MODEL), _row(u_na, TM), _row(u_dil, TM)], [(D_MODEL, BF)])
    def add_norm(d, h, g):
        h = h + d
        return h, h * _rms(h) * g

    h1, cn = _mm("out_proj", mixed, W["w_out"], "nn", 512, 1024, 1024, [F32, BF], epilogue=add_norm, extras=(xs,), consts=(gl,))
    mlp_all = _gather_wait(rest_flight, cn, "mlp")
    W.update({n: t for n, t in _unpack_weights(mlp_all, GATHER_MLP).items() if n.startswith("w_ple")})
    chip_block = (None, D_MODEL, PACK_W)
    up, act = _mm("mlp_up", cn, mlp_all, "nn", 1024,1024, 1024, [BF, BF],
                  epilogue=lambda acc: (acc, jnp.square(jnp.maximum(acc, 0.0))), b_view=(D_FF, chip_block, lambda j, k: (j, 0, 0)))
    h2, en = _mm("mlp_down", act, mlp_all, "nn", 1024, 1024, 1024, [F32, BF], epilogue=add_norm, extras=(h1,), consts=(gp,),
                 b_view=(D_MODEL, chip_block, lambda j, k: (k, 1, 0)))
    pp = _mm("ple_proj", ps, W["w_ple_proj"], "nn", 1024,1024, 256, [F32])

    def head(gtt, h2t, ppt, tg, g):
        sg = _sigmoid(gtt)
        h3 = h2t + sg * ppt
        yo = h3 * _rms(h3) * g
        diff = yo - tg
        loss = 0.5 * jnp.sum(jnp.mean(jnp.square(diff), axis=-1, keepdims=True), axis=0, keepdims=True)
        dh3, dg = _rms_bwd(diff * (1.0 / D_MODEL), h3, g)
        return dh3, dh3 * ppt * sg * (1.0 - sg), dh3 * sg, jnp.broadcast_to(loss, (1, 128)), dg

    dh3, d_gt, d_pp, loss_part, dg_final = _mm(
        "ple_gate_loss_head", en, W["w_ple_gate"], "nn", 512, 1024, 1024, [F32, BF, BF], epilogue=head,
        extras=(h2, pp, tgt), consts=(gf,), sums=[128, D_MODEL])

    early_shapes = {n: shards[n].shape for n in REDUCE_EARLY}
    early_rows = sum(r * c for r, c in early_shapes.values()) // PACK_W
    shard_rows = D_MODEL // N_CHIPS
    early_buf = _mm("g_ple_gate", en, d_gt, "tn", 1024, 1024, 1024, [F32],
                    into=(jax.ShapeDtypeStruct((N_CHIPS, early_rows, PACK_W), F32), (N_CHIPS, shard_rows, PACK_W),
                          lambda i, j: (0, 2 * D_MODEL // shard_rows, 0)))
    g_ple_proj = _mm("g_ple_proj", ps, d_pp, "tn", 256, 1024, 1024,[F32])

    def add_norm_bwd(dn, dh_out, h, g):
        dh, dg = _rms_bwd(dn, h, g)
        dh = dh_out + dh
        return dh, dh, dg

    dh2, dh2_b, dg_ple = _mm("d_ple_gate", d_gt, W["w_ple_gate"], "nt", 512, 1024, 1024, [F32, BF],
                             epilogue=add_norm_bwd, extras=(dh3, h2), consts=(gp,), sums=[D_MODEL])
    d_up = _mm("d_mlp_down", dh2_b, mlp_all, "nt", 1024,1024, 1024, [BF], b_view=(D_FF, chip_block, lambda j, k: (j, 1, 0)),
               epilogue=lambda acc, u: (acc * (2.0 * jnp.maximum(u.astype(F32), 0.0)),), extras=(up,))
    early_buf = _mm("g_mlp_down", act, dh2_b, "tn", 1024, 1024, 1024,[F32],
                    into=(early_buf, (None, D_MODEL, PACK_W), lambda i, j: (i, 1, 0)))
    early_buf = _mm("g_mlp_up", cn, d_up, "tn", 1024, 1024, 1024,[F32],
                    into=(early_buf, (None, D_MODEL, PACK_W), lambda i, j: (j, 0, 0)))
    dh1, dh1_b, dg_mlp = _mm("d_mlp_up", d_up, mlp_all, "nt", 1024, 1024, 1024, [F32, BF], epilogue=add_norm_bwd,
                             b_view=(D_MODEL, chip_block, lambda j, k: (k, 0, 0)),
                             extras=(dh2, h1), consts=(gl,), sums=[D_MODEL])
    d_mixed = _mm("d_out_proj", dh1_b, W["w_out"], "nt", 1024,1024, 1024, [F32])
    early_buf = _mm("g_out_proj", mixed, dh1_b, "tn", 1024, 1024, 1024, [F32],
                    into=(early_buf, (N_CHIPS, shard_rows, PACK_W), lambda i, j: (0, 2 * D_MODEL // shard_rows + 1, 0)))

    def gate_bwd(dm, gn, gd, un, ud):
        gn, gd, un, ud = (t.astype(F32) for t in (gn, gd, un, ud))
        sn, sd = _sigmoid(gn), _sigmoid(gd)
        return jnp.concatenate([dm * un * sn * (1.0 - sn), dm * ud * sd * (1.0 - sd)], axis=1), dm * sn, dm * sd

    dz_gates, d_u_na, d_u_dil = _rowwise(
        "gate_mix_bwd", gate_bwd, T, TM,
        [_row(d_mixed, TM), _row(z_gates, TM, 0, D_MODEL), _row(z_gates, TM, 1, D_MODEL), _row(u_na, TM), _row(u_dil, TM)],
        [(2 * D_MODEL, BF), (D_MODEL, BF), (D_MODEL, BF)])
    g_branch_na = _mm("g_branch_na", y_na, d_u_na, "tn", 1024, 1024, 1024,[F32])
    g_branch_dil = _mm("g_branch_dil", y_dil, d_u_dil, "tn", 256, 1024, 1024,[F32])
    small_rows = [jnp.concatenate([_pack_rows(g[:, s * shard_rows:(s + 1) * shard_rows]) for g in (g_ple_proj, g_branch_na, g_branch_dil)],
                                  axis=0) for s in range(N_CHIPS)]
    early_buf = lax.dynamic_update_slice(early_buf, jnp.stack(small_rows), (0, 2 * D_MODEL + 2 * shard_rows, 0))
    early_tm = early_rows // 4
    swap_flight = _swap_start(early_buf, "early")
    d_y_na = _mm("d_branch_na", d_u_na, W["w_branch_na"], "nt", 1024,512, 1024, [BF], after=(swap_flight.token,))
    d_y_dil = _mm("d_branch_dil", d_u_dil, W["w_branch_dil"], "nt", 1024,256, 1024, [F32])

    dqa, dka, dva, dtab = _na_bwd(qkv, tab, d_y_na)
    early_g, early_got = _swap_wait(swap_flight, dqa, "early")
    early_pair, early_pair_b = _pair_sum(early_g, early_got, early_tm)
    scatter_flight = _scatter_start(early_pair_b, "early")

    do_res, dlse_res = _dil_merge_bwd(d_y_dil, o_nat, w_grp, TM, after=(scatter_flight.token,))
    d_dil = [_band_bwd(*dil_ops[g], do_res[g], dlse_res[g], g) for g in range(len(DIL_GROUPS))]

    dz_qkv = _qkv_unprep((dqa, dka, dva), d_dil, cos2, sin_signed, TM)
    g_in_parts = [_mm("g_in_qkv", a, dz_qkv, "tn", 1024, 1280, 1024,[F32]), _mm("g_in_gates", a, dz_gates, "tn", 1024, 1024, 1024,[F32])]
    early_mine = _chip_sum(early_pair, _scatter_wait(scatter_flight, g_in_parts[1], "early"), early_tm)
    join_flight = _join_start(early_mine)
    in_cols = shards["w_in"].shape[1]

    def owner_columns(s):
        lo, hi, split = s * in_cols, (s + 1) * in_cols, g_in_parts[0].shape[1]
        pieces = [g_in_parts[0][:, lo:min(hi, split)]] if lo < split else []
        pieces += [g_in_parts[1][:, max(lo, split) - split:hi - split]] if hi > split else []
        return pieces[0] if len(pieces) == 1 else jnp.concatenate(pieces, axis=1)

    late_tm = in_cols // 4
    late_swap = _swap_start(jnp.stack([owner_columns(s).T for s in range(N_CHIPS)]), "late")
    d_a = _mm("d_in_qkv", dz_qkv, W["w_in"], "nt", 1024,1024, 1280, [F32], after=(late_swap.token, join_flight.token),
              b_view=(D_MODEL, (D_MODEL, 1280), lambda j, k: (j, k)))
    late_g, late_got = _swap_wait(late_swap, d_a, "late")
    late_pair, late_pair_b = _pair_sum(late_g, late_got, late_tm)
    late_scatter = _scatter_start(late_pair_b, "late")
    d_rpb = _na_rpb_grad(dtab, after=(late_scatter.token,))[:, :2 * NA_WIN_ROWS - 1, :2 * NA_WIN_COLS - 1]
    def first_bwd(dn_gates, dn_qkv, dh_out, h, g):
        dh, dg = _rms_bwd(dn_gates + dn_qkv, h, g)
        return dh_out + dh, dg

    grad_x, dg_mix = _mm("d_in_gates", dz_gates, w_gates, "nt", 512, 1024, 1024, [F32], epilogue=first_bwd,
                         extras=(d_a, dh1, xs), consts=(gm,), sums=[D_MODEL], after=(late_scatter.token,))
    early_shard = _join_wait(join_flight, grad_x)

    n_rpb = rpb.size
    rpb_rows = 4
    small = jnp.concatenate([
        dg_mix, dg_mlp, dg_ple, dg_final,
        jnp.pad(d_rpb.reshape(-1), (0, rpb_rows * D_MODEL - n_rpb)).reshape(rpb_rows, D_MODEL),
        jnp.pad(loss_part, ((0, 0), (0, D_MODEL - loss_part.shape[1]))),
        jnp.zeros((SMALL_ROWS - 5 - rpb_rows, D_MODEL), F32)], axis=0)
    out = {"grad": {}, "delta": {}, "new_m": {}, "new_v": {}}

    def update(n, g, row0):
        res = _adamw("adamw_" + n, g, row0, params[n], params["m_" + n], params["v_" + n])
        for kind, t in zip(("grad", "delta", "new_m", "new_v"), res, strict=True):
            out[kind][n] = t

    row0 = 0
    for n in REDUCE_EARLY:
        rows, cols = early_shapes[n]
        n_rows = rows * cols // PACK_W
        if cols == PACK_W:
            update(n, early_shard, row0)
        else:
            update(n, early_shard[row0:row0 + n_rows].reshape(rows, cols), 0)
        row0 += n_rows
    late_others = _scatter_wait(late_scatter, out["new_v"][REDUCE_EARLY[-1]], "late")
    late_mine = _chip_sum(late_pair, late_others, late_tm)
    small = _allreduce_small(small, after=(late_mine,))
    res = _adamw("adamw_w_in", _join_halves(late_mine), 0, *[jnp.swapaxes(params[n], 1, 2) for n in ("w_in", "m_w_in", "v_w_in")])
    for kind, t in zip(("grad", "delta", "new_m", "new_v"), res, strict=True):
        out[kind]["w_in"] = jnp.swapaxes(t, 1, 2)
    loss = small[4 + rpb_rows, 0]

    def small_pack(a0, a1, a2, a3, r):
        return jnp.concatenate([a0.reshape(1, -1), a1.reshape(1, -1), a2.reshape(1, -1), a3.reshape(1, -1),
                                jnp.pad(r.reshape(-1), (0, rpb_rows * D_MODEL - n_rpb)).reshape(rpb_rows, D_MODEL)], axis=0)

    small_res = _adamw("adamw_small", small, 0, small_pack(g_mix, g_mlp, g_ple, g_final, rpb)[None],
                       small_pack(m_g_mix, m_g_mlp, m_g_ple, m_g_final, m_rpb)[None],
                       small_pack(v_g_mix, v_g_mlp, v_g_ple, v_g_final, v_rpb)[None])

    def small_unpack(t):
        return {"g_mix": t[0].reshape(g_mix.shape), "g_mlp": t[1].reshape(g_mlp.shape), "g_ple": t[2].reshape(g_ple.shape),
                "g_final": t[3].reshape(g_final.shape), "rpb": t[4:].reshape(-1)[:n_rpb].reshape(rpb.shape)}

    for kind, t in zip(("grad", "delta", "new_m", "new_v"), small_res, strict=True):
        out[kind].update(small_unpack(t[0]))

    order = ["g_mix", "w_in", "rpb", "w_branch_na", "w_branch_dil", "w_out", "g_mlp", "w_up", "w_down", "g_ple",
             "w_ple_gate", "w_ple_proj", "g_final"]
    return (loss, grad_x[None], *[out["grad"][n] for n in order], *[out["delta"][n] for n in order],
            *[out["new_m"][n] for n in order], *[out["new_v"][n] for n in order])
```

```python
import functools
from typing import NamedTuple

import jax
import jax.numpy as jnp
from jax import lax
from jax.experimental import pallas as pl
from jax.experimental.pallas import tpu as pltpu

BF = jnp.bfloat16
F32 = jnp.float32
MESH = pl.DeviceIdType.MESH
ANY = pl.BlockSpec(memory_space=pl.ANY)

V7X_VMEM_BYTES = 64 * 1024 * 1024
VMEM_LIMIT = V7X_VMEM_BYTES - 16 * 1024 * 1024

D_MODEL = 1024
HEAD_DIM = 64
GRID_W = 64
NA_HEADS = 8
NA_WIN_ROWS = 8
NA_WIN_COLS = 16
NA_WIDTH = NA_HEADS * HEAD_DIM
DIL_GROUPS = ((128, 1), (512, 4), (2048, 16))
DIL_HPG = 4
DIL_HEADS = DIL_HPG * len(DIL_GROUPS)
DIL_WIDTH = DIL_HEADS * HEAD_DIM
DIL_OUT_WIDTH = DIL_HPG * HEAD_DIM
DIL_RADIUS = 64
QKV_WIDTH = 3 * NA_WIDTH + 3 * DIL_WIDTH
D_FF = 4 * D_MODEL
PLE_DIM = 256
ROPE_THETA = 10000.0
RMS_EPS = 1e-6
NEG_INF = -1e30
Q_SCALE = HEAD_DIM ** -0.5

ADAM_LR = 0.001
ADAM_B1 = 0.9
ADAM_B2 = 0.999
ADAM_EPS = 1e-08
ADAM_WD = 0.01
ADAM_STEP = 10

N_CHIPS = 4
N_DEV = 8
PACK_W = 1024
BIG = ("w_in", "w_branch_na", "w_branch_dil", "w_out", "w_up", "w_down", "w_ple_gate", "w_ple_proj")
GATHER_MIX = ("w_branch_na", "w_branch_dil", "w_out")
GATHER_MLP = ("w_up", "w_down", "w_ple_gate", "w_ple_proj")
REDUCE_EARLY = ("w_up", "w_down", "w_ple_gate", "w_out", "w_ple_proj", "w_branch_na", "w_branch_dil")
SMALL_ROWS = 16


def _cparams(sem=None):
    return pltpu.CompilerParams(dimension_semantics=sem, vmem_limit_bytes=VMEM_LIMIT)


def _mm(name, a, b, mode, tm, tn, tk, out_dtypes, epilogue=None, extras=(), consts=(), sums=(), after=(), into=None,
        b_view=None):
    if mode == "nn":
        (M, K), N = a.shape, b.shape[1]
    elif mode == "nt":
        (M, K), N = a.shape, b.shape[0]
    else:
        (K, M), N = a.shape, b.shape[1]
    if b_view is not None:
        N = b_view[0]
    tm, tn, tk = min(tm, M), min(tn, N), min(tk, K)
    assert M % tm == 0 and N % tn == 0 and K % tk == 0, (name, M, N, K, tm, tn, tk)
    if mode == "nn":
        a_spec = pl.BlockSpec((tm, tk), lambda i, j, k: (i, k))
        b_spec = pl.BlockSpec((tk, tn), lambda i, j, k: (k, j))
        dims = (((1,), (0,)), ((), ()))
    elif mode == "nt":
        a_spec = pl.BlockSpec((tm, tk), lambda i, j, k: (i, k))
        b_spec = pl.BlockSpec((tn, tk), lambda i, j, k: (j, k))
        dims = (((1,), (1,)), ((), ()))
    else:
        a_spec = pl.BlockSpec((tk, tm), lambda i, j, k: (k, i))
        b_spec = pl.BlockSpec((tk, tn), lambda i, j, k: (k, j))
        dims = (((0,), (0,)), ((), ()))
    if b_view is not None:
        b_spec = pl.BlockSpec(b_view[1], lambda i, j, k: b_view[2](j, k))
    nk = K // tk
    n_extra, n_const, n_out, n_sum = len(extras), len(consts), len(out_dtypes), len(sums)
    tile = pl.BlockSpec((tm, tn), lambda i, j, k: (i, j))
    assert not sums or tn == N, "row sums need whole rows in a tile"

    n_after = len(after)

    def body(a_ref, b_ref, *rest):
        extra_refs, rest = rest[:n_extra + n_const], rest[n_extra + n_const + n_after:]
        out_refs, sum_refs, acc = rest[:n_out], rest[n_out:n_out + n_sum], rest[-1]
        i, k = pl.program_id(0), pl.program_id(2)
        def product():
            return lax.dot_general(a_ref[...].astype(BF), b_ref[...].astype(BF), dims, preferred_element_type=F32)

        if nk > 1:
            @pl.when(k == 0)
            def _():
                acc[...] = jnp.zeros_like(acc)

            acc[...] += product()

        @pl.when(k == nk - 1)
        def _():
            total = product() if nk == 1 else acc[...]
            outs = (total,) if epilogue is None else epilogue(total, *[e[...] for e in extra_refs])
            for o_ref, val in zip(out_refs, outs[:n_out], strict=True):
                o_ref[...] = val.astype(o_ref.dtype).reshape(o_ref.shape)
            for s_ref, val in zip(sum_refs, outs[n_out:], strict=True):
                @pl.when(i == 0)
                def _():
                    s_ref[...] = val

                @pl.when(i != 0)
                def _():
                    s_ref[...] += val

    out_specs = [tile] * n_out + [pl.BlockSpec((1, c), lambda i, j, k: (0, 0)) for c in sums]
    out_shape = [jax.ShapeDtypeStruct((M, N), dt) for dt in out_dtypes] + [jax.ShapeDtypeStruct((1, c), F32) for c in sums]
    operands, aliases = [a, b, *extras, *consts, *after], {}
    in_specs = ([a_spec, b_spec] + [tile] * n_extra
                + [pl.BlockSpec(c.shape, functools.partial(lambda nd, i, j, k: (0,) * nd, c.ndim)) for c in consts] + [ANY] * n_after)
    if into is not None:
        assert n_out == 1
        target, block, index = into
        out_specs = [pl.BlockSpec(block, lambda i, j, k: index(i, j))]
        out_shape = [jax.ShapeDtypeStruct(target.shape, target.dtype)]
        if not isinstance(target, jax.ShapeDtypeStruct):
            aliases = {len(operands): 0}
            operands.append(target)
            in_specs.append(ANY)
            n_after += 1

    outs = pl.pallas_call(
        body, name=name, grid=(M // tm, N // tn, nk),
        in_specs=in_specs, out_specs=out_specs, out_shape=out_shape,
        scratch_shapes=[pltpu.VMEM((tm, tn) if nk > 1 else (8, 128), F32)], input_output_aliases=aliases,
        compiler_params=_cparams(("arbitrary",) * 3 if sums else ("parallel", "parallel", "arbitrary")),
    )(*operands)
    return outs[0] if len(outs) == 1 else outs


def _row(arr, tm, col_block=None, width=None):
    width = arr.shape[1] if width is None else width
    cb = 0 if col_block is None else col_block
    return arr, pl.BlockSpec((tm, width), lambda i: (i, cb))


def _full(arr):
    nd = arr.ndim
    return arr, pl.BlockSpec(arr.shape, lambda i: (0,) * nd)


def _rowwise(name, body, T, tm, ins, outs, sums=(), after=()):
    n_in, n_out, n_sum, n_after = len(ins), len(outs), len(sums), len(after)

    def kern(*refs):
        in_refs, refs = refs[:n_in], refs[n_in + n_after:]
        out_refs, sum_refs = refs[:n_out], refs[n_out:]
        res = body(*[r[...] for r in in_refs])
        res = res if isinstance(res, tuple) else (res,)
        for o_ref, val in zip(out_refs, res[:n_out], strict=True):
            o_ref[...] = val.astype(o_ref.dtype)
        if n_sum:
            @pl.when(pl.program_id(0) == 0)
            def _():
                for s_ref in sum_refs:
                    s_ref[...] = jnp.zeros_like(s_ref)

            for s_ref, val in zip(sum_refs, res[n_out:], strict=True):
                s_ref[...] += val

    res = pl.pallas_call(
        kern, name=name, grid=(T // tm,),
        in_specs=[spec for _, spec in ins] + [ANY] * n_after,
        out_specs=[pl.BlockSpec((tm, c), lambda i: (i, 0)) for c, _ in outs]
        + [pl.BlockSpec((1, c), lambda i: (0, 0)) for c in sums],
        out_shape=[jax.ShapeDtypeStruct((T, c), dt) for c, dt in outs]
        + [jax.ShapeDtypeStruct((1, c), F32) for c in sums],
        compiler_params=_cparams(("arbitrary",)),
    )(*[a for a, _ in ins], *after)
    return res[0] if len(res) == 1 else res


def _sigmoid(x):
    return 1.0 / (1.0 + jnp.exp(-x))


def _rms(h):
    return lax.rsqrt(jnp.mean(h * h, axis=-1, keepdims=True) + RMS_EPS)


def _rms_bwd(dy, h, g):
    r = _rms(h)
    n = h * r
    dn = dy * g
    dh = r * (dn - n * jnp.mean(dn * n, axis=-1, keepdims=True))
    return dh, jnp.sum(dy * n, axis=0, keepdims=True)


def _rope(x, cos2, sin_signed):
    lane = lax.broadcasted_iota(jnp.int32, x.shape, 1)
    swapped = jnp.where((lane % HEAD_DIM) < HEAD_DIM // 2, pltpu.roll(x, 128 - HEAD_DIM // 2, 1), pltpu.roll(x, HEAD_DIM // 2, 1))
    return x * cos2 + swapped * sin_signed


NA_KEYS = NA_WIN_ROWS * GRID_W
NA_BASES = 8


def _na_row_geometry(r, rows):
    first = jnp.clip(r - NA_WIN_ROWS // 2, 0, rows - NA_WIN_ROWS)
    base = first - r + (NA_WIN_ROWS - 1)
    return pl.multiple_of(first * GRID_W, GRID_W), base


NA_ROWS_PER_STEP = 16
NA_BWD_ROWS_PER_STEP = 8


def _softmax_rows(s):
    p = jnp.exp(s - jnp.max(s, axis=-1, keepdims=True))
    return p / jnp.sum(p, axis=-1, keepdims=True)


def _na_probs(q, kw, bias):
    return _softmax_rows(lax.dot_general(q, kw, (((1,), (1,)), ((), ())), preferred_element_type=F32) + bias)


def _split_pair(t):
    first = lax.broadcasted_iota(jnp.int32, t.shape, 1) < HEAD_DIM
    zero = jnp.zeros_like(t)
    return jnp.where(first, t, zero), jnp.where(first, zero, t)


def _join_pair(a, b):
    return jnp.where(lax.broadcasted_iota(jnp.int32, a.shape, 1) < HEAD_DIM, a, b)


_NT = (((1,), (1,)), ((), ()))
_TN = (((0,), (0,)), ((), ()))


def _na_fwd(qkv, tab):
    T = qkv.shape[0]
    rows = T // GRID_W
    n_pairs = NA_WIDTH // 128

    def body(q_ref, k_ref, v_ref, tab_ref, y_ref):
        def step(it, carry):
            geo = [_na_row_geometry(it * NA_ROWS_PER_STEP + u, rows) for u in range(NA_ROWS_PER_STEP)]
            q0s = [pl.multiple_of((it * NA_ROWS_PER_STEP + u) * GRID_W, GRID_W) for u in range(NA_ROWS_PER_STEP)]
            ss = [lax.dot_general(jnp.concatenate(_split_pair(q_ref[pl.ds(q0, GRID_W), :] * Q_SCALE), axis=0),
                                  k_ref[pl.ds(k0, NA_KEYS), :], _NT, preferred_element_type=F32)
                  for q0, (k0, _) in zip(q0s, geo)]
            ps = [_softmax_rows(s + jnp.concatenate([tab_ref[0, base], tab_ref[1, base]], axis=0)) for s, (_, base) in zip(ss, geo)]
            ys = [jnp.dot(p.astype(BF), v_ref[pl.ds(k0, NA_KEYS), :], preferred_element_type=F32) for p, (k0, _) in zip(ps, geo)]
            for q0, y2 in zip(q0s, ys):
                y_ref[pl.ds(q0, GRID_W), :] = _join_pair(y2[:GRID_W], y2[GRID_W:]).astype(y_ref.dtype)
            return carry

        lax.fori_loop(0, rows // NA_ROWS_PER_STEP, step, 0)

    def cols(first):
        return pl.BlockSpec((T, 128), lambda j: (0, first + j))

    return pl.pallas_call(
        body, name="na_fwd", grid=(n_pairs,),
        in_specs=[cols(0), cols(n_pairs), cols(2 * n_pairs), pl.BlockSpec((2, NA_BASES, GRID_W, NA_KEYS), lambda j: (j, 0, 0, 0))],
        out_specs=cols(0), out_shape=jax.ShapeDtypeStruct((T, NA_WIDTH), BF),
        compiler_params=_cparams(("parallel",)),
    )(qkv, qkv, qkv, tab)


def _na_bwd(qkv, tab, do):
    T = qkv.shape[0]
    rows = T // GRID_W
    n_pairs = NA_WIDTH // 128

    def body(q_ref, k_ref, v_ref, tab_ref, do_ref, dq_ref, dk_ref, dv_ref, dtab_ref):
        dk_ref[...] = jnp.zeros_like(dk_ref)
        dv_ref[...] = jnp.zeros_like(dv_ref)
        dtab_ref[...] = jnp.zeros_like(dtab_ref)

        def step(it, carry):
            U = NA_BWD_ROWS_PER_STEP
            geo = [_na_row_geometry(it * U + u, rows) for u in range(U)]
            q0s = [pl.multiple_of((it * U + u) * GRID_W, GRID_W) for u in range(U)]
            q2s = [jnp.concatenate(_split_pair(q_ref[pl.ds(q0, GRID_W), :] * Q_SCALE), axis=0) for q0 in q0s]
            do2s = [jnp.concatenate(_split_pair(do_ref[pl.ds(q0, GRID_W), :]), axis=0) for q0 in q0s]
            ss = [lax.dot_general(q2, k_ref[pl.ds(k0, NA_KEYS), :], _NT, preferred_element_type=F32) for q2, (k0, _) in zip(q2s, geo)]
            dps = [lax.dot_general(do2, v_ref[pl.ds(k0, NA_KEYS), :], _NT, preferred_element_type=F32) for do2, (k0, _) in zip(do2s, geo)]
            ps = [_softmax_rows(s + jnp.concatenate([tab_ref[0, base], tab_ref[1, base]], axis=0)) for s, (_, base) in zip(ss, geo)]
            dss = [p * (dp - jnp.sum(dp * p, axis=-1, keepdims=True)) for p, dp in zip(ps, dps)]
            dvs = [lax.dot_general(p.astype(BF), do2, _TN, preferred_element_type=F32) for p, do2 in zip(ps, do2s)]
            dsbs = [ds.astype(BF) for ds in dss]
            dqs = [jnp.dot(dsb, k_ref[pl.ds(k0, NA_KEYS), :], preferred_element_type=F32) for dsb, (k0, _) in zip(dsbs, geo)]
            dks = [lax.dot_general(dsb, q2, _TN, preferred_element_type=F32) for dsb, q2 in zip(dsbs, q2s)]
            for u in range(U):
                k0, base = geo[u]
                dtab_ref[0, base] += dss[u][:GRID_W]
                dtab_ref[1, base] += dss[u][GRID_W:]
                dq_ref[pl.ds(q0s[u], GRID_W), :] = _join_pair(dqs[u][:GRID_W], dqs[u][GRID_W:])
                dk_ref[pl.ds(k0, NA_KEYS), :] += dks[u]
                dv_ref[pl.ds(k0, NA_KEYS), :] += dvs[u]
            return carry

        lax.fori_loop(0, rows // NA_BWD_ROWS_PER_STEP, step, 0)

    def cols(first):
        return pl.BlockSpec((T, 128), lambda j: (0, first + j))

    tabs = pl.BlockSpec((2, NA_BASES, GRID_W, NA_KEYS), lambda j: (j, 0, 0, 0))
    wide = jax.ShapeDtypeStruct((T, NA_WIDTH), F32)
    return pl.pallas_call(
        body, name="na_bwd", grid=(n_pairs,),
        in_specs=[cols(0), cols(n_pairs), cols(2 * n_pairs), tabs, cols(0)],
        out_specs=[cols(0), cols(0), cols(0), tabs],
        out_shape=[wide, wide, wide, jax.ShapeDtypeStruct((NA_HEADS, NA_BASES, GRID_W, NA_KEYS), F32)],
        compiler_params=_cparams(("parallel",)),
    )(qkv, qkv, qkv, tab, do)


def _na_bias_table(rpb):
    H, n_rows, n_cols = rpb.shape

    def body(r_ref, tab_ref):
        q = lax.broadcasted_iota(jnp.int32, (GRID_W, 128), 0)
        kc = lax.broadcasted_iota(jnp.int32, (GRID_W, 128), 1)
        first = jnp.clip(q - NA_WIN_COLS // 2, 0, GRID_W - NA_WIN_COLS)
        valid = (kc >= first) & (kc < first + NA_WIN_COLS)
        toeplitz = []
        for ro in range(n_rows):
            row = jnp.broadcast_to(r_ref[pl.ds(ro, 1), :], (GRID_W, 128))
            shifted = pltpu.roll(pltpu.roll(row, 128 - (NA_WIN_COLS - 1), 1), 0, 1, stride=1, stride_axis=0)
            toeplitz.append(jnp.where(valid, shifted, NEG_INF))
        for base in range(NA_BASES):
            for j in range(NA_WIN_ROWS // 2):
                even, odd = toeplitz[base + 2 * j], toeplitz[base + 2 * j + 1]
                tab_ref[base, :, pl.ds(j * 128, 128)] = jnp.where(kc < GRID_W, even, pltpu.roll(odd, GRID_W, 1))

    padded = jnp.pad(rpb, ((0, 0), (0, 16 - n_rows), (0, 128 - n_cols)))
    return pl.pallas_call(
        body, name="na_bias_table", grid=(H,),
        in_specs=[pl.BlockSpec((None, 16, 128), lambda h: (h, 0, 0))],
        out_specs=pl.BlockSpec((None, NA_BASES, GRID_W, NA_KEYS), lambda h: (h, 0, 0, 0)),
        out_shape=jax.ShapeDtypeStruct((H, NA_BASES, GRID_W, NA_KEYS), F32),
        compiler_params=_cparams(("parallel",)),
    )(padded)


def _na_rpb_grad(dtab, after=()):
    H = dtab.shape[0]
    n_rows = 2 * NA_WIN_ROWS - 1
    n_cols = 2 * NA_WIN_COLS - 1

    def body(d_ref, *rest):
        o_ref = rest[-1]
        lane = lax.broadcasted_iota(jnp.int32, (GRID_W, 128), 1)
        low = lane < GRID_W
        out_rows = []
        for ro in range(n_rows):
            acc = jnp.zeros((GRID_W, 128), F32)
            for base in range(NA_BASES):
                i = ro - base
                if not 0 <= i < NA_WIN_ROWS:
                    continue
                pair = d_ref[base, :, pl.ds((i // 2) * 128, 128)]
                if i % 2:
                    pair = pltpu.roll(pair, GRID_W, 1)
                acc = acc + jnp.where(low, pair, 0.0)
            skew = pltpu.roll(acc, 0, 1, stride=1, stride_axis=0)
            diag = jnp.sum(skew, axis=0, keepdims=True)
            out_rows.append(pltpu.roll(jnp.broadcast_to(diag, (8, 128)), 128 - (GRID_W - NA_WIN_COLS), 1)[:1])
        out_rows.append(jnp.zeros((1, 128), F32))
        res = jnp.concatenate(out_rows, axis=0)
        o_ref[...] = jnp.where(lax.broadcasted_iota(jnp.int32, res.shape, 1) < n_cols, res, 0.0)

    return pl.pallas_call(
        body, name="na_rpb_grad", grid=(H,),
        in_specs=[pl.BlockSpec((None, NA_BASES, GRID_W, NA_KEYS), lambda h: (h, 0, 0, 0))] + [ANY] * len(after),
        out_specs=pl.BlockSpec((None, n_rows + 1, 128), lambda h: (h, 0, 0)),
        out_shape=jax.ShapeDtypeStruct((H, n_rows + 1, 128), F32),
        compiler_params=_cparams(("parallel",)),
    )(jnp.flip(dtab, axis=2), *after)


BAND_Q = 128
BAND_KEYS = BAND_Q + 2 * DIL_RADIUS


def _band_geometry(n, L):
    q0 = pl.multiple_of(n * BAND_Q, BAND_Q)
    k0 = pl.multiple_of(jnp.clip(q0 - DIL_RADIUS, 0, L - BAND_KEYS), DIL_RADIUS)
    qi = q0 + lax.broadcasted_iota(jnp.int32, (BAND_Q, BAND_KEYS), 0)
    kj = k0 + lax.broadcasted_iota(jnp.int32, (BAND_Q, BAND_KEYS), 1)
    return q0, k0, jnp.abs(qi - kj) <= DIL_RADIUS


DIL_PAIRS = DIL_OUT_WIDTH // 128


def _residue_shape(dil, T, dtype):
    return jax.ShapeDtypeStruct((DIL_PAIRS, dil, T // dil, 128), dtype)


def _residue_tile(dil, tm):
    return pl.BlockSpec((DIL_PAIRS, dil, tm // dil, 128), lambda i: (0, 0, i, 0))


def _to_natural(ref, scratch, dil, tm):
    tiles = []
    for pair in range(DIL_PAIRS):
        if dil == 1:
            tiles.append(ref[pair, 0].astype(F32))
            continue
        for r in range(dil):
            scratch[pl.ds(r, tm // dil, stride=dil), :] = ref[pair, r].astype(F32)
        tiles.append(scratch[...])
    return tiles


def _from_natural(tile, scratch, ref, pair, dil, tm):
    if dil == 1:
        ref[pair, 0] = tile.astype(ref.dtype)
        return
    scratch[...] = tile
    for r in range(dil):
        ref[pair, r] = scratch[pl.ds(r, tm // dil, stride=dil), :].astype(ref.dtype)


def _band_specs(group, T):
    dil = DIL_GROUPS[group][1]
    L = T // dil
    assert L % BAND_Q == 0 and L >= BAND_KEYS, (T, dil)
    per_residue = min(BAND_BLOCKS_PER_STEP, L // BAND_Q)
    residues = min(dil, BAND_BLOCKS_PER_STEP // per_residue)
    spec = pl.BlockSpec((None, residues, L, 128), lambda s: (s % DIL_PAIRS, s // DIL_PAIRS, 0, 0))
    return L, residues, per_residue, (dil // residues * DIL_PAIRS,), spec


BAND_BLOCKS_PER_STEP = 8


def _band_softmax(s, valid):
    s = jnp.where(valid, s, NEG_INF)
    m = jnp.max(s, axis=-1, keepdims=True)
    p = jnp.exp(s - m)
    l = jnp.sum(p, axis=-1, keepdims=True)
    return p / l, m + jnp.log(l)


def _band_fwd(q, k, v, group):
    T = q.shape[1] * q.shape[2]
    L, residues, U, grid, spec = _band_specs(group, T)

    def body(q_ref, k_ref, v_ref, o_ref, lse_ref):
        def step(it, carry):
            geo = [(r, *_band_geometry(it * U + u, L)) for r in range(residues) for u in range(U)]
            ss = [lax.dot_general(jnp.concatenate(_split_pair(q_ref[r, pl.ds(q0, BAND_Q), :]), axis=0),
                                  k_ref[r, pl.ds(k0, BAND_KEYS), :], _NT, preferred_element_type=F32) for r, q0, k0, _ in geo]
            pls = [_band_softmax(s, jnp.concatenate([valid, valid], axis=0)) for s, (_, _, _, valid) in zip(ss, geo)]
            os = [jnp.dot(p.astype(BF), v_ref[r, pl.ds(k0, BAND_KEYS), :], preferred_element_type=F32)
                  for (p, _), (r, _, k0, _) in zip(pls, geo)]
            for (r, q0, _, _), o2, (_, lse) in zip(geo, os, pls):
                o_ref[r, pl.ds(q0, BAND_Q), :] = _join_pair(o2[:BAND_Q], o2[BAND_Q:])
                lse2 = jnp.broadcast_to(lse, (2 * BAND_Q, 128))
                lse_ref[r, pl.ds(q0, BAND_Q), :] = _join_pair(lse2[:BAND_Q], lse2[BAND_Q:])
            return carry

        lax.fori_loop(0, L // (BAND_Q * U), step, 0)

    res = _residue_shape(DIL_GROUPS[group][1], T, F32)
    return pl.pallas_call(
        body, name=f"band_fwd_g{group}", grid=grid,
        in_specs=[spec] * 3, out_specs=[spec] * 2, out_shape=[res, res],
        compiler_params=_cparams(("parallel",)),
    )(q, k, v)


def _band_bwd(q, k, v, do, dlse, group):
    T = q.shape[1] * q.shape[2]
    L, residues, U, grid, spec = _band_specs(group, T)

    def body(q_ref, k_ref, v_ref, do_ref, dlse_ref, dq_ref, dk_ref, dv_ref):
        dk_ref[...] = jnp.zeros_like(dk_ref)
        dv_ref[...] = jnp.zeros_like(dv_ref)

        def step(it, carry):
            geo = [(r, *_band_geometry(it * U + u, L)) for r in range(residues) for u in range(U)]
            q2s = [jnp.concatenate(_split_pair(q_ref[r, pl.ds(q0, BAND_Q), :]), axis=0) for r, q0, _, _ in geo]
            do2s = [jnp.concatenate(_split_pair(do_ref[r, pl.ds(q0, BAND_Q), :]), axis=0) for r, q0, _, _ in geo]
            ss = [lax.dot_general(q2, k_ref[r, pl.ds(k0, BAND_KEYS), :], _NT, preferred_element_type=F32)
                  for q2, (r, _, k0, _) in zip(q2s, geo)]
            dps = [lax.dot_general(do2, v_ref[r, pl.ds(k0, BAND_KEYS), :], _NT, preferred_element_type=F32)
                   for do2, (r, _, k0, _) in zip(do2s, geo)]
            ps = [_band_softmax(s, jnp.concatenate([valid, valid], axis=0))[0] for s, (_, _, _, valid) in zip(ss, geo)]
            dss = []
            for p, dp, (r, q0, _, _) in zip(ps, dps, geo):
                dl = dlse_ref[r, pl.ds(q0, BAND_Q), :]
                dl2 = jnp.concatenate([dl[:, :1], dl[:, HEAD_DIM:HEAD_DIM + 1]], axis=0)
                dss.append(p * (dp - jnp.sum(dp * p, axis=-1, keepdims=True) + dl2))
            dvs = [lax.dot_general(p.astype(BF), do2, _TN, preferred_element_type=F32) for p, do2 in zip(ps, do2s)]
            dsbs = [ds.astype(BF) for ds in dss]
            dqs = [jnp.dot(dsb, k_ref[r, pl.ds(k0, BAND_KEYS), :], preferred_element_type=F32) for dsb, (r, _, k0, _) in zip(dsbs, geo)]
            dks = [lax.dot_general(dsb, q2, _TN, preferred_element_type=F32) for dsb, q2 in zip(dsbs, q2s)]
            for u, (r, q0, k0, _) in enumerate(geo):
                dq_ref[r, pl.ds(q0, BAND_Q), :] = _join_pair(dqs[u][:BAND_Q], dqs[u][BAND_Q:])
                dk_ref[r, pl.ds(k0, BAND_KEYS), :] += dks[u]
                dv_ref[r, pl.ds(k0, BAND_KEYS), :] += dvs[u]
            return carry

        lax.fori_loop(0, L // (BAND_Q * U), step, 0)

    res = _residue_shape(DIL_GROUPS[group][1], T, F32)
    return pl.pallas_call(
        body, name=f"band_bwd_g{group}", grid=grid,
        in_specs=[spec] * 5, out_specs=[spec] * 3, out_shape=[res] * 3,
        compiler_params=_cparams(("parallel",)),
    )(q, k, v, do, dlse)


def _head_sums(t):
    head = lax.broadcasted_iota(jnp.int32, t.shape, 1) // HEAD_DIM
    out = jnp.zeros_like(t)
    for h in range(t.shape[1] // HEAD_DIM):
        mine = head == h
        out = jnp.where(mine, jnp.sum(jnp.where(mine, t, 0.0), axis=-1, keepdims=True), out)
    return out


def _dil_merge_fwd(os, lses, T, tm):
    G = len(DIL_GROUPS)
    W = DIL_OUT_WIDTH
    dils = [d for _, d in DIL_GROUPS]

    def body(*refs):
        o_refs, lse_refs = refs[:G], refs[G:2 * G]
        y_ref, w_refs, on_refs, scratch = refs[2 * G], refs[2 * G + 1:3 * G + 1], refs[3 * G + 1:4 * G + 1], refs[-1]
        o = [jnp.concatenate(_to_natural(r, scratch, d, tm), axis=1) for r, d in zip(o_refs, dils)]
        ls = [jnp.concatenate(_to_natural(r, scratch, d, tm), axis=1) for r, d in zip(lse_refs, dils)]
        m = functools.reduce(jnp.maximum, ls)
        es = [jnp.exp(l - m) for l in ls]
        tot = functools.reduce(jnp.add, es)
        ws = [e / tot for e in es]
        y_ref[...] = functools.reduce(jnp.add, [w * t for w, t in zip(ws, o)]).astype(y_ref.dtype)
        for g in range(G):
            w_refs[g][...] = ws[g]
            on_refs[g][...] = o[g]

    nat = pl.BlockSpec((tm, W), lambda i: (i, 0))
    res = pl.pallas_call(
        body, name="dil_merge_fwd", grid=(T // tm,),
        in_specs=[_residue_tile(d, tm) for d in dils] * 2,
        out_specs=[nat] * (2 * G + 1),
        out_shape=[jax.ShapeDtypeStruct((T, W), BF)] + [jax.ShapeDtypeStruct((T, W), F32)] * (2 * G),
        scratch_shapes=[pltpu.VMEM((tm, 128), F32)],
        compiler_params=_cparams(("parallel",)),
    )(*os, *lses)
    return res[0], res[1:G + 1], res[G + 1:]


def _dil_merge_bwd(dy, os, ws, tm, after=()):
    G = len(DIL_GROUPS)
    T, W = dy.shape
    dils = [d for _, d in DIL_GROUPS]
    n_after = len(after)

    def body(*refs):
        dyt = refs[0][...]
        o, w = [r[...] for r in refs[1:G + 1]], [r[...] for r in refs[G + 1:2 * G + 1]]
        refs = refs[2 * G + 1 + n_after:]
        do_refs, dlse_refs, scratch = refs[:G], refs[G:2 * G], refs[-1]
        dws = [_head_sums(dyt * t) for t in o]
        mean = functools.reduce(jnp.add, [a * b for a, b in zip(w, dws)])
        for g, d in enumerate(dils):
            do, dlse = w[g] * dyt, w[g] * (dws[g] - mean)
            for pair in range(DIL_PAIRS):
                cols = slice(pair * 128, (pair + 1) * 128)
                _from_natural(do[:, cols], scratch, do_refs[g], pair, d, tm)
                _from_natural(dlse[:, cols], scratch, dlse_refs[g], pair, d, tm)

    nat = pl.BlockSpec((tm, W), lambda i: (i, 0))
    res = pl.pallas_call(
        body, name="dil_merge_bwd", grid=(T // tm,),
        in_specs=[nat] * (2 * G + 1) + [ANY] * n_after,
        out_specs=[_residue_tile(d, tm) for d in dils] * 2,
        out_shape=[_residue_shape(d, T, BF) for d in dils] + [_residue_shape(d, T, F32) for d in dils],
        scratch_shapes=[pltpu.VMEM((tm, 128), F32)],
        compiler_params=_cparams(("parallel",)),
    )(dy, *os, *ws, *after)
    return res[:G], res[G:]


def _qkv_prep(z, cos2, sin_signed, tm):
    T = z.shape[0]
    G = len(DIL_GROUPS)
    dils = [d for _, d in DIL_GROUPS]
    n_dil_blocks = 3 * DIL_WIDTH // 128

    def body(*refs):
        blocks = refs[:n_dil_blocks]
        cos_ref, sin_ref = refs[n_dil_blocks], refs[1 + n_dil_blocks]
        outs = refs[2 + n_dil_blocks:]
        for part in range(3):
            for g, d in enumerate(dils):
                out = outs[g * 3 + part]
                for pair in range(DIL_PAIRS):
                    blk = blocks[part * (DIL_WIDTH // 128) + g * DIL_PAIRS + pair]
                    for r in range(d):
                        rows = pl.ds(r, tm // d, stride=d) if d > 1 else slice(None)
                        x = blk[rows, :]
                        if part < 2:
                            x = _rope(x, cos_ref[rows, :], sin_ref[rows, :])
                        if part == 0:
                            x = x * Q_SCALE
                        out[pair, r] = x.astype(out.dtype)

    lane_block = [pl.BlockSpec((tm, 128), functools.partial(lambda b, i: (i, b), b)) for b in range(n_dil_blocks)]
    tab = pl.BlockSpec((tm, 128), lambda i: (i, 0))
    res = pl.pallas_call(
        body, name="qkv_prep", grid=(T // tm,),
        in_specs=lane_block + [tab, tab],
        out_specs=[_residue_tile(d, tm) for d in dils for _ in range(3)],
        out_shape=[_residue_shape(d, T, BF) for d in dils for _ in range(3)],
        compiler_params=_cparams(("parallel",)),
    )(*[z] * n_dil_blocks, cos2, sin_signed)
    return [res[3 * g:3 + 3 * g] for g in range(G)]


def _qkv_unprep(d_na, d_dil, cos2, sin_signed, tm, after=()):
    T = d_na[0].shape[0]
    G = len(DIL_GROUPS)
    dils = [d for _, d in DIL_GROUPS]
    n_after = len(after)

    def body(*refs):
        dq, dk, dv = (r[...] for r in refs[:3])
        res_refs = refs[3:3 + 3 * G]
        cs, sn = refs[3 + 3 * G][...], refs[4 + 3 * G][...]
        out, scratch = refs[5 + 3 * G + n_after], refs[-1]
        cols = [dq * Q_SCALE, dk, dv]
        for part in range(3):
            for g, d in enumerate(dils):
                for x in _to_natural(res_refs[g * 3 + part], scratch, d, tm):
                    if part < 2:
                        x = _rope(x, cs, -sn)
                    cols.append(x * Q_SCALE if part == 0 else x)
        out[...] = jnp.concatenate(cols, axis=1).astype(out.dtype)

    wide = pl.BlockSpec((tm, NA_WIDTH), lambda i: (i, 0))
    tab = pl.BlockSpec((tm, 128), lambda i: (i, 0))
    return pl.pallas_call(
        body, name="qkv_unprep", grid=(T // tm,),
        in_specs=[wide] * 3 + [_residue_tile(d, tm) for d in dils for _ in range(3)] + [tab, tab] + [ANY] * n_after,
        out_specs=pl.BlockSpec((tm, QKV_WIDTH), lambda i: (i, 0)),
        out_shape=jax.ShapeDtypeStruct((T, QKV_WIDTH), BF),
        scratch_shapes=[pltpu.VMEM((tm, 128), F32)],
        compiler_params=_cparams(("parallel",)),
    )(*d_na, *[t for g in range(G) for t in d_dil[g]], cos2, sin_signed, *after)


def _rope_tables(positions):
    half = HEAD_DIM // 2
    inv_freq = ROPE_THETA ** (-jnp.arange(half, dtype=F32) / half)
    ang = positions.astype(F32)[:, None] * inv_freq
    cos, sin = jnp.cos(ang), jnp.sin(ang)
    return jnp.tile(jnp.concatenate([cos, cos], axis=1), (1, 2)), jnp.tile(jnp.concatenate([-sin, sin], axis=1), (1, 2))


def _pack_rows(t):
    return t.reshape(-1, PACK_W)


def _me():
    return lax.axis_index("x"), lax.axis_index("y"), lax.axis_index("c")


def _other_chips(x, y):
    return [(1 - x, y), (x, 1 - y), (1 - x, 1 - y)]


def _pair_sum(g, got, tm):
    S, R, W = g.shape
    half = R // 2
    nb = half // tm

    def body(pos_ref, g_ref, got_ref, own_ref, ob_ref):
        tot = g_ref[...] + got_ref[...]
        ob_ref[...] = tot.astype(ob_ref.dtype)

        @pl.when(pl.program_id(1) == pos_ref[1])
        def _():
            own_ref[...] = tot

    tile = pl.BlockSpec((None, tm, W), lambda i, s, pos_ref: (s, i, 0))
    c, chip = lax.axis_index("c"), 2 * lax.axis_index("x") + lax.axis_index("y")
    return pl.pallas_call(
        body, name="pair_sum",
        grid_spec=pltpu.PrefetchScalarGridSpec(
            num_scalar_prefetch=1, grid=(nb, S),
            in_specs=[pl.BlockSpec((None, tm, W), lambda i, s, pos_ref: (s, pos_ref[0] * nb + i, 0)), tile],
            out_specs=[pl.BlockSpec((tm, W), lambda i, s, pos_ref: (i, 0)), tile]),
        out_shape=[jax.ShapeDtypeStruct((half, W), F32), jax.ShapeDtypeStruct((S, half, W), BF)],
        compiler_params=_cparams(("parallel", "arbitrary")),
    )(jnp.stack([c, chip]).astype(jnp.int32), g, got)


def _chip_sum(own, others, tm):
    n, h, W = others.shape
    nb = h // tm

    def body(c_ref, own_ref, p_ref, o_ref):
        o_ref[...] = ((own_ref[...] + p_ref[0].astype(F32)) + p_ref[1].astype(F32)) + p_ref[2].astype(F32)

    return pl.pallas_call(
        body, name="chip_sum",
        grid_spec=pltpu.PrefetchScalarGridSpec(
            num_scalar_prefetch=1, grid=(nb,),
            in_specs=[pl.BlockSpec((tm, W), lambda i, c_ref: (i, 0)), pl.BlockSpec((n, tm, W), lambda i, c_ref: (0, i, 0))],
            out_specs=pl.BlockSpec((tm, W), lambda i, c_ref: (c_ref[0] * nb + i, 0))),
        out_shape=jax.ShapeDtypeStruct((2 * h, W), F32),
        compiler_params=_cparams(("parallel",)),
    )(lax.axis_index("c").reshape(1).astype(jnp.int32), own, others)


def _join_halves(shard):
    h = shard.shape[0] // 2

    def body(in_ref, out_ref, send_sem, recv_sem):
        x, y, c = _me()
        cp = pltpu.make_async_remote_copy(
            src_ref=in_ref.at[pl.ds(c * h, h), :], dst_ref=out_ref.at[pl.ds(c * h, h), :],
            send_sem=send_sem, recv_sem=recv_sem, device_id=(x, y, 1 - c), device_id_type=MESH)
        cp.start()
        pltpu.make_async_remote_copy(
            src_ref=in_ref.at[pl.ds(c * h, h), :], dst_ref=out_ref.at[pl.ds((1 - c) * h, h), :],
            send_sem=send_sem, recv_sem=recv_sem, device_id=(x, y, 1 - c), device_id_type=MESH).wait_recv()
        cp.wait_send()

    return pl.pallas_call(
        body, name="join_halves", in_specs=[ANY], out_specs=ANY,
        out_shape=jax.ShapeDtypeStruct(shard.shape, shard.dtype), input_output_aliases={0: 0},
        scratch_shapes=[pltpu.SemaphoreType.DMA, pltpu.SemaphoreType.DMA],
    )(shard)


def _allreduce_small(s, after=()):
    R, W = s.shape
    n_after = len(after)

    def body(s_ref, *rest):
        o_ref, buf, send_sems, recv_sems = rest[n_after:]
        x, y, c = _me()
        me = 4 * x + 2 * y + c
        buf[me] = s_ref[...]
        peers = [((x + fx) % 2, (y + fy) % 2, (c + fc) % 2) for fx in range(2) for fy in range(2) for fc in range(2)][1:]
        sends = [pltpu.make_async_remote_copy(
            src_ref=s_ref, dst_ref=buf.at[me], send_sem=send_sems.at[k], recv_sem=recv_sems.at[k],
            device_id=peer, device_id_type=MESH) for k, peer in enumerate(peers)]
        for cp in sends:
            cp.start()
        for k, peer in enumerate(peers):
            pltpu.make_async_remote_copy(
                src_ref=s_ref, dst_ref=buf.at[4 * peer[0] + 2 * peer[1] + peer[2]], send_sem=send_sems.at[k],
                recv_sem=recv_sems.at[k], device_id=peer, device_id_type=MESH).wait_recv()
        for cp in sends:
            cp.wait_send()
        total = buf[0]
        for d in range(1, N_DEV):
            total = total + buf[d]
        o_ref[...] = total

    return pl.pallas_call(
        body, name="allreduce_small",
        in_specs=[pl.BlockSpec(memory_space=pltpu.VMEM)] + [ANY] * n_after, out_specs=pl.BlockSpec(memory_space=pltpu.VMEM),
        out_shape=jax.ShapeDtypeStruct((R, W), F32),
        scratch_shapes=[pltpu.VMEM((N_DEV, R, W), F32), pltpu.SemaphoreType.DMA((N_DEV - 1,)), pltpu.SemaphoreType.DMA((N_DEV - 1,))],
    )(s, *after)


HBM_SPEC = pl.BlockSpec(memory_space=pltpu.HBM)
SEM_SPEC = pl.BlockSpec(memory_space=pltpu.SEMAPHORE)
DATAFLOW = pltpu.SideEffectType.DATAFLOW_SIDE_EFFECTING


class _InFlight(NamedTuple):
    sems: tuple
    src: jax.Array
    land: jax.Array
    token: jax.Array


def _split_start(name, src, land_shape, land_dtype, n, copies, after=()):
    n_after = len(after)

    def body(src_ref, land_ref, *rest):
        rest = rest[n_after:]
        sems, token = rest[:2 * n], rest[-1]
        for k, (s, d, peer) in enumerate(copies(src_ref, land_ref)):
            pltpu.make_async_remote_copy(src_ref=s, dst_ref=d, send_sem=sems[k], recv_sem=sems[n + k],
                                         device_id=peer, device_id_type=MESH).start()
        token[...] = jnp.zeros_like(token)

    outs = pl.pallas_call(
        body, name=name,
        out_shape=(*[pltpu.SemaphoreType.DMA(())] * (2 * n), pltpu.HBM(src.shape, src.dtype), pltpu.HBM(land_shape, land_dtype),
                   jax.ShapeDtypeStruct((8, 128), F32)),
        in_specs=(HBM_SPEC, HBM_SPEC, *[ANY] * n_after),
        out_specs=(*[SEM_SPEC] * (2 * n), HBM_SPEC, HBM_SPEC, pl.BlockSpec(memory_space=pltpu.VMEM)),
        input_output_aliases={0: 2 * n, 1: 2 * n + 1},
        compiler_params=pltpu.CompilerParams(has_side_effects=DATAFLOW),
    )(pltpu.with_memory_space_constraint(src, pltpu.HBM), pltpu.with_memory_space_constraint(lax.empty(land_shape, land_dtype), pltpu.HBM),
      *after)
    return _InFlight(tuple(outs[:2 * n]), outs[2 * n], outs[2 * n + 1], outs[2 * n + 2])


def _split_wait(name, flight, after, n, copies):
    def body(src_ref, land_ref, *rest):
        sems = rest[:2 * n]
        for k, (s, d, peer) in enumerate(copies(src_ref, land_ref)):
            cp = pltpu.make_async_remote_copy(src_ref=s, dst_ref=d, send_sem=sems[k], recv_sem=sems[n + k],
                                              device_id=peer, device_id_type=MESH)
            cp.wait_send()
            cp.wait_recv()

    return pl.pallas_call(
        body, name=name,
        out_shape=(pltpu.HBM(flight.src.shape, flight.src.dtype), pltpu.HBM(flight.land.shape, flight.land.dtype)),
        in_specs=(HBM_SPEC, HBM_SPEC, *[SEM_SPEC] * (2 * n), ANY),
        out_specs=(HBM_SPEC, HBM_SPEC), input_output_aliases={0: 0, 1: 1},
        compiler_params=pltpu.CompilerParams(has_side_effects=DATAFLOW),
    )(flight.src, flight.land, *flight.sems, after)


def _gather_copies(src_ref, land_ref):
    x, y, c = _me()
    return [(src_ref, land_ref.at[2 * x + y], (*chip, c)) for chip in _other_chips(x, y)]


def _gather_start(packed, tag, after=()):
    return _split_start(f"gather_start_{tag}", packed, (N_CHIPS, *packed.shape), packed.dtype, 3, _gather_copies, after)


def _gather_wait(flight, after, tag):
    src, others = _split_wait(f"gather_wait_{tag}", flight, after, 3, _gather_copies)
    return lax.dynamic_update_slice(others, src[None], (2 * lax.axis_index("x") + lax.axis_index("y"), 0, 0))


def _across_copies(src_ref, land_ref):
    x, y, c = _me()
    half = src_ref.shape[0] // 2
    rows = pl.ds(c * half, half)
    return [(src_ref.at[rows, :], land_ref.at[2 * x + y, rows, :], (*chip, c)) for chip in _other_chips(x, y)]


def _to_sibling_copies(all_ref, unused_ref):
    x, y, c = _me()
    half = all_ref.shape[1] // 2
    places = [all_ref.at[2 * chip[0] + chip[1], pl.ds(c * half, half), :] for chip in _other_chips(x, y)]
    return [(place, place, (x, y, 1 - c)) for place in places]


def _gather_halves_start(shard, tag, after=()):
    return _split_start(f"gather_{tag}_across_start", shard, (N_CHIPS, *shard.shape), shard.dtype, 3, _across_copies, after)


def _gather_halves_relay(flight, after, tag):
    shard, landed = _split_wait(f"gather_{tag}_across_wait", flight, after, 3, _across_copies)
    return shard, _split_start(f"gather_{tag}_sibling_start", landed, (8, 128), landed.dtype, 3, _to_sibling_copies)


def _gather_halves_finish(shard, relay, after, tag):
    others = _split_wait(f"gather_{tag}_sibling_wait", relay, after, 3, _to_sibling_copies)[0]
    return lax.dynamic_update_slice(others, shard[None], (2 * lax.axis_index("x") + lax.axis_index("y"), 0, 0))


def _assemble_w_in(shards, tm):
    S, R, C = shards.shape
    n_gates = 2 * D_MODEL

    def body(s_ref, w_ref, g_ref):
        full = jnp.concatenate([s_ref[s] for s in range(S)], axis=1)
        w_ref[...] = full
        g_ref[...] = full[:, S * C - n_gates:]

    return pl.pallas_call(
        body, name="assemble_w_in", grid=(R // tm,),
        in_specs=[pl.BlockSpec((S, tm, C), lambda i: (0, i, 0))],
        out_specs=[pl.BlockSpec((tm, S * C), lambda i: (i, 0)), pl.BlockSpec((tm, n_gates), lambda i: (i, 0))],
        out_shape=[jax.ShapeDtypeStruct((R, S * C), shards.dtype), jax.ShapeDtypeStruct((R, n_gates), shards.dtype)],
        compiler_params=_cparams(("parallel",)),
    )(shards)


def _swap_copies(src_ref, land_ref):
    x, y, c = _me()
    half = land_ref.shape[1]
    return [(src_ref.at[:, pl.ds((1 - c) * half, half), :], land_ref, (x, y, 1 - c))]


def _swap_start(g, tag):
    S, R, W = g.shape
    return _split_start(f"swap_halves_start_{tag}", g, (S, R // 2, W), g.dtype, 1, _swap_copies)


def _swap_wait(flight, after, tag):
    return _split_wait(f"swap_halves_wait_{tag}", flight, after, 1, _swap_copies)


def _scatter_copies(src_ref, land_ref):
    x, y, c = _me()
    return [(src_ref.at[2 * chip[0] + chip[1]], land_ref.at[j], (*chip, c)) for j, chip in enumerate(_other_chips(x, y))]


def _scatter_start(part, tag):
    S, h, W = part.shape
    return _split_start(f"scatter_chips_start_{tag}", part, (S - 1, h, W), part.dtype, 3, _scatter_copies)


def _scatter_wait(flight, after, tag):
    return _split_wait(f"scatter_chips_wait_{tag}", flight, after, 3, _scatter_copies)[1]


def _join_copies(shard_ref, unused_ref):
    x, y, c = _me()
    h = shard_ref.shape[0] // 2
    rows = shard_ref.at[pl.ds(c * h, h), :]
    return [(rows, rows, (x, y, 1 - c))]


def _join_start(shard):
    return _split_start("join_halves_start", shard, (8, 128), shard.dtype, 1, _join_copies)


def _join_wait(flight, after):
    return _split_wait("join_halves_wait", flight, after, 1, _join_copies)[0]


def _adamw(name, g, g_row0, w, m, v):
    _, R, C = w.shape
    tm = next(cand for cand in (368, 256, 128, 64, 32, 16, 8) if R % cand == 0)
    assert g_row0 % tm == 0 and g.shape[1] == C

    def body(g_ref, w_ref, m_ref, v_ref, go_ref, d_ref, mo_ref, vo_ref):
        gt = g_ref[...]
        mt = ADAM_B1 * m_ref[...] + (1.0 - ADAM_B1) * gt
        vt = ADAM_B2 * v_ref[...] + (1.0 - ADAM_B2) * jnp.square(gt)
        m_hat = mt / (1.0 - ADAM_B1 ** ADAM_STEP)
        v_hat = vt / (1.0 - ADAM_B2 ** ADAM_STEP)
        go_ref[...] = gt
        d_ref[...] = -ADAM_LR * (m_hat / (jnp.sqrt(v_hat) + ADAM_EPS) + ADAM_WD * w_ref[...])
        mo_ref[...] = mt
        vo_ref[...] = vt

    state = pl.BlockSpec((None, tm, C), lambda i: (0, i, 0))
    return pl.pallas_call(
        body, name=name, grid=(R // tm,),
        in_specs=[pl.BlockSpec((tm, C), lambda i: (g_row0 // tm + i, 0)), state, state, state],
        out_specs=[state] * 4, out_shape=[jax.ShapeDtypeStruct((1, R, C), F32)] * 4,
        compiler_params=_cparams(("parallel",)),
    )(g, w, m, v)


def _unpack_weights(gathered, names):
    S = gathered.shape[0]
    shard_shapes = {"w_in": (D_MODEL, (QKV_WIDTH + 2 * D_MODEL) // S), "w_branch_na": (NA_WIDTH, D_MODEL // S),
                    "w_branch_dil": (DIL_OUT_WIDTH, D_MODEL // S), "w_out": (D_MODEL // S, D_MODEL),
                    "w_up": (D_MODEL, D_FF // S), "w_down": (D_FF // S, D_MODEL),
                    "w_ple_gate": (D_MODEL // S, D_MODEL), "w_ple_proj": (PLE_DIM, D_MODEL // S)}
    col_sharded = {"w_in", "w_branch_na", "w_branch_dil", "w_up", "w_ple_proj"}
    out, r0 = {}, 0
    for name in names:
        rows, cols = shard_shapes[name]
        n = rows * cols // PACK_W
        t = gathered[:, r0:r0 + n, :].reshape(S, rows, cols)
        r0 += n
        out[name] = t.transpose(1, 0, 2).reshape(rows, S * cols) if name in col_sharded else t.reshape(S * rows, cols)
    return out


def kernel(x, p, positions, g_mix, w_in, rpb, w_branch_na, w_branch_dil, w_out, g_mlp, w_up, w_down, g_ple, w_ple_gate, w_ple_proj, g_final, loss_target, m_g_mix, m_w_in, m_rpb, m_w_branch_na, m_w_branch_dil, m_w_out, m_g_mlp, m_w_up, m_w_down, m_g_ple, m_w_ple_gate, m_w_ple_proj, m_g_final, v_g_mix, v_w_in, v_rpb, v_w_branch_na, v_w_branch_dil, v_w_out, v_g_mlp, v_w_up, v_w_down, v_g_ple, v_w_ple_gate, v_w_ple_proj, v_g_final):
    shards = {"w_in": w_in[0], "w_branch_na": w_branch_na[0], "w_branch_dil": w_branch_dil[0], "w_out": w_out[0],
              "w_up": w_up[0], "w_down": w_down[0], "w_ple_gate": w_ple_gate[0], "w_ple_proj": w_ple_proj[0]}
    params = {"w_in": w_in, "w_branch_na": w_branch_na, "w_branch_dil": w_branch_dil, "w_out": w_out, "w_up": w_up,
              "w_down": w_down, "w_ple_gate": w_ple_gate, "w_ple_proj": w_ple_proj,
              "m_w_in": m_w_in, "m_w_branch_na": m_w_branch_na, "m_w_branch_dil": m_w_branch_dil, "m_w_out": m_w_out,
              "m_w_up": m_w_up, "m_w_down": m_w_down, "m_w_ple_gate": m_w_ple_gate, "m_w_ple_proj": m_w_ple_proj,
              "v_w_in": v_w_in, "v_w_branch_na": v_w_branch_na, "v_w_branch_dil": v_w_branch_dil, "v_w_out": v_w_out,
              "v_w_up": v_w_up, "v_w_down": v_w_down, "v_w_ple_gate": v_w_ple_gate, "v_w_ple_proj": v_w_ple_proj}

    xs, ps, tgt = x[0], p[0, 0], loss_target[0]
    T = xs.shape[0]
    TM = 512
    gm, gl, gp, gf = g_mix, g_mlp, g_ple, g_final.reshape(1, D_MODEL)

    across = _gather_halves_start(shards["w_in"].astype(BF), "in")
    a = _rowwise("norm_mix", lambda h, g: h * _rms(h) * g, T, TM, [_row(xs, TM), _full(gm)], [(D_MODEL, BF)],
                 after=(across.token,))
    cos2, sin_signed = _rope_tables(positions[0])
    tab = _na_bias_table(rpb[0])
    w_in_shard, w_in_relay = _gather_halves_relay(across, a, "in")
    w_in_all = _gather_halves_finish(w_in_shard, w_in_relay, w_in_relay.token, "in")
    w_in_full, w_gates = _assemble_w_in(w_in_all, 256)
    W = {"w_in": w_in_full}
    mix_flight = _gather_start(jnp.concatenate([_pack_rows(shards[n].astype(BF)) for n in GATHER_MIX], axis=0), "mix",
                               after=(w_in_all,))
    mlp_across = _gather_halves_start(jnp.concatenate([_pack_rows(shards[n].astype(BF)) for n in GATHER_MLP], axis=0), "mlp",
                                      after=(mix_flight.token,))

    n3 = 3 * NA_WIDTH
    qkv = _mm("in_na", a, W["w_in"], "nn", 1024, 768, 1024, [BF], after=(mlp_across.token,),
              b_view=(n3, (D_MODEL, 768), lambda j, k: (k, j)))
    z_dil = _mm("in_dil", a, W["w_in"], "nn", 1024, 768, 1024, [F32],
                b_view=(3 * DIL_WIDTH, (D_MODEL, 768), lambda j, k: (k, n3 // 768 + j)))
    z_gates = _mm("in_gates", a, w_gates, "nn", 1024,1024, 1024, [BF])

    dil_ops = _qkv_prep(z_dil, cos2, sin_signed, TM)
    y_na = _na_fwd(qkv, tab)
    band = [_band_fwd(*dil_ops[g], g) for g in range(len(DIL_GROUPS))]
    y_dil, w_grp, o_nat = _dil_merge_fwd([b[0] for b in band], [b[1] for b in band], T, TM)

    W.update(_unpack_weights(_gather_wait(mix_flight, y_dil, "mix"), GATHER_MIX))
    mlp_shard, mlp_relay = _gather_halves_relay(mlp_across, y_dil, "mlp")
    u_na = _mm("branch_na", y_na, W["w_branch_na"], "nn", 1024,1024, 512, [BF], after=(mlp_relay.token,))
    u_dil = _mm("branch_dil", y_dil, W["w_branch_dil"], "nn", 1024,1024, 256, [BF])
    mixed = _rowwise(
        "gate_mix", lambda gn, gd, un, ud: _sigmoid(gn.astype(F32)) * un.astype(F32) + _sigmoid(gd.astype(F32)) * ud.astype(F32), T, TM,
        [_row(z_gates, TM, 0, D_MODEL), _row(z_gates, TM, 1, D_MODEL), _row(u_na, TM), _row(u_dil, TM)], [(D_MODEL, BF)])
    def add_norm(d, h, g):
        h = h + d
        return h, h * _rms(h) * g

    h1, cn = _mm("out_proj", mixed, W["w_out"], "nn", 512, 1024, 1024, [F32, BF], epilogue=add_norm, extras=(xs,), consts=(gl,))
    mlp_all = _gather_halves_finish(mlp_shard, mlp_relay, cn, "mlp")
    W.update({n: t for n, t in _unpack_weights(mlp_all, GATHER_MLP).items() if n.startswith("w_ple")})
    chip_block = (None, D_MODEL, PACK_W)
    up, act = _mm("mlp_up", cn, mlp_all, "nn", 1024,1024, 1024, [BF, BF],
                  epilogue=lambda acc: (acc, jnp.square(jnp.maximum(acc, 0.0))), b_view=(D_FF, chip_block, lambda j, k: (j, 0, 0)))
    h2, en = _mm("mlp_down", act, mlp_all, "nn", 1024, 1024, 1024, [F32, BF], epilogue=add_norm, extras=(h1,), consts=(gp,),
                 b_view=(D_MODEL, chip_block, lambda j, k: (k, 1, 0)))
    pp = _mm("ple_proj", ps, W["w_ple_proj"], "nn", 1024,1024, 256, [F32])

    def head(gtt, h2t, ppt, tg, g):
        sg = _sigmoid(gtt)
        h3 = h2t + sg * ppt
        yo = h3 * _rms(h3) * g
        diff = yo - tg
        loss = 0.5 * jnp.sum(jnp.mean(jnp.square(diff), axis=-1, keepdims=True), axis=0, keepdims=True)
        dh3, dg = _rms_bwd(diff * (1.0 / D_MODEL), h3, g)
        return dh3, dh3 * ppt * sg * (1.0 - sg), dh3 * sg, jnp.broadcast_to(loss, (1, 128)), dg

    dh3, d_gt, d_pp, loss_part, dg_final = _mm(
        "ple_gate_loss_head", en, W["w_ple_gate"], "nn", 512, 1024, 1024, [F32, BF, BF], epilogue=head,
        extras=(h2, pp, tgt), consts=(gf,), sums=[128, D_MODEL])

    early_shapes = {n: shards[n].shape for n in REDUCE_EARLY}
    early_rows = sum(r * c for r, c in early_shapes.values()) // PACK_W
    shard_rows = D_MODEL // N_CHIPS
    early_buf = _mm("g_ple_gate", en, d_gt, "tn", 1024, 1024, 1024, [F32],
                    into=(jax.ShapeDtypeStruct((N_CHIPS, early_rows, PACK_W), F32), (N_CHIPS, shard_rows, PACK_W),
                          lambda i, j: (0, 2 * D_MODEL // shard_rows, 0)))
    g_ple_proj = _mm("g_ple_proj", ps, d_pp, "tn", 256, 1024, 1024,[F32])

    def add_norm_bwd(dn, dh_out, h, g):
        dh, dg = _rms_bwd(dn, h, g)
        dh = dh_out + dh
        return dh, dh, dg

    dh2, dh2_b, dg_ple = _mm("d_ple_gate", d_gt, W["w_ple_gate"], "nt", 512, 1024, 1024, [F32, BF],
                             epilogue=add_norm_bwd, extras=(dh3, h2), consts=(gp,), sums=[D_MODEL])
    d_up = _mm("d_mlp_down", dh2_b, mlp_all, "nt", 1024,1024, 1024, [BF], b_view=(D_FF, chip_block, lambda j, k: (j, 1, 0)),
               epilogue=lambda acc, u: (acc * (2.0 * jnp.maximum(u.astype(F32), 0.0)),), extras=(up,))
    early_buf = _mm("g_mlp_down", act, dh2_b, "tn", 1024, 1024, 1024,[F32],
                    into=(early_buf, (None, D_MODEL, PACK_W), lambda i, j: (i, 1, 0)))
    early_buf = _mm("g_mlp_up", cn, d_up, "tn", 1024, 1024, 1024,[F32],
                    into=(early_buf, (None, D_MODEL, PACK_W), lambda i, j: (j, 0, 0)))
    dh1, dh1_b, dg_mlp = _mm("d_mlp_up", d_up, mlp_all, "nt", 1024, 1024, 1024, [F32, BF], epilogue=add_norm_bwd,
                             b_view=(D_MODEL, chip_block, lambda j, k: (k, 0, 0)),
                             extras=(dh2, h1), consts=(gl,), sums=[D_MODEL])
    d_mixed = _mm("d_out_proj", dh1_b, W["w_out"], "nt", 1024,1024, 1024, [F32])
    early_buf = _mm("g_out_proj", mixed, dh1_b, "tn", 1024, 1024, 1024, [F32],
                    into=(early_buf, (N_CHIPS, shard_rows, PACK_W), lambda i, j: (0, 2 * D_MODEL // shard_rows + 1, 0)))

    def gate_bwd(dm, gn, gd, un, ud):
        gn, gd, un, ud = (t.astype(F32) for t in (gn, gd, un, ud))
        sn, sd = _sigmoid(gn), _sigmoid(gd)
        return jnp.concatenate([dm * un * sn * (1.0 - sn), dm * ud * sd * (1.0 - sd)], axis=1), dm * sn, dm * sd

    dz_gates, d_u_na, d_u_dil = _rowwise(
        "gate_mix_bwd", gate_bwd, T, TM,
        [_row(d_mixed, TM), _row(z_gates, TM, 0, D_MODEL), _row(z_gates, TM, 1, D_MODEL), _row(u_na, TM), _row(u_dil, TM)],
        [(2 * D_MODEL, BF), (D_MODEL, BF), (D_MODEL, BF)])
    g_branch_na = _mm("g_branch_na", y_na, d_u_na, "tn", 1024, 1024, 1024,[F32])
    g_branch_dil = _mm("g_branch_dil", y_dil, d_u_dil, "tn", 256, 1024, 1024,[F32])
    small_rows = [jnp.concatenate([_pack_rows(g[:, s * shard_rows:(s + 1) * shard_rows]) for g in (g_ple_proj, g_branch_na, g_branch_dil)],
                                  axis=0) for s in range(N_CHIPS)]
    early_buf = lax.dynamic_update_slice(early_buf, jnp.stack(small_rows), (0, 2 * D_MODEL + 2 * shard_rows, 0))
    early_tm = early_rows // 4
    swap_flight = _swap_start(early_buf, "early")
    d_y_na = _mm("d_branch_na", d_u_na, W["w_branch_na"], "nt", 1024,512, 1024, [BF], after=(swap_flight.token,))
    d_y_dil = _mm("d_branch_dil", d_u_dil, W["w_branch_dil"], "nt", 1024,256, 1024, [F32])

    dqa, dka, dva, dtab = _na_bwd(qkv, tab, d_y_na)
    early_g, early_got = _swap_wait(swap_flight, dqa, "early")
    early_pair, early_pair_b = _pair_sum(early_g, early_got, early_tm)
    scatter_flight = _scatter_start(early_pair_b, "early")

    do_res, dlse_res = _dil_merge_bwd(d_y_dil, o_nat, w_grp, TM, after=(scatter_flight.token,))
    d_dil = [_band_bwd(*dil_ops[g], do_res[g], dlse_res[g], g) for g in range(len(DIL_GROUPS))]

    dz_qkv = _qkv_unprep((dqa, dka, dva), d_dil, cos2, sin_signed, TM)
    g_in_parts = [_mm("g_in_qkv", a, dz_qkv, "tn", 1024, 1280, 1024,[F32]), _mm("g_in_gates", a, dz_gates, "tn", 1024, 1024, 1024,[F32])]
    early_mine = _chip_sum(early_pair, _scatter_wait(scatter_flight, g_in_parts[1], "early"), early_tm)
    join_flight = _join_start(early_mine)
    in_cols = shards["w_in"].shape[1]

    def owner_columns(s):
        lo, hi, split = s * in_cols, (s + 1) * in_cols, g_in_parts[0].shape[1]
        pieces = [g_in_parts[0][:, lo:min(hi, split)]] if lo < split else []
        pieces += [g_in_parts[1][:, max(lo, split) - split:hi - split]] if hi > split else []
        return pieces[0] if len(pieces) == 1 else jnp.concatenate(pieces, axis=1)

    late_tm = in_cols // 4
    late_swap = _swap_start(jnp.stack([owner_columns(s).T for s in range(N_CHIPS)]), "late")
    d_a = _mm("d_in_qkv", dz_qkv, W["w_in"], "nt", 1024,1024, 1280, [F32], after=(late_swap.token, join_flight.token),
              b_view=(D_MODEL, (D_MODEL, 1280), lambda j, k: (j, k)))
    late_g, late_got = _swap_wait(late_swap, d_a, "late")
    late_pair, late_pair_b = _pair_sum(late_g, late_got, late_tm)
    late_scatter = _scatter_start(late_pair_b, "late")
    d_rpb = _na_rpb_grad(dtab, after=(late_scatter.token,))[:, :2 * NA_WIN_ROWS - 1, :2 * NA_WIN_COLS - 1]
    def first_bwd(dn_gates, dn_qkv, dh_out, h, g):
        dh, dg = _rms_bwd(dn_gates + dn_qkv, h, g)
        return dh_out + dh, dg

    grad_x, dg_mix = _mm("d_in_gates", dz_gates, w_gates, "nt", 512, 1024, 1024, [F32], epilogue=first_bwd,
                         extras=(d_a, dh1, xs), consts=(gm,), sums=[D_MODEL], after=(late_scatter.token,))
    early_shard = _join_wait(join_flight, grad_x)

    n_rpb = rpb.size
    rpb_rows = 4
    small = jnp.concatenate([
        dg_mix, dg_mlp, dg_ple, dg_final,
        jnp.pad(d_rpb.reshape(-1), (0, rpb_rows * D_MODEL - n_rpb)).reshape(rpb_rows, D_MODEL),
        jnp.pad(loss_part, ((0, 0), (0, D_MODEL - loss_part.shape[1]))),
        jnp.zeros((SMALL_ROWS - 5 - rpb_rows, D_MODEL), F32)], axis=0)
    out = {"grad": {}, "delta": {}, "new_m": {}, "new_v": {}}

    def update(n, g, row0):
        res = _adamw("adamw_" + n, g, row0, params[n], params["m_" + n], params["v_" + n])
        for kind, t in zip(("grad", "delta", "new_m", "new_v"), res, strict=True):
            out[kind][n] = t

    row0 = 0
    for n in REDUCE_EARLY:
        rows, cols = early_shapes[n]
        n_rows = rows * cols // PACK_W
        if cols == PACK_W:
            update(n, early_shard, row0)
        else:
            update(n, early_shard[row0:row0 + n_rows].reshape(rows, cols), 0)
        row0 += n_rows
    late_others = _scatter_wait(late_scatter, out["new_v"][REDUCE_EARLY[-1]], "late")
    late_mine = _chip_sum(late_pair, late_others, late_tm)
    small = _allreduce_small(small, after=(late_mine,))
    res = _adamw("adamw_w_in", _join_halves(late_mine), 0, *[jnp.swapaxes(params[n], 1, 2) for n in ("w_in", "m_w_in", "v_w_in")])
    for kind, t in zip(("grad", "delta", "new_m", "new_v"), res, strict=True):
        out[kind]["w_in"] = jnp.swapaxes(t, 1, 2)
    loss = small[4 + rpb_rows, 0]

    def small_pack(a0, a1, a2, a3, r):
        return jnp.concatenate([a0.reshape(1, -1), a1.reshape(1, -1), a2.reshape(1, -1), a3.reshape(1, -1),
                                jnp.pad(r.reshape(-1), (0, rpb_rows * D_MODEL - n_rpb)).reshape(rpb_rows, D_MODEL)], axis=0)

    small_res = _adamw("adamw_small", small, 0, small_pack(g_mix, g_mlp, g_ple, g_final, rpb)[None],
                       small_pack(m_g_mix, m_g_mlp, m_g_ple, m_g_final, m_rpb)[None],
                       small_pack(v_g_mix, v_g_mlp, v_g_ple, v_g_final, v_rpb)[None])

    def small_unpack(t):
        return {"g_mix": t[0].reshape(g_mix.shape), "g_mlp": t[1].reshape(g_mlp.shape), "g_ple": t[2].reshape(g_ple.shape),
                "g_final": t[3].reshape(g_final.shape), "rpb": t[4:].reshape(-1)[:n_rpb].reshape(rpb.shape)}

    for kind, t in zip(("grad", "delta", "new_m", "new_v"), small_res, strict=True):
        out[kind].update(small_unpack(t[0]))

    order = ["g_mix", "w_in", "rpb", "w_branch_na", "w_branch_dil", "w_out", "g_mlp", "w_up", "w_down", "g_ple",
             "w_ple_gate", "w_ple_proj", "g_final"]
    return (loss, grad_x[None], *[out["grad"][n] for n in order], *[out["delta"][n] for n in order],
            *[out["new_m"][n] for n in order], *[out["new_v"][n] for n in order])
```

```python
import functools
from typing import NamedTuple

import jax
import jax.numpy as jnp
from jax import lax
from jax.experimental import pallas as pl
from jax.experimental.pallas import tpu as pltpu

BF = jnp.bfloat16
F32 = jnp.float32
MESH = pl.DeviceIdType.MESH
ANY = pl.BlockSpec(memory_space=pl.ANY)

V7X_VMEM_BYTES = 64 * 1024 * 1024
VMEM_LIMIT = V7X_VMEM_BYTES - 16 * 1024 * 1024

D_MODEL = 1024
HEAD_DIM = 64
GRID_W = 64
NA_HEADS = 8
NA_WIN_ROWS = 8
NA_WIN_COLS = 16
NA_WIDTH = NA_HEADS * HEAD_DIM
DIL_GROUPS = ((128, 1), (512, 4), (2048, 16))
DIL_HPG = 4
DIL_HEADS = DIL_HPG * len(DIL_GROUPS)
DIL_WIDTH = DIL_HEADS * HEAD_DIM
DIL_OUT_WIDTH = DIL_HPG * HEAD_DIM
DIL_RADIUS = 64
QKV_WIDTH = 3 * NA_WIDTH + 3 * DIL_WIDTH
D_FF = 4 * D_MODEL
PLE_DIM = 256
ROPE_THETA = 10000.0
RMS_EPS = 1e-6
NEG_INF = -1e30
Q_SCALE = HEAD_DIM ** -0.5

ADAM_LR = 0.001
ADAM_B1 = 0.9
ADAM_B2 = 0.999
ADAM_EPS = 1e-08
ADAM_WD = 0.01
ADAM_STEP = 10

N_CHIPS = 4
N_DEV = 8
PACK_W = 1024
BIG = ("w_in", "w_branch_na", "w_branch_dil", "w_out", "w_up", "w_down", "w_ple_gate", "w_ple_proj")
GATHER_MIX = ("w_branch_na", "w_branch_dil", "w_out")
GATHER_MLP = ("w_up", "w_down", "w_ple_gate", "w_ple_proj")
REDUCE_EARLY = ("w_up", "w_down", "w_ple_gate", "w_out", "w_ple_proj", "w_branch_na", "w_branch_dil")
SMALL_ROWS = 16


def _cparams(sem=None):
    return pltpu.CompilerParams(dimension_semantics=sem, vmem_limit_bytes=VMEM_LIMIT)


def _mm(name, a, b, mode, tm, tn, tk, out_dtypes, epilogue=None, extras=(), consts=(), sums=(), after=(), into=None,
        b_view=None):
    if mode == "nn":
        (M, K), N = a.shape, b.shape[1]
    elif mode == "nt":
        (M, K), N = a.shape, b.shape[0]
    else:
        (K, M), N = a.shape, b.shape[1]
    if b_view is not None:
        N = b_view[0]
    tm, tn, tk = min(tm, M), min(tn, N), min(tk, K)
    assert M % tm == 0 and N % tn == 0 and K % tk == 0, (name, M, N, K, tm, tn, tk)
    if mode == "nn":
        a_spec = pl.BlockSpec((tm, tk), lambda i, j, k: (i, k))
        b_spec = pl.BlockSpec((tk, tn), lambda i, j, k: (k, j))
        dims = (((1,), (0,)), ((), ()))
    elif mode == "nt":
        a_spec = pl.BlockSpec((tm, tk), lambda i, j, k: (i, k))
        b_spec = pl.BlockSpec((tn, tk), lambda i, j, k: (j, k))
        dims = (((1,), (1,)), ((), ()))
    else:
        a_spec = pl.BlockSpec((tk, tm), lambda i, j, k: (k, i))
        b_spec = pl.BlockSpec((tk, tn), lambda i, j, k: (k, j))
        dims = (((0,), (0,)), ((), ()))
    if b_view is not None:
        b_spec = pl.BlockSpec(b_view[1], lambda i, j, k: b_view[2](j, k))
    nk = K // tk
    n_extra, n_const, n_out, n_sum = len(extras), len(consts), len(out_dtypes), len(sums)
    tile = pl.BlockSpec((tm, tn), lambda i, j, k: (i, j))
    assert not sums or tn == N, "row sums need whole rows in a tile"

    n_after = len(after)

    def body(a_ref, b_ref, *rest):
        extra_refs, rest = rest[:n_extra + n_const], rest[n_extra + n_const + n_after:]
        out_refs, sum_refs, acc = rest[:n_out], rest[n_out:n_out + n_sum], rest[-1]
        i, k = pl.program_id(0), pl.program_id(2)
        def product():
            return lax.dot_general(a_ref[...].astype(BF), b_ref[...].astype(BF), dims, preferred_element_type=F32)

        if nk > 1:
            @pl.when(k == 0)
            def _():
                acc[...] = jnp.zeros_like(acc)

            acc[...] += product()

        @pl.when(k == nk - 1)
        def _():
            total = product() if nk == 1 else acc[...]
            outs = (total,) if epilogue is None else epilogue(total, *[e[...] for e in extra_refs])
            for o_ref, val in zip(out_refs, outs[:n_out], strict=True):
                o_ref[...] = val.astype(o_ref.dtype).reshape(o_ref.shape)
            for s_ref, val in zip(sum_refs, outs[n_out:], strict=True):
                @pl.when(i == 0)
                def _():
                    s_ref[...] = val

                @pl.when(i != 0)
                def _():
                    s_ref[...] += val

    out_specs = [tile] * n_out + [pl.BlockSpec((1, c), lambda i, j, k: (0, 0)) for c in sums]
    out_shape = [jax.ShapeDtypeStruct((M, N), dt) for dt in out_dtypes] + [jax.ShapeDtypeStruct((1, c), F32) for c in sums]
    operands, aliases = [a, b, *extras, *consts, *after], {}
    in_specs = ([a_spec, b_spec] + [tile] * n_extra
                + [pl.BlockSpec(c.shape, functools.partial(lambda nd, i, j, k: (0,) * nd, c.ndim)) for c in consts] + [ANY] * n_after)
    if into is not None:
        assert n_out == 1
        target, block, index = into
        out_specs = [pl.BlockSpec(block, lambda i, j, k: index(i, j))]
        out_shape = [jax.ShapeDtypeStruct(target.shape, target.dtype)]
        if not isinstance(target, jax.ShapeDtypeStruct):
            aliases = {len(operands): 0}
            operands.append(target)
            in_specs.append(ANY)
            n_after += 1

    outs = pl.pallas_call(
        body, name=name, grid=(M // tm, N // tn, nk),
        in_specs=in_specs, out_specs=out_specs, out_shape=out_shape,
        scratch_shapes=[pltpu.VMEM((tm, tn) if nk > 1 else (8, 128), F32)], input_output_aliases=aliases,
        compiler_params=_cparams(("arbitrary",) * 3 if sums else ("parallel", "parallel", "arbitrary")),
    )(*operands)
    return outs[0] if len(outs) == 1 else outs


def _row(arr, tm, col_block=None, width=None):
    width = arr.shape[1] if width is None else width
    cb = 0 if col_block is None else col_block
    return arr, pl.BlockSpec((tm, width), lambda i: (i, cb))


def _full(arr):
    nd = arr.ndim
    return arr, pl.BlockSpec(arr.shape, lambda i: (0,) * nd)


def _rowwise(name, body, T, tm, ins, outs, sums=(), after=()):
    n_in, n_out, n_sum, n_after = len(ins), len(outs), len(sums), len(after)

    def kern(*refs):
        in_refs, refs = refs[:n_in], refs[n_in + n_after:]
        out_refs, sum_refs = refs[:n_out], refs[n_out:]
        res = body(*[r[...] for r in in_refs])
        res = res if isinstance(res, tuple) else (res,)
        for o_ref, val in zip(out_refs, res[:n_out], strict=True):
            o_ref[...] = val.astype(o_ref.dtype)
        if n_sum:
            @pl.when(pl.program_id(0) == 0)
            def _():
                for s_ref in sum_refs:
                    s_ref[...] = jnp.zeros_like(s_ref)

            for s_ref, val in zip(sum_refs, res[n_out:], strict=True):
                s_ref[...] += val

    res = pl.pallas_call(
        kern, name=name, grid=(T // tm,),
        in_specs=[spec for _, spec in ins] + [ANY] * n_after,
        out_specs=[pl.BlockSpec((tm, c), lambda i: (i, 0)) for c, _ in outs]
        + [pl.BlockSpec((1, c), lambda i: (0, 0)) for c in sums],
        out_shape=[jax.ShapeDtypeStruct((T, c), dt) for c, dt in outs]
        + [jax.ShapeDtypeStruct((1, c), F32) for c in sums],
        compiler_params=_cparams(("arbitrary",)),
    )(*[a for a, _ in ins], *after)
    return res[0] if len(res) == 1 else res


def _sigmoid(x):
    return 1.0 / (1.0 + jnp.exp(-x))


def _rms(h):
    return lax.rsqrt(jnp.mean(h * h, axis=-1, keepdims=True) + RMS_EPS)


def _rms_bwd(dy, h, g):
    r = _rms(h)
    n = h * r
    dn = dy * g
    dh = r * (dn - n * jnp.mean(dn * n, axis=-1, keepdims=True))
    return dh, jnp.sum(dy * n, axis=0, keepdims=True)


def _rope(x, cos2, sin_signed):
    lane = lax.broadcasted_iota(jnp.int32, x.shape, 1)
    swapped = jnp.where((lane % HEAD_DIM) < HEAD_DIM // 2, pltpu.roll(x, 128 - HEAD_DIM // 2, 1), pltpu.roll(x, HEAD_DIM // 2, 1))
    return x * cos2 + swapped * sin_signed


NA_KEYS = NA_WIN_ROWS * GRID_W
NA_BASES = 8


def _na_row_geometry(r, rows):
    first = jnp.clip(r - NA_WIN_ROWS // 2, 0, rows - NA_WIN_ROWS)
    base = first - r + (NA_WIN_ROWS - 1)
    return pl.multiple_of(first * GRID_W, GRID_W), base


NA_ROWS_PER_STEP = 16
NA_BWD_ROWS_PER_STEP = 8


def _softmax_rows(s):
    p = jnp.exp(s - jnp.max(s, axis=-1, keepdims=True))
    return p / jnp.sum(p, axis=-1, keepdims=True)


def _na_probs(q, kw, bias):
    return _softmax_rows(lax.dot_general(q, kw, (((1,), (1,)), ((), ())), preferred_element_type=F32) + bias)


def _split_pair(t):
    first = lax.broadcasted_iota(jnp.int32, t.shape, 1) < HEAD_DIM
    zero = jnp.zeros_like(t)
    return jnp.where(first, t, zero), jnp.where(first, zero, t)


def _join_pair(a, b):
    return jnp.where(lax.broadcasted_iota(jnp.int32, a.shape, 1) < HEAD_DIM, a, b)


_NT = (((1,), (1,)), ((), ()))
_TN = (((0,), (0,)), ((), ()))


def _na_fwd(qkv, tab):
    T = qkv.shape[0]
    rows = T // GRID_W
    n_pairs = NA_WIDTH // 128

    def body(q_ref, k_ref, v_ref, tab_ref, y_ref):
        def step(it, carry):
            geo = [_na_row_geometry(it * NA_ROWS_PER_STEP + u, rows) for u in range(NA_ROWS_PER_STEP)]
            q0s = [pl.multiple_of((it * NA_ROWS_PER_STEP + u) * GRID_W, GRID_W) for u in range(NA_ROWS_PER_STEP)]
            ss = [lax.dot_general(jnp.concatenate(_split_pair(q_ref[pl.ds(q0, GRID_W), :] * Q_SCALE), axis=0),
                                  k_ref[pl.ds(k0, NA_KEYS), :], _NT, preferred_element_type=F32)
                  for q0, (k0, _) in zip(q0s, geo)]
            ps = [_softmax_rows(s + jnp.concatenate([tab_ref[0, base], tab_ref[1, base]], axis=0)) for s, (_, base) in zip(ss, geo)]
            ys = [jnp.dot(p.astype(BF), v_ref[pl.ds(k0, NA_KEYS), :], preferred_element_type=F32) for p, (k0, _) in zip(ps, geo)]
            for q0, y2 in zip(q0s, ys):
                y_ref[pl.ds(q0, GRID_W), :] = _join_pair(y2[:GRID_W], y2[GRID_W:]).astype(y_ref.dtype)
            return carry

        lax.fori_loop(0, rows // NA_ROWS_PER_STEP, step, 0)

    def cols(first):
        return pl.BlockSpec((T, 128), lambda j: (0, first + j))

    return pl.pallas_call(
        body, name="na_fwd", grid=(n_pairs,),
        in_specs=[cols(0), cols(n_pairs), cols(2 * n_pairs), pl.BlockSpec((2, NA_BASES, GRID_W, NA_KEYS), lambda j: (j, 0, 0, 0))],
        out_specs=cols(0), out_shape=jax.ShapeDtypeStruct((T, NA_WIDTH), BF),
        compiler_params=_cparams(("parallel",)),
    )(qkv, qkv, qkv, tab)


def _na_bwd(qkv, tab, do):
    T = qkv.shape[0]
    rows = T // GRID_W
    n_pairs = NA_WIDTH // 128

    def body(q_ref, k_ref, v_ref, tab_ref, do_ref, dq_ref, dk_ref, dv_ref, dtab_ref):
        dk_ref[...] = jnp.zeros_like(dk_ref)
        dv_ref[...] = jnp.zeros_like(dv_ref)
        dtab_ref[...] = jnp.zeros_like(dtab_ref)

        def step(it, carry):
            U = NA_BWD_ROWS_PER_STEP
            geo = [_na_row_geometry(it * U + u, rows) for u in range(U)]
            q0s = [pl.multiple_of((it * U + u) * GRID_W, GRID_W) for u in range(U)]
            q2s = [jnp.concatenate(_split_pair(q_ref[pl.ds(q0, GRID_W), :] * Q_SCALE), axis=0) for q0 in q0s]
            do2s = [jnp.concatenate(_split_pair(do_ref[pl.ds(q0, GRID_W), :]), axis=0) for q0 in q0s]
            ss = [lax.dot_general(q2, k_ref[pl.ds(k0, NA_KEYS), :], _NT, preferred_element_type=F32) for q2, (k0, _) in zip(q2s, geo)]
            dps = [lax.dot_general(do2, v_ref[pl.ds(k0, NA_KEYS), :], _NT, preferred_element_type=F32) for do2, (k0, _) in zip(do2s, geo)]
            ps = [_softmax_rows(s + jnp.concatenate([tab_ref[0, base], tab_ref[1, base]], axis=0)) for s, (_, base) in zip(ss, geo)]
            dss = [p * (dp - jnp.sum(dp * p, axis=-1, keepdims=True)) for p, dp in zip(ps, dps)]
            dvs = [lax.dot_general(p.astype(BF), do2, _TN, preferred_element_type=F32) for p, do2 in zip(ps, do2s)]
            dsbs = [ds.astype(BF) for ds in dss]
            dqs = [jnp.dot(dsb, k_ref[pl.ds(k0, NA_KEYS), :], preferred_element_type=F32) for dsb, (k0, _) in zip(dsbs, geo)]
            dks = [lax.dot_general(dsb, q2, _TN, preferred_element_type=F32) for dsb, q2 in zip(dsbs, q2s)]
            for u in range(U):
                k0, base = geo[u]
                dtab_ref[0, base] += dss[u][:GRID_W]
                dtab_ref[1, base] += dss[u][GRID_W:]
                dq_ref[pl.ds(q0s[u], GRID_W), :] = _join_pair(dqs[u][:GRID_W], dqs[u][GRID_W:])
                dk_ref[pl.ds(k0, NA_KEYS), :] += dks[u]
                dv_ref[pl.ds(k0, NA_KEYS), :] += dvs[u]
            return carry

        lax.fori_loop(0, rows // NA_BWD_ROWS_PER_STEP, step, 0)

    def cols(first):
        return pl.BlockSpec((T, 128), lambda j: (0, first + j))

    tabs = pl.BlockSpec((2, NA_BASES, GRID_W, NA_KEYS), lambda j: (j, 0, 0, 0))
    wide = jax.ShapeDtypeStruct((T, NA_WIDTH), F32)
    return pl.pallas_call(
        body, name="na_bwd", grid=(n_pairs,),
        in_specs=[cols(0), cols(n_pairs), cols(2 * n_pairs), tabs, cols(0)],
        out_specs=[cols(0), cols(0), cols(0), tabs],
        out_shape=[wide, wide, wide, jax.ShapeDtypeStruct((NA_HEADS, NA_BASES, GRID_W, NA_KEYS), F32)],
        compiler_params=_cparams(("parallel",)),
    )(qkv, qkv, qkv, tab, do)


def _na_bias_table(rpb):
    H, n_rows, n_cols = rpb.shape

    def body(r_ref, tab_ref):
        q = lax.broadcasted_iota(jnp.int32, (GRID_W, 128), 0)
        kc = lax.broadcasted_iota(jnp.int32, (GRID_W, 128), 1)
        first = jnp.clip(q - NA_WIN_COLS // 2, 0, GRID_W - NA_WIN_COLS)
        valid = (kc >= first) & (kc < first + NA_WIN_COLS)
        toeplitz = []
        for ro in range(n_rows):
            row = jnp.broadcast_to(r_ref[pl.ds(ro, 1), :], (GRID_W, 128))
            shifted = pltpu.roll(pltpu.roll(row, 128 - (NA_WIN_COLS - 1), 1), 0, 1, stride=1, stride_axis=0)
            toeplitz.append(jnp.where(valid, shifted, NEG_INF))
        for base in range(NA_BASES):
            for j in range(NA_WIN_ROWS // 2):
                even, odd = toeplitz[base + 2 * j], toeplitz[base + 2 * j + 1]
                tab_ref[base, :, pl.ds(j * 128, 128)] = jnp.where(kc < GRID_W, even, pltpu.roll(odd, GRID_W, 1))

    padded = jnp.pad(rpb, ((0, 0), (0, 16 - n_rows), (0, 128 - n_cols)))
    return pl.pallas_call(
        body, name="na_bias_table", grid=(H,),
        in_specs=[pl.BlockSpec((None, 16, 128), lambda h: (h, 0, 0))],
        out_specs=pl.BlockSpec((None, NA_BASES, GRID_W, NA_KEYS), lambda h: (h, 0, 0, 0)),
        out_shape=jax.ShapeDtypeStruct((H, NA_BASES, GRID_W, NA_KEYS), F32),
        compiler_params=_cparams(("parallel",)),
    )(padded)


def _na_rpb_grad(dtab, after=()):
    H = dtab.shape[0]
    n_rows = 2 * NA_WIN_ROWS - 1
    n_cols = 2 * NA_WIN_COLS - 1

    def body(d_ref, *rest):
        o_ref = rest[-1]
        lane = lax.broadcasted_iota(jnp.int32, (GRID_W, 128), 1)
        low = lane < GRID_W
        out_rows = []
        for ro in range(n_rows):
            acc = jnp.zeros((GRID_W, 128), F32)
            for base in range(NA_BASES):
                i = ro - base
                if not 0 <= i < NA_WIN_ROWS:
                    continue
                pair = d_ref[base, :, pl.ds((i // 2) * 128, 128)]
                if i % 2:
                    pair = pltpu.roll(pair, GRID_W, 1)
                acc = acc + jnp.where(low, pair, 0.0)
            skew = pltpu.roll(acc, 0, 1, stride=1, stride_axis=0)
            diag = jnp.sum(skew, axis=0, keepdims=True)
            out_rows.append(pltpu.roll(jnp.broadcast_to(diag, (8, 128)), 128 - (GRID_W - NA_WIN_COLS), 1)[:1])
        out_rows.append(jnp.zeros((1, 128), F32))
        res = jnp.concatenate(out_rows, axis=0)
        o_ref[...] = jnp.where(lax.broadcasted_iota(jnp.int32, res.shape, 1) < n_cols, res, 0.0)

    return pl.pallas_call(
        body, name="na_rpb_grad", grid=(H,),
        in_specs=[pl.BlockSpec((None, NA_BASES, GRID_W, NA_KEYS), lambda h: (h, 0, 0, 0))] + [ANY] * len(after),
        out_specs=pl.BlockSpec((None, n_rows + 1, 128), lambda h: (h, 0, 0)),
        out_shape=jax.ShapeDtypeStruct((H, n_rows + 1, 128), F32),
        compiler_params=_cparams(("parallel",)),
    )(jnp.flip(dtab, axis=2), *after)


BAND_Q = 128
BAND_KEYS = BAND_Q + 2 * DIL_RADIUS


def _band_geometry(n, L):
    q0 = pl.multiple_of(n * BAND_Q, BAND_Q)
    k0 = pl.multiple_of(jnp.clip(q0 - DIL_RADIUS, 0, L - BAND_KEYS), DIL_RADIUS)
    qi = q0 + lax.broadcasted_iota(jnp.int32, (BAND_Q, BAND_KEYS), 0)
    kj = k0 + lax.broadcasted_iota(jnp.int32, (BAND_Q, BAND_KEYS), 1)
    return q0, k0, jnp.abs(qi - kj) <= DIL_RADIUS


DIL_PAIRS = DIL_OUT_WIDTH // 128


def _residue_shape(dil, T, dtype):
    return jax.ShapeDtypeStruct((DIL_PAIRS, dil, T // dil, 128), dtype)


def _residue_tile(dil, tm):
    return pl.BlockSpec((DIL_PAIRS, dil, tm // dil, 128), lambda i: (0, 0, i, 0))


def _to_natural(ref, scratch, dil, tm):
    tiles = []
    for pair in range(DIL_PAIRS):
        if dil == 1:
            tiles.append(ref[pair, 0].astype(F32))
            continue
        for r in range(dil):
            scratch[pl.ds(r, tm // dil, stride=dil), :] = ref[pair, r].astype(F32)
        tiles.append(scratch[...])
    return tiles


def _from_natural(tile, scratch, ref, pair, dil, tm):
    if dil == 1:
        ref[pair, 0] = tile.astype(ref.dtype)
        return
    scratch[...] = tile
    for r in range(dil):
        ref[pair, r] = scratch[pl.ds(r, tm // dil, stride=dil), :].astype(ref.dtype)


def _band_specs(group, T):
    dil = DIL_GROUPS[group][1]
    L = T // dil
    assert L % BAND_Q == 0 and L >= BAND_KEYS, (T, dil)
    per_residue = min(BAND_BLOCKS_PER_STEP, L // BAND_Q)
    residues = min(dil, BAND_BLOCKS_PER_STEP // per_residue)
    spec = pl.BlockSpec((None, residues, L, 128), lambda s: (s % DIL_PAIRS, s // DIL_PAIRS, 0, 0))
    return L, residues, per_residue, (dil // residues * DIL_PAIRS,), spec


BAND_BLOCKS_PER_STEP = 8


def _band_softmax(s, valid):
    s = jnp.where(valid, s, NEG_INF)
    m = jnp.max(s, axis=-1, keepdims=True)
    p = jnp.exp(s - m)
    l = jnp.sum(p, axis=-1, keepdims=True)
    return p / l, m + jnp.log(l)


def _band_fwd(q, k, v, group):
    T = q.shape[1] * q.shape[2]
    L, residues, U, grid, spec = _band_specs(group, T)

    def body(q_ref, k_ref, v_ref, o_ref, lse_ref):
        def step(it, carry):
            geo = [(r, *_band_geometry(it * U + u, L)) for r in range(residues) for u in range(U)]
            ss = [lax.dot_general(jnp.concatenate(_split_pair(q_ref[r, pl.ds(q0, BAND_Q), :]), axis=0),
                                  k_ref[r, pl.ds(k0, BAND_KEYS), :], _NT, preferred_element_type=F32) for r, q0, k0, _ in geo]
            pls = [_band_softmax(s, jnp.concatenate([valid, valid], axis=0)) for s, (_, _, _, valid) in zip(ss, geo)]
            os = [jnp.dot(p.astype(BF), v_ref[r, pl.ds(k0, BAND_KEYS), :], preferred_element_type=F32)
                  for (p, _), (r, _, k0, _) in zip(pls, geo)]
            for (r, q0, _, _), o2, (_, lse) in zip(geo, os, pls):
                o_ref[r, pl.ds(q0, BAND_Q), :] = _join_pair(o2[:BAND_Q], o2[BAND_Q:])
                lse2 = jnp.broadcast_to(lse, (2 * BAND_Q, 128))
                lse_ref[r, pl.ds(q0, BAND_Q), :] = _join_pair(lse2[:BAND_Q], lse2[BAND_Q:])
            return carry

        lax.fori_loop(0, L // (BAND_Q * U), step, 0)

    res = _residue_shape(DIL_GROUPS[group][1], T, F32)
    return pl.pallas_call(
        body, name=f"band_fwd_g{group}", grid=grid,
        in_specs=[spec] * 3, out_specs=[spec] * 2, out_shape=[res, res],
        compiler_params=_cparams(("parallel",)),
    )(q, k, v)


def _band_bwd(q, k, v, do, dlse, group):
    T = q.shape[1] * q.shape[2]
    L, residues, U, grid, spec = _band_specs(group, T)

    def body(q_ref, k_ref, v_ref, do_ref, dlse_ref, dq_ref, dk_ref, dv_ref):
        dk_ref[...] = jnp.zeros_like(dk_ref)
        dv_ref[...] = jnp.zeros_like(dv_ref)

        def step(it, carry):
            geo = [(r, *_band_geometry(it * U + u, L)) for r in range(residues) for u in range(U)]
            q2s = [jnp.concatenate(_split_pair(q_ref[r, pl.ds(q0, BAND_Q), :]), axis=0) for r, q0, _, _ in geo]
            do2s = [jnp.concatenate(_split_pair(do_ref[r, pl.ds(q0, BAND_Q), :]), axis=0) for r, q0, _, _ in geo]
            ss = [lax.dot_general(q2, k_ref[r, pl.ds(k0, BAND_KEYS), :], _NT, preferred_element_type=F32)
                  for q2, (r, _, k0, _) in zip(q2s, geo)]
            dps = [lax.dot_general(do2, v_ref[r, pl.ds(k0, BAND_KEYS), :], _NT, preferred_element_type=F32)
                   for do2, (r, _, k0, _) in zip(do2s, geo)]
            ps = [_band_softmax(s, jnp.concatenate([valid, valid], axis=0))[0] for s, (_, _, _, valid) in zip(ss, geo)]
            dss = []
            for p, dp, (r, q0, _, _) in zip(ps, dps, geo):
                dl = dlse_ref[r, pl.ds(q0, BAND_Q), :]
                dl2 = jnp.concatenate([dl[:, :1], dl[:, HEAD_DIM:HEAD_DIM + 1]], axis=0)
                dss.append(p * (dp - jnp.sum(dp * p, axis=-1, keepdims=True) + dl2))
            dvs = [lax.dot_general(p.astype(BF), do2, _TN, preferred_element_type=F32) for p, do2 in zip(ps, do2s)]
            dsbs = [ds.astype(BF) for ds in dss]
            dqs = [jnp.dot(dsb, k_ref[r, pl.ds(k0, BAND_KEYS), :], preferred_element_type=F32) for dsb, (r, _, k0, _) in zip(dsbs, geo)]
            dks = [lax.dot_general(dsb, q2, _TN, preferred_element_type=F32) for dsb, q2 in zip(dsbs, q2s)]
            for u, (r, q0, k0, _) in enumerate(geo):
                dq_ref[r, pl.ds(q0, BAND_Q), :] = _join_pair(dqs[u][:BAND_Q], dqs[u][BAND_Q:])
                dk_ref[r, pl.ds(k0, BAND_KEYS), :] += dks[u]
                dv_ref[r, pl.ds(k0, BAND_KEYS), :] += dvs[u]
            return carry

        lax.fori_loop(0, L // (BAND_Q * U), step, 0)

    res = _residue_shape(DIL_GROUPS[group][1], T, F32)
    return pl.pallas_call(
        body, name=f"band_bwd_g{group}", grid=grid,
        in_specs=[spec] * 5, out_specs=[spec] * 3, out_shape=[res] * 3,
        compiler_params=_cparams(("parallel",)),
    )(q, k, v, do, dlse)


def _head_sums(t):
    head = lax.broadcasted_iota(jnp.int32, t.shape, 1) // HEAD_DIM
    out = jnp.zeros_like(t)
    for h in range(t.shape[1] // HEAD_DIM):
        mine = head == h
        out = jnp.where(mine, jnp.sum(jnp.where(mine, t, 0.0), axis=-1, keepdims=True), out)
    return out


def _dil_merge_fwd(os, lses, T, tm):
    G = len(DIL_GROUPS)
    W = DIL_OUT_WIDTH
    dils = [d for _, d in DIL_GROUPS]

    def body(*refs):
        o_refs, lse_refs = refs[:G], refs[G:2 * G]
        y_ref, w_refs, on_refs, scratch = refs[2 * G], refs[2 * G + 1:3 * G + 1], refs[3 * G + 1:4 * G + 1], refs[-1]
        o = [jnp.concatenate(_to_natural(r, scratch, d, tm), axis=1) for r, d in zip(o_refs, dils)]
        ls = [jnp.concatenate(_to_natural(r, scratch, d, tm), axis=1) for r, d in zip(lse_refs, dils)]
        m = functools.reduce(jnp.maximum, ls)
        es = [jnp.exp(l - m) for l in ls]
        tot = functools.reduce(jnp.add, es)
        ws = [e / tot for e in es]
        y_ref[...] = functools.reduce(jnp.add, [w * t for w, t in zip(ws, o)]).astype(y_ref.dtype)
        for g in range(G):
            w_refs[g][...] = ws[g]
            on_refs[g][...] = o[g]

    nat = pl.BlockSpec((tm, W), lambda i: (i, 0))
    res = pl.pallas_call(
        body, name="dil_merge_fwd", grid=(T // tm,),
        in_specs=[_residue_tile(d, tm) for d in dils] * 2,
        out_specs=[nat] * (2 * G + 1),
        out_shape=[jax.ShapeDtypeStruct((T, W), BF)] + [jax.ShapeDtypeStruct((T, W), F32)] * (2 * G),
        scratch_shapes=[pltpu.VMEM((tm, 128), F32)],
        compiler_params=_cparams(("parallel",)),
    )(*os, *lses)
    return res[0], res[1:G + 1], res[G + 1:]


def _dil_merge_bwd(dy, os, ws, tm, after=()):
    G = len(DIL_GROUPS)
    T, W = dy.shape
    dils = [d for _, d in DIL_GROUPS]
    n_after = len(after)

    def body(*refs):
        dyt = refs[0][...]
        o, w = [r[...] for r in refs[1:G + 1]], [r[...] for r in refs[G + 1:2 * G + 1]]
        refs = refs[2 * G + 1 + n_after:]
        do_refs, dlse_refs, scratch = refs[:G], refs[G:2 * G], refs[-1]
        dws = [_head_sums(dyt * t) for t in o]
        mean = functools.reduce(jnp.add, [a * b for a, b in zip(w, dws)])
        for g, d in enumerate(dils):
            do, dlse = w[g] * dyt, w[g] * (dws[g] - mean)
            for pair in range(DIL_PAIRS):
                cols = slice(pair * 128, (pair + 1) * 128)
                _from_natural(do[:, cols], scratch, do_refs[g], pair, d, tm)
                _from_natural(dlse[:, cols], scratch, dlse_refs[g], pair, d, tm)

    nat = pl.BlockSpec((tm, W), lambda i: (i, 0))
    res = pl.pallas_call(
        body, name="dil_merge_bwd", grid=(T // tm,),
        in_specs=[nat] * (2 * G + 1) + [ANY] * n_after,
        out_specs=[_residue_tile(d, tm) for d in dils] * 2,
        out_shape=[_residue_shape(d, T, BF) for d in dils] + [_residue_shape(d, T, F32) for d in dils],
        scratch_shapes=[pltpu.VMEM((tm, 128), F32)],
        compiler_params=_cparams(("parallel",)),
    )(dy, *os, *ws, *after)
    return res[:G], res[G:]


def _qkv_prep(z, cos2, sin_signed, tm):
    T = z.shape[0]
    G = len(DIL_GROUPS)
    dils = [d for _, d in DIL_GROUPS]
    n_dil_blocks = 3 * DIL_WIDTH // 128

    def body(*refs):
        blocks = refs[:n_dil_blocks]
        cos_ref, sin_ref = refs[n_dil_blocks], refs[1 + n_dil_blocks]
        outs = refs[2 + n_dil_blocks:]
        for part in range(3):
            for g, d in enumerate(dils):
                out = outs[g * 3 + part]
                for pair in range(DIL_PAIRS):
                    blk = blocks[part * (DIL_WIDTH // 128) + g * DIL_PAIRS + pair]
                    for r in range(d):
                        rows = pl.ds(r, tm // d, stride=d) if d > 1 else slice(None)
                        x = blk[rows, :]
                        if part < 2:
                            x = _rope(x, cos_ref[rows, :], sin_ref[rows, :])
                        if part == 0:
                            x = x * Q_SCALE
                        out[pair, r] = x.astype(out.dtype)

    lane_block = [pl.BlockSpec((tm, 128), functools.partial(lambda b, i: (i, b), b)) for b in range(n_dil_blocks)]
    tab = pl.BlockSpec((tm, 128), lambda i: (i, 0))
    res = pl.pallas_call(
        body, name="qkv_prep", grid=(T // tm,),
        in_specs=lane_block + [tab, tab],
        out_specs=[_residue_tile(d, tm) for d in dils for _ in range(3)],
        out_shape=[_residue_shape(d, T, BF) for d in dils for _ in range(3)],
        compiler_params=_cparams(("parallel",)),
    )(*[z] * n_dil_blocks, cos2, sin_signed)
    return [res[3 * g:3 + 3 * g] for g in range(G)]


def _qkv_unprep(d_na, d_dil, cos2, sin_signed, tm, after=()):
    T = d_na[0].shape[0]
    G = len(DIL_GROUPS)
    dils = [d for _, d in DIL_GROUPS]
    n_after = len(after)

    def body(*refs):
        dq, dk, dv = (r[...] for r in refs[:3])
        res_refs = refs[3:3 + 3 * G]
        cs, sn = refs[3 + 3 * G][...], refs[4 + 3 * G][...]
        out, scratch = refs[5 + 3 * G + n_after], refs[-1]
        cols = [dq * Q_SCALE, dk, dv]
        for part in range(3):
            for g, d in enumerate(dils):
                for x in _to_natural(res_refs[g * 3 + part], scratch, d, tm):
                    if part < 2:
                        x = _rope(x, cs, -sn)
                    cols.append(x * Q_SCALE if part == 0 else x)
        out[...] = jnp.concatenate(cols, axis=1).astype(out.dtype)

    wide = pl.BlockSpec((tm, NA_WIDTH), lambda i: (i, 0))
    tab = pl.BlockSpec((tm, 128), lambda i: (i, 0))
    return pl.pallas_call(
        body, name="qkv_unprep", grid=(T // tm,),
        in_specs=[wide] * 3 + [_residue_tile(d, tm) for d in dils for _ in range(3)] + [tab, tab] + [ANY] * n_after,
        out_specs=pl.BlockSpec((tm, QKV_WIDTH), lambda i: (i, 0)),
        out_shape=jax.ShapeDtypeStruct((T, QKV_WIDTH), BF),
        scratch_shapes=[pltpu.VMEM((tm, 128), F32)],
        compiler_params=_cparams(("parallel",)),
    )(*d_na, *[t for g in range(G) for t in d_dil[g]], cos2, sin_signed, *after)


def _rope_tables(positions):
    half = HEAD_DIM // 2
    inv_freq = ROPE_THETA ** (-jnp.arange(half, dtype=F32) / half)
    ang = positions.astype(F32)[:, None] * inv_freq
    cos, sin = jnp.cos(ang), jnp.sin(ang)
    return jnp.tile(jnp.concatenate([cos, cos], axis=1), (1, 2)), jnp.tile(jnp.concatenate([-sin, sin], axis=1), (1, 2))


def _pack_rows(t):
    return t.reshape(-1, PACK_W)


def _me():
    return lax.axis_index("x"), lax.axis_index("y"), lax.axis_index("c")


def _other_chips(x, y):
    return [(1 - x, y), (x, 1 - y), (1 - x, 1 - y)]


def _pair_sum(g, got, tm):
    S, R, W = g.shape
    half = R // 2
    nb = half // tm

    def body(pos_ref, g_ref, got_ref, own_ref, ob_ref):
        tot = g_ref[...] + got_ref[...]
        ob_ref[...] = tot.astype(ob_ref.dtype)

        @pl.when(pl.program_id(1) == pos_ref[1])
        def _():
            own_ref[...] = tot

    tile = pl.BlockSpec((None, tm, W), lambda i, s, pos_ref: (s, i, 0))
    c, chip = lax.axis_index("c"), 2 * lax.axis_index("x") + lax.axis_index("y")
    return pl.pallas_call(
        body, name="pair_sum",
        grid_spec=pltpu.PrefetchScalarGridSpec(
            num_scalar_prefetch=1, grid=(nb, S),
            in_specs=[pl.BlockSpec((None, tm, W), lambda i, s, pos_ref: (s, pos_ref[0] * nb + i, 0)), tile],
            out_specs=[pl.BlockSpec((tm, W), lambda i, s, pos_ref: (i, 0)), tile]),
        out_shape=[jax.ShapeDtypeStruct((half, W), F32), jax.ShapeDtypeStruct((S, half, W), BF)],
        compiler_params=_cparams(("parallel", "arbitrary")),
    )(jnp.stack([c, chip]).astype(jnp.int32), g, got)


def _chip_sum(own, others, tm):
    n, h, W = others.shape
    nb = h // tm

    def body(c_ref, own_ref, p_ref, o_ref):
        o_ref[...] = ((own_ref[...] + p_ref[0].astype(F32)) + p_ref[1].astype(F32)) + p_ref[2].astype(F32)

    return pl.pallas_call(
        body, name="chip_sum",
        grid_spec=pltpu.PrefetchScalarGridSpec(
            num_scalar_prefetch=1, grid=(nb,),
            in_specs=[pl.BlockSpec((tm, W), lambda i, c_ref: (i, 0)), pl.BlockSpec((n, tm, W), lambda i, c_ref: (0, i, 0))],
            out_specs=pl.BlockSpec((tm, W), lambda i, c_ref: (c_ref[0] * nb + i, 0))),
        out_shape=jax.ShapeDtypeStruct((2 * h, W), F32),
        compiler_params=_cparams(("parallel",)),
    )(lax.axis_index("c").reshape(1).astype(jnp.int32), own, others)


def _join_halves(shard):
    h = shard.shape[0] // 2

    def body(in_ref, out_ref, send_sem, recv_sem):
        x, y, c = _me()
        cp = pltpu.make_async_remote_copy(
            src_ref=in_ref.at[pl.ds(c * h, h), :], dst_ref=out_ref.at[pl.ds(c * h, h), :],
            send_sem=send_sem, recv_sem=recv_sem, device_id=(x, y, 1 - c), device_id_type=MESH)
        cp.start()
        pltpu.make_async_remote_copy(
            src_ref=in_ref.at[pl.ds(c * h, h), :], dst_ref=out_ref.at[pl.ds((1 - c) * h, h), :],
            send_sem=send_sem, recv_sem=recv_sem, device_id=(x, y, 1 - c), device_id_type=MESH).wait_recv()
        cp.wait_send()

    return pl.pallas_call(
        body, name="join_halves", in_specs=[ANY], out_specs=ANY,
        out_shape=jax.ShapeDtypeStruct(shard.shape, shard.dtype), input_output_aliases={0: 0},
        scratch_shapes=[pltpu.SemaphoreType.DMA, pltpu.SemaphoreType.DMA],
    )(shard)


def _allreduce_small(s, after=()):
    R, W = s.shape
    n_after = len(after)

    def body(s_ref, *rest):
        o_ref, buf, send_sems, recv_sems = rest[n_after:]
        x, y, c = _me()
        me = 4 * x + 2 * y + c
        buf[me] = s_ref[...]
        peers = [((x + fx) % 2, (y + fy) % 2, (c + fc) % 2) for fx in range(2) for fy in range(2) for fc in range(2)][1:]
        sends = [pltpu.make_async_remote_copy(
            src_ref=s_ref, dst_ref=buf.at[me], send_sem=send_sems.at[k], recv_sem=recv_sems.at[k],
            device_id=peer, device_id_type=MESH) for k, peer in enumerate(peers)]
        for cp in sends:
            cp.start()
        for k, peer in enumerate(peers):
            pltpu.make_async_remote_copy(
                src_ref=s_ref, dst_ref=buf.at[4 * peer[0] + 2 * peer[1] + peer[2]], send_sem=send_sems.at[k],
                recv_sem=recv_sems.at[k], device_id=peer, device_id_type=MESH).wait_recv()
        for cp in sends:
            cp.wait_send()
        total = buf[0]
        for d in range(1, N_DEV):
            total = total + buf[d]
        o_ref[...] = total

    return pl.pallas_call(
        body, name="allreduce_small",
        in_specs=[pl.BlockSpec(memory_space=pltpu.VMEM)] + [ANY] * n_after, out_specs=pl.BlockSpec(memory_space=pltpu.VMEM),
        out_shape=jax.ShapeDtypeStruct((R, W), F32),
        scratch_shapes=[pltpu.VMEM((N_DEV, R, W), F32), pltpu.SemaphoreType.DMA((N_DEV - 1,)), pltpu.SemaphoreType.DMA((N_DEV - 1,))],
    )(s, *after)


HBM_SPEC = pl.BlockSpec(memory_space=pltpu.HBM)
SEM_SPEC = pl.BlockSpec(memory_space=pltpu.SEMAPHORE)
DATAFLOW = pltpu.SideEffectType.DATAFLOW_SIDE_EFFECTING


class _InFlight(NamedTuple):
    sems: tuple
    src: jax.Array
    land: jax.Array
    token: jax.Array


def _split_start(name, src, land_shape, land_dtype, n, copies, after=()):
    n_after = len(after)

    def body(src_ref, land_ref, *rest):
        rest = rest[n_after:]
        sems, token = rest[:2 * n], rest[-1]
        for k, (s, d, peer) in enumerate(copies(src_ref, land_ref)):
            pltpu.make_async_remote_copy(src_ref=s, dst_ref=d, send_sem=sems[k], recv_sem=sems[n + k],
                                         device_id=peer, device_id_type=MESH).start()
        token[...] = jnp.zeros_like(token)

    outs = pl.pallas_call(
        body, name=name,
        out_shape=(*[pltpu.SemaphoreType.DMA(())] * (2 * n), pltpu.HBM(src.shape, src.dtype), pltpu.HBM(land_shape, land_dtype),
                   jax.ShapeDtypeStruct((8, 128), F32)),
        in_specs=(HBM_SPEC, HBM_SPEC, *[ANY] * n_after),
        out_specs=(*[SEM_SPEC] * (2 * n), HBM_SPEC, HBM_SPEC, pl.BlockSpec(memory_space=pltpu.VMEM)),
        input_output_aliases={0: 2 * n, 1: 2 * n + 1},
        compiler_params=pltpu.CompilerParams(has_side_effects=DATAFLOW),
    )(pltpu.with_memory_space_constraint(src, pltpu.HBM), pltpu.with_memory_space_constraint(lax.empty(land_shape, land_dtype), pltpu.HBM),
      *after)
    return _InFlight(tuple(outs[:2 * n]), outs[2 * n], outs[2 * n + 1], outs[2 * n + 2])


def _split_wait(name, flight, after, n, copies):
    def body(src_ref, land_ref, *rest):
        sems = rest[:2 * n]
        for k, (s, d, peer) in enumerate(copies(src_ref, land_ref)):
            cp = pltpu.make_async_remote_copy(src_ref=s, dst_ref=d, send_sem=sems[k], recv_sem=sems[n + k],
                                              device_id=peer, device_id_type=MESH)
            cp.wait_send()
            cp.wait_recv()

    return pl.pallas_call(
        body, name=name,
        out_shape=(pltpu.HBM(flight.src.shape, flight.src.dtype), pltpu.HBM(flight.land.shape, flight.land.dtype)),
        in_specs=(HBM_SPEC, HBM_SPEC, *[SEM_SPEC] * (2 * n), ANY),
        out_specs=(HBM_SPEC, HBM_SPEC), input_output_aliases={0: 0, 1: 1},
        compiler_params=pltpu.CompilerParams(has_side_effects=DATAFLOW),
    )(flight.src, flight.land, *flight.sems, after)


def _gather_copies(src_ref, land_ref):
    x, y, c = _me()
    return [(src_ref, land_ref.at[2 * x + y], (*chip, c)) for chip in _other_chips(x, y)]


def _gather_start(packed, tag, after=()):
    return _split_start(f"gather_start_{tag}", packed, (N_CHIPS, *packed.shape), packed.dtype, 3, _gather_copies, after)


def _gather_wait(flight, after, tag):
    src, others = _split_wait(f"gather_wait_{tag}", flight, after, 3, _gather_copies)
    return lax.dynamic_update_slice(others, src[None], (2 * lax.axis_index("x") + lax.axis_index("y"), 0, 0))


def _across_copies(src_ref, land_ref):
    x, y, c = _me()
    half = src_ref.shape[0] // 2
    rows = pl.ds(c * half, half)
    return [(src_ref.at[rows, :], land_ref.at[2 * x + y, rows, :], (*chip, c)) for chip in _other_chips(x, y)]


def _to_sibling_copies(all_ref, unused_ref):
    x, y, c = _me()
    half = all_ref.shape[1] // 2
    places = [all_ref.at[2 * chip[0] + chip[1], pl.ds(c * half, half), :] for chip in _other_chips(x, y)]
    return [(place, place, (x, y, 1 - c)) for place in places]


def _gather_halves_start(shard, tag, after=()):
    return _split_start(f"gather_{tag}_across_start", shard, (N_CHIPS, *shard.shape), shard.dtype, 3, _across_copies, after)


def _gather_halves_relay(flight, after, tag):
    shard, landed = _split_wait(f"gather_{tag}_across_wait", flight, after, 3, _across_copies)
    return shard, _split_start(f"gather_{tag}_sibling_start", landed, (8, 128), landed.dtype, 3, _to_sibling_copies)


def _gather_halves_finish(shard, relay, after, tag):
    others = _split_wait(f"gather_{tag}_sibling_wait", relay, after, 3, _to_sibling_copies)[0]
    return lax.dynamic_update_slice(others, shard[None], (2 * lax.axis_index("x") + lax.axis_index("y"), 0, 0))


def _assemble_w_in(shards, tm):
    S, R, C = shards.shape
    n_gates = 2 * D_MODEL

    def body(s_ref, w_ref, g_ref):
        full = jnp.concatenate([s_ref[s] for s in range(S)], axis=1)
        w_ref[...] = full
        g_ref[...] = full[:, S * C - n_gates:]

    return pl.pallas_call(
        body, name="assemble_w_in", grid=(R // tm,),
        in_specs=[pl.BlockSpec((S, tm, C), lambda i: (0, i, 0))],
        out_specs=[pl.BlockSpec((tm, S * C), lambda i: (i, 0)), pl.BlockSpec((tm, n_gates), lambda i: (i, 0))],
        out_shape=[jax.ShapeDtypeStruct((R, S * C), shards.dtype), jax.ShapeDtypeStruct((R, n_gates), shards.dtype)],
        compiler_params=_cparams(("parallel",)),
    )(shards)


def _swap_copies(src_ref, land_ref):
    x, y, c = _me()
    half = land_ref.shape[1]
    return [(src_ref.at[:, pl.ds((1 - c) * half, half), :], land_ref, (x, y, 1 - c))]


def _swap_start(g, tag):
    S, R, W = g.shape
    return _split_start(f"swap_halves_start_{tag}", g, (S, R // 2, W), g.dtype, 1, _swap_copies)


def _swap_wait(flight, after, tag):
    return _split_wait(f"swap_halves_wait_{tag}", flight, after, 1, _swap_copies)


def _scatter_copies(src_ref, land_ref):
    x, y, c = _me()
    return [(src_ref.at[2 * chip[0] + chip[1]], land_ref.at[j], (*chip, c)) for j, chip in enumerate(_other_chips(x, y))]


def _scatter_start(part, tag):
    S, h, W = part.shape
    return _split_start(f"scatter_chips_start_{tag}", part, (S - 1, h, W), part.dtype, 3, _scatter_copies)


def _scatter_wait(flight, after, tag):
    return _split_wait(f"scatter_chips_wait_{tag}", flight, after, 3, _scatter_copies)[1]


def _join_copies(shard_ref, unused_ref):
    x, y, c = _me()
    h = shard_ref.shape[0] // 2
    rows = shard_ref.at[pl.ds(c * h, h), :]
    return [(rows, rows, (x, y, 1 - c))]


def _join_start(shard):
    return _split_start("join_halves_start", shard, (8, 128), shard.dtype, 1, _join_copies)


def _join_wait(flight, after):
    return _split_wait("join_halves_wait", flight, after, 1, _join_copies)[0]


def _adamw(name, g, g_row0, w, m, v):
    _, R, C = w.shape
    tm = next(cand for cand in (368, 256, 128, 64, 32, 16, 8) if R % cand == 0)
    assert g_row0 % tm == 0 and g.shape[1] == C

    def body(g_ref, w_ref, m_ref, v_ref, go_ref, d_ref, mo_ref, vo_ref):
        gt = g_ref[...]
        mt = ADAM_B1 * m_ref[...] + (1.0 - ADAM_B1) * gt
        vt = ADAM_B2 * v_ref[...] + (1.0 - ADAM_B2) * jnp.square(gt)
        m_hat = mt / (1.0 - ADAM_B1 ** ADAM_STEP)
        v_hat = vt / (1.0 - ADAM_B2 ** ADAM_STEP)
        go_ref[...] = gt
        d_ref[...] = -ADAM_LR * (m_hat / (jnp.sqrt(v_hat) + ADAM_EPS) + ADAM_WD * w_ref[...])
        mo_ref[...] = mt
        vo_ref[...] = vt

    state = pl.BlockSpec((None, tm, C), lambda i: (0, i, 0))
    return pl.pallas_call(
        body, name=name, grid=(R // tm,),
        in_specs=[pl.BlockSpec((tm, C), lambda i: (g_row0 // tm + i, 0)), state, state, state],
        out_specs=[state] * 4, out_shape=[jax.ShapeDtypeStruct((1, R, C), F32)] * 4,
        compiler_params=_cparams(("parallel",)),
    )(g, w, m, v)


def _unpack_weights(gathered, names):
    S = gathered.shape[0]
    shard_shapes = {"w_in": (D_MODEL, (QKV_WIDTH + 2 * D_MODEL) // S), "w_branch_na": (NA_WIDTH, D_MODEL // S),
                    "w_branch_dil": (DIL_OUT_WIDTH, D_MODEL // S), "w_out": (D_MODEL // S, D_MODEL),
                    "w_up": (D_MODEL, D_FF // S), "w_down": (D_FF // S, D_MODEL),
                    "w_ple_gate": (D_MODEL // S, D_MODEL), "w_ple_proj": (PLE_DIM, D_MODEL // S)}
    col_sharded = {"w_in", "w_branch_na", "w_branch_dil", "w_up", "w_ple_proj"}
    out, r0 = {}, 0
    for name in names:
        rows, cols = shard_shapes[name]
        n = rows * cols // PACK_W
        t = gathered[:, r0:r0 + n, :].reshape(S, rows, cols)
        r0 += n
        out[name] = t.transpose(1, 0, 2).reshape(rows, S * cols) if name in col_sharded else t.reshape(S * rows, cols)
    return out


def kernel(x, p, positions, g_mix, w_in, rpb, w_branch_na, w_branch_dil, w_out, g_mlp, w_up, w_down, g_ple, w_ple_gate, w_ple_proj, g_final, loss_target, m_g_mix, m_w_in, m_rpb, m_w_branch_na, m_w_branch_dil, m_w_out, m_g_mlp, m_w_up, m_w_down, m_g_ple, m_w_ple_gate, m_w_ple_proj, m_g_final, v_g_mix, v_w_in, v_rpb, v_w_branch_na, v_w_branch_dil, v_w_out, v_g_mlp, v_w_up, v_w_down, v_g_ple, v_w_ple_gate, v_w_ple_proj, v_g_final):
    shards = {"w_in": w_in[0], "w_branch_na": w_branch_na[0], "w_branch_dil": w_branch_dil[0], "w_out": w_out[0],
              "w_up": w_up[0], "w_down": w_down[0], "w_ple_gate": w_ple_gate[0], "w_ple_proj": w_ple_proj[0]}
    params = {"w_in": w_in, "w_branch_na": w_branch_na, "w_branch_dil": w_branch_dil, "w_out": w_out, "w_up": w_up,
              "w_down": w_down, "w_ple_gate": w_ple_gate, "w_ple_proj": w_ple_proj,
              "m_w_in": m_w_in, "m_w_branch_na": m_w_branch_na, "m_w_branch_dil": m_w_branch_dil, "m_w_out": m_w_out,
              "m_w_up": m_w_up, "m_w_down": m_w_down, "m_w_ple_gate": m_w_ple_gate, "m_w_ple_proj": m_w_ple_proj,
              "v_w_in": v_w_in, "v_w_branch_na": v_w_branch_na, "v_w_branch_dil": v_w_branch_dil, "v_w_out": v_w_out,
              "v_w_up": v_w_up, "v_w_down": v_w_down, "v_w_ple_gate": v_w_ple_gate, "v_w_ple_proj": v_w_ple_proj}

    xs, ps, tgt = x[0], p[0, 0], loss_target[0]
    T = xs.shape[0]
    TM = 512
    gm, gl, gp, gf = g_mix, g_mlp, g_ple, g_final.reshape(1, D_MODEL)

    across = _gather_halves_start(shards["w_in"].astype(BF), "in")
    a = _rowwise("norm_mix", lambda h, g: h * _rms(h) * g, T, TM, [_row(xs, TM), _full(gm)], [(D_MODEL, BF)],
                 after=(across.token,))
    cos2, sin_signed = _rope_tables(positions[0])
    tab = _na_bias_table(rpb[0])
    w_in_shard, w_in_relay = _gather_halves_relay(across, a, "in")
    w_in_all = _gather_halves_finish(w_in_shard, w_in_relay, w_in_relay.token, "in")
    w_in_full, w_gates = _assemble_w_in(w_in_all, 256)
    W = {"w_in": w_in_full}
    mix_flight = _gather_start(jnp.concatenate([_pack_rows(shards[n].astype(BF)) for n in GATHER_MIX], axis=0), "mix",
                               after=(w_in_all,))
    mlp_across = _gather_halves_start(jnp.concatenate([_pack_rows(shards[n].astype(BF)) for n in GATHER_MLP], axis=0), "mlp",
                                      after=(mix_flight.token,))

    n3 = 3 * NA_WIDTH
    qkv = _mm("in_na", a, W["w_in"], "nn", 1024, 768, 1024, [BF], after=(mlp_across.token,),
              b_view=(n3, (D_MODEL, 768), lambda j, k: (k, j)))
    z_dil = _mm("in_dil", a, W["w_in"], "nn", 1024, 768, 1024, [F32], after=(mlp_across.token,),
                b_view=(3 * DIL_WIDTH, (D_MODEL, 768), lambda j, k: (k, n3 // 768 + j)))
    z_gates = _mm("in_gates", a, w_gates, "nn", 1024,1024, 1024, [BF], after=(mlp_across.token,))

    dil_ops = _qkv_prep(z_dil, cos2, sin_signed, TM)
    y_na = _na_fwd(qkv, tab)
    band = [_band_fwd(*dil_ops[g], g) for g in range(len(DIL_GROUPS))]
    y_dil, w_grp, o_nat = _dil_merge_fwd([b[0] for b in band], [b[1] for b in band], T, TM)

    W.update(_unpack_weights(_gather_wait(mix_flight, y_dil, "mix"), GATHER_MIX))
    mlp_shard, mlp_relay = _gather_halves_relay(mlp_across, y_dil, "mlp")
    u_na = _mm("branch_na", y_na, W["w_branch_na"], "nn", 1024,1024, 512, [BF], after=(mlp_relay.token,))
    u_dil = _mm("branch_dil", y_dil, W["w_branch_dil"], "nn", 1024,1024, 256, [BF])
    mixed = _rowwise(
        "gate_mix", lambda gn, gd, un, ud: _sigmoid(gn.astype(F32)) * un.astype(F32) + _sigmoid(gd.astype(F32)) * ud.astype(F32), T, TM,
        [_row(z_gates, TM, 0, D_MODEL), _row(z_gates, TM, 1, D_MODEL), _row(u_na, TM), _row(u_dil, TM)], [(D_MODEL, BF)])
    def add_norm(d, h, g):
        h = h + d
        return h, h * _rms(h) * g

    h1, cn = _mm("out_proj", mixed, W["w_out"], "nn", 512, 1024, 1024, [F32, BF], epilogue=add_norm, extras=(xs,), consts=(gl,))
    mlp_all = _gather_halves_finish(mlp_shard, mlp_relay, cn, "mlp")
    W.update({n: t for n, t in _unpack_weights(mlp_all, GATHER_MLP).items() if n.startswith("w_ple")})
    chip_block = (None, D_MODEL, PACK_W)
    up, act = _mm("mlp_up", cn, mlp_all, "nn", 1024,1024, 1024, [BF, BF],
                  epilogue=lambda acc: (acc, jnp.square(jnp.maximum(acc, 0.0))), b_view=(D_FF, chip_block, lambda j, k: (j, 0, 0)))
    h2, en = _mm("mlp_down", act, mlp_all, "nn", 1024, 1024, 1024, [F32, BF], epilogue=add_norm, extras=(h1,), consts=(gp,),
                 b_view=(D_MODEL, chip_block, lambda j, k: (k, 1, 0)))
    pp = _mm("ple_proj", ps, W["w_ple_proj"], "nn", 1024,1024, 256, [F32])

    def head(gtt, h2t, ppt, tg, g):
        sg = _sigmoid(gtt)
        h3 = h2t + sg * ppt
        yo = h3 * _rms(h3) * g
        diff = yo - tg
        loss = 0.5 * jnp.sum(jnp.mean(jnp.square(diff), axis=-1, keepdims=True), axis=0, keepdims=True)
        dh3, dg = _rms_bwd(diff * (1.0 / D_MODEL), h3, g)
        return dh3, dh3 * ppt * sg * (1.0 - sg), dh3 * sg, jnp.broadcast_to(loss, (1, 128)), dg

    dh3, d_gt, d_pp, loss_part, dg_final = _mm(
        "ple_gate_loss_head", en, W["w_ple_gate"], "nn", 512, 1024, 1024, [F32, BF, BF], epilogue=head,
        extras=(h2, pp, tgt), consts=(gf,), sums=[128, D_MODEL])

    early_shapes = {n: shards[n].shape for n in REDUCE_EARLY}
    early_rows = sum(r * c for r, c in early_shapes.values()) // PACK_W
    shard_rows = D_MODEL // N_CHIPS
    early_buf = _mm("g_ple_gate", en, d_gt, "tn", 1024, 1024, 1024, [F32],
                    into=(jax.ShapeDtypeStruct((N_CHIPS, early_rows, PACK_W), F32), (N_CHIPS, shard_rows, PACK_W),
                          lambda i, j: (0, 2 * D_MODEL // shard_rows, 0)))
    g_ple_proj = _mm("g_ple_proj", ps, d_pp, "tn", 256, 1024, 1024,[F32])

    def add_norm_bwd(dn, dh_out, h, g):
        dh, dg = _rms_bwd(dn, h, g)
        dh = dh_out + dh
        return dh, dh, dg

    dh2, dh2_b, dg_ple = _mm("d_ple_gate", d_gt, W["w_ple_gate"], "nt", 512, 1024, 1024, [F32, BF],
                             epilogue=add_norm_bwd, extras=(dh3, h2), consts=(gp,), sums=[D_MODEL])
    d_up = _mm("d_mlp_down", dh2_b, mlp_all, "nt", 1024,1024, 1024, [BF], b_view=(D_FF, chip_block, lambda j, k: (j, 1, 0)),
               epilogue=lambda acc, u: (acc * (2.0 * jnp.maximum(u.astype(F32), 0.0)),), extras=(up,))
    early_buf = _mm("g_mlp_down", act, dh2_b, "tn", 1024, 1024, 1024,[F32],
                    into=(early_buf, (None, D_MODEL, PACK_W), lambda i, j: (i, 1, 0)))
    early_buf = _mm("g_mlp_up", cn, d_up, "tn", 1024, 1024, 1024,[F32],
                    into=(early_buf, (None, D_MODEL, PACK_W), lambda i, j: (j, 0, 0)))
    dh1, dh1_b, dg_mlp = _mm("d_mlp_up", d_up, mlp_all, "nt", 1024, 1024, 1024, [F32, BF], epilogue=add_norm_bwd,
                             b_view=(D_MODEL, chip_block, lambda j, k: (k, 0, 0)),
                             extras=(dh2, h1), consts=(gl,), sums=[D_MODEL])
    d_mixed = _mm("d_out_proj", dh1_b, W["w_out"], "nt", 1024,1024, 1024, [F32])
    early_buf = _mm("g_out_proj", mixed, dh1_b, "tn", 1024, 1024, 1024, [F32],
                    into=(early_buf, (N_CHIPS, shard_rows, PACK_W), lambda i, j: (0, 2 * D_MODEL // shard_rows + 1, 0)))

    def gate_bwd(dm, gn, gd, un, ud):
        gn, gd, un, ud = (t.astype(F32) for t in (gn, gd, un, ud))
        sn, sd = _sigmoid(gn), _sigmoid(gd)
        return jnp.concatenate([dm * un * sn * (1.0 - sn), dm * ud * sd * (1.0 - sd)], axis=1), dm * sn, dm * sd

    dz_gates, d_u_na, d_u_dil = _rowwise(
        "gate_mix_bwd", gate_bwd, T, TM,
        [_row(d_mixed, TM), _row(z_gates, TM, 0, D_MODEL), _row(z_gates, TM, 1, D_MODEL), _row(u_na, TM), _row(u_dil, TM)],
        [(2 * D_MODEL, BF), (D_MODEL, BF), (D_MODEL, BF)])
    g_branch_na = _mm("g_branch_na", y_na, d_u_na, "tn", 1024, 1024, 1024,[F32])
    g_branch_dil = _mm("g_branch_dil", y_dil, d_u_dil, "tn", 256, 1024, 1024,[F32])
    small_rows = [jnp.concatenate([_pack_rows(g[:, s * shard_rows:(s + 1) * shard_rows]) for g in (g_ple_proj, g_branch_na, g_branch_dil)],
                                  axis=0) for s in range(N_CHIPS)]
    early_buf = lax.dynamic_update_slice(early_buf, jnp.stack(small_rows), (0, 2 * D_MODEL + 2 * shard_rows, 0))
    early_tm = early_rows // 4
    swap_flight = _swap_start(early_buf, "early")
    d_y_na = _mm("d_branch_na", d_u_na, W["w_branch_na"], "nt", 1024,512, 1024, [BF], after=(swap_flight.token,))
    d_y_dil = _mm("d_branch_dil", d_u_dil, W["w_branch_dil"], "nt", 1024,256, 1024, [F32])

    dqa, dka, dva, dtab = _na_bwd(qkv, tab, d_y_na)
    early_g, early_got = _swap_wait(swap_flight, dqa, "early")
    early_pair, early_pair_b = _pair_sum(early_g, early_got, early_tm)
    scatter_flight = _scatter_start(early_pair_b, "early")

    do_res, dlse_res = _dil_merge_bwd(d_y_dil, o_nat, w_grp, TM, after=(scatter_flight.token,))
    d_dil = [_band_bwd(*dil_ops[g], do_res[g], dlse_res[g], g) for g in range(len(DIL_GROUPS))]

    dz_qkv = _qkv_unprep((dqa, dka, dva), d_dil, cos2, sin_signed, TM)
    g_in_parts = [_mm("g_in_qkv", a, dz_qkv, "tn", 1024, 1280, 1024,[F32]), _mm("g_in_gates", a, dz_gates, "tn", 1024, 1024, 1024,[F32])]
    early_mine = _chip_sum(early_pair, _scatter_wait(scatter_flight, g_in_parts[1], "early"), early_tm)
    join_flight = _join_start(early_mine)
    in_cols = shards["w_in"].shape[1]

    def owner_columns(s):
        lo, hi, split = s * in_cols, (s + 1) * in_cols, g_in_parts[0].shape[1]
        pieces = [g_in_parts[0][:, lo:min(hi, split)]] if lo < split else []
        pieces += [g_in_parts[1][:, max(lo, split) - split:hi - split]] if hi > split else []
        return pieces[0] if len(pieces) == 1 else jnp.concatenate(pieces, axis=1)

    late_tm = in_cols // 4
    late_swap = _swap_start(jnp.stack([owner_columns(s).T for s in range(N_CHIPS)]), "late")
    d_a = _mm("d_in_qkv", dz_qkv, W["w_in"], "nt", 1024,1024, 1280, [F32], after=(late_swap.token, join_flight.token),
              b_view=(D_MODEL, (D_MODEL, 1280), lambda j, k: (j, k)))
    late_g, late_got = _swap_wait(late_swap, d_a, "late")
    late_pair, late_pair_b = _pair_sum(late_g, late_got, late_tm)
    late_scatter = _scatter_start(late_pair_b, "late")
    d_rpb = _na_rpb_grad(dtab, after=(late_scatter.token,))[:, :2 * NA_WIN_ROWS - 1, :2 * NA_WIN_COLS - 1]
    def first_bwd(dn_gates, dn_qkv, dh_out, h, g):
        dh, dg = _rms_bwd(dn_gates + dn_qkv, h, g)
        return dh_out + dh, dg

    grad_x, dg_mix = _mm("d_in_gates", dz_gates, w_gates, "nt", 512, 1024, 1024, [F32], epilogue=first_bwd,
                         extras=(d_a, dh1, xs), consts=(gm,), sums=[D_MODEL], after=(late_scatter.token,))
    early_shard = _join_wait(join_flight, grad_x)

    n_rpb = rpb.size
    rpb_rows = 4
    small = jnp.concatenate([
        dg_mix, dg_mlp, dg_ple, dg_final,
        jnp.pad(d_rpb.reshape(-1), (0, rpb_rows * D_MODEL - n_rpb)).reshape(rpb_rows, D_MODEL),
        jnp.pad(loss_part, ((0, 0), (0, D_MODEL - loss_part.shape[1]))),
        jnp.zeros((SMALL_ROWS - 5 - rpb_rows, D_MODEL), F32)], axis=0)
    out = {"grad": {}, "delta": {}, "new_m": {}, "new_v": {}}

    def update(n, g, row0):
        res = _adamw("adamw_" + n, g, row0, params[n], params["m_" + n], params["v_" + n])
        for kind, t in zip(("grad", "delta", "new_m", "new_v"), res, strict=True):
            out[kind][n] = t

    row0 = 0
    for n in REDUCE_EARLY:
        rows, cols = early_shapes[n]
        n_rows = rows * cols // PACK_W
        if cols == PACK_W:
            update(n, early_shard, row0)
        else:
            update(n, early_shard[row0:row0 + n_rows].reshape(rows, cols), 0)
        row0 += n_rows
    late_others = _scatter_wait(late_scatter, out["new_v"][REDUCE_EARLY[-1]], "late")
    late_mine = _chip_sum(late_pair, late_others, late_tm)
    small = _allreduce_small(small, after=(late_mine,))
    res = _adamw("adamw_w_in", _join_halves(late_mine), 0, *[jnp.swapaxes(params[n], 1, 2) for n in ("w_in", "m_w_in", "v_w_in")])
    for kind, t in zip(("grad", "delta", "new_m", "new_v"), res, strict=True):
        out[kind]["w_in"] = jnp.swapaxes(t, 1, 2)
    loss = small[4 + rpb_rows, 0]

    def small_pack(a0, a1, a2, a3, r):
        return jnp.concatenate([a0.reshape(1, -1), a1.reshape(1, -1), a2.reshape(1, -1), a3.reshape(1, -1),
                                jnp.pad(r.reshape(-1), (0, rpb_rows * D_MODEL - n_rpb)).reshape(rpb_rows, D_MODEL)], axis=0)

    small_res = _adamw("adamw_small", small, 0, small_pack(g_mix, g_mlp, g_ple, g_final, rpb)[None],
                       small_pack(m_g_mix, m_g_mlp, m_g_ple, m_g_final, m_rpb)[None],
                       small_pack(v_g_mix, v_g_mlp, v_g_ple, v_g_final, v_rpb)[None])

    def small_unpack(t):
        return {"g_mix": t[0].reshape(g_mix.shape), "g_mlp": t[1].reshape(g_mlp.shape), "g_ple": t[2].reshape(g_ple.shape),
                "g_final": t[3].reshape(g_final.shape), "rpb": t[4:].reshape(-1)[:n_rpb].reshape(rpb.shape)}

    for kind, t in zip(("grad", "delta", "new_m", "new_v"), small_res, strict=True):
        out[kind].update(small_unpack(t[0]))

    order = ["g_mix", "w_in", "rpb", "w_branch_na", "w_branch_dil", "w_out", "g_mlp", "w_up", "w_down", "g_ple",
             "w_ple_gate", "w_ple_proj", "g_final"]
    return (loss, grad_x[None], *[out["grad"][n] for n in order], *[out["delta"][n] for n in order],
            *[out["new_m"][n] for n in order], *[out["new_v"][n] for n in order])
```

```python
import functools
from typing import NamedTuple

import jax
import jax.numpy as jnp
from jax import lax
from jax.experimental import pallas as pl
from jax.experimental.pallas import tpu as pltpu

BF = jnp.bfloat16
F32 = jnp.float32
MESH = pl.DeviceIdType.MESH
ANY = pl.BlockSpec(memory_space=pl.ANY)

V7X_VMEM_BYTES = 64 * 1024 * 1024
VMEM_LIMIT = V7X_VMEM_BYTES - 16 * 1024 * 1024

D_MODEL = 1024
HEAD_DIM = 64
GRID_W = 64
NA_HEADS = 8
NA_WIN_ROWS = 8
NA_WIN_COLS = 16
NA_WIDTH = NA_HEADS * HEAD_DIM
DIL_GROUPS = ((128, 1), (512, 4), (2048, 16))
DIL_HPG = 4
DIL_HEADS = DIL_HPG * len(DIL_GROUPS)
DIL_WIDTH = DIL_HEADS * HEAD_DIM
DIL_OUT_WIDTH = DIL_HPG * HEAD_DIM
DIL_RADIUS = 64
QKV_WIDTH = 3 * NA_WIDTH + 3 * DIL_WIDTH
D_FF = 4 * D_MODEL
PLE_DIM = 256
ROPE_THETA = 10000.0
RMS_EPS = 1e-6
NEG_INF = -1e30
Q_SCALE = HEAD_DIM ** -0.5

ADAM_LR = 0.001
ADAM_B1 = 0.9
ADAM_B2 = 0.999
ADAM_EPS = 1e-08
ADAM_WD = 0.01
ADAM_STEP = 10

N_CHIPS = 4
N_DEV = 8
PACK_W = 1024
BIG = ("w_in", "w_branch_na", "w_branch_dil", "w_out", "w_up", "w_down", "w_ple_gate", "w_ple_proj")
GATHER_MIX = ("w_branch_na", "w_branch_dil", "w_out")
GATHER_MLP = ("w_up", "w_down", "w_ple_gate", "w_ple_proj")
REDUCE_EARLY = ("w_up", "w_down", "w_ple_gate", "w_out", "w_ple_proj", "w_branch_na", "w_branch_dil")
SMALL_ROWS = 16


def _cparams(sem=None):
    return pltpu.CompilerParams(dimension_semantics=sem, vmem_limit_bytes=VMEM_LIMIT)


def _mm(name, a, b, mode, tm, tn, tk, out_dtypes, epilogue=None, extras=(), consts=(), sums=(), after=(), into=None,
        b_view=None):
    if mode == "nn":
        (M, K), N = a.shape, b.shape[1]
    elif mode == "nt":
        (M, K), N = a.shape, b.shape[0]
    else:
        (K, M), N = a.shape, b.shape[1]
    if b_view is not None:
        N = b_view[0]
    tm, tn, tk = min(tm, M), min(tn, N), min(tk, K)
    assert M % tm == 0 and N % tn == 0 and K % tk == 0, (name, M, N, K, tm, tn, tk)
    if mode == "nn":
        a_spec = pl.BlockSpec((tm, tk), lambda i, j, k: (i, k))
        b_spec = pl.BlockSpec((tk, tn), lambda i, j, k: (k, j))
        dims = (((1,), (0,)), ((), ()))
    elif mode == "nt":
        a_spec = pl.BlockSpec((tm, tk), lambda i, j, k: (i, k))
        b_spec = pl.BlockSpec((tn, tk), lambda i, j, k: (j, k))
        dims = (((1,), (1,)), ((), ()))
    else:
        a_spec = pl.BlockSpec((tk, tm), lambda i, j, k: (k, i))
        b_spec = pl.BlockSpec((tk, tn), lambda i, j, k: (k, j))
        dims = (((0,), (0,)), ((), ()))
    if b_view is not None:
        b_spec = pl.BlockSpec(b_view[1], lambda i, j, k: b_view[2](j, k))
    nk = K // tk
    n_extra, n_const, n_out, n_sum = len(extras), len(consts), len(out_dtypes), len(sums)
    tile = pl.BlockSpec((tm, tn), lambda i, j, k: (i, j))
    assert not sums or tn == N, "row sums need whole rows in a tile"

    n_after = len(after)

    def body(a_ref, b_ref, *rest):
        extra_refs, rest = rest[:n_extra + n_const], rest[n_extra + n_const + n_after:]
        out_refs, sum_refs, acc = rest[:n_out], rest[n_out:n_out + n_sum], rest[-1]
        i, k = pl.program_id(0), pl.program_id(2)
        def product():
            return lax.dot_general(a_ref[...].astype(BF), b_ref[...].astype(BF), dims, preferred_element_type=F32)

        if nk > 1:
            @pl.when(k == 0)
            def _():
                acc[...] = jnp.zeros_like(acc)

            acc[...] += product()

        @pl.when(k == nk - 1)
        def _():
            total = product() if nk == 1 else acc[...]
            outs = (total,) if epilogue is None else epilogue(total, *[e[...] for e in extra_refs])
            for o_ref, val in zip(out_refs, outs[:n_out], strict=True):
                o_ref[...] = val.astype(o_ref.dtype).reshape(o_ref.shape)
            for s_ref, val in zip(sum_refs, outs[n_out:], strict=True):
                @pl.when(i == 0)
                def _():
                    s_ref[...] = val

                @pl.when(i != 0)
                def _():
                    s_ref[...] += val

    out_specs = [tile] * n_out + [pl.BlockSpec((1, c), lambda i, j, k: (0, 0)) for c in sums]
    out_shape = [jax.ShapeDtypeStruct((M, N), dt) for dt in out_dtypes] + [jax.ShapeDtypeStruct((1, c), F32) for c in sums]
    operands, aliases = [a, b, *extras, *consts, *after], {}
    in_specs = ([a_spec, b_spec] + [tile] * n_extra
                + [pl.BlockSpec(c.shape, functools.partial(lambda nd, i, j, k: (0,) * nd, c.ndim)) for c in consts] + [ANY] * n_after)
    if into is not None:
        assert n_out == 1
        target, block, index = into
        out_specs = [pl.BlockSpec(block, lambda i, j, k: index(i, j))]
        out_shape = [jax.ShapeDtypeStruct(target.shape, target.dtype)]
        if not isinstance(target, jax.ShapeDtypeStruct):
            aliases = {len(operands): 0}
            operands.append(target)
            in_specs.append(ANY)
            n_after += 1

    outs = pl.pallas_call(
        body, name=name, grid=(M // tm, N // tn, nk),
        in_specs=in_specs, out_specs=out_specs, out_shape=out_shape,
        scratch_shapes=[pltpu.VMEM((tm, tn) if nk > 1 else (8, 128), F32)], input_output_aliases=aliases,
        compiler_params=_cparams(("arbitrary",) * 3 if sums else ("parallel", "parallel", "arbitrary")),
    )(*operands)
    return outs[0] if len(outs) == 1 else outs


def _row(arr, tm, col_block=None, width=None):
    width = arr.shape[1] if width is None else width
    cb = 0 if col_block is None else col_block
    return arr, pl.BlockSpec((tm, width), lambda i: (i, cb))


def _full(arr):
    nd = arr.ndim
    return arr, pl.BlockSpec(arr.shape, lambda i: (0,) * nd)


def _rowwise(name, body, T, tm, ins, outs, sums=(), after=()):
    n_in, n_out, n_sum, n_after = len(ins), len(outs), len(sums), len(after)

    def kern(*refs):
        in_refs, refs = refs[:n_in], refs[n_in + n_after:]
        out_refs, sum_refs = refs[:n_out], refs[n_out:]
        res = body(*[r[...] for r in in_refs])
        res = res if isinstance(res, tuple) else (res,)
        for o_ref, val in zip(out_refs, res[:n_out], strict=True):
            o_ref[...] = val.astype(o_ref.dtype)
        if n_sum:
            @pl.when(pl.program_id(0) == 0)
            def _():
                for s_ref in sum_refs:
                    s_ref[...] = jnp.zeros_like(s_ref)

            for s_ref, val in zip(sum_refs, res[n_out:], strict=True):
                s_ref[...] += val

    res = pl.pallas_call(
        kern, name=name, grid=(T // tm,),
        in_specs=[spec for _, spec in ins] + [ANY] * n_after,
        out_specs=[pl.BlockSpec((tm, c), lambda i: (i, 0)) for c, _ in outs]
        + [pl.BlockSpec((1, c), lambda i: (0, 0)) for c in sums],
        out_shape=[jax.ShapeDtypeStruct((T, c), dt) for c, dt in outs]
        + [jax.ShapeDtypeStruct((1, c), F32) for c in sums],
        compiler_params=_cparams(("arbitrary",)),
    )(*[a for a, _ in ins], *after)
    return res[0] if len(res) == 1 else res


def _sigmoid(x):
    return 1.0 / (1.0 + jnp.exp(-x))


def _rms(h):
    return lax.rsqrt(jnp.mean(h * h, axis=-1, keepdims=True) + RMS_EPS)


def _rms_bwd(dy, h, g):
    r = _rms(h)
    n = h * r
    dn = dy * g
    dh = r * (dn - n * jnp.mean(dn * n, axis=-1, keepdims=True))
    return dh, jnp.sum(dy * n, axis=0, keepdims=True)


def _rope(x, cos2, sin_signed):
    lane = lax.broadcasted_iota(jnp.int32, x.shape, 1)
    swapped = jnp.where((lane % HEAD_DIM) < HEAD_DIM // 2, pltpu.roll(x, 128 - HEAD_DIM // 2, 1), pltpu.roll(x, HEAD_DIM // 2, 1))
    return x * cos2 + swapped * sin_signed


NA_KEYS = NA_WIN_ROWS * GRID_W
NA_BASES = 8


def _na_row_geometry(r, rows):
    first = jnp.clip(r - NA_WIN_ROWS // 2, 0, rows - NA_WIN_ROWS)
    base = first - r + (NA_WIN_ROWS - 1)
    return pl.multiple_of(first * GRID_W, GRID_W), base


NA_ROWS_PER_STEP = 16
NA_BWD_ROWS_PER_STEP = 8


def _softmax_rows(s):
    p = jnp.exp(s - jnp.max(s, axis=-1, keepdims=True))
    return p / jnp.sum(p, axis=-1, keepdims=True)


def _na_probs(q, kw, bias):
    return _softmax_rows(lax.dot_general(q, kw, (((1,), (1,)), ((), ())), preferred_element_type=F32) + bias)


def _split_pair(t):
    first = lax.broadcasted_iota(jnp.int32, t.shape, 1) < HEAD_DIM
    zero = jnp.zeros_like(t)
    return jnp.where(first, t, zero), jnp.where(first, zero, t)


def _join_pair(a, b):
    return jnp.where(lax.broadcasted_iota(jnp.int32, a.shape, 1) < HEAD_DIM, a, b)


_NT = (((1,), (1,)), ((), ()))
_TN = (((0,), (0,)), ((), ()))


def _na_fwd(qkv, tab):
    T = qkv.shape[0]
    rows = T // GRID_W
    n_pairs = NA_WIDTH // 128

    def body(q_ref, k_ref, v_ref, tab_ref, y_ref):
        def step(it, carry):
            geo = [_na_row_geometry(it * NA_ROWS_PER_STEP + u, rows) for u in range(NA_ROWS_PER_STEP)]
            q0s = [pl.multiple_of((it * NA_ROWS_PER_STEP + u) * GRID_W, GRID_W) for u in range(NA_ROWS_PER_STEP)]
            ss = [lax.dot_general(jnp.concatenate(_split_pair(q_ref[pl.ds(q0, GRID_W), :] * Q_SCALE), axis=0),
                                  k_ref[pl.ds(k0, NA_KEYS), :], _NT, preferred_element_type=F32)
                  for q0, (k0, _) in zip(q0s, geo)]
            ps = [_softmax_rows(s + jnp.concatenate([tab_ref[0, base], tab_ref[1, base]], axis=0)) for s, (_, base) in zip(ss, geo)]
            ys = [jnp.dot(p.astype(BF), v_ref[pl.ds(k0, NA_KEYS), :], preferred_element_type=F32) for p, (k0, _) in zip(ps, geo)]
            for q0, y2 in zip(q0s, ys):
                y_ref[pl.ds(q0, GRID_W), :] = _join_pair(y2[:GRID_W], y2[GRID_W:]).astype(y_ref.dtype)
            return carry

        lax.fori_loop(0, rows // NA_ROWS_PER_STEP, step, 0)

    def cols(first):
        return pl.BlockSpec((T, 128), lambda j: (0, first + j))

    return pl.pallas_call(
        body, name="na_fwd", grid=(n_pairs,),
        in_specs=[cols(0), cols(n_pairs), cols(2 * n_pairs), pl.BlockSpec((2, NA_BASES, GRID_W, NA_KEYS), lambda j: (j, 0, 0, 0))],
        out_specs=cols(0), out_shape=jax.ShapeDtypeStruct((T, NA_WIDTH), BF),
        compiler_params=_cparams(("parallel",)),
    )(qkv, qkv, qkv, tab)


def _na_bwd(qkv, tab, do):
    T = qkv.shape[0]
    rows = T // GRID_W
    n_pairs = NA_WIDTH // 128

    def body(q_ref, k_ref, v_ref, tab_ref, do_ref, dq_ref, dk_out, dv_out, dtab_ref, dk_ref, dv_ref):
        dk_ref[...] = jnp.zeros_like(dk_ref)
        dv_ref[...] = jnp.zeros_like(dv_ref)
        dtab_ref[...] = jnp.zeros_like(dtab_ref)

        def step(it, carry):
            U = NA_BWD_ROWS_PER_STEP
            geo = [_na_row_geometry(it * U + u, rows) for u in range(U)]
            q0s = [pl.multiple_of((it * U + u) * GRID_W, GRID_W) for u in range(U)]
            q2s = [jnp.concatenate(_split_pair(q_ref[pl.ds(q0, GRID_W), :] * Q_SCALE), axis=0) for q0 in q0s]
            do2s = [jnp.concatenate(_split_pair(do_ref[pl.ds(q0, GRID_W), :]), axis=0) for q0 in q0s]
            ss = [lax.dot_general(q2, k_ref[pl.ds(k0, NA_KEYS), :], _NT, preferred_element_type=F32) for q2, (k0, _) in zip(q2s, geo)]
            dps = [lax.dot_general(do2, v_ref[pl.ds(k0, NA_KEYS), :], _NT, preferred_element_type=F32) for do2, (k0, _) in zip(do2s, geo)]
            ps = [_softmax_rows(s + jnp.concatenate([tab_ref[0, base], tab_ref[1, base]], axis=0)) for s, (_, base) in zip(ss, geo)]
            dss = [p * (dp - jnp.sum(dp * p, axis=-1, keepdims=True)) for p, dp in zip(ps, dps)]
            dvs = [lax.dot_general(p.astype(BF), do2, _TN, preferred_element_type=F32) for p, do2 in zip(ps, do2s)]
            dsbs = [ds.astype(BF) for ds in dss]
            dqs = [jnp.dot(dsb, k_ref[pl.ds(k0, NA_KEYS), :], preferred_element_type=F32) for dsb, (k0, _) in zip(dsbs, geo)]
            dks = [lax.dot_general(dsb, q2, _TN, preferred_element_type=F32) for dsb, q2 in zip(dsbs, q2s)]
            for u in range(U):
                k0, base = geo[u]
                dtab_ref[0, base] += dss[u][:GRID_W]
                dtab_ref[1, base] += dss[u][GRID_W:]
                dq_ref[pl.ds(q0s[u], GRID_W), :] = (_join_pair(dqs[u][:GRID_W], dqs[u][GRID_W:]) * Q_SCALE).astype(dq_ref.dtype)
                dk_ref[pl.ds(k0, NA_KEYS), :] += dks[u]
                dv_ref[pl.ds(k0, NA_KEYS), :] += dvs[u]
            return carry

        lax.fori_loop(0, rows // NA_BWD_ROWS_PER_STEP, step, 0)
        dk_out[...] = dk_ref[...].astype(dk_out.dtype)
        dv_out[...] = dv_ref[...].astype(dv_out.dtype)

    def cols(first):
        return pl.BlockSpec((T, 128), lambda j: (0, first + j))

    tabs = pl.BlockSpec((2, NA_BASES, GRID_W, NA_KEYS), lambda j: (j, 0, 0, 0))
    wide = jax.ShapeDtypeStruct((T, NA_WIDTH), BF)
    return pl.pallas_call(
        body, name="na_bwd", grid=(n_pairs,),
        in_specs=[cols(0), cols(n_pairs), cols(2 * n_pairs), tabs, cols(0)],
        out_specs=[cols(0), cols(0), cols(0), tabs],
        out_shape=[wide, wide, wide, jax.ShapeDtypeStruct((NA_HEADS, NA_BASES, GRID_W, NA_KEYS), F32)],
        scratch_shapes=[pltpu.VMEM((T, 128), F32), pltpu.VMEM((T, 128), F32)],
        compiler_params=_cparams(("parallel",)),
    )(qkv, qkv, qkv, tab, do)


def _na_bias_table(rpb):
    H, n_rows, n_cols = rpb.shape

    def body(r_ref, tab_ref):
        q = lax.broadcasted_iota(jnp.int32, (GRID_W, 128), 0)
        kc = lax.broadcasted_iota(jnp.int32, (GRID_W, 128), 1)
        first = jnp.clip(q - NA_WIN_COLS // 2, 0, GRID_W - NA_WIN_COLS)
        valid = (kc >= first) & (kc < first + NA_WIN_COLS)
        toeplitz = []
        for ro in range(n_rows):
            row = jnp.broadcast_to(r_ref[pl.ds(ro, 1), :], (GRID_W, 128))
            shifted = pltpu.roll(pltpu.roll(row, 128 - (NA_WIN_COLS - 1), 1), 0, 1, stride=1, stride_axis=0)
            toeplitz.append(jnp.where(valid, shifted, NEG_INF))
        for base in range(NA_BASES):
            for j in range(NA_WIN_ROWS // 2):
                even, odd = toeplitz[base + 2 * j], toeplitz[base + 2 * j + 1]
                tab_ref[base, :, pl.ds(j * 128, 128)] = jnp.where(kc < GRID_W, even, pltpu.roll(odd, GRID_W, 1))

    padded = jnp.pad(rpb, ((0, 0), (0, 16 - n_rows), (0, 128 - n_cols)))
    return pl.pallas_call(
        body, name="na_bias_table", grid=(H,),
        in_specs=[pl.BlockSpec((None, 16, 128), lambda h: (h, 0, 0))],
        out_specs=pl.BlockSpec((None, NA_BASES, GRID_W, NA_KEYS), lambda h: (h, 0, 0, 0)),
        out_shape=jax.ShapeDtypeStruct((H, NA_BASES, GRID_W, NA_KEYS), F32),
        compiler_params=_cparams(("parallel",)),
    )(padded)


def _na_rpb_grad(dtab, after=()):
    H = dtab.shape[0]
    n_rows = 2 * NA_WIN_ROWS - 1
    n_cols = 2 * NA_WIN_COLS - 1

    def body(d_ref, *rest):
        o_ref = rest[-1]
        lane = lax.broadcasted_iota(jnp.int32, (GRID_W, 128), 1)
        low = lane < GRID_W
        flip = (lax.broadcasted_iota(jnp.int32, (GRID_W, GRID_W), 0) + lax.broadcasted_iota(jnp.int32, (GRID_W, GRID_W), 1)
                == GRID_W - 1).astype(BF)

        def reverse_rows(t):
            out = jnp.zeros_like(t)
            for _ in range(3):
                piece = t.astype(BF)
                out = out + jnp.dot(flip, piece, preferred_element_type=F32)
                t = t - piece.astype(F32)
            return out

        out_rows = []
        for ro in range(n_rows):
            acc = jnp.zeros((GRID_W, 128), F32)
            for base in range(NA_BASES):
                i = ro - base
                if not 0 <= i < NA_WIN_ROWS:
                    continue
                pair = d_ref[base, :, pl.ds((i // 2) * 128, 128)]
                if i % 2:
                    pair = pltpu.roll(pair, GRID_W, 1)
                acc = acc + jnp.where(low, pair, 0.0)
            skew = pltpu.roll(reverse_rows(acc), 0, 1, stride=1, stride_axis=0)
            diag = jnp.sum(skew, axis=0, keepdims=True)
            out_rows.append(pltpu.roll(jnp.broadcast_to(diag, (8, 128)), 128 - (GRID_W - NA_WIN_COLS), 1)[:1])
        out_rows.append(jnp.zeros((1, 128), F32))
        res = jnp.concatenate(out_rows, axis=0)
        o_ref[...] = jnp.where(lax.broadcasted_iota(jnp.int32, res.shape, 1) < n_cols, res, 0.0)

    return pl.pallas_call(
        body, name="na_rpb_grad", grid=(H,),
        in_specs=[pl.BlockSpec((None, NA_BASES, GRID_W, NA_KEYS), lambda h: (h, 0, 0, 0))] + [ANY] * len(after),
        out_specs=pl.BlockSpec((None, n_rows + 1, 128), lambda h: (h, 0, 0)),
        out_shape=jax.ShapeDtypeStruct((H, n_rows + 1, 128), F32),
        compiler_params=_cparams(("parallel",)),
    )(dtab, *after)


BAND_Q = 128
BAND_KEYS = BAND_Q + 2 * DIL_RADIUS


def _band_geometry(n, L):
    q0 = pl.multiple_of(n * BAND_Q, BAND_Q)
    k0 = pl.multiple_of(jnp.clip(q0 - DIL_RADIUS, 0, L - BAND_KEYS), DIL_RADIUS)
    qi = q0 + lax.broadcasted_iota(jnp.int32, (BAND_Q, BAND_KEYS), 0)
    kj = k0 + lax.broadcasted_iota(jnp.int32, (BAND_Q, BAND_KEYS), 1)
    return q0, k0, jnp.abs(qi - kj) <= DIL_RADIUS


DIL_PAIRS = DIL_OUT_WIDTH // 128


def _residue_shape(dil, T, dtype):
    return jax.ShapeDtypeStruct((DIL_PAIRS, dil, T // dil, 128), dtype)


def _residue_tile(dil, tm):
    return pl.BlockSpec((DIL_PAIRS, dil, tm // dil, 128), lambda i: (0, 0, i, 0))


def _to_natural(ref, scratch, dil, tm):
    tiles = []
    for pair in range(DIL_PAIRS):
        if dil == 1:
            tiles.append(ref[pair, 0].astype(F32))
            continue
        for r in range(dil):
            scratch[pl.ds(r, tm // dil, stride=dil), :] = ref[pair, r].astype(F32)
        tiles.append(scratch[...])
    return tiles


def _from_natural(tile, scratch, ref, pair, dil, tm):
    if dil == 1:
        ref[pair, 0] = tile.astype(ref.dtype)
        return
    scratch[...] = tile
    for r in range(dil):
        ref[pair, r] = scratch[pl.ds(r, tm // dil, stride=dil), :].astype(ref.dtype)


def _band_specs(group, T):
    dil = DIL_GROUPS[group][1]
    L = T // dil
    assert L % BAND_Q == 0 and L >= BAND_KEYS, (T, dil)
    per_residue = min(BAND_BLOCKS_PER_STEP, L // BAND_Q)
    residues = min(dil, BAND_BLOCKS_PER_STEP // per_residue)
    spec = pl.BlockSpec((None, residues, L, 128), lambda s: (s % DIL_PAIRS, s // DIL_PAIRS, 0, 0))
    return L, residues, per_residue, (dil // residues * DIL_PAIRS,), spec


BAND_BLOCKS_PER_STEP = 8


def _band_softmax(s, valid):
    s = jnp.where(valid, s, NEG_INF)
    m = jnp.max(s, axis=-1, keepdims=True)
    p = jnp.exp(s - m)
    l = jnp.sum(p, axis=-1, keepdims=True)
    return p / l, m + jnp.log(l)


def _band_fwd(q, k, v, group):
    T = q.shape[1] * q.shape[2]
    L, residues, U, grid, spec = _band_specs(group, T)

    def body(q_ref, k_ref, v_ref, o_ref, lse_ref):
        def step(it, carry):
            geo = [(r, *_band_geometry(it * U + u, L)) for r in range(residues) for u in range(U)]
            ss = [lax.dot_general(jnp.concatenate(_split_pair(q_ref[r, pl.ds(q0, BAND_Q), :]), axis=0),
                                  k_ref[r, pl.ds(k0, BAND_KEYS), :], _NT, preferred_element_type=F32) for r, q0, k0, _ in geo]
            pls = [_band_softmax(s, jnp.concatenate([valid, valid], axis=0)) for s, (_, _, _, valid) in zip(ss, geo)]
            os = [jnp.dot(p.astype(BF), v_ref[r, pl.ds(k0, BAND_KEYS), :], preferred_element_type=F32)
                  for (p, _), (r, _, k0, _) in zip(pls, geo)]
            for (r, q0, _, _), o2, (_, lse) in zip(geo, os, pls):
                o_ref[r, pl.ds(q0, BAND_Q), :] = _join_pair(o2[:BAND_Q], o2[BAND_Q:])
                lse2 = jnp.broadcast_to(lse, (2 * BAND_Q, 128))
                lse_ref[r, pl.ds(q0, BAND_Q), :] = _join_pair(lse2[:BAND_Q], lse2[BAND_Q:])
            return carry

        lax.fori_loop(0, L // (BAND_Q * U), step, 0)

    res = _residue_shape(DIL_GROUPS[group][1], T, F32)
    return pl.pallas_call(
        body, name=f"band_fwd_g{group}", grid=grid,
        in_specs=[spec] * 3, out_specs=[spec] * 2, out_shape=[res, res],
        compiler_params=_cparams(("parallel",)),
    )(q, k, v)


def _band_bwd(q, k, v, do, dlse, group):
    T = q.shape[1] * q.shape[2]
    L, residues, U, grid, spec = _band_specs(group, T)

    def body(q_ref, k_ref, v_ref, do_ref, dlse_ref, dq_ref, dk_ref, dv_ref):
        dk_ref[...] = jnp.zeros_like(dk_ref)
        dv_ref[...] = jnp.zeros_like(dv_ref)

        def step(it, carry):
            geo = [(r, *_band_geometry(it * U + u, L)) for r in range(residues) for u in range(U)]
            q2s = [jnp.concatenate(_split_pair(q_ref[r, pl.ds(q0, BAND_Q), :]), axis=0) for r, q0, _, _ in geo]
            do2s = [jnp.concatenate(_split_pair(do_ref[r, pl.ds(q0, BAND_Q), :]), axis=0) for r, q0, _, _ in geo]
            ss = [lax.dot_general(q2, k_ref[r, pl.ds(k0, BAND_KEYS), :], _NT, preferred_element_type=F32)
                  for q2, (r, _, k0, _) in zip(q2s, geo)]
            dps = [lax.dot_general(do2, v_ref[r, pl.ds(k0, BAND_KEYS), :], _NT, preferred_element_type=F32)
                   for do2, (r, _, k0, _) in zip(do2s, geo)]
            ps = [_band_softmax(s, jnp.concatenate([valid, valid], axis=0))[0] for s, (_, _, _, valid) in zip(ss, geo)]
            dss = []
            for p, dp, (r, q0, _, _) in zip(ps, dps, geo):
                dl = dlse_ref[r, pl.ds(q0, BAND_Q), :]
                dl2 = jnp.concatenate([dl[:, :1], dl[:, HEAD_DIM:HEAD_DIM + 1]], axis=0)
                dss.append(p * (dp - jnp.sum(dp * p, axis=-1, keepdims=True) + dl2))
            dvs = [lax.dot_general(p.astype(BF), do2, _TN, preferred_element_type=F32) for p, do2 in zip(ps, do2s)]
            dsbs = [ds.astype(BF) for ds in dss]
            dqs = [jnp.dot(dsb, k_ref[r, pl.ds(k0, BAND_KEYS), :], preferred_element_type=F32) for dsb, (r, _, k0, _) in zip(dsbs, geo)]
            dks = [lax.dot_general(dsb, q2, _TN, preferred_element_type=F32) for dsb, q2 in zip(dsbs, q2s)]
            for u, (r, q0, k0, _) in enumerate(geo):
                dq_ref[r, pl.ds(q0, BAND_Q), :] = _join_pair(dqs[u][:BAND_Q], dqs[u][BAND_Q:])
                dk_ref[r, pl.ds(k0, BAND_KEYS), :] += dks[u]
                dv_ref[r, pl.ds(k0, BAND_KEYS), :] += dvs[u]
            return carry

        lax.fori_loop(0, L // (BAND_Q * U), step, 0)

    res = _residue_shape(DIL_GROUPS[group][1], T, F32)
    return pl.pallas_call(
        body, name=f"band_bwd_g{group}", grid=grid,
        in_specs=[spec] * 5, out_specs=[spec] * 3, out_shape=[res] * 3,
        compiler_params=_cparams(("parallel",)),
    )(q, k, v, do, dlse)


def _head_sums(t):
    head = lax.broadcasted_iota(jnp.int32, t.shape, 1) // HEAD_DIM
    out = jnp.zeros_like(t)
    for h in range(t.shape[1] // HEAD_DIM):
        mine = head == h
        out = jnp.where(mine, jnp.sum(jnp.where(mine, t, 0.0), axis=-1, keepdims=True), out)
    return out


def _dil_merge_fwd(os, lses, T, tm):
    G = len(DIL_GROUPS)
    W = DIL_OUT_WIDTH
    dils = [d for _, d in DIL_GROUPS]

    def body(*refs):
        o_refs, lse_refs = refs[:G], refs[G:2 * G]
        y_ref, w_refs, on_refs, scratch = refs[2 * G], refs[2 * G + 1:3 * G + 1], refs[3 * G + 1:4 * G + 1], refs[-1]
        o = [jnp.concatenate(_to_natural(r, scratch, d, tm), axis=1) for r, d in zip(o_refs, dils)]
        ls = [jnp.concatenate(_to_natural(r, scratch, d, tm), axis=1) for r, d in zip(lse_refs, dils)]
        m = functools.reduce(jnp.maximum, ls)
        es = [jnp.exp(l - m) for l in ls]
        tot = functools.reduce(jnp.add, es)
        ws = [e / tot for e in es]
        y_ref[...] = functools.reduce(jnp.add, [w * t for w, t in zip(ws, o)]).astype(y_ref.dtype)
        for g in range(G):
            w_refs[g][...] = ws[g]
            on_refs[g][...] = o[g]

    nat = pl.BlockSpec((tm, W), lambda i: (i, 0))
    res = pl.pallas_call(
        body, name="dil_merge_fwd", grid=(T // tm,),
        in_specs=[_residue_tile(d, tm) for d in dils] * 2,
        out_specs=[nat] * (2 * G + 1),
        out_shape=[jax.ShapeDtypeStruct((T, W), BF)] + [jax.ShapeDtypeStruct((T, W), F32)] * (2 * G),
        scratch_shapes=[pltpu.VMEM((tm, 128), F32)],
        compiler_params=_cparams(("parallel",)),
    )(*os, *lses)
    return res[0], res[1:G + 1], res[G + 1:]


def _dil_merge_bwd(dy, os, ws, tm, after=()):
    G = len(DIL_GROUPS)
    T, W = dy.shape
    dils = [d for _, d in DIL_GROUPS]
    n_after = len(after)

    def body(*refs):
        dyt = refs[0][...]
        o, w = [r[...] for r in refs[1:G + 1]], [r[...] for r in refs[G + 1:2 * G + 1]]
        refs = refs[2 * G + 1 + n_after:]
        do_refs, dlse_refs, scratch = refs[:G], refs[G:2 * G], refs[-1]
        dws = [_head_sums(dyt * t) for t in o]
        mean = functools.reduce(jnp.add, [a * b for a, b in zip(w, dws)])
        for g, d in enumerate(dils):
            do, dlse = w[g] * dyt, w[g] * (dws[g] - mean)
            for pair in range(DIL_PAIRS):
                cols = slice(pair * 128, (pair + 1) * 128)
                _from_natural(do[:, cols], scratch, do_refs[g], pair, d, tm)
                _from_natural(dlse[:, cols], scratch, dlse_refs[g], pair, d, tm)

    nat = pl.BlockSpec((tm, W), lambda i: (i, 0))
    res = pl.pallas_call(
        body, name="dil_merge_bwd", grid=(T // tm,),
        in_specs=[nat] * (2 * G + 1) + [ANY] * n_after,
        out_specs=[_residue_tile(d, tm) for d in dils] * 2,
        out_shape=[_residue_shape(d, T, BF) for d in dils] + [_residue_shape(d, T, F32) for d in dils],
        scratch_shapes=[pltpu.VMEM((tm, 128), F32)],
        compiler_params=_cparams(("parallel",)),
    )(dy, *os, *ws, *after)
    return res[:G], res[G:]


def _qkv_prep(z, cos2, sin_signed, tm):
    T = z.shape[0]
    G = len(DIL_GROUPS)
    dils = [d for _, d in DIL_GROUPS]
    n_dil_blocks = 3 * DIL_WIDTH // 128

    def body(*refs):
        blocks = refs[:n_dil_blocks]
        cos_ref, sin_ref = refs[n_dil_blocks], refs[1 + n_dil_blocks]
        outs = refs[2 + n_dil_blocks:]
        for part in range(3):
            for g, d in enumerate(dils):
                out = outs[g * 3 + part]
                for pair in range(DIL_PAIRS):
                    blk = blocks[part * (DIL_WIDTH // 128) + g * DIL_PAIRS + pair]
                    for r in range(d):
                        rows = pl.ds(r, tm // d, stride=d) if d > 1 else slice(None)
                        x = blk[rows, :]
                        if part < 2:
                            x = _rope(x, cos_ref[rows, :], sin_ref[rows, :])
                        if part == 0:
                            x = x * Q_SCALE
                        out[pair, r] = x.astype(out.dtype)

    lane_block = [pl.BlockSpec((tm, 128), functools.partial(lambda b, i: (i, b), b)) for b in range(n_dil_blocks)]
    tab = pl.BlockSpec((tm, 128), lambda i: (i, 0))
    res = pl.pallas_call(
        body, name="qkv_prep", grid=(T // tm,),
        in_specs=lane_block + [tab, tab],
        out_specs=[_residue_tile(d, tm) for d in dils for _ in range(3)],
        out_shape=[_residue_shape(d, T, BF) for d in dils for _ in range(3)],
        compiler_params=_cparams(("parallel",)),
    )(*[z] * n_dil_blocks, cos2, sin_signed)
    return [res[3 * g:3 + 3 * g] for g in range(G)]


def _qkv_unprep(d_na, d_dil, cos2, sin_signed, tm, after=()):
    T = d_na[0].shape[0]
    G = len(DIL_GROUPS)
    dils = [d for _, d in DIL_GROUPS]
    n_after = len(after)

    def body(*refs):
        dq, dk, dv = (r[...] for r in refs[:3])
        res_refs = refs[3:3 + 3 * G]
        cs, sn = refs[3 + 3 * G][...], refs[4 + 3 * G][...]
        out, scratch = refs[5 + 3 * G + n_after], refs[-1]
        cols = [dq, dk, dv]
        for part in range(3):
            for g, d in enumerate(dils):
                for x in _to_natural(res_refs[g * 3 + part], scratch, d, tm):
                    if part < 2:
                        x = _rope(x, cs, -sn)
                    cols.append((x * Q_SCALE if part == 0 else x).astype(out.dtype))
        out[...] = jnp.concatenate(cols, axis=1)

    wide = pl.BlockSpec((tm, NA_WIDTH), lambda i: (i, 0))
    tab = pl.BlockSpec((tm, 128), lambda i: (i, 0))
    return pl.pallas_call(
        body, name="qkv_unprep", grid=(T // tm,),
        in_specs=[wide] * 3 + [_residue_tile(d, tm) for d in dils for _ in range(3)] + [tab, tab] + [ANY] * n_after,
        out_specs=pl.BlockSpec((tm, QKV_WIDTH), lambda i: (i, 0)),
        out_shape=jax.ShapeDtypeStruct((T, QKV_WIDTH), BF),
        scratch_shapes=[pltpu.VMEM((tm, 128), F32)],
        compiler_params=_cparams(("parallel",)),
    )(*d_na, *[t for g in range(G) for t in d_dil[g]], cos2, sin_signed, *after)


def _rope_tables(positions):
    half = HEAD_DIM // 2
    inv_freq = ROPE_THETA ** (-jnp.arange(half, dtype=F32) / half)
    ang = positions.astype(F32)[:, None] * inv_freq
    cos, sin = jnp.cos(ang), jnp.sin(ang)
    return jnp.tile(jnp.concatenate([cos, cos], axis=1), (1, 2)), jnp.tile(jnp.concatenate([-sin, sin], axis=1), (1, 2))


def _pack_rows(t):
    return t.reshape(-1, PACK_W)


def _me():
    return lax.axis_index("x"), lax.axis_index("y"), lax.axis_index("c")


def _other_chips(x, y):
    return [(1 - x, y), (x, 1 - y), (1 - x, 1 - y)]


def _pair_sum(g, got, tm):
    S, R, W = g.shape
    half = R // 2
    nb = half // tm

    def body(pos_ref, g_ref, got_ref, own_ref, ob_ref):
        tot = g_ref[...] + got_ref[...]
        ob_ref[...] = tot.astype(ob_ref.dtype)

        @pl.when(pl.program_id(1) == pos_ref[1])
        def _():
            own_ref[...] = tot

    tile = pl.BlockSpec((None, tm, W), lambda i, s, pos_ref: (s, i, 0))
    c, chip = lax.axis_index("c"), 2 * lax.axis_index("x") + lax.axis_index("y")
    return pl.pallas_call(
        body, name="pair_sum",
        grid_spec=pltpu.PrefetchScalarGridSpec(
            num_scalar_prefetch=1, grid=(nb, S),
            in_specs=[pl.BlockSpec((None, tm, W), lambda i, s, pos_ref: (s, pos_ref[0] * nb + i, 0)), tile],
            out_specs=[pl.BlockSpec((tm, W), lambda i, s, pos_ref: (i, 0)), tile]),
        out_shape=[jax.ShapeDtypeStruct((half, W), F32), jax.ShapeDtypeStruct((S, half, W), BF)],
        compiler_params=_cparams(("parallel", "arbitrary")),
    )(jnp.stack([c, chip]).astype(jnp.int32), g, got)


def _chip_sum(own, others, tm):
    n, h, W = others.shape
    nb = h // tm

    def body(c_ref, own_ref, p_ref, o_ref):
        o_ref[...] = ((own_ref[...] + p_ref[0].astype(F32)) + p_ref[1].astype(F32)) + p_ref[2].astype(F32)

    return pl.pallas_call(
        body, name="chip_sum",
        grid_spec=pltpu.PrefetchScalarGridSpec(
            num_scalar_prefetch=1, grid=(nb,),
            in_specs=[pl.BlockSpec((tm, W), lambda i, c_ref: (i, 0)), pl.BlockSpec((n, tm, W), lambda i, c_ref: (0, i, 0))],
            out_specs=pl.BlockSpec((tm, W), lambda i, c_ref: (c_ref[0] * nb + i, 0))),
        out_shape=jax.ShapeDtypeStruct((2 * h, W), F32),
        compiler_params=_cparams(("parallel",)),
    )(lax.axis_index("c").reshape(1).astype(jnp.int32), own, others)


def _join_halves(shard):
    h = shard.shape[0] // 2

    def body(in_ref, out_ref, send_sem, recv_sem):
        x, y, c = _me()
        cp = pltpu.make_async_remote_copy(
            src_ref=in_ref.at[pl.ds(c * h, h), :], dst_ref=out_ref.at[pl.ds(c * h, h), :],
            send_sem=send_sem, recv_sem=recv_sem, device_id=(x, y, 1 - c), device_id_type=MESH)
        cp.start()
        pltpu.make_async_remote_copy(
            src_ref=in_ref.at[pl.ds(c * h, h), :], dst_ref=out_ref.at[pl.ds((1 - c) * h, h), :],
            send_sem=send_sem, recv_sem=recv_sem, device_id=(x, y, 1 - c), device_id_type=MESH).wait_recv()
        cp.wait_send()

    return pl.pallas_call(
        body, name="join_halves", in_specs=[ANY], out_specs=ANY,
        out_shape=jax.ShapeDtypeStruct(shard.shape, shard.dtype), input_output_aliases={0: 0},
        scratch_shapes=[pltpu.SemaphoreType.DMA, pltpu.SemaphoreType.DMA],
    )(shard)


def _allreduce_small(s, after=()):
    R, W = s.shape
    n_after = len(after)

    def body(s_ref, *rest):
        o_ref, buf, send_sems, recv_sems = rest[n_after:]
        x, y, c = _me()
        me = 4 * x + 2 * y + c
        buf[me] = s_ref[...]
        peers = [((x + fx) % 2, (y + fy) % 2, (c + fc) % 2) for fx in range(2) for fy in range(2) for fc in range(2)][1:]
        sends = [pltpu.make_async_remote_copy(
            src_ref=s_ref, dst_ref=buf.at[me], send_sem=send_sems.at[k], recv_sem=recv_sems.at[k],
            device_id=peer, device_id_type=MESH) for k, peer in enumerate(peers)]
        for cp in sends:
            cp.start()
        for k, peer in enumerate(peers):
            pltpu.make_async_remote_copy(
                src_ref=s_ref, dst_ref=buf.at[4 * peer[0] + 2 * peer[1] + peer[2]], send_sem=send_sems.at[k],
                recv_sem=recv_sems.at[k], device_id=peer, device_id_type=MESH).wait_recv()
        for cp in sends:
            cp.wait_send()
        total = buf[0]
        for d in range(1, N_DEV):
            total = total + buf[d]
        o_ref[...] = total

    return pl.pallas_call(
        body, name="allreduce_small",
        in_specs=[pl.BlockSpec(memory_space=pltpu.VMEM)] + [ANY] * n_after, out_specs=pl.BlockSpec(memory_space=pltpu.VMEM),
        out_shape=jax.ShapeDtypeStruct((R, W), F32),
        scratch_shapes=[pltpu.VMEM((N_DEV, R, W), F32), pltpu.SemaphoreType.DMA((N_DEV - 1,)), pltpu.SemaphoreType.DMA((N_DEV - 1,))],
    )(s, *after)


HBM_SPEC = pl.BlockSpec(memory_space=pltpu.HBM)
SEM_SPEC = pl.BlockSpec(memory_space=pltpu.SEMAPHORE)
DATAFLOW = pltpu.SideEffectType.DATAFLOW_SIDE_EFFECTING


class _InFlight(NamedTuple):
    sems: tuple
    src: jax.Array
    land: jax.Array
    token: jax.Array


def _split_start(name, src, land_shape, land_dtype, n, copies, after=()):
    n_after = len(after)

    def body(src_ref, land_ref, *rest):
        rest = rest[n_after:]
        sems, token = rest[:2 * n], rest[-1]
        for k, (s, d, peer) in enumerate(copies(src_ref, land_ref)):
            pltpu.make_async_remote_copy(src_ref=s, dst_ref=d, send_sem=sems[k], recv_sem=sems[n + k],
                                         device_id=peer, device_id_type=MESH).start()
        token[...] = jnp.zeros_like(token)

    outs = pl.pallas_call(
        body, name=name,
        out_shape=(*[pltpu.SemaphoreType.DMA(())] * (2 * n), pltpu.HBM(src.shape, src.dtype), pltpu.HBM(land_shape, land_dtype),
                   jax.ShapeDtypeStruct((8, 128), F32)),
        in_specs=(HBM_SPEC, HBM_SPEC, *[ANY] * n_after),
        out_specs=(*[SEM_SPEC] * (2 * n), HBM_SPEC, HBM_SPEC, pl.BlockSpec(memory_space=pltpu.VMEM)),
        input_output_aliases={0: 2 * n, 1: 2 * n + 1},
        compiler_params=pltpu.CompilerParams(has_side_effects=DATAFLOW),
    )(pltpu.with_memory_space_constraint(src, pltpu.HBM), pltpu.with_memory_space_constraint(lax.empty(land_shape, land_dtype), pltpu.HBM),
      *after)
    return _InFlight(tuple(outs[:2 * n]), outs[2 * n], outs[2 * n + 1], outs[2 * n + 2])


def _split_wait(name, flight, after, n, copies):
    def body(src_ref, land_ref, *rest):
        sems = rest[:2 * n]
        for k, (s, d, peer) in enumerate(copies(src_ref, land_ref)):
            cp = pltpu.make_async_remote_copy(src_ref=s, dst_ref=d, send_sem=sems[k], recv_sem=sems[n + k],
                                              device_id=peer, device_id_type=MESH)
            cp.wait_send()
            cp.wait_recv()

    return pl.pallas_call(
        body, name=name,
        out_shape=(pltpu.HBM(flight.src.shape, flight.src.dtype), pltpu.HBM(flight.land.shape, flight.land.dtype)),
        in_specs=(HBM_SPEC, HBM_SPEC, *[SEM_SPEC] * (2 * n), ANY),
        out_specs=(HBM_SPEC, HBM_SPEC), input_output_aliases={0: 0, 1: 1},
        compiler_params=pltpu.CompilerParams(has_side_effects=DATAFLOW),
    )(flight.src, flight.land, *flight.sems, after)


def _gather_copies(src_ref, land_ref):
    x, y, c = _me()
    return [(src_ref, land_ref.at[2 * x + y], (*chip, c)) for chip in _other_chips(x, y)]


def _gather_start(packed, tag, after=()):
    return _split_start(f"gather_start_{tag}", packed, (N_CHIPS, *packed.shape), packed.dtype, 3, _gather_copies, after)


def _gather_wait(flight, after, tag):
    src, others = _split_wait(f"gather_wait_{tag}", flight, after, 3, _gather_copies)
    return lax.dynamic_update_slice(others, src[None], (2 * lax.axis_index("x") + lax.axis_index("y"), 0, 0))


def _across_copies(src_ref, land_ref):
    x, y, c = _me()
    half = src_ref.shape[0] // 2
    rows = pl.ds(c * half, half)
    return [(src_ref.at[rows, :], land_ref.at[2 * x + y, rows, :], (*chip, c)) for chip in _other_chips(x, y)]


def _to_sibling_copies(all_ref, unused_ref):
    x, y, c = _me()
    half = all_ref.shape[1] // 2
    places = [all_ref.at[2 * chip[0] + chip[1], pl.ds(c * half, half), :] for chip in _other_chips(x, y)]
    return [(place, place, (x, y, 1 - c)) for place in places]


def _gather_halves_start(shard, tag, after=()):
    return _split_start(f"gather_{tag}_across_start", shard, (N_CHIPS, *shard.shape), shard.dtype, 3, _across_copies, after)


def _gather_halves_relay(flight, after, tag):
    shard, landed = _split_wait(f"gather_{tag}_across_wait", flight, after, 3, _across_copies)
    return shard, _split_start(f"gather_{tag}_sibling_start", landed, (8, 128), landed.dtype, 3, _to_sibling_copies)


def _gather_halves_finish(shard, relay, after, tag):
    others = _split_wait(f"gather_{tag}_sibling_wait", relay, after, 3, _to_sibling_copies)[0]
    return lax.dynamic_update_slice(others, shard[None], (2 * lax.axis_index("x") + lax.axis_index("y"), 0, 0))


def _assemble_w_in(shards, tm):
    S, R, C = shards.shape
    n_gates = 2 * D_MODEL

    def body(s_ref, w_ref, g_ref):
        full = jnp.concatenate([s_ref[s] for s in range(S)], axis=1)
        w_ref[...] = full
        g_ref[...] = full[:, S * C - n_gates:]

    return pl.pallas_call(
        body, name="assemble_w_in", grid=(R // tm,),
        in_specs=[pl.BlockSpec((S, tm, C), lambda i: (0, i, 0))],
        out_specs=[pl.BlockSpec((tm, S * C), lambda i: (i, 0)), pl.BlockSpec((tm, n_gates), lambda i: (i, 0))],
        out_shape=[jax.ShapeDtypeStruct((R, S * C), shards.dtype), jax.ShapeDtypeStruct((R, n_gates), shards.dtype)],
        compiler_params=_cparams(("parallel",)),
    )(shards)


def _swap_copies(src_ref, land_ref):
    x, y, c = _me()
    half = land_ref.shape[1]
    return [(src_ref.at[:, pl.ds((1 - c) * half, half), :], land_ref, (x, y, 1 - c))]


def _swap_start(g, tag):
    S, R, W = g.shape
    return _split_start(f"swap_halves_start_{tag}", g, (S, R // 2, W), g.dtype, 1, _swap_copies)


def _swap_wait(flight, after, tag):
    return _split_wait(f"swap_halves_wait_{tag}", flight, after, 1, _swap_copies)


def _scatter_copies(src_ref, land_ref):
    x, y, c = _me()
    return [(src_ref.at[2 * chip[0] + chip[1]], land_ref.at[j], (*chip, c)) for j, chip in enumerate(_other_chips(x, y))]


def _scatter_start(part, tag):
    S, h, W = part.shape
    return _split_start(f"scatter_chips_start_{tag}", part, (S - 1, h, W), part.dtype, 3, _scatter_copies)


def _scatter_wait(flight, after, tag):
    return _split_wait(f"scatter_chips_wait_{tag}", flight, after, 3, _scatter_copies)[1]


def _join_copies(shard_ref, unused_ref):
    x, y, c = _me()
    h = shard_ref.shape[0] // 2
    rows = shard_ref.at[pl.ds(c * h, h), :]
    return [(rows, rows, (x, y, 1 - c))]


def _join_start(shard):
    return _split_start("join_halves_start", shard, (8, 128), shard.dtype, 1, _join_copies)


def _join_wait(flight, after):
    return _split_wait("join_halves_wait", flight, after, 1, _join_copies)[0]


def _adamw(name, g, g_row0, w, m, v):
    _, R, C = w.shape
    tm = next(cand for cand in (368, 256, 128, 64, 32, 16, 8) if R % cand == 0)
    assert g_row0 % tm == 0 and g.shape[1] == C

    def body(g_ref, w_ref, m_ref, v_ref, go_ref, d_ref, mo_ref, vo_ref):
        gt = g_ref[...]
        mt = ADAM_B1 * m_ref[...] + (1.0 - ADAM_B1) * gt
        vt = ADAM_B2 * v_ref[...] + (1.0 - ADAM_B2) * jnp.square(gt)
        m_hat = mt / (1.0 - ADAM_B1 ** ADAM_STEP)
        v_hat = vt / (1.0 - ADAM_B2 ** ADAM_STEP)
        go_ref[...] = gt
        d_ref[...] = -ADAM_LR * (m_hat / (jnp.sqrt(v_hat) + ADAM_EPS) + ADAM_WD * w_ref[...])
        mo_ref[...] = mt
        vo_ref[...] = vt

    state = pl.BlockSpec((None, tm, C), lambda i: (0, i, 0))
    return pl.pallas_call(
        body, name=name, grid=(R // tm,),
        in_specs=[pl.BlockSpec((tm, C), lambda i: (g_row0 // tm + i, 0)), state, state, state],
        out_specs=[state] * 4, out_shape=[jax.ShapeDtypeStruct((1, R, C), F32)] * 4,
        compiler_params=_cparams(("parallel",)),
    )(g, w, m, v)


def _unpack_weights(gathered, names):
    S = gathered.shape[0]
    shard_shapes = {"w_in": (D_MODEL, (QKV_WIDTH + 2 * D_MODEL) // S), "w_branch_na": (NA_WIDTH, D_MODEL // S),
                    "w_branch_dil": (DIL_OUT_WIDTH, D_MODEL // S), "w_out": (D_MODEL // S, D_MODEL),
                    "w_up": (D_MODEL, D_FF // S), "w_down": (D_FF // S, D_MODEL),
                    "w_ple_gate": (D_MODEL // S, D_MODEL), "w_ple_proj": (PLE_DIM, D_MODEL // S)}
    col_sharded = {"w_in", "w_branch_na", "w_branch_dil", "w_up", "w_ple_proj"}
    out, r0 = {}, 0
    for name in names:
        rows, cols = shard_shapes[name]
        n = rows * cols // PACK_W
        t = gathered[:, r0:r0 + n, :].reshape(S, rows, cols)
        r0 += n
        out[name] = t.transpose(1, 0, 2).reshape(rows, S * cols) if name in col_sharded else t.reshape(S * rows, cols)
    return out


def kernel(x, p, positions, g_mix, w_in, rpb, w_branch_na, w_branch_dil, w_out, g_mlp, w_up, w_down, g_ple, w_ple_gate, w_ple_proj, g_final, loss_target, m_g_mix, m_w_in, m_rpb, m_w_branch_na, m_w_branch_dil, m_w_out, m_g_mlp, m_w_up, m_w_down, m_g_ple, m_w_ple_gate, m_w_ple_proj, m_g_final, v_g_mix, v_w_in, v_rpb, v_w_branch_na, v_w_branch_dil, v_w_out, v_g_mlp, v_w_up, v_w_down, v_g_ple, v_w_ple_gate, v_w_ple_proj, v_g_final):
    shards = {"w_in": w_in[0], "w_branch_na": w_branch_na[0], "w_branch_dil": w_branch_dil[0], "w_out": w_out[0],
              "w_up": w_up[0], "w_down": w_down[0], "w_ple_gate": w_ple_gate[0], "w_ple_proj": w_ple_proj[0]}
    params = {"w_in": w_in, "w_branch_na": w_branch_na, "w_branch_dil": w_branch_dil, "w_out": w_out, "w_up": w_up,
              "w_down": w_down, "w_ple_gate": w_ple_gate, "w_ple_proj": w_ple_proj,
              "m_w_in": m_w_in, "m_w_branch_na": m_w_branch_na, "m_w_branch_dil": m_w_branch_dil, "m_w_out": m_w_out,
              "m_w_up": m_w_up, "m_w_down": m_w_down, "m_w_ple_gate": m_w_ple_gate, "m_w_ple_proj": m_w_ple_proj,
              "v_w_in": v_w_in, "v_w_branch_na": v_w_branch_na, "v_w_branch_dil": v_w_branch_dil, "v_w_out": v_w_out,
              "v_w_up": v_w_up, "v_w_down": v_w_down, "v_w_ple_gate": v_w_ple_gate, "v_w_ple_proj": v_w_ple_proj}

    xs, ps, tgt = x[0], p[0, 0], loss_target[0]
    T = xs.shape[0]
    TM = 512
    gm, gl, gp, gf = g_mix, g_mlp, g_ple, g_final.reshape(1, D_MODEL)

    across = _gather_halves_start(shards["w_in"].astype(BF), "in")
    a = _rowwise("norm_mix", lambda h, g: h * _rms(h) * g, T, TM, [_row(xs, TM), _full(gm)], [(D_MODEL, BF)],
                 after=(across.token,))
    cos2, sin_signed = _rope_tables(positions[0])
    tab = _na_bias_table(rpb[0])
    w_in_shard, w_in_relay = _gather_halves_relay(across, a, "in")
    w_in_all = _gather_halves_finish(w_in_shard, w_in_relay, w_in_relay.token, "in")
    w_in_full, w_gates = _assemble_w_in(w_in_all, 256)
    W = {"w_in": w_in_full}
    mix_flight = _gather_start(jnp.concatenate([_pack_rows(shards[n].astype(BF)) for n in GATHER_MIX], axis=0), "mix",
                               after=(w_in_all,))
    mlp_across = _gather_halves_start(jnp.concatenate([_pack_rows(shards[n].astype(BF)) for n in GATHER_MLP], axis=0), "mlp",
                                      after=(mix_flight.token,))

    n3 = 3 * NA_WIDTH
    qkv = _mm("in_na", a, W["w_in"], "nn", 1024, 768, 1024, [BF], after=(mlp_across.token,),
              b_view=(n3, (D_MODEL, 768), lambda j, k: (k, j)))
    z_dil = _mm("in_dil", a, W["w_in"], "nn", 1024, 768, 1024, [F32], after=(mlp_across.token,),
                b_view=(3 * DIL_WIDTH, (D_MODEL, 768), lambda j, k: (k, n3 // 768 + j)))
    z_gates = _mm("in_gates", a, w_gates, "nn", 1024,1024, 1024, [BF], after=(mlp_across.token,))

    dil_ops = _qkv_prep(z_dil, cos2, sin_signed, TM)
    y_na = _na_fwd(qkv, tab)
    band = [_band_fwd(*dil_ops[g], g) for g in range(len(DIL_GROUPS))]
    y_dil, w_grp, o_nat = _dil_merge_fwd([b[0] for b in band], [b[1] for b in band], T, TM)

    W.update(_unpack_weights(_gather_wait(mix_flight, y_dil, "mix"), GATHER_MIX))
    mlp_shard, mlp_relay = _gather_halves_relay(mlp_across, y_dil, "mlp")
    u_na = _mm("branch_na", y_na, W["w_branch_na"], "nn", 1024,1024, 512, [BF], after=(mlp_relay.token,))
    u_dil = _mm("branch_dil", y_dil, W["w_branch_dil"], "nn", 1024,1024, 256, [BF])
    mixed = _rowwise(
        "gate_mix", lambda gn, gd, un, ud: _sigmoid(gn.astype(F32)) * un.astype(F32) + _sigmoid(gd.astype(F32)) * ud.astype(F32), T, TM,
        [_row(z_gates, TM, 0, D_MODEL), _row(z_gates, TM, 1, D_MODEL), _row(u_na, TM), _row(u_dil, TM)], [(D_MODEL, BF)])
    def add_norm(d, h, g):
        h = h + d
        return h, h * _rms(h) * g

    h1, cn = _mm("out_proj", mixed, W["w_out"], "nn", 512, 1024, 1024, [F32, BF], epilogue=add_norm, extras=(xs,), consts=(gl,))
    mlp_all = _gather_halves_finish(mlp_shard, mlp_relay, cn, "mlp")
    W.update({n: t for n, t in _unpack_weights(mlp_all, GATHER_MLP).items() if n.startswith("w_ple")})
    chip_block = (None, D_MODEL, PACK_W)
    up, act = _mm("mlp_up", cn, mlp_all, "nn", 1024,1024, 1024, [BF, BF],
                  epilogue=lambda acc: (acc, jnp.square(jnp.maximum(acc, 0.0))), b_view=(D_FF, chip_block, lambda j, k: (j, 0, 0)))
    h2, en = _mm("mlp_down", act, mlp_all, "nn", 1024, 1024, 1024, [F32, BF], epilogue=add_norm, extras=(h1,), consts=(gp,),
                 b_view=(D_MODEL, chip_block, lambda j, k: (k, 1, 0)))
    pp = _mm("ple_proj", ps, W["w_ple_proj"], "nn", 1024,1024, 256, [F32])

    def head(gtt, h2t, ppt, tg, g):
        sg = _sigmoid(gtt)
        h3 = h2t + sg * ppt
        yo = h3 * _rms(h3) * g
        diff = yo - tg
        loss = 0.5 * jnp.sum(jnp.mean(jnp.square(diff), axis=-1, keepdims=True), axis=0, keepdims=True)
        dh3, dg = _rms_bwd(diff * (1.0 / D_MODEL), h3, g)
        return dh3, dh3 * ppt * sg * (1.0 - sg), dh3 * sg, jnp.broadcast_to(loss, (1, 128)), dg

    dh3, d_gt, d_pp, loss_part, dg_final = _mm(
        "ple_gate_loss_head", en, W["w_ple_gate"], "nn", 512, 1024, 1024, [F32, BF, BF], epilogue=head,
        extras=(h2, pp, tgt), consts=(gf,), sums=[128, D_MODEL])

    early_shapes = {n: shards[n].shape for n in REDUCE_EARLY}
    early_rows = sum(r * c for r, c in early_shapes.values()) // PACK_W
    shard_rows = D_MODEL // N_CHIPS
    early_buf = _mm("g_ple_gate", en, d_gt, "tn", 1024, 1024, 1024, [F32],
                    into=(jax.ShapeDtypeStruct((N_CHIPS, early_rows, PACK_W), F32), (N_CHIPS, shard_rows, PACK_W),
                          lambda i, j: (0, 2 * D_MODEL // shard_rows, 0)))
    g_ple_proj = _mm("g_ple_proj", ps, d_pp, "tn", 256, 1024, 1024,[F32])

    def add_norm_bwd(dn, dh_out, h, g):
        dh, dg = _rms_bwd(dn, h, g)
        dh = dh_out + dh
        return dh, dh, dg

    dh2, dh2_b, dg_ple = _mm("d_ple_gate", d_gt, W["w_ple_gate"], "nt", 512, 1024, 1024, [F32, BF],
                             epilogue=add_norm_bwd, extras=(dh3, h2), consts=(gp,), sums=[D_MODEL])
    d_up = _mm("d_mlp_down", dh2_b, mlp_all, "nt", 1024,1024, 1024, [BF], b_view=(D_FF, chip_block, lambda j, k: (j, 1, 0)),
               epilogue=lambda acc, u: (acc * (2.0 * jnp.maximum(u.astype(F32), 0.0)),), extras=(up,))
    early_buf = _mm("g_mlp_down", act, dh2_b, "tn", 1024, 1024, 1024,[F32],
                    into=(early_buf, (None, D_MODEL, PACK_W), lambda i, j: (i, 1, 0)))
    early_buf = _mm("g_mlp_up", cn, d_up, "tn", 1024, 1024, 1024,[F32],
                    into=(early_buf, (None, D_MODEL, PACK_W), lambda i, j: (j, 0, 0)))
    dh1, dh1_b, dg_mlp = _mm("d_mlp_up", d_up, mlp_all, "nt", 1024, 1024, 1024, [F32, BF], epilogue=add_norm_bwd,
                             b_view=(D_MODEL, chip_block, lambda j, k: (k, 0, 0)),
                             extras=(dh2, h1), consts=(gl,), sums=[D_MODEL])
    d_mixed = _mm("d_out_proj", dh1_b, W["w_out"], "nt", 1024,1024, 1024, [F32])
    early_buf = _mm("g_out_proj", mixed, dh1_b, "tn", 1024, 1024, 1024, [F32],
                    into=(early_buf, (N_CHIPS, shard_rows, PACK_W), lambda i, j: (0, 2 * D_MODEL // shard_rows + 1, 0)))

    def gate_bwd(dm, gn, gd, un, ud):
        gn, gd, un, ud = (t.astype(F32) for t in (gn, gd, un, ud))
        sn, sd = _sigmoid(gn), _sigmoid(gd)
        return jnp.concatenate([dm * un * sn * (1.0 - sn), dm * ud * sd * (1.0 - sd)], axis=1), dm * sn, dm * sd

    dz_gates, d_u_na, d_u_dil = _rowwise(
        "gate_mix_bwd", gate_bwd, T, TM,
        [_row(d_mixed, TM), _row(z_gates, TM, 0, D_MODEL), _row(z_gates, TM, 1, D_MODEL), _row(u_na, TM), _row(u_dil, TM)],
        [(2 * D_MODEL, BF), (D_MODEL, BF), (D_MODEL, BF)])
    g_branch_na = _mm("g_branch_na", y_na, d_u_na, "tn", 1024, 1024, 1024,[F32])
    g_branch_dil = _mm("g_branch_dil", y_dil, d_u_dil, "tn", 256, 1024, 1024,[F32])
    small_rows = [jnp.concatenate([_pack_rows(g[:, s * shard_rows:(s + 1) * shard_rows]) for g in (g_ple_proj, g_branch_na, g_branch_dil)],
                                  axis=0) for s in range(N_CHIPS)]
    early_buf = lax.dynamic_update_slice(early_buf, jnp.stack(small_rows), (0, 2 * D_MODEL + 2 * shard_rows, 0))
    early_tm = early_rows // 4
    swap_flight = _swap_start(early_buf, "early")
    d_y_na = _mm("d_branch_na", d_u_na, W["w_branch_na"], "nt", 1024,512, 1024, [BF], after=(swap_flight.token,))
    d_y_dil = _mm("d_branch_dil", d_u_dil, W["w_branch_dil"], "nt", 1024,256, 1024, [F32])

    dqa, dka, dva, dtab = _na_bwd(qkv, tab, d_y_na)
    early_g, early_got = _swap_wait(swap_flight, dqa, "early")
    early_pair, early_pair_b = _pair_sum(early_g, early_got, early_tm)
    scatter_flight = _scatter_start(early_pair_b, "early")

    do_res, dlse_res = _dil_merge_bwd(d_y_dil, o_nat, w_grp, TM, after=(scatter_flight.token,))
    d_dil = [_band_bwd(*dil_ops[g], do_res[g], dlse_res[g], g) for g in range(len(DIL_GROUPS))]

    dz_qkv = _qkv_unprep((dqa, dka, dva), d_dil, cos2, sin_signed, TM)
    g_in_parts = [_mm("g_in_qkv", a, dz_qkv, "tn", 1024, 1280, 1024,[F32]), _mm("g_in_gates", a, dz_gates, "tn", 1024, 1024, 1024,[F32])]
    early_mine = _chip_sum(early_pair, _scatter_wait(scatter_flight, g_in_parts[1], "early"), early_tm)
    join_flight = _join_start(early_mine)
    in_cols = shards["w_in"].shape[1]

    def owner_columns(s):
        lo, hi, split = s * in_cols, (s + 1) * in_cols, g_in_parts[0].shape[1]
        pieces = [g_in_parts[0][:, lo:min(hi, split)]] if lo < split else []
        pieces += [g_in_parts[1][:, max(lo, split) - split:hi - split]] if hi > split else []
        return pieces[0] if len(pieces) == 1 else jnp.concatenate(pieces, axis=1)

    late_tm = in_cols // 4
    late_swap = _swap_start(jnp.stack([owner_columns(s).T for s in range(N_CHIPS)]), "late")
    d_a = _mm("d_in_qkv", dz_qkv, W["w_in"], "nt", 1024,1024, 1280, [F32], after=(late_swap.token, join_flight.token),
              b_view=(D_MODEL, (D_MODEL, 1280), lambda j, k: (j, k)))
    late_g, late_got = _swap_wait(late_swap, d_a, "late")
    late_pair, late_pair_b = _pair_sum(late_g, late_got, late_tm)
    late_scatter = _scatter_start(late_pair_b, "late")
    d_rpb = _na_rpb_grad(dtab, after=(late_scatter.token,))[:, :2 * NA_WIN_ROWS - 1, :2 * NA_WIN_COLS - 1]
    def first_bwd(dn_gates, dn_qkv, dh_out, h, g):
        dh, dg = _rms_bwd(dn_gates + dn_qkv, h, g)
        return dh_out + dh, dg

    grad_x, dg_mix = _mm("d_in_gates", dz_gates, w_gates, "nt", 512, 1024, 1024, [F32], epilogue=first_bwd,
                         extras=(d_a, dh1, xs), consts=(gm,), sums=[D_MODEL], after=(late_scatter.token,))
    early_shard = _join_wait(join_flight, grad_x)

    n_rpb = rpb.size
    rpb_rows = 4
    small = jnp.concatenate([
        dg_mix, dg_mlp, dg_ple, dg_final,
        jnp.pad(d_rpb.reshape(-1), (0, rpb_rows * D_MODEL - n_rpb)).reshape(rpb_rows, D_MODEL),
        jnp.pad(loss_part, ((0, 0), (0, D_MODEL - loss_part.shape[1]))),
        jnp.zeros((SMALL_ROWS - 5 - rpb_rows, D_MODEL), F32)], axis=0)
    out = {"grad": {}, "delta": {}, "new_m": {}, "new_v": {}}

    def update(n, g, row0):
        res = _adamw("adamw_" + n, g, row0, params[n], params["m_" + n], params["v_" + n])
        for kind, t in zip(("grad", "delta", "new_m", "new_v"), res, strict=True):
            out[kind][n] = t

    row0 = 0
    for n in REDUCE_EARLY:
        rows, cols = early_shapes[n]
        n_rows = rows * cols // PACK_W
        if cols == PACK_W:
            update(n, early_shard, row0)
        else:
            update(n, early_shard[row0:row0 + n_rows].reshape(rows, cols), 0)
        row0 += n_rows
    late_others = _scatter_wait(late_scatter, out["new_v"][REDUCE_EARLY[-1]], "late")
    late_mine = _chip_sum(late_pair, late_others, late_tm)
    small = _allreduce_small(small, after=(late_mine,))
    res = _adamw("adamw_w_in", _join_halves(late_mine), 0, *[jnp.swapaxes(params[n], 1, 2) for n in ("w_in", "m_w_in", "v_w_in")])
    for kind, t in zip(("grad", "delta", "new_m", "new_v"), res, strict=True):
        out[kind]["w_in"] = jnp.swapaxes(t, 1, 2)
    loss = small[4 + rpb_rows, 0]

    def small_pack(a0, a1, a2, a3, r):
        return jnp.concatenate([a0.reshape(1, -1), a1.reshape(1, -1), a2.reshape(1, -1), a3.reshape(1, -1),
                                jnp.pad(r.reshape(-1), (0, rpb_rows * D_MODEL - n_rpb)).reshape(rpb_rows, D_MODEL)], axis=0)

    small_res = _adamw("adamw_small", small, 0, small_pack(g_mix, g_mlp, g_ple, g_final, rpb)[None],
                       small_pack(m_g_mix, m_g_mlp, m_g_ple, m_g_final, m_rpb)[None],
                       small_pack(v_g_mix, v_g_mlp, v_g_ple, v_g_final, v_rpb)[None])

    def small_unpack(t):
        return {"g_mix": t[0].reshape(g_mix.shape), "g_mlp": t[1].reshape(g_mlp.shape), "g_ple": t[2].reshape(g_ple.shape),
                "g_final": t[3].reshape(g_final.shape), "rpb": t[4:].reshape(-1)[:n_rpb].reshape(rpb.shape)}

    for kind, t in zip(("grad", "delta", "new_m", "new_v"), small_res, strict=True):
        out[kind].update(small_unpack(t[0]))

    order = ["g_mix", "w_in", "rpb", "w_branch_na", "w_branch_dil", "w_out", "g_mlp", "w_up", "w_down", "g_ple",
             "w_ple_gate", "w_ple_proj", "g_final"]
    return (loss, grad_x[None], *[out["grad"][n] for n in order], *[out["delta"][n] for n in order],
            *[out["new_m"][n] for n in order], *[out["new_v"][n] for n in order])
```

```python
import functools
from typing import NamedTuple

import jax
import jax.numpy as jnp
from jax import lax
from jax.experimental import pallas as pl
from jax.experimental.pallas import tpu as pltpu

BF = jnp.bfloat16
F32 = jnp.float32
MESH = pl.DeviceIdType.MESH
ANY = pl.BlockSpec(memory_space=pl.ANY)

V7X_VMEM_BYTES = 64 * 1024 * 1024
VMEM_LIMIT = V7X_VMEM_BYTES - 16 * 1024 * 1024

D_MODEL = 1024
HEAD_DIM = 64
GRID_W = 64
NA_HEADS = 8
NA_WIN_ROWS = 8
NA_WIN_COLS = 16
NA_WIDTH = NA_HEADS * HEAD_DIM
DIL_GROUPS = ((128, 1), (512, 4), (2048, 16))
DIL_HPG = 4
DIL_HEADS = DIL_HPG * len(DIL_GROUPS)
DIL_WIDTH = DIL_HEADS * HEAD_DIM
DIL_OUT_WIDTH = DIL_HPG * HEAD_DIM
DIL_RADIUS = 64
QKV_WIDTH = 3 * NA_WIDTH + 3 * DIL_WIDTH
D_FF = 4 * D_MODEL
PLE_DIM = 256
ROPE_THETA = 10000.0
RMS_EPS = 1e-6
NEG_INF = -1e30
Q_SCALE = HEAD_DIM ** -0.5

ADAM_LR = 0.001
ADAM_B1 = 0.9
ADAM_B2 = 0.999
ADAM_EPS = 1e-08
ADAM_WD = 0.01
ADAM_STEP = 10

N_CHIPS = 4
N_DEV = 8
PACK_W = 1024
GATHER_MIX = ("w_branch_na", "w_branch_dil", "w_out")
GATHER_MLP = ("w_up", "w_down", "w_ple_gate", "w_ple_proj")
REDUCE_EARLY = ("w_up", "w_down", "w_ple_gate", "w_out", "w_ple_proj", "w_branch_na", "w_branch_dil")
SMALL_ROWS = 16


def _cparams(sem=None):
    return pltpu.CompilerParams(dimension_semantics=sem, vmem_limit_bytes=VMEM_LIMIT)


def _mm(name, a, b, mode, tm, tn, tk, out_dtypes, epilogue=None, extras=(), consts=(), sums=(), after=(), into=None,
        b_view=None):
    if mode == "nn":
        (M, K), N = a.shape, b.shape[1]
    elif mode == "nt":
        (M, K), N = a.shape, b.shape[0]
    else:
        (K, M), N = a.shape, b.shape[1]
    if b_view is not None:
        N = b_view[0]
    tm, tn, tk = min(tm, M), min(tn, N), min(tk, K)
    assert M % tm == 0 and N % tn == 0 and K % tk == 0, (name, M, N, K, tm, tn, tk)
    if mode == "nn":
        a_spec = pl.BlockSpec((tm, tk), lambda i, j, k: (i, k))
        b_spec = pl.BlockSpec((tk, tn), lambda i, j, k: (k, j))
        dims = (((1,), (0,)), ((), ()))
    elif mode == "nt":
        a_spec = pl.BlockSpec((tm, tk), lambda i, j, k: (i, k))
        b_spec = pl.BlockSpec((tn, tk), lambda i, j, k: (j, k))
        dims = (((1,), (1,)), ((), ()))
    else:
        a_spec = pl.BlockSpec((tk, tm), lambda i, j, k: (k, i))
        b_spec = pl.BlockSpec((tk, tn), lambda i, j, k: (k, j))
        dims = (((0,), (0,)), ((), ()))
    if b_view is not None:
        b_spec = pl.BlockSpec(b_view[1], lambda i, j, k: b_view[2](j, k))
    nk = K // tk
    n_extra, n_const, n_out, n_sum = len(extras), len(consts), len(out_dtypes), len(sums)
    tile = pl.BlockSpec((tm, tn), lambda i, j, k: (i, j))
    assert not sums or tn == N, "row sums need whole rows in a tile"

    n_after = len(after)

    def body(a_ref, b_ref, *rest):
        extra_refs, rest = rest[:n_extra + n_const], rest[n_extra + n_const + n_after:]
        out_refs, sum_refs, acc = rest[:n_out], rest[n_out:n_out + n_sum], rest[-1]
        i, k = pl.program_id(0), pl.program_id(2)
        def product():
            return lax.dot_general(a_ref[...].astype(BF), b_ref[...].astype(BF), dims, preferred_element_type=F32)

        if nk > 1:
            @pl.when(k == 0)
            def _():
                acc[...] = jnp.zeros_like(acc)

            acc[...] += product()

        @pl.when(k == nk - 1)
        def _():
            total = product() if nk == 1 else acc[...]
            outs = (total,) if epilogue is None else epilogue(total, *[e[...] for e in extra_refs])
            for o_ref, val in zip(out_refs, outs[:n_out], strict=True):
                o_ref[...] = val.astype(o_ref.dtype).reshape(o_ref.shape)
            for s_ref, val in zip(sum_refs, outs[n_out:], strict=True):
                @pl.when(i == 0)
                def _():
                    s_ref[...] = val

                @pl.when(i != 0)
                def _():
                    s_ref[...] += val

    out_specs = [tile] * n_out + [pl.BlockSpec((1, c), lambda i, j, k: (0, 0)) for c in sums]
    out_shape = [jax.ShapeDtypeStruct((M, N), dt) for dt in out_dtypes] + [jax.ShapeDtypeStruct((1, c), F32) for c in sums]
    operands, aliases = [a, b, *extras, *consts, *after], {}
    in_specs = ([a_spec, b_spec] + [tile] * n_extra
                + [pl.BlockSpec(c.shape, functools.partial(lambda nd, i, j, k: (0,) * nd, c.ndim)) for c in consts] + [ANY] * n_after)
    if into is not None:
        assert n_out == 1
        target, block, index = into
        out_specs = [pl.BlockSpec(block, lambda i, j, k: index(i, j))]
        out_shape = [jax.ShapeDtypeStruct(target.shape, target.dtype)]
        if not isinstance(target, jax.ShapeDtypeStruct):
            aliases = {len(operands): 0}
            operands.append(target)
            in_specs.append(ANY)
            n_after += 1

    outs = pl.pallas_call(
        body, name=name, grid=(M // tm, N // tn, nk),
        in_specs=in_specs, out_specs=out_specs, out_shape=out_shape,
        scratch_shapes=[pltpu.VMEM((tm, tn) if nk > 1 else (8, 128), F32)], input_output_aliases=aliases,
        compiler_params=_cparams(("arbitrary",) * 3 if sums else ("parallel", "parallel", "arbitrary")),
    )(*operands)
    return outs[0] if len(outs) == 1 else outs


def _row(arr, tm, col_block=None, width=None):
    width = arr.shape[1] if width is None else width
    cb = 0 if col_block is None else col_block
    return arr, pl.BlockSpec((tm, width), lambda i: (i, cb))


def _full(arr):
    nd = arr.ndim
    return arr, pl.BlockSpec(arr.shape, lambda i: (0,) * nd)


def _rowwise(name, body, T, tm, ins, outs, sums=(), after=()):
    n_in, n_out, n_sum, n_after = len(ins), len(outs), len(sums), len(after)

    def kern(*refs):
        in_refs, refs = refs[:n_in], refs[n_in + n_after:]
        out_refs, sum_refs = refs[:n_out], refs[n_out:]
        res = body(*[r[...] for r in in_refs])
        res = res if isinstance(res, tuple) else (res,)
        for o_ref, val in zip(out_refs, res[:n_out], strict=True):
            o_ref[...] = val.astype(o_ref.dtype)
        if n_sum:
            @pl.when(pl.program_id(0) == 0)
            def _():
                for s_ref in sum_refs:
                    s_ref[...] = jnp.zeros_like(s_ref)

            for s_ref, val in zip(sum_refs, res[n_out:], strict=True):
                s_ref[...] += val

    res = pl.pallas_call(
        kern, name=name, grid=(T // tm,),
        in_specs=[spec for _, spec in ins] + [ANY] * n_after,
        out_specs=[pl.BlockSpec((tm, c), lambda i: (i, 0)) for c, _ in outs]
        + [pl.BlockSpec((1, c), lambda i: (0, 0)) for c in sums],
        out_shape=[jax.ShapeDtypeStruct((T, c), dt) for c, dt in outs]
        + [jax.ShapeDtypeStruct((1, c), F32) for c in sums],
        compiler_params=_cparams(("arbitrary",)),
    )(*[a for a, _ in ins], *after)
    return res[0] if len(res) == 1 else res


def _sigmoid(x):
    return 1.0 / (1.0 + jnp.exp(-x))


def _rms(h):
    return lax.rsqrt(jnp.mean(h * h, axis=-1, keepdims=True) + RMS_EPS)


def _rms_bwd(dy, h, g):
    r = _rms(h)
    n = h * r
    dn = dy * g
    dh = r * (dn - n * jnp.mean(dn * n, axis=-1, keepdims=True))
    return dh, jnp.sum(dy * n, axis=0, keepdims=True)


def _rope(x, cos2, sin_signed):
    lane = lax.broadcasted_iota(jnp.int32, x.shape, 1)
    swapped = jnp.where((lane % HEAD_DIM) < HEAD_DIM // 2, pltpu.roll(x, 128 - HEAD_DIM // 2, 1), pltpu.roll(x, HEAD_DIM // 2, 1))
    return x * cos2 + swapped * sin_signed


NA_KEYS = NA_WIN_ROWS * GRID_W
NA_BASES = 8


def _na_row_geometry(r, rows):
    first = jnp.clip(r - NA_WIN_ROWS // 2, 0, rows - NA_WIN_ROWS)
    base = first - r + (NA_WIN_ROWS - 1)
    return pl.multiple_of(first * GRID_W, GRID_W), base


NA_ROWS_PER_STEP = 16
NA_BWD_ROWS_PER_STEP = 8


def _softmax_rows(s):
    p = jnp.exp(s - jnp.max(s, axis=-1, keepdims=True))
    return p / jnp.sum(p, axis=-1, keepdims=True)


def _split_pair(t):
    first = lax.broadcasted_iota(jnp.int32, t.shape, 1) < HEAD_DIM
    zero = jnp.zeros_like(t)
    return jnp.where(first, t, zero), jnp.where(first, zero, t)


def _join_pair(a, b):
    return jnp.where(lax.broadcasted_iota(jnp.int32, a.shape, 1) < HEAD_DIM, a, b)


_NT = (((1,), (1,)), ((), ()))
_TN = (((0,), (0,)), ((), ()))


def _na_fwd(qkv, tab):
    T = qkv.shape[0]
    rows = T // GRID_W
    n_pairs = NA_WIDTH // 128

    def body(q_ref, k_ref, v_ref, tab_ref, y_ref):
        def step(it, carry):
            geo = [_na_row_geometry(it * NA_ROWS_PER_STEP + u, rows) for u in range(NA_ROWS_PER_STEP)]
            q0s = [pl.multiple_of((it * NA_ROWS_PER_STEP + u) * GRID_W, GRID_W) for u in range(NA_ROWS_PER_STEP)]
            ss = [lax.dot_general(jnp.concatenate(_split_pair(q_ref[pl.ds(q0, GRID_W), :] * Q_SCALE), axis=0),
                                  k_ref[pl.ds(k0, NA_KEYS), :], _NT, preferred_element_type=F32)
                  for q0, (k0, _) in zip(q0s, geo)]
            ps = [_softmax_rows(s + jnp.concatenate([tab_ref[0, base], tab_ref[1, base]], axis=0)) for s, (_, base) in zip(ss, geo)]
            ys = [jnp.dot(p.astype(BF), v_ref[pl.ds(k0, NA_KEYS), :], preferred_element_type=F32) for p, (k0, _) in zip(ps, geo)]
            for q0, y2 in zip(q0s, ys):
                y_ref[pl.ds(q0, GRID_W), :] = _join_pair(y2[:GRID_W], y2[GRID_W:]).astype(y_ref.dtype)
            return carry

        lax.fori_loop(0, rows // NA_ROWS_PER_STEP, step, 0)

    def cols(first):
        return pl.BlockSpec((T, 128), lambda j: (0, first + j))

    return pl.pallas_call(
        body, name="na_fwd", grid=(n_pairs,),
        in_specs=[cols(0), cols(n_pairs), cols(2 * n_pairs), pl.BlockSpec((2, NA_BASES, GRID_W, NA_KEYS), lambda j: (j, 0, 0, 0))],
        out_specs=cols(0), out_shape=jax.ShapeDtypeStruct((T, NA_WIDTH), BF),
        compiler_params=_cparams(("parallel",)),
    )(qkv, qkv, qkv, tab)


def _na_bwd(qkv, tab, do):
    T = qkv.shape[0]
    rows = T // GRID_W
    n_pairs = NA_WIDTH // 128

    def body(q_ref, k_ref, v_ref, tab_ref, do_ref, dq_ref, dk_out, dv_out, dtab_ref, dk_ref, dv_ref):
        dk_ref[...] = jnp.zeros_like(dk_ref)
        dv_ref[...] = jnp.zeros_like(dv_ref)
        dtab_ref[...] = jnp.zeros_like(dtab_ref)

        def step(it, carry):
            U = NA_BWD_ROWS_PER_STEP
            geo = [_na_row_geometry(it * U + u, rows) for u in range(U)]
            q0s = [pl.multiple_of((it * U + u) * GRID_W, GRID_W) for u in range(U)]
            q2s = [jnp.concatenate(_split_pair(q_ref[pl.ds(q0, GRID_W), :] * Q_SCALE), axis=0) for q0 in q0s]
            do2s = [jnp.concatenate(_split_pair(do_ref[pl.ds(q0, GRID_W), :]), axis=0) for q0 in q0s]
            ss = [lax.dot_general(q2, k_ref[pl.ds(k0, NA_KEYS), :], _NT, preferred_element_type=F32) for q2, (k0, _) in zip(q2s, geo)]
            dps = [lax.dot_general(do2, v_ref[pl.ds(k0, NA_KEYS), :], _NT, preferred_element_type=F32) for do2, (k0, _) in zip(do2s, geo)]
            ps = [_softmax_rows(s + jnp.concatenate([tab_ref[0, base], tab_ref[1, base]], axis=0)) for s, (_, base) in zip(ss, geo)]
            dss = [p * (dp - jnp.sum(dp * p, axis=-1, keepdims=True)) for p, dp in zip(ps, dps)]
            dvs = [lax.dot_general(p.astype(BF), do2, _TN, preferred_element_type=F32) for p, do2 in zip(ps, do2s)]
            dsbs = [ds.astype(BF) for ds in dss]
            dqs = [jnp.dot(dsb, k_ref[pl.ds(k0, NA_KEYS), :], preferred_element_type=F32) for dsb, (k0, _) in zip(dsbs, geo)]
            dks = [lax.dot_general(dsb, q2, _TN, preferred_element_type=F32) for dsb, q2 in zip(dsbs, q2s)]
            for u in range(U):
                k0, base = geo[u]
                dtab_ref[0, base] += dss[u][:GRID_W]
                dtab_ref[1, base] += dss[u][GRID_W:]
                dq_ref[pl.ds(q0s[u], GRID_W), :] = (_join_pair(dqs[u][:GRID_W], dqs[u][GRID_W:]) * Q_SCALE).astype(dq_ref.dtype)
                dk_ref[pl.ds(k0, NA_KEYS), :] += dks[u]
                dv_ref[pl.ds(k0, NA_KEYS), :] += dvs[u]
            return carry

        lax.fori_loop(0, rows // NA_BWD_ROWS_PER_STEP, step, 0)
        dk_out[...] = dk_ref[...].astype(dk_out.dtype)
        dv_out[...] = dv_ref[...].astype(dv_out.dtype)

    def cols(first):
        return pl.BlockSpec((T, 128), lambda j: (0, first + j))

    tabs = pl.BlockSpec((2, NA_BASES, GRID_W, NA_KEYS), lambda j: (j, 0, 0, 0))
    wide = jax.ShapeDtypeStruct((T, NA_WIDTH), BF)
    return pl.pallas_call(
        body, name="na_bwd", grid=(n_pairs,),
        in_specs=[cols(0), cols(n_pairs), cols(2 * n_pairs), tabs, cols(0)],
        out_specs=[cols(0), cols(0), cols(0), tabs],
        out_shape=[wide, wide, wide, jax.ShapeDtypeStruct((NA_HEADS, NA_BASES, GRID_W, NA_KEYS), F32)],
        scratch_shapes=[pltpu.VMEM((T, 128), F32), pltpu.VMEM((T, 128), F32)],
        compiler_params=_cparams(("parallel",)),
    )(qkv, qkv, qkv, tab, do)


def _na_bias_table(rpb):
    H, n_rows, n_cols = rpb.shape

    def body(r_ref, tab_ref):
        q = lax.broadcasted_iota(jnp.int32, (GRID_W, 128), 0)
        kc = lax.broadcasted_iota(jnp.int32, (GRID_W, 128), 1)
        first = jnp.clip(q - NA_WIN_COLS // 2, 0, GRID_W - NA_WIN_COLS)
        valid = (kc >= first) & (kc < first + NA_WIN_COLS)
        toeplitz = []
        for ro in range(n_rows):
            row = jnp.broadcast_to(r_ref[pl.ds(ro, 1), :], (GRID_W, 128))
            shifted = pltpu.roll(pltpu.roll(row, 128 - (NA_WIN_COLS - 1), 1), 0, 1, stride=1, stride_axis=0)
            toeplitz.append(jnp.where(valid, shifted, NEG_INF))
        for base in range(NA_BASES):
            for j in range(NA_WIN_ROWS // 2):
                even, odd = toeplitz[base + 2 * j], toeplitz[base + 2 * j + 1]
                tab_ref[base, :, pl.ds(j * 128, 128)] = jnp.where(kc < GRID_W, even, pltpu.roll(odd, GRID_W, 1))

    padded = jnp.pad(rpb, ((0, 0), (0, 16 - n_rows), (0, 128 - n_cols)))
    return pl.pallas_call(
        body, name="na_bias_table", grid=(H,),
        in_specs=[pl.BlockSpec((None, 16, 128), lambda h: (h, 0, 0))],
        out_specs=pl.BlockSpec((None, NA_BASES, GRID_W, NA_KEYS), lambda h: (h, 0, 0, 0)),
        out_shape=jax.ShapeDtypeStruct((H, NA_BASES, GRID_W, NA_KEYS), F32),
        compiler_params=_cparams(("parallel",)),
    )(padded)


def _na_rpb_grad(dtab, after=()):
    H = dtab.shape[0]
    n_rows = 2 * NA_WIN_ROWS - 1
    n_cols = 2 * NA_WIN_COLS - 1

    def body(d_ref, *rest):
        o_ref = rest[-1]
        lane = lax.broadcasted_iota(jnp.int32, (GRID_W, 128), 1)
        low = lane < GRID_W
        flip = (lax.broadcasted_iota(jnp.int32, (GRID_W, GRID_W), 0) + lax.broadcasted_iota(jnp.int32, (GRID_W, GRID_W), 1)
                == GRID_W - 1).astype(BF)

        def reverse_rows(t):
            out = jnp.zeros_like(t)
            for _ in range(3):
                piece = t.astype(BF)
                out = out + jnp.dot(flip, piece, preferred_element_type=F32)
                t = t - piece.astype(F32)
            return out

        out_rows = []
        for ro in range(n_rows):
            acc = jnp.zeros((GRID_W, 128), F32)
            for base in range(NA_BASES):
                i = ro - base
                if not 0 <= i < NA_WIN_ROWS:
                    continue
                pair = d_ref[base, :, pl.ds((i // 2) * 128, 128)]
                if i % 2:
                    pair = pltpu.roll(pair, GRID_W, 1)
                acc = acc + jnp.where(low, pair, 0.0)
            skew = pltpu.roll(reverse_rows(acc), 0, 1, stride=1, stride_axis=0)
            diag = jnp.sum(skew, axis=0, keepdims=True)
            out_rows.append(pltpu.roll(jnp.broadcast_to(diag, (8, 128)), 128 - (GRID_W - NA_WIN_COLS), 1)[:1])
        out_rows.append(jnp.zeros((1, 128), F32))
        res = jnp.concatenate(out_rows, axis=0)
        o_ref[...] = jnp.where(lax.broadcasted_iota(jnp.int32, res.shape, 1) < n_cols, res, 0.0)

    return pl.pallas_call(
        body, name="na_rpb_grad", grid=(H,),
        in_specs=[pl.BlockSpec((None, NA_BASES, GRID_W, NA_KEYS), lambda h: (h, 0, 0, 0))] + [ANY] * len(after),
        out_specs=pl.BlockSpec((None, n_rows + 1, 128), lambda h: (h, 0, 0)),
        out_shape=jax.ShapeDtypeStruct((H, n_rows + 1, 128), F32),
        compiler_params=_cparams(("parallel",)),
    )(dtab, *after)


BAND_Q = 128
BAND_KEYS = BAND_Q + 2 * DIL_RADIUS


def _band_geometry(n, L):
    q0 = pl.multiple_of(n * BAND_Q, BAND_Q)
    k0 = pl.multiple_of(jnp.clip(q0 - DIL_RADIUS, 0, L - BAND_KEYS), DIL_RADIUS)
    qi = q0 + lax.broadcasted_iota(jnp.int32, (BAND_Q, BAND_KEYS), 0)
    kj = k0 + lax.broadcasted_iota(jnp.int32, (BAND_Q, BAND_KEYS), 1)
    return q0, k0, jnp.abs(qi - kj) <= DIL_RADIUS


DIL_PAIRS = DIL_OUT_WIDTH // 128


def _residue_shape(dil, T, dtype):
    return jax.ShapeDtypeStruct((DIL_PAIRS, dil, T // dil, 128), dtype)


def _residue_tile(dil, tm):
    return pl.BlockSpec((DIL_PAIRS, dil, tm // dil, 128), lambda i: (0, 0, i, 0))


def _to_natural(ref, scratch, dil, tm):
    tiles = []
    for pair in range(DIL_PAIRS):
        if dil == 1:
            tiles.append(ref[pair, 0].astype(F32))
            continue
        for r in range(dil):
            scratch[pl.ds(r, tm // dil, stride=dil), :] = ref[pair, r].astype(F32)
        tiles.append(scratch[...])
    return tiles


def _from_natural(tile, scratch, ref, pair, dil, tm):
    if dil == 1:
        ref[pair, 0] = tile.astype(ref.dtype)
        return
    scratch[...] = tile
    for r in range(dil):
        ref[pair, r] = scratch[pl.ds(r, tm // dil, stride=dil), :].astype(ref.dtype)


def _band_specs(group, T):
    dil = DIL_GROUPS[group][1]
    L = T // dil
    assert L % BAND_Q == 0 and L >= BAND_KEYS, (T, dil)
    per_residue = min(BAND_BLOCKS_PER_STEP, L // BAND_Q)
    residues = min(dil, BAND_BLOCKS_PER_STEP // per_residue)
    spec = pl.BlockSpec((None, residues, L, 128), lambda s: (s % DIL_PAIRS, s // DIL_PAIRS, 0, 0))
    return L, residues, per_residue, (dil // residues * DIL_PAIRS,), spec


BAND_BLOCKS_PER_STEP = 8


def _band_softmax(s, valid):
    s = jnp.where(valid, s, NEG_INF)
    m = jnp.max(s, axis=-1, keepdims=True)
    p = jnp.exp(s - m)
    l = jnp.sum(p, axis=-1, keepdims=True)
    return p / l, m + jnp.log(l)


def _band_fwd(q, k, v, group):
    T = q.shape[1] * q.shape[2]
    L, residues, U, grid, spec = _band_specs(group, T)

    def body(q_ref, k_ref, v_ref, o_ref, lse_ref):
        def step(it, carry):
            geo = [(r, *_band_geometry(it * U + u, L)) for r in range(residues) for u in range(U)]
            ss = [lax.dot_general(jnp.concatenate(_split_pair(q_ref[r, pl.ds(q0, BAND_Q), :]), axis=0),
                                  k_ref[r, pl.ds(k0, BAND_KEYS), :], _NT, preferred_element_type=F32) for r, q0, k0, _ in geo]
            pls = [_band_softmax(s, jnp.concatenate([valid, valid], axis=0)) for s, (_, _, _, valid) in zip(ss, geo)]
            os = [jnp.dot(p.astype(BF), v_ref[r, pl.ds(k0, BAND_KEYS), :], preferred_element_type=F32)
                  for (p, _), (r, _, k0, _) in zip(pls, geo)]
            for (r, q0, _, _), o2, (_, lse) in zip(geo, os, pls):
                o_ref[r, pl.ds(q0, BAND_Q), :] = _join_pair(o2[:BAND_Q], o2[BAND_Q:])
                lse2 = jnp.broadcast_to(lse, (2 * BAND_Q, 128))
                lse_ref[r, pl.ds(q0, BAND_Q), :] = _join_pair(lse2[:BAND_Q], lse2[BAND_Q:])
            return carry

        lax.fori_loop(0, L // (BAND_Q * U), step, 0)

    res = _residue_shape(DIL_GROUPS[group][1], T, F32)
    return pl.pallas_call(
        body, name=f"band_fwd_g{group}", grid=grid,
        in_specs=[spec] * 3, out_specs=[spec] * 2, out_shape=[res, res],
        compiler_params=_cparams(("parallel",)),
    )(q, k, v)


def _band_bwd(q, k, v, do, dlse, group):
    T = q.shape[1] * q.shape[2]
    L, residues, U, grid, spec = _band_specs(group, T)

    def body(q_ref, k_ref, v_ref, do_ref, dlse_ref, dq_ref, dk_ref, dv_ref):
        dk_ref[...] = jnp.zeros_like(dk_ref)
        dv_ref[...] = jnp.zeros_like(dv_ref)

        def step(it, carry):
            geo = [(r, *_band_geometry(it * U + u, L)) for r in range(residues) for u in range(U)]
            q2s = [jnp.concatenate(_split_pair(q_ref[r, pl.ds(q0, BAND_Q), :]), axis=0) for r, q0, _, _ in geo]
            do2s = [jnp.concatenate(_split_pair(do_ref[r, pl.ds(q0, BAND_Q), :]), axis=0) for r, q0, _, _ in geo]
            ss = [lax.dot_general(q2, k_ref[r, pl.ds(k0, BAND_KEYS), :], _NT, preferred_element_type=F32)
                  for q2, (r, _, k0, _) in zip(q2s, geo)]
            dps = [lax.dot_general(do2, v_ref[r, pl.ds(k0, BAND_KEYS), :], _NT, preferred_element_type=F32)
                   for do2, (r, _, k0, _) in zip(do2s, geo)]
            ps = [_band_softmax(s, jnp.concatenate([valid, valid], axis=0))[0] for s, (_, _, _, valid) in zip(ss, geo)]
            dss = []
            for p, dp, (r, q0, _, _) in zip(ps, dps, geo):
                dl = dlse_ref[r, pl.ds(q0, BAND_Q), :]
                dl2 = jnp.concatenate([dl[:, :1], dl[:, HEAD_DIM:HEAD_DIM + 1]], axis=0)
                dss.append(p * (dp - jnp.sum(dp * p, axis=-1, keepdims=True) + dl2))
            dvs = [lax.dot_general(p.astype(BF), do2, _TN, preferred_element_type=F32) for p, do2 in zip(ps, do2s)]
            dsbs = [ds.astype(BF) for ds in dss]
            dqs = [jnp.dot(dsb, k_ref[r, pl.ds(k0, BAND_KEYS), :], preferred_element_type=F32) for dsb, (r, _, k0, _) in zip(dsbs, geo)]
            dks = [lax.dot_general(dsb, q2, _TN, preferred_element_type=F32) for dsb, q2 in zip(dsbs, q2s)]
            for u, (r, q0, k0, _) in enumerate(geo):
                dq_ref[r, pl.ds(q0, BAND_Q), :] = _join_pair(dqs[u][:BAND_Q], dqs[u][BAND_Q:])
                dk_ref[r, pl.ds(k0, BAND_KEYS), :] += dks[u]
                dv_ref[r, pl.ds(k0, BAND_KEYS), :] += dvs[u]
            return carry

        lax.fori_loop(0, L // (BAND_Q * U), step, 0)

    res = _residue_shape(DIL_GROUPS[group][1], T, F32)
    return pl.pallas_call(
        body, name=f"band_bwd_g{group}", grid=grid,
        in_specs=[spec] * 5, out_specs=[spec] * 3, out_shape=[res] * 3,
        compiler_params=_cparams(("parallel",)),
    )(q, k, v, do, dlse)


def _head_sums(t):
    head = lax.broadcasted_iota(jnp.int32, t.shape, 1) // HEAD_DIM
    out = jnp.zeros_like(t)
    for h in range(t.shape[1] // HEAD_DIM):
        mine = head == h
        out = jnp.where(mine, jnp.sum(jnp.where(mine, t, 0.0), axis=-1, keepdims=True), out)
    return out


def _dil_merge_fwd(os, lses, T, tm):
    G = len(DIL_GROUPS)
    W = DIL_OUT_WIDTH
    dils = [d for _, d in DIL_GROUPS]

    def body(*refs):
        o_refs, lse_refs = refs[:G], refs[G:2 * G]
        y_ref, w_refs, on_refs, scratch = refs[2 * G], refs[2 * G + 1:3 * G + 1], refs[3 * G + 1:4 * G + 1], refs[-1]
        o = [jnp.concatenate(_to_natural(r, scratch, d, tm), axis=1) for r, d in zip(o_refs, dils)]
        ls = [jnp.concatenate(_to_natural(r, scratch, d, tm), axis=1) for r, d in zip(lse_refs, dils)]
        m = functools.reduce(jnp.maximum, ls)
        es = [jnp.exp(l - m) for l in ls]
        tot = functools.reduce(jnp.add, es)
        ws = [e / tot for e in es]
        y_ref[...] = functools.reduce(jnp.add, [w * t for w, t in zip(ws, o)]).astype(y_ref.dtype)
        for g in range(G):
            w_refs[g][...] = ws[g]
            on_refs[g][...] = o[g]

    nat = pl.BlockSpec((tm, W), lambda i: (i, 0))
    res = pl.pallas_call(
        body, name="dil_merge_fwd", grid=(T // tm,),
        in_specs=[_residue_tile(d, tm) for d in dils] * 2,
        out_specs=[nat] * (2 * G + 1),
        out_shape=[jax.ShapeDtypeStruct((T, W), BF)] + [jax.ShapeDtypeStruct((T, W), F32)] * (2 * G),
        scratch_shapes=[pltpu.VMEM((tm, 128), F32)],
        compiler_params=_cparams(("parallel",)),
    )(*os, *lses)
    return res[0], res[1:G + 1], res[G + 1:]


def _dil_merge_bwd(dy, os, ws, tm, after=()):
    G = len(DIL_GROUPS)
    T, W = dy.shape
    dils = [d for _, d in DIL_GROUPS]
    n_after = len(after)

    def body(*refs):
        dyt = refs[0][...]
        o, w = [r[...] for r in refs[1:G + 1]], [r[...] for r in refs[G + 1:2 * G + 1]]
        refs = refs[2 * G + 1 + n_after:]
        do_refs, dlse_refs, scratch = refs[:G], refs[G:2 * G], refs[-1]
        dws = [_head_sums(dyt * t) for t in o]
        mean = functools.reduce(jnp.add, [a * b for a, b in zip(w, dws)])
        for g, d in enumerate(dils):
            do, dlse = w[g] * dyt, w[g] * (dws[g] - mean)
            for pair in range(DIL_PAIRS):
                cols = slice(pair * 128, (pair + 1) * 128)
                _from_natural(do[:, cols], scratch, do_refs[g], pair, d, tm)
                _from_natural(dlse[:, cols], scratch, dlse_refs[g], pair, d, tm)

    nat = pl.BlockSpec((tm, W), lambda i: (i, 0))
    res = pl.pallas_call(
        body, name="dil_merge_bwd", grid=(T // tm,),
        in_specs=[nat] * (2 * G + 1) + [ANY] * n_after,
        out_specs=[_residue_tile(d, tm) for d in dils] * 2,
        out_shape=[_residue_shape(d, T, BF) for d in dils] + [_residue_shape(d, T, F32) for d in dils],
        scratch_shapes=[pltpu.VMEM((tm, 128), F32)],
        compiler_params=_cparams(("parallel",)),
    )(dy, *os, *ws, *after)
    return res[:G], res[G:]


def _qkv_prep(z, cos2, sin_signed, tm):
    T = z.shape[0]
    G = len(DIL_GROUPS)
    dils = [d for _, d in DIL_GROUPS]
    n_dil_blocks = 3 * DIL_WIDTH // 128

    def body(*refs):
        blocks = refs[:n_dil_blocks]
        cos_ref, sin_ref = refs[n_dil_blocks], refs[1 + n_dil_blocks]
        outs = refs[2 + n_dil_blocks:]
        for part in range(3):
            for g, d in enumerate(dils):
                out = outs[g * 3 + part]
                for pair in range(DIL_PAIRS):
                    blk = blocks[part * (DIL_WIDTH // 128) + g * DIL_PAIRS + pair]
                    for r in range(d):
                        rows = pl.ds(r, tm // d, stride=d) if d > 1 else slice(None)
                        x = blk[rows, :]
                        if part < 2:
                            x = _rope(x, cos_ref[rows, :], sin_ref[rows, :])
                        if part == 0:
                            x = x * Q_SCALE
                        out[pair, r] = x.astype(out.dtype)

    lane_block = [pl.BlockSpec((tm, 128), functools.partial(lambda b, i: (i, b), b)) for b in range(n_dil_blocks)]
    tab = pl.BlockSpec((tm, 128), lambda i: (i, 0))
    res = pl.pallas_call(
        body, name="qkv_prep", grid=(T // tm,),
        in_specs=lane_block + [tab, tab],
        out_specs=[_residue_tile(d, tm) for d in dils for _ in range(3)],
        out_shape=[_residue_shape(d, T, BF) for d in dils for _ in range(3)],
        compiler_params=_cparams(("parallel",)),
    )(*[z] * n_dil_blocks, cos2, sin_signed)
    return [res[3 * g:3 + 3 * g] for g in range(G)]


def _qkv_unprep(d_na, d_dil, cos2, sin_signed, tm, after=()):
    T = d_na[0].shape[0]
    G = len(DIL_GROUPS)
    dils = [d for _, d in DIL_GROUPS]
    n_after = len(after)

    def body(*refs):
        dq, dk, dv = (r[...] for r in refs[:3])
        res_refs = refs[3:3 + 3 * G]
        cs, sn = refs[3 + 3 * G][...], refs[4 + 3 * G][...]
        out, scratch = refs[5 + 3 * G + n_after], refs[-1]
        cols = [dq, dk, dv]
        for part in range(3):
            for g, d in enumerate(dils):
                for x in _to_natural(res_refs[g * 3 + part], scratch, d, tm):
                    if part < 2:
                        x = _rope(x, cs, -sn)
                    cols.append((x * Q_SCALE if part == 0 else x).astype(out.dtype))
        out[...] = jnp.concatenate(cols, axis=1)

    wide = pl.BlockSpec((tm, NA_WIDTH), lambda i: (i, 0))
    tab = pl.BlockSpec((tm, 128), lambda i: (i, 0))
    return pl.pallas_call(
        body, name="qkv_unprep", grid=(T // tm,),
        in_specs=[wide] * 3 + [_residue_tile(d, tm) for d in dils for _ in range(3)] + [tab, tab] + [ANY] * n_after,
        out_specs=pl.BlockSpec((tm, QKV_WIDTH), lambda i: (i, 0)),
        out_shape=jax.ShapeDtypeStruct((T, QKV_WIDTH), BF),
        scratch_shapes=[pltpu.VMEM((tm, 128), F32)],
        compiler_params=_cparams(("parallel",)),
    )(*d_na, *[t for g in range(G) for t in d_dil[g]], cos2, sin_signed, *after)


def _rope_tables(positions):
    half = HEAD_DIM // 2
    inv_freq = ROPE_THETA ** (-jnp.arange(half, dtype=F32) / half)
    ang = positions.astype(F32)[:, None] * inv_freq
    cos, sin = jnp.cos(ang), jnp.sin(ang)
    return jnp.tile(jnp.concatenate([cos, cos], axis=1), (1, 2)), jnp.tile(jnp.concatenate([-sin, sin], axis=1), (1, 2))


def _pack_rows(t):
    return t.reshape(-1, PACK_W)


def _me():
    return lax.axis_index("x"), lax.axis_index("y"), lax.axis_index("c")


def _other_chips(x, y):
    return [(1 - x, y), (x, 1 - y), (1 - x, 1 - y)]


def _pair_sum(g, got, tm):
    S, R, W = g.shape
    half = R // 2
    nb = half // tm

    def body(pos_ref, g_ref, got_ref, own_ref, ob_ref):
        tot = g_ref[...] + got_ref[...]
        ob_ref[...] = tot.astype(ob_ref.dtype)

        @pl.when(pl.program_id(1) == pos_ref[1])
        def _():
            own_ref[...] = tot

    tile = pl.BlockSpec((None, tm, W), lambda i, s, pos_ref: (s, i, 0))
    c, chip = lax.axis_index("c"), 2 * lax.axis_index("x") + lax.axis_index("y")
    return pl.pallas_call(
        body, name="pair_sum",
        grid_spec=pltpu.PrefetchScalarGridSpec(
            num_scalar_prefetch=1, grid=(nb, S),
            in_specs=[pl.BlockSpec((None, tm, W), lambda i, s, pos_ref: (s, pos_ref[0] * nb + i, 0)), tile],
            out_specs=[pl.BlockSpec((tm, W), lambda i, s, pos_ref: (i, 0)), tile]),
        out_shape=[jax.ShapeDtypeStruct((half, W), F32), jax.ShapeDtypeStruct((S, half, W), BF)],
        compiler_params=_cparams(("parallel", "arbitrary")),
    )(jnp.stack([c, chip]).astype(jnp.int32), g, got)


def _chip_sum(own, others, tm):
    n, h, W = others.shape
    nb = h // tm

    def body(c_ref, own_ref, p_ref, o_ref):
        o_ref[...] = ((own_ref[...] + p_ref[0].astype(F32)) + p_ref[1].astype(F32)) + p_ref[2].astype(F32)

    return pl.pallas_call(
        body, name="chip_sum",
        grid_spec=pltpu.PrefetchScalarGridSpec(
            num_scalar_prefetch=1, grid=(nb,),
            in_specs=[pl.BlockSpec((tm, W), lambda i, c_ref: (i, 0)), pl.BlockSpec((n, tm, W), lambda i, c_ref: (0, i, 0))],
            out_specs=pl.BlockSpec((tm, W), lambda i, c_ref: (c_ref[0] * nb + i, 0))),
        out_shape=jax.ShapeDtypeStruct((2 * h, W), F32),
        compiler_params=_cparams(("parallel",)),
    )(lax.axis_index("c").reshape(1).astype(jnp.int32), own, others)


def _join_halves(shard):
    h = shard.shape[0] // 2

    def body(in_ref, out_ref, send_sem, recv_sem):
        x, y, c = _me()
        cp = pltpu.make_async_remote_copy(
            src_ref=in_ref.at[pl.ds(c * h, h), :], dst_ref=out_ref.at[pl.ds(c * h, h), :],
            send_sem=send_sem, recv_sem=recv_sem, device_id=(x, y, 1 - c), device_id_type=MESH)
        cp.start()
        pltpu.make_async_remote_copy(
            src_ref=in_ref.at[pl.ds(c * h, h), :], dst_ref=out_ref.at[pl.ds((1 - c) * h, h), :],
            send_sem=send_sem, recv_sem=recv_sem, device_id=(x, y, 1 - c), device_id_type=MESH).wait_recv()
        cp.wait_send()

    return pl.pallas_call(
        body, name="join_halves", in_specs=[ANY], out_specs=ANY,
        out_shape=jax.ShapeDtypeStruct(shard.shape, shard.dtype), input_output_aliases={0: 0},
        scratch_shapes=[pltpu.SemaphoreType.DMA, pltpu.SemaphoreType.DMA],
    )(shard)


def _allreduce_small(s, after=()):
    R, W = s.shape
    n_after = len(after)

    def body(s_ref, *rest):
        o_ref, buf, send_sems, recv_sems = rest[n_after:]
        x, y, c = _me()
        me = 4 * x + 2 * y + c
        buf[me] = s_ref[...]
        peers = [((x + fx) % 2, (y + fy) % 2, (c + fc) % 2) for fx in range(2) for fy in range(2) for fc in range(2)][1:]
        sends = [pltpu.make_async_remote_copy(
            src_ref=s_ref, dst_ref=buf.at[me], send_sem=send_sems.at[k], recv_sem=recv_sems.at[k],
            device_id=peer, device_id_type=MESH) for k, peer in enumerate(peers)]
        for cp in sends:
            cp.start()
        for k, peer in enumerate(peers):
            pltpu.make_async_remote_copy(
                src_ref=s_ref, dst_ref=buf.at[4 * peer[0] + 2 * peer[1] + peer[2]], send_sem=send_sems.at[k],
                recv_sem=recv_sems.at[k], device_id=peer, device_id_type=MESH).wait_recv()
        for cp in sends:
            cp.wait_send()
        total = buf[0]
        for d in range(1, N_DEV):
            total = total + buf[d]
        o_ref[...] = total

    return pl.pallas_call(
        body, name="allreduce_small",
        in_specs=[pl.BlockSpec(memory_space=pltpu.VMEM)] + [ANY] * n_after, out_specs=pl.BlockSpec(memory_space=pltpu.VMEM),
        out_shape=jax.ShapeDtypeStruct((R, W), F32),
        scratch_shapes=[pltpu.VMEM((N_DEV, R, W), F32), pltpu.SemaphoreType.DMA((N_DEV - 1,)), pltpu.SemaphoreType.DMA((N_DEV - 1,))],
    )(s, *after)


HBM_SPEC = pl.BlockSpec(memory_space=pltpu.HBM)
SEM_SPEC = pl.BlockSpec(memory_space=pltpu.SEMAPHORE)
DATAFLOW = pltpu.SideEffectType.DATAFLOW_SIDE_EFFECTING


class _InFlight(NamedTuple):
    sems: tuple
    src: jax.Array
    land: jax.Array
    token: jax.Array


def _split_start(name, src, land_shape, land_dtype, n, copies, after=()):
    n_after = len(after)

    def body(src_ref, land_ref, *rest):
        rest = rest[n_after:]
        sems, token = rest[:2 * n], rest[-1]
        for k, (s, d, peer) in enumerate(copies(src_ref, land_ref)):
            pltpu.make_async_remote_copy(src_ref=s, dst_ref=d, send_sem=sems[k], recv_sem=sems[n + k],
                                         device_id=peer, device_id_type=MESH).start()
        token[...] = jnp.zeros_like(token)

    outs = pl.pallas_call(
        body, name=name,
        out_shape=(*[pltpu.SemaphoreType.DMA(())] * (2 * n), pltpu.HBM(src.shape, src.dtype), pltpu.HBM(land_shape, land_dtype),
                   jax.ShapeDtypeStruct((8, 128), F32)),
        in_specs=(HBM_SPEC, HBM_SPEC, *[ANY] * n_after),
        out_specs=(*[SEM_SPEC] * (2 * n), HBM_SPEC, HBM_SPEC, pl.BlockSpec(memory_space=pltpu.VMEM)),
        input_output_aliases={0: 2 * n, 1: 2 * n + 1},
        compiler_params=pltpu.CompilerParams(has_side_effects=DATAFLOW),
    )(pltpu.with_memory_space_constraint(src, pltpu.HBM), pltpu.with_memory_space_constraint(lax.empty(land_shape, land_dtype), pltpu.HBM),
      *after)
    return _InFlight(tuple(outs[:2 * n]), outs[2 * n], outs[2 * n + 1], outs[2 * n + 2])


def _split_wait(name, flight, after, n, copies):
    after = after if isinstance(after, tuple) else (after,)

    def body(src_ref, land_ref, *rest):
        sems = rest[:2 * n]
        for k, (s, d, peer) in enumerate(copies(src_ref, land_ref)):
            cp = pltpu.make_async_remote_copy(src_ref=s, dst_ref=d, send_sem=sems[k], recv_sem=sems[n + k],
                                              device_id=peer, device_id_type=MESH)
            cp.wait_send()
            cp.wait_recv()

    return pl.pallas_call(
        body, name=name,
        out_shape=(pltpu.HBM(flight.src.shape, flight.src.dtype), pltpu.HBM(flight.land.shape, flight.land.dtype)),
        in_specs=(HBM_SPEC, HBM_SPEC, *[SEM_SPEC] * (2 * n), *[ANY] * len(after)),
        out_specs=(HBM_SPEC, HBM_SPEC), input_output_aliases={0: 0, 1: 1},
        compiler_params=pltpu.CompilerParams(has_side_effects=DATAFLOW),
    )(flight.src, flight.land, *flight.sems, *after)


def _gather_copies(src_ref, land_ref):
    x, y, c = _me()
    return [(src_ref, land_ref.at[2 * x + y], (*chip, c)) for chip in _other_chips(x, y)]


def _gather_start(packed, tag, after=()):
    return _split_start(f"gather_start_{tag}", packed, (N_CHIPS, *packed.shape), packed.dtype, 3, _gather_copies, after)


def _gather_wait(flight, after, tag):
    src, others = _split_wait(f"gather_wait_{tag}", flight, after, 3, _gather_copies)
    return lax.dynamic_update_slice(others, src[None], (2 * lax.axis_index("x") + lax.axis_index("y"), 0, 0))


def _across_copies(src_ref, land_ref):
    x, y, c = _me()
    half = src_ref.shape[0] // 2
    rows = pl.ds(c * half, half)
    return [(src_ref.at[rows, :], land_ref.at[2 * x + y, rows, :], (*chip, c)) for chip in _other_chips(x, y)]


def _to_sibling_copies(all_ref, unused_ref):
    x, y, c = _me()
    half = all_ref.shape[1] // 2
    places = [all_ref.at[2 * chip[0] + chip[1], pl.ds(c * half, half), :] for chip in _other_chips(x, y)]
    return [(place, place, (x, y, 1 - c)) for place in places]


def _gather_halves_start(shard, tag, after=()):
    return _split_start(f"gather_{tag}_across_start", shard, (N_CHIPS, *shard.shape), shard.dtype, 3, _across_copies, after)


def _gather_halves_relay(flight, after, tag):
    shard, landed = _split_wait(f"gather_{tag}_across_wait", flight, after, 3, _across_copies)
    return shard, _split_start(f"gather_{tag}_sibling_start", landed, (8, 128), landed.dtype, 3, _to_sibling_copies)


def _gather_halves_finish(shard, relay, after, tag):
    others = _split_wait(f"gather_{tag}_sibling_wait", relay, after, 3, _to_sibling_copies)[0]
    return lax.dynamic_update_slice(others, shard[None], (2 * lax.axis_index("x") + lax.axis_index("y"), 0, 0))


def _assemble_w_in(shards, tm):
    S, R, C = shards.shape
    n_gates = 2 * D_MODEL

    def body(s_ref, w_ref, g_ref):
        full = jnp.concatenate([s_ref[s] for s in range(S)], axis=1)
        w_ref[...] = full
        g_ref[...] = full[:, S * C - n_gates:]

    return pl.pallas_call(
        body, name="assemble_w_in", grid=(R // tm,),
        in_specs=[pl.BlockSpec((S, tm, C), lambda i: (0, i, 0))],
        out_specs=[pl.BlockSpec((tm, S * C), lambda i: (i, 0)), pl.BlockSpec((tm, n_gates), lambda i: (i, 0))],
        out_shape=[jax.ShapeDtypeStruct((R, S * C), shards.dtype), jax.ShapeDtypeStruct((R, n_gates), shards.dtype)],
        compiler_params=_cparams(("parallel",)),
    )(shards)


def _swap_copies(src_ref, land_ref):
    x, y, c = _me()
    half = land_ref.shape[1]
    return [(src_ref.at[:, pl.ds((1 - c) * half, half), :], land_ref, (x, y, 1 - c))]


def _swap_start(g, tag):
    S, R, W = g.shape
    return _split_start(f"swap_halves_start_{tag}", g, (S, R // 2, W), g.dtype, 1, _swap_copies)


def _swap_wait(flight, after, tag):
    return _split_wait(f"swap_halves_wait_{tag}", flight, after, 1, _swap_copies)


def _scatter_copies(src_ref, land_ref):
    x, y, c = _me()
    return [(src_ref.at[2 * chip[0] + chip[1]], land_ref.at[j], (*chip, c)) for j, chip in enumerate(_other_chips(x, y))]


def _scatter_start(part, tag):
    S, h, W = part.shape
    return _split_start(f"scatter_chips_start_{tag}", part, (S - 1, h, W), part.dtype, 3, _scatter_copies)


def _scatter_wait(flight, after, tag):
    return _split_wait(f"scatter_chips_wait_{tag}", flight, after, 3, _scatter_copies)[1]


def _join_copies(shard_ref, unused_ref):
    x, y, c = _me()
    h = shard_ref.shape[0] // 2
    rows = shard_ref.at[pl.ds(c * h, h), :]
    return [(rows, rows, (x, y, 1 - c))]


def _join_start(shard):
    return _split_start("join_halves_start", shard, (8, 128), shard.dtype, 1, _join_copies)


def _join_wait(flight, after):
    return _split_wait("join_halves_wait", flight, after, 1, _join_copies)[0]


def _adamw(name, g, g_row0, w, m, v):
    _, R, C = w.shape
    tm = next(cand for cand in (368, 256, 128, 64, 32, 16, 8) if R % cand == 0)
    assert g_row0 % tm == 0 and g.shape[1] == C

    def body(g_ref, w_ref, m_ref, v_ref, go_ref, d_ref, mo_ref, vo_ref):
        gt = g_ref[...]
        mt = ADAM_B1 * m_ref[...] + (1.0 - ADAM_B1) * gt
        vt = ADAM_B2 * v_ref[...] + (1.0 - ADAM_B2) * jnp.square(gt)
        m_hat = mt / (1.0 - ADAM_B1 ** ADAM_STEP)
        v_hat = vt / (1.0 - ADAM_B2 ** ADAM_STEP)
        go_ref[...] = gt
        d_ref[...] = -ADAM_LR * (m_hat / (jnp.sqrt(v_hat) + ADAM_EPS) + ADAM_WD * w_ref[...])
        mo_ref[...] = mt
        vo_ref[...] = vt

    state = pl.BlockSpec((None, tm, C), lambda i: (0, i, 0))
    return pl.pallas_call(
        body, name=name, grid=(R // tm,),
        in_specs=[pl.BlockSpec((tm, C), lambda i: (g_row0 // tm + i, 0)), state, state, state],
        out_specs=[state] * 4, out_shape=[jax.ShapeDtypeStruct((1, R, C), F32)] * 4,
        compiler_params=_cparams(("parallel",)),
    )(g, w, m, v)


def _unpack_weights(gathered, names):
    S = gathered.shape[0]
    shard_shapes = {"w_in": (D_MODEL, (QKV_WIDTH + 2 * D_MODEL) // S), "w_branch_na": (NA_WIDTH, D_MODEL // S),
                    "w_branch_dil": (DIL_OUT_WIDTH, D_MODEL // S), "w_out": (D_MODEL // S, D_MODEL),
                    "w_up": (D_MODEL, D_FF // S), "w_down": (D_FF // S, D_MODEL),
                    "w_ple_gate": (D_MODEL // S, D_MODEL), "w_ple_proj": (PLE_DIM, D_MODEL // S)}
    col_sharded = {"w_in", "w_branch_na", "w_branch_dil", "w_up", "w_ple_proj"}
    out, r0 = {}, 0
    for name in names:
        rows, cols = shard_shapes[name]
        n = rows * cols // PACK_W
        t = gathered[:, r0:r0 + n, :].reshape(S, rows, cols)
        r0 += n
        out[name] = t.transpose(1, 0, 2).reshape(rows, S * cols) if name in col_sharded else t.reshape(S * rows, cols)
    return out


def kernel(x, p, positions, g_mix, w_in, rpb, w_branch_na, w_branch_dil, w_out, g_mlp, w_up, w_down, g_ple, w_ple_gate, w_ple_proj, g_final, loss_target, m_g_mix, m_w_in, m_rpb, m_w_branch_na, m_w_branch_dil, m_w_out, m_g_mlp, m_w_up, m_w_down, m_g_ple, m_w_ple_gate, m_w_ple_proj, m_g_final, v_g_mix, v_w_in, v_rpb, v_w_branch_na, v_w_branch_dil, v_w_out, v_g_mlp, v_w_up, v_w_down, v_g_ple, v_w_ple_gate, v_w_ple_proj, v_g_final):
    shards = {"w_in": w_in[0], "w_branch_na": w_branch_na[0], "w_branch_dil": w_branch_dil[0], "w_out": w_out[0],
              "w_up": w_up[0], "w_down": w_down[0], "w_ple_gate": w_ple_gate[0], "w_ple_proj": w_ple_proj[0]}
    params = {"w_in": w_in, "w_branch_na": w_branch_na, "w_branch_dil": w_branch_dil, "w_out": w_out, "w_up": w_up,
              "w_down": w_down, "w_ple_gate": w_ple_gate, "w_ple_proj": w_ple_proj,
              "m_w_in": m_w_in, "m_w_branch_na": m_w_branch_na, "m_w_branch_dil": m_w_branch_dil, "m_w_out": m_w_out,
              "m_w_up": m_w_up, "m_w_down": m_w_down, "m_w_ple_gate": m_w_ple_gate, "m_w_ple_proj": m_w_ple_proj,
              "v_w_in": v_w_in, "v_w_branch_na": v_w_branch_na, "v_w_branch_dil": v_w_branch_dil, "v_w_out": v_w_out,
              "v_w_up": v_w_up, "v_w_down": v_w_down, "v_w_ple_gate": v_w_ple_gate, "v_w_ple_proj": v_w_ple_proj}

    xs, ps, tgt = x[0], p[0, 0], loss_target[0]
    T = xs.shape[0]
    TM = 512
    gm, gl, gp, gf = g_mix, g_mlp, g_ple, g_final.reshape(1, D_MODEL)

    across = _gather_halves_start(shards["w_in"].astype(BF), "in")
    a = _rowwise("norm_mix", lambda h, g: h * _rms(h) * g, T, TM, [_row(xs, TM), _full(gm)], [(D_MODEL, BF)],
                 after=(across.token,))
    cos2, sin_signed = _rope_tables(positions[0])
    tab = _na_bias_table(rpb[0])
    packed_mix = jnp.concatenate([_pack_rows(shards[n].astype(BF)) for n in GATHER_MIX], axis=0)
    packed_mlp = jnp.concatenate([_pack_rows(shards[n].astype(BF)) for n in GATHER_MLP], axis=0)
    w_in_shard, w_in_relay = _gather_halves_relay(across, (a, tab, cos2, sin_signed, packed_mix, packed_mlp), "in")
    w_in_all = _gather_halves_finish(w_in_shard, w_in_relay, w_in_relay.token, "in")
    w_in_full, w_gates = _assemble_w_in(w_in_all, 256)
    W = {"w_in": w_in_full}
    mix_flight = _gather_start(packed_mix, "mix", after=(w_in_all,))
    mlp_across = _gather_halves_start(packed_mlp, "mlp", after=(mix_flight.token,))

    n3 = 3 * NA_WIDTH
    qkv = _mm("in_na", a, W["w_in"], "nn", 1024, 768, 1024, [BF], after=(mlp_across.token,),
              b_view=(n3, (D_MODEL, 768), lambda j, k: (k, j)))
    z_dil = _mm("in_dil", a, W["w_in"], "nn", 1024, 768, 1024, [F32], after=(mlp_across.token,),
                b_view=(3 * DIL_WIDTH, (D_MODEL, 768), lambda j, k: (k, n3 // 768 + j)))
    z_gates = _mm("in_gates", a, w_gates, "nn", 1024,1024, 1024, [BF], after=(mlp_across.token,))

    dil_ops = _qkv_prep(z_dil, cos2, sin_signed, TM)
    y_na = _na_fwd(qkv, tab)
    band = [_band_fwd(*dil_ops[g], g) for g in range(len(DIL_GROUPS))]
    y_dil, w_grp, o_nat = _dil_merge_fwd([b[0] for b in band], [b[1] for b in band], T, TM)

    W.update(_unpack_weights(_gather_wait(mix_flight, y_dil, "mix"), GATHER_MIX))
    mlp_shard, mlp_relay = _gather_halves_relay(mlp_across, y_dil, "mlp")
    u_na = _mm("branch_na", y_na, W["w_branch_na"], "nn", 1024,1024, 512, [BF], after=(mlp_relay.token,))
    u_dil = _mm("branch_dil", y_dil, W["w_branch_dil"], "nn", 1024,1024, 256, [BF])
    mixed = _rowwise(
        "gate_mix", lambda gn, gd, un, ud: _sigmoid(gn.astype(F32)) * un.astype(F32) + _sigmoid(gd.astype(F32)) * ud.astype(F32), T, TM,
        [_row(z_gates, TM, 0, D_MODEL), _row(z_gates, TM, 1, D_MODEL), _row(u_na, TM), _row(u_dil, TM)], [(D_MODEL, BF)])
    def add_norm(d, h, g):
        h = h + d
        return h, h * _rms(h) * g

    h1, cn = _mm("out_proj", mixed, W["w_out"], "nn", 512, 1024, 1024, [F32, BF], epilogue=add_norm, extras=(xs,), consts=(gl,))
    mlp_all = _gather_halves_finish(mlp_shard, mlp_relay, cn, "mlp")
    W.update({n: t for n, t in _unpack_weights(mlp_all, GATHER_MLP).items() if n.startswith("w_ple")})
    chip_block = (None, D_MODEL, PACK_W)
    up, act = _mm("mlp_up", cn, mlp_all, "nn", 1024,1024, 1024, [BF, BF],
                  epilogue=lambda acc: (acc, jnp.square(jnp.maximum(acc, 0.0))), b_view=(D_FF, chip_block, lambda j, k: (j, 0, 0)))
    h2, en = _mm("mlp_down", act, mlp_all, "nn", 1024, 1024, 1024, [F32, BF], epilogue=add_norm, extras=(h1,), consts=(gp,),
                 b_view=(D_MODEL, chip_block, lambda j, k: (k, 1, 0)))
    pp = _mm("ple_proj", ps, W["w_ple_proj"], "nn", 1024,1024, 256, [F32])

    def head(gtt, h2t, ppt, tg, g):
        sg = _sigmoid(gtt)
        h3 = h2t + sg * ppt
        yo = h3 * _rms(h3) * g
        diff = yo - tg
        loss = 0.5 * jnp.sum(jnp.mean(jnp.square(diff), axis=-1, keepdims=True), axis=0, keepdims=True)
        dh3, dg = _rms_bwd(diff * (1.0 / D_MODEL), h3, g)
        return dh3, dh3 * ppt * sg * (1.0 - sg), dh3 * sg, jnp.broadcast_to(loss, (1, 128)), dg

    dh3, d_gt, d_pp, loss_part, dg_final = _mm(
        "ple_gate_loss_head", en, W["w_ple_gate"], "nn", 512, 1024, 1024, [F32, BF, BF], epilogue=head,
        extras=(h2, pp, tgt), consts=(gf,), sums=[128, D_MODEL])

    early_shapes = {n: shards[n].shape for n in REDUCE_EARLY}
    early_rows = sum(r * c for r, c in early_shapes.values()) // PACK_W
    shard_rows = D_MODEL // N_CHIPS
    early_buf = _mm("g_ple_gate", en, d_gt, "tn", 1024, 1024, 1024, [F32],
                    into=(jax.ShapeDtypeStruct((N_CHIPS, early_rows, PACK_W), F32), (N_CHIPS, shard_rows, PACK_W),
                          lambda i, j: (0, 2 * D_MODEL // shard_rows, 0)))
    g_ple_proj = _mm("g_ple_proj", ps, d_pp, "tn", 256, 1024, 1024,[F32])

    def add_norm_bwd(dn, dh_out, h, g):
        dh, dg = _rms_bwd(dn, h, g)
        dh = dh_out + dh
        return dh, dh, dg

    dh2, dh2_b, dg_ple = _mm("d_ple_gate", d_gt, W["w_ple_gate"], "nt", 512, 1024, 1024, [F32, BF],
                             epilogue=add_norm_bwd, extras=(dh3, h2), consts=(gp,), sums=[D_MODEL])
    d_up = _mm("d_mlp_down", dh2_b, mlp_all, "nt", 1024,1024, 1024, [BF], b_view=(D_FF, chip_block, lambda j, k: (j, 1, 0)),
               epilogue=lambda acc, u: (acc * (2.0 * jnp.maximum(u.astype(F32), 0.0)),), extras=(up,))
    early_buf = _mm("g_mlp_down", act, dh2_b, "tn", 1024, 1024, 1024,[F32],
                    into=(early_buf, (None, D_MODEL, PACK_W), lambda i, j: (i, 1, 0)))
    early_buf = _mm("g_mlp_up", cn, d_up, "tn", 1024, 1024, 1024,[F32],
                    into=(early_buf, (None, D_MODEL, PACK_W), lambda i, j: (j, 0, 0)))
    dh1, dh1_b, dg_mlp = _mm("d_mlp_up", d_up, mlp_all, "nt", 1024, 1024, 1024, [F32, BF], epilogue=add_norm_bwd,
                             b_view=(D_MODEL, chip_block, lambda j, k: (k, 0, 0)),
                             extras=(dh2, h1), consts=(gl,), sums=[D_MODEL])
    d_mixed = _mm("d_out_proj", dh1_b, W["w_out"], "nt", 1024,1024, 1024, [F32])
    early_buf = _mm("g_out_proj", mixed, dh1_b, "tn", 1024, 1024, 1024, [F32],
                    into=(early_buf, (N_CHIPS, shard_rows, PACK_W), lambda i, j: (0, 2 * D_MODEL // shard_rows + 1, 0)))

    def gate_bwd(dm, gn, gd, un, ud):
        gn, gd, un, ud = (t.astype(F32) for t in (gn, gd, un, ud))
        sn, sd = _sigmoid(gn), _sigmoid(gd)
        return jnp.concatenate([dm * un * sn * (1.0 - sn), dm * ud * sd * (1.0 - sd)], axis=1), dm * sn, dm * sd

    dz_gates, d_u_na, d_u_dil = _rowwise(
        "gate_mix_bwd", gate_bwd, T, TM,
        [_row(d_mixed, TM), _row(z_gates, TM, 0, D_MODEL), _row(z_gates, TM, 1, D_MODEL), _row(u_na, TM), _row(u_dil, TM)],
        [(2 * D_MODEL, BF), (D_MODEL, BF), (D_MODEL, BF)])
    g_branch_na = _mm("g_branch_na", y_na, d_u_na, "tn", 1024, 1024, 1024,[F32])
    g_branch_dil = _mm("g_branch_dil", y_dil, d_u_dil, "tn", 256, 1024, 1024,[F32])
    small_rows = [jnp.concatenate([_pack_rows(g[:, s * shard_rows:(s + 1) * shard_rows]) for g in (g_ple_proj, g_branch_na, g_branch_dil)],
                                  axis=0) for s in range(N_CHIPS)]
    early_buf = lax.dynamic_update_slice(early_buf, jnp.stack(small_rows), (0, 2 * D_MODEL + 2 * shard_rows, 0))
    early_tm = early_rows // 4
    swap_flight = _swap_start(early_buf, "early")
    d_y_na = _mm("d_branch_na", d_u_na, W["w_branch_na"], "nt", 1024,512, 1024, [BF], after=(swap_flight.token,))
    d_y_dil = _mm("d_branch_dil", d_u_dil, W["w_branch_dil"], "nt", 1024,256, 1024, [F32])

    dqa, dka, dva, dtab = _na_bwd(qkv, tab, d_y_na)
    early_g, early_got = _swap_wait(swap_flight, dqa, "early")
    early_pair, early_pair_b = _pair_sum(early_g, early_got, early_tm)
    scatter_flight = _scatter_start(early_pair_b, "early")

    do_res, dlse_res = _dil_merge_bwd(d_y_dil, o_nat, w_grp, TM, after=(scatter_flight.token,))
    d_dil = [_band_bwd(*dil_ops[g], do_res[g], dlse_res[g], g) for g in range(len(DIL_GROUPS))]

    dz_qkv = _qkv_unprep((dqa, dka, dva), d_dil, cos2, sin_signed, TM)
    g_in_parts = [_mm("g_in_qkv", dz_qkv, a, "tn", 1280, 1024, 1024,[F32]), _mm("g_in_gates", dz_gates, a, "tn", 1024, 1024, 1024,[F32])]
    early_mine = _chip_sum(early_pair, _scatter_wait(scatter_flight, g_in_parts[1], "early"), early_tm)
    join_flight = _join_start(early_mine)
    in_cols = shards["w_in"].shape[1]

    def owner_rows(s):
        lo, hi, split = s * in_cols, (s + 1) * in_cols, g_in_parts[0].shape[0]
        pieces = [g_in_parts[0][lo:min(hi, split)]] if lo < split else []
        pieces += [g_in_parts[1][max(lo, split) - split:hi - split]] if hi > split else []
        return pieces[0] if len(pieces) == 1 else jnp.concatenate(pieces, axis=0)

    late_tm = in_cols // 4
    late_swap = _swap_start(jnp.stack([owner_rows(s) for s in range(N_CHIPS)]), "late")
    d_a = _mm("d_in_qkv", dz_qkv, W["w_in"], "nt", 1024,1024, 1280, [F32], after=(late_swap.token, join_flight.token),
              b_view=(D_MODEL, (D_MODEL, 1280), lambda j, k: (j, k)))
    late_g, late_got = _swap_wait(late_swap, d_a, "late")
    late_pair, late_pair_b = _pair_sum(late_g, late_got, late_tm)
    late_scatter = _scatter_start(late_pair_b, "late")
    d_rpb = _na_rpb_grad(dtab, after=(late_scatter.token,))[:, :2 * NA_WIN_ROWS - 1, :2 * NA_WIN_COLS - 1]
    def first_bwd(dn_gates, dn_qkv, dh_out, h, g):
        dh, dg = _rms_bwd(dn_gates + dn_qkv, h, g)
        return dh_out + dh, dg

    grad_x, dg_mix = _mm("d_in_gates", dz_gates, w_gates, "nt", 512, 1024, 1024, [F32], epilogue=first_bwd,
                         extras=(d_a, dh1, xs), consts=(gm,), sums=[D_MODEL], after=(late_scatter.token,))
    early_shard = _join_wait(join_flight, grad_x)

    n_rpb = rpb.size
    rpb_rows = 4
    small = jnp.concatenate([
        dg_mix, dg_mlp, dg_ple, dg_final,
        jnp.pad(d_rpb.reshape(-1), (0, rpb_rows * D_MODEL - n_rpb)).reshape(rpb_rows, D_MODEL),
        jnp.pad(loss_part, ((0, 0), (0, D_MODEL - loss_part.shape[1]))),
        jnp.zeros((SMALL_ROWS - 5 - rpb_rows, D_MODEL), F32)], axis=0)
    out = {"grad": {}, "delta": {}, "new_m": {}, "new_v": {}}

    def update(n, g, row0):
        res = _adamw("adamw_" + n, g, row0, params[n], params["m_" + n], params["v_" + n])
        for kind, t in zip(("grad", "delta", "new_m", "new_v"), res, strict=True):
            out[kind][n] = t

    row0 = 0
    for n in REDUCE_EARLY:
        rows, cols = early_shapes[n]
        n_rows = rows * cols // PACK_W
        if cols == PACK_W:
            update(n, early_shard, row0)
        else:
            update(n, early_shard[row0:row0 + n_rows].reshape(rows, cols), 0)
        row0 += n_rows
    late_others = _scatter_wait(late_scatter, out["new_v"][REDUCE_EARLY[-1]], "late")
    late_mine = _chip_sum(late_pair, late_others, late_tm)
    small = _allreduce_small(small, after=(late_mine,))
    res = _adamw("adamw_w_in", _join_halves(late_mine), 0, *[jnp.swapaxes(params[n], 1, 2) for n in ("w_in", "m_w_in", "v_w_in")])
    for kind, t in zip(("grad", "delta", "new_m", "new_v"), res, strict=True):
        out[kind]["w_in"] = jnp.swapaxes(t, 1, 2)
    loss = small[4 + rpb_rows, 0]

    def small_pack(a0, a1, a2, a3, r):
        return jnp.concatenate([a0.reshape(1, -1), a1.reshape(1, -1), a2.reshape(1, -1), a3.reshape(1, -1),
                                jnp.pad(r.reshape(-1), (0, rpb_rows * D_MODEL - n_rpb)).reshape(rpb_rows, D_MODEL)], axis=0)

    small_res = _adamw("adamw_small", small, 0, small_pack(g_mix, g_mlp, g_ple, g_final, rpb)[None],
                       small_pack(m_g_mix, m_g_mlp, m_g_ple, m_g_final, m_rpb)[None],
                       small_pack(v_g_mix, v_g_mlp, v_g_ple, v_g_final, v_rpb)[None])

    def small_unpack(t):
        return {"g_mix": t[0].reshape(g_mix.shape), "g_mlp": t[1].reshape(g_mlp.shape), "g_ple": t[2].reshape(g_ple.shape),
                "g_final": t[3].reshape(g_final.shape), "rpb": t[4:].reshape(-1)[:n_rpb].reshape(rpb.shape)}

    for kind, t in zip(("grad", "delta", "new_m", "new_v"), small_res, strict=True):
        out[kind].update(small_unpack(t[0]))

    order = ["g_mix", "w_in", "rpb", "w_branch_na", "w_branch_dil", "w_out", "g_mlp", "w_up", "w_down", "g_ple",
             "w_ple_gate", "w_ple_proj", "g_final"]
    return (loss, grad_x[None], *[out["grad"][n] for n in order], *[out["delta"][n] for n in order],
            *[out["new_m"][n] for n in order], *[out["new_v"][n] for n in order])
```

```python
import functools
from typing import NamedTuple

import jax
import jax.numpy as jnp
from jax import lax
from jax.experimental import pallas as pl
from jax.experimental.pallas import tpu as pltpu

BF = jnp.bfloat16
F32 = jnp.float32
MESH = pl.DeviceIdType.MESH
ANY = pl.BlockSpec(memory_space=pl.ANY)

V7X_VMEM_BYTES = 64 * 1024 * 1024
VMEM_LIMIT = V7X_VMEM_BYTES - 16 * 1024 * 1024

D_MODEL = 1024
HEAD_DIM = 64
GRID_W = 64
NA_HEADS = 8
NA_WIN_ROWS = 8
NA_WIN_COLS = 16
NA_WIDTH = NA_HEADS * HEAD_DIM
DIL_GROUPS = ((128, 1), (512, 4), (2048, 16))
DIL_HPG = 4
DIL_HEADS = DIL_HPG * len(DIL_GROUPS)
DIL_WIDTH = DIL_HEADS * HEAD_DIM
DIL_OUT_WIDTH = DIL_HPG * HEAD_DIM
DIL_RADIUS = 64
QKV_WIDTH = 3 * NA_WIDTH + 3 * DIL_WIDTH
D_FF = 4 * D_MODEL
PLE_DIM = 256
ROPE_THETA = 10000.0
RMS_EPS = 1e-6
NEG_INF = -1e30
Q_SCALE = HEAD_DIM ** -0.5

ADAM_LR = 0.001
ADAM_B1 = 0.9
ADAM_B2 = 0.999
ADAM_EPS = 1e-08
ADAM_WD = 0.01
ADAM_STEP = 10

N_CHIPS = 4
N_DEV = 8
PACK_W = 1024
GATHER_MIX = ("w_branch_na", "w_branch_dil", "w_out")
GATHER_MLP = ("w_up", "w_down", "w_ple_gate", "w_ple_proj")
REDUCE_EARLY = ("w_up", "w_down", "w_ple_gate", "w_out", "w_ple_proj", "w_branch_na", "w_branch_dil")
SMALL_ROWS = 16


def _cparams(sem=None):
    return pltpu.CompilerParams(dimension_semantics=sem, vmem_limit_bytes=VMEM_LIMIT)


def _mm(name, a, b, mode, tm, tn, tk, out_dtypes, epilogue=None, extras=(), consts=(), sums=(), after=(), into=None,
        b_view=None):
    if mode == "nn":
        (M, K), N = a.shape, b.shape[1]
    elif mode == "nt":
        (M, K), N = a.shape, b.shape[0]
    else:
        (K, M), N = a.shape, b.shape[1]
    if b_view is not None:
        N = b_view[0]
    tm, tn, tk = min(tm, M), min(tn, N), min(tk, K)
    assert M % tm == 0 and N % tn == 0 and K % tk == 0, (name, M, N, K, tm, tn, tk)
    if mode == "nn":
        a_spec = pl.BlockSpec((tm, tk), lambda i, j, k: (i, k))
        b_spec = pl.BlockSpec((tk, tn), lambda i, j, k: (k, j))
        dims = (((1,), (0,)), ((), ()))
    elif mode == "nt":
        a_spec = pl.BlockSpec((tm, tk), lambda i, j, k: (i, k))
        b_spec = pl.BlockSpec((tn, tk), lambda i, j, k: (j, k))
        dims = (((1,), (1,)), ((), ()))
    else:
        a_spec = pl.BlockSpec((tk, tm), lambda i, j, k: (k, i))
        b_spec = pl.BlockSpec((tk, tn), lambda i, j, k: (k, j))
        dims = (((0,), (0,)), ((), ()))
    if b_view is not None:
        b_spec = pl.BlockSpec(b_view[1], lambda i, j, k: b_view[2](j, k))
    nk = K // tk
    n_extra, n_const, n_out, n_sum = len(extras), len(consts), len(out_dtypes), len(sums)
    tile = pl.BlockSpec((tm, tn), lambda i, j, k: (i, j))
    assert not sums or tn == N, "row sums need whole rows in a tile"

    n_after = len(after)

    def body(a_ref, b_ref, *rest):
        extra_refs, rest = rest[:n_extra + n_const], rest[n_extra + n_const + n_after:]
        out_refs, sum_refs, acc = rest[:n_out], rest[n_out:n_out + n_sum], rest[-1]
        i, k = pl.program_id(0), pl.program_id(2)
        def product():
            return lax.dot_general(a_ref[...].astype(BF), b_ref[...].astype(BF), dims, preferred_element_type=F32)

        if nk > 1:
            @pl.when(k == 0)
            def _():
                acc[...] = jnp.zeros_like(acc)

            acc[...] += product()

        @pl.when(k == nk - 1)
        def _():
            total = product() if nk == 1 else acc[...]
            outs = (total,) if epilogue is None else epilogue(total, *[e[...] for e in extra_refs])
            for o_ref, val in zip(out_refs, outs[:n_out], strict=True):
                o_ref[...] = val.astype(o_ref.dtype).reshape(o_ref.shape)
            for s_ref, val in zip(sum_refs, outs[n_out:], strict=True):
                @pl.when(i == 0)
                def _():
                    s_ref[...] = val

                @pl.when(i != 0)
                def _():
                    s_ref[...] += val

    out_specs = [tile] * n_out + [pl.BlockSpec((1, c), lambda i, j, k: (0, 0)) for c in sums]
    out_shape = [jax.ShapeDtypeStruct((M, N), dt) for dt in out_dtypes] + [jax.ShapeDtypeStruct((1, c), F32) for c in sums]
    operands, aliases = [a, b, *extras, *consts, *after], {}
    in_specs = ([a_spec, b_spec] + [tile] * n_extra
                + [pl.BlockSpec(c.shape, functools.partial(lambda nd, i, j, k: (0,) * nd, c.ndim)) for c in consts] + [ANY] * n_after)
    if into is not None:
        assert n_out == 1
        target, block, index = into
        out_specs = [pl.BlockSpec(block, lambda i, j, k: index(i, j))]
        out_shape = [jax.ShapeDtypeStruct(target.shape, target.dtype)]
        if not isinstance(target, jax.ShapeDtypeStruct):
            aliases = {len(operands): 0}
            operands.append(target)
            in_specs.append(ANY)
            n_after += 1

    outs = pl.pallas_call(
        body, name=name, grid=(M // tm, N // tn, nk),
        in_specs=in_specs, out_specs=out_specs, out_shape=out_shape,
        scratch_shapes=[pltpu.VMEM((tm, tn) if nk > 1 else (8, 128), F32)], input_output_aliases=aliases,
        compiler_params=_cparams(("arbitrary",) * 3 if sums else ("parallel", "parallel", "arbitrary")),
    )(*operands)
    return outs[0] if len(outs) == 1 else outs


def _row(arr, tm, col_block=None, width=None):
    width = arr.shape[1] if width is None else width
    cb = 0 if col_block is None else col_block
    return arr, pl.BlockSpec((tm, width), lambda i: (i, cb))


def _full(arr):
    nd = arr.ndim
    return arr, pl.BlockSpec(arr.shape, lambda i: (0,) * nd)


def _rowwise(name, body, T, tm, ins, outs, sums=(), after=()):
    n_in, n_out, n_sum, n_after = len(ins), len(outs), len(sums), len(after)

    def kern(*refs):
        in_refs, refs = refs[:n_in], refs[n_in + n_after:]
        out_refs, sum_refs = refs[:n_out], refs[n_out:]
        res = body(*[r[...] for r in in_refs])
        res = res if isinstance(res, tuple) else (res,)
        for o_ref, val in zip(out_refs, res[:n_out], strict=True):
            o_ref[...] = val.astype(o_ref.dtype)
        if n_sum:
            @pl.when(pl.program_id(0) == 0)
            def _():
                for s_ref in sum_refs:
                    s_ref[...] = jnp.zeros_like(s_ref)

            for s_ref, val in zip(sum_refs, res[n_out:], strict=True):
                s_ref[...] += val

    res = pl.pallas_call(
        kern, name=name, grid=(T // tm,),
        in_specs=[spec for _, spec in ins] + [ANY] * n_after,
        out_specs=[pl.BlockSpec((tm, c), lambda i: (i, 0)) for c, _ in outs]
        + [pl.BlockSpec((1, c), lambda i: (0, 0)) for c in sums],
        out_shape=[jax.ShapeDtypeStruct((T, c), dt) for c, dt in outs]
        + [jax.ShapeDtypeStruct((1, c), F32) for c in sums],
        compiler_params=_cparams(("arbitrary",)),
    )(*[a for a, _ in ins], *after)
    return res[0] if len(res) == 1 else res


def _sigmoid(x):
    return 1.0 / (1.0 + jnp.exp(-x))


def _rms(h):
    return lax.rsqrt(jnp.mean(h * h, axis=-1, keepdims=True) + RMS_EPS)


def _rms_bwd(dy, h, g):
    r = _rms(h)
    n = h * r
    dn = dy * g
    dh = r * (dn - n * jnp.mean(dn * n, axis=-1, keepdims=True))
    return dh, jnp.sum(dy * n, axis=0, keepdims=True)


def _rope(x, cos2, sin_signed):
    lane = lax.broadcasted_iota(jnp.int32, x.shape, 1)
    swapped = jnp.where((lane % HEAD_DIM) < HEAD_DIM // 2, pltpu.roll(x, 128 - HEAD_DIM // 2, 1), pltpu.roll(x, HEAD_DIM // 2, 1))
    return x * cos2 + swapped * sin_signed


NA_KEYS = NA_WIN_ROWS * GRID_W
NA_BASES = 8


def _na_row_geometry(r, rows):
    first = jnp.clip(r - NA_WIN_ROWS // 2, 0, rows - NA_WIN_ROWS)
    base = first - r + (NA_WIN_ROWS - 1)
    return pl.multiple_of(first * GRID_W, GRID_W), base


NA_ROWS_PER_STEP = 16
NA_BWD_ROWS_PER_STEP = 8


def _softmax_rows(s):
    p = jnp.exp(s - jnp.max(s, axis=-1, keepdims=True))
    return p / jnp.sum(p, axis=-1, keepdims=True)


def _split_pair(t):
    first = lax.broadcasted_iota(jnp.int32, t.shape, 1) < HEAD_DIM
    zero = jnp.zeros_like(t)
    return jnp.where(first, t, zero), jnp.where(first, zero, t)


def _join_pair(a, b):
    return jnp.where(lax.broadcasted_iota(jnp.int32, a.shape, 1) < HEAD_DIM, a, b)


_NT = (((1,), (1,)), ((), ()))
_TN = (((0,), (0,)), ((), ()))


def _na_fwd(qkv, tab):
    T = qkv.shape[0]
    rows = T // GRID_W
    n_pairs = NA_WIDTH // 128

    def body(q_ref, k_ref, v_ref, tab_ref, y_ref):
        def step(it, carry):
            geo = [_na_row_geometry(it * NA_ROWS_PER_STEP + u, rows) for u in range(NA_ROWS_PER_STEP)]
            q0s = [pl.multiple_of((it * NA_ROWS_PER_STEP + u) * GRID_W, GRID_W) for u in range(NA_ROWS_PER_STEP)]
            ss = [lax.dot_general(jnp.concatenate(_split_pair(q_ref[pl.ds(q0, GRID_W), :] * Q_SCALE), axis=0),
                                  k_ref[pl.ds(k0, NA_KEYS), :], _NT, preferred_element_type=F32)
                  for q0, (k0, _) in zip(q0s, geo)]
            ps = [_softmax_rows(s + jnp.concatenate([tab_ref[0, base], tab_ref[1, base]], axis=0)) for s, (_, base) in zip(ss, geo)]
            ys = [jnp.dot(p.astype(BF), v_ref[pl.ds(k0, NA_KEYS), :], preferred_element_type=F32) for p, (k0, _) in zip(ps, geo)]
            for q0, y2 in zip(q0s, ys):
                y_ref[pl.ds(q0, GRID_W), :] = _join_pair(y2[:GRID_W], y2[GRID_W:]).astype(y_ref.dtype)
            return carry

        lax.fori_loop(0, rows // NA_ROWS_PER_STEP, step, 0)

    def cols(first):
        return pl.BlockSpec((T, 128), lambda j: (0, first + j))

    return pl.pallas_call(
        body, name="na_fwd", grid=(n_pairs,),
        in_specs=[cols(0), cols(n_pairs), cols(2 * n_pairs), pl.BlockSpec((2, NA_BASES, GRID_W, NA_KEYS), lambda j: (j, 0, 0, 0))],
        out_specs=cols(0), out_shape=jax.ShapeDtypeStruct((T, NA_WIDTH), BF),
        compiler_params=_cparams(("parallel",)),
    )(qkv, qkv, qkv, tab)


def _na_bwd(qkv, tab, do):
    T = qkv.shape[0]
    rows = T // GRID_W
    n_pairs = NA_WIDTH // 128

    def body(q_ref, k_ref, v_ref, tab_ref, do_ref, dq_ref, dk_out, dv_out, dtab_ref, dk_ref, dv_ref):
        dk_ref[...] = jnp.zeros_like(dk_ref)
        dv_ref[...] = jnp.zeros_like(dv_ref)
        dtab_ref[...] = jnp.zeros_like(dtab_ref)

        def step(it, carry):
            U = NA_BWD_ROWS_PER_STEP
            geo = [_na_row_geometry(it * U + u, rows) for u in range(U)]
            q0s = [pl.multiple_of((it * U + u) * GRID_W, GRID_W) for u in range(U)]
            q2s = [jnp.concatenate(_split_pair(q_ref[pl.ds(q0, GRID_W), :] * Q_SCALE), axis=0) for q0 in q0s]
            do2s = [jnp.concatenate(_split_pair(do_ref[pl.ds(q0, GRID_W), :]), axis=0) for q0 in q0s]
            ss = [lax.dot_general(q2, k_ref[pl.ds(k0, NA_KEYS), :], _NT, preferred_element_type=F32) for q2, (k0, _) in zip(q2s, geo)]
            dps = [lax.dot_general(do2, v_ref[pl.ds(k0, NA_KEYS), :], _NT, preferred_element_type=F32) for do2, (k0, _) in zip(do2s, geo)]
            ps = [_softmax_rows(s + jnp.concatenate([tab_ref[0, base], tab_ref[1, base]], axis=0)) for s, (_, base) in zip(ss, geo)]
            dss = [p * (dp - jnp.sum(dp * p, axis=-1, keepdims=True)) for p, dp in zip(ps, dps)]
            dvs = [lax.dot_general(p.astype(BF), do2, _TN, preferred_element_type=F32) for p, do2 in zip(ps, do2s)]
            dsbs = [ds.astype(BF) for ds in dss]
            dqs = [jnp.dot(dsb, k_ref[pl.ds(k0, NA_KEYS), :], preferred_element_type=F32) for dsb, (k0, _) in zip(dsbs, geo)]
            dks = [lax.dot_general(dsb, q2, _TN, preferred_element_type=F32) for dsb, q2 in zip(dsbs, q2s)]
            for u in range(U):
                k0, base = geo[u]
                dtab_ref[0, base] += dss[u][:GRID_W]
                dtab_ref[1, base] += dss[u][GRID_W:]
                dq_ref[pl.ds(q0s[u], GRID_W), :] = (_join_pair(dqs[u][:GRID_W], dqs[u][GRID_W:]) * Q_SCALE).astype(dq_ref.dtype)
                dk_ref[pl.ds(k0, NA_KEYS), :] += dks[u]
                dv_ref[pl.ds(k0, NA_KEYS), :] += dvs[u]
            return carry

        lax.fori_loop(0, rows // NA_BWD_ROWS_PER_STEP, step, 0)
        dk_out[...] = dk_ref[...].astype(dk_out.dtype)
        dv_out[...] = dv_ref[...].astype(dv_out.dtype)

    def cols(first):
        return pl.BlockSpec((T, 128), lambda j: (0, first + j))

    tabs = pl.BlockSpec((2, NA_BASES, GRID_W, NA_KEYS), lambda j: (j, 0, 0, 0))
    wide = jax.ShapeDtypeStruct((T, NA_WIDTH), BF)
    return pl.pallas_call(
        body, name="na_bwd", grid=(n_pairs,),
        in_specs=[cols(0), cols(n_pairs), cols(2 * n_pairs), tabs, cols(0)],
        out_specs=[cols(0), cols(0), cols(0), tabs],
        out_shape=[wide, wide, wide, jax.ShapeDtypeStruct((NA_HEADS, NA_BASES, GRID_W, NA_KEYS), F32)],
        scratch_shapes=[pltpu.VMEM((T, 128), F32), pltpu.VMEM((T, 128), F32)],
        compiler_params=_cparams(("parallel",)),
    )(qkv, qkv, qkv, tab, do)


def _na_bias_table(rpb):
    H, n_rows, n_cols = rpb.shape

    def body(r_ref, tab_ref):
        q = lax.broadcasted_iota(jnp.int32, (GRID_W, 128), 0)
        kc = lax.broadcasted_iota(jnp.int32, (GRID_W, 128), 1)
        first = jnp.clip(q - NA_WIN_COLS // 2, 0, GRID_W - NA_WIN_COLS)
        valid = (kc >= first) & (kc < first + NA_WIN_COLS)
        toeplitz = []
        for ro in range(n_rows):
            row = jnp.broadcast_to(r_ref[pl.ds(ro, 1), :], (GRID_W, 128))
            shifted = pltpu.roll(pltpu.roll(row, 128 - (NA_WIN_COLS - 1), 1), 0, 1, stride=1, stride_axis=0)
            toeplitz.append(jnp.where(valid, shifted, NEG_INF))
        for base in range(NA_BASES):
            for j in range(NA_WIN_ROWS // 2):
                even, odd = toeplitz[base + 2 * j], toeplitz[base + 2 * j + 1]
                tab_ref[base, :, pl.ds(j * 128, 128)] = jnp.where(kc < GRID_W, even, pltpu.roll(odd, GRID_W, 1))

    padded = jnp.pad(rpb, ((0, 0), (0, 16 - n_rows), (0, 128 - n_cols)))
    return pl.pallas_call(
        body, name="na_bias_table", grid=(H,),
        in_specs=[pl.BlockSpec((None, 16, 128), lambda h: (h, 0, 0))],
        out_specs=pl.BlockSpec((None, NA_BASES, GRID_W, NA_KEYS), lambda h: (h, 0, 0, 0)),
        out_shape=jax.ShapeDtypeStruct((H, NA_BASES, GRID_W, NA_KEYS), F32),
        compiler_params=_cparams(("parallel",)),
    )(padded)


def _na_rpb_grad(dtab, after=()):
    H = dtab.shape[0]
    n_rows = 2 * NA_WIN_ROWS - 1
    n_cols = 2 * NA_WIN_COLS - 1

    def body(d_ref, *rest):
        o_ref = rest[-1]
        lane = lax.broadcasted_iota(jnp.int32, (GRID_W, 128), 1)
        low = lane < GRID_W
        flip = (lax.broadcasted_iota(jnp.int32, (GRID_W, GRID_W), 0) + lax.broadcasted_iota(jnp.int32, (GRID_W, GRID_W), 1)
                == GRID_W - 1).astype(BF)

        def reverse_rows(t):
            out = jnp.zeros_like(t)
            for _ in range(3):
                piece = t.astype(BF)
                out = out + jnp.dot(flip, piece, preferred_element_type=F32)
                t = t - piece.astype(F32)
            return out

        out_rows = []
        for ro in range(n_rows):
            acc = jnp.zeros((GRID_W, 128), F32)
            for base in range(NA_BASES):
                i = ro - base
                if not 0 <= i < NA_WIN_ROWS:
                    continue
                pair = d_ref[base, :, pl.ds((i // 2) * 128, 128)]
                if i % 2:
                    pair = pltpu.roll(pair, GRID_W, 1)
                acc = acc + jnp.where(low, pair, 0.0)
            skew = pltpu.roll(reverse_rows(acc), 0, 1, stride=1, stride_axis=0)
            diag = jnp.sum(skew, axis=0, keepdims=True)
            out_rows.append(pltpu.roll(jnp.broadcast_to(diag, (8, 128)), 128 - (GRID_W - NA_WIN_COLS), 1)[:1])
        out_rows.append(jnp.zeros((1, 128), F32))
        res = jnp.concatenate(out_rows, axis=0)
        o_ref[...] = jnp.where(lax.broadcasted_iota(jnp.int32, res.shape, 1) < n_cols, res, 0.0)

    return pl.pallas_call(
        body, name="na_rpb_grad", grid=(H,),
        in_specs=[pl.BlockSpec((None, NA_BASES, GRID_W, NA_KEYS), lambda h: (h, 0, 0, 0))] + [ANY] * len(after),
        out_specs=pl.BlockSpec((None, n_rows + 1, 128), lambda h: (h, 0, 0)),
        out_shape=jax.ShapeDtypeStruct((H, n_rows + 1, 128), F32),
        compiler_params=_cparams(("parallel",)),
    )(dtab, *after)


BAND_Q = 128
BAND_KEYS = BAND_Q + 2 * DIL_RADIUS


def _band_geometry(n, L):
    q0 = pl.multiple_of(n * BAND_Q, BAND_Q)
    k0 = pl.multiple_of(jnp.clip(q0 - DIL_RADIUS, 0, L - BAND_KEYS), DIL_RADIUS)
    qi = q0 + lax.broadcasted_iota(jnp.int32, (BAND_Q, BAND_KEYS), 0)
    kj = k0 + lax.broadcasted_iota(jnp.int32, (BAND_Q, BAND_KEYS), 1)
    return q0, k0, jnp.abs(qi - kj) <= DIL_RADIUS


DIL_PAIRS = DIL_OUT_WIDTH // 128


def _residue_shape(dil, T, dtype):
    return jax.ShapeDtypeStruct((DIL_PAIRS, dil, T // dil, 128), dtype)


def _residue_tile(dil, tm):
    return pl.BlockSpec((DIL_PAIRS, dil, tm // dil, 128), lambda i: (0, 0, i, 0))


def _to_natural(ref, scratch, dil, tm):
    tiles = []
    for pair in range(DIL_PAIRS):
        if dil == 1:
            tiles.append(ref[pair, 0].astype(F32))
            continue
        for r in range(dil):
            scratch[pl.ds(r, tm // dil, stride=dil), :] = ref[pair, r].astype(F32)
        tiles.append(scratch[...])
    return tiles


def _from_natural(tile, scratch, ref, pair, dil, tm):
    if dil == 1:
        ref[pair, 0] = tile.astype(ref.dtype)
        return
    scratch[...] = tile
    for r in range(dil):
        ref[pair, r] = scratch[pl.ds(r, tm // dil, stride=dil), :].astype(ref.dtype)


def _band_specs(group, T):
    dil = DIL_GROUPS[group][1]
    L = T // dil
    assert L % BAND_Q == 0 and L >= BAND_KEYS, (T, dil)
    per_residue = min(BAND_BLOCKS_PER_STEP, L // BAND_Q)
    residues = min(dil, BAND_BLOCKS_PER_STEP // per_residue)
    spec = pl.BlockSpec((None, residues, L, 128), lambda s: (s % DIL_PAIRS, s // DIL_PAIRS, 0, 0))
    return L, residues, per_residue, (dil // residues * DIL_PAIRS,), spec


BAND_BLOCKS_PER_STEP = 8


def _band_softmax(s, valid):
    s = jnp.where(valid, s, NEG_INF)
    m = jnp.max(s, axis=-1, keepdims=True)
    p = jnp.exp(s - m)
    l = jnp.sum(p, axis=-1, keepdims=True)
    return p / l, m + jnp.log(l)


def _band_fwd(q, k, v, group):
    T = q.shape[1] * q.shape[2]
    L, residues, U, grid, spec = _band_specs(group, T)

    def body(q_ref, k_ref, v_ref, o_ref, lse_ref):
        def step(it, carry):
            geo = [(r, *_band_geometry(it * U + u, L)) for r in range(residues) for u in range(U)]
            ss = [lax.dot_general(jnp.concatenate(_split_pair(q_ref[r, pl.ds(q0, BAND_Q), :]), axis=0),
                                  k_ref[r, pl.ds(k0, BAND_KEYS), :], _NT, preferred_element_type=F32) for r, q0, k0, _ in geo]
            pls = [_band_softmax(s, jnp.concatenate([valid, valid], axis=0)) for s, (_, _, _, valid) in zip(ss, geo)]
            os = [jnp.dot(p.astype(BF), v_ref[r, pl.ds(k0, BAND_KEYS), :], preferred_element_type=F32)
                  for (p, _), (r, _, k0, _) in zip(pls, geo)]
            for (r, q0, _, _), o2, (_, lse) in zip(geo, os, pls):
                o_ref[r, pl.ds(q0, BAND_Q), :] = _join_pair(o2[:BAND_Q], o2[BAND_Q:])
                lse2 = jnp.broadcast_to(lse, (2 * BAND_Q, 128))
                lse_ref[r, pl.ds(q0, BAND_Q), :] = _join_pair(lse2[:BAND_Q], lse2[BAND_Q:])
            return carry

        lax.fori_loop(0, L // (BAND_Q * U), step, 0)

    res = _residue_shape(DIL_GROUPS[group][1], T, F32)
    return pl.pallas_call(
        body, name=f"band_fwd_g{group}", grid=grid,
        in_specs=[spec] * 3, out_specs=[spec] * 2, out_shape=[res, res],
        compiler_params=_cparams(("parallel",)),
    )(q, k, v)


def _band_bwd(q, k, v, do, dlse, group):
    T = q.shape[1] * q.shape[2]
    L, residues, U, grid, spec = _band_specs(group, T)

    def body(q_ref, k_ref, v_ref, do_ref, dlse_ref, dq_ref, dk_ref, dv_ref):
        dk_ref[...] = jnp.zeros_like(dk_ref)
        dv_ref[...] = jnp.zeros_like(dv_ref)

        def step(it, carry):
            geo = [(r, *_band_geometry(it * U + u, L)) for r in range(residues) for u in range(U)]
            q2s = [jnp.concatenate(_split_pair(q_ref[r, pl.ds(q0, BAND_Q), :]), axis=0) for r, q0, _, _ in geo]
            do2s = [jnp.concatenate(_split_pair(do_ref[r, pl.ds(q0, BAND_Q), :]), axis=0) for r, q0, _, _ in geo]
            ss = [lax.dot_general(q2, k_ref[r, pl.ds(k0, BAND_KEYS), :], _NT, preferred_element_type=F32)
                  for q2, (r, _, k0, _) in zip(q2s, geo)]
            dps = [lax.dot_general(do2, v_ref[r, pl.ds(k0, BAND_KEYS), :], _NT, preferred_element_type=F32)
                   for do2, (r, _, k0, _) in zip(do2s, geo)]
            ps = [_band_softmax(s, jnp.concatenate([valid, valid], axis=0))[0] for s, (_, _, _, valid) in zip(ss, geo)]
            dss = []
            for p, dp, (r, q0, _, _) in zip(ps, dps, geo):
                dl = dlse_ref[r, pl.ds(q0, BAND_Q), :]
                dl2 = jnp.concatenate([dl[:, :1], dl[:, HEAD_DIM:HEAD_DIM + 1]], axis=0)
                dss.append(p * (dp - jnp.sum(dp * p, axis=-1, keepdims=True) + dl2))
            dvs = [lax.dot_general(p.astype(BF), do2, _TN, preferred_element_type=F32) for p, do2 in zip(ps, do2s)]
            dsbs = [ds.astype(BF) for ds in dss]
            dqs = [jnp.dot(dsb, k_ref[r, pl.ds(k0, BAND_KEYS), :], preferred_element_type=F32) for dsb, (r, _, k0, _) in zip(dsbs, geo)]
            dks = [lax.dot_general(dsb, q2, _TN, preferred_element_type=F32) for dsb, q2 in zip(dsbs, q2s)]
            for u, (r, q0, k0, _) in enumerate(geo):
                dq_ref[r, pl.ds(q0, BAND_Q), :] = _join_pair(dqs[u][:BAND_Q], dqs[u][BAND_Q:])
                dk_ref[r, pl.ds(k0, BAND_KEYS), :] += dks[u]
                dv_ref[r, pl.ds(k0, BAND_KEYS), :] += dvs[u]
            return carry

        lax.fori_loop(0, L // (BAND_Q * U), step, 0)

    res = _residue_shape(DIL_GROUPS[group][1], T, F32)
    return pl.pallas_call(
        body, name=f"band_bwd_g{group}", grid=grid,
        in_specs=[spec] * 5, out_specs=[spec] * 3, out_shape=[res] * 3,
        compiler_params=_cparams(("parallel",)),
    )(q, k, v, do, dlse)


def _head_sums(t):
    head = lax.broadcasted_iota(jnp.int32, t.shape, 1) // HEAD_DIM
    out = jnp.zeros_like(t)
    for h in range(t.shape[1] // HEAD_DIM):
        mine = head == h
        out = jnp.where(mine, jnp.sum(jnp.where(mine, t, 0.0), axis=-1, keepdims=True), out)
    return out


def _dil_merge_fwd(os, lses, T, tm):
    G = len(DIL_GROUPS)
    W = DIL_OUT_WIDTH
    dils = [d for _, d in DIL_GROUPS]

    def body(*refs):
        o_refs, lse_refs = refs[:G], refs[G:2 * G]
        y_ref, w_refs, on_refs, scratch = refs[2 * G], refs[2 * G + 1:3 * G + 1], refs[3 * G + 1:4 * G + 1], refs[-1]
        o = [jnp.concatenate(_to_natural(r, scratch, d, tm), axis=1) for r, d in zip(o_refs, dils)]
        ls = [jnp.concatenate(_to_natural(r, scratch, d, tm), axis=1) for r, d in zip(lse_refs, dils)]
        m = functools.reduce(jnp.maximum, ls)
        es = [jnp.exp(l - m) for l in ls]
        tot = functools.reduce(jnp.add, es)
        ws = [e / tot for e in es]
        y_ref[...] = functools.reduce(jnp.add, [w * t for w, t in zip(ws, o)]).astype(y_ref.dtype)
        for g in range(G):
            w_refs[g][...] = ws[g]
            on_refs[g][...] = o[g]

    nat = pl.BlockSpec((tm, W), lambda i: (i, 0))
    res = pl.pallas_call(
        body, name="dil_merge_fwd", grid=(T // tm,),
        in_specs=[_residue_tile(d, tm) for d in dils] * 2,
        out_specs=[nat] * (2 * G + 1),
        out_shape=[jax.ShapeDtypeStruct((T, W), BF)] + [jax.ShapeDtypeStruct((T, W), F32)] * (2 * G),
        scratch_shapes=[pltpu.VMEM((tm, 128), F32)],
        compiler_params=_cparams(("parallel",)),
    )(*os, *lses)
    return res[0], res[1:G + 1], res[G + 1:]


def _dil_merge_bwd(dy, os, ws, tm, after=()):
    G = len(DIL_GROUPS)
    T, W = dy.shape
    dils = [d for _, d in DIL_GROUPS]
    n_after = len(after)

    def body(*refs):
        dyt = refs[0][...]
        o, w = [r[...] for r in refs[1:G + 1]], [r[...] for r in refs[G + 1:2 * G + 1]]
        refs = refs[2 * G + 1 + n_after:]
        do_refs, dlse_refs, scratch = refs[:G], refs[G:2 * G], refs[-1]
        dws = [_head_sums(dyt * t) for t in o]
        mean = functools.reduce(jnp.add, [a * b for a, b in zip(w, dws)])
        for g, d in enumerate(dils):
            do, dlse = w[g] * dyt, w[g] * (dws[g] - mean)
            for pair in range(DIL_PAIRS):
                cols = slice(pair * 128, (pair + 1) * 128)
                _from_natural(do[:, cols], scratch, do_refs[g], pair, d, tm)
                _from_natural(dlse[:, cols], scratch, dlse_refs[g], pair, d, tm)

    nat = pl.BlockSpec((tm, W), lambda i: (i, 0))
    res = pl.pallas_call(
        body, name="dil_merge_bwd", grid=(T // tm,),
        in_specs=[nat] * (2 * G + 1) + [ANY] * n_after,
        out_specs=[_residue_tile(d, tm) for d in dils] * 2,
        out_shape=[_residue_shape(d, T, BF) for d in dils] + [_residue_shape(d, T, F32) for d in dils],
        scratch_shapes=[pltpu.VMEM((tm, 128), F32)],
        compiler_params=_cparams(("parallel",)),
    )(dy, *os, *ws, *after)
    return res[:G], res[G:]


def _qkv_prep(z, cos2, sin_signed, tm):
    T = z.shape[0]
    G = len(DIL_GROUPS)
    dils = [d for _, d in DIL_GROUPS]
    n_dil_blocks = 3 * DIL_WIDTH // 128

    def body(*refs):
        blocks = refs[:n_dil_blocks]
        cos_ref, sin_ref = refs[n_dil_blocks], refs[1 + n_dil_blocks]
        outs = refs[2 + n_dil_blocks:]
        for part in range(3):
            for g, d in enumerate(dils):
                out = outs[g * 3 + part]
                for pair in range(DIL_PAIRS):
                    blk = blocks[part * (DIL_WIDTH // 128) + g * DIL_PAIRS + pair]
                    for r in range(d):
                        rows = pl.ds(r, tm // d, stride=d) if d > 1 else slice(None)
                        x = blk[rows, :]
                        if part < 2:
                            x = _rope(x, cos_ref[rows, :], sin_ref[rows, :])
                        if part == 0:
                            x = x * Q_SCALE
                        out[pair, r] = x.astype(out.dtype)

    lane_block = [pl.BlockSpec((tm, 128), functools.partial(lambda b, i: (i, b), b)) for b in range(n_dil_blocks)]
    tab = pl.BlockSpec((tm, 128), lambda i: (i, 0))
    res = pl.pallas_call(
        body, name="qkv_prep", grid=(T // tm,),
        in_specs=lane_block + [tab, tab],
        out_specs=[_residue_tile(d, tm) for d in dils for _ in range(3)],
        out_shape=[_residue_shape(d, T, BF) for d in dils for _ in range(3)],
        compiler_params=_cparams(("parallel",)),
    )(*[z] * n_dil_blocks, cos2, sin_signed)
    return [res[3 * g:3 + 3 * g] for g in range(G)]


def _qkv_unprep(d_na, d_dil, cos2, sin_signed, tm, after=()):
    T = d_na[0].shape[0]
    G = len(DIL_GROUPS)
    dils = [d for _, d in DIL_GROUPS]
    n_after = len(after)

    def body(*refs):
        dq, dk, dv = (r[...] for r in refs[:3])
        res_refs = refs[3:3 + 3 * G]
        cs, sn = refs[3 + 3 * G][...], refs[4 + 3 * G][...]
        out, scratch = refs[5 + 3 * G + n_after], refs[-1]
        cols = [dq, dk, dv]
        for part in range(3):
            for g, d in enumerate(dils):
                for x in _to_natural(res_refs[g * 3 + part], scratch, d, tm):
                    if part < 2:
                        x = _rope(x, cs, -sn)
                    cols.append((x * Q_SCALE if part == 0 else x).astype(out.dtype))
        out[...] = jnp.concatenate(cols, axis=1)

    wide = pl.BlockSpec((tm, NA_WIDTH), lambda i: (i, 0))
    tab = pl.BlockSpec((tm, 128), lambda i: (i, 0))
    return pl.pallas_call(
        body, name="qkv_unprep", grid=(T // tm,),
        in_specs=[wide] * 3 + [_residue_tile(d, tm) for d in dils for _ in range(3)] + [tab, tab] + [ANY] * n_after,
        out_specs=pl.BlockSpec((tm, QKV_WIDTH), lambda i: (i, 0)),
        out_shape=jax.ShapeDtypeStruct((T, QKV_WIDTH), BF),
        scratch_shapes=[pltpu.VMEM((tm, 128), F32)],
        compiler_params=_cparams(("parallel",)),
    )(*d_na, *[t for g in range(G) for t in d_dil[g]], cos2, sin_signed, *after)


def _rope_tables(positions):
    half = HEAD_DIM // 2
    inv_freq = ROPE_THETA ** (-jnp.arange(half, dtype=F32) / half)
    ang = positions.astype(F32)[:, None] * inv_freq
    cos, sin = jnp.cos(ang), jnp.sin(ang)
    return jnp.tile(jnp.concatenate([cos, cos], axis=1), (1, 2)), jnp.tile(jnp.concatenate([-sin, sin], axis=1), (1, 2))


def _pack_rows(t):
    return t.reshape(-1, PACK_W)


def _me():
    return lax.axis_index("x"), lax.axis_index("y"), lax.axis_index("c")


def _other_chips(x, y):
    return [(1 - x, y), (x, 1 - y), (1 - x, 1 - y)]


def _pair_sum(g, got, tm):
    S, R, W = g.shape
    half = R // 2
    nb = half // tm

    def body(pos_ref, g_ref, got_ref, own_ref, ob_ref):
        tot = g_ref[...] + got_ref[...]
        ob_ref[...] = tot.astype(ob_ref.dtype)

        @pl.when(pl.program_id(1) == pos_ref[1])
        def _():
            own_ref[...] = tot

    tile = pl.BlockSpec((None, tm, W), lambda i, s, pos_ref: (s, i, 0))
    c, chip = lax.axis_index("c"), 2 * lax.axis_index("x") + lax.axis_index("y")
    return pl.pallas_call(
        body, name="pair_sum",
        grid_spec=pltpu.PrefetchScalarGridSpec(
            num_scalar_prefetch=1, grid=(nb, S),
            in_specs=[pl.BlockSpec((None, tm, W), lambda i, s, pos_ref: (s, pos_ref[0] * nb + i, 0)), tile],
            out_specs=[pl.BlockSpec((tm, W), lambda i, s, pos_ref: (i, 0)), tile]),
        out_shape=[jax.ShapeDtypeStruct((half, W), F32), jax.ShapeDtypeStruct((S, half, W), BF)],
        compiler_params=_cparams(("parallel", "arbitrary")),
    )(jnp.stack([c, chip]).astype(jnp.int32), g, got)


def _chip_sum(own, others, tm):
    n, h, W = others.shape
    nb = h // tm

    def body(c_ref, own_ref, p_ref, o_ref):
        o_ref[...] = ((own_ref[...] + p_ref[0].astype(F32)) + p_ref[1].astype(F32)) + p_ref[2].astype(F32)

    return pl.pallas_call(
        body, name="chip_sum",
        grid_spec=pltpu.PrefetchScalarGridSpec(
            num_scalar_prefetch=1, grid=(nb,),
            in_specs=[pl.BlockSpec((tm, W), lambda i, c_ref: (i, 0)), pl.BlockSpec((n, tm, W), lambda i, c_ref: (0, i, 0))],
            out_specs=pl.BlockSpec((tm, W), lambda i, c_ref: (c_ref[0] * nb + i, 0))),
        out_shape=jax.ShapeDtypeStruct((2 * h, W), F32),
        compiler_params=_cparams(("parallel",)),
    )(lax.axis_index("c").reshape(1).astype(jnp.int32), own, others)


def _join_halves(shard):
    h = shard.shape[0] // 2

    def body(in_ref, out_ref, send_sem, recv_sem):
        x, y, c = _me()
        cp = pltpu.make_async_remote_copy(
            src_ref=in_ref.at[pl.ds(c * h, h), :], dst_ref=out_ref.at[pl.ds(c * h, h), :],
            send_sem=send_sem, recv_sem=recv_sem, device_id=(x, y, 1 - c), device_id_type=MESH)
        cp.start()
        pltpu.make_async_remote_copy(
            src_ref=in_ref.at[pl.ds(c * h, h), :], dst_ref=out_ref.at[pl.ds((1 - c) * h, h), :],
            send_sem=send_sem, recv_sem=recv_sem, device_id=(x, y, 1 - c), device_id_type=MESH).wait_recv()
        cp.wait_send()

    return pl.pallas_call(
        body, name="join_halves", in_specs=[ANY], out_specs=ANY,
        out_shape=jax.ShapeDtypeStruct(shard.shape, shard.dtype), input_output_aliases={0: 0},
        scratch_shapes=[pltpu.SemaphoreType.DMA, pltpu.SemaphoreType.DMA],
    )(shard)


def _allreduce_small(s, after=()):
    R, W = s.shape
    n_after = len(after)

    def body(s_ref, *rest):
        o_ref, buf, send_sems, recv_sems = rest[n_after:]
        x, y, c = _me()
        me = 4 * x + 2 * y + c
        buf[me] = s_ref[...]
        peers = [((x + fx) % 2, (y + fy) % 2, (c + fc) % 2) for fx in range(2) for fy in range(2) for fc in range(2)][1:]
        sends = [pltpu.make_async_remote_copy(
            src_ref=s_ref, dst_ref=buf.at[me], send_sem=send_sems.at[k], recv_sem=recv_sems.at[k],
            device_id=peer, device_id_type=MESH) for k, peer in enumerate(peers)]
        for cp in sends:
            cp.start()
        for k, peer in enumerate(peers):
            pltpu.make_async_remote_copy(
                src_ref=s_ref, dst_ref=buf.at[4 * peer[0] + 2 * peer[1] + peer[2]], send_sem=send_sems.at[k],
                recv_sem=recv_sems.at[k], device_id=peer, device_id_type=MESH).wait_recv()
        for cp in sends:
            cp.wait_send()
        total = buf[0]
        for d in range(1, N_DEV):
            total = total + buf[d]
        o_ref[...] = total

    return pl.pallas_call(
        body, name="allreduce_small",
        in_specs=[pl.BlockSpec(memory_space=pltpu.VMEM)] + [ANY] * n_after, out_specs=pl.BlockSpec(memory_space=pltpu.VMEM),
        out_shape=jax.ShapeDtypeStruct((R, W), F32),
        scratch_shapes=[pltpu.VMEM((N_DEV, R, W), F32), pltpu.SemaphoreType.DMA((N_DEV - 1,)), pltpu.SemaphoreType.DMA((N_DEV - 1,))],
    )(s, *after)


HBM_SPEC = pl.BlockSpec(memory_space=pltpu.HBM)
SEM_SPEC = pl.BlockSpec(memory_space=pltpu.SEMAPHORE)
DATAFLOW = pltpu.SideEffectType.DATAFLOW_SIDE_EFFECTING


class _InFlight(NamedTuple):
    sems: tuple
    src: jax.Array
    land: jax.Array
    token: jax.Array


def _split_start(name, src, land_shape, land_dtype, n, copies, after=()):
    n_after = len(after)

    def body(src_ref, land_ref, *rest):
        rest = rest[n_after:]
        sems, token = rest[:2 * n], rest[-1]
        for k, (s, d, peer) in enumerate(copies(src_ref, land_ref)):
            pltpu.make_async_remote_copy(src_ref=s, dst_ref=d, send_sem=sems[k], recv_sem=sems[n + k],
                                         device_id=peer, device_id_type=MESH).start()
        token[...] = jnp.zeros_like(token)

    outs = pl.pallas_call(
        body, name=name,
        out_shape=(*[pltpu.SemaphoreType.DMA(())] * (2 * n), pltpu.HBM(src.shape, src.dtype), pltpu.HBM(land_shape, land_dtype),
                   jax.ShapeDtypeStruct((8, 128), F32)),
        in_specs=(HBM_SPEC, HBM_SPEC, *[ANY] * n_after),
        out_specs=(*[SEM_SPEC] * (2 * n), HBM_SPEC, HBM_SPEC, pl.BlockSpec(memory_space=pltpu.VMEM)),
        input_output_aliases={0: 2 * n, 1: 2 * n + 1},
        compiler_params=pltpu.CompilerParams(has_side_effects=DATAFLOW),
    )(pltpu.with_memory_space_constraint(src, pltpu.HBM), pltpu.with_memory_space_constraint(lax.empty(land_shape, land_dtype), pltpu.HBM),
      *after)
    return _InFlight(tuple(outs[:2 * n]), outs[2 * n], outs[2 * n + 1], outs[2 * n + 2])


def _split_wait(name, flight, after, n, copies):
    after = after if isinstance(after, tuple) else (after,)

    def body(src_ref, land_ref, *rest):
        sems = rest[:2 * n]
        for k, (s, d, peer) in enumerate(copies(src_ref, land_ref)):
            cp = pltpu.make_async_remote_copy(src_ref=s, dst_ref=d, send_sem=sems[k], recv_sem=sems[n + k],
                                              device_id=peer, device_id_type=MESH)
            cp.wait_send()
            cp.wait_recv()

    return pl.pallas_call(
        body, name=name,
        out_shape=(pltpu.HBM(flight.src.shape, flight.src.dtype), pltpu.HBM(flight.land.shape, flight.land.dtype)),
        in_specs=(HBM_SPEC, HBM_SPEC, *[SEM_SPEC] * (2 * n), *[ANY] * len(after)),
        out_specs=(HBM_SPEC, HBM_SPEC), input_output_aliases={0: 0, 1: 1},
        compiler_params=pltpu.CompilerParams(has_side_effects=DATAFLOW),
    )(flight.src, flight.land, *flight.sems, *after)


def _gather_copies(src_ref, land_ref):
    x, y, c = _me()
    return [(src_ref, land_ref.at[2 * x + y], (*chip, c)) for chip in _other_chips(x, y)]


def _gather_start(packed, tag, after=()):
    return _split_start(f"gather_start_{tag}", packed, (N_CHIPS, *packed.shape), packed.dtype, 3, _gather_copies, after)


def _gather_wait(flight, after, tag):
    src, others = _split_wait(f"gather_wait_{tag}", flight, after, 3, _gather_copies)
    return lax.dynamic_update_slice(others, src[None], (2 * lax.axis_index("x") + lax.axis_index("y"), 0, 0))


def _across_copies(src_ref, land_ref):
    x, y, c = _me()
    half = src_ref.shape[0] // 2
    rows = pl.ds(c * half, half)
    return [(src_ref.at[rows, :], land_ref.at[2 * x + y, rows, :], (*chip, c)) for chip in _other_chips(x, y)]


def _to_sibling_copies(all_ref, unused_ref):
    x, y, c = _me()
    half = all_ref.shape[1] // 2
    places = [all_ref.at[2 * chip[0] + chip[1], pl.ds(c * half, half), :] for chip in _other_chips(x, y)]
    return [(place, place, (x, y, 1 - c)) for place in places]


def _gather_halves_start(shard, tag, after=()):
    return _split_start(f"gather_{tag}_across_start", shard, (N_CHIPS, *shard.shape), shard.dtype, 3, _across_copies, after)


def _gather_halves_relay(flight, after, tag):
    shard, landed = _split_wait(f"gather_{tag}_across_wait", flight, after, 3, _across_copies)
    return shard, _split_start(f"gather_{tag}_sibling_start", landed, (8, 128), landed.dtype, 3, _to_sibling_copies)


def _gather_halves_finish(shard, relay, after, tag):
    others = _split_wait(f"gather_{tag}_sibling_wait", relay, after, 3, _to_sibling_copies)[0]
    return lax.dynamic_update_slice(others, shard[None], (2 * lax.axis_index("x") + lax.axis_index("y"), 0, 0))


def _assemble_w_in(shards, tm):
    S, R, C = shards.shape
    n_gates = 2 * D_MODEL

    def body(s_ref, w_ref, g_ref):
        full = jnp.concatenate([s_ref[s] for s in range(S)], axis=1)
        w_ref[...] = full
        g_ref[...] = full[:, S * C - n_gates:]

    return pl.pallas_call(
        body, name="assemble_w_in", grid=(R // tm,),
        in_specs=[pl.BlockSpec((S, tm, C), lambda i: (0, i, 0))],
        out_specs=[pl.BlockSpec((tm, S * C), lambda i: (i, 0)), pl.BlockSpec((tm, n_gates), lambda i: (i, 0))],
        out_shape=[jax.ShapeDtypeStruct((R, S * C), shards.dtype), jax.ShapeDtypeStruct((R, n_gates), shards.dtype)],
        compiler_params=_cparams(("parallel",)),
    )(shards)


def _swap_copies(src_ref, land_ref):
    x, y, c = _me()
    half = land_ref.shape[1]
    return [(src_ref.at[:, pl.ds((1 - c) * half, half), :], land_ref, (x, y, 1 - c))]


def _swap_start(g, tag):
    S, R, W = g.shape
    return _split_start(f"swap_halves_start_{tag}", g, (S, R // 2, W), g.dtype, 1, _swap_copies)


def _swap_wait(flight, after, tag):
    return _split_wait(f"swap_halves_wait_{tag}", flight, after, 1, _swap_copies)


def _scatter_copies(src_ref, land_ref):
    x, y, c = _me()
    return [(src_ref.at[2 * chip[0] + chip[1]], land_ref.at[j], (*chip, c)) for j, chip in enumerate(_other_chips(x, y))]


def _scatter_start(part, tag):
    S, h, W = part.shape
    return _split_start(f"scatter_chips_start_{tag}", part, (S - 1, h, W), part.dtype, 3, _scatter_copies)


def _scatter_wait(flight, after, tag):
    return _split_wait(f"scatter_chips_wait_{tag}", flight, after, 3, _scatter_copies)[1]


def _join_copies(shard_ref, unused_ref):
    x, y, c = _me()
    h = shard_ref.shape[0] // 2
    rows = shard_ref.at[pl.ds(c * h, h), :]
    return [(rows, rows, (x, y, 1 - c))]


def _join_start(shard):
    return _split_start("join_halves_start", shard, (8, 128), shard.dtype, 1, _join_copies)


def _join_wait(flight, after):
    return _split_wait("join_halves_wait", flight, after, 1, _join_copies)[0]


def _adamw(name, g, g_row0, w, m, v):
    _, R, C = w.shape
    tm = next(cand for cand in (368, 256, 128, 64, 32, 16, 8) if R % cand == 0)
    assert g_row0 % tm == 0 and g.shape[1] == C

    def body(g_ref, w_ref, m_ref, v_ref, go_ref, d_ref, mo_ref, vo_ref):
        gt = g_ref[...]
        mt = ADAM_B1 * m_ref[...] + (1.0 - ADAM_B1) * gt
        vt = ADAM_B2 * v_ref[...] + (1.0 - ADAM_B2) * jnp.square(gt)
        m_hat = mt / (1.0 - ADAM_B1 ** ADAM_STEP)
        v_hat = vt / (1.0 - ADAM_B2 ** ADAM_STEP)
        go_ref[...] = gt
        d_ref[...] = -ADAM_LR * (m_hat / (jnp.sqrt(v_hat) + ADAM_EPS) + ADAM_WD * w_ref[...])
        mo_ref[...] = mt
        vo_ref[...] = vt

    state = pl.BlockSpec((None, tm, C), lambda i: (0, i, 0))
    return pl.pallas_call(
        body, name=name, grid=(R // tm,),
        in_specs=[pl.BlockSpec((tm, C), lambda i: (g_row0 // tm + i, 0)), state, state, state],
        out_specs=[state] * 4, out_shape=[jax.ShapeDtypeStruct((1, R, C), F32)] * 4,
        compiler_params=_cparams(("parallel",)),
    )(g, w, m, v)


def _unpack_weights(gathered, names):
    S = gathered.shape[0]
    shard_shapes = {"w_in": (D_MODEL, (QKV_WIDTH + 2 * D_MODEL) // S), "w_branch_na": (NA_WIDTH, D_MODEL // S),
                    "w_branch_dil": (DIL_OUT_WIDTH, D_MODEL // S), "w_out": (D_MODEL // S, D_MODEL),
                    "w_up": (D_MODEL, D_FF // S), "w_down": (D_FF // S, D_MODEL),
                    "w_ple_gate": (D_MODEL // S, D_MODEL), "w_ple_proj": (PLE_DIM, D_MODEL // S)}
    col_sharded = {"w_in", "w_branch_na", "w_branch_dil", "w_up", "w_ple_proj"}
    out, r0 = {}, 0
    for name in names:
        rows, cols = shard_shapes[name]
        n = rows * cols // PACK_W
        t = gathered[:, r0:r0 + n, :].reshape(S, rows, cols)
        r0 += n
        out[name] = t.transpose(1, 0, 2).reshape(rows, S * cols) if name in col_sharded else t.reshape(S * rows, cols)
    return out


def kernel(x, p, positions, g_mix, w_in, rpb, w_branch_na, w_branch_dil, w_out, g_mlp, w_up, w_down, g_ple, w_ple_gate, w_ple_proj, g_final, loss_target, m_g_mix, m_w_in, m_rpb, m_w_branch_na, m_w_branch_dil, m_w_out, m_g_mlp, m_w_up, m_w_down, m_g_ple, m_w_ple_gate, m_w_ple_proj, m_g_final, v_g_mix, v_w_in, v_rpb, v_w_branch_na, v_w_branch_dil, v_w_out, v_g_mlp, v_w_up, v_w_down, v_g_ple, v_w_ple_gate, v_w_ple_proj, v_g_final):
    shards = {"w_in": w_in[0], "w_branch_na": w_branch_na[0], "w_branch_dil": w_branch_dil[0], "w_out": w_out[0],
              "w_up": w_up[0], "w_down": w_down[0], "w_ple_gate": w_ple_gate[0], "w_ple_proj": w_ple_proj[0]}
    params = {"w_in": w_in, "w_branch_na": w_branch_na, "w_branch_dil": w_branch_dil, "w_out": w_out, "w_up": w_up,
              "w_down": w_down, "w_ple_gate": w_ple_gate, "w_ple_proj": w_ple_proj,
              "m_w_in": m_w_in, "m_w_branch_na": m_w_branch_na, "m_w_branch_dil": m_w_branch_dil, "m_w_out": m_w_out,
              "m_w_up": m_w_up, "m_w_down": m_w_down, "m_w_ple_gate": m_w_ple_gate, "m_w_ple_proj": m_w_ple_proj,
              "v_w_in": v_w_in, "v_w_branch_na": v_w_branch_na, "v_w_branch_dil": v_w_branch_dil, "v_w_out": v_w_out,
              "v_w_up": v_w_up, "v_w_down": v_w_down, "v_w_ple_gate": v_w_ple_gate, "v_w_ple_proj": v_w_ple_proj}

    xs, ps, tgt = x[0], p[0, 0], loss_target[0]
    T = xs.shape[0]
    TM = 512
    gm, gl, gp, gf = g_mix, g_mlp, g_ple, g_final.reshape(1, D_MODEL)

    across = _gather_halves_start(shards["w_in"].astype(BF), "in")
    a = _rowwise("norm_mix", lambda h, g: h * _rms(h) * g, T, TM, [_row(xs, TM), _full(gm)], [(D_MODEL, BF)],
                 after=(across.token,))
    cos2, sin_signed = _rope_tables(positions[0])
    tab = _na_bias_table(rpb[0])
    packed_mix = jnp.concatenate([_pack_rows(shards[n].astype(BF)) for n in GATHER_MIX], axis=0)
    packed_mlp = jnp.concatenate([_pack_rows(shards[n].astype(BF)) for n in GATHER_MLP], axis=0)
    w_in_shard, w_in_relay = _gather_halves_relay(across, (a, tab, cos2, sin_signed, packed_mix, packed_mlp), "in")
    w_in_all = _gather_halves_finish(w_in_shard, w_in_relay, w_in_relay.token, "in")
    w_in_full, w_gates = _assemble_w_in(w_in_all, 256)
    W = {"w_in": w_in_full}
    mix_flight = _gather_start(packed_mix, "mix", after=(w_in_all,))
    mlp_across = _gather_halves_start(packed_mlp, "mlp", after=(mix_flight.token,))

    n3 = 3 * NA_WIDTH
    qkv = _mm("in_na", a, W["w_in"], "nn", 1024, 768, 1024, [BF], after=(mlp_across.token,),
              b_view=(n3, (D_MODEL, 768), lambda j, k: (k, j)))
    z_dil = _mm("in_dil", a, W["w_in"], "nn", 1024, 768, 1024, [F32], after=(mlp_across.token,),
                b_view=(3 * DIL_WIDTH, (D_MODEL, 768), lambda j, k: (k, n3 // 768 + j)))
    z_gates = _mm("in_gates", a, w_gates, "nn", 1024,1024, 1024, [BF], after=(mlp_across.token,))

    dil_ops = _qkv_prep(z_dil, cos2, sin_signed, TM)
    y_na = _na_fwd(qkv, tab)
    band = [_band_fwd(*dil_ops[g], g) for g in range(len(DIL_GROUPS))]
    y_dil, w_grp, o_nat = _dil_merge_fwd([b[0] for b in band], [b[1] for b in band], T, TM)

    W.update(_unpack_weights(_gather_wait(mix_flight, y_dil, "mix"), GATHER_MIX))
    mlp_shard, mlp_relay = _gather_halves_relay(mlp_across, y_dil, "mlp")
    u_na = _mm("branch_na", y_na, W["w_branch_na"], "nn", 1024,1024, 512, [BF], after=(mlp_relay.token,))
    u_dil = _mm("branch_dil", y_dil, W["w_branch_dil"], "nn", 1024,1024, 256, [BF])
    mixed = _rowwise(
        "gate_mix", lambda gn, gd, un, ud: _sigmoid(gn.astype(F32)) * un.astype(F32) + _sigmoid(gd.astype(F32)) * ud.astype(F32), T, TM,
        [_row(z_gates, TM, 0, D_MODEL), _row(z_gates, TM, 1, D_MODEL), _row(u_na, TM), _row(u_dil, TM)], [(D_MODEL, BF)])
    def add_norm(d, h, g):
        h = h + d
        return h, h * _rms(h) * g

    h1, cn = _mm("out_proj", mixed, W["w_out"], "nn", 512, 1024, 1024, [F32, BF], epilogue=add_norm, extras=(xs,), consts=(gl,))
    mlp_all = _gather_halves_finish(mlp_shard, mlp_relay, cn, "mlp")
    W.update({n: t for n, t in _unpack_weights(mlp_all, GATHER_MLP).items() if n.startswith("w_ple")})
    chip_block = (None, D_MODEL, PACK_W)
    up, act = _mm("mlp_up", cn, mlp_all, "nn", 1024,1024, 1024, [BF, BF],
                  epilogue=lambda acc: (acc, jnp.square(jnp.maximum(acc, 0.0))), b_view=(D_FF, chip_block, lambda j, k: (j, 0, 0)))
    h2, en = _mm("mlp_down", act, mlp_all, "nn", 1024, 1024, 1024, [F32, BF], epilogue=add_norm, extras=(h1,), consts=(gp,),
                 b_view=(D_MODEL, chip_block, lambda j, k: (k, 1, 0)))
    pp = _mm("ple_proj", ps, W["w_ple_proj"], "nn", 1024,1024, 256, [F32])

    def head(gtt, h2t, ppt, tg, g):
        sg = _sigmoid(gtt)
        h3 = h2t + sg * ppt
        yo = h3 * _rms(h3) * g
        diff = yo - tg
        loss = 0.5 * jnp.sum(jnp.mean(jnp.square(diff), axis=-1, keepdims=True), axis=0, keepdims=True)
        dh3, dg = _rms_bwd(diff * (1.0 / D_MODEL), h3, g)
        return dh3, dh3 * ppt * sg * (1.0 - sg), dh3 * sg, jnp.broadcast_to(loss, (1, 128)), dg

    dh3, d_gt, d_pp, loss_part, dg_final = _mm(
        "ple_gate_loss_head", en, W["w_ple_gate"], "nn", 512, 1024, 1024, [F32, BF, BF], epilogue=head,
        extras=(h2, pp, tgt), consts=(gf,), sums=[128, D_MODEL])

    early_shapes = {n: shards[n].shape for n in REDUCE_EARLY}
    early_rows = sum(r * c for r, c in early_shapes.values()) // PACK_W
    shard_rows = D_MODEL // N_CHIPS
    early_buf = _mm("g_ple_gate", en, d_gt, "tn", 1024, 1024, 1024, [F32],
                    into=(jax.ShapeDtypeStruct((N_CHIPS, early_rows, PACK_W), F32), (N_CHIPS, shard_rows, PACK_W),
                          lambda i, j: (0, 2 * D_MODEL // shard_rows, 0)))
    g_ple_proj = _mm("g_ple_proj", ps, d_pp, "tn", 256, 1024, 1024,[F32])

    def add_norm_bwd(dn, dh_out, h, g):
        dh, dg = _rms_bwd(dn, h, g)
        dh = dh_out + dh
        return dh, dh, dg

    dh2, dh2_b, dg_ple = _mm("d_ple_gate", d_gt, W["w_ple_gate"], "nt", 512, 1024, 1024, [F32, BF],
                             epilogue=add_norm_bwd, extras=(dh3, h2), consts=(gp,), sums=[D_MODEL])
    d_up = _mm("d_mlp_down", dh2_b, mlp_all, "nt", 1024,1024, 1024, [BF], b_view=(D_FF, chip_block, lambda j, k: (j, 1, 0)),
               epilogue=lambda acc, u: (acc * (2.0 * jnp.maximum(u.astype(F32), 0.0)),), extras=(up,))
    early_buf = _mm("g_mlp_down", act, dh2_b, "tn", 1024, 1024, 1024,[F32],
                    into=(early_buf, (None, D_MODEL, PACK_W), lambda i, j: (i, 1, 0)))
    early_buf = _mm("g_mlp_up", cn, d_up, "tn", 1024, 1024, 1024,[F32],
                    into=(early_buf, (None, D_MODEL, PACK_W), lambda i, j: (j, 0, 0)))
    dh1, dh1_b, dg_mlp = _mm("d_mlp_up", d_up, mlp_all, "nt", 1024, 1024, 1024, [F32, BF], epilogue=add_norm_bwd,
                             b_view=(D_MODEL, chip_block, lambda j, k: (k, 0, 0)),
                             extras=(dh2, h1), consts=(gl,), sums=[D_MODEL])
    d_mixed = _mm("d_out_proj", dh1_b, W["w_out"], "nt", 1024,1024, 1024, [F32])
    early_buf = _mm("g_out_proj", mixed, dh1_b, "tn", 1024, 1024, 1024, [F32],
                    into=(early_buf, (N_CHIPS, shard_rows, PACK_W), lambda i, j: (0, 2 * D_MODEL // shard_rows + 1, 0)))

    def gate_bwd(dm, gn, gd, un, ud):
        gn, gd, un, ud = (t.astype(F32) for t in (gn, gd, un, ud))
        sn, sd = _sigmoid(gn), _sigmoid(gd)
        return jnp.concatenate([dm * un * sn * (1.0 - sn), dm * ud * sd * (1.0 - sd)], axis=1), dm * sn, dm * sd

    dz_gates, d_u_na, d_u_dil = _rowwise(
        "gate_mix_bwd", gate_bwd, T, TM,
        [_row(d_mixed, TM), _row(z_gates, TM, 0, D_MODEL), _row(z_gates, TM, 1, D_MODEL), _row(u_na, TM), _row(u_dil, TM)],
        [(2 * D_MODEL, BF), (D_MODEL, BF), (D_MODEL, BF)])
    g_branch_na = _mm("g_branch_na", y_na, d_u_na, "tn", 1024, 1024, 1024,[F32])
    g_branch_dil = _mm("g_branch_dil", y_dil, d_u_dil, "tn", 256, 1024, 1024,[F32])
    small_rows = [jnp.concatenate([_pack_rows(g[:, s * shard_rows:(s + 1) * shard_rows]) for g in (g_ple_proj, g_branch_na, g_branch_dil)],
                                  axis=0) for s in range(N_CHIPS)]
    early_buf = lax.dynamic_update_slice(early_buf, jnp.stack(small_rows), (0, 2 * D_MODEL + 2 * shard_rows, 0))
    early_tm = early_rows // 4
    swap_flight = _swap_start(early_buf, "early")
    d_y_na = _mm("d_branch_na", d_u_na, W["w_branch_na"], "nt", 1024,512, 1024, [BF], after=(swap_flight.token,))
    d_y_dil = _mm("d_branch_dil", d_u_dil, W["w_branch_dil"], "nt", 1024,256, 1024, [F32])

    dqa, dka, dva, dtab = _na_bwd(qkv, tab, d_y_na)
    early_g, early_got = _swap_wait(swap_flight, dqa, "early")
    early_pair, early_pair_b = _pair_sum(early_g, early_got, early_tm)
    scatter_flight = _scatter_start(early_pair_b, "early")

    do_res, dlse_res = _dil_merge_bwd(d_y_dil, o_nat, w_grp, TM, after=(scatter_flight.token,))
    d_dil = [_band_bwd(*dil_ops[g], do_res[g], dlse_res[g], g) for g in range(len(DIL_GROUPS))]

    dz_qkv = _qkv_unprep((dqa, dka, dva), d_dil, cos2, sin_signed, TM)
    g_in_parts = [_mm("g_in_qkv", dz_qkv, a, "tn", 1280, 1024, 1024,[F32]), _mm("g_in_gates", dz_gates, a, "tn", 1024, 1024, 1024,[F32])]
    early_mine = _chip_sum(early_pair, _scatter_wait(scatter_flight, tuple(g_in_parts), "early"), early_tm)
    join_flight = _join_start(early_mine)
    in_cols = shards["w_in"].shape[1]

    def owner_rows(s):
        lo, hi, split = s * in_cols, (s + 1) * in_cols, g_in_parts[0].shape[0]
        pieces = [g_in_parts[0][lo:min(hi, split)]] if lo < split else []
        pieces += [g_in_parts[1][max(lo, split) - split:hi - split]] if hi > split else []
        return pieces[0] if len(pieces) == 1 else jnp.concatenate(pieces, axis=0)

    late_tm = in_cols // 4
    late_swap = _swap_start(jnp.stack([owner_rows(s) for s in range(N_CHIPS)]), "late")
    d_a = _mm("d_in_qkv", dz_qkv, W["w_in"], "nt", 1024,1024, 1280, [F32], after=(late_swap.token, join_flight.token),
              b_view=(D_MODEL, (D_MODEL, 1280), lambda j, k: (j, k)))
    late_g, late_got = _swap_wait(late_swap, d_a, "late")
    late_pair, late_pair_b = _pair_sum(late_g, late_got, late_tm)
    late_scatter = _scatter_start(late_pair_b, "late")
    d_rpb = _na_rpb_grad(dtab, after=(late_scatter.token,))[:, :2 * NA_WIN_ROWS - 1, :2 * NA_WIN_COLS - 1]
    def first_bwd(dn_gates, dn_qkv, dh_out, h, g):
        dh, dg = _rms_bwd(dn_gates + dn_qkv, h, g)
        return dh_out + dh, dg

    grad_x, dg_mix = _mm("d_in_gates", dz_gates, w_gates, "nt", 512, 1024, 1024, [F32], epilogue=first_bwd,
                         extras=(d_a, dh1, xs), consts=(gm,), sums=[D_MODEL], after=(late_scatter.token,))
    early_shard = _join_wait(join_flight, grad_x)

    n_rpb = rpb.size
    rpb_rows = 4
    small = jnp.concatenate([
        dg_mix, dg_mlp, dg_ple, dg_final,
        jnp.pad(d_rpb.reshape(-1), (0, rpb_rows * D_MODEL - n_rpb)).reshape(rpb_rows, D_MODEL),
        jnp.pad(loss_part, ((0, 0), (0, D_MODEL - loss_part.shape[1]))),
        jnp.zeros((SMALL_ROWS - 5 - rpb_rows, D_MODEL), F32)], axis=0)
    out = {"grad": {}, "delta": {}, "new_m": {}, "new_v": {}}

    def update(n, g, row0):
        res = _adamw("adamw_" + n, g, row0, params[n], params["m_" + n], params["v_" + n])
        for kind, t in zip(("grad", "delta", "new_m", "new_v"), res, strict=True):
            out[kind][n] = t

    row0 = 0
    for n in REDUCE_EARLY:
        rows, cols = early_shapes[n]
        n_rows = rows * cols // PACK_W
        if cols == PACK_W:
            update(n, early_shard, row0)
        else:
            update(n, early_shard[row0:row0 + n_rows].reshape(rows, cols), 0)
        row0 += n_rows
    late_others = _scatter_wait(late_scatter, (*[out["new_v"][n] for n in REDUCE_EARLY], d_rpb), "late")
    late_mine = _chip_sum(late_pair, late_others, late_tm)
    small = _allreduce_small(small, after=(late_mine,))
    res = _adamw("adamw_w_in", _join_halves(late_mine), 0, *[jnp.swapaxes(params[n], 1, 2) for n in ("w_in", "m_w_in", "v_w_in")])
    for kind, t in zip(("grad", "delta", "new_m", "new_v"), res, strict=True):
        out[kind]["w_in"] = jnp.swapaxes(t, 1, 2)
    loss = small[4 + rpb_rows, 0]

    def small_pack(a0, a1, a2, a3, r):
        return jnp.concatenate([a0.reshape(1, -1), a1.reshape(1, -1), a2.reshape(1, -1), a3.reshape(1, -1),
                                jnp.pad(r.reshape(-1), (0, rpb_rows * D_MODEL - n_rpb)).reshape(rpb_rows, D_MODEL)], axis=0)

    small_res = _adamw("adamw_small", small, 0, small_pack(g_mix, g_mlp, g_ple, g_final, rpb)[None],
                       small_pack(m_g_mix, m_g_mlp, m_g_ple, m_g_final, m_rpb)[None],
                       small_pack(v_g_mix, v_g_mlp, v_g_ple, v_g_final, v_rpb)[None])

    def small_unpack(t):
        return {"g_mix": t[0].reshape(g_mix.shape), "g_mlp": t[1].reshape(g_mlp.shape), "g_ple": t[2].reshape(g_ple.shape),
                "g_final": t[3].reshape(g_final.shape), "rpb": t[4:].reshape(-1)[:n_rpb].reshape(rpb.shape)}

    for kind, t in zip(("grad", "delta", "new_m", "new_v"), small_res, strict=True):
        out[kind].update(small_unpack(t[0]))

    order = ["g_mix", "w_in", "rpb", "w_branch_na", "w_branch_dil", "w_out", "g_mlp", "w_up", "w_down", "g_ple",
             "w_ple_gate", "w_ple_proj", "g_final"]
    return (loss, grad_x[None], *[out["grad"][n] for n in order], *[out["delta"][n] for n in order],
            *[out["new_m"][n] for n in order], *[out["new_v"][n] for n in order])
```

```python
import functools
from typing import NamedTuple

import jax
import jax.numpy as jnp
from jax import lax
from jax.experimental import pallas as pl
from jax.experimental.pallas import tpu as pltpu

BF = jnp.bfloat16
F32 = jnp.float32
MESH = pl.DeviceIdType.MESH
ANY = pl.BlockSpec(memory_space=pl.ANY)

V7X_VMEM_BYTES = 64 * 1024 * 1024
VMEM_LIMIT = V7X_VMEM_BYTES - 16 * 1024 * 1024

D_MODEL = 1024
HEAD_DIM = 64
GRID_W = 64
NA_HEADS = 8
NA_WIN_ROWS = 8
NA_WIN_COLS = 16
NA_WIDTH = NA_HEADS * HEAD_DIM
DIL_GROUPS = ((128, 1), (512, 4), (2048, 16))
DIL_HPG = 4
DIL_HEADS = DIL_HPG * len(DIL_GROUPS)
DIL_WIDTH = DIL_HEADS * HEAD_DIM
DIL_OUT_WIDTH = DIL_HPG * HEAD_DIM
DIL_RADIUS = 64
QKV_WIDTH = 3 * NA_WIDTH + 3 * DIL_WIDTH
D_FF = 4 * D_MODEL
PLE_DIM = 256
ROPE_THETA = 10000.0
RMS_EPS = 1e-6
NEG_INF = -1e30
Q_SCALE = HEAD_DIM ** -0.5

ADAM_LR = 0.001
ADAM_B1 = 0.9
ADAM_B2 = 0.999
ADAM_EPS = 1e-08
ADAM_WD = 0.01
ADAM_STEP = 10

N_CHIPS = 4
N_DEV = 8
PACK_W = 1024
GATHER_MIX = ("w_branch_na", "w_branch_dil", "w_out")
GATHER_MLP = ("w_up", "w_down", "w_ple_gate", "w_ple_proj")
REDUCE_EARLY = ("w_up", "w_down", "w_ple_gate", "w_out", "w_ple_proj", "w_branch_na", "w_branch_dil")
SMALL_ROWS = 16


def _cparams(sem=None):
    return pltpu.CompilerParams(dimension_semantics=sem, vmem_limit_bytes=VMEM_LIMIT)


def _mm(name, a, b, mode, tm, tn, tk, out_dtypes, epilogue=None, extras=(), consts=(), sums=(), after=(), into=None,
        b_view=None):
    if mode == "nn":
        (M, K), N = a.shape, b.shape[1]
    elif mode == "nt":
        (M, K), N = a.shape, b.shape[0]
    else:
        (K, M), N = a.shape, b.shape[1]
    if b_view is not None:
        N = b_view[0]
    tm, tn, tk = min(tm, M), min(tn, N), min(tk, K)
    assert M % tm == 0 and N % tn == 0 and K % tk == 0, (name, M, N, K, tm, tn, tk)
    if mode == "nn":
        a_spec = pl.BlockSpec((tm, tk), lambda i, j, k: (i, k))
        b_spec = pl.BlockSpec((tk, tn), lambda i, j, k: (k, j))
        dims = (((1,), (0,)), ((), ()))
    elif mode == "nt":
        a_spec = pl.BlockSpec((tm, tk), lambda i, j, k: (i, k))
        b_spec = pl.BlockSpec((tn, tk), lambda i, j, k: (j, k))
        dims = (((1,), (1,)), ((), ()))
    else:
        a_spec = pl.BlockSpec((tk, tm), lambda i, j, k: (k, i))
        b_spec = pl.BlockSpec((tk, tn), lambda i, j, k: (k, j))
        dims = (((0,), (0,)), ((), ()))
    if b_view is not None:
        b_spec = pl.BlockSpec(b_view[1], lambda i, j, k: b_view[2](j, k))
    nk = K // tk
    n_extra, n_const, n_out, n_sum = len(extras), len(consts), len(out_dtypes), len(sums)
    tile = pl.BlockSpec((tm, tn), lambda i, j, k: (i, j))
    assert not sums or tn == N, "row sums need whole rows in a tile"

    n_after = len(after)

    def body(a_ref, b_ref, *rest):
        extra_refs, rest = rest[:n_extra + n_const], rest[n_extra + n_const + n_after:]
        out_refs, sum_refs, acc = rest[:n_out], rest[n_out:n_out + n_sum], rest[-1]
        i, k = pl.program_id(0), pl.program_id(2)
        def product():
            return lax.dot_general(a_ref[...].astype(BF), b_ref[...].astype(BF), dims, preferred_element_type=F32)

        if nk > 1:
            @pl.when(k == 0)
            def _():
                acc[...] = jnp.zeros_like(acc)

            acc[...] += product()

        @pl.when(k == nk - 1)
        def _():
            total = product() if nk == 1 else acc[...]
            outs = (total,) if epilogue is None else epilogue(total, *[e[...] for e in extra_refs])
            for o_ref, val in zip(out_refs, outs[:n_out], strict=True):
                o_ref[...] = val.astype(o_ref.dtype).reshape(o_ref.shape)
            for s_ref, val in zip(sum_refs, outs[n_out:], strict=True):
                @pl.when(i == 0)
                def _():
                    s_ref[...] = val

                @pl.when(i != 0)
                def _():
                    s_ref[...] += val

    out_specs = [tile] * n_out + [pl.BlockSpec((1, c), lambda i, j, k: (0, 0)) for c in sums]
    out_shape = [jax.ShapeDtypeStruct((M, N), dt) for dt in out_dtypes] + [jax.ShapeDtypeStruct((1, c), F32) for c in sums]
    operands, aliases = [a, b, *extras, *consts, *after], {}
    in_specs = ([a_spec, b_spec] + [tile] * n_extra
                + [pl.BlockSpec(c.shape, functools.partial(lambda nd, i, j, k: (0,) * nd, c.ndim)) for c in consts] + [ANY] * n_after)
    if into is not None:
        assert n_out == 1
        target, block, index = into
        out_specs = [pl.BlockSpec(block, lambda i, j, k: index(i, j))]
        out_shape = [jax.ShapeDtypeStruct(target.shape, target.dtype)]
        if not isinstance(target, jax.ShapeDtypeStruct):
            aliases = {len(operands): 0}
            operands.append(target)
            in_specs.append(ANY)
            n_after += 1

    outs = pl.pallas_call(
        body, name=name, grid=(M // tm, N // tn, nk),
        in_specs=in_specs, out_specs=out_specs, out_shape=out_shape,
        scratch_shapes=[pltpu.VMEM((tm, tn) if nk > 1 else (8, 128), F32)], input_output_aliases=aliases,
        compiler_params=_cparams(("arbitrary",) * 3 if sums else ("parallel", "parallel", "arbitrary")),
    )(*operands)
    return outs[0] if len(outs) == 1 else outs


def _row(arr, tm, col_block=None, width=None):
    width = arr.shape[1] if width is None else width
    cb = 0 if col_block is None else col_block
    return arr, pl.BlockSpec((tm, width), lambda i: (i, cb))


def _full(arr):
    nd = arr.ndim
    return arr, pl.BlockSpec(arr.shape, lambda i: (0,) * nd)


def _rowwise(name, body, T, tm, ins, outs, sums=(), after=()):
    n_in, n_out, n_sum, n_after = len(ins), len(outs), len(sums), len(after)

    def kern(*refs):
        in_refs, refs = refs[:n_in], refs[n_in + n_after:]
        out_refs, sum_refs = refs[:n_out], refs[n_out:]
        res = body(*[r[...] for r in in_refs])
        res = res if isinstance(res, tuple) else (res,)
        for o_ref, val in zip(out_refs, res[:n_out], strict=True):
            o_ref[...] = val.astype(o_ref.dtype)
        if n_sum:
            @pl.when(pl.program_id(0) == 0)
            def _():
                for s_ref in sum_refs:
                    s_ref[...] = jnp.zeros_like(s_ref)

            for s_ref, val in zip(sum_refs, res[n_out:], strict=True):
                s_ref[...] += val

    res = pl.pallas_call(
        kern, name=name, grid=(T // tm,),
        in_specs=[spec for _, spec in ins] + [ANY] * n_after,
        out_specs=[pl.BlockSpec((tm, c), lambda i: (i, 0)) for c, _ in outs]
        + [pl.BlockSpec((1, c), lambda i: (0, 0)) for c in sums],
        out_shape=[jax.ShapeDtypeStruct((T, c), dt) for c, dt in outs]
        + [jax.ShapeDtypeStruct((1, c), F32) for c in sums],
        compiler_params=_cparams(("arbitrary",)),
    )(*[a for a, _ in ins], *after)
    return res[0] if len(res) == 1 else res


def _sigmoid(x):
    return 1.0 / (1.0 + jnp.exp(-x))


def _rms(h):
    return lax.rsqrt(jnp.mean(h * h, axis=-1, keepdims=True) + RMS_EPS)


def _rms_bwd(dy, h, g):
    r = _rms(h)
    n = h * r
    dn = dy * g
    dh = r * (dn - n * jnp.mean(dn * n, axis=-1, keepdims=True))
    return dh, jnp.sum(dy * n, axis=0, keepdims=True)


def _rope(x, cos2, sin_signed):
    lane = lax.broadcasted_iota(jnp.int32, x.shape, 1)
    swapped = jnp.where((lane % HEAD_DIM) < HEAD_DIM // 2, pltpu.roll(x, 128 - HEAD_DIM // 2, 1), pltpu.roll(x, HEAD_DIM // 2, 1))
    return x * cos2 + swapped * sin_signed


NA_KEYS = NA_WIN_ROWS * GRID_W
NA_BASES = 8


def _na_row_geometry(r, rows):
    first = jnp.clip(r - NA_WIN_ROWS // 2, 0, rows - NA_WIN_ROWS)
    base = first - r + (NA_WIN_ROWS - 1)
    return pl.multiple_of(first * GRID_W, GRID_W), base


NA_ROWS_PER_STEP = 16
NA_BWD_ROWS_PER_STEP = 8


def _softmax_rows(s):
    p = jnp.exp(s - jnp.max(s, axis=-1, keepdims=True))
    return p / jnp.sum(p, axis=-1, keepdims=True)


def _split_pair(t):
    first = lax.broadcasted_iota(jnp.int32, t.shape, 1) < HEAD_DIM
    zero = jnp.zeros_like(t)
    return jnp.where(first, t, zero), jnp.where(first, zero, t)


def _join_pair(a, b):
    return jnp.where(lax.broadcasted_iota(jnp.int32, a.shape, 1) < HEAD_DIM, a, b)


_NT = (((1,), (1,)), ((), ()))
_TN = (((0,), (0,)), ((), ()))


def _na_fwd(qkv, tab):
    T = qkv.shape[0]
    rows = T // GRID_W
    n_pairs = NA_WIDTH // 128

    def body(q_ref, k_ref, v_ref, tab_ref, y_ref):
        def step(it, carry):
            geo = [_na_row_geometry(it * NA_ROWS_PER_STEP + u, rows) for u in range(NA_ROWS_PER_STEP)]
            q0s = [pl.multiple_of((it * NA_ROWS_PER_STEP + u) * GRID_W, GRID_W) for u in range(NA_ROWS_PER_STEP)]
            ss = [lax.dot_general(jnp.concatenate(_split_pair(q_ref[pl.ds(q0, GRID_W), :] * Q_SCALE), axis=0),
                                  k_ref[pl.ds(k0, NA_KEYS), :], _NT, preferred_element_type=F32)
                  for q0, (k0, _) in zip(q0s, geo)]
            ps = [_softmax_rows(s + jnp.concatenate([tab_ref[0, base], tab_ref[1, base]], axis=0)) for s, (_, base) in zip(ss, geo)]
            ys = [jnp.dot(p.astype(BF), v_ref[pl.ds(k0, NA_KEYS), :], preferred_element_type=F32) for p, (k0, _) in zip(ps, geo)]
            for q0, y2 in zip(q0s, ys):
                y_ref[pl.ds(q0, GRID_W), :] = _join_pair(y2[:GRID_W], y2[GRID_W:]).astype(y_ref.dtype)
            return carry

        lax.fori_loop(0, rows // NA_ROWS_PER_STEP, step, 0)

    def cols(first):
        return pl.BlockSpec((T, 128), lambda j: (0, first + j))

    return pl.pallas_call(
        body, name="na_fwd", grid=(n_pairs,),
        in_specs=[cols(0), cols(n_pairs), cols(2 * n_pairs), pl.BlockSpec((2, NA_BASES, GRID_W, NA_KEYS), lambda j: (j, 0, 0, 0))],
        out_specs=cols(0), out_shape=jax.ShapeDtypeStruct((T, NA_WIDTH), BF),
        compiler_params=_cparams(("parallel",)),
    )(qkv, qkv, qkv, tab)


def _na_bwd(qkv, tab, do):
    T = qkv.shape[0]
    rows = T // GRID_W
    n_pairs = NA_WIDTH // 128

    def body(q_ref, k_ref, v_ref, tab_ref, do_ref, dq_ref, dk_out, dv_out, dtab_ref, dk_ref, dv_ref):
        dk_ref[...] = jnp.zeros_like(dk_ref)
        dv_ref[...] = jnp.zeros_like(dv_ref)
        dtab_ref[...] = jnp.zeros_like(dtab_ref)

        def step(it, carry):
            U = NA_BWD_ROWS_PER_STEP
            geo = [_na_row_geometry(it * U + u, rows) for u in range(U)]
            q0s = [pl.multiple_of((it * U + u) * GRID_W, GRID_W) for u in range(U)]
            q2s = [jnp.concatenate(_split_pair(q_ref[pl.ds(q0, GRID_W), :] * Q_SCALE), axis=0) for q0 in q0s]
            do2s = [jnp.concatenate(_split_pair(do_ref[pl.ds(q0, GRID_W), :]), axis=0) for q0 in q0s]
            ss = [lax.dot_general(q2, k_ref[pl.ds(k0, NA_KEYS), :], _NT, preferred_element_type=F32) for q2, (k0, _) in zip(q2s, geo)]
            dps = [lax.dot_general(do2, v_ref[pl.ds(k0, NA_KEYS), :], _NT, preferred_element_type=F32) for do2, (k0, _) in zip(do2s, geo)]
            ps = [_softmax_rows(s + jnp.concatenate([tab_ref[0, base], tab_ref[1, base]], axis=0)) for s, (_, base) in zip(ss, geo)]
            dss = [p * (dp - jnp.sum(dp * p, axis=-1, keepdims=True)) for p, dp in zip(ps, dps)]
            dvs = [lax.dot_general(p.astype(BF), do2, _TN, preferred_element_type=F32) for p, do2 in zip(ps, do2s)]
            dsbs = [ds.astype(BF) for ds in dss]
            dqs = [jnp.dot(dsb, k_ref[pl.ds(k0, NA_KEYS), :], preferred_element_type=F32) for dsb, (k0, _) in zip(dsbs, geo)]
            dks = [lax.dot_general(dsb, q2, _TN, preferred_element_type=F32) for dsb, q2 in zip(dsbs, q2s)]
            for u in range(U):
                k0, base = geo[u]
                dtab_ref[0, base] += dss[u][:GRID_W]
                dtab_ref[1, base] += dss[u][GRID_W:]
                dq_ref[pl.ds(q0s[u], GRID_W), :] = (_join_pair(dqs[u][:GRID_W], dqs[u][GRID_W:]) * Q_SCALE).astype(dq_ref.dtype)
                dk_ref[pl.ds(k0, NA_KEYS), :] += dks[u]
                dv_ref[pl.ds(k0, NA_KEYS), :] += dvs[u]
            return carry

        lax.fori_loop(0, rows // NA_BWD_ROWS_PER_STEP, step, 0)
        dk_out[...] = dk_ref[...].astype(dk_out.dtype)
        dv_out[...] = dv_ref[...].astype(dv_out.dtype)

    def cols(first):
        return pl.BlockSpec((T, 128), lambda j: (0, first + j))

    tabs = pl.BlockSpec((2, NA_BASES, GRID_W, NA_KEYS), lambda j: (j, 0, 0, 0))
    wide = jax.ShapeDtypeStruct((T, NA_WIDTH), BF)
    return pl.pallas_call(
        body, name="na_bwd", grid=(n_pairs,),
        in_specs=[cols(0), cols(n_pairs), cols(2 * n_pairs), tabs, cols(0)],
        out_specs=[cols(0), cols(0), cols(0), tabs],
        out_shape=[wide, wide, wide, jax.ShapeDtypeStruct((NA_HEADS, NA_BASES, GRID_W, NA_KEYS), F32)],
        scratch_shapes=[pltpu.VMEM((T, 128), F32), pltpu.VMEM((T, 128), F32)],
        compiler_params=_cparams(("parallel",)),
    )(qkv, qkv, qkv, tab, do)


def _na_bias_table(rpb):
    H, n_rows, n_cols = rpb.shape

    def body(r_ref, tab_ref):
        q = lax.broadcasted_iota(jnp.int32, (GRID_W, 128), 0)
        kc = lax.broadcasted_iota(jnp.int32, (GRID_W, 128), 1)
        first = jnp.clip(q - NA_WIN_COLS // 2, 0, GRID_W - NA_WIN_COLS)
        valid = (kc >= first) & (kc < first + NA_WIN_COLS)
        toeplitz = []
        for ro in range(n_rows):
            row = jnp.broadcast_to(r_ref[pl.ds(ro, 1), :], (GRID_W, 128))
            shifted = pltpu.roll(pltpu.roll(row, 128 - (NA_WIN_COLS - 1), 1), 0, 1, stride=1, stride_axis=0)
            toeplitz.append(jnp.where(valid, shifted, NEG_INF))
        for base in range(NA_BASES):
            for j in range(NA_WIN_ROWS // 2):
                even, odd = toeplitz[base + 2 * j], toeplitz[base + 2 * j + 1]
                tab_ref[base, :, pl.ds(j * 128, 128)] = jnp.where(kc < GRID_W, even, pltpu.roll(odd, GRID_W, 1))

    padded = jnp.pad(rpb, ((0, 0), (0, 16 - n_rows), (0, 128 - n_cols)))
    return pl.pallas_call(
        body, name="na_bias_table", grid=(H,),
        in_specs=[pl.BlockSpec((None, 16, 128), lambda h: (h, 0, 0))],
        out_specs=pl.BlockSpec((None, NA_BASES, GRID_W, NA_KEYS), lambda h: (h, 0, 0, 0)),
        out_shape=jax.ShapeDtypeStruct((H, NA_BASES, GRID_W, NA_KEYS), F32),
        compiler_params=_cparams(("parallel",)),
    )(padded)


def _na_rpb_grad(dtab, after=()):
    H = dtab.shape[0]
    n_rows = 2 * NA_WIN_ROWS - 1
    n_cols = 2 * NA_WIN_COLS - 1

    def body(d_ref, *rest):
        o_ref = rest[-1]
        lane = lax.broadcasted_iota(jnp.int32, (GRID_W, 128), 1)
        low = lane < GRID_W
        flip = (lax.broadcasted_iota(jnp.int32, (GRID_W, GRID_W), 0) + lax.broadcasted_iota(jnp.int32, (GRID_W, GRID_W), 1)
                == GRID_W - 1).astype(BF)

        def reverse_rows(t):
            out = jnp.zeros_like(t)
            for _ in range(3):
                piece = t.astype(BF)
                out = out + jnp.dot(flip, piece, preferred_element_type=F32)
                t = t - piece.astype(F32)
            return out

        out_rows = []
        for ro in range(n_rows):
            acc = jnp.zeros((GRID_W, 128), F32)
            for base in range(NA_BASES):
                i = ro - base
                if not 0 <= i < NA_WIN_ROWS:
                    continue
                pair = d_ref[base, :, pl.ds((i // 2) * 128, 128)]
                if i % 2:
                    pair = pltpu.roll(pair, GRID_W, 1)
                acc = acc + jnp.where(low, pair, 0.0)
            skew = pltpu.roll(reverse_rows(acc), 0, 1, stride=1, stride_axis=0)
            diag = jnp.sum(skew, axis=0, keepdims=True)
            out_rows.append(pltpu.roll(jnp.broadcast_to(diag, (8, 128)), 128 - (GRID_W - NA_WIN_COLS), 1)[:1])
        out_rows.append(jnp.zeros((1, 128), F32))
        res = jnp.concatenate(out_rows, axis=0)
        o_ref[...] = jnp.where(lax.broadcasted_iota(jnp.int32, res.shape, 1) < n_cols, res, 0.0)

    return pl.pallas_call(
        body, name="na_rpb_grad", grid=(H,),
        in_specs=[pl.BlockSpec((None, NA_BASES, GRID_W, NA_KEYS), lambda h: (h, 0, 0, 0))] + [ANY] * len(after),
        out_specs=pl.BlockSpec((None, n_rows + 1, 128), lambda h: (h, 0, 0)),
        out_shape=jax.ShapeDtypeStruct((H, n_rows + 1, 128), F32),
        compiler_params=_cparams(("parallel",)),
    )(dtab, *after)


BAND_Q = 128
BAND_KEYS = BAND_Q + 2 * DIL_RADIUS


def _band_geometry(n, L):
    q0 = pl.multiple_of(n * BAND_Q, BAND_Q)
    k0 = pl.multiple_of(jnp.clip(q0 - DIL_RADIUS, 0, L - BAND_KEYS), DIL_RADIUS)
    qi = q0 + lax.broadcasted_iota(jnp.int32, (BAND_Q, BAND_KEYS), 0)
    kj = k0 + lax.broadcasted_iota(jnp.int32, (BAND_Q, BAND_KEYS), 1)
    return q0, k0, jnp.abs(qi - kj) <= DIL_RADIUS


DIL_PAIRS = DIL_OUT_WIDTH // 128


def _residue_shape(dil, T, dtype):
    return jax.ShapeDtypeStruct((DIL_PAIRS, dil, T // dil, 128), dtype)


def _residue_tile(dil, tm):
    return pl.BlockSpec((DIL_PAIRS, dil, tm // dil, 128), lambda i: (0, 0, i, 0))


def _to_natural(ref, scratch, dil, tm):
    tiles = []
    for pair in range(DIL_PAIRS):
        if dil == 1:
            tiles.append(ref[pair, 0].astype(F32))
            continue
        for r in range(dil):
            scratch[pl.ds(r, tm // dil, stride=dil), :] = ref[pair, r].astype(F32)
        tiles.append(scratch[...])
    return tiles


def _from_natural(tile, scratch, ref, pair, dil, tm):
    if dil == 1:
        ref[pair, 0] = tile.astype(ref.dtype)
        return
    scratch[...] = tile
    for r in range(dil):
        ref[pair, r] = scratch[pl.ds(r, tm // dil, stride=dil), :].astype(ref.dtype)


def _band_specs(group, T):
    dil = DIL_GROUPS[group][1]
    L = T // dil
    assert L % BAND_Q == 0 and L >= BAND_KEYS, (T, dil)
    per_residue = min(BAND_BLOCKS_PER_STEP, L // BAND_Q)
    residues = min(dil, BAND_BLOCKS_PER_STEP // per_residue)
    spec = pl.BlockSpec((None, residues, L, 128), lambda s: (s % DIL_PAIRS, s // DIL_PAIRS, 0, 0))
    return L, residues, per_residue, (dil // residues * DIL_PAIRS,), spec


BAND_BLOCKS_PER_STEP = 8


def _band_softmax(s, valid):
    s = jnp.where(valid, s, NEG_INF)
    m = jnp.max(s, axis=-1, keepdims=True)
    p = jnp.exp(s - m)
    l = jnp.sum(p, axis=-1, keepdims=True)
    return p / l, m + jnp.log(l)


def _band_fwd(q, k, v, group):
    T = q.shape[1] * q.shape[2]
    L, residues, U, grid, spec = _band_specs(group, T)

    def body(q_ref, k_ref, v_ref, o_ref, lse_ref):
        def step(it, carry):
            geo = [(r, *_band_geometry(it * U + u, L)) for r in range(residues) for u in range(U)]
            ss = [lax.dot_general(jnp.concatenate(_split_pair(q_ref[r, pl.ds(q0, BAND_Q), :]), axis=0),
                                  k_ref[r, pl.ds(k0, BAND_KEYS), :], _NT, preferred_element_type=F32) for r, q0, k0, _ in geo]
            pls = [_band_softmax(s, jnp.concatenate([valid, valid], axis=0)) for s, (_, _, _, valid) in zip(ss, geo)]
            os = [jnp.dot(p.astype(BF), v_ref[r, pl.ds(k0, BAND_KEYS), :], preferred_element_type=F32)
                  for (p, _), (r, _, k0, _) in zip(pls, geo)]
            for (r, q0, _, _), o2, (_, lse) in zip(geo, os, pls):
                o_ref[r, pl.ds(q0, BAND_Q), :] = _join_pair(o2[:BAND_Q], o2[BAND_Q:])
                lse2 = jnp.broadcast_to(lse, (2 * BAND_Q, 128))
                lse_ref[r, pl.ds(q0, BAND_Q), :] = _join_pair(lse2[:BAND_Q], lse2[BAND_Q:])
            return carry

        lax.fori_loop(0, L // (BAND_Q * U), step, 0)

    res = _residue_shape(DIL_GROUPS[group][1], T, F32)
    return pl.pallas_call(
        body, name=f"band_fwd_g{group}", grid=grid,
        in_specs=[spec] * 3, out_specs=[spec] * 2, out_shape=[res, res],
        compiler_params=_cparams(("parallel",)),
    )(q, k, v)


def _band_bwd(q, k, v, do, dlse, group):
    T = q.shape[1] * q.shape[2]
    L, residues, U, grid, spec = _band_specs(group, T)

    def body(q_ref, k_ref, v_ref, do_ref, dlse_ref, dq_ref, dk_ref, dv_ref):
        dk_ref[...] = jnp.zeros_like(dk_ref)
        dv_ref[...] = jnp.zeros_like(dv_ref)

        def step(it, carry):
            geo = [(r, *_band_geometry(it * U + u, L)) for r in range(residues) for u in range(U)]
            q2s = [jnp.concatenate(_split_pair(q_ref[r, pl.ds(q0, BAND_Q), :]), axis=0) for r, q0, _, _ in geo]
            do2s = [jnp.concatenate(_split_pair(do_ref[r, pl.ds(q0, BAND_Q), :]), axis=0) for r, q0, _, _ in geo]
            ss = [lax.dot_general(q2, k_ref[r, pl.ds(k0, BAND_KEYS), :], _NT, preferred_element_type=F32)
                  for q2, (r, _, k0, _) in zip(q2s, geo)]
            dps = [lax.dot_general(do2, v_ref[r, pl.ds(k0, BAND_KEYS), :], _NT, preferred_element_type=F32)
                   for do2, (r, _, k0, _) in zip(do2s, geo)]
            ps = [_band_softmax(s, jnp.concatenate([valid, valid], axis=0))[0] for s, (_, _, _, valid) in zip(ss, geo)]
            dss = []
            for p, dp, (r, q0, _, _) in zip(ps, dps, geo):
                dl = dlse_ref[r, pl.ds(q0, BAND_Q), :]
                dl2 = jnp.concatenate([dl[:, :1], dl[:, HEAD_DIM:HEAD_DIM + 1]], axis=0)
                dss.append(p * (dp - jnp.sum(dp * p, axis=-1, keepdims=True) + dl2))
            dvs = [lax.dot_general(p.astype(BF), do2, _TN, preferred_element_type=F32) for p, do2 in zip(ps, do2s)]
            dsbs = [ds.astype(BF) for ds in dss]
            dqs = [jnp.dot(dsb, k_ref[r, pl.ds(k0, BAND_KEYS), :], preferred_element_type=F32) for dsb, (r, _, k0, _) in zip(dsbs, geo)]
            dks = [lax.dot_general(dsb, q2, _TN, preferred_element_type=F32) for dsb, q2 in zip(dsbs, q2s)]
            for u, (r, q0, k0, _) in enumerate(geo):
                dq_ref[r, pl.ds(q0, BAND_Q), :] = _join_pair(dqs[u][:BAND_Q], dqs[u][BAND_Q:])
                dk_ref[r, pl.ds(k0, BAND_KEYS), :] += dks[u]
                dv_ref[r, pl.ds(k0, BAND_KEYS), :] += dvs[u]
            return carry

        lax.fori_loop(0, L // (BAND_Q * U), step, 0)

    res = _residue_shape(DIL_GROUPS[group][1], T, F32)
    return pl.pallas_call(
        body, name=f"band_bwd_g{group}", grid=grid,
        in_specs=[spec] * 5, out_specs=[spec] * 3, out_shape=[res] * 3,
        compiler_params=_cparams(("parallel",)),
    )(q, k, v, do, dlse)


def _head_sums(t):
    head = lax.broadcasted_iota(jnp.int32, t.shape, 1) // HEAD_DIM
    out = jnp.zeros_like(t)
    for h in range(t.shape[1] // HEAD_DIM):
        mine = head == h
        out = jnp.where(mine, jnp.sum(jnp.where(mine, t, 0.0), axis=-1, keepdims=True), out)
    return out


def _dil_merge_fwd(os, lses, T, tm):
    G = len(DIL_GROUPS)
    W = DIL_OUT_WIDTH
    dils = [d for _, d in DIL_GROUPS]

    def body(*refs):
        o_refs, lse_refs = refs[:G], refs[G:2 * G]
        y_ref, w_refs, on_refs, scratch = refs[2 * G], refs[2 * G + 1:3 * G + 1], refs[3 * G + 1:4 * G + 1], refs[-1]
        o = [jnp.concatenate(_to_natural(r, scratch, d, tm), axis=1) for r, d in zip(o_refs, dils)]
        ls = [jnp.concatenate(_to_natural(r, scratch, d, tm), axis=1) for r, d in zip(lse_refs, dils)]
        m = functools.reduce(jnp.maximum, ls)
        es = [jnp.exp(l - m) for l in ls]
        tot = functools.reduce(jnp.add, es)
        ws = [e / tot for e in es]
        y_ref[...] = functools.reduce(jnp.add, [w * t for w, t in zip(ws, o)]).astype(y_ref.dtype)
        for g in range(G):
            w_refs[g][...] = ws[g]
            on_refs[g][...] = o[g]

    nat = pl.BlockSpec((tm, W), lambda i: (i, 0))
    res = pl.pallas_call(
        body, name="dil_merge_fwd", grid=(T // tm,),
        in_specs=[_residue_tile(d, tm) for d in dils] * 2,
        out_specs=[nat] * (2 * G + 1),
        out_shape=[jax.ShapeDtypeStruct((T, W), BF)] + [jax.ShapeDtypeStruct((T, W), F32)] * (2 * G),
        scratch_shapes=[pltpu.VMEM((tm, 128), F32)],
        compiler_params=_cparams(("parallel",)),
    )(*os, *lses)
    return res[0], res[1:G + 1], res[G + 1:]


def _dil_merge_bwd(dy, os, ws, tm, after=()):
    G = len(DIL_GROUPS)
    T, W = dy.shape
    dils = [d for _, d in DIL_GROUPS]
    n_after = len(after)

    def body(*refs):
        dyt = refs[0][...]
        o, w = [r[...] for r in refs[1:G + 1]], [r[...] for r in refs[G + 1:2 * G + 1]]
        refs = refs[2 * G + 1 + n_after:]
        do_refs, dlse_refs, scratch = refs[:G], refs[G:2 * G], refs[-1]
        dws = [_head_sums(dyt * t) for t in o]
        mean = functools.reduce(jnp.add, [a * b for a, b in zip(w, dws)])
        for g, d in enumerate(dils):
            do, dlse = w[g] * dyt, w[g] * (dws[g] - mean)
            for pair in range(DIL_PAIRS):
                cols = slice(pair * 128, (pair + 1) * 128)
                _from_natural(do[:, cols], scratch, do_refs[g], pair, d, tm)
                _from_natural(dlse[:, cols], scratch, dlse_refs[g], pair, d, tm)

    nat = pl.BlockSpec((tm, W), lambda i: (i, 0))
    res = pl.pallas_call(
        body, name="dil_merge_bwd", grid=(T // tm,),
        in_specs=[nat] * (2 * G + 1) + [ANY] * n_after,
        out_specs=[_residue_tile(d, tm) for d in dils] * 2,
        out_shape=[_residue_shape(d, T, BF) for d in dils] + [_residue_shape(d, T, F32) for d in dils],
        scratch_shapes=[pltpu.VMEM((tm, 128), F32)],
        compiler_params=_cparams(("parallel",)),
    )(dy, *os, *ws, *after)
    return res[:G], res[G:]


def _qkv_prep(z, cos2, sin_signed, tm):
    T = z.shape[0]
    G = len(DIL_GROUPS)
    dils = [d for _, d in DIL_GROUPS]
    n_dil_blocks = 3 * DIL_WIDTH // 128

    def body(*refs):
        blocks = refs[:n_dil_blocks]
        cos_ref, sin_ref = refs[n_dil_blocks], refs[1 + n_dil_blocks]
        outs = refs[2 + n_dil_blocks:]
        for part in range(3):
            for g, d in enumerate(dils):
                out = outs[g * 3 + part]
                for pair in range(DIL_PAIRS):
                    blk = blocks[part * (DIL_WIDTH // 128) + g * DIL_PAIRS + pair]
                    for r in range(d):
                        rows = pl.ds(r, tm // d, stride=d) if d > 1 else slice(None)
                        x = blk[rows, :]
                        if part < 2:
                            x = _rope(x, cos_ref[rows, :], sin_ref[rows, :])
                        if part == 0:
                            x = x * Q_SCALE
                        out[pair, r] = x.astype(out.dtype)

    lane_block = [pl.BlockSpec((tm, 128), functools.partial(lambda b, i: (i, b), b)) for b in range(n_dil_blocks)]
    tab = pl.BlockSpec((tm, 128), lambda i: (i, 0))
    res = pl.pallas_call(
        body, name="qkv_prep", grid=(T // tm,),
        in_specs=lane_block + [tab, tab],
        out_specs=[_residue_tile(d, tm) for d in dils for _ in range(3)],
        out_shape=[_residue_shape(d, T, BF) for d in dils for _ in range(3)],
        compiler_params=_cparams(("parallel",)),
    )(*[z] * n_dil_blocks, cos2, sin_signed)
    return [res[3 * g:3 + 3 * g] for g in range(G)]


def _qkv_unprep(d_na, d_dil, cos2, sin_signed, tm, after=()):
    T = d_na[0].shape[0]
    G = len(DIL_GROUPS)
    dils = [d for _, d in DIL_GROUPS]
    n_after = len(after)

    def body(*refs):
        dq, dk, dv = (r[...] for r in refs[:3])
        res_refs = refs[3:3 + 3 * G]
        cs, sn = refs[3 + 3 * G][...], refs[4 + 3 * G][...]
        out, scratch = refs[5 + 3 * G + n_after], refs[-1]
        cols = [dq, dk, dv]
        for part in range(3):
            for g, d in enumerate(dils):
                for x in _to_natural(res_refs[g * 3 + part], scratch, d, tm):
                    if part < 2:
                        x = _rope(x, cs, -sn)
                    cols.append((x * Q_SCALE if part == 0 else x).astype(out.dtype))
        out[...] = jnp.concatenate(cols, axis=1)

    wide = pl.BlockSpec((tm, NA_WIDTH), lambda i: (i, 0))
    tab = pl.BlockSpec((tm, 128), lambda i: (i, 0))
    return pl.pallas_call(
        body, name="qkv_unprep", grid=(T // tm,),
        in_specs=[wide] * 3 + [_residue_tile(d, tm) for d in dils for _ in range(3)] + [tab, tab] + [ANY] * n_after,
        out_specs=pl.BlockSpec((tm, QKV_WIDTH), lambda i: (i, 0)),
        out_shape=jax.ShapeDtypeStruct((T, QKV_WIDTH), BF),
        scratch_shapes=[pltpu.VMEM((tm, 128), F32)],
        compiler_params=_cparams(("parallel",)),
    )(*d_na, *[t for g in range(G) for t in d_dil[g]], cos2, sin_signed, *after)


def _rope_tables(positions):
    half = HEAD_DIM // 2
    inv_freq = ROPE_THETA ** (-jnp.arange(half, dtype=F32) / half)
    ang = positions.astype(F32)[:, None] * inv_freq
    cos, sin = jnp.cos(ang), jnp.sin(ang)
    return jnp.tile(jnp.concatenate([cos, cos], axis=1), (1, 2)), jnp.tile(jnp.concatenate([-sin, sin], axis=1), (1, 2))


def _pack_rows(t):
    return t.reshape(-1, PACK_W)


def _me():
    return lax.axis_index("x"), lax.axis_index("y"), lax.axis_index("c")


def _other_chips(x, y):
    return [(1 - x, y), (x, 1 - y), (1 - x, 1 - y)]


def _pair_sum(g, got, tm):
    S, R, W = g.shape
    half = R // 2
    nb = half // tm

    def body(pos_ref, g_ref, got_ref, own_ref, ob_ref):
        tot = g_ref[...] + got_ref[...]
        ob_ref[...] = tot.astype(ob_ref.dtype)

        @pl.when(pl.program_id(1) == pos_ref[1])
        def _():
            own_ref[...] = tot

    tile = pl.BlockSpec((None, tm, W), lambda i, s, pos_ref: (s, i, 0))
    c, chip = lax.axis_index("c"), 2 * lax.axis_index("x") + lax.axis_index("y")
    return pl.pallas_call(
        body, name="pair_sum",
        grid_spec=pltpu.PrefetchScalarGridSpec(
            num_scalar_prefetch=1, grid=(nb, S),
            in_specs=[pl.BlockSpec((None, tm, W), lambda i, s, pos_ref: (s, pos_ref[0] * nb + i, 0)), tile],
            out_specs=[pl.BlockSpec((tm, W), lambda i, s, pos_ref: (i, 0)), tile]),
        out_shape=[jax.ShapeDtypeStruct((half, W), F32), jax.ShapeDtypeStruct((S, half, W), BF)],
        compiler_params=_cparams(("parallel", "arbitrary")),
    )(jnp.stack([c, chip]).astype(jnp.int32), g, got)


def _chip_sum(own, others, tm):
    n, h, W = others.shape
    nb = h // tm

    def body(c_ref, own_ref, p_ref, o_ref):
        o_ref[...] = ((own_ref[...] + p_ref[0].astype(F32)) + p_ref[1].astype(F32)) + p_ref[2].astype(F32)

    return pl.pallas_call(
        body, name="chip_sum",
        grid_spec=pltpu.PrefetchScalarGridSpec(
            num_scalar_prefetch=1, grid=(nb,),
            in_specs=[pl.BlockSpec((tm, W), lambda i, c_ref: (i, 0)), pl.BlockSpec((n, tm, W), lambda i, c_ref: (0, i, 0))],
            out_specs=pl.BlockSpec((tm, W), lambda i, c_ref: (c_ref[0] * nb + i, 0))),
        out_shape=jax.ShapeDtypeStruct((2 * h, W), F32),
        compiler_params=_cparams(("parallel",)),
    )(lax.axis_index("c").reshape(1).astype(jnp.int32), own, others)


def _join_halves(shard):
    h = shard.shape[0] // 2

    def body(in_ref, out_ref, send_sem, recv_sem):
        x, y, c = _me()
        cp = pltpu.make_async_remote_copy(
            src_ref=in_ref.at[pl.ds(c * h, h), :], dst_ref=out_ref.at[pl.ds(c * h, h), :],
            send_sem=send_sem, recv_sem=recv_sem, device_id=(x, y, 1 - c), device_id_type=MESH)
        cp.start()
        pltpu.make_async_remote_copy(
            src_ref=in_ref.at[pl.ds(c * h, h), :], dst_ref=out_ref.at[pl.ds((1 - c) * h, h), :],
            send_sem=send_sem, recv_sem=recv_sem, device_id=(x, y, 1 - c), device_id_type=MESH).wait_recv()
        cp.wait_send()

    return pl.pallas_call(
        body, name="join_halves", in_specs=[ANY], out_specs=ANY,
        out_shape=jax.ShapeDtypeStruct(shard.shape, shard.dtype), input_output_aliases={0: 0},
        scratch_shapes=[pltpu.SemaphoreType.DMA, pltpu.SemaphoreType.DMA],
    )(shard)


def _allreduce_small(s, after=()):
    R, W = s.shape
    n_after = len(after)

    def body(s_ref, *rest):
        o_ref, buf, send_sems, recv_sems = rest[n_after:]
        x, y, c = _me()
        me = 4 * x + 2 * y + c
        buf[me] = s_ref[...]
        peers = [((x + fx) % 2, (y + fy) % 2, (c + fc) % 2) for fx in range(2) for fy in range(2) for fc in range(2)][1:]
        sends = [pltpu.make_async_remote_copy(
            src_ref=s_ref, dst_ref=buf.at[me], send_sem=send_sems.at[k], recv_sem=recv_sems.at[k],
            device_id=peer, device_id_type=MESH) for k, peer in enumerate(peers)]
        for cp in sends:
            cp.start()
        for k, peer in enumerate(peers):
            pltpu.make_async_remote_copy(
                src_ref=s_ref, dst_ref=buf.at[4 * peer[0] + 2 * peer[1] + peer[2]], send_sem=send_sems.at[k],
                recv_sem=recv_sems.at[k], device_id=peer, device_id_type=MESH).wait_recv()
        for cp in sends:
            cp.wait_send()
        total = buf[0]
        for d in range(1, N_DEV):
            total = total + buf[d]
        o_ref[...] = total

    return pl.pallas_call(
        body, name="allreduce_small",
        in_specs=[pl.BlockSpec(memory_space=pltpu.VMEM)] + [ANY] * n_after, out_specs=pl.BlockSpec(memory_space=pltpu.VMEM),
        out_shape=jax.ShapeDtypeStruct((R, W), F32),
        scratch_shapes=[pltpu.VMEM((N_DEV, R, W), F32), pltpu.SemaphoreType.DMA((N_DEV - 1,)), pltpu.SemaphoreType.DMA((N_DEV - 1,))],
    )(s, *after)


HBM_SPEC = pl.BlockSpec(memory_space=pltpu.HBM)
SEM_SPEC = pl.BlockSpec(memory_space=pltpu.SEMAPHORE)
DATAFLOW = pltpu.SideEffectType.DATAFLOW_SIDE_EFFECTING


class _InFlight(NamedTuple):
    sems: tuple
    src: jax.Array
    land: jax.Array
    token: jax.Array


def _split_start(name, src, land_shape, land_dtype, n, copies, after=()):
    n_after = len(after)

    def body(src_ref, land_ref, *rest):
        rest = rest[n_after:]
        sems, token = rest[:2 * n], rest[-1]
        for k, (s, d, peer) in enumerate(copies(src_ref, land_ref)):
            pltpu.make_async_remote_copy(src_ref=s, dst_ref=d, send_sem=sems[k], recv_sem=sems[n + k],
                                         device_id=peer, device_id_type=MESH).start()
        token[...] = jnp.zeros_like(token)

    outs = pl.pallas_call(
        body, name=name,
        out_shape=(*[pltpu.SemaphoreType.DMA(())] * (2 * n), pltpu.HBM(src.shape, src.dtype), pltpu.HBM(land_shape, land_dtype),
                   jax.ShapeDtypeStruct((8, 128), F32)),
        in_specs=(HBM_SPEC, HBM_SPEC, *[ANY] * n_after),
        out_specs=(*[SEM_SPEC] * (2 * n), HBM_SPEC, HBM_SPEC, pl.BlockSpec(memory_space=pltpu.VMEM)),
        input_output_aliases={0: 2 * n, 1: 2 * n + 1},
        compiler_params=pltpu.CompilerParams(has_side_effects=DATAFLOW),
    )(pltpu.with_memory_space_constraint(src, pltpu.HBM), pltpu.with_memory_space_constraint(lax.empty(land_shape, land_dtype), pltpu.HBM),
      *after)
    return _InFlight(tuple(outs[:2 * n]), outs[2 * n], outs[2 * n + 1], outs[2 * n + 2])


def _split_wait(name, flight, after, n, copies):
    after = after if isinstance(after, tuple) else (after,)

    def body(src_ref, land_ref, *rest):
        sems = rest[:2 * n]
        for k, (s, d, peer) in enumerate(copies(src_ref, land_ref)):
            cp = pltpu.make_async_remote_copy(src_ref=s, dst_ref=d, send_sem=sems[k], recv_sem=sems[n + k],
                                              device_id=peer, device_id_type=MESH)
            cp.wait_send()
            cp.wait_recv()

    return pl.pallas_call(
        body, name=name,
        out_shape=(pltpu.HBM(flight.src.shape, flight.src.dtype), pltpu.HBM(flight.land.shape, flight.land.dtype)),
        in_specs=(HBM_SPEC, HBM_SPEC, *[SEM_SPEC] * (2 * n), *[ANY] * len(after)),
        out_specs=(HBM_SPEC, HBM_SPEC), input_output_aliases={0: 0, 1: 1},
        compiler_params=pltpu.CompilerParams(has_side_effects=DATAFLOW),
    )(flight.src, flight.land, *flight.sems, *after)


def _gather_copies(src_ref, land_ref):
    x, y, c = _me()
    return [(src_ref, land_ref.at[2 * x + y], (*chip, c)) for chip in _other_chips(x, y)]


def _gather_start(packed, tag, after=()):
    return _split_start(f"gather_start_{tag}", packed, (N_CHIPS, *packed.shape), packed.dtype, 3, _gather_copies, after)


def _gather_wait(flight, after, tag):
    src, others = _split_wait(f"gather_wait_{tag}", flight, after, 3, _gather_copies)
    return lax.dynamic_update_slice(others, src[None], (2 * lax.axis_index("x") + lax.axis_index("y"), 0, 0))


def _across_copies(src_ref, land_ref):
    x, y, c = _me()
    half = src_ref.shape[0] // 2
    rows = pl.ds(c * half, half)
    return [(src_ref.at[rows, :], land_ref.at[2 * x + y, rows, :], (*chip, c)) for chip in _other_chips(x, y)]


def _to_sibling_copies(all_ref, unused_ref):
    x, y, c = _me()
    half = all_ref.shape[1] // 2
    places = [all_ref.at[2 * chip[0] + chip[1], pl.ds(c * half, half), :] for chip in _other_chips(x, y)]
    return [(place, place, (x, y, 1 - c)) for place in places]


def _gather_halves_start(shard, tag, after=()):
    return _split_start(f"gather_{tag}_across_start", shard, (N_CHIPS, *shard.shape), shard.dtype, 3, _across_copies, after)


def _gather_halves_relay(flight, after, tag):
    shard, landed = _split_wait(f"gather_{tag}_across_wait", flight, after, 3, _across_copies)
    return shard, _split_start(f"gather_{tag}_sibling_start", landed, (8, 128), landed.dtype, 3, _to_sibling_copies)


def _gather_halves_finish(shard, relay, after, tag):
    others = _split_wait(f"gather_{tag}_sibling_wait", relay, after, 3, _to_sibling_copies)[0]
    return lax.dynamic_update_slice(others, shard[None], (2 * lax.axis_index("x") + lax.axis_index("y"), 0, 0))


def _assemble_w_in(shards, tm):
    S, R, C = shards.shape
    n_gates = 2 * D_MODEL

    def body(s_ref, w_ref, g_ref):
        full = jnp.concatenate([s_ref[s] for s in range(S)], axis=1)
        w_ref[...] = full
        g_ref[...] = full[:, S * C - n_gates:]

    return pl.pallas_call(
        body, name="assemble_w_in", grid=(R // tm,),
        in_specs=[pl.BlockSpec((S, tm, C), lambda i: (0, i, 0))],
        out_specs=[pl.BlockSpec((tm, S * C), lambda i: (i, 0)), pl.BlockSpec((tm, n_gates), lambda i: (i, 0))],
        out_shape=[jax.ShapeDtypeStruct((R, S * C), shards.dtype), jax.ShapeDtypeStruct((R, n_gates), shards.dtype)],
        compiler_params=_cparams(("parallel",)),
    )(shards)


def _swap_copies(src_ref, land_ref):
    x, y, c = _me()
    half = land_ref.shape[1]
    return [(src_ref.at[:, pl.ds((1 - c) * half, half), :], land_ref, (x, y, 1 - c))]


def _swap_start(g, tag):
    S, R, W = g.shape
    return _split_start(f"swap_halves_start_{tag}", g, (S, R // 2, W), g.dtype, 1, _swap_copies)


def _swap_wait(flight, after, tag):
    return _split_wait(f"swap_halves_wait_{tag}", flight, after, 1, _swap_copies)


def _scatter_copies(src_ref, land_ref):
    x, y, c = _me()
    return [(src_ref.at[2 * chip[0] + chip[1]], land_ref.at[j], (*chip, c)) for j, chip in enumerate(_other_chips(x, y))]


def _scatter_start(part, tag):
    S, h, W = part.shape
    return _split_start(f"scatter_chips_start_{tag}", part, (S - 1, h, W), part.dtype, 3, _scatter_copies)


def _scatter_wait(flight, after, tag):
    return _split_wait(f"scatter_chips_wait_{tag}", flight, after, 3, _scatter_copies)[1]


def _join_copies(shard_ref, unused_ref):
    x, y, c = _me()
    h = shard_ref.shape[0] // 2
    rows = shard_ref.at[pl.ds(c * h, h), :]
    return [(rows, rows, (x, y, 1 - c))]


def _join_start(shard):
    return _split_start("join_halves_start", shard, (8, 128), shard.dtype, 1, _join_copies)


def _join_wait(flight, after):
    return _split_wait("join_halves_wait", flight, after, 1, _join_copies)[0]


def _adamw(name, g, g_row0, w, m, v):
    _, R, C = w.shape
    tm = next(cand for cand in (368, 256, 128, 64, 32, 16, 8) if R % cand == 0)
    assert g_row0 % tm == 0 and g.shape[1] == C

    def body(g_ref, w_ref, m_ref, v_ref, go_ref, d_ref, mo_ref, vo_ref):
        gt = g_ref[...]
        mt = ADAM_B1 * m_ref[...] + (1.0 - ADAM_B1) * gt
        vt = ADAM_B2 * v_ref[...] + (1.0 - ADAM_B2) * jnp.square(gt)
        m_hat = mt / (1.0 - ADAM_B1 ** ADAM_STEP)
        v_hat = vt / (1.0 - ADAM_B2 ** ADAM_STEP)
        go_ref[...] = gt
        d_ref[...] = -ADAM_LR * (m_hat / (jnp.sqrt(v_hat) + ADAM_EPS) + ADAM_WD * w_ref[...])
        mo_ref[...] = mt
        vo_ref[...] = vt

    state = pl.BlockSpec((None, tm, C), lambda i: (0, i, 0))
    return pl.pallas_call(
        body, name=name, grid=(R // tm,),
        in_specs=[pl.BlockSpec((tm, C), lambda i: (g_row0 // tm + i, 0)), state, state, state],
        out_specs=[state] * 4, out_shape=[jax.ShapeDtypeStruct((1, R, C), F32)] * 4,
        compiler_params=_cparams(("parallel",)),
    )(g, w, m, v)


def _unpack_weights(gathered, names):
    S = gathered.shape[0]
    shard_shapes = {"w_in": (D_MODEL, (QKV_WIDTH + 2 * D_MODEL) // S), "w_branch_na": (NA_WIDTH, D_MODEL // S),
                    "w_branch_dil": (DIL_OUT_WIDTH, D_MODEL // S), "w_out": (D_MODEL // S, D_MODEL),
                    "w_up": (D_MODEL, D_FF // S), "w_down": (D_FF // S, D_MODEL),
                    "w_ple_gate": (D_MODEL // S, D_MODEL), "w_ple_proj": (PLE_DIM, D_MODEL // S)}
    col_sharded = {"w_in", "w_branch_na", "w_branch_dil", "w_up", "w_ple_proj"}
    out, r0 = {}, 0
    for name in names:
        rows, cols = shard_shapes[name]
        n = rows * cols // PACK_W
        t = gathered[:, r0:r0 + n, :].reshape(S, rows, cols)
        r0 += n
        out[name] = t.transpose(1, 0, 2).reshape(rows, S * cols) if name in col_sharded else t.reshape(S * rows, cols)
    return out


def kernel(x, p, positions, g_mix, w_in, rpb, w_branch_na, w_branch_dil, w_out, g_mlp, w_up, w_down, g_ple, w_ple_gate, w_ple_proj, g_final, loss_target, m_g_mix, m_w_in, m_rpb, m_w_branch_na, m_w_branch_dil, m_w_out, m_g_mlp, m_w_up, m_w_down, m_g_ple, m_w_ple_gate, m_w_ple_proj, m_g_final, v_g_mix, v_w_in, v_rpb, v_w_branch_na, v_w_branch_dil, v_w_out, v_g_mlp, v_w_up, v_w_down, v_g_ple, v_w_ple_gate, v_w_ple_proj, v_g_final):
    shards = {"w_in": w_in[0], "w_branch_na": w_branch_na[0], "w_branch_dil": w_branch_dil[0], "w_out": w_out[0],
              "w_up": w_up[0], "w_down": w_down[0], "w_ple_gate": w_ple_gate[0], "w_ple_proj": w_ple_proj[0]}
    params = {"w_in": w_in, "w_branch_na": w_branch_na, "w_branch_dil": w_branch_dil, "w_out": w_out, "w_up": w_up,
              "w_down": w_down, "w_ple_gate": w_ple_gate, "w_ple_proj": w_ple_proj,
              "m_w_in": m_w_in, "m_w_branch_na": m_w_branch_na, "m_w_branch_dil": m_w_branch_dil, "m_w_out": m_w_out,
              "m_w_up": m_w_up, "m_w_down": m_w_down, "m_w_ple_gate": m_w_ple_gate, "m_w_ple_proj": m_w_ple_proj,
              "v_w_in": v_w_in, "v_w_branch_na": v_w_branch_na, "v_w_branch_dil": v_w_branch_dil, "v_w_out": v_w_out,
              "v_w_up": v_w_up, "v_w_down": v_w_down, "v_w_ple_gate": v_w_ple_gate, "v_w_ple_proj": v_w_ple_proj}

    xs, ps, tgt = x[0], p[0, 0], loss_target[0]
    T = xs.shape[0]
    TM = 512
    gm, gl, gp, gf = g_mix, g_mlp, g_ple, g_final.reshape(1, D_MODEL)

    across = _gather_halves_start(shards["w_in"].astype(BF), "in")
    a = _rowwise("norm_mix", lambda h, g: h * _rms(h) * g, T, TM, [_row(xs, TM), _full(gm)], [(D_MODEL, BF)],
                 after=(across.token,))
    cos2, sin_signed = _rope_tables(positions[0])
    tab = _na_bias_table(rpb[0])
    packed_mix = jnp.concatenate([_pack_rows(shards[n].astype(BF)) for n in GATHER_MIX], axis=0)
    packed_mlp = jnp.concatenate([_pack_rows(shards[n].astype(BF)) for n in GATHER_MLP], axis=0)
    w_in_shard, w_in_relay = _gather_halves_relay(across, (a, tab, cos2, sin_signed, packed_mix, packed_mlp), "in")
    w_in_all = _gather_halves_finish(w_in_shard, w_in_relay, w_in_relay.token, "in")
    w_in_full, w_gates = _assemble_w_in(w_in_all, 256)
    W = {"w_in": w_in_full}
    mix_flight = _gather_start(packed_mix, "mix", after=(w_in_all,))
    mlp_across = _gather_halves_start(packed_mlp, "mlp", after=(mix_flight.token,))

    n3 = 3 * NA_WIDTH
    qkv = _mm("in_na", a, W["w_in"], "nn", 1024, 768, 1024, [BF], after=(mlp_across.token,),
              b_view=(n3, (D_MODEL, 768), lambda j, k: (k, j)))
    z_dil = _mm("in_dil", a, W["w_in"], "nn", 1024, 768, 1024, [F32], after=(mlp_across.token,),
                b_view=(3 * DIL_WIDTH, (D_MODEL, 768), lambda j, k: (k, n3 // 768 + j)))
    z_gates = _mm("in_gates", a, w_gates, "nn", 1024,1024, 1024, [BF], after=(mlp_across.token,))

    dil_ops = _qkv_prep(z_dil, cos2, sin_signed, TM)
    y_na = _na_fwd(qkv, tab)
    band = [_band_fwd(*dil_ops[g], g) for g in range(len(DIL_GROUPS))]
    y_dil, w_grp, o_nat = _dil_merge_fwd([b[0] for b in band], [b[1] for b in band], T, TM)

    W.update(_unpack_weights(_gather_wait(mix_flight, y_dil, "mix"), GATHER_MIX))
    mlp_shard, mlp_relay = _gather_halves_relay(mlp_across, y_dil, "mlp")
    u_na = _mm("branch_na", y_na, W["w_branch_na"], "nn", 1024,1024, 512, [BF], after=(mlp_relay.token,))
    u_dil = _mm("branch_dil", y_dil, W["w_branch_dil"], "nn", 1024,1024, 256, [BF])
    mixed = _rowwise(
        "gate_mix", lambda gn, gd, un, ud: _sigmoid(gn.astype(F32)) * un.astype(F32) + _sigmoid(gd.astype(F32)) * ud.astype(F32), T, TM,
        [_row(z_gates, TM, 0, D_MODEL), _row(z_gates, TM, 1, D_MODEL), _row(u_na, TM), _row(u_dil, TM)], [(D_MODEL, BF)])
    def add_norm(d, h, g):
        h = h + d
        return h, h * _rms(h) * g

    h1, cn = _mm("out_proj", mixed, W["w_out"], "nn", 512, 1024, 1024, [F32, BF], epilogue=add_norm, extras=(xs,), consts=(gl,))
    mlp_all = _gather_halves_finish(mlp_shard, mlp_relay, cn, "mlp")
    W.update({n: t for n, t in _unpack_weights(mlp_all, GATHER_MLP).items() if n.startswith("w_ple")})
    chip_block = (None, D_MODEL, PACK_W)
    up, act = _mm("mlp_up", cn, mlp_all, "nn", 1024,1024, 1024, [BF, BF],
                  epilogue=lambda acc: (acc, jnp.square(jnp.maximum(acc, 0.0))), b_view=(D_FF, chip_block, lambda j, k: (j, 0, 0)))
    h2, en = _mm("mlp_down", act, mlp_all, "nn", 1024, 1024, 1024, [F32, BF], epilogue=add_norm, extras=(h1,), consts=(gp,),
                 b_view=(D_MODEL, chip_block, lambda j, k: (k, 1, 0)))
    pp = _mm("ple_proj", ps, W["w_ple_proj"], "nn", 1024,1024, 256, [F32])

    def head(gtt, h2t, ppt, tg, g):
        sg = _sigmoid(gtt)
        h3 = h2t + sg * ppt
        yo = h3 * _rms(h3) * g
        diff = yo - tg
        loss = 0.5 * jnp.sum(jnp.mean(jnp.square(diff), axis=-1, keepdims=True), axis=0, keepdims=True)
        dh3, dg = _rms_bwd(diff * (1.0 / D_MODEL), h3, g)
        return dh3, dh3 * ppt * sg * (1.0 - sg), dh3 * sg, jnp.broadcast_to(loss, (1, 128)), dg

    dh3, d_gt, d_pp, loss_part, dg_final = _mm(
        "ple_gate_loss_head", en, W["w_ple_gate"], "nn", 512, 1024, 1024, [F32, BF, BF], epilogue=head,
        extras=(h2, pp, tgt), consts=(gf,), sums=[128, D_MODEL])

    early_shapes = {n: shards[n].shape for n in REDUCE_EARLY}
    early_rows = sum(r * c for r, c in early_shapes.values()) // PACK_W
    shard_rows = D_MODEL // N_CHIPS
    early_buf = _mm("g_ple_gate", en, d_gt, "tn", 1024, 1024, 1024, [F32],
                    into=(jax.ShapeDtypeStruct((N_CHIPS, early_rows, PACK_W), F32), (N_CHIPS, shard_rows, PACK_W),
                          lambda i, j: (0, 2 * D_MODEL // shard_rows, 0)))
    g_ple_proj = _mm("g_ple_proj", ps, d_pp, "tn", 256, 1024, 1024,[F32])

    def add_norm_bwd(dn, dh_out, h, g):
        dh, dg = _rms_bwd(dn, h, g)
        dh = dh_out + dh
        return dh, dh, dg

    dh2, dh2_b, dg_ple = _mm("d_ple_gate", d_gt, W["w_ple_gate"], "nt", 512, 1024, 1024, [F32, BF],
                             epilogue=add_norm_bwd, extras=(dh3, h2), consts=(gp,), sums=[D_MODEL])
    d_up = _mm("d_mlp_down", dh2_b, mlp_all, "nt", 1024,1024, 1024, [BF], b_view=(D_FF, chip_block, lambda j, k: (j, 1, 0)),
               epilogue=lambda acc, u: (acc * (2.0 * jnp.maximum(u.astype(F32), 0.0)),), extras=(up,))
    early_buf = _mm("g_mlp_down", act, dh2_b, "tn", 1024, 1024, 1024,[F32],
                    into=(early_buf, (None, D_MODEL, PACK_W), lambda i, j: (i, 1, 0)))
    early_buf = _mm("g_mlp_up", cn, d_up, "tn", 1024, 1024, 1024,[F32],
                    into=(early_buf, (None, D_MODEL, PACK_W), lambda i, j: (j, 0, 0)))
    dh1, dh1_b, dg_mlp = _mm("d_mlp_up", d_up, mlp_all, "nt", 1024, 1024, 1024, [F32, BF], epilogue=add_norm_bwd,
                             b_view=(D_MODEL, chip_block, lambda j, k: (k, 0, 0)),
                             extras=(dh2, h1), consts=(gl,), sums=[D_MODEL])
    d_mixed = _mm("d_out_proj", dh1_b, W["w_out"], "nt", 1024,1024, 1024, [F32])
    early_buf = _mm("g_out_proj", mixed, dh1_b, "tn", 1024, 1024, 1024, [F32],
                    into=(early_buf, (N_CHIPS, shard_rows, PACK_W), lambda i, j: (0, 2 * D_MODEL // shard_rows + 1, 0)))

    def gate_bwd(dm, gn, gd, un, ud):
        gn, gd, un, ud = (t.astype(F32) for t in (gn, gd, un, ud))
        sn, sd = _sigmoid(gn), _sigmoid(gd)
        return jnp.concatenate([dm * un * sn * (1.0 - sn), dm * ud * sd * (1.0 - sd)], axis=1), dm * sn, dm * sd

    dz_gates, d_u_na, d_u_dil = _rowwise(
        "gate_mix_bwd", gate_bwd, T, TM,
        [_row(d_mixed, TM), _row(z_gates, TM, 0, D_MODEL), _row(z_gates, TM, 1, D_MODEL), _row(u_na, TM), _row(u_dil, TM)],
        [(2 * D_MODEL, BF), (D_MODEL, BF), (D_MODEL, BF)])
    g_branch_na = _mm("g_branch_na", y_na, d_u_na, "tn", 1024, 1024, 1024,[F32])
    g_branch_dil = _mm("g_branch_dil", y_dil, d_u_dil, "tn", 256, 1024, 1024,[F32])
    small_rows = [jnp.concatenate([_pack_rows(g[:, s * shard_rows:(s + 1) * shard_rows]) for g in (g_ple_proj, g_branch_na, g_branch_dil)],
                                  axis=0) for s in range(N_CHIPS)]
    early_buf = lax.dynamic_update_slice(early_buf, jnp.stack(small_rows), (0, 2 * D_MODEL + 2 * shard_rows, 0))
    early_tm = early_rows // 4
    swap_flight = _swap_start(early_buf, "early")
    d_y_na = _mm("d_branch_na", d_u_na, W["w_branch_na"], "nt", 1024,512, 1024, [BF], after=(swap_flight.token,))
    d_y_dil = _mm("d_branch_dil", d_u_dil, W["w_branch_dil"], "nt", 1024,256, 1024, [F32])

    dqa, dka, dva, dtab = _na_bwd(qkv, tab, d_y_na)
    early_g, early_got = _swap_wait(swap_flight, dqa, "early")
    early_pair, early_pair_b = _pair_sum(early_g, early_got, early_tm)
    scatter_flight = _scatter_start(early_pair_b, "early")

    do_res, dlse_res = _dil_merge_bwd(d_y_dil, o_nat, w_grp, TM, after=(scatter_flight.token,))
    d_dil = [_band_bwd(*dil_ops[g], do_res[g], dlse_res[g], g) for g in range(len(DIL_GROUPS))]

    dz_qkv = _qkv_unprep((dqa, dka, dva), d_dil, cos2, sin_signed, TM)
    in_cols = shards["w_in"].shape[1]
    qkv_rows, gate_tm = dz_qkv.shape[1], 256
    assert qkv_rows % gate_tm == 0
    g_in = _mm("g_in_qkv", dz_qkv, a, "tn", 1280, 1024, 1024, [F32],
               into=(jax.ShapeDtypeStruct((N_CHIPS * in_cols, D_MODEL), F32), (1280, D_MODEL), lambda i, j: (i, 0)))
    g_in = _mm("g_in_gates", dz_gates, a, "tn", gate_tm, 1024, T, [F32],
               into=(g_in, (gate_tm, D_MODEL), lambda i, j: (qkv_rows // gate_tm + i, 0)))
    early_mine = _chip_sum(early_pair, _scatter_wait(scatter_flight, (g_in,), "early"), early_tm)
    join_flight = _join_start(early_mine)

    late_tm = in_cols // 4
    late_swap = _swap_start(g_in.reshape(N_CHIPS, in_cols, D_MODEL), "late")
    d_a = _mm("d_in_qkv", dz_qkv, W["w_in"], "nt", 1024,1024, 1280, [F32], after=(late_swap.token, join_flight.token),
              b_view=(D_MODEL, (D_MODEL, 1280), lambda j, k: (j, k)))
    late_g, late_got = _swap_wait(late_swap, d_a, "late")
    late_pair, late_pair_b = _pair_sum(late_g, late_got, late_tm)
    late_scatter = _scatter_start(late_pair_b, "late")
    d_rpb = _na_rpb_grad(dtab, after=(late_scatter.token,))[:, :2 * NA_WIN_ROWS - 1, :2 * NA_WIN_COLS - 1]
    def first_bwd(dn_gates, dn_qkv, dh_out, h, g):
        dh, dg = _rms_bwd(dn_gates + dn_qkv, h, g)
        return dh_out + dh, dg

    grad_x, dg_mix = _mm("d_in_gates", dz_gates, w_gates, "nt", 512, 1024, 1024, [F32], epilogue=first_bwd,
                         extras=(d_a, dh1, xs), consts=(gm,), sums=[D_MODEL], after=(late_scatter.token,))
    early_shard = _join_wait(join_flight, grad_x)

    n_rpb = rpb.size
    rpb_rows = 4
    small = jnp.concatenate([
        dg_mix, dg_mlp, dg_ple, dg_final,
        jnp.pad(d_rpb.reshape(-1), (0, rpb_rows * D_MODEL - n_rpb)).reshape(rpb_rows, D_MODEL),
        jnp.pad(loss_part, ((0, 0), (0, D_MODEL - loss_part.shape[1]))),
        jnp.zeros((SMALL_ROWS - 5 - rpb_rows, D_MODEL), F32)], axis=0)
    out = {"grad": {}, "delta": {}, "new_m": {}, "new_v": {}}

    def update(n, g, row0):
        res = _adamw("adamw_" + n, g, row0, params[n], params["m_" + n], params["v_" + n])
        for kind, t in zip(("grad", "delta", "new_m", "new_v"), res, strict=True):
            out[kind][n] = t

    row0 = 0
    for n in REDUCE_EARLY:
        rows, cols = early_shapes[n]
        n_rows = rows * cols // PACK_W
        if cols == PACK_W:
            update(n, early_shard, row0)
        else:
            update(n, early_shard[row0:row0 + n_rows].reshape(rows, cols), 0)
        row0 += n_rows
    late_others = _scatter_wait(late_scatter, (*[out["new_v"][n] for n in REDUCE_EARLY], d_rpb), "late")
    late_mine = _chip_sum(late_pair, late_others, late_tm)
    small = _allreduce_small(small, after=(late_mine,))
    res = _adamw("adamw_w_in", _join_halves(late_mine), 0, *[jnp.swapaxes(params[n], 1, 2) for n in ("w_in", "m_w_in", "v_w_in")])
    for kind, t in zip(("grad", "delta", "new_m", "new_v"), res, strict=True):
        out[kind]["w_in"] = jnp.swapaxes(t, 1, 2)
    loss = small[4 + rpb_rows, 0]

    def small_pack(a0, a1, a2, a3, r):
        return jnp.concatenate([a0.reshape(1, -1), a1.reshape(1, -1), a2.reshape(1, -1), a3.reshape(1, -1),
                                jnp.pad(r.reshape(-1), (0, rpb_rows * D_MODEL - n_rpb)).reshape(rpb_rows, D_MODEL)], axis=0)

    small_res = _adamw("adamw_small", small, 0, small_pack(g_mix, g_mlp, g_ple, g_final, rpb)[None],
                       small_pack(m_g_mix, m_g_mlp, m_g_ple, m_g_final, m_rpb)[None],
                       small_pack(v_g_mix, v_g_mlp, v_g_ple, v_g_final, v_rpb)[None])

    def small_unpack(t):
        return {"g_mix": t[0].reshape(g_mix.shape), "g_mlp": t[1].reshape(g_mlp.shape), "g_ple": t[2].reshape(g_ple.shape),
                "g_final": t[3].reshape(g_final.shape), "rpb": t[4:].reshape(-1)[:n_rpb].reshape(rpb.shape)}

    for kind, t in zip(("grad", "delta", "new_m", "new_v"), small_res, strict=True):
        out[kind].update(small_unpack(t[0]))

    order = ["g_mix", "w_in", "rpb", "w_branch_na", "w_branch_dil", "w_out", "g_mlp", "w_up", "w_down", "g_ple",
             "w_ple_gate", "w_ple_proj", "g_final"]
    return (loss, grad_x[None], *[out["grad"][n] for n in order], *[out["delta"][n] for n in order],
            *[out["new_m"][n] for n in order], *[out["new_v"][n] for n in order])
```

```python
import functools
from typing import NamedTuple

import jax
import jax.numpy as jnp
from jax import lax
from jax.experimental import pallas as pl
from jax.experimental.pallas import tpu as pltpu

BF = jnp.bfloat16
F32 = jnp.float32
MESH = pl.DeviceIdType.MESH
ANY = pl.BlockSpec(memory_space=pl.ANY)

V7X_VMEM_BYTES = 64 * 1024 * 1024
VMEM_LIMIT = V7X_VMEM_BYTES - 16 * 1024 * 1024

D_MODEL = 1024
HEAD_DIM = 64
GRID_W = 64
NA_HEADS = 8
NA_WIN_ROWS = 8
NA_WIN_COLS = 16
NA_WIDTH = NA_HEADS * HEAD_DIM
DIL_GROUPS = ((128, 1), (512, 4), (2048, 16))
DIL_HPG = 4
DIL_HEADS = DIL_HPG * len(DIL_GROUPS)
DIL_WIDTH = DIL_HEADS * HEAD_DIM
DIL_OUT_WIDTH = DIL_HPG * HEAD_DIM
DIL_RADIUS = 64
QKV_WIDTH = 3 * NA_WIDTH + 3 * DIL_WIDTH
D_FF = 4 * D_MODEL
PLE_DIM = 256
ROPE_THETA = 10000.0
RMS_EPS = 1e-6
NEG_INF = -1e30
Q_SCALE = HEAD_DIM ** -0.5

ADAM_LR = 0.001
ADAM_B1 = 0.9
ADAM_B2 = 0.999
ADAM_EPS = 1e-08
ADAM_WD = 0.01
ADAM_STEP = 10

N_CHIPS = 4
N_DEV = 8
PACK_W = 1024
GATHER_MIX = ("w_branch_na", "w_branch_dil", "w_out")
GATHER_MLP = ("w_up", "w_down", "w_ple_gate", "w_ple_proj")
REDUCE_EARLY = ("w_up", "w_down", "w_ple_gate", "w_out", "w_ple_proj", "w_branch_na", "w_branch_dil")
SMALL_ROWS = 16


def _cparams(sem=None):
    return pltpu.CompilerParams(dimension_semantics=sem, vmem_limit_bytes=VMEM_LIMIT)


def _mm(name, a, b, mode, tm, tn, tk, out_dtypes, epilogue=None, extras=(), consts=(), sums=(), after=(), into=None,
        b_view=None):
    if mode == "nn":
        (M, K), N = a.shape, b.shape[1]
    elif mode == "nt":
        (M, K), N = a.shape, b.shape[0]
    else:
        (K, M), N = a.shape, b.shape[1]
    if b_view is not None:
        N = b_view[0]
    tm, tn, tk = min(tm, M), min(tn, N), min(tk, K)
    assert M % tm == 0 and N % tn == 0 and K % tk == 0, (name, M, N, K, tm, tn, tk)
    if mode == "nn":
        a_spec = pl.BlockSpec((tm, tk), lambda i, j, k: (i, k))
        b_spec = pl.BlockSpec((tk, tn), lambda i, j, k: (k, j))
        dims = (((1,), (0,)), ((), ()))
    elif mode == "nt":
        a_spec = pl.BlockSpec((tm, tk), lambda i, j, k: (i, k))
        b_spec = pl.BlockSpec((tn, tk), lambda i, j, k: (j, k))
        dims = (((1,), (1,)), ((), ()))
    else:
        a_spec = pl.BlockSpec((tk, tm), lambda i, j, k: (k, i))
        b_spec = pl.BlockSpec((tk, tn), lambda i, j, k: (k, j))
        dims = (((0,), (0,)), ((), ()))
    if b_view is not None:
        b_spec = pl.BlockSpec(b_view[1], lambda i, j, k: b_view[2](j, k))
    nk = K // tk
    n_extra, n_const, n_out, n_sum = len(extras), len(consts), len(out_dtypes), len(sums)
    tile = pl.BlockSpec((tm, tn), lambda i, j, k: (i, j))
    assert not sums or tn == N, "row sums need whole rows in a tile"
    wide = [e for e in (*extras, *out_dtypes) if isinstance(e, tuple)]
    assert not wide or tn == N
    extra_specs = [pl.BlockSpec((tm, tn), functools.partial(lambda c, i, j, k: (i, c), e[1])) if isinstance(e, tuple) else tile
                   for e in extras]
    extras = [e[0] if isinstance(e, tuple) else e for e in extras]
    out_widths = [d[1] if isinstance(d, tuple) else N for d in out_dtypes]
    out_dtypes = [d[0] if isinstance(d, tuple) else d for d in out_dtypes]

    n_after = len(after)

    def body(a_ref, b_ref, *rest):
        extra_refs, rest = rest[:n_extra + n_const], rest[n_extra + n_const + n_after:]
        out_refs, sum_refs, acc = rest[:n_out], rest[n_out:n_out + n_sum], rest[-1]
        i, k = pl.program_id(0), pl.program_id(2)
        def product():
            return lax.dot_general(a_ref[...].astype(BF), b_ref[...].astype(BF), dims, preferred_element_type=F32)

        if nk > 1:
            @pl.when(k == 0)
            def _():
                acc[...] = jnp.zeros_like(acc)

            acc[...] += product()

        @pl.when(k == nk - 1)
        def _():
            total = product() if nk == 1 else acc[...]
            outs = (total,) if epilogue is None else epilogue(total, *[e[...] for e in extra_refs])
            for o_ref, val in zip(out_refs, outs[:n_out], strict=True):
                o_ref[...] = val.astype(o_ref.dtype).reshape(o_ref.shape)
            for s_ref, val in zip(sum_refs, outs[n_out:], strict=True):
                @pl.when(i == 0)
                def _():
                    s_ref[...] = val

                @pl.when(i != 0)
                def _():
                    s_ref[...] += val

    out_specs = ([tile if w == N else pl.BlockSpec((tm, w), lambda i, j, k: (i, 0)) for w in out_widths]
                 + [pl.BlockSpec((1, c), lambda i, j, k: (0, 0)) for c in sums])
    out_shape = ([jax.ShapeDtypeStruct((M, w), dt) for dt, w in zip(out_dtypes, out_widths, strict=True)]
                 + [jax.ShapeDtypeStruct((1, c), F32) for c in sums])
    operands, aliases = [a, b, *extras, *consts, *after], {}
    in_specs = ([a_spec, b_spec] + extra_specs
                + [pl.BlockSpec(c.shape, functools.partial(lambda nd, i, j, k: (0,) * nd, c.ndim)) for c in consts] + [ANY] * n_after)
    if into is not None:
        assert n_out == 1
        target, block, index = into
        out_specs = [pl.BlockSpec(block, lambda i, j, k: index(i, j))]
        out_shape = [jax.ShapeDtypeStruct(target.shape, target.dtype)]
        if not isinstance(target, jax.ShapeDtypeStruct):
            aliases = {len(operands): 0}
            operands.append(target)
            in_specs.append(ANY)
            n_after += 1

    outs = pl.pallas_call(
        body, name=name, grid=(M // tm, N // tn, nk),
        in_specs=in_specs, out_specs=out_specs, out_shape=out_shape,
        scratch_shapes=[pltpu.VMEM((tm, tn) if nk > 1 else (8, 128), F32)], input_output_aliases=aliases,
        compiler_params=_cparams(("arbitrary",) * 3 if sums else ("parallel", "parallel", "arbitrary")),
    )(*operands)
    return outs[0] if len(outs) == 1 else outs


def _row(arr, tm, col_block=None, width=None):
    width = arr.shape[1] if width is None else width
    cb = 0 if col_block is None else col_block
    return arr, pl.BlockSpec((tm, width), lambda i: (i, cb))


def _full(arr):
    nd = arr.ndim
    return arr, pl.BlockSpec(arr.shape, lambda i: (0,) * nd)


def _rowwise(name, body, T, tm, ins, outs, sums=(), after=()):
    n_in, n_out, n_sum, n_after = len(ins), len(outs), len(sums), len(after)

    def kern(*refs):
        in_refs, refs = refs[:n_in], refs[n_in + n_after:]
        out_refs, sum_refs = refs[:n_out], refs[n_out:]
        res = body(*[r[...] for r in in_refs])
        res = res if isinstance(res, tuple) else (res,)
        for o_ref, val in zip(out_refs, res[:n_out], strict=True):
            o_ref[...] = val.astype(o_ref.dtype)
        if n_sum:
            @pl.when(pl.program_id(0) == 0)
            def _():
                for s_ref in sum_refs:
                    s_ref[...] = jnp.zeros_like(s_ref)

            for s_ref, val in zip(sum_refs, res[n_out:], strict=True):
                s_ref[...] += val

    res = pl.pallas_call(
        kern, name=name, grid=(T // tm,),
        in_specs=[spec for _, spec in ins] + [ANY] * n_after,
        out_specs=[pl.BlockSpec((tm, c), lambda i: (i, 0)) for c, _ in outs]
        + [pl.BlockSpec((1, c), lambda i: (0, 0)) for c in sums],
        out_shape=[jax.ShapeDtypeStruct((T, c), dt) for c, dt in outs]
        + [jax.ShapeDtypeStruct((1, c), F32) for c in sums],
        compiler_params=_cparams(("arbitrary",)),
    )(*[a for a, _ in ins], *after)
    return res[0] if len(res) == 1 else res


def _sigmoid(x):
    return 1.0 / (1.0 + jnp.exp(-x))


def _rms(h):
    return lax.rsqrt(jnp.mean(h * h, axis=-1, keepdims=True) + RMS_EPS)


def _rms_bwd(dy, h, g):
    r = _rms(h)
    n = h * r
    dn = dy * g
    dh = r * (dn - n * jnp.mean(dn * n, axis=-1, keepdims=True))
    return dh, jnp.sum(dy * n, axis=0, keepdims=True)


def _rope(x, cos2, sin_signed):
    lane = lax.broadcasted_iota(jnp.int32, x.shape, 1)
    swapped = jnp.where((lane % HEAD_DIM) < HEAD_DIM // 2, pltpu.roll(x, 128 - HEAD_DIM // 2, 1), pltpu.roll(x, HEAD_DIM // 2, 1))
    return x * cos2 + swapped * sin_signed


NA_KEYS = NA_WIN_ROWS * GRID_W
NA_BASES = 8


def _na_row_geometry(r, rows):
    first = jnp.clip(r - NA_WIN_ROWS // 2, 0, rows - NA_WIN_ROWS)
    base = first - r + (NA_WIN_ROWS - 1)
    return pl.multiple_of(first * GRID_W, GRID_W), base


NA_ROWS_PER_STEP = 16
NA_BWD_ROWS_PER_STEP = 8


def _softmax_rows(s):
    p = jnp.exp(s - jnp.max(s, axis=-1, keepdims=True))
    return p / jnp.sum(p, axis=-1, keepdims=True)


def _split_pair(t):
    first = lax.broadcasted_iota(jnp.int32, t.shape, 1) < HEAD_DIM
    zero = jnp.zeros_like(t)
    return jnp.where(first, t, zero), jnp.where(first, zero, t)


def _join_pair(a, b):
    return jnp.where(lax.broadcasted_iota(jnp.int32, a.shape, 1) < HEAD_DIM, a, b)


_NT = (((1,), (1,)), ((), ()))
_TN = (((0,), (0,)), ((), ()))


def _na_fwd(qkv, tab):
    T = qkv.shape[0]
    rows = T // GRID_W
    n_pairs = NA_WIDTH // 128

    def body(q_ref, k_ref, v_ref, tab_ref, y_ref):
        def step(it, carry):
            geo = [_na_row_geometry(it * NA_ROWS_PER_STEP + u, rows) for u in range(NA_ROWS_PER_STEP)]
            q0s = [pl.multiple_of((it * NA_ROWS_PER_STEP + u) * GRID_W, GRID_W) for u in range(NA_ROWS_PER_STEP)]
            ss = [lax.dot_general(jnp.concatenate(_split_pair(q_ref[pl.ds(q0, GRID_W), :] * Q_SCALE), axis=0),
                                  k_ref[pl.ds(k0, NA_KEYS), :], _NT, preferred_element_type=F32)
                  for q0, (k0, _) in zip(q0s, geo)]
            ps = [_softmax_rows(s + jnp.concatenate([tab_ref[0, base], tab_ref[1, base]], axis=0)) for s, (_, base) in zip(ss, geo)]
            ys = [jnp.dot(p.astype(BF), v_ref[pl.ds(k0, NA_KEYS), :], preferred_element_type=F32) for p, (k0, _) in zip(ps, geo)]
            for q0, y2 in zip(q0s, ys):
                y_ref[pl.ds(q0, GRID_W), :] = _join_pair(y2[:GRID_W], y2[GRID_W:]).astype(y_ref.dtype)
            return carry

        lax.fori_loop(0, rows // NA_ROWS_PER_STEP, step, 0)

    def cols(first):
        return pl.BlockSpec((T, 128), lambda j: (0, first + j))

    return pl.pallas_call(
        body, name="na_fwd", grid=(n_pairs,),
        in_specs=[cols(0), cols(n_pairs), cols(2 * n_pairs), pl.BlockSpec((2, NA_BASES, GRID_W, NA_KEYS), lambda j: (j, 0, 0, 0))],
        out_specs=cols(0), out_shape=jax.ShapeDtypeStruct((T, NA_WIDTH), BF),
        compiler_params=_cparams(("parallel",)),
    )(qkv, qkv, qkv, tab)


def _na_bwd(qkv, tab, do):
    T = qkv.shape[0]
    rows = T // GRID_W
    n_pairs = NA_WIDTH // 128

    def body(q_ref, k_ref, v_ref, tab_ref, do_ref, dq_ref, dk_out, dv_out, dtab_ref, dk_ref, dv_ref):
        dk_ref[...] = jnp.zeros_like(dk_ref)
        dv_ref[...] = jnp.zeros_like(dv_ref)
        dtab_ref[...] = jnp.zeros_like(dtab_ref)

        def step(it, carry):
            U = NA_BWD_ROWS_PER_STEP
            geo = [_na_row_geometry(it * U + u, rows) for u in range(U)]
            q0s = [pl.multiple_of((it * U + u) * GRID_W, GRID_W) for u in range(U)]
            q2s = [jnp.concatenate(_split_pair(q_ref[pl.ds(q0, GRID_W), :] * Q_SCALE), axis=0) for q0 in q0s]
            do2s = [jnp.concatenate(_split_pair(do_ref[pl.ds(q0, GRID_W), :]), axis=0) for q0 in q0s]
            ss = [lax.dot_general(q2, k_ref[pl.ds(k0, NA_KEYS), :], _NT, preferred_element_type=F32) for q2, (k0, _) in zip(q2s, geo)]
            dps = [lax.dot_general(do2, v_ref[pl.ds(k0, NA_KEYS), :], _NT, preferred_element_type=F32) for do2, (k0, _) in zip(do2s, geo)]
            ps = [_softmax_rows(s + jnp.concatenate([tab_ref[0, base], tab_ref[1, base]], axis=0)) for s, (_, base) in zip(ss, geo)]
            dss = [p * (dp - jnp.sum(dp * p, axis=-1, keepdims=True)) for p, dp in zip(ps, dps)]
            dvs = [lax.dot_general(p.astype(BF), do2, _TN, preferred_element_type=F32) for p, do2 in zip(ps, do2s)]
            dsbs = [ds.astype(BF) for ds in dss]
            dqs = [jnp.dot(dsb, k_ref[pl.ds(k0, NA_KEYS), :], preferred_element_type=F32) for dsb, (k0, _) in zip(dsbs, geo)]
            dks = [lax.dot_general(dsb, q2, _TN, preferred_element_type=F32) for dsb, q2 in zip(dsbs, q2s)]
            for u in range(U):
                k0, base = geo[u]
                dtab_ref[0, base] += dss[u][:GRID_W]
                dtab_ref[1, base] += dss[u][GRID_W:]
                dq_ref[pl.ds(q0s[u], GRID_W), :] = (_join_pair(dqs[u][:GRID_W], dqs[u][GRID_W:]) * Q_SCALE).astype(dq_ref.dtype)
                dk_ref[pl.ds(k0, NA_KEYS), :] += dks[u]
                dv_ref[pl.ds(k0, NA_KEYS), :] += dvs[u]
            return carry

        lax.fori_loop(0, rows // NA_BWD_ROWS_PER_STEP, step, 0)
        dk_out[...] = dk_ref[...].astype(dk_out.dtype)
        dv_out[...] = dv_ref[...].astype(dv_out.dtype)

    def cols(first):
        return pl.BlockSpec((T, 128), lambda j: (0, first + j))

    tabs = pl.BlockSpec((2, NA_BASES, GRID_W, NA_KEYS), lambda j: (j, 0, 0, 0))
    wide = jax.ShapeDtypeStruct((T, NA_WIDTH), BF)
    return pl.pallas_call(
        body, name="na_bwd", grid=(n_pairs,),
        in_specs=[cols(0), cols(n_pairs), cols(2 * n_pairs), tabs, cols(0)],
        out_specs=[cols(0), cols(0), cols(0), tabs],
        out_shape=[wide, wide, wide, jax.ShapeDtypeStruct((NA_HEADS, NA_BASES, GRID_W, NA_KEYS), F32)],
        scratch_shapes=[pltpu.VMEM((T, 128), F32), pltpu.VMEM((T, 128), F32)],
        compiler_params=_cparams(("parallel",)),
    )(qkv, qkv, qkv, tab, do)


def _na_bias_table(rpb):
    H, n_rows, n_cols = rpb.shape

    def body(r_ref, tab_ref):
        q = lax.broadcasted_iota(jnp.int32, (GRID_W, 128), 0)
        kc = lax.broadcasted_iota(jnp.int32, (GRID_W, 128), 1)
        first = jnp.clip(q - NA_WIN_COLS // 2, 0, GRID_W - NA_WIN_COLS)
        valid = (kc >= first) & (kc < first + NA_WIN_COLS)
        toeplitz = []
        for ro in range(n_rows):
            row = jnp.broadcast_to(r_ref[pl.ds(ro, 1), :], (GRID_W, 128))
            shifted = pltpu.roll(pltpu.roll(row, 128 - (NA_WIN_COLS - 1), 1), 0, 1, stride=1, stride_axis=0)
            toeplitz.append(jnp.where(valid, shifted, NEG_INF))
        for base in range(NA_BASES):
            for j in range(NA_WIN_ROWS // 2):
                even, odd = toeplitz[base + 2 * j], toeplitz[base + 2 * j + 1]
                tab_ref[base, :, pl.ds(j * 128, 128)] = jnp.where(kc < GRID_W, even, pltpu.roll(odd, GRID_W, 1))

    padded = jnp.pad(rpb, ((0, 0), (0, 16 - n_rows), (0, 128 - n_cols)))
    return pl.pallas_call(
        body, name="na_bias_table", grid=(H,),
        in_specs=[pl.BlockSpec((None, 16, 128), lambda h: (h, 0, 0))],
        out_specs=pl.BlockSpec((None, NA_BASES, GRID_W, NA_KEYS), lambda h: (h, 0, 0, 0)),
        out_shape=jax.ShapeDtypeStruct((H, NA_BASES, GRID_W, NA_KEYS), F32),
        compiler_params=_cparams(("parallel",)),
    )(padded)


def _na_rpb_grad(dtab, after=()):
    H = dtab.shape[0]
    n_rows = 2 * NA_WIN_ROWS - 1
    n_cols = 2 * NA_WIN_COLS - 1

    def body(d_ref, *rest):
        o_ref = rest[-1]
        lane = lax.broadcasted_iota(jnp.int32, (GRID_W, 128), 1)
        low = lane < GRID_W
        flip = (lax.broadcasted_iota(jnp.int32, (GRID_W, GRID_W), 0) + lax.broadcasted_iota(jnp.int32, (GRID_W, GRID_W), 1)
                == GRID_W - 1).astype(BF)

        def reverse_rows(t):
            out = jnp.zeros_like(t)
            for _ in range(3):
                piece = t.astype(BF)
                out = out + jnp.dot(flip, piece, preferred_element_type=F32)
                t = t - piece.astype(F32)
            return out

        out_rows = []
        for ro in range(n_rows):
            acc = jnp.zeros((GRID_W, 128), F32)
            for base in range(NA_BASES):
                i = ro - base
                if not 0 <= i < NA_WIN_ROWS:
                    continue
                pair = d_ref[base, :, pl.ds((i // 2) * 128, 128)]
                if i % 2:
                    pair = pltpu.roll(pair, GRID_W, 1)
                acc = acc + jnp.where(low, pair, 0.0)
            skew = pltpu.roll(reverse_rows(acc), 0, 1, stride=1, stride_axis=0)
            diag = jnp.sum(skew, axis=0, keepdims=True)
            out_rows.append(pltpu.roll(jnp.broadcast_to(diag, (8, 128)), 128 - (GRID_W - NA_WIN_COLS), 1)[:1])
        out_rows.append(jnp.zeros((1, 128), F32))
        res = jnp.concatenate(out_rows, axis=0)
        o_ref[...] = jnp.where(lax.broadcasted_iota(jnp.int32, res.shape, 1) < n_cols, res, 0.0)

    return pl.pallas_call(
        body, name="na_rpb_grad", grid=(H,),
        in_specs=[pl.BlockSpec((None, NA_BASES, GRID_W, NA_KEYS), lambda h: (h, 0, 0, 0))] + [ANY] * len(after),
        out_specs=pl.BlockSpec((None, n_rows + 1, 128), lambda h: (h, 0, 0)),
        out_shape=jax.ShapeDtypeStruct((H, n_rows + 1, 128), F32),
        compiler_params=_cparams(("parallel",)),
    )(dtab, *after)


BAND_Q = 128
BAND_KEYS = BAND_Q + 2 * DIL_RADIUS


def _band_geometry(n, L):
    q0 = pl.multiple_of(n * BAND_Q, BAND_Q)
    k0 = pl.multiple_of(jnp.clip(q0 - DIL_RADIUS, 0, L - BAND_KEYS), DIL_RADIUS)
    qi = q0 + lax.broadcasted_iota(jnp.int32, (BAND_Q, BAND_KEYS), 0)
    kj = k0 + lax.broadcasted_iota(jnp.int32, (BAND_Q, BAND_KEYS), 1)
    return q0, k0, jnp.abs(qi - kj) <= DIL_RADIUS


DIL_PAIRS = DIL_OUT_WIDTH // 128


def _residue_shape(dil, T, dtype):
    return jax.ShapeDtypeStruct((DIL_PAIRS, dil, T // dil, 128), dtype)


def _residue_tile(dil, tm):
    return pl.BlockSpec((DIL_PAIRS, dil, tm // dil, 128), lambda i: (0, 0, i, 0))


def _to_natural(ref, scratch, dil, tm):
    tiles = []
    for pair in range(DIL_PAIRS):
        if dil == 1:
            tiles.append(ref[pair, 0].astype(F32))
            continue
        for r in range(dil):
            scratch[pl.ds(r, tm // dil, stride=dil), :] = ref[pair, r].astype(F32)
        tiles.append(scratch[...])
    return tiles


def _from_natural(tile, scratch, ref, pair, dil, tm):
    if dil == 1:
        ref[pair, 0] = tile.astype(ref.dtype)
        return
    scratch[...] = tile
    for r in range(dil):
        ref[pair, r] = scratch[pl.ds(r, tm // dil, stride=dil), :].astype(ref.dtype)


def _band_specs(group, T):
    dil = DIL_GROUPS[group][1]
    L = T // dil
    assert L % BAND_Q == 0 and L >= BAND_KEYS, (T, dil)
    per_residue = min(BAND_BLOCKS_PER_STEP, L // BAND_Q)
    residues = min(dil, BAND_BLOCKS_PER_STEP // per_residue)
    spec = pl.BlockSpec((None, residues, L, 128), lambda s: (s % DIL_PAIRS, s // DIL_PAIRS, 0, 0))
    return L, residues, per_residue, (dil // residues * DIL_PAIRS,), spec


BAND_BLOCKS_PER_STEP = 8


def _band_softmax(s, valid):
    s = jnp.where(valid, s, NEG_INF)
    m = jnp.max(s, axis=-1, keepdims=True)
    p = jnp.exp(s - m)
    l = jnp.sum(p, axis=-1, keepdims=True)
    return p / l, m + jnp.log(l)


def _band_fwd(q, k, v, group):
    T = q.shape[1] * q.shape[2]
    L, residues, U, grid, spec = _band_specs(group, T)

    def body(q_ref, k_ref, v_ref, o_ref, lse_ref):
        def step(it, carry):
            geo = [(r, *_band_geometry(it * U + u, L)) for r in range(residues) for u in range(U)]
            ss = [lax.dot_general(jnp.concatenate(_split_pair(q_ref[r, pl.ds(q0, BAND_Q), :]), axis=0),
                                  k_ref[r, pl.ds(k0, BAND_KEYS), :], _NT, preferred_element_type=F32) for r, q0, k0, _ in geo]
            pls = [_band_softmax(s, jnp.concatenate([valid, valid], axis=0)) for s, (_, _, _, valid) in zip(ss, geo)]
            os = [jnp.dot(p.astype(BF), v_ref[r, pl.ds(k0, BAND_KEYS), :], preferred_element_type=F32)
                  for (p, _), (r, _, k0, _) in zip(pls, geo)]
            for (r, q0, _, _), o2, (_, lse) in zip(geo, os, pls):
                o_ref[r, pl.ds(q0, BAND_Q), :] = _join_pair(o2[:BAND_Q], o2[BAND_Q:])
                lse2 = jnp.broadcast_to(lse, (2 * BAND_Q, 128))
                lse_ref[r, pl.ds(q0, BAND_Q), :] = _join_pair(lse2[:BAND_Q], lse2[BAND_Q:])
            return carry

        lax.fori_loop(0, L // (BAND_Q * U), step, 0)

    res = _residue_shape(DIL_GROUPS[group][1], T, F32)
    return pl.pallas_call(
        body, name=f"band_fwd_g{group}", grid=grid,
        in_specs=[spec] * 3, out_specs=[spec] * 2, out_shape=[res, res],
        compiler_params=_cparams(("parallel",)),
    )(q, k, v)


def _band_bwd(q, k, v, do, dlse, group):
    T = q.shape[1] * q.shape[2]
    L, residues, U, grid, spec = _band_specs(group, T)

    def body(q_ref, k_ref, v_ref, do_ref, dlse_ref, dq_ref, dk_ref, dv_ref):
        dk_ref[...] = jnp.zeros_like(dk_ref)
        dv_ref[...] = jnp.zeros_like(dv_ref)

        def step(it, carry):
            geo = [(r, *_band_geometry(it * U + u, L)) for r in range(residues) for u in range(U)]
            q2s = [jnp.concatenate(_split_pair(q_ref[r, pl.ds(q0, BAND_Q), :]), axis=0) for r, q0, _, _ in geo]
            do2s = [jnp.concatenate(_split_pair(do_ref[r, pl.ds(q0, BAND_Q), :]), axis=0) for r, q0, _, _ in geo]
            ss = [lax.dot_general(q2, k_ref[r, pl.ds(k0, BAND_KEYS), :], _NT, preferred_element_type=F32)
                  for q2, (r, _, k0, _) in zip(q2s, geo)]
            dps = [lax.dot_general(do2, v_ref[r, pl.ds(k0, BAND_KEYS), :], _NT, preferred_element_type=F32)
                   for do2, (r, _, k0, _) in zip(do2s, geo)]
            ps = [_band_softmax(s, jnp.concatenate([valid, valid], axis=0))[0] for s, (_, _, _, valid) in zip(ss, geo)]
            dss = []
            for p, dp, (r, q0, _, _) in zip(ps, dps, geo):
                dl = dlse_ref[r, pl.ds(q0, BAND_Q), :]
                dl2 = jnp.concatenate([dl[:, :1], dl[:, HEAD_DIM:HEAD_DIM + 1]], axis=0)
                dss.append(p * (dp - jnp.sum(dp * p, axis=-1, keepdims=True) + dl2))
            dvs = [lax.dot_general(p.astype(BF), do2, _TN, preferred_element_type=F32) for p, do2 in zip(ps, do2s)]
            dsbs = [ds.astype(BF) for ds in dss]
            dqs = [jnp.dot(dsb, k_ref[r, pl.ds(k0, BAND_KEYS), :], preferred_element_type=F32) for dsb, (r, _, k0, _) in zip(dsbs, geo)]
            dks = [lax.dot_general(dsb, q2, _TN, preferred_element_type=F32) for dsb, q2 in zip(dsbs, q2s)]
            for u, (r, q0, k0, _) in enumerate(geo):
                dq_ref[r, pl.ds(q0, BAND_Q), :] = _join_pair(dqs[u][:BAND_Q], dqs[u][BAND_Q:])
                dk_ref[r, pl.ds(k0, BAND_KEYS), :] += dks[u]
                dv_ref[r, pl.ds(k0, BAND_KEYS), :] += dvs[u]
            return carry

        lax.fori_loop(0, L // (BAND_Q * U), step, 0)

    res = _residue_shape(DIL_GROUPS[group][1], T, F32)
    return pl.pallas_call(
        body, name=f"band_bwd_g{group}", grid=grid,
        in_specs=[spec] * 5, out_specs=[spec] * 3, out_shape=[res] * 3,
        compiler_params=_cparams(("parallel",)),
    )(q, k, v, do, dlse)


def _head_sums(t):
    head = lax.broadcasted_iota(jnp.int32, t.shape, 1) // HEAD_DIM
    out = jnp.zeros_like(t)
    for h in range(t.shape[1] // HEAD_DIM):
        mine = head == h
        out = jnp.where(mine, jnp.sum(jnp.where(mine, t, 0.0), axis=-1, keepdims=True), out)
    return out


def _dil_merge_fwd(os, lses, T, tm):
    G = len(DIL_GROUPS)
    W = DIL_OUT_WIDTH
    dils = [d for _, d in DIL_GROUPS]

    def body(*refs):
        o_refs, lse_refs = refs[:G], refs[G:2 * G]
        y_ref, w_refs, on_refs, scratch = refs[2 * G], refs[2 * G + 1:3 * G + 1], refs[3 * G + 1:4 * G + 1], refs[-1]
        o = [jnp.concatenate(_to_natural(r, scratch, d, tm), axis=1) for r, d in zip(o_refs, dils)]
        ls = [jnp.concatenate(_to_natural(r, scratch, d, tm), axis=1) for r, d in zip(lse_refs, dils)]
        m = functools.reduce(jnp.maximum, ls)
        es = [jnp.exp(l - m) for l in ls]
        tot = functools.reduce(jnp.add, es)
        ws = [e / tot for e in es]
        y_ref[...] = functools.reduce(jnp.add, [w * t for w, t in zip(ws, o)]).astype(y_ref.dtype)
        for g in range(G):
            w_refs[g][...] = ws[g]
            on_refs[g][...] = o[g]

    nat = pl.BlockSpec((tm, W), lambda i: (i, 0))
    res = pl.pallas_call(
        body, name="dil_merge_fwd", grid=(T // tm,),
        in_specs=[_residue_tile(d, tm) for d in dils] * 2,
        out_specs=[nat] * (2 * G + 1),
        out_shape=[jax.ShapeDtypeStruct((T, W), BF)] + [jax.ShapeDtypeStruct((T, W), F32)] * (2 * G),
        scratch_shapes=[pltpu.VMEM((tm, 128), F32)],
        compiler_params=_cparams(("parallel",)),
    )(*os, *lses)
    return res[0], res[1:G + 1], res[G + 1:]


def _dil_merge_bwd(dy, os, ws, tm, after=()):
    G = len(DIL_GROUPS)
    T, W = dy.shape
    dils = [d for _, d in DIL_GROUPS]
    n_after = len(after)

    def body(*refs):
        dyt = refs[0][...]
        o, w = [r[...] for r in refs[1:G + 1]], [r[...] for r in refs[G + 1:2 * G + 1]]
        refs = refs[2 * G + 1 + n_after:]
        do_refs, dlse_refs, scratch = refs[:G], refs[G:2 * G], refs[-1]
        dws = [_head_sums(dyt * t) for t in o]
        mean = functools.reduce(jnp.add, [a * b for a, b in zip(w, dws)])
        for g, d in enumerate(dils):
            do, dlse = w[g] * dyt, w[g] * (dws[g] - mean)
            for pair in range(DIL_PAIRS):
                cols = slice(pair * 128, (pair + 1) * 128)
                _from_natural(do[:, cols], scratch, do_refs[g], pair, d, tm)
                _from_natural(dlse[:, cols], scratch, dlse_refs[g], pair, d, tm)

    nat = pl.BlockSpec((tm, W), lambda i: (i, 0))
    res = pl.pallas_call(
        body, name="dil_merge_bwd", grid=(T // tm,),
        in_specs=[nat] * (2 * G + 1) + [ANY] * n_after,
        out_specs=[_residue_tile(d, tm) for d in dils] * 2,
        out_shape=[_residue_shape(d, T, BF) for d in dils] + [_residue_shape(d, T, F32) for d in dils],
        scratch_shapes=[pltpu.VMEM((tm, 128), F32)],
        compiler_params=_cparams(("parallel",)),
    )(dy, *os, *ws, *after)
    return res[:G], res[G:]


def _qkv_prep(z, cos2, sin_signed, tm):
    T = z.shape[0]
    G = len(DIL_GROUPS)
    dils = [d for _, d in DIL_GROUPS]
    n_dil_blocks = 3 * DIL_WIDTH // 128

    def body(*refs):
        blocks = refs[:n_dil_blocks]
        cos_ref, sin_ref = refs[n_dil_blocks], refs[1 + n_dil_blocks]
        outs = refs[2 + n_dil_blocks:]
        for part in range(3):
            for g, d in enumerate(dils):
                out = outs[g * 3 + part]
                for pair in range(DIL_PAIRS):
                    blk = blocks[part * (DIL_WIDTH // 128) + g * DIL_PAIRS + pair]
                    for r in range(d):
                        rows = pl.ds(r, tm // d, stride=d) if d > 1 else slice(None)
                        x = blk[rows, :]
                        if part < 2:
                            x = _rope(x, cos_ref[rows, :], sin_ref[rows, :])
                        if part == 0:
                            x = x * Q_SCALE
                        out[pair, r] = x.astype(out.dtype)

    lane_block = [pl.BlockSpec((tm, 128), functools.partial(lambda b, i: (i, b), b)) for b in range(n_dil_blocks)]
    tab = pl.BlockSpec((tm, 128), lambda i: (i, 0))
    res = pl.pallas_call(
        body, name="qkv_prep", grid=(T // tm,),
        in_specs=lane_block + [tab, tab],
        out_specs=[_residue_tile(d, tm) for d in dils for _ in range(3)],
        out_shape=[_residue_shape(d, T, BF) for d in dils for _ in range(3)],
        compiler_params=_cparams(("parallel",)),
    )(*[z] * n_dil_blocks, cos2, sin_signed)
    return [res[3 * g:3 + 3 * g] for g in range(G)]


def _qkv_unprep(d_na, d_dil, cos2, sin_signed, tm, after=()):
    T = d_na[0].shape[0]
    G = len(DIL_GROUPS)
    dils = [d for _, d in DIL_GROUPS]
    n_after = len(after)

    def body(*refs):
        dq, dk, dv = (r[...] for r in refs[:3])
        res_refs = refs[3:3 + 3 * G]
        cs, sn = refs[3 + 3 * G][...], refs[4 + 3 * G][...]
        out, scratch = refs[5 + 3 * G + n_after], refs[-1]
        cols = [dq, dk, dv]
        for part in range(3):
            for g, d in enumerate(dils):
                for x in _to_natural(res_refs[g * 3 + part], scratch, d, tm):
                    if part < 2:
                        x = _rope(x, cs, -sn)
                    cols.append((x * Q_SCALE if part == 0 else x).astype(out.dtype))
        out[...] = jnp.concatenate(cols, axis=1)

    wide = pl.BlockSpec((tm, NA_WIDTH), lambda i: (i, 0))
    tab = pl.BlockSpec((tm, 128), lambda i: (i, 0))
    return pl.pallas_call(
        body, name="qkv_unprep", grid=(T // tm,),
        in_specs=[wide] * 3 + [_residue_tile(d, tm) for d in dils for _ in range(3)] + [tab, tab] + [ANY] * n_after,
        out_specs=pl.BlockSpec((tm, QKV_WIDTH), lambda i: (i, 0)),
        out_shape=jax.ShapeDtypeStruct((T, QKV_WIDTH), BF),
        scratch_shapes=[pltpu.VMEM((tm, 128), F32)],
        compiler_params=_cparams(("parallel",)),
    )(*d_na, *[t for g in range(G) for t in d_dil[g]], cos2, sin_signed, *after)


def _rope_tables(positions):
    half = HEAD_DIM // 2
    inv_freq = ROPE_THETA ** (-jnp.arange(half, dtype=F32) / half)
    ang = positions.astype(F32)[:, None] * inv_freq
    cos, sin = jnp.cos(ang), jnp.sin(ang)
    return jnp.tile(jnp.concatenate([cos, cos], axis=1), (1, 2)), jnp.tile(jnp.concatenate([-sin, sin], axis=1), (1, 2))


def _pack_rows(t):
    return t.reshape(-1, PACK_W)


def _me():
    return lax.axis_index("x"), lax.axis_index("y"), lax.axis_index("c")


def _other_chips(x, y):
    return [(1 - x, y), (x, 1 - y), (1 - x, 1 - y)]


def _pair_sum(g, got, tm):
    S, R, W = g.shape
    half = R // 2
    nb = half // tm

    def body(pos_ref, g_ref, got_ref, own_ref, ob_ref):
        tot = g_ref[...] + got_ref[...]
        ob_ref[...] = tot.astype(ob_ref.dtype)

        @pl.when(pl.program_id(1) == pos_ref[1])
        def _():
            own_ref[...] = tot

    tile = pl.BlockSpec((None, tm, W), lambda i, s, pos_ref: (s, i, 0))
    c, chip = lax.axis_index("c"), 2 * lax.axis_index("x") + lax.axis_index("y")
    return pl.pallas_call(
        body, name="pair_sum",
        grid_spec=pltpu.PrefetchScalarGridSpec(
            num_scalar_prefetch=1, grid=(nb, S),
            in_specs=[pl.BlockSpec((None, tm, W), lambda i, s, pos_ref: (s, pos_ref[0] * nb + i, 0)), tile],
            out_specs=[pl.BlockSpec((tm, W), lambda i, s, pos_ref: (i, 0)), tile]),
        out_shape=[jax.ShapeDtypeStruct((half, W), F32), jax.ShapeDtypeStruct((S, half, W), BF)],
        compiler_params=_cparams(("parallel", "arbitrary")),
    )(jnp.stack([c, chip]).astype(jnp.int32), g, got)


def _chip_sum(own, others, tm):
    n, h, W = others.shape
    nb = h // tm

    def body(c_ref, own_ref, p_ref, o_ref):
        o_ref[...] = ((own_ref[...] + p_ref[0].astype(F32)) + p_ref[1].astype(F32)) + p_ref[2].astype(F32)

    return pl.pallas_call(
        body, name="chip_sum",
        grid_spec=pltpu.PrefetchScalarGridSpec(
            num_scalar_prefetch=1, grid=(nb,),
            in_specs=[pl.BlockSpec((tm, W), lambda i, c_ref: (i, 0)), pl.BlockSpec((n, tm, W), lambda i, c_ref: (0, i, 0))],
            out_specs=pl.BlockSpec((tm, W), lambda i, c_ref: (c_ref[0] * nb + i, 0))),
        out_shape=jax.ShapeDtypeStruct((2 * h, W), F32),
        compiler_params=_cparams(("parallel",)),
    )(lax.axis_index("c").reshape(1).astype(jnp.int32), own, others)


def _join_halves(shard):
    h = shard.shape[0] // 2

    def body(in_ref, out_ref, send_sem, recv_sem):
        x, y, c = _me()
        cp = pltpu.make_async_remote_copy(
            src_ref=in_ref.at[pl.ds(c * h, h), :], dst_ref=out_ref.at[pl.ds(c * h, h), :],
            send_sem=send_sem, recv_sem=recv_sem, device_id=(x, y, 1 - c), device_id_type=MESH)
        cp.start()
        pltpu.make_async_remote_copy(
            src_ref=in_ref.at[pl.ds(c * h, h), :], dst_ref=out_ref.at[pl.ds((1 - c) * h, h), :],
            send_sem=send_sem, recv_sem=recv_sem, device_id=(x, y, 1 - c), device_id_type=MESH).wait_recv()
        cp.wait_send()

    return pl.pallas_call(
        body, name="join_halves", in_specs=[ANY], out_specs=ANY,
        out_shape=jax.ShapeDtypeStruct(shard.shape, shard.dtype), input_output_aliases={0: 0},
        scratch_shapes=[pltpu.SemaphoreType.DMA, pltpu.SemaphoreType.DMA],
    )(shard)


def _allreduce_small(s, after=()):
    R, W = s.shape
    n_after = len(after)

    def body(s_ref, *rest):
        o_ref, buf, send_sems, recv_sems = rest[n_after:]
        x, y, c = _me()
        me = 4 * x + 2 * y + c
        buf[me] = s_ref[...]
        peers = [((x + fx) % 2, (y + fy) % 2, (c + fc) % 2) for fx in range(2) for fy in range(2) for fc in range(2)][1:]
        sends = [pltpu.make_async_remote_copy(
            src_ref=s_ref, dst_ref=buf.at[me], send_sem=send_sems.at[k], recv_sem=recv_sems.at[k],
            device_id=peer, device_id_type=MESH) for k, peer in enumerate(peers)]
        for cp in sends:
            cp.start()
        for k, peer in enumerate(peers):
            pltpu.make_async_remote_copy(
                src_ref=s_ref, dst_ref=buf.at[4 * peer[0] + 2 * peer[1] + peer[2]], send_sem=send_sems.at[k],
                recv_sem=recv_sems.at[k], device_id=peer, device_id_type=MESH).wait_recv()
        for cp in sends:
            cp.wait_send()
        total = buf[0]
        for d in range(1, N_DEV):
            total = total + buf[d]
        o_ref[...] = total

    return pl.pallas_call(
        body, name="allreduce_small",
        in_specs=[pl.BlockSpec(memory_space=pltpu.VMEM)] + [ANY] * n_after, out_specs=pl.BlockSpec(memory_space=pltpu.VMEM),
        out_shape=jax.ShapeDtypeStruct((R, W), F32),
        scratch_shapes=[pltpu.VMEM((N_DEV, R, W), F32), pltpu.SemaphoreType.DMA((N_DEV - 1,)), pltpu.SemaphoreType.DMA((N_DEV - 1,))],
    )(s, *after)


HBM_SPEC = pl.BlockSpec(memory_space=pltpu.HBM)
SEM_SPEC = pl.BlockSpec(memory_space=pltpu.SEMAPHORE)
DATAFLOW = pltpu.SideEffectType.DATAFLOW_SIDE_EFFECTING


class _InFlight(NamedTuple):
    sems: tuple
    src: jax.Array
    land: jax.Array
    token: jax.Array


def _split_start(name, src, land_shape, land_dtype, n, copies, after=()):
    n_after = len(after)

    def body(src_ref, land_ref, *rest):
        rest = rest[n_after:]
        sems, token = rest[:2 * n], rest[-1]
        for k, (s, d, peer) in enumerate(copies(src_ref, land_ref)):
            pltpu.make_async_remote_copy(src_ref=s, dst_ref=d, send_sem=sems[k], recv_sem=sems[n + k],
                                         device_id=peer, device_id_type=MESH).start()
        token[...] = jnp.zeros_like(token)

    outs = pl.pallas_call(
        body, name=name,
        out_shape=(*[pltpu.SemaphoreType.DMA(())] * (2 * n), pltpu.HBM(src.shape, src.dtype), pltpu.HBM(land_shape, land_dtype),
                   jax.ShapeDtypeStruct((8, 128), F32)),
        in_specs=(HBM_SPEC, HBM_SPEC, *[ANY] * n_after),
        out_specs=(*[SEM_SPEC] * (2 * n), HBM_SPEC, HBM_SPEC, pl.BlockSpec(memory_space=pltpu.VMEM)),
        input_output_aliases={0: 2 * n, 1: 2 * n + 1},
        compiler_params=pltpu.CompilerParams(has_side_effects=DATAFLOW),
    )(pltpu.with_memory_space_constraint(src, pltpu.HBM), pltpu.with_memory_space_constraint(lax.empty(land_shape, land_dtype), pltpu.HBM),
      *after)
    return _InFlight(tuple(outs[:2 * n]), outs[2 * n], outs[2 * n + 1], outs[2 * n + 2])


def _split_wait(name, flight, after, n, copies):
    after = after if isinstance(after, tuple) else (after,)

    def body(src_ref, land_ref, *rest):
        sems = rest[:2 * n]
        for k, (s, d, peer) in enumerate(copies(src_ref, land_ref)):
            cp = pltpu.make_async_remote_copy(src_ref=s, dst_ref=d, send_sem=sems[k], recv_sem=sems[n + k],
                                              device_id=peer, device_id_type=MESH)
            cp.wait_send()
            cp.wait_recv()

    return pl.pallas_call(
        body, name=name,
        out_shape=(pltpu.HBM(flight.src.shape, flight.src.dtype), pltpu.HBM(flight.land.shape, flight.land.dtype)),
        in_specs=(HBM_SPEC, HBM_SPEC, *[SEM_SPEC] * (2 * n), *[ANY] * len(after)),
        out_specs=(HBM_SPEC, HBM_SPEC), input_output_aliases={0: 0, 1: 1},
        compiler_params=pltpu.CompilerParams(has_side_effects=DATAFLOW),
    )(flight.src, flight.land, *flight.sems, *after)


def _gather_copies(src_ref, land_ref):
    x, y, c = _me()
    return [(src_ref, land_ref.at[2 * x + y], (*chip, c)) for chip in _other_chips(x, y)]


def _gather_start(packed, tag, after=()):
    return _split_start(f"gather_start_{tag}", packed, (N_CHIPS, *packed.shape), packed.dtype, 3, _gather_copies, after)


def _gather_wait(flight, after, tag):
    src, others = _split_wait(f"gather_wait_{tag}", flight, after, 3, _gather_copies)
    return lax.dynamic_update_slice(others, src[None], (2 * lax.axis_index("x") + lax.axis_index("y"), 0, 0))


def _across_copies(src_ref, land_ref):
    x, y, c = _me()
    half = src_ref.shape[0] // 2
    rows = pl.ds(c * half, half)
    return [(src_ref.at[rows, :], land_ref.at[2 * x + y, rows, :], (*chip, c)) for chip in _other_chips(x, y)]


def _to_sibling_copies(all_ref, unused_ref):
    x, y, c = _me()
    half = all_ref.shape[1] // 2
    places = [all_ref.at[2 * chip[0] + chip[1], pl.ds(c * half, half), :] for chip in _other_chips(x, y)]
    return [(place, place, (x, y, 1 - c)) for place in places]


def _gather_halves_start(shard, tag, after=()):
    return _split_start(f"gather_{tag}_across_start", shard, (N_CHIPS, *shard.shape), shard.dtype, 3, _across_copies, after)


def _gather_halves_relay(flight, after, tag):
    shard, landed = _split_wait(f"gather_{tag}_across_wait", flight, after, 3, _across_copies)
    return shard, _split_start(f"gather_{tag}_sibling_start", landed, (8, 128), landed.dtype, 3, _to_sibling_copies)


def _gather_halves_finish(shard, relay, after, tag):
    others = _split_wait(f"gather_{tag}_sibling_wait", relay, after, 3, _to_sibling_copies)[0]
    return lax.dynamic_update_slice(others, shard[None], (2 * lax.axis_index("x") + lax.axis_index("y"), 0, 0))


def _assemble_w_in(shards, tm):
    S, R, C = shards.shape
    n_gates = 2 * D_MODEL

    def body(s_ref, w_ref, g_ref):
        full = jnp.concatenate([s_ref[s] for s in range(S)], axis=1)
        w_ref[...] = full
        g_ref[...] = full[:, S * C - n_gates:]

    return pl.pallas_call(
        body, name="assemble_w_in", grid=(R // tm,),
        in_specs=[pl.BlockSpec((S, tm, C), lambda i: (0, i, 0))],
        out_specs=[pl.BlockSpec((tm, S * C), lambda i: (i, 0)), pl.BlockSpec((tm, n_gates), lambda i: (i, 0))],
        out_shape=[jax.ShapeDtypeStruct((R, S * C), shards.dtype), jax.ShapeDtypeStruct((R, n_gates), shards.dtype)],
        compiler_params=_cparams(("parallel",)),
    )(shards)


def _swap_copies(src_ref, land_ref):
    x, y, c = _me()
    half = land_ref.shape[1]
    return [(src_ref.at[:, pl.ds((1 - c) * half, half), :], land_ref, (x, y, 1 - c))]


def _swap_start(g, tag):
    S, R, W = g.shape
    return _split_start(f"swap_halves_start_{tag}", g, (S, R // 2, W), g.dtype, 1, _swap_copies)


def _swap_wait(flight, after, tag):
    return _split_wait(f"swap_halves_wait_{tag}", flight, after, 1, _swap_copies)


def _scatter_copies(src_ref, land_ref):
    x, y, c = _me()
    return [(src_ref.at[2 * chip[0] + chip[1]], land_ref.at[j], (*chip, c)) for j, chip in enumerate(_other_chips(x, y))]


def _scatter_start(part, tag):
    S, h, W = part.shape
    return _split_start(f"scatter_chips_start_{tag}", part, (S - 1, h, W), part.dtype, 3, _scatter_copies)


def _scatter_wait(flight, after, tag):
    return _split_wait(f"scatter_chips_wait_{tag}", flight, after, 3, _scatter_copies)[1]


def _join_copies(shard_ref, unused_ref):
    x, y, c = _me()
    h = shard_ref.shape[0] // 2
    rows = shard_ref.at[pl.ds(c * h, h), :]
    return [(rows, rows, (x, y, 1 - c))]


def _join_start(shard):
    return _split_start("join_halves_start", shard, (8, 128), shard.dtype, 1, _join_copies)


def _join_wait(flight, after):
    return _split_wait("join_halves_wait", flight, after, 1, _join_copies)[0]


def _adamw(name, g, g_row0, w, m, v):
    _, R, C = w.shape
    tm = next(cand for cand in (368, 256, 128, 64, 32, 16, 8) if R % cand == 0)
    assert g_row0 % tm == 0 and g.shape[1] == C

    def body(g_ref, w_ref, m_ref, v_ref, go_ref, d_ref, mo_ref, vo_ref):
        gt = g_ref[...]
        mt = ADAM_B1 * m_ref[...] + (1.0 - ADAM_B1) * gt
        vt = ADAM_B2 * v_ref[...] + (1.0 - ADAM_B2) * jnp.square(gt)
        m_hat = mt / (1.0 - ADAM_B1 ** ADAM_STEP)
        v_hat = vt / (1.0 - ADAM_B2 ** ADAM_STEP)
        go_ref[...] = gt
        d_ref[...] = -ADAM_LR * (m_hat / (jnp.sqrt(v_hat) + ADAM_EPS) + ADAM_WD * w_ref[...])
        mo_ref[...] = mt
        vo_ref[...] = vt

    state = pl.BlockSpec((None, tm, C), lambda i: (0, i, 0))
    return pl.pallas_call(
        body, name=name, grid=(R // tm,),
        in_specs=[pl.BlockSpec((tm, C), lambda i: (g_row0 // tm + i, 0)), state, state, state],
        out_specs=[state] * 4, out_shape=[jax.ShapeDtypeStruct((1, R, C), F32)] * 4,
        compiler_params=_cparams(("parallel",)),
    )(g, w, m, v)


def _unpack_weights(gathered, names):
    S = gathered.shape[0]
    shard_shapes = {"w_in": (D_MODEL, (QKV_WIDTH + 2 * D_MODEL) // S), "w_branch_na": (NA_WIDTH, D_MODEL // S),
                    "w_branch_dil": (DIL_OUT_WIDTH, D_MODEL // S), "w_out": (D_MODEL // S, D_MODEL),
                    "w_up": (D_MODEL, D_FF // S), "w_down": (D_FF // S, D_MODEL),
                    "w_ple_gate": (D_MODEL // S, D_MODEL), "w_ple_proj": (PLE_DIM, D_MODEL // S)}
    col_sharded = {"w_in", "w_branch_na", "w_branch_dil", "w_up", "w_ple_proj"}
    out, r0 = {}, 0
    for name in names:
        rows, cols = shard_shapes[name]
        n = rows * cols // PACK_W
        t = gathered[:, r0:r0 + n, :].reshape(S, rows, cols)
        r0 += n
        out[name] = t.transpose(1, 0, 2).reshape(rows, S * cols) if name in col_sharded else t.reshape(S * rows, cols)
    return out


def kernel(x, p, positions, g_mix, w_in, rpb, w_branch_na, w_branch_dil, w_out, g_mlp, w_up, w_down, g_ple, w_ple_gate, w_ple_proj, g_final, loss_target, m_g_mix, m_w_in, m_rpb, m_w_branch_na, m_w_branch_dil, m_w_out, m_g_mlp, m_w_up, m_w_down, m_g_ple, m_w_ple_gate, m_w_ple_proj, m_g_final, v_g_mix, v_w_in, v_rpb, v_w_branch_na, v_w_branch_dil, v_w_out, v_g_mlp, v_w_up, v_w_down, v_g_ple, v_w_ple_gate, v_w_ple_proj, v_g_final):
    shards = {"w_in": w_in[0], "w_branch_na": w_branch_na[0], "w_branch_dil": w_branch_dil[0], "w_out": w_out[0],
              "w_up": w_up[0], "w_down": w_down[0], "w_ple_gate": w_ple_gate[0], "w_ple_proj": w_ple_proj[0]}
    params = {"w_in": w_in, "w_branch_na": w_branch_na, "w_branch_dil": w_branch_dil, "w_out": w_out, "w_up": w_up,
              "w_down": w_down, "w_ple_gate": w_ple_gate, "w_ple_proj": w_ple_proj,
              "m_w_in": m_w_in, "m_w_branch_na": m_w_branch_na, "m_w_branch_dil": m_w_branch_dil, "m_w_out": m_w_out,
              "m_w_up": m_w_up, "m_w_down": m_w_down, "m_w_ple_gate": m_w_ple_gate, "m_w_ple_proj": m_w_ple_proj,
              "v_w_in": v_w_in, "v_w_branch_na": v_w_branch_na, "v_w_branch_dil": v_w_branch_dil, "v_w_out": v_w_out,
              "v_w_up": v_w_up, "v_w_down": v_w_down, "v_w_ple_gate": v_w_ple_gate, "v_w_ple_proj": v_w_ple_proj}

    xs, ps, tgt = x[0], p[0, 0], loss_target[0]
    T = xs.shape[0]
    TM = 512
    gm, gl, gp, gf = g_mix, g_mlp, g_ple, g_final.reshape(1, D_MODEL)

    across = _gather_halves_start(shards["w_in"].astype(BF), "in")
    a = _rowwise("norm_mix", lambda h, g: h * _rms(h) * g, T, TM, [_row(xs, TM), _full(gm)], [(D_MODEL, BF)],
                 after=(across.token,))
    cos2, sin_signed = _rope_tables(positions[0])
    tab = _na_bias_table(rpb[0])
    packed_mix = jnp.concatenate([_pack_rows(shards[n].astype(BF)) for n in GATHER_MIX], axis=0)
    packed_mlp = jnp.concatenate([_pack_rows(shards[n].astype(BF)) for n in GATHER_MLP], axis=0)
    w_in_shard, w_in_relay = _gather_halves_relay(across, (a, tab, cos2, sin_signed, packed_mix, packed_mlp), "in")
    w_in_all = _gather_halves_finish(w_in_shard, w_in_relay, w_in_relay.token, "in")
    w_in_full, w_gates = _assemble_w_in(w_in_all, 256)
    W = {"w_in": w_in_full}
    mix_flight = _gather_start(packed_mix, "mix", after=(w_in_all,))
    mlp_across = _gather_halves_start(packed_mlp, "mlp", after=(mix_flight.token,))

    n3 = 3 * NA_WIDTH
    qkv = _mm("in_na", a, W["w_in"], "nn", 1024, 768, 1024, [BF], after=(mlp_across.token,),
              b_view=(n3, (D_MODEL, 768), lambda j, k: (k, j)))
    z_dil = _mm("in_dil", a, W["w_in"], "nn", 1024, 768, 1024, [F32], after=(mlp_across.token,),
                b_view=(3 * DIL_WIDTH, (D_MODEL, 768), lambda j, k: (k, n3 // 768 + j)))
    z_gates = _mm("in_gates", a, w_gates, "nn", 1024,1024, 1024, [BF], after=(mlp_across.token,))

    dil_ops = _qkv_prep(z_dil, cos2, sin_signed, TM)
    y_na = _na_fwd(qkv, tab)
    band = [_band_fwd(*dil_ops[g], g) for g in range(len(DIL_GROUPS))]
    y_dil, w_grp, o_nat = _dil_merge_fwd([b[0] for b in band], [b[1] for b in band], T, TM)

    W.update(_unpack_weights(_gather_wait(mix_flight, y_dil, "mix"), GATHER_MIX))
    mlp_shard, mlp_relay = _gather_halves_relay(mlp_across, y_dil, "mlp")
    u_na = _mm("branch_na", y_na, W["w_branch_na"], "nn", 1024,1024, 512, [BF], after=(mlp_relay.token,))
    u_dil = _mm("branch_dil", y_dil, W["w_branch_dil"], "nn", 1024,1024, 256, [BF])
    mixed = _rowwise(
        "gate_mix", lambda gn, gd, un, ud: _sigmoid(gn.astype(F32)) * un.astype(F32) + _sigmoid(gd.astype(F32)) * ud.astype(F32), T, TM,
        [_row(z_gates, TM, 0, D_MODEL), _row(z_gates, TM, 1, D_MODEL), _row(u_na, TM), _row(u_dil, TM)], [(D_MODEL, BF)])
    def add_norm(d, h, g):
        h = h + d
        return h, h * _rms(h) * g

    h1, cn = _mm("out_proj", mixed, W["w_out"], "nn", 512, 1024, 1024, [F32, BF], epilogue=add_norm, extras=(xs,), consts=(gl,))
    mlp_all = _gather_halves_finish(mlp_shard, mlp_relay, cn, "mlp")
    W.update({n: t for n, t in _unpack_weights(mlp_all, GATHER_MLP).items() if n.startswith("w_ple")})
    chip_block = (None, D_MODEL, PACK_W)
    up, act = _mm("mlp_up", cn, mlp_all, "nn", 1024,1024, 1024, [BF, BF],
                  epilogue=lambda acc: (acc, jnp.square(jnp.maximum(acc, 0.0))), b_view=(D_FF, chip_block, lambda j, k: (j, 0, 0)))
    h2, en = _mm("mlp_down", act, mlp_all, "nn", 1024, 1024, 1024, [F32, BF], epilogue=add_norm, extras=(h1,), consts=(gp,),
                 b_view=(D_MODEL, chip_block, lambda j, k: (k, 1, 0)))
    pp = _mm("ple_proj", ps, W["w_ple_proj"], "nn", 1024,1024, 256, [F32])

    def head(gtt, h2t, ppt, tg, g):
        sg = _sigmoid(gtt)
        h3 = h2t + sg * ppt
        yo = h3 * _rms(h3) * g
        diff = yo - tg
        loss = 0.5 * jnp.sum(jnp.mean(jnp.square(diff), axis=-1, keepdims=True), axis=0, keepdims=True)
        dh3, dg = _rms_bwd(diff * (1.0 / D_MODEL), h3, g)
        return dh3, dh3 * ppt * sg * (1.0 - sg), dh3 * sg, jnp.broadcast_to(loss, (1, 128)), dg

    dh3, d_gt, d_pp, loss_part, dg_final = _mm(
        "ple_gate_loss_head", en, W["w_ple_gate"], "nn", 512, 1024, 1024, [F32, BF, BF], epilogue=head,
        extras=(h2, pp, tgt), consts=(gf,), sums=[128, D_MODEL])

    early_shapes = {n: shards[n].shape for n in REDUCE_EARLY}
    early_rows = sum(r * c for r, c in early_shapes.values()) // PACK_W
    shard_rows = D_MODEL // N_CHIPS
    early_buf = _mm("g_ple_gate", en, d_gt, "tn", 1024, 1024, 1024, [F32],
                    into=(jax.ShapeDtypeStruct((N_CHIPS, early_rows, PACK_W), F32), (N_CHIPS, shard_rows, PACK_W),
                          lambda i, j: (0, 2 * D_MODEL // shard_rows, 0)))
    g_ple_proj = _mm("g_ple_proj", ps, d_pp, "tn", 256, 1024, 1024,[F32])

    def add_norm_bwd(dn, dh_out, h, g):
        dh, dg = _rms_bwd(dn, h, g)
        dh = dh_out + dh
        return dh, dh, dg

    dh2, dh2_b, dg_ple = _mm("d_ple_gate", d_gt, W["w_ple_gate"], "nt", 512, 1024, 1024, [F32, BF],
                             epilogue=add_norm_bwd, extras=(dh3, h2), consts=(gp,), sums=[D_MODEL])
    d_up = _mm("d_mlp_down", dh2_b, mlp_all, "nt", 1024,1024, 1024, [BF], b_view=(D_FF, chip_block, lambda j, k: (j, 1, 0)),
               epilogue=lambda acc, u: (acc * (2.0 * jnp.maximum(u.astype(F32), 0.0)),), extras=(up,))
    early_buf = _mm("g_mlp_down", act, dh2_b, "tn", 1024, 1024, 1024,[F32],
                    into=(early_buf, (None, D_MODEL, PACK_W), lambda i, j: (i, 1, 0)))
    early_buf = _mm("g_mlp_up", cn, d_up, "tn", 1024, 1024, 1024,[F32],
                    into=(early_buf, (None, D_MODEL, PACK_W), lambda i, j: (j, 0, 0)))
    dh1, dh1_b, dg_mlp = _mm("d_mlp_up", d_up, mlp_all, "nt", 1024, 1024, 1024, [F32, BF], epilogue=add_norm_bwd,
                             b_view=(D_MODEL, chip_block, lambda j, k: (k, 0, 0)),
                             extras=(dh2, h1), consts=(gl,), sums=[D_MODEL])
    early_buf = _mm("g_out_proj", mixed, dh1_b, "tn", 1024, 1024, 1024, [F32],
                    into=(early_buf, (N_CHIPS, shard_rows, PACK_W), lambda i, j: (0, 2 * D_MODEL // shard_rows + 1, 0)))

    def gate_bwd(dm, gn, gd, un, ud):
        gn, gd, un, ud = (t.astype(F32) for t in (gn, gd, un, ud))
        sn, sd = _sigmoid(gn), _sigmoid(gd)
        return jnp.concatenate([dm * un * sn * (1.0 - sn), dm * ud * sd * (1.0 - sd)], axis=1), dm * sn, dm * sd

    dz_gates, d_u_na, d_u_dil = _mm("d_out_proj", dh1_b, W["w_out"], "nt", 512, 1024, 1024, [(BF, 2 * D_MODEL), BF, BF],
                                    epilogue=gate_bwd, extras=((z_gates, 0), (z_gates, 1), u_na, u_dil))
    g_branch_na = _mm("g_branch_na", y_na, d_u_na, "tn", 1024, 1024, 1024,[F32])
    g_branch_dil = _mm("g_branch_dil", y_dil, d_u_dil, "tn", 256, 1024, 1024,[F32])
    small_rows = [jnp.concatenate([_pack_rows(g[:, s * shard_rows:(s + 1) * shard_rows]) for g in (g_ple_proj, g_branch_na, g_branch_dil)],
                                  axis=0) for s in range(N_CHIPS)]
    early_buf = lax.dynamic_update_slice(early_buf, jnp.stack(small_rows), (0, 2 * D_MODEL + 2 * shard_rows, 0))
    early_tm = early_rows // 4
    swap_flight = _swap_start(early_buf, "early")
    d_y_na = _mm("d_branch_na", d_u_na, W["w_branch_na"], "nt", 1024,512, 1024, [BF], after=(swap_flight.token,))
    d_y_dil = _mm("d_branch_dil", d_u_dil, W["w_branch_dil"], "nt", 1024,256, 1024, [F32])

    dqa, dka, dva, dtab = _na_bwd(qkv, tab, d_y_na)
    early_g, early_got = _swap_wait(swap_flight, dqa, "early")
    early_pair, early_pair_b = _pair_sum(early_g, early_got, early_tm)
    scatter_flight = _scatter_start(early_pair_b, "early")

    do_res, dlse_res = _dil_merge_bwd(d_y_dil, o_nat, w_grp, TM, after=(scatter_flight.token,))
    d_dil = [_band_bwd(*dil_ops[g], do_res[g], dlse_res[g], g) for g in range(len(DIL_GROUPS))]

    dz_qkv = _qkv_unprep((dqa, dka, dva), d_dil, cos2, sin_signed, TM)
    in_cols = shards["w_in"].shape[1]
    qkv_rows, gate_tm = dz_qkv.shape[1], 256
    assert qkv_rows % gate_tm == 0
    g_in = _mm("g_in_qkv", dz_qkv, a, "tn", 1280, 1024, 1024, [F32],
               into=(jax.ShapeDtypeStruct((N_CHIPS * in_cols, D_MODEL), F32), (1280, D_MODEL), lambda i, j: (i, 0)))
    g_in = _mm("g_in_gates", dz_gates, a, "tn", gate_tm, 1024, T, [F32],
               into=(g_in, (gate_tm, D_MODEL), lambda i, j: (qkv_rows // gate_tm + i, 0)))
    early_mine = _chip_sum(early_pair, _scatter_wait(scatter_flight, (g_in,), "early"), early_tm)
    join_flight = _join_start(early_mine)

    late_tm = in_cols // 4
    late_swap = _swap_start(g_in.reshape(N_CHIPS, in_cols, D_MODEL), "late")
    d_a = _mm("d_in_qkv", dz_qkv, W["w_in"], "nt", 1024,1024, 1280, [F32], after=(late_swap.token, join_flight.token),
              b_view=(D_MODEL, (D_MODEL, 1280), lambda j, k: (j, k)))
    late_g, late_got = _swap_wait(late_swap, d_a, "late")
    late_pair, late_pair_b = _pair_sum(late_g, late_got, late_tm)
    late_scatter = _scatter_start(late_pair_b, "late")
    d_rpb = _na_rpb_grad(dtab, after=(late_scatter.token,))[:, :2 * NA_WIN_ROWS - 1, :2 * NA_WIN_COLS - 1]
    def first_bwd(dn_gates, dn_qkv, dh_out, h, g):
        dh, dg = _rms_bwd(dn_gates + dn_qkv, h, g)
        return dh_out + dh, dg

    grad_x, dg_mix = _mm("d_in_gates", dz_gates, w_gates, "nt", 512, 1024, 1024, [F32], epilogue=first_bwd,
                         extras=(d_a, dh1, xs), consts=(gm,), sums=[D_MODEL], after=(late_scatter.token,))
    early_shard = _join_wait(join_flight, grad_x)

    n_rpb = rpb.size
    rpb_rows = 4
    small = jnp.concatenate([
        dg_mix, dg_mlp, dg_ple, dg_final,
        jnp.pad(d_rpb.reshape(-1), (0, rpb_rows * D_MODEL - n_rpb)).reshape(rpb_rows, D_MODEL),
        jnp.pad(loss_part, ((0, 0), (0, D_MODEL - loss_part.shape[1]))),
        jnp.zeros((SMALL_ROWS - 5 - rpb_rows, D_MODEL), F32)], axis=0)
    out = {"grad": {}, "delta": {}, "new_m": {}, "new_v": {}}

    def update(n, g, row0):
        res = _adamw("adamw_" + n, g, row0, params[n], params["m_" + n], params["v_" + n])
        for kind, t in zip(("grad", "delta", "new_m", "new_v"), res, strict=True):
            out[kind][n] = t

    row0 = 0
    for n in REDUCE_EARLY:
        rows, cols = early_shapes[n]
        n_rows = rows * cols // PACK_W
        if cols == PACK_W:
            update(n, early_shard, row0)
        else:
            update(n, early_shard[row0:row0 + n_rows].reshape(rows, cols), 0)
        row0 += n_rows
    late_others = _scatter_wait(late_scatter, (*[out["new_v"][n] for n in REDUCE_EARLY], d_rpb), "late")
    late_mine = _chip_sum(late_pair, late_others, late_tm)
    small = _allreduce_small(small, after=(late_mine,))
    res = _adamw("adamw_w_in", _join_halves(late_mine), 0, *[jnp.swapaxes(params[n], 1, 2) for n in ("w_in", "m_w_in", "v_w_in")])
    for kind, t in zip(("grad", "delta", "new_m", "new_v"), res, strict=True):
        out[kind]["w_in"] = jnp.swapaxes(t, 1, 2)
    loss = small[4 + rpb_rows, 0]

    def small_pack(a0, a1, a2, a3, r):
        return jnp.concatenate([a0.reshape(1, -1), a1.reshape(1, -1), a2.reshape(1, -1), a3.reshape(1, -1),
                                jnp.pad(r.reshape(-1), (0, rpb_rows * D_MODEL - n_rpb)).reshape(rpb_rows, D_MODEL)], axis=0)

    small_res = _adamw("adamw_small", small, 0, small_pack(g_mix, g_mlp, g_ple, g_final, rpb)[None],
                       small_pack(m_g_mix, m_g_mlp, m_g_ple, m_g_final, m_rpb)[None],
                       small_pack(v_g_mix, v_g_mlp, v_g_ple, v_g_final, v_rpb)[None])

    def small_unpack(t):
        return {"g_mix": t[0].reshape(g_mix.shape), "g_mlp": t[1].reshape(g_mlp.shape), "g_ple": t[2].reshape(g_ple.shape),
                "g_final": t[3].reshape(g_final.shape), "rpb": t[4:].reshape(-1)[:n_rpb].reshape(rpb.shape)}

    for kind, t in zip(("grad", "delta", "new_m", "new_v"), small_res, strict=True):
        out[kind].update(small_unpack(t[0]))

    order = ["g_mix", "w_in", "rpb", "w_branch_na", "w_branch_dil", "w_out", "g_mlp", "w_up", "w_down", "g_ple",
             "w_ple_gate", "w_ple_proj", "g_final"]
    return (loss, grad_x[None], *[out["grad"][n] for n in order], *[out["delta"][n] for n in order],
            *[out["new_m"][n] for n in order], *[out["new_v"][n] for n in order])
```

```python
import functools
from typing import NamedTuple

import jax
import jax.numpy as jnp
from jax import lax
from jax.experimental import pallas as pl
from jax.experimental.pallas import tpu as pltpu

BF = jnp.bfloat16
F32 = jnp.float32
MESH = pl.DeviceIdType.MESH
ANY = pl.BlockSpec(memory_space=pl.ANY)

V7X_VMEM_BYTES = 64 * 1024 * 1024
VMEM_LIMIT = V7X_VMEM_BYTES - 16 * 1024 * 1024

D_MODEL = 1024
HEAD_DIM = 64
GRID_W = 64
NA_HEADS = 8
NA_WIN_ROWS = 8
NA_WIN_COLS = 16
NA_WIDTH = NA_HEADS * HEAD_DIM
DIL_GROUPS = ((128, 1), (512, 4), (2048, 16))
DIL_HPG = 4
DIL_HEADS = DIL_HPG * len(DIL_GROUPS)
DIL_WIDTH = DIL_HEADS * HEAD_DIM
DIL_OUT_WIDTH = DIL_HPG * HEAD_DIM
DIL_RADIUS = 64
QKV_WIDTH = 3 * NA_WIDTH + 3 * DIL_WIDTH
D_FF = 4 * D_MODEL
PLE_DIM = 256
ROPE_THETA = 10000.0
RMS_EPS = 1e-6
NEG_INF = -1e30
Q_SCALE = HEAD_DIM ** -0.5

ADAM_LR = 0.001
ADAM_B1 = 0.9
ADAM_B2 = 0.999
ADAM_EPS = 1e-08
ADAM_WD = 0.01
ADAM_STEP = 10

N_CHIPS = 4
N_DEV = 8
PACK_W = 1024
GATHER_MIX = ("w_branch_na", "w_branch_dil", "w_out")
GATHER_MLP = ("w_up", "w_down", "w_ple_gate", "w_ple_proj")
REDUCE_EARLY = ("w_up", "w_down", "w_ple_gate", "w_out", "w_ple_proj", "w_branch_na", "w_branch_dil")
SMALL_ROWS = 16


def _cparams(sem=None):
    return pltpu.CompilerParams(dimension_semantics=sem, vmem_limit_bytes=VMEM_LIMIT)


def _mm(name, a, b, mode, tm, tn, tk, out_dtypes, epilogue=None, extras=(), consts=(), sums=(), after=(), into=None,
        b_view=None):
    if mode == "nn":
        (M, K), N = a.shape, b.shape[1]
    elif mode == "nt":
        (M, K), N = a.shape, b.shape[0]
    else:
        (K, M), N = a.shape, b.shape[1]
    if b_view is not None:
        N = b_view[0]
    tm, tn, tk = min(tm, M), min(tn, N), min(tk, K)
    assert M % tm == 0 and N % tn == 0 and K % tk == 0, (name, M, N, K, tm, tn, tk)
    if mode == "nn":
        a_spec = pl.BlockSpec((tm, tk), lambda i, j, k: (i, k))
        b_spec = pl.BlockSpec((tk, tn), lambda i, j, k: (k, j))
        dims = (((1,), (0,)), ((), ()))
    elif mode == "nt":
        a_spec = pl.BlockSpec((tm, tk), lambda i, j, k: (i, k))
        b_spec = pl.BlockSpec((tn, tk), lambda i, j, k: (j, k))
        dims = (((1,), (1,)), ((), ()))
    else:
        a_spec = pl.BlockSpec((tk, tm), lambda i, j, k: (k, i))
        b_spec = pl.BlockSpec((tk, tn), lambda i, j, k: (k, j))
        dims = (((0,), (0,)), ((), ()))
    if b_view is not None:
        b_spec = pl.BlockSpec(b_view[1], lambda i, j, k: b_view[2](j, k))
    nk = K // tk
    n_extra, n_const, n_out, n_sum = len(extras), len(consts), len(out_dtypes), len(sums)
    tile = pl.BlockSpec((tm, tn), lambda i, j, k: (i, j))
    assert not sums or tn == N, "row sums need whole rows in a tile"
    wide = [e for e in (*extras, *out_dtypes) if isinstance(e, tuple)]
    assert not wide or tn == N
    extra_specs = [pl.BlockSpec((tm, tn), functools.partial(lambda c, i, j, k: (i, c), e[1])) if isinstance(e, tuple) else tile
                   for e in extras]
    extras = [e[0] if isinstance(e, tuple) else e for e in extras]
    out_widths = [d[1] if isinstance(d, tuple) else N for d in out_dtypes]
    out_dtypes = [d[0] if isinstance(d, tuple) else d for d in out_dtypes]

    n_after = len(after)

    def body(a_ref, b_ref, *rest):
        extra_refs, rest = rest[:n_extra + n_const], rest[n_extra + n_const + n_after:]
        out_refs, sum_refs, acc = rest[:n_out], rest[n_out:n_out + n_sum], rest[-1]
        i, k = pl.program_id(0), pl.program_id(2)
        def product():
            return lax.dot_general(a_ref[...].astype(BF), b_ref[...].astype(BF), dims, preferred_element_type=F32)

        if nk > 1:
            @pl.when(k == 0)
            def _():
                acc[...] = jnp.zeros_like(acc)

            acc[...] += product()

        @pl.when(k == nk - 1)
        def _():
            total = product() if nk == 1 else acc[...]
            outs = (total,) if epilogue is None else epilogue(total, *[e[...] for e in extra_refs])
            for o_ref, val in zip(out_refs, outs[:n_out], strict=True):
                o_ref[...] = val.astype(o_ref.dtype).reshape(o_ref.shape)
            for s_ref, val in zip(sum_refs, outs[n_out:], strict=True):
                @pl.when(i == 0)
                def _():
                    s_ref[...] = val

                @pl.when(i != 0)
                def _():
                    s_ref[...] += val

    out_specs = ([tile if w == N else pl.BlockSpec((tm, w), lambda i, j, k: (i, 0)) for w in out_widths]
                 + [pl.BlockSpec((1, c), lambda i, j, k: (0, 0)) for c in sums])
    out_shape = ([jax.ShapeDtypeStruct((M, w), dt) for dt, w in zip(out_dtypes, out_widths, strict=True)]
                 + [jax.ShapeDtypeStruct((1, c), F32) for c in sums])
    operands, aliases = [a, b, *extras, *consts, *after], {}
    in_specs = ([a_spec, b_spec] + extra_specs
                + [pl.BlockSpec(c.shape, functools.partial(lambda nd, i, j, k: (0,) * nd, c.ndim)) for c in consts] + [ANY] * n_after)
    if into is not None:
        assert n_out == 1
        target, block, index = into
        out_specs = [pl.BlockSpec(block, lambda i, j, k: index(i, j))]
        out_shape = [jax.ShapeDtypeStruct(target.shape, target.dtype)]
        if not isinstance(target, jax.ShapeDtypeStruct):
            aliases = {len(operands): 0}
            operands.append(target)
            in_specs.append(ANY)
            n_after += 1

    outs = pl.pallas_call(
        body, name=name, grid=(M // tm, N // tn, nk),
        in_specs=in_specs, out_specs=out_specs, out_shape=out_shape,
        scratch_shapes=[pltpu.VMEM((tm, tn) if nk > 1 else (8, 128), F32)], input_output_aliases=aliases,
        compiler_params=_cparams(("arbitrary",) * 3 if sums else ("parallel", "parallel", "arbitrary")),
    )(*operands)
    return outs[0] if len(outs) == 1 else outs


def _row(arr, tm, col_block=None, width=None):
    width = arr.shape[1] if width is None else width
    cb = 0 if col_block is None else col_block
    return arr, pl.BlockSpec((tm, width), lambda i: (i, cb))


def _full(arr):
    nd = arr.ndim
    return arr, pl.BlockSpec(arr.shape, lambda i: (0,) * nd)


def _rowwise(name, body, T, tm, ins, outs, sums=(), after=()):
    n_in, n_out, n_sum, n_after = len(ins), len(outs), len(sums), len(after)

    def kern(*refs):
        in_refs, refs = refs[:n_in], refs[n_in + n_after:]
        out_refs, sum_refs = refs[:n_out], refs[n_out:]
        res = body(*[r[...] for r in in_refs])
        res = res if isinstance(res, tuple) else (res,)
        for o_ref, val in zip(out_refs, res[:n_out], strict=True):
            o_ref[...] = val.astype(o_ref.dtype)
        if n_sum:
            @pl.when(pl.program_id(0) == 0)
            def _():
                for s_ref in sum_refs:
                    s_ref[...] = jnp.zeros_like(s_ref)

            for s_ref, val in zip(sum_refs, res[n_out:], strict=True):
                s_ref[...] += val

    res = pl.pallas_call(
        kern, name=name, grid=(T // tm,),
        in_specs=[spec for _, spec in ins] + [ANY] * n_after,
        out_specs=[pl.BlockSpec((tm, c), lambda i: (i, 0)) for c, _ in outs]
        + [pl.BlockSpec((1, c), lambda i: (0, 0)) for c in sums],
        out_shape=[jax.ShapeDtypeStruct((T, c), dt) for c, dt in outs]
        + [jax.ShapeDtypeStruct((1, c), F32) for c in sums],
        compiler_params=_cparams(("arbitrary",)),
    )(*[a for a, _ in ins], *after)
    return res[0] if len(res) == 1 else res


def _sigmoid(x):
    return 1.0 / (1.0 + jnp.exp(-x))


def _rms(h):
    return lax.rsqrt(jnp.mean(h * h, axis=-1, keepdims=True) + RMS_EPS)


def _rms_bwd(dy, h, g):
    r = _rms(h)
    n = h * r
    dn = dy * g
    dh = r * (dn - n * jnp.mean(dn * n, axis=-1, keepdims=True))
    return dh, jnp.sum(dy * n, axis=0, keepdims=True)


def _rope(x, cos2, sin_signed):
    lane = lax.broadcasted_iota(jnp.int32, x.shape, 1)
    swapped = jnp.where((lane % HEAD_DIM) < HEAD_DIM // 2, pltpu.roll(x, 128 - HEAD_DIM // 2, 1), pltpu.roll(x, HEAD_DIM // 2, 1))
    return x * cos2 + swapped * sin_signed


NA_KEYS = NA_WIN_ROWS * GRID_W
NA_BASES = 8


def _na_row_geometry(r, rows):
    first = jnp.clip(r - NA_WIN_ROWS // 2, 0, rows - NA_WIN_ROWS)
    base = first - r + (NA_WIN_ROWS - 1)
    return pl.multiple_of(first * GRID_W, GRID_W), base


NA_ROWS_PER_STEP = 16
NA_BWD_ROWS_PER_STEP = 8


def _softmax_rows(s):
    p = jnp.exp(s - jnp.max(s, axis=-1, keepdims=True))
    return p / jnp.sum(p, axis=-1, keepdims=True)


def _split_pair(t):
    first = lax.broadcasted_iota(jnp.int32, t.shape, 1) < HEAD_DIM
    zero = jnp.zeros_like(t)
    return jnp.where(first, t, zero), jnp.where(first, zero, t)


def _join_pair(a, b):
    return jnp.where(lax.broadcasted_iota(jnp.int32, a.shape, 1) < HEAD_DIM, a, b)


_NT = (((1,), (1,)), ((), ()))
_TN = (((0,), (0,)), ((), ()))


def _na_fwd(qkv, tab):
    T = qkv.shape[0]
    rows = T // GRID_W
    n_pairs = NA_WIDTH // 128

    def body(q_ref, k_ref, v_ref, tab_ref, y_ref):
        def step(it, carry):
            geo = [_na_row_geometry(it * NA_ROWS_PER_STEP + u, rows) for u in range(NA_ROWS_PER_STEP)]
            q0s = [pl.multiple_of((it * NA_ROWS_PER_STEP + u) * GRID_W, GRID_W) for u in range(NA_ROWS_PER_STEP)]
            ss = [lax.dot_general(jnp.concatenate(_split_pair(q_ref[pl.ds(q0, GRID_W), :] * Q_SCALE), axis=0),
                                  k_ref[pl.ds(k0, NA_KEYS), :], _NT, preferred_element_type=F32)
                  for q0, (k0, _) in zip(q0s, geo)]
            ps = [_softmax_rows(s + jnp.concatenate([tab_ref[0, base], tab_ref[1, base]], axis=0)) for s, (_, base) in zip(ss, geo)]
            ys = [jnp.dot(p.astype(BF), v_ref[pl.ds(k0, NA_KEYS), :], preferred_element_type=F32) for p, (k0, _) in zip(ps, geo)]
            for q0, y2 in zip(q0s, ys):
                y_ref[pl.ds(q0, GRID_W), :] = _join_pair(y2[:GRID_W], y2[GRID_W:]).astype(y_ref.dtype)
            return carry

        lax.fori_loop(0, rows // NA_ROWS_PER_STEP, step, 0)

    def cols(first):
        return pl.BlockSpec((T, 128), lambda j: (0, first + j))

    return pl.pallas_call(
        body, name="na_fwd", grid=(n_pairs,),
        in_specs=[cols(0), cols(n_pairs), cols(2 * n_pairs), pl.BlockSpec((2, NA_BASES, GRID_W, NA_KEYS), lambda j: (j, 0, 0, 0))],
        out_specs=cols(0), out_shape=jax.ShapeDtypeStruct((T, NA_WIDTH), BF),
        compiler_params=_cparams(("parallel",)),
    )(qkv, qkv, qkv, tab)


def _na_bwd(qkv, tab, do):
    T = qkv.shape[0]
    rows = T // GRID_W
    n_pairs = NA_WIDTH // 128

    def body(q_ref, k_ref, v_ref, tab_ref, do_ref, dq_ref, dk_out, dv_out, dtab_ref, dk_ref, dv_ref):
        dk_ref[...] = jnp.zeros_like(dk_ref)
        dv_ref[...] = jnp.zeros_like(dv_ref)
        dtab_ref[...] = jnp.zeros_like(dtab_ref)

        def step(it, carry):
            U = NA_BWD_ROWS_PER_STEP
            geo = [_na_row_geometry(it * U + u, rows) for u in range(U)]
            q0s = [pl.multiple_of((it * U + u) * GRID_W, GRID_W) for u in range(U)]
            q2s = [jnp.concatenate(_split_pair(q_ref[pl.ds(q0, GRID_W), :] * Q_SCALE), axis=0) for q0 in q0s]
            do2s = [jnp.concatenate(_split_pair(do_ref[pl.ds(q0, GRID_W), :]), axis=0) for q0 in q0s]
            ss = [lax.dot_general(q2, k_ref[pl.ds(k0, NA_KEYS), :], _NT, preferred_element_type=F32) for q2, (k0, _) in zip(q2s, geo)]
            dps = [lax.dot_general(do2, v_ref[pl.ds(k0, NA_KEYS), :], _NT, preferred_element_type=F32) for do2, (k0, _) in zip(do2s, geo)]
            ps = [_softmax_rows(s + jnp.concatenate([tab_ref[0, base], tab_ref[1, base]], axis=0)) for s, (_, base) in zip(ss, geo)]
            dss = [p * (dp - jnp.sum(dp * p, axis=-1, keepdims=True)) for p, dp in zip(ps, dps)]
            dvs = [lax.dot_general(p.astype(BF), do2, _TN, preferred_element_type=F32) for p, do2 in zip(ps, do2s)]
            dsbs = [ds.astype(BF) for ds in dss]
            dqs = [jnp.dot(dsb, k_ref[pl.ds(k0, NA_KEYS), :], preferred_element_type=F32) for dsb, (k0, _) in zip(dsbs, geo)]
            dks = [lax.dot_general(dsb, q2, _TN, preferred_element_type=F32) for dsb, q2 in zip(dsbs, q2s)]
            for u in range(U):
                k0, base = geo[u]
                dtab_ref[0, base] += dss[u][:GRID_W]
                dtab_ref[1, base] += dss[u][GRID_W:]
                dq_ref[pl.ds(q0s[u], GRID_W), :] = (_join_pair(dqs[u][:GRID_W], dqs[u][GRID_W:]) * Q_SCALE).astype(dq_ref.dtype)
                dk_ref[pl.ds(k0, NA_KEYS), :] += dks[u]
                dv_ref[pl.ds(k0, NA_KEYS), :] += dvs[u]
            return carry

        lax.fori_loop(0, rows // NA_BWD_ROWS_PER_STEP, step, 0)
        dk_out[...] = dk_ref[...].astype(dk_out.dtype)
        dv_out[...] = dv_ref[...].astype(dv_out.dtype)

    def cols(first):
        return pl.BlockSpec((T, 128), lambda j: (0, first + j))

    tabs = pl.BlockSpec((2, NA_BASES, GRID_W, NA_KEYS), lambda j: (j, 0, 0, 0))
    wide = jax.ShapeDtypeStruct((T, NA_WIDTH), BF)
    return pl.pallas_call(
        body, name="na_bwd", grid=(n_pairs,),
        in_specs=[cols(0), cols(n_pairs), cols(2 * n_pairs), tabs, cols(0)],
        out_specs=[cols(0), cols(0), cols(0), tabs],
        out_shape=[wide, wide, wide, jax.ShapeDtypeStruct((NA_HEADS, NA_BASES, GRID_W, NA_KEYS), F32)],
        scratch_shapes=[pltpu.VMEM((T, 128), F32), pltpu.VMEM((T, 128), F32)],
        compiler_params=_cparams(("parallel",)),
    )(qkv, qkv, qkv, tab, do)


def _na_bias_table(rpb):
    H, n_rows, n_cols = rpb.shape

    def body(r_ref, tab_ref):
        q = lax.broadcasted_iota(jnp.int32, (GRID_W, 128), 0)
        kc = lax.broadcasted_iota(jnp.int32, (GRID_W, 128), 1)
        first = jnp.clip(q - NA_WIN_COLS // 2, 0, GRID_W - NA_WIN_COLS)
        valid = (kc >= first) & (kc < first + NA_WIN_COLS)
        toeplitz = []
        for ro in range(n_rows):
            row = jnp.broadcast_to(r_ref[pl.ds(ro, 1), :], (GRID_W, 128))
            shifted = pltpu.roll(pltpu.roll(row, 128 - (NA_WIN_COLS - 1), 1), 0, 1, stride=1, stride_axis=0)
            toeplitz.append(jnp.where(valid, shifted, NEG_INF))
        for base in range(NA_BASES):
            for j in range(NA_WIN_ROWS // 2):
                even, odd = toeplitz[base + 2 * j], toeplitz[base + 2 * j + 1]
                tab_ref[base, :, pl.ds(j * 128, 128)] = jnp.where(kc < GRID_W, even, pltpu.roll(odd, GRID_W, 1))

    padded = jnp.pad(rpb, ((0, 0), (0, 16 - n_rows), (0, 128 - n_cols)))
    return pl.pallas_call(
        body, name="na_bias_table", grid=(H,),
        in_specs=[pl.BlockSpec((None, 16, 128), lambda h: (h, 0, 0))],
        out_specs=pl.BlockSpec((None, NA_BASES, GRID_W, NA_KEYS), lambda h: (h, 0, 0, 0)),
        out_shape=jax.ShapeDtypeStruct((H, NA_BASES, GRID_W, NA_KEYS), F32),
        compiler_params=_cparams(("parallel",)),
    )(padded)


def _na_rpb_grad(dtab, after=()):
    H = dtab.shape[0]
    n_rows = 2 * NA_WIN_ROWS - 1
    n_cols = 2 * NA_WIN_COLS - 1

    def body(d_ref, *rest):
        o_ref = rest[-1]
        lane = lax.broadcasted_iota(jnp.int32, (GRID_W, 128), 1)
        low = lane < GRID_W
        flip = (lax.broadcasted_iota(jnp.int32, (GRID_W, GRID_W), 0) + lax.broadcasted_iota(jnp.int32, (GRID_W, GRID_W), 1)
                == GRID_W - 1).astype(BF)

        def reverse_rows(t):
            out = jnp.zeros_like(t)
            for _ in range(3):
                piece = t.astype(BF)
                out = out + jnp.dot(flip, piece, preferred_element_type=F32)
                t = t - piece.astype(F32)
            return out

        out_rows = []
        for ro in range(n_rows):
            acc = jnp.zeros((GRID_W, 128), F32)
            for base in range(NA_BASES):
                i = ro - base
                if not 0 <= i < NA_WIN_ROWS:
                    continue
                pair = d_ref[base, :, pl.ds((i // 2) * 128, 128)]
                if i % 2:
                    pair = pltpu.roll(pair, GRID_W, 1)
                acc = acc + jnp.where(low, pair, 0.0)
            skew = pltpu.roll(reverse_rows(acc), 0, 1, stride=1, stride_axis=0)
            diag = jnp.sum(skew, axis=0, keepdims=True)
            out_rows.append(pltpu.roll(jnp.broadcast_to(diag, (8, 128)), 128 - (GRID_W - NA_WIN_COLS), 1)[:1])
        out_rows.append(jnp.zeros((1, 128), F32))
        res = jnp.concatenate(out_rows, axis=0)
        o_ref[...] = jnp.where(lax.broadcasted_iota(jnp.int32, res.shape, 1) < n_cols, res, 0.0)

    return pl.pallas_call(
        body, name="na_rpb_grad", grid=(H,),
        in_specs=[pl.BlockSpec((None, NA_BASES, GRID_W, NA_KEYS), lambda h: (h, 0, 0, 0))] + [ANY] * len(after),
        out_specs=pl.BlockSpec((None, n_rows + 1, 128), lambda h: (h, 0, 0)),
        out_shape=jax.ShapeDtypeStruct((H, n_rows + 1, 128), F32),
        compiler_params=_cparams(("parallel",)),
    )(dtab, *after)


BAND_Q = 128
BAND_KEYS = BAND_Q + 2 * DIL_RADIUS


def _band_geometry(n, L):
    q0 = pl.multiple_of(n * BAND_Q, BAND_Q)
    k0 = pl.multiple_of(jnp.clip(q0 - DIL_RADIUS, 0, L - BAND_KEYS), DIL_RADIUS)
    qi = q0 + lax.broadcasted_iota(jnp.int32, (BAND_Q, BAND_KEYS), 0)
    kj = k0 + lax.broadcasted_iota(jnp.int32, (BAND_Q, BAND_KEYS), 1)
    return q0, k0, jnp.abs(qi - kj) <= DIL_RADIUS


DIL_PAIRS = DIL_OUT_WIDTH // 128


def _residue_shape(dil, T, dtype):
    return jax.ShapeDtypeStruct((DIL_PAIRS, dil, T // dil, 128), dtype)


def _residue_tile(dil, tm):
    return pl.BlockSpec((DIL_PAIRS, dil, tm // dil, 128), lambda i: (0, 0, i, 0))


def _to_natural(ref, scratch, dil, tm):
    tiles = []
    for pair in range(DIL_PAIRS):
        if dil == 1:
            tiles.append(ref[pair, 0].astype(F32))
            continue
        for r in range(dil):
            scratch[pl.ds(r, tm // dil, stride=dil), :] = ref[pair, r].astype(F32)
        tiles.append(scratch[...])
    return tiles


def _from_natural(tile, scratch, ref, pair, dil, tm):
    if dil == 1:
        ref[pair, 0] = tile.astype(ref.dtype)
        return
    scratch[...] = tile
    for r in range(dil):
        ref[pair, r] = scratch[pl.ds(r, tm // dil, stride=dil), :].astype(ref.dtype)


def _band_specs(group, T):
    dil = DIL_GROUPS[group][1]
    L = T // dil
    assert L % BAND_Q == 0 and L >= BAND_KEYS, (T, dil)
    per_residue = min(BAND_BLOCKS_PER_STEP, L // BAND_Q)
    residues = min(dil, BAND_BLOCKS_PER_STEP // per_residue)
    spec = pl.BlockSpec((None, residues, L, 128), lambda s: (s % DIL_PAIRS, s // DIL_PAIRS, 0, 0))
    return L, residues, per_residue, (dil // residues * DIL_PAIRS,), spec


BAND_BLOCKS_PER_STEP = 8


def _band_softmax(s, valid):
    s = jnp.where(valid, s, NEG_INF)
    m = jnp.max(s, axis=-1, keepdims=True)
    p = jnp.exp(s - m)
    l = jnp.sum(p, axis=-1, keepdims=True)
    return p / l, m + jnp.log(l)


def _band_fwd(q, k, v, group):
    T = q.shape[1] * q.shape[2]
    L, residues, U, grid, spec = _band_specs(group, T)

    def body(q_ref, k_ref, v_ref, o_ref, lse_ref):
        def step(it, carry):
            geo = [(r, *_band_geometry(it * U + u, L)) for r in range(residues) for u in range(U)]
            ss = [lax.dot_general(jnp.concatenate(_split_pair(q_ref[r, pl.ds(q0, BAND_Q), :]), axis=0),
                                  k_ref[r, pl.ds(k0, BAND_KEYS), :], _NT, preferred_element_type=F32) for r, q0, k0, _ in geo]
            pls = [_band_softmax(s, jnp.concatenate([valid, valid], axis=0)) for s, (_, _, _, valid) in zip(ss, geo)]
            os = [jnp.dot(p.astype(BF), v_ref[r, pl.ds(k0, BAND_KEYS), :], preferred_element_type=F32)
                  for (p, _), (r, _, k0, _) in zip(pls, geo)]
            for (r, q0, _, _), o2, (_, lse) in zip(geo, os, pls):
                o_ref[r, pl.ds(q0, BAND_Q), :] = _join_pair(o2[:BAND_Q], o2[BAND_Q:])
                lse2 = jnp.broadcast_to(lse, (2 * BAND_Q, 128))
                lse_ref[r, pl.ds(q0, BAND_Q), :] = _join_pair(lse2[:BAND_Q], lse2[BAND_Q:])
            return carry

        lax.fori_loop(0, L // (BAND_Q * U), step, 0)

    res = _residue_shape(DIL_GROUPS[group][1], T, F32)
    return pl.pallas_call(
        body, name=f"band_fwd_g{group}", grid=grid,
        in_specs=[spec] * 3, out_specs=[spec] * 2, out_shape=[res, res],
        compiler_params=_cparams(("parallel",)),
    )(q, k, v)


def _band_bwd(q, k, v, do, dlse, group):
    T = q.shape[1] * q.shape[2]
    L, residues, U, grid, spec = _band_specs(group, T)

    def body(q_ref, k_ref, v_ref, do_ref, dlse_ref, dq_ref, dk_ref, dv_ref):
        dk_ref[...] = jnp.zeros_like(dk_ref)
        dv_ref[...] = jnp.zeros_like(dv_ref)

        def step(it, carry):
            geo = [(r, *_band_geometry(it * U + u, L)) for r in range(residues) for u in range(U)]
            q2s = [jnp.concatenate(_split_pair(q_ref[r, pl.ds(q0, BAND_Q), :]), axis=0) for r, q0, _, _ in geo]
            do2s = [jnp.concatenate(_split_pair(do_ref[r, pl.ds(q0, BAND_Q), :]), axis=0) for r, q0, _, _ in geo]
            ss = [lax.dot_general(q2, k_ref[r, pl.ds(k0, BAND_KEYS), :], _NT, preferred_element_type=F32)
                  for q2, (r, _, k0, _) in zip(q2s, geo)]
            dps = [lax.dot_general(do2, v_ref[r, pl.ds(k0, BAND_KEYS), :], _NT, preferred_element_type=F32)
                   for do2, (r, _, k0, _) in zip(do2s, geo)]
            ps = [_band_softmax(s, jnp.concatenate([valid, valid], axis=0))[0] for s, (_, _, _, valid) in zip(ss, geo)]
            dss = []
            for p, dp, (r, q0, _, _) in zip(ps, dps, geo):
                dl = dlse_ref[r, pl.ds(q0, BAND_Q), :]
                dl2 = jnp.concatenate([dl[:, :1], dl[:, HEAD_DIM:HEAD_DIM + 1]], axis=0)
                dss.append(p * (dp - jnp.sum(dp * p, axis=-1, keepdims=True) + dl2))
            dvs = [lax.dot_general(p.astype(BF), do2, _TN, preferred_element_type=F32) for p, do2 in zip(ps, do2s)]
            dsbs = [ds.astype(BF) for ds in dss]
            dqs = [jnp.dot(dsb, k_ref[r, pl.ds(k0, BAND_KEYS), :], preferred_element_type=F32) for dsb, (r, _, k0, _) in zip(dsbs, geo)]
            dks = [lax.dot_general(dsb, q2, _TN, preferred_element_type=F32) for dsb, q2 in zip(dsbs, q2s)]
            for u, (r, q0, k0, _) in enumerate(geo):
                dq_ref[r, pl.ds(q0, BAND_Q), :] = _join_pair(dqs[u][:BAND_Q], dqs[u][BAND_Q:])
                dk_ref[r, pl.ds(k0, BAND_KEYS), :] += dks[u]
                dv_ref[r, pl.ds(k0, BAND_KEYS), :] += dvs[u]
            return carry

        lax.fori_loop(0, L // (BAND_Q * U), step, 0)

    res = _residue_shape(DIL_GROUPS[group][1], T, F32)
    return pl.pallas_call(
        body, name=f"band_bwd_g{group}", grid=grid,
        in_specs=[spec] * 5, out_specs=[spec] * 3, out_shape=[res] * 3,
        compiler_params=_cparams(("parallel",)),
    )(q, k, v, do, dlse)


def _head_sums(t):
    head = lax.broadcasted_iota(jnp.int32, t.shape, 1) // HEAD_DIM
    out = jnp.zeros_like(t)
    for h in range(t.shape[1] // HEAD_DIM):
        mine = head == h
        out = jnp.where(mine, jnp.sum(jnp.where(mine, t, 0.0), axis=-1, keepdims=True), out)
    return out


def _dil_merge_fwd(os, lses, T, tm):
    G = len(DIL_GROUPS)
    W = DIL_OUT_WIDTH
    dils = [d for _, d in DIL_GROUPS]

    def body(*refs):
        o_refs, lse_refs = refs[:G], refs[G:2 * G]
        y_ref, w_refs, on_refs, scratch = refs[2 * G], refs[2 * G + 1:3 * G + 1], refs[3 * G + 1:4 * G + 1], refs[-1]
        o = [jnp.concatenate(_to_natural(r, scratch, d, tm), axis=1) for r, d in zip(o_refs, dils)]
        ls = [jnp.concatenate(_to_natural(r, scratch, d, tm), axis=1) for r, d in zip(lse_refs, dils)]
        m = functools.reduce(jnp.maximum, ls)
        es = [jnp.exp(l - m) for l in ls]
        tot = functools.reduce(jnp.add, es)
        ws = [e / tot for e in es]
        y_ref[...] = functools.reduce(jnp.add, [w * t for w, t in zip(ws, o)]).astype(y_ref.dtype)
        for g in range(G):
            w_refs[g][...] = ws[g]
            on_refs[g][...] = o[g]

    nat = pl.BlockSpec((tm, W), lambda i: (i, 0))
    res = pl.pallas_call(
        body, name="dil_merge_fwd", grid=(T // tm,),
        in_specs=[_residue_tile(d, tm) for d in dils] * 2,
        out_specs=[nat] * (2 * G + 1),
        out_shape=[jax.ShapeDtypeStruct((T, W), BF)] + [jax.ShapeDtypeStruct((T, W), F32)] * (2 * G),
        scratch_shapes=[pltpu.VMEM((tm, 128), F32)],
        compiler_params=_cparams(("parallel",)),
    )(*os, *lses)
    return res[0], res[1:G + 1], res[G + 1:]


def _dil_merge_bwd(dy, os, ws, tm, after=()):
    G = len(DIL_GROUPS)
    T, W = dy.shape
    dils = [d for _, d in DIL_GROUPS]
    n_after = len(after)

    def body(*refs):
        dyt = refs[0][...]
        o, w = [r[...] for r in refs[1:G + 1]], [r[...] for r in refs[G + 1:2 * G + 1]]
        refs = refs[2 * G + 1 + n_after:]
        do_refs, dlse_refs, scratch = refs[:G], refs[G:2 * G], refs[-1]
        dws = [_head_sums(dyt * t) for t in o]
        mean = functools.reduce(jnp.add, [a * b for a, b in zip(w, dws)])
        for g, d in enumerate(dils):
            do, dlse = w[g] * dyt, w[g] * (dws[g] - mean)
            for pair in range(DIL_PAIRS):
                cols = slice(pair * 128, (pair + 1) * 128)
                _from_natural(do[:, cols], scratch, do_refs[g], pair, d, tm)
                _from_natural(dlse[:, cols], scratch, dlse_refs[g], pair, d, tm)

    nat = pl.BlockSpec((tm, W), lambda i: (i, 0))
    res = pl.pallas_call(
        body, name="dil_merge_bwd", grid=(T // tm,),
        in_specs=[nat] * (2 * G + 1) + [ANY] * n_after,
        out_specs=[_residue_tile(d, tm) for d in dils] * 2,
        out_shape=[_residue_shape(d, T, BF) for d in dils] + [_residue_shape(d, T, F32) for d in dils],
        scratch_shapes=[pltpu.VMEM((tm, 128), F32)],
        compiler_params=_cparams(("parallel",)),
    )(dy, *os, *ws, *after)
    return res[:G], res[G:]


def _qkv_prep(z, cos2, sin_signed, tm):
    T = z.shape[0]
    G = len(DIL_GROUPS)
    dils = [d for _, d in DIL_GROUPS]
    n_dil_blocks = 3 * DIL_WIDTH // 128

    def body(*refs):
        blocks = refs[:n_dil_blocks]
        cos_ref, sin_ref = refs[n_dil_blocks], refs[1 + n_dil_blocks]
        outs = refs[2 + n_dil_blocks:]
        for part in range(3):
            for g, d in enumerate(dils):
                out = outs[g * 3 + part]
                for pair in range(DIL_PAIRS):
                    blk = blocks[part * (DIL_WIDTH // 128) + g * DIL_PAIRS + pair]
                    for r in range(d):
                        rows = pl.ds(r, tm // d, stride=d) if d > 1 else slice(None)
                        x = blk[rows, :]
                        if part < 2:
                            x = _rope(x, cos_ref[rows, :], sin_ref[rows, :])
                        if part == 0:
                            x = x * Q_SCALE
                        out[pair, r] = x.astype(out.dtype)

    lane_block = [pl.BlockSpec((tm, 128), functools.partial(lambda b, i: (i, b), b)) for b in range(n_dil_blocks)]
    tab = pl.BlockSpec((tm, 128), lambda i: (i, 0))
    res = pl.pallas_call(
        body, name="qkv_prep", grid=(T // tm,),
        in_specs=lane_block + [tab, tab],
        out_specs=[_residue_tile(d, tm) for d in dils for _ in range(3)],
        out_shape=[_residue_shape(d, T, BF) for d in dils for _ in range(3)],
        compiler_params=_cparams(("parallel",)),
    )(*[z] * n_dil_blocks, cos2, sin_signed)
    return [res[3 * g:3 + 3 * g] for g in range(G)]


def _qkv_unprep(d_na, d_dil, cos2, sin_signed, tm, after=()):
    T = d_na[0].shape[0]
    G = len(DIL_GROUPS)
    dils = [d for _, d in DIL_GROUPS]
    n_after = len(after)

    def body(*refs):
        dq, dk, dv = (r[...] for r in refs[:3])
        res_refs = refs[3:3 + 3 * G]
        cs, sn = refs[3 + 3 * G][...], refs[4 + 3 * G][...]
        out, scratch = refs[5 + 3 * G + n_after], refs[-1]
        cols = [dq, dk, dv]
        for part in range(3):
            for g, d in enumerate(dils):
                for x in _to_natural(res_refs[g * 3 + part], scratch, d, tm):
                    if part < 2:
                        x = _rope(x, cs, -sn)
                    cols.append((x * Q_SCALE if part == 0 else x).astype(out.dtype))
        out[...] = jnp.concatenate(cols, axis=1)

    wide = pl.BlockSpec((tm, NA_WIDTH), lambda i: (i, 0))
    tab = pl.BlockSpec((tm, 128), lambda i: (i, 0))
    return pl.pallas_call(
        body, name="qkv_unprep", grid=(T // tm,),
        in_specs=[wide] * 3 + [_residue_tile(d, tm) for d in dils for _ in range(3)] + [tab, tab] + [ANY] * n_after,
        out_specs=pl.BlockSpec((tm, QKV_WIDTH), lambda i: (i, 0)),
        out_shape=jax.ShapeDtypeStruct((T, QKV_WIDTH), BF),
        scratch_shapes=[pltpu.VMEM((tm, 128), F32)],
        compiler_params=_cparams(("parallel",)),
    )(*d_na, *[t for g in range(G) for t in d_dil[g]], cos2, sin_signed, *after)


def _rope_tables(positions):
    half = HEAD_DIM // 2
    inv_freq = ROPE_THETA ** (-jnp.arange(half, dtype=F32) / half)
    ang = positions.astype(F32)[:, None] * inv_freq
    cos, sin = jnp.cos(ang), jnp.sin(ang)
    return jnp.tile(jnp.concatenate([cos, cos], axis=1), (1, 2)), jnp.tile(jnp.concatenate([-sin, sin], axis=1), (1, 2))


def _pack_rows(t):
    return t.reshape(-1, PACK_W)


def _me():
    return lax.axis_index("x"), lax.axis_index("y"), lax.axis_index("c")


def _other_chips(x, y):
    return [(1 - x, y), (x, 1 - y), (1 - x, 1 - y)]


def _pair_sum(g, got, tm):
    S, R, W = g.shape
    half = R // 2
    nb = half // tm

    def body(pos_ref, g_ref, got_ref, own_ref, ob_ref):
        tot = g_ref[...] + got_ref[...]
        ob_ref[...] = tot.astype(ob_ref.dtype)

        @pl.when(pl.program_id(1) == pos_ref[1])
        def _():
            own_ref[...] = tot

    tile = pl.BlockSpec((None, tm, W), lambda i, s, pos_ref: (s, i, 0))
    c, chip = lax.axis_index("c"), 2 * lax.axis_index("x") + lax.axis_index("y")
    return pl.pallas_call(
        body, name="pair_sum",
        grid_spec=pltpu.PrefetchScalarGridSpec(
            num_scalar_prefetch=1, grid=(nb, S),
            in_specs=[pl.BlockSpec((None, tm, W), lambda i, s, pos_ref: (s, pos_ref[0] * nb + i, 0)), tile],
            out_specs=[pl.BlockSpec((tm, W), lambda i, s, pos_ref: (i, 0)), tile]),
        out_shape=[jax.ShapeDtypeStruct((half, W), F32), jax.ShapeDtypeStruct((S, half, W), BF)],
        compiler_params=_cparams(("parallel", "arbitrary")),
    )(jnp.stack([c, chip]).astype(jnp.int32), g, got)


def _chip_sum(own, others, tm):
    n, h, W = others.shape
    nb = h // tm

    def body(c_ref, own_ref, p_ref, o_ref):
        o_ref[...] = ((own_ref[...] + p_ref[0].astype(F32)) + p_ref[1].astype(F32)) + p_ref[2].astype(F32)

    return pl.pallas_call(
        body, name="chip_sum",
        grid_spec=pltpu.PrefetchScalarGridSpec(
            num_scalar_prefetch=1, grid=(nb,),
            in_specs=[pl.BlockSpec((tm, W), lambda i, c_ref: (i, 0)), pl.BlockSpec((n, tm, W), lambda i, c_ref: (0, i, 0))],
            out_specs=pl.BlockSpec((tm, W), lambda i, c_ref: (c_ref[0] * nb + i, 0))),
        out_shape=jax.ShapeDtypeStruct((2 * h, W), F32),
        compiler_params=_cparams(("parallel",)),
    )(lax.axis_index("c").reshape(1).astype(jnp.int32), own, others)


def _join_halves(shard):
    h = shard.shape[0] // 2

    def body(in_ref, out_ref, send_sem, recv_sem):
        x, y, c = _me()
        cp = pltpu.make_async_remote_copy(
            src_ref=in_ref.at[pl.ds(c * h, h), :], dst_ref=out_ref.at[pl.ds(c * h, h), :],
            send_sem=send_sem, recv_sem=recv_sem, device_id=(x, y, 1 - c), device_id_type=MESH)
        cp.start()
        pltpu.make_async_remote_copy(
            src_ref=in_ref.at[pl.ds(c * h, h), :], dst_ref=out_ref.at[pl.ds((1 - c) * h, h), :],
            send_sem=send_sem, recv_sem=recv_sem, device_id=(x, y, 1 - c), device_id_type=MESH).wait_recv()
        cp.wait_send()

    return pl.pallas_call(
        body, name="join_halves", in_specs=[ANY], out_specs=ANY,
        out_shape=jax.ShapeDtypeStruct(shard.shape, shard.dtype), input_output_aliases={0: 0},
        scratch_shapes=[pltpu.SemaphoreType.DMA, pltpu.SemaphoreType.DMA],
    )(shard)


def _allreduce_small(s, after=()):
    R, W = s.shape
    n_after = len(after)

    def body(s_ref, *rest):
        o_ref, buf, send_sems, recv_sems = rest[n_after:]
        x, y, c = _me()
        me = 4 * x + 2 * y + c
        buf[me] = s_ref[...]
        peers = [((x + fx) % 2, (y + fy) % 2, (c + fc) % 2) for fx in range(2) for fy in range(2) for fc in range(2)][1:]
        sends = [pltpu.make_async_remote_copy(
            src_ref=s_ref, dst_ref=buf.at[me], send_sem=send_sems.at[k], recv_sem=recv_sems.at[k],
            device_id=peer, device_id_type=MESH) for k, peer in enumerate(peers)]
        for cp in sends:
            cp.start()
        for k, peer in enumerate(peers):
            pltpu.make_async_remote_copy(
                src_ref=s_ref, dst_ref=buf.at[4 * peer[0] + 2 * peer[1] + peer[2]], send_sem=send_sems.at[k],
                recv_sem=recv_sems.at[k], device_id=peer, device_id_type=MESH).wait_recv()
        for cp in sends:
            cp.wait_send()
        total = buf[0]
        for d in range(1, N_DEV):
            total = total + buf[d]
        o_ref[...] = total

    return pl.pallas_call(
        body, name="allreduce_small",
        in_specs=[pl.BlockSpec(memory_space=pltpu.VMEM)] + [ANY] * n_after, out_specs=pl.BlockSpec(memory_space=pltpu.VMEM),
        out_shape=jax.ShapeDtypeStruct((R, W), F32),
        scratch_shapes=[pltpu.VMEM((N_DEV, R, W), F32), pltpu.SemaphoreType.DMA((N_DEV - 1,)), pltpu.SemaphoreType.DMA((N_DEV - 1,))],
    )(s, *after)


HBM_SPEC = pl.BlockSpec(memory_space=pltpu.HBM)
SEM_SPEC = pl.BlockSpec(memory_space=pltpu.SEMAPHORE)
DATAFLOW = pltpu.SideEffectType.DATAFLOW_SIDE_EFFECTING


class _InFlight(NamedTuple):
    sems: tuple
    src: jax.Array
    land: jax.Array
    token: jax.Array


def _split_start(name, src, land_shape, land_dtype, n, copies, after=()):
    n_after = len(after)

    def body(src_ref, land_ref, *rest):
        rest = rest[n_after:]
        sems, token = rest[:2 * n], rest[-1]
        for k, (s, d, peer) in enumerate(copies(src_ref, land_ref)):
            pltpu.make_async_remote_copy(src_ref=s, dst_ref=d, send_sem=sems[k], recv_sem=sems[n + k],
                                         device_id=peer, device_id_type=MESH).start()
        token[...] = jnp.zeros_like(token)

    outs = pl.pallas_call(
        body, name=name,
        out_shape=(*[pltpu.SemaphoreType.DMA(())] * (2 * n), pltpu.HBM(src.shape, src.dtype), pltpu.HBM(land_shape, land_dtype),
                   jax.ShapeDtypeStruct((8, 128), F32)),
        in_specs=(HBM_SPEC, HBM_SPEC, *[ANY] * n_after),
        out_specs=(*[SEM_SPEC] * (2 * n), HBM_SPEC, HBM_SPEC, pl.BlockSpec(memory_space=pltpu.VMEM)),
        input_output_aliases={0: 2 * n, 1: 2 * n + 1},
        compiler_params=pltpu.CompilerParams(has_side_effects=DATAFLOW),
    )(pltpu.with_memory_space_constraint(src, pltpu.HBM), pltpu.with_memory_space_constraint(lax.empty(land_shape, land_dtype), pltpu.HBM),
      *after)
    return _InFlight(tuple(outs[:2 * n]), outs[2 * n], outs[2 * n + 1], outs[2 * n + 2])


def _split_wait(name, flight, after, n, copies):
    after = after if isinstance(after, tuple) else (after,)

    def body(src_ref, land_ref, *rest):
        sems = rest[:2 * n]
        for k, (s, d, peer) in enumerate(copies(src_ref, land_ref)):
            cp = pltpu.make_async_remote_copy(src_ref=s, dst_ref=d, send_sem=sems[k], recv_sem=sems[n + k],
                                              device_id=peer, device_id_type=MESH)
            cp.wait_send()
            cp.wait_recv()

    return pl.pallas_call(
        body, name=name,
        out_shape=(pltpu.HBM(flight.src.shape, flight.src.dtype), pltpu.HBM(flight.land.shape, flight.land.dtype)),
        in_specs=(HBM_SPEC, HBM_SPEC, *[SEM_SPEC] * (2 * n), *[ANY] * len(after)),
        out_specs=(HBM_SPEC, HBM_SPEC), input_output_aliases={0: 0, 1: 1},
        compiler_params=pltpu.CompilerParams(has_side_effects=DATAFLOW),
    )(flight.src, flight.land, *flight.sems, *after)


def _gather_copies(src_ref, land_ref):
    x, y, c = _me()
    return [(src_ref, land_ref.at[2 * x + y], (*chip, c)) for chip in _other_chips(x, y)]


def _gather_start(packed, tag, after=()):
    return _split_start(f"gather_start_{tag}", packed, (N_CHIPS, *packed.shape), packed.dtype, 3, _gather_copies, after)


def _gather_wait(flight, after, tag):
    src, others = _split_wait(f"gather_wait_{tag}", flight, after, 3, _gather_copies)
    return lax.dynamic_update_slice(others, src[None], (2 * lax.axis_index("x") + lax.axis_index("y"), 0, 0))


def _across_copies(src_ref, land_ref):
    x, y, c = _me()
    half = src_ref.shape[0] // 2
    rows = pl.ds(c * half, half)
    return [(src_ref.at[rows, :], land_ref.at[2 * x + y, rows, :], (*chip, c)) for chip in _other_chips(x, y)]


def _to_sibling_copies(all_ref, unused_ref):
    x, y, c = _me()
    half = all_ref.shape[1] // 2
    places = [all_ref.at[2 * chip[0] + chip[1], pl.ds(c * half, half), :] for chip in _other_chips(x, y)]
    return [(place, place, (x, y, 1 - c)) for place in places]


def _gather_halves_start(shard, tag, after=()):
    return _split_start(f"gather_{tag}_across_start", shard, (N_CHIPS, *shard.shape), shard.dtype, 3, _across_copies, after)


def _gather_halves_relay(flight, after, tag):
    shard, landed = _split_wait(f"gather_{tag}_across_wait", flight, after, 3, _across_copies)
    return shard, _split_start(f"gather_{tag}_sibling_start", landed, (8, 128), landed.dtype, 3, _to_sibling_copies)


def _gather_halves_finish(shard, relay, after, tag):
    others = _split_wait(f"gather_{tag}_sibling_wait", relay, after, 3, _to_sibling_copies)[0]
    return lax.dynamic_update_slice(others, shard[None], (2 * lax.axis_index("x") + lax.axis_index("y"), 0, 0))


def _assemble_w_in(shards, tm):
    S, R, C = shards.shape
    n_gates = 2 * D_MODEL

    def body(s_ref, w_ref, g_ref):
        full = jnp.concatenate([s_ref[s] for s in range(S)], axis=1)
        w_ref[...] = full
        g_ref[...] = full[:, S * C - n_gates:]

    return pl.pallas_call(
        body, name="assemble_w_in", grid=(R // tm,),
        in_specs=[pl.BlockSpec((S, tm, C), lambda i: (0, i, 0))],
        out_specs=[pl.BlockSpec((tm, S * C), lambda i: (i, 0)), pl.BlockSpec((tm, n_gates), lambda i: (i, 0))],
        out_shape=[jax.ShapeDtypeStruct((R, S * C), shards.dtype), jax.ShapeDtypeStruct((R, n_gates), shards.dtype)],
        compiler_params=_cparams(("parallel",)),
    )(shards)


def _swap_copies(src_ref, land_ref):
    x, y, c = _me()
    half = land_ref.shape[1]
    return [(src_ref.at[:, pl.ds((1 - c) * half, half), :], land_ref, (x, y, 1 - c))]


def _swap_start(g, tag):
    S, R, W = g.shape
    return _split_start(f"swap_halves_start_{tag}", g, (S, R // 2, W), g.dtype, 1, _swap_copies)


def _swap_wait(flight, after, tag):
    return _split_wait(f"swap_halves_wait_{tag}", flight, after, 1, _swap_copies)


def _scatter_copies(src_ref, land_ref):
    x, y, c = _me()
    return [(src_ref.at[2 * chip[0] + chip[1]], land_ref.at[j], (*chip, c)) for j, chip in enumerate(_other_chips(x, y))]


def _scatter_start(part, tag):
    S, h, W = part.shape
    return _split_start(f"scatter_chips_start_{tag}", part, (S - 1, h, W), part.dtype, 3, _scatter_copies)


def _scatter_wait(flight, after, tag):
    return _split_wait(f"scatter_chips_wait_{tag}", flight, after, 3, _scatter_copies)[1]


def _join_copies(shard_ref, unused_ref):
    x, y, c = _me()
    h = shard_ref.shape[0] // 2
    rows = shard_ref.at[pl.ds(c * h, h), :]
    return [(rows, rows, (x, y, 1 - c))]


def _join_start(shard):
    return _split_start("join_halves_start", shard, (8, 128), shard.dtype, 1, _join_copies)


def _join_wait(flight, after):
    return _split_wait("join_halves_wait", flight, after, 1, _join_copies)[0]


def _adamw(name, g, g_row0, w, m, v):
    _, R, C = w.shape
    tm = next(cand for cand in (368, 256, 128, 64, 32, 16, 8) if R % cand == 0)
    assert g_row0 % tm == 0 and g.shape[1] == C

    def body(g_ref, w_ref, m_ref, v_ref, go_ref, d_ref, mo_ref, vo_ref):
        gt = g_ref[...]
        mt = ADAM_B1 * m_ref[...] + (1.0 - ADAM_B1) * gt
        vt = ADAM_B2 * v_ref[...] + (1.0 - ADAM_B2) * jnp.square(gt)
        m_hat = mt / (1.0 - ADAM_B1 ** ADAM_STEP)
        v_hat = vt / (1.0 - ADAM_B2 ** ADAM_STEP)
        go_ref[...] = gt
        d_ref[...] = -ADAM_LR * (m_hat / (jnp.sqrt(v_hat) + ADAM_EPS) + ADAM_WD * w_ref[...])
        mo_ref[...] = mt
        vo_ref[...] = vt

    state = pl.BlockSpec((None, tm, C), lambda i: (0, i, 0))
    return pl.pallas_call(
        body, name=name, grid=(R // tm,),
        in_specs=[pl.BlockSpec((tm, C), lambda i: (g_row0 // tm + i, 0)), state, state, state],
        out_specs=[state] * 4, out_shape=[jax.ShapeDtypeStruct((1, R, C), F32)] * 4,
        compiler_params=_cparams(("parallel",)),
    )(g, w, m, v)


def _unpack_weights(gathered, names):
    S = gathered.shape[0]
    shard_shapes = {"w_in": (D_MODEL, (QKV_WIDTH + 2 * D_MODEL) // S), "w_branch_na": (NA_WIDTH, D_MODEL // S),
                    "w_branch_dil": (DIL_OUT_WIDTH, D_MODEL // S), "w_out": (D_MODEL // S, D_MODEL),
                    "w_up": (D_MODEL, D_FF // S), "w_down": (D_FF // S, D_MODEL),
                    "w_ple_gate": (D_MODEL // S, D_MODEL), "w_ple_proj": (PLE_DIM, D_MODEL // S)}
    col_sharded = {"w_in", "w_branch_na", "w_branch_dil", "w_up", "w_ple_proj"}
    out, r0 = {}, 0
    for name in names:
        rows, cols = shard_shapes[name]
        n = rows * cols // PACK_W
        t = gathered[:, r0:r0 + n, :].reshape(S, rows, cols)
        r0 += n
        out[name] = t.transpose(1, 0, 2).reshape(rows, S * cols) if name in col_sharded else t.reshape(S * rows, cols)
    return out


def kernel(x, p, positions, g_mix, w_in, rpb, w_branch_na, w_branch_dil, w_out, g_mlp, w_up, w_down, g_ple, w_ple_gate, w_ple_proj, g_final, loss_target, m_g_mix, m_w_in, m_rpb, m_w_branch_na, m_w_branch_dil, m_w_out, m_g_mlp, m_w_up, m_w_down, m_g_ple, m_w_ple_gate, m_w_ple_proj, m_g_final, v_g_mix, v_w_in, v_rpb, v_w_branch_na, v_w_branch_dil, v_w_out, v_g_mlp, v_w_up, v_w_down, v_g_ple, v_w_ple_gate, v_w_ple_proj, v_g_final):
    shards = {"w_in": w_in[0], "w_branch_na": w_branch_na[0], "w_branch_dil": w_branch_dil[0], "w_out": w_out[0],
              "w_up": w_up[0], "w_down": w_down[0], "w_ple_gate": w_ple_gate[0], "w_ple_proj": w_ple_proj[0]}
    params = {"w_in": w_in, "w_branch_na": w_branch_na, "w_branch_dil": w_branch_dil, "w_out": w_out, "w_up": w_up,
              "w_down": w_down, "w_ple_gate": w_ple_gate, "w_ple_proj": w_ple_proj,
              "m_w_in": m_w_in, "m_w_branch_na": m_w_branch_na, "m_w_branch_dil": m_w_branch_dil, "m_w_out": m_w_out,
              "m_w_up": m_w_up, "m_w_down": m_w_down, "m_w_ple_gate": m_w_ple_gate, "m_w_ple_proj": m_w_ple_proj,
              "v_w_in": v_w_in, "v_w_branch_na": v_w_branch_na, "v_w_branch_dil": v_w_branch_dil, "v_w_out": v_w_out,
              "v_w_up": v_w_up, "v_w_down": v_w_down, "v_w_ple_gate": v_w_ple_gate, "v_w_ple_proj": v_w_ple_proj}

    xs, ps, tgt = x[0], p[0, 0], loss_target[0]
    T = xs.shape[0]
    TM = 512
    gm, gl, gp, gf = g_mix, g_mlp, g_ple, g_final.reshape(1, D_MODEL)

    across = _gather_halves_start(shards["w_in"].astype(BF), "in")
    a = _rowwise("norm_mix", lambda h, g: h * _rms(h) * g, T, TM, [_row(xs, TM), _full(gm)], [(D_MODEL, BF)],
                 after=(across.token,))
    cos2, sin_signed = _rope_tables(positions[0])
    tab = _na_bias_table(rpb[0])
    packed_mix = jnp.concatenate([_pack_rows(shards[n].astype(BF)) for n in GATHER_MIX], axis=0)
    packed_mlp = jnp.concatenate([_pack_rows(shards[n].astype(BF)) for n in GATHER_MLP], axis=0)
    w_in_shard, w_in_relay = _gather_halves_relay(across, (a, tab, cos2, sin_signed, packed_mix, packed_mlp), "in")
    w_in_all = _gather_halves_finish(w_in_shard, w_in_relay, w_in_relay.token, "in")
    w_in_full, w_gates = _assemble_w_in(w_in_all, 256)
    W = {"w_in": w_in_full}
    mix_flight = _gather_start(packed_mix, "mix", after=(w_in_all,))
    mlp_across = _gather_halves_start(packed_mlp, "mlp", after=(mix_flight.token,))

    n3 = 3 * NA_WIDTH
    qkv = _mm("in_na", a, W["w_in"], "nn", 1024, 768, 1024, [BF], after=(mlp_across.token,),
              b_view=(n3, (D_MODEL, 768), lambda j, k: (k, j)))
    z_dil = _mm("in_dil", a, W["w_in"], "nn", 1024, 768, 1024, [F32], after=(mlp_across.token,),
                b_view=(3 * DIL_WIDTH, (D_MODEL, 768), lambda j, k: (k, n3 // 768 + j)))
    z_gates = _mm("in_gates", a, w_gates, "nn", 1024,1024, 1024, [BF], after=(mlp_across.token,))

    dil_ops = _qkv_prep(z_dil, cos2, sin_signed, TM)
    y_na = _na_fwd(qkv, tab)
    band = [_band_fwd(*dil_ops[g], g) for g in range(len(DIL_GROUPS))]
    y_dil, w_grp, o_nat = _dil_merge_fwd([b[0] for b in band], [b[1] for b in band], T, TM)

    W.update(_unpack_weights(_gather_wait(mix_flight, y_dil, "mix"), GATHER_MIX))
    mlp_shard, mlp_relay = _gather_halves_relay(mlp_across, y_dil, "mlp")
    u_na = _mm("branch_na", y_na, W["w_branch_na"], "nn", 1024,1024, 512, [BF], after=(mlp_relay.token,))
    def gate_mix(acc, gn, gd, un):
        ud = acc.astype(BF)
        return ud, _sigmoid(gn.astype(F32)) * un.astype(F32) + _sigmoid(gd.astype(F32)) * ud.astype(F32)

    u_dil, mixed = _mm("branch_dil", y_dil, W["w_branch_dil"], "nn", 512, 1024, 256, [BF, BF], epilogue=gate_mix,
                       extras=((z_gates, 0), (z_gates, 1), u_na))

    def add_norm(d, h, g):
        h = h + d
        return h, h * _rms(h) * g

    h1, cn = _mm("out_proj", mixed, W["w_out"], "nn", 512, 1024, 1024, [F32, BF], epilogue=add_norm, extras=(xs,), consts=(gl,))
    mlp_all = _gather_halves_finish(mlp_shard, mlp_relay, cn, "mlp")
    W.update({n: t for n, t in _unpack_weights(mlp_all, GATHER_MLP).items() if n.startswith("w_ple")})
    chip_block = (None, D_MODEL, PACK_W)
    up, act = _mm("mlp_up", cn, mlp_all, "nn", 1024,1024, 1024, [BF, BF],
                  epilogue=lambda acc: (acc, jnp.square(jnp.maximum(acc, 0.0))), b_view=(D_FF, chip_block, lambda j, k: (j, 0, 0)))
    h2, en = _mm("mlp_down", act, mlp_all, "nn", 1024, 1024, 1024, [F32, BF], epilogue=add_norm, extras=(h1,), consts=(gp,),
                 b_view=(D_MODEL, chip_block, lambda j, k: (k, 1, 0)))
    pp = _mm("ple_proj", ps, W["w_ple_proj"], "nn", 1024,1024, 256, [F32])

    def head(gtt, h2t, ppt, tg, g):
        sg = _sigmoid(gtt)
        h3 = h2t + sg * ppt
        yo = h3 * _rms(h3) * g
        diff = yo - tg
        loss = 0.5 * jnp.sum(jnp.mean(jnp.square(diff), axis=-1, keepdims=True), axis=0, keepdims=True)
        dh3, dg = _rms_bwd(diff * (1.0 / D_MODEL), h3, g)
        return dh3, dh3 * ppt * sg * (1.0 - sg), dh3 * sg, jnp.broadcast_to(loss, (1, 128)), dg

    dh3, d_gt, d_pp, loss_part, dg_final = _mm(
        "ple_gate_loss_head", en, W["w_ple_gate"], "nn", 512, 1024, 1024, [F32, BF, BF], epilogue=head,
        extras=(h2, pp, tgt), consts=(gf,), sums=[128, D_MODEL])

    early_shapes = {n: shards[n].shape for n in REDUCE_EARLY}
    early_rows = sum(r * c for r, c in early_shapes.values()) // PACK_W
    shard_rows = D_MODEL // N_CHIPS
    early_buf = _mm("g_ple_gate", en, d_gt, "tn", 1024, 1024, 1024, [F32],
                    into=(jax.ShapeDtypeStruct((N_CHIPS, early_rows, PACK_W), F32), (N_CHIPS, shard_rows, PACK_W),
                          lambda i, j: (0, 2 * D_MODEL // shard_rows, 0)))
    g_ple_proj = _mm("g_ple_proj", ps, d_pp, "tn", 256, 1024, 1024,[F32])

    def add_norm_bwd(dn, dh_out, h, g):
        dh, dg = _rms_bwd(dn, h, g)
        dh = dh_out + dh
        return dh, dh, dg

    dh2, dh2_b, dg_ple = _mm("d_ple_gate", d_gt, W["w_ple_gate"], "nt", 512, 1024, 1024, [F32, BF],
                             epilogue=add_norm_bwd, extras=(dh3, h2), consts=(gp,), sums=[D_MODEL])
    d_up = _mm("d_mlp_down", dh2_b, mlp_all, "nt", 1024,1024, 1024, [BF], b_view=(D_FF, chip_block, lambda j, k: (j, 1, 0)),
               epilogue=lambda acc, u: (acc * (2.0 * jnp.maximum(u.astype(F32), 0.0)),), extras=(up,))
    early_buf = _mm("g_mlp_down", act, dh2_b, "tn", 1024, 1024, 1024,[F32],
                    into=(early_buf, (None, D_MODEL, PACK_W), lambda i, j: (i, 1, 0)))
    early_buf = _mm("g_mlp_up", cn, d_up, "tn", 1024, 1024, 1024,[F32],
                    into=(early_buf, (None, D_MODEL, PACK_W), lambda i, j: (j, 0, 0)))
    dh1, dh1_b, dg_mlp = _mm("d_mlp_up", d_up, mlp_all, "nt", 1024, 1024, 1024, [F32, BF], epilogue=add_norm_bwd,
                             b_view=(D_MODEL, chip_block, lambda j, k: (k, 0, 0)),
                             extras=(dh2, h1), consts=(gl,), sums=[D_MODEL])
    early_buf = _mm("g_out_proj", mixed, dh1_b, "tn", 1024, 1024, 1024, [F32],
                    into=(early_buf, (N_CHIPS, shard_rows, PACK_W), lambda i, j: (0, 2 * D_MODEL // shard_rows + 1, 0)))

    def gate_bwd(dm, gn, gd, un, ud):
        gn, gd, un, ud = (t.astype(F32) for t in (gn, gd, un, ud))
        sn, sd = _sigmoid(gn), _sigmoid(gd)
        return jnp.concatenate([dm * un * sn * (1.0 - sn), dm * ud * sd * (1.0 - sd)], axis=1), dm * sn, dm * sd

    dz_gates, d_u_na, d_u_dil = _mm("d_out_proj", dh1_b, W["w_out"], "nt", 512, 1024, 1024, [(BF, 2 * D_MODEL), BF, BF],
                                    epilogue=gate_bwd, extras=((z_gates, 0), (z_gates, 1), u_na, u_dil))
    g_branch_na = _mm("g_branch_na", y_na, d_u_na, "tn", 1024, 1024, 1024,[F32])
    g_branch_dil = _mm("g_branch_dil", y_dil, d_u_dil, "tn", 256, 1024, 1024,[F32])
    small_rows = [jnp.concatenate([_pack_rows(g[:, s * shard_rows:(s + 1) * shard_rows]) for g in (g_ple_proj, g_branch_na, g_branch_dil)],
                                  axis=0) for s in range(N_CHIPS)]
    early_buf = lax.dynamic_update_slice(early_buf, jnp.stack(small_rows), (0, 2 * D_MODEL + 2 * shard_rows, 0))
    early_tm = early_rows // 4
    swap_flight = _swap_start(early_buf, "early")
    d_y_na = _mm("d_branch_na", d_u_na, W["w_branch_na"], "nt", 1024,512, 1024, [BF], after=(swap_flight.token,))
    d_y_dil = _mm("d_branch_dil", d_u_dil, W["w_branch_dil"], "nt", 1024,256, 1024, [F32])

    dqa, dka, dva, dtab = _na_bwd(qkv, tab, d_y_na)
    early_g, early_got = _swap_wait(swap_flight, dqa, "early")
    early_pair, early_pair_b = _pair_sum(early_g, early_got, early_tm)
    scatter_flight = _scatter_start(early_pair_b, "early")

    do_res, dlse_res = _dil_merge_bwd(d_y_dil, o_nat, w_grp, TM, after=(scatter_flight.token,))
    d_dil = [_band_bwd(*dil_ops[g], do_res[g], dlse_res[g], g) for g in range(len(DIL_GROUPS))]

    dz_qkv = _qkv_unprep((dqa, dka, dva), d_dil, cos2, sin_signed, TM)
    in_cols = shards["w_in"].shape[1]
    qkv_rows, gate_tm = dz_qkv.shape[1], 256
    assert qkv_rows % gate_tm == 0
    g_in = _mm("g_in_qkv", dz_qkv, a, "tn", 1280, 1024, 1024, [F32],
               into=(jax.ShapeDtypeStruct((N_CHIPS * in_cols, D_MODEL), F32), (1280, D_MODEL), lambda i, j: (i, 0)))
    g_in = _mm("g_in_gates", dz_gates, a, "tn", gate_tm, 1024, T, [F32],
               into=(g_in, (gate_tm, D_MODEL), lambda i, j: (qkv_rows // gate_tm + i, 0)))
    early_mine = _chip_sum(early_pair, _scatter_wait(scatter_flight, (g_in,), "early"), early_tm)
    join_flight = _join_start(early_mine)

    late_tm = in_cols // 4
    late_swap = _swap_start(g_in.reshape(N_CHIPS, in_cols, D_MODEL), "late")
    d_a = _mm("d_in_qkv", dz_qkv, W["w_in"], "nt", 1024,1024, 1280, [F32], after=(late_swap.token, join_flight.token),
              b_view=(D_MODEL, (D_MODEL, 1280), lambda j, k: (j, k)))
    late_g, late_got = _swap_wait(late_swap, d_a, "late")
    late_pair, late_pair_b = _pair_sum(late_g, late_got, late_tm)
    late_scatter = _scatter_start(late_pair_b, "late")
    d_rpb = _na_rpb_grad(dtab, after=(late_scatter.token,))[:, :2 * NA_WIN_ROWS - 1, :2 * NA_WIN_COLS - 1]
    def first_bwd(dn_gates, dn_qkv, dh_out, h, g):
        dh, dg = _rms_bwd(dn_gates + dn_qkv, h, g)
        return dh_out + dh, dg

    grad_x, dg_mix = _mm("d_in_gates", dz_gates, w_gates, "nt", 512, 1024, 1024, [F32], epilogue=first_bwd,
                         extras=(d_a, dh1, xs), consts=(gm,), sums=[D_MODEL], after=(late_scatter.token,))
    early_shard = _join_wait(join_flight, grad_x)

    n_rpb = rpb.size
    rpb_rows = 4
    small = jnp.concatenate([
        dg_mix, dg_mlp, dg_ple, dg_final,
        jnp.pad(d_rpb.reshape(-1), (0, rpb_rows * D_MODEL - n_rpb)).reshape(rpb_rows, D_MODEL),
        jnp.pad(loss_part, ((0, 0), (0, D_MODEL - loss_part.shape[1]))),
        jnp.zeros((SMALL_ROWS - 5 - rpb_rows, D_MODEL), F32)], axis=0)
    out = {"grad": {}, "delta": {}, "new_m": {}, "new_v": {}}

    def update(n, g, row0):
        res = _adamw("adamw_" + n, g, row0, params[n], params["m_" + n], params["v_" + n])
        for kind, t in zip(("grad", "delta", "new_m", "new_v"), res, strict=True):
            out[kind][n] = t

    row0 = 0
    for n in REDUCE_EARLY:
        rows, cols = early_shapes[n]
        n_rows = rows * cols // PACK_W
        if cols == PACK_W:
            update(n, early_shard, row0)
        else:
            update(n, early_shard[row0:row0 + n_rows].reshape(rows, cols), 0)
        row0 += n_rows
    late_others = _scatter_wait(late_scatter, (*[out["new_v"][n] for n in REDUCE_EARLY], d_rpb), "late")
    late_mine = _chip_sum(late_pair, late_others, late_tm)
    small = _allreduce_small(small, after=(late_mine,))
    res = _adamw("adamw_w_in", _join_halves(late_mine), 0, *[jnp.swapaxes(params[n], 1, 2) for n in ("w_in", "m_w_in", "v_w_in")])
    for kind, t in zip(("grad", "delta", "new_m", "new_v"), res, strict=True):
        out[kind]["w_in"] = jnp.swapaxes(t, 1, 2)
    loss = small[4 + rpb_rows, 0]

    def small_pack(a0, a1, a2, a3, r):
        return jnp.concatenate([a0.reshape(1, -1), a1.reshape(1, -1), a2.reshape(1, -1), a3.reshape(1, -1),
                                jnp.pad(r.reshape(-1), (0, rpb_rows * D_MODEL - n_rpb)).reshape(rpb_rows, D_MODEL)], axis=0)

    small_res = _adamw("adamw_small", small, 0, small_pack(g_mix, g_mlp, g_ple, g_final, rpb)[None],
                       small_pack(m_g_mix, m_g_mlp, m_g_ple, m_g_final, m_rpb)[None],
                       small_pack(v_g_mix, v_g_mlp, v_g_ple, v_g_final, v_rpb)[None])

    def small_unpack(t):
        return {"g_mix": t[0].reshape(g_mix.shape), "g_mlp": t[1].reshape(g_mlp.shape), "g_ple": t[2].reshape(g_ple.shape),
                "g_final": t[3].reshape(g_final.shape), "rpb": t[4:].reshape(-1)[:n_rpb].reshape(rpb.shape)}

    for kind, t in zip(("grad", "delta", "new_m", "new_v"), small_res, strict=True):
        out[kind].update(small_unpack(t[0]))

    order = ["g_mix", "w_in", "rpb", "w_branch_na", "w_branch_dil", "w_out", "g_mlp", "w_up", "w_down", "g_ple",
             "w_ple_gate", "w_ple_proj", "g_final"]
    return (loss, grad_x[None], *[out["grad"][n] for n in order], *[out["delta"][n] for n in order],
            *[out["new_m"][n] for n in order], *[out["new_v"][n] for n in order])
```

```python
import functools
from typing import NamedTuple

import jax
import jax.numpy as jnp
from jax import lax
from jax.experimental import pallas as pl
from jax.experimental.pallas import tpu as pltpu

BF = jnp.bfloat16
F32 = jnp.float32
MESH = pl.DeviceIdType.MESH
ANY = pl.BlockSpec(memory_space=pl.ANY)

V7X_VMEM_BYTES = 64 * 1024 * 1024
VMEM_LIMIT = V7X_VMEM_BYTES - 16 * 1024 * 1024

D_MODEL = 1024
HEAD_DIM = 64
GRID_W = 64
NA_HEADS = 8
NA_WIN_ROWS = 8
NA_WIN_COLS = 16
NA_WIDTH = NA_HEADS * HEAD_DIM
DIL_GROUPS = ((128, 1), (512, 4), (2048, 16))
DIL_HPG = 4
DIL_HEADS = DIL_HPG * len(DIL_GROUPS)
DIL_WIDTH = DIL_HEADS * HEAD_DIM
DIL_OUT_WIDTH = DIL_HPG * HEAD_DIM
DIL_RADIUS = 64
QKV_WIDTH = 3 * NA_WIDTH + 3 * DIL_WIDTH
D_FF = 4 * D_MODEL
PLE_DIM = 256
ROPE_THETA = 10000.0
RMS_EPS = 1e-6
NEG_INF = -1e30
Q_SCALE = HEAD_DIM ** -0.5

ADAM_LR = 0.001
ADAM_B1 = 0.9
ADAM_B2 = 0.999
ADAM_EPS = 1e-08
ADAM_WD = 0.01
ADAM_STEP = 10

N_CHIPS = 4
N_DEV = 8
PACK_W = 1024
GATHER_MIX = ("w_branch_na", "w_branch_dil", "w_out")
GATHER_MLP = ("w_up", "w_down", "w_ple_gate", "w_ple_proj")
REDUCE_EARLY = ("w_up", "w_down", "w_ple_gate", "w_out", "w_ple_proj", "w_branch_na", "w_branch_dil")
SMALL_ROWS = 16


def _cparams(sem=None):
    return pltpu.CompilerParams(dimension_semantics=sem, vmem_limit_bytes=VMEM_LIMIT)


def _mm(name, a, b, mode, tm, tn, tk, out_dtypes, epilogue=None, extras=(), consts=(), sums=(), after=(), into=None,
        b_view=None):
    if mode == "nn":
        (M, K), N = a.shape, b.shape[1]
    elif mode == "nt":
        (M, K), N = a.shape, b.shape[0]
    else:
        (K, M), N = a.shape, b.shape[1]
    if b_view is not None:
        N = b_view[0]
    tm, tn, tk = min(tm, M), min(tn, N), min(tk, K)
    assert M % tm == 0 and N % tn == 0 and K % tk == 0, (name, M, N, K, tm, tn, tk)
    if mode == "nn":
        a_spec = pl.BlockSpec((tm, tk), lambda i, j, k: (i, k))
        b_spec = pl.BlockSpec((tk, tn), lambda i, j, k: (k, j))
        dims = (((1,), (0,)), ((), ()))
    elif mode == "nt":
        a_spec = pl.BlockSpec((tm, tk), lambda i, j, k: (i, k))
        b_spec = pl.BlockSpec((tn, tk), lambda i, j, k: (j, k))
        dims = (((1,), (1,)), ((), ()))
    else:
        a_spec = pl.BlockSpec((tk, tm), lambda i, j, k: (k, i))
        b_spec = pl.BlockSpec((tk, tn), lambda i, j, k: (k, j))
        dims = (((0,), (0,)), ((), ()))
    if b_view is not None:
        b_spec = pl.BlockSpec(b_view[1], lambda i, j, k: b_view[2](j, k))
    nk = K // tk
    n_extra, n_const, n_out, n_sum = len(extras), len(consts), len(out_dtypes), len(sums)
    tile = pl.BlockSpec((tm, tn), lambda i, j, k: (i, j))
    assert not sums or tn == N, "row sums need whole rows in a tile"
    wide = [e for e in (*extras, *out_dtypes) if isinstance(e, tuple)]
    assert not wide or tn == N
    extra_specs = [pl.BlockSpec((tm, tn), functools.partial(lambda c, i, j, k: (i, c), e[1])) if isinstance(e, tuple) else tile
                   for e in extras]
    extras = [e[0] if isinstance(e, tuple) else e for e in extras]
    out_widths = [d[1] if isinstance(d, tuple) else N for d in out_dtypes]
    out_dtypes = [d[0] if isinstance(d, tuple) else d for d in out_dtypes]

    n_after = len(after)

    def body(a_ref, b_ref, *rest):
        extra_refs, rest = rest[:n_extra + n_const], rest[n_extra + n_const + n_after:]
        out_refs, sum_refs, acc = rest[:n_out], rest[n_out:n_out + n_sum], rest[-1]
        i, k = pl.program_id(0), pl.program_id(2)
        def product():
            return lax.dot_general(a_ref[...].astype(BF), b_ref[...].astype(BF), dims, preferred_element_type=F32)

        if nk > 1:
            @pl.when(k == 0)
            def _():
                acc[...] = jnp.zeros_like(acc)

            acc[...] += product()

        @pl.when(k == nk - 1)
        def _():
            total = product() if nk == 1 else acc[...]
            outs = (total,) if epilogue is None else epilogue(total, *[e[...] for e in extra_refs])
            for o_ref, val in zip(out_refs, outs[:n_out], strict=True):
                o_ref[...] = val.astype(o_ref.dtype).reshape(o_ref.shape)
            for s_ref, val in zip(sum_refs, outs[n_out:], strict=True):
                @pl.when(i == 0)
                def _():
                    s_ref[...] = val

                @pl.when(i != 0)
                def _():
                    s_ref[...] += val

    out_specs = ([tile if w == N else pl.BlockSpec((tm, w), lambda i, j, k: (i, 0)) for w in out_widths]
                 + [pl.BlockSpec((1, c), lambda i, j, k: (0, 0)) for c in sums])
    out_shape = ([jax.ShapeDtypeStruct((M, w), dt) for dt, w in zip(out_dtypes, out_widths, strict=True)]
                 + [jax.ShapeDtypeStruct((1, c), F32) for c in sums])
    operands, aliases = [a, b, *extras, *consts, *after], {}
    in_specs = ([a_spec, b_spec] + extra_specs
                + [pl.BlockSpec(c.shape, functools.partial(lambda nd, i, j, k: (0,) * nd, c.ndim)) for c in consts] + [ANY] * n_after)
    if into is not None:
        assert n_out == 1
        target, block, index = into
        out_specs = [pl.BlockSpec(block, lambda i, j, k: index(i, j))]
        out_shape = [jax.ShapeDtypeStruct(target.shape, target.dtype)]
        if not isinstance(target, jax.ShapeDtypeStruct):
            aliases = {len(operands): 0}
            operands.append(target)
            in_specs.append(ANY)
            n_after += 1

    outs = pl.pallas_call(
        body, name=name, grid=(M // tm, N // tn, nk),
        in_specs=in_specs, out_specs=out_specs, out_shape=out_shape,
        scratch_shapes=[pltpu.VMEM((tm, tn) if nk > 1 else (8, 128), F32)], input_output_aliases=aliases,
        compiler_params=_cparams(("arbitrary",) * 3 if sums else ("parallel", "parallel", "arbitrary")),
    )(*operands)
    return outs[0] if len(outs) == 1 else outs


def _row(arr, tm, col_block=None, width=None):
    width = arr.shape[1] if width is None else width
    cb = 0 if col_block is None else col_block
    return arr, pl.BlockSpec((tm, width), lambda i: (i, cb))


def _full(arr):
    nd = arr.ndim
    return arr, pl.BlockSpec(arr.shape, lambda i: (0,) * nd)


def _rowwise(name, body, T, tm, ins, outs, sums=(), after=()):
    n_in, n_out, n_sum, n_after = len(ins), len(outs), len(sums), len(after)

    def kern(*refs):
        in_refs, refs = refs[:n_in], refs[n_in + n_after:]
        out_refs, sum_refs = refs[:n_out], refs[n_out:]
        res = body(*[r[...] for r in in_refs])
        res = res if isinstance(res, tuple) else (res,)
        for o_ref, val in zip(out_refs, res[:n_out], strict=True):
            o_ref[...] = val.astype(o_ref.dtype)
        if n_sum:
            @pl.when(pl.program_id(0) == 0)
            def _():
                for s_ref in sum_refs:
                    s_ref[...] = jnp.zeros_like(s_ref)

            for s_ref, val in zip(sum_refs, res[n_out:], strict=True):
                s_ref[...] += val

    res = pl.pallas_call(
        kern, name=name, grid=(T // tm,),
        in_specs=[spec for _, spec in ins] + [ANY] * n_after,
        out_specs=[pl.BlockSpec((tm, c), lambda i: (i, 0)) for c, _ in outs]
        + [pl.BlockSpec((1, c), lambda i: (0, 0)) for c in sums],
        out_shape=[jax.ShapeDtypeStruct((T, c), dt) for c, dt in outs]
        + [jax.ShapeDtypeStruct((1, c), F32) for c in sums],
        compiler_params=_cparams(("arbitrary",)),
    )(*[a for a, _ in ins], *after)
    return res[0] if len(res) == 1 else res


def _sigmoid(x):
    return 1.0 / (1.0 + jnp.exp(-x))


def _rms(h):
    return lax.rsqrt(jnp.mean(h * h, axis=-1, keepdims=True) + RMS_EPS)


def _rms_bwd(dy, h, g):
    r = _rms(h)
    n = h * r
    dn = dy * g
    dh = r * (dn - n * jnp.mean(dn * n, axis=-1, keepdims=True))
    return dh, jnp.sum(dy * n, axis=0, keepdims=True)


def _rope(x, cos2, sin_signed):
    lane = lax.broadcasted_iota(jnp.int32, x.shape, 1)
    swapped = jnp.where((lane % HEAD_DIM) < HEAD_DIM // 2, pltpu.roll(x, 128 - HEAD_DIM // 2, 1), pltpu.roll(x, HEAD_DIM // 2, 1))
    return x * cos2 + swapped * sin_signed


NA_KEYS = NA_WIN_ROWS * GRID_W
NA_BASES = 8


def _na_row_geometry(r, rows):
    first = jnp.clip(r - NA_WIN_ROWS // 2, 0, rows - NA_WIN_ROWS)
    base = first - r + (NA_WIN_ROWS - 1)
    return pl.multiple_of(first * GRID_W, GRID_W), base


NA_ROWS_PER_STEP = 16
NA_BWD_ROWS_PER_STEP = 8


def _softmax_rows(s):
    p = jnp.exp(s - jnp.max(s, axis=-1, keepdims=True))
    return p / jnp.sum(p, axis=-1, keepdims=True)


def _split_pair(t):
    first = lax.broadcasted_iota(jnp.int32, t.shape, 1) < HEAD_DIM
    zero = jnp.zeros_like(t)
    return jnp.where(first, t, zero), jnp.where(first, zero, t)


def _join_pair(a, b):
    return jnp.where(lax.broadcasted_iota(jnp.int32, a.shape, 1) < HEAD_DIM, a, b)


_NT = (((1,), (1,)), ((), ()))
_TN = (((0,), (0,)), ((), ()))


def _na_fwd(qkv, tab):
    T = qkv.shape[0]
    rows = T // GRID_W
    n_pairs = NA_WIDTH // 128

    def body(q_ref, k_ref, v_ref, tab_ref, y_ref):
        def step(it, carry):
            geo = [_na_row_geometry(it * NA_ROWS_PER_STEP + u, rows) for u in range(NA_ROWS_PER_STEP)]
            q0s = [pl.multiple_of((it * NA_ROWS_PER_STEP + u) * GRID_W, GRID_W) for u in range(NA_ROWS_PER_STEP)]
            ss = [lax.dot_general(jnp.concatenate(_split_pair(q_ref[pl.ds(q0, GRID_W), :] * Q_SCALE), axis=0),
                                  k_ref[pl.ds(k0, NA_KEYS), :], _NT, preferred_element_type=F32)
                  for q0, (k0, _) in zip(q0s, geo)]
            ps = [_softmax_rows(s + jnp.concatenate([tab_ref[0, base], tab_ref[1, base]], axis=0)) for s, (_, base) in zip(ss, geo)]
            ys = [jnp.dot(p.astype(BF), v_ref[pl.ds(k0, NA_KEYS), :], preferred_element_type=F32) for p, (k0, _) in zip(ps, geo)]
            for q0, y2 in zip(q0s, ys):
                y_ref[pl.ds(q0, GRID_W), :] = _join_pair(y2[:GRID_W], y2[GRID_W:]).astype(y_ref.dtype)
            return carry

        lax.fori_loop(0, rows // NA_ROWS_PER_STEP, step, 0)

    def cols(first):
        return pl.BlockSpec((T, 128), lambda j: (0, first + j))

    return pl.pallas_call(
        body, name="na_fwd", grid=(n_pairs,),
        in_specs=[cols(0), cols(n_pairs), cols(2 * n_pairs), pl.BlockSpec((2, NA_BASES, GRID_W, NA_KEYS), lambda j: (j, 0, 0, 0))],
        out_specs=cols(0), out_shape=jax.ShapeDtypeStruct((T, NA_WIDTH), BF),
        compiler_params=_cparams(("parallel",)),
    )(qkv, qkv, qkv, tab)


def _na_bwd(qkv, tab, do):
    T = qkv.shape[0]
    rows = T // GRID_W
    n_pairs = NA_WIDTH // 128

    def body(q_ref, k_ref, v_ref, tab_ref, do_ref, dq_ref, dk_out, dv_out, dtab_ref, dk_ref, dv_ref):
        dk_ref[...] = jnp.zeros_like(dk_ref)
        dv_ref[...] = jnp.zeros_like(dv_ref)
        dtab_ref[...] = jnp.zeros_like(dtab_ref)

        def step(it, carry):
            U = NA_BWD_ROWS_PER_STEP
            geo = [_na_row_geometry(it * U + u, rows) for u in range(U)]
            q0s = [pl.multiple_of((it * U + u) * GRID_W, GRID_W) for u in range(U)]
            q2s = [jnp.concatenate(_split_pair(q_ref[pl.ds(q0, GRID_W), :] * Q_SCALE), axis=0) for q0 in q0s]
            do2s = [jnp.concatenate(_split_pair(do_ref[pl.ds(q0, GRID_W), :]), axis=0) for q0 in q0s]
            ss = [lax.dot_general(q2, k_ref[pl.ds(k0, NA_KEYS), :], _NT, preferred_element_type=F32) for q2, (k0, _) in zip(q2s, geo)]
            dps = [lax.dot_general(do2, v_ref[pl.ds(k0, NA_KEYS), :], _NT, preferred_element_type=F32) for do2, (k0, _) in zip(do2s, geo)]
            ps = [_softmax_rows(s + jnp.concatenate([tab_ref[0, base], tab_ref[1, base]], axis=0)) for s, (_, base) in zip(ss, geo)]
            dss = [p * (dp - jnp.sum(dp * p, axis=-1, keepdims=True)) for p, dp in zip(ps, dps)]
            dvs = [lax.dot_general(p.astype(BF), do2, _TN, preferred_element_type=F32) for p, do2 in zip(ps, do2s)]
            dsbs = [ds.astype(BF) for ds in dss]
            dqs = [jnp.dot(dsb, k_ref[pl.ds(k0, NA_KEYS), :], preferred_element_type=F32) for dsb, (k0, _) in zip(dsbs, geo)]
            dks = [lax.dot_general(dsb, q2, _TN, preferred_element_type=F32) for dsb, q2 in zip(dsbs, q2s)]
            for u in range(U):
                k0, base = geo[u]
                dtab_ref[0, base] += dss[u][:GRID_W]
                dtab_ref[1, base] += dss[u][GRID_W:]
                dq_ref[pl.ds(q0s[u], GRID_W), :] = (_join_pair(dqs[u][:GRID_W], dqs[u][GRID_W:]) * Q_SCALE).astype(dq_ref.dtype)
                dk_ref[pl.ds(k0, NA_KEYS), :] += dks[u]
                dv_ref[pl.ds(k0, NA_KEYS), :] += dvs[u]
            return carry

        lax.fori_loop(0, rows // NA_BWD_ROWS_PER_STEP, step, 0)
        dk_out[...] = dk_ref[...].astype(dk_out.dtype)
        dv_out[...] = dv_ref[...].astype(dv_out.dtype)

    def cols(first):
        return pl.BlockSpec((T, 128), lambda j: (0, first + j))

    tabs = pl.BlockSpec((2, NA_BASES, GRID_W, NA_KEYS), lambda j: (j, 0, 0, 0))
    wide = jax.ShapeDtypeStruct((T, NA_WIDTH), BF)
    return pl.pallas_call(
        body, name="na_bwd", grid=(n_pairs,),
        in_specs=[cols(0), cols(n_pairs), cols(2 * n_pairs), tabs, cols(0)],
        out_specs=[cols(0), cols(0), cols(0), tabs],
        out_shape=[wide, wide, wide, jax.ShapeDtypeStruct((NA_HEADS, NA_BASES, GRID_W, NA_KEYS), F32)],
        scratch_shapes=[pltpu.VMEM((T, 128), F32), pltpu.VMEM((T, 128), F32)],
        compiler_params=_cparams(("parallel",)),
    )(qkv, qkv, qkv, tab, do)


def _na_bias_table(rpb):
    H, n_rows, n_cols = rpb.shape

    def body(r_ref, tab_ref):
        q = lax.broadcasted_iota(jnp.int32, (GRID_W, 128), 0)
        kc = lax.broadcasted_iota(jnp.int32, (GRID_W, 128), 1)
        first = jnp.clip(q - NA_WIN_COLS // 2, 0, GRID_W - NA_WIN_COLS)
        valid = (kc >= first) & (kc < first + NA_WIN_COLS)
        toeplitz = []
        for ro in range(n_rows):
            row = jnp.broadcast_to(r_ref[pl.ds(ro, 1), :], (GRID_W, 128))
            shifted = pltpu.roll(pltpu.roll(row, 128 - (NA_WIN_COLS - 1), 1), 0, 1, stride=1, stride_axis=0)
            toeplitz.append(jnp.where(valid, shifted, NEG_INF))
        for base in range(NA_BASES):
            for j in range(NA_WIN_ROWS // 2):
                even, odd = toeplitz[base + 2 * j], toeplitz[base + 2 * j + 1]
                tab_ref[base, :, pl.ds(j * 128, 128)] = jnp.where(kc < GRID_W, even, pltpu.roll(odd, GRID_W, 1))

    padded = jnp.pad(rpb, ((0, 0), (0, 16 - n_rows), (0, 128 - n_cols)))
    return pl.pallas_call(
        body, name="na_bias_table", grid=(H,),
        in_specs=[pl.BlockSpec((None, 16, 128), lambda h: (h, 0, 0))],
        out_specs=pl.BlockSpec((None, NA_BASES, GRID_W, NA_KEYS), lambda h: (h, 0, 0, 0)),
        out_shape=jax.ShapeDtypeStruct((H, NA_BASES, GRID_W, NA_KEYS), F32),
        compiler_params=_cparams(("parallel",)),
    )(padded)


def _na_rpb_grad(dtab, after=()):
    H = dtab.shape[0]
    n_rows = 2 * NA_WIN_ROWS - 1
    n_cols = 2 * NA_WIN_COLS - 1

    def body(d_ref, *rest):
        o_ref = rest[-1]
        lane = lax.broadcasted_iota(jnp.int32, (GRID_W, 128), 1)
        low = lane < GRID_W
        flip = (lax.broadcasted_iota(jnp.int32, (GRID_W, GRID_W), 0) + lax.broadcasted_iota(jnp.int32, (GRID_W, GRID_W), 1)
                == GRID_W - 1).astype(BF)

        def reverse_rows(t):
            out = jnp.zeros_like(t)
            for _ in range(3):
                piece = t.astype(BF)
                out = out + jnp.dot(flip, piece, preferred_element_type=F32)
                t = t - piece.astype(F32)
            return out

        out_rows = []
        for ro in range(n_rows):
            acc = jnp.zeros((GRID_W, 128), F32)
            for base in range(NA_BASES):
                i = ro - base
                if not 0 <= i < NA_WIN_ROWS:
                    continue
                pair = d_ref[base, :, pl.ds((i // 2) * 128, 128)]
                if i % 2:
                    pair = pltpu.roll(pair, GRID_W, 1)
                acc = acc + jnp.where(low, pair, 0.0)
            skew = pltpu.roll(reverse_rows(acc), 0, 1, stride=1, stride_axis=0)
            diag = jnp.sum(skew, axis=0, keepdims=True)
            out_rows.append(pltpu.roll(jnp.broadcast_to(diag, (8, 128)), 128 - (GRID_W - NA_WIN_COLS), 1)[:1])
        out_rows.append(jnp.zeros((1, 128), F32))
        res = jnp.concatenate(out_rows, axis=0)
        o_ref[...] = jnp.where(lax.broadcasted_iota(jnp.int32, res.shape, 1) < n_cols, res, 0.0)

    return pl.pallas_call(
        body, name="na_rpb_grad", grid=(H,),
        in_specs=[pl.BlockSpec((None, NA_BASES, GRID_W, NA_KEYS), lambda h: (h, 0, 0, 0))] + [ANY] * len(after),
        out_specs=pl.BlockSpec((None, n_rows + 1, 128), lambda h: (h, 0, 0)),
        out_shape=jax.ShapeDtypeStruct((H, n_rows + 1, 128), F32),
        compiler_params=_cparams(("parallel",)),
    )(dtab, *after)


BAND_Q = 128
BAND_KEYS = BAND_Q + 2 * DIL_RADIUS


def _band_geometry(n, L):
    q0 = pl.multiple_of(n * BAND_Q, BAND_Q)
    k0 = pl.multiple_of(jnp.clip(q0 - DIL_RADIUS, 0, L - BAND_KEYS), DIL_RADIUS)
    qi = q0 + lax.broadcasted_iota(jnp.int32, (BAND_Q, BAND_KEYS), 0)
    kj = k0 + lax.broadcasted_iota(jnp.int32, (BAND_Q, BAND_KEYS), 1)
    return q0, k0, jnp.abs(qi - kj) <= DIL_RADIUS


DIL_PAIRS = DIL_OUT_WIDTH // 128


def _residue_shape(dil, T, dtype):
    return jax.ShapeDtypeStruct((DIL_PAIRS, dil, T // dil, 128), dtype)


def _residue_tile(dil, tm):
    return pl.BlockSpec((DIL_PAIRS, dil, tm // dil, 128), lambda i: (0, 0, i, 0))


def _to_natural(ref, scratch, dil, tm):
    tiles = []
    for pair in range(DIL_PAIRS):
        if dil == 1:
            tiles.append(ref[pair, 0].astype(F32))
            continue
        for r in range(dil):
            scratch[pl.ds(r, tm // dil, stride=dil), :] = ref[pair, r].astype(F32)
        tiles.append(scratch[...])
    return tiles


def _from_natural(tile, scratch, ref, pair, dil, tm):
    if dil == 1:
        ref[pair, 0] = tile.astype(ref.dtype)
        return
    scratch[...] = tile
    for r in range(dil):
        ref[pair, r] = scratch[pl.ds(r, tm // dil, stride=dil), :].astype(ref.dtype)


def _band_specs(group, T):
    dil = DIL_GROUPS[group][1]
    L = T // dil
    assert L % BAND_Q == 0 and L >= BAND_KEYS, (T, dil)
    per_residue = min(BAND_BLOCKS_PER_STEP, L // BAND_Q)
    residues = min(dil, BAND_BLOCKS_PER_STEP // per_residue)
    spec = pl.BlockSpec((None, residues, L, 128), lambda s: (s % DIL_PAIRS, s // DIL_PAIRS, 0, 0))
    return L, residues, per_residue, (dil // residues * DIL_PAIRS,), spec


BAND_BLOCKS_PER_STEP = 8


def _band_softmax(s, valid):
    s = jnp.where(valid, s, NEG_INF)
    m = jnp.max(s, axis=-1, keepdims=True)
    p = jnp.exp(s - m)
    l = jnp.sum(p, axis=-1, keepdims=True)
    return p / l, m + jnp.log(l)


def _band_fwd(q, k, v, group):
    T = q.shape[1] * q.shape[2]
    L, residues, U, grid, spec = _band_specs(group, T)

    def body(q_ref, k_ref, v_ref, o_ref, lse_ref):
        def step(it, carry):
            geo = [(r, *_band_geometry(it * U + u, L)) for r in range(residues) for u in range(U)]
            ss = [lax.dot_general(jnp.concatenate(_split_pair(q_ref[r, pl.ds(q0, BAND_Q), :]), axis=0),
                                  k_ref[r, pl.ds(k0, BAND_KEYS), :], _NT, preferred_element_type=F32) for r, q0, k0, _ in geo]
            pls = [_band_softmax(s, jnp.concatenate([valid, valid], axis=0)) for s, (_, _, _, valid) in zip(ss, geo)]
            os = [jnp.dot(p.astype(BF), v_ref[r, pl.ds(k0, BAND_KEYS), :], preferred_element_type=F32)
                  for (p, _), (r, _, k0, _) in zip(pls, geo)]
            for (r, q0, _, _), o2, (_, lse) in zip(geo, os, pls):
                o_ref[r, pl.ds(q0, BAND_Q), :] = _join_pair(o2[:BAND_Q], o2[BAND_Q:])
                lse2 = jnp.broadcast_to(lse, (2 * BAND_Q, 128))
                lse_ref[r, pl.ds(q0, BAND_Q), :] = _join_pair(lse2[:BAND_Q], lse2[BAND_Q:])
            return carry

        lax.fori_loop(0, L // (BAND_Q * U), step, 0)

    res = _residue_shape(DIL_GROUPS[group][1], T, F32)
    return pl.pallas_call(
        body, name=f"band_fwd_g{group}", grid=grid,
        in_specs=[spec] * 3, out_specs=[spec] * 2, out_shape=[res, res],
        compiler_params=_cparams(("parallel",)),
    )(q, k, v)


def _band_bwd(q, k, v, do, dlse, group):
    T = q.shape[1] * q.shape[2]
    L, residues, U, grid, spec = _band_specs(group, T)

    def body(q_ref, k_ref, v_ref, do_ref, dlse_ref, dq_ref, dk_ref, dv_ref):
        dk_ref[...] = jnp.zeros_like(dk_ref)
        dv_ref[...] = jnp.zeros_like(dv_ref)

        def step(it, carry):
            geo = [(r, *_band_geometry(it * U + u, L)) for r in range(residues) for u in range(U)]
            q2s = [jnp.concatenate(_split_pair(q_ref[r, pl.ds(q0, BAND_Q), :]), axis=0) for r, q0, _, _ in geo]
            do2s = [jnp.concatenate(_split_pair(do_ref[r, pl.ds(q0, BAND_Q), :]), axis=0) for r, q0, _, _ in geo]
            ss = [lax.dot_general(q2, k_ref[r, pl.ds(k0, BAND_KEYS), :], _NT, preferred_element_type=F32)
                  for q2, (r, _, k0, _) in zip(q2s, geo)]
            dps = [lax.dot_general(do2, v_ref[r, pl.ds(k0, BAND_KEYS), :], _NT, preferred_element_type=F32)
                   for do2, (r, _, k0, _) in zip(do2s, geo)]
            ps = [_band_softmax(s, jnp.concatenate([valid, valid], axis=0))[0] for s, (_, _, _, valid) in zip(ss, geo)]
            dss = []
            for p, dp, (r, q0, _, _) in zip(ps, dps, geo):
                dl = dlse_ref[r, pl.ds(q0, BAND_Q), :]
                dl2 = jnp.concatenate([dl[:, :1], dl[:, HEAD_DIM:HEAD_DIM + 1]], axis=0)
                dss.append(p * (dp - jnp.sum(dp * p, axis=-1, keepdims=True) + dl2))
            dvs = [lax.dot_general(p.astype(BF), do2, _TN, preferred_element_type=F32) for p, do2 in zip(ps, do2s)]
            dsbs = [ds.astype(BF) for ds in dss]
            dqs = [jnp.dot(dsb, k_ref[r, pl.ds(k0, BAND_KEYS), :], preferred_element_type=F32) for dsb, (r, _, k0, _) in zip(dsbs, geo)]
            dks = [lax.dot_general(dsb, q2, _TN, preferred_element_type=F32) for dsb, q2 in zip(dsbs, q2s)]
            for u, (r, q0, k0, _) in enumerate(geo):
                dq_ref[r, pl.ds(q0, BAND_Q), :] = _join_pair(dqs[u][:BAND_Q], dqs[u][BAND_Q:])
                dk_ref[r, pl.ds(k0, BAND_KEYS), :] += dks[u]
                dv_ref[r, pl.ds(k0, BAND_KEYS), :] += dvs[u]
            return carry

        lax.fori_loop(0, L // (BAND_Q * U), step, 0)

    res = _residue_shape(DIL_GROUPS[group][1], T, F32)
    return pl.pallas_call(
        body, name=f"band_bwd_g{group}", grid=grid,
        in_specs=[spec] * 5, out_specs=[spec] * 3, out_shape=[res] * 3,
        compiler_params=_cparams(("parallel",)),
    )(q, k, v, do, dlse)


def _head_sums(t):
    head = lax.broadcasted_iota(jnp.int32, t.shape, 1) // HEAD_DIM
    out = jnp.zeros_like(t)
    for h in range(t.shape[1] // HEAD_DIM):
        mine = head == h
        out = jnp.where(mine, jnp.sum(jnp.where(mine, t, 0.0), axis=-1, keepdims=True), out)
    return out


def _dil_merge_fwd(os, lses, T, tm):
    G = len(DIL_GROUPS)
    W = DIL_OUT_WIDTH
    dils = [d for _, d in DIL_GROUPS]

    def body(*refs):
        o_refs, lse_refs = refs[:G], refs[G:2 * G]
        y_ref, w_refs, on_refs, scratch = refs[2 * G], refs[2 * G + 1:3 * G + 1], refs[3 * G + 1:4 * G + 1], refs[-1]
        o = [jnp.concatenate(_to_natural(r, scratch, d, tm), axis=1) for r, d in zip(o_refs, dils)]
        ls = [jnp.concatenate(_to_natural(r, scratch, d, tm), axis=1) for r, d in zip(lse_refs, dils)]
        m = functools.reduce(jnp.maximum, ls)
        es = [jnp.exp(l - m) for l in ls]
        tot = functools.reduce(jnp.add, es)
        ws = [e / tot for e in es]
        y_ref[...] = functools.reduce(jnp.add, [w * t for w, t in zip(ws, o)]).astype(y_ref.dtype)
        for g in range(G):
            w_refs[g][...] = ws[g]
            on_refs[g][...] = o[g]

    nat = pl.BlockSpec((tm, W), lambda i: (i, 0))
    res = pl.pallas_call(
        body, name="dil_merge_fwd", grid=(T // tm,),
        in_specs=[_residue_tile(d, tm) for d in dils] * 2,
        out_specs=[nat] * (2 * G + 1),
        out_shape=[jax.ShapeDtypeStruct((T, W), BF)] + [jax.ShapeDtypeStruct((T, W), F32)] * (2 * G),
        scratch_shapes=[pltpu.VMEM((tm, 128), F32)],
        compiler_params=_cparams(("parallel",)),
    )(*os, *lses)
    return res[0], res[1:G + 1], res[G + 1:]


def _dil_merge_bwd(dy, os, ws, tm, after=()):
    G = len(DIL_GROUPS)
    T, W = dy.shape
    dils = [d for _, d in DIL_GROUPS]
    n_after = len(after)

    def body(*refs):
        dyt = refs[0][...]
        o, w = [r[...] for r in refs[1:G + 1]], [r[...] for r in refs[G + 1:2 * G + 1]]
        refs = refs[2 * G + 1 + n_after:]
        do_refs, dlse_refs, scratch = refs[:G], refs[G:2 * G], refs[-1]
        dws = [_head_sums(dyt * t) for t in o]
        mean = functools.reduce(jnp.add, [a * b for a, b in zip(w, dws)])
        for g, d in enumerate(dils):
            do, dlse = w[g] * dyt, w[g] * (dws[g] - mean)
            for pair in range(DIL_PAIRS):
                cols = slice(pair * 128, (pair + 1) * 128)
                _from_natural(do[:, cols], scratch, do_refs[g], pair, d, tm)
                _from_natural(dlse[:, cols], scratch, dlse_refs[g], pair, d, tm)

    nat = pl.BlockSpec((tm, W), lambda i: (i, 0))
    res = pl.pallas_call(
        body, name="dil_merge_bwd", grid=(T // tm,),
        in_specs=[nat] * (2 * G + 1) + [ANY] * n_after,
        out_specs=[_residue_tile(d, tm) for d in dils] * 2,
        out_shape=[_residue_shape(d, T, BF) for d in dils] + [_residue_shape(d, T, F32) for d in dils],
        scratch_shapes=[pltpu.VMEM((tm, 128), F32)],
        compiler_params=_cparams(("parallel",)),
    )(dy, *os, *ws, *after)
    return res[:G], res[G:]


def _qkv_prep(z, cos2, sin_signed, tm):
    T = z.shape[0]
    G = len(DIL_GROUPS)
    dils = [d for _, d in DIL_GROUPS]
    n_dil_blocks = 3 * DIL_WIDTH // 128

    def body(*refs):
        blocks = refs[:n_dil_blocks]
        cos_ref, sin_ref = refs[n_dil_blocks], refs[1 + n_dil_blocks]
        outs = refs[2 + n_dil_blocks:]
        for part in range(3):
            for g, d in enumerate(dils):
                out = outs[g * 3 + part]
                for pair in range(DIL_PAIRS):
                    blk = blocks[part * (DIL_WIDTH // 128) + g * DIL_PAIRS + pair]
                    for r in range(d):
                        rows = pl.ds(r, tm // d, stride=d) if d > 1 else slice(None)
                        x = blk[rows, :]
                        if part < 2:
                            x = _rope(x, cos_ref[rows, :], sin_ref[rows, :])
                        if part == 0:
                            x = x * Q_SCALE
                        out[pair, r] = x.astype(out.dtype)

    lane_block = [pl.BlockSpec((tm, 128), functools.partial(lambda b, i: (i, b), b)) for b in range(n_dil_blocks)]
    tab = pl.BlockSpec((tm, 128), lambda i: (i, 0))
    res = pl.pallas_call(
        body, name="qkv_prep", grid=(T // tm,),
        in_specs=lane_block + [tab, tab],
        out_specs=[_residue_tile(d, tm) for d in dils for _ in range(3)],
        out_shape=[_residue_shape(d, T, BF) for d in dils for _ in range(3)],
        compiler_params=_cparams(("parallel",)),
    )(*[z] * n_dil_blocks, cos2, sin_signed)
    return [res[3 * g:3 + 3 * g] for g in range(G)]


def _qkv_unprep(d_na, d_dil, cos2, sin_signed, tm, after=()):
    T = d_na[0].shape[0]
    G = len(DIL_GROUPS)
    dils = [d for _, d in DIL_GROUPS]
    n_after = len(after)

    def body(*refs):
        dq, dk, dv = (r[...] for r in refs[:3])
        res_refs = refs[3:3 + 3 * G]
        cs, sn = refs[3 + 3 * G][...], refs[4 + 3 * G][...]
        out, scratch = refs[5 + 3 * G + n_after], refs[-1]
        cols = [dq, dk, dv]
        for part in range(3):
            for g, d in enumerate(dils):
                for x in _to_natural(res_refs[g * 3 + part], scratch, d, tm):
                    if part < 2:
                        x = _rope(x, cs, -sn)
                    cols.append((x * Q_SCALE if part == 0 else x).astype(out.dtype))
        out[...] = jnp.concatenate(cols, axis=1)

    wide = pl.BlockSpec((tm, NA_WIDTH), lambda i: (i, 0))
    tab = pl.BlockSpec((tm, 128), lambda i: (i, 0))
    return pl.pallas_call(
        body, name="qkv_unprep", grid=(T // tm,),
        in_specs=[wide] * 3 + [_residue_tile(d, tm) for d in dils for _ in range(3)] + [tab, tab] + [ANY] * n_after,
        out_specs=pl.BlockSpec((tm, QKV_WIDTH), lambda i: (i, 0)),
        out_shape=jax.ShapeDtypeStruct((T, QKV_WIDTH), BF),
        scratch_shapes=[pltpu.VMEM((tm, 128), F32)],
        compiler_params=_cparams(("parallel",)),
    )(*d_na, *[t for g in range(G) for t in d_dil[g]], cos2, sin_signed, *after)


def _rope_tables(positions):
    half = HEAD_DIM // 2
    inv_freq = ROPE_THETA ** (-jnp.arange(half, dtype=F32) / half)
    ang = positions.astype(F32)[:, None] * inv_freq
    cos, sin = jnp.cos(ang), jnp.sin(ang)
    return jnp.tile(jnp.concatenate([cos, cos], axis=1), (1, 2)), jnp.tile(jnp.concatenate([-sin, sin], axis=1), (1, 2))


def _pack_rows(t):
    return t.reshape(-1, PACK_W)


def _me():
    return lax.axis_index("x"), lax.axis_index("y"), lax.axis_index("c")


def _other_chips(x, y):
    return [(1 - x, y), (x, 1 - y), (1 - x, 1 - y)]


def _pair_sum(g, got, tm):
    S, R, W = g.shape
    half = R // 2
    nb = half // tm

    def body(pos_ref, g_ref, got_ref, own_ref, ob_ref):
        tot = g_ref[...] + got_ref[...]
        ob_ref[...] = tot.astype(ob_ref.dtype)

        @pl.when(pl.program_id(1) == pos_ref[1])
        def _():
            own_ref[...] = tot

    tile = pl.BlockSpec((None, tm, W), lambda i, s, pos_ref: (s, i, 0))
    c, chip = lax.axis_index("c"), 2 * lax.axis_index("x") + lax.axis_index("y")
    return pl.pallas_call(
        body, name="pair_sum",
        grid_spec=pltpu.PrefetchScalarGridSpec(
            num_scalar_prefetch=1, grid=(nb, S),
            in_specs=[pl.BlockSpec((None, tm, W), lambda i, s, pos_ref: (s, pos_ref[0] * nb + i, 0)), tile],
            out_specs=[pl.BlockSpec((tm, W), lambda i, s, pos_ref: (i, 0)), tile]),
        out_shape=[jax.ShapeDtypeStruct((half, W), F32), jax.ShapeDtypeStruct((S, half, W), BF)],
        compiler_params=_cparams(("parallel", "arbitrary")),
    )(jnp.stack([c, chip]).astype(jnp.int32), g, got)


def _chip_sum(own, others, tm):
    n, h, W = others.shape
    nb = h // tm

    def body(c_ref, own_ref, p_ref, o_ref):
        o_ref[...] = ((own_ref[...] + p_ref[0].astype(F32)) + p_ref[1].astype(F32)) + p_ref[2].astype(F32)

    return pl.pallas_call(
        body, name="chip_sum",
        grid_spec=pltpu.PrefetchScalarGridSpec(
            num_scalar_prefetch=1, grid=(nb,),
            in_specs=[pl.BlockSpec((tm, W), lambda i, c_ref: (i, 0)), pl.BlockSpec((n, tm, W), lambda i, c_ref: (0, i, 0))],
            out_specs=pl.BlockSpec((tm, W), lambda i, c_ref: (c_ref[0] * nb + i, 0))),
        out_shape=jax.ShapeDtypeStruct((2 * h, W), F32),
        compiler_params=_cparams(("parallel",)),
    )(lax.axis_index("c").reshape(1).astype(jnp.int32), own, others)


def _join_halves(shard, after=()):
    h = shard.shape[0] // 2

    def body(in_ref, *rest):
        out_ref, send_sem, recv_sem = rest[len(after):]
        x, y, c = _me()
        cp = pltpu.make_async_remote_copy(
            src_ref=in_ref.at[pl.ds(c * h, h), :], dst_ref=out_ref.at[pl.ds(c * h, h), :],
            send_sem=send_sem, recv_sem=recv_sem, device_id=(x, y, 1 - c), device_id_type=MESH)
        cp.start()
        pltpu.make_async_remote_copy(
            src_ref=in_ref.at[pl.ds(c * h, h), :], dst_ref=out_ref.at[pl.ds((1 - c) * h, h), :],
            send_sem=send_sem, recv_sem=recv_sem, device_id=(x, y, 1 - c), device_id_type=MESH).wait_recv()
        cp.wait_send()

    return pl.pallas_call(
        body, name="join_halves", in_specs=[ANY] * (1 + len(after)), out_specs=ANY,
        out_shape=jax.ShapeDtypeStruct(shard.shape, shard.dtype), input_output_aliases={0: 0},
        scratch_shapes=[pltpu.SemaphoreType.DMA, pltpu.SemaphoreType.DMA],
    )(shard, *after)


def _allreduce_copies(src_ref, land_ref):
    x, y, c = _me()
    peers = [((x + fx) % 2, (y + fy) % 2, (c + fc) % 2) for fx in range(2) for fy in range(2) for fc in range(2)][1:]
    return [(src_ref, land_ref.at[4 * x + 2 * y + c], peer) for peer in peers]


def _allreduce_start(s, after):
    return _split_start("allreduce_small_start", s, (N_DEV, *s.shape), s.dtype, N_DEV - 1, _allreduce_copies, after)


def _allreduce_finish(flight, after):
    own, landed = _split_wait("allreduce_small_wait", flight, after, N_DEV - 1, _allreduce_copies)
    me = 4 * lax.axis_index("x") + 2 * lax.axis_index("y") + lax.axis_index("c")
    parts = lax.dynamic_update_slice(landed, own[None], (me, 0, 0))

    def body(p_ref, o_ref):
        total = p_ref[0]
        for d in range(1, N_DEV):
            total = total + p_ref[d]
        o_ref[...] = total

    return pl.pallas_call(
        body, name="allreduce_small_sum",
        in_specs=[pl.BlockSpec(memory_space=pltpu.VMEM)], out_specs=pl.BlockSpec(memory_space=pltpu.VMEM),
        out_shape=jax.ShapeDtypeStruct(own.shape, F32),
    )(parts)


HBM_SPEC = pl.BlockSpec(memory_space=pltpu.HBM)
SEM_SPEC = pl.BlockSpec(memory_space=pltpu.SEMAPHORE)
DATAFLOW = pltpu.SideEffectType.DATAFLOW_SIDE_EFFECTING


class _InFlight(NamedTuple):
    sems: tuple
    src: jax.Array
    land: jax.Array
    token: jax.Array


def _split_start(name, src, land_shape, land_dtype, n, copies, after=()):
    n_after = len(after)

    def body(src_ref, land_ref, *rest):
        rest = rest[n_after:]
        sems, token = rest[:2 * n], rest[-1]
        for k, (s, d, peer) in enumerate(copies(src_ref, land_ref)):
            pltpu.make_async_remote_copy(src_ref=s, dst_ref=d, send_sem=sems[k], recv_sem=sems[n + k],
                                         device_id=peer, device_id_type=MESH).start()
        token[...] = jnp.zeros_like(token)

    outs = pl.pallas_call(
        body, name=name,
        out_shape=(*[pltpu.SemaphoreType.DMA(())] * (2 * n), pltpu.HBM(src.shape, src.dtype), pltpu.HBM(land_shape, land_dtype),
                   jax.ShapeDtypeStruct((8, 128), F32)),
        in_specs=(HBM_SPEC, HBM_SPEC, *[ANY] * n_after),
        out_specs=(*[SEM_SPEC] * (2 * n), HBM_SPEC, HBM_SPEC, pl.BlockSpec(memory_space=pltpu.VMEM)),
        input_output_aliases={0: 2 * n, 1: 2 * n + 1},
        compiler_params=pltpu.CompilerParams(has_side_effects=DATAFLOW),
    )(pltpu.with_memory_space_constraint(src, pltpu.HBM), pltpu.with_memory_space_constraint(lax.empty(land_shape, land_dtype), pltpu.HBM),
      *after)
    return _InFlight(tuple(outs[:2 * n]), outs[2 * n], outs[2 * n + 1], outs[2 * n + 2])


def _split_wait(name, flight, after, n, copies):
    after = after if isinstance(after, tuple) else (after,)

    def body(src_ref, land_ref, *rest):
        sems = rest[:2 * n]
        for k, (s, d, peer) in enumerate(copies(src_ref, land_ref)):
            cp = pltpu.make_async_remote_copy(src_ref=s, dst_ref=d, send_sem=sems[k], recv_sem=sems[n + k],
                                              device_id=peer, device_id_type=MESH)
            cp.wait_send()
            cp.wait_recv()

    return pl.pallas_call(
        body, name=name,
        out_shape=(pltpu.HBM(flight.src.shape, flight.src.dtype), pltpu.HBM(flight.land.shape, flight.land.dtype)),
        in_specs=(HBM_SPEC, HBM_SPEC, *[SEM_SPEC] * (2 * n), *[ANY] * len(after)),
        out_specs=(HBM_SPEC, HBM_SPEC), input_output_aliases={0: 0, 1: 1},
        compiler_params=pltpu.CompilerParams(has_side_effects=DATAFLOW),
    )(flight.src, flight.land, *flight.sems, *after)


def _gather_copies(src_ref, land_ref):
    x, y, c = _me()
    return [(src_ref, land_ref.at[2 * x + y], (*chip, c)) for chip in _other_chips(x, y)]


def _gather_start(packed, tag, after=()):
    return _split_start(f"gather_start_{tag}", packed, (N_CHIPS, *packed.shape), packed.dtype, 3, _gather_copies, after)


def _gather_wait(flight, after, tag):
    src, others = _split_wait(f"gather_wait_{tag}", flight, after, 3, _gather_copies)
    return lax.dynamic_update_slice(others, src[None], (2 * lax.axis_index("x") + lax.axis_index("y"), 0, 0))


def _across_copies(src_ref, land_ref):
    x, y, c = _me()
    half = src_ref.shape[0] // 2
    rows = pl.ds(c * half, half)
    return [(src_ref.at[rows, :], land_ref.at[2 * x + y, rows, :], (*chip, c)) for chip in _other_chips(x, y)]


def _to_sibling_copies(all_ref, unused_ref):
    x, y, c = _me()
    half = all_ref.shape[1] // 2
    places = [all_ref.at[2 * chip[0] + chip[1], pl.ds(c * half, half), :] for chip in _other_chips(x, y)]
    return [(place, place, (x, y, 1 - c)) for place in places]


def _gather_halves_start(shard, tag, after=()):
    return _split_start(f"gather_{tag}_across_start", shard, (N_CHIPS, *shard.shape), shard.dtype, 3, _across_copies, after)


def _gather_halves_relay(flight, after, tag):
    shard, landed = _split_wait(f"gather_{tag}_across_wait", flight, after, 3, _across_copies)
    return shard, _split_start(f"gather_{tag}_sibling_start", landed, (8, 128), landed.dtype, 3, _to_sibling_copies)


def _gather_halves_finish(shard, relay, after, tag):
    others = _split_wait(f"gather_{tag}_sibling_wait", relay, after, 3, _to_sibling_copies)[0]
    return lax.dynamic_update_slice(others, shard[None], (2 * lax.axis_index("x") + lax.axis_index("y"), 0, 0))


def _assemble_w_in(shards, tm):
    S, R, C = shards.shape
    n_gates = 2 * D_MODEL

    def body(s_ref, w_ref, g_ref):
        full = jnp.concatenate([s_ref[s] for s in range(S)], axis=1)
        w_ref[...] = full
        g_ref[...] = full[:, S * C - n_gates:]

    return pl.pallas_call(
        body, name="assemble_w_in", grid=(R // tm,),
        in_specs=[pl.BlockSpec((S, tm, C), lambda i: (0, i, 0))],
        out_specs=[pl.BlockSpec((tm, S * C), lambda i: (i, 0)), pl.BlockSpec((tm, n_gates), lambda i: (i, 0))],
        out_shape=[jax.ShapeDtypeStruct((R, S * C), shards.dtype), jax.ShapeDtypeStruct((R, n_gates), shards.dtype)],
        compiler_params=_cparams(("parallel",)),
    )(shards)


def _swap_copies(src_ref, land_ref):
    x, y, c = _me()
    half = land_ref.shape[1]
    return [(src_ref.at[:, pl.ds((1 - c) * half, half), :], land_ref, (x, y, 1 - c))]


def _swap_start(g, tag):
    S, R, W = g.shape
    return _split_start(f"swap_halves_start_{tag}", g, (S, R // 2, W), g.dtype, 1, _swap_copies)


def _swap_wait(flight, after, tag):
    return _split_wait(f"swap_halves_wait_{tag}", flight, after, 1, _swap_copies)


def _scatter_copies(src_ref, land_ref):
    x, y, c = _me()
    return [(src_ref.at[2 * chip[0] + chip[1]], land_ref.at[j], (*chip, c)) for j, chip in enumerate(_other_chips(x, y))]


def _scatter_start(part, tag):
    S, h, W = part.shape
    return _split_start(f"scatter_chips_start_{tag}", part, (S - 1, h, W), part.dtype, 3, _scatter_copies)


def _scatter_wait(flight, after, tag):
    return _split_wait(f"scatter_chips_wait_{tag}", flight, after, 3, _scatter_copies)[1]


def _join_copies(shard_ref, unused_ref):
    x, y, c = _me()
    h = shard_ref.shape[0] // 2
    rows = shard_ref.at[pl.ds(c * h, h), :]
    return [(rows, rows, (x, y, 1 - c))]


def _join_start(shard):
    return _split_start("join_halves_start", shard, (8, 128), shard.dtype, 1, _join_copies)


def _join_wait(flight, after):
    return _split_wait("join_halves_wait", flight, after, 1, _join_copies)[0]


def _adamw(name, g, g_row0, w, m, v):
    _, R, C = w.shape
    tm = next(cand for cand in (368, 256, 128, 64, 32, 16, 8) if R % cand == 0)
    assert g_row0 % tm == 0 and g.shape[1] == C

    def body(g_ref, w_ref, m_ref, v_ref, go_ref, d_ref, mo_ref, vo_ref):
        gt = g_ref[...]
        mt = ADAM_B1 * m_ref[...] + (1.0 - ADAM_B1) * gt
        vt = ADAM_B2 * v_ref[...] + (1.0 - ADAM_B2) * jnp.square(gt)
        m_hat = mt / (1.0 - ADAM_B1 ** ADAM_STEP)
        v_hat = vt / (1.0 - ADAM_B2 ** ADAM_STEP)
        go_ref[...] = gt
        d_ref[...] = -ADAM_LR * (m_hat / (jnp.sqrt(v_hat) + ADAM_EPS) + ADAM_WD * w_ref[...])
        mo_ref[...] = mt
        vo_ref[...] = vt

    state = pl.BlockSpec((None, tm, C), lambda i: (0, i, 0))
    return pl.pallas_call(
        body, name=name, grid=(R // tm,),
        in_specs=[pl.BlockSpec((tm, C), lambda i: (g_row0 // tm + i, 0)), state, state, state],
        out_specs=[state] * 4, out_shape=[jax.ShapeDtypeStruct((1, R, C), F32)] * 4,
        compiler_params=_cparams(("parallel",)),
    )(g, w, m, v)


def _unpack_weights(gathered, names):
    S = gathered.shape[0]
    shard_shapes = {"w_in": (D_MODEL, (QKV_WIDTH + 2 * D_MODEL) // S), "w_branch_na": (NA_WIDTH, D_MODEL // S),
                    "w_branch_dil": (DIL_OUT_WIDTH, D_MODEL // S), "w_out": (D_MODEL // S, D_MODEL),
                    "w_up": (D_MODEL, D_FF // S), "w_down": (D_FF // S, D_MODEL),
                    "w_ple_gate": (D_MODEL // S, D_MODEL), "w_ple_proj": (PLE_DIM, D_MODEL // S)}
    col_sharded = {"w_in", "w_branch_na", "w_branch_dil", "w_up", "w_ple_proj"}
    out, r0 = {}, 0
    for name in names:
        rows, cols = shard_shapes[name]
        n = rows * cols // PACK_W
        t = gathered[:, r0:r0 + n, :].reshape(S, rows, cols)
        r0 += n
        out[name] = t.transpose(1, 0, 2).reshape(rows, S * cols) if name in col_sharded else t.reshape(S * rows, cols)
    return out


def kernel(x, p, positions, g_mix, w_in, rpb, w_branch_na, w_branch_dil, w_out, g_mlp, w_up, w_down, g_ple, w_ple_gate, w_ple_proj, g_final, loss_target, m_g_mix, m_w_in, m_rpb, m_w_branch_na, m_w_branch_dil, m_w_out, m_g_mlp, m_w_up, m_w_down, m_g_ple, m_w_ple_gate, m_w_ple_proj, m_g_final, v_g_mix, v_w_in, v_rpb, v_w_branch_na, v_w_branch_dil, v_w_out, v_g_mlp, v_w_up, v_w_down, v_g_ple, v_w_ple_gate, v_w_ple_proj, v_g_final):
    shards = {"w_in": w_in[0], "w_branch_na": w_branch_na[0], "w_branch_dil": w_branch_dil[0], "w_out": w_out[0],
              "w_up": w_up[0], "w_down": w_down[0], "w_ple_gate": w_ple_gate[0], "w_ple_proj": w_ple_proj[0]}
    params = {"w_in": w_in, "w_branch_na": w_branch_na, "w_branch_dil": w_branch_dil, "w_out": w_out, "w_up": w_up,
              "w_down": w_down, "w_ple_gate": w_ple_gate, "w_ple_proj": w_ple_proj,
              "m_w_in": m_w_in, "m_w_branch_na": m_w_branch_na, "m_w_branch_dil": m_w_branch_dil, "m_w_out": m_w_out,
              "m_w_up": m_w_up, "m_w_down": m_w_down, "m_w_ple_gate": m_w_ple_gate, "m_w_ple_proj": m_w_ple_proj,
              "v_w_in": v_w_in, "v_w_branch_na": v_w_branch_na, "v_w_branch_dil": v_w_branch_dil, "v_w_out": v_w_out,
              "v_w_up": v_w_up, "v_w_down": v_w_down, "v_w_ple_gate": v_w_ple_gate, "v_w_ple_proj": v_w_ple_proj}

    xs, ps, tgt = x[0], p[0, 0], loss_target[0]
    T = xs.shape[0]
    TM = 512
    gm, gl, gp, gf = g_mix, g_mlp, g_ple, g_final.reshape(1, D_MODEL)

    across = _gather_halves_start(shards["w_in"].astype(BF), "in")
    a = _rowwise("norm_mix", lambda h, g: h * _rms(h) * g, T, TM, [_row(xs, TM), _full(gm)], [(D_MODEL, BF)],
                 after=(across.token,))
    cos2, sin_signed = _rope_tables(positions[0])
    tab = _na_bias_table(rpb[0])
    packed_mix = jnp.concatenate([_pack_rows(shards[n].astype(BF)) for n in GATHER_MIX], axis=0)
    packed_mlp = jnp.concatenate([_pack_rows(shards[n].astype(BF)) for n in GATHER_MLP], axis=0)
    w_in_shard, w_in_relay = _gather_halves_relay(across, (a, tab, cos2, sin_signed, packed_mix, packed_mlp), "in")
    w_in_all = _gather_halves_finish(w_in_shard, w_in_relay, w_in_relay.token, "in")
    w_in_full, w_gates = _assemble_w_in(w_in_all, 256)
    W = {"w_in": w_in_full}
    mix_flight = _gather_start(packed_mix, "mix", after=(w_in_all,))
    mlp_across = _gather_halves_start(packed_mlp, "mlp", after=(mix_flight.token,))

    n3 = 3 * NA_WIDTH
    qkv = _mm("in_na", a, W["w_in"], "nn", 1024, 768, 1024, [BF], after=(mlp_across.token,),
              b_view=(n3, (D_MODEL, 768), lambda j, k: (k, j)))
    z_dil = _mm("in_dil", a, W["w_in"], "nn", 1024, 768, 1024, [F32], after=(mlp_across.token,),
                b_view=(3 * DIL_WIDTH, (D_MODEL, 768), lambda j, k: (k, n3 // 768 + j)))
    z_gates = _mm("in_gates", a, w_gates, "nn", 1024,1024, 1024, [BF], after=(mlp_across.token,))

    dil_ops = _qkv_prep(z_dil, cos2, sin_signed, TM)
    y_na = _na_fwd(qkv, tab)
    band = [_band_fwd(*dil_ops[g], g) for g in range(len(DIL_GROUPS))]
    y_dil, w_grp, o_nat = _dil_merge_fwd([b[0] for b in band], [b[1] for b in band], T, TM)

    W.update(_unpack_weights(_gather_wait(mix_flight, y_dil, "mix"), GATHER_MIX))
    mlp_shard, mlp_relay = _gather_halves_relay(mlp_across, y_dil, "mlp")
    u_na = _mm("branch_na", y_na, W["w_branch_na"], "nn", 1024,1024, 512, [BF], after=(mlp_relay.token,))
    def gate_mix(acc, gn, gd, un):
        ud = acc.astype(BF)
        return ud, _sigmoid(gn.astype(F32)) * un.astype(F32) + _sigmoid(gd.astype(F32)) * ud.astype(F32)

    u_dil, mixed = _mm("branch_dil", y_dil, W["w_branch_dil"], "nn", 512, 1024, 256, [BF, BF], epilogue=gate_mix,
                       extras=((z_gates, 0), (z_gates, 1), u_na))

    def add_norm(d, h, g):
        h = h + d
        return h, h * _rms(h) * g

    h1, cn = _mm("out_proj", mixed, W["w_out"], "nn", 512, 1024, 1024, [F32, BF], epilogue=add_norm, extras=(xs,), consts=(gl,))
    mlp_all = _gather_halves_finish(mlp_shard, mlp_relay, cn, "mlp")
    W.update({n: t for n, t in _unpack_weights(mlp_all, GATHER_MLP).items() if n.startswith("w_ple")})
    chip_block = (None, D_MODEL, PACK_W)
    up, act = _mm("mlp_up", cn, mlp_all, "nn", 1024,1024, 1024, [BF, BF],
                  epilogue=lambda acc: (acc, jnp.square(jnp.maximum(acc, 0.0))), b_view=(D_FF, chip_block, lambda j, k: (j, 0, 0)))
    h2, en = _mm("mlp_down", act, mlp_all, "nn", 1024, 1024, 1024, [F32, BF], epilogue=add_norm, extras=(h1,), consts=(gp,),
                 b_view=(D_MODEL, chip_block, lambda j, k: (k, 1, 0)))
    pp = _mm("ple_proj", ps, W["w_ple_proj"], "nn", 1024,1024, 256, [F32])

    def head(gtt, h2t, ppt, tg, g):
        sg = _sigmoid(gtt)
        h3 = h2t + sg * ppt
        yo = h3 * _rms(h3) * g
        diff = yo - tg
        loss = 0.5 * jnp.sum(jnp.mean(jnp.square(diff), axis=-1, keepdims=True), axis=0, keepdims=True)
        dh3, dg = _rms_bwd(diff * (1.0 / D_MODEL), h3, g)
        return dh3, dh3 * ppt * sg * (1.0 - sg), dh3 * sg, jnp.broadcast_to(loss, (1, 128)), dg

    dh3, d_gt, d_pp, loss_part, dg_final = _mm(
        "ple_gate_loss_head", en, W["w_ple_gate"], "nn", 512, 1024, 1024, [F32, BF, BF], epilogue=head,
        extras=(h2, pp, tgt), consts=(gf,), sums=[128, D_MODEL])

    early_shapes = {n: shards[n].shape for n in REDUCE_EARLY}
    early_rows = sum(r * c for r, c in early_shapes.values()) // PACK_W
    shard_rows = D_MODEL // N_CHIPS
    early_buf = _mm("g_ple_gate", en, d_gt, "tn", 1024, 1024, 1024, [F32],
                    into=(jax.ShapeDtypeStruct((N_CHIPS, early_rows, PACK_W), F32), (N_CHIPS, shard_rows, PACK_W),
                          lambda i, j: (0, 2 * D_MODEL // shard_rows, 0)))
    g_ple_proj = _mm("g_ple_proj", ps, d_pp, "tn", 256, 1024, 1024,[F32])

    def add_norm_bwd(dn, dh_out, h, g):
        dh, dg = _rms_bwd(dn, h, g)
        dh = dh_out + dh
        return dh, dh, dg

    dh2, dh2_b, dg_ple = _mm("d_ple_gate", d_gt, W["w_ple_gate"], "nt", 512, 1024, 1024, [F32, BF],
                             epilogue=add_norm_bwd, extras=(dh3, h2), consts=(gp,), sums=[D_MODEL])
    d_up = _mm("d_mlp_down", dh2_b, mlp_all, "nt", 1024,1024, 1024, [BF], b_view=(D_FF, chip_block, lambda j, k: (j, 1, 0)),
               epilogue=lambda acc, u: (acc * (2.0 * jnp.maximum(u.astype(F32), 0.0)),), extras=(up,))
    early_buf = _mm("g_mlp_down", act, dh2_b, "tn", 1024, 1024, 1024,[F32],
                    into=(early_buf, (None, D_MODEL, PACK_W), lambda i, j: (i, 1, 0)))
    early_buf = _mm("g_mlp_up", cn, d_up, "tn", 1024, 1024, 1024,[F32],
                    into=(early_buf, (None, D_MODEL, PACK_W), lambda i, j: (j, 0, 0)))
    dh1, dh1_b, dg_mlp = _mm("d_mlp_up", d_up, mlp_all, "nt", 1024, 1024, 1024, [F32, BF], epilogue=add_norm_bwd,
                             b_view=(D_MODEL, chip_block, lambda j, k: (k, 0, 0)),
                             extras=(dh2, h1), consts=(gl,), sums=[D_MODEL])
    early_buf = _mm("g_out_proj", mixed, dh1_b, "tn", 1024, 1024, 1024, [F32],
                    into=(early_buf, (N_CHIPS, shard_rows, PACK_W), lambda i, j: (0, 2 * D_MODEL // shard_rows + 1, 0)))

    def gate_bwd(dm, gn, gd, un, ud):
        gn, gd, un, ud = (t.astype(F32) for t in (gn, gd, un, ud))
        sn, sd = _sigmoid(gn), _sigmoid(gd)
        return jnp.concatenate([dm * un * sn * (1.0 - sn), dm * ud * sd * (1.0 - sd)], axis=1), dm * sn, dm * sd

    dz_gates, d_u_na, d_u_dil = _mm("d_out_proj", dh1_b, W["w_out"], "nt", 512, 1024, 1024, [(BF, 2 * D_MODEL), BF, BF],
                                    epilogue=gate_bwd, extras=((z_gates, 0), (z_gates, 1), u_na, u_dil))
    g_branch_na = _mm("g_branch_na", y_na, d_u_na, "tn", 1024, 1024, 1024,[F32])
    g_branch_dil = _mm("g_branch_dil", y_dil, d_u_dil, "tn", 256, 1024, 1024,[F32])
    small_rows = [jnp.concatenate([_pack_rows(g[:, s * shard_rows:(s + 1) * shard_rows]) for g in (g_ple_proj, g_branch_na, g_branch_dil)],
                                  axis=0) for s in range(N_CHIPS)]
    early_buf = lax.dynamic_update_slice(early_buf, jnp.stack(small_rows), (0, 2 * D_MODEL + 2 * shard_rows, 0))
    early_tm = early_rows // 4
    swap_flight = _swap_start(early_buf, "early")
    d_y_na = _mm("d_branch_na", d_u_na, W["w_branch_na"], "nt", 1024,512, 1024, [BF], after=(swap_flight.token,))
    d_y_dil = _mm("d_branch_dil", d_u_dil, W["w_branch_dil"], "nt", 1024,256, 1024, [F32])

    dqa, dka, dva, dtab = _na_bwd(qkv, tab, d_y_na)
    early_g, early_got = _swap_wait(swap_flight, dqa, "early")
    early_pair, early_pair_b = _pair_sum(early_g, early_got, early_tm)
    scatter_flight = _scatter_start(early_pair_b, "early")

    do_res, dlse_res = _dil_merge_bwd(d_y_dil, o_nat, w_grp, TM, after=(scatter_flight.token,))
    d_dil = [_band_bwd(*dil_ops[g], do_res[g], dlse_res[g], g) for g in range(len(DIL_GROUPS))]

    dz_qkv = _qkv_unprep((dqa, dka, dva), d_dil, cos2, sin_signed, TM)
    in_cols = shards["w_in"].shape[1]
    qkv_rows, gate_tm = dz_qkv.shape[1], 256
    assert qkv_rows % gate_tm == 0
    g_in = _mm("g_in_qkv", dz_qkv, a, "tn", 1280, 1024, 1024, [F32],
               into=(jax.ShapeDtypeStruct((N_CHIPS * in_cols, D_MODEL), F32), (1280, D_MODEL), lambda i, j: (i, 0)))
    g_in = _mm("g_in_gates", dz_gates, a, "tn", gate_tm, 1024, T, [F32],
               into=(g_in, (gate_tm, D_MODEL), lambda i, j: (qkv_rows // gate_tm + i, 0)))
    early_mine = _chip_sum(early_pair, _scatter_wait(scatter_flight, (g_in,), "early"), early_tm)
    join_flight = _join_start(early_mine)

    late_tm = in_cols // 4
    late_swap = _swap_start(g_in.reshape(N_CHIPS, in_cols, D_MODEL), "late")
    d_a = _mm("d_in_qkv", dz_qkv, W["w_in"], "nt", 1024,1024, 1280, [F32], after=(late_swap.token, join_flight.token),
              b_view=(D_MODEL, (D_MODEL, 1280), lambda j, k: (j, k)))
    late_g, late_got = _swap_wait(late_swap, d_a, "late")
    late_pair, late_pair_b = _pair_sum(late_g, late_got, late_tm)
    late_scatter = _scatter_start(late_pair_b, "late")
    d_rpb = _na_rpb_grad(dtab, after=(late_scatter.token,))[:, :2 * NA_WIN_ROWS - 1, :2 * NA_WIN_COLS - 1]
    def first_bwd(dn_gates, dn_qkv, dh_out, h, g):
        dh, dg = _rms_bwd(dn_gates + dn_qkv, h, g)
        return dh_out + dh, dg

    grad_x, dg_mix = _mm("d_in_gates", dz_gates, w_gates, "nt", 512, 1024, 1024, [F32], epilogue=first_bwd,
                         extras=(d_a, dh1, xs), consts=(gm,), sums=[D_MODEL], after=(late_scatter.token,))
    early_shard = _join_wait(join_flight, grad_x)

    n_rpb = rpb.size
    rpb_rows = 4
    small = jnp.concatenate([
        dg_mix, dg_mlp, dg_ple, dg_final,
        jnp.pad(d_rpb.reshape(-1), (0, rpb_rows * D_MODEL - n_rpb)).reshape(rpb_rows, D_MODEL),
        jnp.pad(loss_part, ((0, 0), (0, D_MODEL - loss_part.shape[1]))),
        jnp.zeros((SMALL_ROWS - 5 - rpb_rows, D_MODEL), F32)], axis=0)
    out = {"grad": {}, "delta": {}, "new_m": {}, "new_v": {}}

    def update(n, g, row0):
        res = _adamw("adamw_" + n, g, row0, params[n], params["m_" + n], params["v_" + n])
        for kind, t in zip(("grad", "delta", "new_m", "new_v"), res, strict=True):
            out[kind][n] = t

    row0 = 0
    for n in REDUCE_EARLY:
        rows, cols = early_shapes[n]
        n_rows = rows * cols // PACK_W
        if cols == PACK_W:
            update(n, early_shard, row0)
        else:
            update(n, early_shard[row0:row0 + n_rows].reshape(rows, cols), 0)
        row0 += n_rows
    late_others = _scatter_wait(late_scatter, (*[out["new_v"][n] for n in REDUCE_EARLY], d_rpb), "late")
    late_mine = _chip_sum(late_pair, late_others, late_tm)
    small_flight = _allreduce_start(small, after=(late_mine,))
    res = _adamw("adamw_w_in", _join_halves(late_mine, after=(small_flight.token,)), 0,
                 *[jnp.swapaxes(params[n], 1, 2) for n in ("w_in", "m_w_in", "v_w_in")])
    small = _allreduce_finish(small_flight, res[3])
    for kind, t in zip(("grad", "delta", "new_m", "new_v"), res, strict=True):
        out[kind]["w_in"] = jnp.swapaxes(t, 1, 2)
    loss = small[4 + rpb_rows, 0]

    def small_pack(a0, a1, a2, a3, r):
        return jnp.concatenate([a0.reshape(1, -1), a1.reshape(1, -1), a2.reshape(1, -1), a3.reshape(1, -1),
                                jnp.pad(r.reshape(-1), (0, rpb_rows * D_MODEL - n_rpb)).reshape(rpb_rows, D_MODEL)], axis=0)

    small_res = _adamw("adamw_small", small, 0, small_pack(g_mix, g_mlp, g_ple, g_final, rpb)[None],
                       small_pack(m_g_mix, m_g_mlp, m_g_ple, m_g_final, m_rpb)[None],
                       small_pack(v_g_mix, v_g_mlp, v_g_ple, v_g_final, v_rpb)[None])

    def small_unpack(t):
        return {"g_mix": t[0].reshape(g_mix.shape), "g_mlp": t[1].reshape(g_mlp.shape), "g_ple": t[2].reshape(g_ple.shape),
                "g_final": t[3].reshape(g_final.shape), "rpb": t[4:].reshape(-1)[:n_rpb].reshape(rpb.shape)}

    for kind, t in zip(("grad", "delta", "new_m", "new_v"), small_res, strict=True):
        out[kind].update(small_unpack(t[0]))

    order = ["g_mix", "w_in", "rpb", "w_branch_na", "w_branch_dil", "w_out", "g_mlp", "w_up", "w_down", "g_ple",
             "w_ple_gate", "w_ple_proj", "g_final"]
    return (loss, grad_x[None], *[out["grad"][n] for n in order], *[out["delta"][n] for n in order],
            *[out["new_m"][n] for n in order], *[out["new_v"][n] for n in order])
```

```python
import functools
from typing import NamedTuple

import jax
import jax.numpy as jnp
from jax import lax
from jax.experimental import pallas as pl
from jax.experimental.pallas import tpu as pltpu

BF = jnp.bfloat16
F32 = jnp.float32
MESH = pl.DeviceIdType.MESH
ANY = pl.BlockSpec(memory_space=pl.ANY)

V7X_VMEM_BYTES = 64 * 1024 * 1024
VMEM_LIMIT = V7X_VMEM_BYTES - 16 * 1024 * 1024

D_MODEL = 1024
HEAD_DIM = 64
GRID_W = 64
NA_HEADS = 8
NA_WIN_ROWS = 8
NA_WIN_COLS = 16
NA_WIDTH = NA_HEADS * HEAD_DIM
DIL_GROUPS = ((128, 1), (512, 4), (2048, 16))
DIL_HPG = 4
DIL_HEADS = DIL_HPG * len(DIL_GROUPS)
DIL_WIDTH = DIL_HEADS * HEAD_DIM
DIL_OUT_WIDTH = DIL_HPG * HEAD_DIM
DIL_RADIUS = 64
QKV_WIDTH = 3 * NA_WIDTH + 3 * DIL_WIDTH
D_FF = 4 * D_MODEL
PLE_DIM = 256
ROPE_THETA = 10000.0
RMS_EPS = 1e-6
NEG_INF = -1e30
Q_SCALE = HEAD_DIM ** -0.5

ADAM_LR = 0.001
ADAM_B1 = 0.9
ADAM_B2 = 0.999
ADAM_EPS = 1e-08
ADAM_WD = 0.01
ADAM_STEP = 10

N_CHIPS = 4
N_DEV = 8
PACK_W = 1024
GATHER_MIX = ("w_branch_na", "w_branch_dil", "w_out")
GATHER_MLP = ("w_up", "w_down", "w_ple_gate", "w_ple_proj")
REDUCE_EARLY = ("w_up", "w_down", "w_ple_gate", "w_out", "w_ple_proj", "w_branch_na", "w_branch_dil")
SMALL_ROWS = 16


def _cparams(sem=None):
    return pltpu.CompilerParams(dimension_semantics=sem, vmem_limit_bytes=VMEM_LIMIT)


def _hbm(*arrays):
    return [pltpu.with_memory_space_constraint(t, pltpu.HBM) for t in arrays]


def _mm(name, a, b, mode, tm, tn, tk, out_dtypes, epilogue=None, extras=(), consts=(), sums=(), after=(), into=None,
        b_view=None):
    if mode == "nn":
        (M, K), N = a.shape, b.shape[1]
    elif mode == "nt":
        (M, K), N = a.shape, b.shape[0]
    else:
        (K, M), N = a.shape, b.shape[1]
    if b_view is not None:
        N = b_view[0]
    tm, tn, tk = min(tm, M), min(tn, N), min(tk, K)
    assert M % tm == 0 and N % tn == 0 and K % tk == 0, (name, M, N, K, tm, tn, tk)
    if mode == "nn":
        a_spec = pl.BlockSpec((tm, tk), lambda i, j, k: (i, k))
        b_spec = pl.BlockSpec((tk, tn), lambda i, j, k: (k, j))
        dims = (((1,), (0,)), ((), ()))
    elif mode == "nt":
        a_spec = pl.BlockSpec((tm, tk), lambda i, j, k: (i, k))
        b_spec = pl.BlockSpec((tn, tk), lambda i, j, k: (j, k))
        dims = (((1,), (1,)), ((), ()))
    else:
        a_spec = pl.BlockSpec((tk, tm), lambda i, j, k: (k, i))
        b_spec = pl.BlockSpec((tk, tn), lambda i, j, k: (k, j))
        dims = (((0,), (0,)), ((), ()))
    if b_view is not None:
        b_spec = pl.BlockSpec(b_view[1], lambda i, j, k: b_view[2](j, k))
    nk = K // tk
    n_extra, n_const, n_out, n_sum = len(extras), len(consts), len(out_dtypes), len(sums)
    tile = pl.BlockSpec((tm, tn), lambda i, j, k: (i, j))
    assert not sums or tn == N, "row sums need whole rows in a tile"
    wide = [e for e in (*extras, *out_dtypes) if isinstance(e, tuple)]
    assert not wide or tn == N
    extra_specs = [pl.BlockSpec((tm, tn), functools.partial(lambda c, i, j, k: (i, c), e[1])) if isinstance(e, tuple) else tile
                   for e in extras]
    extras = [e[0] if isinstance(e, tuple) else e for e in extras]
    out_widths = [d[1] if isinstance(d, tuple) else N for d in out_dtypes]
    out_dtypes = [d[0] if isinstance(d, tuple) else d for d in out_dtypes]

    n_after = len(after)

    def body(a_ref, b_ref, *rest):
        extra_refs, rest = rest[:n_extra + n_const], rest[n_extra + n_const + n_after:]
        out_refs, sum_refs, acc = rest[:n_out], rest[n_out:n_out + n_sum], rest[-1]
        i, k = pl.program_id(0), pl.program_id(2)
        def product():
            return lax.dot_general(a_ref[...].astype(BF), b_ref[...].astype(BF), dims, preferred_element_type=F32)

        if nk > 1:
            @pl.when(k == 0)
            def _():
                acc[...] = jnp.zeros_like(acc)

            acc[...] += product()

        @pl.when(k == nk - 1)
        def _():
            total = product() if nk == 1 else acc[...]
            outs = (total,) if epilogue is None else epilogue(total, *[e[...] for e in extra_refs])
            for o_ref, val in zip(out_refs, outs[:n_out], strict=True):
                o_ref[...] = val.astype(o_ref.dtype).reshape(o_ref.shape)
            for s_ref, val in zip(sum_refs, outs[n_out:], strict=True):
                @pl.when(i == 0)
                def _():
                    s_ref[...] = val

                @pl.when(i != 0)
                def _():
                    s_ref[...] += val

    out_specs = ([tile if w == N else pl.BlockSpec((tm, w), lambda i, j, k: (i, 0)) for w in out_widths]
                 + [pl.BlockSpec((1, c), lambda i, j, k: (0, 0)) for c in sums])
    out_shape = ([jax.ShapeDtypeStruct((M, w), dt) for dt, w in zip(out_dtypes, out_widths, strict=True)]
                 + [jax.ShapeDtypeStruct((1, c), F32) for c in sums])
    operands, aliases = [a, b, *extras, *consts, *after], {}
    in_specs = ([a_spec, b_spec] + extra_specs
                + [pl.BlockSpec(c.shape, functools.partial(lambda nd, i, j, k: (0,) * nd, c.ndim)) for c in consts] + [ANY] * n_after)
    if into is not None:
        assert n_out == 1
        target, block, index = into
        out_specs = [pl.BlockSpec(block, lambda i, j, k: index(i, j))]
        out_shape = [jax.ShapeDtypeStruct(target.shape, target.dtype)]
        if not isinstance(target, jax.ShapeDtypeStruct):
            aliases = {len(operands): 0}
            operands.append(target)
            in_specs.append(ANY)
            n_after += 1

    outs = pl.pallas_call(
        body, name=name, grid=(M // tm, N // tn, nk),
        in_specs=in_specs, out_specs=out_specs, out_shape=out_shape,
        scratch_shapes=[pltpu.VMEM((tm, tn) if nk > 1 else (8, 128), F32)], input_output_aliases=aliases,
        compiler_params=_cparams(("arbitrary",) * 3 if sums else ("parallel", "parallel", "arbitrary")),
    )(*_hbm(*operands))
    return outs[0] if len(outs) == 1 else outs


def _row(arr, tm, col_block=None, width=None):
    width = arr.shape[1] if width is None else width
    cb = 0 if col_block is None else col_block
    return arr, pl.BlockSpec((tm, width), lambda i: (i, cb))


def _full(arr):
    nd = arr.ndim
    return arr, pl.BlockSpec(arr.shape, lambda i: (0,) * nd)


def _rowwise(name, body, T, tm, ins, outs, sums=(), after=()):
    n_in, n_out, n_sum, n_after = len(ins), len(outs), len(sums), len(after)

    def kern(*refs):
        in_refs, refs = refs[:n_in], refs[n_in + n_after:]
        out_refs, sum_refs = refs[:n_out], refs[n_out:]
        res = body(*[r[...] for r in in_refs])
        res = res if isinstance(res, tuple) else (res,)
        for o_ref, val in zip(out_refs, res[:n_out], strict=True):
            o_ref[...] = val.astype(o_ref.dtype)
        if n_sum:
            @pl.when(pl.program_id(0) == 0)
            def _():
                for s_ref in sum_refs:
                    s_ref[...] = jnp.zeros_like(s_ref)

            for s_ref, val in zip(sum_refs, res[n_out:], strict=True):
                s_ref[...] += val

    res = pl.pallas_call(
        kern, name=name, grid=(T // tm,),
        in_specs=[spec for _, spec in ins] + [ANY] * n_after,
        out_specs=[pl.BlockSpec((tm, c), lambda i: (i, 0)) for c, _ in outs]
        + [pl.BlockSpec((1, c), lambda i: (0, 0)) for c in sums],
        out_shape=[jax.ShapeDtypeStruct((T, c), dt) for c, dt in outs]
        + [jax.ShapeDtypeStruct((1, c), F32) for c in sums],
        compiler_params=_cparams(("arbitrary",)),
    )(*_hbm(*[a for a, _ in ins], *after))
    return res[0] if len(res) == 1 else res


def _sigmoid(x):
    return 1.0 / (1.0 + jnp.exp(-x))


def _rms(h):
    return lax.rsqrt(jnp.mean(h * h, axis=-1, keepdims=True) + RMS_EPS)


def _rms_bwd(dy, h, g):
    r = _rms(h)
    n = h * r
    dn = dy * g
    dh = r * (dn - n * jnp.mean(dn * n, axis=-1, keepdims=True))
    return dh, jnp.sum(dy * n, axis=0, keepdims=True)


def _rope(x, cos2, sin_signed):
    lane = lax.broadcasted_iota(jnp.int32, x.shape, 1)
    swapped = jnp.where((lane % HEAD_DIM) < HEAD_DIM // 2, pltpu.roll(x, 128 - HEAD_DIM // 2, 1), pltpu.roll(x, HEAD_DIM // 2, 1))
    return x * cos2 + swapped * sin_signed


NA_KEYS = NA_WIN_ROWS * GRID_W
NA_BASES = 8


def _na_row_geometry(r, rows):
    first = jnp.clip(r - NA_WIN_ROWS // 2, 0, rows - NA_WIN_ROWS)
    base = first - r + (NA_WIN_ROWS - 1)
    return pl.multiple_of(first * GRID_W, GRID_W), base


NA_ROWS_PER_STEP = 16
NA_BWD_ROWS_PER_STEP = 8


def _softmax_rows(s):
    p = jnp.exp(s - jnp.max(s, axis=-1, keepdims=True))
    return p / jnp.sum(p, axis=-1, keepdims=True)


def _split_pair(t):
    first = lax.broadcasted_iota(jnp.int32, t.shape, 1) < HEAD_DIM
    zero = jnp.zeros_like(t)
    return jnp.where(first, t, zero), jnp.where(first, zero, t)


def _join_pair(a, b):
    return jnp.where(lax.broadcasted_iota(jnp.int32, a.shape, 1) < HEAD_DIM, a, b)


_NT = (((1,), (1,)), ((), ()))
_TN = (((0,), (0,)), ((), ()))


def _na_fwd(qkv, tab):
    T = qkv.shape[0]
    rows = T // GRID_W
    n_pairs = NA_WIDTH // 128

    def body(q_ref, k_ref, v_ref, tab_ref, y_ref):
        def step(it, carry):
            geo = [_na_row_geometry(it * NA_ROWS_PER_STEP + u, rows) for u in range(NA_ROWS_PER_STEP)]
            q0s = [pl.multiple_of((it * NA_ROWS_PER_STEP + u) * GRID_W, GRID_W) for u in range(NA_ROWS_PER_STEP)]
            ss = [lax.dot_general(jnp.concatenate(_split_pair(q_ref[pl.ds(q0, GRID_W), :] * Q_SCALE), axis=0),
                                  k_ref[pl.ds(k0, NA_KEYS), :], _NT, preferred_element_type=F32)
                  for q0, (k0, _) in zip(q0s, geo)]
            ps = [_softmax_rows(s + jnp.concatenate([tab_ref[0, base], tab_ref[1, base]], axis=0)) for s, (_, base) in zip(ss, geo)]
            ys = [jnp.dot(p.astype(BF), v_ref[pl.ds(k0, NA_KEYS), :], preferred_element_type=F32) for p, (k0, _) in zip(ps, geo)]
            for q0, y2 in zip(q0s, ys):
                y_ref[pl.ds(q0, GRID_W), :] = _join_pair(y2[:GRID_W], y2[GRID_W:]).astype(y_ref.dtype)
            return carry

        lax.fori_loop(0, rows // NA_ROWS_PER_STEP, step, 0)

    def cols(first):
        return pl.BlockSpec((T, 128), lambda j: (0, first + j))

    return pl.pallas_call(
        body, name="na_fwd", grid=(n_pairs,),
        in_specs=[cols(0), cols(n_pairs), cols(2 * n_pairs), pl.BlockSpec((2, NA_BASES, GRID_W, NA_KEYS), lambda j: (j, 0, 0, 0))],
        out_specs=cols(0), out_shape=jax.ShapeDtypeStruct((T, NA_WIDTH), BF),
        compiler_params=_cparams(("parallel",)),
    )(*_hbm(qkv, qkv, qkv, tab))


def _na_bwd(qkv, tab, do):
    T = qkv.shape[0]
    rows = T // GRID_W
    n_pairs = NA_WIDTH // 128

    def body(q_ref, k_ref, v_ref, tab_ref, do_ref, dq_ref, dk_out, dv_out, dtab_ref, dk_ref, dv_ref):
        dk_ref[...] = jnp.zeros_like(dk_ref)
        dv_ref[...] = jnp.zeros_like(dv_ref)
        dtab_ref[...] = jnp.zeros_like(dtab_ref)

        def step(it, carry):
            U = NA_BWD_ROWS_PER_STEP
            geo = [_na_row_geometry(it * U + u, rows) for u in range(U)]
            q0s = [pl.multiple_of((it * U + u) * GRID_W, GRID_W) for u in range(U)]
            q2s = [jnp.concatenate(_split_pair(q_ref[pl.ds(q0, GRID_W), :] * Q_SCALE), axis=0) for q0 in q0s]
            do2s = [jnp.concatenate(_split_pair(do_ref[pl.ds(q0, GRID_W), :]), axis=0) for q0 in q0s]
            ss = [lax.dot_general(q2, k_ref[pl.ds(k0, NA_KEYS), :], _NT, preferred_element_type=F32) for q2, (k0, _) in zip(q2s, geo)]
            dps = [lax.dot_general(do2, v_ref[pl.ds(k0, NA_KEYS), :], _NT, preferred_element_type=F32) for do2, (k0, _) in zip(do2s, geo)]
            ps = [_softmax_rows(s + jnp.concatenate([tab_ref[0, base], tab_ref[1, base]], axis=0)) for s, (_, base) in zip(ss, geo)]
            dss = [p * (dp - jnp.sum(dp * p, axis=-1, keepdims=True)) for p, dp in zip(ps, dps)]
            dvs = [lax.dot_general(p.astype(BF), do2, _TN, preferred_element_type=F32) for p, do2 in zip(ps, do2s)]
            dsbs = [ds.astype(BF) for ds in dss]
            dqs = [jnp.dot(dsb, k_ref[pl.ds(k0, NA_KEYS), :], preferred_element_type=F32) for dsb, (k0, _) in zip(dsbs, geo)]
            dks = [lax.dot_general(dsb, q2, _TN, preferred_element_type=F32) for dsb, q2 in zip(dsbs, q2s)]
            for u in range(U):
                k0, base = geo[u]
                dtab_ref[0, base] += dss[u][:GRID_W]
                dtab_ref[1, base] += dss[u][GRID_W:]
                dq_ref[pl.ds(q0s[u], GRID_W), :] = (_join_pair(dqs[u][:GRID_W], dqs[u][GRID_W:]) * Q_SCALE).astype(dq_ref.dtype)
                dk_ref[pl.ds(k0, NA_KEYS), :] += dks[u]
                dv_ref[pl.ds(k0, NA_KEYS), :] += dvs[u]
            return carry

        lax.fori_loop(0, rows // NA_BWD_ROWS_PER_STEP, step, 0)
        dk_out[...] = dk_ref[...].astype(dk_out.dtype)
        dv_out[...] = dv_ref[...].astype(dv_out.dtype)

    def cols(first):
        return pl.BlockSpec((T, 128), lambda j: (0, first + j))

    tabs = pl.BlockSpec((2, NA_BASES, GRID_W, NA_KEYS), lambda j: (j, 0, 0, 0))
    wide = jax.ShapeDtypeStruct((T, NA_WIDTH), BF)
    return pl.pallas_call(
        body, name="na_bwd", grid=(n_pairs,),
        in_specs=[cols(0), cols(n_pairs), cols(2 * n_pairs), tabs, cols(0)],
        out_specs=[cols(0), cols(0), cols(0), tabs],
        out_shape=[wide, wide, wide, jax.ShapeDtypeStruct((NA_HEADS, NA_BASES, GRID_W, NA_KEYS), F32)],
        scratch_shapes=[pltpu.VMEM((T, 128), F32), pltpu.VMEM((T, 128), F32)],
        compiler_params=_cparams(("parallel",)),
    )(*_hbm(qkv, qkv, qkv, tab, do))


def _na_bias_table(rpb):
    H, n_rows, n_cols = rpb.shape

    def body(r_ref, tab_ref):
        q = lax.broadcasted_iota(jnp.int32, (GRID_W, 128), 0)
        kc = lax.broadcasted_iota(jnp.int32, (GRID_W, 128), 1)
        first = jnp.clip(q - NA_WIN_COLS // 2, 0, GRID_W - NA_WIN_COLS)
        valid = (kc >= first) & (kc < first + NA_WIN_COLS)
        toeplitz = []
        for ro in range(n_rows):
            row = jnp.broadcast_to(r_ref[pl.ds(ro, 1), :], (GRID_W, 128))
            shifted = pltpu.roll(pltpu.roll(row, 128 - (NA_WIN_COLS - 1), 1), 0, 1, stride=1, stride_axis=0)
            toeplitz.append(jnp.where(valid, shifted, NEG_INF))
        for base in range(NA_BASES):
            for j in range(NA_WIN_ROWS // 2):
                even, odd = toeplitz[base + 2 * j], toeplitz[base + 2 * j + 1]
                tab_ref[base, :, pl.ds(j * 128, 128)] = jnp.where(kc < GRID_W, even, pltpu.roll(odd, GRID_W, 1))

    padded = jnp.pad(rpb, ((0, 0), (0, 16 - n_rows), (0, 128 - n_cols)))
    return pl.pallas_call(
        body, name="na_bias_table", grid=(H,),
        in_specs=[pl.BlockSpec((None, 16, 128), lambda h: (h, 0, 0))],
        out_specs=pl.BlockSpec((None, NA_BASES, GRID_W, NA_KEYS), lambda h: (h, 0, 0, 0)),
        out_shape=jax.ShapeDtypeStruct((H, NA_BASES, GRID_W, NA_KEYS), F32),
        compiler_params=_cparams(("parallel",)),
    )(padded)


def _na_rpb_grad(dtab, after=()):
    H = dtab.shape[0]
    n_rows = 2 * NA_WIN_ROWS - 1
    n_cols = 2 * NA_WIN_COLS - 1

    def body(d_ref, *rest):
        o_ref = rest[-1]
        lane = lax.broadcasted_iota(jnp.int32, (GRID_W, 128), 1)
        low = lane < GRID_W
        flip = (lax.broadcasted_iota(jnp.int32, (GRID_W, GRID_W), 0) + lax.broadcasted_iota(jnp.int32, (GRID_W, GRID_W), 1)
                == GRID_W - 1).astype(BF)

        def reverse_rows(t):
            out = jnp.zeros_like(t)
            for _ in range(3):
                piece = t.astype(BF)
                out = out + jnp.dot(flip, piece, preferred_element_type=F32)
                t = t - piece.astype(F32)
            return out

        out_rows = []
        for ro in range(n_rows):
            acc = jnp.zeros((GRID_W, 128), F32)
            for base in range(NA_BASES):
                i = ro - base
                if not 0 <= i < NA_WIN_ROWS:
                    continue
                pair = d_ref[base, :, pl.ds((i // 2) * 128, 128)]
                if i % 2:
                    pair = pltpu.roll(pair, GRID_W, 1)
                acc = acc + jnp.where(low, pair, 0.0)
            skew = pltpu.roll(reverse_rows(acc), 0, 1, stride=1, stride_axis=0)
            diag = jnp.sum(skew, axis=0, keepdims=True)
            out_rows.append(pltpu.roll(jnp.broadcast_to(diag, (8, 128)), 128 - (GRID_W - NA_WIN_COLS), 1)[:1])
        out_rows.append(jnp.zeros((1, 128), F32))
        res = jnp.concatenate(out_rows, axis=0)
        o_ref[...] = jnp.where(lax.broadcasted_iota(jnp.int32, res.shape, 1) < n_cols, res, 0.0)

    return pl.pallas_call(
        body, name="na_rpb_grad", grid=(H,),
        in_specs=[pl.BlockSpec((None, NA_BASES, GRID_W, NA_KEYS), lambda h: (h, 0, 0, 0))] + [ANY] * len(after),
        out_specs=pl.BlockSpec((None, n_rows + 1, 128), lambda h: (h, 0, 0)),
        out_shape=jax.ShapeDtypeStruct((H, n_rows + 1, 128), F32),
        compiler_params=_cparams(("parallel",)),
    )(dtab, *after)


BAND_Q = 128
BAND_KEYS = BAND_Q + 2 * DIL_RADIUS


def _band_geometry(n, L):
    q0 = pl.multiple_of(n * BAND_Q, BAND_Q)
    k0 = pl.multiple_of(jnp.clip(q0 - DIL_RADIUS, 0, L - BAND_KEYS), DIL_RADIUS)
    qi = q0 + lax.broadcasted_iota(jnp.int32, (BAND_Q, BAND_KEYS), 0)
    kj = k0 + lax.broadcasted_iota(jnp.int32, (BAND_Q, BAND_KEYS), 1)
    return q0, k0, jnp.abs(qi - kj) <= DIL_RADIUS


DIL_PAIRS = DIL_OUT_WIDTH // 128


def _residue_shape(dil, T, dtype):
    return jax.ShapeDtypeStruct((DIL_PAIRS, dil, T // dil, 128), dtype)


def _residue_tile(dil, tm):
    return pl.BlockSpec((DIL_PAIRS, dil, tm // dil, 128), lambda i: (0, 0, i, 0))


def _to_natural(ref, scratch, dil, tm):
    tiles = []
    for pair in range(DIL_PAIRS):
        if dil == 1:
            tiles.append(ref[pair, 0].astype(F32))
            continue
        for r in range(dil):
            scratch[pl.ds(r, tm // dil, stride=dil), :] = ref[pair, r].astype(F32)
        tiles.append(scratch[...])
    return tiles


def _from_natural(tile, scratch, ref, pair, dil, tm):
    if dil == 1:
        ref[pair, 0] = tile.astype(ref.dtype)
        return
    scratch[...] = tile
    for r in range(dil):
        ref[pair, r] = scratch[pl.ds(r, tm // dil, stride=dil), :].astype(ref.dtype)


def _band_specs(group, T):
    dil = DIL_GROUPS[group][1]
    L = T // dil
    assert L % BAND_Q == 0 and L >= BAND_KEYS, (T, dil)
    per_residue = min(BAND_BLOCKS_PER_STEP, L // BAND_Q)
    residues = min(dil, BAND_BLOCKS_PER_STEP // per_residue)
    spec = pl.BlockSpec((None, residues, L, 128), lambda s: (s % DIL_PAIRS, s // DIL_PAIRS, 0, 0))
    return L, residues, per_residue, (dil // residues * DIL_PAIRS,), spec


BAND_BLOCKS_PER_STEP = 8


def _band_softmax(s, valid):
    s = jnp.where(valid, s, NEG_INF)
    m = jnp.max(s, axis=-1, keepdims=True)
    p = jnp.exp(s - m)
    l = jnp.sum(p, axis=-1, keepdims=True)
    return p / l, m + jnp.log(l)


def _band_fwd(q, k, v, group):
    T = q.shape[1] * q.shape[2]
    L, residues, U, grid, spec = _band_specs(group, T)

    def body(q_ref, k_ref, v_ref, o_ref, lse_ref):
        def step(it, carry):
            geo = [(r, *_band_geometry(it * U + u, L)) for r in range(residues) for u in range(U)]
            ss = [lax.dot_general(jnp.concatenate(_split_pair(q_ref[r, pl.ds(q0, BAND_Q), :]), axis=0),
                                  k_ref[r, pl.ds(k0, BAND_KEYS), :], _NT, preferred_element_type=F32) for r, q0, k0, _ in geo]
            pls = [_band_softmax(s, jnp.concatenate([valid, valid], axis=0)) for s, (_, _, _, valid) in zip(ss, geo)]
            os = [jnp.dot(p.astype(BF), v_ref[r, pl.ds(k0, BAND_KEYS), :], preferred_element_type=F32)
                  for (p, _), (r, _, k0, _) in zip(pls, geo)]
            for (r, q0, _, _), o2, (_, lse) in zip(geo, os, pls):
                o_ref[r, pl.ds(q0, BAND_Q), :] = _join_pair(o2[:BAND_Q], o2[BAND_Q:])
                lse2 = jnp.broadcast_to(lse, (2 * BAND_Q, 128))
                lse_ref[r, pl.ds(q0, BAND_Q), :] = _join_pair(lse2[:BAND_Q], lse2[BAND_Q:])
            return carry

        lax.fori_loop(0, L // (BAND_Q * U), step, 0)

    res = _residue_shape(DIL_GROUPS[group][1], T, F32)
    return pl.pallas_call(
        body, name=f"band_fwd_g{group}", grid=grid,
        in_specs=[spec] * 3, out_specs=[spec] * 2, out_shape=[res, res],
        compiler_params=_cparams(("parallel",)),
    )(*_hbm(q, k, v))


def _band_bwd(q, k, v, do, dlse, group):
    T = q.shape[1] * q.shape[2]
    L, residues, U, grid, spec = _band_specs(group, T)

    def body(q_ref, k_ref, v_ref, do_ref, dlse_ref, dq_ref, dk_ref, dv_ref):
        dk_ref[...] = jnp.zeros_like(dk_ref)
        dv_ref[...] = jnp.zeros_like(dv_ref)

        def step(it, carry):
            geo = [(r, *_band_geometry(it * U + u, L)) for r in range(residues) for u in range(U)]
            q2s = [jnp.concatenate(_split_pair(q_ref[r, pl.ds(q0, BAND_Q), :]), axis=0) for r, q0, _, _ in geo]
            do2s = [jnp.concatenate(_split_pair(do_ref[r, pl.ds(q0, BAND_Q), :]), axis=0) for r, q0, _, _ in geo]
            ss = [lax.dot_general(q2, k_ref[r, pl.ds(k0, BAND_KEYS), :], _NT, preferred_element_type=F32)
                  for q2, (r, _, k0, _) in zip(q2s, geo)]
            dps = [lax.dot_general(do2, v_ref[r, pl.ds(k0, BAND_KEYS), :], _NT, preferred_element_type=F32)
                   for do2, (r, _, k0, _) in zip(do2s, geo)]
            ps = [_band_softmax(s, jnp.concatenate([valid, valid], axis=0))[0] for s, (_, _, _, valid) in zip(ss, geo)]
            dss = []
            for p, dp, (r, q0, _, _) in zip(ps, dps, geo):
                dl = dlse_ref[r, pl.ds(q0, BAND_Q), :]
                dl2 = jnp.concatenate([dl[:, :1], dl[:, HEAD_DIM:HEAD_DIM + 1]], axis=0)
                dss.append(p * (dp - jnp.sum(dp * p, axis=-1, keepdims=True) + dl2))
            dvs = [lax.dot_general(p.astype(BF), do2, _TN, preferred_element_type=F32) for p, do2 in zip(ps, do2s)]
            dsbs = [ds.astype(BF) for ds in dss]
            dqs = [jnp.dot(dsb, k_ref[r, pl.ds(k0, BAND_KEYS), :], preferred_element_type=F32) for dsb, (r, _, k0, _) in zip(dsbs, geo)]
            dks = [lax.dot_general(dsb, q2, _TN, preferred_element_type=F32) for dsb, q2 in zip(dsbs, q2s)]
            for u, (r, q0, k0, _) in enumerate(geo):
                dq_ref[r, pl.ds(q0, BAND_Q), :] = _join_pair(dqs[u][:BAND_Q], dqs[u][BAND_Q:])
                dk_ref[r, pl.ds(k0, BAND_KEYS), :] += dks[u]
                dv_ref[r, pl.ds(k0, BAND_KEYS), :] += dvs[u]
            return carry

        lax.fori_loop(0, L // (BAND_Q * U), step, 0)

    res = _residue_shape(DIL_GROUPS[group][1], T, F32)
    return pl.pallas_call(
        body, name=f"band_bwd_g{group}", grid=grid,
        in_specs=[spec] * 5, out_specs=[spec] * 3, out_shape=[res] * 3,
        compiler_params=_cparams(("parallel",)),
    )(*_hbm(q, k, v, do, dlse))


def _head_sums(t):
    head = lax.broadcasted_iota(jnp.int32, t.shape, 1) // HEAD_DIM
    out = jnp.zeros_like(t)
    for h in range(t.shape[1] // HEAD_DIM):
        mine = head == h
        out = jnp.where(mine, jnp.sum(jnp.where(mine, t, 0.0), axis=-1, keepdims=True), out)
    return out


def _dil_merge_fwd(os, lses, T, tm):
    G = len(DIL_GROUPS)
    W = DIL_OUT_WIDTH
    dils = [d for _, d in DIL_GROUPS]

    def body(*refs):
        o_refs, lse_refs = refs[:G], refs[G:2 * G]
        y_ref, w_refs, on_refs, scratch = refs[2 * G], refs[2 * G + 1:3 * G + 1], refs[3 * G + 1:4 * G + 1], refs[-1]
        o = [jnp.concatenate(_to_natural(r, scratch, d, tm), axis=1) for r, d in zip(o_refs, dils)]
        ls = [jnp.concatenate(_to_natural(r, scratch, d, tm), axis=1) for r, d in zip(lse_refs, dils)]
        m = functools.reduce(jnp.maximum, ls)
        es = [jnp.exp(l - m) for l in ls]
        tot = functools.reduce(jnp.add, es)
        ws = [e / tot for e in es]
        y_ref[...] = functools.reduce(jnp.add, [w * t for w, t in zip(ws, o)]).astype(y_ref.dtype)
        for g in range(G):
            w_refs[g][...] = ws[g]
            on_refs[g][...] = o[g]

    nat = pl.BlockSpec((tm, W), lambda i: (i, 0))
    res = pl.pallas_call(
        body, name="dil_merge_fwd", grid=(T // tm,),
        in_specs=[_residue_tile(d, tm) for d in dils] * 2,
        out_specs=[nat] * (2 * G + 1),
        out_shape=[jax.ShapeDtypeStruct((T, W), BF)] + [jax.ShapeDtypeStruct((T, W), F32)] * (2 * G),
        scratch_shapes=[pltpu.VMEM((tm, 128), F32)],
        compiler_params=_cparams(("parallel",)),
    )(*_hbm(*os, *lses))
    return res[0], res[1:G + 1], res[G + 1:]


def _dil_merge_bwd(dy, os, ws, tm, after=()):
    G = len(DIL_GROUPS)
    T, W = dy.shape
    dils = [d for _, d in DIL_GROUPS]
    n_after = len(after)

    def body(*refs):
        dyt = refs[0][...]
        o, w = [r[...] for r in refs[1:G + 1]], [r[...] for r in refs[G + 1:2 * G + 1]]
        refs = refs[2 * G + 1 + n_after:]
        do_refs, dlse_refs, scratch = refs[:G], refs[G:2 * G], refs[-1]
        dws = [_head_sums(dyt * t) for t in o]
        mean = functools.reduce(jnp.add, [a * b for a, b in zip(w, dws)])
        for g, d in enumerate(dils):
            do, dlse = w[g] * dyt, w[g] * (dws[g] - mean)
            for pair in range(DIL_PAIRS):
                cols = slice(pair * 128, (pair + 1) * 128)
                _from_natural(do[:, cols], scratch, do_refs[g], pair, d, tm)
                _from_natural(dlse[:, cols], scratch, dlse_refs[g], pair, d, tm)

    nat = pl.BlockSpec((tm, W), lambda i: (i, 0))
    res = pl.pallas_call(
        body, name="dil_merge_bwd", grid=(T // tm,),
        in_specs=[nat] * (2 * G + 1) + [ANY] * n_after,
        out_specs=[_residue_tile(d, tm) for d in dils] * 2,
        out_shape=[_residue_shape(d, T, BF) for d in dils] + [_residue_shape(d, T, F32) for d in dils],
        scratch_shapes=[pltpu.VMEM((tm, 128), F32)],
        compiler_params=_cparams(("parallel",)),
    )(*_hbm(dy, *os, *ws, *after))
    return res[:G], res[G:]


def _qkv_prep(z, cos2, sin_signed, tm):
    T = z.shape[0]
    G = len(DIL_GROUPS)
    dils = [d for _, d in DIL_GROUPS]
    n_dil_blocks = 3 * DIL_WIDTH // 128

    def body(*refs):
        blocks = refs[:n_dil_blocks]
        cos_ref, sin_ref = refs[n_dil_blocks], refs[1 + n_dil_blocks]
        outs = refs[2 + n_dil_blocks:]
        for part in range(3):
            for g, d in enumerate(dils):
                out = outs[g * 3 + part]
                for pair in range(DIL_PAIRS):
                    blk = blocks[part * (DIL_WIDTH // 128) + g * DIL_PAIRS + pair]
                    for r in range(d):
                        rows = pl.ds(r, tm // d, stride=d) if d > 1 else slice(None)
                        x = blk[rows, :]
                        if part < 2:
                            x = _rope(x, cos_ref[rows, :], sin_ref[rows, :])
                        if part == 0:
                            x = x * Q_SCALE
                        out[pair, r] = x.astype(out.dtype)

    lane_block = [pl.BlockSpec((tm, 128), functools.partial(lambda b, i: (i, b), b)) for b in range(n_dil_blocks)]
    tab = pl.BlockSpec((tm, 128), lambda i: (i, 0))
    res = pl.pallas_call(
        body, name="qkv_prep", grid=(T // tm,),
        in_specs=lane_block + [tab, tab],
        out_specs=[_residue_tile(d, tm) for d in dils for _ in range(3)],
        out_shape=[_residue_shape(d, T, BF) for d in dils for _ in range(3)],
        compiler_params=_cparams(("parallel",)),
    )(*_hbm(*[z] * n_dil_blocks, cos2, sin_signed))
    return [res[3 * g:3 + 3 * g] for g in range(G)]


def _qkv_unprep(d_na, d_dil, cos2, sin_signed, tm, after=()):
    T = d_na[0].shape[0]
    G = len(DIL_GROUPS)
    dils = [d for _, d in DIL_GROUPS]
    n_after = len(after)

    def body(*refs):
        dq, dk, dv = (r[...] for r in refs[:3])
        res_refs = refs[3:3 + 3 * G]
        cs, sn = refs[3 + 3 * G][...], refs[4 + 3 * G][...]
        out, scratch = refs[5 + 3 * G + n_after], refs[-1]
        cols = [dq, dk, dv]
        for part in range(3):
            for g, d in enumerate(dils):
                for x in _to_natural(res_refs[g * 3 + part], scratch, d, tm):
                    if part < 2:
                        x = _rope(x, cs, -sn)
                    cols.append((x * Q_SCALE if part == 0 else x).astype(out.dtype))
        out[...] = jnp.concatenate(cols, axis=1)

    wide = pl.BlockSpec((tm, NA_WIDTH), lambda i: (i, 0))
    tab = pl.BlockSpec((tm, 128), lambda i: (i, 0))
    return pl.pallas_call(
        body, name="qkv_unprep", grid=(T // tm,),
        in_specs=[wide] * 3 + [_residue_tile(d, tm) for d in dils for _ in range(3)] + [tab, tab] + [ANY] * n_after,
        out_specs=pl.BlockSpec((tm, QKV_WIDTH), lambda i: (i, 0)),
        out_shape=jax.ShapeDtypeStruct((T, QKV_WIDTH), BF),
        scratch_shapes=[pltpu.VMEM((tm, 128), F32)],
        compiler_params=_cparams(("parallel",)),
    )(*_hbm(*d_na, *[t for g in range(G) for t in d_dil[g]], cos2, sin_signed, *after))


def _rope_tables(positions):
    half = HEAD_DIM // 2
    inv_freq = ROPE_THETA ** (-jnp.arange(half, dtype=F32) / half)
    ang = positions.astype(F32)[:, None] * inv_freq
    cos, sin = jnp.cos(ang), jnp.sin(ang)
    return jnp.tile(jnp.concatenate([cos, cos], axis=1), (1, 2)), jnp.tile(jnp.concatenate([-sin, sin], axis=1), (1, 2))


def _pack_rows(t):
    return t.reshape(-1, PACK_W)


def _me():
    return lax.axis_index("x"), lax.axis_index("y"), lax.axis_index("c")


def _other_chips(x, y):
    return [(1 - x, y), (x, 1 - y), (1 - x, 1 - y)]


def _pair_sum(g, got, tm):
    S, R, W = g.shape
    half = R // 2
    nb = half // tm

    def body(pos_ref, g_ref, got_ref, own_ref, ob_ref):
        tot = g_ref[...] + got_ref[...]
        ob_ref[...] = tot.astype(ob_ref.dtype)

        @pl.when(pl.program_id(1) == pos_ref[1])
        def _():
            own_ref[...] = tot

    tile = pl.BlockSpec((None, tm, W), lambda i, s, pos_ref: (s, i, 0))
    c, chip = lax.axis_index("c"), 2 * lax.axis_index("x") + lax.axis_index("y")
    return pl.pallas_call(
        body, name="pair_sum",
        grid_spec=pltpu.PrefetchScalarGridSpec(
            num_scalar_prefetch=1, grid=(nb, S),
            in_specs=[pl.BlockSpec((None, tm, W), lambda i, s, pos_ref: (s, pos_ref[0] * nb + i, 0)), tile],
            out_specs=[pl.BlockSpec((tm, W), lambda i, s, pos_ref: (i, 0)), tile]),
        out_shape=[jax.ShapeDtypeStruct((half, W), F32), jax.ShapeDtypeStruct((S, half, W), BF)],
        compiler_params=_cparams(("parallel", "arbitrary")),
    )(jnp.stack([c, chip]).astype(jnp.int32), *_hbm(g, got))


def _chip_sum(own, others, tm):
    n, h, W = others.shape
    nb = h // tm

    def body(c_ref, own_ref, p_ref, o_ref):
        o_ref[...] = ((own_ref[...] + p_ref[0].astype(F32)) + p_ref[1].astype(F32)) + p_ref[2].astype(F32)

    return pl.pallas_call(
        body, name="chip_sum",
        grid_spec=pltpu.PrefetchScalarGridSpec(
            num_scalar_prefetch=1, grid=(nb,),
            in_specs=[pl.BlockSpec((tm, W), lambda i, c_ref: (i, 0)), pl.BlockSpec((n, tm, W), lambda i, c_ref: (0, i, 0))],
            out_specs=pl.BlockSpec((tm, W), lambda i, c_ref: (c_ref[0] * nb + i, 0))),
        out_shape=jax.ShapeDtypeStruct((2 * h, W), F32),
        compiler_params=_cparams(("parallel",)),
    )(lax.axis_index("c").reshape(1).astype(jnp.int32), *_hbm(own, others))


def _join_halves(shard, after=()):
    h = shard.shape[0] // 2

    def body(in_ref, *rest):
        out_ref, send_sem, recv_sem = rest[len(after):]
        x, y, c = _me()
        cp = pltpu.make_async_remote_copy(
            src_ref=in_ref.at[pl.ds(c * h, h), :], dst_ref=out_ref.at[pl.ds(c * h, h), :],
            send_sem=send_sem, recv_sem=recv_sem, device_id=(x, y, 1 - c), device_id_type=MESH)
        cp.start()
        pltpu.make_async_remote_copy(
            src_ref=in_ref.at[pl.ds(c * h, h), :], dst_ref=out_ref.at[pl.ds((1 - c) * h, h), :],
            send_sem=send_sem, recv_sem=recv_sem, device_id=(x, y, 1 - c), device_id_type=MESH).wait_recv()
        cp.wait_send()

    return pl.pallas_call(
        body, name="join_halves", in_specs=[ANY] * (1 + len(after)), out_specs=ANY,
        out_shape=jax.ShapeDtypeStruct(shard.shape, shard.dtype), input_output_aliases={0: 0},
        scratch_shapes=[pltpu.SemaphoreType.DMA, pltpu.SemaphoreType.DMA],
    )(shard, *after)


def _allreduce_copies(src_ref, land_ref):
    x, y, c = _me()
    peers = [((x + fx) % 2, (y + fy) % 2, (c + fc) % 2) for fx in range(2) for fy in range(2) for fc in range(2)][1:]
    return [(src_ref, land_ref.at[4 * x + 2 * y + c], peer) for peer in peers]


def _allreduce_start(s, after):
    return _split_start("allreduce_small_start", s, (N_DEV, *s.shape), s.dtype, N_DEV - 1, _allreduce_copies, after)


def _allreduce_finish(flight, after):
    own, landed = _split_wait("allreduce_small_wait", flight, after, N_DEV - 1, _allreduce_copies)
    me = 4 * lax.axis_index("x") + 2 * lax.axis_index("y") + lax.axis_index("c")
    parts = lax.dynamic_update_slice(landed, own[None], (me, 0, 0))

    def body(p_ref, o_ref):
        total = p_ref[0]
        for d in range(1, N_DEV):
            total = total + p_ref[d]
        o_ref[...] = total

    return pl.pallas_call(
        body, name="allreduce_small_sum",
        in_specs=[pl.BlockSpec(memory_space=pltpu.VMEM)], out_specs=pl.BlockSpec(memory_space=pltpu.VMEM),
        out_shape=jax.ShapeDtypeStruct(own.shape, F32),
    )(parts)


HBM_SPEC = pl.BlockSpec(memory_space=pltpu.HBM)
SEM_SPEC = pl.BlockSpec(memory_space=pltpu.SEMAPHORE)
DATAFLOW = pltpu.SideEffectType.DATAFLOW_SIDE_EFFECTING


class _InFlight(NamedTuple):
    sems: tuple
    src: jax.Array
    land: jax.Array
    token: jax.Array


def _split_start(name, src, land_shape, land_dtype, n, copies, after=()):
    n_after = len(after)

    def body(src_ref, land_ref, *rest):
        rest = rest[n_after:]
        sems, token = rest[:2 * n], rest[-1]
        for k, (s, d, peer) in enumerate(copies(src_ref, land_ref)):
            pltpu.make_async_remote_copy(src_ref=s, dst_ref=d, send_sem=sems[k], recv_sem=sems[n + k],
                                         device_id=peer, device_id_type=MESH).start()
        token[...] = jnp.zeros_like(token)

    outs = pl.pallas_call(
        body, name=name,
        out_shape=(*[pltpu.SemaphoreType.DMA(())] * (2 * n), pltpu.HBM(src.shape, src.dtype), pltpu.HBM(land_shape, land_dtype),
                   jax.ShapeDtypeStruct((8, 128), F32)),
        in_specs=(HBM_SPEC, HBM_SPEC, *[ANY] * n_after),
        out_specs=(*[SEM_SPEC] * (2 * n), HBM_SPEC, HBM_SPEC, pl.BlockSpec(memory_space=pltpu.VMEM)),
        input_output_aliases={0: 2 * n, 1: 2 * n + 1},
        compiler_params=pltpu.CompilerParams(has_side_effects=DATAFLOW),
    )(pltpu.with_memory_space_constraint(src, pltpu.HBM), pltpu.with_memory_space_constraint(lax.empty(land_shape, land_dtype), pltpu.HBM),
      *after)
    return _InFlight(tuple(outs[:2 * n]), outs[2 * n], outs[2 * n + 1], outs[2 * n + 2])


def _split_wait(name, flight, after, n, copies):
    after = after if isinstance(after, tuple) else (after,)

    def body(src_ref, land_ref, *rest):
        sems = rest[:2 * n]
        for k, (s, d, peer) in enumerate(copies(src_ref, land_ref)):
            cp = pltpu.make_async_remote_copy(src_ref=s, dst_ref=d, send_sem=sems[k], recv_sem=sems[n + k],
                                              device_id=peer, device_id_type=MESH)
            cp.wait_send()
            cp.wait_recv()

    return pl.pallas_call(
        body, name=name,
        out_shape=(pltpu.HBM(flight.src.shape, flight.src.dtype), pltpu.HBM(flight.land.shape, flight.land.dtype)),
        in_specs=(HBM_SPEC, HBM_SPEC, *[SEM_SPEC] * (2 * n), *[ANY] * len(after)),
        out_specs=(HBM_SPEC, HBM_SPEC), input_output_aliases={0: 0, 1: 1},
        compiler_params=pltpu.CompilerParams(has_side_effects=DATAFLOW),
    )(flight.src, flight.land, *flight.sems, *after)


def _gather_copies(src_ref, land_ref):
    x, y, c = _me()
    return [(src_ref, land_ref.at[2 * x + y], (*chip, c)) for chip in _other_chips(x, y)]


def _gather_start(packed, tag, after=()):
    return _split_start(f"gather_start_{tag}", packed, (N_CHIPS, *packed.shape), packed.dtype, 3, _gather_copies, after)


def _gather_wait(flight, after, tag):
    src, others = _split_wait(f"gather_wait_{tag}", flight, after, 3, _gather_copies)
    return lax.dynamic_update_slice(others, src[None], (2 * lax.axis_index("x") + lax.axis_index("y"), 0, 0))


def _across_copies(src_ref, land_ref):
    x, y, c = _me()
    half = src_ref.shape[0] // 2
    rows = pl.ds(c * half, half)
    return [(src_ref.at[rows, :], land_ref.at[2 * x + y, rows, :], (*chip, c)) for chip in _other_chips(x, y)]


def _to_sibling_copies(all_ref, unused_ref):
    x, y, c = _me()
    half = all_ref.shape[1] // 2
    places = [all_ref.at[2 * chip[0] + chip[1], pl.ds(c * half, half), :] for chip in _other_chips(x, y)]
    return [(place, place, (x, y, 1 - c)) for place in places]


def _gather_halves_start(shard, tag, after=()):
    return _split_start(f"gather_{tag}_across_start", shard, (N_CHIPS, *shard.shape), shard.dtype, 3, _across_copies, after)


def _gather_halves_relay(flight, after, tag):
    shard, landed = _split_wait(f"gather_{tag}_across_wait", flight, after, 3, _across_copies)
    return shard, _split_start(f"gather_{tag}_sibling_start", landed, (8, 128), landed.dtype, 3, _to_sibling_copies)


def _gather_halves_finish(shard, relay, after, tag):
    others = _split_wait(f"gather_{tag}_sibling_wait", relay, after, 3, _to_sibling_copies)[0]
    return lax.dynamic_update_slice(others, shard[None], (2 * lax.axis_index("x") + lax.axis_index("y"), 0, 0))


def _assemble_w_in(shards, tm):
    S, R, C = shards.shape
    n_gates = 2 * D_MODEL

    def body(s_ref, w_ref, g_ref):
        full = jnp.concatenate([s_ref[s] for s in range(S)], axis=1)
        w_ref[...] = full
        g_ref[...] = full[:, S * C - n_gates:]

    return pl.pallas_call(
        body, name="assemble_w_in", grid=(R // tm,),
        in_specs=[pl.BlockSpec((S, tm, C), lambda i: (0, i, 0))],
        out_specs=[pl.BlockSpec((tm, S * C), lambda i: (i, 0)), pl.BlockSpec((tm, n_gates), lambda i: (i, 0))],
        out_shape=[jax.ShapeDtypeStruct((R, S * C), shards.dtype), jax.ShapeDtypeStruct((R, n_gates), shards.dtype)],
        compiler_params=_cparams(("parallel",)),
    )(shards)


def _swap_copies(src_ref, land_ref):
    x, y, c = _me()
    half = land_ref.shape[1]
    return [(src_ref.at[:, pl.ds((1 - c) * half, half), :], land_ref, (x, y, 1 - c))]


def _swap_start(g, tag):
    S, R, W = g.shape
    return _split_start(f"swap_halves_start_{tag}", g, (S, R // 2, W), g.dtype, 1, _swap_copies)


def _swap_wait(flight, after, tag):
    return _split_wait(f"swap_halves_wait_{tag}", flight, after, 1, _swap_copies)


def _scatter_copies(src_ref, land_ref):
    x, y, c = _me()
    return [(src_ref.at[2 * chip[0] + chip[1]], land_ref.at[j], (*chip, c)) for j, chip in enumerate(_other_chips(x, y))]


def _scatter_start(part, tag):
    S, h, W = part.shape
    return _split_start(f"scatter_chips_start_{tag}", part, (S - 1, h, W), part.dtype, 3, _scatter_copies)


def _scatter_wait(flight, after, tag):
    return _split_wait(f"scatter_chips_wait_{tag}", flight, after, 3, _scatter_copies)[1]


def _join_copies(shard_ref, unused_ref):
    x, y, c = _me()
    h = shard_ref.shape[0] // 2
    rows = shard_ref.at[pl.ds(c * h, h), :]
    return [(rows, rows, (x, y, 1 - c))]


def _join_start(shard):
    return _split_start("join_halves_start", shard, (8, 128), shard.dtype, 1, _join_copies)


def _join_wait(flight, after):
    return _split_wait("join_halves_wait", flight, after, 1, _join_copies)[0]


def _adamw(name, g, g_row0, w, m, v):
    _, R, C = w.shape
    tm = next(cand for cand in (368, 256, 128, 64, 32, 16, 8) if R % cand == 0)
    assert g_row0 % tm == 0 and g.shape[1] == C

    def body(g_ref, w_ref, m_ref, v_ref, go_ref, d_ref, mo_ref, vo_ref):
        gt = g_ref[...]
        mt = ADAM_B1 * m_ref[...] + (1.0 - ADAM_B1) * gt
        vt = ADAM_B2 * v_ref[...] + (1.0 - ADAM_B2) * jnp.square(gt)
        m_hat = mt / (1.0 - ADAM_B1 ** ADAM_STEP)
        v_hat = vt / (1.0 - ADAM_B2 ** ADAM_STEP)
        go_ref[...] = gt
        d_ref[...] = -ADAM_LR * (m_hat / (jnp.sqrt(v_hat) + ADAM_EPS) + ADAM_WD * w_ref[...])
        mo_ref[...] = mt
        vo_ref[...] = vt

    state = pl.BlockSpec((None, tm, C), lambda i: (0, i, 0))
    return pl.pallas_call(
        body, name=name, grid=(R // tm,),
        in_specs=[pl.BlockSpec((tm, C), lambda i: (g_row0 // tm + i, 0)), state, state, state],
        out_specs=[state] * 4, out_shape=[jax.ShapeDtypeStruct((1, R, C), F32)] * 4,
        compiler_params=_cparams(("parallel",)),
    )(*_hbm(g, w, m, v))


def _unpack_weights(gathered, names):
    S = gathered.shape[0]
    shard_shapes = {"w_in": (D_MODEL, (QKV_WIDTH + 2 * D_MODEL) // S), "w_branch_na": (NA_WIDTH, D_MODEL // S),
                    "w_branch_dil": (DIL_OUT_WIDTH, D_MODEL // S), "w_out": (D_MODEL // S, D_MODEL),
                    "w_up": (D_MODEL, D_FF // S), "w_down": (D_FF // S, D_MODEL),
                    "w_ple_gate": (D_MODEL // S, D_MODEL), "w_ple_proj": (PLE_DIM, D_MODEL // S)}
    col_sharded = {"w_in", "w_branch_na", "w_branch_dil", "w_up", "w_ple_proj"}
    out, r0 = {}, 0
    for name in names:
        rows, cols = shard_shapes[name]
        n = rows * cols // PACK_W
        t = gathered[:, r0:r0 + n, :].reshape(S, rows, cols)
        r0 += n
        out[name] = t.transpose(1, 0, 2).reshape(rows, S * cols) if name in col_sharded else t.reshape(S * rows, cols)
    return out


def kernel(x, p, positions, g_mix, w_in, rpb, w_branch_na, w_branch_dil, w_out, g_mlp, w_up, w_down, g_ple, w_ple_gate, w_ple_proj, g_final, loss_target, m_g_mix, m_w_in, m_rpb, m_w_branch_na, m_w_branch_dil, m_w_out, m_g_mlp, m_w_up, m_w_down, m_g_ple, m_w_ple_gate, m_w_ple_proj, m_g_final, v_g_mix, v_w_in, v_rpb, v_w_branch_na, v_w_branch_dil, v_w_out, v_g_mlp, v_w_up, v_w_down, v_g_ple, v_w_ple_gate, v_w_ple_proj, v_g_final):
    shards = {"w_in": w_in[0], "w_branch_na": w_branch_na[0], "w_branch_dil": w_branch_dil[0], "w_out": w_out[0],
              "w_up": w_up[0], "w_down": w_down[0], "w_ple_gate": w_ple_gate[0], "w_ple_proj": w_ple_proj[0]}
    params = {"w_in": w_in, "w_branch_na": w_branch_na, "w_branch_dil": w_branch_dil, "w_out": w_out, "w_up": w_up,
              "w_down": w_down, "w_ple_gate": w_ple_gate, "w_ple_proj": w_ple_proj,
              "m_w_in": m_w_in, "m_w_branch_na": m_w_branch_na, "m_w_branch_dil": m_w_branch_dil, "m_w_out": m_w_out,
              "m_w_up": m_w_up, "m_w_down": m_w_down, "m_w_ple_gate": m_w_ple_gate, "m_w_ple_proj": m_w_ple_proj,
              "v_w_in": v_w_in, "v_w_branch_na": v_w_branch_na, "v_w_branch_dil": v_w_branch_dil, "v_w_out": v_w_out,
              "v_w_up": v_w_up, "v_w_down": v_w_down, "v_w_ple_gate": v_w_ple_gate, "v_w_ple_proj": v_w_ple_proj}

    xs, ps, tgt = x[0], p[0, 0], loss_target[0]
    T = xs.shape[0]
    TM = 512
    gm, gl, gp, gf = g_mix, g_mlp, g_ple, g_final.reshape(1, D_MODEL)

    across = _gather_halves_start(shards["w_in"].astype(BF), "in")
    a = _rowwise("norm_mix", lambda h, g: h * _rms(h) * g, T, TM, [_row(xs, TM), _full(gm)], [(D_MODEL, BF)],
                 after=(across.token,))
    cos2, sin_signed = _rope_tables(positions[0])
    tab = _na_bias_table(rpb[0])
    packed_mix = jnp.concatenate([_pack_rows(shards[n].astype(BF)) for n in GATHER_MIX], axis=0)
    packed_mlp = jnp.concatenate([_pack_rows(shards[n].astype(BF)) for n in GATHER_MLP], axis=0)
    w_in_shard, w_in_relay = _gather_halves_relay(across, (a, tab, cos2, sin_signed, packed_mix, packed_mlp), "in")
    w_in_all = _gather_halves_finish(w_in_shard, w_in_relay, w_in_relay.token, "in")
    w_in_full, w_gates = _assemble_w_in(w_in_all, 256)
    W = {"w_in": w_in_full}
    mix_flight = _gather_start(packed_mix, "mix", after=(w_in_all,))
    mlp_across = _gather_halves_start(packed_mlp, "mlp", after=(mix_flight.token,))

    n3 = 3 * NA_WIDTH
    qkv = _mm("in_na", a, W["w_in"], "nn", 1024, 768, 1024, [BF], after=(mlp_across.token,),
              b_view=(n3, (D_MODEL, 768), lambda j, k: (k, j)))
    z_dil = _mm("in_dil", a, W["w_in"], "nn", 1024, 768, 1024, [F32], after=(mlp_across.token,),
                b_view=(3 * DIL_WIDTH, (D_MODEL, 768), lambda j, k: (k, n3 // 768 + j)))
    z_gates = _mm("in_gates", a, w_gates, "nn", 1024,1024, 1024, [BF], after=(mlp_across.token,))

    dil_ops = _qkv_prep(z_dil, cos2, sin_signed, TM)
    y_na = _na_fwd(qkv, tab)
    band = [_band_fwd(*dil_ops[g], g) for g in range(len(DIL_GROUPS))]
    y_dil, w_grp, o_nat = _dil_merge_fwd([b[0] for b in band], [b[1] for b in band], T, TM)

    W.update(_unpack_weights(_gather_wait(mix_flight, y_dil, "mix"), GATHER_MIX))
    mlp_shard, mlp_relay = _gather_halves_relay(mlp_across, y_dil, "mlp")
    u_na = _mm("branch_na", y_na, W["w_branch_na"], "nn", 1024,1024, 512, [BF], after=(mlp_relay.token,))
    def gate_mix(acc, gn, gd, un):
        ud = acc.astype(BF)
        return ud, _sigmoid(gn.astype(F32)) * un.astype(F32) + _sigmoid(gd.astype(F32)) * ud.astype(F32)

    u_dil, mixed = _mm("branch_dil", y_dil, W["w_branch_dil"], "nn", 512, 1024, 256, [BF, BF], epilogue=gate_mix,
                       extras=((z_gates, 0), (z_gates, 1), u_na))

    def add_norm(d, h, g):
        h = h + d
        return h, h * _rms(h) * g

    h1, cn = _mm("out_proj", mixed, W["w_out"], "nn", 512, 1024, 1024, [F32, BF], epilogue=add_norm, extras=(xs,), consts=(gl,))
    mlp_all = _gather_halves_finish(mlp_shard, mlp_relay, cn, "mlp")
    W.update({n: t for n, t in _unpack_weights(mlp_all, GATHER_MLP).items() if n.startswith("w_ple")})
    chip_block = (None, D_MODEL, PACK_W)
    up, act = _mm("mlp_up", cn, mlp_all, "nn", 1024,1024, 1024, [BF, BF],
                  epilogue=lambda acc: (acc, jnp.square(jnp.maximum(acc, 0.0))), b_view=(D_FF, chip_block, lambda j, k: (j, 0, 0)))
    h2, en = _mm("mlp_down", act, mlp_all, "nn", 1024, 1024, 1024, [F32, BF], epilogue=add_norm, extras=(h1,), consts=(gp,),
                 b_view=(D_MODEL, chip_block, lambda j, k: (k, 1, 0)))
    pp = _mm("ple_proj", ps, W["w_ple_proj"], "nn", 1024,1024, 256, [F32])

    def head(gtt, h2t, ppt, tg, g):
        sg = _sigmoid(gtt)
        h3 = h2t + sg * ppt
        yo = h3 * _rms(h3) * g
        diff = yo - tg
        loss = 0.5 * jnp.sum(jnp.mean(jnp.square(diff), axis=-1, keepdims=True), axis=0, keepdims=True)
        dh3, dg = _rms_bwd(diff * (1.0 / D_MODEL), h3, g)
        return dh3, dh3 * ppt * sg * (1.0 - sg), dh3 * sg, jnp.broadcast_to(loss, (1, 128)), dg

    dh3, d_gt, d_pp, loss_part, dg_final = _mm(
        "ple_gate_loss_head", en, W["w_ple_gate"], "nn", 512, 1024, 1024, [F32, BF, BF], epilogue=head,
        extras=(h2, pp, tgt), consts=(gf,), sums=[128, D_MODEL])

    early_shapes = {n: shards[n].shape for n in REDUCE_EARLY}
    early_rows = sum(r * c for r, c in early_shapes.values()) // PACK_W
    shard_rows = D_MODEL // N_CHIPS
    early_buf = _mm("g_ple_gate", en, d_gt, "tn", 1024, 1024, 1024, [F32],
                    into=(jax.ShapeDtypeStruct((N_CHIPS, early_rows, PACK_W), F32), (N_CHIPS, shard_rows, PACK_W),
                          lambda i, j: (0, 2 * D_MODEL // shard_rows, 0)))
    g_ple_proj = _mm("g_ple_proj", ps, d_pp, "tn", 256, 1024, 1024,[F32])

    def add_norm_bwd(dn, dh_out, h, g):
        dh, dg = _rms_bwd(dn, h, g)
        dh = dh_out + dh
        return dh, dh, dg

    dh2, dh2_b, dg_ple = _mm("d_ple_gate", d_gt, W["w_ple_gate"], "nt", 512, 1024, 1024, [F32, BF],
                             epilogue=add_norm_bwd, extras=(dh3, h2), consts=(gp,), sums=[D_MODEL])
    d_up = _mm("d_mlp_down", dh2_b, mlp_all, "nt", 1024,1024, 1024, [BF], b_view=(D_FF, chip_block, lambda j, k: (j, 1, 0)),
               epilogue=lambda acc, u: (acc * (2.0 * jnp.maximum(u.astype(F32), 0.0)),), extras=(up,))
    early_buf = _mm("g_mlp_down", act, dh2_b, "tn", 1024, 1024, 1024,[F32],
                    into=(early_buf, (None, D_MODEL, PACK_W), lambda i, j: (i, 1, 0)))
    early_buf = _mm("g_mlp_up", cn, d_up, "tn", 1024, 1024, 1024,[F32],
                    into=(early_buf, (None, D_MODEL, PACK_W), lambda i, j: (j, 0, 0)))
    dh1, dh1_b, dg_mlp = _mm("d_mlp_up", d_up, mlp_all, "nt", 1024, 1024, 1024, [F32, BF], epilogue=add_norm_bwd,
                             b_view=(D_MODEL, chip_block, lambda j, k: (k, 0, 0)),
                             extras=(dh2, h1), consts=(gl,), sums=[D_MODEL])
    early_buf = _mm("g_out_proj", mixed, dh1_b, "tn", 1024, 1024, 1024, [F32],
                    into=(early_buf, (N_CHIPS, shard_rows, PACK_W), lambda i, j: (0, 2 * D_MODEL // shard_rows + 1, 0)))

    def gate_bwd(dm, gn, gd, un, ud):
        gn, gd, un, ud = (t.astype(F32) for t in (gn, gd, un, ud))
        sn, sd = _sigmoid(gn), _sigmoid(gd)
        return jnp.concatenate([dm * un * sn * (1.0 - sn), dm * ud * sd * (1.0 - sd)], axis=1), dm * sn, dm * sd

    dz_gates, d_u_na, d_u_dil = _mm("d_out_proj", dh1_b, W["w_out"], "nt", 512, 1024, 1024, [(BF, 2 * D_MODEL), BF, BF],
                                    epilogue=gate_bwd, extras=((z_gates, 0), (z_gates, 1), u_na, u_dil))
    g_branch_na = _mm("g_branch_na", y_na, d_u_na, "tn", 1024, 1024, 1024,[F32])
    g_branch_dil = _mm("g_branch_dil", y_dil, d_u_dil, "tn", 256, 1024, 1024,[F32])
    small_rows = [jnp.concatenate([_pack_rows(g[:, s * shard_rows:(s + 1) * shard_rows]) for g in (g_ple_proj, g_branch_na, g_branch_dil)],
                                  axis=0) for s in range(N_CHIPS)]
    early_buf = lax.dynamic_update_slice(early_buf, jnp.stack(small_rows), (0, 2 * D_MODEL + 2 * shard_rows, 0))
    early_tm = early_rows // 4
    swap_flight = _swap_start(early_buf, "early")
    d_y_na = _mm("d_branch_na", d_u_na, W["w_branch_na"], "nt", 1024,512, 1024, [BF], after=(swap_flight.token,))
    d_y_dil = _mm("d_branch_dil", d_u_dil, W["w_branch_dil"], "nt", 1024,256, 1024, [F32])

    dqa, dka, dva, dtab = _na_bwd(qkv, tab, d_y_na)
    early_g, early_got = _swap_wait(swap_flight, dqa, "early")
    early_pair, early_pair_b = _pair_sum(early_g, early_got, early_tm)
    scatter_flight = _scatter_start(early_pair_b, "early")

    do_res, dlse_res = _dil_merge_bwd(d_y_dil, o_nat, w_grp, TM, after=(scatter_flight.token,))
    d_dil = [_band_bwd(*dil_ops[g], do_res[g], dlse_res[g], g) for g in range(len(DIL_GROUPS))]

    dz_qkv = _qkv_unprep((dqa, dka, dva), d_dil, cos2, sin_signed, TM)
    in_cols = shards["w_in"].shape[1]
    qkv_rows, gate_tm = dz_qkv.shape[1], 256
    assert qkv_rows % gate_tm == 0
    g_in = _mm("g_in_qkv", dz_qkv, a, "tn", 1280, 1024, 1024, [F32],
               into=(jax.ShapeDtypeStruct((N_CHIPS * in_cols, D_MODEL), F32), (1280, D_MODEL), lambda i, j: (i, 0)))
    g_in = _mm("g_in_gates", dz_gates, a, "tn", gate_tm, 1024, T, [F32],
               into=(g_in, (gate_tm, D_MODEL), lambda i, j: (qkv_rows // gate_tm + i, 0)))
    early_mine = _chip_sum(early_pair, _scatter_wait(scatter_flight, (g_in,), "early"), early_tm)
    join_flight = _join_start(early_mine)

    late_tm = in_cols // 4
    late_swap = _swap_start(g_in.reshape(N_CHIPS, in_cols, D_MODEL), "late")
    d_a = _mm("d_in_qkv", dz_qkv, W["w_in"], "nt", 1024,1024, 1280, [F32], after=(late_swap.token, join_flight.token),
              b_view=(D_MODEL, (D_MODEL, 1280), lambda j, k: (j, k)))
    late_g, late_got = _swap_wait(late_swap, d_a, "late")
    late_pair, late_pair_b = _pair_sum(late_g, late_got, late_tm)
    late_scatter = _scatter_start(late_pair_b, "late")
    d_rpb = _na_rpb_grad(dtab, after=(late_scatter.token,))[:, :2 * NA_WIN_ROWS - 1, :2 * NA_WIN_COLS - 1]
    def first_bwd(dn_gates, dn_qkv, dh_out, h, g):
        dh, dg = _rms_bwd(dn_gates + dn_qkv, h, g)
        return dh_out + dh, dg

    grad_x, dg_mix = _mm("d_in_gates", dz_gates, w_gates, "nt", 512, 1024, 1024, [F32], epilogue=first_bwd,
                         extras=(d_a, dh1, xs), consts=(gm,), sums=[D_MODEL], after=(late_scatter.token,))
    early_shard = _join_wait(join_flight, grad_x)

    n_rpb = rpb.size
    rpb_rows = 4
    small = jnp.concatenate([
        dg_mix, dg_mlp, dg_ple, dg_final,
        jnp.pad(d_rpb.reshape(-1), (0, rpb_rows * D_MODEL - n_rpb)).reshape(rpb_rows, D_MODEL),
        jnp.pad(loss_part, ((0, 0), (0, D_MODEL - loss_part.shape[1]))),
        jnp.zeros((SMALL_ROWS - 5 - rpb_rows, D_MODEL), F32)], axis=0)
    out = {"grad": {}, "delta": {}, "new_m": {}, "new_v": {}}

    def update(n, g, row0):
        res = _adamw("adamw_" + n, g, row0, params[n], params["m_" + n], params["v_" + n])
        for kind, t in zip(("grad", "delta", "new_m", "new_v"), res, strict=True):
            out[kind][n] = t

    row0 = 0
    for n in REDUCE_EARLY:
        rows, cols = early_shapes[n]
        n_rows = rows * cols // PACK_W
        if cols == PACK_W:
            update(n, early_shard, row0)
        else:
            update(n, early_shard[row0:row0 + n_rows].reshape(rows, cols), 0)
        row0 += n_rows
    late_others = _scatter_wait(late_scatter, (*[out["new_v"][n] for n in REDUCE_EARLY], d_rpb), "late")
    late_mine = _chip_sum(late_pair, late_others, late_tm)
    small_flight = _allreduce_start(small, after=(late_mine,))
    res = _adamw("adamw_w_in", _join_halves(late_mine, after=(small_flight.token,)), 0,
                 *[jnp.swapaxes(params[n], 1, 2) for n in ("w_in", "m_w_in", "v_w_in")])
    small = _allreduce_finish(small_flight, res[3])
    for kind, t in zip(("grad", "delta", "new_m", "new_v"), res, strict=True):
        out[kind]["w_in"] = jnp.swapaxes(t, 1, 2)
    loss = small[4 + rpb_rows, 0]

    def small_pack(a0, a1, a2, a3, r):
        return jnp.concatenate([a0.reshape(1, -1), a1.reshape(1, -1), a2.reshape(1, -1), a3.reshape(1, -1),
                                jnp.pad(r.reshape(-1), (0, rpb_rows * D_MODEL - n_rpb)).reshape(rpb_rows, D_MODEL)], axis=0)

    small_res = _adamw("adamw_small", small, 0, small_pack(g_mix, g_mlp, g_ple, g_final, rpb)[None],
                       small_pack(m_g_mix, m_g_mlp, m_g_ple, m_g_final, m_rpb)[None],
                       small_pack(v_g_mix, v_g_mlp, v_g_ple, v_g_final, v_rpb)[None])

    def small_unpack(t):
        return {"g_mix": t[0].reshape(g_mix.shape), "g_mlp": t[1].reshape(g_mlp.shape), "g_ple": t[2].reshape(g_ple.shape),
                "g_final": t[3].reshape(g_final.shape), "rpb": t[4:].reshape(-1)[:n_rpb].reshape(rpb.shape)}

    for kind, t in zip(("grad", "delta", "new_m", "new_v"), small_res, strict=True):
        out[kind].update(small_unpack(t[0]))

    order = ["g_mix", "w_in", "rpb", "w_branch_na", "w_branch_dil", "w_out", "g_mlp", "w_up", "w_down", "g_ple",
             "w_ple_gate", "w_ple_proj", "g_final"]
    return (loss, grad_x[None], *[out["grad"][n] for n in order], *[out["delta"][n] for n in order],
            *[out["new_m"][n] for n in order], *[out["new_v"][n] for n in order])
```

```python
import functools
from typing import NamedTuple

import jax
import jax.numpy as jnp
from jax import lax
from jax.experimental import pallas as pl
from jax.experimental.pallas import tpu as pltpu

BF = jnp.bfloat16
F32 = jnp.float32
MESH = pl.DeviceIdType.MESH
ANY = pl.BlockSpec(memory_space=pl.ANY)

V7X_VMEM_BYTES = 64 * 1024 * 1024
VMEM_LIMIT = V7X_VMEM_BYTES - 16 * 1024 * 1024

D_MODEL = 1024
HEAD_DIM = 64
GRID_W = 64
NA_HEADS = 8
NA_WIN_ROWS = 8
NA_WIN_COLS = 16
NA_WIDTH = NA_HEADS * HEAD_DIM
DIL_GROUPS = ((128, 1), (512, 4), (2048, 16))
DIL_HPG = 4
DIL_HEADS = DIL_HPG * len(DIL_GROUPS)
DIL_WIDTH = DIL_HEADS * HEAD_DIM
DIL_OUT_WIDTH = DIL_HPG * HEAD_DIM
DIL_RADIUS = 64
QKV_WIDTH = 3 * NA_WIDTH + 3 * DIL_WIDTH
D_FF = 4 * D_MODEL
PLE_DIM = 256
ROPE_THETA = 10000.0
RMS_EPS = 1e-6
NEG_INF = -1e30
Q_SCALE = HEAD_DIM ** -0.5

ADAM_LR = 0.001
ADAM_B1 = 0.9
ADAM_B2 = 0.999
ADAM_EPS = 1e-08
ADAM_WD = 0.01
ADAM_STEP = 10

N_CHIPS = 4
N_DEV = 8
PACK_W = 1024
GATHER_MIX = ("w_branch_na", "w_branch_dil", "w_out")
GATHER_MLP = ("w_up", "w_down", "w_ple_gate", "w_ple_proj")
REDUCE_EARLY = ("w_up", "w_down", "w_ple_gate", "w_out", "w_ple_proj", "w_branch_na", "w_branch_dil")
SMALL_ROWS = 16


def _cparams(sem=None):
    return pltpu.CompilerParams(dimension_semantics=sem, vmem_limit_bytes=VMEM_LIMIT)


def _mm(name, a, b, mode, tm, tn, tk, out_dtypes, epilogue=None, extras=(), consts=(), sums=(), after=(), into=None,
        b_view=None):
    if mode == "nn":
        (M, K), N = a.shape, b.shape[1]
    elif mode == "nt":
        (M, K), N = a.shape, b.shape[0]
    else:
        (K, M), N = a.shape, b.shape[1]
    if b_view is not None:
        N = b_view[0]
    tm, tn, tk = min(tm, M), min(tn, N), min(tk, K)
    assert M % tm == 0 and N % tn == 0 and K % tk == 0, (name, M, N, K, tm, tn, tk)
    if mode == "nn":
        a_spec = pl.BlockSpec((tm, tk), lambda i, j, k: (i, k))
        b_spec = pl.BlockSpec((tk, tn), lambda i, j, k: (k, j))
        dims = (((1,), (0,)), ((), ()))
    elif mode == "nt":
        a_spec = pl.BlockSpec((tm, tk), lambda i, j, k: (i, k))
        b_spec = pl.BlockSpec((tn, tk), lambda i, j, k: (j, k))
        dims = (((1,), (1,)), ((), ()))
    else:
        a_spec = pl.BlockSpec((tk, tm), lambda i, j, k: (k, i))
        b_spec = pl.BlockSpec((tk, tn), lambda i, j, k: (k, j))
        dims = (((0,), (0,)), ((), ()))
    if b_view is not None:
        b_spec = pl.BlockSpec(b_view[1], lambda i, j, k: b_view[2](j, k))
    nk = K // tk
    n_extra, n_const, n_out, n_sum = len(extras), len(consts), len(out_dtypes), len(sums)
    tile = pl.BlockSpec((tm, tn), lambda i, j, k: (i, j))
    assert not sums or tn == N, "row sums need whole rows in a tile"
    wide = [e for e in (*extras, *out_dtypes) if isinstance(e, tuple)]
    assert not wide or tn == N
    extra_specs = [pl.BlockSpec((tm, tn), functools.partial(lambda c, i, j, k: (i, c), e[1])) if isinstance(e, tuple) else tile
                   for e in extras]
    extras = [e[0] if isinstance(e, tuple) else e for e in extras]
    out_widths = [d[1] if isinstance(d, tuple) else N for d in out_dtypes]
    out_dtypes = [d[0] if isinstance(d, tuple) else d for d in out_dtypes]

    n_after = len(after)

    def body(a_ref, b_ref, *rest):
        extra_refs, rest = rest[:n_extra + n_const], rest[n_extra + n_const + n_after:]
        out_refs, sum_refs, acc = rest[:n_out], rest[n_out:n_out + n_sum], rest[-1]
        i, k = pl.program_id(0), pl.program_id(2)
        def product():
            return lax.dot_general(a_ref[...].astype(BF), b_ref[...].astype(BF), dims, preferred_element_type=F32)

        if nk > 1:
            @pl.when(k == 0)
            def _():
                acc[...] = jnp.zeros_like(acc)

            acc[...] += product()

        @pl.when(k == nk - 1)
        def _():
            total = product() if nk == 1 else acc[...]
            outs = (total,) if epilogue is None else epilogue(total, *[e[...] for e in extra_refs])
            for o_ref, val in zip(out_refs, outs[:n_out], strict=True):
                o_ref[...] = val.astype(o_ref.dtype).reshape(o_ref.shape)
            for s_ref, val in zip(sum_refs, outs[n_out:], strict=True):
                @pl.when(i == 0)
                def _():
                    s_ref[...] = val

                @pl.when(i != 0)
                def _():
                    s_ref[...] += val

    out_specs = ([tile if w == N else pl.BlockSpec((tm, w), lambda i, j, k: (i, 0)) for w in out_widths]
                 + [pl.BlockSpec((1, c), lambda i, j, k: (0, 0)) for c in sums])
    out_shape = ([jax.ShapeDtypeStruct((M, w), dt) for dt, w in zip(out_dtypes, out_widths, strict=True)]
                 + [jax.ShapeDtypeStruct((1, c), F32) for c in sums])
    operands, aliases = [a, b, *extras, *consts, *after], {}
    in_specs = ([a_spec, b_spec] + extra_specs
                + [pl.BlockSpec(c.shape, functools.partial(lambda nd, i, j, k: (0,) * nd, c.ndim)) for c in consts] + [ANY] * n_after)
    if into is not None:
        assert n_out == 1
        target, block, index = into
        out_specs = [pl.BlockSpec(block, lambda i, j, k: index(i, j))]
        out_shape = [jax.ShapeDtypeStruct(target.shape, target.dtype)]
        if not isinstance(target, jax.ShapeDtypeStruct):
            aliases = {len(operands): 0}
            operands.append(target)
            in_specs.append(ANY)
            n_after += 1

    outs = pl.pallas_call(
        body, name=name, grid=(M // tm, N // tn, nk),
        in_specs=in_specs, out_specs=out_specs, out_shape=out_shape,
        scratch_shapes=[pltpu.VMEM((tm, tn) if nk > 1 else (8, 128), F32)], input_output_aliases=aliases,
        compiler_params=_cparams(("arbitrary",) * 3 if sums else ("parallel", "parallel", "arbitrary")),
    )(*operands)
    return outs[0] if len(outs) == 1 else outs


def _row(arr, tm, col_block=None, width=None):
    width = arr.shape[1] if width is None else width
    cb = 0 if col_block is None else col_block
    return arr, pl.BlockSpec((tm, width), lambda i: (i, cb))


def _full(arr):
    nd = arr.ndim
    return arr, pl.BlockSpec(arr.shape, lambda i: (0,) * nd)


def _rowwise(name, body, T, tm, ins, outs, sums=(), after=()):
    n_in, n_out, n_sum, n_after = len(ins), len(outs), len(sums), len(after)

    def kern(*refs):
        in_refs, refs = refs[:n_in], refs[n_in + n_after:]
        out_refs, sum_refs = refs[:n_out], refs[n_out:]
        res = body(*[r[...] for r in in_refs])
        res = res if isinstance(res, tuple) else (res,)
        for o_ref, val in zip(out_refs, res[:n_out], strict=True):
            o_ref[...] = val.astype(o_ref.dtype)
        if n_sum:
            @pl.when(pl.program_id(0) == 0)
            def _():
                for s_ref in sum_refs:
                    s_ref[...] = jnp.zeros_like(s_ref)

            for s_ref, val in zip(sum_refs, res[n_out:], strict=True):
                s_ref[...] += val

    res = pl.pallas_call(
        kern, name=name, grid=(T // tm,),
        in_specs=[spec for _, spec in ins] + [ANY] * n_after,
        out_specs=[pl.BlockSpec((tm, c), lambda i: (i, 0)) for c, _ in outs]
        + [pl.BlockSpec((1, c), lambda i: (0, 0)) for c in sums],
        out_shape=[jax.ShapeDtypeStruct((T, c), dt) for c, dt in outs]
        + [jax.ShapeDtypeStruct((1, c), F32) for c in sums],
        compiler_params=_cparams(("arbitrary",)),
    )(*[a for a, _ in ins], *after)
    return res[0] if len(res) == 1 else res


def _sigmoid(x):
    return 1.0 / (1.0 + jnp.exp(-x))


def _rms(h):
    return lax.rsqrt(jnp.mean(h * h, axis=-1, keepdims=True) + RMS_EPS)


def _rms_bwd(dy, h, g):
    r = _rms(h)
    n = h * r
    dn = dy * g
    dh = r * (dn - n * jnp.mean(dn * n, axis=-1, keepdims=True))
    return dh, jnp.sum(dy * n, axis=0, keepdims=True)


def _rope(x, cos2, sin_signed):
    lane = lax.broadcasted_iota(jnp.int32, x.shape, 1)
    swapped = jnp.where((lane % HEAD_DIM) < HEAD_DIM // 2, pltpu.roll(x, 128 - HEAD_DIM // 2, 1), pltpu.roll(x, HEAD_DIM // 2, 1))
    return x * cos2 + swapped * sin_signed


NA_KEYS = NA_WIN_ROWS * GRID_W
NA_BASES = 8


def _na_row_geometry(r, rows):
    first = jnp.clip(r - NA_WIN_ROWS // 2, 0, rows - NA_WIN_ROWS)
    base = first - r + (NA_WIN_ROWS - 1)
    return pl.multiple_of(first * GRID_W, GRID_W), base


NA_ROWS_PER_STEP = 16
NA_BWD_ROWS_PER_STEP = 8


def _softmax_rows(s):
    p = jnp.exp(s - jnp.max(s, axis=-1, keepdims=True))
    return p / jnp.sum(p, axis=-1, keepdims=True)


def _split_pair(t):
    first = lax.broadcasted_iota(jnp.int32, t.shape, 1) < HEAD_DIM
    zero = jnp.zeros_like(t)
    return jnp.where(first, t, zero), jnp.where(first, zero, t)


def _join_pair(a, b):
    return jnp.where(lax.broadcasted_iota(jnp.int32, a.shape, 1) < HEAD_DIM, a, b)


_NT = (((1,), (1,)), ((), ()))
_TN = (((0,), (0,)), ((), ()))


def _na_fwd(qkv, tab):
    T = qkv.shape[0]
    rows = T // GRID_W
    n_pairs = NA_WIDTH // 128

    def body(q_ref, k_ref, v_ref, tab_ref, y_ref):
        def step(it, carry):
            geo = [_na_row_geometry(it * NA_ROWS_PER_STEP + u, rows) for u in range(NA_ROWS_PER_STEP)]
            q0s = [pl.multiple_of((it * NA_ROWS_PER_STEP + u) * GRID_W, GRID_W) for u in range(NA_ROWS_PER_STEP)]
            ss = [lax.dot_general(jnp.concatenate(_split_pair(q_ref[pl.ds(q0, GRID_W), :] * Q_SCALE), axis=0),
                                  k_ref[pl.ds(k0, NA_KEYS), :], _NT, preferred_element_type=F32)
                  for q0, (k0, _) in zip(q0s, geo)]
            ps = [_softmax_rows(s + jnp.concatenate([tab_ref[0, base], tab_ref[1, base]], axis=0)) for s, (_, base) in zip(ss, geo)]
            ys = [jnp.dot(p.astype(BF), v_ref[pl.ds(k0, NA_KEYS), :], preferred_element_type=F32) for p, (k0, _) in zip(ps, geo)]
            for q0, y2 in zip(q0s, ys):
                y_ref[pl.ds(q0, GRID_W), :] = _join_pair(y2[:GRID_W], y2[GRID_W:]).astype(y_ref.dtype)
            return carry

        lax.fori_loop(0, rows // NA_ROWS_PER_STEP, step, 0)

    def cols(first):
        return pl.BlockSpec((T, 128), lambda j: (0, first + j))

    return pl.pallas_call(
        body, name="na_fwd", grid=(n_pairs,),
        in_specs=[cols(0), cols(n_pairs), cols(2 * n_pairs), pl.BlockSpec((2, NA_BASES, GRID_W, NA_KEYS), lambda j: (j, 0, 0, 0))],
        out_specs=cols(0), out_shape=jax.ShapeDtypeStruct((T, NA_WIDTH), BF),
        compiler_params=_cparams(("parallel",)),
    )(qkv, qkv, qkv, tab)


def _na_bwd(qkv, tab, do):
    T = qkv.shape[0]
    rows = T // GRID_W
    n_pairs = NA_WIDTH // 128

    def body(q_ref, k_ref, v_ref, tab_ref, do_ref, dq_ref, dk_out, dv_out, dtab_ref, dk_ref, dv_ref):
        dk_ref[...] = jnp.zeros_like(dk_ref)
        dv_ref[...] = jnp.zeros_like(dv_ref)
        dtab_ref[...] = jnp.zeros_like(dtab_ref)

        def step(it, carry):
            U = NA_BWD_ROWS_PER_STEP
            geo = [_na_row_geometry(it * U + u, rows) for u in range(U)]
            q0s = [pl.multiple_of((it * U + u) * GRID_W, GRID_W) for u in range(U)]
            q2s = [jnp.concatenate(_split_pair(q_ref[pl.ds(q0, GRID_W), :] * Q_SCALE), axis=0) for q0 in q0s]
            do2s = [jnp.concatenate(_split_pair(do_ref[pl.ds(q0, GRID_W), :]), axis=0) for q0 in q0s]
            ss = [lax.dot_general(q2, k_ref[pl.ds(k0, NA_KEYS), :], _NT, preferred_element_type=F32) for q2, (k0, _) in zip(q2s, geo)]
            dps = [lax.dot_general(do2, v_ref[pl.ds(k0, NA_KEYS), :], _NT, preferred_element_type=F32) for do2, (k0, _) in zip(do2s, geo)]
            ps = [_softmax_rows(s + jnp.concatenate([tab_ref[0, base], tab_ref[1, base]], axis=0)) for s, (_, base) in zip(ss, geo)]
            dss = [p * (dp - jnp.sum(dp * p, axis=-1, keepdims=True)) for p, dp in zip(ps, dps)]
            dvs = [lax.dot_general(p.astype(BF), do2, _TN, preferred_element_type=F32) for p, do2 in zip(ps, do2s)]
            dsbs = [ds.astype(BF) for ds in dss]
            dqs = [jnp.dot(dsb, k_ref[pl.ds(k0, NA_KEYS), :], preferred_element_type=F32) for dsb, (k0, _) in zip(dsbs, geo)]
            dks = [lax.dot_general(dsb, q2, _TN, preferred_element_type=F32) for dsb, q2 in zip(dsbs, q2s)]
            for u in range(U):
                k0, base = geo[u]
                dtab_ref[0, base] += dss[u][:GRID_W]
                dtab_ref[1, base] += dss[u][GRID_W:]
                dq_ref[pl.ds(q0s[u], GRID_W), :] = (_join_pair(dqs[u][:GRID_W], dqs[u][GRID_W:]) * Q_SCALE).astype(dq_ref.dtype)
                dk_ref[pl.ds(k0, NA_KEYS), :] += dks[u]
                dv_ref[pl.ds(k0, NA_KEYS), :] += dvs[u]
            return carry

        lax.fori_loop(0, rows // NA_BWD_ROWS_PER_STEP, step, 0)
        dk_out[...] = dk_ref[...].astype(dk_out.dtype)
        dv_out[...] = dv_ref[...].astype(dv_out.dtype)

    def cols(first):
        return pl.BlockSpec((T, 128), lambda j: (0, first + j))

    tabs = pl.BlockSpec((2, NA_BASES, GRID_W, NA_KEYS), lambda j: (j, 0, 0, 0))
    wide = jax.ShapeDtypeStruct((T, NA_WIDTH), BF)
    return pl.pallas_call(
        body, name="na_bwd", grid=(n_pairs,),
        in_specs=[cols(0), cols(n_pairs), cols(2 * n_pairs), tabs, cols(0)],
        out_specs=[cols(0), cols(0), cols(0), tabs],
        out_shape=[wide, wide, wide, jax.ShapeDtypeStruct((NA_HEADS, NA_BASES, GRID_W, NA_KEYS), F32)],
        scratch_shapes=[pltpu.VMEM((T, 128), F32), pltpu.VMEM((T, 128), F32)],
        compiler_params=_cparams(("parallel",)),
    )(qkv, qkv, qkv, tab, do)


def _na_bias_table(rpb):
    H, n_rows, n_cols = rpb.shape

    def body(r_ref, tab_ref):
        q = lax.broadcasted_iota(jnp.int32, (GRID_W, 128), 0)
        kc = lax.broadcasted_iota(jnp.int32, (GRID_W, 128), 1)
        first = jnp.clip(q - NA_WIN_COLS // 2, 0, GRID_W - NA_WIN_COLS)
        valid = (kc >= first) & (kc < first + NA_WIN_COLS)
        toeplitz = []
        for ro in range(n_rows):
            row = jnp.broadcast_to(r_ref[pl.ds(ro, 1), :], (GRID_W, 128))
            shifted = pltpu.roll(pltpu.roll(row, 128 - (NA_WIN_COLS - 1), 1), 0, 1, stride=1, stride_axis=0)
            toeplitz.append(jnp.where(valid, shifted, NEG_INF))
        for base in range(NA_BASES):
            for j in range(NA_WIN_ROWS // 2):
                even, odd = toeplitz[base + 2 * j], toeplitz[base + 2 * j + 1]
                tab_ref[base, :, pl.ds(j * 128, 128)] = jnp.where(kc < GRID_W, even, pltpu.roll(odd, GRID_W, 1))

    padded = jnp.pad(rpb, ((0, 0), (0, 16 - n_rows), (0, 128 - n_cols)))
    return pl.pallas_call(
        body, name="na_bias_table", grid=(H,),
        in_specs=[pl.BlockSpec((None, 16, 128), lambda h: (h, 0, 0))],
        out_specs=pl.BlockSpec((None, NA_BASES, GRID_W, NA_KEYS), lambda h: (h, 0, 0, 0)),
        out_shape=jax.ShapeDtypeStruct((H, NA_BASES, GRID_W, NA_KEYS), F32),
        compiler_params=_cparams(("parallel",)),
    )(padded)


def _na_rpb_grad(dtab, after=()):
    H = dtab.shape[0]
    n_rows = 2 * NA_WIN_ROWS - 1
    n_cols = 2 * NA_WIN_COLS - 1

    def body(d_ref, *rest):
        o_ref = rest[-1]
        lane = lax.broadcasted_iota(jnp.int32, (GRID_W, 128), 1)
        low = lane < GRID_W
        flip = (lax.broadcasted_iota(jnp.int32, (GRID_W, GRID_W), 0) + lax.broadcasted_iota(jnp.int32, (GRID_W, GRID_W), 1)
                == GRID_W - 1).astype(BF)

        def reverse_rows(t):
            out = jnp.zeros_like(t)
            for _ in range(3):
                piece = t.astype(BF)
                out = out + jnp.dot(flip, piece, preferred_element_type=F32)
                t = t - piece.astype(F32)
            return out

        out_rows = []
        for ro in range(n_rows):
            acc = jnp.zeros((GRID_W, 128), F32)
            for base in range(NA_BASES):
                i = ro - base
                if not 0 <= i < NA_WIN_ROWS:
                    continue
                pair = d_ref[base, :, pl.ds((i // 2) * 128, 128)]
                if i % 2:
                    pair = pltpu.roll(pair, GRID_W, 1)
                acc = acc + jnp.where(low, pair, 0.0)
            skew = pltpu.roll(reverse_rows(acc), 0, 1, stride=1, stride_axis=0)
            diag = jnp.sum(skew, axis=0, keepdims=True)
            out_rows.append(pltpu.roll(jnp.broadcast_to(diag, (8, 128)), 128 - (GRID_W - NA_WIN_COLS), 1)[:1])
        out_rows.append(jnp.zeros((1, 128), F32))
        res = jnp.concatenate(out_rows, axis=0)
        o_ref[...] = jnp.where(lax.broadcasted_iota(jnp.int32, res.shape, 1) < n_cols, res, 0.0)

    return pl.pallas_call(
        body, name="na_rpb_grad", grid=(H,),
        in_specs=[pl.BlockSpec((None, NA_BASES, GRID_W, NA_KEYS), lambda h: (h, 0, 0, 0))] + [ANY] * len(after),
        out_specs=pl.BlockSpec((None, n_rows + 1, 128), lambda h: (h, 0, 0)),
        out_shape=jax.ShapeDtypeStruct((H, n_rows + 1, 128), F32),
        compiler_params=_cparams(("parallel",)),
    )(dtab, *after)


BAND_Q = 128
BAND_KEYS = BAND_Q + 2 * DIL_RADIUS


def _band_geometry(n, L):
    q0 = pl.multiple_of(n * BAND_Q, BAND_Q)
    k0 = pl.multiple_of(jnp.clip(q0 - DIL_RADIUS, 0, L - BAND_KEYS), DIL_RADIUS)
    qi = q0 + lax.broadcasted_iota(jnp.int32, (BAND_Q, BAND_KEYS), 0)
    kj = k0 + lax.broadcasted_iota(jnp.int32, (BAND_Q, BAND_KEYS), 1)
    return q0, k0, jnp.abs(qi - kj) <= DIL_RADIUS


DIL_PAIRS = DIL_OUT_WIDTH // 128


def _residue_shape(dil, T, dtype):
    return jax.ShapeDtypeStruct((DIL_PAIRS, dil, T // dil, 128), dtype)


def _residue_tile(dil, tm):
    return pl.BlockSpec((DIL_PAIRS, dil, tm // dil, 128), lambda i: (0, 0, i, 0))


def _to_natural(ref, scratch, dil, tm):
    tiles = []
    for pair in range(DIL_PAIRS):
        if dil == 1:
            tiles.append(ref[pair, 0].astype(F32))
            continue
        for r in range(dil):
            scratch[pl.ds(r, tm // dil, stride=dil), :] = ref[pair, r].astype(F32)
        tiles.append(scratch[...])
    return tiles


def _from_natural(tile, scratch, ref, pair, dil, tm):
    if dil == 1:
        ref[pair, 0] = tile.astype(ref.dtype)
        return
    scratch[...] = tile
    for r in range(dil):
        ref[pair, r] = scratch[pl.ds(r, tm // dil, stride=dil), :].astype(ref.dtype)


def _band_specs(group, T):
    dil = DIL_GROUPS[group][1]
    L = T // dil
    assert L % BAND_Q == 0 and L >= BAND_KEYS, (T, dil)
    per_residue = min(BAND_BLOCKS_PER_STEP, L // BAND_Q)
    residues = min(dil, BAND_BLOCKS_PER_STEP // per_residue)
    spec = pl.BlockSpec((None, residues, L, 128), lambda s: (s % DIL_PAIRS, s // DIL_PAIRS, 0, 0))
    return L, residues, per_residue, (dil // residues * DIL_PAIRS,), spec


BAND_BLOCKS_PER_STEP = 8


def _band_softmax(s, valid):
    s = jnp.where(valid, s, NEG_INF)
    m = jnp.max(s, axis=-1, keepdims=True)
    p = jnp.exp(s - m)
    l = jnp.sum(p, axis=-1, keepdims=True)
    return p / l, m + jnp.log(l)


def _band_fwd(q, k, v, group):
    T = q.shape[1] * q.shape[2]
    L, residues, U, grid, spec = _band_specs(group, T)

    def body(q_ref, k_ref, v_ref, o_ref, lse_ref):
        def step(it, carry):
            geo = [(r, *_band_geometry(it * U + u, L)) for r in range(residues) for u in range(U)]
            ss = [lax.dot_general(jnp.concatenate(_split_pair(q_ref[r, pl.ds(q0, BAND_Q), :]), axis=0),
                                  k_ref[r, pl.ds(k0, BAND_KEYS), :], _NT, preferred_element_type=F32) for r, q0, k0, _ in geo]
            pls = [_band_softmax(s, jnp.concatenate([valid, valid], axis=0)) for s, (_, _, _, valid) in zip(ss, geo)]
            os = [jnp.dot(p.astype(BF), v_ref[r, pl.ds(k0, BAND_KEYS), :], preferred_element_type=F32)
                  for (p, _), (r, _, k0, _) in zip(pls, geo)]
            for (r, q0, _, _), o2, (_, lse) in zip(geo, os, pls):
                o_ref[r, pl.ds(q0, BAND_Q), :] = _join_pair(o2[:BAND_Q], o2[BAND_Q:])
                lse2 = jnp.broadcast_to(lse, (2 * BAND_Q, 128))
                lse_ref[r, pl.ds(q0, BAND_Q), :] = _join_pair(lse2[:BAND_Q], lse2[BAND_Q:])
            return carry

        lax.fori_loop(0, L // (BAND_Q * U), step, 0)

    res = _residue_shape(DIL_GROUPS[group][1], T, F32)
    return pl.pallas_call(
        body, name=f"band_fwd_g{group}", grid=grid,
        in_specs=[spec] * 3, out_specs=[spec] * 2, out_shape=[res, res],
        compiler_params=_cparams(("parallel",)),
    )(q, k, v)


def _band_bwd(q, k, v, do, dlse, group):
    T = q.shape[1] * q.shape[2]
    L, residues, U, grid, spec = _band_specs(group, T)

    def body(q_ref, k_ref, v_ref, do_ref, dlse_ref, dq_ref, dk_ref, dv_ref):
        dk_ref[...] = jnp.zeros_like(dk_ref)
        dv_ref[...] = jnp.zeros_like(dv_ref)

        def step(it, carry):
            geo = [(r, *_band_geometry(it * U + u, L)) for r in range(residues) for u in range(U)]
            q2s = [jnp.concatenate(_split_pair(q_ref[r, pl.ds(q0, BAND_Q), :]), axis=0) for r, q0, _, _ in geo]
            do2s = [jnp.concatenate(_split_pair(do_ref[r, pl.ds(q0, BAND_Q), :]), axis=0) for r, q0, _, _ in geo]
            ss = [lax.dot_general(q2, k_ref[r, pl.ds(k0, BAND_KEYS), :], _NT, preferred_element_type=F32)
                  for q2, (r, _, k0, _) in zip(q2s, geo)]
            dps = [lax.dot_general(do2, v_ref[r, pl.ds(k0, BAND_KEYS), :], _NT, preferred_element_type=F32)
                   for do2, (r, _, k0, _) in zip(do2s, geo)]
            ps = [_band_softmax(s, jnp.concatenate([valid, valid], axis=0))[0] for s, (_, _, _, valid) in zip(ss, geo)]
            dss = []
            for p, dp, (r, q0, _, _) in zip(ps, dps, geo):
                dl = dlse_ref[r, pl.ds(q0, BAND_Q), :]
                dl2 = jnp.concatenate([dl[:, :1], dl[:, HEAD_DIM:HEAD_DIM + 1]], axis=0)
                dss.append(p * (dp - jnp.sum(dp * p, axis=-1, keepdims=True) + dl2))
            dvs = [lax.dot_general(p.astype(BF), do2, _TN, preferred_element_type=F32) for p, do2 in zip(ps, do2s)]
            dsbs = [ds.astype(BF) for ds in dss]
            dqs = [jnp.dot(dsb, k_ref[r, pl.ds(k0, BAND_KEYS), :], preferred_element_type=F32) for dsb, (r, _, k0, _) in zip(dsbs, geo)]
            dks = [lax.dot_general(dsb, q2, _TN, preferred_element_type=F32) for dsb, q2 in zip(dsbs, q2s)]
            for u, (r, q0, k0, _) in enumerate(geo):
                dq_ref[r, pl.ds(q0, BAND_Q), :] = _join_pair(dqs[u][:BAND_Q], dqs[u][BAND_Q:])
                dk_ref[r, pl.ds(k0, BAND_KEYS), :] += dks[u]
                dv_ref[r, pl.ds(k0, BAND_KEYS), :] += dvs[u]
            return carry

        lax.fori_loop(0, L // (BAND_Q * U), step, 0)

    res = _residue_shape(DIL_GROUPS[group][1], T, F32)
    return pl.pallas_call(
        body, name=f"band_bwd_g{group}", grid=grid,
        in_specs=[spec] * 5, out_specs=[spec] * 3, out_shape=[res] * 3,
        compiler_params=_cparams(("parallel",)),
    )(q, k, v, do, dlse)


def _head_sums(t):
    head = lax.broadcasted_iota(jnp.int32, t.shape, 1) // HEAD_DIM
    out = jnp.zeros_like(t)
    for h in range(t.shape[1] // HEAD_DIM):
        mine = head == h
        out = jnp.where(mine, jnp.sum(jnp.where(mine, t, 0.0), axis=-1, keepdims=True), out)
    return out


def _dil_merge_fwd(os, lses, T, tm):
    G = len(DIL_GROUPS)
    W = DIL_OUT_WIDTH
    dils = [d for _, d in DIL_GROUPS]

    def body(*refs):
        o_refs, lse_refs = refs[:G], refs[G:2 * G]
        y_ref, w_refs, on_refs, scratch = refs[2 * G], refs[2 * G + 1:3 * G + 1], refs[3 * G + 1:4 * G + 1], refs[-1]
        o = [jnp.concatenate(_to_natural(r, scratch, d, tm), axis=1) for r, d in zip(o_refs, dils)]
        ls = [jnp.concatenate(_to_natural(r, scratch, d, tm), axis=1) for r, d in zip(lse_refs, dils)]
        m = functools.reduce(jnp.maximum, ls)
        es = [jnp.exp(l - m) for l in ls]
        tot = functools.reduce(jnp.add, es)
        ws = [e / tot for e in es]
        y_ref[...] = functools.reduce(jnp.add, [w * t for w, t in zip(ws, o)]).astype(y_ref.dtype)
        for g in range(G):
            w_refs[g][...] = ws[g]
            on_refs[g][...] = o[g]

    nat = pl.BlockSpec((tm, W), lambda i: (i, 0))
    res = pl.pallas_call(
        body, name="dil_merge_fwd", grid=(T // tm,),
        in_specs=[_residue_tile(d, tm) for d in dils] * 2,
        out_specs=[nat] * (2 * G + 1),
        out_shape=[jax.ShapeDtypeStruct((T, W), BF)] + [jax.ShapeDtypeStruct((T, W), F32)] * (2 * G),
        scratch_shapes=[pltpu.VMEM((tm, 128), F32)],
        compiler_params=_cparams(("parallel",)),
    )(*os, *lses)
    return res[0], res[1:G + 1], res[G + 1:]


def _dil_merge_bwd(dy, os, ws, tm, after=()):
    G = len(DIL_GROUPS)
    T, W = dy.shape
    dils = [d for _, d in DIL_GROUPS]
    n_after = len(after)

    def body(*refs):
        dyt = refs[0][...]
        o, w = [r[...] for r in refs[1:G + 1]], [r[...] for r in refs[G + 1:2 * G + 1]]
        refs = refs[2 * G + 1 + n_after:]
        do_refs, dlse_refs, scratch = refs[:G], refs[G:2 * G], refs[-1]
        dws = [_head_sums(dyt * t) for t in o]
        mean = functools.reduce(jnp.add, [a * b for a, b in zip(w, dws)])
        for g, d in enumerate(dils):
            do, dlse = w[g] * dyt, w[g] * (dws[g] - mean)
            for pair in range(DIL_PAIRS):
                cols = slice(pair * 128, (pair + 1) * 128)
                _from_natural(do[:, cols], scratch, do_refs[g], pair, d, tm)
                _from_natural(dlse[:, cols], scratch, dlse_refs[g], pair, d, tm)

    nat = pl.BlockSpec((tm, W), lambda i: (i, 0))
    res = pl.pallas_call(
        body, name="dil_merge_bwd", grid=(T // tm,),
        in_specs=[nat] * (2 * G + 1) + [ANY] * n_after,
        out_specs=[_residue_tile(d, tm) for d in dils] * 2,
        out_shape=[_residue_shape(d, T, BF) for d in dils] + [_residue_shape(d, T, F32) for d in dils],
        scratch_shapes=[pltpu.VMEM((tm, 128), F32)],
        compiler_params=_cparams(("parallel",)),
    )(dy, *os, *ws, *after)
    return res[:G], res[G:]


def _qkv_prep(z, cos2, sin_signed, tm):
    T = z.shape[0]
    G = len(DIL_GROUPS)
    dils = [d for _, d in DIL_GROUPS]
    n_dil_blocks = 3 * DIL_WIDTH // 128

    def body(*refs):
        blocks = refs[:n_dil_blocks]
        cos_ref, sin_ref = refs[n_dil_blocks], refs[1 + n_dil_blocks]
        outs = refs[2 + n_dil_blocks:]
        for part in range(3):
            for g, d in enumerate(dils):
                out = outs[g * 3 + part]
                for pair in range(DIL_PAIRS):
                    blk = blocks[part * (DIL_WIDTH // 128) + g * DIL_PAIRS + pair]
                    for r in range(d):
                        rows = pl.ds(r, tm // d, stride=d) if d > 1 else slice(None)
                        x = blk[rows, :]
                        if part < 2:
                            x = _rope(x, cos_ref[rows, :], sin_ref[rows, :])
                        if part == 0:
                            x = x * Q_SCALE
                        out[pair, r] = x.astype(out.dtype)

    lane_block = [pl.BlockSpec((tm, 128), functools.partial(lambda b, i: (i, b), b)) for b in range(n_dil_blocks)]
    tab = pl.BlockSpec((tm, 128), lambda i: (i, 0))
    res = pl.pallas_call(
        body, name="qkv_prep", grid=(T // tm,),
        in_specs=lane_block + [tab, tab],
        out_specs=[_residue_tile(d, tm) for d in dils for _ in range(3)],
        out_shape=[_residue_shape(d, T, BF) for d in dils for _ in range(3)],
        compiler_params=_cparams(("parallel",)),
    )(*[z] * n_dil_blocks, cos2, sin_signed)
    return [res[3 * g:3 + 3 * g] for g in range(G)]


def _qkv_unprep(d_na, d_dil, cos2, sin_signed, tm, after=()):
    T = d_na[0].shape[0]
    G = len(DIL_GROUPS)
    dils = [d for _, d in DIL_GROUPS]
    n_after = len(after)

    def body(*refs):
        dq, dk, dv = (r[...] for r in refs[:3])
        res_refs = refs[3:3 + 3 * G]
        cs, sn = refs[3 + 3 * G][...], refs[4 + 3 * G][...]
        out, scratch = refs[5 + 3 * G + n_after], refs[-1]
        cols = [dq, dk, dv]
        for part in range(3):
            for g, d in enumerate(dils):
                for x in _to_natural(res_refs[g * 3 + part], scratch, d, tm):
                    if part < 2:
                        x = _rope(x, cs, -sn)
                    cols.append((x * Q_SCALE if part == 0 else x).astype(out.dtype))
        out[...] = jnp.concatenate(cols, axis=1)

    wide = pl.BlockSpec((tm, NA_WIDTH), lambda i: (i, 0))
    tab = pl.BlockSpec((tm, 128), lambda i: (i, 0))
    return pl.pallas_call(
        body, name="qkv_unprep", grid=(T // tm,),
        in_specs=[wide] * 3 + [_residue_tile(d, tm) for d in dils for _ in range(3)] + [tab, tab] + [ANY] * n_after,
        out_specs=pl.BlockSpec((tm, QKV_WIDTH), lambda i: (i, 0)),
        out_shape=jax.ShapeDtypeStruct((T, QKV_WIDTH), BF),
        scratch_shapes=[pltpu.VMEM((tm, 128), F32)],
        compiler_params=_cparams(("parallel",)),
    )(*d_na, *[t for g in range(G) for t in d_dil[g]], cos2, sin_signed, *after)


def _rope_tables(positions):
    half = HEAD_DIM // 2
    inv_freq = ROPE_THETA ** (-jnp.arange(half, dtype=F32) / half)
    ang = positions.astype(F32)[:, None] * inv_freq
    cos, sin = jnp.cos(ang), jnp.sin(ang)
    return jnp.tile(jnp.concatenate([cos, cos], axis=1), (1, 2)), jnp.tile(jnp.concatenate([-sin, sin], axis=1), (1, 2))


def _pack_rows(t):
    return t.reshape(-1, PACK_W)


def _me():
    return lax.axis_index("x"), lax.axis_index("y"), lax.axis_index("c")


def _other_chips(x, y):
    return [(1 - x, y), (x, 1 - y), (1 - x, 1 - y)]


def _pair_sum(g, got, tm):
    S, R, W = g.shape
    half = R // 2
    nb = half // tm

    def body(pos_ref, g_ref, got_ref, own_ref, ob_ref):
        tot = g_ref[...] + got_ref[...]
        ob_ref[...] = tot.astype(ob_ref.dtype)

        @pl.when(pl.program_id(1) == pos_ref[1])
        def _():
            own_ref[...] = tot

    tile = pl.BlockSpec((None, tm, W), lambda i, s, pos_ref: (s, i, 0))
    c, chip = lax.axis_index("c"), 2 * lax.axis_index("x") + lax.axis_index("y")
    return pl.pallas_call(
        body, name="pair_sum",
        grid_spec=pltpu.PrefetchScalarGridSpec(
            num_scalar_prefetch=1, grid=(nb, S),
            in_specs=[pl.BlockSpec((None, tm, W), lambda i, s, pos_ref: (s, pos_ref[0] * nb + i, 0)), tile],
            out_specs=[pl.BlockSpec((tm, W), lambda i, s, pos_ref: (i, 0)), tile]),
        out_shape=[jax.ShapeDtypeStruct((half, W), F32), jax.ShapeDtypeStruct((S, half, W), BF)],
        compiler_params=_cparams(("parallel", "arbitrary")),
    )(jnp.stack([c, chip]).astype(jnp.int32), g, got)


def _chip_sum(own, others, tm):
    n, h, W = others.shape
    nb = h // tm

    def body(c_ref, own_ref, p_ref, o_ref):
        o_ref[...] = ((own_ref[...] + p_ref[0].astype(F32)) + p_ref[1].astype(F32)) + p_ref[2].astype(F32)

    return pl.pallas_call(
        body, name="chip_sum",
        grid_spec=pltpu.PrefetchScalarGridSpec(
            num_scalar_prefetch=1, grid=(nb,),
            in_specs=[pl.BlockSpec((tm, W), lambda i, c_ref: (i, 0)), pl.BlockSpec((n, tm, W), lambda i, c_ref: (0, i, 0))],
            out_specs=pl.BlockSpec((tm, W), lambda i, c_ref: (c_ref[0] * nb + i, 0))),
        out_shape=jax.ShapeDtypeStruct((2 * h, W), F32),
        compiler_params=_cparams(("parallel",)),
    )(lax.axis_index("c").reshape(1).astype(jnp.int32), own, others)


def _join_halves(shard, after=()):
    h = shard.shape[0] // 2

    def body(in_ref, *rest):
        out_ref, send_sem, recv_sem = rest[len(after):]
        x, y, c = _me()
        cp = pltpu.make_async_remote_copy(
            src_ref=in_ref.at[pl.ds(c * h, h), :], dst_ref=out_ref.at[pl.ds(c * h, h), :],
            send_sem=send_sem, recv_sem=recv_sem, device_id=(x, y, 1 - c), device_id_type=MESH)
        cp.start()
        pltpu.make_async_remote_copy(
            src_ref=in_ref.at[pl.ds(c * h, h), :], dst_ref=out_ref.at[pl.ds((1 - c) * h, h), :],
            send_sem=send_sem, recv_sem=recv_sem, device_id=(x, y, 1 - c), device_id_type=MESH).wait_recv()
        cp.wait_send()

    return pl.pallas_call(
        body, name="join_halves", in_specs=[ANY] * (1 + len(after)), out_specs=ANY,
        out_shape=jax.ShapeDtypeStruct(shard.shape, shard.dtype), input_output_aliases={0: 0},
        scratch_shapes=[pltpu.SemaphoreType.DMA, pltpu.SemaphoreType.DMA],
    )(shard, *after)


def _allreduce_copies(src_ref, land_ref):
    x, y, c = _me()
    peers = [((x + fx) % 2, (y + fy) % 2, (c + fc) % 2) for fx in range(2) for fy in range(2) for fc in range(2)][1:]
    return [(src_ref, land_ref.at[4 * x + 2 * y + c], peer) for peer in peers]


def _allreduce_start(s, after):
    return _split_start("allreduce_small_start", s, (N_DEV, *s.shape), s.dtype, N_DEV - 1, _allreduce_copies, after)


def _allreduce_finish(flight, after):
    own, landed = _split_wait("allreduce_small_wait", flight, after, N_DEV - 1, _allreduce_copies)
    me = 4 * lax.axis_index("x") + 2 * lax.axis_index("y") + lax.axis_index("c")
    parts = lax.dynamic_update_slice(landed, own[None], (me, 0, 0))

    def body(p_ref, o_ref):
        total = p_ref[0]
        for d in range(1, N_DEV):
            total = total + p_ref[d]
        o_ref[...] = total

    return pl.pallas_call(
        body, name="allreduce_small_sum",
        in_specs=[pl.BlockSpec(memory_space=pltpu.VMEM)], out_specs=pl.BlockSpec(memory_space=pltpu.VMEM),
        out_shape=jax.ShapeDtypeStruct(own.shape, F32),
    )(parts)


HBM_SPEC = pl.BlockSpec(memory_space=pltpu.HBM)
SEM_SPEC = pl.BlockSpec(memory_space=pltpu.SEMAPHORE)
DATAFLOW = pltpu.SideEffectType.DATAFLOW_SIDE_EFFECTING


class _InFlight(NamedTuple):
    sems: tuple
    src: jax.Array
    land: jax.Array
    token: jax.Array


def _split_start(name, src, land_shape, land_dtype, n, copies, after=()):
    n_after = len(after)

    def body(src_ref, land_ref, *rest):
        rest = rest[n_after:]
        sems, token = rest[:2 * n], rest[-1]
        for k, (s, d, peer) in enumerate(copies(src_ref, land_ref)):
            pltpu.make_async_remote_copy(src_ref=s, dst_ref=d, send_sem=sems[k], recv_sem=sems[n + k],
                                         device_id=peer, device_id_type=MESH).start()
        token[...] = jnp.zeros_like(token)

    outs = pl.pallas_call(
        body, name=name,
        out_shape=(*[pltpu.SemaphoreType.DMA(())] * (2 * n), pltpu.HBM(src.shape, src.dtype), pltpu.HBM(land_shape, land_dtype),
                   jax.ShapeDtypeStruct((8, 128), F32)),
        in_specs=(HBM_SPEC, HBM_SPEC, *[ANY] * n_after),
        out_specs=(*[SEM_SPEC] * (2 * n), HBM_SPEC, HBM_SPEC, pl.BlockSpec(memory_space=pltpu.VMEM)),
        input_output_aliases={0: 2 * n, 1: 2 * n + 1},
        compiler_params=pltpu.CompilerParams(has_side_effects=DATAFLOW),
    )(pltpu.with_memory_space_constraint(src, pltpu.HBM), pltpu.with_memory_space_constraint(lax.empty(land_shape, land_dtype), pltpu.HBM),
      *after)
    return _InFlight(tuple(outs[:2 * n]), outs[2 * n], outs[2 * n + 1], outs[2 * n + 2])


def _split_wait(name, flight, after, n, copies):
    after = after if isinstance(after, tuple) else (after,)

    def body(src_ref, land_ref, *rest):
        sems = rest[:2 * n]
        for k, (s, d, peer) in enumerate(copies(src_ref, land_ref)):
            cp = pltpu.make_async_remote_copy(src_ref=s, dst_ref=d, send_sem=sems[k], recv_sem=sems[n + k],
                                              device_id=peer, device_id_type=MESH)
            cp.wait_send()
            cp.wait_recv()

    return pl.pallas_call(
        body, name=name,
        out_shape=(pltpu.HBM(flight.src.shape, flight.src.dtype), pltpu.HBM(flight.land.shape, flight.land.dtype)),
        in_specs=(HBM_SPEC, HBM_SPEC, *[SEM_SPEC] * (2 * n), *[ANY] * len(after)),
        out_specs=(HBM_SPEC, HBM_SPEC), input_output_aliases={0: 0, 1: 1},
        compiler_params=pltpu.CompilerParams(has_side_effects=DATAFLOW),
    )(flight.src, flight.land, *flight.sems, *after)


def _gather_copies(src_ref, land_ref):
    x, y, c = _me()
    return [(src_ref, land_ref.at[2 * x + y], (*chip, c)) for chip in _other_chips(x, y)]


def _gather_start(packed, tag, after=()):
    return _split_start(f"gather_start_{tag}", packed, (N_CHIPS, *packed.shape), packed.dtype, 3, _gather_copies, after)


def _gather_wait(flight, after, tag):
    src, others = _split_wait(f"gather_wait_{tag}", flight, after, 3, _gather_copies)
    return lax.dynamic_update_slice(others, src[None], (2 * lax.axis_index("x") + lax.axis_index("y"), 0, 0))


def _across_copies(src_ref, land_ref):
    x, y, c = _me()
    half = src_ref.shape[0] // 2
    rows = pl.ds(c * half, half)
    return [(src_ref.at[rows, :], land_ref.at[2 * x + y, rows, :], (*chip, c)) for chip in _other_chips(x, y)]


def _to_sibling_copies(all_ref, unused_ref):
    x, y, c = _me()
    half = all_ref.shape[1] // 2
    places = [all_ref.at[2 * chip[0] + chip[1], pl.ds(c * half, half), :] for chip in _other_chips(x, y)]
    return [(place, place, (x, y, 1 - c)) for place in places]


def _gather_halves_start(shard, tag, after=()):
    return _split_start(f"gather_{tag}_across_start", shard, (N_CHIPS, *shard.shape), shard.dtype, 3, _across_copies, after)


def _gather_halves_relay(flight, after, tag):
    shard, landed = _split_wait(f"gather_{tag}_across_wait", flight, after, 3, _across_copies)
    return shard, _split_start(f"gather_{tag}_sibling_start", landed, (8, 128), landed.dtype, 3, _to_sibling_copies)


def _gather_halves_finish(shard, relay, after, tag):
    others = _split_wait(f"gather_{tag}_sibling_wait", relay, after, 3, _to_sibling_copies)[0]
    return lax.dynamic_update_slice(others, shard[None], (2 * lax.axis_index("x") + lax.axis_index("y"), 0, 0))


def _assemble_w_in(shards, tm):
    S, R, C = shards.shape
    n_gates = 2 * D_MODEL

    def body(s_ref, w_ref, g_ref):
        full = jnp.concatenate([s_ref[s] for s in range(S)], axis=1)
        w_ref[...] = full
        g_ref[...] = full[:, S * C - n_gates:]

    return pl.pallas_call(
        body, name="assemble_w_in", grid=(R // tm,),
        in_specs=[pl.BlockSpec((S, tm, C), lambda i: (0, i, 0))],
        out_specs=[pl.BlockSpec((tm, S * C), lambda i: (i, 0)), pl.BlockSpec((tm, n_gates), lambda i: (i, 0))],
        out_shape=[jax.ShapeDtypeStruct((R, S * C), shards.dtype), jax.ShapeDtypeStruct((R, n_gates), shards.dtype)],
        compiler_params=_cparams(("parallel",)),
    )(shards)


def _swap_copies(src_ref, land_ref):
    x, y, c = _me()
    half = land_ref.shape[1]
    return [(src_ref.at[:, pl.ds((1 - c) * half, half), :], land_ref, (x, y, 1 - c))]


def _swap_start(g, tag):
    S, R, W = g.shape
    return _split_start(f"swap_halves_start_{tag}", g, (S, R // 2, W), g.dtype, 1, _swap_copies)


def _swap_wait(flight, after, tag):
    return _split_wait(f"swap_halves_wait_{tag}", flight, after, 1, _swap_copies)


def _scatter_copies(src_ref, land_ref):
    x, y, c = _me()
    return [(src_ref.at[2 * chip[0] + chip[1]], land_ref.at[j], (*chip, c)) for j, chip in enumerate(_other_chips(x, y))]


def _scatter_start(part, tag):
    S, h, W = part.shape
    return _split_start(f"scatter_chips_start_{tag}", part, (S - 1, h, W), part.dtype, 3, _scatter_copies)


def _scatter_wait(flight, after, tag):
    return _split_wait(f"scatter_chips_wait_{tag}", flight, after, 3, _scatter_copies)[1]


def _join_copies(shard_ref, unused_ref):
    x, y, c = _me()
    h = shard_ref.shape[0] // 2
    rows = shard_ref.at[pl.ds(c * h, h), :]
    return [(rows, rows, (x, y, 1 - c))]


def _join_start(shard):
    return _split_start("join_halves_start", shard, (8, 128), shard.dtype, 1, _join_copies)


def _join_wait(flight, after):
    return _split_wait("join_halves_wait", flight, after, 1, _join_copies)[0]


def _adamw(name, g, g_row0, w, m, v):
    _, R, C = w.shape
    tm = next(cand for cand in (368, 256, 128, 64, 32, 16, 8) if R % cand == 0)
    assert g_row0 % tm == 0 and g.shape[1] == C

    def body(g_ref, w_ref, m_ref, v_ref, go_ref, d_ref, mo_ref, vo_ref):
        gt = g_ref[...]
        mt = ADAM_B1 * m_ref[...] + (1.0 - ADAM_B1) * gt
        vt = ADAM_B2 * v_ref[...] + (1.0 - ADAM_B2) * jnp.square(gt)
        m_hat = mt / (1.0 - ADAM_B1 ** ADAM_STEP)
        v_hat = vt / (1.0 - ADAM_B2 ** ADAM_STEP)
        go_ref[...] = gt
        d_ref[...] = -ADAM_LR * (m_hat / (jnp.sqrt(v_hat) + ADAM_EPS) + ADAM_WD * w_ref[...])
        mo_ref[...] = mt
        vo_ref[...] = vt

    state = pl.BlockSpec((None, tm, C), lambda i: (0, i, 0))
    return pl.pallas_call(
        body, name=name, grid=(R // tm,),
        in_specs=[pl.BlockSpec((tm, C), lambda i: (g_row0 // tm + i, 0)), state, state, state],
        out_specs=[state] * 4, out_shape=[jax.ShapeDtypeStruct((1, R, C), F32)] * 4,
        compiler_params=_cparams(("parallel",)),
    )(g, w, m, v)


def _unpack_weights(gathered, names):
    S = gathered.shape[0]
    shard_shapes = {"w_in": (D_MODEL, (QKV_WIDTH + 2 * D_MODEL) // S), "w_branch_na": (NA_WIDTH, D_MODEL // S),
                    "w_branch_dil": (DIL_OUT_WIDTH, D_MODEL // S), "w_out": (D_MODEL // S, D_MODEL),
                    "w_up": (D_MODEL, D_FF // S), "w_down": (D_FF // S, D_MODEL),
                    "w_ple_gate": (D_MODEL // S, D_MODEL), "w_ple_proj": (PLE_DIM, D_MODEL // S)}
    col_sharded = {"w_in", "w_branch_na", "w_branch_dil", "w_up", "w_ple_proj"}
    out, r0 = {}, 0
    for name in names:
        rows, cols = shard_shapes[name]
        n = rows * cols // PACK_W
        t = gathered[:, r0:r0 + n, :].reshape(S, rows, cols)
        r0 += n
        out[name] = t.transpose(1, 0, 2).reshape(rows, S * cols) if name in col_sharded else t.reshape(S * rows, cols)
    return out


def kernel(x, p, positions, g_mix, w_in, rpb, w_branch_na, w_branch_dil, w_out, g_mlp, w_up, w_down, g_ple, w_ple_gate, w_ple_proj, g_final, loss_target, m_g_mix, m_w_in, m_rpb, m_w_branch_na, m_w_branch_dil, m_w_out, m_g_mlp, m_w_up, m_w_down, m_g_ple, m_w_ple_gate, m_w_ple_proj, m_g_final, v_g_mix, v_w_in, v_rpb, v_w_branch_na, v_w_branch_dil, v_w_out, v_g_mlp, v_w_up, v_w_down, v_g_ple, v_w_ple_gate, v_w_ple_proj, v_g_final):
    shards = {"w_in": w_in[0], "w_branch_na": w_branch_na[0], "w_branch_dil": w_branch_dil[0], "w_out": w_out[0],
              "w_up": w_up[0], "w_down": w_down[0], "w_ple_gate": w_ple_gate[0], "w_ple_proj": w_ple_proj[0]}
    params = {"w_in": w_in, "w_branch_na": w_branch_na, "w_branch_dil": w_branch_dil, "w_out": w_out, "w_up": w_up,
              "w_down": w_down, "w_ple_gate": w_ple_gate, "w_ple_proj": w_ple_proj,
              "m_w_in": m_w_in, "m_w_branch_na": m_w_branch_na, "m_w_branch_dil": m_w_branch_dil, "m_w_out": m_w_out,
              "m_w_up": m_w_up, "m_w_down": m_w_down, "m_w_ple_gate": m_w_ple_gate, "m_w_ple_proj": m_w_ple_proj,
              "v_w_in": v_w_in, "v_w_branch_na": v_w_branch_na, "v_w_branch_dil": v_w_branch_dil, "v_w_out": v_w_out,
              "v_w_up": v_w_up, "v_w_down": v_w_down, "v_w_ple_gate": v_w_ple_gate, "v_w_ple_proj": v_w_ple_proj}

    xs, ps, tgt = x[0], p[0, 0], loss_target[0]
    T = xs.shape[0]
    TM = 512
    gm, gl, gp, gf = g_mix, g_mlp, g_ple, g_final.reshape(1, D_MODEL)

    across = _gather_halves_start(shards["w_in"].astype(BF), "in")
    a = _rowwise("norm_mix", lambda h, g: h * _rms(h) * g, T, TM, [_row(xs, TM), _full(gm)], [(D_MODEL, BF)],
                 after=(across.token,))
    cos2, sin_signed = _rope_tables(positions[0])
    tab = _na_bias_table(rpb[0])
    packed_mix = jnp.concatenate([_pack_rows(shards[n].astype(BF)) for n in GATHER_MIX], axis=0)
    packed_mlp = jnp.concatenate([_pack_rows(shards[n].astype(BF)) for n in GATHER_MLP], axis=0)
    w_in_shard, w_in_relay = _gather_halves_relay(across, (a, tab, cos2, sin_signed, packed_mix, packed_mlp), "in")
    w_in_all = _gather_halves_finish(w_in_shard, w_in_relay, w_in_relay.token, "in")
    w_in_full, w_gates = _assemble_w_in(w_in_all, 256)
    W = {"w_in": w_in_full}
    mix_flight = _gather_start(packed_mix, "mix", after=(w_in_all,))
    mlp_across = _gather_halves_start(packed_mlp, "mlp", after=(mix_flight.token,))

    n3 = 3 * NA_WIDTH
    qkv = _mm("in_na", a, W["w_in"], "nn", 1024, 768, 1024, [BF], after=(mlp_across.token,),
              b_view=(n3, (D_MODEL, 768), lambda j, k: (k, j)))
    z_dil = _mm("in_dil", a, W["w_in"], "nn", 1024, 768, 1024, [F32], after=(mlp_across.token,),
                b_view=(3 * DIL_WIDTH, (D_MODEL, 768), lambda j, k: (k, n3 // 768 + j)))
    z_gates = _mm("in_gates", a, w_gates, "nn", 1024,1024, 1024, [BF], after=(mlp_across.token,))

    dil_ops = _qkv_prep(z_dil, cos2, sin_signed, TM)
    y_na = _na_fwd(qkv, tab)
    band = [_band_fwd(*dil_ops[g], g) for g in range(len(DIL_GROUPS))]
    y_dil, w_grp, o_nat = _dil_merge_fwd([b[0] for b in band], [b[1] for b in band], T, TM)

    W.update(_unpack_weights(_gather_wait(mix_flight, y_dil, "mix"), GATHER_MIX))
    mlp_shard, mlp_relay = _gather_halves_relay(mlp_across, y_dil, "mlp")
    u_na = _mm("branch_na", y_na, W["w_branch_na"], "nn", 1024,1024, 512, [BF], after=(mlp_relay.token,))
    def gate_mix(acc, gn, gd, un):
        ud = acc.astype(BF)
        return ud, _sigmoid(gn.astype(F32)) * un.astype(F32) + _sigmoid(gd.astype(F32)) * ud.astype(F32)

    u_dil, mixed = _mm("branch_dil", y_dil, W["w_branch_dil"], "nn", 512, 1024, 256, [BF, BF], epilogue=gate_mix,
                       extras=((z_gates, 0), (z_gates, 1), u_na))

    def add_norm(d, h, g):
        h = h + d
        return h, h * _rms(h) * g

    h1, cn = _mm("out_proj", mixed, W["w_out"], "nn", 512, 1024, 1024, [F32, BF], epilogue=add_norm, extras=(xs,), consts=(gl,))
    mlp_all = _gather_halves_finish(mlp_shard, mlp_relay, cn, "mlp")
    W.update({n: t for n, t in _unpack_weights(mlp_all, GATHER_MLP).items() if n.startswith("w_ple")})
    chip_block = (None, D_MODEL, PACK_W)
    up, act = _mm("mlp_up", cn, mlp_all, "nn", 1024,1024, 1024, [BF, BF],
                  epilogue=lambda acc: (acc, jnp.square(jnp.maximum(acc, 0.0))), b_view=(D_FF, chip_block, lambda j, k: (j, 0, 0)))
    h2, en = _mm("mlp_down", act, mlp_all, "nn", 1024, 1024, 1024, [F32, BF], epilogue=add_norm, extras=(h1,), consts=(gp,),
                 b_view=(D_MODEL, chip_block, lambda j, k: (k, 1, 0)))
    pp = _mm("ple_proj", ps, W["w_ple_proj"], "nn", 1024,1024, 256, [F32])

    def head(gtt, h2t, ppt, tg, g):
        sg = _sigmoid(gtt)
        h3 = h2t + sg * ppt
        yo = h3 * _rms(h3) * g
        diff = yo - tg
        loss = 0.5 * jnp.sum(jnp.mean(jnp.square(diff), axis=-1, keepdims=True), axis=0, keepdims=True)
        dh3, dg = _rms_bwd(diff * (1.0 / D_MODEL), h3, g)
        return dh3, dh3 * ppt * sg * (1.0 - sg), dh3 * sg, jnp.broadcast_to(loss, (1, 128)), dg

    dh3, d_gt, d_pp, loss_part, dg_final = _mm(
        "ple_gate_loss_head", en, W["w_ple_gate"], "nn", 512, 1024, 1024, [F32, BF, BF], epilogue=head,
        extras=(h2, pp, tgt), consts=(gf,), sums=[128, D_MODEL])

    early_shapes = {n: shards[n].shape for n in REDUCE_EARLY}
    early_rows = sum(r * c for r, c in early_shapes.values()) // PACK_W
    shard_rows = D_MODEL // N_CHIPS
    early_buf = _mm("g_ple_gate", en, d_gt, "tn", 1024, 1024, 2048, [F32],
                    into=(jax.ShapeDtypeStruct((N_CHIPS, early_rows, PACK_W), F32), (N_CHIPS, shard_rows, PACK_W),
                          lambda i, j: (0, 2 * D_MODEL // shard_rows, 0)))
    g_ple_proj = _mm("g_ple_proj", ps, d_pp, "tn", 256, 1024, 1024,[F32])

    def add_norm_bwd(dn, dh_out, h, g):
        dh, dg = _rms_bwd(dn, h, g)
        dh = dh_out + dh
        return dh, dh, dg

    dh2, dh2_b, dg_ple = _mm("d_ple_gate", d_gt, W["w_ple_gate"], "nt", 512, 1024, 1024, [F32, BF],
                             epilogue=add_norm_bwd, extras=(dh3, h2), consts=(gp,), sums=[D_MODEL])
    d_up = _mm("d_mlp_down", dh2_b, mlp_all, "nt", 1024,1024, 1024, [BF], b_view=(D_FF, chip_block, lambda j, k: (j, 1, 0)),
               epilogue=lambda acc, u: (acc * (2.0 * jnp.maximum(u.astype(F32), 0.0)),), extras=(up,))
    early_buf = _mm("g_mlp_down", act, dh2_b, "tn", 1024, 1024, 2048, [F32],
                    into=(early_buf, (None, D_MODEL, PACK_W), lambda i, j: (i, 1, 0)))
    early_buf = _mm("g_mlp_up", cn, d_up, "tn", 1024, 1024, 2048, [F32],
                    into=(early_buf, (None, D_MODEL, PACK_W), lambda i, j: (j, 0, 0)))
    dh1, dh1_b, dg_mlp = _mm("d_mlp_up", d_up, mlp_all, "nt", 1024, 1024, 1024, [F32, BF], epilogue=add_norm_bwd,
                             b_view=(D_MODEL, chip_block, lambda j, k: (k, 0, 0)),
                             extras=(dh2, h1), consts=(gl,), sums=[D_MODEL])
    early_buf = _mm("g_out_proj", mixed, dh1_b, "tn", 1024, 1024, 2048, [F32],
                    into=(early_buf, (N_CHIPS, shard_rows, PACK_W), lambda i, j: (0, 2 * D_MODEL // shard_rows + 1, 0)))

    def gate_bwd(dm, gn, gd, un, ud):
        gn, gd, un, ud = (t.astype(F32) for t in (gn, gd, un, ud))
        sn, sd = _sigmoid(gn), _sigmoid(gd)
        return jnp.concatenate([dm * un * sn * (1.0 - sn), dm * ud * sd * (1.0 - sd)], axis=1), dm * sn, dm * sd

    dz_gates, d_u_na, d_u_dil = _mm("d_out_proj", dh1_b, W["w_out"], "nt", 512, 1024, 1024, [(BF, 2 * D_MODEL), BF, BF],
                                    epilogue=gate_bwd, extras=((z_gates, 0), (z_gates, 1), u_na, u_dil))
    g_branch_na = _mm("g_branch_na", y_na, d_u_na, "tn", 1024, 1024, 1024,[F32])
    g_branch_dil = _mm("g_branch_dil", y_dil, d_u_dil, "tn", 256, 1024, 1024,[F32])
    small_rows = [jnp.concatenate([_pack_rows(g[:, s * shard_rows:(s + 1) * shard_rows]) for g in (g_ple_proj, g_branch_na, g_branch_dil)],
                                  axis=0) for s in range(N_CHIPS)]
    early_buf = lax.dynamic_update_slice(early_buf, jnp.stack(small_rows), (0, 2 * D_MODEL + 2 * shard_rows, 0))
    early_tm = early_rows // 4
    swap_flight = _swap_start(early_buf, "early")
    d_y_na = _mm("d_branch_na", d_u_na, W["w_branch_na"], "nt", 1024,512, 1024, [BF], after=(swap_flight.token,))
    d_y_dil = _mm("d_branch_dil", d_u_dil, W["w_branch_dil"], "nt", 1024,256, 1024, [F32])

    dqa, dka, dva, dtab = _na_bwd(qkv, tab, d_y_na)
    early_g, early_got = _swap_wait(swap_flight, dqa, "early")
    early_pair, early_pair_b = _pair_sum(early_g, early_got, early_tm)
    scatter_flight = _scatter_start(early_pair_b, "early")

    do_res, dlse_res = _dil_merge_bwd(d_y_dil, o_nat, w_grp, TM, after=(scatter_flight.token,))
    d_dil = [_band_bwd(*dil_ops[g], do_res[g], dlse_res[g], g) for g in range(len(DIL_GROUPS))]

    dz_qkv = _qkv_unprep((dqa, dka, dva), d_dil, cos2, sin_signed, TM)
    in_cols = shards["w_in"].shape[1]
    qkv_rows, gate_tm = dz_qkv.shape[1], 256
    assert qkv_rows % gate_tm == 0
    g_in = _mm("g_in_qkv", dz_qkv, a, "tn", 1280, 1024, 2048, [F32],
               into=(jax.ShapeDtypeStruct((N_CHIPS * in_cols, D_MODEL), F32), (1280, D_MODEL), lambda i, j: (i, 0)))
    g_in = _mm("g_in_gates", dz_gates, a, "tn", gate_tm, 1024, T, [F32],
               into=(g_in, (gate_tm, D_MODEL), lambda i, j: (qkv_rows // gate_tm + i, 0)))
    early_mine = _chip_sum(early_pair, _scatter_wait(scatter_flight, (g_in,), "early"), early_tm)
    join_flight = _join_start(early_mine)

    late_tm = in_cols // 4
    late_swap = _swap_start(g_in.reshape(N_CHIPS, in_cols, D_MODEL), "late")
    d_a = _mm("d_in_qkv", dz_qkv, W["w_in"], "nt", 1024,1024, 1920, [F32], after=(late_swap.token, join_flight.token),
              b_view=(D_MODEL, (D_MODEL, 1920), lambda j, k: (j, k)))
    late_g, late_got = _swap_wait(late_swap, d_a, "late")
    late_pair, late_pair_b = _pair_sum(late_g, late_got, late_tm)
    late_scatter = _scatter_start(late_pair_b, "late")
    d_rpb = _na_rpb_grad(dtab, after=(late_scatter.token,))[:, :2 * NA_WIN_ROWS - 1, :2 * NA_WIN_COLS - 1]
    def first_bwd(dn_gates, dn_qkv, dh_out, h, g):
        dh, dg = _rms_bwd(dn_gates + dn_qkv, h, g)
        return dh_out + dh, dg

    grad_x, dg_mix = _mm("d_in_gates", dz_gates, w_gates, "nt", 512, 1024, 2048, [F32], epilogue=first_bwd,
                         extras=(d_a, dh1, xs), consts=(gm,), sums=[D_MODEL], after=(late_scatter.token,))
    early_shard = _join_wait(join_flight, grad_x)

    n_rpb = rpb.size
    rpb_rows = 4
    small = jnp.concatenate([
        dg_mix, dg_mlp, dg_ple, dg_final,
        jnp.pad(d_rpb.reshape(-1), (0, rpb_rows * D_MODEL - n_rpb)).reshape(rpb_rows, D_MODEL),
        jnp.pad(loss_part, ((0, 0), (0, D_MODEL - loss_part.shape[1]))),
        jnp.zeros((SMALL_ROWS - 5 - rpb_rows, D_MODEL), F32)], axis=0)
    out = {"grad": {}, "delta": {}, "new_m": {}, "new_v": {}}

    def update(n, g, row0):
        res = _adamw("adamw_" + n, g, row0, params[n], params["m_" + n], params["v_" + n])
        for kind, t in zip(("grad", "delta", "new_m", "new_v"), res, strict=True):
            out[kind][n] = t

    row0 = 0
    for n in REDUCE_EARLY:
        rows, cols = early_shapes[n]
        n_rows = rows * cols // PACK_W
        if cols == PACK_W:
            update(n, early_shard, row0)
        else:
            update(n, early_shard[row0:row0 + n_rows].reshape(rows, cols), 0)
        row0 += n_rows
    late_others = _scatter_wait(late_scatter, (*[out["new_v"][n] for n in REDUCE_EARLY], d_rpb), "late")
    late_mine = _chip_sum(late_pair, late_others, late_tm)
    small_flight = _allreduce_start(small, after=(late_mine,))
    res = _adamw("adamw_w_in", _join_halves(late_mine, after=(small_flight.token,)), 0,
                 *[jnp.swapaxes(params[n], 1, 2) for n in ("w_in", "m_w_in", "v_w_in")])
    small = _allreduce_finish(small_flight, res[3])
    for kind, t in zip(("grad", "delta", "new_m", "new_v"), res, strict=True):
        out[kind]["w_in"] = jnp.swapaxes(t, 1, 2)
    loss = small[4 + rpb_rows, 0]

    def small_pack(a0, a1, a2, a3, r):
        return jnp.concatenate([a0.reshape(1, -1), a1.reshape(1, -1), a2.reshape(1, -1), a3.reshape(1, -1),
                                jnp.pad(r.reshape(-1), (0, rpb_rows * D_MODEL - n_rpb)).reshape(rpb_rows, D_MODEL)], axis=0)

    small_res = _adamw("adamw_small", small, 0, small_pack(g_mix, g_mlp, g_ple, g_final, rpb)[None],
                       small_pack(m_g_mix, m_g_mlp, m_g_ple, m_g_final, m_rpb)[None],
                       small_pack(v_g_mix, v_g_mlp, v_g_ple, v_g_final, v_rpb)[None])

    def small_unpack(t):
        return {"g_mix": t[0].reshape(g_mix.shape), "g_mlp": t[1].reshape(g_mlp.shape), "g_ple": t[2].reshape(g_ple.shape),
                "g_final": t[3].reshape(g_final.shape), "rpb": t[4:].reshape(-1)[:n_rpb].reshape(rpb.shape)}

    for kind, t in zip(("grad", "delta", "new_m", "new_v"), small_res, strict=True):
        out[kind].update(small_unpack(t[0]))

    order = ["g_mix", "w_in", "rpb", "w_branch_na", "w_branch_dil", "w_out", "g_mlp", "w_up", "w_down", "g_ple",
             "w_ple_gate", "w_ple_proj", "g_final"]
    return (loss, grad_x[None], *[out["grad"][n] for n in order], *[out["delta"][n] for n in order],
            *[out["new_m"][n] for n in order], *[out["new_v"][n] for n in order])
```

```python
import functools
from typing import NamedTuple

import jax
import jax.numpy as jnp
from jax import lax
from jax.experimental import pallas as pl
from jax.experimental.pallas import tpu as pltpu

BF = jnp.bfloat16
F32 = jnp.float32
MESH = pl.DeviceIdType.MESH
ANY = pl.BlockSpec(memory_space=pl.ANY)

V7X_VMEM_BYTES = 64 * 1024 * 1024
VMEM_LIMIT = V7X_VMEM_BYTES - 16 * 1024 * 1024

D_MODEL = 1024
HEAD_DIM = 64
GRID_W = 64
NA_HEADS = 8
NA_WIN_ROWS = 8
NA_WIN_COLS = 16
NA_WIDTH = NA_HEADS * HEAD_DIM
DIL_GROUPS = ((128, 1), (512, 4), (2048, 16))
DIL_HPG = 4
DIL_HEADS = DIL_HPG * len(DIL_GROUPS)
DIL_WIDTH = DIL_HEADS * HEAD_DIM
DIL_OUT_WIDTH = DIL_HPG * HEAD_DIM
DIL_RADIUS = 64
QKV_WIDTH = 3 * NA_WIDTH + 3 * DIL_WIDTH
D_FF = 4 * D_MODEL
PLE_DIM = 256
ROPE_THETA = 10000.0
RMS_EPS = 1e-6
NEG_INF = -1e30
Q_SCALE = HEAD_DIM ** -0.5

ADAM_LR = 0.001
ADAM_B1 = 0.9
ADAM_B2 = 0.999
ADAM_EPS = 1e-08
ADAM_WD = 0.01
ADAM_STEP = 10

N_CHIPS = 4
N_DEV = 8
PACK_W = 1024
GATHER_MIX = ("w_branch_na", "w_branch_dil", "w_out")
GATHER_MLP = ("w_up", "w_down", "w_ple_gate", "w_ple_proj")
REDUCE_EARLY = ("w_up", "w_down", "w_ple_gate", "w_out", "w_ple_proj", "w_branch_na", "w_branch_dil")
SMALL_ROWS = 16


def _cparams(sem=None):
    return pltpu.CompilerParams(dimension_semantics=sem, vmem_limit_bytes=VMEM_LIMIT)


def _mm(name, a, b, mode, tm, tn, tk, out_dtypes, epilogue=None, extras=(), consts=(), sums=(), after=(), into=None,
        b_view=None):
    if mode == "nn":
        (M, K), N = a.shape, b.shape[1]
    elif mode == "nt":
        (M, K), N = a.shape, b.shape[0]
    else:
        (K, M), N = a.shape, b.shape[1]
    if b_view is not None:
        N = b_view[0]
    tm, tn, tk = min(tm, M), min(tn, N), min(tk, K)
    assert M % tm == 0 and N % tn == 0 and K % tk == 0, (name, M, N, K, tm, tn, tk)
    if mode == "nn":
        a_spec = pl.BlockSpec((tm, tk), lambda i, j, k: (i, k))
        b_spec = pl.BlockSpec((tk, tn), lambda i, j, k: (k, j))
        dims = (((1,), (0,)), ((), ()))
    elif mode == "nt":
        a_spec = pl.BlockSpec((tm, tk), lambda i, j, k: (i, k))
        b_spec = pl.BlockSpec((tn, tk), lambda i, j, k: (j, k))
        dims = (((1,), (1,)), ((), ()))
    else:
        a_spec = pl.BlockSpec((tk, tm), lambda i, j, k: (k, i))
        b_spec = pl.BlockSpec((tk, tn), lambda i, j, k: (k, j))
        dims = (((0,), (0,)), ((), ()))
    if b_view is not None:
        b_spec = pl.BlockSpec(b_view[1], lambda i, j, k: b_view[2](j, k))
    nk = K // tk
    n_extra, n_const, n_out, n_sum = len(extras), len(consts), len(out_dtypes), len(sums)
    tile = pl.BlockSpec((tm, tn), lambda i, j, k: (i, j))
    assert not sums or tn == N, "row sums need whole rows in a tile"
    wide = [e for e in (*extras, *out_dtypes) if isinstance(e, tuple)]
    assert not wide or tn == N
    extra_specs = [pl.BlockSpec((tm, tn), functools.partial(lambda c, i, j, k: (i, c), e[1])) if isinstance(e, tuple) else tile
                   for e in extras]
    extras = [e[0] if isinstance(e, tuple) else e for e in extras]
    out_widths = [d[1] if isinstance(d, tuple) else N for d in out_dtypes]
    out_dtypes = [d[0] if isinstance(d, tuple) else d for d in out_dtypes]

    n_after = len(after)

    def body(a_ref, b_ref, *rest):
        extra_refs, rest = rest[:n_extra + n_const], rest[n_extra + n_const + n_after:]
        out_refs, sum_refs, acc = rest[:n_out], rest[n_out:n_out + n_sum], rest[-1]
        i, k = pl.program_id(0), pl.program_id(2)
        def product():
            if len(b_ref.shape) == 3:
                w = tk // b_ref.shape[0]
                parts = [lax.dot_general(a_ref[:, s * w:(s + 1) * w].astype(BF), b_ref[s].astype(BF), dims, preferred_element_type=F32)
                         for s in range(b_ref.shape[0])]
                return functools.reduce(lambda x, y: x + y, parts)
            return lax.dot_general(a_ref[...].astype(BF), b_ref[...].astype(BF), dims, preferred_element_type=F32)

        if nk > 1:
            @pl.when(k == 0)
            def _():
                acc[...] = jnp.zeros_like(acc)

            acc[...] += product()

        @pl.when(k == nk - 1)
        def _():
            total = product() if nk == 1 else acc[...]
            outs = (total,) if epilogue is None else epilogue(total, *[e[...] for e in extra_refs])
            for o_ref, val in zip(out_refs, outs[:n_out], strict=True):
                o_ref[...] = val.astype(o_ref.dtype).reshape(o_ref.shape)
            for s_ref, val in zip(sum_refs, outs[n_out:], strict=True):
                @pl.when(i == 0)
                def _():
                    s_ref[...] = val

                @pl.when(i != 0)
                def _():
                    s_ref[...] += val

    out_specs = ([tile if w == N else pl.BlockSpec((tm, w), lambda i, j, k: (i, 0)) for w in out_widths]
                 + [pl.BlockSpec((1, c), lambda i, j, k: (0, 0)) for c in sums])
    out_shape = ([jax.ShapeDtypeStruct((M, w), dt) for dt, w in zip(out_dtypes, out_widths, strict=True)]
                 + [jax.ShapeDtypeStruct((1, c), F32) for c in sums])
    operands, aliases = [a, b, *extras, *consts, *after], {}
    in_specs = ([a_spec, b_spec] + extra_specs
                + [pl.BlockSpec(c.shape, functools.partial(lambda nd, i, j, k: (0,) * nd, c.ndim)) for c in consts] + [ANY] * n_after)
    if into is not None:
        assert n_out == 1
        target, block, index = into
        out_specs = [pl.BlockSpec(block, lambda i, j, k: index(i, j))]
        out_shape = [jax.ShapeDtypeStruct(target.shape, target.dtype)]
        if not isinstance(target, jax.ShapeDtypeStruct):
            aliases = {len(operands): 0}
            operands.append(target)
            in_specs.append(ANY)
            n_after += 1

    outs = pl.pallas_call(
        body, name=name, grid=(M // tm, N // tn, nk),
        in_specs=in_specs, out_specs=out_specs, out_shape=out_shape,
        scratch_shapes=[pltpu.VMEM((tm, tn) if nk > 1 else (8, 128), F32)], input_output_aliases=aliases,
        compiler_params=_cparams(("arbitrary",) * 3 if sums else ("parallel", "parallel", "arbitrary")),
    )(*operands)
    return outs[0] if len(outs) == 1 else outs


def _row(arr, tm, col_block=None, width=None):
    width = arr.shape[1] if width is None else width
    cb = 0 if col_block is None else col_block
    return arr, pl.BlockSpec((tm, width), lambda i: (i, cb))


def _full(arr):
    nd = arr.ndim
    return arr, pl.BlockSpec(arr.shape, lambda i: (0,) * nd)


def _rowwise(name, body, T, tm, ins, outs, sums=(), after=()):
    n_in, n_out, n_sum, n_after = len(ins), len(outs), len(sums), len(after)

    def kern(*refs):
        in_refs, refs = refs[:n_in], refs[n_in + n_after:]
        out_refs, sum_refs = refs[:n_out], refs[n_out:]
        res = body(*[r[...] for r in in_refs])
        res = res if isinstance(res, tuple) else (res,)
        for o_ref, val in zip(out_refs, res[:n_out], strict=True):
            o_ref[...] = val.astype(o_ref.dtype)
        if n_sum:
            @pl.when(pl.program_id(0) == 0)
            def _():
                for s_ref in sum_refs:
                    s_ref[...] = jnp.zeros_like(s_ref)

            for s_ref, val in zip(sum_refs, res[n_out:], strict=True):
                s_ref[...] += val

    res = pl.pallas_call(
        kern, name=name, grid=(T // tm,),
        in_specs=[spec for _, spec in ins] + [ANY] * n_after,
        out_specs=[pl.BlockSpec((tm, c), lambda i: (i, 0)) for c, _ in outs]
        + [pl.BlockSpec((1, c), lambda i: (0, 0)) for c in sums],
        out_shape=[jax.ShapeDtypeStruct((T, c), dt) for c, dt in outs]
        + [jax.ShapeDtypeStruct((1, c), F32) for c in sums],
        compiler_params=_cparams(("arbitrary",)),
    )(*[a for a, _ in ins], *after)
    return res[0] if len(res) == 1 else res


def _sigmoid(x):
    return 1.0 / (1.0 + jnp.exp(-x))


def _rms(h):
    return lax.rsqrt(jnp.mean(h * h, axis=-1, keepdims=True) + RMS_EPS)


def _rms_bwd(dy, h, g):
    r = _rms(h)
    n = h * r
    dn = dy * g
    dh = r * (dn - n * jnp.mean(dn * n, axis=-1, keepdims=True))
    return dh, jnp.sum(dy * n, axis=0, keepdims=True)


def _rope(x, cos2, sin_signed):
    lane = lax.broadcasted_iota(jnp.int32, x.shape, 1)
    swapped = jnp.where((lane % HEAD_DIM) < HEAD_DIM // 2, pltpu.roll(x, 128 - HEAD_DIM // 2, 1), pltpu.roll(x, HEAD_DIM // 2, 1))
    return x * cos2 + swapped * sin_signed


NA_KEYS = NA_WIN_ROWS * GRID_W
NA_BASES = 8


def _na_row_geometry(r, rows):
    first = jnp.clip(r - NA_WIN_ROWS // 2, 0, rows - NA_WIN_ROWS)
    base = first - r + (NA_WIN_ROWS - 1)
    return pl.multiple_of(first * GRID_W, GRID_W), base


NA_ROWS_PER_STEP = 16
NA_BWD_ROWS_PER_STEP = 8


def _softmax_rows(s):
    p = jnp.exp(s - jnp.max(s, axis=-1, keepdims=True))
    return p / jnp.sum(p, axis=-1, keepdims=True)


def _split_pair(t):
    first = lax.broadcasted_iota(jnp.int32, t.shape, 1) < HEAD_DIM
    zero = jnp.zeros_like(t)
    return jnp.where(first, t, zero), jnp.where(first, zero, t)


def _join_pair(a, b):
    return jnp.where(lax.broadcasted_iota(jnp.int32, a.shape, 1) < HEAD_DIM, a, b)


_NT = (((1,), (1,)), ((), ()))
_TN = (((0,), (0,)), ((), ()))


def _na_fwd(qkv, tab):
    T = qkv.shape[0]
    rows = T // GRID_W
    n_pairs = NA_WIDTH // 128

    def body(q_ref, k_ref, v_ref, tab_ref, y_ref):
        def step(it, carry):
            geo = [_na_row_geometry(it * NA_ROWS_PER_STEP + u, rows) for u in range(NA_ROWS_PER_STEP)]
            q0s = [pl.multiple_of((it * NA_ROWS_PER_STEP + u) * GRID_W, GRID_W) for u in range(NA_ROWS_PER_STEP)]
            ss = [lax.dot_general(jnp.concatenate(_split_pair(q_ref[pl.ds(q0, GRID_W), :] * Q_SCALE), axis=0),
                                  k_ref[pl.ds(k0, NA_KEYS), :], _NT, preferred_element_type=F32)
                  for q0, (k0, _) in zip(q0s, geo)]
            ps = [_softmax_rows(s + jnp.concatenate([tab_ref[0, base], tab_ref[1, base]], axis=0)) for s, (_, base) in zip(ss, geo)]
            ys = [jnp.dot(p.astype(BF), v_ref[pl.ds(k0, NA_KEYS), :], preferred_element_type=F32) for p, (k0, _) in zip(ps, geo)]
            for q0, y2 in zip(q0s, ys):
                y_ref[pl.ds(q0, GRID_W), :] = _join_pair(y2[:GRID_W], y2[GRID_W:]).astype(y_ref.dtype)
            return carry

        lax.fori_loop(0, rows // NA_ROWS_PER_STEP, step, 0)

    def cols(first):
        return pl.BlockSpec((T, 128), lambda j: (0, first + j))

    return pl.pallas_call(
        body, name="na_fwd", grid=(n_pairs,),
        in_specs=[cols(0), cols(n_pairs), cols(2 * n_pairs), pl.BlockSpec((2, NA_BASES, GRID_W, NA_KEYS), lambda j: (j, 0, 0, 0))],
        out_specs=cols(0), out_shape=jax.ShapeDtypeStruct((T, NA_WIDTH), BF),
        compiler_params=_cparams(("parallel",)),
    )(qkv, qkv, qkv, tab)


def _na_bwd(qkv, tab, do):
    T = qkv.shape[0]
    rows = T // GRID_W
    n_pairs = NA_WIDTH // 128

    def body(q_ref, k_ref, v_ref, tab_ref, do_ref, dq_ref, dk_out, dv_out, dtab_ref, dk_ref, dv_ref):
        dk_ref[...] = jnp.zeros_like(dk_ref)
        dv_ref[...] = jnp.zeros_like(dv_ref)
        dtab_ref[...] = jnp.zeros_like(dtab_ref)

        def step(it, carry):
            U = NA_BWD_ROWS_PER_STEP
            geo = [_na_row_geometry(it * U + u, rows) for u in range(U)]
            q0s = [pl.multiple_of((it * U + u) * GRID_W, GRID_W) for u in range(U)]
            q2s = [jnp.concatenate(_split_pair(q_ref[pl.ds(q0, GRID_W), :] * Q_SCALE), axis=0) for q0 in q0s]
            do2s = [jnp.concatenate(_split_pair(do_ref[pl.ds(q0, GRID_W), :]), axis=0) for q0 in q0s]
            ss = [lax.dot_general(q2, k_ref[pl.ds(k0, NA_KEYS), :], _NT, preferred_element_type=F32) for q2, (k0, _) in zip(q2s, geo)]
            dps = [lax.dot_general(do2, v_ref[pl.ds(k0, NA_KEYS), :], _NT, preferred_element_type=F32) for do2, (k0, _) in zip(do2s, geo)]
            ps = [_softmax_rows(s + jnp.concatenate([tab_ref[0, base], tab_ref[1, base]], axis=0)) for s, (_, base) in zip(ss, geo)]
            dss = [p * (dp - jnp.sum(dp * p, axis=-1, keepdims=True)) for p, dp in zip(ps, dps)]
            dvs = [lax.dot_general(p.astype(BF), do2, _TN, preferred_element_type=F32) for p, do2 in zip(ps, do2s)]
            dsbs = [ds.astype(BF) for ds in dss]
            dqs = [jnp.dot(dsb, k_ref[pl.ds(k0, NA_KEYS), :], preferred_element_type=F32) for dsb, (k0, _) in zip(dsbs, geo)]
            dks = [lax.dot_general(dsb, q2, _TN, preferred_element_type=F32) for dsb, q2 in zip(dsbs, q2s)]
            for u in range(U):
                k0, base = geo[u]
                dtab_ref[0, base] += dss[u][:GRID_W]
                dtab_ref[1, base] += dss[u][GRID_W:]
                dq_ref[pl.ds(q0s[u], GRID_W), :] = (_join_pair(dqs[u][:GRID_W], dqs[u][GRID_W:]) * Q_SCALE).astype(dq_ref.dtype)
                dk_ref[pl.ds(k0, NA_KEYS), :] += dks[u]
                dv_ref[pl.ds(k0, NA_KEYS), :] += dvs[u]
            return carry

        lax.fori_loop(0, rows // NA_BWD_ROWS_PER_STEP, step, 0)
        dk_out[...] = dk_ref[...].astype(dk_out.dtype)
        dv_out[...] = dv_ref[...].astype(dv_out.dtype)

    def cols(first):
        return pl.BlockSpec((T, 128), lambda j: (0, first + j))

    tabs = pl.BlockSpec((2, NA_BASES, GRID_W, NA_KEYS), lambda j: (j, 0, 0, 0))
    wide = jax.ShapeDtypeStruct((T, NA_WIDTH), BF)
    return pl.pallas_call(
        body, name="na_bwd", grid=(n_pairs,),
        in_specs=[cols(0), cols(n_pairs), cols(2 * n_pairs), tabs, cols(0)],
        out_specs=[cols(0), cols(0), cols(0), tabs],
        out_shape=[wide, wide, wide, jax.ShapeDtypeStruct((NA_HEADS, NA_BASES, GRID_W, NA_KEYS), F32)],
        scratch_shapes=[pltpu.VMEM((T, 128), F32), pltpu.VMEM((T, 128), F32)],
        compiler_params=_cparams(("parallel",)),
    )(qkv, qkv, qkv, tab, do)


def _na_bias_table(rpb):
    H, n_rows, n_cols = rpb.shape

    def body(r_ref, tab_ref):
        q = lax.broadcasted_iota(jnp.int32, (GRID_W, 128), 0)
        kc = lax.broadcasted_iota(jnp.int32, (GRID_W, 128), 1)
        first = jnp.clip(q - NA_WIN_COLS // 2, 0, GRID_W - NA_WIN_COLS)
        valid = (kc >= first) & (kc < first + NA_WIN_COLS)
        toeplitz = []
        for ro in range(n_rows):
            row = jnp.broadcast_to(r_ref[pl.ds(ro, 1), :], (GRID_W, 128))
            shifted = pltpu.roll(pltpu.roll(row, 128 - (NA_WIN_COLS - 1), 1), 0, 1, stride=1, stride_axis=0)
            toeplitz.append(jnp.where(valid, shifted, NEG_INF))
        for base in range(NA_BASES):
            for j in range(NA_WIN_ROWS // 2):
                even, odd = toeplitz[base + 2 * j], toeplitz[base + 2 * j + 1]
                tab_ref[base, :, pl.ds(j * 128, 128)] = jnp.where(kc < GRID_W, even, pltpu.roll(odd, GRID_W, 1))

    padded = jnp.pad(rpb, ((0, 0), (0, 16 - n_rows), (0, 128 - n_cols)))
    return pl.pallas_call(
        body, name="na_bias_table", grid=(H,),
        in_specs=[pl.BlockSpec((None, 16, 128), lambda h: (h, 0, 0))],
        out_specs=pl.BlockSpec((None, NA_BASES, GRID_W, NA_KEYS), lambda h: (h, 0, 0, 0)),
        out_shape=jax.ShapeDtypeStruct((H, NA_BASES, GRID_W, NA_KEYS), F32),
        compiler_params=_cparams(("parallel",)),
    )(padded)


def _na_rpb_grad(dtab, after=()):
    H = dtab.shape[0]
    n_rows = 2 * NA_WIN_ROWS - 1
    n_cols = 2 * NA_WIN_COLS - 1

    def body(d_ref, *rest):
        o_ref = rest[-1]
        lane = lax.broadcasted_iota(jnp.int32, (GRID_W, 128), 1)
        low = lane < GRID_W
        flip = (lax.broadcasted_iota(jnp.int32, (GRID_W, GRID_W), 0) + lax.broadcasted_iota(jnp.int32, (GRID_W, GRID_W), 1)
                == GRID_W - 1).astype(BF)

        def reverse_rows(t):
            out = jnp.zeros_like(t)
            for _ in range(3):
                piece = t.astype(BF)
                out = out + jnp.dot(flip, piece, preferred_element_type=F32)
                t = t - piece.astype(F32)
            return out

        out_rows = []
        for ro in range(n_rows):
            acc = jnp.zeros((GRID_W, 128), F32)
            for base in range(NA_BASES):
                i = ro - base
                if not 0 <= i < NA_WIN_ROWS:
                    continue
                pair = d_ref[base, :, pl.ds((i // 2) * 128, 128)]
                if i % 2:
                    pair = pltpu.roll(pair, GRID_W, 1)
                acc = acc + jnp.where(low, pair, 0.0)
            skew = pltpu.roll(reverse_rows(acc), 0, 1, stride=1, stride_axis=0)
            diag = jnp.sum(skew, axis=0, keepdims=True)
            out_rows.append(pltpu.roll(jnp.broadcast_to(diag, (8, 128)), 128 - (GRID_W - NA_WIN_COLS), 1)[:1])
        out_rows.append(jnp.zeros((1, 128), F32))
        res = jnp.concatenate(out_rows, axis=0)
        o_ref[...] = jnp.where(lax.broadcasted_iota(jnp.int32, res.shape, 1) < n_cols, res, 0.0)

    return pl.pallas_call(
        body, name="na_rpb_grad", grid=(H,),
        in_specs=[pl.BlockSpec((None, NA_BASES, GRID_W, NA_KEYS), lambda h: (h, 0, 0, 0))] + [ANY] * len(after),
        out_specs=pl.BlockSpec((None, n_rows + 1, 128), lambda h: (h, 0, 0)),
        out_shape=jax.ShapeDtypeStruct((H, n_rows + 1, 128), F32),
        compiler_params=_cparams(("parallel",)),
    )(dtab, *after)


BAND_Q = 128
BAND_KEYS = BAND_Q + 2 * DIL_RADIUS


def _band_geometry(n, L):
    q0 = pl.multiple_of(n * BAND_Q, BAND_Q)
    k0 = pl.multiple_of(jnp.clip(q0 - DIL_RADIUS, 0, L - BAND_KEYS), DIL_RADIUS)
    qi = q0 + lax.broadcasted_iota(jnp.int32, (BAND_Q, BAND_KEYS), 0)
    kj = k0 + lax.broadcasted_iota(jnp.int32, (BAND_Q, BAND_KEYS), 1)
    return q0, k0, jnp.abs(qi - kj) <= DIL_RADIUS


DIL_PAIRS = DIL_OUT_WIDTH // 128


def _residue_shape(dil, T, dtype):
    return jax.ShapeDtypeStruct((DIL_PAIRS, dil, T // dil, 128), dtype)


def _residue_tile(dil, tm):
    return pl.BlockSpec((DIL_PAIRS, dil, tm // dil, 128), lambda i: (0, 0, i, 0))


def _to_natural(ref, scratch, dil, tm):
    tiles = []
    for pair in range(DIL_PAIRS):
        if dil == 1:
            tiles.append(ref[pair, 0].astype(F32))
            continue
        for r in range(dil):
            scratch[pl.ds(r, tm // dil, stride=dil), :] = ref[pair, r].astype(F32)
        tiles.append(scratch[...])
    return tiles


def _from_natural(tile, scratch, ref, pair, dil, tm):
    if dil == 1:
        ref[pair, 0] = tile.astype(ref.dtype)
        return
    scratch[...] = tile
    for r in range(dil):
        ref[pair, r] = scratch[pl.ds(r, tm // dil, stride=dil), :].astype(ref.dtype)


def _band_specs(group, T):
    dil = DIL_GROUPS[group][1]
    L = T // dil
    assert L % BAND_Q == 0 and L >= BAND_KEYS, (T, dil)
    per_residue = min(BAND_BLOCKS_PER_STEP, L // BAND_Q)
    residues = min(dil, BAND_BLOCKS_PER_STEP // per_residue)
    spec = pl.BlockSpec((None, residues, L, 128), lambda s: (s % DIL_PAIRS, s // DIL_PAIRS, 0, 0))
    return L, residues, per_residue, (dil // residues * DIL_PAIRS,), spec


BAND_BLOCKS_PER_STEP = 8


def _band_softmax(s, valid):
    s = jnp.where(valid, s, NEG_INF)
    m = jnp.max(s, axis=-1, keepdims=True)
    p = jnp.exp(s - m)
    l = jnp.sum(p, axis=-1, keepdims=True)
    return p / l, m + jnp.log(l)


def _band_fwd(q, k, v, group):
    T = q.shape[1] * q.shape[2]
    L, residues, U, grid, spec = _band_specs(group, T)

    def body(q_ref, k_ref, v_ref, o_ref, lse_ref):
        def step(it, carry):
            geo = [(r, *_band_geometry(it * U + u, L)) for r in range(residues) for u in range(U)]
            ss = [lax.dot_general(jnp.concatenate(_split_pair(q_ref[r, pl.ds(q0, BAND_Q), :]), axis=0),
                                  k_ref[r, pl.ds(k0, BAND_KEYS), :], _NT, preferred_element_type=F32) for r, q0, k0, _ in geo]
            pls = [_band_softmax(s, jnp.concatenate([valid, valid], axis=0)) for s, (_, _, _, valid) in zip(ss, geo)]
            os = [jnp.dot(p.astype(BF), v_ref[r, pl.ds(k0, BAND_KEYS), :], preferred_element_type=F32)
                  for (p, _), (r, _, k0, _) in zip(pls, geo)]
            for (r, q0, _, _), o2, (_, lse) in zip(geo, os, pls):
                o_ref[r, pl.ds(q0, BAND_Q), :] = _join_pair(o2[:BAND_Q], o2[BAND_Q:])
                lse2 = jnp.broadcast_to(lse, (2 * BAND_Q, 128))
                lse_ref[r, pl.ds(q0, BAND_Q), :] = _join_pair(lse2[:BAND_Q], lse2[BAND_Q:])
            return carry

        lax.fori_loop(0, L // (BAND_Q * U), step, 0)

    res = _residue_shape(DIL_GROUPS[group][1], T, F32)
    return pl.pallas_call(
        body, name=f"band_fwd_g{group}", grid=grid,
        in_specs=[spec] * 3, out_specs=[spec] * 2, out_shape=[res, res],
        compiler_params=_cparams(("parallel",)),
    )(q, k, v)


def _band_bwd(q, k, v, do, dlse, group):
    T = q.shape[1] * q.shape[2]
    L, residues, U, grid, spec = _band_specs(group, T)

    def body(q_ref, k_ref, v_ref, do_ref, dlse_ref, dq_ref, dk_ref, dv_ref):
        dk_ref[...] = jnp.zeros_like(dk_ref)
        dv_ref[...] = jnp.zeros_like(dv_ref)

        def step(it, carry):
            geo = [(r, *_band_geometry(it * U + u, L)) for r in range(residues) for u in range(U)]
            q2s = [jnp.concatenate(_split_pair(q_ref[r, pl.ds(q0, BAND_Q), :]), axis=0) for r, q0, _, _ in geo]
            do2s = [jnp.concatenate(_split_pair(do_ref[r, pl.ds(q0, BAND_Q), :]), axis=0) for r, q0, _, _ in geo]
            ss = [lax.dot_general(q2, k_ref[r, pl.ds(k0, BAND_KEYS), :], _NT, preferred_element_type=F32)
                  for q2, (r, _, k0, _) in zip(q2s, geo)]
            dps = [lax.dot_general(do2, v_ref[r, pl.ds(k0, BAND_KEYS), :], _NT, preferred_element_type=F32)
                   for do2, (r, _, k0, _) in zip(do2s, geo)]
            ps = [_band_softmax(s, jnp.concatenate([valid, valid], axis=0))[0] for s, (_, _, _, valid) in zip(ss, geo)]
            dss = []
            for p, dp, (r, q0, _, _) in zip(ps, dps, geo):
                dl = dlse_ref[r, pl.ds(q0, BAND_Q), :]
                dl2 = jnp.concatenate([dl[:, :1], dl[:, HEAD_DIM:HEAD_DIM + 1]], axis=0)
                dss.append(p * (dp - jnp.sum(dp * p, axis=-1, keepdims=True) + dl2))
            dvs = [lax.dot_general(p.astype(BF), do2, _TN, preferred_element_type=F32) for p, do2 in zip(ps, do2s)]
            dsbs = [ds.astype(BF) for ds in dss]
            dqs = [jnp.dot(dsb, k_ref[r, pl.ds(k0, BAND_KEYS), :], preferred_element_type=F32) for dsb, (r, _, k0, _) in zip(dsbs, geo)]
            dks = [lax.dot_general(dsb, q2, _TN, preferred_element_type=F32) for dsb, q2 in zip(dsbs, q2s)]
            for u, (r, q0, k0, _) in enumerate(geo):
                dq_ref[r, pl.ds(q0, BAND_Q), :] = _join_pair(dqs[u][:BAND_Q], dqs[u][BAND_Q:])
                dk_ref[r, pl.ds(k0, BAND_KEYS), :] += dks[u]
                dv_ref[r, pl.ds(k0, BAND_KEYS), :] += dvs[u]
            return carry

        lax.fori_loop(0, L // (BAND_Q * U), step, 0)

    res = _residue_shape(DIL_GROUPS[group][1], T, F32)
    return pl.pallas_call(
        body, name=f"band_bwd_g{group}", grid=grid,
        in_specs=[spec] * 5, out_specs=[spec] * 3, out_shape=[res] * 3,
        compiler_params=_cparams(("parallel",)),
    )(q, k, v, do, dlse)


def _head_sums(t):
    head = lax.broadcasted_iota(jnp.int32, t.shape, 1) // HEAD_DIM
    out = jnp.zeros_like(t)
    for h in range(t.shape[1] // HEAD_DIM):
        mine = head == h
        out = jnp.where(mine, jnp.sum(jnp.where(mine, t, 0.0), axis=-1, keepdims=True), out)
    return out


def _dil_merge_fwd(os, lses, T, tm):
    G = len(DIL_GROUPS)
    W = DIL_OUT_WIDTH
    dils = [d for _, d in DIL_GROUPS]

    def body(*refs):
        o_refs, lse_refs = refs[:G], refs[G:2 * G]
        y_ref, w_refs, on_refs, scratch = refs[2 * G], refs[2 * G + 1:3 * G + 1], refs[3 * G + 1:4 * G + 1], refs[-1]
        o = [jnp.concatenate(_to_natural(r, scratch, d, tm), axis=1) for r, d in zip(o_refs, dils)]
        ls = [jnp.concatenate(_to_natural(r, scratch, d, tm), axis=1) for r, d in zip(lse_refs, dils)]
        m = functools.reduce(jnp.maximum, ls)
        es = [jnp.exp(l - m) for l in ls]
        tot = functools.reduce(jnp.add, es)
        ws = [e / tot for e in es]
        y_ref[...] = functools.reduce(jnp.add, [w * t for w, t in zip(ws, o)]).astype(y_ref.dtype)
        for g in range(G):
            w_refs[g][...] = ws[g]
            on_refs[g][...] = o[g]

    nat = pl.BlockSpec((tm, W), lambda i: (i, 0))
    res = pl.pallas_call(
        body, name="dil_merge_fwd", grid=(T // tm,),
        in_specs=[_residue_tile(d, tm) for d in dils] * 2,
        out_specs=[nat] * (2 * G + 1),
        out_shape=[jax.ShapeDtypeStruct((T, W), BF)] + [jax.ShapeDtypeStruct((T, W), F32)] * (2 * G),
        scratch_shapes=[pltpu.VMEM((tm, 128), F32)],
        compiler_params=_cparams(("parallel",)),
    )(*os, *lses)
    return res[0], res[1:G + 1], res[G + 1:]


def _dil_merge_bwd(dy, os, ws, tm, after=()):
    G = len(DIL_GROUPS)
    T, W = dy.shape
    dils = [d for _, d in DIL_GROUPS]
    n_after = len(after)

    def body(*refs):
        dyt = refs[0][...]
        o, w = [r[...] for r in refs[1:G + 1]], [r[...] for r in refs[G + 1:2 * G + 1]]
        refs = refs[2 * G + 1 + n_after:]
        do_refs, dlse_refs, scratch = refs[:G], refs[G:2 * G], refs[-1]
        dws = [_head_sums(dyt * t) for t in o]
        mean = functools.reduce(jnp.add, [a * b for a, b in zip(w, dws)])
        for g, d in enumerate(dils):
            do, dlse = w[g] * dyt, w[g] * (dws[g] - mean)
            for pair in range(DIL_PAIRS):
                cols = slice(pair * 128, (pair + 1) * 128)
                _from_natural(do[:, cols], scratch, do_refs[g], pair, d, tm)
                _from_natural(dlse[:, cols], scratch, dlse_refs[g], pair, d, tm)

    nat = pl.BlockSpec((tm, W), lambda i: (i, 0))
    res = pl.pallas_call(
        body, name="dil_merge_bwd", grid=(T // tm,),
        in_specs=[nat] * (2 * G + 1) + [ANY] * n_after,
        out_specs=[_residue_tile(d, tm) for d in dils] * 2,
        out_shape=[_residue_shape(d, T, BF) for d in dils] + [_residue_shape(d, T, F32) for d in dils],
        scratch_shapes=[pltpu.VMEM((tm, 128), F32)],
        compiler_params=_cparams(("parallel",)),
    )(dy, *os, *ws, *after)
    return res[:G], res[G:]


def _qkv_prep(z, cos2, sin_signed, tm):
    T = z.shape[0]
    G = len(DIL_GROUPS)
    dils = [d for _, d in DIL_GROUPS]
    n_dil_blocks = 3 * DIL_WIDTH // 128

    def body(*refs):
        blocks = refs[:n_dil_blocks]
        cos_ref, sin_ref = refs[n_dil_blocks], refs[1 + n_dil_blocks]
        outs = refs[2 + n_dil_blocks:]
        for part in range(3):
            for g, d in enumerate(dils):
                out = outs[g * 3 + part]
                for pair in range(DIL_PAIRS):
                    blk = blocks[part * (DIL_WIDTH // 128) + g * DIL_PAIRS + pair]
                    for r in range(d):
                        rows = pl.ds(r, tm // d, stride=d) if d > 1 else slice(None)
                        x = blk[rows, :]
                        if part < 2:
                            x = _rope(x, cos_ref[rows, :], sin_ref[rows, :])
                        if part == 0:
                            x = x * Q_SCALE
                        out[pair, r] = x.astype(out.dtype)

    lane_block = [pl.BlockSpec((tm, 128), functools.partial(lambda b, i: (i, b), b)) for b in range(n_dil_blocks)]
    tab = pl.BlockSpec((tm, 128), lambda i: (i, 0))
    res = pl.pallas_call(
        body, name="qkv_prep", grid=(T // tm,),
        in_specs=lane_block + [tab, tab],
        out_specs=[_residue_tile(d, tm) for d in dils for _ in range(3)],
        out_shape=[_residue_shape(d, T, BF) for d in dils for _ in range(3)],
        compiler_params=_cparams(("parallel",)),
    )(*[z] * n_dil_blocks, cos2, sin_signed)
    return [res[3 * g:3 + 3 * g] for g in range(G)]


def _qkv_unprep(d_na, d_dil, cos2, sin_signed, tm, after=()):
    T = d_na[0].shape[0]
    G = len(DIL_GROUPS)
    dils = [d for _, d in DIL_GROUPS]
    n_after = len(after)

    def body(*refs):
        dq, dk, dv = (r[...] for r in refs[:3])
        res_refs = refs[3:3 + 3 * G]
        cs, sn = refs[3 + 3 * G][...], refs[4 + 3 * G][...]
        out, scratch = refs[5 + 3 * G + n_after], refs[-1]
        cols = [dq, dk, dv]
        for part in range(3):
            for g, d in enumerate(dils):
                for x in _to_natural(res_refs[g * 3 + part], scratch, d, tm):
                    if part < 2:
                        x = _rope(x, cs, -sn)
                    cols.append((x * Q_SCALE if part == 0 else x).astype(out.dtype))
        out[...] = jnp.concatenate(cols, axis=1)

    wide = pl.BlockSpec((tm, NA_WIDTH), lambda i: (i, 0))
    tab = pl.BlockSpec((tm, 128), lambda i: (i, 0))
    return pl.pallas_call(
        body, name="qkv_unprep", grid=(T // tm,),
        in_specs=[wide] * 3 + [_residue_tile(d, tm) for d in dils for _ in range(3)] + [tab, tab] + [ANY] * n_after,
        out_specs=pl.BlockSpec((tm, QKV_WIDTH), lambda i: (i, 0)),
        out_shape=jax.ShapeDtypeStruct((T, QKV_WIDTH), BF),
        scratch_shapes=[pltpu.VMEM((tm, 128), F32)],
        compiler_params=_cparams(("parallel",)),
    )(*d_na, *[t for g in range(G) for t in d_dil[g]], cos2, sin_signed, *after)


def _rope_tables(positions):
    half = HEAD_DIM // 2
    inv_freq = ROPE_THETA ** (-jnp.arange(half, dtype=F32) / half)
    ang = positions.astype(F32)[:, None] * inv_freq
    cos, sin = jnp.cos(ang), jnp.sin(ang)
    return jnp.tile(jnp.concatenate([cos, cos], axis=1), (1, 2)), jnp.tile(jnp.concatenate([-sin, sin], axis=1), (1, 2))


def _pack_rows(t):
    return t.reshape(-1, PACK_W)


def _me():
    return lax.axis_index("x"), lax.axis_index("y"), lax.axis_index("c")


def _other_chips(x, y):
    return [(1 - x, y), (x, 1 - y), (1 - x, 1 - y)]


def _pair_sum(g, got, tm):
    S, R, W = g.shape
    half = R // 2
    nb = half // tm

    def body(pos_ref, g_ref, got_ref, own_ref, ob_ref):
        tot = g_ref[...] + got_ref[...]
        ob_ref[...] = tot.astype(ob_ref.dtype)

        @pl.when(pl.program_id(1) == pos_ref[1])
        def _():
            own_ref[...] = tot

    tile = pl.BlockSpec((None, tm, W), lambda i, s, pos_ref: (s, i, 0))
    c, chip = lax.axis_index("c"), 2 * lax.axis_index("x") + lax.axis_index("y")
    return pl.pallas_call(
        body, name="pair_sum",
        grid_spec=pltpu.PrefetchScalarGridSpec(
            num_scalar_prefetch=1, grid=(nb, S),
            in_specs=[pl.BlockSpec((None, tm, W), lambda i, s, pos_ref: (s, pos_ref[0] * nb + i, 0)), tile],
            out_specs=[pl.BlockSpec((tm, W), lambda i, s, pos_ref: (i, 0)), tile]),
        out_shape=[jax.ShapeDtypeStruct((half, W), F32), jax.ShapeDtypeStruct((S, half, W), BF)],
        compiler_params=_cparams(("parallel", "arbitrary")),
    )(jnp.stack([c, chip]).astype(jnp.int32), g, got)


def _chip_sum(own, others, tm):
    n, h, W = others.shape
    nb = h // tm

    def body(c_ref, own_ref, p_ref, o_ref):
        o_ref[...] = ((own_ref[...] + p_ref[0].astype(F32)) + p_ref[1].astype(F32)) + p_ref[2].astype(F32)

    return pl.pallas_call(
        body, name="chip_sum",
        grid_spec=pltpu.PrefetchScalarGridSpec(
            num_scalar_prefetch=1, grid=(nb,),
            in_specs=[pl.BlockSpec((tm, W), lambda i, c_ref: (i, 0)), pl.BlockSpec((n, tm, W), lambda i, c_ref: (0, i, 0))],
            out_specs=pl.BlockSpec((tm, W), lambda i, c_ref: (c_ref[0] * nb + i, 0))),
        out_shape=jax.ShapeDtypeStruct((2 * h, W), F32),
        compiler_params=_cparams(("parallel",)),
    )(lax.axis_index("c").reshape(1).astype(jnp.int32), own, others)


def _join_halves(shard, after=()):
    h = shard.shape[0] // 2

    def body(in_ref, *rest):
        out_ref, send_sem, recv_sem = rest[len(after):]
        x, y, c = _me()
        cp = pltpu.make_async_remote_copy(
            src_ref=in_ref.at[pl.ds(c * h, h), :], dst_ref=out_ref.at[pl.ds(c * h, h), :],
            send_sem=send_sem, recv_sem=recv_sem, device_id=(x, y, 1 - c), device_id_type=MESH)
        cp.start()
        pltpu.make_async_remote_copy(
            src_ref=in_ref.at[pl.ds(c * h, h), :], dst_ref=out_ref.at[pl.ds((1 - c) * h, h), :],
            send_sem=send_sem, recv_sem=recv_sem, device_id=(x, y, 1 - c), device_id_type=MESH).wait_recv()
        cp.wait_send()

    return pl.pallas_call(
        body, name="join_halves", in_specs=[ANY] * (1 + len(after)), out_specs=ANY,
        out_shape=jax.ShapeDtypeStruct(shard.shape, shard.dtype), input_output_aliases={0: 0},
        scratch_shapes=[pltpu.SemaphoreType.DMA, pltpu.SemaphoreType.DMA],
    )(shard, *after)


def _allreduce_copies(src_ref, land_ref):
    x, y, c = _me()
    peers = [((x + fx) % 2, (y + fy) % 2, (c + fc) % 2) for fx in range(2) for fy in range(2) for fc in range(2)][1:]
    return [(src_ref, land_ref.at[4 * x + 2 * y + c], peer) for peer in peers]


def _allreduce_start(s, after):
    return _split_start("allreduce_small_start", s, (N_DEV, *s.shape), s.dtype, N_DEV - 1, _allreduce_copies, after)


def _allreduce_finish(flight, after):
    own, landed = _split_wait("allreduce_small_wait", flight, after, N_DEV - 1, _allreduce_copies)
    me = 4 * lax.axis_index("x") + 2 * lax.axis_index("y") + lax.axis_index("c")
    parts = lax.dynamic_update_slice(landed, own[None], (me, 0, 0))

    def body(p_ref, o_ref):
        total = p_ref[0]
        for d in range(1, N_DEV):
            total = total + p_ref[d]
        o_ref[...] = total

    return pl.pallas_call(
        body, name="allreduce_small_sum",
        in_specs=[pl.BlockSpec(memory_space=pltpu.VMEM)], out_specs=pl.BlockSpec(memory_space=pltpu.VMEM),
        out_shape=jax.ShapeDtypeStruct(own.shape, F32),
    )(parts)


HBM_SPEC = pl.BlockSpec(memory_space=pltpu.HBM)
SEM_SPEC = pl.BlockSpec(memory_space=pltpu.SEMAPHORE)
DATAFLOW = pltpu.SideEffectType.DATAFLOW_SIDE_EFFECTING


class _InFlight(NamedTuple):
    sems: tuple
    src: jax.Array
    land: jax.Array
    token: jax.Array


def _split_start(name, src, land_shape, land_dtype, n, copies, after=()):
    n_after = len(after)

    def body(src_ref, land_ref, *rest):
        rest = rest[n_after:]
        sems, token = rest[:2 * n], rest[-1]
        for k, (s, d, peer) in enumerate(copies(src_ref, land_ref)):
            pltpu.make_async_remote_copy(src_ref=s, dst_ref=d, send_sem=sems[k], recv_sem=sems[n + k],
                                         device_id=peer, device_id_type=MESH).start()
        token[...] = jnp.zeros_like(token)

    outs = pl.pallas_call(
        body, name=name,
        out_shape=(*[pltpu.SemaphoreType.DMA(())] * (2 * n), pltpu.HBM(src.shape, src.dtype), pltpu.HBM(land_shape, land_dtype),
                   jax.ShapeDtypeStruct((8, 128), F32)),
        in_specs=(HBM_SPEC, HBM_SPEC, *[ANY] * n_after),
        out_specs=(*[SEM_SPEC] * (2 * n), HBM_SPEC, HBM_SPEC, pl.BlockSpec(memory_space=pltpu.VMEM)),
        input_output_aliases={0: 2 * n, 1: 2 * n + 1},
        compiler_params=pltpu.CompilerParams(has_side_effects=DATAFLOW),
    )(pltpu.with_memory_space_constraint(src, pltpu.HBM), pltpu.with_memory_space_constraint(lax.empty(land_shape, land_dtype), pltpu.HBM),
      *after)
    return _InFlight(tuple(outs[:2 * n]), outs[2 * n], outs[2 * n + 1], outs[2 * n + 2])


def _split_wait(name, flight, after, n, copies):
    after = after if isinstance(after, tuple) else (after,)

    def body(src_ref, land_ref, *rest):
        sems = rest[:2 * n]
        for k, (s, d, peer) in enumerate(copies(src_ref, land_ref)):
            cp = pltpu.make_async_remote_copy(src_ref=s, dst_ref=d, send_sem=sems[k], recv_sem=sems[n + k],
                                              device_id=peer, device_id_type=MESH)
            cp.wait_send()
            cp.wait_recv()

    return pl.pallas_call(
        body, name=name,
        out_shape=(pltpu.HBM(flight.src.shape, flight.src.dtype), pltpu.HBM(flight.land.shape, flight.land.dtype)),
        in_specs=(HBM_SPEC, HBM_SPEC, *[SEM_SPEC] * (2 * n), *[ANY] * len(after)),
        out_specs=(HBM_SPEC, HBM_SPEC), input_output_aliases={0: 0, 1: 1},
        compiler_params=pltpu.CompilerParams(has_side_effects=DATAFLOW),
    )(flight.src, flight.land, *flight.sems, *after)


def _gather_copies(src_ref, land_ref):
    x, y, c = _me()
    return [(src_ref, land_ref.at[2 * x + y], (*chip, c)) for chip in _other_chips(x, y)]


def _gather_start(packed, tag, after=()):
    return _split_start(f"gather_start_{tag}", packed, (N_CHIPS, *packed.shape), packed.dtype, 3, _gather_copies, after)


def _gather_wait(flight, after, tag):
    src, others = _split_wait(f"gather_wait_{tag}", flight, after, 3, _gather_copies)
    return lax.dynamic_update_slice(others, src[None], (2 * lax.axis_index("x") + lax.axis_index("y"), 0, 0))


def _across_copies(src_ref, land_ref):
    x, y, c = _me()
    half = src_ref.shape[0] // 2
    rows = pl.ds(c * half, half)
    return [(src_ref.at[rows, :], land_ref.at[2 * x + y, rows, :], (*chip, c)) for chip in _other_chips(x, y)]


def _to_sibling_copies(all_ref, unused_ref):
    x, y, c = _me()
    half = all_ref.shape[1] // 2
    places = [all_ref.at[2 * chip[0] + chip[1], pl.ds(c * half, half), :] for chip in _other_chips(x, y)]
    return [(place, place, (x, y, 1 - c)) for place in places]


def _gather_halves_start(shard, tag, after=()):
    return _split_start(f"gather_{tag}_across_start", shard, (N_CHIPS, *shard.shape), shard.dtype, 3, _across_copies, after)


def _gather_halves_relay(flight, after, tag):
    shard, landed = _split_wait(f"gather_{tag}_across_wait", flight, after, 3, _across_copies)
    return shard, _split_start(f"gather_{tag}_sibling_start", landed, (8, 128), landed.dtype, 3, _to_sibling_copies)


def _gather_halves_finish(shard, relay, after, tag):
    others = _split_wait(f"gather_{tag}_sibling_wait", relay, after, 3, _to_sibling_copies)[0]
    return lax.dynamic_update_slice(others, shard[None], (2 * lax.axis_index("x") + lax.axis_index("y"), 0, 0))


def _assemble_w_in(shards, tm):
    S, R, C = shards.shape
    n_gates = 2 * D_MODEL

    def body(s_ref, w_ref, g_ref):
        full = jnp.concatenate([s_ref[s] for s in range(S)], axis=1)
        w_ref[...] = full
        g_ref[...] = full[:, S * C - n_gates:]

    return pl.pallas_call(
        body, name="assemble_w_in", grid=(R // tm,),
        in_specs=[pl.BlockSpec((S, tm, C), lambda i: (0, i, 0))],
        out_specs=[pl.BlockSpec((tm, S * C), lambda i: (i, 0)), pl.BlockSpec((tm, n_gates), lambda i: (i, 0))],
        out_shape=[jax.ShapeDtypeStruct((R, S * C), shards.dtype), jax.ShapeDtypeStruct((R, n_gates), shards.dtype)],
        compiler_params=_cparams(("parallel",)),
    )(shards)


def _swap_copies(src_ref, land_ref):
    x, y, c = _me()
    half = land_ref.shape[1]
    return [(src_ref.at[:, pl.ds((1 - c) * half, half), :], land_ref, (x, y, 1 - c))]


def _swap_start(g, tag):
    S, R, W = g.shape
    return _split_start(f"swap_halves_start_{tag}", g, (S, R // 2, W), g.dtype, 1, _swap_copies)


def _swap_wait(flight, after, tag):
    return _split_wait(f"swap_halves_wait_{tag}", flight, after, 1, _swap_copies)


def _scatter_copies(src_ref, land_ref):
    x, y, c = _me()
    return [(src_ref.at[2 * chip[0] + chip[1]], land_ref.at[j], (*chip, c)) for j, chip in enumerate(_other_chips(x, y))]


def _scatter_start(part, tag):
    S, h, W = part.shape
    return _split_start(f"scatter_chips_start_{tag}", part, (S - 1, h, W), part.dtype, 3, _scatter_copies)


def _scatter_wait(flight, after, tag):
    return _split_wait(f"scatter_chips_wait_{tag}", flight, after, 3, _scatter_copies)[1]


def _join_copies(shard_ref, unused_ref):
    x, y, c = _me()
    h = shard_ref.shape[0] // 2
    rows = shard_ref.at[pl.ds(c * h, h), :]
    return [(rows, rows, (x, y, 1 - c))]


def _join_start(shard):
    return _split_start("join_halves_start", shard, (8, 128), shard.dtype, 1, _join_copies)


def _join_wait(flight, after):
    return _split_wait("join_halves_wait", flight, after, 1, _join_copies)[0]


def _adamw(name, g, g_row0, w, m, v):
    _, R, C = w.shape
    tm = next(cand for cand in (368, 256, 128, 64, 32, 16, 8) if R % cand == 0)
    assert g_row0 % tm == 0 and g.shape[1] == C

    def body(g_ref, w_ref, m_ref, v_ref, go_ref, d_ref, mo_ref, vo_ref):
        gt = g_ref[...]
        mt = ADAM_B1 * m_ref[...] + (1.0 - ADAM_B1) * gt
        vt = ADAM_B2 * v_ref[...] + (1.0 - ADAM_B2) * jnp.square(gt)
        m_hat = mt / (1.0 - ADAM_B1 ** ADAM_STEP)
        v_hat = vt / (1.0 - ADAM_B2 ** ADAM_STEP)
        go_ref[...] = gt
        d_ref[...] = -ADAM_LR * (m_hat / (jnp.sqrt(v_hat) + ADAM_EPS) + ADAM_WD * w_ref[...])
        mo_ref[...] = mt
        vo_ref[...] = vt

    state = pl.BlockSpec((None, tm, C), lambda i: (0, i, 0))
    return pl.pallas_call(
        body, name=name, grid=(R // tm,),
        in_specs=[pl.BlockSpec((tm, C), lambda i: (g_row0 // tm + i, 0)), state, state, state],
        out_specs=[state] * 4, out_shape=[jax.ShapeDtypeStruct((1, R, C), F32)] * 4,
        compiler_params=_cparams(("parallel",)),
    )(g, w, m, v)


def _unpack_weights(gathered, names):
    S = gathered.shape[0]
    shard_shapes = {"w_in": (D_MODEL, (QKV_WIDTH + 2 * D_MODEL) // S), "w_branch_na": (NA_WIDTH, D_MODEL // S),
                    "w_branch_dil": (DIL_OUT_WIDTH, D_MODEL // S), "w_out": (D_MODEL // S, D_MODEL),
                    "w_up": (D_MODEL, D_FF // S), "w_down": (D_FF // S, D_MODEL),
                    "w_ple_gate": (D_MODEL // S, D_MODEL), "w_ple_proj": (PLE_DIM, D_MODEL // S)}
    col_sharded = {"w_in", "w_branch_na", "w_branch_dil", "w_up", "w_ple_proj"}
    out, r0 = {}, 0
    for name in names:
        rows, cols = shard_shapes[name]
        n = rows * cols // PACK_W
        t = gathered[:, r0:r0 + n, :].reshape(S, rows, cols)
        r0 += n
        out[name] = t.transpose(1, 0, 2).reshape(rows, S * cols) if name in col_sharded else t.reshape(S * rows, cols)
    return out


def kernel(x, p, positions, g_mix, w_in, rpb, w_branch_na, w_branch_dil, w_out, g_mlp, w_up, w_down, g_ple, w_ple_gate, w_ple_proj, g_final, loss_target, m_g_mix, m_w_in, m_rpb, m_w_branch_na, m_w_branch_dil, m_w_out, m_g_mlp, m_w_up, m_w_down, m_g_ple, m_w_ple_gate, m_w_ple_proj, m_g_final, v_g_mix, v_w_in, v_rpb, v_w_branch_na, v_w_branch_dil, v_w_out, v_g_mlp, v_w_up, v_w_down, v_g_ple, v_w_ple_gate, v_w_ple_proj, v_g_final):
    shards = {"w_in": w_in[0], "w_branch_na": w_branch_na[0], "w_branch_dil": w_branch_dil[0], "w_out": w_out[0],
              "w_up": w_up[0], "w_down": w_down[0], "w_ple_gate": w_ple_gate[0], "w_ple_proj": w_ple_proj[0]}
    params = {"w_in": w_in, "w_branch_na": w_branch_na, "w_branch_dil": w_branch_dil, "w_out": w_out, "w_up": w_up,
              "w_down": w_down, "w_ple_gate": w_ple_gate, "w_ple_proj": w_ple_proj,
              "m_w_in": m_w_in, "m_w_branch_na": m_w_branch_na, "m_w_branch_dil": m_w_branch_dil, "m_w_out": m_w_out,
              "m_w_up": m_w_up, "m_w_down": m_w_down, "m_w_ple_gate": m_w_ple_gate, "m_w_ple_proj": m_w_ple_proj,
              "v_w_in": v_w_in, "v_w_branch_na": v_w_branch_na, "v_w_branch_dil": v_w_branch_dil, "v_w_out": v_w_out,
              "v_w_up": v_w_up, "v_w_down": v_w_down, "v_w_ple_gate": v_w_ple_gate, "v_w_ple_proj": v_w_ple_proj}

    xs, ps, tgt = x[0], p[0, 0], loss_target[0]
    T = xs.shape[0]
    TM = 512
    gm, gl, gp, gf = g_mix, g_mlp, g_ple, g_final.reshape(1, D_MODEL)

    across = _gather_halves_start(shards["w_in"].astype(BF), "in")
    a = _rowwise("norm_mix", lambda h, g: h * _rms(h) * g, T, TM, [_row(xs, TM), _full(gm)], [(D_MODEL, BF)],
                 after=(across.token,))
    cos2, sin_signed = _rope_tables(positions[0])
    tab = _na_bias_table(rpb[0])
    packed_mix = jnp.concatenate([_pack_rows(shards[n].astype(BF)) for n in GATHER_MIX], axis=0)
    packed_mlp = jnp.concatenate([_pack_rows(shards[n].astype(BF)) for n in GATHER_MLP], axis=0)
    w_in_shard, w_in_relay = _gather_halves_relay(across, (a, tab, cos2, sin_signed, packed_mix, packed_mlp), "in")
    w_in_all = _gather_halves_finish(w_in_shard, w_in_relay, w_in_relay.token, "in")
    w_in_full, w_gates = _assemble_w_in(w_in_all, 256)
    W = {"w_in": w_in_full}
    mix_flight = _gather_start(packed_mix, "mix", after=(w_in_all,))
    mlp_across = _gather_halves_start(packed_mlp, "mlp", after=(mix_flight.token,))

    n3 = 3 * NA_WIDTH
    qkv = _mm("in_na", a, W["w_in"], "nn", 1024, 768, 1024, [BF], after=(mlp_across.token,),
              b_view=(n3, (D_MODEL, 768), lambda j, k: (k, j)))
    z_dil = _mm("in_dil", a, W["w_in"], "nn", 1024, 768, 1024, [F32], after=(mlp_across.token,),
                b_view=(3 * DIL_WIDTH, (D_MODEL, 768), lambda j, k: (k, n3 // 768 + j)))
    z_gates = _mm("in_gates", a, w_gates, "nn", 1024,1024, 1024, [BF], after=(mlp_across.token,))

    dil_ops = _qkv_prep(z_dil, cos2, sin_signed, TM)
    y_na = _na_fwd(qkv, tab)
    band = [_band_fwd(*dil_ops[g], g) for g in range(len(DIL_GROUPS))]
    y_dil, w_grp, o_nat = _dil_merge_fwd([b[0] for b in band], [b[1] for b in band], T, TM)

    W.update(_unpack_weights(_gather_wait(mix_flight, y_dil, "mix"), GATHER_MIX))
    mlp_shard, mlp_relay = _gather_halves_relay(mlp_across, y_dil, "mlp")
    u_na = _mm("branch_na", y_na, W["w_branch_na"], "nn", 1024,1024, 512, [BF], after=(mlp_relay.token,))
    def gate_mix(acc, gn, gd, un):
        ud = acc.astype(BF)
        return ud, _sigmoid(gn.astype(F32)) * un.astype(F32) + _sigmoid(gd.astype(F32)) * ud.astype(F32)

    u_dil, mixed = _mm("branch_dil", y_dil, W["w_branch_dil"], "nn", 512, 1024, 256, [BF, BF], epilogue=gate_mix,
                       extras=((z_gates, 0), (z_gates, 1), u_na))

    def add_norm(d, h, g):
        h = h + d
        return h, h * _rms(h) * g

    h1, cn = _mm("out_proj", mixed, W["w_out"], "nn", 512, 1024, 1024, [F32, BF], epilogue=add_norm, extras=(xs,), consts=(gl,))
    mlp_all = _gather_halves_finish(mlp_shard, mlp_relay, cn, "mlp")
    W.update({n: t for n, t in _unpack_weights(mlp_all, GATHER_MLP).items() if n.startswith("w_ple")})
    chip_block = (None, D_MODEL, PACK_W)
    up, act = _mm("mlp_up", cn, mlp_all, "nn", 1024,1024, 1024, [BF, BF],
                  epilogue=lambda acc: (acc, jnp.square(jnp.maximum(acc, 0.0))), b_view=(D_FF, chip_block, lambda j, k: (j, 0, 0)))
    h2, en = _mm("mlp_down", act, mlp_all, "nn", 1024, 1024, 2048, [F32, BF], epilogue=add_norm, extras=(h1,), consts=(gp,),
                 b_view=(D_MODEL, (2, D_MODEL, PACK_W), lambda j, k: (k, 1, 0)))
    pp = _mm("ple_proj", ps, W["w_ple_proj"], "nn", 1024,1024, 256, [F32])

    def head(gtt, h2t, ppt, tg, g):
        sg = _sigmoid(gtt)
        h3 = h2t + sg * ppt
        yo = h3 * _rms(h3) * g
        diff = yo - tg
        loss = 0.5 * jnp.sum(jnp.mean(jnp.square(diff), axis=-1, keepdims=True), axis=0, keepdims=True)
        dh3, dg = _rms_bwd(diff * (1.0 / D_MODEL), h3, g)
        return dh3, dh3 * ppt * sg * (1.0 - sg), dh3 * sg, jnp.broadcast_to(loss, (1, 128)), dg

    dh3, d_gt, d_pp, loss_part, dg_final = _mm(
        "ple_gate_loss_head", en, W["w_ple_gate"], "nn", 512, 1024, 1024, [F32, BF, BF], epilogue=head,
        extras=(h2, pp, tgt), consts=(gf,), sums=[128, D_MODEL])

    early_shapes = {n: shards[n].shape for n in REDUCE_EARLY}
    early_rows = sum(r * c for r, c in early_shapes.values()) // PACK_W
    shard_rows = D_MODEL // N_CHIPS
    early_buf = _mm("g_ple_gate", en, d_gt, "tn", 1024, 1024, 2048, [F32],
                    into=(jax.ShapeDtypeStruct((N_CHIPS, early_rows, PACK_W), F32), (N_CHIPS, shard_rows, PACK_W),
                          lambda i, j: (0, 2 * D_MODEL // shard_rows, 0)))
    g_ple_proj = _mm("g_ple_proj", ps, d_pp, "tn", 256, 1024, 1024,[F32])

    def add_norm_bwd(dn, dh_out, h, g):
        dh, dg = _rms_bwd(dn, h, g)
        dh = dh_out + dh
        return dh, dh, dg

    dh2, dh2_b, dg_ple = _mm("d_ple_gate", d_gt, W["w_ple_gate"], "nt", 512, 1024, 1024, [F32, BF],
                             epilogue=add_norm_bwd, extras=(dh3, h2), consts=(gp,), sums=[D_MODEL])
    d_up = _mm("d_mlp_down", dh2_b, mlp_all, "nt", 1024,1024, 1024, [BF], b_view=(D_FF, chip_block, lambda j, k: (j, 1, 0)),
               epilogue=lambda acc, u: (acc * (2.0 * jnp.maximum(u.astype(F32), 0.0)),), extras=(up,))
    early_buf = _mm("g_mlp_down", act, dh2_b, "tn", 1024, 1024, 2048, [F32],
                    into=(early_buf, (None, D_MODEL, PACK_W), lambda i, j: (i, 1, 0)))
    early_buf = _mm("g_mlp_up", cn, d_up, "tn", 1024, 1024, 2048, [F32],
                    into=(early_buf, (None, D_MODEL, PACK_W), lambda i, j: (j, 0, 0)))
    dh1, dh1_b, dg_mlp = _mm("d_mlp_up", d_up, mlp_all, "nt", 1024, 1024, 1024, [F32, BF], epilogue=add_norm_bwd,
                             b_view=(D_MODEL, chip_block, lambda j, k: (k, 0, 0)),
                             extras=(dh2, h1), consts=(gl,), sums=[D_MODEL])
    early_buf = _mm("g_out_proj", mixed, dh1_b, "tn", 1024, 1024, 1024, [F32],
                    into=(early_buf, (N_CHIPS, shard_rows, PACK_W), lambda i, j: (0, 2 * D_MODEL // shard_rows + 1, 0)))

    def gate_bwd(dm, gn, gd, un, ud):
        gn, gd, un, ud = (t.astype(F32) for t in (gn, gd, un, ud))
        sn, sd = _sigmoid(gn), _sigmoid(gd)
        return jnp.concatenate([dm * un * sn * (1.0 - sn), dm * ud * sd * (1.0 - sd)], axis=1), dm * sn, dm * sd

    dz_gates, d_u_na, d_u_dil = _mm("d_out_proj", dh1_b, W["w_out"], "nt", 512, 1024, 1024, [(BF, 2 * D_MODEL), BF, BF],
                                    epilogue=gate_bwd, extras=((z_gates, 0), (z_gates, 1), u_na, u_dil))
    g_branch_na = _mm("g_branch_na", y_na, d_u_na, "tn", 1024, 1024, 1024,[F32])
    g_branch_dil = _mm("g_branch_dil", y_dil, d_u_dil, "tn", 256, 1024, 1024,[F32])
    small_rows = [jnp.concatenate([_pack_rows(g[:, s * shard_rows:(s + 1) * shard_rows]) for g in (g_ple_proj, g_branch_na, g_branch_dil)],
                                  axis=0) for s in range(N_CHIPS)]
    early_buf = lax.dynamic_update_slice(early_buf, jnp.stack(small_rows), (0, 2 * D_MODEL + 2 * shard_rows, 0))
    early_tm = early_rows // 4
    swap_flight = _swap_start(early_buf, "early")
    d_y_na = _mm("d_branch_na", d_u_na, W["w_branch_na"], "nt", 1024,512, 1024, [BF], after=(swap_flight.token,))
    d_y_dil = _mm("d_branch_dil", d_u_dil, W["w_branch_dil"], "nt", 1024,256, 1024, [F32])

    dqa, dka, dva, dtab = _na_bwd(qkv, tab, d_y_na)
    early_g, early_got = _swap_wait(swap_flight, dqa, "early")
    early_pair, early_pair_b = _pair_sum(early_g, early_got, early_tm)
    scatter_flight = _scatter_start(early_pair_b, "early")

    do_res, dlse_res = _dil_merge_bwd(d_y_dil, o_nat, w_grp, TM, after=(scatter_flight.token,))
    d_dil = [_band_bwd(*dil_ops[g], do_res[g], dlse_res[g], g) for g in range(len(DIL_GROUPS))]

    dz_qkv = _qkv_unprep((dqa, dka, dva), d_dil, cos2, sin_signed, TM)
    in_cols = shards["w_in"].shape[1]
    qkv_rows, gate_tm = dz_qkv.shape[1], 256
    assert qkv_rows % gate_tm == 0
    g_in = _mm("g_in_qkv", dz_qkv, a, "tn", 1280, 1024, 2048, [F32],
               into=(jax.ShapeDtypeStruct((N_CHIPS * in_cols, D_MODEL), F32), (1280, D_MODEL), lambda i, j: (i, 0)))
    g_in = _mm("g_in_gates", dz_gates, a, "tn", gate_tm, 1024, T, [F32],
               into=(g_in, (gate_tm, D_MODEL), lambda i, j: (qkv_rows // gate_tm + i, 0)))
    early_mine = _chip_sum(early_pair, _scatter_wait(scatter_flight, (g_in,), "early"), early_tm)
    join_flight = _join_start(early_mine)

    late_tm = in_cols // 4
    late_swap = _swap_start(g_in.reshape(N_CHIPS, in_cols, D_MODEL), "late")
    d_a = _mm("d_in_qkv", dz_qkv, W["w_in"], "nt", 1024,1024, 1920, [F32], after=(late_swap.token, join_flight.token),
              b_view=(D_MODEL, (D_MODEL, 1920), lambda j, k: (j, k)))
    late_g, late_got = _swap_wait(late_swap, d_a, "late")
    late_pair, late_pair_b = _pair_sum(late_g, late_got, late_tm)
    late_scatter = _scatter_start(late_pair_b, "late")
    d_rpb = _na_rpb_grad(dtab, after=(late_scatter.token,))[:, :2 * NA_WIN_ROWS - 1, :2 * NA_WIN_COLS - 1]
    def first_bwd(dn_gates, dn_qkv, dh_out, h, g):
        dh, dg = _rms_bwd(dn_gates + dn_qkv, h, g)
        return dh_out + dh, dg

    grad_x, dg_mix = _mm("d_in_gates", dz_gates, w_gates, "nt", 512, 1024, 2048, [F32], epilogue=first_bwd,
                         extras=(d_a, dh1, xs), consts=(gm,), sums=[D_MODEL], after=(late_scatter.token,))
    early_shard = _join_wait(join_flight, grad_x)

    n_rpb = rpb.size
    rpb_rows = 4
    small = jnp.concatenate([
        dg_mix, dg_mlp, dg_ple, dg_final,
        jnp.pad(d_rpb.reshape(-1), (0, rpb_rows * D_MODEL - n_rpb)).reshape(rpb_rows, D_MODEL),
        jnp.pad(loss_part, ((0, 0), (0, D_MODEL - loss_part.shape[1]))),
        jnp.zeros((SMALL_ROWS - 5 - rpb_rows, D_MODEL), F32)], axis=0)
    out = {"grad": {}, "delta": {}, "new_m": {}, "new_v": {}}

    def update(n, g, row0):
        res = _adamw("adamw_" + n, g, row0, params[n], params["m_" + n], params["v_" + n])
        for kind, t in zip(("grad", "delta", "new_m", "new_v"), res, strict=True):
            out[kind][n] = t

    row0 = 0
    for n in REDUCE_EARLY:
        rows, cols = early_shapes[n]
        n_rows = rows * cols // PACK_W
        if cols == PACK_W:
            update(n, early_shard, row0)
        else:
            update(n, early_shard[row0:row0 + n_rows].reshape(rows, cols), 0)
        row0 += n_rows
    late_others = _scatter_wait(late_scatter, (*[out["new_v"][n] for n in REDUCE_EARLY], d_rpb), "late")
    late_mine = _chip_sum(late_pair, late_others, late_tm)
    small_flight = _allreduce_start(small, after=(late_mine,))
    res = _adamw("adamw_w_in", _join_halves(late_mine, after=(small_flight.token,)), 0,
                 *[jnp.swapaxes(params[n], 1, 2) for n in ("w_in", "m_w_in", "v_w_in")])
    small = _allreduce_finish(small_flight, res[3])
    for kind, t in zip(("grad", "delta", "new_m", "new_v"), res, strict=True):
        out[kind]["w_in"] = jnp.swapaxes(t, 1, 2)
    loss = small[4 + rpb_rows, 0]

    def small_pack(a0, a1, a2, a3, r):
        return jnp.concatenate([a0.reshape(1, -1), a1.reshape(1, -1), a2.reshape(1, -1), a3.reshape(1, -1),
                                jnp.pad(r.reshape(-1), (0, rpb_rows * D_MODEL - n_rpb)).reshape(rpb_rows, D_MODEL)], axis=0)

    small_res = _adamw("adamw_small", small, 0, small_pack(g_mix, g_mlp, g_ple, g_final, rpb)[None],
                       small_pack(m_g_mix, m_g_mlp, m_g_ple, m_g_final, m_rpb)[None],
                       small_pack(v_g_mix, v_g_mlp, v_g_ple, v_g_final, v_rpb)[None])

    def small_unpack(t):
        return {"g_mix": t[0].reshape(g_mix.shape), "g_mlp": t[1].reshape(g_mlp.shape), "g_ple": t[2].reshape(g_ple.shape),
                "g_final": t[3].reshape(g_final.shape), "rpb": t[4:].reshape(-1)[:n_rpb].reshape(rpb.shape)}

    for kind, t in zip(("grad", "delta", "new_m", "new_v"), small_res, strict=True):
        out[kind].update(small_unpack(t[0]))

    order = ["g_mix", "w_in", "rpb", "w_branch_na", "w_branch_dil", "w_out", "g_mlp", "w_up", "w_down", "g_ple",
             "w_ple_gate", "w_ple_proj", "g_final"]
    return (loss, grad_x[None], *[out["grad"][n] for n in order], *[out["delta"][n] for n in order],
            *[out["new_m"][n] for n in order], *[out["new_v"][n] for n in order])
```

```python
import functools
from typing import NamedTuple

import jax
import jax.numpy as jnp
from jax import lax
from jax.experimental import pallas as pl
from jax.experimental.pallas import tpu as pltpu

BF = jnp.bfloat16
F32 = jnp.float32
MESH = pl.DeviceIdType.MESH
ANY = pl.BlockSpec(memory_space=pl.ANY)

V7X_VMEM_BYTES = 64 * 1024 * 1024
VMEM_LIMIT = V7X_VMEM_BYTES - 16 * 1024 * 1024

D_MODEL = 1024
HEAD_DIM = 64
GRID_W = 64
NA_HEADS = 8
NA_WIN_ROWS = 8
NA_WIN_COLS = 16
NA_WIDTH = NA_HEADS * HEAD_DIM
DIL_GROUPS = ((128, 1), (512, 4), (2048, 16))
DIL_HPG = 4
DIL_HEADS = DIL_HPG * len(DIL_GROUPS)
DIL_WIDTH = DIL_HEADS * HEAD_DIM
DIL_OUT_WIDTH = DIL_HPG * HEAD_DIM
DIL_RADIUS = 64
QKV_WIDTH = 3 * NA_WIDTH + 3 * DIL_WIDTH
D_FF = 4 * D_MODEL
PLE_DIM = 256
ROPE_THETA = 10000.0
RMS_EPS = 1e-6
NEG_INF = -1e30
Q_SCALE = HEAD_DIM ** -0.5

ADAM_LR = 0.001
ADAM_B1 = 0.9
ADAM_B2 = 0.999
ADAM_EPS = 1e-08
ADAM_WD = 0.01
ADAM_STEP = 10

N_CHIPS = 4
N_DEV = 8
PACK_W = 1024
GATHER_MIX = ("w_branch_na", "w_branch_dil", "w_out")
GATHER_MLP = ("w_up", "w_down", "w_ple_gate", "w_ple_proj")
REDUCE_EARLY = ("w_up", "w_down", "w_ple_gate", "w_out", "w_ple_proj", "w_branch_na", "w_branch_dil")
SMALL_ROWS = 16


def _cparams(sem=None):
    return pltpu.CompilerParams(dimension_semantics=sem, vmem_limit_bytes=VMEM_LIMIT)


def _mm(name, a, b, mode, tm, tn, tk, out_dtypes, epilogue=None, extras=(), consts=(), sums=(), after=(), into=None,
        b_view=None):
    if mode == "nn":
        (M, K), N = a.shape, b.shape[1]
    elif mode == "nt":
        (M, K), N = a.shape, b.shape[0]
    else:
        (K, M), N = a.shape, b.shape[1]
    if b_view is not None:
        N = b_view[0]
    tm, tn, tk = min(tm, M), min(tn, N), min(tk, K)
    assert M % tm == 0 and N % tn == 0 and K % tk == 0, (name, M, N, K, tm, tn, tk)
    if mode == "nn":
        a_spec = pl.BlockSpec((tm, tk), lambda i, j, k: (i, k))
        b_spec = pl.BlockSpec((tk, tn), lambda i, j, k: (k, j))
        dims = (((1,), (0,)), ((), ()))
    elif mode == "nt":
        a_spec = pl.BlockSpec((tm, tk), lambda i, j, k: (i, k))
        b_spec = pl.BlockSpec((tn, tk), lambda i, j, k: (j, k))
        dims = (((1,), (1,)), ((), ()))
    else:
        a_spec = pl.BlockSpec((tk, tm), lambda i, j, k: (k, i))
        b_spec = pl.BlockSpec((tk, tn), lambda i, j, k: (k, j))
        dims = (((0,), (0,)), ((), ()))
    if b_view is not None:
        b_spec = pl.BlockSpec(b_view[1], lambda i, j, k: b_view[2](j, k))
    nk = K // tk
    n_extra, n_const, n_out, n_sum = len(extras), len(consts), len(out_dtypes), len(sums)
    tile = pl.BlockSpec((tm, tn), lambda i, j, k: (i, j))
    assert not sums or tn == N, "row sums need whole rows in a tile"
    wide = [e for e in (*extras, *out_dtypes) if isinstance(e, tuple)]
    assert not wide or tn == N
    extra_specs = [pl.BlockSpec((tm, tn), functools.partial(lambda c, i, j, k: (i, c), e[1])) if isinstance(e, tuple) else tile
                   for e in extras]
    extras = [e[0] if isinstance(e, tuple) else e for e in extras]
    out_widths = [d[1] if isinstance(d, tuple) else N for d in out_dtypes]
    out_dtypes = [d[0] if isinstance(d, tuple) else d for d in out_dtypes]

    n_after = len(after)

    def body(a_ref, b_ref, *rest):
        extra_refs, rest = rest[:n_extra + n_const], rest[n_extra + n_const + n_after:]
        out_refs, sum_refs, acc = rest[:n_out], rest[n_out:n_out + n_sum], rest[-1]
        i, k = pl.program_id(0), pl.program_id(2)
        def product():
            if len(b_ref.shape) == 3:
                w = tk // b_ref.shape[0]
                parts = [lax.dot_general(a_ref[:, s * w:(s + 1) * w].astype(BF), b_ref[s].astype(BF), dims, preferred_element_type=F32)
                         for s in range(b_ref.shape[0])]
                return functools.reduce(lambda x, y: x + y, parts)
            return lax.dot_general(a_ref[...].astype(BF), b_ref[...].astype(BF), dims, preferred_element_type=F32)

        if nk > 1:
            @pl.when(k == 0)
            def _():
                acc[...] = jnp.zeros_like(acc)

            acc[...] += product()

        @pl.when(k == nk - 1)
        def _():
            total = product() if nk == 1 else acc[...]
            outs = (total,) if epilogue is None else epilogue(total, *[e[...] for e in extra_refs])
            for o_ref, val in zip(out_refs, outs[:n_out], strict=True):
                o_ref[...] = val.astype(o_ref.dtype).reshape(o_ref.shape)
            for s_ref, val in zip(sum_refs, outs[n_out:], strict=True):
                @pl.when(i == 0)
                def _():
                    s_ref[...] = val

                @pl.when(i != 0)
                def _():
                    s_ref[...] += val

    out_specs = ([tile if w == N else pl.BlockSpec((tm, w), lambda i, j, k: (i, 0)) for w in out_widths]
                 + [pl.BlockSpec((1, c), lambda i, j, k: (0, 0)) for c in sums])
    out_shape = ([jax.ShapeDtypeStruct((M, w), dt) for dt, w in zip(out_dtypes, out_widths, strict=True)]
                 + [jax.ShapeDtypeStruct((1, c), F32) for c in sums])
    operands, aliases = [a, b, *extras, *consts, *after], {}
    in_specs = ([a_spec, b_spec] + extra_specs
                + [pl.BlockSpec(c.shape, functools.partial(lambda nd, i, j, k: (0,) * nd, c.ndim)) for c in consts] + [ANY] * n_after)
    if into is not None:
        assert n_out == 1
        target, block, index = into
        out_specs = [pl.BlockSpec(block, lambda i, j, k: index(i, j))]
        out_shape = [jax.ShapeDtypeStruct(target.shape, target.dtype)]
        if not isinstance(target, jax.ShapeDtypeStruct):
            aliases = {len(operands): 0}
            operands.append(target)
            in_specs.append(ANY)
            n_after += 1

    outs = pl.pallas_call(
        body, name=name, grid=(M // tm, N // tn, nk),
        in_specs=in_specs, out_specs=out_specs, out_shape=out_shape,
        scratch_shapes=[pltpu.VMEM((tm, tn) if nk > 1 else (8, 128), F32)], input_output_aliases=aliases,
        compiler_params=_cparams(("arbitrary",) * 3 if sums else ("parallel", "parallel", "arbitrary")),
    )(*operands)
    return outs[0] if len(outs) == 1 else outs


def _row(arr, tm, col_block=None, width=None):
    width = arr.shape[1] if width is None else width
    cb = 0 if col_block is None else col_block
    return arr, pl.BlockSpec((tm, width), lambda i: (i, cb))


def _full(arr):
    nd = arr.ndim
    return arr, pl.BlockSpec(arr.shape, lambda i: (0,) * nd)


def _rowwise(name, body, T, tm, ins, outs, sums=(), after=()):
    n_in, n_out, n_sum, n_after = len(ins), len(outs), len(sums), len(after)

    def kern(*refs):
        in_refs, refs = refs[:n_in], refs[n_in + n_after:]
        out_refs, sum_refs = refs[:n_out], refs[n_out:]
        res = body(*[r[...] for r in in_refs])
        res = res if isinstance(res, tuple) else (res,)
        for o_ref, val in zip(out_refs, res[:n_out], strict=True):
            o_ref[...] = val.astype(o_ref.dtype)
        if n_sum:
            @pl.when(pl.program_id(0) == 0)
            def _():
                for s_ref in sum_refs:
                    s_ref[...] = jnp.zeros_like(s_ref)

            for s_ref, val in zip(sum_refs, res[n_out:], strict=True):
                s_ref[...] += val

    res = pl.pallas_call(
        kern, name=name, grid=(T // tm,),
        in_specs=[spec for _, spec in ins] + [ANY] * n_after,
        out_specs=[pl.BlockSpec((tm, c), lambda i: (i, 0)) for c, _ in outs]
        + [pl.BlockSpec((1, c), lambda i: (0, 0)) for c in sums],
        out_shape=[jax.ShapeDtypeStruct((T, c), dt) for c, dt in outs]
        + [jax.ShapeDtypeStruct((1, c), F32) for c in sums],
        compiler_params=_cparams(("arbitrary",)),
    )(*[a for a, _ in ins], *after)
    return res[0] if len(res) == 1 else res


def _sigmoid(x):
    return 1.0 / (1.0 + jnp.exp(-x))


def _rms(h):
    return lax.rsqrt(jnp.mean(h * h, axis=-1, keepdims=True) + RMS_EPS)


def _rms_bwd(dy, h, g):
    r = _rms(h)
    n = h * r
    dn = dy * g
    dh = r * (dn - n * jnp.mean(dn * n, axis=-1, keepdims=True))
    return dh, jnp.sum(dy * n, axis=0, keepdims=True)


def _rope(x, cos2, sin_signed):
    lane = lax.broadcasted_iota(jnp.int32, x.shape, 1)
    swapped = jnp.where((lane % HEAD_DIM) < HEAD_DIM // 2, pltpu.roll(x, 128 - HEAD_DIM // 2, 1), pltpu.roll(x, HEAD_DIM // 2, 1))
    return x * cos2 + swapped * sin_signed


NA_KEYS = NA_WIN_ROWS * GRID_W
NA_BASES = 8


def _na_row_geometry(r, rows):
    first = jnp.clip(r - NA_WIN_ROWS // 2, 0, rows - NA_WIN_ROWS)
    base = first - r + (NA_WIN_ROWS - 1)
    return pl.multiple_of(first * GRID_W, GRID_W), base


NA_ROWS_PER_STEP = 16
NA_BWD_ROWS_PER_STEP = 8


def _softmax_rows(s):
    p = jnp.exp(s - jnp.max(s, axis=-1, keepdims=True))
    return p / jnp.sum(p, axis=-1, keepdims=True)


def _split_pair(t):
    first = lax.broadcasted_iota(jnp.int32, t.shape, 1) < HEAD_DIM
    zero = jnp.zeros_like(t)
    return jnp.where(first, t, zero), jnp.where(first, zero, t)


def _join_pair(a, b):
    return jnp.where(lax.broadcasted_iota(jnp.int32, a.shape, 1) < HEAD_DIM, a, b)


_NT = (((1,), (1,)), ((), ()))
_TN = (((0,), (0,)), ((), ()))


def _na_fwd(qkv, tab):
    T = qkv.shape[0]
    rows = T // GRID_W
    n_pairs = NA_WIDTH // 128

    def body(q_ref, k_ref, v_ref, tab_ref, y_ref):
        def step(it, carry):
            geo = [_na_row_geometry(it * NA_ROWS_PER_STEP + u, rows) for u in range(NA_ROWS_PER_STEP)]
            q0s = [pl.multiple_of((it * NA_ROWS_PER_STEP + u) * GRID_W, GRID_W) for u in range(NA_ROWS_PER_STEP)]
            ss = [lax.dot_general(jnp.concatenate(_split_pair(q_ref[pl.ds(q0, GRID_W), :] * Q_SCALE), axis=0),
                                  k_ref[pl.ds(k0, NA_KEYS), :], _NT, preferred_element_type=F32)
                  for q0, (k0, _) in zip(q0s, geo)]
            ps = [_softmax_rows(s + jnp.concatenate([tab_ref[0, base], tab_ref[1, base]], axis=0)) for s, (_, base) in zip(ss, geo)]
            ys = [jnp.dot(p.astype(BF), v_ref[pl.ds(k0, NA_KEYS), :], preferred_element_type=F32) for p, (k0, _) in zip(ps, geo)]
            for q0, y2 in zip(q0s, ys):
                y_ref[pl.ds(q0, GRID_W), :] = _join_pair(y2[:GRID_W], y2[GRID_W:]).astype(y_ref.dtype)
            return carry

        lax.fori_loop(0, rows // NA_ROWS_PER_STEP, step, 0)

    def cols(first):
        return pl.BlockSpec((T, 128), lambda j: (0, first + j))

    return pl.pallas_call(
        body, name="na_fwd", grid=(n_pairs,),
        in_specs=[cols(0), cols(n_pairs), cols(2 * n_pairs), pl.BlockSpec((2, NA_BASES, GRID_W, NA_KEYS), lambda j: (j, 0, 0, 0))],
        out_specs=cols(0), out_shape=jax.ShapeDtypeStruct((T, NA_WIDTH), BF),
        compiler_params=_cparams(("parallel",)),
    )(qkv, qkv, qkv, tab)


def _na_bwd(qkv, tab, do):
    T = qkv.shape[0]
    rows = T // GRID_W
    n_pairs = NA_WIDTH // 128

    def body(q_ref, k_ref, v_ref, tab_ref, do_ref, dq_ref, dk_out, dv_out, dtab_ref, dk_ref, dv_ref):
        dk_ref[...] = jnp.zeros_like(dk_ref)
        dv_ref[...] = jnp.zeros_like(dv_ref)
        dtab_ref[...] = jnp.zeros_like(dtab_ref)

        def step(it, carry):
            U = NA_BWD_ROWS_PER_STEP
            geo = [_na_row_geometry(it * U + u, rows) for u in range(U)]
            q0s = [pl.multiple_of((it * U + u) * GRID_W, GRID_W) for u in range(U)]
            q2s = [jnp.concatenate(_split_pair(q_ref[pl.ds(q0, GRID_W), :] * Q_SCALE), axis=0) for q0 in q0s]
            do2s = [jnp.concatenate(_split_pair(do_ref[pl.ds(q0, GRID_W), :]), axis=0) for q0 in q0s]
            ss = [lax.dot_general(q2, k_ref[pl.ds(k0, NA_KEYS), :], _NT, preferred_element_type=F32) for q2, (k0, _) in zip(q2s, geo)]
            dps = [lax.dot_general(do2, v_ref[pl.ds(k0, NA_KEYS), :], _NT, preferred_element_type=F32) for do2, (k0, _) in zip(do2s, geo)]
            ps = [_softmax_rows(s + jnp.concatenate([tab_ref[0, base], tab_ref[1, base]], axis=0)) for s, (_, base) in zip(ss, geo)]
            dss = [p * (dp - jnp.sum(dp * p, axis=-1, keepdims=True)) for p, dp in zip(ps, dps)]
            dvs = [lax.dot_general(p.astype(BF), do2, _TN, preferred_element_type=F32) for p, do2 in zip(ps, do2s)]
            dsbs = [ds.astype(BF) for ds in dss]
            dqs = [jnp.dot(dsb, k_ref[pl.ds(k0, NA_KEYS), :], preferred_element_type=F32) for dsb, (k0, _) in zip(dsbs, geo)]
            dks = [lax.dot_general(dsb, q2, _TN, preferred_element_type=F32) for dsb, q2 in zip(dsbs, q2s)]
            for u in range(U):
                k0, base = geo[u]
                dtab_ref[0, base] += dss[u][:GRID_W]
                dtab_ref[1, base] += dss[u][GRID_W:]
                dq_ref[pl.ds(q0s[u], GRID_W), :] = (_join_pair(dqs[u][:GRID_W], dqs[u][GRID_W:]) * Q_SCALE).astype(dq_ref.dtype)
                dk_ref[pl.ds(k0, NA_KEYS), :] += dks[u]
                dv_ref[pl.ds(k0, NA_KEYS), :] += dvs[u]
            return carry

        lax.fori_loop(0, rows // NA_BWD_ROWS_PER_STEP, step, 0)
        dk_out[...] = dk_ref[...].astype(dk_out.dtype)
        dv_out[...] = dv_ref[...].astype(dv_out.dtype)

    def cols(first):
        return pl.BlockSpec((T, 128), lambda j: (0, first + j))

    tabs = pl.BlockSpec((2, NA_BASES, GRID_W, NA_KEYS), lambda j: (j, 0, 0, 0))
    wide = jax.ShapeDtypeStruct((T, NA_WIDTH), BF)
    return pl.pallas_call(
        body, name="na_bwd", grid=(n_pairs,),
        in_specs=[cols(0), cols(n_pairs), cols(2 * n_pairs), tabs, cols(0)],
        out_specs=[cols(0), cols(0), cols(0), tabs],
        out_shape=[wide, wide, wide, jax.ShapeDtypeStruct((NA_HEADS, NA_BASES, GRID_W, NA_KEYS), F32)],
        scratch_shapes=[pltpu.VMEM((T, 128), F32), pltpu.VMEM((T, 128), F32)],
        compiler_params=_cparams(("parallel",)),
    )(qkv, qkv, qkv, tab, do)


def _na_bias_table(rpb):
    H, n_rows, n_cols = rpb.shape

    def body(r_ref, tab_ref):
        q = lax.broadcasted_iota(jnp.int32, (GRID_W, 128), 0)
        kc = lax.broadcasted_iota(jnp.int32, (GRID_W, 128), 1)
        first = jnp.clip(q - NA_WIN_COLS // 2, 0, GRID_W - NA_WIN_COLS)
        valid = (kc >= first) & (kc < first + NA_WIN_COLS)
        toeplitz = []
        for ro in range(n_rows):
            row = jnp.broadcast_to(r_ref[pl.ds(ro, 1), :], (GRID_W, 128))
            shifted = pltpu.roll(pltpu.roll(row, 128 - (NA_WIN_COLS - 1), 1), 0, 1, stride=1, stride_axis=0)
            toeplitz.append(jnp.where(valid, shifted, NEG_INF))
        for base in range(NA_BASES):
            for j in range(NA_WIN_ROWS // 2):
                even, odd = toeplitz[base + 2 * j], toeplitz[base + 2 * j + 1]
                tab_ref[base, :, pl.ds(j * 128, 128)] = jnp.where(kc < GRID_W, even, pltpu.roll(odd, GRID_W, 1))

    padded = jnp.pad(rpb, ((0, 0), (0, 16 - n_rows), (0, 128 - n_cols)))
    return pl.pallas_call(
        body, name="na_bias_table", grid=(H,),
        in_specs=[pl.BlockSpec((None, 16, 128), lambda h: (h, 0, 0))],
        out_specs=pl.BlockSpec((None, NA_BASES, GRID_W, NA_KEYS), lambda h: (h, 0, 0, 0)),
        out_shape=jax.ShapeDtypeStruct((H, NA_BASES, GRID_W, NA_KEYS), F32),
        compiler_params=_cparams(("parallel",)),
    )(padded)


def _na_rpb_grad(dtab, after=()):
    H = dtab.shape[0]
    n_rows = 2 * NA_WIN_ROWS - 1
    n_cols = 2 * NA_WIN_COLS - 1

    def body(d_ref, *rest):
        o_ref = rest[-1]
        lane = lax.broadcasted_iota(jnp.int32, (GRID_W, 128), 1)
        low = lane < GRID_W
        flip = (lax.broadcasted_iota(jnp.int32, (GRID_W, GRID_W), 0) + lax.broadcasted_iota(jnp.int32, (GRID_W, GRID_W), 1)
                == GRID_W - 1).astype(BF)

        def reverse_rows(t):
            out = jnp.zeros_like(t)
            for _ in range(3):
                piece = t.astype(BF)
                out = out + jnp.dot(flip, piece, preferred_element_type=F32)
                t = t - piece.astype(F32)
            return out

        out_rows = []
        for ro in range(n_rows):
            acc = jnp.zeros((GRID_W, 128), F32)
            for base in range(NA_BASES):
                i = ro - base
                if not 0 <= i < NA_WIN_ROWS:
                    continue
                pair = d_ref[base, :, pl.ds((i // 2) * 128, 128)]
                if i % 2:
                    pair = pltpu.roll(pair, GRID_W, 1)
                acc = acc + jnp.where(low, pair, 0.0)
            skew = pltpu.roll(reverse_rows(acc), 0, 1, stride=1, stride_axis=0)
            diag = jnp.sum(skew, axis=0, keepdims=True)
            out_rows.append(pltpu.roll(jnp.broadcast_to(diag, (8, 128)), 128 - (GRID_W - NA_WIN_COLS), 1)[:1])
        out_rows.append(jnp.zeros((1, 128), F32))
        res = jnp.concatenate(out_rows, axis=0)
        o_ref[...] = jnp.where(lax.broadcasted_iota(jnp.int32, res.shape, 1) < n_cols, res, 0.0)

    return pl.pallas_call(
        body, name="na_rpb_grad", grid=(H,),
        in_specs=[pl.BlockSpec((None, NA_BASES, GRID_W, NA_KEYS), lambda h: (h, 0, 0, 0))] + [ANY] * len(after),
        out_specs=pl.BlockSpec((None, n_rows + 1, 128), lambda h: (h, 0, 0)),
        out_shape=jax.ShapeDtypeStruct((H, n_rows + 1, 128), F32),
        compiler_params=_cparams(("parallel",)),
    )(dtab, *after)


BAND_Q = 128
BAND_KEYS = BAND_Q + 2 * DIL_RADIUS


def _band_geometry(n, L):
    q0 = pl.multiple_of(n * BAND_Q, BAND_Q)
    k0 = pl.multiple_of(jnp.clip(q0 - DIL_RADIUS, 0, L - BAND_KEYS), DIL_RADIUS)
    qi = q0 + lax.broadcasted_iota(jnp.int32, (BAND_Q, BAND_KEYS), 0)
    kj = k0 + lax.broadcasted_iota(jnp.int32, (BAND_Q, BAND_KEYS), 1)
    return q0, k0, jnp.abs(qi - kj) <= DIL_RADIUS


DIL_PAIRS = DIL_OUT_WIDTH // 128


def _residue_shape(dil, T, dtype):
    return jax.ShapeDtypeStruct((DIL_PAIRS, dil, T // dil, 128), dtype)


def _residue_tile(dil, tm):
    return pl.BlockSpec((DIL_PAIRS, dil, tm // dil, 128), lambda i: (0, 0, i, 0))


def _to_natural(ref, scratch, dil, tm):
    tiles = []
    for pair in range(DIL_PAIRS):
        if dil == 1:
            tiles.append(ref[pair, 0].astype(F32))
            continue
        for r in range(dil):
            scratch[pl.ds(r, tm // dil, stride=dil), :] = ref[pair, r].astype(F32)
        tiles.append(scratch[...])
    return tiles


def _from_natural(tile, scratch, ref, pair, dil, tm):
    if dil == 1:
        ref[pair, 0] = tile.astype(ref.dtype)
        return
    scratch[...] = tile
    for r in range(dil):
        ref[pair, r] = scratch[pl.ds(r, tm // dil, stride=dil), :].astype(ref.dtype)


def _band_specs(group, T):
    dil = DIL_GROUPS[group][1]
    L = T // dil
    assert L % BAND_Q == 0 and L >= BAND_KEYS, (T, dil)
    per_residue = min(BAND_BLOCKS_PER_STEP, L // BAND_Q)
    residues = min(dil, BAND_BLOCKS_PER_STEP // per_residue)
    spec = pl.BlockSpec((None, residues, L, 128), lambda s: (s % DIL_PAIRS, s // DIL_PAIRS, 0, 0))
    return L, residues, per_residue, (dil // residues * DIL_PAIRS,), spec


BAND_BLOCKS_PER_STEP = 8


def _band_softmax(s, valid):
    s = jnp.where(valid, s, NEG_INF)
    m = jnp.max(s, axis=-1, keepdims=True)
    p = jnp.exp(s - m)
    l = jnp.sum(p, axis=-1, keepdims=True)
    return p / l, m + jnp.log(l)


def _band_fwd(q, k, v, group):
    T = q.shape[1] * q.shape[2]
    L, residues, U, grid, spec = _band_specs(group, T)

    def body(q_ref, k_ref, v_ref, o_ref, lse_ref):
        def step(it, carry):
            geo = [(r, *_band_geometry(it * U + u, L)) for r in range(residues) for u in range(U)]
            ss = [lax.dot_general(jnp.concatenate(_split_pair(q_ref[r, pl.ds(q0, BAND_Q), :]), axis=0),
                                  k_ref[r, pl.ds(k0, BAND_KEYS), :], _NT, preferred_element_type=F32) for r, q0, k0, _ in geo]
            pls = [_band_softmax(s, jnp.concatenate([valid, valid], axis=0)) for s, (_, _, _, valid) in zip(ss, geo)]
            os = [jnp.dot(p.astype(BF), v_ref[r, pl.ds(k0, BAND_KEYS), :], preferred_element_type=F32)
                  for (p, _), (r, _, k0, _) in zip(pls, geo)]
            for (r, q0, _, _), o2, (_, lse) in zip(geo, os, pls):
                o_ref[r, pl.ds(q0, BAND_Q), :] = _join_pair(o2[:BAND_Q], o2[BAND_Q:])
                lse2 = jnp.broadcast_to(lse, (2 * BAND_Q, 128))
                lse_ref[r, pl.ds(q0, BAND_Q), :] = _join_pair(lse2[:BAND_Q], lse2[BAND_Q:])
            return carry

        lax.fori_loop(0, L // (BAND_Q * U), step, 0)

    res = _residue_shape(DIL_GROUPS[group][1], T, F32)
    return pl.pallas_call(
        body, name=f"band_fwd_g{group}", grid=grid,
        in_specs=[spec] * 3, out_specs=[spec] * 2, out_shape=[res, res],
        compiler_params=_cparams(("parallel",)),
    )(q, k, v)


def _band_bwd(q, k, v, do, dlse, group):
    T = q.shape[1] * q.shape[2]
    L, residues, U, grid, spec = _band_specs(group, T)

    def body(q_ref, k_ref, v_ref, do_ref, dlse_ref, dq_ref, dk_ref, dv_ref):
        dk_ref[...] = jnp.zeros_like(dk_ref)
        dv_ref[...] = jnp.zeros_like(dv_ref)

        def step(it, carry):
            geo = [(r, *_band_geometry(it * U + u, L)) for r in range(residues) for u in range(U)]
            q2s = [jnp.concatenate(_split_pair(q_ref[r, pl.ds(q0, BAND_Q), :]), axis=0) for r, q0, _, _ in geo]
            do2s = [jnp.concatenate(_split_pair(do_ref[r, pl.ds(q0, BAND_Q), :]), axis=0) for r, q0, _, _ in geo]
            ss = [lax.dot_general(q2, k_ref[r, pl.ds(k0, BAND_KEYS), :], _NT, preferred_element_type=F32)
                  for q2, (r, _, k0, _) in zip(q2s, geo)]
            dps = [lax.dot_general(do2, v_ref[r, pl.ds(k0, BAND_KEYS), :], _NT, preferred_element_type=F32)
                   for do2, (r, _, k0, _) in zip(do2s, geo)]
            ps = [_band_softmax(s, jnp.concatenate([valid, valid], axis=0))[0] for s, (_, _, _, valid) in zip(ss, geo)]
            dss = []
            for p, dp, (r, q0, _, _) in zip(ps, dps, geo):
                dl = dlse_ref[r, pl.ds(q0, BAND_Q), :]
                dl2 = jnp.concatenate([dl[:, :1], dl[:, HEAD_DIM:HEAD_DIM + 1]], axis=0)
                dss.append(p * (dp - jnp.sum(dp * p, axis=-1, keepdims=True) + dl2))
            dvs = [lax.dot_general(p.astype(BF), do2, _TN, preferred_element_type=F32) for p, do2 in zip(ps, do2s)]
            dsbs = [ds.astype(BF) for ds in dss]
            dqs = [jnp.dot(dsb, k_ref[r, pl.ds(k0, BAND_KEYS), :], preferred_element_type=F32) for dsb, (r, _, k0, _) in zip(dsbs, geo)]
            dks = [lax.dot_general(dsb, q2, _TN, preferred_element_type=F32) for dsb, q2 in zip(dsbs, q2s)]
            for u, (r, q0, k0, _) in enumerate(geo):
                dq_ref[r, pl.ds(q0, BAND_Q), :] = _join_pair(dqs[u][:BAND_Q], dqs[u][BAND_Q:])
                dk_ref[r, pl.ds(k0, BAND_KEYS), :] += dks[u]
                dv_ref[r, pl.ds(k0, BAND_KEYS), :] += dvs[u]
            return carry

        lax.fori_loop(0, L // (BAND_Q * U), step, 0)

    res = _residue_shape(DIL_GROUPS[group][1], T, F32)
    return pl.pallas_call(
        body, name=f"band_bwd_g{group}", grid=grid,
        in_specs=[spec] * 5, out_specs=[spec] * 3, out_shape=[res] * 3,
        compiler_params=_cparams(("parallel",)),
    )(q, k, v, do, dlse)


def _head_sums(t):
    head = lax.broadcasted_iota(jnp.int32, t.shape, 1) // HEAD_DIM
    out = jnp.zeros_like(t)
    for h in range(t.shape[1] // HEAD_DIM):
        mine = head == h
        out = jnp.where(mine, jnp.sum(jnp.where(mine, t, 0.0), axis=-1, keepdims=True), out)
    return out


def _dil_merge_fwd(os, lses, T, tm):
    G = len(DIL_GROUPS)
    W = DIL_OUT_WIDTH
    dils = [d for _, d in DIL_GROUPS]

    def body(*refs):
        o_refs, lse_refs = refs[:G], refs[G:2 * G]
        y_ref, w_refs, on_refs, scratch = refs[2 * G], refs[2 * G + 1:3 * G + 1], refs[3 * G + 1:4 * G + 1], refs[-1]
        o = [jnp.concatenate(_to_natural(r, scratch, d, tm), axis=1) for r, d in zip(o_refs, dils)]
        ls = [jnp.concatenate(_to_natural(r, scratch, d, tm), axis=1) for r, d in zip(lse_refs, dils)]
        m = functools.reduce(jnp.maximum, ls)
        es = [jnp.exp(l - m) for l in ls]
        tot = functools.reduce(jnp.add, es)
        ws = [e / tot for e in es]
        y_ref[...] = functools.reduce(jnp.add, [w * t for w, t in zip(ws, o)]).astype(y_ref.dtype)
        for g in range(G):
            w_refs[g][...] = ws[g]
            on_refs[g][...] = o[g]

    nat = pl.BlockSpec((tm, W), lambda i: (i, 0))
    res = pl.pallas_call(
        body, name="dil_merge_fwd", grid=(T // tm,),
        in_specs=[_residue_tile(d, tm) for d in dils] * 2,
        out_specs=[nat] * (2 * G + 1),
        out_shape=[jax.ShapeDtypeStruct((T, W), BF)] + [jax.ShapeDtypeStruct((T, W), F32)] * (2 * G),
        scratch_shapes=[pltpu.VMEM((tm, 128), F32)],
        compiler_params=_cparams(("parallel",)),
    )(*os, *lses)
    return res[0], res[1:G + 1], res[G + 1:]


def _dil_merge_bwd(dy, os, ws, tm, after=()):
    G = len(DIL_GROUPS)
    T, W = dy.shape
    dils = [d for _, d in DIL_GROUPS]
    n_after = len(after)

    def body(*refs):
        dyt = refs[0][...]
        o, w = [r[...] for r in refs[1:G + 1]], [r[...] for r in refs[G + 1:2 * G + 1]]
        refs = refs[2 * G + 1 + n_after:]
        do_refs, dlse_refs, scratch = refs[:G], refs[G:2 * G], refs[-1]
        dws = [_head_sums(dyt * t) for t in o]
        mean = functools.reduce(jnp.add, [a * b for a, b in zip(w, dws)])
        for g, d in enumerate(dils):
            do, dlse = w[g] * dyt, w[g] * (dws[g] - mean)
            for pair in range(DIL_PAIRS):
                cols = slice(pair * 128, (pair + 1) * 128)
                _from_natural(do[:, cols], scratch, do_refs[g], pair, d, tm)
                _from_natural(dlse[:, cols], scratch, dlse_refs[g], pair, d, tm)

    nat = pl.BlockSpec((tm, W), lambda i: (i, 0))
    res = pl.pallas_call(
        body, name="dil_merge_bwd", grid=(T // tm,),
        in_specs=[nat] * (2 * G + 1) + [ANY] * n_after,
        out_specs=[_residue_tile(d, tm) for d in dils] * 2,
        out_shape=[_residue_shape(d, T, BF) for d in dils] + [_residue_shape(d, T, F32) for d in dils],
        scratch_shapes=[pltpu.VMEM((tm, 128), F32)],
        compiler_params=_cparams(("parallel",)),
    )(dy, *os, *ws, *after)
    return res[:G], res[G:]


def _qkv_prep(z, cos2, sin_signed, tm):
    T = z.shape[0]
    G = len(DIL_GROUPS)
    dils = [d for _, d in DIL_GROUPS]
    n_dil_blocks = 3 * DIL_WIDTH // 128

    def body(*refs):
        blocks = refs[:n_dil_blocks]
        cos_ref, sin_ref = refs[n_dil_blocks], refs[1 + n_dil_blocks]
        outs = refs[2 + n_dil_blocks:]
        for part in range(3):
            for g, d in enumerate(dils):
                out = outs[g * 3 + part]
                for pair in range(DIL_PAIRS):
                    blk = blocks[part * (DIL_WIDTH // 128) + g * DIL_PAIRS + pair]
                    for r in range(d):
                        rows = pl.ds(r, tm // d, stride=d) if d > 1 else slice(None)
                        x = blk[rows, :]
                        if part < 2:
                            x = _rope(x, cos_ref[rows, :], sin_ref[rows, :])
                        if part == 0:
                            x = x * Q_SCALE
                        out[pair, r] = x.astype(out.dtype)

    lane_block = [pl.BlockSpec((tm, 128), functools.partial(lambda b, i: (i, b), b)) for b in range(n_dil_blocks)]
    tab = pl.BlockSpec((tm, 128), lambda i: (i, 0))
    res = pl.pallas_call(
        body, name="qkv_prep", grid=(T // tm,),
        in_specs=lane_block + [tab, tab],
        out_specs=[_residue_tile(d, tm) for d in dils for _ in range(3)],
        out_shape=[_residue_shape(d, T, BF) for d in dils for _ in range(3)],
        compiler_params=_cparams(("parallel",)),
    )(*[z] * n_dil_blocks, cos2, sin_signed)
    return [res[3 * g:3 + 3 * g] for g in range(G)]


def _qkv_unprep(d_na, d_dil, cos2, sin_signed, tm, after=()):
    T = d_na[0].shape[0]
    G = len(DIL_GROUPS)
    dils = [d for _, d in DIL_GROUPS]
    n_after = len(after)

    def body(*refs):
        dq, dk, dv = (r[...] for r in refs[:3])
        res_refs = refs[3:3 + 3 * G]
        cs, sn = refs[3 + 3 * G][...], refs[4 + 3 * G][...]
        out, scratch = refs[5 + 3 * G + n_after], refs[-1]
        cols = [dq, dk, dv]
        for part in range(3):
            for g, d in enumerate(dils):
                for x in _to_natural(res_refs[g * 3 + part], scratch, d, tm):
                    if part < 2:
                        x = _rope(x, cs, -sn)
                    cols.append((x * Q_SCALE if part == 0 else x).astype(out.dtype))
        out[...] = jnp.concatenate(cols, axis=1)

    wide = pl.BlockSpec((tm, NA_WIDTH), lambda i: (i, 0))
    tab = pl.BlockSpec((tm, 128), lambda i: (i, 0))
    return pl.pallas_call(
        body, name="qkv_unprep", grid=(T // tm,),
        in_specs=[wide] * 3 + [_residue_tile(d, tm) for d in dils for _ in range(3)] + [tab, tab] + [ANY] * n_after,
        out_specs=pl.BlockSpec((tm, QKV_WIDTH), lambda i: (i, 0)),
        out_shape=jax.ShapeDtypeStruct((T, QKV_WIDTH), BF),
        scratch_shapes=[pltpu.VMEM((tm, 128), F32)],
        compiler_params=_cparams(("parallel",)),
    )(*d_na, *[t for g in range(G) for t in d_dil[g]], cos2, sin_signed, *after)


def _rope_tables(positions):
    half = HEAD_DIM // 2
    inv_freq = ROPE_THETA ** (-jnp.arange(half, dtype=F32) / half)
    ang = positions.astype(F32)[:, None] * inv_freq
    cos, sin = jnp.cos(ang), jnp.sin(ang)
    return jnp.tile(jnp.concatenate([cos, cos], axis=1), (1, 2)), jnp.tile(jnp.concatenate([-sin, sin], axis=1), (1, 2))


def _pack_rows(t):
    return t.reshape(-1, PACK_W)


def _me():
    return lax.axis_index("x"), lax.axis_index("y"), lax.axis_index("c")


def _other_chips(x, y):
    return [(1 - x, y), (x, 1 - y), (1 - x, 1 - y)]


def _pair_sum(g, got, tm):
    S, R, W = g.shape
    half = R // 2
    nb = half // tm

    def body(pos_ref, g_ref, got_ref, own_ref, ob_ref):
        tot = g_ref[...] + got_ref[...]
        ob_ref[...] = tot.astype(ob_ref.dtype)

        @pl.when(pl.program_id(1) == pos_ref[1])
        def _():
            own_ref[...] = tot

    tile = pl.BlockSpec((None, tm, W), lambda i, s, pos_ref: (s, i, 0))
    c, chip = lax.axis_index("c"), 2 * lax.axis_index("x") + lax.axis_index("y")
    return pl.pallas_call(
        body, name="pair_sum",
        grid_spec=pltpu.PrefetchScalarGridSpec(
            num_scalar_prefetch=1, grid=(nb, S),
            in_specs=[pl.BlockSpec((None, tm, W), lambda i, s, pos_ref: (s, pos_ref[0] * nb + i, 0)), tile],
            out_specs=[pl.BlockSpec((tm, W), lambda i, s, pos_ref: (i, 0)), tile]),
        out_shape=[jax.ShapeDtypeStruct((half, W), F32), jax.ShapeDtypeStruct((S, half, W), BF)],
        compiler_params=_cparams(("parallel", "arbitrary")),
    )(jnp.stack([c, chip]).astype(jnp.int32), g, got)


def _chip_sum(own, others, tm):
    n, h, W = others.shape
    nb = h // tm

    def body(c_ref, own_ref, p_ref, o_ref):
        o_ref[...] = ((own_ref[...] + p_ref[0].astype(F32)) + p_ref[1].astype(F32)) + p_ref[2].astype(F32)

    return pl.pallas_call(
        body, name="chip_sum",
        grid_spec=pltpu.PrefetchScalarGridSpec(
            num_scalar_prefetch=1, grid=(nb,),
            in_specs=[pl.BlockSpec((tm, W), lambda i, c_ref: (i, 0)), pl.BlockSpec((n, tm, W), lambda i, c_ref: (0, i, 0))],
            out_specs=pl.BlockSpec((tm, W), lambda i, c_ref: (c_ref[0] * nb + i, 0))),
        out_shape=jax.ShapeDtypeStruct((2 * h, W), F32),
        compiler_params=_cparams(("parallel",)),
    )(lax.axis_index("c").reshape(1).astype(jnp.int32), own, others)


def _join_halves(shard, after=()):
    h = shard.shape[0] // 2

    def body(in_ref, *rest):
        out_ref, send_sem, recv_sem = rest[len(after):]
        x, y, c = _me()
        cp = pltpu.make_async_remote_copy(
            src_ref=in_ref.at[pl.ds(c * h, h), :], dst_ref=out_ref.at[pl.ds(c * h, h), :],
            send_sem=send_sem, recv_sem=recv_sem, device_id=(x, y, 1 - c), device_id_type=MESH)
        cp.start()
        pltpu.make_async_remote_copy(
            src_ref=in_ref.at[pl.ds(c * h, h), :], dst_ref=out_ref.at[pl.ds((1 - c) * h, h), :],
            send_sem=send_sem, recv_sem=recv_sem, device_id=(x, y, 1 - c), device_id_type=MESH).wait_recv()
        cp.wait_send()

    return pl.pallas_call(
        body, name="join_halves", in_specs=[ANY] * (1 + len(after)), out_specs=ANY,
        out_shape=jax.ShapeDtypeStruct(shard.shape, shard.dtype), input_output_aliases={0: 0},
        scratch_shapes=[pltpu.SemaphoreType.DMA, pltpu.SemaphoreType.DMA],
    )(shard, *after)


def _allreduce_copies(src_ref, land_ref):
    x, y, c = _me()
    peers = [((x + fx) % 2, (y + fy) % 2, (c + fc) % 2) for fx in range(2) for fy in range(2) for fc in range(2)][1:]
    return [(src_ref, land_ref.at[4 * x + 2 * y + c], peer) for peer in peers]


def _allreduce_start(s, after):
    return _split_start("allreduce_small_start", s, (N_DEV, *s.shape), s.dtype, N_DEV - 1, _allreduce_copies, after)


def _allreduce_finish(flight, after):
    own, landed = _split_wait("allreduce_small_wait", flight, after, N_DEV - 1, _allreduce_copies)
    me = 4 * lax.axis_index("x") + 2 * lax.axis_index("y") + lax.axis_index("c")
    parts = lax.dynamic_update_slice(landed, own[None], (me, 0, 0))

    def body(p_ref, o_ref):
        total = p_ref[0]
        for d in range(1, N_DEV):
            total = total + p_ref[d]
        o_ref[...] = total

    return pl.pallas_call(
        body, name="allreduce_small_sum",
        in_specs=[pl.BlockSpec(memory_space=pltpu.VMEM)], out_specs=pl.BlockSpec(memory_space=pltpu.VMEM),
        out_shape=jax.ShapeDtypeStruct(own.shape, F32),
    )(parts)


HBM_SPEC = pl.BlockSpec(memory_space=pltpu.HBM)
SEM_SPEC = pl.BlockSpec(memory_space=pltpu.SEMAPHORE)
DATAFLOW = pltpu.SideEffectType.DATAFLOW_SIDE_EFFECTING


class _InFlight(NamedTuple):
    sems: tuple
    src: jax.Array
    land: jax.Array
    token: jax.Array


def _split_start(name, src, land_shape, land_dtype, n, copies, after=()):
    n_after = len(after)

    def body(src_ref, land_ref, *rest):
        rest = rest[n_after:]
        sems, token = rest[:2 * n], rest[-1]
        for k, (s, d, peer) in enumerate(copies(src_ref, land_ref)):
            pltpu.make_async_remote_copy(src_ref=s, dst_ref=d, send_sem=sems[k], recv_sem=sems[n + k],
                                         device_id=peer, device_id_type=MESH).start()
        token[...] = jnp.zeros_like(token)

    outs = pl.pallas_call(
        body, name=name,
        out_shape=(*[pltpu.SemaphoreType.DMA(())] * (2 * n), pltpu.HBM(src.shape, src.dtype), pltpu.HBM(land_shape, land_dtype),
                   jax.ShapeDtypeStruct((8, 128), F32)),
        in_specs=(HBM_SPEC, HBM_SPEC, *[ANY] * n_after),
        out_specs=(*[SEM_SPEC] * (2 * n), HBM_SPEC, HBM_SPEC, pl.BlockSpec(memory_space=pltpu.VMEM)),
        input_output_aliases={0: 2 * n, 1: 2 * n + 1},
        compiler_params=pltpu.CompilerParams(has_side_effects=DATAFLOW),
    )(pltpu.with_memory_space_constraint(src, pltpu.HBM), pltpu.with_memory_space_constraint(lax.empty(land_shape, land_dtype), pltpu.HBM),
      *after)
    return _InFlight(tuple(outs[:2 * n]), outs[2 * n], outs[2 * n + 1], outs[2 * n + 2])


def _split_wait(name, flight, after, n, copies):
    after = after if isinstance(after, tuple) else (after,)

    def body(src_ref, land_ref, *rest):
        sems = rest[:2 * n]
        for k, (s, d, peer) in enumerate(copies(src_ref, land_ref)):
            cp = pltpu.make_async_remote_copy(src_ref=s, dst_ref=d, send_sem=sems[k], recv_sem=sems[n + k],
                                              device_id=peer, device_id_type=MESH)
            cp.wait_send()
            cp.wait_recv()

    return pl.pallas_call(
        body, name=name,
        out_shape=(pltpu.HBM(flight.src.shape, flight.src.dtype), pltpu.HBM(flight.land.shape, flight.land.dtype)),
        in_specs=(HBM_SPEC, HBM_SPEC, *[SEM_SPEC] * (2 * n), *[ANY] * len(after)),
        out_specs=(HBM_SPEC, HBM_SPEC), input_output_aliases={0: 0, 1: 1},
        compiler_params=pltpu.CompilerParams(has_side_effects=DATAFLOW),
    )(flight.src, flight.land, *flight.sems, *after)


def _gather_copies(src_ref, land_ref):
    x, y, c = _me()
    return [(src_ref, land_ref.at[2 * x + y], (*chip, c)) for chip in _other_chips(x, y)]


def _gather_start(packed, tag, after=()):
    return _split_start(f"gather_start_{tag}", packed, (N_CHIPS, *packed.shape), packed.dtype, 3, _gather_copies, after)


def _gather_wait(flight, after, tag):
    src, others = _split_wait(f"gather_wait_{tag}", flight, after, 3, _gather_copies)
    return lax.dynamic_update_slice(others, src[None], (2 * lax.axis_index("x") + lax.axis_index("y"), 0, 0))


def _across_copies(src_ref, land_ref):
    x, y, c = _me()
    half = src_ref.shape[0] // 2
    rows = pl.ds(c * half, half)
    return [(src_ref.at[rows, :], land_ref.at[2 * x + y, rows, :], (*chip, c)) for chip in _other_chips(x, y)]


def _to_sibling_copies(all_ref, unused_ref):
    x, y, c = _me()
    half = all_ref.shape[1] // 2
    places = [all_ref.at[2 * chip[0] + chip[1], pl.ds(c * half, half), :] for chip in _other_chips(x, y)]
    return [(place, place, (x, y, 1 - c)) for place in places]


def _gather_halves_start(shard, tag, after=()):
    return _split_start(f"gather_{tag}_across_start", shard, (N_CHIPS, *shard.shape), shard.dtype, 3, _across_copies, after)


def _gather_halves_relay(flight, after, tag):
    shard, landed = _split_wait(f"gather_{tag}_across_wait", flight, after, 3, _across_copies)
    return shard, _split_start(f"gather_{tag}_sibling_start", landed, (8, 128), landed.dtype, 3, _to_sibling_copies)


def _gather_halves_finish(shard, relay, after, tag):
    others = _split_wait(f"gather_{tag}_sibling_wait", relay, after, 3, _to_sibling_copies)[0]
    return lax.dynamic_update_slice(others, shard[None], (2 * lax.axis_index("x") + lax.axis_index("y"), 0, 0))


def _assemble_w_in(shards, tm):
    S, R, C = shards.shape
    n_gates = 2 * D_MODEL

    def body(s_ref, w_ref, g_ref):
        full = jnp.concatenate([s_ref[s] for s in range(S)], axis=1)
        w_ref[...] = full
        g_ref[...] = full[:, S * C - n_gates:]

    return pl.pallas_call(
        body, name="assemble_w_in", grid=(R // tm,),
        in_specs=[pl.BlockSpec((S, tm, C), lambda i: (0, i, 0))],
        out_specs=[pl.BlockSpec((tm, S * C), lambda i: (i, 0)), pl.BlockSpec((tm, n_gates), lambda i: (i, 0))],
        out_shape=[jax.ShapeDtypeStruct((R, S * C), shards.dtype), jax.ShapeDtypeStruct((R, n_gates), shards.dtype)],
        compiler_params=_cparams(("parallel",)),
    )(shards)


def _swap_copies(src_ref, land_ref):
    x, y, c = _me()
    half = land_ref.shape[1]
    return [(src_ref.at[:, pl.ds((1 - c) * half, half), :], land_ref, (x, y, 1 - c))]


def _swap_start(g, tag):
    S, R, W = g.shape
    return _split_start(f"swap_halves_start_{tag}", g, (S, R // 2, W), g.dtype, 1, _swap_copies)


def _swap_wait(flight, after, tag):
    return _split_wait(f"swap_halves_wait_{tag}", flight, after, 1, _swap_copies)


def _scatter_copies(src_ref, land_ref):
    x, y, c = _me()
    return [(src_ref.at[2 * chip[0] + chip[1]], land_ref.at[j], (*chip, c)) for j, chip in enumerate(_other_chips(x, y))]


def _scatter_start(part, tag):
    S, h, W = part.shape
    return _split_start(f"scatter_chips_start_{tag}", part, (S - 1, h, W), part.dtype, 3, _scatter_copies)


def _scatter_wait(flight, after, tag):
    return _split_wait(f"scatter_chips_wait_{tag}", flight, after, 3, _scatter_copies)[1]


def _join_copies(shard_ref, unused_ref):
    x, y, c = _me()
    h = shard_ref.shape[0] // 2
    rows = shard_ref.at[pl.ds(c * h, h), :]
    return [(rows, rows, (x, y, 1 - c))]


def _join_start(shard):
    return _split_start("join_halves_start", shard, (8, 128), shard.dtype, 1, _join_copies)


def _join_wait(flight, after):
    return _split_wait("join_halves_wait", flight, after, 1, _join_copies)[0]


def _adamw(name, g, g_row0, w, m, v):
    _, R, C = w.shape
    tm = next(cand for cand in (368, 256, 128, 64, 32, 16, 8) if R % cand == 0)
    assert g_row0 % tm == 0 and g.shape[1] == C

    def body(g_ref, w_ref, m_ref, v_ref, go_ref, d_ref, mo_ref, vo_ref):
        gt = g_ref[...]
        mt = ADAM_B1 * m_ref[...] + (1.0 - ADAM_B1) * gt
        vt = ADAM_B2 * v_ref[...] + (1.0 - ADAM_B2) * jnp.square(gt)
        m_hat = mt / (1.0 - ADAM_B1 ** ADAM_STEP)
        v_hat = vt / (1.0 - ADAM_B2 ** ADAM_STEP)
        go_ref[...] = gt
        d_ref[...] = -ADAM_LR * (m_hat / (jnp.sqrt(v_hat) + ADAM_EPS) + ADAM_WD * w_ref[...])
        mo_ref[...] = mt
        vo_ref[...] = vt

    state = pl.BlockSpec((None, tm, C), lambda i: (0, i, 0))
    return pl.pallas_call(
        body, name=name, grid=(R // tm,),
        in_specs=[pl.BlockSpec((tm, C), lambda i: (g_row0 // tm + i, 0)), state, state, state],
        out_specs=[state] * 4, out_shape=[jax.ShapeDtypeStruct((1, R, C), F32)] * 4,
        compiler_params=_cparams(("parallel",)),
    )(g, w, m, v)


def _unpack_weights(gathered, names):
    S = gathered.shape[0]
    shard_shapes = {"w_in": (D_MODEL, (QKV_WIDTH + 2 * D_MODEL) // S), "w_branch_na": (NA_WIDTH, D_MODEL // S),
                    "w_branch_dil": (DIL_OUT_WIDTH, D_MODEL // S), "w_out": (D_MODEL // S, D_MODEL),
                    "w_up": (D_MODEL, D_FF // S), "w_down": (D_FF // S, D_MODEL),
                    "w_ple_gate": (D_MODEL // S, D_MODEL), "w_ple_proj": (PLE_DIM, D_MODEL // S)}
    col_sharded = {"w_in", "w_branch_na", "w_branch_dil", "w_up", "w_ple_proj"}
    out, r0 = {}, 0
    for name in names:
        rows, cols = shard_shapes[name]
        n = rows * cols // PACK_W
        t = gathered[:, r0:r0 + n, :].reshape(S, rows, cols)
        r0 += n
        out[name] = t.transpose(1, 0, 2).reshape(rows, S * cols) if name in col_sharded else t.reshape(S * rows, cols)
    return out


def kernel(x, p, positions, g_mix, w_in, rpb, w_branch_na, w_branch_dil, w_out, g_mlp, w_up, w_down, g_ple, w_ple_gate, w_ple_proj, g_final, loss_target, m_g_mix, m_w_in, m_rpb, m_w_branch_na, m_w_branch_dil, m_w_out, m_g_mlp, m_w_up, m_w_down, m_g_ple, m_w_ple_gate, m_w_ple_proj, m_g_final, v_g_mix, v_w_in, v_rpb, v_w_branch_na, v_w_branch_dil, v_w_out, v_g_mlp, v_w_up, v_w_down, v_g_ple, v_w_ple_gate, v_w_ple_proj, v_g_final):
    shards = {"w_in": w_in[0], "w_branch_na": w_branch_na[0], "w_branch_dil": w_branch_dil[0], "w_out": w_out[0],
              "w_up": w_up[0], "w_down": w_down[0], "w_ple_gate": w_ple_gate[0], "w_ple_proj": w_ple_proj[0]}
    params = {"w_in": w_in, "w_branch_na": w_branch_na, "w_branch_dil": w_branch_dil, "w_out": w_out, "w_up": w_up,
              "w_down": w_down, "w_ple_gate": w_ple_gate, "w_ple_proj": w_ple_proj,
              "m_w_in": m_w_in, "m_w_branch_na": m_w_branch_na, "m_w_branch_dil": m_w_branch_dil, "m_w_out": m_w_out,
              "m_w_up": m_w_up, "m_w_down": m_w_down, "m_w_ple_gate": m_w_ple_gate, "m_w_ple_proj": m_w_ple_proj,
              "v_w_in": v_w_in, "v_w_branch_na": v_w_branch_na, "v_w_branch_dil": v_w_branch_dil, "v_w_out": v_w_out,
              "v_w_up": v_w_up, "v_w_down": v_w_down, "v_w_ple_gate": v_w_ple_gate, "v_w_ple_proj": v_w_ple_proj}

    xs, ps, tgt = x[0], p[0, 0], loss_target[0]
    T = xs.shape[0]
    TM = 512
    gm, gl, gp, gf = g_mix, g_mlp, g_ple, g_final.reshape(1, D_MODEL)

    across = _gather_halves_start(shards["w_in"].astype(BF), "in")
    a = _rowwise("norm_mix", lambda h, g: h * _rms(h) * g, T, TM, [_row(xs, TM), _full(gm)], [(D_MODEL, BF)],
                 after=(across.token,))
    cos2, sin_signed = _rope_tables(positions[0])
    tab = _na_bias_table(rpb[0])
    packed_mix = jnp.concatenate([_pack_rows(shards[n].astype(BF)) for n in GATHER_MIX], axis=0)
    packed_mlp = jnp.concatenate([_pack_rows(shards[n].astype(BF)) for n in GATHER_MLP], axis=0)
    w_in_shard, w_in_relay = _gather_halves_relay(across, (a, tab, cos2, sin_signed, packed_mix, packed_mlp), "in")
    w_in_all = _gather_halves_finish(w_in_shard, w_in_relay, w_in_relay.token, "in")
    w_in_full, w_gates = _assemble_w_in(w_in_all, 256)
    W = {"w_in": w_in_full}
    mix_flight = _gather_start(packed_mix, "mix", after=(w_in_all,))
    mlp_across = _gather_halves_start(packed_mlp, "mlp", after=(mix_flight.token,))

    n3 = 3 * NA_WIDTH
    qkv = _mm("in_na", a, W["w_in"], "nn", 1024, 768, 1024, [BF], after=(mlp_across.token,),
              b_view=(n3, (D_MODEL, 768), lambda j, k: (k, j)))
    z_dil = _mm("in_dil", a, W["w_in"], "nn", 1024, 768, 1024, [F32], after=(mlp_across.token,),
                b_view=(3 * DIL_WIDTH, (D_MODEL, 768), lambda j, k: (k, n3 // 768 + j)))
    z_gates = _mm("in_gates", a, w_gates, "nn", 1024,1024, 1024, [BF], after=(mlp_across.token,))

    dil_ops = _qkv_prep(z_dil, cos2, sin_signed, TM)
    y_na = _na_fwd(qkv, tab)
    band = [_band_fwd(*dil_ops[g], g) for g in range(len(DIL_GROUPS))]
    y_dil, w_grp, o_nat = _dil_merge_fwd([b[0] for b in band], [b[1] for b in band], T, TM)

    W.update(_unpack_weights(_gather_wait(mix_flight, y_dil, "mix"), GATHER_MIX))
    mlp_shard, mlp_relay = _gather_halves_relay(mlp_across, y_dil, "mlp")
    u_na = _mm("branch_na", y_na, W["w_branch_na"], "nn", 1024,1024, 512, [BF], after=(mlp_relay.token,))
    def gate_mix(acc, gn, gd, un):
        ud = acc.astype(BF)
        return ud, _sigmoid(gn.astype(F32)) * un.astype(F32) + _sigmoid(gd.astype(F32)) * ud.astype(F32)

    u_dil, mixed = _mm("branch_dil", y_dil, W["w_branch_dil"], "nn", 512, 1024, 256, [BF, BF], epilogue=gate_mix,
                       extras=((z_gates, 0), (z_gates, 1), u_na))

    def add_norm(d, h, g):
        h = h + d
        return h, h * _rms(h) * g

    h1, cn = _mm("out_proj", mixed, W["w_out"], "nn", 512, 1024, 1024, [F32, BF], epilogue=add_norm, extras=(xs,), consts=(gl,))
    mlp_all = _gather_halves_finish(mlp_shard, mlp_relay, cn, "mlp")
    W.update({n: t for n, t in _unpack_weights(mlp_all, GATHER_MLP).items() if n.startswith("w_ple")})
    chip_block = (None, D_MODEL, PACK_W)
    up, act = _mm("mlp_up", cn, mlp_all, "nn", 1024,1024, 1024, [BF, BF],
                  epilogue=lambda acc: (acc, jnp.square(jnp.maximum(acc, 0.0))), b_view=(D_FF, chip_block, lambda j, k: (j, 0, 0)))
    h2, en = _mm("mlp_down", act, mlp_all, "nn", 1024, 1024, 2048, [F32, BF], epilogue=add_norm, extras=(h1,), consts=(gp,),
                 b_view=(D_MODEL, (2, D_MODEL, PACK_W), lambda j, k: (k, 1, 0)))
    pp = _mm("ple_proj", ps, W["w_ple_proj"], "nn", 1024,1024, 256, [F32])

    def head(gtt, h2t, ppt, tg, g):
        sg = _sigmoid(gtt)
        h3 = h2t + sg * ppt
        yo = h3 * _rms(h3) * g
        diff = yo - tg
        loss = 0.5 * jnp.sum(jnp.mean(jnp.square(diff), axis=-1, keepdims=True), axis=0, keepdims=True)
        dh3, dg = _rms_bwd(diff * (1.0 / D_MODEL), h3, g)
        return dh3, dh3 * ppt * sg * (1.0 - sg), dh3 * sg, jnp.broadcast_to(loss, (1, 128)), dg

    dh3, d_gt, d_pp, loss_part, dg_final = _mm(
        "ple_gate_loss_head", en, W["w_ple_gate"], "nn", 512, 1024, 1024, [F32, BF, BF], epilogue=head,
        extras=(h2, pp, tgt), consts=(gf,), sums=[128, D_MODEL])

    early_shapes = {n: shards[n].shape for n in REDUCE_EARLY}
    early_rows = sum(r * c for r, c in early_shapes.values()) // PACK_W
    shard_rows = D_MODEL // N_CHIPS
    early_buf = _mm("g_ple_gate", en, d_gt, "tn", 1024, 1024, 2048, [F32],
                    into=(jax.ShapeDtypeStruct((N_CHIPS, early_rows, PACK_W), F32), (N_CHIPS, shard_rows, PACK_W),
                          lambda i, j: (0, 2 * D_MODEL // shard_rows, 0)))
    g_ple_proj = _mm("g_ple_proj", ps, d_pp, "tn", 256, 1024, 1024,[F32])

    def add_norm_bwd(dn, dh_out, h, g):
        dh, dg = _rms_bwd(dn, h, g)
        dh = dh_out + dh
        return dh, dh, dg

    dh2, dh2_b, dg_ple = _mm("d_ple_gate", d_gt, W["w_ple_gate"], "nt", 512, 1024, 1024, [F32, BF],
                             epilogue=add_norm_bwd, extras=(dh3, h2), consts=(gp,), sums=[D_MODEL])
    d_up = _mm("d_mlp_down", dh2_b, mlp_all, "nt", 1024,1024, 1024, [BF], b_view=(D_FF, chip_block, lambda j, k: (j, 1, 0)),
               epilogue=lambda acc, u: (acc * (2.0 * jnp.maximum(u.astype(F32), 0.0)),), extras=(up,))
    early_buf = _mm("g_mlp_down", act, dh2_b, "tn", 1024, 1024, 2048, [F32],
                    into=(early_buf, (None, D_MODEL, PACK_W), lambda i, j: (i, 1, 0)))
    early_buf = _mm("g_mlp_up", cn, d_up, "tn", 1024, 1024, 2048, [F32],
                    into=(early_buf, (None, D_MODEL, PACK_W), lambda i, j: (j, 0, 0)))
    dh1, dh1_b, dg_mlp = _mm("d_mlp_up", d_up, mlp_all, "nt", 1024, 1024, 1024, [F32, BF], epilogue=add_norm_bwd,
                             b_view=(D_MODEL, chip_block, lambda j, k: (k, 0, 0)),
                             extras=(dh2, h1), consts=(gl,), sums=[D_MODEL])
    early_buf = _mm("g_out_proj", mixed, dh1_b, "tn", 1024, 1024, 1024, [F32],
                    into=(early_buf, (N_CHIPS, shard_rows, PACK_W), lambda i, j: (0, 2 * D_MODEL // shard_rows + 1, 0)))

    def gate_bwd(dm, gn, gd, un, ud):
        gn, gd, un, ud = (t.astype(F32) for t in (gn, gd, un, ud))
        sn, sd = _sigmoid(gn), _sigmoid(gd)
        return jnp.concatenate([dm * un * sn * (1.0 - sn), dm * ud * sd * (1.0 - sd)], axis=1), dm * sn, dm * sd

    dz_gates, d_u_na, d_u_dil = _mm("d_out_proj", dh1_b, W["w_out"], "nt", 512, 1024, 1024, [(BF, 2 * D_MODEL), BF, BF],
                                    epilogue=gate_bwd, extras=((z_gates, 0), (z_gates, 1), u_na, u_dil))
    g_branch_na = _mm("g_branch_na", y_na, d_u_na, "tn", 1024, 1024, 1024,[F32])
    g_branch_dil = _mm("g_branch_dil", y_dil, d_u_dil, "tn", 256, 1024, 1024,[F32])
    small_rows = [jnp.concatenate([_pack_rows(g[:, s * shard_rows:(s + 1) * shard_rows]) for g in (g_ple_proj, g_branch_na, g_branch_dil)],
                                  axis=0) for s in range(N_CHIPS)]
    early_buf = lax.dynamic_update_slice(early_buf, jnp.stack(small_rows), (0, 2 * D_MODEL + 2 * shard_rows, 0))
    early_tm = early_rows // 4
    swap_flight = _swap_start(early_buf, "early")
    d_y_na = _mm("d_branch_na", d_u_na, W["w_branch_na"], "nt", 1024,512, 1024, [BF], after=(swap_flight.token,))
    d_y_dil = _mm("d_branch_dil", d_u_dil, W["w_branch_dil"], "nt", 1024,256, 1024, [F32])

    dqa, dka, dva, dtab = _na_bwd(qkv, tab, d_y_na)
    early_g, early_got = _swap_wait(swap_flight, dqa, "early")
    early_pair, early_pair_b = _pair_sum(early_g, early_got, early_tm)
    scatter_flight = _scatter_start(early_pair_b, "early")

    do_res, dlse_res = _dil_merge_bwd(d_y_dil, o_nat, w_grp, TM, after=(scatter_flight.token,))
    d_dil = [_band_bwd(*dil_ops[g], do_res[g], dlse_res[g], g) for g in range(len(DIL_GROUPS))]

    dz_qkv = _qkv_unprep((dqa, dka, dva), d_dil, cos2, sin_signed, TM)
    in_cols = shards["w_in"].shape[1]
    qkv_rows, gate_tm = dz_qkv.shape[1], 256
    assert qkv_rows % gate_tm == 0
    g_in = _mm("g_in_qkv", dz_qkv, a, "tn", 640, 1024, T, [F32],
               into=(jax.ShapeDtypeStruct((N_CHIPS * in_cols, D_MODEL), F32), (640, D_MODEL), lambda i, j: (i, 0)))
    g_in = _mm("g_in_gates", dz_gates, a, "tn", gate_tm, 1024, T, [F32],
               into=(g_in, (gate_tm, D_MODEL), lambda i, j: (qkv_rows // gate_tm + i, 0)))
    early_mine = _chip_sum(early_pair, _scatter_wait(scatter_flight, (g_in,), "early"), early_tm)
    join_flight = _join_start(early_mine)

    late_tm = in_cols // 4
    late_swap = _swap_start(g_in.reshape(N_CHIPS, in_cols, D_MODEL), "late")
    d_a = _mm("d_in_qkv", dz_qkv, W["w_in"], "nt", 512, 1024, qkv_rows, [F32], after=(late_swap.token, join_flight.token),
              b_view=(D_MODEL, (D_MODEL, qkv_rows), lambda j, k: (j, k)))
    late_g, late_got = _swap_wait(late_swap, d_a, "late")
    late_pair, late_pair_b = _pair_sum(late_g, late_got, late_tm)
    late_scatter = _scatter_start(late_pair_b, "late")
    d_rpb = _na_rpb_grad(dtab, after=(late_scatter.token,))[:, :2 * NA_WIN_ROWS - 1, :2 * NA_WIN_COLS - 1]
    def first_bwd(dn_gates, dn_qkv, dh_out, h, g):
        dh, dg = _rms_bwd(dn_gates + dn_qkv, h, g)
        return dh_out + dh, dg

    grad_x, dg_mix = _mm("d_in_gates", dz_gates, w_gates, "nt", 512, 1024, 2048, [F32], epilogue=first_bwd,
                         extras=(d_a, dh1, xs), consts=(gm,), sums=[D_MODEL], after=(late_scatter.token,))
    early_shard = _join_wait(join_flight, grad_x)

    n_rpb = rpb.size
    rpb_rows = 4
    small = jnp.concatenate([
        dg_mix, dg_mlp, dg_ple, dg_final,
        jnp.pad(d_rpb.reshape(-1), (0, rpb_rows * D_MODEL - n_rpb)).reshape(rpb_rows, D_MODEL),
        jnp.pad(loss_part, ((0, 0), (0, D_MODEL - loss_part.shape[1]))),
        jnp.zeros((SMALL_ROWS - 5 - rpb_rows, D_MODEL), F32)], axis=0)
    out = {"grad": {}, "delta": {}, "new_m": {}, "new_v": {}}

    def update(n, g, row0):
        res = _adamw("adamw_" + n, g, row0, params[n], params["m_" + n], params["v_" + n])
        for kind, t in zip(("grad", "delta", "new_m", "new_v"), res, strict=True):
            out[kind][n] = t

    row0 = 0
    for n in REDUCE_EARLY:
        rows, cols = early_shapes[n]
        n_rows = rows * cols // PACK_W
        if cols == PACK_W:
            update(n, early_shard, row0)
        else:
            update(n, early_shard[row0:row0 + n_rows].reshape(rows, cols), 0)
        row0 += n_rows
    late_others = _scatter_wait(late_scatter, (*[out["new_v"][n] for n in REDUCE_EARLY], d_rpb), "late")
    late_mine = _chip_sum(late_pair, late_others, late_tm)
    small_flight = _allreduce_start(small, after=(late_mine,))
    res = _adamw("adamw_w_in", _join_halves(late_mine, after=(small_flight.token,)), 0,
                 *[jnp.swapaxes(params[n], 1, 2) for n in ("w_in", "m_w_in", "v_w_in")])
    small = _allreduce_finish(small_flight, res[3])
    for kind, t in zip(("grad", "delta", "new_m", "new_v"), res, strict=True):
        out[kind]["w_in"] = jnp.swapaxes(t, 1, 2)
    loss = small[4 + rpb_rows, 0]

    def small_pack(a0, a1, a2, a3, r):
        return jnp.concatenate([a0.reshape(1, -1), a1.reshape(1, -1), a2.reshape(1, -1), a3.reshape(1, -1),
                                jnp.pad(r.reshape(-1), (0, rpb_rows * D_MODEL - n_rpb)).reshape(rpb_rows, D_MODEL)], axis=0)

    small_res = _adamw("adamw_small", small, 0, small_pack(g_mix, g_mlp, g_ple, g_final, rpb)[None],
                       small_pack(m_g_mix, m_g_mlp, m_g_ple, m_g_final, m_rpb)[None],
                       small_pack(v_g_mix, v_g_mlp, v_g_ple, v_g_final, v_rpb)[None])

    def small_unpack(t):
        return {"g_mix": t[0].reshape(g_mix.shape), "g_mlp": t[1].reshape(g_mlp.shape), "g_ple": t[2].reshape(g_ple.shape),
                "g_final": t[3].reshape(g_final.shape), "rpb": t[4:].reshape(-1)[:n_rpb].reshape(rpb.shape)}

    for kind, t in zip(("grad", "delta", "new_m", "new_v"), small_res, strict=True):
        out[kind].update(small_unpack(t[0]))

    order = ["g_mix", "w_in", "rpb", "w_branch_na", "w_branch_dil", "w_out", "g_mlp", "w_up", "w_down", "g_ple",
             "w_ple_gate", "w_ple_proj", "g_final"]
    return (loss, grad_x[None], *[out["grad"][n] for n in order], *[out["delta"][n] for n in order],
            *[out["new_m"][n] for n in order], *[out["new_v"][n] for n in order])
```

```python
import functools
from typing import NamedTuple

import jax
import jax.numpy as jnp
from jax import lax
from jax.experimental import pallas as pl
from jax.experimental.pallas import tpu as pltpu

BF = jnp.bfloat16
F32 = jnp.float32
MESH = pl.DeviceIdType.MESH
ANY = pl.BlockSpec(memory_space=pl.ANY)

V7X_VMEM_BYTES = 64 * 1024 * 1024
VMEM_LIMIT = V7X_VMEM_BYTES - 16 * 1024 * 1024

D_MODEL = 1024
HEAD_DIM = 64
GRID_W = 64
NA_HEADS = 8
NA_WIN_ROWS = 8
NA_WIN_COLS = 16
NA_WIDTH = NA_HEADS * HEAD_DIM
DIL_GROUPS = ((128, 1), (512, 4), (2048, 16))
DIL_HPG = 4
DIL_HEADS = DIL_HPG * len(DIL_GROUPS)
DIL_WIDTH = DIL_HEADS * HEAD_DIM
DIL_OUT_WIDTH = DIL_HPG * HEAD_DIM
DIL_RADIUS = 64
QKV_WIDTH = 3 * NA_WIDTH + 3 * DIL_WIDTH
D_FF = 4 * D_MODEL
PLE_DIM = 256
ROPE_THETA = 10000.0
RMS_EPS = 1e-6
NEG_INF = -1e30
Q_SCALE = HEAD_DIM ** -0.5

ADAM_LR = 0.001
ADAM_B1 = 0.9
ADAM_B2 = 0.999
ADAM_EPS = 1e-08
ADAM_WD = 0.01
ADAM_STEP = 10

N_CHIPS = 4
N_DEV = 8
PACK_W = 1024
GATHER_MIX = ("w_branch_na", "w_branch_dil", "w_out")
GATHER_MLP = ("w_up", "w_down", "w_ple_gate", "w_ple_proj")
REDUCE_EARLY = ("w_up", "w_down", "w_ple_gate", "w_out", "w_ple_proj", "w_branch_na", "w_branch_dil")
SMALL_ROWS = 16


def _cparams(sem=None):
    return pltpu.CompilerParams(dimension_semantics=sem, vmem_limit_bytes=VMEM_LIMIT)


def _mm(name, a, b, mode, tm, tn, tk, out_dtypes, epilogue=None, extras=(), consts=(), sums=(), after=(), into=None,
        b_view=None):
    if mode == "nn":
        (M, K), N = a.shape, b.shape[1]
    elif mode == "nt":
        (M, K), N = a.shape, b.shape[0]
    else:
        (K, M), N = a.shape, b.shape[1]
    if b_view is not None:
        N = b_view[0]
    tm, tn, tk = min(tm, M), min(tn, N), min(tk, K)
    assert M % tm == 0 and N % tn == 0 and K % tk == 0, (name, M, N, K, tm, tn, tk)
    if mode == "nn":
        a_spec = pl.BlockSpec((tm, tk), lambda i, j, k: (i, k))
        b_spec = pl.BlockSpec((tk, tn), lambda i, j, k: (k, j))
        dims = (((1,), (0,)), ((), ()))
    elif mode == "nt":
        a_spec = pl.BlockSpec((tm, tk), lambda i, j, k: (i, k))
        b_spec = pl.BlockSpec((tn, tk), lambda i, j, k: (j, k))
        dims = (((1,), (1,)), ((), ()))
    else:
        a_spec = pl.BlockSpec((tk, tm), lambda i, j, k: (k, i))
        b_spec = pl.BlockSpec((tk, tn), lambda i, j, k: (k, j))
        dims = (((0,), (0,)), ((), ()))
    if b_view is not None:
        b_spec = pl.BlockSpec(b_view[1], lambda i, j, k: b_view[2](j, k))
    nk = K // tk
    n_extra, n_const, n_out, n_sum = len(extras), len(consts), len(out_dtypes), len(sums)
    tile = pl.BlockSpec((tm, tn), lambda i, j, k: (i, j))
    assert not sums or tn == N, "row sums need whole rows in a tile"
    wide = [e for e in (*extras, *out_dtypes) if isinstance(e, tuple)]
    assert not wide or tn == N
    extra_specs = [pl.BlockSpec((tm, tn), functools.partial(lambda c, i, j, k: (i, c), e[1])) if isinstance(e, tuple) else tile
                   for e in extras]
    extras = [e[0] if isinstance(e, tuple) else e for e in extras]
    out_widths = [d[1] if isinstance(d, tuple) else N for d in out_dtypes]
    out_dtypes = [d[0] if isinstance(d, tuple) else d for d in out_dtypes]

    n_after = len(after)

    def body(a_ref, b_ref, *rest):
        extra_refs, rest = rest[:n_extra + n_const], rest[n_extra + n_const + n_after:]
        out_refs, sum_refs, acc = rest[:n_out], rest[n_out:n_out + n_sum], rest[-1]
        i, k = pl.program_id(0), pl.program_id(2)
        def product():
            if len(b_ref.shape) == 3:
                w = tk // b_ref.shape[0]
                parts = [lax.dot_general(a_ref[:, s * w:(s + 1) * w].astype(BF), b_ref[s].astype(BF), dims, preferred_element_type=F32)
                         for s in range(b_ref.shape[0])]
                return functools.reduce(lambda x, y: x + y, parts)
            return lax.dot_general(a_ref[...].astype(BF), b_ref[...].astype(BF), dims, preferred_element_type=F32)

        if nk > 1:
            @pl.when(k == 0)
            def _():
                acc[...] = jnp.zeros_like(acc)

            acc[...] += product()

        @pl.when(k == nk - 1)
        def _():
            total = product() if nk == 1 else acc[...]
            outs = (total,) if epilogue is None else epilogue(total, *[e[...] for e in extra_refs])
            for o_ref, val in zip(out_refs, outs[:n_out], strict=True):
                o_ref[...] = val.astype(o_ref.dtype).reshape(o_ref.shape)
            for s_ref, val in zip(sum_refs, outs[n_out:], strict=True):
                @pl.when(i == 0)
                def _():
                    s_ref[...] = val

                @pl.when(i != 0)
                def _():
                    s_ref[...] += val

    out_specs = ([tile if w == N else pl.BlockSpec((tm, w), lambda i, j, k: (i, 0)) for w in out_widths]
                 + [pl.BlockSpec((1, c), lambda i, j, k: (0, 0)) for c in sums])
    out_shape = ([jax.ShapeDtypeStruct((M, w), dt) for dt, w in zip(out_dtypes, out_widths, strict=True)]
                 + [jax.ShapeDtypeStruct((1, c), F32) for c in sums])
    operands, aliases = [a, b, *extras, *consts, *after], {}
    in_specs = ([a_spec, b_spec] + extra_specs
                + [pl.BlockSpec(c.shape, functools.partial(lambda nd, i, j, k: (0,) * nd, c.ndim)) for c in consts] + [ANY] * n_after)
    if into is not None:
        assert n_out == 1
        target, block, index = into
        out_specs = [pl.BlockSpec(block, lambda i, j, k: index(i, j))]
        out_shape = [jax.ShapeDtypeStruct(target.shape, target.dtype)]
        if not isinstance(target, jax.ShapeDtypeStruct):
            aliases = {len(operands): 0}
            operands.append(target)
            in_specs.append(ANY)
            n_after += 1

    outs = pl.pallas_call(
        body, name=name, grid=(M // tm, N // tn, nk),
        in_specs=in_specs, out_specs=out_specs, out_shape=out_shape,
        scratch_shapes=[pltpu.VMEM((tm, tn) if nk > 1 else (8, 128), F32)], input_output_aliases=aliases,
        compiler_params=_cparams(("arbitrary",) * 3 if sums else ("parallel", "parallel", "arbitrary")),
    )(*operands)
    return outs[0] if len(outs) == 1 else outs


def _row(arr, tm, col_block=None, width=None):
    width = arr.shape[1] if width is None else width
    cb = 0 if col_block is None else col_block
    return arr, pl.BlockSpec((tm, width), lambda i: (i, cb))


def _full(arr):
    nd = arr.ndim
    return arr, pl.BlockSpec(arr.shape, lambda i: (0,) * nd)


def _rowwise(name, body, T, tm, ins, outs, sums=(), after=()):
    n_in, n_out, n_sum, n_after = len(ins), len(outs), len(sums), len(after)

    def kern(*refs):
        in_refs, refs = refs[:n_in], refs[n_in + n_after:]
        out_refs, sum_refs = refs[:n_out], refs[n_out:]
        res = body(*[r[...] for r in in_refs])
        res = res if isinstance(res, tuple) else (res,)
        for o_ref, val in zip(out_refs, res[:n_out], strict=True):
            o_ref[...] = val.astype(o_ref.dtype)
        if n_sum:
            @pl.when(pl.program_id(0) == 0)
            def _():
                for s_ref in sum_refs:
                    s_ref[...] = jnp.zeros_like(s_ref)

            for s_ref, val in zip(sum_refs, res[n_out:], strict=True):
                s_ref[...] += val

    res = pl.pallas_call(
        kern, name=name, grid=(T // tm,),
        in_specs=[spec for _, spec in ins] + [ANY] * n_after,
        out_specs=[pl.BlockSpec((tm, c), lambda i: (i, 0)) for c, _ in outs]
        + [pl.BlockSpec((1, c), lambda i: (0, 0)) for c in sums],
        out_shape=[jax.ShapeDtypeStruct((T, c), dt) for c, dt in outs]
        + [jax.ShapeDtypeStruct((1, c), F32) for c in sums],
        compiler_params=_cparams(("arbitrary",)),
    )(*[a for a, _ in ins], *after)
    return res[0] if len(res) == 1 else res


def _sigmoid(x):
    return 1.0 / (1.0 + jnp.exp(-x))


def _rms(h):
    return lax.rsqrt(jnp.mean(h * h, axis=-1, keepdims=True) + RMS_EPS)


def _rms_bwd(dy, h, g):
    r = _rms(h)
    n = h * r
    dn = dy * g
    dh = r * (dn - n * jnp.mean(dn * n, axis=-1, keepdims=True))
    return dh, jnp.sum(dy * n, axis=0, keepdims=True)


def _rope(x, cos2, sin_signed):
    lane = lax.broadcasted_iota(jnp.int32, x.shape, 1)
    swapped = jnp.where((lane % HEAD_DIM) < HEAD_DIM // 2, pltpu.roll(x, 128 - HEAD_DIM // 2, 1), pltpu.roll(x, HEAD_DIM // 2, 1))
    return x * cos2 + swapped * sin_signed


NA_KEYS = NA_WIN_ROWS * GRID_W
NA_BASES = 8


def _na_row_geometry(r, rows):
    first = jnp.clip(r - NA_WIN_ROWS // 2, 0, rows - NA_WIN_ROWS)
    base = first - r + (NA_WIN_ROWS - 1)
    return pl.multiple_of(first * GRID_W, GRID_W), base


NA_ROWS_PER_STEP = 16
NA_BWD_ROWS_PER_STEP = 8


def _softmax_rows(s):
    p = jnp.exp(s - jnp.max(s, axis=-1, keepdims=True))
    return p / jnp.sum(p, axis=-1, keepdims=True)


def _split_pair(t):
    first = lax.broadcasted_iota(jnp.int32, t.shape, 1) < HEAD_DIM
    zero = jnp.zeros_like(t)
    return jnp.where(first, t, zero), jnp.where(first, zero, t)


def _join_pair(a, b):
    return jnp.where(lax.broadcasted_iota(jnp.int32, a.shape, 1) < HEAD_DIM, a, b)


_NT = (((1,), (1,)), ((), ()))
_TN = (((0,), (0,)), ((), ()))


def _na_fwd(qkv, tab):
    T = qkv.shape[0]
    rows = T // GRID_W
    n_pairs = NA_WIDTH // 128

    def body(q_ref, k_ref, v_ref, tab_ref, y_ref):
        def step(it, carry):
            geo = [_na_row_geometry(it * NA_ROWS_PER_STEP + u, rows) for u in range(NA_ROWS_PER_STEP)]
            q0s = [pl.multiple_of((it * NA_ROWS_PER_STEP + u) * GRID_W, GRID_W) for u in range(NA_ROWS_PER_STEP)]
            ss = [lax.dot_general(jnp.concatenate(_split_pair(q_ref[pl.ds(q0, GRID_W), :] * Q_SCALE), axis=0),
                                  k_ref[pl.ds(k0, NA_KEYS), :], _NT, preferred_element_type=F32)
                  for q0, (k0, _) in zip(q0s, geo)]
            ps = [_softmax_rows(s + jnp.concatenate([tab_ref[0, base], tab_ref[1, base]], axis=0)) for s, (_, base) in zip(ss, geo)]
            ys = [jnp.dot(p.astype(BF), v_ref[pl.ds(k0, NA_KEYS), :], preferred_element_type=F32) for p, (k0, _) in zip(ps, geo)]
            for q0, y2 in zip(q0s, ys):
                y_ref[pl.ds(q0, GRID_W), :] = _join_pair(y2[:GRID_W], y2[GRID_W:]).astype(y_ref.dtype)
            return carry

        lax.fori_loop(0, rows // NA_ROWS_PER_STEP, step, 0)

    def cols(first):
        return pl.BlockSpec((T, 128), lambda j: (0, first + j))

    return pl.pallas_call(
        body, name="na_fwd", grid=(n_pairs,),
        in_specs=[cols(0), cols(n_pairs), cols(2 * n_pairs), pl.BlockSpec((2, NA_BASES, GRID_W, NA_KEYS), lambda j: (j, 0, 0, 0))],
        out_specs=cols(0), out_shape=jax.ShapeDtypeStruct((T, NA_WIDTH), BF),
        compiler_params=_cparams(("parallel",)),
    )(qkv, qkv, qkv, tab)


def _na_bwd(qkv, tab, do):
    T = qkv.shape[0]
    rows = T // GRID_W
    n_pairs = NA_WIDTH // 128

    def body(q_ref, k_ref, v_ref, tab_ref, do_ref, dq_ref, dk_out, dv_out, dtab_ref, dk_ref, dv_ref):
        dk_ref[...] = jnp.zeros_like(dk_ref)
        dv_ref[...] = jnp.zeros_like(dv_ref)
        dtab_ref[...] = jnp.zeros_like(dtab_ref)

        def step(it, carry):
            U = NA_BWD_ROWS_PER_STEP
            geo = [_na_row_geometry(it * U + u, rows) for u in range(U)]
            q0s = [pl.multiple_of((it * U + u) * GRID_W, GRID_W) for u in range(U)]
            q2s = [jnp.concatenate(_split_pair(q_ref[pl.ds(q0, GRID_W), :] * Q_SCALE), axis=0) for q0 in q0s]
            do2s = [jnp.concatenate(_split_pair(do_ref[pl.ds(q0, GRID_W), :]), axis=0) for q0 in q0s]
            ss = [lax.dot_general(q2, k_ref[pl.ds(k0, NA_KEYS), :], _NT, preferred_element_type=F32) for q2, (k0, _) in zip(q2s, geo)]
            dps = [lax.dot_general(do2, v_ref[pl.ds(k0, NA_KEYS), :], _NT, preferred_element_type=F32) for do2, (k0, _) in zip(do2s, geo)]
            ps = [_softmax_rows(s + jnp.concatenate([tab_ref[0, base], tab_ref[1, base]], axis=0)) for s, (_, base) in zip(ss, geo)]
            dss = [p * (dp - jnp.sum(dp * p, axis=-1, keepdims=True)) for p, dp in zip(ps, dps)]
            dvs = [lax.dot_general(p.astype(BF), do2, _TN, preferred_element_type=F32) for p, do2 in zip(ps, do2s)]
            dsbs = [ds.astype(BF) for ds in dss]
            dqs = [jnp.dot(dsb, k_ref[pl.ds(k0, NA_KEYS), :], preferred_element_type=F32) for dsb, (k0, _) in zip(dsbs, geo)]
            dks = [lax.dot_general(dsb, q2, _TN, preferred_element_type=F32) for dsb, q2 in zip(dsbs, q2s)]
            for u in range(U):
                k0, base = geo[u]
                dtab_ref[0, base] += dss[u][:GRID_W]
                dtab_ref[1, base] += dss[u][GRID_W:]
                dq_ref[pl.ds(q0s[u], GRID_W), :] = (_join_pair(dqs[u][:GRID_W], dqs[u][GRID_W:]) * Q_SCALE).astype(dq_ref.dtype)
                dk_ref[pl.ds(k0, NA_KEYS), :] += dks[u]
                dv_ref[pl.ds(k0, NA_KEYS), :] += dvs[u]
            return carry

        lax.fori_loop(0, rows // NA_BWD_ROWS_PER_STEP, step, 0)
        dk_out[...] = dk_ref[...].astype(dk_out.dtype)
        dv_out[...] = dv_ref[...].astype(dv_out.dtype)

    def cols(first):
        return pl.BlockSpec((T, 128), lambda j: (0, first + j))

    tabs = pl.BlockSpec((2, NA_BASES, GRID_W, NA_KEYS), lambda j: (j, 0, 0, 0))
    wide = jax.ShapeDtypeStruct((T, NA_WIDTH), BF)
    return pl.pallas_call(
        body, name="na_bwd", grid=(n_pairs,),
        in_specs=[cols(0), cols(n_pairs), cols(2 * n_pairs), tabs, cols(0)],
        out_specs=[cols(0), cols(0), cols(0), tabs],
        out_shape=[wide, wide, wide, jax.ShapeDtypeStruct((NA_HEADS, NA_BASES, GRID_W, NA_KEYS), F32)],
        scratch_shapes=[pltpu.VMEM((T, 128), F32), pltpu.VMEM((T, 128), F32)],
        compiler_params=_cparams(("parallel",)),
    )(qkv, qkv, qkv, tab, do)


def _na_bias_table(rpb):
    H, n_rows, n_cols = rpb.shape

    def body(r_ref, tab_ref):
        q = lax.broadcasted_iota(jnp.int32, (GRID_W, 128), 0)
        kc = lax.broadcasted_iota(jnp.int32, (GRID_W, 128), 1)
        first = jnp.clip(q - NA_WIN_COLS // 2, 0, GRID_W - NA_WIN_COLS)
        valid = (kc >= first) & (kc < first + NA_WIN_COLS)
        toeplitz = []
        for ro in range(n_rows):
            row = jnp.broadcast_to(r_ref[pl.ds(ro, 1), :], (GRID_W, 128))
            shifted = pltpu.roll(pltpu.roll(row, 128 - (NA_WIN_COLS - 1), 1), 0, 1, stride=1, stride_axis=0)
            toeplitz.append(jnp.where(valid, shifted, NEG_INF))
        for base in range(NA_BASES):
            for j in range(NA_WIN_ROWS // 2):
                even, odd = toeplitz[base + 2 * j], toeplitz[base + 2 * j + 1]
                tab_ref[base, :, pl.ds(j * 128, 128)] = jnp.where(kc < GRID_W, even, pltpu.roll(odd, GRID_W, 1))

    padded = jnp.pad(rpb, ((0, 0), (0, 16 - n_rows), (0, 128 - n_cols)))
    return pl.pallas_call(
        body, name="na_bias_table", grid=(H,),
        in_specs=[pl.BlockSpec((None, 16, 128), lambda h: (h, 0, 0))],
        out_specs=pl.BlockSpec((None, NA_BASES, GRID_W, NA_KEYS), lambda h: (h, 0, 0, 0)),
        out_shape=jax.ShapeDtypeStruct((H, NA_BASES, GRID_W, NA_KEYS), F32),
        compiler_params=_cparams(("parallel",)),
    )(padded)


def _na_rpb_grad(dtab, after=()):
    H = dtab.shape[0]
    n_rows = 2 * NA_WIN_ROWS - 1
    n_cols = 2 * NA_WIN_COLS - 1

    def body(d_ref, *rest):
        o_ref = rest[-1]
        lane = lax.broadcasted_iota(jnp.int32, (GRID_W, 128), 1)
        low = lane < GRID_W
        flip = (lax.broadcasted_iota(jnp.int32, (GRID_W, GRID_W), 0) + lax.broadcasted_iota(jnp.int32, (GRID_W, GRID_W), 1)
                == GRID_W - 1).astype(BF)

        def reverse_rows(t):
            out = jnp.zeros_like(t)
            for _ in range(3):
                piece = t.astype(BF)
                out = out + jnp.dot(flip, piece, preferred_element_type=F32)
                t = t - piece.astype(F32)
            return out

        out_rows = []
        for ro in range(n_rows):
            acc = jnp.zeros((GRID_W, 128), F32)
            for base in range(NA_BASES):
                i = ro - base
                if not 0 <= i < NA_WIN_ROWS:
                    continue
                pair = d_ref[base, :, pl.ds((i // 2) * 128, 128)]
                if i % 2:
                    pair = pltpu.roll(pair, GRID_W, 1)
                acc = acc + jnp.where(low, pair, 0.0)
            skew = pltpu.roll(reverse_rows(acc), 0, 1, stride=1, stride_axis=0)
            diag = jnp.sum(skew, axis=0, keepdims=True)
            out_rows.append(pltpu.roll(jnp.broadcast_to(diag, (8, 128)), 128 - (GRID_W - NA_WIN_COLS), 1)[:1])
        out_rows.append(jnp.zeros((1, 128), F32))
        res = jnp.concatenate(out_rows, axis=0)
        o_ref[...] = jnp.where(lax.broadcasted_iota(jnp.int32, res.shape, 1) < n_cols, res, 0.0)

    return pl.pallas_call(
        body, name="na_rpb_grad", grid=(H,),
        in_specs=[pl.BlockSpec((None, NA_BASES, GRID_W, NA_KEYS), lambda h: (h, 0, 0, 0))] + [ANY] * len(after),
        out_specs=pl.BlockSpec((None, n_rows + 1, 128), lambda h: (h, 0, 0)),
        out_shape=jax.ShapeDtypeStruct((H, n_rows + 1, 128), F32),
        compiler_params=_cparams(("parallel",)),
    )(dtab, *after)


BAND_Q = 128
BAND_KEYS = BAND_Q + 2 * DIL_RADIUS


def _band_geometry(n, L):
    q0 = pl.multiple_of(n * BAND_Q, BAND_Q)
    k0 = pl.multiple_of(jnp.clip(q0 - DIL_RADIUS, 0, L - BAND_KEYS), DIL_RADIUS)
    qi = q0 + lax.broadcasted_iota(jnp.int32, (BAND_Q, BAND_KEYS), 0)
    kj = k0 + lax.broadcasted_iota(jnp.int32, (BAND_Q, BAND_KEYS), 1)
    return q0, k0, jnp.abs(qi - kj) <= DIL_RADIUS


DIL_PAIRS = DIL_OUT_WIDTH // 128


def _residue_shape(dil, T, dtype):
    return jax.ShapeDtypeStruct((DIL_PAIRS, dil, T // dil, 128), dtype)


def _residue_tile(dil, tm):
    return pl.BlockSpec((DIL_PAIRS, dil, tm // dil, 128), lambda i: (0, 0, i, 0))


def _to_natural(ref, scratch, dil, tm):
    tiles = []
    for pair in range(DIL_PAIRS):
        if dil == 1:
            tiles.append(ref[pair, 0].astype(F32))
            continue
        for r in range(dil):
            scratch[pl.ds(r, tm // dil, stride=dil), :] = ref[pair, r].astype(F32)
        tiles.append(scratch[...])
    return tiles


def _from_natural(tile, scratch, ref, pair, dil, tm):
    if dil == 1:
        ref[pair, 0] = tile.astype(ref.dtype)
        return
    scratch[...] = tile
    for r in range(dil):
        ref[pair, r] = scratch[pl.ds(r, tm // dil, stride=dil), :].astype(ref.dtype)


def _band_specs(group, T):
    dil = DIL_GROUPS[group][1]
    L = T // dil
    assert L % BAND_Q == 0 and L >= BAND_KEYS, (T, dil)
    per_residue = min(BAND_BLOCKS_PER_STEP, L // BAND_Q)
    residues = min(dil, BAND_BLOCKS_PER_STEP // per_residue)
    spec = pl.BlockSpec((None, residues, L, 128), lambda s: (s % DIL_PAIRS, s // DIL_PAIRS, 0, 0))
    return L, residues, per_residue, (dil // residues * DIL_PAIRS,), spec


BAND_BLOCKS_PER_STEP = 8


def _band_softmax(s, valid):
    s = jnp.where(valid, s, NEG_INF)
    m = jnp.max(s, axis=-1, keepdims=True)
    p = jnp.exp(s - m)
    l = jnp.sum(p, axis=-1, keepdims=True)
    return p / l, m + jnp.log(l)


def _band_fwd(q, k, v, group):
    T = q.shape[1] * q.shape[2]
    L, residues, U, grid, spec = _band_specs(group, T)

    def body(q_ref, k_ref, v_ref, o_ref, lse_ref):
        def step(it, carry):
            geo = [(r, *_band_geometry(it * U + u, L)) for r in range(residues) for u in range(U)]
            ss = [lax.dot_general(jnp.concatenate(_split_pair(q_ref[r, pl.ds(q0, BAND_Q), :]), axis=0),
                                  k_ref[r, pl.ds(k0, BAND_KEYS), :], _NT, preferred_element_type=F32) for r, q0, k0, _ in geo]
            pls = [_band_softmax(s, jnp.concatenate([valid, valid], axis=0)) for s, (_, _, _, valid) in zip(ss, geo)]
            os = [jnp.dot(p.astype(BF), v_ref[r, pl.ds(k0, BAND_KEYS), :], preferred_element_type=F32)
                  for (p, _), (r, _, k0, _) in zip(pls, geo)]
            for (r, q0, _, _), o2, (_, lse) in zip(geo, os, pls):
                o_ref[r, pl.ds(q0, BAND_Q), :] = _join_pair(o2[:BAND_Q], o2[BAND_Q:])
                lse2 = jnp.broadcast_to(lse, (2 * BAND_Q, 128))
                lse_ref[r, pl.ds(q0, BAND_Q), :] = _join_pair(lse2[:BAND_Q], lse2[BAND_Q:])
            return carry

        lax.fori_loop(0, L // (BAND_Q * U), step, 0)

    res = _residue_shape(DIL_GROUPS[group][1], T, F32)
    return pl.pallas_call(
        body, name=f"band_fwd_g{group}", grid=grid,
        in_specs=[spec] * 3, out_specs=[spec] * 2, out_shape=[res, res],
        compiler_params=_cparams(("parallel",)),
    )(q, k, v)


def _band_bwd(q, k, v, do, dlse, group):
    T = q.shape[1] * q.shape[2]
    L, residues, U, grid, spec = _band_specs(group, T)

    def body(q_ref, k_ref, v_ref, do_ref, dlse_ref, dq_ref, dk_ref, dv_ref):
        dk_ref[...] = jnp.zeros_like(dk_ref)
        dv_ref[...] = jnp.zeros_like(dv_ref)

        def step(it, carry):
            geo = [(r, *_band_geometry(it * U + u, L)) for r in range(residues) for u in range(U)]
            q2s = [jnp.concatenate(_split_pair(q_ref[r, pl.ds(q0, BAND_Q), :]), axis=0) for r, q0, _, _ in geo]
            do2s = [jnp.concatenate(_split_pair(do_ref[r, pl.ds(q0, BAND_Q), :]), axis=0) for r, q0, _, _ in geo]
            ss = [lax.dot_general(q2, k_ref[r, pl.ds(k0, BAND_KEYS), :], _NT, preferred_element_type=F32)
                  for q2, (r, _, k0, _) in zip(q2s, geo)]
            dps = [lax.dot_general(do2, v_ref[r, pl.ds(k0, BAND_KEYS), :], _NT, preferred_element_type=F32)
                   for do2, (r, _, k0, _) in zip(do2s, geo)]
            ps = [_band_softmax(s, jnp.concatenate([valid, valid], axis=0))[0] for s, (_, _, _, valid) in zip(ss, geo)]
            dss = []
            for p, dp, (r, q0, _, _) in zip(ps, dps, geo):
                dl = dlse_ref[r, pl.ds(q0, BAND_Q), :]
                dl2 = jnp.concatenate([dl[:, :1], dl[:, HEAD_DIM:HEAD_DIM + 1]], axis=0)
                dss.append(p * (dp - jnp.sum(dp * p, axis=-1, keepdims=True) + dl2))
            dvs = [lax.dot_general(p.astype(BF), do2, _TN, preferred_element_type=F32) for p, do2 in zip(ps, do2s)]
            dsbs = [ds.astype(BF) for ds in dss]
            dqs = [jnp.dot(dsb, k_ref[r, pl.ds(k0, BAND_KEYS), :], preferred_element_type=F32) for dsb, (r, _, k0, _) in zip(dsbs, geo)]
            dks = [lax.dot_general(dsb, q2, _TN, preferred_element_type=F32) for dsb, q2 in zip(dsbs, q2s)]
            for u, (r, q0, k0, _) in enumerate(geo):
                dq_ref[r, pl.ds(q0, BAND_Q), :] = _join_pair(dqs[u][:BAND_Q], dqs[u][BAND_Q:])
                dk_ref[r, pl.ds(k0, BAND_KEYS), :] += dks[u]
                dv_ref[r, pl.ds(k0, BAND_KEYS), :] += dvs[u]
            return carry

        lax.fori_loop(0, L // (BAND_Q * U), step, 0)

    res = _residue_shape(DIL_GROUPS[group][1], T, F32)
    return pl.pallas_call(
        body, name=f"band_bwd_g{group}", grid=grid,
        in_specs=[spec] * 5, out_specs=[spec] * 3, out_shape=[res] * 3,
        compiler_params=_cparams(("parallel",)),
    )(q, k, v, do, dlse)


def _head_sums(t):
    head = lax.broadcasted_iota(jnp.int32, t.shape, 1) // HEAD_DIM
    out = jnp.zeros_like(t)
    for h in range(t.shape[1] // HEAD_DIM):
        mine = head == h
        out = jnp.where(mine, jnp.sum(jnp.where(mine, t, 0.0), axis=-1, keepdims=True), out)
    return out


def _dil_merge_fwd(os, lses, T, tm):
    G = len(DIL_GROUPS)
    W = DIL_OUT_WIDTH
    dils = [d for _, d in DIL_GROUPS]

    def body(*refs):
        o_refs, lse_refs = refs[:G], refs[G:2 * G]
        y_ref, w_refs, on_refs, scratch = refs[2 * G], refs[2 * G + 1:3 * G + 1], refs[3 * G + 1:4 * G + 1], refs[-1]
        o = [jnp.concatenate(_to_natural(r, scratch, d, tm), axis=1) for r, d in zip(o_refs, dils)]
        ls = [jnp.concatenate(_to_natural(r, scratch, d, tm), axis=1) for r, d in zip(lse_refs, dils)]
        m = functools.reduce(jnp.maximum, ls)
        es = [jnp.exp(l - m) for l in ls]
        tot = functools.reduce(jnp.add, es)
        ws = [e / tot for e in es]
        y_ref[...] = functools.reduce(jnp.add, [w * t for w, t in zip(ws, o)]).astype(y_ref.dtype)
        for g in range(G):
            w_refs[g][...] = ws[g]
            on_refs[g][...] = o[g]

    nat = pl.BlockSpec((tm, W), lambda i: (i, 0))
    res = pl.pallas_call(
        body, name="dil_merge_fwd", grid=(T // tm,),
        in_specs=[_residue_tile(d, tm) for d in dils] * 2,
        out_specs=[nat] * (2 * G + 1),
        out_shape=[jax.ShapeDtypeStruct((T, W), BF)] + [jax.ShapeDtypeStruct((T, W), F32)] * (2 * G),
        scratch_shapes=[pltpu.VMEM((tm, 128), F32)],
        compiler_params=_cparams(("parallel",)),
    )(*os, *lses)
    return res[0], res[1:G + 1], res[G + 1:]


def _dil_merge_bwd(dy, os, ws, tm, after=()):
    G = len(DIL_GROUPS)
    T, W = dy.shape
    dils = [d for _, d in DIL_GROUPS]
    n_after = len(after)

    def body(*refs):
        dyt = refs[0][...]
        o, w = [r[...] for r in refs[1:G + 1]], [r[...] for r in refs[G + 1:2 * G + 1]]
        refs = refs[2 * G + 1 + n_after:]
        do_refs, dlse_refs, scratch = refs[:G], refs[G:2 * G], refs[-1]
        dws = [_head_sums(dyt * t) for t in o]
        mean = functools.reduce(jnp.add, [a * b for a, b in zip(w, dws)])
        for g, d in enumerate(dils):
            do, dlse = w[g] * dyt, w[g] * (dws[g] - mean)
            for pair in range(DIL_PAIRS):
                cols = slice(pair * 128, (pair + 1) * 128)
                _from_natural(do[:, cols], scratch, do_refs[g], pair, d, tm)
                _from_natural(dlse[:, cols], scratch, dlse_refs[g], pair, d, tm)

    nat = pl.BlockSpec((tm, W), lambda i: (i, 0))
    res = pl.pallas_call(
        body, name="dil_merge_bwd", grid=(T // tm,),
        in_specs=[nat] * (2 * G + 1) + [ANY] * n_after,
        out_specs=[_residue_tile(d, tm) for d in dils] * 2,
        out_shape=[_residue_shape(d, T, BF) for d in dils] + [_residue_shape(d, T, F32) for d in dils],
        scratch_shapes=[pltpu.VMEM((tm, 128), F32)],
        compiler_params=_cparams(("parallel",)),
    )(dy, *os, *ws, *after)
    return res[:G], res[G:]


def _qkv_prep(z, cos2, sin_signed, tm):
    T = z.shape[0]
    G = len(DIL_GROUPS)
    dils = [d for _, d in DIL_GROUPS]
    n_dil_blocks = 3 * DIL_WIDTH // 128

    def body(*refs):
        blocks = refs[:n_dil_blocks]
        cos_ref, sin_ref = refs[n_dil_blocks], refs[1 + n_dil_blocks]
        outs = refs[2 + n_dil_blocks:]
        for part in range(3):
            for g, d in enumerate(dils):
                out = outs[g * 3 + part]
                for pair in range(DIL_PAIRS):
                    blk = blocks[part * (DIL_WIDTH // 128) + g * DIL_PAIRS + pair]
                    for r in range(d):
                        rows = pl.ds(r, tm // d, stride=d) if d > 1 else slice(None)
                        x = blk[rows, :]
                        if part < 2:
                            x = _rope(x, cos_ref[rows, :], sin_ref[rows, :])
                        if part == 0:
                            x = x * Q_SCALE
                        out[pair, r] = x.astype(out.dtype)

    lane_block = [pl.BlockSpec((tm, 128), functools.partial(lambda b, i: (i, b), b)) for b in range(n_dil_blocks)]
    tab = pl.BlockSpec((tm, 128), lambda i: (i, 0))
    res = pl.pallas_call(
        body, name="qkv_prep", grid=(T // tm,),
        in_specs=lane_block + [tab, tab],
        out_specs=[_residue_tile(d, tm) for d in dils for _ in range(3)],
        out_shape=[_residue_shape(d, T, BF) for d in dils for _ in range(3)],
        compiler_params=_cparams(("parallel",)),
    )(*[z] * n_dil_blocks, cos2, sin_signed)
    return [res[3 * g:3 + 3 * g] for g in range(G)]


def _qkv_unprep(d_na, d_dil, cos2, sin_signed, tm, after=()):
    T = d_na[0].shape[0]
    G = len(DIL_GROUPS)
    dils = [d for _, d in DIL_GROUPS]
    n_after = len(after)

    def body(*refs):
        dq, dk, dv = (r[...] for r in refs[:3])
        res_refs = refs[3:3 + 3 * G]
        cs, sn = refs[3 + 3 * G][...], refs[4 + 3 * G][...]
        out, scratch = refs[5 + 3 * G + n_after], refs[-1]
        cols = [dq, dk, dv]
        for part in range(3):
            for g, d in enumerate(dils):
                for x in _to_natural(res_refs[g * 3 + part], scratch, d, tm):
                    if part < 2:
                        x = _rope(x, cs, -sn)
                    cols.append((x * Q_SCALE if part == 0 else x).astype(out.dtype))
        out[...] = jnp.concatenate(cols, axis=1)

    wide = pl.BlockSpec((tm, NA_WIDTH), lambda i: (i, 0))
    tab = pl.BlockSpec((tm, 128), lambda i: (i, 0))
    return pl.pallas_call(
        body, name="qkv_unprep", grid=(T // tm,),
        in_specs=[wide] * 3 + [_residue_tile(d, tm) for d in dils for _ in range(3)] + [tab, tab] + [ANY] * n_after,
        out_specs=pl.BlockSpec((tm, QKV_WIDTH), lambda i: (i, 0)),
        out_shape=jax.ShapeDtypeStruct((T, QKV_WIDTH), BF),
        scratch_shapes=[pltpu.VMEM((tm, 128), F32)],
        compiler_params=_cparams(("parallel",)),
    )(*d_na, *[t for g in range(G) for t in d_dil[g]], cos2, sin_signed, *after)


def _rope_tables(positions):
    half = HEAD_DIM // 2
    inv_freq = ROPE_THETA ** (-jnp.arange(half, dtype=F32) / half)
    ang = positions.astype(F32)[:, None] * inv_freq
    cos, sin = jnp.cos(ang), jnp.sin(ang)
    return jnp.tile(jnp.concatenate([cos, cos], axis=1), (1, 2)), jnp.tile(jnp.concatenate([-sin, sin], axis=1), (1, 2))


def _pack_rows(t):
    return t.reshape(-1, PACK_W)


def _me():
    return lax.axis_index("x"), lax.axis_index("y"), lax.axis_index("c")


def _other_chips(x, y):
    return [(1 - x, y), (x, 1 - y), (1 - x, 1 - y)]


def _pair_sum(g, got, tm):
    S, R, W = g.shape
    half = R // 2
    nb = half // tm

    def body(pos_ref, g_ref, got_ref, own_ref, ob_ref):
        tot = g_ref[...] + got_ref[...]
        ob_ref[...] = tot.astype(ob_ref.dtype)

        @pl.when(pl.program_id(1) == pos_ref[1])
        def _():
            own_ref[...] = tot

    tile = pl.BlockSpec((None, tm, W), lambda i, s, pos_ref: (s, i, 0))
    c, chip = lax.axis_index("c"), 2 * lax.axis_index("x") + lax.axis_index("y")
    return pl.pallas_call(
        body, name="pair_sum",
        grid_spec=pltpu.PrefetchScalarGridSpec(
            num_scalar_prefetch=1, grid=(nb, S),
            in_specs=[pl.BlockSpec((None, tm, W), lambda i, s, pos_ref: (s, pos_ref[0] * nb + i, 0)), tile],
            out_specs=[pl.BlockSpec((tm, W), lambda i, s, pos_ref: (i, 0)), tile]),
        out_shape=[jax.ShapeDtypeStruct((half, W), F32), jax.ShapeDtypeStruct((S, half, W), BF)],
        compiler_params=_cparams(("parallel", "arbitrary")),
    )(jnp.stack([c, chip]).astype(jnp.int32), g, got)


def _chip_sum(own, others, tm):
    n, h, W = others.shape
    nb = h // tm

    def body(c_ref, own_ref, p_ref, o_ref):
        o_ref[...] = ((own_ref[...] + p_ref[0].astype(F32)) + p_ref[1].astype(F32)) + p_ref[2].astype(F32)

    return pl.pallas_call(
        body, name="chip_sum",
        grid_spec=pltpu.PrefetchScalarGridSpec(
            num_scalar_prefetch=1, grid=(nb,),
            in_specs=[pl.BlockSpec((tm, W), lambda i, c_ref: (i, 0)), pl.BlockSpec((n, tm, W), lambda i, c_ref: (0, i, 0))],
            out_specs=pl.BlockSpec((tm, W), lambda i, c_ref: (c_ref[0] * nb + i, 0))),
        out_shape=jax.ShapeDtypeStruct((2 * h, W), F32),
        compiler_params=_cparams(("parallel",)),
    )(lax.axis_index("c").reshape(1).astype(jnp.int32), own, others)


def _join_halves(shard, after=()):
    h = shard.shape[0] // 2

    def body(in_ref, *rest):
        out_ref, send_sem, recv_sem = rest[len(after):]
        x, y, c = _me()
        cp = pltpu.make_async_remote_copy(
            src_ref=in_ref.at[pl.ds(c * h, h), :], dst_ref=out_ref.at[pl.ds(c * h, h), :],
            send_sem=send_sem, recv_sem=recv_sem, device_id=(x, y, 1 - c), device_id_type=MESH)
        cp.start()
        pltpu.make_async_remote_copy(
            src_ref=in_ref.at[pl.ds(c * h, h), :], dst_ref=out_ref.at[pl.ds((1 - c) * h, h), :],
            send_sem=send_sem, recv_sem=recv_sem, device_id=(x, y, 1 - c), device_id_type=MESH).wait_recv()
        cp.wait_send()

    return pl.pallas_call(
        body, name="join_halves", in_specs=[ANY] * (1 + len(after)), out_specs=ANY,
        out_shape=jax.ShapeDtypeStruct(shard.shape, shard.dtype), input_output_aliases={0: 0},
        scratch_shapes=[pltpu.SemaphoreType.DMA, pltpu.SemaphoreType.DMA],
    )(shard, *after)


def _allreduce_copies(src_ref, land_ref):
    x, y, c = _me()
    peers = [((x + fx) % 2, (y + fy) % 2, (c + fc) % 2) for fx in range(2) for fy in range(2) for fc in range(2)][1:]
    return [(src_ref, land_ref.at[4 * x + 2 * y + c], peer) for peer in peers]


def _allreduce_start(s, after):
    return _split_start("allreduce_small_start", s, (N_DEV, *s.shape), s.dtype, N_DEV - 1, _allreduce_copies, after)


def _allreduce_finish(flight, after):
    own, landed = _split_wait("allreduce_small_wait", flight, after, N_DEV - 1, _allreduce_copies)
    me = 4 * lax.axis_index("x") + 2 * lax.axis_index("y") + lax.axis_index("c")
    parts = lax.dynamic_update_slice(landed, own[None], (me, 0, 0))

    def body(p_ref, o_ref):
        total = p_ref[0]
        for d in range(1, N_DEV):
            total = total + p_ref[d]
        o_ref[...] = total

    return pl.pallas_call(
        body, name="allreduce_small_sum",
        in_specs=[pl.BlockSpec(memory_space=pltpu.VMEM)], out_specs=pl.BlockSpec(memory_space=pltpu.VMEM),
        out_shape=jax.ShapeDtypeStruct(own.shape, F32),
    )(parts)


HBM_SPEC = pl.BlockSpec(memory_space=pltpu.HBM)
SEM_SPEC = pl.BlockSpec(memory_space=pltpu.SEMAPHORE)
DATAFLOW = pltpu.SideEffectType.DATAFLOW_SIDE_EFFECTING


class _InFlight(NamedTuple):
    sems: tuple
    src: jax.Array
    land: jax.Array
    token: jax.Array


def _split_start(name, src, land_shape, land_dtype, n, copies, after=()):
    n_after = len(after)

    def body(src_ref, land_ref, *rest):
        rest = rest[n_after:]
        sems, token = rest[:2 * n], rest[-1]
        for k, (s, d, peer) in enumerate(copies(src_ref, land_ref)):
            pltpu.make_async_remote_copy(src_ref=s, dst_ref=d, send_sem=sems[k], recv_sem=sems[n + k],
                                         device_id=peer, device_id_type=MESH).start()
        token[...] = jnp.zeros_like(token)

    outs = pl.pallas_call(
        body, name=name,
        out_shape=(*[pltpu.SemaphoreType.DMA(())] * (2 * n), pltpu.HBM(src.shape, src.dtype), pltpu.HBM(land_shape, land_dtype),
                   jax.ShapeDtypeStruct((8, 128), F32)),
        in_specs=(HBM_SPEC, HBM_SPEC, *[ANY] * n_after),
        out_specs=(*[SEM_SPEC] * (2 * n), HBM_SPEC, HBM_SPEC, pl.BlockSpec(memory_space=pltpu.VMEM)),
        input_output_aliases={0: 2 * n, 1: 2 * n + 1},
        compiler_params=pltpu.CompilerParams(has_side_effects=DATAFLOW),
    )(pltpu.with_memory_space_constraint(src, pltpu.HBM), pltpu.with_memory_space_constraint(lax.empty(land_shape, land_dtype), pltpu.HBM),
      *after)
    return _InFlight(tuple(outs[:2 * n]), outs[2 * n], outs[2 * n + 1], outs[2 * n + 2])


def _split_wait(name, flight, after, n, copies):
    after = after if isinstance(after, tuple) else (after,)

    def body(src_ref, land_ref, *rest):
        sems = rest[:2 * n]
        for k, (s, d, peer) in enumerate(copies(src_ref, land_ref)):
            cp = pltpu.make_async_remote_copy(src_ref=s, dst_ref=d, send_sem=sems[k], recv_sem=sems[n + k],
                                              device_id=peer, device_id_type=MESH)
            cp.wait_send()
            cp.wait_recv()

    return pl.pallas_call(
        body, name=name,
        out_shape=(pltpu.HBM(flight.src.shape, flight.src.dtype), pltpu.HBM(flight.land.shape, flight.land.dtype)),
        in_specs=(HBM_SPEC, HBM_SPEC, *[SEM_SPEC] * (2 * n), *[ANY] * len(after)),
        out_specs=(HBM_SPEC, HBM_SPEC), input_output_aliases={0: 0, 1: 1},
        compiler_params=pltpu.CompilerParams(has_side_effects=DATAFLOW),
    )(flight.src, flight.land, *flight.sems, *after)


def _gather_copies(src_ref, land_ref):
    x, y, c = _me()
    return [(src_ref, land_ref.at[2 * x + y], (*chip, c)) for chip in _other_chips(x, y)]


def _gather_start(packed, tag, after=()):
    return _split_start(f"gather_start_{tag}", packed, (N_CHIPS, *packed.shape), packed.dtype, 3, _gather_copies, after)


def _gather_wait(flight, after, tag):
    src, others = _split_wait(f"gather_wait_{tag}", flight, after, 3, _gather_copies)
    return lax.dynamic_update_slice(others, src[None], (2 * lax.axis_index("x") + lax.axis_index("y"), 0, 0))


def _across_copies(src_ref, land_ref):
    x, y, c = _me()
    half = src_ref.shape[0] // 2
    rows = pl.ds(c * half, half)
    return [(src_ref.at[rows, :], land_ref.at[2 * x + y, rows, :], (*chip, c)) for chip in _other_chips(x, y)]


def _to_sibling_copies(all_ref, unused_ref):
    x, y, c = _me()
    half = all_ref.shape[1] // 2
    places = [all_ref.at[2 * chip[0] + chip[1], pl.ds(c * half, half), :] for chip in _other_chips(x, y)]
    return [(place, place, (x, y, 1 - c)) for place in places]


def _gather_halves_start(shard, tag, after=()):
    return _split_start(f"gather_{tag}_across_start", shard, (N_CHIPS, *shard.shape), shard.dtype, 3, _across_copies, after)


def _gather_halves_relay(flight, after, tag):
    shard, landed = _split_wait(f"gather_{tag}_across_wait", flight, after, 3, _across_copies)
    return shard, _split_start(f"gather_{tag}_sibling_start", landed, (8, 128), landed.dtype, 3, _to_sibling_copies)


def _gather_halves_finish(shard, relay, after, tag):
    others = _split_wait(f"gather_{tag}_sibling_wait", relay, after, 3, _to_sibling_copies)[0]
    return lax.dynamic_update_slice(others, shard[None], (2 * lax.axis_index("x") + lax.axis_index("y"), 0, 0))


def _assemble_w_in(shards, tm):
    S, R, C = shards.shape
    n_gates = 2 * D_MODEL

    def body(s_ref, w_ref, g_ref):
        full = jnp.concatenate([s_ref[s] for s in range(S)], axis=1)
        w_ref[...] = full
        g_ref[...] = full[:, S * C - n_gates:]

    return pl.pallas_call(
        body, name="assemble_w_in", grid=(R // tm,),
        in_specs=[pl.BlockSpec((S, tm, C), lambda i: (0, i, 0))],
        out_specs=[pl.BlockSpec((tm, S * C), lambda i: (i, 0)), pl.BlockSpec((tm, n_gates), lambda i: (i, 0))],
        out_shape=[jax.ShapeDtypeStruct((R, S * C), shards.dtype), jax.ShapeDtypeStruct((R, n_gates), shards.dtype)],
        compiler_params=_cparams(("parallel",)),
    )(shards)


def _swap_copies(src_ref, land_ref):
    x, y, c = _me()
    half = land_ref.shape[1]
    return [(src_ref.at[:, pl.ds((1 - c) * half, half), :], land_ref, (x, y, 1 - c))]


def _swap_start(g, tag):
    S, R, W = g.shape
    return _split_start(f"swap_halves_start_{tag}", g, (S, R // 2, W), g.dtype, 1, _swap_copies)


def _swap_wait(flight, after, tag):
    return _split_wait(f"swap_halves_wait_{tag}", flight, after, 1, _swap_copies)


def _scatter_copies(src_ref, land_ref):
    x, y, c = _me()
    return [(src_ref.at[2 * chip[0] + chip[1]], land_ref.at[j], (*chip, c)) for j, chip in enumerate(_other_chips(x, y))]


def _scatter_start(part, tag):
    S, h, W = part.shape
    return _split_start(f"scatter_chips_start_{tag}", part, (S - 1, h, W), part.dtype, 3, _scatter_copies)


def _scatter_wait(flight, after, tag):
    return _split_wait(f"scatter_chips_wait_{tag}", flight, after, 3, _scatter_copies)[1]


def _join_copies(shard_ref, unused_ref):
    x, y, c = _me()
    h = shard_ref.shape[0] // 2
    rows = shard_ref.at[pl.ds(c * h, h), :]
    return [(rows, rows, (x, y, 1 - c))]


def _join_start(shard):
    return _split_start("join_halves_start", shard, (8, 128), shard.dtype, 1, _join_copies)


def _join_wait(flight, after):
    return _split_wait("join_halves_wait", flight, after, 1, _join_copies)[0]


def _adamw(name, g, g_row0, w, m, v):
    _, R, C = w.shape
    tm = next(cand for cand in (368, 256, 128, 64, 32, 16, 8) if R % cand == 0)
    assert g_row0 % tm == 0 and g.shape[1] == C

    def body(g_ref, w_ref, m_ref, v_ref, go_ref, d_ref, mo_ref, vo_ref):
        gt = g_ref[...]
        mt = ADAM_B1 * m_ref[...] + (1.0 - ADAM_B1) * gt
        vt = ADAM_B2 * v_ref[...] + (1.0 - ADAM_B2) * jnp.square(gt)
        m_hat = mt / (1.0 - ADAM_B1 ** ADAM_STEP)
        v_hat = vt / (1.0 - ADAM_B2 ** ADAM_STEP)
        go_ref[...] = gt
        d_ref[...] = -ADAM_LR * (m_hat / (jnp.sqrt(v_hat) + ADAM_EPS) + ADAM_WD * w_ref[...])
        mo_ref[...] = mt
        vo_ref[...] = vt

    state = pl.BlockSpec((None, tm, C), lambda i: (0, i, 0))
    return pl.pallas_call(
        body, name=name, grid=(R // tm,),
        in_specs=[pl.BlockSpec((tm, C), lambda i: (g_row0 // tm + i, 0)), state, state, state],
        out_specs=[state] * 4, out_shape=[jax.ShapeDtypeStruct((1, R, C), F32)] * 4,
        compiler_params=_cparams(("parallel",)),
    )(g, w, m, v)


def _unpack_weights(gathered, names):
    S = gathered.shape[0]
    shard_shapes = {"w_in": (D_MODEL, (QKV_WIDTH + 2 * D_MODEL) // S), "w_branch_na": (NA_WIDTH, D_MODEL // S),
                    "w_branch_dil": (DIL_OUT_WIDTH, D_MODEL // S), "w_out": (D_MODEL // S, D_MODEL),
                    "w_up": (D_MODEL, D_FF // S), "w_down": (D_FF // S, D_MODEL),
                    "w_ple_gate": (D_MODEL // S, D_MODEL), "w_ple_proj": (PLE_DIM, D_MODEL // S)}
    col_sharded = {"w_in", "w_branch_na", "w_branch_dil", "w_up", "w_ple_proj"}
    out, r0 = {}, 0
    for name in names:
        rows, cols = shard_shapes[name]
        n = rows * cols // PACK_W
        t = gathered[:, r0:r0 + n, :].reshape(S, rows, cols)
        r0 += n
        out[name] = t.transpose(1, 0, 2).reshape(rows, S * cols) if name in col_sharded else t.reshape(S * rows, cols)
    return out


def kernel(x, p, positions, g_mix, w_in, rpb, w_branch_na, w_branch_dil, w_out, g_mlp, w_up, w_down, g_ple, w_ple_gate, w_ple_proj, g_final, loss_target, m_g_mix, m_w_in, m_rpb, m_w_branch_na, m_w_branch_dil, m_w_out, m_g_mlp, m_w_up, m_w_down, m_g_ple, m_w_ple_gate, m_w_ple_proj, m_g_final, v_g_mix, v_w_in, v_rpb, v_w_branch_na, v_w_branch_dil, v_w_out, v_g_mlp, v_w_up, v_w_down, v_g_ple, v_w_ple_gate, v_w_ple_proj, v_g_final):
    shards = {"w_in": w_in[0], "w_branch_na": w_branch_na[0], "w_branch_dil": w_branch_dil[0], "w_out": w_out[0],
              "w_up": w_up[0], "w_down": w_down[0], "w_ple_gate": w_ple_gate[0], "w_ple_proj": w_ple_proj[0]}
    params = {"w_in": w_in, "w_branch_na": w_branch_na, "w_branch_dil": w_branch_dil, "w_out": w_out, "w_up": w_up,
              "w_down": w_down, "w_ple_gate": w_ple_gate, "w_ple_proj": w_ple_proj,
              "m_w_in": m_w_in, "m_w_branch_na": m_w_branch_na, "m_w_branch_dil": m_w_branch_dil, "m_w_out": m_w_out,
              "m_w_up": m_w_up, "m_w_down": m_w_down, "m_w_ple_gate": m_w_ple_gate, "m_w_ple_proj": m_w_ple_proj,
              "v_w_in": v_w_in, "v_w_branch_na": v_w_branch_na, "v_w_branch_dil": v_w_branch_dil, "v_w_out": v_w_out,
              "v_w_up": v_w_up, "v_w_down": v_w_down, "v_w_ple_gate": v_w_ple_gate, "v_w_ple_proj": v_w_ple_proj}

    xs, ps, tgt = x[0], p[0, 0], loss_target[0]
    T = xs.shape[0]
    TM = 512
    gm, gl, gp, gf = g_mix, g_mlp, g_ple, g_final.reshape(1, D_MODEL)

    across = _gather_halves_start(shards["w_in"].astype(BF), "in")
    a = _rowwise("norm_mix", lambda h, g: h * _rms(h) * g, T, TM, [_row(xs, TM), _full(gm)], [(D_MODEL, BF)],
                 after=(across.token,))
    cos2, sin_signed = _rope_tables(positions[0])
    tab = _na_bias_table(rpb[0])
    packed_mix = jnp.concatenate([_pack_rows(shards[n].astype(BF)) for n in GATHER_MIX], axis=0)
    packed_mlp = jnp.concatenate([_pack_rows(shards[n].astype(BF)) for n in GATHER_MLP], axis=0)
    w_in_shard, w_in_relay = _gather_halves_relay(across, (a, tab, cos2, sin_signed, packed_mix, packed_mlp), "in")
    w_in_all = _gather_halves_finish(w_in_shard, w_in_relay, w_in_relay.token, "in")
    w_in_full, w_gates = _assemble_w_in(w_in_all, 256)
    W = {"w_in": w_in_full}
    mix_flight = _gather_start(packed_mix, "mix", after=(w_in_all,))
    mlp_across = _gather_halves_start(packed_mlp, "mlp", after=(mix_flight.token,))

    n3 = 3 * NA_WIDTH
    qkv = _mm("in_na", a, W["w_in"], "nn", 1024, 768, 1024, [BF], after=(mlp_across.token,),
              b_view=(n3, (D_MODEL, 768), lambda j, k: (k, j)))
    z_dil = _mm("in_dil", a, W["w_in"], "nn", 1024, 768, 1024, [F32], after=(mlp_across.token,),
                b_view=(3 * DIL_WIDTH, (D_MODEL, 768), lambda j, k: (k, n3 // 768 + j)))
    z_gates = _mm("in_gates", a, w_gates, "nn", 1024,1024, 1024, [BF], after=(mlp_across.token,))

    dil_ops = _qkv_prep(z_dil, cos2, sin_signed, TM)
    y_na = _na_fwd(qkv, tab)
    band = [_band_fwd(*dil_ops[g], g) for g in range(len(DIL_GROUPS))]
    y_dil, w_grp, o_nat = _dil_merge_fwd([b[0] for b in band], [b[1] for b in band], T, TM)

    W.update(_unpack_weights(_gather_wait(mix_flight, y_dil, "mix"), GATHER_MIX))
    mlp_shard, mlp_relay = _gather_halves_relay(mlp_across, y_dil, "mlp")
    u_na = _mm("branch_na", y_na, W["w_branch_na"], "nn", 1024,1024, 512, [BF], after=(mlp_relay.token,))
    def gate_mix(acc, gn, gd, un):
        ud = acc.astype(BF)
        return ud, _sigmoid(gn.astype(F32)) * un.astype(F32) + _sigmoid(gd.astype(F32)) * ud.astype(F32)

    u_dil, mixed = _mm("branch_dil", y_dil, W["w_branch_dil"], "nn", 512, 1024, 256, [BF, BF], epilogue=gate_mix,
                       extras=((z_gates, 0), (z_gates, 1), u_na))

    def add_norm(d, h, g):
        h = h + d
        return h, h * _rms(h) * g

    h1, cn = _mm("out_proj", mixed, W["w_out"], "nn", 512, 1024, 1024, [F32, BF], epilogue=add_norm, extras=(xs,), consts=(gl,))
    mlp_all = _gather_halves_finish(mlp_shard, mlp_relay, cn, "mlp")
    W.update({n: t for n, t in _unpack_weights(mlp_all, GATHER_MLP).items() if n.startswith("w_ple")})
    chip_block = (None, D_MODEL, PACK_W)
    up, act = _mm("mlp_up", cn, mlp_all, "nn", 1024,1024, 1024, [BF, BF],
                  epilogue=lambda acc: (acc, jnp.square(jnp.maximum(acc, 0.0))), b_view=(D_FF, chip_block, lambda j, k: (j, 0, 0)))
    h2, en = _mm("mlp_down", act, mlp_all, "nn", 1024, 1024, 2048, [F32, BF], epilogue=add_norm, extras=(h1,), consts=(gp,),
                 b_view=(D_MODEL, (2, D_MODEL, PACK_W), lambda j, k: (k, 1, 0)))
    pp = _mm("ple_proj", ps, W["w_ple_proj"], "nn", 1024,1024, 256, [F32])

    def head(gtt, h2t, ppt, tg, g):
        sg = _sigmoid(gtt)
        h3 = h2t + sg * ppt
        yo = h3 * _rms(h3) * g
        diff = yo - tg
        loss = 0.5 * jnp.sum(jnp.mean(jnp.square(diff), axis=-1, keepdims=True), axis=0, keepdims=True)
        dh3, dg = _rms_bwd(diff * (1.0 / D_MODEL), h3, g)
        return dh3, dh3 * ppt * sg * (1.0 - sg), dh3 * sg, jnp.broadcast_to(loss, (1, 128)), dg

    dh3, d_gt, d_pp, loss_part, dg_final = _mm(
        "ple_gate_loss_head", en, W["w_ple_gate"], "nn", 512, 1024, 1024, [F32, BF, BF], epilogue=head,
        extras=(h2, pp, tgt), consts=(gf,), sums=[128, D_MODEL])

    early_shapes = {n: shards[n].shape for n in REDUCE_EARLY}
    early_rows = sum(r * c for r, c in early_shapes.values()) // PACK_W
    shard_rows = D_MODEL // N_CHIPS
    early_buf = _mm("g_ple_gate", en, d_gt, "tn", 1024, 1024, 2048, [F32],
                    into=(jax.ShapeDtypeStruct((N_CHIPS, early_rows, PACK_W), F32), (N_CHIPS, shard_rows, PACK_W),
                          lambda i, j: (0, 2 * D_MODEL // shard_rows, 0)))
    g_ple_proj = _mm("g_ple_proj", ps, d_pp, "tn", 256, 1024, 1024,[F32])

    def add_norm_bwd(dn, dh_out, h, g):
        dh, dg = _rms_bwd(dn, h, g)
        dh = dh_out + dh
        return dh, dh, dg

    dh2, dh2_b, dg_ple = _mm("d_ple_gate", d_gt, W["w_ple_gate"], "nt", 512, 1024, 1024, [F32, BF],
                             epilogue=add_norm_bwd, extras=(dh3, h2), consts=(gp,), sums=[D_MODEL])
    d_up = _mm("d_mlp_down", dh2_b, mlp_all, "nt", 1024,1024, 1024, [BF], b_view=(D_FF, chip_block, lambda j, k: (j, 1, 0)),
               epilogue=lambda acc, u: (acc * (2.0 * jnp.maximum(u.astype(F32), 0.0)),), extras=(up,))
    early_buf = _mm("g_mlp_down", act, dh2_b, "tn", 512, 1024, T, [F32],
                    into=(early_buf, (None, 512, PACK_W), lambda i, j: (i // 2, 2 + i % 2, 0)))
    early_buf = _mm("g_mlp_up", cn, d_up, "tn", 1024, 512, T, [F32],
                    into=(early_buf, (None, D_MODEL, 512), lambda i, j: (j // 2, 0, j % 2)))
    dh1, dh1_b, dg_mlp = _mm("d_mlp_up", d_up, mlp_all, "nt", 1024, 1024, 1024, [F32, BF], epilogue=add_norm_bwd,
                             b_view=(D_MODEL, chip_block, lambda j, k: (k, 0, 0)),
                             extras=(dh2, h1), consts=(gl,), sums=[D_MODEL])
    early_buf = _mm("g_out_proj", mixed, dh1_b, "tn", 1024, 1024, 1024, [F32],
                    into=(early_buf, (N_CHIPS, shard_rows, PACK_W), lambda i, j: (0, 2 * D_MODEL // shard_rows + 1, 0)))

    def gate_bwd(dm, gn, gd, un, ud):
        gn, gd, un, ud = (t.astype(F32) for t in (gn, gd, un, ud))
        sn, sd = _sigmoid(gn), _sigmoid(gd)
        return jnp.concatenate([dm * un * sn * (1.0 - sn), dm * ud * sd * (1.0 - sd)], axis=1), dm * sn, dm * sd

    dz_gates, d_u_na, d_u_dil = _mm("d_out_proj", dh1_b, W["w_out"], "nt", 512, 1024, 1024, [(BF, 2 * D_MODEL), BF, BF],
                                    epilogue=gate_bwd, extras=((z_gates, 0), (z_gates, 1), u_na, u_dil))
    g_branch_na = _mm("g_branch_na", y_na, d_u_na, "tn", 1024, 1024, 1024,[F32])
    g_branch_dil = _mm("g_branch_dil", y_dil, d_u_dil, "tn", 256, 1024, 1024,[F32])
    small_rows = [jnp.concatenate([_pack_rows(g[:, s * shard_rows:(s + 1) * shard_rows]) for g in (g_ple_proj, g_branch_na, g_branch_dil)],
                                  axis=0) for s in range(N_CHIPS)]
    early_buf = lax.dynamic_update_slice(early_buf, jnp.stack(small_rows), (0, 2 * D_MODEL + 2 * shard_rows, 0))
    early_tm = early_rows // 4
    swap_flight = _swap_start(early_buf, "early")
    d_y_na = _mm("d_branch_na", d_u_na, W["w_branch_na"], "nt", 1024,512, 1024, [BF], after=(swap_flight.token,))
    d_y_dil = _mm("d_branch_dil", d_u_dil, W["w_branch_dil"], "nt", 1024,256, 1024, [F32])

    dqa, dka, dva, dtab = _na_bwd(qkv, tab, d_y_na)
    early_g, early_got = _swap_wait(swap_flight, dqa, "early")
    early_pair, early_pair_b = _pair_sum(early_g, early_got, early_tm)
    scatter_flight = _scatter_start(early_pair_b, "early")

    do_res, dlse_res = _dil_merge_bwd(d_y_dil, o_nat, w_grp, TM, after=(scatter_flight.token,))
    d_dil = [_band_bwd(*dil_ops[g], do_res[g], dlse_res[g], g) for g in range(len(DIL_GROUPS))]

    dz_qkv = _qkv_unprep((dqa, dka, dva), d_dil, cos2, sin_signed, TM)
    in_cols = shards["w_in"].shape[1]
    qkv_rows, gate_tm = dz_qkv.shape[1], 256
    assert qkv_rows % gate_tm == 0
    g_in = _mm("g_in_qkv", dz_qkv, a, "tn", 640, 1024, T, [F32],
               into=(jax.ShapeDtypeStruct((N_CHIPS * in_cols, D_MODEL), F32), (640, D_MODEL), lambda i, j: (i, 0)))
    g_in = _mm("g_in_gates", dz_gates, a, "tn", gate_tm, 1024, T, [F32],
               into=(g_in, (gate_tm, D_MODEL), lambda i, j: (qkv_rows // gate_tm + i, 0)))
    early_mine = _chip_sum(early_pair, _scatter_wait(scatter_flight, (g_in,), "early"), early_tm)
    join_flight = _join_start(early_mine)

    late_tm = in_cols // 4
    late_swap = _swap_start(g_in.reshape(N_CHIPS, in_cols, D_MODEL), "late")
    d_a = _mm("d_in_qkv", dz_qkv, W["w_in"], "nt", 512, 1024, qkv_rows, [F32], after=(late_swap.token, join_flight.token),
              b_view=(D_MODEL, (D_MODEL, qkv_rows), lambda j, k: (j, k)))
    late_g, late_got = _swap_wait(late_swap, d_a, "late")
    late_pair, late_pair_b = _pair_sum(late_g, late_got, late_tm)
    late_scatter = _scatter_start(late_pair_b, "late")
    d_rpb = _na_rpb_grad(dtab, after=(late_scatter.token,))[:, :2 * NA_WIN_ROWS - 1, :2 * NA_WIN_COLS - 1]
    def first_bwd(dn_gates, dn_qkv, dh_out, h, g):
        dh, dg = _rms_bwd(dn_gates + dn_qkv, h, g)
        return dh_out + dh, dg

    grad_x, dg_mix = _mm("d_in_gates", dz_gates, w_gates, "nt", 512, 1024, 2048, [F32], epilogue=first_bwd,
                         extras=(d_a, dh1, xs), consts=(gm,), sums=[D_MODEL], after=(late_scatter.token,))
    early_shard = _join_wait(join_flight, grad_x)

    n_rpb = rpb.size
    rpb_rows = 4
    small = jnp.concatenate([
        dg_mix, dg_mlp, dg_ple, dg_final,
        jnp.pad(d_rpb.reshape(-1), (0, rpb_rows * D_MODEL - n_rpb)).reshape(rpb_rows, D_MODEL),
        jnp.pad(loss_part, ((0, 0), (0, D_MODEL - loss_part.shape[1]))),
        jnp.zeros((SMALL_ROWS - 5 - rpb_rows, D_MODEL), F32)], axis=0)
    out = {"grad": {}, "delta": {}, "new_m": {}, "new_v": {}}

    def update(n, g, row0):
        res = _adamw("adamw_" + n, g, row0, params[n], params["m_" + n], params["v_" + n])
        for kind, t in zip(("grad", "delta", "new_m", "new_v"), res, strict=True):
            out[kind][n] = t

    row0 = 0
    for n in REDUCE_EARLY:
        rows, cols = early_shapes[n]
        n_rows = rows * cols // PACK_W
        if cols == PACK_W:
            update(n, early_shard, row0)
        else:
            update(n, early_shard[row0:row0 + n_rows].reshape(rows, cols), 0)
        row0 += n_rows
    late_others = _scatter_wait(late_scatter, (*[out["new_v"][n] for n in REDUCE_EARLY], d_rpb), "late")
    late_mine = _chip_sum(late_pair, late_others, late_tm)
    small_flight = _allreduce_start(small, after=(late_mine,))
    res = _adamw("adamw_w_in", _join_halves(late_mine, after=(small_flight.token,)), 0,
                 *[jnp.swapaxes(params[n], 1, 2) for n in ("w_in", "m_w_in", "v_w_in")])
    small = _allreduce_finish(small_flight, res[3])
    for kind, t in zip(("grad", "delta", "new_m", "new_v"), res, strict=True):
        out[kind]["w_in"] = jnp.swapaxes(t, 1, 2)
    loss = small[4 + rpb_rows, 0]

    def small_pack(a0, a1, a2, a3, r):
        return jnp.concatenate([a0.reshape(1, -1), a1.reshape(1, -1), a2.reshape(1, -1), a3.reshape(1, -1),
                                jnp.pad(r.reshape(-1), (0, rpb_rows * D_MODEL - n_rpb)).reshape(rpb_rows, D_MODEL)], axis=0)

    small_res = _adamw("adamw_small", small, 0, small_pack(g_mix, g_mlp, g_ple, g_final, rpb)[None],
                       small_pack(m_g_mix, m_g_mlp, m_g_ple, m_g_final, m_rpb)[None],
                       small_pack(v_g_mix, v_g_mlp, v_g_ple, v_g_final, v_rpb)[None])

    def small_unpack(t):
        return {"g_mix": t[0].reshape(g_mix.shape), "g_mlp": t[1].reshape(g_mlp.shape), "g_ple": t[2].reshape(g_ple.shape),
                "g_final": t[3].reshape(g_final.shape), "rpb": t[4:].reshape(-1)[:n_rpb].reshape(rpb.shape)}

    for kind, t in zip(("grad", "delta", "new_m", "new_v"), small_res, strict=True):
        out[kind].update(small_unpack(t[0]))

    order = ["g_mix", "w_in", "rpb", "w_branch_na", "w_branch_dil", "w_out", "g_mlp", "w_up", "w_down", "g_ple",
             "w_ple_gate", "w_ple_proj", "g_final"]
    return (loss, grad_x[None], *[out["grad"][n] for n in order], *[out["delta"][n] for n in order],
            *[out["new_m"][n] for n in order], *[out["new_v"][n] for n in order])
```

```python
import functools
from typing import NamedTuple

import jax
import jax.numpy as jnp
from jax import lax
from jax.experimental import pallas as pl
from jax.experimental.pallas import tpu as pltpu

BF = jnp.bfloat16
F32 = jnp.float32
MESH = pl.DeviceIdType.MESH
ANY = pl.BlockSpec(memory_space=pl.ANY)

V7X_VMEM_BYTES = 64 * 1024 * 1024
VMEM_LIMIT = V7X_VMEM_BYTES - 16 * 1024 * 1024

D_MODEL = 1024
HEAD_DIM = 64
GRID_W = 64
NA_HEADS = 8
NA_WIN_ROWS = 8
NA_WIN_COLS = 16
NA_WIDTH = NA_HEADS * HEAD_DIM
DIL_GROUPS = ((128, 1), (512, 4), (2048, 16))
DIL_HPG = 4
DIL_HEADS = DIL_HPG * len(DIL_GROUPS)
DIL_WIDTH = DIL_HEADS * HEAD_DIM
DIL_OUT_WIDTH = DIL_HPG * HEAD_DIM
DIL_RADIUS = 64
QKV_WIDTH = 3 * NA_WIDTH + 3 * DIL_WIDTH
D_FF = 4 * D_MODEL
PLE_DIM = 256
ROPE_THETA = 10000.0
RMS_EPS = 1e-6
NEG_INF = -1e30
Q_SCALE = HEAD_DIM ** -0.5

ADAM_LR = 0.001
ADAM_B1 = 0.9
ADAM_B2 = 0.999
ADAM_EPS = 1e-08
ADAM_WD = 0.01
ADAM_STEP = 10

N_CHIPS = 4
N_DEV = 8
PACK_W = 1024
GATHER_MIX = ("w_branch_na", "w_branch_dil", "w_out")
GATHER_MLP = ("w_up", "w_down", "w_ple_gate", "w_ple_proj")
REDUCE_EARLY = ("w_up", "w_down", "w_ple_gate", "w_out", "w_ple_proj", "w_branch_na", "w_branch_dil")
SMALL_ROWS = 16


def _cparams(sem=None):
    return pltpu.CompilerParams(dimension_semantics=sem, vmem_limit_bytes=VMEM_LIMIT)


def _mm(name, a, b, mode, tm, tn, tk, out_dtypes, epilogue=None, extras=(), consts=(), sums=(), after=(), into=None,
        b_view=None):
    if mode == "nn":
        (M, K), N = a.shape, b.shape[1]
    elif mode == "nt":
        (M, K), N = a.shape, b.shape[0]
    else:
        (K, M), N = a.shape, b.shape[1]
    if b_view is not None:
        N = b_view[0]
    tm, tn, tk = min(tm, M), min(tn, N), min(tk, K)
    assert M % tm == 0 and N % tn == 0 and K % tk == 0, (name, M, N, K, tm, tn, tk)
    if mode == "nn":
        a_spec = pl.BlockSpec((tm, tk), lambda i, j, k: (i, k))
        b_spec = pl.BlockSpec((tk, tn), lambda i, j, k: (k, j))
        dims = (((1,), (0,)), ((), ()))
    elif mode == "nt":
        a_spec = pl.BlockSpec((tm, tk), lambda i, j, k: (i, k))
        b_spec = pl.BlockSpec((tn, tk), lambda i, j, k: (j, k))
        dims = (((1,), (1,)), ((), ()))
    else:
        a_spec = pl.BlockSpec((tk, tm), lambda i, j, k: (k, i))
        b_spec = pl.BlockSpec((tk, tn), lambda i, j, k: (k, j))
        dims = (((0,), (0,)), ((), ()))
    if b_view is not None:
        b_spec = pl.BlockSpec(b_view[1], lambda i, j, k: b_view[2](j, k))
    nk = K // tk
    n_extra, n_const, n_out, n_sum = len(extras), len(consts), len(out_dtypes), len(sums)
    tile = pl.BlockSpec((tm, tn), lambda i, j, k: (i, j))
    assert not sums or tn == N, "row sums need whole rows in a tile"
    wide = [e for e in (*extras, *out_dtypes) if isinstance(e, tuple)]
    assert not wide or tn == N
    extra_specs = [pl.BlockSpec((tm, tn), functools.partial(lambda c, i, j, k: (i, c), e[1])) if isinstance(e, tuple) else tile
                   for e in extras]
    extras = [e[0] if isinstance(e, tuple) else e for e in extras]
    out_widths = [d[1] if isinstance(d, tuple) else N for d in out_dtypes]
    out_dtypes = [d[0] if isinstance(d, tuple) else d for d in out_dtypes]

    n_after = len(after)

    def body(a_ref, b_ref, *rest):
        extra_refs, rest = rest[:n_extra + n_const], rest[n_extra + n_const + n_after:]
        out_refs, sum_refs, acc = rest[:n_out], rest[n_out:n_out + n_sum], rest[-1]
        i, k = pl.program_id(0), pl.program_id(2)
        def product():
            if len(b_ref.shape) == 3:
                w = tk // b_ref.shape[0]
                parts = [lax.dot_general(a_ref[:, s * w:(s + 1) * w].astype(BF), b_ref[s].astype(BF), dims, preferred_element_type=F32)
                         for s in range(b_ref.shape[0])]
                return functools.reduce(lambda x, y: x + y, parts)
            return lax.dot_general(a_ref[...].astype(BF), b_ref[...].astype(BF), dims, preferred_element_type=F32)

        if nk > 1:
            @pl.when(k == 0)
            def _():
                acc[...] = jnp.zeros_like(acc)

            acc[...] += product()

        @pl.when(k == nk - 1)
        def _():
            total = product() if nk == 1 else acc[...]
            outs = (total,) if epilogue is None else epilogue(total, *[e[...] for e in extra_refs])
            for o_ref, val in zip(out_refs, outs[:n_out], strict=True):
                o_ref[...] = val.astype(o_ref.dtype).reshape(o_ref.shape)
            for s_ref, val in zip(sum_refs, outs[n_out:], strict=True):
                @pl.when(i == 0)
                def _():
                    s_ref[...] = val

                @pl.when(i != 0)
                def _():
                    s_ref[...] += val

    out_specs = ([tile if w == N else pl.BlockSpec((tm, w), lambda i, j, k: (i, 0)) for w in out_widths]
                 + [pl.BlockSpec((1, c), lambda i, j, k: (0, 0)) for c in sums])
    out_shape = ([jax.ShapeDtypeStruct((M, w), dt) for dt, w in zip(out_dtypes, out_widths, strict=True)]
                 + [jax.ShapeDtypeStruct((1, c), F32) for c in sums])
    operands, aliases = [a, b, *extras, *consts, *after], {}
    in_specs = ([a_spec, b_spec] + extra_specs
                + [pl.BlockSpec(c.shape, functools.partial(lambda nd, i, j, k: (0,) * nd, c.ndim)) for c in consts] + [ANY] * n_after)
    if into is not None:
        assert n_out == 1
        target, block, index = into
        out_specs = [pl.BlockSpec(block, lambda i, j, k: index(i, j))]
        out_shape = [jax.ShapeDtypeStruct(target.shape, target.dtype)]
        if not isinstance(target, jax.ShapeDtypeStruct):
            aliases = {len(operands): 0}
            operands.append(target)
            in_specs.append(ANY)
            n_after += 1

    outs = pl.pallas_call(
        body, name=name, grid=(M // tm, N // tn, nk),
        in_specs=in_specs, out_specs=out_specs, out_shape=out_shape,
        scratch_shapes=[pltpu.VMEM((tm, tn) if nk > 1 else (8, 128), F32)], input_output_aliases=aliases,
        compiler_params=_cparams(("arbitrary",) * 3 if sums else ("parallel", "parallel", "arbitrary")),
    )(*operands)
    return outs[0] if len(outs) == 1 else outs


def _row(arr, tm, col_block=None, width=None):
    width = arr.shape[1] if width is None else width
    cb = 0 if col_block is None else col_block
    return arr, pl.BlockSpec((tm, width), lambda i: (i, cb))


def _full(arr):
    nd = arr.ndim
    return arr, pl.BlockSpec(arr.shape, lambda i: (0,) * nd)


def _rowwise(name, body, T, tm, ins, outs, sums=(), after=()):
    n_in, n_out, n_sum, n_after = len(ins), len(outs), len(sums), len(after)

    def kern(*refs):
        in_refs, refs = refs[:n_in], refs[n_in + n_after:]
        out_refs, sum_refs = refs[:n_out], refs[n_out:]
        res = body(*[r[...] for r in in_refs])
        res = res if isinstance(res, tuple) else (res,)
        for o_ref, val in zip(out_refs, res[:n_out], strict=True):
            o_ref[...] = val.astype(o_ref.dtype)
        if n_sum:
            @pl.when(pl.program_id(0) == 0)
            def _():
                for s_ref in sum_refs:
                    s_ref[...] = jnp.zeros_like(s_ref)

            for s_ref, val in zip(sum_refs, res[n_out:], strict=True):
                s_ref[...] += val

    res = pl.pallas_call(
        kern, name=name, grid=(T // tm,),
        in_specs=[spec for _, spec in ins] + [ANY] * n_after,
        out_specs=[pl.BlockSpec((tm, c), lambda i: (i, 0)) for c, _ in outs]
        + [pl.BlockSpec((1, c), lambda i: (0, 0)) for c in sums],
        out_shape=[jax.ShapeDtypeStruct((T, c), dt) for c, dt in outs]
        + [jax.ShapeDtypeStruct((1, c), F32) for c in sums],
        compiler_params=_cparams(("arbitrary",)),
    )(*[a for a, _ in ins], *after)
    return res[0] if len(res) == 1 else res


def _sigmoid(x):
    return 1.0 / (1.0 + jnp.exp(-x))


def _rms(h):
    return lax.rsqrt(jnp.mean(h * h, axis=-1, keepdims=True) + RMS_EPS)


def _rms_bwd(dy, h, g):
    r = _rms(h)
    n = h * r
    dn = dy * g
    dh = r * (dn - n * jnp.mean(dn * n, axis=-1, keepdims=True))
    return dh, jnp.sum(dy * n, axis=0, keepdims=True)


def _rope(x, cos2, sin_signed):
    lane = lax.broadcasted_iota(jnp.int32, x.shape, 1)
    swapped = jnp.where((lane % HEAD_DIM) < HEAD_DIM // 2, pltpu.roll(x, 128 - HEAD_DIM // 2, 1), pltpu.roll(x, HEAD_DIM // 2, 1))
    return x * cos2 + swapped * sin_signed


NA_KEYS = NA_WIN_ROWS * GRID_W
NA_BASES = 8


def _na_row_geometry(r, rows):
    first = jnp.clip(r - NA_WIN_ROWS // 2, 0, rows - NA_WIN_ROWS)
    base = first - r + (NA_WIN_ROWS - 1)
    return pl.multiple_of(first * GRID_W, GRID_W), base


NA_ROWS_PER_STEP = 16
NA_BWD_ROWS_PER_STEP = 8


def _softmax_rows(s):
    p = jnp.exp(s - jnp.max(s, axis=-1, keepdims=True))
    return p / jnp.sum(p, axis=-1, keepdims=True)


def _split_pair(t):
    first = lax.broadcasted_iota(jnp.int32, t.shape, 1) < HEAD_DIM
    zero = jnp.zeros_like(t)
    return jnp.where(first, t, zero), jnp.where(first, zero, t)


def _join_pair(a, b):
    return jnp.where(lax.broadcasted_iota(jnp.int32, a.shape, 1) < HEAD_DIM, a, b)


_NT = (((1,), (1,)), ((), ()))
_TN = (((0,), (0,)), ((), ()))


def _na_fwd(qkv, tab):
    T = qkv.shape[0]
    rows = T // GRID_W
    n_pairs = NA_WIDTH // 128

    def body(q_ref, k_ref, v_ref, tab_ref, y_ref):
        def step(it, carry):
            geo = [_na_row_geometry(it * NA_ROWS_PER_STEP + u, rows) for u in range(NA_ROWS_PER_STEP)]
            q0s = [pl.multiple_of((it * NA_ROWS_PER_STEP + u) * GRID_W, GRID_W) for u in range(NA_ROWS_PER_STEP)]
            ss = [lax.dot_general(jnp.concatenate(_split_pair(q_ref[pl.ds(q0, GRID_W), :] * Q_SCALE), axis=0),
                                  k_ref[pl.ds(k0, NA_KEYS), :], _NT, preferred_element_type=F32)
                  for q0, (k0, _) in zip(q0s, geo)]
            ps = [_softmax_rows(s + jnp.concatenate([tab_ref[0, base], tab_ref[1, base]], axis=0)) for s, (_, base) in zip(ss, geo)]
            ys = [jnp.dot(p.astype(BF), v_ref[pl.ds(k0, NA_KEYS), :], preferred_element_type=F32) for p, (k0, _) in zip(ps, geo)]
            for q0, y2 in zip(q0s, ys):
                y_ref[pl.ds(q0, GRID_W), :] = _join_pair(y2[:GRID_W], y2[GRID_W:]).astype(y_ref.dtype)
            return carry

        lax.fori_loop(0, rows // NA_ROWS_PER_STEP, step, 0)

    def cols(first):
        return pl.BlockSpec((T, 128), lambda j: (0, first + j))

    return pl.pallas_call(
        body, name="na_fwd", grid=(n_pairs,),
        in_specs=[cols(0), cols(n_pairs), cols(2 * n_pairs), pl.BlockSpec((2, NA_BASES, GRID_W, NA_KEYS), lambda j: (j, 0, 0, 0))],
        out_specs=cols(0), out_shape=jax.ShapeDtypeStruct((T, NA_WIDTH), BF),
        compiler_params=_cparams(("parallel",)),
    )(qkv, qkv, qkv, tab)


def _na_bwd(qkv, tab, do):
    T = qkv.shape[0]
    rows = T // GRID_W
    n_pairs = NA_WIDTH // 128

    def body(q_ref, k_ref, v_ref, tab_ref, do_ref, dq_ref, dk_out, dv_out, dtab_ref, dk_ref, dv_ref):
        dk_ref[...] = jnp.zeros_like(dk_ref)
        dv_ref[...] = jnp.zeros_like(dv_ref)
        dtab_ref[...] = jnp.zeros_like(dtab_ref)

        def step(it, carry):
            U = NA_BWD_ROWS_PER_STEP
            geo = [_na_row_geometry(it * U + u, rows) for u in range(U)]
            q0s = [pl.multiple_of((it * U + u) * GRID_W, GRID_W) for u in range(U)]
            q2s = [jnp.concatenate(_split_pair(q_ref[pl.ds(q0, GRID_W), :] * Q_SCALE), axis=0) for q0 in q0s]
            do2s = [jnp.concatenate(_split_pair(do_ref[pl.ds(q0, GRID_W), :]), axis=0) for q0 in q0s]
            ss = [lax.dot_general(q2, k_ref[pl.ds(k0, NA_KEYS), :], _NT, preferred_element_type=F32) for q2, (k0, _) in zip(q2s, geo)]
            dps = [lax.dot_general(do2, v_ref[pl.ds(k0, NA_KEYS), :], _NT, preferred_element_type=F32) for do2, (k0, _) in zip(do2s, geo)]
            ps = [_softmax_rows(s + jnp.concatenate([tab_ref[0, base], tab_ref[1, base]], axis=0)) for s, (_, base) in zip(ss, geo)]
            dss = [p * (dp - jnp.sum(dp * p, axis=-1, keepdims=True)) for p, dp in zip(ps, dps)]
            dvs = [lax.dot_general(p.astype(BF), do2, _TN, preferred_element_type=F32) for p, do2 in zip(ps, do2s)]
            dsbs = [ds.astype(BF) for ds in dss]
            dqs = [jnp.dot(dsb, k_ref[pl.ds(k0, NA_KEYS), :], preferred_element_type=F32) for dsb, (k0, _) in zip(dsbs, geo)]
            dks = [lax.dot_general(dsb, q2, _TN, preferred_element_type=F32) for dsb, q2 in zip(dsbs, q2s)]
            for u in range(U):
                k0, base = geo[u]
                dtab_ref[0, base] += dss[u][:GRID_W]
                dtab_ref[1, base] += dss[u][GRID_W:]
                dq_ref[pl.ds(q0s[u], GRID_W), :] = (_join_pair(dqs[u][:GRID_W], dqs[u][GRID_W:]) * Q_SCALE).astype(dq_ref.dtype)
                dk_ref[pl.ds(k0, NA_KEYS), :] += dks[u]
                dv_ref[pl.ds(k0, NA_KEYS), :] += dvs[u]
            return carry

        lax.fori_loop(0, rows // NA_BWD_ROWS_PER_STEP, step, 0)
        dk_out[...] = dk_ref[...].astype(dk_out.dtype)
        dv_out[...] = dv_ref[...].astype(dv_out.dtype)

    def cols(first):
        return pl.BlockSpec((T, 128), lambda j: (0, first + j))

    tabs = pl.BlockSpec((2, NA_BASES, GRID_W, NA_KEYS), lambda j: (j, 0, 0, 0))
    wide = jax.ShapeDtypeStruct((T, NA_WIDTH), BF)
    return pl.pallas_call(
        body, name="na_bwd", grid=(n_pairs,),
        in_specs=[cols(0), cols(n_pairs), cols(2 * n_pairs), tabs, cols(0)],
        out_specs=[cols(0), cols(0), cols(0), tabs],
        out_shape=[wide, wide, wide, jax.ShapeDtypeStruct((NA_HEADS, NA_BASES, GRID_W, NA_KEYS), F32)],
        scratch_shapes=[pltpu.VMEM((T, 128), F32), pltpu.VMEM((T, 128), F32)],
        compiler_params=_cparams(("parallel",)),
    )(qkv, qkv, qkv, tab, do)


def _na_bias_table(rpb):
    H, n_rows, n_cols = rpb.shape

    def body(r_ref, tab_ref):
        q = lax.broadcasted_iota(jnp.int32, (GRID_W, 128), 0)
        kc = lax.broadcasted_iota(jnp.int32, (GRID_W, 128), 1)
        first = jnp.clip(q - NA_WIN_COLS // 2, 0, GRID_W - NA_WIN_COLS)
        valid = (kc >= first) & (kc < first + NA_WIN_COLS)
        toeplitz = []
        for ro in range(n_rows):
            row = jnp.broadcast_to(r_ref[pl.ds(ro, 1), :], (GRID_W, 128))
            shifted = pltpu.roll(pltpu.roll(row, 128 - (NA_WIN_COLS - 1), 1), 0, 1, stride=1, stride_axis=0)
            toeplitz.append(jnp.where(valid, shifted, NEG_INF))
        for base in range(NA_BASES):
            for j in range(NA_WIN_ROWS // 2):
                even, odd = toeplitz[base + 2 * j], toeplitz[base + 2 * j + 1]
                tab_ref[base, :, pl.ds(j * 128, 128)] = jnp.where(kc < GRID_W, even, pltpu.roll(odd, GRID_W, 1))

    padded = jnp.pad(rpb, ((0, 0), (0, 16 - n_rows), (0, 128 - n_cols)))
    return pl.pallas_call(
        body, name="na_bias_table", grid=(H,),
        in_specs=[pl.BlockSpec((None, 16, 128), lambda h: (h, 0, 0))],
        out_specs=pl.BlockSpec((None, NA_BASES, GRID_W, NA_KEYS), lambda h: (h, 0, 0, 0)),
        out_shape=jax.ShapeDtypeStruct((H, NA_BASES, GRID_W, NA_KEYS), F32),
        compiler_params=_cparams(("parallel",)),
    )(padded)


def _na_rpb_grad(dtab, after=()):
    H = dtab.shape[0]
    n_rows = 2 * NA_WIN_ROWS - 1
    n_cols = 2 * NA_WIN_COLS - 1

    def body(d_ref, *rest):
        o_ref = rest[-1]
        lane = lax.broadcasted_iota(jnp.int32, (GRID_W, 128), 1)
        low = lane < GRID_W
        flip = (lax.broadcasted_iota(jnp.int32, (GRID_W, GRID_W), 0) + lax.broadcasted_iota(jnp.int32, (GRID_W, GRID_W), 1)
                == GRID_W - 1).astype(BF)

        def reverse_rows(t):
            out = jnp.zeros_like(t)
            for _ in range(3):
                piece = t.astype(BF)
                out = out + jnp.dot(flip, piece, preferred_element_type=F32)
                t = t - piece.astype(F32)
            return out

        out_rows = []
        for ro in range(n_rows):
            acc = jnp.zeros((GRID_W, 128), F32)
            for base in range(NA_BASES):
                i = ro - base
                if not 0 <= i < NA_WIN_ROWS:
                    continue
                pair = d_ref[base, :, pl.ds((i // 2) * 128, 128)]
                if i % 2:
                    pair = pltpu.roll(pair, GRID_W, 1)
                acc = acc + jnp.where(low, pair, 0.0)
            skew = pltpu.roll(reverse_rows(acc), 0, 1, stride=1, stride_axis=0)
            diag = jnp.sum(skew, axis=0, keepdims=True)
            out_rows.append(pltpu.roll(jnp.broadcast_to(diag, (8, 128)), 128 - (GRID_W - NA_WIN_COLS), 1)[:1])
        out_rows.append(jnp.zeros((1, 128), F32))
        res = jnp.concatenate(out_rows, axis=0)
        o_ref[...] = jnp.where(lax.broadcasted_iota(jnp.int32, res.shape, 1) < n_cols, res, 0.0)

    return pl.pallas_call(
        body, name="na_rpb_grad", grid=(H,),
        in_specs=[pl.BlockSpec((None, NA_BASES, GRID_W, NA_KEYS), lambda h: (h, 0, 0, 0))] + [ANY] * len(after),
        out_specs=pl.BlockSpec((None, n_rows + 1, 128), lambda h: (h, 0, 0)),
        out_shape=jax.ShapeDtypeStruct((H, n_rows + 1, 128), F32),
        compiler_params=_cparams(("parallel",)),
    )(dtab, *after)


BAND_Q = 128
BAND_KEYS = BAND_Q + 2 * DIL_RADIUS


def _band_geometry(n, L):
    q0 = pl.multiple_of(n * BAND_Q, BAND_Q)
    k0 = pl.multiple_of(jnp.clip(q0 - DIL_RADIUS, 0, L - BAND_KEYS), DIL_RADIUS)
    qi = q0 + lax.broadcasted_iota(jnp.int32, (BAND_Q, BAND_KEYS), 0)
    kj = k0 + lax.broadcasted_iota(jnp.int32, (BAND_Q, BAND_KEYS), 1)
    return q0, k0, jnp.abs(qi - kj) <= DIL_RADIUS


DIL_PAIRS = DIL_OUT_WIDTH // 128


def _residue_shape(dil, T, dtype):
    return jax.ShapeDtypeStruct((DIL_PAIRS, dil, T // dil, 128), dtype)


def _residue_tile(dil, tm):
    return pl.BlockSpec((DIL_PAIRS, dil, tm // dil, 128), lambda i: (0, 0, i, 0))


def _to_natural(ref, scratch, dil, tm):
    tiles = []
    for pair in range(DIL_PAIRS):
        if dil == 1:
            tiles.append(ref[pair, 0].astype(F32))
            continue
        for r in range(dil):
            scratch[pl.ds(r, tm // dil, stride=dil), :] = ref[pair, r].astype(F32)
        tiles.append(scratch[...])
    return tiles


def _from_natural(tile, scratch, ref, pair, dil, tm):
    if dil == 1:
        ref[pair, 0] = tile.astype(ref.dtype)
        return
    scratch[...] = tile
    for r in range(dil):
        ref[pair, r] = scratch[pl.ds(r, tm // dil, stride=dil), :].astype(ref.dtype)


def _band_specs(group, T):
    dil = DIL_GROUPS[group][1]
    L = T // dil
    assert L % BAND_Q == 0 and L >= BAND_KEYS, (T, dil)
    per_residue = min(BAND_BLOCKS_PER_STEP, L // BAND_Q)
    residues = min(dil, BAND_BLOCKS_PER_STEP // per_residue)
    spec = pl.BlockSpec((None, residues, L, 128), lambda s: (s % DIL_PAIRS, s // DIL_PAIRS, 0, 0))
    return L, residues, per_residue, (dil // residues * DIL_PAIRS,), spec


BAND_BLOCKS_PER_STEP = 8


def _band_softmax(s, valid):
    s = jnp.where(valid, s, NEG_INF)
    m = jnp.max(s, axis=-1, keepdims=True)
    p = jnp.exp(s - m)
    l = jnp.sum(p, axis=-1, keepdims=True)
    return p / l, m + jnp.log(l)


def _band_fwd(q, k, v, group):
    T = q.shape[1] * q.shape[2]
    L, residues, U, grid, spec = _band_specs(group, T)

    def body(q_ref, k_ref, v_ref, o_ref, lse_ref):
        def step(it, carry):
            geo = [(r, *_band_geometry(it * U + u, L)) for r in range(residues) for u in range(U)]
            ss = [lax.dot_general(jnp.concatenate(_split_pair(q_ref[r, pl.ds(q0, BAND_Q), :]), axis=0),
                                  k_ref[r, pl.ds(k0, BAND_KEYS), :], _NT, preferred_element_type=F32) for r, q0, k0, _ in geo]
            pls = [_band_softmax(s, jnp.concatenate([valid, valid], axis=0)) for s, (_, _, _, valid) in zip(ss, geo)]
            os = [jnp.dot(p.astype(BF), v_ref[r, pl.ds(k0, BAND_KEYS), :], preferred_element_type=F32)
                  for (p, _), (r, _, k0, _) in zip(pls, geo)]
            for (r, q0, _, _), o2, (_, lse) in zip(geo, os, pls):
                o_ref[r, pl.ds(q0, BAND_Q), :] = _join_pair(o2[:BAND_Q], o2[BAND_Q:])
                lse2 = jnp.broadcast_to(lse, (2 * BAND_Q, 128))
                lse_ref[r, pl.ds(q0, BAND_Q), :] = _join_pair(lse2[:BAND_Q], lse2[BAND_Q:])
            return carry

        lax.fori_loop(0, L // (BAND_Q * U), step, 0)

    res = _residue_shape(DIL_GROUPS[group][1], T, F32)
    return pl.pallas_call(
        body, name=f"band_fwd_g{group}", grid=grid,
        in_specs=[spec] * 3, out_specs=[spec] * 2, out_shape=[res, res],
        compiler_params=_cparams(("parallel",)),
    )(q, k, v)


def _band_bwd(q, k, v, do, dlse, group):
    T = q.shape[1] * q.shape[2]
    L, residues, U, grid, spec = _band_specs(group, T)

    def body(q_ref, k_ref, v_ref, do_ref, dlse_ref, dq_ref, dk_ref, dv_ref):
        dk_ref[...] = jnp.zeros_like(dk_ref)
        dv_ref[...] = jnp.zeros_like(dv_ref)

        def step(it, carry):
            geo = [(r, *_band_geometry(it * U + u, L)) for r in range(residues) for u in range(U)]
            q2s = [jnp.concatenate(_split_pair(q_ref[r, pl.ds(q0, BAND_Q), :]), axis=0) for r, q0, _, _ in geo]
            do2s = [jnp.concatenate(_split_pair(do_ref[r, pl.ds(q0, BAND_Q), :]), axis=0) for r, q0, _, _ in geo]
            ss = [lax.dot_general(q2, k_ref[r, pl.ds(k0, BAND_KEYS), :], _NT, preferred_element_type=F32)
                  for q2, (r, _, k0, _) in zip(q2s, geo)]
            dps = [lax.dot_general(do2, v_ref[r, pl.ds(k0, BAND_KEYS), :], _NT, preferred_element_type=F32)
                   for do2, (r, _, k0, _) in zip(do2s, geo)]
            ps = [_band_softmax(s, jnp.concatenate([valid, valid], axis=0))[0] for s, (_, _, _, valid) in zip(ss, geo)]
            dss = []
            for p, dp, (r, q0, _, _) in zip(ps, dps, geo):
                dl = dlse_ref[r, pl.ds(q0, BAND_Q), :]
                dl2 = jnp.concatenate([dl[:, :1], dl[:, HEAD_DIM:HEAD_DIM + 1]], axis=0)
                dss.append(p * (dp - jnp.sum(dp * p, axis=-1, keepdims=True) + dl2))
            dvs = [lax.dot_general(p.astype(BF), do2, _TN, preferred_element_type=F32) for p, do2 in zip(ps, do2s)]
            dsbs = [ds.astype(BF) for ds in dss]
            dqs = [jnp.dot(dsb, k_ref[r, pl.ds(k0, BAND_KEYS), :], preferred_element_type=F32) for dsb, (r, _, k0, _) in zip(dsbs, geo)]
            dks = [lax.dot_general(dsb, q2, _TN, preferred_element_type=F32) for dsb, q2 in zip(dsbs, q2s)]
            for u, (r, q0, k0, _) in enumerate(geo):
                dq_ref[r, pl.ds(q0, BAND_Q), :] = _join_pair(dqs[u][:BAND_Q], dqs[u][BAND_Q:])
                dk_ref[r, pl.ds(k0, BAND_KEYS), :] += dks[u]
                dv_ref[r, pl.ds(k0, BAND_KEYS), :] += dvs[u]
            return carry

        lax.fori_loop(0, L // (BAND_Q * U), step, 0)

    res = _residue_shape(DIL_GROUPS[group][1], T, F32)
    return pl.pallas_call(
        body, name=f"band_bwd_g{group}", grid=grid,
        in_specs=[spec] * 5, out_specs=[spec] * 3, out_shape=[res] * 3,
        compiler_params=_cparams(("parallel",)),
    )(q, k, v, do, dlse)


def _head_sums(t):
    head = lax.broadcasted_iota(jnp.int32, t.shape, 1) // HEAD_DIM
    out = jnp.zeros_like(t)
    for h in range(t.shape[1] // HEAD_DIM):
        mine = head == h
        out = jnp.where(mine, jnp.sum(jnp.where(mine, t, 0.0), axis=-1, keepdims=True), out)
    return out


def _dil_merge_fwd(os, lses, T, tm):
    G = len(DIL_GROUPS)
    W = DIL_OUT_WIDTH
    dils = [d for _, d in DIL_GROUPS]

    def body(*refs):
        o_refs, lse_refs = refs[:G], refs[G:2 * G]
        y_ref, w_refs, on_refs, scratch = refs[2 * G], refs[2 * G + 1:3 * G + 1], refs[3 * G + 1:4 * G + 1], refs[-1]
        o = [jnp.concatenate(_to_natural(r, scratch, d, tm), axis=1) for r, d in zip(o_refs, dils)]
        ls = [jnp.concatenate(_to_natural(r, scratch, d, tm), axis=1) for r, d in zip(lse_refs, dils)]
        m = functools.reduce(jnp.maximum, ls)
        es = [jnp.exp(l - m) for l in ls]
        tot = functools.reduce(jnp.add, es)
        ws = [e / tot for e in es]
        y_ref[...] = functools.reduce(jnp.add, [w * t for w, t in zip(ws, o)]).astype(y_ref.dtype)
        for g in range(G):
            w_refs[g][...] = ws[g]
            on_refs[g][...] = o[g]

    nat = pl.BlockSpec((tm, W), lambda i: (i, 0))
    res = pl.pallas_call(
        body, name="dil_merge_fwd", grid=(T // tm,),
        in_specs=[_residue_tile(d, tm) for d in dils] * 2,
        out_specs=[nat] * (2 * G + 1),
        out_shape=[jax.ShapeDtypeStruct((T, W), BF)] + [jax.ShapeDtypeStruct((T, W), F32)] * (2 * G),
        scratch_shapes=[pltpu.VMEM((tm, 128), F32)],
        compiler_params=_cparams(("parallel",)),
    )(*os, *lses)
    return res[0], res[1:G + 1], res[G + 1:]


def _dil_merge_bwd(dy, os, ws, tm, after=()):
    G = len(DIL_GROUPS)
    T, W = dy.shape
    dils = [d for _, d in DIL_GROUPS]
    n_after = len(after)

    def body(*refs):
        dyt = refs[0][...]
        o, w = [r[...] for r in refs[1:G + 1]], [r[...] for r in refs[G + 1:2 * G + 1]]
        refs = refs[2 * G + 1 + n_after:]
        do_refs, dlse_refs, scratch = refs[:G], refs[G:2 * G], refs[-1]
        dws = [_head_sums(dyt * t) for t in o]
        mean = functools.reduce(jnp.add, [a * b for a, b in zip(w, dws)])
        for g, d in enumerate(dils):
            do, dlse = w[g] * dyt, w[g] * (dws[g] - mean)
            for pair in range(DIL_PAIRS):
                cols = slice(pair * 128, (pair + 1) * 128)
                _from_natural(do[:, cols], scratch, do_refs[g], pair, d, tm)
                _from_natural(dlse[:, cols], scratch, dlse_refs[g], pair, d, tm)

    nat = pl.BlockSpec((tm, W), lambda i: (i, 0))
    res = pl.pallas_call(
        body, name="dil_merge_bwd", grid=(T // tm,),
        in_specs=[nat] * (2 * G + 1) + [ANY] * n_after,
        out_specs=[_residue_tile(d, tm) for d in dils] * 2,
        out_shape=[_residue_shape(d, T, BF) for d in dils] + [_residue_shape(d, T, F32) for d in dils],
        scratch_shapes=[pltpu.VMEM((tm, 128), F32)],
        compiler_params=_cparams(("parallel",)),
    )(dy, *os, *ws, *after)
    return res[:G], res[G:]


def _qkv_prep(z, cos2, sin_signed, tm):
    T = z.shape[0]
    G = len(DIL_GROUPS)
    dils = [d for _, d in DIL_GROUPS]
    n_dil_blocks = 3 * DIL_WIDTH // 128

    def body(*refs):
        blocks = refs[:n_dil_blocks]
        cos_ref, sin_ref = refs[n_dil_blocks], refs[1 + n_dil_blocks]
        outs = refs[2 + n_dil_blocks:]
        for part in range(3):
            for g, d in enumerate(dils):
                out = outs[g * 3 + part]
                for pair in range(DIL_PAIRS):
                    blk = blocks[part * (DIL_WIDTH // 128) + g * DIL_PAIRS + pair]
                    for r in range(d):
                        rows = pl.ds(r, tm // d, stride=d) if d > 1 else slice(None)
                        x = blk[rows, :]
                        if part < 2:
                            x = _rope(x, cos_ref[rows, :], sin_ref[rows, :])
                        if part == 0:
                            x = x * Q_SCALE
                        out[pair, r] = x.astype(out.dtype)

    lane_block = [pl.BlockSpec((tm, 128), functools.partial(lambda b, i: (i, b), b)) for b in range(n_dil_blocks)]
    tab = pl.BlockSpec((tm, 128), lambda i: (i, 0))
    res = pl.pallas_call(
        body, name="qkv_prep", grid=(T // tm,),
        in_specs=lane_block + [tab, tab],
        out_specs=[_residue_tile(d, tm) for d in dils for _ in range(3)],
        out_shape=[_residue_shape(d, T, BF) for d in dils for _ in range(3)],
        compiler_params=_cparams(("parallel",)),
    )(*[z] * n_dil_blocks, cos2, sin_signed)
    return [res[3 * g:3 + 3 * g] for g in range(G)]


def _qkv_unprep(d_na, d_dil, cos2, sin_signed, tm, after=()):
    T = d_na[0].shape[0]
    G = len(DIL_GROUPS)
    dils = [d for _, d in DIL_GROUPS]
    n_after = len(after)

    def body(*refs):
        dq, dk, dv = (r[...] for r in refs[:3])
        res_refs = refs[3:3 + 3 * G]
        cs, sn = refs[3 + 3 * G][...], refs[4 + 3 * G][...]
        out, scratch = refs[5 + 3 * G + n_after], refs[-1]
        cols = [dq, dk, dv]
        for part in range(3):
            for g, d in enumerate(dils):
                for x in _to_natural(res_refs[g * 3 + part], scratch, d, tm):
                    if part < 2:
                        x = _rope(x, cs, -sn)
                    cols.append((x * Q_SCALE if part == 0 else x).astype(out.dtype))
        out[...] = jnp.concatenate(cols, axis=1)

    wide = pl.BlockSpec((tm, NA_WIDTH), lambda i: (i, 0))
    tab = pl.BlockSpec((tm, 128), lambda i: (i, 0))
    return pl.pallas_call(
        body, name="qkv_unprep", grid=(T // tm,),
        in_specs=[wide] * 3 + [_residue_tile(d, tm) for d in dils for _ in range(3)] + [tab, tab] + [ANY] * n_after,
        out_specs=pl.BlockSpec((tm, QKV_WIDTH), lambda i: (i, 0)),
        out_shape=jax.ShapeDtypeStruct((T, QKV_WIDTH), BF),
        scratch_shapes=[pltpu.VMEM((tm, 128), F32)],
        compiler_params=_cparams(("parallel",)),
    )(*d_na, *[t for g in range(G) for t in d_dil[g]], cos2, sin_signed, *after)


def _rope_tables(positions):
    half = HEAD_DIM // 2
    inv_freq = ROPE_THETA ** (-jnp.arange(half, dtype=F32) / half)
    ang = positions.astype(F32)[:, None] * inv_freq
    cos, sin = jnp.cos(ang), jnp.sin(ang)
    return jnp.tile(jnp.concatenate([cos, cos], axis=1), (1, 2)), jnp.tile(jnp.concatenate([-sin, sin], axis=1), (1, 2))


def _pack_rows(t):
    return t.reshape(-1, PACK_W)


def _me():
    return lax.axis_index("x"), lax.axis_index("y"), lax.axis_index("c")


def _other_chips(x, y):
    return [(1 - x, y), (x, 1 - y), (1 - x, 1 - y)]


def _pair_sum(g, got, tm):
    S, R, W = g.shape
    half = R // 2
    nb = half // tm

    def body(pos_ref, g_ref, got_ref, own_ref, ob_ref):
        tot = g_ref[...] + got_ref[...]
        ob_ref[...] = tot.astype(ob_ref.dtype)

        @pl.when(pl.program_id(1) == pos_ref[1])
        def _():
            own_ref[...] = tot

    tile = pl.BlockSpec((None, tm, W), lambda i, s, pos_ref: (s, i, 0))
    c, chip = lax.axis_index("c"), 2 * lax.axis_index("x") + lax.axis_index("y")
    return pl.pallas_call(
        body, name="pair_sum",
        grid_spec=pltpu.PrefetchScalarGridSpec(
            num_scalar_prefetch=1, grid=(nb, S),
            in_specs=[pl.BlockSpec((None, tm, W), lambda i, s, pos_ref: (s, pos_ref[0] * nb + i, 0)), tile],
            out_specs=[pl.BlockSpec((tm, W), lambda i, s, pos_ref: (i, 0)), tile]),
        out_shape=[jax.ShapeDtypeStruct((half, W), F32), jax.ShapeDtypeStruct((S, half, W), BF)],
        compiler_params=_cparams(("parallel", "arbitrary")),
    )(jnp.stack([c, chip]).astype(jnp.int32), g, got)


def _chip_sum(own, others, tm):
    n, h, W = others.shape
    nb = h // tm

    def body(c_ref, own_ref, p_ref, o_ref):
        o_ref[...] = ((own_ref[...] + p_ref[0].astype(F32)) + p_ref[1].astype(F32)) + p_ref[2].astype(F32)

    return pl.pallas_call(
        body, name="chip_sum",
        grid_spec=pltpu.PrefetchScalarGridSpec(
            num_scalar_prefetch=1, grid=(nb,),
            in_specs=[pl.BlockSpec((tm, W), lambda i, c_ref: (i, 0)), pl.BlockSpec((n, tm, W), lambda i, c_ref: (0, i, 0))],
            out_specs=pl.BlockSpec((tm, W), lambda i, c_ref: (c_ref[0] * nb + i, 0))),
        out_shape=jax.ShapeDtypeStruct((2 * h, W), F32),
        compiler_params=_cparams(("parallel",)),
    )(lax.axis_index("c").reshape(1).astype(jnp.int32), own, others)


def _join_halves(shard, after=()):
    h = shard.shape[0] // 2

    def body(in_ref, *rest):
        out_ref, send_sem, recv_sem = rest[len(after):]
        x, y, c = _me()
        cp = pltpu.make_async_remote_copy(
            src_ref=in_ref.at[pl.ds(c * h, h), :], dst_ref=out_ref.at[pl.ds(c * h, h), :],
            send_sem=send_sem, recv_sem=recv_sem, device_id=(x, y, 1 - c), device_id_type=MESH)
        cp.start()
        pltpu.make_async_remote_copy(
            src_ref=in_ref.at[pl.ds(c * h, h), :], dst_ref=out_ref.at[pl.ds((1 - c) * h, h), :],
            send_sem=send_sem, recv_sem=recv_sem, device_id=(x, y, 1 - c), device_id_type=MESH).wait_recv()
        cp.wait_send()

    return pl.pallas_call(
        body, name="join_halves", in_specs=[ANY] * (1 + len(after)), out_specs=ANY,
        out_shape=jax.ShapeDtypeStruct(shard.shape, shard.dtype), input_output_aliases={0: 0},
        scratch_shapes=[pltpu.SemaphoreType.DMA, pltpu.SemaphoreType.DMA],
    )(shard, *after)


def _allreduce_copies(src_ref, land_ref):
    x, y, c = _me()
    peers = [((x + fx) % 2, (y + fy) % 2, (c + fc) % 2) for fx in range(2) for fy in range(2) for fc in range(2)][1:]
    return [(src_ref, land_ref.at[4 * x + 2 * y + c], peer) for peer in peers]


def _allreduce_start(s, after):
    return _split_start("allreduce_small_start", s, (N_DEV, *s.shape), s.dtype, N_DEV - 1, _allreduce_copies, after)


def _allreduce_finish(flight, after):
    own, landed = _split_wait("allreduce_small_wait", flight, after, N_DEV - 1, _allreduce_copies)
    me = 4 * lax.axis_index("x") + 2 * lax.axis_index("y") + lax.axis_index("c")
    parts = lax.dynamic_update_slice(landed, own[None], (me, 0, 0))

    def body(p_ref, o_ref):
        total = p_ref[0]
        for d in range(1, N_DEV):
            total = total + p_ref[d]
        o_ref[...] = total

    return pl.pallas_call(
        body, name="allreduce_small_sum",
        in_specs=[pl.BlockSpec(memory_space=pltpu.VMEM)], out_specs=pl.BlockSpec(memory_space=pltpu.VMEM),
        out_shape=jax.ShapeDtypeStruct(own.shape, F32),
    )(parts)


HBM_SPEC = pl.BlockSpec(memory_space=pltpu.HBM)
SEM_SPEC = pl.BlockSpec(memory_space=pltpu.SEMAPHORE)
DATAFLOW = pltpu.SideEffectType.DATAFLOW_SIDE_EFFECTING


class _InFlight(NamedTuple):
    sems: tuple
    src: jax.Array
    land: jax.Array
    token: jax.Array


def _split_start(name, src, land_shape, land_dtype, n, copies, after=()):
    n_after = len(after)

    def body(src_ref, land_ref, *rest):
        rest = rest[n_after:]
        sems, token = rest[:2 * n], rest[-1]
        for k, (s, d, peer) in enumerate(copies(src_ref, land_ref)):
            pltpu.make_async_remote_copy(src_ref=s, dst_ref=d, send_sem=sems[k], recv_sem=sems[n + k],
                                         device_id=peer, device_id_type=MESH).start()
        token[...] = jnp.zeros_like(token)

    outs = pl.pallas_call(
        body, name=name,
        out_shape=(*[pltpu.SemaphoreType.DMA(())] * (2 * n), pltpu.HBM(src.shape, src.dtype), pltpu.HBM(land_shape, land_dtype),
                   jax.ShapeDtypeStruct((8, 128), F32)),
        in_specs=(HBM_SPEC, HBM_SPEC, *[ANY] * n_after),
        out_specs=(*[SEM_SPEC] * (2 * n), HBM_SPEC, HBM_SPEC, pl.BlockSpec(memory_space=pltpu.VMEM)),
        input_output_aliases={0: 2 * n, 1: 2 * n + 1},
        compiler_params=pltpu.CompilerParams(has_side_effects=DATAFLOW),
    )(pltpu.with_memory_space_constraint(src, pltpu.HBM), pltpu.with_memory_space_constraint(lax.empty(land_shape, land_dtype), pltpu.HBM),
      *after)
    return _InFlight(tuple(outs[:2 * n]), outs[2 * n], outs[2 * n + 1], outs[2 * n + 2])


def _split_wait(name, flight, after, n, copies):
    after = after if isinstance(after, tuple) else (after,)

    def body(src_ref, land_ref, *rest):
        sems = rest[:2 * n]
        for k, (s, d, peer) in enumerate(copies(src_ref, land_ref)):
            cp = pltpu.make_async_remote_copy(src_ref=s, dst_ref=d, send_sem=sems[k], recv_sem=sems[n + k],
                                              device_id=peer, device_id_type=MESH)
            cp.wait_send()
            cp.wait_recv()

    return pl.pallas_call(
        body, name=name,
        out_shape=(pltpu.HBM(flight.src.shape, flight.src.dtype), pltpu.HBM(flight.land.shape, flight.land.dtype)),
        in_specs=(HBM_SPEC, HBM_SPEC, *[SEM_SPEC] * (2 * n), *[ANY] * len(after)),
        out_specs=(HBM_SPEC, HBM_SPEC), input_output_aliases={0: 0, 1: 1},
        compiler_params=pltpu.CompilerParams(has_side_effects=DATAFLOW),
    )(flight.src, flight.land, *flight.sems, *after)


def _gather_copies(src_ref, land_ref):
    x, y, c = _me()
    return [(src_ref, land_ref.at[2 * x + y], (*chip, c)) for chip in _other_chips(x, y)]


def _gather_start(packed, tag, after=()):
    return _split_start(f"gather_start_{tag}", packed, (N_CHIPS, *packed.shape), packed.dtype, 3, _gather_copies, after)


def _gather_wait(flight, after, tag):
    src, others = _split_wait(f"gather_wait_{tag}", flight, after, 3, _gather_copies)
    return lax.dynamic_update_slice(others, src[None], (2 * lax.axis_index("x") + lax.axis_index("y"), 0, 0))


def _across_copies(src_ref, land_ref):
    x, y, c = _me()
    half = src_ref.shape[0] // 2
    rows = pl.ds(c * half, half)
    return [(src_ref.at[rows, :], land_ref.at[2 * x + y, rows, :], (*chip, c)) for chip in _other_chips(x, y)]


def _to_sibling_copies(all_ref, unused_ref):
    x, y, c = _me()
    half = all_ref.shape[1] // 2
    places = [all_ref.at[2 * chip[0] + chip[1], pl.ds(c * half, half), :] for chip in _other_chips(x, y)]
    return [(place, place, (x, y, 1 - c)) for place in places]


def _gather_halves_start(shard, tag, after=()):
    return _split_start(f"gather_{tag}_across_start", shard, (N_CHIPS, *shard.shape), shard.dtype, 3, _across_copies, after)


def _gather_halves_relay(flight, after, tag):
    shard, landed = _split_wait(f"gather_{tag}_across_wait", flight, after, 3, _across_copies)
    return shard, _split_start(f"gather_{tag}_sibling_start", landed, (8, 128), landed.dtype, 3, _to_sibling_copies)


def _gather_halves_finish(shard, relay, after, tag):
    others = _split_wait(f"gather_{tag}_sibling_wait", relay, after, 3, _to_sibling_copies)[0]
    return lax.dynamic_update_slice(others, shard[None], (2 * lax.axis_index("x") + lax.axis_index("y"), 0, 0))


def _assemble_w_in(shards, tm):
    S, R, C = shards.shape
    n_gates = 2 * D_MODEL

    def body(s_ref, w_ref, g_ref):
        full = jnp.concatenate([s_ref[s] for s in range(S)], axis=1)
        w_ref[...] = full
        g_ref[...] = full[:, S * C - n_gates:]

    return pl.pallas_call(
        body, name="assemble_w_in", grid=(R // tm,),
        in_specs=[pl.BlockSpec((S, tm, C), lambda i: (0, i, 0))],
        out_specs=[pl.BlockSpec((tm, S * C), lambda i: (i, 0)), pl.BlockSpec((tm, n_gates), lambda i: (i, 0))],
        out_shape=[jax.ShapeDtypeStruct((R, S * C), shards.dtype), jax.ShapeDtypeStruct((R, n_gates), shards.dtype)],
        compiler_params=_cparams(("parallel",)),
    )(shards)


def _swap_copies(src_ref, land_ref):
    x, y, c = _me()
    half = land_ref.shape[1]
    return [(src_ref.at[:, pl.ds((1 - c) * half, half), :], land_ref, (x, y, 1 - c))]


def _swap_start(g, tag):
    S, R, W = g.shape
    return _split_start(f"swap_halves_start_{tag}", g, (S, R // 2, W), g.dtype, 1, _swap_copies)


def _swap_wait(flight, after, tag):
    return _split_wait(f"swap_halves_wait_{tag}", flight, after, 1, _swap_copies)


def _scatter_copies(src_ref, land_ref):
    x, y, c = _me()
    return [(src_ref.at[2 * chip[0] + chip[1]], land_ref.at[j], (*chip, c)) for j, chip in enumerate(_other_chips(x, y))]


def _scatter_start(part, tag):
    S, h, W = part.shape
    return _split_start(f"scatter_chips_start_{tag}", part, (S - 1, h, W), part.dtype, 3, _scatter_copies)


def _scatter_wait(flight, after, tag):
    return _split_wait(f"scatter_chips_wait_{tag}", flight, after, 3, _scatter_copies)[1]


def _join_copies(shard_ref, unused_ref):
    x, y, c = _me()
    h = shard_ref.shape[0] // 2
    rows = shard_ref.at[pl.ds(c * h, h), :]
    return [(rows, rows, (x, y, 1 - c))]


def _join_start(shard):
    return _split_start("join_halves_start", shard, (8, 128), shard.dtype, 1, _join_copies)


def _join_wait(flight, after):
    return _split_wait("join_halves_wait", flight, after, 1, _join_copies)[0]


def _adamw(name, g, g_row0, w, m, v):
    _, R, C = w.shape
    tm = next(cand for cand in (368, 256, 128, 64, 32, 16, 8) if R % cand == 0)
    assert g_row0 % tm == 0 and g.shape[1] == C

    def body(g_ref, w_ref, m_ref, v_ref, go_ref, d_ref, mo_ref, vo_ref):
        gt = g_ref[...]
        mt = ADAM_B1 * m_ref[...] + (1.0 - ADAM_B1) * gt
        vt = ADAM_B2 * v_ref[...] + (1.0 - ADAM_B2) * jnp.square(gt)
        m_hat = mt / (1.0 - ADAM_B1 ** ADAM_STEP)
        v_hat = vt / (1.0 - ADAM_B2 ** ADAM_STEP)
        go_ref[...] = gt
        d_ref[...] = -ADAM_LR * (m_hat / (jnp.sqrt(v_hat) + ADAM_EPS) + ADAM_WD * w_ref[...])
        mo_ref[...] = mt
        vo_ref[...] = vt

    state = pl.BlockSpec((None, tm, C), lambda i: (0, i, 0))
    return pl.pallas_call(
        body, name=name, grid=(R // tm,),
        in_specs=[pl.BlockSpec((tm, C), lambda i: (g_row0 // tm + i, 0)), state, state, state],
        out_specs=[state] * 4, out_shape=[jax.ShapeDtypeStruct((1, R, C), F32)] * 4,
        compiler_params=_cparams(("parallel",)),
    )(g, w, m, v)


def _unpack_weights(gathered, names):
    S = gathered.shape[0]
    shard_shapes = {"w_in": (D_MODEL, (QKV_WIDTH + 2 * D_MODEL) // S), "w_branch_na": (NA_WIDTH, D_MODEL // S),
                    "w_branch_dil": (DIL_OUT_WIDTH, D_MODEL // S), "w_out": (D_MODEL // S, D_MODEL),
                    "w_up": (D_MODEL, D_FF // S), "w_down": (D_FF // S, D_MODEL),
                    "w_ple_gate": (D_MODEL // S, D_MODEL), "w_ple_proj": (PLE_DIM, D_MODEL // S)}
    col_sharded = {"w_in", "w_branch_na", "w_branch_dil", "w_up", "w_ple_proj"}
    out, r0 = {}, 0
    for name in names:
        rows, cols = shard_shapes[name]
        n = rows * cols // PACK_W
        t = gathered[:, r0:r0 + n, :].reshape(S, rows, cols)
        r0 += n
        out[name] = t.transpose(1, 0, 2).reshape(rows, S * cols) if name in col_sharded else t.reshape(S * rows, cols)
    return out


def kernel(x, p, positions, g_mix, w_in, rpb, w_branch_na, w_branch_dil, w_out, g_mlp, w_up, w_down, g_ple, w_ple_gate, w_ple_proj, g_final, loss_target, m_g_mix, m_w_in, m_rpb, m_w_branch_na, m_w_branch_dil, m_w_out, m_g_mlp, m_w_up, m_w_down, m_g_ple, m_w_ple_gate, m_w_ple_proj, m_g_final, v_g_mix, v_w_in, v_rpb, v_w_branch_na, v_w_branch_dil, v_w_out, v_g_mlp, v_w_up, v_w_down, v_g_ple, v_w_ple_gate, v_w_ple_proj, v_g_final):
    shards = {"w_in": w_in[0], "w_branch_na": w_branch_na[0], "w_branch_dil": w_branch_dil[0], "w_out": w_out[0],
              "w_up": w_up[0], "w_down": w_down[0], "w_ple_gate": w_ple_gate[0], "w_ple_proj": w_ple_proj[0]}
    params = {"w_in": w_in, "w_branch_na": w_branch_na, "w_branch_dil": w_branch_dil, "w_out": w_out, "w_up": w_up,
              "w_down": w_down, "w_ple_gate": w_ple_gate, "w_ple_proj": w_ple_proj,
              "m_w_in": m_w_in, "m_w_branch_na": m_w_branch_na, "m_w_branch_dil": m_w_branch_dil, "m_w_out": m_w_out,
              "m_w_up": m_w_up, "m_w_down": m_w_down, "m_w_ple_gate": m_w_ple_gate, "m_w_ple_proj": m_w_ple_proj,
              "v_w_in": v_w_in, "v_w_branch_na": v_w_branch_na, "v_w_branch_dil": v_w_branch_dil, "v_w_out": v_w_out,
              "v_w_up": v_w_up, "v_w_down": v_w_down, "v_w_ple_gate": v_w_ple_gate, "v_w_ple_proj": v_w_ple_proj}

    xs, ps, tgt = x[0], p[0, 0], loss_target[0]
    T = xs.shape[0]
    TM = 512
    gm, gl, gp, gf = g_mix, g_mlp, g_ple, g_final.reshape(1, D_MODEL)

    across = _gather_halves_start(shards["w_in"].astype(BF), "in")
    a = _rowwise("norm_mix", lambda h, g: h * _rms(h) * g, T, TM, [_row(xs, TM), _full(gm)], [(D_MODEL, BF)],
                 after=(across.token,))
    cos2, sin_signed = _rope_tables(positions[0])
    tab = _na_bias_table(rpb[0])
    packed_mix = jnp.concatenate([_pack_rows(shards[n].astype(BF)) for n in GATHER_MIX], axis=0)
    packed_mlp = jnp.concatenate([_pack_rows(shards[n].astype(BF)) for n in GATHER_MLP], axis=0)
    w_in_shard, w_in_relay = _gather_halves_relay(across, (a, tab, cos2, sin_signed, packed_mix, packed_mlp), "in")
    w_in_all = _gather_halves_finish(w_in_shard, w_in_relay, w_in_relay.token, "in")
    w_in_full, w_gates = _assemble_w_in(w_in_all, 256)
    W = {"w_in": w_in_full}
    mix_flight = _gather_start(packed_mix, "mix", after=(w_in_all,))
    mlp_across = _gather_halves_start(packed_mlp, "mlp", after=(mix_flight.token,))

    n3 = 3 * NA_WIDTH
    qkv = _mm("in_na", a, W["w_in"], "nn", 1024, 768, 1024, [BF], after=(mlp_across.token,),
              b_view=(n3, (D_MODEL, 768), lambda j, k: (k, j)))
    z_dil = _mm("in_dil", a, W["w_in"], "nn", 1024, 768, 1024, [F32], after=(mlp_across.token,),
                b_view=(3 * DIL_WIDTH, (D_MODEL, 768), lambda j, k: (k, n3 // 768 + j)))
    z_gates = _mm("in_gates", a, w_gates, "nn", 1024,1024, 1024, [BF], after=(mlp_across.token,))

    dil_ops = _qkv_prep(z_dil, cos2, sin_signed, TM)
    y_na = _na_fwd(qkv, tab)
    band = [_band_fwd(*dil_ops[g], g) for g in range(len(DIL_GROUPS))]
    y_dil, w_grp, o_nat = _dil_merge_fwd([b[0] for b in band], [b[1] for b in band], T, TM)

    W.update(_unpack_weights(_gather_wait(mix_flight, y_dil, "mix"), GATHER_MIX))
    mlp_shard, mlp_relay = _gather_halves_relay(mlp_across, y_dil, "mlp")
    u_na = _mm("branch_na", y_na, W["w_branch_na"], "nn", 1024,1024, 512, [BF], after=(mlp_relay.token,))
    def gate_mix(acc, gn, gd, un):
        ud = acc.astype(BF)
        return ud, _sigmoid(gn.astype(F32)) * un.astype(F32) + _sigmoid(gd.astype(F32)) * ud.astype(F32)

    u_dil, mixed = _mm("branch_dil", y_dil, W["w_branch_dil"], "nn", 512, 1024, 256, [BF, BF], epilogue=gate_mix,
                       extras=((z_gates, 0), (z_gates, 1), u_na))

    def add_norm(d, h, g):
        h = h + d
        return h, h * _rms(h) * g

    h1, cn = _mm("out_proj", mixed, W["w_out"], "nn", 512, 1024, 1024, [F32, BF], epilogue=add_norm, extras=(xs,), consts=(gl,))
    mlp_all = _gather_halves_finish(mlp_shard, mlp_relay, cn, "mlp")
    W.update({n: t for n, t in _unpack_weights(mlp_all, GATHER_MLP).items() if n.startswith("w_ple")})
    chip_block = (None, D_MODEL, PACK_W)
    up, act = _mm("mlp_up", cn, mlp_all, "nn", 1024,1024, 1024, [BF, BF],
                  epilogue=lambda acc: (acc, jnp.square(jnp.maximum(acc, 0.0))), b_view=(D_FF, chip_block, lambda j, k: (j, 0, 0)))
    h2, en = _mm("mlp_down", act, mlp_all, "nn", 1024, 1024, 2048, [F32, BF], epilogue=add_norm, extras=(h1,), consts=(gp,),
                 b_view=(D_MODEL, (2, D_MODEL, PACK_W), lambda j, k: (k, 1, 0)))
    pp = _mm("ple_proj", ps, W["w_ple_proj"], "nn", 1024,1024, 256, [F32])

    def head(gtt, h2t, ppt, tg, g):
        sg = _sigmoid(gtt)
        h3 = h2t + sg * ppt
        yo = h3 * _rms(h3) * g
        diff = yo - tg
        loss = 0.5 * jnp.sum(jnp.mean(jnp.square(diff), axis=-1, keepdims=True), axis=0, keepdims=True)
        dh3, dg = _rms_bwd(diff * (1.0 / D_MODEL), h3, g)
        return dh3, dh3 * ppt * sg * (1.0 - sg), dh3 * sg, jnp.broadcast_to(loss, (1, 128)), dg

    dh3, d_gt, d_pp, loss_part, dg_final = _mm(
        "ple_gate_loss_head", en, W["w_ple_gate"], "nn", 512, 1024, 1024, [F32, BF, BF], epilogue=head,
        extras=(h2, pp, tgt), consts=(gf,), sums=[128, D_MODEL])

    early_shapes = {n: shards[n].shape for n in REDUCE_EARLY}
    early_rows = sum(r * c for r, c in early_shapes.values()) // PACK_W
    shard_rows = D_MODEL // N_CHIPS
    early_buf = _mm("g_ple_gate", en, d_gt, "tn", 512, 1024, T, [F32],
                    into=(jax.ShapeDtypeStruct((N_CHIPS, early_rows, PACK_W), F32), (N_CHIPS // 2, shard_rows, PACK_W),
                          lambda i, j: (i, 2 * D_MODEL // shard_rows, 0)))
    g_ple_proj = _mm("g_ple_proj", ps, d_pp, "tn", 256, 1024, 1024,[F32])

    def add_norm_bwd(dn, dh_out, h, g):
        dh, dg = _rms_bwd(dn, h, g)
        dh = dh_out + dh
        return dh, dh, dg

    dh2, dh2_b, dg_ple = _mm("d_ple_gate", d_gt, W["w_ple_gate"], "nt", 512, 1024, 1024, [F32, BF],
                             epilogue=add_norm_bwd, extras=(dh3, h2), consts=(gp,), sums=[D_MODEL])
    d_up = _mm("d_mlp_down", dh2_b, mlp_all, "nt", 1024,1024, 1024, [BF], b_view=(D_FF, chip_block, lambda j, k: (j, 1, 0)),
               epilogue=lambda acc, u: (acc * (2.0 * jnp.maximum(u.astype(F32), 0.0)),), extras=(up,))
    early_buf = _mm("g_mlp_down", act, dh2_b, "tn", 512, 1024, T, [F32],
                    into=(early_buf, (None, 512, PACK_W), lambda i, j: (i // 2, 2 + i % 2, 0)))
    early_buf = _mm("g_mlp_up", cn, d_up, "tn", 1024, 512, T, [F32],
                    into=(early_buf, (None, D_MODEL, 512), lambda i, j: (j // 2, 0, j % 2)))
    dh1, dh1_b, dg_mlp = _mm("d_mlp_up", d_up, mlp_all, "nt", 1024, 1024, 1024, [F32, BF], epilogue=add_norm_bwd,
                             b_view=(D_MODEL, chip_block, lambda j, k: (k, 0, 0)),
                             extras=(dh2, h1), consts=(gl,), sums=[D_MODEL])
    early_buf = _mm("g_out_proj", mixed, dh1_b, "tn", 512, 1024, T, [F32],
                    into=(early_buf, (N_CHIPS // 2, shard_rows, PACK_W), lambda i, j: (i, 2 * D_MODEL // shard_rows + 1, 0)))

    def gate_bwd(dm, gn, gd, un, ud):
        gn, gd, un, ud = (t.astype(F32) for t in (gn, gd, un, ud))
        sn, sd = _sigmoid(gn), _sigmoid(gd)
        return jnp.concatenate([dm * un * sn * (1.0 - sn), dm * ud * sd * (1.0 - sd)], axis=1), dm * sn, dm * sd

    dz_gates, d_u_na, d_u_dil = _mm("d_out_proj", dh1_b, W["w_out"], "nt", 512, 1024, 1024, [(BF, 2 * D_MODEL), BF, BF],
                                    epilogue=gate_bwd, extras=((z_gates, 0), (z_gates, 1), u_na, u_dil))
    g_branch_na = _mm("g_branch_na", y_na, d_u_na, "tn", 1024, 1024, 1024,[F32])
    g_branch_dil = _mm("g_branch_dil", y_dil, d_u_dil, "tn", 256, 1024, 1024,[F32])
    small_rows = [jnp.concatenate([_pack_rows(g[:, s * shard_rows:(s + 1) * shard_rows]) for g in (g_ple_proj, g_branch_na, g_branch_dil)],
                                  axis=0) for s in range(N_CHIPS)]
    early_buf = lax.dynamic_update_slice(early_buf, jnp.stack(small_rows), (0, 2 * D_MODEL + 2 * shard_rows, 0))
    early_tm = early_rows // 4
    swap_flight = _swap_start(early_buf, "early")
    d_y_na = _mm("d_branch_na", d_u_na, W["w_branch_na"], "nt", 1024,512, 1024, [BF], after=(swap_flight.token,))
    d_y_dil = _mm("d_branch_dil", d_u_dil, W["w_branch_dil"], "nt", 1024,256, 1024, [F32])

    dqa, dka, dva, dtab = _na_bwd(qkv, tab, d_y_na)
    early_g, early_got = _swap_wait(swap_flight, dqa, "early")
    early_pair, early_pair_b = _pair_sum(early_g, early_got, early_tm)
    scatter_flight = _scatter_start(early_pair_b, "early")

    do_res, dlse_res = _dil_merge_bwd(d_y_dil, o_nat, w_grp, TM, after=(scatter_flight.token,))
    d_dil = [_band_bwd(*dil_ops[g], do_res[g], dlse_res[g], g) for g in range(len(DIL_GROUPS))]

    dz_qkv = _qkv_unprep((dqa, dka, dva), d_dil, cos2, sin_signed, TM)
    in_cols = shards["w_in"].shape[1]
    qkv_rows, gate_tm = dz_qkv.shape[1], 256
    assert qkv_rows % gate_tm == 0
    g_in = _mm("g_in_qkv", dz_qkv, a, "tn", 640, 1024, T, [F32],
               into=(jax.ShapeDtypeStruct((N_CHIPS * in_cols, D_MODEL), F32), (640, D_MODEL), lambda i, j: (i, 0)))
    g_in = _mm("g_in_gates", dz_gates, a, "tn", gate_tm, 1024, T, [F32],
               into=(g_in, (gate_tm, D_MODEL), lambda i, j: (qkv_rows // gate_tm + i, 0)))
    early_mine = _chip_sum(early_pair, _scatter_wait(scatter_flight, (g_in,), "early"), early_tm)
    join_flight = _join_start(early_mine)

    late_tm = in_cols // 4
    late_swap = _swap_start(g_in.reshape(N_CHIPS, in_cols, D_MODEL), "late")
    d_a = _mm("d_in_qkv", dz_qkv, W["w_in"], "nt", 512, 1024, qkv_rows, [F32], after=(late_swap.token, join_flight.token),
              b_view=(D_MODEL, (D_MODEL, qkv_rows), lambda j, k: (j, k)))
    late_g, late_got = _swap_wait(late_swap, d_a, "late")
    late_pair, late_pair_b = _pair_sum(late_g, late_got, late_tm)
    late_scatter = _scatter_start(late_pair_b, "late")
    d_rpb = _na_rpb_grad(dtab, after=(late_scatter.token,))[:, :2 * NA_WIN_ROWS - 1, :2 * NA_WIN_COLS - 1]
    def first_bwd(dn_gates, dn_qkv, dh_out, h, g):
        dh, dg = _rms_bwd(dn_gates + dn_qkv, h, g)
        return dh_out + dh, dg

    grad_x, dg_mix = _mm("d_in_gates", dz_gates, w_gates, "nt", 512, 1024, 2048, [F32], epilogue=first_bwd,
                         extras=(d_a, dh1, xs), consts=(gm,), sums=[D_MODEL], after=(late_scatter.token,))
    early_shard = _join_wait(join_flight, grad_x)

    n_rpb = rpb.size
    rpb_rows = 4
    small = jnp.concatenate([
        dg_mix, dg_mlp, dg_ple, dg_final,
        jnp.pad(d_rpb.reshape(-1), (0, rpb_rows * D_MODEL - n_rpb)).reshape(rpb_rows, D_MODEL),
        jnp.pad(loss_part, ((0, 0), (0, D_MODEL - loss_part.shape[1]))),
        jnp.zeros((SMALL_ROWS - 5 - rpb_rows, D_MODEL), F32)], axis=0)
    out = {"grad": {}, "delta": {}, "new_m": {}, "new_v": {}}

    def update(n, g, row0):
        res = _adamw("adamw_" + n, g, row0, params[n], params["m_" + n], params["v_" + n])
        for kind, t in zip(("grad", "delta", "new_m", "new_v"), res, strict=True):
            out[kind][n] = t

    row0 = 0
    for n in REDUCE_EARLY:
        rows, cols = early_shapes[n]
        n_rows = rows * cols // PACK_W
        if cols == PACK_W:
            update(n, early_shard, row0)
        else:
            update(n, early_shard[row0:row0 + n_rows].reshape(rows, cols), 0)
        row0 += n_rows
    late_others = _scatter_wait(late_scatter, (*[out["new_v"][n] for n in REDUCE_EARLY], d_rpb), "late")
    late_mine = _chip_sum(late_pair, late_others, late_tm)
    small_flight = _allreduce_start(small, after=(late_mine,))
    res = _adamw("adamw_w_in", _join_halves(late_mine, after=(small_flight.token,)), 0,
                 *[jnp.swapaxes(params[n], 1, 2) for n in ("w_in", "m_w_in", "v_w_in")])
    small = _allreduce_finish(small_flight, res[3])
    for kind, t in zip(("grad", "delta", "new_m", "new_v"), res, strict=True):
        out[kind]["w_in"] = jnp.swapaxes(t, 1, 2)
    loss = small[4 + rpb_rows, 0]

    def small_pack(a0, a1, a2, a3, r):
        return jnp.concatenate([a0.reshape(1, -1), a1.reshape(1, -1), a2.reshape(1, -1), a3.reshape(1, -1),
                                jnp.pad(r.reshape(-1), (0, rpb_rows * D_MODEL - n_rpb)).reshape(rpb_rows, D_MODEL)], axis=0)

    small_res = _adamw("adamw_small", small, 0, small_pack(g_mix, g_mlp, g_ple, g_final, rpb)[None],
                       small_pack(m_g_mix, m_g_mlp, m_g_ple, m_g_final, m_rpb)[None],
                       small_pack(v_g_mix, v_g_mlp, v_g_ple, v_g_final, v_rpb)[None])

    def small_unpack(t):
        return {"g_mix": t[0].reshape(g_mix.shape), "g_mlp": t[1].reshape(g_mlp.shape), "g_ple": t[2].reshape(g_ple.shape),
                "g_final": t[3].reshape(g_final.shape), "rpb": t[4:].reshape(-1)[:n_rpb].reshape(rpb.shape)}

    for kind, t in zip(("grad", "delta", "new_m", "new_v"), small_res, strict=True):
        out[kind].update(small_unpack(t[0]))

    order = ["g_mix", "w_in", "rpb", "w_branch_na", "w_branch_dil", "w_out", "g_mlp", "w_up", "w_down", "g_ple",
             "w_ple_gate", "w_ple_proj", "g_final"]
    return (loss, grad_x[None], *[out["grad"][n] for n in order], *[out["delta"][n] for n in order],
            *[out["new_m"][n] for n in order], *[out["new_v"][n] for n in order])
```

```python
import functools
from typing import NamedTuple

import jax
import jax.numpy as jnp
from jax import lax
from jax.experimental import pallas as pl
from jax.experimental.pallas import tpu as pltpu

BF = jnp.bfloat16
F32 = jnp.float32
MESH = pl.DeviceIdType.MESH
ANY = pl.BlockSpec(memory_space=pl.ANY)

V7X_VMEM_BYTES = 64 * 1024 * 1024
VMEM_LIMIT = V7X_VMEM_BYTES - 16 * 1024 * 1024

D_MODEL = 1024
HEAD_DIM = 64
GRID_W = 64
NA_HEADS = 8
NA_WIN_ROWS = 8
NA_WIN_COLS = 16
NA_WIDTH = NA_HEADS * HEAD_DIM
DIL_GROUPS = ((128, 1), (512, 4), (2048, 16))
DIL_HPG = 4
DIL_HEADS = DIL_HPG * len(DIL_GROUPS)
DIL_WIDTH = DIL_HEADS * HEAD_DIM
DIL_OUT_WIDTH = DIL_HPG * HEAD_DIM
DIL_RADIUS = 64
QKV_WIDTH = 3 * NA_WIDTH + 3 * DIL_WIDTH
D_FF = 4 * D_MODEL
PLE_DIM = 256
ROPE_THETA = 10000.0
RMS_EPS = 1e-6
NEG_INF = -1e30
Q_SCALE = HEAD_DIM ** -0.5

ADAM_LR = 0.001
ADAM_B1 = 0.9
ADAM_B2 = 0.999
ADAM_EPS = 1e-08
ADAM_WD = 0.01
ADAM_STEP = 10

N_CHIPS = 4
N_DEV = 8
PACK_W = 1024
GATHER_MIX = ("w_branch_na", "w_branch_dil", "w_out")
GATHER_MLP = ("w_up", "w_down", "w_ple_gate", "w_ple_proj")
REDUCE_EARLY = ("w_up", "w_down", "w_ple_gate", "w_out", "w_ple_proj", "w_branch_na", "w_branch_dil")
SMALL_ROWS = 16


def _cparams(sem=None):
    return pltpu.CompilerParams(dimension_semantics=sem, vmem_limit_bytes=VMEM_LIMIT)


def _mm(name, a, b, mode, tm, tn, tk, out_dtypes, epilogue=None, extras=(), consts=(), sums=(), after=(), into=None,
        b_view=None):
    if mode == "nn":
        (M, K), N = a.shape, b.shape[1]
    elif mode == "nt":
        (M, K), N = a.shape, b.shape[0]
    else:
        (K, M), N = a.shape, b.shape[1]
    if b_view is not None:
        N = b_view[0]
    tm, tn, tk = min(tm, M), min(tn, N), min(tk, K)
    assert M % tm == 0 and N % tn == 0 and K % tk == 0, (name, M, N, K, tm, tn, tk)
    if mode == "nn":
        a_spec = pl.BlockSpec((tm, tk), lambda i, j, k: (i, k))
        b_spec = pl.BlockSpec((tk, tn), lambda i, j, k: (k, j))
        dims = (((1,), (0,)), ((), ()))
    elif mode == "nt":
        a_spec = pl.BlockSpec((tm, tk), lambda i, j, k: (i, k))
        b_spec = pl.BlockSpec((tn, tk), lambda i, j, k: (j, k))
        dims = (((1,), (1,)), ((), ()))
    else:
        a_spec = pl.BlockSpec((tk, tm), lambda i, j, k: (k, i))
        b_spec = pl.BlockSpec((tk, tn), lambda i, j, k: (k, j))
        dims = (((0,), (0,)), ((), ()))
    if b_view is not None:
        b_spec = pl.BlockSpec(b_view[1], lambda i, j, k: b_view[2](j, k))
    nk = K // tk
    n_extra, n_const, n_out, n_sum = len(extras), len(consts), len(out_dtypes), len(sums)
    tile = pl.BlockSpec((tm, tn), lambda i, j, k: (i, j))
    assert not sums or tn == N, "row sums need whole rows in a tile"
    wide = [e for e in (*extras, *out_dtypes) if isinstance(e, tuple)]
    assert not wide or tn == N
    extra_specs = [pl.BlockSpec((tm, tn), functools.partial(lambda c, i, j, k: (i, c), e[1])) if isinstance(e, tuple) else tile
                   for e in extras]
    extras = [e[0] if isinstance(e, tuple) else e for e in extras]
    out_widths = [d[1] if isinstance(d, tuple) else N for d in out_dtypes]
    out_dtypes = [d[0] if isinstance(d, tuple) else d for d in out_dtypes]

    n_after = len(after)

    def body(a_ref, b_ref, *rest):
        extra_refs, rest = rest[:n_extra + n_const], rest[n_extra + n_const + n_after:]
        out_refs, sum_refs, acc = rest[:n_out], rest[n_out:n_out + n_sum], rest[-1]
        i, k = pl.program_id(0), pl.program_id(2)
        def product():
            if len(b_ref.shape) == 3:
                w = tk // b_ref.shape[0]
                parts = [lax.dot_general(a_ref[:, s * w:(s + 1) * w].astype(BF), b_ref[s].astype(BF), dims, preferred_element_type=F32)
                         for s in range(b_ref.shape[0])]
                return functools.reduce(lambda x, y: x + y, parts)
            return lax.dot_general(a_ref[...].astype(BF), b_ref[...].astype(BF), dims, preferred_element_type=F32)

        if nk > 1:
            @pl.when(k == 0)
            def _():
                acc[...] = jnp.zeros_like(acc)

            acc[...] += product()

        @pl.when(k == nk - 1)
        def _():
            total = product() if nk == 1 else acc[...]
            outs = (total,) if epilogue is None else epilogue(total, *[e[...] for e in extra_refs])
            for o_ref, val in zip(out_refs, outs[:n_out], strict=True):
                o_ref[...] = val.astype(o_ref.dtype).reshape(o_ref.shape)
            for s_ref, val in zip(sum_refs, outs[n_out:], strict=True):
                @pl.when(i == 0)
                def _():
                    s_ref[...] = val

                @pl.when(i != 0)
                def _():
                    s_ref[...] += val

    out_specs = ([tile if w == N else pl.BlockSpec((tm, w), lambda i, j, k: (i, 0)) for w in out_widths]
                 + [pl.BlockSpec((1, c), lambda i, j, k: (0, 0)) for c in sums])
    out_shape = ([jax.ShapeDtypeStruct((M, w), dt) for dt, w in zip(out_dtypes, out_widths, strict=True)]
                 + [jax.ShapeDtypeStruct((1, c), F32) for c in sums])
    operands, aliases = [a, b, *extras, *consts, *after], {}
    in_specs = ([a_spec, b_spec] + extra_specs
                + [pl.BlockSpec(c.shape, functools.partial(lambda nd, i, j, k: (0,) * nd, c.ndim)) for c in consts] + [ANY] * n_after)
    if into is not None:
        assert n_out == 1
        target, block, index = into
        out_specs = [pl.BlockSpec(block, lambda i, j, k: index(i, j))]
        out_shape = [jax.ShapeDtypeStruct(target.shape, target.dtype)]
        if not isinstance(target, jax.ShapeDtypeStruct):
            aliases = {len(operands): 0}
            operands.append(target)
            in_specs.append(ANY)
            n_after += 1

    outs = pl.pallas_call(
        body, name=name, grid=(M // tm, N // tn, nk),
        in_specs=in_specs, out_specs=out_specs, out_shape=out_shape,
        scratch_shapes=[pltpu.VMEM((tm, tn) if nk > 1 else (8, 128), F32)], input_output_aliases=aliases,
        compiler_params=_cparams(("arbitrary",) * 3 if sums else ("parallel", "parallel", "arbitrary")),
    )(*operands)
    return outs[0] if len(outs) == 1 else outs


def _row(arr, tm, col_block=None, width=None):
    width = arr.shape[1] if width is None else width
    cb = 0 if col_block is None else col_block
    return arr, pl.BlockSpec((tm, width), lambda i: (i, cb))


def _full(arr):
    nd = arr.ndim
    return arr, pl.BlockSpec(arr.shape, lambda i: (0,) * nd)


def _rowwise(name, body, T, tm, ins, outs, sums=(), after=()):
    n_in, n_out, n_sum, n_after = len(ins), len(outs), len(sums), len(after)

    def kern(*refs):
        in_refs, refs = refs[:n_in], refs[n_in + n_after:]
        out_refs, sum_refs = refs[:n_out], refs[n_out:]
        res = body(*[r[...] for r in in_refs])
        res = res if isinstance(res, tuple) else (res,)
        for o_ref, val in zip(out_refs, res[:n_out], strict=True):
            o_ref[...] = val.astype(o_ref.dtype)
        if n_sum:
            @pl.when(pl.program_id(0) == 0)
            def _():
                for s_ref in sum_refs:
                    s_ref[...] = jnp.zeros_like(s_ref)

            for s_ref, val in zip(sum_refs, res[n_out:], strict=True):
                s_ref[...] += val

    res = pl.pallas_call(
        kern, name=name, grid=(T // tm,),
        in_specs=[spec for _, spec in ins] + [ANY] * n_after,
        out_specs=[pl.BlockSpec((tm, c), lambda i: (i, 0)) for c, _ in outs]
        + [pl.BlockSpec((1, c), lambda i: (0, 0)) for c in sums],
        out_shape=[jax.ShapeDtypeStruct((T, c), dt) for c, dt in outs]
        + [jax.ShapeDtypeStruct((1, c), F32) for c in sums],
        compiler_params=_cparams(("arbitrary",)),
    )(*[a for a, _ in ins], *after)
    return res[0] if len(res) == 1 else res


def _sigmoid(x):
    return 1.0 / (1.0 + jnp.exp(-x))


def _rms(h):
    return lax.rsqrt(jnp.mean(h * h, axis=-1, keepdims=True) + RMS_EPS)


def _rms_bwd(dy, h, g):
    r = _rms(h)
    n = h * r
    dn = dy * g
    dh = r * (dn - n * jnp.mean(dn * n, axis=-1, keepdims=True))
    return dh, jnp.sum(dy * n, axis=0, keepdims=True)


def _rope(x, cos2, sin_signed):
    lane = lax.broadcasted_iota(jnp.int32, x.shape, 1)
    swapped = jnp.where((lane % HEAD_DIM) < HEAD_DIM // 2, pltpu.roll(x, 128 - HEAD_DIM // 2, 1), pltpu.roll(x, HEAD_DIM // 2, 1))
    return x * cos2 + swapped * sin_signed


NA_KEYS = NA_WIN_ROWS * GRID_W
NA_BASES = 8


def _na_row_geometry(r, rows):
    first = jnp.clip(r - NA_WIN_ROWS // 2, 0, rows - NA_WIN_ROWS)
    base = first - r + (NA_WIN_ROWS - 1)
    return pl.multiple_of(first * GRID_W, GRID_W), base


NA_ROWS_PER_STEP = 16
NA_BWD_ROWS_PER_STEP = 8


def _softmax_rows(s):
    p = jnp.exp(s - jnp.max(s, axis=-1, keepdims=True))
    return p / jnp.sum(p, axis=-1, keepdims=True)


def _split_pair(t):
    first = lax.broadcasted_iota(jnp.int32, t.shape, 1) < HEAD_DIM
    zero = jnp.zeros_like(t)
    return jnp.where(first, t, zero), jnp.where(first, zero, t)


def _join_pair(a, b):
    return jnp.where(lax.broadcasted_iota(jnp.int32, a.shape, 1) < HEAD_DIM, a, b)


_NT = (((1,), (1,)), ((), ()))
_TN = (((0,), (0,)), ((), ()))


def _na_fwd(qkv, tab):
    T = qkv.shape[0]
    rows = T // GRID_W
    n_pairs = NA_WIDTH // 128

    def body(q_ref, k_ref, v_ref, tab_ref, y_ref):
        def step(it, carry):
            geo = [_na_row_geometry(it * NA_ROWS_PER_STEP + u, rows) for u in range(NA_ROWS_PER_STEP)]
            q0s = [pl.multiple_of((it * NA_ROWS_PER_STEP + u) * GRID_W, GRID_W) for u in range(NA_ROWS_PER_STEP)]
            ss = [lax.dot_general(jnp.concatenate(_split_pair(q_ref[pl.ds(q0, GRID_W), :] * Q_SCALE), axis=0),
                                  k_ref[pl.ds(k0, NA_KEYS), :], _NT, preferred_element_type=F32)
                  for q0, (k0, _) in zip(q0s, geo)]
            ps = [_softmax_rows(s + jnp.concatenate([tab_ref[0, base], tab_ref[1, base]], axis=0)) for s, (_, base) in zip(ss, geo)]
            ys = [jnp.dot(p.astype(BF), v_ref[pl.ds(k0, NA_KEYS), :], preferred_element_type=F32) for p, (k0, _) in zip(ps, geo)]
            for q0, y2 in zip(q0s, ys):
                y_ref[pl.ds(q0, GRID_W), :] = _join_pair(y2[:GRID_W], y2[GRID_W:]).astype(y_ref.dtype)
            return carry

        lax.fori_loop(0, rows // NA_ROWS_PER_STEP, step, 0)

    def cols(first):
        return pl.BlockSpec((T, 128), lambda j: (0, first + j))

    return pl.pallas_call(
        body, name="na_fwd", grid=(n_pairs,),
        in_specs=[cols(0), cols(n_pairs), cols(2 * n_pairs), pl.BlockSpec((2, NA_BASES, GRID_W, NA_KEYS), lambda j: (j, 0, 0, 0))],
        out_specs=cols(0), out_shape=jax.ShapeDtypeStruct((T, NA_WIDTH), BF),
        compiler_params=_cparams(("parallel",)),
    )(qkv, qkv, qkv, tab)


def _na_bwd(qkv, tab, do):
    T = qkv.shape[0]
    rows = T // GRID_W
    n_pairs = NA_WIDTH // 128

    def body(q_ref, k_ref, v_ref, tab_ref, do_ref, dq_ref, dk_out, dv_out, dtab_ref, dk_ref, dv_ref):
        dk_ref[...] = jnp.zeros_like(dk_ref)
        dv_ref[...] = jnp.zeros_like(dv_ref)
        dtab_ref[...] = jnp.zeros_like(dtab_ref)

        def step(it, carry):
            U = NA_BWD_ROWS_PER_STEP
            geo = [_na_row_geometry(it * U + u, rows) for u in range(U)]
            q0s = [pl.multiple_of((it * U + u) * GRID_W, GRID_W) for u in range(U)]
            q2s = [jnp.concatenate(_split_pair(q_ref[pl.ds(q0, GRID_W), :] * Q_SCALE), axis=0) for q0 in q0s]
            do2s = [jnp.concatenate(_split_pair(do_ref[pl.ds(q0, GRID_W), :]), axis=0) for q0 in q0s]
            ss = [lax.dot_general(q2, k_ref[pl.ds(k0, NA_KEYS), :], _NT, preferred_element_type=F32) for q2, (k0, _) in zip(q2s, geo)]
            dps = [lax.dot_general(do2, v_ref[pl.ds(k0, NA_KEYS), :], _NT, preferred_element_type=F32) for do2, (k0, _) in zip(do2s, geo)]
            ps = [_softmax_rows(s + jnp.concatenate([tab_ref[0, base], tab_ref[1, base]], axis=0)) for s, (_, base) in zip(ss, geo)]
            dss = [p * (dp - jnp.sum(dp * p, axis=-1, keepdims=True)) for p, dp in zip(ps, dps)]
            dvs = [lax.dot_general(p.astype(BF), do2, _TN, preferred_element_type=F32) for p, do2 in zip(ps, do2s)]
            dsbs = [ds.astype(BF) for ds in dss]
            dqs = [jnp.dot(dsb, k_ref[pl.ds(k0, NA_KEYS), :], preferred_element_type=F32) for dsb, (k0, _) in zip(dsbs, geo)]
            dks = [lax.dot_general(dsb, q2, _TN, preferred_element_type=F32) for dsb, q2 in zip(dsbs, q2s)]
            for u in range(U):
                k0, base = geo[u]
                dtab_ref[0, base] += dss[u][:GRID_W]
                dtab_ref[1, base] += dss[u][GRID_W:]
                dq_ref[pl.ds(q0s[u], GRID_W), :] = (_join_pair(dqs[u][:GRID_W], dqs[u][GRID_W:]) * Q_SCALE).astype(dq_ref.dtype)
                dk_ref[pl.ds(k0, NA_KEYS), :] += dks[u]
                dv_ref[pl.ds(k0, NA_KEYS), :] += dvs[u]
            return carry

        lax.fori_loop(0, rows // NA_BWD_ROWS_PER_STEP, step, 0)
        dk_out[...] = dk_ref[...].astype(dk_out.dtype)
        dv_out[...] = dv_ref[...].astype(dv_out.dtype)

    def cols(first):
        return pl.BlockSpec((T, 128), lambda j: (0, first + j))

    tabs = pl.BlockSpec((2, NA_BASES, GRID_W, NA_KEYS), lambda j: (j, 0, 0, 0))
    wide = jax.ShapeDtypeStruct((T, NA_WIDTH), BF)
    return pl.pallas_call(
        body, name="na_bwd", grid=(n_pairs,),
        in_specs=[cols(0), cols(n_pairs), cols(2 * n_pairs), tabs, cols(0)],
        out_specs=[cols(0), cols(0), cols(0), tabs],
        out_shape=[wide, wide, wide, jax.ShapeDtypeStruct((NA_HEADS, NA_BASES, GRID_W, NA_KEYS), F32)],
        scratch_shapes=[pltpu.VMEM((T, 128), F32), pltpu.VMEM((T, 128), F32)],
        compiler_params=_cparams(("parallel",)),
    )(qkv, qkv, qkv, tab, do)


def _na_bias_table(rpb):
    H, n_rows, n_cols = rpb.shape

    def body(r_ref, tab_ref):
        q = lax.broadcasted_iota(jnp.int32, (GRID_W, 128), 0)
        kc = lax.broadcasted_iota(jnp.int32, (GRID_W, 128), 1)
        first = jnp.clip(q - NA_WIN_COLS // 2, 0, GRID_W - NA_WIN_COLS)
        valid = (kc >= first) & (kc < first + NA_WIN_COLS)
        toeplitz = []
        for ro in range(n_rows):
            row = jnp.broadcast_to(r_ref[pl.ds(ro, 1), :], (GRID_W, 128))
            shifted = pltpu.roll(pltpu.roll(row, 128 - (NA_WIN_COLS - 1), 1), 0, 1, stride=1, stride_axis=0)
            toeplitz.append(jnp.where(valid, shifted, NEG_INF))
        for base in range(NA_BASES):
            for j in range(NA_WIN_ROWS // 2):
                even, odd = toeplitz[base + 2 * j], toeplitz[base + 2 * j + 1]
                tab_ref[base, :, pl.ds(j * 128, 128)] = jnp.where(kc < GRID_W, even, pltpu.roll(odd, GRID_W, 1))

    padded = jnp.pad(rpb, ((0, 0), (0, 16 - n_rows), (0, 128 - n_cols)))
    return pl.pallas_call(
        body, name="na_bias_table", grid=(H,),
        in_specs=[pl.BlockSpec((None, 16, 128), lambda h: (h, 0, 0))],
        out_specs=pl.BlockSpec((None, NA_BASES, GRID_W, NA_KEYS), lambda h: (h, 0, 0, 0)),
        out_shape=jax.ShapeDtypeStruct((H, NA_BASES, GRID_W, NA_KEYS), F32),
        compiler_params=_cparams(("parallel",)),
    )(padded)


def _na_rpb_grad(dtab, after=()):
    H = dtab.shape[0]
    n_rows = 2 * NA_WIN_ROWS - 1
    n_cols = 2 * NA_WIN_COLS - 1

    def body(d_ref, *rest):
        o_ref = rest[-1]
        lane = lax.broadcasted_iota(jnp.int32, (GRID_W, 128), 1)
        low = lane < GRID_W
        flip = (lax.broadcasted_iota(jnp.int32, (GRID_W, GRID_W), 0) + lax.broadcasted_iota(jnp.int32, (GRID_W, GRID_W), 1)
                == GRID_W - 1).astype(BF)

        def reverse_rows(t):
            out = jnp.zeros_like(t)
            for _ in range(3):
                piece = t.astype(BF)
                out = out + jnp.dot(flip, piece, preferred_element_type=F32)
                t = t - piece.astype(F32)
            return out

        out_rows = []
        for ro in range(n_rows):
            acc = jnp.zeros((GRID_W, 128), F32)
            for base in range(NA_BASES):
                i = ro - base
                if not 0 <= i < NA_WIN_ROWS:
                    continue
                pair = d_ref[base, :, pl.ds((i // 2) * 128, 128)]
                if i % 2:
                    pair = pltpu.roll(pair, GRID_W, 1)
                acc = acc + jnp.where(low, pair, 0.0)
            skew = pltpu.roll(reverse_rows(acc), 0, 1, stride=1, stride_axis=0)
            diag = jnp.sum(skew, axis=0, keepdims=True)
            out_rows.append(pltpu.roll(jnp.broadcast_to(diag, (8, 128)), 128 - (GRID_W - NA_WIN_COLS), 1)[:1])
        out_rows.append(jnp.zeros((1, 128), F32))
        res = jnp.concatenate(out_rows, axis=0)
        o_ref[...] = jnp.where(lax.broadcasted_iota(jnp.int32, res.shape, 1) < n_cols, res, 0.0)

    return pl.pallas_call(
        body, name="na_rpb_grad", grid=(H,),
        in_specs=[pl.BlockSpec((None, NA_BASES, GRID_W, NA_KEYS), lambda h: (h, 0, 0, 0))] + [ANY] * len(after),
        out_specs=pl.BlockSpec((None, n_rows + 1, 128), lambda h: (h, 0, 0)),
        out_shape=jax.ShapeDtypeStruct((H, n_rows + 1, 128), F32),
        compiler_params=_cparams(("parallel",)),
    )(dtab, *after)


BAND_Q = 128
BAND_KEYS = BAND_Q + 2 * DIL_RADIUS


def _band_geometry(n, L):
    q0 = pl.multiple_of(n * BAND_Q, BAND_Q)
    k0 = pl.multiple_of(jnp.clip(q0 - DIL_RADIUS, 0, L - BAND_KEYS), DIL_RADIUS)
    qi = q0 + lax.broadcasted_iota(jnp.int32, (BAND_Q, BAND_KEYS), 0)
    kj = k0 + lax.broadcasted_iota(jnp.int32, (BAND_Q, BAND_KEYS), 1)
    return q0, k0, jnp.abs(qi - kj) <= DIL_RADIUS


DIL_PAIRS = DIL_OUT_WIDTH // 128


def _residue_shape(dil, T, dtype):
    return jax.ShapeDtypeStruct((DIL_PAIRS, dil, T // dil, 128), dtype)


def _residue_tile(dil, tm):
    return pl.BlockSpec((DIL_PAIRS, dil, tm // dil, 128), lambda i: (0, 0, i, 0))


def _to_natural(ref, scratch, dil, tm):
    tiles = []
    for pair in range(DIL_PAIRS):
        if dil == 1:
            tiles.append(ref[pair, 0].astype(F32))
            continue
        for r in range(dil):
            scratch[pl.ds(r, tm // dil, stride=dil), :] = ref[pair, r].astype(F32)
        tiles.append(scratch[...])
    return tiles


def _from_natural(tile, scratch, ref, pair, dil, tm):
    if dil == 1:
        ref[pair, 0] = tile.astype(ref.dtype)
        return
    scratch[...] = tile
    for r in range(dil):
        ref[pair, r] = scratch[pl.ds(r, tm // dil, stride=dil), :].astype(ref.dtype)


def _band_specs(group, T):
    dil = DIL_GROUPS[group][1]
    L = T // dil
    assert L % BAND_Q == 0 and L >= BAND_KEYS, (T, dil)
    per_residue = min(BAND_BLOCKS_PER_STEP, L // BAND_Q)
    residues = min(dil, BAND_BLOCKS_PER_STEP // per_residue)
    spec = pl.BlockSpec((None, residues, L, 128), lambda s: (s % DIL_PAIRS, s // DIL_PAIRS, 0, 0))
    return L, residues, per_residue, (dil // residues * DIL_PAIRS,), spec


BAND_BLOCKS_PER_STEP = 8


def _band_softmax(s, valid):
    s = jnp.where(valid, s, NEG_INF)
    m = jnp.max(s, axis=-1, keepdims=True)
    p = jnp.exp(s - m)
    l = jnp.sum(p, axis=-1, keepdims=True)
    return p / l, m + jnp.log(l)


def _band_fwd(q, k, v, group):
    T = q.shape[1] * q.shape[2]
    L, residues, U, grid, spec = _band_specs(group, T)

    def body(q_ref, k_ref, v_ref, o_ref, lse_ref):
        def step(it, carry):
            geo = [(r, *_band_geometry(it * U + u, L)) for r in range(residues) for u in range(U)]
            ss = [lax.dot_general(jnp.concatenate(_split_pair(q_ref[r, pl.ds(q0, BAND_Q), :]), axis=0),
                                  k_ref[r, pl.ds(k0, BAND_KEYS), :], _NT, preferred_element_type=F32) for r, q0, k0, _ in geo]
            pls = [_band_softmax(s, jnp.concatenate([valid, valid], axis=0)) for s, (_, _, _, valid) in zip(ss, geo)]
            os = [jnp.dot(p.astype(BF), v_ref[r, pl.ds(k0, BAND_KEYS), :], preferred_element_type=F32)
                  for (p, _), (r, _, k0, _) in zip(pls, geo)]
            for (r, q0, _, _), o2, (_, lse) in zip(geo, os, pls):
                o_ref[r, pl.ds(q0, BAND_Q), :] = _join_pair(o2[:BAND_Q], o2[BAND_Q:])
                lse2 = jnp.broadcast_to(lse, (2 * BAND_Q, 128))
                lse_ref[r, pl.ds(q0, BAND_Q), :] = _join_pair(lse2[:BAND_Q], lse2[BAND_Q:])
            return carry

        lax.fori_loop(0, L // (BAND_Q * U), step, 0)

    res = _residue_shape(DIL_GROUPS[group][1], T, F32)
    return pl.pallas_call(
        body, name=f"band_fwd_g{group}", grid=grid,
        in_specs=[spec] * 3, out_specs=[spec] * 2, out_shape=[res, res],
        compiler_params=_cparams(("parallel",)),
    )(q, k, v)


def _band_bwd(q, k, v, do, dlse, group):
    T = q.shape[1] * q.shape[2]
    L, residues, U, grid, spec = _band_specs(group, T)

    def body(q_ref, k_ref, v_ref, do_ref, dlse_ref, dq_ref, dk_ref, dv_ref):
        dk_ref[...] = jnp.zeros_like(dk_ref)
        dv_ref[...] = jnp.zeros_like(dv_ref)

        def step(it, carry):
            geo = [(r, *_band_geometry(it * U + u, L)) for r in range(residues) for u in range(U)]
            q2s = [jnp.concatenate(_split_pair(q_ref[r, pl.ds(q0, BAND_Q), :]), axis=0) for r, q0, _, _ in geo]
            do2s = [jnp.concatenate(_split_pair(do_ref[r, pl.ds(q0, BAND_Q), :]), axis=0) for r, q0, _, _ in geo]
            ss = [lax.dot_general(q2, k_ref[r, pl.ds(k0, BAND_KEYS), :], _NT, preferred_element_type=F32)
                  for q2, (r, _, k0, _) in zip(q2s, geo)]
            dps = [lax.dot_general(do2, v_ref[r, pl.ds(k0, BAND_KEYS), :], _NT, preferred_element_type=F32)
                   for do2, (r, _, k0, _) in zip(do2s, geo)]
            ps = [_band_softmax(s, jnp.concatenate([valid, valid], axis=0))[0] for s, (_, _, _, valid) in zip(ss, geo)]
            dss = []
            for p, dp, (r, q0, _, _) in zip(ps, dps, geo):
                dl = dlse_ref[r, pl.ds(q0, BAND_Q), :]
                dl2 = jnp.concatenate([dl[:, :1], dl[:, HEAD_DIM:HEAD_DIM + 1]], axis=0)
                dss.append(p * (dp - jnp.sum(dp * p, axis=-1, keepdims=True) + dl2))
            dvs = [lax.dot_general(p.astype(BF), do2, _TN, preferred_element_type=F32) for p, do2 in zip(ps, do2s)]
            dsbs = [ds.astype(BF) for ds in dss]
            dqs = [jnp.dot(dsb, k_ref[r, pl.ds(k0, BAND_KEYS), :], preferred_element_type=F32) for dsb, (r, _, k0, _) in zip(dsbs, geo)]
            dks = [lax.dot_general(dsb, q2, _TN, preferred_element_type=F32) for dsb, q2 in zip(dsbs, q2s)]
            for u, (r, q0, k0, _) in enumerate(geo):
                dq_ref[r, pl.ds(q0, BAND_Q), :] = _join_pair(dqs[u][:BAND_Q], dqs[u][BAND_Q:])
                dk_ref[r, pl.ds(k0, BAND_KEYS), :] += dks[u]
                dv_ref[r, pl.ds(k0, BAND_KEYS), :] += dvs[u]
            return carry

        lax.fori_loop(0, L // (BAND_Q * U), step, 0)

    res = _residue_shape(DIL_GROUPS[group][1], T, F32)
    return pl.pallas_call(
        body, name=f"band_bwd_g{group}", grid=grid,
        in_specs=[spec] * 5, out_specs=[spec] * 3, out_shape=[res] * 3,
        compiler_params=_cparams(("parallel",)),
    )(q, k, v, do, dlse)


def _head_sums(t):
    head = lax.broadcasted_iota(jnp.int32, t.shape, 1) // HEAD_DIM
    out = jnp.zeros_like(t)
    for h in range(t.shape[1] // HEAD_DIM):
        mine = head == h
        out = jnp.where(mine, jnp.sum(jnp.where(mine, t, 0.0), axis=-1, keepdims=True), out)
    return out


def _dil_merge_fwd(os, lses, T, tm):
    G = len(DIL_GROUPS)
    W = DIL_OUT_WIDTH
    dils = [d for _, d in DIL_GROUPS]

    def body(*refs):
        o_refs, lse_refs = refs[:G], refs[G:2 * G]
        y_ref, w_refs, on_refs, scratch = refs[2 * G], refs[2 * G + 1:3 * G + 1], refs[3 * G + 1:4 * G + 1], refs[-1]
        o = [jnp.concatenate(_to_natural(r, scratch, d, tm), axis=1) for r, d in zip(o_refs, dils)]
        ls = [jnp.concatenate(_to_natural(r, scratch, d, tm), axis=1) for r, d in zip(lse_refs, dils)]
        m = functools.reduce(jnp.maximum, ls)
        es = [jnp.exp(l - m) for l in ls]
        tot = functools.reduce(jnp.add, es)
        ws = [e / tot for e in es]
        y_ref[...] = functools.reduce(jnp.add, [w * t for w, t in zip(ws, o)]).astype(y_ref.dtype)
        for g in range(G):
            w_refs[g][...] = ws[g]
            on_refs[g][...] = o[g]

    nat = pl.BlockSpec((tm, W), lambda i: (i, 0))
    res = pl.pallas_call(
        body, name="dil_merge_fwd", grid=(T // tm,),
        in_specs=[_residue_tile(d, tm) for d in dils] * 2,
        out_specs=[nat] * (2 * G + 1),
        out_shape=[jax.ShapeDtypeStruct((T, W), BF)] + [jax.ShapeDtypeStruct((T, W), F32)] * (2 * G),
        scratch_shapes=[pltpu.VMEM((tm, 128), F32)],
        compiler_params=_cparams(("parallel",)),
    )(*os, *lses)
    return res[0], res[1:G + 1], res[G + 1:]


def _dil_merge_bwd(dy, os, ws, tm, after=()):
    G = len(DIL_GROUPS)
    T, W = dy.shape
    dils = [d for _, d in DIL_GROUPS]
    n_after = len(after)

    def body(*refs):
        dyt = refs[0][...]
        o, w = [r[...] for r in refs[1:G + 1]], [r[...] for r in refs[G + 1:2 * G + 1]]
        refs = refs[2 * G + 1 + n_after:]
        do_refs, dlse_refs, scratch = refs[:G], refs[G:2 * G], refs[-1]
        dws = [_head_sums(dyt * t) for t in o]
        mean = functools.reduce(jnp.add, [a * b for a, b in zip(w, dws)])
        for g, d in enumerate(dils):
            do, dlse = w[g] * dyt, w[g] * (dws[g] - mean)
            for pair in range(DIL_PAIRS):
                cols = slice(pair * 128, (pair + 1) * 128)
                _from_natural(do[:, cols], scratch, do_refs[g], pair, d, tm)
                _from_natural(dlse[:, cols], scratch, dlse_refs[g], pair, d, tm)

    nat = pl.BlockSpec((tm, W), lambda i: (i, 0))
    res = pl.pallas_call(
        body, name="dil_merge_bwd", grid=(T // tm,),
        in_specs=[nat] * (2 * G + 1) + [ANY] * n_after,
        out_specs=[_residue_tile(d, tm) for d in dils] * 2,
        out_shape=[_residue_shape(d, T, BF) for d in dils] + [_residue_shape(d, T, F32) for d in dils],
        scratch_shapes=[pltpu.VMEM((tm, 128), F32)],
        compiler_params=_cparams(("parallel",)),
    )(dy, *os, *ws, *after)
    return res[:G], res[G:]


def _qkv_prep(z, cos2, sin_signed, tm):
    T = z.shape[0]
    G = len(DIL_GROUPS)
    dils = [d for _, d in DIL_GROUPS]
    n_dil_blocks = 3 * DIL_WIDTH // 128

    def body(*refs):
        blocks = refs[:n_dil_blocks]
        cos_ref, sin_ref = refs[n_dil_blocks], refs[1 + n_dil_blocks]
        outs = refs[2 + n_dil_blocks:]
        for part in range(3):
            for g, d in enumerate(dils):
                out = outs[g * 3 + part]
                for pair in range(DIL_PAIRS):
                    blk = blocks[part * (DIL_WIDTH // 128) + g * DIL_PAIRS + pair]
                    for r in range(d):
                        rows = pl.ds(r, tm // d, stride=d) if d > 1 else slice(None)
                        x = blk[rows, :]
                        if part < 2:
                            x = _rope(x, cos_ref[rows, :], sin_ref[rows, :])
                        if part == 0:
                            x = x * Q_SCALE
                        out[pair, r] = x.astype(out.dtype)

    lane_block = [pl.BlockSpec((tm, 128), functools.partial(lambda b, i: (i, b), b)) for b in range(n_dil_blocks)]
    tab = pl.BlockSpec((tm, 128), lambda i: (i, 0))
    res = pl.pallas_call(
        body, name="qkv_prep", grid=(T // tm,),
        in_specs=lane_block + [tab, tab],
        out_specs=[_residue_tile(d, tm) for d in dils for _ in range(3)],
        out_shape=[_residue_shape(d, T, BF) for d in dils for _ in range(3)],
        compiler_params=_cparams(("parallel",)),
    )(*[z] * n_dil_blocks, cos2, sin_signed)
    return [res[3 * g:3 + 3 * g] for g in range(G)]


def _qkv_unprep(d_na, d_dil, cos2, sin_signed, tm, after=()):
    T = d_na[0].shape[0]
    G = len(DIL_GROUPS)
    dils = [d for _, d in DIL_GROUPS]
    n_after = len(after)

    def body(*refs):
        dq, dk, dv = (r[...] for r in refs[:3])
        res_refs = refs[3:3 + 3 * G]
        cs, sn = refs[3 + 3 * G][...], refs[4 + 3 * G][...]
        out, scratch = refs[5 + 3 * G + n_after], refs[-1]
        cols = [dq, dk, dv]
        for part in range(3):
            for g, d in enumerate(dils):
                for x in _to_natural(res_refs[g * 3 + part], scratch, d, tm):
                    if part < 2:
                        x = _rope(x, cs, -sn)
                    cols.append((x * Q_SCALE if part == 0 else x).astype(out.dtype))
        out[...] = jnp.concatenate(cols, axis=1)

    wide = pl.BlockSpec((tm, NA_WIDTH), lambda i: (i, 0))
    tab = pl.BlockSpec((tm, 128), lambda i: (i, 0))
    return pl.pallas_call(
        body, name="qkv_unprep", grid=(T // tm,),
        in_specs=[wide] * 3 + [_residue_tile(d, tm) for d in dils for _ in range(3)] + [tab, tab] + [ANY] * n_after,
        out_specs=pl.BlockSpec((tm, QKV_WIDTH), lambda i: (i, 0)),
        out_shape=jax.ShapeDtypeStruct((T, QKV_WIDTH), BF),
        scratch_shapes=[pltpu.VMEM((tm, 128), F32)],
        compiler_params=_cparams(("parallel",)),
    )(*d_na, *[t for g in range(G) for t in d_dil[g]], cos2, sin_signed, *after)


def _rope_tables(positions):
    half = HEAD_DIM // 2
    inv_freq = ROPE_THETA ** (-jnp.arange(half, dtype=F32) / half)
    ang = positions.astype(F32)[:, None] * inv_freq
    cos, sin = jnp.cos(ang), jnp.sin(ang)
    return jnp.tile(jnp.concatenate([cos, cos], axis=1), (1, 2)), jnp.tile(jnp.concatenate([-sin, sin], axis=1), (1, 2))


def _pack_rows(t):
    return t.reshape(-1, PACK_W)


def _me():
    return lax.axis_index("x"), lax.axis_index("y"), lax.axis_index("c")


def _other_chips(x, y):
    return [(1 - x, y), (x, 1 - y), (1 - x, 1 - y)]


def _pair_sum(g, got, tm):
    S, R, W = g.shape
    half = R // 2
    nb = half // tm

    def body(pos_ref, g_ref, got_ref, own_ref, ob_ref):
        tot = g_ref[...] + got_ref[...]
        ob_ref[...] = tot.astype(ob_ref.dtype)

        @pl.when(pl.program_id(1) == pos_ref[1])
        def _():
            own_ref[...] = tot

    tile = pl.BlockSpec((None, tm, W), lambda i, s, pos_ref: (s, i, 0))
    c, chip = lax.axis_index("c"), 2 * lax.axis_index("x") + lax.axis_index("y")
    return pl.pallas_call(
        body, name="pair_sum",
        grid_spec=pltpu.PrefetchScalarGridSpec(
            num_scalar_prefetch=1, grid=(nb, S),
            in_specs=[pl.BlockSpec((None, tm, W), lambda i, s, pos_ref: (s, pos_ref[0] * nb + i, 0)), tile],
            out_specs=[pl.BlockSpec((tm, W), lambda i, s, pos_ref: (i, 0)), tile]),
        out_shape=[jax.ShapeDtypeStruct((half, W), F32), jax.ShapeDtypeStruct((S, half, W), BF)],
        compiler_params=_cparams(("parallel", "arbitrary")),
    )(jnp.stack([c, chip]).astype(jnp.int32), g, got)


def _chip_sum(own, others, tm):
    n, h, W = others.shape
    nb = h // tm

    def body(c_ref, own_ref, p_ref, o_ref):
        o_ref[...] = ((own_ref[...] + p_ref[0].astype(F32)) + p_ref[1].astype(F32)) + p_ref[2].astype(F32)

    return pl.pallas_call(
        body, name="chip_sum",
        grid_spec=pltpu.PrefetchScalarGridSpec(
            num_scalar_prefetch=1, grid=(nb,),
            in_specs=[pl.BlockSpec((tm, W), lambda i, c_ref: (i, 0)), pl.BlockSpec((n, tm, W), lambda i, c_ref: (0, i, 0))],
            out_specs=pl.BlockSpec((tm, W), lambda i, c_ref: (c_ref[0] * nb + i, 0))),
        out_shape=jax.ShapeDtypeStruct((2 * h, W), F32),
        compiler_params=_cparams(("parallel",)),
    )(lax.axis_index("c").reshape(1).astype(jnp.int32), own, others)


def _join_halves(shard, after=()):
    h = shard.shape[0] // 2

    def body(in_ref, *rest):
        out_ref, send_sem, recv_sem = rest[len(after):]
        x, y, c = _me()
        cp = pltpu.make_async_remote_copy(
            src_ref=in_ref.at[pl.ds(c * h, h), :], dst_ref=out_ref.at[pl.ds(c * h, h), :],
            send_sem=send_sem, recv_sem=recv_sem, device_id=(x, y, 1 - c), device_id_type=MESH)
        cp.start()
        pltpu.make_async_remote_copy(
            src_ref=in_ref.at[pl.ds(c * h, h), :], dst_ref=out_ref.at[pl.ds((1 - c) * h, h), :],
            send_sem=send_sem, recv_sem=recv_sem, device_id=(x, y, 1 - c), device_id_type=MESH).wait_recv()
        cp.wait_send()

    return pl.pallas_call(
        body, name="join_halves", in_specs=[ANY] * (1 + len(after)), out_specs=ANY,
        out_shape=jax.ShapeDtypeStruct(shard.shape, shard.dtype), input_output_aliases={0: 0},
        scratch_shapes=[pltpu.SemaphoreType.DMA, pltpu.SemaphoreType.DMA],
    )(shard, *after)


def _allreduce_copies(src_ref, land_ref):
    x, y, c = _me()
    peers = [((x + fx) % 2, (y + fy) % 2, (c + fc) % 2) for fx in range(2) for fy in range(2) for fc in range(2)][1:]
    return [(src_ref, land_ref.at[4 * x + 2 * y + c], peer) for peer in peers]


def _allreduce_start(s, after):
    return _split_start("allreduce_small_start", s, (N_DEV, *s.shape), s.dtype, N_DEV - 1, _allreduce_copies, after)


def _allreduce_finish(flight, after):
    own, landed = _split_wait("allreduce_small_wait", flight, after, N_DEV - 1, _allreduce_copies)
    me = 4 * lax.axis_index("x") + 2 * lax.axis_index("y") + lax.axis_index("c")
    parts = lax.dynamic_update_slice(landed, own[None], (me, 0, 0))

    def body(p_ref, o_ref):
        total = p_ref[0]
        for d in range(1, N_DEV):
            total = total + p_ref[d]
        o_ref[...] = total

    return pl.pallas_call(
        body, name="allreduce_small_sum",
        in_specs=[pl.BlockSpec(memory_space=pltpu.VMEM)], out_specs=pl.BlockSpec(memory_space=pltpu.VMEM),
        out_shape=jax.ShapeDtypeStruct(own.shape, F32),
    )(parts)


HBM_SPEC = pl.BlockSpec(memory_space=pltpu.HBM)
SEM_SPEC = pl.BlockSpec(memory_space=pltpu.SEMAPHORE)
DATAFLOW = pltpu.SideEffectType.DATAFLOW_SIDE_EFFECTING


class _InFlight(NamedTuple):
    sems: tuple
    src: jax.Array
    land: jax.Array
    token: jax.Array


def _split_start(name, src, land_shape, land_dtype, n, copies, after=()):
    n_after = len(after)

    def body(src_ref, land_ref, *rest):
        rest = rest[n_after:]
        sems, token = rest[:2 * n], rest[-1]
        for k, (s, d, peer) in enumerate(copies(src_ref, land_ref)):
            pltpu.make_async_remote_copy(src_ref=s, dst_ref=d, send_sem=sems[k], recv_sem=sems[n + k],
                                         device_id=peer, device_id_type=MESH).start()
        token[...] = jnp.zeros_like(token)

    outs = pl.pallas_call(
        body, name=name,
        out_shape=(*[pltpu.SemaphoreType.DMA(())] * (2 * n), pltpu.HBM(src.shape, src.dtype), pltpu.HBM(land_shape, land_dtype),
                   jax.ShapeDtypeStruct((8, 128), F32)),
        in_specs=(HBM_SPEC, HBM_SPEC, *[ANY] * n_after),
        out_specs=(*[SEM_SPEC] * (2 * n), HBM_SPEC, HBM_SPEC, pl.BlockSpec(memory_space=pltpu.VMEM)),
        input_output_aliases={0: 2 * n, 1: 2 * n + 1},
        compiler_params=pltpu.CompilerParams(has_side_effects=DATAFLOW),
    )(pltpu.with_memory_space_constraint(src, pltpu.HBM), pltpu.with_memory_space_constraint(lax.empty(land_shape, land_dtype), pltpu.HBM),
      *after)
    return _InFlight(tuple(outs[:2 * n]), outs[2 * n], outs[2 * n + 1], outs[2 * n + 2])


def _split_wait(name, flight, after, n, copies):
    after = after if isinstance(after, tuple) else (after,)

    def body(src_ref, land_ref, *rest):
        sems = rest[:2 * n]
        for k, (s, d, peer) in enumerate(copies(src_ref, land_ref)):
            cp = pltpu.make_async_remote_copy(src_ref=s, dst_ref=d, send_sem=sems[k], recv_sem=sems[n + k],
                                              device_id=peer, device_id_type=MESH)
            cp.wait_send()
            cp.wait_recv()

    return pl.pallas_call(
        body, name=name,
        out_shape=(pltpu.HBM(flight.src.shape, flight.src.dtype), pltpu.HBM(flight.land.shape, flight.land.dtype)),
        in_specs=(HBM_SPEC, HBM_SPEC, *[SEM_SPEC] * (2 * n), *[ANY] * len(after)),
        out_specs=(HBM_SPEC, HBM_SPEC), input_output_aliases={0: 0, 1: 1},
        compiler_params=pltpu.CompilerParams(has_side_effects=DATAFLOW),
    )(flight.src, flight.land, *flight.sems, *after)


def _gather_copies(src_ref, land_ref):
    x, y, c = _me()
    return [(src_ref, land_ref.at[2 * x + y], (*chip, c)) for chip in _other_chips(x, y)]


def _gather_start(packed, tag, after=()):
    return _split_start(f"gather_start_{tag}", packed, (N_CHIPS, *packed.shape), packed.dtype, 3, _gather_copies, after)


def _gather_wait(flight, after, tag):
    src, others = _split_wait(f"gather_wait_{tag}", flight, after, 3, _gather_copies)
    return lax.dynamic_update_slice(others, src[None], (2 * lax.axis_index("x") + lax.axis_index("y"), 0, 0))


def _across_copies(src_ref, land_ref):
    x, y, c = _me()
    half = src_ref.shape[0] // 2
    rows = pl.ds(c * half, half)
    return [(src_ref.at[rows, :], land_ref.at[2 * x + y, rows, :], (*chip, c)) for chip in _other_chips(x, y)]


def _to_sibling_copies(all_ref, unused_ref):
    x, y, c = _me()
    half = all_ref.shape[1] // 2
    places = [all_ref.at[2 * chip[0] + chip[1], pl.ds(c * half, half), :] for chip in _other_chips(x, y)]
    return [(place, place, (x, y, 1 - c)) for place in places]


def _gather_halves_start(shard, tag, after=()):
    return _split_start(f"gather_{tag}_across_start", shard, (N_CHIPS, *shard.shape), shard.dtype, 3, _across_copies, after)


def _gather_halves_relay(flight, after, tag):
    shard, landed = _split_wait(f"gather_{tag}_across_wait", flight, after, 3, _across_copies)
    return shard, _split_start(f"gather_{tag}_sibling_start", landed, (8, 128), landed.dtype, 3, _to_sibling_copies)


def _gather_halves_finish(shard, relay, after, tag):
    others = _split_wait(f"gather_{tag}_sibling_wait", relay, after, 3, _to_sibling_copies)[0]
    return lax.dynamic_update_slice(others, shard[None], (2 * lax.axis_index("x") + lax.axis_index("y"), 0, 0))


def _assemble_w_in(shards, tm):
    S, R, C = shards.shape
    n_gates = 2 * D_MODEL

    def body(s_ref, w_ref, g_ref):
        full = jnp.concatenate([s_ref[s] for s in range(S)], axis=1)
        w_ref[...] = full
        g_ref[...] = full[:, S * C - n_gates:]

    return pl.pallas_call(
        body, name="assemble_w_in", grid=(R // tm,),
        in_specs=[pl.BlockSpec((S, tm, C), lambda i: (0, i, 0))],
        out_specs=[pl.BlockSpec((tm, S * C), lambda i: (i, 0)), pl.BlockSpec((tm, n_gates), lambda i: (i, 0))],
        out_shape=[jax.ShapeDtypeStruct((R, S * C), shards.dtype), jax.ShapeDtypeStruct((R, n_gates), shards.dtype)],
        compiler_params=_cparams(("parallel",)),
    )(shards)


def _swap_copies(src_ref, land_ref):
    x, y, c = _me()
    half = land_ref.shape[1]
    return [(src_ref.at[:, pl.ds((1 - c) * half, half), :], land_ref, (x, y, 1 - c))]


def _swap_start(g, tag):
    S, R, W = g.shape
    return _split_start(f"swap_halves_start_{tag}", g, (S, R // 2, W), g.dtype, 1, _swap_copies)


def _swap_wait(flight, after, tag):
    return _split_wait(f"swap_halves_wait_{tag}", flight, after, 1, _swap_copies)


def _scatter_copies(src_ref, land_ref):
    x, y, c = _me()
    return [(src_ref.at[2 * chip[0] + chip[1]], land_ref.at[j], (*chip, c)) for j, chip in enumerate(_other_chips(x, y))]


def _scatter_start(part, tag):
    S, h, W = part.shape
    return _split_start(f"scatter_chips_start_{tag}", part, (S - 1, h, W), part.dtype, 3, _scatter_copies)


def _scatter_wait(flight, after, tag):
    return _split_wait(f"scatter_chips_wait_{tag}", flight, after, 3, _scatter_copies)[1]


def _join_copies(shard_ref, unused_ref):
    x, y, c = _me()
    h = shard_ref.shape[0] // 2
    rows = shard_ref.at[pl.ds(c * h, h), :]
    return [(rows, rows, (x, y, 1 - c))]


def _join_start(shard):
    return _split_start("join_halves_start", shard, (8, 128), shard.dtype, 1, _join_copies)


def _join_wait(flight, after):
    return _split_wait("join_halves_wait", flight, after, 1, _join_copies)[0]


def _adamw(name, g, g_row0, w, m, v):
    _, R, C = w.shape
    tm = next(cand for cand in (368, 256, 128, 64, 32, 16, 8) if R % cand == 0)
    assert g_row0 % tm == 0 and g.shape[1] == C

    def body(g_ref, w_ref, m_ref, v_ref, go_ref, d_ref, mo_ref, vo_ref):
        gt = g_ref[...]
        mt = ADAM_B1 * m_ref[...] + (1.0 - ADAM_B1) * gt
        vt = ADAM_B2 * v_ref[...] + (1.0 - ADAM_B2) * jnp.square(gt)
        m_hat = mt / (1.0 - ADAM_B1 ** ADAM_STEP)
        v_hat = vt / (1.0 - ADAM_B2 ** ADAM_STEP)
        go_ref[...] = gt
        d_ref[...] = -ADAM_LR * (m_hat / (jnp.sqrt(v_hat) + ADAM_EPS) + ADAM_WD * w_ref[...])
        mo_ref[...] = mt
        vo_ref[...] = vt

    state = pl.BlockSpec((None, tm, C), lambda i: (0, i, 0))
    return pl.pallas_call(
        body, name=name, grid=(R // tm,),
        in_specs=[pl.BlockSpec((tm, C), lambda i: (g_row0 // tm + i, 0)), state, state, state],
        out_specs=[state] * 4, out_shape=[jax.ShapeDtypeStruct((1, R, C), F32)] * 4,
        compiler_params=_cparams(("parallel",)),
    )(g, w, m, v)


def _unpack_weights(gathered, names):
    S = gathered.shape[0]
    shard_shapes = {"w_in": (D_MODEL, (QKV_WIDTH + 2 * D_MODEL) // S), "w_branch_na": (NA_WIDTH, D_MODEL // S),
                    "w_branch_dil": (DIL_OUT_WIDTH, D_MODEL // S), "w_out": (D_MODEL // S, D_MODEL),
                    "w_up": (D_MODEL, D_FF // S), "w_down": (D_FF // S, D_MODEL),
                    "w_ple_gate": (D_MODEL // S, D_MODEL), "w_ple_proj": (PLE_DIM, D_MODEL // S)}
    col_sharded = {"w_in", "w_branch_na", "w_branch_dil", "w_up", "w_ple_proj"}
    out, r0 = {}, 0
    for name in names:
        rows, cols = shard_shapes[name]
        n = rows * cols // PACK_W
        t = gathered[:, r0:r0 + n, :].reshape(S, rows, cols)
        r0 += n
        out[name] = t.transpose(1, 0, 2).reshape(rows, S * cols) if name in col_sharded else t.reshape(S * rows, cols)
    return out


def kernel(x, p, positions, g_mix, w_in, rpb, w_branch_na, w_branch_dil, w_out, g_mlp, w_up, w_down, g_ple, w_ple_gate, w_ple_proj, g_final, loss_target, m_g_mix, m_w_in, m_rpb, m_w_branch_na, m_w_branch_dil, m_w_out, m_g_mlp, m_w_up, m_w_down, m_g_ple, m_w_ple_gate, m_w_ple_proj, m_g_final, v_g_mix, v_w_in, v_rpb, v_w_branch_na, v_w_branch_dil, v_w_out, v_g_mlp, v_w_up, v_w_down, v_g_ple, v_w_ple_gate, v_w_ple_proj, v_g_final):
    shards = {"w_in": w_in[0], "w_branch_na": w_branch_na[0], "w_branch_dil": w_branch_dil[0], "w_out": w_out[0],
              "w_up": w_up[0], "w_down": w_down[0], "w_ple_gate": w_ple_gate[0], "w_ple_proj": w_ple_proj[0]}
    params = {"w_in": w_in, "w_branch_na": w_branch_na, "w_branch_dil": w_branch_dil, "w_out": w_out, "w_up": w_up,
              "w_down": w_down, "w_ple_gate": w_ple_gate, "w_ple_proj": w_ple_proj,
              "m_w_in": m_w_in, "m_w_branch_na": m_w_branch_na, "m_w_branch_dil": m_w_branch_dil, "m_w_out": m_w_out,
              "m_w_up": m_w_up, "m_w_down": m_w_down, "m_w_ple_gate": m_w_ple_gate, "m_w_ple_proj": m_w_ple_proj,
              "v_w_in": v_w_in, "v_w_branch_na": v_w_branch_na, "v_w_branch_dil": v_w_branch_dil, "v_w_out": v_w_out,
              "v_w_up": v_w_up, "v_w_down": v_w_down, "v_w_ple_gate": v_w_ple_gate, "v_w_ple_proj": v_w_ple_proj}

    xs, ps, tgt = x[0], p[0, 0], loss_target[0]
    T = xs.shape[0]
    TM = 512
    gm, gl, gp, gf = g_mix, g_mlp, g_ple, g_final.reshape(1, D_MODEL)

    across = _gather_halves_start(shards["w_in"].astype(BF), "in")
    a = _rowwise("norm_mix", lambda h, g: h * _rms(h) * g, T, TM, [_row(xs, TM), _full(gm)], [(D_MODEL, BF)],
                 after=(across.token,))
    cos2, sin_signed = _rope_tables(positions[0])
    tab = _na_bias_table(rpb[0])
    packed_mix = jnp.concatenate([_pack_rows(shards[n].astype(BF)) for n in GATHER_MIX], axis=0)
    packed_mlp = jnp.concatenate([_pack_rows(shards[n].astype(BF)) for n in GATHER_MLP], axis=0)
    w_in_shard, w_in_relay = _gather_halves_relay(across, (a, tab, cos2, sin_signed, packed_mix, packed_mlp), "in")
    w_in_all = _gather_halves_finish(w_in_shard, w_in_relay, w_in_relay.token, "in")
    w_in_full, w_gates = _assemble_w_in(w_in_all, 256)
    W = {"w_in": w_in_full}
    mix_flight = _gather_start(packed_mix, "mix", after=(w_in_all,))
    mlp_across = _gather_halves_start(packed_mlp, "mlp", after=(mix_flight.token,))

    n3 = 3 * NA_WIDTH
    qkv = _mm("in_na", a, W["w_in"], "nn", 1024, 768, 1024, [BF], after=(mlp_across.token,),
              b_view=(n3, (D_MODEL, 768), lambda j, k: (k, j)))
    z_dil = _mm("in_dil", a, W["w_in"], "nn", 1024, 768, 1024, [F32], after=(mlp_across.token,),
                b_view=(3 * DIL_WIDTH, (D_MODEL, 768), lambda j, k: (k, n3 // 768 + j)))
    z_gates = _mm("in_gates", a, w_gates, "nn", 1024,1024, 1024, [BF], after=(mlp_across.token,))

    dil_ops = _qkv_prep(z_dil, cos2, sin_signed, TM)
    y_na = _na_fwd(qkv, tab)
    band = [_band_fwd(*dil_ops[g], g) for g in range(len(DIL_GROUPS))]
    y_dil, w_grp, o_nat = _dil_merge_fwd([b[0] for b in band], [b[1] for b in band], T, TM)

    W.update(_unpack_weights(_gather_wait(mix_flight, y_dil, "mix"), GATHER_MIX))
    mlp_shard, mlp_relay = _gather_halves_relay(mlp_across, y_dil, "mlp")
    u_na = _mm("branch_na", y_na, W["w_branch_na"], "nn", 1024,1024, 512, [BF], after=(mlp_relay.token,))
    def gate_mix(acc, gn, gd, un):
        ud = acc.astype(BF)
        return ud, _sigmoid(gn.astype(F32)) * un.astype(F32) + _sigmoid(gd.astype(F32)) * ud.astype(F32)

    u_dil, mixed = _mm("branch_dil", y_dil, W["w_branch_dil"], "nn", 512, 1024, 256, [BF, BF], epilogue=gate_mix,
                       extras=((z_gates, 0), (z_gates, 1), u_na))

    def add_norm(d, h, g):
        h = h + d
        return h, h * _rms(h) * g

    h1, cn = _mm("out_proj", mixed, W["w_out"], "nn", 512, 1024, 1024, [F32, BF], epilogue=add_norm, extras=(xs,), consts=(gl,))
    mlp_all = _gather_halves_finish(mlp_shard, mlp_relay, cn, "mlp")
    W.update({n: t for n, t in _unpack_weights(mlp_all, GATHER_MLP).items() if n.startswith("w_ple")})
    chip_block = (None, D_MODEL, PACK_W)
    up, act = _mm("mlp_up", cn, mlp_all, "nn", 1024,1024, 1024, [BF, BF],
                  epilogue=lambda acc: (acc, jnp.square(jnp.maximum(acc, 0.0))), b_view=(D_FF, chip_block, lambda j, k: (j, 0, 0)))
    h2, en = _mm("mlp_down", act, mlp_all, "nn", 512, 1024, D_FF, [F32, BF], epilogue=add_norm, extras=(h1,), consts=(gp,),
                 b_view=(D_MODEL, (N_CHIPS, D_MODEL, PACK_W), lambda j, k: (k, 1, 0)))
    pp = _mm("ple_proj", ps, W["w_ple_proj"], "nn", 1024,1024, 256, [F32])

    def head(gtt, h2t, ppt, tg, g):
        sg = _sigmoid(gtt)
        h3 = h2t + sg * ppt
        yo = h3 * _rms(h3) * g
        diff = yo - tg
        loss = 0.5 * jnp.sum(jnp.mean(jnp.square(diff), axis=-1, keepdims=True), axis=0, keepdims=True)
        dh3, dg = _rms_bwd(diff * (1.0 / D_MODEL), h3, g)
        return dh3, dh3 * ppt * sg * (1.0 - sg), dh3 * sg, jnp.broadcast_to(loss, (1, 128)), dg

    dh3, d_gt, d_pp, loss_part, dg_final = _mm(
        "ple_gate_loss_head", en, W["w_ple_gate"], "nn", 512, 1024, 1024, [F32, BF, BF], epilogue=head,
        extras=(h2, pp, tgt), consts=(gf,), sums=[128, D_MODEL])

    early_shapes = {n: shards[n].shape for n in REDUCE_EARLY}
    early_rows = sum(r * c for r, c in early_shapes.values()) // PACK_W
    shard_rows = D_MODEL // N_CHIPS
    early_buf = _mm("g_ple_gate", en, d_gt, "tn", 1024, 1024, 2048, [F32],
                    into=(jax.ShapeDtypeStruct((N_CHIPS, early_rows, PACK_W), F32), (N_CHIPS, shard_rows, PACK_W),
                          lambda i, j: (0, 2 * D_MODEL // shard_rows, 0)))
    g_ple_proj = _mm("g_ple_proj", ps, d_pp, "tn", 256, 1024, 1024,[F32])

    def add_norm_bwd(dn, dh_out, h, g):
        dh, dg = _rms_bwd(dn, h, g)
        dh = dh_out + dh
        return dh, dh, dg

    dh2, dh2_b, dg_ple = _mm("d_ple_gate", d_gt, W["w_ple_gate"], "nt", 512, 1024, 1024, [F32, BF],
                             epilogue=add_norm_bwd, extras=(dh3, h2), consts=(gp,), sums=[D_MODEL])
    d_up = _mm("d_mlp_down", dh2_b, mlp_all, "nt", 1024,1024, 1024, [BF], b_view=(D_FF, chip_block, lambda j, k: (j, 1, 0)),
               epilogue=lambda acc, u: (acc * (2.0 * jnp.maximum(u.astype(F32), 0.0)),), extras=(up,))
    early_buf = _mm("g_mlp_down", act, dh2_b, "tn", 512, 1024, T, [F32],
                    into=(early_buf, (None, 512, PACK_W), lambda i, j: (i // 2, 2 + i % 2, 0)))
    early_buf = _mm("g_mlp_up", cn, d_up, "tn", 1024, 512, T, [F32],
                    into=(early_buf, (None, D_MODEL, 512), lambda i, j: (j // 2, 0, j % 2)))
    dh1, dh1_b, dg_mlp = _mm("d_mlp_up", d_up, mlp_all, "nt", 1024, 1024, 1024, [F32, BF], epilogue=add_norm_bwd,
                             b_view=(D_MODEL, chip_block, lambda j, k: (k, 0, 0)),
                             extras=(dh2, h1), consts=(gl,), sums=[D_MODEL])
    early_buf = _mm("g_out_proj", mixed, dh1_b, "tn", 1024, 1024, 1024, [F32],
                    into=(early_buf, (N_CHIPS, shard_rows, PACK_W), lambda i, j: (0, 2 * D_MODEL // shard_rows + 1, 0)))

    def gate_bwd(dm, gn, gd, un, ud):
        gn, gd, un, ud = (t.astype(F32) for t in (gn, gd, un, ud))
        sn, sd = _sigmoid(gn), _sigmoid(gd)
        return jnp.concatenate([dm * un * sn * (1.0 - sn), dm * ud * sd * (1.0 - sd)], axis=1), dm * sn, dm * sd

    dz_gates, d_u_na, d_u_dil = _mm("d_out_proj", dh1_b, W["w_out"], "nt", 512, 1024, 1024, [(BF, 2 * D_MODEL), BF, BF],
                                    epilogue=gate_bwd, extras=((z_gates, 0), (z_gates, 1), u_na, u_dil))
    g_branch_na = _mm("g_branch_na", y_na, d_u_na, "tn", 1024, 1024, 1024,[F32])
    g_branch_dil = _mm("g_branch_dil", y_dil, d_u_dil, "tn", 256, 1024, 1024,[F32])
    small_rows = [jnp.concatenate([_pack_rows(g[:, s * shard_rows:(s + 1) * shard_rows]) for g in (g_ple_proj, g_branch_na, g_branch_dil)],
                                  axis=0) for s in range(N_CHIPS)]
    early_buf = lax.dynamic_update_slice(early_buf, jnp.stack(small_rows), (0, 2 * D_MODEL + 2 * shard_rows, 0))
    early_tm = early_rows // 4
    swap_flight = _swap_start(early_buf, "early")
    d_y_na = _mm("d_branch_na", d_u_na, W["w_branch_na"], "nt", 1024,512, 1024, [BF], after=(swap_flight.token,))
    d_y_dil = _mm("d_branch_dil", d_u_dil, W["w_branch_dil"], "nt", 1024,256, 1024, [F32])

    dqa, dka, dva, dtab = _na_bwd(qkv, tab, d_y_na)
    early_g, early_got = _swap_wait(swap_flight, dqa, "early")
    early_pair, early_pair_b = _pair_sum(early_g, early_got, early_tm)
    scatter_flight = _scatter_start(early_pair_b, "early")

    do_res, dlse_res = _dil_merge_bwd(d_y_dil, o_nat, w_grp, TM, after=(scatter_flight.token,))
    d_dil = [_band_bwd(*dil_ops[g], do_res[g], dlse_res[g], g) for g in range(len(DIL_GROUPS))]

    dz_qkv = _qkv_unprep((dqa, dka, dva), d_dil, cos2, sin_signed, TM)
    in_cols = shards["w_in"].shape[1]
    qkv_rows, gate_tm = dz_qkv.shape[1], 256
    assert qkv_rows % gate_tm == 0
    g_in = _mm("g_in_qkv", dz_qkv, a, "tn", 640, 1024, T, [F32],
               into=(jax.ShapeDtypeStruct((N_CHIPS * in_cols, D_MODEL), F32), (640, D_MODEL), lambda i, j: (i, 0)))
    g_in = _mm("g_in_gates", dz_gates, a, "tn", gate_tm, 1024, T, [F32],
               into=(g_in, (gate_tm, D_MODEL), lambda i, j: (qkv_rows // gate_tm + i, 0)))
    early_mine = _chip_sum(early_pair, _scatter_wait(scatter_flight, (g_in,), "early"), early_tm)
    join_flight = _join_start(early_mine)

    late_tm = in_cols // 4
    late_swap = _swap_start(g_in.reshape(N_CHIPS, in_cols, D_MODEL), "late")
    d_a = _mm("d_in_qkv", dz_qkv, W["w_in"], "nt", 512, 1024, qkv_rows, [F32], after=(late_swap.token, join_flight.token),
              b_view=(D_MODEL, (D_MODEL, qkv_rows), lambda j, k: (j, k)))
    late_g, late_got = _swap_wait(late_swap, d_a, "late")
    late_pair, late_pair_b = _pair_sum(late_g, late_got, late_tm)
    late_scatter = _scatter_start(late_pair_b, "late")
    d_rpb = _na_rpb_grad(dtab, after=(late_scatter.token,))[:, :2 * NA_WIN_ROWS - 1, :2 * NA_WIN_COLS - 1]
    def first_bwd(dn_gates, dn_qkv, dh_out, h, g):
        dh, dg = _rms_bwd(dn_gates + dn_qkv, h, g)
        return dh_out + dh, dg

    grad_x, dg_mix = _mm("d_in_gates", dz_gates, w_gates, "nt", 512, 1024, 2048, [F32], epilogue=first_bwd,
                         extras=(d_a, dh1, xs), consts=(gm,), sums=[D_MODEL], after=(late_scatter.token,))
    early_shard = _join_wait(join_flight, grad_x)

    n_rpb = rpb.size
    rpb_rows = 4
    small = jnp.concatenate([
        dg_mix, dg_mlp, dg_ple, dg_final,
        jnp.pad(d_rpb.reshape(-1), (0, rpb_rows * D_MODEL - n_rpb)).reshape(rpb_rows, D_MODEL),
        jnp.pad(loss_part, ((0, 0), (0, D_MODEL - loss_part.shape[1]))),
        jnp.zeros((SMALL_ROWS - 5 - rpb_rows, D_MODEL), F32)], axis=0)
    out = {"grad": {}, "delta": {}, "new_m": {}, "new_v": {}}

    def update(n, g, row0):
        res = _adamw("adamw_" + n, g, row0, params[n], params["m_" + n], params["v_" + n])
        for kind, t in zip(("grad", "delta", "new_m", "new_v"), res, strict=True):
            out[kind][n] = t

    row0 = 0
    for n in REDUCE_EARLY:
        rows, cols = early_shapes[n]
        n_rows = rows * cols // PACK_W
        if cols == PACK_W:
            update(n, early_shard, row0)
        else:
            update(n, early_shard[row0:row0 + n_rows].reshape(rows, cols), 0)
        row0 += n_rows
    late_others = _scatter_wait(late_scatter, (*[out["new_v"][n] for n in REDUCE_EARLY], d_rpb), "late")
    late_mine = _chip_sum(late_pair, late_others, late_tm)
    small_flight = _allreduce_start(small, after=(late_mine,))
    res = _adamw("adamw_w_in", _join_halves(late_mine, after=(small_flight.token,)), 0,
                 *[jnp.swapaxes(params[n], 1, 2) for n in ("w_in", "m_w_in", "v_w_in")])
    small = _allreduce_finish(small_flight, res[3])
    for kind, t in zip(("grad", "delta", "new_m", "new_v"), res, strict=True):
        out[kind]["w_in"] = jnp.swapaxes(t, 1, 2)
    loss = small[4 + rpb_rows, 0]

    def small_pack(a0, a1, a2, a3, r):
        return jnp.concatenate([a0.reshape(1, -1), a1.reshape(1, -1), a2.reshape(1, -1), a3.reshape(1, -1),
                                jnp.pad(r.reshape(-1), (0, rpb_rows * D_MODEL - n_rpb)).reshape(rpb_rows, D_MODEL)], axis=0)

    small_res = _adamw("adamw_small", small, 0, small_pack(g_mix, g_mlp, g_ple, g_final, rpb)[None],
                       small_pack(m_g_mix, m_g_mlp, m_g_ple, m_g_final, m_rpb)[None],
                       small_pack(v_g_mix, v_g_mlp, v_g_ple, v_g_final, v_rpb)[None])

    def small_unpack(t):
        return {"g_mix": t[0].reshape(g_mix.shape), "g_mlp": t[1].reshape(g_mlp.shape), "g_ple": t[2].reshape(g_ple.shape),
                "g_final": t[3].reshape(g_final.shape), "rpb": t[4:].reshape(-1)[:n_rpb].reshape(rpb.shape)}

    for kind, t in zip(("grad", "delta", "new_m", "new_v"), small_res, strict=True):
        out[kind].update(small_unpack(t[0]))

    order = ["g_mix", "w_in", "rpb", "w_branch_na", "w_branch_dil", "w_out", "g_mlp", "w_up", "w_down", "g_ple",
             "w_ple_gate", "w_ple_proj", "g_final"]
    return (loss, grad_x[None], *[out["grad"][n] for n in order], *[out["delta"][n] for n in order],
            *[out["new_m"][n] for n in order], *[out["new_v"][n] for n in order])
```

```python
import functools
from typing import NamedTuple

import jax
import jax.numpy as jnp
from jax import lax
from jax.experimental import pallas as pl
from jax.experimental.pallas import tpu as pltpu

BF = jnp.bfloat16
F32 = jnp.float32
MESH = pl.DeviceIdType.MESH
ANY = pl.BlockSpec(memory_space=pl.ANY)

V7X_VMEM_BYTES = 64 * 1024 * 1024
VMEM_LIMIT = V7X_VMEM_BYTES - 16 * 1024 * 1024

D_MODEL = 1024
HEAD_DIM = 64
GRID_W = 64
NA_HEADS = 8
NA_WIN_ROWS = 8
NA_WIN_COLS = 16
NA_WIDTH = NA_HEADS * HEAD_DIM
DIL_GROUPS = ((128, 1), (512, 4), (2048, 16))
DIL_HPG = 4
DIL_HEADS = DIL_HPG * len(DIL_GROUPS)
DIL_WIDTH = DIL_HEADS * HEAD_DIM
DIL_OUT_WIDTH = DIL_HPG * HEAD_DIM
DIL_RADIUS = 64
QKV_WIDTH = 3 * NA_WIDTH + 3 * DIL_WIDTH
D_FF = 4 * D_MODEL
PLE_DIM = 256
ROPE_THETA = 10000.0
RMS_EPS = 1e-6
NEG_INF = -1e30
Q_SCALE = HEAD_DIM ** -0.5

ADAM_LR = 0.001
ADAM_B1 = 0.9
ADAM_B2 = 0.999
ADAM_EPS = 1e-08
ADAM_WD = 0.01
ADAM_STEP = 10

N_CHIPS = 4
N_DEV = 8
PACK_W = 1024
GATHER_MIX = ("w_branch_na", "w_branch_dil", "w_out")
GATHER_MLP = ("w_up", "w_down", "w_ple_gate", "w_ple_proj")
REDUCE_EARLY = ("w_up", "w_down", "w_ple_gate", "w_out", "w_ple_proj", "w_branch_na", "w_branch_dil")
SMALL_ROWS = 16


def _cparams(sem=None):
    return pltpu.CompilerParams(dimension_semantics=sem, vmem_limit_bytes=VMEM_LIMIT)


def _mm(name, a, b, mode, tm, tn, tk, out_dtypes, epilogue=None, extras=(), consts=(), sums=(), after=(), into=None,
        b_view=None):
    if mode == "nn":
        (M, K), N = a.shape, b.shape[1]
    elif mode == "nt":
        (M, K), N = a.shape, b.shape[0]
    else:
        (K, M), N = a.shape, b.shape[1]
    if b_view is not None:
        N = b_view[0]
    tm, tn, tk = min(tm, M), min(tn, N), min(tk, K)
    assert M % tm == 0 and N % tn == 0 and K % tk == 0, (name, M, N, K, tm, tn, tk)
    if mode == "nn":
        a_spec = pl.BlockSpec((tm, tk), lambda i, j, k: (i, k))
        b_spec = pl.BlockSpec((tk, tn), lambda i, j, k: (k, j))
        dims = (((1,), (0,)), ((), ()))
    elif mode == "nt":
        a_spec = pl.BlockSpec((tm, tk), lambda i, j, k: (i, k))
        b_spec = pl.BlockSpec((tn, tk), lambda i, j, k: (j, k))
        dims = (((1,), (1,)), ((), ()))
    else:
        a_spec = pl.BlockSpec((tk, tm), lambda i, j, k: (k, i))
        b_spec = pl.BlockSpec((tk, tn), lambda i, j, k: (k, j))
        dims = (((0,), (0,)), ((), ()))
    if b_view is not None:
        b_spec = pl.BlockSpec(b_view[1], lambda i, j, k: b_view[2](j, k))
    nk = K // tk
    n_extra, n_const, n_out, n_sum = len(extras), len(consts), len(out_dtypes), len(sums)
    tile = pl.BlockSpec((tm, tn), lambda i, j, k: (i, j))
    assert not sums or tn == N, "row sums need whole rows in a tile"
    wide = [e for e in (*extras, *out_dtypes) if isinstance(e, tuple)]
    assert not wide or tn == N
    extra_specs = [pl.BlockSpec((tm, tn), functools.partial(lambda c, i, j, k: (i, c), e[1])) if isinstance(e, tuple) else tile
                   for e in extras]
    extras = [e[0] if isinstance(e, tuple) else e for e in extras]
    out_widths = [d[1] if isinstance(d, tuple) else N for d in out_dtypes]
    out_dtypes = [d[0] if isinstance(d, tuple) else d for d in out_dtypes]

    n_after = len(after)

    def body(a_ref, b_ref, *rest):
        extra_refs, rest = rest[:n_extra + n_const], rest[n_extra + n_const + n_after:]
        out_refs, sum_refs, acc = rest[:n_out], rest[n_out:n_out + n_sum], rest[-1]
        i, k = pl.program_id(0), pl.program_id(2)
        def product():
            if len(b_ref.shape) == 3:
                w = tk // b_ref.shape[0]
                parts = [lax.dot_general(a_ref[:, s * w:(s + 1) * w].astype(BF), b_ref[s].astype(BF), dims, preferred_element_type=F32)
                         for s in range(b_ref.shape[0])]
                return functools.reduce(lambda x, y: x + y, parts)
            return lax.dot_general(a_ref[...].astype(BF), b_ref[...].astype(BF), dims, preferred_element_type=F32)

        if nk > 1:
            @pl.when(k == 0)
            def _():
                acc[...] = jnp.zeros_like(acc)

            acc[...] += product()

        @pl.when(k == nk - 1)
        def _():
            total = product() if nk == 1 else acc[...]
            outs = (total,) if epilogue is None else epilogue(total, *[e[...] for e in extra_refs])
            for o_ref, val in zip(out_refs, outs[:n_out], strict=True):
                o_ref[...] = val.astype(o_ref.dtype).reshape(o_ref.shape)
            for s_ref, val in zip(sum_refs, outs[n_out:], strict=True):
                @pl.when(i == 0)
                def _():
                    s_ref[...] = val

                @pl.when(i != 0)
                def _():
                    s_ref[...] += val

    out_specs = ([tile if w == N else pl.BlockSpec((tm, w), lambda i, j, k: (i, 0)) for w in out_widths]
                 + [pl.BlockSpec((1, c), lambda i, j, k: (0, 0)) for c in sums])
    out_shape = ([jax.ShapeDtypeStruct((M, w), dt) for dt, w in zip(out_dtypes, out_widths, strict=True)]
                 + [jax.ShapeDtypeStruct((1, c), F32) for c in sums])
    operands, aliases = [a, b, *extras, *consts, *after], {}
    in_specs = ([a_spec, b_spec] + extra_specs
                + [pl.BlockSpec(c.shape, functools.partial(lambda nd, i, j, k: (0,) * nd, c.ndim)) for c in consts] + [ANY] * n_after)
    if into is not None:
        assert n_out == 1
        target, block, index = into
        out_specs = [pl.BlockSpec(block, lambda i, j, k: index(i, j))]
        out_shape = [jax.ShapeDtypeStruct(target.shape, target.dtype)]
        if not isinstance(target, jax.ShapeDtypeStruct):
            aliases = {len(operands): 0}
            operands.append(target)
            in_specs.append(ANY)
            n_after += 1

    outs = pl.pallas_call(
        body, name=name, grid=(M // tm, N // tn, nk),
        in_specs=in_specs, out_specs=out_specs, out_shape=out_shape,
        scratch_shapes=[pltpu.VMEM((tm, tn) if nk > 1 else (8, 128), F32)], input_output_aliases=aliases,
        compiler_params=_cparams(("arbitrary",) * 3 if sums else ("parallel", "parallel", "arbitrary")),
    )(*operands)
    return outs[0] if len(outs) == 1 else outs


def _row(arr, tm, col_block=None, width=None):
    width = arr.shape[1] if width is None else width
    cb = 0 if col_block is None else col_block
    return arr, pl.BlockSpec((tm, width), lambda i: (i, cb))


def _full(arr):
    nd = arr.ndim
    return arr, pl.BlockSpec(arr.shape, lambda i: (0,) * nd)


def _rowwise(name, body, T, tm, ins, outs, sums=(), after=()):
    n_in, n_out, n_sum, n_after = len(ins), len(outs), len(sums), len(after)

    def kern(*refs):
        in_refs, refs = refs[:n_in], refs[n_in + n_after:]
        out_refs, sum_refs = refs[:n_out], refs[n_out:]
        res = body(*[r[...] for r in in_refs])
        res = res if isinstance(res, tuple) else (res,)
        for o_ref, val in zip(out_refs, res[:n_out], strict=True):
            o_ref[...] = val.astype(o_ref.dtype)
        if n_sum:
            @pl.when(pl.program_id(0) == 0)
            def _():
                for s_ref in sum_refs:
                    s_ref[...] = jnp.zeros_like(s_ref)

            for s_ref, val in zip(sum_refs, res[n_out:], strict=True):
                s_ref[...] += val

    res = pl.pallas_call(
        kern, name=name, grid=(T // tm,),
        in_specs=[spec for _, spec in ins] + [ANY] * n_after,
        out_specs=[pl.BlockSpec((tm, c), lambda i: (i, 0)) for c, _ in outs]
        + [pl.BlockSpec((1, c), lambda i: (0, 0)) for c in sums],
        out_shape=[jax.ShapeDtypeStruct((T, c), dt) for c, dt in outs]
        + [jax.ShapeDtypeStruct((1, c), F32) for c in sums],
        compiler_params=_cparams(("arbitrary",)),
    )(*[a for a, _ in ins], *after)
    return res[0] if len(res) == 1 else res


def _sigmoid(x):
    return 1.0 / (1.0 + jnp.exp(-x))


def _rms(h):
    return lax.rsqrt(jnp.mean(h * h, axis=-1, keepdims=True) + RMS_EPS)


def _rms_bwd(dy, h, g):
    r = _rms(h)
    n = h * r
    dn = dy * g
    dh = r * (dn - n * jnp.mean(dn * n, axis=-1, keepdims=True))
    return dh, jnp.sum(dy * n, axis=0, keepdims=True)


def _rope(x, cos2, sin_signed):
    lane = lax.broadcasted_iota(jnp.int32, x.shape, 1)
    swapped = jnp.where((lane % HEAD_DIM) < HEAD_DIM // 2, pltpu.roll(x, 128 - HEAD_DIM // 2, 1), pltpu.roll(x, HEAD_DIM // 2, 1))
    return x * cos2 + swapped * sin_signed


NA_KEYS = NA_WIN_ROWS * GRID_W
NA_BASES = 8


def _na_row_geometry(r, rows):
    first = jnp.clip(r - NA_WIN_ROWS // 2, 0, rows - NA_WIN_ROWS)
    base = first - r + (NA_WIN_ROWS - 1)
    return pl.multiple_of(first * GRID_W, GRID_W), base


NA_ROWS_PER_STEP = 16
NA_BWD_ROWS_PER_STEP = 8


def _softmax_rows(s):
    p = jnp.exp(s - jnp.max(s, axis=-1, keepdims=True))
    return p / jnp.sum(p, axis=-1, keepdims=True)


def _split_pair(t):
    first = lax.broadcasted_iota(jnp.int32, t.shape, 1) < HEAD_DIM
    zero = jnp.zeros_like(t)
    return jnp.where(first, t, zero), jnp.where(first, zero, t)


def _join_pair(a, b):
    return jnp.where(lax.broadcasted_iota(jnp.int32, a.shape, 1) < HEAD_DIM, a, b)


_NT = (((1,), (1,)), ((), ()))
_TN = (((0,), (0,)), ((), ()))


def _na_fwd(qkv, tab):
    T = qkv.shape[0]
    rows = T // GRID_W
    n_pairs = NA_WIDTH // 128

    def body(q_ref, k_ref, v_ref, tab_ref, y_ref):
        def step(it, carry):
            geo = [_na_row_geometry(it * NA_ROWS_PER_STEP + u, rows) for u in range(NA_ROWS_PER_STEP)]
            q0s = [pl.multiple_of((it * NA_ROWS_PER_STEP + u) * GRID_W, GRID_W) for u in range(NA_ROWS_PER_STEP)]
            ss = [lax.dot_general(jnp.concatenate(_split_pair(q_ref[pl.ds(q0, GRID_W), :] * Q_SCALE), axis=0),
                                  k_ref[pl.ds(k0, NA_KEYS), :], _NT, preferred_element_type=F32)
                  for q0, (k0, _) in zip(q0s, geo)]
            ps = [_softmax_rows(s + jnp.concatenate([tab_ref[0, base], tab_ref[1, base]], axis=0)) for s, (_, base) in zip(ss, geo)]
            ys = [jnp.dot(p.astype(BF), v_ref[pl.ds(k0, NA_KEYS), :], preferred_element_type=F32) for p, (k0, _) in zip(ps, geo)]
            for q0, y2 in zip(q0s, ys):
                y_ref[pl.ds(q0, GRID_W), :] = _join_pair(y2[:GRID_W], y2[GRID_W:]).astype(y_ref.dtype)
            return carry

        lax.fori_loop(0, rows // NA_ROWS_PER_STEP, step, 0)

    def cols(first):
        return pl.BlockSpec((T, 128), lambda j: (0, first + j))

    return pl.pallas_call(
        body, name="na_fwd", grid=(n_pairs,),
        in_specs=[cols(0), cols(n_pairs), cols(2 * n_pairs), pl.BlockSpec((2, NA_BASES, GRID_W, NA_KEYS), lambda j: (j, 0, 0, 0))],
        out_specs=cols(0), out_shape=jax.ShapeDtypeStruct((T, NA_WIDTH), BF),
        compiler_params=_cparams(("parallel",)),
    )(qkv, qkv, qkv, tab)


def _na_bwd(qkv, tab, do):
    T = qkv.shape[0]
    rows = T // GRID_W
    n_pairs = NA_WIDTH // 128

    def body(q_ref, k_ref, v_ref, tab_ref, do_ref, dq_ref, dk_out, dv_out, dtab_ref, dk_ref, dv_ref):
        dk_ref[...] = jnp.zeros_like(dk_ref)
        dv_ref[...] = jnp.zeros_like(dv_ref)
        dtab_ref[...] = jnp.zeros_like(dtab_ref)

        def step(it, carry):
            U = NA_BWD_ROWS_PER_STEP
            geo = [_na_row_geometry(it * U + u, rows) for u in range(U)]
            q0s = [pl.multiple_of((it * U + u) * GRID_W, GRID_W) for u in range(U)]
            q2s = [jnp.concatenate(_split_pair(q_ref[pl.ds(q0, GRID_W), :] * Q_SCALE), axis=0) for q0 in q0s]
            do2s = [jnp.concatenate(_split_pair(do_ref[pl.ds(q0, GRID_W), :]), axis=0) for q0 in q0s]
            ss = [lax.dot_general(q2, k_ref[pl.ds(k0, NA_KEYS), :], _NT, preferred_element_type=F32) for q2, (k0, _) in zip(q2s, geo)]
            dps = [lax.dot_general(do2, v_ref[pl.ds(k0, NA_KEYS), :], _NT, preferred_element_type=F32) for do2, (k0, _) in zip(do2s, geo)]
            ps = [_softmax_rows(s + jnp.concatenate([tab_ref[0, base], tab_ref[1, base]], axis=0)) for s, (_, base) in zip(ss, geo)]
            dss = [p * (dp - jnp.sum(dp * p, axis=-1, keepdims=True)) for p, dp in zip(ps, dps)]
            dvs = [lax.dot_general(p.astype(BF), do2, _TN, preferred_element_type=F32) for p, do2 in zip(ps, do2s)]
            dsbs = [ds.astype(BF) for ds in dss]
            dqs = [jnp.dot(dsb, k_ref[pl.ds(k0, NA_KEYS), :], preferred_element_type=F32) for dsb, (k0, _) in zip(dsbs, geo)]
            dks = [lax.dot_general(dsb, q2, _TN, preferred_element_type=F32) for dsb, q2 in zip(dsbs, q2s)]
            for u in range(U):
                k0, base = geo[u]
                dtab_ref[0, base] += dss[u][:GRID_W]
                dtab_ref[1, base] += dss[u][GRID_W:]
                dq_ref[pl.ds(q0s[u], GRID_W), :] = (_join_pair(dqs[u][:GRID_W], dqs[u][GRID_W:]) * Q_SCALE).astype(dq_ref.dtype)
                dk_ref[pl.ds(k0, NA_KEYS), :] += dks[u]
                dv_ref[pl.ds(k0, NA_KEYS), :] += dvs[u]
            return carry

        lax.fori_loop(0, rows // NA_BWD_ROWS_PER_STEP, step, 0)
        dk_out[...] = dk_ref[...].astype(dk_out.dtype)
        dv_out[...] = dv_ref[...].astype(dv_out.dtype)

    def cols(first):
        return pl.BlockSpec((T, 128), lambda j: (0, first + j))

    tabs = pl.BlockSpec((2, NA_BASES, GRID_W, NA_KEYS), lambda j: (j, 0, 0, 0))
    wide = jax.ShapeDtypeStruct((T, NA_WIDTH), BF)
    return pl.pallas_call(
        body, name="na_bwd", grid=(n_pairs,),
        in_specs=[cols(0), cols(n_pairs), cols(2 * n_pairs), tabs, cols(0)],
        out_specs=[cols(0), cols(0), cols(0), tabs],
        out_shape=[wide, wide, wide, jax.ShapeDtypeStruct((NA_HEADS, NA_BASES, GRID_W, NA_KEYS), F32)],
        scratch_shapes=[pltpu.VMEM((T, 128), F32), pltpu.VMEM((T, 128), F32)],
        compiler_params=_cparams(("parallel",)),
    )(qkv, qkv, qkv, tab, do)


def _na_bias_table(rpb):
    H, n_rows, n_cols = rpb.shape

    def body(r_ref, tab_ref):
        q = lax.broadcasted_iota(jnp.int32, (GRID_W, 128), 0)
        kc = lax.broadcasted_iota(jnp.int32, (GRID_W, 128), 1)
        first = jnp.clip(q - NA_WIN_COLS // 2, 0, GRID_W - NA_WIN_COLS)
        valid = (kc >= first) & (kc < first + NA_WIN_COLS)
        toeplitz = []
        for ro in range(n_rows):
            row = jnp.broadcast_to(r_ref[pl.ds(ro, 1), :], (GRID_W, 128))
            shifted = pltpu.roll(pltpu.roll(row, 128 - (NA_WIN_COLS - 1), 1), 0, 1, stride=1, stride_axis=0)
            toeplitz.append(jnp.where(valid, shifted, NEG_INF))
        for base in range(NA_BASES):
            for j in range(NA_WIN_ROWS // 2):
                even, odd = toeplitz[base + 2 * j], toeplitz[base + 2 * j + 1]
                tab_ref[base, :, pl.ds(j * 128, 128)] = jnp.where(kc < GRID_W, even, pltpu.roll(odd, GRID_W, 1))

    padded = jnp.pad(rpb, ((0, 0), (0, 16 - n_rows), (0, 128 - n_cols)))
    return pl.pallas_call(
        body, name="na_bias_table", grid=(H,),
        in_specs=[pl.BlockSpec((None, 16, 128), lambda h: (h, 0, 0))],
        out_specs=pl.BlockSpec((None, NA_BASES, GRID_W, NA_KEYS), lambda h: (h, 0, 0, 0)),
        out_shape=jax.ShapeDtypeStruct((H, NA_BASES, GRID_W, NA_KEYS), F32),
        compiler_params=_cparams(("parallel",)),
    )(padded)


def _na_rpb_grad(dtab, after=()):
    H = dtab.shape[0]
    n_rows = 2 * NA_WIN_ROWS - 1
    n_cols = 2 * NA_WIN_COLS - 1

    def body(d_ref, *rest):
        o_ref = rest[-1]
        lane = lax.broadcasted_iota(jnp.int32, (GRID_W, 128), 1)
        low = lane < GRID_W
        flip = (lax.broadcasted_iota(jnp.int32, (GRID_W, GRID_W), 0) + lax.broadcasted_iota(jnp.int32, (GRID_W, GRID_W), 1)
                == GRID_W - 1).astype(BF)

        def reverse_rows(t):
            out = jnp.zeros_like(t)
            for _ in range(3):
                piece = t.astype(BF)
                out = out + jnp.dot(flip, piece, preferred_element_type=F32)
                t = t - piece.astype(F32)
            return out

        out_rows = []
        for ro in range(n_rows):
            acc = jnp.zeros((GRID_W, 128), F32)
            for base in range(NA_BASES):
                i = ro - base
                if not 0 <= i < NA_WIN_ROWS:
                    continue
                pair = d_ref[base, :, pl.ds((i // 2) * 128, 128)]
                if i % 2:
                    pair = pltpu.roll(pair, GRID_W, 1)
                acc = acc + jnp.where(low, pair, 0.0)
            skew = pltpu.roll(reverse_rows(acc), 0, 1, stride=1, stride_axis=0)
            diag = jnp.sum(skew, axis=0, keepdims=True)
            out_rows.append(pltpu.roll(jnp.broadcast_to(diag, (8, 128)), 128 - (GRID_W - NA_WIN_COLS), 1)[:1])
        out_rows.append(jnp.zeros((1, 128), F32))
        res = jnp.concatenate(out_rows, axis=0)
        o_ref[...] = jnp.where(lax.broadcasted_iota(jnp.int32, res.shape, 1) < n_cols, res, 0.0)

    return pl.pallas_call(
        body, name="na_rpb_grad", grid=(H,),
        in_specs=[pl.BlockSpec((None, NA_BASES, GRID_W, NA_KEYS), lambda h: (h, 0, 0, 0))] + [ANY] * len(after),
        out_specs=pl.BlockSpec((None, n_rows + 1, 128), lambda h: (h, 0, 0)),
        out_shape=jax.ShapeDtypeStruct((H, n_rows + 1, 128), F32),
        compiler_params=_cparams(("parallel",)),
    )(dtab, *after)


BAND_Q = 128
BAND_KEYS = BAND_Q + 2 * DIL_RADIUS


def _band_geometry(n, L):
    q0 = pl.multiple_of(n * BAND_Q, BAND_Q)
    k0 = pl.multiple_of(jnp.clip(q0 - DIL_RADIUS, 0, L - BAND_KEYS), DIL_RADIUS)
    qi = q0 + lax.broadcasted_iota(jnp.int32, (BAND_Q, BAND_KEYS), 0)
    kj = k0 + lax.broadcasted_iota(jnp.int32, (BAND_Q, BAND_KEYS), 1)
    return q0, k0, jnp.abs(qi - kj) <= DIL_RADIUS


DIL_PAIRS = DIL_OUT_WIDTH // 128


def _residue_shape(dil, T, dtype):
    return jax.ShapeDtypeStruct((DIL_PAIRS, dil, T // dil, 128), dtype)


def _residue_tile(dil, tm):
    return pl.BlockSpec((DIL_PAIRS, dil, tm // dil, 128), lambda i: (0, 0, i, 0))


def _to_natural(ref, scratch, dil, tm):
    tiles = []
    for pair in range(DIL_PAIRS):
        if dil == 1:
            tiles.append(ref[pair, 0].astype(F32))
            continue
        for r in range(dil):
            scratch[pl.ds(r, tm // dil, stride=dil), :] = ref[pair, r].astype(F32)
        tiles.append(scratch[...])
    return tiles


def _from_natural(tile, scratch, ref, pair, dil, tm):
    if dil == 1:
        ref[pair, 0] = tile.astype(ref.dtype)
        return
    scratch[...] = tile
    for r in range(dil):
        ref[pair, r] = scratch[pl.ds(r, tm // dil, stride=dil), :].astype(ref.dtype)


def _band_specs(group, T):
    dil = DIL_GROUPS[group][1]
    L = T // dil
    assert L % BAND_Q == 0 and L >= BAND_KEYS, (T, dil)
    per_residue = min(BAND_BLOCKS_PER_STEP, L // BAND_Q)
    residues = min(dil, BAND_BLOCKS_PER_STEP // per_residue)
    spec = pl.BlockSpec((None, residues, L, 128), lambda s: (s % DIL_PAIRS, s // DIL_PAIRS, 0, 0))
    return L, residues, per_residue, (dil // residues * DIL_PAIRS,), spec


BAND_BLOCKS_PER_STEP = 8


def _band_softmax(s, valid):
    s = jnp.where(valid, s, NEG_INF)
    m = jnp.max(s, axis=-1, keepdims=True)
    p = jnp.exp(s - m)
    l = jnp.sum(p, axis=-1, keepdims=True)
    return p / l, m + jnp.log(l)


def _band_fwd(q, k, v, group):
    T = q.shape[1] * q.shape[2]
    L, residues, U, grid, spec = _band_specs(group, T)

    def body(q_ref, k_ref, v_ref, o_ref, lse_ref):
        def step(it, carry):
            geo = [(r, *_band_geometry(it * U + u, L)) for r in range(residues) for u in range(U)]
            ss = [lax.dot_general(jnp.concatenate(_split_pair(q_ref[r, pl.ds(q0, BAND_Q), :]), axis=0),
                                  k_ref[r, pl.ds(k0, BAND_KEYS), :], _NT, preferred_element_type=F32) for r, q0, k0, _ in geo]
            pls = [_band_softmax(s, jnp.concatenate([valid, valid], axis=0)) for s, (_, _, _, valid) in zip(ss, geo)]
            os = [jnp.dot(p.astype(BF), v_ref[r, pl.ds(k0, BAND_KEYS), :], preferred_element_type=F32)
                  for (p, _), (r, _, k0, _) in zip(pls, geo)]
            for (r, q0, _, _), o2, (_, lse) in zip(geo, os, pls):
                o_ref[r, pl.ds(q0, BAND_Q), :] = _join_pair(o2[:BAND_Q], o2[BAND_Q:])
                lse2 = jnp.broadcast_to(lse, (2 * BAND_Q, 128))
                lse_ref[r, pl.ds(q0, BAND_Q), :] = _join_pair(lse2[:BAND_Q], lse2[BAND_Q:])
            return carry

        lax.fori_loop(0, L // (BAND_Q * U), step, 0)

    res = _residue_shape(DIL_GROUPS[group][1], T, F32)
    return pl.pallas_call(
        body, name=f"band_fwd_g{group}", grid=grid,
        in_specs=[spec] * 3, out_specs=[spec] * 2, out_shape=[res, res],
        compiler_params=_cparams(("parallel",)),
    )(q, k, v)


def _band_bwd(q, k, v, do, dlse, group):
    T = q.shape[1] * q.shape[2]
    L, residues, U, grid, spec = _band_specs(group, T)

    def body(q_ref, k_ref, v_ref, do_ref, dlse_ref, dq_ref, dk_ref, dv_ref):
        dk_ref[...] = jnp.zeros_like(dk_ref)
        dv_ref[...] = jnp.zeros_like(dv_ref)

        def step(it, carry):
            geo = [(r, *_band_geometry(it * U + u, L)) for r in range(residues) for u in range(U)]
            q2s = [jnp.concatenate(_split_pair(q_ref[r, pl.ds(q0, BAND_Q), :]), axis=0) for r, q0, _, _ in geo]
            do2s = [jnp.concatenate(_split_pair(do_ref[r, pl.ds(q0, BAND_Q), :]), axis=0) for r, q0, _, _ in geo]
            ss = [lax.dot_general(q2, k_ref[r, pl.ds(k0, BAND_KEYS), :], _NT, preferred_element_type=F32)
                  for q2, (r, _, k0, _) in zip(q2s, geo)]
            dps = [lax.dot_general(do2, v_ref[r, pl.ds(k0, BAND_KEYS), :], _NT, preferred_element_type=F32)
                   for do2, (r, _, k0, _) in zip(do2s, geo)]
            ps = [_band_softmax(s, jnp.concatenate([valid, valid], axis=0))[0] for s, (_, _, _, valid) in zip(ss, geo)]
            dss = []
            for p, dp, (r, q0, _, _) in zip(ps, dps, geo):
                dl = dlse_ref[r, pl.ds(q0, BAND_Q), :]
                dl2 = jnp.concatenate([dl[:, :1], dl[:, HEAD_DIM:HEAD_DIM + 1]], axis=0)
                dss.append(p * (dp - jnp.sum(dp * p, axis=-1, keepdims=True) + dl2))
            dvs = [lax.dot_general(p.astype(BF), do2, _TN, preferred_element_type=F32) for p, do2 in zip(ps, do2s)]
            dsbs = [ds.astype(BF) for ds in dss]
            dqs = [jnp.dot(dsb, k_ref[r, pl.ds(k0, BAND_KEYS), :], preferred_element_type=F32) for dsb, (r, _, k0, _) in zip(dsbs, geo)]
            dks = [lax.dot_general(dsb, q2, _TN, preferred_element_type=F32) for dsb, q2 in zip(dsbs, q2s)]
            for u, (r, q0, k0, _) in enumerate(geo):
                dq_ref[r, pl.ds(q0, BAND_Q), :] = _join_pair(dqs[u][:BAND_Q], dqs[u][BAND_Q:])
                dk_ref[r, pl.ds(k0, BAND_KEYS), :] += dks[u]
                dv_ref[r, pl.ds(k0, BAND_KEYS), :] += dvs[u]
            return carry

        lax.fori_loop(0, L // (BAND_Q * U), step, 0)

    res = _residue_shape(DIL_GROUPS[group][1], T, F32)
    return pl.pallas_call(
        body, name=f"band_bwd_g{group}", grid=grid,
        in_specs=[spec] * 5, out_specs=[spec] * 3, out_shape=[res] * 3,
        compiler_params=_cparams(("parallel",)),
    )(q, k, v, do, dlse)


def _head_sums(t):
    head = lax.broadcasted_iota(jnp.int32, t.shape, 1) // HEAD_DIM
    out = jnp.zeros_like(t)
    for h in range(t.shape[1] // HEAD_DIM):
        mine = head == h
        out = jnp.where(mine, jnp.sum(jnp.where(mine, t, 0.0), axis=-1, keepdims=True), out)
    return out


def _dil_merge_fwd(os, lses, T, tm):
    G = len(DIL_GROUPS)
    W = DIL_OUT_WIDTH
    dils = [d for _, d in DIL_GROUPS]

    def body(*refs):
        o_refs, lse_refs = refs[:G], refs[G:2 * G]
        y_ref, w_refs, on_refs, scratch = refs[2 * G], refs[2 * G + 1:3 * G + 1], refs[3 * G + 1:4 * G + 1], refs[-1]
        o = [jnp.concatenate(_to_natural(r, scratch, d, tm), axis=1) for r, d in zip(o_refs, dils)]
        ls = [jnp.concatenate(_to_natural(r, scratch, d, tm), axis=1) for r, d in zip(lse_refs, dils)]
        m = functools.reduce(jnp.maximum, ls)
        es = [jnp.exp(l - m) for l in ls]
        tot = functools.reduce(jnp.add, es)
        ws = [e / tot for e in es]
        y_ref[...] = functools.reduce(jnp.add, [w * t for w, t in zip(ws, o)]).astype(y_ref.dtype)
        for g in range(G):
            w_refs[g][...] = ws[g]
            on_refs[g][...] = o[g]

    nat = pl.BlockSpec((tm, W), lambda i: (i, 0))
    res = pl.pallas_call(
        body, name="dil_merge_fwd", grid=(T // tm,),
        in_specs=[_residue_tile(d, tm) for d in dils] * 2,
        out_specs=[nat] * (2 * G + 1),
        out_shape=[jax.ShapeDtypeStruct((T, W), BF)] + [jax.ShapeDtypeStruct((T, W), F32)] * (2 * G),
        scratch_shapes=[pltpu.VMEM((tm, 128), F32)],
        compiler_params=_cparams(("parallel",)),
    )(*os, *lses)
    return res[0], res[1:G + 1], res[G + 1:]


def _dil_merge_bwd(dy, os, ws, tm, after=()):
    G = len(DIL_GROUPS)
    T, W = dy.shape
    dils = [d for _, d in DIL_GROUPS]
    n_after = len(after)

    def body(*refs):
        dyt = refs[0][...]
        o, w = [r[...] for r in refs[1:G + 1]], [r[...] for r in refs[G + 1:2 * G + 1]]
        refs = refs[2 * G + 1 + n_after:]
        do_refs, dlse_refs, scratch = refs[:G], refs[G:2 * G], refs[-1]
        dws = [_head_sums(dyt * t) for t in o]
        mean = functools.reduce(jnp.add, [a * b for a, b in zip(w, dws)])
        for g, d in enumerate(dils):
            do, dlse = w[g] * dyt, w[g] * (dws[g] - mean)
            for pair in range(DIL_PAIRS):
                cols = slice(pair * 128, (pair + 1) * 128)
                _from_natural(do[:, cols], scratch, do_refs[g], pair, d, tm)
                _from_natural(dlse[:, cols], scratch, dlse_refs[g], pair, d, tm)

    nat = pl.BlockSpec((tm, W), lambda i: (i, 0))
    res = pl.pallas_call(
        body, name="dil_merge_bwd", grid=(T // tm,),
        in_specs=[nat] * (2 * G + 1) + [ANY] * n_after,
        out_specs=[_residue_tile(d, tm) for d in dils] * 2,
        out_shape=[_residue_shape(d, T, BF) for d in dils] + [_residue_shape(d, T, F32) for d in dils],
        scratch_shapes=[pltpu.VMEM((tm, 128), F32)],
        compiler_params=_cparams(("parallel",)),
    )(dy, *os, *ws, *after)
    return res[:G], res[G:]


def _qkv_prep(z, cos2, sin_signed, tm):
    T = z.shape[0]
    G = len(DIL_GROUPS)
    dils = [d for _, d in DIL_GROUPS]
    n_dil_blocks = 3 * DIL_WIDTH // 128

    def body(*refs):
        blocks = refs[:n_dil_blocks]
        cos_ref, sin_ref = refs[n_dil_blocks], refs[1 + n_dil_blocks]
        outs = refs[2 + n_dil_blocks:]
        for part in range(3):
            for g, d in enumerate(dils):
                out = outs[g * 3 + part]
                for pair in range(DIL_PAIRS):
                    blk = blocks[part * (DIL_WIDTH // 128) + g * DIL_PAIRS + pair]
                    for r in range(d):
                        rows = pl.ds(r, tm // d, stride=d) if d > 1 else slice(None)
                        x = blk[rows, :]
                        if part < 2:
                            x = _rope(x, cos_ref[rows, :], sin_ref[rows, :])
                        if part == 0:
                            x = x * Q_SCALE
                        out[pair, r] = x.astype(out.dtype)

    lane_block = [pl.BlockSpec((tm, 128), functools.partial(lambda b, i: (i, b), b)) for b in range(n_dil_blocks)]
    tab = pl.BlockSpec((tm, 128), lambda i: (i, 0))
    res = pl.pallas_call(
        body, name="qkv_prep", grid=(T // tm,),
        in_specs=lane_block + [tab, tab],
        out_specs=[_residue_tile(d, tm) for d in dils for _ in range(3)],
        out_shape=[_residue_shape(d, T, BF) for d in dils for _ in range(3)],
        compiler_params=_cparams(("parallel",)),
    )(*[z] * n_dil_blocks, cos2, sin_signed)
    return [res[3 * g:3 + 3 * g] for g in range(G)]


def _qkv_unprep(d_na, d_dil, cos2, sin_signed, tm, after=()):
    T = d_na[0].shape[0]
    G = len(DIL_GROUPS)
    dils = [d for _, d in DIL_GROUPS]
    n_after = len(after)

    def body(*refs):
        dq, dk, dv = (r[...] for r in refs[:3])
        res_refs = refs[3:3 + 3 * G]
        cs, sn = refs[3 + 3 * G][...], refs[4 + 3 * G][...]
        out, scratch = refs[5 + 3 * G + n_after], refs[-1]
        cols = [dq, dk, dv]
        for part in range(3):
            for g, d in enumerate(dils):
                for x in _to_natural(res_refs[g * 3 + part], scratch, d, tm):
                    if part < 2:
                        x = _rope(x, cs, -sn)
                    cols.append((x * Q_SCALE if part == 0 else x).astype(out.dtype))
        out[...] = jnp.concatenate(cols, axis=1)

    wide = pl.BlockSpec((tm, NA_WIDTH), lambda i: (i, 0))
    tab = pl.BlockSpec((tm, 128), lambda i: (i, 0))
    return pl.pallas_call(
        body, name="qkv_unprep", grid=(T // tm,),
        in_specs=[wide] * 3 + [_residue_tile(d, tm) for d in dils for _ in range(3)] + [tab, tab] + [ANY] * n_after,
        out_specs=pl.BlockSpec((tm, QKV_WIDTH), lambda i: (i, 0)),
        out_shape=jax.ShapeDtypeStruct((T, QKV_WIDTH), BF),
        scratch_shapes=[pltpu.VMEM((tm, 128), F32)],
        compiler_params=_cparams(("parallel",)),
    )(*d_na, *[t for g in range(G) for t in d_dil[g]], cos2, sin_signed, *after)


def _rope_tables(positions):
    half = HEAD_DIM // 2
    inv_freq = ROPE_THETA ** (-jnp.arange(half, dtype=F32) / half)
    ang = positions.astype(F32)[:, None] * inv_freq
    cos, sin = jnp.cos(ang), jnp.sin(ang)
    return jnp.tile(jnp.concatenate([cos, cos], axis=1), (1, 2)), jnp.tile(jnp.concatenate([-sin, sin], axis=1), (1, 2))


def _pack_rows(t):
    return t.reshape(-1, PACK_W)


def _me():
    return lax.axis_index("x"), lax.axis_index("y"), lax.axis_index("c")


def _other_chips(x, y):
    return [(1 - x, y), (x, 1 - y), (1 - x, 1 - y)]


def _pair_sum(g, got, tm):
    S, R, W = g.shape
    half = R // 2
    nb = half // tm

    def body(pos_ref, g_ref, got_ref, own_ref, ob_ref):
        tot = g_ref[...] + got_ref[...]
        ob_ref[...] = tot.astype(ob_ref.dtype)

        @pl.when(pl.program_id(1) == pos_ref[1])
        def _():
            own_ref[...] = tot

    tile = pl.BlockSpec((None, tm, W), lambda i, s, pos_ref: (s, i, 0))
    c, chip = lax.axis_index("c"), 2 * lax.axis_index("x") + lax.axis_index("y")
    return pl.pallas_call(
        body, name="pair_sum",
        grid_spec=pltpu.PrefetchScalarGridSpec(
            num_scalar_prefetch=1, grid=(nb, S),
            in_specs=[pl.BlockSpec((None, tm, W), lambda i, s, pos_ref: (s, pos_ref[0] * nb + i, 0)), tile],
            out_specs=[pl.BlockSpec((tm, W), lambda i, s, pos_ref: (i, 0)), tile]),
        out_shape=[jax.ShapeDtypeStruct((half, W), F32), jax.ShapeDtypeStruct((S, half, W), BF)],
        compiler_params=_cparams(("parallel", "arbitrary")),
    )(jnp.stack([c, chip]).astype(jnp.int32), g, got)


def _chip_sum(own, others, tm):
    n, h, W = others.shape
    nb = h // tm

    def body(c_ref, own_ref, p_ref, o_ref):
        o_ref[...] = ((own_ref[...] + p_ref[0].astype(F32)) + p_ref[1].astype(F32)) + p_ref[2].astype(F32)

    return pl.pallas_call(
        body, name="chip_sum",
        grid_spec=pltpu.PrefetchScalarGridSpec(
            num_scalar_prefetch=1, grid=(nb,),
            in_specs=[pl.BlockSpec((tm, W), lambda i, c_ref: (i, 0)), pl.BlockSpec((n, tm, W), lambda i, c_ref: (0, i, 0))],
            out_specs=pl.BlockSpec((tm, W), lambda i, c_ref: (c_ref[0] * nb + i, 0))),
        out_shape=jax.ShapeDtypeStruct((2 * h, W), F32),
        compiler_params=_cparams(("parallel",)),
    )(lax.axis_index("c").reshape(1).astype(jnp.int32), own, others)


def _join_halves(shard, after=()):
    h = shard.shape[0] // 2

    def body(in_ref, *rest):
        out_ref, send_sem, recv_sem = rest[len(after):]
        x, y, c = _me()
        cp = pltpu.make_async_remote_copy(
            src_ref=in_ref.at[pl.ds(c * h, h), :], dst_ref=out_ref.at[pl.ds(c * h, h), :],
            send_sem=send_sem, recv_sem=recv_sem, device_id=(x, y, 1 - c), device_id_type=MESH)
        cp.start()
        pltpu.make_async_remote_copy(
            src_ref=in_ref.at[pl.ds(c * h, h), :], dst_ref=out_ref.at[pl.ds((1 - c) * h, h), :],
            send_sem=send_sem, recv_sem=recv_sem, device_id=(x, y, 1 - c), device_id_type=MESH).wait_recv()
        cp.wait_send()

    return pl.pallas_call(
        body, name="join_halves", in_specs=[ANY] * (1 + len(after)), out_specs=ANY,
        out_shape=jax.ShapeDtypeStruct(shard.shape, shard.dtype), input_output_aliases={0: 0},
        scratch_shapes=[pltpu.SemaphoreType.DMA, pltpu.SemaphoreType.DMA],
    )(shard, *after)


def _allreduce_copies(src_ref, land_ref):
    x, y, c = _me()
    peers = [((x + fx) % 2, (y + fy) % 2, (c + fc) % 2) for fx in range(2) for fy in range(2) for fc in range(2)][1:]
    return [(src_ref, land_ref.at[4 * x + 2 * y + c], peer) for peer in peers]


def _allreduce_start(s, after):
    return _split_start("allreduce_small_start", s, (N_DEV, *s.shape), s.dtype, N_DEV - 1, _allreduce_copies, after)


def _allreduce_finish(flight, after):
    own, landed = _split_wait("allreduce_small_wait", flight, after, N_DEV - 1, _allreduce_copies)
    me = 4 * lax.axis_index("x") + 2 * lax.axis_index("y") + lax.axis_index("c")
    parts = lax.dynamic_update_slice(landed, own[None], (me, 0, 0))

    def body(p_ref, o_ref):
        total = p_ref[0]
        for d in range(1, N_DEV):
            total = total + p_ref[d]
        o_ref[...] = total

    return pl.pallas_call(
        body, name="allreduce_small_sum",
        in_specs=[pl.BlockSpec(memory_space=pltpu.VMEM)], out_specs=pl.BlockSpec(memory_space=pltpu.VMEM),
        out_shape=jax.ShapeDtypeStruct(own.shape, F32),
    )(parts)


HBM_SPEC = pl.BlockSpec(memory_space=pltpu.HBM)
SEM_SPEC = pl.BlockSpec(memory_space=pltpu.SEMAPHORE)
DATAFLOW = pltpu.SideEffectType.DATAFLOW_SIDE_EFFECTING


class _InFlight(NamedTuple):
    sems: tuple
    src: jax.Array
    land: jax.Array
    token: jax.Array


def _split_start(name, src, land_shape, land_dtype, n, copies, after=()):
    n_after = len(after)

    def body(src_ref, land_ref, *rest):
        rest = rest[n_after:]
        sems, token = rest[:2 * n], rest[-1]
        for k, (s, d, peer) in enumerate(copies(src_ref, land_ref)):
            pltpu.make_async_remote_copy(src_ref=s, dst_ref=d, send_sem=sems[k], recv_sem=sems[n + k],
                                         device_id=peer, device_id_type=MESH).start()
        token[...] = jnp.zeros_like(token)

    outs = pl.pallas_call(
        body, name=name,
        out_shape=(*[pltpu.SemaphoreType.DMA(())] * (2 * n), pltpu.HBM(src.shape, src.dtype), pltpu.HBM(land_shape, land_dtype),
                   jax.ShapeDtypeStruct((8, 128), F32)),
        in_specs=(HBM_SPEC, HBM_SPEC, *[ANY] * n_after),
        out_specs=(*[SEM_SPEC] * (2 * n), HBM_SPEC, HBM_SPEC, pl.BlockSpec(memory_space=pltpu.VMEM)),
        input_output_aliases={0: 2 * n, 1: 2 * n + 1},
        compiler_params=pltpu.CompilerParams(has_side_effects=DATAFLOW),
    )(pltpu.with_memory_space_constraint(src, pltpu.HBM), pltpu.with_memory_space_constraint(lax.empty(land_shape, land_dtype), pltpu.HBM),
      *after)
    return _InFlight(tuple(outs[:2 * n]), outs[2 * n], outs[2 * n + 1], outs[2 * n + 2])


def _split_wait(name, flight, after, n, copies):
    after = after if isinstance(after, tuple) else (after,)

    def body(src_ref, land_ref, *rest):
        sems = rest[:2 * n]
        for k, (s, d, peer) in enumerate(copies(src_ref, land_ref)):
            cp = pltpu.make_async_remote_copy(src_ref=s, dst_ref=d, send_sem=sems[k], recv_sem=sems[n + k],
                                              device_id=peer, device_id_type=MESH)
            cp.wait_send()
            cp.wait_recv()

    return pl.pallas_call(
        body, name=name,
        out_shape=(pltpu.HBM(flight.src.shape, flight.src.dtype), pltpu.HBM(flight.land.shape, flight.land.dtype)),
        in_specs=(HBM_SPEC, HBM_SPEC, *[SEM_SPEC] * (2 * n), *[ANY] * len(after)),
        out_specs=(HBM_SPEC, HBM_SPEC), input_output_aliases={0: 0, 1: 1},
        compiler_params=pltpu.CompilerParams(has_side_effects=DATAFLOW),
    )(flight.src, flight.land, *flight.sems, *after)


def _gather_copies(src_ref, land_ref):
    x, y, c = _me()
    return [(src_ref, land_ref.at[2 * x + y], (*chip, c)) for chip in _other_chips(x, y)]


def _gather_start(packed, tag, after=()):
    return _split_start(f"gather_start_{tag}", packed, (N_CHIPS, *packed.shape), packed.dtype, 3, _gather_copies, after)


def _gather_wait(flight, after, tag):
    src, others = _split_wait(f"gather_wait_{tag}", flight, after, 3, _gather_copies)
    return lax.dynamic_update_slice(others, src[None], (2 * lax.axis_index("x") + lax.axis_index("y"), 0, 0))


def _across_copies(src_ref, land_ref):
    x, y, c = _me()
    half = src_ref.shape[0] // 2
    rows = pl.ds(c * half, half)
    return [(src_ref.at[rows, :], land_ref.at[2 * x + y, rows, :], (*chip, c)) for chip in _other_chips(x, y)]


def _to_sibling_copies(all_ref, unused_ref):
    x, y, c = _me()
    half = all_ref.shape[1] // 2
    places = [all_ref.at[2 * chip[0] + chip[1], pl.ds(c * half, half), :] for chip in _other_chips(x, y)]
    return [(place, place, (x, y, 1 - c)) for place in places]


def _gather_halves_start(shard, tag, after=()):
    return _split_start(f"gather_{tag}_across_start", shard, (N_CHIPS, *shard.shape), shard.dtype, 3, _across_copies, after)


def _gather_halves_relay(flight, after, tag):
    shard, landed = _split_wait(f"gather_{tag}_across_wait", flight, after, 3, _across_copies)
    return shard, _split_start(f"gather_{tag}_sibling_start", landed, (8, 128), landed.dtype, 3, _to_sibling_copies)


def _gather_halves_finish(shard, relay, after, tag):
    others = _split_wait(f"gather_{tag}_sibling_wait", relay, after, 3, _to_sibling_copies)[0]
    return lax.dynamic_update_slice(others, shard[None], (2 * lax.axis_index("x") + lax.axis_index("y"), 0, 0))


def _assemble_w_in(shards, tm):
    S, R, C = shards.shape
    n_gates = 2 * D_MODEL

    def body(s_ref, w_ref, g_ref):
        full = jnp.concatenate([s_ref[s] for s in range(S)], axis=1)
        w_ref[...] = full
        g_ref[...] = full[:, S * C - n_gates:]

    return pl.pallas_call(
        body, name="assemble_w_in", grid=(R // tm,),
        in_specs=[pl.BlockSpec((S, tm, C), lambda i: (0, i, 0))],
        out_specs=[pl.BlockSpec((tm, S * C), lambda i: (i, 0)), pl.BlockSpec((tm, n_gates), lambda i: (i, 0))],
        out_shape=[jax.ShapeDtypeStruct((R, S * C), shards.dtype), jax.ShapeDtypeStruct((R, n_gates), shards.dtype)],
        compiler_params=_cparams(("parallel",)),
    )(shards)


def _swap_copies(src_ref, land_ref):
    x, y, c = _me()
    half = land_ref.shape[1]
    return [(src_ref.at[:, pl.ds((1 - c) * half, half), :], land_ref, (x, y, 1 - c))]


def _swap_start(g, tag):
    S, R, W = g.shape
    return _split_start(f"swap_halves_start_{tag}", g, (S, R // 2, W), g.dtype, 1, _swap_copies)


def _swap_wait(flight, after, tag):
    return _split_wait(f"swap_halves_wait_{tag}", flight, after, 1, _swap_copies)


def _scatter_copies(src_ref, land_ref):
    x, y, c = _me()
    return [(src_ref.at[2 * chip[0] + chip[1]], land_ref.at[j], (*chip, c)) for j, chip in enumerate(_other_chips(x, y))]


def _scatter_start(part, tag):
    S, h, W = part.shape
    return _split_start(f"scatter_chips_start_{tag}", part, (S - 1, h, W), part.dtype, 3, _scatter_copies)


def _scatter_wait(flight, after, tag):
    return _split_wait(f"scatter_chips_wait_{tag}", flight, after, 3, _scatter_copies)[1]


def _join_copies(shard_ref, unused_ref):
    x, y, c = _me()
    h = shard_ref.shape[0] // 2
    rows = shard_ref.at[pl.ds(c * h, h), :]
    return [(rows, rows, (x, y, 1 - c))]


def _join_start(shard):
    return _split_start("join_halves_start", shard, (8, 128), shard.dtype, 1, _join_copies)


def _join_wait(flight, after):
    return _split_wait("join_halves_wait", flight, after, 1, _join_copies)[0]


def _adamw(name, g, g_row0, w, m, v):
    _, R, C = w.shape
    tm = next(cand for cand in (368, 256, 128, 64, 32, 16, 8) if R % cand == 0)
    assert g_row0 % tm == 0 and g.shape[1] == C

    def body(g_ref, w_ref, m_ref, v_ref, go_ref, d_ref, mo_ref, vo_ref):
        gt = g_ref[...]
        mt = ADAM_B1 * m_ref[...] + (1.0 - ADAM_B1) * gt
        vt = ADAM_B2 * v_ref[...] + (1.0 - ADAM_B2) * jnp.square(gt)
        m_hat = mt / (1.0 - ADAM_B1 ** ADAM_STEP)
        v_hat = vt / (1.0 - ADAM_B2 ** ADAM_STEP)
        go_ref[...] = gt
        d_ref[...] = -ADAM_LR * (m_hat / (jnp.sqrt(v_hat) + ADAM_EPS) + ADAM_WD * w_ref[...])
        mo_ref[...] = mt
        vo_ref[...] = vt

    state = pl.BlockSpec((None, tm, C), lambda i: (0, i, 0))
    return pl.pallas_call(
        body, name=name, grid=(R // tm,),
        in_specs=[pl.BlockSpec((tm, C), lambda i: (g_row0 // tm + i, 0)), state, state, state],
        out_specs=[state] * 4, out_shape=[jax.ShapeDtypeStruct((1, R, C), F32)] * 4,
        compiler_params=_cparams(("parallel",)),
    )(g, w, m, v)


def _unpack_weights(gathered, names):
    S = gathered.shape[0]
    shard_shapes = {"w_in": (D_MODEL, (QKV_WIDTH + 2 * D_MODEL) // S), "w_branch_na": (NA_WIDTH, D_MODEL // S),
                    "w_branch_dil": (DIL_OUT_WIDTH, D_MODEL // S), "w_out": (D_MODEL // S, D_MODEL),
                    "w_up": (D_MODEL, D_FF // S), "w_down": (D_FF // S, D_MODEL),
                    "w_ple_gate": (D_MODEL // S, D_MODEL), "w_ple_proj": (PLE_DIM, D_MODEL // S)}
    col_sharded = {"w_in", "w_branch_na", "w_branch_dil", "w_up", "w_ple_proj"}
    out, r0 = {}, 0
    for name in names:
        rows, cols = shard_shapes[name]
        n = rows * cols // PACK_W
        t = gathered[:, r0:r0 + n, :].reshape(S, rows, cols)
        r0 += n
        out[name] = t.transpose(1, 0, 2).reshape(rows, S * cols) if name in col_sharded else t.reshape(S * rows, cols)
    return out


def kernel(x, p, positions, g_mix, w_in, rpb, w_branch_na, w_branch_dil, w_out, g_mlp, w_up, w_down, g_ple, w_ple_gate, w_ple_proj, g_final, loss_target, m_g_mix, m_w_in, m_rpb, m_w_branch_na, m_w_branch_dil, m_w_out, m_g_mlp, m_w_up, m_w_down, m_g_ple, m_w_ple_gate, m_w_ple_proj, m_g_final, v_g_mix, v_w_in, v_rpb, v_w_branch_na, v_w_branch_dil, v_w_out, v_g_mlp, v_w_up, v_w_down, v_g_ple, v_w_ple_gate, v_w_ple_proj, v_g_final):
    shards = {"w_in": w_in[0], "w_branch_na": w_branch_na[0], "w_branch_dil": w_branch_dil[0], "w_out": w_out[0],
              "w_up": w_up[0], "w_down": w_down[0], "w_ple_gate": w_ple_gate[0], "w_ple_proj": w_ple_proj[0]}
    params = {"w_in": w_in, "w_branch_na": w_branch_na, "w_branch_dil": w_branch_dil, "w_out": w_out, "w_up": w_up,
              "w_down": w_down, "w_ple_gate": w_ple_gate, "w_ple_proj": w_ple_proj,
              "m_w_in": m_w_in, "m_w_branch_na": m_w_branch_na, "m_w_branch_dil": m_w_branch_dil, "m_w_out": m_w_out,
              "m_w_up": m_w_up, "m_w_down": m_w_down, "m_w_ple_gate": m_w_ple_gate, "m_w_ple_proj": m_w_ple_proj,
              "v_w_in": v_w_in, "v_w_branch_na": v_w_branch_na, "v_w_branch_dil": v_w_branch_dil, "v_w_out": v_w_out,
              "v_w_up": v_w_up, "v_w_down": v_w_down, "v_w_ple_gate": v_w_ple_gate, "v_w_ple_proj": v_w_ple_proj}

    xs, ps, tgt = x[0], p[0, 0], loss_target[0]
    T = xs.shape[0]
    TM = 512
    gm, gl, gp, gf = g_mix, g_mlp, g_ple, g_final.reshape(1, D_MODEL)

    across = _gather_halves_start(shards["w_in"].astype(BF), "in")
    a = _rowwise("norm_mix", lambda h, g: h * _rms(h) * g, T, TM, [_row(xs, TM), _full(gm)], [(D_MODEL, BF)],
                 after=(across.token,))
    cos2, sin_signed = _rope_tables(positions[0])
    tab = _na_bias_table(rpb[0])
    packed_mix = jnp.concatenate([_pack_rows(shards[n].astype(BF)) for n in GATHER_MIX], axis=0)
    packed_mlp = jnp.concatenate([_pack_rows(shards[n].astype(BF)) for n in GATHER_MLP], axis=0)
    w_in_shard, w_in_relay = _gather_halves_relay(across, (a, tab, cos2, sin_signed, packed_mix, packed_mlp), "in")
    w_in_all = _gather_halves_finish(w_in_shard, w_in_relay, w_in_relay.token, "in")
    w_in_full, w_gates = _assemble_w_in(w_in_all, 256)
    W = {"w_in": w_in_full}
    mix_flight = _gather_start(packed_mix, "mix", after=(w_in_all,))
    mlp_across = _gather_halves_start(packed_mlp, "mlp", after=(mix_flight.token,))

    n3 = 3 * NA_WIDTH
    qkv = _mm("in_na", a, W["w_in"], "nn", 1024, 768, 1024, [BF], after=(mlp_across.token,),
              b_view=(n3, (D_MODEL, 768), lambda j, k: (k, j)))
    z_dil = _mm("in_dil", a, W["w_in"], "nn", 1024, 768, 1024, [F32], after=(mlp_across.token,),
                b_view=(3 * DIL_WIDTH, (D_MODEL, 768), lambda j, k: (k, n3 // 768 + j)))
    z_gates = _mm("in_gates", a, w_gates, "nn", 1024,1024, 1024, [BF], after=(mlp_across.token,))

    dil_ops = _qkv_prep(z_dil, cos2, sin_signed, TM)
    y_na = _na_fwd(qkv, tab)
    band = [_band_fwd(*dil_ops[g], g) for g in range(len(DIL_GROUPS))]
    y_dil, w_grp, o_nat = _dil_merge_fwd([b[0] for b in band], [b[1] for b in band], T, TM)

    W.update(_unpack_weights(_gather_wait(mix_flight, y_dil, "mix"), GATHER_MIX))
    mlp_shard, mlp_relay = _gather_halves_relay(mlp_across, y_dil, "mlp")
    u_na = _mm("branch_na", y_na, W["w_branch_na"], "nn", 1024,1024, 512, [BF], after=(mlp_relay.token,))
    def gate_mix(acc, gn, gd, un):
        ud = acc.astype(BF)
        return ud, _sigmoid(gn.astype(F32)) * un.astype(F32) + _sigmoid(gd.astype(F32)) * ud.astype(F32)

    u_dil, mixed = _mm("branch_dil", y_dil, W["w_branch_dil"], "nn", 512, 1024, 256, [BF, BF], epilogue=gate_mix,
                       extras=((z_gates, 0), (z_gates, 1), u_na))

    def add_norm(d, h, g):
        h = h + d
        return h, h * _rms(h) * g

    h1, cn = _mm("out_proj", mixed, W["w_out"], "nn", 512, 1024, 1024, [F32, BF], epilogue=add_norm, extras=(xs,), consts=(gl,))
    mlp_all = _gather_halves_finish(mlp_shard, mlp_relay, cn, "mlp")
    W.update({n: t for n, t in _unpack_weights(mlp_all, GATHER_MLP).items() if n.startswith("w_ple")})
    chip_block = (None, D_MODEL, PACK_W)
    up, act = _mm("mlp_up", cn, mlp_all, "nn", 1024,1024, 1024, [BF, BF],
                  epilogue=lambda acc: (acc, jnp.square(jnp.maximum(acc, 0.0))), b_view=(D_FF, chip_block, lambda j, k: (j, 0, 0)))
    h2, en = _mm("mlp_down", act, mlp_all, "nn", 512, 1024, D_FF, [F32, BF], epilogue=add_norm, extras=(h1,), consts=(gp,),
                 b_view=(D_MODEL, (N_CHIPS, D_MODEL, PACK_W), lambda j, k: (k, 1, 0)))
    pp = _mm("ple_proj", ps, W["w_ple_proj"], "nn", 1024,1024, 256, [F32])

    def head(gtt, h2t, ppt, tg, g):
        sg = _sigmoid(gtt)
        h3 = h2t + sg * ppt
        yo = h3 * _rms(h3) * g
        diff = yo - tg
        loss = 0.5 * jnp.sum(jnp.mean(jnp.square(diff), axis=-1, keepdims=True), axis=0, keepdims=True)
        dh3, dg = _rms_bwd(diff * (1.0 / D_MODEL), h3, g)
        return dh3, dh3 * ppt * sg * (1.0 - sg), dh3 * sg, jnp.broadcast_to(loss, (1, 128)), dg

    dh3, d_gt, d_pp, loss_part, dg_final = _mm(
        "ple_gate_loss_head", en, W["w_ple_gate"], "nn", 512, 1024, 1024, [F32, BF, BF], epilogue=head,
        extras=(h2, pp, tgt), consts=(gf,), sums=[128, D_MODEL])

    early_shapes = {n: shards[n].shape for n in REDUCE_EARLY}
    early_rows = sum(r * c for r, c in early_shapes.values()) // PACK_W
    shard_rows = D_MODEL // N_CHIPS
    early_buf = _mm("g_ple_gate", en, d_gt, "tn", 1024, 1024, 2048, [F32],
                    into=(jax.ShapeDtypeStruct((N_CHIPS, early_rows, PACK_W), F32), (N_CHIPS, shard_rows, PACK_W),
                          lambda i, j: (0, 2 * D_MODEL // shard_rows, 0)))
    g_ple_proj = _mm("g_ple_proj", ps, d_pp, "tn", 256, 1024, 1024,[F32])

    def add_norm_bwd(dn, dh_out, h, g):
        dh, dg = _rms_bwd(dn, h, g)
        dh = dh_out + dh
        return dh, dh, dg

    dh2, dh2_b, dg_ple = _mm("d_ple_gate", d_gt, W["w_ple_gate"], "nt", 512, 1024, 1024, [F32, BF],
                             epilogue=add_norm_bwd, extras=(dh3, h2), consts=(gp,), sums=[D_MODEL])
    d_up = _mm("d_mlp_down", dh2_b, mlp_all, "nt", 1024,1024, 1024, [BF], b_view=(D_FF, chip_block, lambda j, k: (j, 1, 0)),
               epilogue=lambda acc, u: (acc * (2.0 * jnp.maximum(u.astype(F32), 0.0)),), extras=(up,))
    early_buf = _mm("g_mlp_down", act, dh2_b, "tn", 512, 1024, T, [F32],
                    into=(early_buf, (None, 512, PACK_W), lambda i, j: (i // 2, 2 + i % 2, 0)))
    early_buf = _mm("g_mlp_up", cn, d_up, "tn", 1024, 512, T, [F32],
                    into=(early_buf, (None, D_MODEL, 512), lambda i, j: (j // 2, 0, j % 2)))
    dh1, dh1_b, dg_mlp = _mm("d_mlp_up", d_up, mlp_all, "nt", 512, 1024, D_FF, [F32, BF], epilogue=add_norm_bwd,
                             b_view=(D_MODEL, (N_CHIPS, D_MODEL, PACK_W), lambda j, k: (k, 0, 0)),
                             extras=(dh2, h1), consts=(gl,), sums=[D_MODEL])
    early_buf = _mm("g_out_proj", mixed, dh1_b, "tn", 1024, 1024, 1024, [F32],
                    into=(early_buf, (N_CHIPS, shard_rows, PACK_W), lambda i, j: (0, 2 * D_MODEL // shard_rows + 1, 0)))

    def gate_bwd(dm, gn, gd, un, ud):
        gn, gd, un, ud = (t.astype(F32) for t in (gn, gd, un, ud))
        sn, sd = _sigmoid(gn), _sigmoid(gd)
        return jnp.concatenate([dm * un * sn * (1.0 - sn), dm * ud * sd * (1.0 - sd)], axis=1), dm * sn, dm * sd

    dz_gates, d_u_na, d_u_dil = _mm("d_out_proj", dh1_b, W["w_out"], "nt", 512, 1024, 1024, [(BF, 2 * D_MODEL), BF, BF],
                                    epilogue=gate_bwd, extras=((z_gates, 0), (z_gates, 1), u_na, u_dil))
    g_branch_na = _mm("g_branch_na", y_na, d_u_na, "tn", 1024, 1024, 1024,[F32])
    g_branch_dil = _mm("g_branch_dil", y_dil, d_u_dil, "tn", 256, 1024, 1024,[F32])
    small_rows = [jnp.concatenate([_pack_rows(g[:, s * shard_rows:(s + 1) * shard_rows]) for g in (g_ple_proj, g_branch_na, g_branch_dil)],
                                  axis=0) for s in range(N_CHIPS)]
    early_buf = lax.dynamic_update_slice(early_buf, jnp.stack(small_rows), (0, 2 * D_MODEL + 2 * shard_rows, 0))
    early_tm = early_rows // 4
    swap_flight = _swap_start(early_buf, "early")
    d_y_na = _mm("d_branch_na", d_u_na, W["w_branch_na"], "nt", 1024,512, 1024, [BF], after=(swap_flight.token,))
    d_y_dil = _mm("d_branch_dil", d_u_dil, W["w_branch_dil"], "nt", 1024,256, 1024, [F32])

    dqa, dka, dva, dtab = _na_bwd(qkv, tab, d_y_na)
    early_g, early_got = _swap_wait(swap_flight, dqa, "early")
    early_pair, early_pair_b = _pair_sum(early_g, early_got, early_tm)
    scatter_flight = _scatter_start(early_pair_b, "early")

    do_res, dlse_res = _dil_merge_bwd(d_y_dil, o_nat, w_grp, TM, after=(scatter_flight.token,))
    d_dil = [_band_bwd(*dil_ops[g], do_res[g], dlse_res[g], g) for g in range(len(DIL_GROUPS))]

    dz_qkv = _qkv_unprep((dqa, dka, dva), d_dil, cos2, sin_signed, TM)
    in_cols = shards["w_in"].shape[1]
    qkv_rows, gate_tm = dz_qkv.shape[1], 256
    assert qkv_rows % gate_tm == 0
    g_in = _mm("g_in_qkv", dz_qkv, a, "tn", 640, 1024, T, [F32],
               into=(jax.ShapeDtypeStruct((N_CHIPS * in_cols, D_MODEL), F32), (640, D_MODEL), lambda i, j: (i, 0)))
    g_in = _mm("g_in_gates", dz_gates, a, "tn", gate_tm, 1024, T, [F32],
               into=(g_in, (gate_tm, D_MODEL), lambda i, j: (qkv_rows // gate_tm + i, 0)))
    early_mine = _chip_sum(early_pair, _scatter_wait(scatter_flight, (g_in,), "early"), early_tm)
    join_flight = _join_start(early_mine)

    late_tm = in_cols // 4
    late_swap = _swap_start(g_in.reshape(N_CHIPS, in_cols, D_MODEL), "late")
    d_a = _mm("d_in_qkv", dz_qkv, W["w_in"], "nt", 512, 1024, qkv_rows, [F32], after=(late_swap.token, join_flight.token),
              b_view=(D_MODEL, (D_MODEL, qkv_rows), lambda j, k: (j, k)))
    late_g, late_got = _swap_wait(late_swap, d_a, "late")
    late_pair, late_pair_b = _pair_sum(late_g, late_got, late_tm)
    late_scatter = _scatter_start(late_pair_b, "late")
    d_rpb = _na_rpb_grad(dtab, after=(late_scatter.token,))[:, :2 * NA_WIN_ROWS - 1, :2 * NA_WIN_COLS - 1]
    def first_bwd(dn_gates, dn_qkv, dh_out, h, g):
        dh, dg = _rms_bwd(dn_gates + dn_qkv, h, g)
        return dh_out + dh, dg

    grad_x, dg_mix = _mm("d_in_gates", dz_gates, w_gates, "nt", 512, 1024, 2048, [F32], epilogue=first_bwd,
                         extras=(d_a, dh1, xs), consts=(gm,), sums=[D_MODEL], after=(late_scatter.token,))
    early_shard = _join_wait(join_flight, grad_x)

    n_rpb = rpb.size
    rpb_rows = 4
    small = jnp.concatenate([
        dg_mix, dg_mlp, dg_ple, dg_final,
        jnp.pad(d_rpb.reshape(-1), (0, rpb_rows * D_MODEL - n_rpb)).reshape(rpb_rows, D_MODEL),
        jnp.pad(loss_part, ((0, 0), (0, D_MODEL - loss_part.shape[1]))),
        jnp.zeros((SMALL_ROWS - 5 - rpb_rows, D_MODEL), F32)], axis=0)
    out = {"grad": {}, "delta": {}, "new_m": {}, "new_v": {}}

    def update(n, g, row0):
        res = _adamw("adamw_" + n, g, row0, params[n], params["m_" + n], params["v_" + n])
        for kind, t in zip(("grad", "delta", "new_m", "new_v"), res, strict=True):
            out[kind][n] = t

    row0 = 0
    for n in REDUCE_EARLY:
        rows, cols = early_shapes[n]
        n_rows = rows * cols // PACK_W
        if cols == PACK_W:
            update(n, early_shard, row0)
        else:
            update(n, early_shard[row0:row0 + n_rows].reshape(rows, cols), 0)
        row0 += n_rows
    late_others = _scatter_wait(late_scatter, (*[out["new_v"][n] for n in REDUCE_EARLY], d_rpb), "late")
    late_mine = _chip_sum(late_pair, late_others, late_tm)
    small_flight = _allreduce_start(small, after=(late_mine,))
    res = _adamw("adamw_w_in", _join_halves(late_mine, after=(small_flight.token,)), 0,
                 *[jnp.swapaxes(params[n], 1, 2) for n in ("w_in", "m_w_in", "v_w_in")])
    small = _allreduce_finish(small_flight, res[3])
    for kind, t in zip(("grad", "delta", "new_m", "new_v"), res, strict=True):
        out[kind]["w_in"] = jnp.swapaxes(t, 1, 2)
    loss = small[4 + rpb_rows, 0]

    def small_pack(a0, a1, a2, a3, r):
        return jnp.concatenate([a0.reshape(1, -1), a1.reshape(1, -1), a2.reshape(1, -1), a3.reshape(1, -1),
                                jnp.pad(r.reshape(-1), (0, rpb_rows * D_MODEL - n_rpb)).reshape(rpb_rows, D_MODEL)], axis=0)

    small_res = _adamw("adamw_small", small, 0, small_pack(g_mix, g_mlp, g_ple, g_final, rpb)[None],
                       small_pack(m_g_mix, m_g_mlp, m_g_ple, m_g_final, m_rpb)[None],
                       small_pack(v_g_mix, v_g_mlp, v_g_ple, v_g_final, v_rpb)[None])

    def small_unpack(t):
        return {"g_mix": t[0].reshape(g_mix.shape), "g_mlp": t[1].reshape(g_mlp.shape), "g_ple": t[2].reshape(g_ple.shape),
                "g_final": t[3].reshape(g_final.shape), "rpb": t[4:].reshape(-1)[:n_rpb].reshape(rpb.shape)}

    for kind, t in zip(("grad", "delta", "new_m", "new_v"), small_res, strict=True):
        out[kind].update(small_unpack(t[0]))

    order = ["g_mix", "w_in", "rpb", "w_branch_na", "w_branch_dil", "w_out", "g_mlp", "w_up", "w_down", "g_ple",
             "w_ple_gate", "w_ple_proj", "g_final"]
    return (loss, grad_x[None], *[out["grad"][n] for n in order], *[out["delta"][n] for n in order],
            *[out["new_m"][n] for n in order], *[out["new_v"][n] for n in order])
```
